```python
import math
import jax, jax.numpy as jnp
from jax import lax
import numpy as np

D_MODEL = 1024
BATCH = 8
SEQ = 4096
DEPTH = 1

MEM_LEN = 256
HEAD_DIM = 64
N_ATTN_HEADS = 8
ATTN_WIDTH = N_ATTN_HEADS * HEAD_DIM
CONV_WIDTH = D_MODEL - ATTN_WIDTH
N_CONV_GROUPS = CONV_WIDTH // HEAD_DIM
MIX_WIDTH = ATTN_WIDTH + CONV_WIDTH
IN_PROJ_COLS = 3 * ATTN_WIDTH + 3 * CONV_WIDTH
DILATED_PATTERNS = ((128, 1), (512, 4), (2048, 16))
SEQ_PAD_MULT = max(w for w, _ in DILATED_PATTERNS)
N_BUCKETS = 32
BUCKET_MAX_EXACT = N_BUCKETS // 2
BUCKET_MAX_DISTANCE = 2048
SHORT_CONV_K = 3
FFN_CONV_K = 3
D_FF = 2816
N_MEM_HEADS = 4
MEM_HEAD_DIM = D_MODEL // N_MEM_HEADS
EPS = 1e-6

kernel_name = "hymba_dilated_shortconv_convffn_memxattn"


def rms_norm(x, g):
    xf = x.astype(jnp.float32)
    y = xf * lax.rsqrt(jnp.mean(xf * xf, axis=-1, keepdims=True) + EPS)
    return (y * g.astype(jnp.float32)).astype(x.dtype)


def causal_dwconv(u, w):
    k_width = w.shape[0]
    s = u.shape[1]
    up = jnp.pad(u, ((0, 0), (k_width - 1, 0), (0, 0)))
    out = up[:, 0:s, :] * w[0]
    for k in range(1, k_width):
        out = out + up[:, k:k + s, :] * w[k]
    return out


def t5_bucket(distance):
    d = jnp.maximum(distance, 1).astype(jnp.float32)
    large = BUCKET_MAX_EXACT + (
        jnp.log(d / BUCKET_MAX_EXACT) / math.log(BUCKET_MAX_DISTANCE / BUCKET_MAX_EXACT)
        * (N_BUCKETS - BUCKET_MAX_EXACT)).astype(jnp.int32)
    large = jnp.minimum(large, N_BUCKETS - 1)
    return jnp.where(distance < BUCKET_MAX_EXACT, distance, large)


def dilated_window_attention(q, k, v, rel_bias, window, dilation):
    b, h, sp, dh = q.shape
    w = window // dilation
    n_sub = sp // dilation
    nb = n_sub // w

    def to_blocks(t):
        t = t.reshape(b, h, n_sub, dilation, -1).transpose(0, 1, 3, 2, 4)
        return t.reshape(b, h, dilation, nb, w, t.shape[-1])

    def from_blocks(t):
        t = t.reshape(b, h, dilation, n_sub, -1).transpose(0, 1, 3, 2, 4)
        return t.reshape(b, h, sp, t.shape[-1])

    def with_prev(t):
        prev = jnp.pad(t, ((0, 0), (0, 0), (0, 0), (1, 0), (0, 0), (0, 0)))[:, :, :, :-1]
        return jnp.concatenate([prev, t], axis=4)

    qb = to_blocks(q)
    kk = with_prev(to_blocks(k))
    vv = with_prev(to_blocks(v))

    logits = jnp.einsum('bhrnid,bhrnjd->bhrnij', qb, kk).astype(jnp.float32)
    qi = jnp.arange(w)[:, None]
    kj = jnp.arange(2 * w)[None, :]
    steps = qi + w - kj
    valid_local = (steps >= 0) & (steps <= w)
    block_idx = jnp.arange(nb)[:, None, None]
    valid = valid_local[None] & ((block_idx > 0) | (kj >= w)[None])
    bucket = t5_bucket(jnp.clip(steps, 0, w) * dilation)
    bias = rel_bias.astype(jnp.float32)[:, bucket]
    logits = logits + bias[None, :, None, None]
    logits = jnp.where(valid, logits, -jnp.inf)
    m = jnp.max(logits, axis=-1, keepdims=True)
    p = jnp.exp(logits - m)
    s = jnp.sum(p, axis=-1, keepdims=True)
    o = jnp.einsum('bhrnij,bhrnjd->bhrnid', p, vv.astype(jnp.float32)) / s
    return from_blocks(o), from_blocks(m), from_blocks(s)


def hybrid_mixer(h, rel_bias, w_in, w_short_conv, g_attn_out, g_conv_out, w_out):
    b, s, _ = h.shape
    proj = h @ w_in
    q, k, v, gate_b, gate_c, x_in = jnp.split(
        proj, [ATTN_WIDTH, 2 * ATTN_WIDTH, 3 * ATTN_WIDTH,
               3 * ATTN_WIDTH + CONV_WIDTH, 3 * ATTN_WIDTH + 2 * CONV_WIDTH], axis=-1)

    sp = ((s + SEQ_PAD_MULT - 1) // SEQ_PAD_MULT) * SEQ_PAD_MULT

    def heads(t):
        t = t.reshape(b, s, N_ATTN_HEADS, HEAD_DIM).transpose(0, 2, 1, 3)
        return jnp.pad(t, ((0, 0), (0, 0), (0, sp - s), (0, 0)))

    qh = heads(q) * (HEAD_DIM ** -0.5)
    kh, vh = heads(k), heads(v)
    branches = [dilated_window_attention(qh, kh, vh, rel_bias, w, d) for (w, d) in DILATED_PATTERNS]
    m_all = branches[0][1]
    for _, m_i, _ in branches[1:]:
        m_all = jnp.maximum(m_all, m_i)
    num = jnp.zeros_like(branches[0][0])
    den = jnp.zeros_like(m_all)
    for o_i, m_i, s_i in branches:
        wt = s_i * jnp.exp(m_i - m_all)
        num = num + wt * o_i
        den = den + wt
    attn = (num / den)[:, :, :s].transpose(0, 2, 1, 3).reshape(b, s, ATTN_WIDTH).astype(h.dtype)

    conv = gate_b * causal_dwconv(gate_c * x_in, w_short_conv)

    mixed = jnp.concatenate([rms_norm(attn, g_attn_out), rms_norm(conv, g_conv_out)], axis=-1)
    return mixed @ w_out


def memory_cross_attention(h, mem_n, w_xq, w_xk, w_xv, w_xo):
    b, s, _ = h.shape
    q = (h @ w_xq).reshape(b, s, N_MEM_HEADS, MEM_HEAD_DIM)
    k = (mem_n @ w_xk).reshape(b, MEM_LEN, N_MEM_HEADS, MEM_HEAD_DIM)
    v = (mem_n @ w_xv).reshape(b, MEM_LEN, N_MEM_HEADS, MEM_HEAD_DIM)
    logits = jnp.einsum('bshd,bmhd->bhsm', q, k).astype(jnp.float32) * (MEM_HEAD_DIM ** -0.5)
    p = jax.nn.softmax(logits, axis=-1)
    o = jnp.einsum('bhsm,bmhd->bshd', p, v.astype(jnp.float32)).astype(h.dtype)
    return o.reshape(b, s, D_MODEL) @ w_xo


def conv_ffn(h, w_up, w_ffn_conv, b_ffn_conv, w_down):
    up = causal_dwconv(h @ w_up, w_ffn_conv) + b_ffn_conv
    gate, val = jnp.split(up, 2, axis=-1)
    return (jax.nn.silu(gate) * val) @ w_down


def _fwd_setup_inputs(seed: int = 0) -> dict:
    key = jax.random.key(seed)
    ks = iter(jax.random.split(key, 32))
    f32 = jnp.float32
    L = DEPTH

    def dense(shape, fan_in):
        return jax.random.normal(next(ks), shape, f32) * fan_in ** -0.5

    def gain(shape):
        return 1.0 + 0.02 * jax.random.normal(next(ks), shape, f32)

    return {
        "x": jax.random.normal(next(ks), (BATCH, SEQ, D_MODEL), f32),
        "mem": jax.random.normal(next(ks), (BATCH, MEM_LEN, D_MODEL), f32),
        "rel_bias": 0.2 * jax.random.normal(next(ks), (N_ATTN_HEADS, N_BUCKETS), f32),
        "g_mix": gain((L, D_MODEL)),
        "w_in": dense((L, D_MODEL, IN_PROJ_COLS), D_MODEL),
        "w_short_conv": dense((L, SHORT_CONV_K, CONV_WIDTH), SHORT_CONV_K),
        "g_attn_out": gain((L, ATTN_WIDTH)),
        "g_conv_out": gain((L, CONV_WIDTH)),
        "w_out": dense((L, MIX_WIDTH, D_MODEL), MIX_WIDTH),
        "g_xattn": gain((L, D_MODEL)),
        "g_mem": gain((L, D_MODEL)),
        "w_xq": dense((L, D_MODEL, D_MODEL), D_MODEL),
        "w_xk": dense((L, D_MODEL, D_MODEL), D_MODEL),
        "w_xv": dense((L, D_MODEL, D_MODEL), D_MODEL),
        "w_xo": dense((L, D_MODEL, D_MODEL), D_MODEL),
        "g_ffn": gain((L, D_MODEL)),
        "w_up": dense((L, D_MODEL, 2 * D_FF), D_MODEL),
        "w_ffn_conv": dense((L, FFN_CONV_K, 2 * D_FF), FFN_CONV_K),
        "b_ffn_conv": 0.02 * jax.random.normal(next(ks), (L, 2 * D_FF), f32),
        "w_down": dense((L, D_FF, D_MODEL), D_FF),
        "g_final": gain((D_MODEL,)),
    }


def _fwd_reference(x, mem, rel_bias, g_mix, w_in, w_short_conv, g_attn_out, g_conv_out, w_out,
              g_xattn, g_mem, w_xq, w_xk, w_xv, w_xo, g_ffn, w_up, w_ffn_conv, b_ffn_conv,
              w_down, g_final):
    for l in range(DEPTH):
        x = x + hybrid_mixer(rms_norm(x, g_mix[l]), rel_bias, w_in[l], w_short_conv[l],
                             g_attn_out[l], g_conv_out[l], w_out[l])
        x = x + memory_cross_attention(rms_norm(x, g_xattn[l]), rms_norm(mem, g_mem[l]),
                                       w_xq[l], w_xk[l], w_xv[l], w_xo[l])
        x = x + conv_ffn(rms_norm(x, g_ffn[l]), w_up[l], w_ffn_conv[l], b_ffn_conv[l], w_down[l])
    return rms_norm(x, g_final)


import jax as _jax
import jax.numpy as _jnp

TWIN_FORMAT = 'train_step'
FWD_PARAMS = ['x', 'mem', 'rel_bias', 'g_mix', 'w_in', 'w_short_conv', 'g_attn_out', 'g_conv_out', 'w_out', 'g_xattn', 'g_mem', 'w_xq', 'w_xk', 'w_xv', 'w_xo', 'g_ffn', 'w_up', 'w_ffn_conv', 'b_ffn_conv', 'w_down', 'g_final']
TWIN_WEIGHTS = ['rel_bias', 'g_mix', 'w_in', 'w_short_conv', 'g_attn_out', 'g_conv_out', 'w_out', 'g_xattn', 'g_mem', 'w_xq', 'w_xk', 'w_xv', 'w_xo', 'g_ffn', 'w_up', 'w_ffn_conv', 'b_ffn_conv', 'w_down', 'g_final']
TWIN_DIFF_INPUT = 'x'
TWIN_INPUTS = ['x', 'mem', 'rel_bias', 'g_mix', 'w_in', 'w_short_conv', 'g_attn_out', 'g_conv_out', 'w_out', 'g_xattn', 'g_mem', 'w_xq', 'w_xk', 'w_xv', 'w_xo', 'g_ffn', 'w_up', 'w_ffn_conv', 'b_ffn_conv', 'w_down', 'g_final', 'loss_target', 'm_rel_bias', 'm_g_mix', 'm_w_in', 'm_w_short_conv', 'm_g_attn_out', 'm_g_conv_out', 'm_w_out', 'm_g_xattn', 'm_g_mem', 'm_w_xq', 'm_w_xk', 'm_w_xv', 'm_w_xo', 'm_g_ffn', 'm_w_up', 'm_w_ffn_conv', 'm_b_ffn_conv', 'm_w_down', 'm_g_final', 'v_rel_bias', 'v_g_mix', 'v_w_in', 'v_w_short_conv', 'v_g_attn_out', 'v_g_conv_out', 'v_w_out', 'v_g_xattn', 'v_g_mem', 'v_w_xq', 'v_w_xk', 'v_w_xv', 'v_w_xo', 'v_g_ffn', 'v_w_up', 'v_w_ffn_conv', 'v_b_ffn_conv', 'v_w_down', 'v_g_final']
TWIN_OUTPUTS = ['loss', 'grad_x', 'grad_rel_bias', 'grad_g_mix', 'grad_w_in', 'grad_w_short_conv', 'grad_g_attn_out', 'grad_g_conv_out', 'grad_w_out', 'grad_g_xattn', 'grad_g_mem', 'grad_w_xq', 'grad_w_xk', 'grad_w_xv', 'grad_w_xo', 'grad_g_ffn', 'grad_w_up', 'grad_w_ffn_conv', 'grad_b_ffn_conv', 'grad_w_down', 'grad_g_final', 'delta_rel_bias', 'delta_g_mix', 'delta_w_in', 'delta_w_short_conv', 'delta_g_attn_out', 'delta_g_conv_out', 'delta_w_out', 'delta_g_xattn', 'delta_g_mem', 'delta_w_xq', 'delta_w_xk', 'delta_w_xv', 'delta_w_xo', 'delta_g_ffn', 'delta_w_up', 'delta_w_ffn_conv', 'delta_b_ffn_conv', 'delta_w_down', 'delta_g_final', 'new_m_rel_bias', 'new_m_g_mix', 'new_m_w_in', 'new_m_w_short_conv', 'new_m_g_attn_out', 'new_m_g_conv_out', 'new_m_w_out', 'new_m_g_xattn', 'new_m_g_mem', 'new_m_w_xq', 'new_m_w_xk', 'new_m_w_xv', 'new_m_w_xo', 'new_m_g_ffn', 'new_m_w_up', 'new_m_w_ffn_conv', 'new_m_b_ffn_conv', 'new_m_w_down', 'new_m_g_final', 'new_v_rel_bias', 'new_v_g_mix', 'new_v_w_in', 'new_v_w_short_conv', 'new_v_g_attn_out', 'new_v_g_conv_out', 'new_v_w_out', 'new_v_g_xattn', 'new_v_g_mem', 'new_v_w_xq', 'new_v_w_xk', 'new_v_w_xv', 'new_v_w_xo', 'new_v_g_ffn', 'new_v_w_up', 'new_v_w_ffn_conv', 'new_v_b_ffn_conv', 'new_v_w_down', 'new_v_g_final']
TWIN_LEAF_KINDS = {'loss': 'loss', 'grad_x': 'grad_x', 'grad_rel_bias': 'grad_w', 'grad_g_mix': 'grad_w', 'grad_w_in': 'grad_w', 'grad_w_short_conv': 'grad_w', 'grad_g_attn_out': 'grad_w', 'grad_g_conv_out': 'grad_w', 'grad_w_out': 'grad_w', 'grad_g_xattn': 'grad_w', 'grad_g_mem': 'grad_w', 'grad_w_xq': 'grad_w', 'grad_w_xk': 'grad_w', 'grad_w_xv': 'grad_w', 'grad_w_xo': 'grad_w', 'grad_g_ffn': 'grad_w', 'grad_w_up': 'grad_w', 'grad_w_ffn_conv': 'grad_w', 'grad_b_ffn_conv': 'grad_w', 'grad_w_down': 'grad_w', 'grad_g_final': 'grad_w', 'delta_rel_bias': 'delta_w', 'delta_g_mix': 'delta_w', 'delta_w_in': 'delta_w', 'delta_w_short_conv': 'delta_w', 'delta_g_attn_out': 'delta_w', 'delta_g_conv_out': 'delta_w', 'delta_w_out': 'delta_w', 'delta_g_xattn': 'delta_w', 'delta_g_mem': 'delta_w', 'delta_w_xq': 'delta_w', 'delta_w_xk': 'delta_w', 'delta_w_xv': 'delta_w', 'delta_w_xo': 'delta_w', 'delta_g_ffn': 'delta_w', 'delta_w_up': 'delta_w', 'delta_w_ffn_conv': 'delta_w', 'delta_b_ffn_conv': 'delta_w', 'delta_w_down': 'delta_w', 'delta_g_final': 'delta_w', 'new_m_rel_bias': 'new_m', 'new_m_g_mix': 'new_m', 'new_m_w_in': 'new_m', 'new_m_w_short_conv': 'new_m', 'new_m_g_attn_out': 'new_m', 'new_m_g_conv_out': 'new_m', 'new_m_w_out': 'new_m', 'new_m_g_xattn': 'new_m', 'new_m_g_mem': 'new_m', 'new_m_w_xq': 'new_m', 'new_m_w_xk': 'new_m', 'new_m_w_xv': 'new_m', 'new_m_w_xo': 'new_m', 'new_m_g_ffn': 'new_m', 'new_m_w_up': 'new_m', 'new_m_w_ffn_conv': 'new_m', 'new_m_b_ffn_conv': 'new_m', 'new_m_w_down': 'new_m', 'new_m_g_final': 'new_m', 'new_v_rel_bias': 'new_v', 'new_v_g_mix': 'new_v', 'new_v_w_in': 'new_v', 'new_v_w_short_conv': 'new_v', 'new_v_g_attn_out': 'new_v', 'new_v_g_conv_out': 'new_v', 'new_v_w_out': 'new_v', 'new_v_g_xattn': 'new_v', 'new_v_g_mem': 'new_v', 'new_v_w_xq': 'new_v', 'new_v_w_xk': 'new_v', 'new_v_w_xv': 'new_v', 'new_v_w_xo': 'new_v', 'new_v_g_ffn': 'new_v', 'new_v_w_up': 'new_v', 'new_v_w_ffn_conv': 'new_v', 'new_v_b_ffn_conv': 'new_v', 'new_v_w_down': 'new_v', 'new_v_g_final': 'new_v'}


def _forward(args):
    return _fwd_reference(*[args[k] for k in FWD_PARAMS])


def _output_shape():
    def fwd():
        inp = _fwd_setup_inputs(0)
        return _fwd_reference(*[inp[k] for k in FWD_PARAMS])
    out = _jax.eval_shape(fwd)
    return out.shape, out.dtype

N_MICROBATCH = 1
ADAM_LR = 0.001
ADAM_B1 = 0.9
ADAM_B2 = 0.999
ADAM_EPS = 1e-08
ADAM_WD = 0.01
ADAM_STEP = 10
PER_EXAMPLE_BATCH_AXIS = {'x': 0, 'mem': 0, 'loss_target': 0}
SHARED_INPUTS = []
_WEIGHT_DTYPES = {'rel_bias': _jnp.float32, 'g_mix': _jnp.float32, 'w_in': _jnp.float32, 'w_short_conv': _jnp.float32, 'g_attn_out': _jnp.float32, 'g_conv_out': _jnp.float32, 'w_out': _jnp.float32, 'g_xattn': _jnp.float32, 'g_mem': _jnp.float32, 'w_xq': _jnp.float32, 'w_xk': _jnp.float32, 'w_xv': _jnp.float32, 'w_xo': _jnp.float32, 'g_ffn': _jnp.float32, 'w_up': _jnp.float32, 'w_ffn_conv': _jnp.float32, 'b_ffn_conv': _jnp.float32, 'w_down': _jnp.float32, 'g_final': _jnp.float32}
MOMENT_SCALE = {'rel_bias': 2.359782e-01, 'g_mix': 2.287294e-01, 'w_in': 1.353783e-01, 'w_short_conv': 1.404939e-01, 'g_attn_out': 1.295556e-01, 'g_conv_out': 1.596682e-01, 'w_out': 1.362941e-01, 'g_xattn': 1.491750e-02, 'g_mem': 2.181285e-02, 'w_xq': 1.417183e-02, 'w_xk': 1.418633e-02, 'w_xv': 1.442255e-02, 'w_xo': 1.435163e-02, 'g_ffn': 1.038242e-01, 'w_up': 4.274380e-02, 'w_ffn_conv': 4.348129e-02, 'b_ffn_conv': 4.190329e-02, 'w_down': 6.991783e-02, 'g_final': 3.203353e+01}


def _to_microbatches(a, axis):
    t = _jnp.moveaxis(a, axis, 0)
    t = t.reshape((N_MICROBATCH, t.shape[0] // N_MICROBATCH) + t.shape[1:])
    return _jnp.moveaxis(t, 1, axis + 1)


def setup_inputs(seed: int = 0) -> dict:
    inp = _fwd_setup_inputs(seed)
    key = _jax.random.fold_in(_jax.random.key(seed), 7919)
    shape, _ = _output_shape()
    out = dict(inp)
    out["loss_target"] = _jax.random.normal(_jax.random.fold_in(key, 0), shape, _jnp.float32)
    for i, name in enumerate(TWIN_WEIGHTS):
        w = inp[name].astype(_jnp.float32)
        if MOMENT_SCALE is None:
            s = _jnp.sqrt(_jnp.mean(_jnp.square(w)) + 1e-30)
        else:
            s = MOMENT_SCALE[name]
        km, kv = _jax.random.split(_jax.random.fold_in(key, i + 1))
        out[name] = w
        out["m_" + name] = s * _jax.random.normal(km, w.shape, _jnp.float32)
        out["v_" + name] = (s * s) * _jax.random.uniform(kv, w.shape, _jnp.float32, 0.5, 1.5)
    if N_MICROBATCH > 1:
        for name, axis in PER_EXAMPLE_BATCH_AXIS.items():
            out[name] = _to_microbatches(out[name], axis)
    return {'x': out['x'], 'mem': out['mem'], 'rel_bias': out['rel_bias'], 'g_mix': out['g_mix'], 'w_in': out['w_in'], 'w_short_conv': out['w_short_conv'], 'g_attn_out': out['g_attn_out'], 'g_conv_out': out['g_conv_out'], 'w_out': out['w_out'], 'g_xattn': out['g_xattn'], 'g_mem': out['g_mem'], 'w_xq': out['w_xq'], 'w_xk': out['w_xk'], 'w_xv': out['w_xv'], 'w_xo': out['w_xo'], 'g_ffn': out['g_ffn'], 'w_up': out['w_up'], 'w_ffn_conv': out['w_ffn_conv'], 'b_ffn_conv': out['b_ffn_conv'], 'w_down': out['w_down'], 'g_final': out['g_final'], 'loss_target': out['loss_target'], 'm_rel_bias': out['m_rel_bias'], 'm_g_mix': out['m_g_mix'], 'm_w_in': out['m_w_in'], 'm_w_short_conv': out['m_w_short_conv'], 'm_g_attn_out': out['m_g_attn_out'], 'm_g_conv_out': out['m_g_conv_out'], 'm_w_out': out['m_w_out'], 'm_g_xattn': out['m_g_xattn'], 'm_g_mem': out['m_g_mem'], 'm_w_xq': out['m_w_xq'], 'm_w_xk': out['m_w_xk'], 'm_w_xv': out['m_w_xv'], 'm_w_xo': out['m_w_xo'], 'm_g_ffn': out['m_g_ffn'], 'm_w_up': out['m_w_up'], 'm_w_ffn_conv': out['m_w_ffn_conv'], 'm_b_ffn_conv': out['m_b_ffn_conv'], 'm_w_down': out['m_w_down'], 'm_g_final': out['m_g_final'], 'v_rel_bias': out['v_rel_bias'], 'v_g_mix': out['v_g_mix'], 'v_w_in': out['v_w_in'], 'v_w_short_conv': out['v_w_short_conv'], 'v_g_attn_out': out['v_g_attn_out'], 'v_g_conv_out': out['v_g_conv_out'], 'v_w_out': out['v_w_out'], 'v_g_xattn': out['v_g_xattn'], 'v_g_mem': out['v_g_mem'], 'v_w_xq': out['v_w_xq'], 'v_w_xk': out['v_w_xk'], 'v_w_xv': out['v_w_xv'], 'v_w_xo': out['v_w_xo'], 'v_g_ffn': out['v_g_ffn'], 'v_w_up': out['v_w_up'], 'v_w_ffn_conv': out['v_w_ffn_conv'], 'v_b_ffn_conv': out['v_b_ffn_conv'], 'v_w_down': out['v_w_down'], 'v_g_final': out['v_g_final']}


def _loss(weights, diff, rest, loss_target):
    with _jax.named_scope("forward"):
        args = {**rest, TWIN_DIFF_INPUT: diff, **{k: w.astype(_WEIGHT_DTYPES[k]) for k, w in weights.items()}}
        y = _forward(args)
    with _jax.named_scope("loss_head"):
        err = _jnp.square(y.astype(_jnp.float32) - loss_target)
        return 0.5 * _jnp.sum(_jnp.mean(err, axis=-1)) if err.ndim else 0.5 * err


def _adamw(w, g, m, v):
    m = ADAM_B1 * m + (1.0 - ADAM_B1) * g
    v = ADAM_B2 * v + (1.0 - ADAM_B2) * _jnp.square(g)
    m_hat = m / (1.0 - ADAM_B1 ** ADAM_STEP)
    v_hat = v / (1.0 - ADAM_B2 ** ADAM_STEP)
    delta = -ADAM_LR * (m_hat / (_jnp.sqrt(v_hat) + ADAM_EPS) + ADAM_WD * w)
    return delta, m, v


def reference(x, mem, rel_bias, g_mix, w_in, w_short_conv, g_attn_out, g_conv_out, w_out, g_xattn, g_mem, w_xq, w_xk, w_xv, w_xo, g_ffn, w_up, w_ffn_conv, b_ffn_conv, w_down, g_final, loss_target, m_rel_bias, m_g_mix, m_w_in, m_w_short_conv, m_g_attn_out, m_g_conv_out, m_w_out, m_g_xattn, m_g_mem, m_w_xq, m_w_xk, m_w_xv, m_w_xo, m_g_ffn, m_w_up, m_w_ffn_conv, m_b_ffn_conv, m_w_down, m_g_final, v_rel_bias, v_g_mix, v_w_in, v_w_short_conv, v_g_attn_out, v_g_conv_out, v_w_out, v_g_xattn, v_g_mem, v_w_xq, v_w_xk, v_w_xv, v_w_xo, v_g_ffn, v_w_up, v_w_ffn_conv, v_b_ffn_conv, v_w_down, v_g_final):
    given = dict(x=x, mem=mem, rel_bias=rel_bias, g_mix=g_mix, w_in=w_in, w_short_conv=w_short_conv, g_attn_out=g_attn_out, g_conv_out=g_conv_out, w_out=w_out, g_xattn=g_xattn, g_mem=g_mem, w_xq=w_xq, w_xk=w_xk, w_xv=w_xv, w_xo=w_xo, g_ffn=g_ffn, w_up=w_up, w_ffn_conv=w_ffn_conv, b_ffn_conv=b_ffn_conv, w_down=w_down, g_final=g_final, loss_target=loss_target, m_rel_bias=m_rel_bias, m_g_mix=m_g_mix, m_w_in=m_w_in, m_w_short_conv=m_w_short_conv, m_g_attn_out=m_g_attn_out, m_g_conv_out=m_g_conv_out, m_w_out=m_w_out, m_g_xattn=m_g_xattn, m_g_mem=m_g_mem, m_w_xq=m_w_xq, m_w_xk=m_w_xk, m_w_xv=m_w_xv, m_w_xo=m_w_xo, m_g_ffn=m_g_ffn, m_w_up=m_w_up, m_w_ffn_conv=m_w_ffn_conv, m_b_ffn_conv=m_b_ffn_conv, m_w_down=m_w_down, m_g_final=m_g_final, v_rel_bias=v_rel_bias, v_g_mix=v_g_mix, v_w_in=v_w_in, v_w_short_conv=v_w_short_conv, v_g_attn_out=v_g_attn_out, v_g_conv_out=v_g_conv_out, v_w_out=v_w_out, v_g_xattn=v_g_xattn, v_g_mem=v_g_mem, v_w_xq=v_w_xq, v_w_xk=v_w_xk, v_w_xv=v_w_xv, v_w_xo=v_w_xo, v_g_ffn=v_g_ffn, v_w_up=v_w_up, v_w_ffn_conv=v_w_ffn_conv, v_b_ffn_conv=v_b_ffn_conv, v_w_down=v_w_down, v_g_final=v_g_final)
    weights = {n: given[n] for n in TWIN_WEIGHTS}
    shared = {n: given[n] for n in SHARED_INPUTS}
    per_example = {n: given[n] for n in ['x', 'mem']}
    grad_fn = _jax.value_and_grad(_loss, argnums=(0, 1))

    def one_microbatch(ex, loss_target):
        ex = dict(ex)
        diff = ex.pop(TWIN_DIFF_INPUT)
        return grad_fn(weights, diff, {**shared, **ex}, loss_target)

    if N_MICROBATCH == 1:
        loss, (grad_w, grad_x) = one_microbatch(per_example, given["loss_target"])
    else:
        def body(carry, xs):
            loss_sum, grad_sum = carry
            l_k, (gw_k, gx_k) = one_microbatch(xs[0], xs[1])
            with _jax.named_scope("update"):
                return (loss_sum + l_k, _jax.tree.map(_jnp.add, grad_sum, gw_k)), gx_k

        init = (_jnp.zeros((), _jnp.float32), _jax.tree.map(_jnp.zeros_like, weights))
        (loss, grad_w), grad_x = _jax.lax.scan(body, init, (per_example, given["loss_target"]))
    with _jax.named_scope("update"):
        delta_w, new_m, new_v = {}, {}, {}
        for n in TWIN_WEIGHTS:
            delta_w[n], new_m[n], new_v[n] = _adamw(weights[n], grad_w[n], given["m_" + n], given["v_" + n])
    return (loss, grad_x, *[grad_w[n] for n in TWIN_WEIGHTS], *[delta_w[n] for n in TWIN_WEIGHTS],
            *[new_m[n] for n in TWIN_WEIGHTS], *[new_v[n] for n in TWIN_WEIGHTS])
```

```python
import functools
import math

import numpy as np
import jax
import jax.numpy as jnp
from jax import lax
from jax.experimental import pallas as pl
from jax.experimental.pallas import tpu as pltpu

F32 = jnp.float32
BF16 = jnp.bfloat16
SDS = jax.ShapeDtypeStruct
MESH = pl.DeviceIdType.MESH

D_MODEL = 1024
ATTN_W = 512
N_HEADS = 8
BLK = 128
PATTERNS = ((128, 1), (512, 4), (2048, 16))
N_BUCKETS = 32
D_FF = 2816
N_MEM_HEADS = 4
MEM_HD = 256
EPS = 1e-6
NEG = -1e30
VMEM_LIMIT = 56 * 1024 * 1024

ADAM_LR, ADAM_B1, ADAM_B2, ADAM_EPS, ADAM_WD, ADAM_STEP = 0.001, 0.9, 0.999, 1e-08, 0.01, 10


def _cp(*sem):
    return pltpu.CompilerParams(dimension_semantics=sem, vmem_limit_bytes=VMEM_LIMIT)


def _dot(a, b):
    return jnp.dot(a, b, preferred_element_type=F32)


def _dot_nt(a, b):
    return lax.dot_general(a, b, (((1,), (1,)), ((), ())), preferred_element_type=F32)


def _dot_tn(a, b):
    return lax.dot_general(a, b, (((0,), (0,)), ((), ())), preferred_element_type=F32)


def _rsum(x):
    return jnp.sum(x, axis=1, keepdims=True)


def rmsnorm_fwd(x, g, name):
    S, Dm = x.shape
    tm = min(S, 512)

    def body(x_ref, g_ref, o_ref):
        xv = x_ref[...]
        r = lax.rsqrt(jnp.mean(xv * xv, axis=1, keepdims=True) + EPS)
        o_ref[...] = (xv * r * g_ref[...]).astype(o_ref.dtype)

    return pl.pallas_call(
        body, grid=(S // tm,), name=name,
        in_specs=[pl.BlockSpec((tm, Dm), lambda i: (i, 0)), pl.BlockSpec((1, Dm), lambda i: (0, 0))],
        out_specs=pl.BlockSpec((tm, Dm), lambda i: (i, 0)),
        out_shape=SDS((S, Dm), BF16), compiler_params=_cp("parallel"))(x, g)


def rmsnorm_bwd(x, g, dh, res, name):
    S, Dm = x.shape
    tm = min(S, 512)
    want_dx = res is not None

    def body(*refs):
        if want_dx:
            x_ref, g_ref, dh_ref, res_ref, dx_ref, dg_ref = refs
        else:
            x_ref, g_ref, dh_ref, dg_ref = refs
        i = pl.program_id(0)
        xv = x_ref[...]
        r = lax.rsqrt(jnp.mean(xv * xv, axis=1, keepdims=True) + EPS)
        xh = xv * r
        dh = dh_ref[...].astype(F32)
        if want_dx:
            gd = dh * g_ref[...]
            dx_ref[...] = res_ref[...] + r * (gd - xh * jnp.mean(gd * xh, axis=1, keepdims=True))

        @pl.when(i == 0)
        def _():
            dg_ref[...] = jnp.zeros_like(dg_ref)

        dg_ref[...] += jnp.sum(dh * xh, axis=0, keepdims=True)

    row = pl.BlockSpec((tm, Dm), lambda i: (i, 0))
    vec = pl.BlockSpec((1, Dm), lambda i: (0, 0))
    if want_dx:
        return pl.pallas_call(
            body, grid=(S // tm,), name=name, in_specs=[row, vec, row, row], out_specs=[row, vec],
            out_shape=[SDS((S, Dm), F32), SDS((1, Dm), F32)], compiler_params=_cp("arbitrary"))(x, g, dh, res)
    return pl.pallas_call(
        body, grid=(S // tm,), name=name, in_specs=[row, vec, row], out_specs=vec,
        out_shape=SDS((1, Dm), F32), compiler_params=_cp("arbitrary"))(x, g, dh)


def _col_spec(bw, tn, rows, row_of, col_of):
    if bw is None:
        return pl.BlockSpec((rows, tn), lambda *g: (row_of(*g), col_of(*g)))
    if tn % bw == 0:
        return pl.BlockSpec((tn // bw, rows, bw), lambda *g: (col_of(*g), row_of(*g), 0))
    per = bw // tn
    return pl.BlockSpec((None, rows, tn), lambda *g: (col_of(*g) // per, row_of(*g), col_of(*g) % per))


def _read_cols(ref, bw, tn):
    if bw is None or tn % bw != 0:
        return ref[...]
    if tn == bw:
        return ref[0]
    return jnp.concatenate([ref[q] for q in range(tn // bw)], axis=1)


def _write_cols(ref, bw, tn, val):
    if bw is None or tn % bw != 0:
        ref[...] = val.astype(ref.dtype)
    else:
        for q in range(tn // bw):
            ref[q] = val[:, q * bw:(q + 1) * bw].astype(ref.dtype)


def mm_nn(a, b, name, *, res=None, out_dtype=F32, out_bw=None, tm=1024, tn=256):
    M, K = a.shape
    N = b.shape[1]
    tm = min(tm, M)

    def body(*refs):
        if res is None:
            a_ref, b_ref, o_ref = refs
        else:
            a_ref, b_ref, r_ref, o_ref = refs
        acc = _dot(a_ref[...].astype(BF16), b_ref[...])
        if res is not None:
            acc = acc + r_ref[...]
        _write_cols(o_ref, out_bw, tn, acc)

    ri, ci = (lambda i, j: i), (lambda i, j: j)
    in_specs = [pl.BlockSpec((tm, K), lambda i, j: (i, 0)), pl.BlockSpec((K, tn), lambda i, j: (0, j))]
    args = [a, b]
    if res is not None:
        in_specs.append(pl.BlockSpec((tm, tn), lambda i, j: (i, j)))
        args.append(res)
    oshape = (M, N) if out_bw is None else (N // out_bw, M, out_bw)
    return pl.pallas_call(
        body, grid=(M // tm, N // tn), name=name, in_specs=in_specs,
        out_specs=_col_spec(out_bw, tn, tm, ri, ci), out_shape=SDS(oshape, out_dtype),
        compiler_params=_cp("parallel", "parallel"))(*args)


def mm_nt(a, a_bw, b, name, *, res=None, out_dtype=F32, tm=512, tn=256):
    if a_bw is None:
        M, K = a.shape
    else:
        M, K = a.shape[1], a.shape[0] * a_bw
    N = b.shape[0]
    tm = min(tm, M)

    def body(*refs):
        if res is None:
            a_ref, b_ref, o_ref = refs
        else:
            a_ref, b_ref, r_ref, o_ref = refs
        av = _read_cols(a_ref, a_bw, K).astype(BF16)
        acc = _dot_nt(av, b_ref[...])
        if res is not None:
            acc = acc + r_ref[...]
        o_ref[...] = acc.astype(o_ref.dtype)

    in_specs = [_col_spec(a_bw, K, tm, lambda i, j: i, lambda i, j: 0), pl.BlockSpec((tn, K), lambda i, j: (j, 0))]
    args = [a, b]
    if res is not None:
        in_specs.append(pl.BlockSpec((tm, tn), lambda i, j: (i, j)))
        args.append(res)
    return pl.pallas_call(
        body, grid=(M // tm, N // tn), name=name, in_specs=in_specs,
        out_specs=pl.BlockSpec((tm, tn), lambda i, j: (i, j)), out_shape=SDS((M, N), out_dtype),
        compiler_params=_cp("parallel", "parallel"))(*args)


def mm_tn(a, b, b_bw, name, *, shards=None, tm=256, tn=256, ts=1024):
    S, Ka = a.shape
    N = b.shape[1] if b_bw is None else b.shape[0] * b_bw
    ts = min(ts, S)
    tm = min(tm, Ka)

    def body(a_ref, b_ref, o_ref):
        @pl.when(pl.program_id(2) == 0)
        def _():
            o_ref[...] = jnp.zeros_like(o_ref)

        bv = _read_cols(b_ref, b_bw, tn).astype(BF16)
        o_ref[...] += _dot_tn(a_ref[...].astype(BF16), bv)

    in_specs = [pl.BlockSpec((ts, tm), lambda i, j, k: (k, i)),
                _col_spec(b_bw, tn, ts, lambda i, j, k: k, lambda i, j, k: j)]
    if shards is None:
        out_spec, oshape = pl.BlockSpec((tm, tn), lambda i, j, k: (i, j)), (Ka, N)
    else:
        per = (N // shards) // tn
        out_spec = pl.BlockSpec((None, tm, tn), lambda i, j, k: (j // per, i, j % per))
        oshape = (shards, Ka, N // shards)
    return pl.pallas_call(
        body, grid=(Ka // tm, N // tn, S // ts), name=name, in_specs=in_specs, out_specs=out_spec,
        out_shape=SDS(oshape, F32), compiler_params=_cp("parallel", "parallel", "arbitrary"))(a, b)


def _bucket_tables():
    out = np.zeros((3, 2, BLK, BLK), np.int32)
    qi = np.arange(BLK)[:, None]
    for p, (win, dil) in enumerate(PATTERNS):
        w = win // dil
        assert w == BLK
        for half in range(2):
            kj = np.arange(BLK)[None, :] + half * BLK
            steps = qi + w - kj
            valid = (steps >= 0) & (steps <= w)
            dist = np.clip(steps, 0, w) * dil
            dd = np.maximum(dist, 1).astype(np.float32)
            large = 16 + (np.log(dd / np.float32(16)) / np.float32(math.log(2048 / 16)) * np.float32(16)).astype(np.int32)
            large = np.minimum(large, N_BUCKETS - 1)
            out[p, half] = np.where(valid, np.where(dist < 16, dist, large), -1)
    return out


def bias_tables(rel_bias):
    bkt = jnp.asarray(_bucket_tables())

    def body(rb_ref, bkt_ref, o_ref):
        h = pl.program_id(1)
        for half in range(2):
            bk = bkt_ref[half]
            acc = jnp.full((BLK, BLK), NEG, F32)
            for b in range(N_BUCKETS):
                acc = jnp.where(bk == b, rb_ref[h, b], acc)
            o_ref[half] = acc

    return pl.pallas_call(
        body, grid=(3, N_HEADS), name="bias_tables",
        in_specs=[pl.BlockSpec(memory_space=pltpu.SMEM),
                  pl.BlockSpec((None, 2, BLK, BLK), lambda p, h: (p, 0, 0, 0))],
        out_specs=pl.BlockSpec((None, None, 2, BLK, BLK), lambda p, h: (p, h, 0, 0, 0)),
        out_shape=SDS((3, N_HEADS, 2, BLK, BLK), F32), compiler_params=_cp("parallel", "parallel"))(rel_bias, bkt)


def bias_tables_bwd(ds_sum):
    bkt = jnp.asarray(_bucket_tables())

    def body(ds_ref, bkt_ref, o_ref):
        h = pl.program_id(0)

        @pl.when(h == 0)
        def _():
            o_ref[...] = jnp.zeros_like(o_ref)

        rows = lax.broadcasted_iota(jnp.int32, (N_HEADS, N_BUCKETS), 0)
        cols = lax.broadcasted_iota(jnp.int32, (N_HEADS, N_BUCKETS), 1)
        acc = jnp.zeros((N_HEADS, N_BUCKETS), F32)
        for b in range(N_BUCKETS):
            tot = jnp.zeros((1, 1), F32)
            for p in range(3):
                for half in range(2):
                    t = jnp.where(bkt_ref[p, half] == b, ds_ref[p, half], 0.0)
                    tot = tot + jnp.sum(_rsum(t), axis=0, keepdims=True)
            acc = acc + jnp.where((rows == h) & (cols == b), tot, 0.0)
        o_ref[...] += acc

    return pl.pallas_call(
        body, grid=(N_HEADS,), name="bias_tables_bwd",
        in_specs=[pl.BlockSpec((3, None, 2, BLK, BLK), lambda h: (0, h, 0, 0, 0)),
                  pl.BlockSpec((3, 2, BLK, BLK), lambda h: (0, 0, 0, 0))],
        out_specs=pl.BlockSpec((N_HEADS, N_BUCKETS), lambda h: (0, 0)),
        out_shape=SDS((N_HEADS, N_BUCKETS), F32), compiler_params=_cp("arbitrary"))(ds_sum, bkt)


def _rows(start, dil):
    if dil == 1:
        return pl.ds(pl.multiple_of(start, BLK), BLK)
    return pl.ds(start, BLK, stride=dil)


def _block_rows(i, dil):
    b, r = i // dil, i % dil
    cur = b * (BLK * dil) + r
    prv = jnp.maximum(b - 1, 0) * (BLK * dil) + r
    return _rows(cur, dil), _rows(prv, dil), b > 0


def attn_fwd(qkv, bias):
    S = qkv.shape[2]
    nblk = S // BLK

    def body(qkv_ref, bias_ref, o_ref, lse_ref, l_ref):
        lane = lax.broadcasted_iota(jnp.int32, (BLK, BLK), 1)
        lo = lane < 64
        masks = (lo, jnp.logical_not(lo))
        for p, (_, dil) in enumerate(PATTERNS):
            def step(i, carry, p=p, dil=dil):
                rc, rp, has_prev = _block_rows(i, dil)
                pen = jnp.where(has_prev, 0.0, NEG)
                q_ref, k_ref, v_ref = qkv_ref.at[0], qkv_ref.at[1], qkv_ref.at[2]
                q2 = q_ref[rc, :] * 0.125
                kc = k_ref[rc, :].astype(BF16)
                kp = k_ref[rp, :].astype(BF16)
                vc = v_ref[rc, :]
                vp = v_ref[rp, :]
                if p > 0:
                    acc_old, m_old, l_old = o_ref[rc, :], lse_ref[rc, :], l_ref[rc, :]
                pv = jnp.zeros((BLK, BLK), F32)
                m_new, l_new, alpha = [], [], []
                for h in range(2):
                    qh = jnp.where(masks[h], q2, 0.0).astype(BF16)
                    sp = _dot_nt(qh, kp) + bias_ref[p, h, 0] + pen
                    sc = _dot_nt(qh, kc) + bias_ref[p, h, 1]
                    mn = jnp.maximum(jnp.max(sp, axis=1, keepdims=True), jnp.max(sc, axis=1, keepdims=True))
                    if p > 0:
                        mo = m_old[:, 64 * h:64 * h + 1]
                        mn = jnp.maximum(mn, mo)
                        al = jnp.exp(mo - mn)
                        alpha.append(al)
                    pp = jnp.exp(sp - mn)
                    pc = jnp.exp(sc - mn)
                    ln = _rsum(pp) + _rsum(pc)
                    if p > 0:
                        ln = ln + al * l_old[:, 64 * h:64 * h + 1]
                    vhp = jnp.where(masks[h], vp, 0.0).astype(BF16)
                    vhc = jnp.where(masks[h], vc, 0.0).astype(BF16)
                    pv = pv + _dot(pp.astype(BF16), vhp) + _dot(pc.astype(BF16), vhc)
                    m_new.append(mn)
                    l_new.append(ln)
                if p > 0:
                    pv = pv + acc_old * jnp.where(lo, alpha[0], alpha[1])
                o_ref[rc, :] = pv
                lse_ref[rc, :] = jnp.where(lo, m_new[0], m_new[1])
                l_ref[rc, :] = jnp.where(lo, l_new[0], l_new[1])
                return carry

            lax.fori_loop(0, nblk, step, 0)

        def fin(i, carry):
            rows = pl.ds(pl.multiple_of(i * 512, 512), 512)
            lv = l_ref[rows, :]
            o_ref[rows, :] = o_ref[rows, :] / lv
            lse_ref[rows, :] = lse_ref[rows, :] + jnp.log(lv)
            return carry

        lax.fori_loop(0, S // 512, fin, 0)

    return pl.pallas_call(
        body, grid=(4,), name="attn_fwd",
        in_specs=[pl.BlockSpec((3, None, S, BLK), lambda g: (0, g, 0, 0)),
                  pl.BlockSpec((3, 2, 2, BLK, BLK), lambda g: (0, g, 0, 0, 0))],
        out_specs=[pl.BlockSpec((None, S, BLK), lambda g: (g, 0, 0)), pl.BlockSpec((None, S, BLK), lambda g: (g, 0, 0))],
        out_shape=[SDS((4, S, BLK), F32), SDS((4, S, BLK), F32)],
        scratch_shapes=[pltpu.VMEM((S, BLK), F32)], compiler_params=_cp("parallel"))(qkv, bias)


def attn_bwd(qkv, bias, o, lse, do):
    S = qkv.shape[2]
    nblk = S // BLK

    def body(qkv_ref, bias_ref, o_ref, lse_ref, do_ref, dqkv_ref, ds_ref):
        lane = lax.broadcasted_iota(jnp.int32, (BLK, BLK), 1)
        lo = lane < 64
        masks = (lo, jnp.logical_not(lo))
        dqkv_ref[...] = jnp.zeros_like(dqkv_ref)
        ds_ref[...] = jnp.zeros_like(ds_ref)
        for p, (_, dil) in enumerate(PATTERNS):
            def step(i, carry, p=p, dil=dil):
                rc, rp, has_prev = _block_rows(i, dil)
                pen = jnp.where(has_prev, 0.0, NEG)
                q_ref, k_ref, v_ref = qkv_ref.at[0], qkv_ref.at[1], qkv_ref.at[2]
                q2 = q_ref[rc, :] * 0.125
                kc = k_ref[rc, :].astype(BF16)
                kp = k_ref[rp, :].astype(BF16)
                vc = v_ref[rc, :].astype(BF16)
                vp = v_ref[rp, :].astype(BF16)
                dot = do_ref[rc, :]
                lset = lse_ref[rc, :]
                prod = dot * o_ref[rc, :]
                dq = jnp.zeros((BLK, BLK), F32)
                dkc = jnp.zeros((BLK, BLK), F32)
                dkp = jnp.zeros((BLK, BLK), F32)
                dvc = jnp.zeros((BLK, BLK), F32)
                dvp = jnp.zeros((BLK, BLK), F32)
                for h in range(2):
                    qh = jnp.where(masks[h], q2, 0.0).astype(BF16)
                    doh = jnp.where(masks[h], dot, 0.0).astype(BF16)
                    delta = _rsum(jnp.where(masks[h], prod, 0.0))
                    lh = lset[:, 64 * h:64 * h + 1]
                    pp = jnp.exp(_dot_nt(qh, kp) + bias_ref[p, h, 0] + pen - lh)
                    pc = jnp.exp(_dot_nt(qh, kc) + bias_ref[p, h, 1] - lh)
                    dsp = pp * (_dot_nt(doh, vp) - delta)
                    dsc = pc * (_dot_nt(doh, vc) - delta)
                    ds_ref[p, h, 0] += dsp
                    ds_ref[p, h, 1] += dsc
                    dspb, dscb = dsp.astype(BF16), dsc.astype(BF16)
                    dq = dq + jnp.where(masks[h], _dot(dspb, kp) + _dot(dscb, kc), 0.0)
                    dkp = dkp + _dot_tn(dspb, qh)
                    dkc = dkc + _dot_tn(dscb, qh)
                    dvp = dvp + _dot_tn(pp.astype(BF16), doh)
                    dvc = dvc + _dot_tn(pc.astype(BF16), doh)
                dq_ref, dk_ref, dv_ref = dqkv_ref.at[0], dqkv_ref.at[1], dqkv_ref.at[2]
                dq_ref[rc, :] += dq * 0.125
                dk_ref[rp, :] += dkp
                dk_ref[rc, :] += dkc
                dv_ref[rp, :] += dvp
                dv_ref[rc, :] += dvc
                return carry

            lax.fori_loop(0, nblk, step, 0)

    blk = pl.BlockSpec((None, S, BLK), lambda g: (g, 0, 0))
    return pl.pallas_call(
        body, grid=(4,), name="attn_bwd",
        in_specs=[pl.BlockSpec((3, None, S, BLK), lambda g: (0, g, 0, 0)),
                  pl.BlockSpec((3, 2, 2, BLK, BLK), lambda g: (0, g, 0, 0, 0)), blk, blk, blk],
        out_specs=[pl.BlockSpec((None, 3, None, S, BLK), lambda g: (0, 0, g, 0, 0)),
                   pl.BlockSpec((3, 2, 2, BLK, BLK), lambda g: (0, g, 0, 0, 0))],
        out_shape=[SDS((2, 3, 4, S, BLK), F32), SDS((3, N_HEADS, 2, BLK, BLK), F32)],
        compiler_params=_cp("parallel"))(qkv, bias, o, lse, do)


def _shift_down(u, k, halo):
    n = u.shape[0]
    row = lax.broadcasted_iota(jnp.int32, u.shape, 0)
    out = pltpu.roll(u, k, 0)
    hn = halo.shape[0]
    for j in range(k):
        out = jnp.where(row == j, halo[hn - k + j:hn - k + j + 1, :], out)
    return out


def _shift_up(u, k, halo):
    n = u.shape[0]
    row = lax.broadcasted_iota(jnp.int32, u.shape, 0)
    out = pltpu.roll(u, n - k, 0)
    for j in range(k):
        out = jnp.where(row == n - k + j, halo[j:j + 1, :], out)
    return out


def _conv3(u, halo, w0, w1, w2):
    return _shift_down(u, 2, halo) * w0 + _shift_down(u, 1, halo) * w1 + u * w2


def _conv3_bwd_in(d, halo, w0, w1, w2):
    return d * w2 + _shift_up(d, 1, halo) * w1 + _shift_up(d, 2, halo) * w0


def _colsum(x):
    return jnp.sum(x, axis=0, keepdims=True)


def _mixer_specs(S, tm):
    conv = pl.BlockSpec((None, 12, tm, BLK), lambda i: (1, 0, i, 0))
    halo = pl.BlockSpec((None, 12, 8, BLK), lambda i: (1, 0, jnp.maximum(i * (tm // 8) - 1, 0), 0))
    ob = pl.BlockSpec((4, tm, BLK), lambda i: (0, i, 0))
    return conv, halo, ob


def _mixer_recompute(i, o_ref, pr_ref, ph_ref, w_ref):
    ob = [o_ref[q] for q in range(4)]
    gb = [pr_ref[q] for q in range(4)]
    gc = [pr_ref[4 + q] for q in range(4)]
    xi = [pr_ref[8 + q] for q in range(4)]
    keep = jnp.where(i > 0, 1.0, 0.0)
    u = [gc[q] * xi[q] for q in range(4)]
    hu = [ph_ref[4 + q] * ph_ref[8 + q] * keep for q in range(4)]
    w = [[w_ref[k:k + 1, q * BLK:(q + 1) * BLK] for k in range(3)] for q in range(4)]
    cv = [_conv3(u[q], hu[q], *w[q]) for q in range(4)]
    return ob, gb, gc, xi, u, hu, cv, w


def _rms_blocks(blocks):
    ss = sum(_rsum(b * b) for b in blocks)
    return lax.rsqrt(ss / (BLK * len(blocks)) + EPS)


def mixer_post_fwd(o, proj, w_sc, g_a, g_c):
    S = o.shape[1]
    tm = 256

    def body(o_ref, pr_ref, ph_ref, w_ref, ga_ref, gc_ref, m_ref):
        i = pl.program_id(0)
        ob, gb, _, _, _, _, cv, _ = _mixer_recompute(i, o_ref, pr_ref, ph_ref, w_ref)
        conv = [gb[q] * cv[q] for q in range(4)]
        ra, rc = _rms_blocks(ob), _rms_blocks(conv)
        for q in range(4):
            sl = slice(q * BLK, (q + 1) * BLK)
            m_ref[:, q * BLK:(q + 1) * BLK] = (ob[q] * ra * ga_ref[:, sl]).astype(BF16)
            m_ref[:, ATTN_W + q * BLK:ATTN_W + (q + 1) * BLK] = (conv[q] * rc * gc_ref[:, sl]).astype(BF16)

    conv_s, halo_s, ob_s = _mixer_specs(S, tm)
    full = lambda r, c: pl.BlockSpec((r, c), lambda i: (0, 0))
    return pl.pallas_call(
        body, grid=(S // tm,), name="mixer_post_fwd",
        in_specs=[ob_s, conv_s, halo_s, full(3, 512), full(1, 512), full(1, 512)],
        out_specs=pl.BlockSpec((tm, D_MODEL), lambda i: (i, 0)), out_shape=SDS((S, D_MODEL), BF16),
        compiler_params=_cp("parallel"))(o, proj, proj, w_sc, g_a, g_c)


def mixer_post_bwd_a(dmixed, o, proj, w_sc, g_a, g_c):
    S = o.shape[1]
    tm = 256

    def body(dm_ref, o_ref, pr_ref, ph_ref, w_ref, ga_ref, gc_ref, do_ref, dgb_ref, dcv_ref, dga_ref, dgc_ref):
        i = pl.program_id(0)
        ob, gb, _, _, _, _, cv, _ = _mixer_recompute(i, o_ref, pr_ref, ph_ref, w_ref)
        conv = [gb[q] * cv[q] for q in range(4)]
        ra, rc = _rms_blocks(ob), _rms_blocks(conv)

        @pl.when(i == 0)
        def _():
            dga_ref[...] = jnp.zeros_like(dga_ref)
            dgc_ref[...] = jnp.zeros_like(dgc_ref)

        for blocks, r, g_ref, off, dg_ref, is_attn in ((ob, ra, ga_ref, 0, dga_ref, True), (conv, rc, gc_ref, ATTN_W, dgc_ref, False)):
            xh = [blocks[q] * r for q in range(4)]
            dm = [dm_ref[:, off + q * BLK:off + (q + 1) * BLK].astype(F32) for q in range(4)]
            gd = [dm[q] * g_ref[:, q * BLK:(q + 1) * BLK] for q in range(4)]
            mean = sum(_rsum(gd[q] * xh[q]) for q in range(4)) / (4 * BLK)
            for q in range(4):
                dg_ref[:, q * BLK:(q + 1) * BLK] += _colsum(dm[q] * xh[q])
                dx = r * (gd[q] - xh[q] * mean)
                if is_attn:
                    do_ref[q] = dx
                else:
                    dgb_ref[q] = dx * cv[q]
                    dcv_ref[q] = dx * gb[q]

    conv_s, halo_s, ob_s = _mixer_specs(S, tm)
    full = lambda r, c: pl.BlockSpec((r, c), lambda i: (0, 0))
    return pl.pallas_call(
        body, grid=(S // tm,), name="mixer_post_bwd_a",
        in_specs=[pl.BlockSpec((tm, D_MODEL), lambda i: (i, 0)), ob_s, conv_s, halo_s, full(3, 512), full(1, 512), full(1, 512)],
        out_specs=[ob_s, ob_s, ob_s, full(1, 512), full(1, 512)],
        out_shape=[SDS((4, S, BLK), F32)] * 3 + [SDS((1, 512), F32)] * 2,
        compiler_params=_cp("arbitrary"))(dmixed, o, proj, proj, w_sc, g_a, g_c)


def mixer_post_bwd_b(dproj, dgb, dcv, proj, w_sc):
    S = proj.shape[2]
    tm = 256
    last = S // 8 - 1

    def body(dp_in, dgb_ref, dcv_ref, dn_ref, pr_ref, ph_ref, w_ref, dp_ref, dw_ref):
        i = pl.program_id(0)
        keep_prev = jnp.where(i > 0, 1.0, 0.0)
        keep_next = jnp.where(i < pl.num_programs(0) - 1, 1.0, 0.0)

        @pl.when(i == 0)
        def _():
            dw_ref[...] = jnp.zeros_like(dw_ref)

        for q in range(4):
            sl = slice(q * BLK, (q + 1) * BLK)
            gc, xi = pr_ref[4 + q], pr_ref[8 + q]
            u = gc * xi
            hu = ph_ref[4 + q] * ph_ref[8 + q] * keep_prev
            dcv = dcv_ref[q]
            w0, w1, w2 = (w_ref[k:k + 1, sl] for k in range(3))
            du = _conv3_bwd_in(dcv, dn_ref[q] * keep_next, w0, w1, w2)
            dw_ref[0:1, sl] += _colsum(dcv * _shift_down(u, 2, hu))
            dw_ref[1:2, sl] += _colsum(dcv * _shift_down(u, 1, hu))
            dw_ref[2:3, sl] += _colsum(dcv * u)
            dp_ref[q] = dgb_ref[q]
            dp_ref[4 + q] = du * xi
            dp_ref[8 + q] = du * gc

    conv_s, halo_s, ob_s = _mixer_specs(S, tm)
    nxt = pl.BlockSpec((4, 8, BLK), lambda i: (0, jnp.minimum((i + 1) * (tm // 8), last), 0))
    full = lambda r, c: pl.BlockSpec((r, c), lambda i: (0, 0))
    return pl.pallas_call(
        body, grid=(S // tm,), name="mixer_post_bwd_b",
        in_specs=[pl.BlockSpec(memory_space=pl.ANY), ob_s, ob_s, nxt, conv_s, halo_s, full(3, 512)],
        out_specs=[conv_s, full(3, 512)],
        out_shape=[SDS(dproj.shape, F32), SDS((3, 512), F32)],
        input_output_aliases={0: 0}, compiler_params=_cp("arbitrary"))(dproj, dgb, dcv, dcv, proj, proj, w_sc)


def xattn_fwd(q, k, v):
    S = q.shape[0]
    tm = 512
    scale = MEM_HD ** -0.5

    def body(q_ref, k_ref, v_ref, o_ref):
        for h in range(N_MEM_HEADS):
            sl = slice(h * MEM_HD, (h + 1) * MEM_HD)
            s = _dot_nt(q_ref[:, sl], k_ref[:, sl]) * scale
            p = jnp.exp(s - jnp.max(s, axis=1, keepdims=True))
            p = p / _rsum(p)
            o_ref[:, sl] = _dot(p.astype(BF16), v_ref[:, sl]).astype(BF16)

    row = pl.BlockSpec((tm, D_MODEL), lambda i: (i, 0))
    kv = pl.BlockSpec(k.shape, lambda i: (0, 0))
    return pl.pallas_call(body, grid=(S // tm,), name="xattn_fwd", in_specs=[row, kv, kv], out_specs=row,
                          out_shape=SDS((S, D_MODEL), BF16), compiler_params=_cp("parallel"))(q, k, v)


def xattn_bwd(q, k, v, do):
    S = q.shape[0]
    tm = 512
    scale = MEM_HD ** -0.5

    def body(q_ref, k_ref, v_ref, do_ref, dq_ref, dk_ref, dv_ref):
        @pl.when(pl.program_id(0) == 0)
        def _():
            dk_ref[...] = jnp.zeros_like(dk_ref)
            dv_ref[...] = jnp.zeros_like(dv_ref)

        for h in range(N_MEM_HEADS):
            sl = slice(h * MEM_HD, (h + 1) * MEM_HD)
            qh, kh, vh, doh = q_ref[:, sl], k_ref[:, sl], v_ref[:, sl], do_ref[:, sl]
            s = _dot_nt(qh, kh) * scale
            p = jnp.exp(s - jnp.max(s, axis=1, keepdims=True))
            p = p / _rsum(p)
            dp = _dot_nt(doh, vh)
            ds = (p * (dp - _rsum(p * dp)) * scale).astype(BF16)
            dq_ref[:, sl] = _dot(ds, kh).astype(BF16)
            dk_ref[:, sl] += _dot_tn(ds, qh)
            dv_ref[:, sl] += _dot_tn(p.astype(BF16), doh)

    row = pl.BlockSpec((tm, D_MODEL), lambda i: (i, 0))
    kv = pl.BlockSpec(k.shape, lambda i: (0, 0))
    return pl.pallas_call(body, grid=(S // tm,), name="xattn_bwd", in_specs=[row, kv, kv, row], out_specs=[row, kv, kv],
                          out_shape=[SDS((S, D_MODEL), BF16), SDS(k.shape, F32), SDS(k.shape, F32)],
                          compiler_params=_cp("arbitrary"))(q, k, v, do)


FFN_TM, FFN_TC = 512, 256


def _ffn_specs(S, order):
    tm, tc = FFN_TM, FFN_TC
    ij = (lambda a, b: (a, b)) if order == "ij" else (lambda a, b: (b, a))
    blk = pl.BlockSpec((2, tm, tc), lambda a, b: (0,) + ij(a, b))
    prev = pl.BlockSpec((2, 16, tc), lambda a, b: (0, jnp.maximum(ij(a, b)[0] * (tm // 16) - 1, 0), ij(a, b)[1]))
    nxt = pl.BlockSpec((2, 16, tc), lambda a, b: (0, jnp.minimum((ij(a, b)[0] + 1) * (tm // 16), S // 16 - 1), ij(a, b)[1]))
    wsp = pl.BlockSpec((2, 3, tc), lambda a, b: (0, 0, ij(a, b)[1]))
    bsp = pl.BlockSpec((2, 1, tc), lambda a, b: (0, 0, ij(a, b)[1]))
    act = pl.BlockSpec((tm, tc), lambda a, b: ij(a, b))
    return blk, prev, nxt, wsp, bsp, act


def _ffn_up(i, up_ref, uh_ref, w_ref, b_ref):
    keep = jnp.where(i > 0, 1.0, 0.0)
    out = []
    for half in range(2):
        u = up_ref[half].astype(F32)
        hu = uh_ref[half].astype(F32) * keep
        w0, w1, w2 = (w_ref[half, k:k + 1, :] for k in range(3))
        out.append((_conv3(u, hu, w0, w1, w2) + b_ref[half], u, hu))
    return out


def ffn_act_fwd(up_pre, w_fc, b_fc):
    S = up_pre.shape[1]

    def body(up_ref, uh_ref, w_ref, b_ref, a_ref):
        (g, _, _), (v, _, _) = _ffn_up(pl.program_id(0), up_ref, uh_ref, w_ref, b_ref)
        a_ref[...] = (g * jax.nn.sigmoid(g) * v).astype(BF16)

    blk, prev, _, wsp, bsp, act = _ffn_specs(S, "ij")
    return pl.pallas_call(body, grid=(S // FFN_TM, D_FF // FFN_TC), name="ffn_act_fwd",
                          in_specs=[blk, prev, wsp, bsp], out_specs=act, out_shape=SDS((S, D_FF), BF16),
                          compiler_params=_cp("parallel", "parallel"))(up_pre, up_pre, w_fc, b_fc)


def ffn_act_bwd_a(dact, up_pre, w_fc, b_fc):
    S = up_pre.shape[1]

    def body(da_ref, up_ref, uh_ref, w_ref, b_ref, du_ref, db_ref):
        i = pl.program_id(1)
        (g, _, _), (v, _, _) = _ffn_up(i, up_ref, uh_ref, w_ref, b_ref)
        da = da_ref[...].astype(F32)
        sg = jax.nn.sigmoid(g)
        dg = da * v * (sg * (1.0 + g * (1.0 - sg)))
        dv = da * g * sg

        @pl.when(i == 0)
        def _():
            db_ref[...] = jnp.zeros_like(db_ref)

        du_ref[0] = dg.astype(BF16)
        du_ref[1] = dv.astype(BF16)
        db_ref[0] += _colsum(dg)
        db_ref[1] += _colsum(dv)

    blk, prev, _, wsp, bsp, act = _ffn_specs(S, "ji")
    return pl.pallas_call(body, grid=(D_FF // FFN_TC, S // FFN_TM), name="ffn_act_bwd_a",
                          in_specs=[act, blk, prev, wsp, bsp], out_specs=[blk, bsp],
                          out_shape=[SDS((2, S, D_FF), BF16), SDS((2, 1, D_FF), F32)],
                          compiler_params=_cp("parallel", "arbitrary"))(dact, up_pre, up_pre, w_fc, b_fc)


def ffn_act_bwd_b(d_up, up_pre, w_fc):
    S = up_pre.shape[1]

    def body(du_ref, dn_ref, up_ref, uh_ref, w_ref, dp_ref, dw_ref):
        i = pl.program_id(1)
        keep_prev = jnp.where(i > 0, 1.0, 0.0)
        keep_next = jnp.where(i < pl.num_programs(1) - 1, 1.0, 0.0)

        @pl.when(i == 0)
        def _():
            dw_ref[...] = jnp.zeros_like(dw_ref)

        for half in range(2):
            d = du_ref[half].astype(F32)
            dn = dn_ref[half].astype(F32) * keep_next
            u = up_ref[half].astype(F32)
            hu = uh_ref[half].astype(F32) * keep_prev
            w0, w1, w2 = (w_ref[half, k:k + 1, :] for k in range(3))
            dp_ref[half] = _conv3_bwd_in(d, dn, w0, w1, w2).astype(BF16)
            dw_ref[half, 0:1, :] += _colsum(d * _shift_down(u, 2, hu))
            dw_ref[half, 1:2, :] += _colsum(d * _shift_down(u, 1, hu))
            dw_ref[half, 2:3, :] += _colsum(d * u)

    blk, prev, nxt, wsp, _, _ = _ffn_specs(S, "ji")
    return pl.pallas_call(body, grid=(D_FF // FFN_TC, S // FFN_TM), name="ffn_act_bwd_b",
                          in_specs=[blk, nxt, blk, prev, wsp], out_specs=[blk, wsp],
                          out_shape=[SDS((2, S, D_FF), BF16), SDS((2, 3, D_FF), F32)],
                          compiler_params=_cp("parallel", "arbitrary"))(d_up, d_up, up_pre, up_pre, w_fc)


def final_loss(x3, g, target):
    S, Dm = x3.shape
    tm = 512

    def body(x_ref, g_ref, t_ref, loss_ref, dx_ref, dg_ref):
        i = pl.program_id(0)
        xv = x_ref[...]
        r = lax.rsqrt(jnp.mean(xv * xv, axis=1, keepdims=True) + EPS)
        xh = xv * r
        err = xh * g_ref[...] - t_ref[...]
        dy = err / Dm
        gd = dy * g_ref[...]
        dx_ref[...] = r * (gd - xh * jnp.mean(gd * xh, axis=1, keepdims=True))

        @pl.when(i == 0)
        def _():
            dg_ref[...] = jnp.zeros_like(dg_ref)
            loss_ref[...] = jnp.zeros_like(loss_ref)

        dg_ref[...] += _colsum(dy * xh)
        loss_ref[...] += 0.5 * _colsum(jnp.mean(err * err, axis=1, keepdims=True))

    row = pl.BlockSpec((tm, Dm), lambda i: (i, 0))
    vec = pl.BlockSpec((1, Dm), lambda i: (0, 0))
    one = pl.BlockSpec((1, 1), lambda i: (0, 0))
    return pl.pallas_call(body, grid=(S // tm,), name="final_loss", in_specs=[row, vec, row], out_specs=[one, row, vec],
                          out_shape=[SDS((1, 1), F32), SDS((S, Dm), F32), SDS((1, Dm), F32)],
                          compiler_params=_cp("arbitrary"))(x3, g, target)


def local_step(x, mem, target, wb, ws):
    S = x.shape[0]
    assert S % 2048 == 0
    w_fc = ws["w_ffn_conv"].reshape(3, 2, D_FF).transpose(1, 0, 2)
    b_fc = ws["b_ffn_conv"].reshape(2, 1, D_FF)

    bias = bias_tables(ws["rel_bias"])
    h1 = rmsnorm_fwd(x, ws["g_mix"], "norm_mix")
    proj = mm_nn(h1, wb["w_in"], "proj", out_bw=BLK)
    qkv = proj.reshape(2, 3, 4, S, BLK)[0]
    o, lse = attn_fwd(qkv, bias)
    proj4 = proj.reshape(2, 12, S, BLK)
    mixed = mixer_post_fwd(o, proj4, ws["w_short_conv"], ws["g_attn_out"], ws["g_conv_out"])
    x1 = mm_nn(mixed, wb["w_out"], "out_proj", res=x)
    h2 = rmsnorm_fwd(x1, ws["g_xattn"], "norm_xattn")
    mem_n = rmsnorm_fwd(mem, ws["g_mem"], "norm_mem")
    xq = mm_nn(h2, wb["w_xq"], "xq", out_dtype=BF16)
    xk = mm_nn(mem_n, wb["w_xk"], "xk", out_dtype=BF16)
    xv = mm_nn(mem_n, wb["w_xv"], "xv", out_dtype=BF16)
    xo = xattn_fwd(xq, xk, xv)
    x2 = mm_nn(xo, wb["w_xo"], "xo_proj", res=x1)
    h3 = rmsnorm_fwd(x2, ws["g_ffn"], "norm_ffn")
    up_pre = mm_nn(h3, wb["w_up"], "up_proj", out_dtype=BF16, out_bw=D_FF)
    act = ffn_act_fwd(up_pre, w_fc, b_fc)
    x3 = mm_nn(act, wb["w_down"], "down_proj", res=x2, tm=512)
    loss, dx3, dg_final = final_loss(x3, ws["g_final"].reshape(1, -1), target)

    gb, gs = {}, {"g_final": dg_final}
    gb["w_down"] = mm_tn(act, dx3, None, "dw_down")
    dact = mm_nt(dx3, None, wb["w_down"], "d_act", out_dtype=BF16)
    d_up, db_fc = ffn_act_bwd_a(dact, up_pre, w_fc, b_fc)
    d_up_pre, dw_fc = ffn_act_bwd_b(d_up, up_pre, w_fc)
    gs["b_ffn_conv"] = db_fc.reshape(1, 2 * D_FF)
    gs["w_ffn_conv"] = dw_fc.transpose(1, 0, 2).reshape(3, 2 * D_FF)
    gb["w_up"] = mm_tn(h3, d_up_pre, D_FF, "dw_up", shards=4, tn=1408)
    dh3 = mm_nt(d_up_pre, D_FF, wb["w_up"], "d_h3")
    dx2, gs["g_ffn"] = rmsnorm_bwd(x2, ws["g_ffn"], dh3, dx3, "norm_ffn_bwd")
    gb["w_xo"] = mm_tn(xo, dx2, None, "dw_xo")
    dxo = mm_nt(dx2, None, wb["w_xo"], "d_xo", out_dtype=BF16)
    dxq, dxk, dxv = xattn_bwd(xq, xk, xv, dxo)
    gb["w_xq"] = mm_tn(h2, dxq, None, "dw_xq")
    gb["w_xk"] = mm_tn(mem_n, dxk, None, "dw_xk")
    gb["w_xv"] = mm_tn(mem_n, dxv, None, "dw_xv")
    dh2 = mm_nt(dxq, None, wb["w_xq"], "d_h2")
    dmem_n = mm_nt(dxk, None, wb["w_xk"], "d_memk")
    dmem_n = mm_nt(dxv, None, wb["w_xv"], "d_memv", res=dmem_n)
    gs["g_mem"] = rmsnorm_bwd(mem, ws["g_mem"], dmem_n, None, "norm_mem_bwd")
    dx1, gs["g_xattn"] = rmsnorm_bwd(x1, ws["g_xattn"], dh2, dx2, "norm_xattn_bwd")
    gb["w_out"] = mm_tn(mixed, dx1, None, "dw_out")
    dmixed = mm_nt(dx1, None, wb["w_out"], "d_mixed")
    do, dgb, dcv, gs["g_attn_out"], gs["g_conv_out"] = mixer_post_bwd_a(
        dmixed, o, proj4, ws["w_short_conv"], ws["g_attn_out"], ws["g_conv_out"])
    dproj, ds_sum = attn_bwd(qkv, bias, o, lse, do)
    gs["rel_bias"] = bias_tables_bwd(ds_sum)
    dproj, gs["w_short_conv"] = mixer_post_bwd_b(dproj.reshape(2, 12, S, BLK), dgb, dcv, proj4, ws["w_short_conv"])
    dproj = dproj.reshape(24, S, BLK)
    gb["w_in"] = mm_tn(h1, dproj, BLK, "dw_in", shards=4)
    dh1 = mm_nt(dproj, BLK, wb["w_in"], "d_h1")
    grad_x, gs["g_mix"] = rmsnorm_bwd(x, ws["g_mix"], dh1, dx1, "norm_mix_bwd")
    return loss, grad_x, gb, gs


def adamw(w, g, m, v, name):
    R, C = w.shape
    tr = R
    for cand in (256, 352):
        if R > cand and R % cand == 0:
            tr = cand
            break

    def body(w_ref, g_ref, m_ref, v_ref, d_ref, nm_ref, nv_ref):
        gv = g_ref[...]
        mn = ADAM_B1 * m_ref[...] + (1.0 - ADAM_B1) * gv
        vn = ADAM_B2 * v_ref[...] + (1.0 - ADAM_B2) * (gv * gv)
        m_hat = mn / (1.0 - ADAM_B1 ** ADAM_STEP)
        v_hat = vn / (1.0 - ADAM_B2 ** ADAM_STEP)
        d_ref[...] = -ADAM_LR * (m_hat / (jnp.sqrt(v_hat) + ADAM_EPS) + ADAM_WD * w_ref[...])
        nm_ref[...] = mn
        nv_ref[...] = vn

    blk = pl.BlockSpec((tr, C), lambda i: (i, 0))
    return pl.pallas_call(body, grid=(R // tr,), name=name, in_specs=[blk] * 4, out_specs=[blk] * 3,
                          out_shape=[SDS((R, C), F32)] * 3, compiler_params=_cp("parallel"))(w, g, m, v)


BIG = ("w_in", "w_out", "w_xq", "w_xk", "w_xv", "w_xo", "w_up", "w_down")
COL_SHARDED = ("w_in", "w_up")
N_BIG = len(BIG)
ANY = pl.BlockSpec(memory_space=pl.ANY)


def _place():
    x, y, c = lax.axis_index("x"), lax.axis_index("y"), lax.axis_index("c")
    chips = [(1 - x, y), (x, 1 - y), (1 - x, 1 - y)]
    return x, y, c, chips


def _window(full, name, R, C, shard, half):
    r0, nr = (0, R) if half is None else (half * (R // 2), R // 2)
    if name in COL_SHARDED:
        return full.at[pl.ds(r0, nr), pl.ds(shard * C, C)]
    return full.at[pl.ds(shard * R + r0, nr), :]


def gather_weights(shards):
    shapes = [shards[n].shape for n in BIG]

    def body(*refs):
        src, dst = refs[:N_BIG], refs[N_BIG:2 * N_BIG]
        send, recv, fsend, frecv, lsem = refs[2 * N_BIG:]
        x, y, c, chips = _place()
        mine = 2 * x + y
        sib = (x, y, 1 - c)
        pending = []
        for w, n in enumerate(BIG):
            R, C = shapes[w]
            cp = pltpu.make_async_copy(src[w], _window(dst[w], n, R, C, mine, None), lsem.at[w])
            cp.start()
            pending.append(cp)
        sends = []
        for w, n in enumerate(BIG):
            R, C = shapes[w]
            for j, chip in enumerate(chips):
                cp = pltpu.make_async_remote_copy(
                    src_ref=src[w].at[pl.ds(c * (R // 2), R // 2), :], dst_ref=_window(dst[w], n, R, C, mine, c),
                    send_sem=send.at[w, j], recv_sem=recv.at[w, j], device_id=(*chip, c), device_id_type=MESH)
                cp.start()
                sends.append(cp)
        for w, n in enumerate(BIG):
            R, C = shapes[w]
            for j, chip in enumerate(chips):
                landed = _window(dst[w], n, R, C, 2 * chip[0] + chip[1], c)
                pltpu.make_async_remote_copy(src_ref=landed, dst_ref=landed, send_sem=send.at[w, j], recv_sem=recv.at[w, j],
                                             device_id=(*chip, c), device_id_type=MESH).wait_recv()
                cp = pltpu.make_async_remote_copy(src_ref=landed, dst_ref=landed, send_sem=fsend.at[w, j],
                                                  recv_sem=frecv.at[w, j], device_id=sib, device_id_type=MESH)
                cp.start()
                sends.append(cp)
        for w, n in enumerate(BIG):
            R, C = shapes[w]
            for j, chip in enumerate(chips):
                other = _window(dst[w], n, R, C, 2 * chip[0] + chip[1], 1 - c)
                pltpu.make_async_remote_copy(src_ref=other, dst_ref=other, send_sem=fsend.at[w, j], recv_sem=frecv.at[w, j],
                                             device_id=sib, device_id_type=MESH).wait_recv()
        for cp in sends:
            cp.wait_send()
        for cp in pending:
            cp.wait()

    def full_shape(n, shp):
        R, C = shp
        return (R, 4 * C) if n in COL_SHARDED else (4 * R, C)

    outs = pl.pallas_call(
        body, name="gather_weights", in_specs=[ANY] * N_BIG, out_specs=[ANY] * N_BIG,
        out_shape=[SDS(full_shape(n, s), BF16) for n, s in zip(BIG, shapes)],
        scratch_shapes=[pltpu.SemaphoreType.DMA((N_BIG, 3))] * 4 + [pltpu.SemaphoreType.DMA((N_BIG,))],
    )(*[shards[n] for n in BIG])
    return dict(zip(BIG, outs))


def allreduce_small(v, name):
    rows = v.shape[0]

    def body(v_ref, o_ref, buf, send, recv):
        x, y, c, _ = _place()
        me = 4 * x + 2 * y + c
        buf[0] = v_ref[...]
        sends = []
        for k in range(1, 8):
            peer = (x ^ (k >> 2), y ^ ((k >> 1) & 1), c ^ (k & 1))
            cp = pltpu.make_async_remote_copy(src_ref=v_ref, dst_ref=buf.at[k], send_sem=send.at[k - 1], recv_sem=recv.at[k - 1],
                                              device_id=peer, device_id_type=MESH)
            cp.start()
            sends.append(cp)
        for cp in sends:
            cp.wait_recv()
        acc = buf[me]
        for d in range(1, 8):
            acc = acc + buf[d ^ me]
        o_ref[...] = acc
        for cp in sends:
            cp.wait_send()

    vm = pl.BlockSpec(memory_space=pltpu.VMEM)
    return pl.pallas_call(
        body, name=name, in_specs=[vm], out_specs=vm, out_shape=SDS((rows, 128), F32),
        scratch_shapes=[pltpu.VMEM((8, rows, 128), F32), pltpu.SemaphoreType.DMA((7,)), pltpu.SemaphoreType.DMA((7,))],
    )(v)


def pair_exchange(grads):
    shapes = [grads[n].shape for n in BIG]

    def body(*refs):
        src, dst = refs[:N_BIG], refs[N_BIG:2 * N_BIG]
        send, recv = refs[2 * N_BIG:]
        x, y, c, _ = _place()
        cps = []
        for w in range(N_BIG):
            _, R, C = shapes[w]
            cp = pltpu.make_async_remote_copy(
                src_ref=src[w].at[:, pl.ds((1 - c) * (R // 2), R // 2), :], dst_ref=dst[w],
                send_sem=send.at[w], recv_sem=recv.at[w], device_id=(x, y, 1 - c), device_id_type=MESH)
            cp.start()
            cps.append(cp)
        for cp in cps:
            cp.wait()

    outs = pl.pallas_call(
        body, name="pair_exchange", in_specs=[ANY] * N_BIG, out_specs=[ANY] * N_BIG,
        out_shape=[SDS((4, s[1] // 2, s[2]), F32) for s in shapes],
        scratch_shapes=[pltpu.SemaphoreType.DMA((N_BIG,))] * 2,
    )(*[grads[n] for n in BIG])
    return dict(zip(BIG, outs))


def pair_sum(g, xh, c_arr, name):
    _, R, C = g.shape
    Rh = R // 2

    def body(c_ref, g_ref, x_ref, p32_ref, p16_ref):
        s = g_ref[...] + x_ref[...]
        p32_ref[...] = s
        p16_ref[...] = s.astype(BF16)

    same = pl.BlockSpec((None, Rh, C), lambda s, c: (s, 0, 0))
    grid_spec = pltpu.PrefetchScalarGridSpec(
        num_scalar_prefetch=1, grid=(4,),
        in_specs=[pl.BlockSpec((None, Rh, C), lambda s, c: (s, c[0], 0)), same], out_specs=[same, same])
    return pl.pallas_call(body, grid_spec=grid_spec, name=name, out_shape=[SDS((4, Rh, C), F32), SDS((4, Rh, C), BF16)],
                          compiler_params=_cp("parallel"))(c_arr, g, xh)


def chip_exchange(p16):
    shapes = [p16[n].shape for n in BIG]

    def body(*refs):
        src, dst = refs[:N_BIG], refs[N_BIG:2 * N_BIG]
        send, recv = refs[2 * N_BIG:]
        x, y, c, chips = _place()
        cps = []
        for w in range(N_BIG):
            for j, chip in enumerate(chips):
                cp = pltpu.make_async_remote_copy(
                    src_ref=src[w].at[2 * chip[0] + chip[1]], dst_ref=dst[w].at[j],
                    send_sem=send.at[w, j], recv_sem=recv.at[w, j], device_id=(*chip, c), device_id_type=MESH)
                cp.start()
                cps.append(cp)
        for cp in cps:
            cp.wait()

    outs = pl.pallas_call(
        body, name="chip_exchange", in_specs=[ANY] * N_BIG, out_specs=[ANY] * N_BIG,
        out_shape=[SDS((3,) + tuple(s[1:]), BF16) for s in shapes],
        scratch_shapes=[pltpu.SemaphoreType.DMA((N_BIG, 3))] * 2,
    )(*[p16[n] for n in BIG])
    return dict(zip(BIG, outs))


def chip_sum(p32, y, mine_arr, name):
    _, Rh, C = p32.shape

    def body(s_ref, p_ref, y_ref, o_ref):
        o_ref[...] = ((p_ref[...] + y_ref[0].astype(F32)) + y_ref[1].astype(F32)) + y_ref[2].astype(F32)

    grid_spec = pltpu.PrefetchScalarGridSpec(
        num_scalar_prefetch=1, grid=(1,),
        in_specs=[pl.BlockSpec((None, Rh, C), lambda i, s: (s[0], 0, 0)), pl.BlockSpec((3, Rh, C), lambda i, s: (0, 0, 0))],
        out_specs=pl.BlockSpec((Rh, C), lambda i, s: (0, 0)))
    return pl.pallas_call(body, grid_spec=grid_spec, name=name, out_shape=SDS((Rh, C), F32),
                          compiler_params=_cp("arbitrary"))(mine_arr, p32, y)


def half_exchange(halves):
    shapes = [halves[n].shape for n in BIG]

    def body(*refs):
        src, dst = refs[:N_BIG], refs[N_BIG:2 * N_BIG]
        send, recv, lsem = refs[2 * N_BIG:]
        x, y, c, _ = _place()
        cps, loc = [], []
        for w in range(N_BIG):
            Rh = shapes[w][0]
            rows = dst[w].at[pl.ds(c * Rh, Rh), :]
            lc = pltpu.make_async_copy(src[w], rows, lsem.at[w])
            lc.start()
            loc.append(lc)
            cp = pltpu.make_async_remote_copy(src_ref=src[w], dst_ref=rows, send_sem=send.at[w], recv_sem=recv.at[w],
                                              device_id=(x, y, 1 - c), device_id_type=MESH)
            cp.start()
            cps.append(cp)
        for w in range(N_BIG):
            Rh = shapes[w][0]
            other = dst[w].at[pl.ds((1 - c) * Rh, Rh), :]
            pltpu.make_async_remote_copy(src_ref=src[w], dst_ref=other, send_sem=send.at[w], recv_sem=recv.at[w],
                                         device_id=(x, y, 1 - c), device_id_type=MESH).wait_recv()
        for cp in cps:
            cp.wait_send()
        for lc in loc:
            lc.wait()

    outs = pl.pallas_call(
        body, name="half_exchange", in_specs=[ANY] * N_BIG, out_specs=[ANY] * N_BIG,
        out_shape=[SDS((2 * s[0], s[1]), F32) for s in shapes],
        scratch_shapes=[pltpu.SemaphoreType.DMA((N_BIG,))] * 3,
    )(*[halves[n] for n in BIG])
    return dict(zip(BIG, outs))


SMALL = ("rel_bias", "g_mix", "w_short_conv", "g_attn_out", "g_conv_out", "g_xattn", "g_mem", "g_ffn",
         "w_ffn_conv", "b_ffn_conv", "g_final")
SMALL_FULL = {"rel_bias": (8, 32), "g_mix": (1, 1024), "w_short_conv": (3, 512), "g_attn_out": (1, 512), "g_conv_out": (1, 512),
              "g_xattn": (1, 1024), "g_mem": (1, 1024), "g_ffn": (1, 1024), "w_ffn_conv": (3, 5632), "b_ffn_conv": (1, 5632),
              "g_final": (1024,)}
SMALL_SHARDED = ("w_short_conv", "w_ffn_conv")


def _pack(parts):
    flat = jnp.concatenate([p.reshape(-1) for p in parts])
    rows = -(-flat.shape[0] // 1024) * 8
    return jnp.pad(flat, (0, rows * 128 - flat.shape[0])).reshape(rows, 128)


def _unpack(packed, names):
    flat, out, off = packed.reshape(-1), {}, 0
    for n in names:
        size = int(np.prod(SMALL_FULL[n]))
        out[n] = flat[off:off + size].reshape(SMALL_FULL[n])
        off += size
    return out


def kernel(x, mem, rel_bias, g_mix, w_in, w_short_conv, g_attn_out, g_conv_out, w_out, g_xattn, g_mem, w_xq, w_xk, w_xv, w_xo, g_ffn, w_up, w_ffn_conv, b_ffn_conv, w_down, g_final, loss_target, m_rel_bias, m_g_mix, m_w_in, m_w_short_conv, m_g_attn_out, m_g_conv_out, m_w_out, m_g_xattn, m_g_mem, m_w_xq, m_w_xk, m_w_xv, m_w_xo, m_g_ffn, m_w_up, m_w_ffn_conv, m_b_ffn_conv, m_w_down, m_g_final, v_rel_bias, v_g_mix, v_w_in, v_w_short_conv, v_g_attn_out, v_g_conv_out, v_w_out, v_g_xattn, v_g_mem, v_w_xq, v_w_xk, v_w_xv, v_w_xo, v_g_ffn, v_w_up, v_w_ffn_conv, v_b_ffn_conv, v_w_down, v_g_final):
    names = ("rel_bias", "g_mix", "w_in", "w_short_conv", "g_attn_out", "g_conv_out", "w_out", "g_xattn", "g_mem", "w_xq",
             "w_xk", "w_xv", "w_xo", "g_ffn", "w_up", "w_ffn_conv", "b_ffn_conv", "w_down", "g_final")
    W = dict(zip(names, (rel_bias, g_mix, w_in, w_short_conv, g_attn_out, g_conv_out, w_out, g_xattn, g_mem, w_xq, w_xk, w_xv,
                         w_xo, g_ffn, w_up, w_ffn_conv, b_ffn_conv, w_down, g_final)))
    M = dict(zip(names, (m_rel_bias, m_g_mix, m_w_in, m_w_short_conv, m_g_attn_out, m_g_conv_out, m_w_out, m_g_xattn, m_g_mem,
                         m_w_xq, m_w_xk, m_w_xv, m_w_xo, m_g_ffn, m_w_up, m_w_ffn_conv, m_b_ffn_conv, m_w_down, m_g_final)))
    V = dict(zip(names, (v_rel_bias, v_g_mix, v_w_in, v_w_short_conv, v_g_attn_out, v_g_conv_out, v_w_out, v_g_xattn, v_g_mem,
                         v_w_xq, v_w_xk, v_w_xv, v_w_xo, v_g_ffn, v_w_up, v_w_ffn_conv, v_b_ffn_conv, v_w_down, v_g_final)))
    xi, yi, ci = lax.axis_index("x"), lax.axis_index("y"), lax.axis_index("c")
    mine = 2 * xi + yi
    c_arr = jnp.reshape(ci, (1,)).astype(jnp.int32)
    mine_arr = jnp.reshape(mine, (1,)).astype(jnp.int32)

    wb = gather_weights({n: W[n][0].astype(BF16) for n in BIG})
    first = jnp.where(ci == 0, 1.0, 0.0).astype(F32)
    emb = []
    for n in SMALL_SHARDED:
        shard = W[n][0]
        full = jnp.zeros(SMALL_FULL[n], F32)
        emb.append(lax.dynamic_update_slice(full, shard * first, (0, mine * shard.shape[1])))
    conv_w = allreduce_small(_pack(emb), "gather_small")
    conv_w = _unpack(conv_w, SMALL_SHARDED)
    ws = {n: W[n] for n in SMALL if n not in SMALL_SHARDED}
    ws.update(conv_w)

    loss, grad_x, gb, gs = local_step(x[0], mem[0], loss_target[0], wb, ws)
    loss = lax.psum(loss[0, 0], ("x", "y", "c"))

    gb = {n: (g if g.ndim == 3 else g.reshape(4, g.shape[0] // 4, g.shape[1])) for n, g in gb.items()}
    xh = pair_exchange(gb)
    p32, p16 = {}, {}
    for n in BIG:
        p32[n], p16[n] = pair_sum(gb[n], xh[n], c_arr, "pair_sum_" + n)
    yb = chip_exchange(p16)
    halves = {n: chip_sum(p32[n], yb[n], mine_arr, "chip_sum_" + n) for n in BIG}
    gfull = half_exchange(halves)

    gsm = _unpack(allreduce_small(_pack([gs[n] for n in SMALL]), "reduce_small"), SMALL)

    grads, delta, new_m, new_v = {}, {}, {}, {}
    for n in BIG:
        grads[n] = gfull[n][None]
        d, nm, nv = adamw(W[n][0], gfull[n], M[n][0], V[n][0], "adamw_" + n)
        delta[n], new_m[n], new_v[n] = d[None], nm[None], nv[None]
    for n in SMALL_SHARDED:
        wid = W[n].shape[2]
        g = lax.dynamic_slice(gsm[n], (0, mine * wid), (3, wid))
        grads[n] = g[None]
        d, nm, nv = adamw(W[n][0], g, M[n][0], V[n][0], "adamw_" + n)
        delta[n], new_m[n], new_v[n] = d[None], nm[None], nv[None]
    rep = [n for n in SMALL if n not in SMALL_SHARDED]
    d, nm, nv = adamw(_pack([W[n] for n in rep]), _pack([gsm[n] for n in rep]), _pack([M[n] for n in rep]),
                      _pack([V[n] for n in rep]), "adamw_small")

    def unpack_rep(p):
        flat, out, off = p.reshape(-1), {}, 0
        for n in rep:
            size = int(np.prod(SMALL_FULL[n]))
            out[n] = flat[off:off + size].reshape(SMALL_FULL[n])
            off += size
        return out

    d, nm, nv = unpack_rep(d), unpack_rep(nm), unpack_rep(nv)
    for n in rep:
        grads[n], delta[n], new_m[n], new_v[n] = gsm[n], d[n], nm[n], nv[n]
    return (loss, grad_x[None], *[grads[n] for n in names], *[delta[n] for n in names],
            *[new_m[n] for n in names], *[new_v[n] for n in names])
```

```python
import functools
import math

import numpy as np
import jax
import jax.numpy as jnp
from jax import lax
from jax.experimental import pallas as pl
from jax.experimental.pallas import tpu as pltpu

F32 = jnp.float32
BF16 = jnp.bfloat16
SDS = jax.ShapeDtypeStruct
MESH = pl.DeviceIdType.MESH

D_MODEL = 1024
ATTN_W = 512
N_HEADS = 8
BLK = 128
PATTERNS = ((128, 1), (512, 4), (2048, 16))
N_BUCKETS = 32
D_FF = 2816
N_MEM_HEADS = 4
MEM_HD = 256
EPS = 1e-6
NEG = -1e30
VMEM_LIMIT = 56 * 1024 * 1024

ADAM_LR, ADAM_B1, ADAM_B2, ADAM_EPS, ADAM_WD, ADAM_STEP = 0.001, 0.9, 0.999, 1e-08, 0.01, 10


def _cp(*sem):
    return pltpu.CompilerParams(dimension_semantics=sem, vmem_limit_bytes=VMEM_LIMIT)


def _dot(a, b):
    return jnp.dot(a, b, preferred_element_type=F32)


def _dot_nt(a, b):
    return lax.dot_general(a, b, (((1,), (1,)), ((), ())), preferred_element_type=F32)


def _dot_tn(a, b):
    return lax.dot_general(a, b, (((0,), (0,)), ((), ())), preferred_element_type=F32)


def _rsum(x):
    return jnp.sum(x, axis=1, keepdims=True)


def rmsnorm_fwd(x, g, name):
    S, Dm = x.shape
    tm = min(S, 512)

    def body(x_ref, g_ref, o_ref):
        xv = x_ref[...]
        r = lax.rsqrt(jnp.mean(xv * xv, axis=1, keepdims=True) + EPS)
        o_ref[...] = (xv * r * g_ref[...]).astype(o_ref.dtype)

    return pl.pallas_call(
        body, grid=(S // tm,), name=name,
        in_specs=[pl.BlockSpec((tm, Dm), lambda i: (i, 0)), pl.BlockSpec((1, Dm), lambda i: (0, 0))],
        out_specs=pl.BlockSpec((tm, Dm), lambda i: (i, 0)),
        out_shape=SDS((S, Dm), BF16), compiler_params=_cp("parallel"))(x, g)


def rmsnorm_bwd(x, g, dh, res, name):
    S, Dm = x.shape
    tm = min(S, 512)
    want_dx = res is not None

    def body(*refs):
        if want_dx:
            x_ref, g_ref, dh_ref, res_ref, dx_ref, dg_ref = refs
        else:
            x_ref, g_ref, dh_ref, dg_ref = refs
        i = pl.program_id(0)
        xv = x_ref[...]
        r = lax.rsqrt(jnp.mean(xv * xv, axis=1, keepdims=True) + EPS)
        xh = xv * r
        dh = dh_ref[...].astype(F32)
        if want_dx:
            gd = dh * g_ref[...]
            dx_ref[...] = res_ref[...] + r * (gd - xh * jnp.mean(gd * xh, axis=1, keepdims=True))

        @pl.when(i == 0)
        def _():
            dg_ref[...] = jnp.zeros_like(dg_ref)

        dg_ref[...] += jnp.sum(dh * xh, axis=0, keepdims=True)

    row = pl.BlockSpec((tm, Dm), lambda i: (i, 0))
    vec = pl.BlockSpec((1, Dm), lambda i: (0, 0))
    if want_dx:
        return pl.pallas_call(
            body, grid=(S // tm,), name=name, in_specs=[row, vec, row, row], out_specs=[row, vec],
            out_shape=[SDS((S, Dm), F32), SDS((1, Dm), F32)], compiler_params=_cp("arbitrary"))(x, g, dh, res)
    return pl.pallas_call(
        body, grid=(S // tm,), name=name, in_specs=[row, vec, row], out_specs=vec,
        out_shape=SDS((1, Dm), F32), compiler_params=_cp("arbitrary"))(x, g, dh)


def _col_spec(bw, tn, rows, row_of, col_of):
    if bw is None:
        return pl.BlockSpec((rows, tn), lambda *g: (row_of(*g), col_of(*g)))
    if tn % bw == 0:
        return pl.BlockSpec((tn // bw, rows, bw), lambda *g: (col_of(*g), row_of(*g), 0))
    per = bw // tn
    return pl.BlockSpec((None, rows, tn), lambda *g: (col_of(*g) // per, row_of(*g), col_of(*g) % per))


def _read_cols(ref, bw, tn):
    if bw is None or tn % bw != 0:
        return ref[...]
    if tn == bw:
        return ref[0]
    return jnp.concatenate([ref[q] for q in range(tn // bw)], axis=1)


def _write_cols(ref, bw, tn, val):
    if bw is None or tn % bw != 0:
        ref[...] = val.astype(ref.dtype)
    else:
        for q in range(tn // bw):
            ref[q] = val[:, q * bw:(q + 1) * bw].astype(ref.dtype)


def mm_nn(a, b, name, *, res=None, out_dtype=F32, out_bw=None, tm=1024, tn=512):
    M, K = a.shape
    N = b.shape[1]
    tm = min(tm, M)

    def body(*refs):
        if res is None:
            a_ref, b_ref, o_ref = refs
        else:
            a_ref, b_ref, r_ref, o_ref = refs
        acc = _dot(a_ref[...].astype(BF16), b_ref[...])
        if res is not None:
            acc = acc + r_ref[...]
        _write_cols(o_ref, out_bw, tn, acc)

    ri, ci = (lambda i, j: i), (lambda i, j: j)
    in_specs = [pl.BlockSpec((tm, K), lambda i, j: (i, 0)), pl.BlockSpec((K, tn), lambda i, j: (0, j))]
    args = [a, b]
    if res is not None:
        in_specs.append(pl.BlockSpec((tm, tn), lambda i, j: (i, j)))
        args.append(res)
    oshape = (M, N) if out_bw is None else (N // out_bw, M, out_bw)
    return pl.pallas_call(
        body, grid=(M // tm, N // tn), name=name, in_specs=in_specs,
        out_specs=_col_spec(out_bw, tn, tm, ri, ci), out_shape=SDS(oshape, out_dtype),
        compiler_params=_cp("parallel", "parallel"))(*args)


def mm_nt(a, a_bw, b, name, *, res=None, out_dtype=F32, tm=1024, tn=512):
    if a_bw is None:
        M, K = a.shape
    else:
        M, K = a.shape[1], a.shape[0] * a_bw
    N = b.shape[0]
    tm = min(tm, M)

    def body(*refs):
        if res is None:
            a_ref, b_ref, o_ref = refs
        else:
            a_ref, b_ref, r_ref, o_ref = refs
        av = _read_cols(a_ref, a_bw, K).astype(BF16)
        acc = _dot_nt(av, b_ref[...])
        if res is not None:
            acc = acc + r_ref[...]
        o_ref[...] = acc.astype(o_ref.dtype)

    in_specs = [_col_spec(a_bw, K, tm, lambda i, j: i, lambda i, j: 0), pl.BlockSpec((tn, K), lambda i, j: (j, 0))]
    args = [a, b]
    if res is not None:
        in_specs.append(pl.BlockSpec((tm, tn), lambda i, j: (i, j)))
        args.append(res)
    return pl.pallas_call(
        body, grid=(M // tm, N // tn), name=name, in_specs=in_specs,
        out_specs=pl.BlockSpec((tm, tn), lambda i, j: (i, j)), out_shape=SDS((M, N), out_dtype),
        compiler_params=_cp("parallel", "parallel"))(*args)


def mm_tn(a, b, b_bw, name, *, shards=None, tm=1024, tn=512, ts=512):
    S, Ka = a.shape
    N = b.shape[1] if b_bw is None else b.shape[0] * b_bw
    ts = min(ts, S)
    tm = min(tm, Ka)

    def body(a_ref, b_ref, o_ref):
        @pl.when(pl.program_id(2) == 0)
        def _():
            o_ref[...] = jnp.zeros_like(o_ref)

        bv = _read_cols(b_ref, b_bw, tn).astype(BF16)
        o_ref[...] += _dot_tn(a_ref[...].astype(BF16), bv)

    in_specs = [pl.BlockSpec((ts, tm), lambda i, j, k: (k, i)),
                _col_spec(b_bw, tn, ts, lambda i, j, k: k, lambda i, j, k: j)]
    if shards is None:
        out_spec, oshape = pl.BlockSpec((tm, tn), lambda i, j, k: (i, j)), (Ka, N)
    else:
        per = (N // shards) // tn
        out_spec = pl.BlockSpec((None, tm, tn), lambda i, j, k: (j // per, i, j % per))
        oshape = (shards, Ka, N // shards)
    return pl.pallas_call(
        body, grid=(Ka // tm, N // tn, S // ts), name=name, in_specs=in_specs, out_specs=out_spec,
        out_shape=SDS(oshape, F32), compiler_params=_cp("parallel", "parallel", "arbitrary"))(a, b)


def _bucket_tables():
    out = np.zeros((3, 2, BLK, BLK), np.int32)
    qi = np.arange(BLK)[:, None]
    for p, (win, dil) in enumerate(PATTERNS):
        w = win // dil
        assert w == BLK
        for half in range(2):
            kj = np.arange(BLK)[None, :] + half * BLK
            steps = qi + w - kj
            valid = (steps >= 0) & (steps <= w)
            dist = np.clip(steps, 0, w) * dil
            dd = np.maximum(dist, 1).astype(np.float32)
            large = 16 + (np.log(dd / np.float32(16)) / np.float32(math.log(2048 / 16)) * np.float32(16)).astype(np.int32)
            large = np.minimum(large, N_BUCKETS - 1)
            out[p, half] = np.where(valid, np.where(dist < 16, dist, large), -1)
    return out


def bias_tables(rel_bias):
    bkt = jnp.asarray(_bucket_tables())

    def body(rb_ref, bkt_ref, o_ref):
        h = pl.program_id(1)
        for half in range(2):
            bk = bkt_ref[half]
            acc = jnp.full((BLK, BLK), NEG, F32)
            for b in range(N_BUCKETS):
                acc = jnp.where(bk == b, rb_ref[h, b], acc)
            o_ref[half] = acc

    return pl.pallas_call(
        body, grid=(3, N_HEADS), name="bias_tables",
        in_specs=[pl.BlockSpec(memory_space=pltpu.SMEM),
                  pl.BlockSpec((None, 2, BLK, BLK), lambda p, h: (p, 0, 0, 0))],
        out_specs=pl.BlockSpec((None, None, 2, BLK, BLK), lambda p, h: (p, h, 0, 0, 0)),
        out_shape=SDS((3, N_HEADS, 2, BLK, BLK), F32), compiler_params=_cp("parallel", "parallel"))(rel_bias, bkt)


def bias_tables_bwd(ds_sum):
    bkt = jnp.asarray(_bucket_tables())

    def body(ds_ref, bkt_ref, o_ref):
        h = pl.program_id(0)

        @pl.when(h == 0)
        def _():
            o_ref[...] = jnp.zeros_like(o_ref)

        rows = lax.broadcasted_iota(jnp.int32, (N_HEADS, N_BUCKETS), 0)
        cols = lax.broadcasted_iota(jnp.int32, (N_HEADS, N_BUCKETS), 1)
        acc = jnp.zeros((N_HEADS, N_BUCKETS), F32)
        for b in range(N_BUCKETS):
            tot = jnp.zeros((1, 1), F32)
            for p in range(3):
                for half in range(2):
                    t = jnp.where(bkt_ref[p, half] == b, ds_ref[p, half], 0.0)
                    tot = tot + jnp.sum(_rsum(t), axis=0, keepdims=True)
            acc = acc + jnp.where((rows == h) & (cols == b), tot, 0.0)
        o_ref[...] += acc

    return pl.pallas_call(
        body, grid=(N_HEADS,), name="bias_tables_bwd",
        in_specs=[pl.BlockSpec((3, None, 2, BLK, BLK), lambda h: (0, h, 0, 0, 0)),
                  pl.BlockSpec((3, 2, BLK, BLK), lambda h: (0, 0, 0, 0))],
        out_specs=pl.BlockSpec((N_HEADS, N_BUCKETS), lambda h: (0, 0)),
        out_shape=SDS((N_HEADS, N_BUCKETS), F32), compiler_params=_cp("arbitrary"))(ds_sum, bkt)


def _rows(start, dil):
    if dil == 1:
        return pl.ds(pl.multiple_of(start, BLK), BLK)
    return pl.ds(start, BLK, stride=dil)


def _block_rows(i, dil):
    b, r = i // dil, i % dil
    cur = b * (BLK * dil) + r
    prv = jnp.maximum(b - 1, 0) * (BLK * dil) + r
    return _rows(cur, dil), _rows(prv, dil), b > 0


def attn_fwd(qkv, bias):
    S = qkv.shape[2]
    nblk = S // BLK

    def body(qkv_ref, bias_ref, o_ref, lse_ref, l_ref):
        lane = lax.broadcasted_iota(jnp.int32, (BLK, BLK), 1)
        lo = lane < 64
        masks = (lo, jnp.logical_not(lo))
        for p, (_, dil) in enumerate(PATTERNS):
            def step(i, carry, p=p, dil=dil):
                rc, rp, has_prev = _block_rows(i, dil)
                pen = jnp.where(has_prev, 0.0, NEG)
                q_ref, k_ref, v_ref = qkv_ref.at[0], qkv_ref.at[1], qkv_ref.at[2]
                q2 = q_ref[rc, :] * 0.125
                kc = k_ref[rc, :].astype(BF16)
                kp = k_ref[rp, :].astype(BF16)
                vc = v_ref[rc, :]
                vp = v_ref[rp, :]
                if p > 0:
                    acc_old, m_old, l_old = o_ref[rc, :], lse_ref[rc, :], l_ref[rc, :]
                pv = jnp.zeros((BLK, BLK), F32)
                m_new, l_new, alpha = [], [], []
                for h in range(2):
                    qh = jnp.where(masks[h], q2, 0.0).astype(BF16)
                    sp = _dot_nt(qh, kp) + bias_ref[p, h, 0] + pen
                    sc = _dot_nt(qh, kc) + bias_ref[p, h, 1]
                    mn = jnp.maximum(jnp.max(sp, axis=1, keepdims=True), jnp.max(sc, axis=1, keepdims=True))
                    if p > 0:
                        mo = m_old[:, 64 * h:64 * h + 1]
                        mn = jnp.maximum(mn, mo)
                        al = jnp.exp(mo - mn)
                        alpha.append(al)
                    pp = jnp.exp(sp - mn)
                    pc = jnp.exp(sc - mn)
                    ln = _rsum(pp) + _rsum(pc)
                    if p > 0:
                        ln = ln + al * l_old[:, 64 * h:64 * h + 1]
                    vhp = jnp.where(masks[h], vp, 0.0).astype(BF16)
                    vhc = jnp.where(masks[h], vc, 0.0).astype(BF16)
                    pv = pv + _dot(pp.astype(BF16), vhp) + _dot(pc.astype(BF16), vhc)
                    m_new.append(mn)
                    l_new.append(ln)
                if p > 0:
                    pv = pv + acc_old * jnp.where(lo, alpha[0], alpha[1])
                o_ref[rc, :] = pv
                lse_ref[rc, :] = jnp.where(lo, m_new[0], m_new[1])
                l_ref[rc, :] = jnp.where(lo, l_new[0], l_new[1])
                return carry

            lax.fori_loop(0, nblk, step, 0)

        def fin(i, carry):
            rows = pl.ds(pl.multiple_of(i * 512, 512), 512)
            lv = l_ref[rows, :]
            o_ref[rows, :] = o_ref[rows, :] / lv
            lse_ref[rows, :] = lse_ref[rows, :] + jnp.log(lv)
            return carry

        lax.fori_loop(0, S // 512, fin, 0)

    return pl.pallas_call(
        body, grid=(4,), name="attn_fwd",
        in_specs=[pl.BlockSpec((3, None, S, BLK), lambda g: (0, g, 0, 0)),
                  pl.BlockSpec((3, 2, 2, BLK, BLK), lambda g: (0, g, 0, 0, 0))],
        out_specs=[pl.BlockSpec((None, S, BLK), lambda g: (g, 0, 0)), pl.BlockSpec((None, S, BLK), lambda g: (g, 0, 0))],
        out_shape=[SDS((4, S, BLK), F32), SDS((4, S, BLK), F32)],
        scratch_shapes=[pltpu.VMEM((S, BLK), F32)], compiler_params=_cp("parallel"))(qkv, bias)


def attn_bwd(qkv, bias, o, lse, do):
    S = qkv.shape[2]
    nblk = S // BLK

    def body(qkv_ref, bias_ref, o_ref, lse_ref, do_ref, dqkv_ref, ds_ref):
        lane = lax.broadcasted_iota(jnp.int32, (BLK, BLK), 1)
        lo = lane < 64
        masks = (lo, jnp.logical_not(lo))
        dqkv_ref[...] = jnp.zeros_like(dqkv_ref)
        ds_ref[...] = jnp.zeros_like(ds_ref)
        for p, (_, dil) in enumerate(PATTERNS):
            def step(i, carry, p=p, dil=dil):
                rc, rp, has_prev = _block_rows(i, dil)
                pen = jnp.where(has_prev, 0.0, NEG)
                q_ref, k_ref, v_ref = qkv_ref.at[0], qkv_ref.at[1], qkv_ref.at[2]
                q2 = q_ref[rc, :] * 0.125
                kc = k_ref[rc, :].astype(BF16)
                kp = k_ref[rp, :].astype(BF16)
                vc = v_ref[rc, :].astype(BF16)
                vp = v_ref[rp, :].astype(BF16)
                dot = do_ref[rc, :]
                lset = lse_ref[rc, :]
                prod = dot * o_ref[rc, :]
                dq = jnp.zeros((BLK, BLK), F32)
                dkc = jnp.zeros((BLK, BLK), F32)
                dkp = jnp.zeros((BLK, BLK), F32)
                dvc = jnp.zeros((BLK, BLK), F32)
                dvp = jnp.zeros((BLK, BLK), F32)
                for h in range(2):
                    qh = jnp.where(masks[h], q2, 0.0).astype(BF16)
                    doh = jnp.where(masks[h], dot, 0.0).astype(BF16)
                    delta = _rsum(jnp.where(masks[h], prod, 0.0))
                    lh = lset[:, 64 * h:64 * h + 1]
                    pp = jnp.exp(_dot_nt(qh, kp) + bias_ref[p, h, 0] + pen - lh)
                    pc = jnp.exp(_dot_nt(qh, kc) + bias_ref[p, h, 1] - lh)
                    dsp = pp * (_dot_nt(doh, vp) - delta)
                    dsc = pc * (_dot_nt(doh, vc) - delta)
                    ds_ref[p, h, 0] += dsp
                    ds_ref[p, h, 1] += dsc
                    dspb, dscb = dsp.astype(BF16), dsc.astype(BF16)
                    dq = dq + jnp.where(masks[h], _dot(dspb, kp) + _dot(dscb, kc), 0.0)
                    dkp = dkp + _dot_tn(dspb, qh)
                    dkc = dkc + _dot_tn(dscb, qh)
                    dvp = dvp + _dot_tn(pp.astype(BF16), doh)
                    dvc = dvc + _dot_tn(pc.astype(BF16), doh)
                dq_ref, dk_ref, dv_ref = dqkv_ref.at[0], dqkv_ref.at[1], dqkv_ref.at[2]
                dq_ref[rc, :] += dq * 0.125
                dk_ref[rp, :] += dkp
                dk_ref[rc, :] += dkc
                dv_ref[rp, :] += dvp
                dv_ref[rc, :] += dvc
                return carry

            lax.fori_loop(0, nblk, step, 0)

    blk = pl.BlockSpec((None, S, BLK), lambda g: (g, 0, 0))
    return pl.pallas_call(
        body, grid=(4,), name="attn_bwd",
        in_specs=[pl.BlockSpec((3, None, S, BLK), lambda g: (0, g, 0, 0)),
                  pl.BlockSpec((3, 2, 2, BLK, BLK), lambda g: (0, g, 0, 0, 0)), blk, blk, blk],
        out_specs=[pl.BlockSpec((None, 3, None, S, BLK), lambda g: (0, 0, g, 0, 0)),
                   pl.BlockSpec((3, 2, 2, BLK, BLK), lambda g: (0, g, 0, 0, 0))],
        out_shape=[SDS((2, 3, 4, S, BLK), F32), SDS((3, N_HEADS, 2, BLK, BLK), F32)],
        compiler_params=_cp("parallel"))(qkv, bias, o, lse, do)


def _shift_down(u, k, halo):
    n = u.shape[0]
    row = lax.broadcasted_iota(jnp.int32, u.shape, 0)
    out = pltpu.roll(u, k, 0)
    hn = halo.shape[0]
    for j in range(k):
        out = jnp.where(row == j, halo[hn - k + j:hn - k + j + 1, :], out)
    return out


def _shift_up(u, k, halo):
    n = u.shape[0]
    row = lax.broadcasted_iota(jnp.int32, u.shape, 0)
    out = pltpu.roll(u, n - k, 0)
    for j in range(k):
        out = jnp.where(row == n - k + j, halo[j:j + 1, :], out)
    return out


def _conv3(u, halo, w0, w1, w2):
    return _shift_down(u, 2, halo) * w0 + _shift_down(u, 1, halo) * w1 + u * w2


def _conv3_bwd_in(d, halo, w0, w1, w2):
    return d * w2 + _shift_up(d, 1, halo) * w1 + _shift_up(d, 2, halo) * w0


def _colsum(x):
    return jnp.sum(x, axis=0, keepdims=True)


def _mixer_specs(S, tm):
    conv = pl.BlockSpec((None, 12, tm, BLK), lambda i: (1, 0, i, 0))
    halo = pl.BlockSpec((None, 12, 8, BLK), lambda i: (1, 0, jnp.maximum(i * (tm // 8) - 1, 0), 0))
    ob = pl.BlockSpec((4, tm, BLK), lambda i: (0, i, 0))
    return conv, halo, ob


def _mixer_recompute(i, o_ref, pr_ref, ph_ref, w_ref):
    ob = [o_ref[q] for q in range(4)]
    gb = [pr_ref[q] for q in range(4)]
    gc = [pr_ref[4 + q] for q in range(4)]
    xi = [pr_ref[8 + q] for q in range(4)]
    keep = jnp.where(i > 0, 1.0, 0.0)
    u = [gc[q] * xi[q] for q in range(4)]
    hu = [ph_ref[4 + q] * ph_ref[8 + q] * keep for q in range(4)]
    w = [[w_ref[k:k + 1, q * BLK:(q + 1) * BLK] for k in range(3)] for q in range(4)]
    cv = [_conv3(u[q], hu[q], *w[q]) for q in range(4)]
    return ob, gb, gc, xi, u, hu, cv, w


def _rms_blocks(blocks):
    ss = sum(_rsum(b * b) for b in blocks)
    return lax.rsqrt(ss / (BLK * len(blocks)) + EPS)


def mixer_post_fwd(o, proj, w_sc, g_a, g_c):
    S = o.shape[1]
    tm = 256

    def body(o_ref, pr_ref, ph_ref, w_ref, ga_ref, gc_ref, m_ref):
        i = pl.program_id(0)
        ob, gb, _, _, _, _, cv, _ = _mixer_recompute(i, o_ref, pr_ref, ph_ref, w_ref)
        conv = [gb[q] * cv[q] for q in range(4)]
        ra, rc = _rms_blocks(ob), _rms_blocks(conv)
        for q in range(4):
            sl = slice(q * BLK, (q + 1) * BLK)
            m_ref[:, q * BLK:(q + 1) * BLK] = (ob[q] * ra * ga_ref[:, sl]).astype(BF16)
            m_ref[:, ATTN_W + q * BLK:ATTN_W + (q + 1) * BLK] = (conv[q] * rc * gc_ref[:, sl]).astype(BF16)

    conv_s, halo_s, ob_s = _mixer_specs(S, tm)
    full = lambda r, c: pl.BlockSpec((r, c), lambda i: (0, 0))
    return pl.pallas_call(
        body, grid=(S // tm,), name="mixer_post_fwd",
        in_specs=[ob_s, conv_s, halo_s, full(3, 512), full(1, 512), full(1, 512)],
        out_specs=pl.BlockSpec((tm, D_MODEL), lambda i: (i, 0)), out_shape=SDS((S, D_MODEL), BF16),
        compiler_params=_cp("parallel"))(o, proj, proj, w_sc, g_a, g_c)


def mixer_post_bwd_a(dmixed, o, proj, w_sc, g_a, g_c):
    S = o.shape[1]
    tm = 256

    def body(dm_ref, o_ref, pr_ref, ph_ref, w_ref, ga_ref, gc_ref, do_ref, dgb_ref, dcv_ref, dga_ref, dgc_ref):
        i = pl.program_id(0)
        ob, gb, _, _, _, _, cv, _ = _mixer_recompute(i, o_ref, pr_ref, ph_ref, w_ref)
        conv = [gb[q] * cv[q] for q in range(4)]
        ra, rc = _rms_blocks(ob), _rms_blocks(conv)

        @pl.when(i == 0)
        def _():
            dga_ref[...] = jnp.zeros_like(dga_ref)
            dgc_ref[...] = jnp.zeros_like(dgc_ref)

        for blocks, r, g_ref, off, dg_ref, is_attn in ((ob, ra, ga_ref, 0, dga_ref, True), (conv, rc, gc_ref, ATTN_W, dgc_ref, False)):
            xh = [blocks[q] * r for q in range(4)]
            dm = [dm_ref[:, off + q * BLK:off + (q + 1) * BLK].astype(F32) for q in range(4)]
            gd = [dm[q] * g_ref[:, q * BLK:(q + 1) * BLK] for q in range(4)]
            mean = sum(_rsum(gd[q] * xh[q]) for q in range(4)) / (4 * BLK)
            for q in range(4):
                dg_ref[:, q * BLK:(q + 1) * BLK] += _colsum(dm[q] * xh[q])
                dx = r * (gd[q] - xh[q] * mean)
                if is_attn:
                    do_ref[q] = dx
                else:
                    dgb_ref[q] = dx * cv[q]
                    dcv_ref[q] = dx * gb[q]

    conv_s, halo_s, ob_s = _mixer_specs(S, tm)
    full = lambda r, c: pl.BlockSpec((r, c), lambda i: (0, 0))
    return pl.pallas_call(
        body, grid=(S // tm,), name="mixer_post_bwd_a",
        in_specs=[pl.BlockSpec((tm, D_MODEL), lambda i: (i, 0)), ob_s, conv_s, halo_s, full(3, 512), full(1, 512), full(1, 512)],
        out_specs=[ob_s, ob_s, ob_s, full(1, 512), full(1, 512)],
        out_shape=[SDS((4, S, BLK), F32)] * 3 + [SDS((1, 512), F32)] * 2,
        compiler_params=_cp("arbitrary"))(dmixed, o, proj, proj, w_sc, g_a, g_c)


def mixer_post_bwd_b(dproj, dgb, dcv, proj, w_sc):
    S = proj.shape[2]
    tm = 256
    last = S // 8 - 1

    def body(dp_in, dgb_ref, dcv_ref, dn_ref, pr_ref, ph_ref, w_ref, dp_ref, dw_ref):
        i = pl.program_id(0)
        keep_prev = jnp.where(i > 0, 1.0, 0.0)
        keep_next = jnp.where(i < pl.num_programs(0) - 1, 1.0, 0.0)

        @pl.when(i == 0)
        def _():
            dw_ref[...] = jnp.zeros_like(dw_ref)

        for q in range(4):
            sl = slice(q * BLK, (q + 1) * BLK)
            gc, xi = pr_ref[4 + q], pr_ref[8 + q]
            u = gc * xi
            hu = ph_ref[4 + q] * ph_ref[8 + q] * keep_prev
            dcv = dcv_ref[q]
            w0, w1, w2 = (w_ref[k:k + 1, sl] for k in range(3))
            du = _conv3_bwd_in(dcv, dn_ref[q] * keep_next, w0, w1, w2)
            dw_ref[0:1, sl] += _colsum(dcv * _shift_down(u, 2, hu))
            dw_ref[1:2, sl] += _colsum(dcv * _shift_down(u, 1, hu))
            dw_ref[2:3, sl] += _colsum(dcv * u)
            dp_ref[q] = dgb_ref[q]
            dp_ref[4 + q] = du * xi
            dp_ref[8 + q] = du * gc

    conv_s, halo_s, ob_s = _mixer_specs(S, tm)
    nxt = pl.BlockSpec((4, 8, BLK), lambda i: (0, jnp.minimum((i + 1) * (tm // 8), last), 0))
    full = lambda r, c: pl.BlockSpec((r, c), lambda i: (0, 0))
    return pl.pallas_call(
        body, grid=(S // tm,), name="mixer_post_bwd_b",
        in_specs=[pl.BlockSpec(memory_space=pl.ANY), ob_s, ob_s, nxt, conv_s, halo_s, full(3, 512)],
        out_specs=[conv_s, full(3, 512)],
        out_shape=[SDS(dproj.shape, F32), SDS((3, 512), F32)],
        input_output_aliases={0: 0}, compiler_params=_cp("arbitrary"))(dproj, dgb, dcv, dcv, proj, proj, w_sc)


def xattn_fwd(q, k, v):
    S = q.shape[0]
    tm = 512
    scale = MEM_HD ** -0.5

    def body(q_ref, k_ref, v_ref, o_ref):
        for h in range(N_MEM_HEADS):
            sl = slice(h * MEM_HD, (h + 1) * MEM_HD)
            s = _dot_nt(q_ref[:, sl], k_ref[:, sl]) * scale
            p = jnp.exp(s - jnp.max(s, axis=1, keepdims=True))
            p = p / _rsum(p)
            o_ref[:, sl] = _dot(p.astype(BF16), v_ref[:, sl]).astype(BF16)

    row = pl.BlockSpec((tm, D_MODEL), lambda i: (i, 0))
    kv = pl.BlockSpec(k.shape, lambda i: (0, 0))
    return pl.pallas_call(body, grid=(S // tm,), name="xattn_fwd", in_specs=[row, kv, kv], out_specs=row,
                          out_shape=SDS((S, D_MODEL), BF16), compiler_params=_cp("parallel"))(q, k, v)


def xattn_bwd(q, k, v, do):
    S = q.shape[0]
    tm = 512
    scale = MEM_HD ** -0.5

    def body(q_ref, k_ref, v_ref, do_ref, dq_ref, dk_ref, dv_ref):
        @pl.when(pl.program_id(0) == 0)
        def _():
            dk_ref[...] = jnp.zeros_like(dk_ref)
            dv_ref[...] = jnp.zeros_like(dv_ref)

        for h in range(N_MEM_HEADS):
            sl = slice(h * MEM_HD, (h + 1) * MEM_HD)
            qh, kh, vh, doh = q_ref[:, sl], k_ref[:, sl], v_ref[:, sl], do_ref[:, sl]
            s = _dot_nt(qh, kh) * scale
            p = jnp.exp(s - jnp.max(s, axis=1, keepdims=True))
            p = p / _rsum(p)
            dp = _dot_nt(doh, vh)
            ds = (p * (dp - _rsum(p * dp)) * scale).astype(BF16)
            dq_ref[:, sl] = _dot(ds, kh).astype(BF16)
            dk_ref[:, sl] += _dot_tn(ds, qh)
            dv_ref[:, sl] += _dot_tn(p.astype(BF16), doh)

    row = pl.BlockSpec((tm, D_MODEL), lambda i: (i, 0))
    kv = pl.BlockSpec(k.shape, lambda i: (0, 0))
    return pl.pallas_call(body, grid=(S // tm,), name="xattn_bwd", in_specs=[row, kv, kv, row], out_specs=[row, kv, kv],
                          out_shape=[SDS((S, D_MODEL), BF16), SDS(k.shape, F32), SDS(k.shape, F32)],
                          compiler_params=_cp("arbitrary"))(q, k, v, do)


FFN_TM, FFN_TC = 512, 256


def _ffn_specs(S, order):
    tm, tc = FFN_TM, FFN_TC
    ij = (lambda a, b: (a, b)) if order == "ij" else (lambda a, b: (b, a))
    blk = pl.BlockSpec((2, tm, tc), lambda a, b: (0,) + ij(a, b))
    prev = pl.BlockSpec((2, 16, tc), lambda a, b: (0, jnp.maximum(ij(a, b)[0] * (tm // 16) - 1, 0), ij(a, b)[1]))
    nxt = pl.BlockSpec((2, 16, tc), lambda a, b: (0, jnp.minimum((ij(a, b)[0] + 1) * (tm // 16), S // 16 - 1), ij(a, b)[1]))
    wsp = pl.BlockSpec((2, 3, tc), lambda a, b: (0, 0, ij(a, b)[1]))
    bsp = pl.BlockSpec((2, 1, tc), lambda a, b: (0, 0, ij(a, b)[1]))
    act = pl.BlockSpec((tm, tc), lambda a, b: ij(a, b))
    return blk, prev, nxt, wsp, bsp, act


def _ffn_up(i, up_ref, uh_ref, w_ref, b_ref):
    keep = jnp.where(i > 0, 1.0, 0.0)
    out = []
    for half in range(2):
        u = up_ref[half].astype(F32)
        hu = uh_ref[half].astype(F32) * keep
        w0, w1, w2 = (w_ref[half, k:k + 1, :] for k in range(3))
        out.append((_conv3(u, hu, w0, w1, w2) + b_ref[half], u, hu))
    return out


def ffn_act_fwd(up_pre, w_fc, b_fc):
    S = up_pre.shape[1]

    def body(up_ref, uh_ref, w_ref, b_ref, a_ref):
        (g, _, _), (v, _, _) = _ffn_up(pl.program_id(0), up_ref, uh_ref, w_ref, b_ref)
        a_ref[...] = (g * jax.nn.sigmoid(g) * v).astype(BF16)

    blk, prev, _, wsp, bsp, act = _ffn_specs(S, "ij")
    return pl.pallas_call(body, grid=(S // FFN_TM, D_FF // FFN_TC), name="ffn_act_fwd",
                          in_specs=[blk, prev, wsp, bsp], out_specs=act, out_shape=SDS((S, D_FF), BF16),
                          compiler_params=_cp("parallel", "parallel"))(up_pre, up_pre, w_fc, b_fc)


def ffn_act_bwd_a(dact, up_pre, w_fc, b_fc):
    S = up_pre.shape[1]

    def body(da_ref, up_ref, uh_ref, w_ref, b_ref, du_ref, db_ref):
        i = pl.program_id(1)
        (g, _, _), (v, _, _) = _ffn_up(i, up_ref, uh_ref, w_ref, b_ref)
        da = da_ref[...].astype(F32)
        sg = jax.nn.sigmoid(g)
        dg = da * v * (sg * (1.0 + g * (1.0 - sg)))
        dv = da * g * sg

        @pl.when(i == 0)
        def _():
            db_ref[...] = jnp.zeros_like(db_ref)

        du_ref[0] = dg.astype(BF16)
        du_ref[1] = dv.astype(BF16)
        db_ref[0] += _colsum(dg)
        db_ref[1] += _colsum(dv)

    blk, prev, _, wsp, bsp, act = _ffn_specs(S, "ji")
    return pl.pallas_call(body, grid=(D_FF // FFN_TC, S // FFN_TM), name="ffn_act_bwd_a",
                          in_specs=[act, blk, prev, wsp, bsp], out_specs=[blk, bsp],
                          out_shape=[SDS((2, S, D_FF), BF16), SDS((2, 1, D_FF), F32)],
                          compiler_params=_cp("parallel", "arbitrary"))(dact, up_pre, up_pre, w_fc, b_fc)


def ffn_act_bwd_b(d_up, up_pre, w_fc):
    S = up_pre.shape[1]

    def body(du_ref, dn_ref, up_ref, uh_ref, w_ref, dp_ref, dw_ref):
        i = pl.program_id(1)
        keep_prev = jnp.where(i > 0, 1.0, 0.0)
        keep_next = jnp.where(i < pl.num_programs(1) - 1, 1.0, 0.0)

        @pl.when(i == 0)
        def _():
            dw_ref[...] = jnp.zeros_like(dw_ref)

        for half in range(2):
            d = du_ref[half].astype(F32)
            dn = dn_ref[half].astype(F32) * keep_next
            u = up_ref[half].astype(F32)
            hu = uh_ref[half].astype(F32) * keep_prev
            w0, w1, w2 = (w_ref[half, k:k + 1, :] for k in range(3))
            dp_ref[half] = _conv3_bwd_in(d, dn, w0, w1, w2).astype(BF16)
            dw_ref[half, 0:1, :] += _colsum(d * _shift_down(u, 2, hu))
            dw_ref[half, 1:2, :] += _colsum(d * _shift_down(u, 1, hu))
            dw_ref[half, 2:3, :] += _colsum(d * u)

    blk, prev, nxt, wsp, _, _ = _ffn_specs(S, "ji")
    return pl.pallas_call(body, grid=(D_FF // FFN_TC, S // FFN_TM), name="ffn_act_bwd_b",
                          in_specs=[blk, nxt, blk, prev, wsp], out_specs=[blk, wsp],
                          out_shape=[SDS((2, S, D_FF), BF16), SDS((2, 3, D_FF), F32)],
                          compiler_params=_cp("parallel", "arbitrary"))(d_up, d_up, up_pre, up_pre, w_fc)


def final_loss(x3, g, target):
    S, Dm = x3.shape
    tm = 512

    def body(x_ref, g_ref, t_ref, loss_ref, dx_ref, dg_ref):
        i = pl.program_id(0)
        xv = x_ref[...]
        r = lax.rsqrt(jnp.mean(xv * xv, axis=1, keepdims=True) + EPS)
        xh = xv * r
        err = xh * g_ref[...] - t_ref[...]
        dy = err / Dm
        gd = dy * g_ref[...]
        dx_ref[...] = r * (gd - xh * jnp.mean(gd * xh, axis=1, keepdims=True))

        @pl.when(i == 0)
        def _():
            dg_ref[...] = jnp.zeros_like(dg_ref)
            loss_ref[...] = jnp.zeros_like(loss_ref)

        dg_ref[...] += _colsum(dy * xh)
        loss_ref[...] += 0.5 * _colsum(jnp.mean(err * err, axis=1, keepdims=True))

    row = pl.BlockSpec((tm, Dm), lambda i: (i, 0))
    vec = pl.BlockSpec((1, Dm), lambda i: (0, 0))
    one = pl.BlockSpec((1, 1), lambda i: (0, 0))
    return pl.pallas_call(body, grid=(S // tm,), name="final_loss", in_specs=[row, vec, row], out_specs=[one, row, vec],
                          out_shape=[SDS((1, 1), F32), SDS((S, Dm), F32), SDS((1, Dm), F32)],
                          compiler_params=_cp("arbitrary"))(x3, g, target)


def local_step(x, mem, target, wb, ws):
    S = x.shape[0]
    assert S % 2048 == 0
    w_fc = ws["w_ffn_conv"].reshape(3, 2, D_FF).transpose(1, 0, 2)
    b_fc = ws["b_ffn_conv"].reshape(2, 1, D_FF)

    bias = bias_tables(ws["rel_bias"])
    h1 = rmsnorm_fwd(x, ws["g_mix"], "norm_mix")
    proj = mm_nn(h1, wb["w_in"], "proj", out_bw=BLK, tn=768)
    qkv = proj.reshape(2, 3, 4, S, BLK)[0]
    o, lse = attn_fwd(qkv, bias)
    proj4 = proj.reshape(2, 12, S, BLK)
    mixed = mixer_post_fwd(o, proj4, ws["w_short_conv"], ws["g_attn_out"], ws["g_conv_out"])
    x1 = mm_nn(mixed, wb["w_out"], "out_proj", res=x)
    h2 = rmsnorm_fwd(x1, ws["g_xattn"], "norm_xattn")
    mem_n = rmsnorm_fwd(mem, ws["g_mem"], "norm_mem")
    xq = mm_nn(h2, wb["w_xq"], "xq", out_dtype=BF16)
    xk = mm_nn(mem_n, wb["w_xk"], "xk", out_dtype=BF16, tn=1024)
    xv = mm_nn(mem_n, wb["w_xv"], "xv", out_dtype=BF16, tn=1024)
    xo = xattn_fwd(xq, xk, xv)
    x2 = mm_nn(xo, wb["w_xo"], "xo_proj", res=x1)
    h3 = rmsnorm_fwd(x2, ws["g_ffn"], "norm_ffn")
    up_pre = mm_nn(h3, wb["w_up"], "up_proj", out_dtype=BF16, out_bw=D_FF, tn=1408)
    act = ffn_act_fwd(up_pre, w_fc, b_fc)
    x3 = mm_nn(act, wb["w_down"], "down_proj", res=x2, tm=512)
    loss, dx3, dg_final = final_loss(x3, ws["g_final"].reshape(1, -1), target)

    gb, gs = {}, {"g_final": dg_final}
    gb["w_down"] = mm_tn(act, dx3, None, "dw_down", tm=1408, tn=1024)
    dact = mm_nt(dx3, None, wb["w_down"], "d_act", out_dtype=BF16, tn=1408)
    d_up, db_fc = ffn_act_bwd_a(dact, up_pre, w_fc, b_fc)
    d_up_pre, dw_fc = ffn_act_bwd_b(d_up, up_pre, w_fc)
    gs["b_ffn_conv"] = db_fc.reshape(1, 2 * D_FF)
    gs["w_ffn_conv"] = dw_fc.transpose(1, 0, 2).reshape(3, 2 * D_FF)
    gb["w_up"] = mm_tn(h3, d_up_pre, D_FF, "dw_up", shards=4, tn=1408)
    dh3 = mm_nt(d_up_pre, D_FF, wb["w_up"], "d_h3", tm=512)
    dx2, gs["g_ffn"] = rmsnorm_bwd(x2, ws["g_ffn"], dh3, dx3, "norm_ffn_bwd")
    gb["w_xo"] = mm_tn(xo, dx2, None, "dw_xo")
    dxo = mm_nt(dx2, None, wb["w_xo"], "d_xo", out_dtype=BF16)
    dxq, dxk, dxv = xattn_bwd(xq, xk, xv, dxo)
    gb["w_xq"] = mm_tn(h2, dxq, None, "dw_xq")
    gb["w_xk"] = mm_tn(mem_n, dxk, None, "dw_xk", tn=1024)
    gb["w_xv"] = mm_tn(mem_n, dxv, None, "dw_xv", tn=1024)
    dh2 = mm_nt(dxq, None, wb["w_xq"], "d_h2")
    dmem_n = mm_nt(dxk, None, wb["w_xk"], "d_memk", tn=1024)
    dmem_n = mm_nt(dxv, None, wb["w_xv"], "d_memv", res=dmem_n, tn=1024)
    gs["g_mem"] = rmsnorm_bwd(mem, ws["g_mem"], dmem_n, None, "norm_mem_bwd")
    dx1, gs["g_xattn"] = rmsnorm_bwd(x1, ws["g_xattn"], dh2, dx2, "norm_xattn_bwd")
    gb["w_out"] = mm_tn(mixed, dx1, None, "dw_out")
    dmixed = mm_nt(dx1, None, wb["w_out"], "d_mixed")
    do, dgb, dcv, gs["g_attn_out"], gs["g_conv_out"] = mixer_post_bwd_a(
        dmixed, o, proj4, ws["w_short_conv"], ws["g_attn_out"], ws["g_conv_out"])
    dproj, ds_sum = attn_bwd(qkv, bias, o, lse, do)
    gs["rel_bias"] = bias_tables_bwd(ds_sum)
    dproj, gs["w_short_conv"] = mixer_post_bwd_b(dproj.reshape(2, 12, S, BLK), dgb, dcv, proj4, ws["w_short_conv"])
    dproj = dproj.reshape(24, S, BLK)
    gb["w_in"] = mm_tn(h1, dproj, BLK, "dw_in", shards=4, tn=768)
    dh1 = mm_nt(dproj, BLK, wb["w_in"], "d_h1", tm=512)
    grad_x, gs["g_mix"] = rmsnorm_bwd(x, ws["g_mix"], dh1, dx1, "norm_mix_bwd")
    return loss, grad_x, gb, gs


def adamw(w, g, m, v, name):
    R, C = w.shape
    tr = R
    for cand in (256, 352):
        if R > cand and R % cand == 0:
            tr = cand
            break

    def body(w_ref, g_ref, m_ref, v_ref, d_ref, nm_ref, nv_ref):
        gv = g_ref[...]
        mn = ADAM_B1 * m_ref[...] + (1.0 - ADAM_B1) * gv
        vn = ADAM_B2 * v_ref[...] + (1.0 - ADAM_B2) * (gv * gv)
        m_hat = mn / (1.0 - ADAM_B1 ** ADAM_STEP)
        v_hat = vn / (1.0 - ADAM_B2 ** ADAM_STEP)
        d_ref[...] = -ADAM_LR * (m_hat / (jnp.sqrt(v_hat) + ADAM_EPS) + ADAM_WD * w_ref[...])
        nm_ref[...] = mn
        nv_ref[...] = vn

    blk = pl.BlockSpec((tr, C), lambda i: (i, 0))
    return pl.pallas_call(body, grid=(R // tr,), name=name, in_specs=[blk] * 4, out_specs=[blk] * 3,
                          out_shape=[SDS((R, C), F32)] * 3, compiler_params=_cp("parallel"))(w, g, m, v)


BIG = ("w_in", "w_out", "w_xq", "w_xk", "w_xv", "w_xo", "w_up", "w_down")
COL_SHARDED = ("w_in", "w_up")
N_BIG = len(BIG)
ANY = pl.BlockSpec(memory_space=pl.ANY)


def _place():
    x, y, c = lax.axis_index("x"), lax.axis_index("y"), lax.axis_index("c")
    chips = [(1 - x, y), (x, 1 - y), (1 - x, 1 - y)]
    return x, y, c, chips


def _window(full, name, R, C, shard, half):
    r0, nr = (0, R) if half is None else (half * (R // 2), R // 2)
    if name in COL_SHARDED:
        return full.at[pl.ds(r0, nr), pl.ds(shard * C, C)]
    return full.at[pl.ds(shard * R + r0, nr), :]


def place_shard(w, mine_arr, n):
    R, C = w.shape
    col = n in COL_SHARDED

    def body(s_ref, w_ref, o_ref):
        o_ref[...] = w_ref[...].astype(BF16)

    grid_spec = pltpu.PrefetchScalarGridSpec(
        num_scalar_prefetch=1, grid=(1,), in_specs=[pl.BlockSpec((R, C), lambda i, s: (0, 0))],
        out_specs=pl.BlockSpec((R, C), (lambda i, s: (0, s[0])) if col else (lambda i, s: (s[0], 0))))
    return pl.pallas_call(body, grid_spec=grid_spec, name="place_" + n,
                          out_shape=SDS((R, 4 * C) if col else (4 * R, C), BF16),
                          compiler_params=_cp("arbitrary"))(mine_arr, w)


def _gather_jobs(names, shapes):
    nw = len(names)

    def start(full, sems):
        send, recv, fsend, frecv = sems
        x, y, c, chips = _place()
        mine = 2 * x + y
        for w, n in enumerate(names):
            R, C = shapes[w]
            own = _window(full[w], n, R, C, mine, c)
            for j, chip in enumerate(chips):
                pltpu.make_async_remote_copy(src_ref=own, dst_ref=own, send_sem=send.at[w, j], recv_sem=recv.at[w, j],
                                             device_id=(*chip, c), device_id_type=MESH).start()

    def finish(full, sems):
        send, recv, fsend, frecv = sems
        x, y, c, chips = _place()
        mine = 2 * x + y
        sib = (x, y, 1 - c)
        for w, n in enumerate(names):
            R, C = shapes[w]
            for j, chip in enumerate(chips):
                landed = _window(full[w], n, R, C, 2 * chip[0] + chip[1], c)
                pltpu.make_async_remote_copy(src_ref=landed, dst_ref=landed, send_sem=send.at[w, j], recv_sem=recv.at[w, j],
                                             device_id=(*chip, c), device_id_type=MESH).wait_recv()
                pltpu.make_async_remote_copy(src_ref=landed, dst_ref=landed, send_sem=fsend.at[w, j],
                                             recv_sem=frecv.at[w, j], device_id=sib, device_id_type=MESH).start()
        for w, n in enumerate(names):
            R, C = shapes[w]
            own = _window(full[w], n, R, C, mine, c)
            for j, chip in enumerate(chips):
                landed = _window(full[w], n, R, C, 2 * chip[0] + chip[1], c)
                other = _window(full[w], n, R, C, 2 * chip[0] + chip[1], 1 - c)
                pltpu.make_async_remote_copy(src_ref=other, dst_ref=other, send_sem=fsend.at[w, j], recv_sem=frecv.at[w, j],
                                             device_id=sib, device_id_type=MESH).wait_recv()
                pltpu.make_async_remote_copy(src_ref=own, dst_ref=own, send_sem=send.at[w, j], recv_sem=recv.at[w, j],
                                             device_id=(*chip, c), device_id_type=MESH).wait_send()
                pltpu.make_async_remote_copy(src_ref=landed, dst_ref=landed, send_sem=fsend.at[w, j],
                                             recv_sem=frecv.at[w, j], device_id=sib, device_id_type=MESH).wait_send()

    return start, finish, [pltpu.SemaphoreType.DMA((nw, 3))] * 4


class CommJob:
    def __init__(self, ins, out_shapes, inplace, start, finish, sems):
        self.ins, self.out_shapes, self.inplace = list(ins), list(out_shapes), inplace
        self.start, self.finish, self.sems = start, finish, list(sems)


def run_job(name, job):
    n_in, n_out = len(job.ins), len(job.out_shapes)

    def body(*refs):
        ins, outs, sm = refs[:n_in], refs[n_in:n_in + n_out], refs[n_in + n_out:]
        job.start(ins, outs, sm)
        job.finish(ins, outs, sm)

    return pl.pallas_call(
        body, name=name, in_specs=[ANY] * n_in, out_specs=[ANY] * n_out, out_shape=job.out_shapes,
        input_output_aliases={i: i for i in range(n_in)} if job.inplace else {},
        scratch_shapes=job.sems)(*job.ins)


def gather_job(placed, names):
    shapes = []
    for n in names:
        R, C = placed[n].shape
        shapes.append((R, C // 4) if n in COL_SHARDED else (R // 4, C))
    start, finish, sems = _gather_jobs(names, shapes)
    arrays = [placed[n] for n in names]
    return CommJob(arrays, [SDS(a.shape, a.dtype) for a in arrays], True,
                   lambda i, o, s: start(o, s), lambda i, o, s: finish(o, s), sems)


def pair_exchange_job(grads, names):
    shapes = [grads[n].shape for n in names]

    def copies(ins, outs, sems):
        x, y, c, _ = _place()
        return [pltpu.make_async_remote_copy(
            src_ref=ins[w].at[:, pl.ds((1 - c) * (shapes[w][1] // 2), shapes[w][1] // 2), :], dst_ref=outs[w],
            send_sem=sems[0].at[w], recv_sem=sems[1].at[w], device_id=(x, y, 1 - c), device_id_type=MESH)
            for w in range(len(names))]

    def start(ins, outs, sems):
        for cp in copies(ins, outs, sems):
            cp.start()

    def finish(ins, outs, sems):
        for cp in copies(ins, outs, sems):
            cp.wait()

    return CommJob([grads[n] for n in names], [SDS((4, s[1] // 2, s[2]), F32) for s in shapes], False, start, finish,
                   [pltpu.SemaphoreType.DMA((len(names),))] * 2)


def chip_exchange_job(p16, names):
    shapes = [p16[n].shape for n in names]

    def copies(ins, outs, sems):
        x, y, c, chips = _place()
        return [pltpu.make_async_remote_copy(
            src_ref=ins[w].at[2 * chip[0] + chip[1]], dst_ref=outs[w].at[j],
            send_sem=sems[0].at[w, j], recv_sem=sems[1].at[w, j], device_id=(*chip, c), device_id_type=MESH)
            for w in range(len(names)) for j, chip in enumerate(chips)]

    def start(ins, outs, sems):
        for cp in copies(ins, outs, sems):
            cp.start()

    def finish(ins, outs, sems):
        for cp in copies(ins, outs, sems):
            cp.wait()

    return CommJob([p16[n] for n in names], [SDS((3,) + tuple(s[1:]), BF16) for s in shapes], False, start, finish,
                   [pltpu.SemaphoreType.DMA((len(names), 3))] * 2)


def half_exchange_job(full, names):
    shapes = [full[n].shape for n in names]

    def copies(outs, sems):
        x, y, c, _ = _place()
        cps = []
        for w in range(len(names)):
            Rh = shapes[w][0] // 2
            rows = outs[w].at[pl.ds(c * Rh, Rh), :]
            other = outs[w].at[pl.ds((1 - c) * Rh, Rh), :]
            cps.append((pltpu.make_async_remote_copy(src_ref=rows, dst_ref=rows, send_sem=sems[0].at[w], recv_sem=sems[1].at[w],
                                                     device_id=(x, y, 1 - c), device_id_type=MESH),
                        pltpu.make_async_remote_copy(src_ref=other, dst_ref=other, send_sem=sems[0].at[w], recv_sem=sems[1].at[w],
                                                     device_id=(x, y, 1 - c), device_id_type=MESH)))
        return cps

    def start(ins, outs, sems):
        for send, _ in copies(outs, sems):
            send.start()

    def finish(ins, outs, sems):
        for send, recv in copies(outs, sems):
            recv.wait_recv()
            send.wait_send()

    arrays = [full[n] for n in names]
    return CommJob(arrays, [SDS(a.shape, a.dtype) for a in arrays], True, start, finish,
                   [pltpu.SemaphoreType.DMA((len(names),))] * 2)


def allreduce_small(v, name):
    rows = v.shape[0]

    def body(v_ref, o_ref, buf, send, recv):
        x, y, c, _ = _place()
        me = 4 * x + 2 * y + c
        buf[0] = v_ref[...]
        sends = []
        for k in range(1, 8):
            peer = (x ^ (k >> 2), y ^ ((k >> 1) & 1), c ^ (k & 1))
            cp = pltpu.make_async_remote_copy(src_ref=v_ref, dst_ref=buf.at[k], send_sem=send.at[k - 1], recv_sem=recv.at[k - 1],
                                              device_id=peer, device_id_type=MESH)
            cp.start()
            sends.append(cp)
        for cp in sends:
            cp.wait_recv()
        acc = buf[me]
        for d in range(1, 8):
            acc = acc + buf[d ^ me]
        o_ref[...] = acc
        for cp in sends:
            cp.wait_send()

    vm = pl.BlockSpec(memory_space=pltpu.VMEM)
    return pl.pallas_call(
        body, name=name, in_specs=[vm], out_specs=vm, out_shape=SDS((rows, 128), F32),
        scratch_shapes=[pltpu.VMEM((8, rows, 128), F32), pltpu.SemaphoreType.DMA((7,)), pltpu.SemaphoreType.DMA((7,))],
    )(v)


def pair_sum(g, xh, c_arr, name):
    _, R, C = g.shape
    Rh = R // 2

    def body(c_ref, g_ref, x_ref, p32_ref, p16_ref):
        s = g_ref[...] + x_ref[...]
        p32_ref[...] = s
        p16_ref[...] = s.astype(BF16)

    same = pl.BlockSpec((None, Rh, C), lambda s, c: (s, 0, 0))
    grid_spec = pltpu.PrefetchScalarGridSpec(
        num_scalar_prefetch=1, grid=(4,),
        in_specs=[pl.BlockSpec((None, Rh, C), lambda s, c: (s, c[0], 0)), same], out_specs=[same, same])
    return pl.pallas_call(body, grid_spec=grid_spec, name=name, out_shape=[SDS((4, Rh, C), F32), SDS((4, Rh, C), BF16)],
                          compiler_params=_cp("parallel"))(c_arr, g, xh)


def chip_sum(p32, y, sel_arr, name):
    _, Rh, C = p32.shape

    def body(s_ref, p_ref, y_ref, o_ref):
        o_ref[...] = ((p_ref[...] + y_ref[0].astype(F32)) + y_ref[1].astype(F32)) + y_ref[2].astype(F32)

    grid_spec = pltpu.PrefetchScalarGridSpec(
        num_scalar_prefetch=1, grid=(1,),
        in_specs=[pl.BlockSpec((None, Rh, C), lambda i, s: (s[0], 0, 0)), pl.BlockSpec((3, Rh, C), lambda i, s: (0, 0, 0))],
        out_specs=pl.BlockSpec((Rh, C), lambda i, s: (s[1], 0)))
    return pl.pallas_call(body, grid_spec=grid_spec, name=name, out_shape=SDS((2 * Rh, C), F32),
                          compiler_params=_cp("arbitrary"))(sel_arr, p32, y)


SMALL = ("rel_bias", "g_mix", "w_short_conv", "g_attn_out", "g_conv_out", "g_xattn", "g_mem", "g_ffn",
         "w_ffn_conv", "b_ffn_conv", "g_final")
SMALL_FULL = {"rel_bias": (8, 32), "g_mix": (1, 1024), "w_short_conv": (3, 512), "g_attn_out": (1, 512), "g_conv_out": (1, 512),
              "g_xattn": (1, 1024), "g_mem": (1, 1024), "g_ffn": (1, 1024), "w_ffn_conv": (3, 5632), "b_ffn_conv": (1, 5632),
              "g_final": (1024,)}
SMALL_SHARDED = ("w_short_conv", "w_ffn_conv")


def _pack(parts):
    flat = jnp.concatenate([p.reshape(-1) for p in parts])
    rows = -(-flat.shape[0] // 1024) * 8
    return jnp.pad(flat, (0, rows * 128 - flat.shape[0])).reshape(rows, 128)


def _unpack(packed, names):
    flat, out, off = packed.reshape(-1), {}, 0
    for n in names:
        size = int(np.prod(SMALL_FULL[n]))
        out[n] = flat[off:off + size].reshape(SMALL_FULL[n])
        off += size
    return out


def kernel(x, mem, rel_bias, g_mix, w_in, w_short_conv, g_attn_out, g_conv_out, w_out, g_xattn, g_mem, w_xq, w_xk, w_xv, w_xo, g_ffn, w_up, w_ffn_conv, b_ffn_conv, w_down, g_final, loss_target, m_rel_bias, m_g_mix, m_w_in, m_w_short_conv, m_g_attn_out, m_g_conv_out, m_w_out, m_g_xattn, m_g_mem, m_w_xq, m_w_xk, m_w_xv, m_w_xo, m_g_ffn, m_w_up, m_w_ffn_conv, m_b_ffn_conv, m_w_down, m_g_final, v_rel_bias, v_g_mix, v_w_in, v_w_short_conv, v_g_attn_out, v_g_conv_out, v_w_out, v_g_xattn, v_g_mem, v_w_xq, v_w_xk, v_w_xv, v_w_xo, v_g_ffn, v_w_up, v_w_ffn_conv, v_b_ffn_conv, v_w_down, v_g_final):
    names = ("rel_bias", "g_mix", "w_in", "w_short_conv", "g_attn_out", "g_conv_out", "w_out", "g_xattn", "g_mem", "w_xq",
             "w_xk", "w_xv", "w_xo", "g_ffn", "w_up", "w_ffn_conv", "b_ffn_conv", "w_down", "g_final")
    W = dict(zip(names, (rel_bias, g_mix, w_in, w_short_conv, g_attn_out, g_conv_out, w_out, g_xattn, g_mem, w_xq, w_xk, w_xv,
                         w_xo, g_ffn, w_up, w_ffn_conv, b_ffn_conv, w_down, g_final)))
    M = dict(zip(names, (m_rel_bias, m_g_mix, m_w_in, m_w_short_conv, m_g_attn_out, m_g_conv_out, m_w_out, m_g_xattn, m_g_mem,
                         m_w_xq, m_w_xk, m_w_xv, m_w_xo, m_g_ffn, m_w_up, m_w_ffn_conv, m_b_ffn_conv, m_w_down, m_g_final)))
    V = dict(zip(names, (v_rel_bias, v_g_mix, v_w_in, v_w_short_conv, v_g_attn_out, v_g_conv_out, v_w_out, v_g_xattn, v_g_mem,
                         v_w_xq, v_w_xk, v_w_xv, v_w_xo, v_g_ffn, v_w_up, v_w_ffn_conv, v_b_ffn_conv, v_w_down, v_g_final)))
    xi, yi, ci = lax.axis_index("x"), lax.axis_index("y"), lax.axis_index("c")
    mine = 2 * xi + yi
    c_arr = jnp.reshape(ci, (1,)).astype(jnp.int32)
    mine_arr = jnp.reshape(mine, (1,)).astype(jnp.int32)

    placed = {n: place_shard(W[n][0], mine_arr, n) for n in BIG}
    wb = dict(zip(BIG, run_job("gather_weights", gather_job(placed, BIG))))
    first = jnp.where(ci == 0, 1.0, 0.0).astype(F32)
    emb = []
    for n in SMALL_SHARDED:
        shard = W[n][0]
        full = jnp.zeros(SMALL_FULL[n], F32)
        emb.append(lax.dynamic_update_slice(full, shard * first, (0, mine * shard.shape[1])))
    conv_w = allreduce_small(_pack(emb), "gather_small")
    conv_w = _unpack(conv_w, SMALL_SHARDED)
    ws = {n: W[n] for n in SMALL if n not in SMALL_SHARDED}
    ws.update(conv_w)

    loss, grad_x, gb, gs = local_step(x[0], mem[0], loss_target[0], wb, ws)
    loss = lax.psum(loss[0, 0], ("x", "y", "c"))

    gb = {n: (g if g.ndim == 3 else g.reshape(4, g.shape[0] // 4, g.shape[1])) for n, g in gb.items()}
    sel_arr = jnp.stack([mine, ci]).astype(jnp.int32)
    xh = dict(zip(BIG, run_job("pair_exchange", pair_exchange_job(gb, BIG))))
    p32, p16 = {}, {}
    for n in BIG:
        p32[n], p16[n] = pair_sum(gb[n], xh[n], c_arr, "pair_sum_" + n)
    yb = dict(zip(BIG, run_job("chip_exchange", chip_exchange_job(p16, BIG))))
    halves = {n: chip_sum(p32[n], yb[n], sel_arr, "chip_sum_" + n) for n in BIG}
    gfull = dict(zip(BIG, run_job("half_exchange", half_exchange_job(halves, BIG))))

    gsm = _unpack(allreduce_small(_pack([gs[n] for n in SMALL]), "reduce_small"), SMALL)

    grads, delta, new_m, new_v = {}, {}, {}, {}
    for n in BIG:
        grads[n] = gfull[n][None]
        d, nm, nv = adamw(W[n][0], gfull[n], M[n][0], V[n][0], "adamw_" + n)
        delta[n], new_m[n], new_v[n] = d[None], nm[None], nv[None]
    for n in SMALL_SHARDED:
        wid = W[n].shape[2]
        g = lax.dynamic_slice(gsm[n], (0, mine * wid), (3, wid))
        grads[n] = g[None]
        d, nm, nv = adamw(W[n][0], g, M[n][0], V[n][0], "adamw_" + n)
        delta[n], new_m[n], new_v[n] = d[None], nm[None], nv[None]
    rep = [n for n in SMALL if n not in SMALL_SHARDED]
    d, nm, nv = adamw(_pack([W[n] for n in rep]), _pack([gsm[n] for n in rep]), _pack([M[n] for n in rep]),
                      _pack([V[n] for n in rep]), "adamw_small")

    def unpack_rep(p):
        flat, out, off = p.reshape(-1), {}, 0
        for n in rep:
            size = int(np.prod(SMALL_FULL[n]))
            out[n] = flat[off:off + size].reshape(SMALL_FULL[n])
            off += size
        return out

    d, nm, nv = unpack_rep(d), unpack_rep(nm), unpack_rep(nv)
    for n in rep:
        grads[n], delta[n], new_m[n], new_v[n] = gsm[n], d[n], nm[n], nv[n]
    return (loss, grad_x[None], *[grads[n] for n in names], *[delta[n] for n in names],
            *[new_m[n] for n in names], *[new_v[n] for n in names])
```

```python
import functools
import math

import numpy as np
import jax
import jax.numpy as jnp
from jax import lax
from jax.experimental import pallas as pl
from jax.experimental.pallas import tpu as pltpu

F32 = jnp.float32
BF16 = jnp.bfloat16
SDS = jax.ShapeDtypeStruct
MESH = pl.DeviceIdType.MESH

D_MODEL = 1024
ATTN_W = 512
N_HEADS = 8
BLK = 128
PATTERNS = ((128, 1), (512, 4), (2048, 16))
N_BUCKETS = 32
D_FF = 2816
N_MEM_HEADS = 4
MEM_HD = 256
EPS = 1e-6
NEG = -1e30
VMEM_LIMIT = 56 * 1024 * 1024

ADAM_LR, ADAM_B1, ADAM_B2, ADAM_EPS, ADAM_WD, ADAM_STEP = 0.001, 0.9, 0.999, 1e-08, 0.01, 10


def _cp(*sem):
    return pltpu.CompilerParams(dimension_semantics=sem, vmem_limit_bytes=VMEM_LIMIT)


def _dot(a, b):
    return jnp.dot(a, b, preferred_element_type=F32)


def _dot_nt(a, b):
    return lax.dot_general(a, b, (((1,), (1,)), ((), ())), preferred_element_type=F32)


def _dot_tn(a, b):
    return lax.dot_general(a, b, (((0,), (0,)), ((), ())), preferred_element_type=F32)


def _rsum(x):
    return jnp.sum(x, axis=1, keepdims=True)


def rmsnorm_fwd(x, g, name):
    S, Dm = x.shape
    tm = min(S, 512)

    def body(x_ref, g_ref, o_ref):
        xv = x_ref[...]
        r = lax.rsqrt(jnp.mean(xv * xv, axis=1, keepdims=True) + EPS)
        o_ref[...] = (xv * r * g_ref[...]).astype(o_ref.dtype)

    return pl.pallas_call(
        body, grid=(S // tm,), name=name,
        in_specs=[pl.BlockSpec((tm, Dm), lambda i: (i, 0)), pl.BlockSpec((1, Dm), lambda i: (0, 0))],
        out_specs=pl.BlockSpec((tm, Dm), lambda i: (i, 0)),
        out_shape=SDS((S, Dm), BF16), compiler_params=_cp("parallel"))(x, g)


def rmsnorm_bwd(x, g, dh, res, name):
    S, Dm = x.shape
    tm = min(S, 512)
    want_dx = res is not None

    def body(*refs):
        if want_dx:
            x_ref, g_ref, dh_ref, res_ref, dx_ref, dg_ref = refs
        else:
            x_ref, g_ref, dh_ref, dg_ref = refs
        i = pl.program_id(0)
        xv = x_ref[...]
        r = lax.rsqrt(jnp.mean(xv * xv, axis=1, keepdims=True) + EPS)
        xh = xv * r
        dh = dh_ref[...].astype(F32)
        if want_dx:
            gd = dh * g_ref[...]
            dx_ref[...] = res_ref[...] + r * (gd - xh * jnp.mean(gd * xh, axis=1, keepdims=True))

        @pl.when(i == 0)
        def _():
            dg_ref[...] = jnp.zeros_like(dg_ref)

        dg_ref[...] += jnp.sum(dh * xh, axis=0, keepdims=True)

    row = pl.BlockSpec((tm, Dm), lambda i: (i, 0))
    vec = pl.BlockSpec((1, Dm), lambda i: (0, 0))
    if want_dx:
        return pl.pallas_call(
            body, grid=(S // tm,), name=name, in_specs=[row, vec, row, row], out_specs=[row, vec],
            out_shape=[SDS((S, Dm), F32), SDS((1, Dm), F32)], compiler_params=_cp("arbitrary"))(x, g, dh, res)
    return pl.pallas_call(
        body, grid=(S // tm,), name=name, in_specs=[row, vec, row], out_specs=vec,
        out_shape=SDS((1, Dm), F32), compiler_params=_cp("arbitrary"))(x, g, dh)


def _col_spec(bw, tn, rows, row_of, col_of):
    if bw is None:
        return pl.BlockSpec((rows, tn), lambda *g: (row_of(*g), col_of(*g)))
    if tn % bw == 0:
        return pl.BlockSpec((tn // bw, rows, bw), lambda *g: (col_of(*g), row_of(*g), 0))
    per = bw // tn
    return pl.BlockSpec((None, rows, tn), lambda *g: (col_of(*g) // per, row_of(*g), col_of(*g) % per))


def _read_cols(ref, bw, tn):
    if bw is None or tn % bw != 0:
        return ref[...]
    if tn == bw:
        return ref[0]
    return jnp.concatenate([ref[q] for q in range(tn // bw)], axis=1)


def _write_cols(ref, bw, tn, val):
    if bw is None or tn % bw != 0:
        ref[...] = val.astype(ref.dtype)
    else:
        for q in range(tn // bw):
            ref[q] = val[:, q * bw:(q + 1) * bw].astype(ref.dtype)


def mm_nn(a, b, name, *, res=None, out_dtype=F32, out_bw=None, tm=1024, tn=512):
    M, K = a.shape
    N = b.shape[1]
    tm = min(tm, M)

    def body(*refs):
        if res is None:
            a_ref, b_ref, o_ref = refs
        else:
            a_ref, b_ref, r_ref, o_ref = refs
        acc = _dot(a_ref[...].astype(BF16), b_ref[...])
        if res is not None:
            acc = acc + r_ref[...]
        _write_cols(o_ref, out_bw, tn, acc)

    ri, ci = (lambda i, j: i), (lambda i, j: j)
    in_specs = [pl.BlockSpec((tm, K), lambda i, j: (i, 0)), pl.BlockSpec((K, tn), lambda i, j: (0, j))]
    args = [a, b]
    if res is not None:
        in_specs.append(pl.BlockSpec((tm, tn), lambda i, j: (i, j)))
        args.append(res)
    oshape = (M, N) if out_bw is None else (N // out_bw, M, out_bw)
    return pl.pallas_call(
        body, grid=(M // tm, N // tn), name=name, in_specs=in_specs,
        out_specs=_col_spec(out_bw, tn, tm, ri, ci), out_shape=SDS(oshape, out_dtype),
        compiler_params=_cp("parallel", "parallel"))(*args)


def mm_nt(a, a_bw, b, name, *, res=None, out_dtype=F32, tm=1024, tn=512):
    if a_bw is None:
        M, K = a.shape
    else:
        M, K = a.shape[1], a.shape[0] * a_bw
    N = b.shape[0]
    tm = min(tm, M)

    def body(*refs):
        if res is None:
            a_ref, b_ref, o_ref = refs
        else:
            a_ref, b_ref, r_ref, o_ref = refs
        av = _read_cols(a_ref, a_bw, K).astype(BF16)
        acc = _dot_nt(av, b_ref[...])
        if res is not None:
            acc = acc + r_ref[...]
        o_ref[...] = acc.astype(o_ref.dtype)

    in_specs = [_col_spec(a_bw, K, tm, lambda i, j: i, lambda i, j: 0), pl.BlockSpec((tn, K), lambda i, j: (j, 0))]
    args = [a, b]
    if res is not None:
        in_specs.append(pl.BlockSpec((tm, tn), lambda i, j: (i, j)))
        args.append(res)
    return pl.pallas_call(
        body, grid=(M // tm, N // tn), name=name, in_specs=in_specs,
        out_specs=pl.BlockSpec((tm, tn), lambda i, j: (i, j)), out_shape=SDS((M, N), out_dtype),
        compiler_params=_cp("parallel", "parallel"))(*args)


def mm_tn(a, b, b_bw, name, *, shards=None, tm=1024, tn=512, ts=512):
    S, Ka = a.shape
    N = b.shape[1] if b_bw is None else b.shape[0] * b_bw
    ts = min(ts, S)
    tm = min(tm, Ka)

    def body(a_ref, b_ref, o_ref):
        @pl.when(pl.program_id(2) == 0)
        def _():
            o_ref[...] = jnp.zeros_like(o_ref)

        bv = _read_cols(b_ref, b_bw, tn).astype(BF16)
        o_ref[...] += _dot_tn(a_ref[...].astype(BF16), bv)

    in_specs = [pl.BlockSpec((ts, tm), lambda i, j, k: (k, i)),
                _col_spec(b_bw, tn, ts, lambda i, j, k: k, lambda i, j, k: j)]
    if shards is None:
        out_spec, oshape = pl.BlockSpec((tm, tn), lambda i, j, k: (i, j)), (Ka, N)
    else:
        per = (N // shards) // tn
        out_spec = pl.BlockSpec((None, tm, tn), lambda i, j, k: (j // per, i, j % per))
        oshape = (shards, Ka, N // shards)
    return pl.pallas_call(
        body, grid=(Ka // tm, N // tn, S // ts), name=name, in_specs=in_specs, out_specs=out_spec,
        out_shape=SDS(oshape, F32), compiler_params=_cp("parallel", "parallel", "arbitrary"))(a, b)


def _bucket_tables():
    out = np.zeros((3, 2, BLK, BLK), np.int32)
    qi = np.arange(BLK)[:, None]
    for p, (win, dil) in enumerate(PATTERNS):
        w = win // dil
        assert w == BLK
        for half in range(2):
            kj = np.arange(BLK)[None, :] + half * BLK
            steps = qi + w - kj
            valid = (steps >= 0) & (steps <= w)
            dist = np.clip(steps, 0, w) * dil
            dd = np.maximum(dist, 1).astype(np.float32)
            large = 16 + (np.log(dd / np.float32(16)) / np.float32(math.log(2048 / 16)) * np.float32(16)).astype(np.int32)
            large = np.minimum(large, N_BUCKETS - 1)
            out[p, half] = np.where(valid, np.where(dist < 16, dist, large), -1)
    return out


def bias_tables(rel_bias):
    bkt = jnp.asarray(_bucket_tables())

    def body(rb_ref, bkt_ref, o_ref):
        h = pl.program_id(1)
        for half in range(2):
            bk = bkt_ref[half]
            acc = jnp.full((BLK, BLK), NEG, F32)
            for b in range(N_BUCKETS):
                acc = jnp.where(bk == b, rb_ref[h, b], acc)
            o_ref[half] = acc

    return pl.pallas_call(
        body, grid=(3, N_HEADS), name="bias_tables",
        in_specs=[pl.BlockSpec(memory_space=pltpu.SMEM),
                  pl.BlockSpec((None, 2, BLK, BLK), lambda p, h: (p, 0, 0, 0))],
        out_specs=pl.BlockSpec((None, None, 2, BLK, BLK), lambda p, h: (p, h, 0, 0, 0)),
        out_shape=SDS((3, N_HEADS, 2, BLK, BLK), F32), compiler_params=_cp("parallel", "parallel"))(rel_bias, bkt)


def bias_tables_bwd(ds_sum):
    bkt = jnp.asarray(_bucket_tables())

    def body(ds_ref, bkt_ref, o_ref):
        h = pl.program_id(0)

        @pl.when(h == 0)
        def _():
            o_ref[...] = jnp.zeros_like(o_ref)

        rows = lax.broadcasted_iota(jnp.int32, (N_HEADS, N_BUCKETS), 0)
        cols = lax.broadcasted_iota(jnp.int32, (N_HEADS, N_BUCKETS), 1)
        acc = jnp.zeros((N_HEADS, N_BUCKETS), F32)
        for b in range(N_BUCKETS):
            tot = jnp.zeros((1, 1), F32)
            for p in range(3):
                for half in range(2):
                    t = jnp.where(bkt_ref[p, half] == b, ds_ref[p, half], 0.0)
                    tot = tot + jnp.sum(_rsum(t), axis=0, keepdims=True)
            acc = acc + jnp.where((rows == h) & (cols == b), tot, 0.0)
        o_ref[...] += acc

    return pl.pallas_call(
        body, grid=(N_HEADS,), name="bias_tables_bwd",
        in_specs=[pl.BlockSpec((3, None, 2, BLK, BLK), lambda h: (0, h, 0, 0, 0)),
                  pl.BlockSpec((3, 2, BLK, BLK), lambda h: (0, 0, 0, 0))],
        out_specs=pl.BlockSpec((N_HEADS, N_BUCKETS), lambda h: (0, 0)),
        out_shape=SDS((N_HEADS, N_BUCKETS), F32), compiler_params=_cp("arbitrary"))(ds_sum, bkt)


def _rows(start, dil):
    if dil == 1:
        return pl.ds(pl.multiple_of(start, BLK), BLK)
    return pl.ds(start, BLK, stride=dil)


def _block_rows(i, dil):
    b, r = i // dil, i % dil
    cur = b * (BLK * dil) + r
    prv = jnp.maximum(b - 1, 0) * (BLK * dil) + r
    return _rows(cur, dil), _rows(prv, dil), b > 0


def attn_fwd(qkv, bias, job=None):
    S = qkv.shape[2]
    nblk = S // BLK
    car = Carried(job, 2, 2, 1, 4)

    def body(*refs):
        (qkv_ref, bias_ref, o_ref, lse_ref, l_ref), jrefs = car.split(refs)
        car.phase(0, pl.program_id(0), jrefs)
        lane = lax.broadcasted_iota(jnp.int32, (BLK, BLK), 1)
        lo = lane < 64
        masks = (lo, jnp.logical_not(lo))
        for p, (_, dil) in enumerate(PATTERNS):
            def step(i, carry, p=p, dil=dil):
                rc, rp, has_prev = _block_rows(i, dil)
                pen = jnp.where(has_prev, 0.0, NEG)
                q_ref, k_ref, v_ref = qkv_ref.at[0], qkv_ref.at[1], qkv_ref.at[2]
                q2 = q_ref[rc, :] * 0.125
                kc = k_ref[rc, :].astype(BF16)
                kp = k_ref[rp, :].astype(BF16)
                vc = v_ref[rc, :]
                vp = v_ref[rp, :]
                if p > 0:
                    acc_old, m_old, l_old = o_ref[rc, :], lse_ref[rc, :], l_ref[rc, :]
                pv = jnp.zeros((BLK, BLK), F32)
                m_new, l_new, alpha = [], [], []
                for h in range(2):
                    qh = jnp.where(masks[h], q2, 0.0).astype(BF16)
                    sp = _dot_nt(qh, kp) + bias_ref[p, h, 0] + pen
                    sc = _dot_nt(qh, kc) + bias_ref[p, h, 1]
                    mn = jnp.maximum(jnp.max(sp, axis=1, keepdims=True), jnp.max(sc, axis=1, keepdims=True))
                    if p > 0:
                        mo = m_old[:, 64 * h:64 * h + 1]
                        mn = jnp.maximum(mn, mo)
                        al = jnp.exp(mo - mn)
                        alpha.append(al)
                    pp = jnp.exp(sp - mn)
                    pc = jnp.exp(sc - mn)
                    ln = _rsum(pp) + _rsum(pc)
                    if p > 0:
                        ln = ln + al * l_old[:, 64 * h:64 * h + 1]
                    vhp = jnp.where(masks[h], vp, 0.0).astype(BF16)
                    vhc = jnp.where(masks[h], vc, 0.0).astype(BF16)
                    pv = pv + _dot(pp.astype(BF16), vhp) + _dot(pc.astype(BF16), vhc)
                    m_new.append(mn)
                    l_new.append(ln)
                if p > 0:
                    pv = pv + acc_old * jnp.where(lo, alpha[0], alpha[1])
                o_ref[rc, :] = pv
                lse_ref[rc, :] = jnp.where(lo, m_new[0], m_new[1])
                l_ref[rc, :] = jnp.where(lo, l_new[0], l_new[1])
                return carry

            lax.fori_loop(0, nblk, step, 0)

        def fin(i, carry):
            rows = pl.ds(pl.multiple_of(i * 512, 512), 512)
            lv = l_ref[rows, :]
            o_ref[rows, :] = o_ref[rows, :] / lv
            lse_ref[rows, :] = lse_ref[rows, :] + jnp.log(lv)
            return carry

        lax.fori_loop(0, S // 512, fin, 0)
        car.phase(1, pl.program_id(0), jrefs)
        car.phase(2, pl.program_id(0), jrefs)

    outs = pl.pallas_call(
        body, grid=(4,), name="attn_fwd",
        in_specs=[pl.BlockSpec((3, None, S, BLK), lambda g: (0, g, 0, 0)),
                  pl.BlockSpec((3, 2, 2, BLK, BLK), lambda g: (0, g, 0, 0, 0))] + car.in_specs(),
        out_specs=[pl.BlockSpec((None, S, BLK), lambda g: (g, 0, 0)),
                   pl.BlockSpec((None, S, BLK), lambda g: (g, 0, 0))] + car.out_specs(),
        out_shape=[SDS((4, S, BLK), F32), SDS((4, S, BLK), F32)] + car.out_shapes(),
        input_output_aliases=car.aliases(),
        scratch_shapes=[pltpu.VMEM((S, BLK), F32)] + car.sems(), compiler_params=_cp("arbitrary"))(qkv, bias, *car.args())
    return outs[0], outs[1], outs[2:]


def attn_bwd(qkv, bias, o, lse, do, job=None):
    S = qkv.shape[2]
    nblk = S // BLK
    car = Carried(job, 5, 2, 0, 4)

    def body(*refs):
        (qkv_ref, bias_ref, o_ref, lse_ref, do_ref, dqkv_ref, ds_ref), jrefs = car.split(refs)
        car.phase(0, pl.program_id(0), jrefs)
        lane = lax.broadcasted_iota(jnp.int32, (BLK, BLK), 1)
        lo = lane < 64
        masks = (lo, jnp.logical_not(lo))
        dqkv_ref[...] = jnp.zeros_like(dqkv_ref)
        ds_ref[...] = jnp.zeros_like(ds_ref)
        for p, (_, dil) in enumerate(PATTERNS):
            def step(i, carry, p=p, dil=dil):
                rc, rp, has_prev = _block_rows(i, dil)
                pen = jnp.where(has_prev, 0.0, NEG)
                q_ref, k_ref, v_ref = qkv_ref.at[0], qkv_ref.at[1], qkv_ref.at[2]
                q2 = q_ref[rc, :] * 0.125
                kc = k_ref[rc, :].astype(BF16)
                kp = k_ref[rp, :].astype(BF16)
                vc = v_ref[rc, :].astype(BF16)
                vp = v_ref[rp, :].astype(BF16)
                dot = do_ref[rc, :]
                lset = lse_ref[rc, :]
                prod = dot * o_ref[rc, :]
                dq = jnp.zeros((BLK, BLK), F32)
                dkc = jnp.zeros((BLK, BLK), F32)
                dkp = jnp.zeros((BLK, BLK), F32)
                dvc = jnp.zeros((BLK, BLK), F32)
                dvp = jnp.zeros((BLK, BLK), F32)
                for h in range(2):
                    qh = jnp.where(masks[h], q2, 0.0).astype(BF16)
                    doh = jnp.where(masks[h], dot, 0.0).astype(BF16)
                    delta = _rsum(jnp.where(masks[h], prod, 0.0))
                    lh = lset[:, 64 * h:64 * h + 1]
                    pp = jnp.exp(_dot_nt(qh, kp) + bias_ref[p, h, 0] + pen - lh)
                    pc = jnp.exp(_dot_nt(qh, kc) + bias_ref[p, h, 1] - lh)
                    dsp = pp * (_dot_nt(doh, vp) - delta)
                    dsc = pc * (_dot_nt(doh, vc) - delta)
                    ds_ref[p, h, 0] += dsp
                    ds_ref[p, h, 1] += dsc
                    dspb, dscb = dsp.astype(BF16), dsc.astype(BF16)
                    dq = dq + jnp.where(masks[h], _dot(dspb, kp) + _dot(dscb, kc), 0.0)
                    dkp = dkp + _dot_tn(dspb, qh)
                    dkc = dkc + _dot_tn(dscb, qh)
                    dvp = dvp + _dot_tn(pp.astype(BF16), doh)
                    dvc = dvc + _dot_tn(pc.astype(BF16), doh)
                dq_ref, dk_ref, dv_ref = dqkv_ref.at[0], dqkv_ref.at[1], dqkv_ref.at[2]
                dq_ref[rc, :] += dq * 0.125
                dk_ref[rp, :] += dkp
                dk_ref[rc, :] += dkc
                dv_ref[rp, :] += dvp
                dv_ref[rc, :] += dvc
                return carry

            lax.fori_loop(0, nblk, step, 0)
        car.phase(1, pl.program_id(0), jrefs)
        car.phase(2, pl.program_id(0), jrefs)

    blk = pl.BlockSpec((None, S, BLK), lambda g: (g, 0, 0))
    outs = pl.pallas_call(
        body, grid=(4,), name="attn_bwd",
        in_specs=[pl.BlockSpec((3, None, S, BLK), lambda g: (0, g, 0, 0)),
                  pl.BlockSpec((3, 2, 2, BLK, BLK), lambda g: (0, g, 0, 0, 0)), blk, blk, blk] + car.in_specs(),
        out_specs=[pl.BlockSpec((None, 3, None, S, BLK), lambda g: (0, 0, g, 0, 0)),
                   pl.BlockSpec((3, 2, 2, BLK, BLK), lambda g: (0, g, 0, 0, 0))] + car.out_specs(),
        out_shape=[SDS((2, 3, 4, S, BLK), F32), SDS((3, N_HEADS, 2, BLK, BLK), F32)] + car.out_shapes(),
        input_output_aliases=car.aliases(), scratch_shapes=car.sems(),
        compiler_params=_cp("arbitrary"))(qkv, bias, o, lse, do, *car.args())
    return outs[0], outs[1], outs[2:]


def _shift_down(u, k, halo):
    n = u.shape[0]
    row = lax.broadcasted_iota(jnp.int32, u.shape, 0)
    out = pltpu.roll(u, k, 0)
    hn = halo.shape[0]
    for j in range(k):
        out = jnp.where(row == j, halo[hn - k + j:hn - k + j + 1, :], out)
    return out


def _shift_up(u, k, halo):
    n = u.shape[0]
    row = lax.broadcasted_iota(jnp.int32, u.shape, 0)
    out = pltpu.roll(u, n - k, 0)
    for j in range(k):
        out = jnp.where(row == n - k + j, halo[j:j + 1, :], out)
    return out


def _conv3(u, halo, w0, w1, w2):
    return _shift_down(u, 2, halo) * w0 + _shift_down(u, 1, halo) * w1 + u * w2


def _conv3_bwd_in(d, halo, w0, w1, w2):
    return d * w2 + _shift_up(d, 1, halo) * w1 + _shift_up(d, 2, halo) * w0


def _colsum(x):
    return jnp.sum(x, axis=0, keepdims=True)


def _mixer_specs(S, tm):
    conv = pl.BlockSpec((None, 12, tm, BLK), lambda i: (1, 0, i, 0))
    halo = pl.BlockSpec((None, 12, 8, BLK), lambda i: (1, 0, jnp.maximum(i * (tm // 8) - 1, 0), 0))
    ob = pl.BlockSpec((4, tm, BLK), lambda i: (0, i, 0))
    return conv, halo, ob


def _mixer_recompute(i, o_ref, pr_ref, ph_ref, w_ref):
    ob = [o_ref[q] for q in range(4)]
    gb = [pr_ref[q] for q in range(4)]
    gc = [pr_ref[4 + q] for q in range(4)]
    xi = [pr_ref[8 + q] for q in range(4)]
    keep = jnp.where(i > 0, 1.0, 0.0)
    u = [gc[q] * xi[q] for q in range(4)]
    hu = [ph_ref[4 + q] * ph_ref[8 + q] * keep for q in range(4)]
    w = [[w_ref[k:k + 1, q * BLK:(q + 1) * BLK] for k in range(3)] for q in range(4)]
    cv = [_conv3(u[q], hu[q], *w[q]) for q in range(4)]
    return ob, gb, gc, xi, u, hu, cv, w


def _rms_blocks(blocks):
    ss = sum(_rsum(b * b) for b in blocks)
    return lax.rsqrt(ss / (BLK * len(blocks)) + EPS)


def mixer_post_fwd(o, proj, w_sc, g_a, g_c):
    S = o.shape[1]
    tm = 256

    def body(o_ref, pr_ref, ph_ref, w_ref, ga_ref, gc_ref, m_ref):
        i = pl.program_id(0)
        ob, gb, _, _, _, _, cv, _ = _mixer_recompute(i, o_ref, pr_ref, ph_ref, w_ref)
        conv = [gb[q] * cv[q] for q in range(4)]
        ra, rc = _rms_blocks(ob), _rms_blocks(conv)
        for q in range(4):
            sl = slice(q * BLK, (q + 1) * BLK)
            m_ref[:, q * BLK:(q + 1) * BLK] = (ob[q] * ra * ga_ref[:, sl]).astype(BF16)
            m_ref[:, ATTN_W + q * BLK:ATTN_W + (q + 1) * BLK] = (conv[q] * rc * gc_ref[:, sl]).astype(BF16)

    conv_s, halo_s, ob_s = _mixer_specs(S, tm)
    full = lambda r, c: pl.BlockSpec((r, c), lambda i: (0, 0))
    return pl.pallas_call(
        body, grid=(S // tm,), name="mixer_post_fwd",
        in_specs=[ob_s, conv_s, halo_s, full(3, 512), full(1, 512), full(1, 512)],
        out_specs=pl.BlockSpec((tm, D_MODEL), lambda i: (i, 0)), out_shape=SDS((S, D_MODEL), BF16),
        compiler_params=_cp("parallel"))(o, proj, proj, w_sc, g_a, g_c)


def mixer_post_bwd_a(dmixed, o, proj, w_sc, g_a, g_c):
    S = o.shape[1]
    tm = 256

    def body(dm_ref, o_ref, pr_ref, ph_ref, w_ref, ga_ref, gc_ref, do_ref, dgb_ref, dcv_ref, dga_ref, dgc_ref):
        i = pl.program_id(0)
        ob, gb, _, _, _, _, cv, _ = _mixer_recompute(i, o_ref, pr_ref, ph_ref, w_ref)
        conv = [gb[q] * cv[q] for q in range(4)]
        ra, rc = _rms_blocks(ob), _rms_blocks(conv)

        @pl.when(i == 0)
        def _():
            dga_ref[...] = jnp.zeros_like(dga_ref)
            dgc_ref[...] = jnp.zeros_like(dgc_ref)

        for blocks, r, g_ref, off, dg_ref, is_attn in ((ob, ra, ga_ref, 0, dga_ref, True), (conv, rc, gc_ref, ATTN_W, dgc_ref, False)):
            xh = [blocks[q] * r for q in range(4)]
            dm = [dm_ref[:, off + q * BLK:off + (q + 1) * BLK].astype(F32) for q in range(4)]
            gd = [dm[q] * g_ref[:, q * BLK:(q + 1) * BLK] for q in range(4)]
            mean = sum(_rsum(gd[q] * xh[q]) for q in range(4)) / (4 * BLK)
            for q in range(4):
                dg_ref[:, q * BLK:(q + 1) * BLK] += _colsum(dm[q] * xh[q])
                dx = r * (gd[q] - xh[q] * mean)
                if is_attn:
                    do_ref[q] = dx
                else:
                    dgb_ref[q] = dx * cv[q]
                    dcv_ref[q] = dx * gb[q]

    conv_s, halo_s, ob_s = _mixer_specs(S, tm)
    full = lambda r, c: pl.BlockSpec((r, c), lambda i: (0, 0))
    return pl.pallas_call(
        body, grid=(S // tm,), name="mixer_post_bwd_a",
        in_specs=[pl.BlockSpec((tm, D_MODEL), lambda i: (i, 0)), ob_s, conv_s, halo_s, full(3, 512), full(1, 512), full(1, 512)],
        out_specs=[ob_s, ob_s, ob_s, full(1, 512), full(1, 512)],
        out_shape=[SDS((4, S, BLK), F32)] * 3 + [SDS((1, 512), F32)] * 2,
        compiler_params=_cp("arbitrary"))(dmixed, o, proj, proj, w_sc, g_a, g_c)


def mixer_post_bwd_b(dproj, dgb, dcv, proj, w_sc):
    S = proj.shape[2]
    tm = 256
    last = S // 8 - 1

    def body(dp_in, dgb_ref, dcv_ref, dn_ref, pr_ref, ph_ref, w_ref, dp_ref, dw_ref):
        i = pl.program_id(0)
        keep_prev = jnp.where(i > 0, 1.0, 0.0)
        keep_next = jnp.where(i < pl.num_programs(0) - 1, 1.0, 0.0)

        @pl.when(i == 0)
        def _():
            dw_ref[...] = jnp.zeros_like(dw_ref)

        for q in range(4):
            sl = slice(q * BLK, (q + 1) * BLK)
            gc, xi = pr_ref[4 + q], pr_ref[8 + q]
            u = gc * xi
            hu = ph_ref[4 + q] * ph_ref[8 + q] * keep_prev
            dcv = dcv_ref[q]
            w0, w1, w2 = (w_ref[k:k + 1, sl] for k in range(3))
            du = _conv3_bwd_in(dcv, dn_ref[q] * keep_next, w0, w1, w2)
            dw_ref[0:1, sl] += _colsum(dcv * _shift_down(u, 2, hu))
            dw_ref[1:2, sl] += _colsum(dcv * _shift_down(u, 1, hu))
            dw_ref[2:3, sl] += _colsum(dcv * u)
            dp_ref[q] = dgb_ref[q]
            dp_ref[4 + q] = du * xi
            dp_ref[8 + q] = du * gc

    conv_s, halo_s, ob_s = _mixer_specs(S, tm)
    nxt = pl.BlockSpec((4, 8, BLK), lambda i: (0, jnp.minimum((i + 1) * (tm // 8), last), 0))
    full = lambda r, c: pl.BlockSpec((r, c), lambda i: (0, 0))
    return pl.pallas_call(
        body, grid=(S // tm,), name="mixer_post_bwd_b",
        in_specs=[pl.BlockSpec(memory_space=pl.ANY), ob_s, ob_s, nxt, conv_s, halo_s, full(3, 512)],
        out_specs=[conv_s, full(3, 512)],
        out_shape=[SDS(dproj.shape, F32), SDS((3, 512), F32)],
        input_output_aliases={0: 0}, compiler_params=_cp("arbitrary"))(dproj, dgb, dcv, dcv, proj, proj, w_sc)


def xattn_fwd(q, k, v):
    S = q.shape[0]
    tm = 512
    scale = MEM_HD ** -0.5

    def body(q_ref, k_ref, v_ref, o_ref):
        for h in range(N_MEM_HEADS):
            sl = slice(h * MEM_HD, (h + 1) * MEM_HD)
            s = _dot_nt(q_ref[:, sl], k_ref[:, sl]) * scale
            p = jnp.exp(s - jnp.max(s, axis=1, keepdims=True))
            p = p / _rsum(p)
            o_ref[:, sl] = _dot(p.astype(BF16), v_ref[:, sl]).astype(BF16)

    row = pl.BlockSpec((tm, D_MODEL), lambda i: (i, 0))
    kv = pl.BlockSpec(k.shape, lambda i: (0, 0))
    return pl.pallas_call(body, grid=(S // tm,), name="xattn_fwd", in_specs=[row, kv, kv], out_specs=row,
                          out_shape=SDS((S, D_MODEL), BF16), compiler_params=_cp("parallel"))(q, k, v)


def xattn_bwd(q, k, v, do):
    S = q.shape[0]
    tm = 512
    scale = MEM_HD ** -0.5

    def body(q_ref, k_ref, v_ref, do_ref, dq_ref, dk_ref, dv_ref):
        @pl.when(pl.program_id(0) == 0)
        def _():
            dk_ref[...] = jnp.zeros_like(dk_ref)
            dv_ref[...] = jnp.zeros_like(dv_ref)

        for h in range(N_MEM_HEADS):
            sl = slice(h * MEM_HD, (h + 1) * MEM_HD)
            qh, kh, vh, doh = q_ref[:, sl], k_ref[:, sl], v_ref[:, sl], do_ref[:, sl]
            s = _dot_nt(qh, kh) * scale
            p = jnp.exp(s - jnp.max(s, axis=1, keepdims=True))
            p = p / _rsum(p)
            dp = _dot_nt(doh, vh)
            ds = (p * (dp - _rsum(p * dp)) * scale).astype(BF16)
            dq_ref[:, sl] = _dot(ds, kh).astype(BF16)
            dk_ref[:, sl] += _dot_tn(ds, qh)
            dv_ref[:, sl] += _dot_tn(p.astype(BF16), doh)

    row = pl.BlockSpec((tm, D_MODEL), lambda i: (i, 0))
    kv = pl.BlockSpec(k.shape, lambda i: (0, 0))
    return pl.pallas_call(body, grid=(S // tm,), name="xattn_bwd", in_specs=[row, kv, kv, row], out_specs=[row, kv, kv],
                          out_shape=[SDS((S, D_MODEL), BF16), SDS(k.shape, F32), SDS(k.shape, F32)],
                          compiler_params=_cp("arbitrary"))(q, k, v, do)


FFN_TM, FFN_TC = 512, 256


def _ffn_specs(S, order):
    tm, tc = FFN_TM, FFN_TC
    ij = (lambda a, b: (a, b)) if order == "ij" else (lambda a, b: (b, a))
    blk = pl.BlockSpec((2, tm, tc), lambda a, b: (0,) + ij(a, b))
    prev = pl.BlockSpec((2, 16, tc), lambda a, b: (0, jnp.maximum(ij(a, b)[0] * (tm // 16) - 1, 0), ij(a, b)[1]))
    nxt = pl.BlockSpec((2, 16, tc), lambda a, b: (0, jnp.minimum((ij(a, b)[0] + 1) * (tm // 16), S // 16 - 1), ij(a, b)[1]))
    wsp = pl.BlockSpec((2, 3, tc), lambda a, b: (0, 0, ij(a, b)[1]))
    bsp = pl.BlockSpec((2, 1, tc), lambda a, b: (0, 0, ij(a, b)[1]))
    act = pl.BlockSpec((tm, tc), lambda a, b: ij(a, b))
    return blk, prev, nxt, wsp, bsp, act


def _ffn_up(i, up_ref, uh_ref, w_ref, b_ref):
    keep = jnp.where(i > 0, 1.0, 0.0)
    out = []
    for half in range(2):
        u = up_ref[half].astype(F32)
        hu = uh_ref[half].astype(F32) * keep
        w0, w1, w2 = (w_ref[half, k:k + 1, :] for k in range(3))
        out.append((_conv3(u, hu, w0, w1, w2) + b_ref[half], u, hu))
    return out


def ffn_act_fwd(up_pre, w_fc, b_fc):
    S = up_pre.shape[1]

    def body(up_ref, uh_ref, w_ref, b_ref, a_ref):
        (g, _, _), (v, _, _) = _ffn_up(pl.program_id(0), up_ref, uh_ref, w_ref, b_ref)
        a_ref[...] = (g * jax.nn.sigmoid(g) * v).astype(BF16)

    blk, prev, _, wsp, bsp, act = _ffn_specs(S, "ij")
    return pl.pallas_call(body, grid=(S // FFN_TM, D_FF // FFN_TC), name="ffn_act_fwd",
                          in_specs=[blk, prev, wsp, bsp], out_specs=act, out_shape=SDS((S, D_FF), BF16),
                          compiler_params=_cp("parallel", "parallel"))(up_pre, up_pre, w_fc, b_fc)


def ffn_act_bwd_a(dact, up_pre, w_fc, b_fc):
    S = up_pre.shape[1]

    def body(da_ref, up_ref, uh_ref, w_ref, b_ref, du_ref, db_ref):
        i = pl.program_id(1)
        (g, _, _), (v, _, _) = _ffn_up(i, up_ref, uh_ref, w_ref, b_ref)
        da = da_ref[...].astype(F32)
        sg = jax.nn.sigmoid(g)
        dg = da * v * (sg * (1.0 + g * (1.0 - sg)))
        dv = da * g * sg

        @pl.when(i == 0)
        def _():
            db_ref[...] = jnp.zeros_like(db_ref)

        du_ref[0] = dg.astype(BF16)
        du_ref[1] = dv.astype(BF16)
        db_ref[0] += _colsum(dg)
        db_ref[1] += _colsum(dv)

    blk, prev, _, wsp, bsp, act = _ffn_specs(S, "ji")
    return pl.pallas_call(body, grid=(D_FF // FFN_TC, S // FFN_TM), name="ffn_act_bwd_a",
                          in_specs=[act, blk, prev, wsp, bsp], out_specs=[blk, bsp],
                          out_shape=[SDS((2, S, D_FF), BF16), SDS((2, 1, D_FF), F32)],
                          compiler_params=_cp("parallel", "arbitrary"))(dact, up_pre, up_pre, w_fc, b_fc)


def ffn_act_bwd_b(d_up, up_pre, w_fc):
    S = up_pre.shape[1]

    def body(du_ref, dn_ref, up_ref, uh_ref, w_ref, dp_ref, dw_ref):
        i = pl.program_id(1)
        keep_prev = jnp.where(i > 0, 1.0, 0.0)
        keep_next = jnp.where(i < pl.num_programs(1) - 1, 1.0, 0.0)

        @pl.when(i == 0)
        def _():
            dw_ref[...] = jnp.zeros_like(dw_ref)

        for half in range(2):
            d = du_ref[half].astype(F32)
            dn = dn_ref[half].astype(F32) * keep_next
            u = up_ref[half].astype(F32)
            hu = uh_ref[half].astype(F32) * keep_prev
            w0, w1, w2 = (w_ref[half, k:k + 1, :] for k in range(3))
            dp_ref[half] = _conv3_bwd_in(d, dn, w0, w1, w2).astype(BF16)
            dw_ref[half, 0:1, :] += _colsum(d * _shift_down(u, 2, hu))
            dw_ref[half, 1:2, :] += _colsum(d * _shift_down(u, 1, hu))
            dw_ref[half, 2:3, :] += _colsum(d * u)

    blk, prev, nxt, wsp, _, _ = _ffn_specs(S, "ji")
    return pl.pallas_call(body, grid=(D_FF // FFN_TC, S // FFN_TM), name="ffn_act_bwd_b",
                          in_specs=[blk, nxt, blk, prev, wsp], out_specs=[blk, wsp],
                          out_shape=[SDS((2, S, D_FF), BF16), SDS((2, 3, D_FF), F32)],
                          compiler_params=_cp("parallel", "arbitrary"))(d_up, d_up, up_pre, up_pre, w_fc)


def final_loss(x3, g, target):
    S, Dm = x3.shape
    tm = 512

    def body(x_ref, g_ref, t_ref, loss_ref, dx_ref, dg_ref):
        i = pl.program_id(0)
        xv = x_ref[...]
        r = lax.rsqrt(jnp.mean(xv * xv, axis=1, keepdims=True) + EPS)
        xh = xv * r
        err = xh * g_ref[...] - t_ref[...]
        dy = err / Dm
        gd = dy * g_ref[...]
        dx_ref[...] = r * (gd - xh * jnp.mean(gd * xh, axis=1, keepdims=True))

        @pl.when(i == 0)
        def _():
            dg_ref[...] = jnp.zeros_like(dg_ref)
            loss_ref[...] = jnp.zeros_like(loss_ref)

        dg_ref[...] += _colsum(dy * xh)
        loss_ref[...] += 0.5 * _colsum(jnp.mean(err * err, axis=1, keepdims=True))

    row = pl.BlockSpec((tm, Dm), lambda i: (i, 0))
    vec = pl.BlockSpec((1, Dm), lambda i: (0, 0))
    one = pl.BlockSpec((1, 1), lambda i: (0, 0))
    return pl.pallas_call(body, grid=(S // tm,), name="final_loss", in_specs=[row, vec, row], out_specs=[one, row, vec],
                          out_shape=[SDS((1, 1), F32), SDS((S, Dm), F32), SDS((1, Dm), F32)],
                          compiler_params=_cp("arbitrary"))(x3, g, target)


def local_step(x, mem, target, wb, ws, comm=None):
    S = x.shape[0]
    if comm is not None:
        wb = comm.gather_first()
    assert S % 2048 == 0
    w_fc = ws["w_ffn_conv"].reshape(3, 2, D_FF).transpose(1, 0, 2)
    b_fc = ws["b_ffn_conv"].reshape(2, 1, D_FF)

    bias = bias_tables(ws["rel_bias"])
    h1 = rmsnorm_fwd(x, ws["g_mix"], "norm_mix")
    proj = mm_nn(h1, wb["w_in"], "proj", out_bw=BLK, tn=768)
    qkv = proj.reshape(2, 3, 4, S, BLK)[0]
    o, lse, landed = attn_fwd(qkv, bias, job=None if comm is None else comm.gather_rest_job())
    if comm is not None:
        wb.update(comm.gathered_rest(landed))
    proj4 = proj.reshape(2, 12, S, BLK)
    mixed = mixer_post_fwd(o, proj4, ws["w_short_conv"], ws["g_attn_out"], ws["g_conv_out"])
    x1 = mm_nn(mixed, wb["w_out"], "out_proj", res=x)
    h2 = rmsnorm_fwd(x1, ws["g_xattn"], "norm_xattn")
    mem_n = rmsnorm_fwd(mem, ws["g_mem"], "norm_mem")
    xq = mm_nn(h2, wb["w_xq"], "xq", out_dtype=BF16)
    xk = mm_nn(mem_n, wb["w_xk"], "xk", out_dtype=BF16, tn=1024)
    xv = mm_nn(mem_n, wb["w_xv"], "xv", out_dtype=BF16, tn=1024)
    xo = xattn_fwd(xq, xk, xv)
    x2 = mm_nn(xo, wb["w_xo"], "xo_proj", res=x1)
    h3 = rmsnorm_fwd(x2, ws["g_ffn"], "norm_ffn")
    up_pre = mm_nn(h3, wb["w_up"], "up_proj", out_dtype=BF16, out_bw=D_FF, tn=1408)
    act = ffn_act_fwd(up_pre, w_fc, b_fc)
    x3 = mm_nn(act, wb["w_down"], "down_proj", res=x2, tm=512)
    loss, dx3, dg_final = final_loss(x3, ws["g_final"].reshape(1, -1), target)

    gb, gs = {}, {"g_final": dg_final}
    gb["w_down"] = mm_tn(act, dx3, None, "dw_down", tm=1408, tn=1024)
    dact = mm_nt(dx3, None, wb["w_down"], "d_act", out_dtype=BF16, tn=1408)
    d_up, db_fc = ffn_act_bwd_a(dact, up_pre, w_fc, b_fc)
    d_up_pre, dw_fc = ffn_act_bwd_b(d_up, up_pre, w_fc)
    gs["b_ffn_conv"] = db_fc.reshape(1, 2 * D_FF)
    gs["w_ffn_conv"] = dw_fc.transpose(1, 0, 2).reshape(3, 2 * D_FF)
    gb["w_up"] = mm_tn(h3, d_up_pre, D_FF, "dw_up", shards=4, tn=1408)
    dh3 = mm_nt(d_up_pre, D_FF, wb["w_up"], "d_h3", tm=512)
    dx2, gs["g_ffn"] = rmsnorm_bwd(x2, ws["g_ffn"], dh3, dx3, "norm_ffn_bwd")
    gb["w_xo"] = mm_tn(xo, dx2, None, "dw_xo")
    dxo = mm_nt(dx2, None, wb["w_xo"], "d_xo", out_dtype=BF16)
    dxq, dxk, dxv = xattn_bwd(xq, xk, xv, dxo)
    gb["w_xq"] = mm_tn(h2, dxq, None, "dw_xq")
    gb["w_xk"] = mm_tn(mem_n, dxk, None, "dw_xk", tn=1024)
    gb["w_xv"] = mm_tn(mem_n, dxv, None, "dw_xv", tn=1024)
    dh2 = mm_nt(dxq, None, wb["w_xq"], "d_h2")
    dmem_n = mm_nt(dxk, None, wb["w_xk"], "d_memk", tn=1024)
    dmem_n = mm_nt(dxv, None, wb["w_xv"], "d_memv", res=dmem_n, tn=1024)
    gs["g_mem"] = rmsnorm_bwd(mem, ws["g_mem"], dmem_n, None, "norm_mem_bwd")
    dx1, gs["g_xattn"] = rmsnorm_bwd(x1, ws["g_xattn"], dh2, dx2, "norm_xattn_bwd")
    gb["w_out"] = mm_tn(mixed, dx1, None, "dw_out")
    dmixed = mm_nt(dx1, None, wb["w_out"], "d_mixed")
    do, dgb, dcv, gs["g_attn_out"], gs["g_conv_out"] = mixer_post_bwd_a(
        dmixed, o, proj4, ws["w_short_conv"], ws["g_attn_out"], ws["g_conv_out"])
    dproj, ds_sum, landed = attn_bwd(qkv, bias, o, lse, do, job=None if comm is None else comm.reduce_early_job(gb))
    if comm is not None:
        comm.reduce_early_landed(landed)
    gs["rel_bias"] = bias_tables_bwd(ds_sum)
    dproj, gs["w_short_conv"] = mixer_post_bwd_b(dproj.reshape(2, 12, S, BLK), dgb, dcv, proj4, ws["w_short_conv"])
    dproj = dproj.reshape(24, S, BLK)
    gb["w_in"] = mm_tn(h1, dproj, BLK, "dw_in", shards=4, tn=768)
    dh1 = mm_nt(dproj, BLK, wb["w_in"], "d_h1", tm=512)
    grad_x, gs["g_mix"] = rmsnorm_bwd(x, ws["g_mix"], dh1, dx1, "norm_mix_bwd")
    if comm is not None:
        gb = comm.reduce_finish(gb)
    return loss, grad_x, gb, gs


def adamw(w, g, m, v, name):
    R, C = w.shape
    tr = R
    for cand in (256, 352):
        if R > cand and R % cand == 0:
            tr = cand
            break

    def body(w_ref, g_ref, m_ref, v_ref, d_ref, nm_ref, nv_ref):
        gv = g_ref[...]
        mn = ADAM_B1 * m_ref[...] + (1.0 - ADAM_B1) * gv
        vn = ADAM_B2 * v_ref[...] + (1.0 - ADAM_B2) * (gv * gv)
        m_hat = mn / (1.0 - ADAM_B1 ** ADAM_STEP)
        v_hat = vn / (1.0 - ADAM_B2 ** ADAM_STEP)
        d_ref[...] = -ADAM_LR * (m_hat / (jnp.sqrt(v_hat) + ADAM_EPS) + ADAM_WD * w_ref[...])
        nm_ref[...] = mn
        nv_ref[...] = vn

    blk = pl.BlockSpec((tr, C), lambda i: (i, 0))
    return pl.pallas_call(body, grid=(R // tr,), name=name, in_specs=[blk] * 4, out_specs=[blk] * 3,
                          out_shape=[SDS((R, C), F32)] * 3, compiler_params=_cp("parallel"))(w, g, m, v)


BIG = ("w_in", "w_out", "w_xq", "w_xk", "w_xv", "w_xo", "w_up", "w_down")
COL_SHARDED = ("w_in", "w_up")
N_BIG = len(BIG)
ANY = pl.BlockSpec(memory_space=pl.ANY)


def _place():
    x, y, c = lax.axis_index("x"), lax.axis_index("y"), lax.axis_index("c")
    chips = [(1 - x, y), (x, 1 - y), (1 - x, 1 - y)]
    return x, y, c, chips


def _window(full, name, R, C, shard, half):
    r0, nr = (0, R) if half is None else (half * (R // 2), R // 2)
    if name in COL_SHARDED:
        return full.at[pl.ds(r0, nr), pl.ds(shard * C, C)]
    return full.at[pl.ds(shard * R + r0, nr), :]


def place_shard(w, mine_arr, n):
    R, C = w.shape
    col = n in COL_SHARDED

    def body(s_ref, w_ref, o_ref):
        o_ref[...] = w_ref[...].astype(BF16)

    grid_spec = pltpu.PrefetchScalarGridSpec(
        num_scalar_prefetch=1, grid=(1,), in_specs=[pl.BlockSpec((R, C), lambda i, s: (0, 0))],
        out_specs=pl.BlockSpec((R, C), (lambda i, s: (0, s[0])) if col else (lambda i, s: (s[0], 0))))
    return pl.pallas_call(body, grid_spec=grid_spec, name="place_" + n,
                          out_shape=SDS((R, 4 * C) if col else (4 * R, C), BF16),
                          compiler_params=_cp("arbitrary"))(mine_arr, w)


def _gather_jobs(names, shapes):
    nw = len(names)

    def start(full, sems):
        send, recv, fsend, frecv = sems
        x, y, c, chips = _place()
        mine = 2 * x + y
        for w, n in enumerate(names):
            R, C = shapes[w]
            own = _window(full[w], n, R, C, mine, c)
            for j, chip in enumerate(chips):
                pltpu.make_async_remote_copy(src_ref=own, dst_ref=own, send_sem=send.at[w, j], recv_sem=recv.at[w, j],
                                             device_id=(*chip, c), device_id_type=MESH).start()

    def mid(full, sems):
        send, recv, fsend, frecv = sems
        x, y, c, chips = _place()
        sib = (x, y, 1 - c)
        for w, n in enumerate(names):
            R, C = shapes[w]
            for j, chip in enumerate(chips):
                landed = _window(full[w], n, R, C, 2 * chip[0] + chip[1], c)
                pltpu.make_async_remote_copy(src_ref=landed, dst_ref=landed, send_sem=send.at[w, j], recv_sem=recv.at[w, j],
                                             device_id=(*chip, c), device_id_type=MESH).wait_recv()
                pltpu.make_async_remote_copy(src_ref=landed, dst_ref=landed, send_sem=fsend.at[w, j],
                                             recv_sem=frecv.at[w, j], device_id=sib, device_id_type=MESH).start()

    def finish(full, sems):
        send, recv, fsend, frecv = sems
        x, y, c, chips = _place()
        mine = 2 * x + y
        sib = (x, y, 1 - c)
        for w, n in enumerate(names):
            R, C = shapes[w]
            own = _window(full[w], n, R, C, mine, c)
            for j, chip in enumerate(chips):
                landed = _window(full[w], n, R, C, 2 * chip[0] + chip[1], c)
                other = _window(full[w], n, R, C, 2 * chip[0] + chip[1], 1 - c)
                pltpu.make_async_remote_copy(src_ref=other, dst_ref=other, send_sem=fsend.at[w, j], recv_sem=frecv.at[w, j],
                                             device_id=sib, device_id_type=MESH).wait_recv()
                pltpu.make_async_remote_copy(src_ref=own, dst_ref=own, send_sem=send.at[w, j], recv_sem=recv.at[w, j],
                                             device_id=(*chip, c), device_id_type=MESH).wait_send()
                pltpu.make_async_remote_copy(src_ref=landed, dst_ref=landed, send_sem=fsend.at[w, j],
                                             recv_sem=frecv.at[w, j], device_id=sib, device_id_type=MESH).wait_send()

    return start, mid, finish, [pltpu.SemaphoreType.DMA((nw, 3))] * 4


class CommJob:
    def __init__(self, ins, out_shapes, inplace, start, finish, sems, mid=None):
        self.ins, self.out_shapes, self.inplace = list(ins), list(out_shapes), inplace
        self.start, self.mid, self.finish, self.sems = start, mid, finish, list(sems)

    def run(self, phase, ins, outs, sems):
        if phase == 0:
            self.start(ins, outs, sems)
        elif phase == 1:
            if self.mid is not None:
                self.mid(ins, outs, sems)
        else:
            self.finish(ins, outs, sems)


def run_job(name, job):
    n_in, n_out = len(job.ins), len(job.out_shapes)

    def body(*refs):
        ins, outs, sm = refs[:n_in], refs[n_in:n_in + n_out], refs[n_in + n_out:]
        for phase in range(3):
            job.run(phase, ins, outs, sm)

    return pl.pallas_call(
        body, name=name, in_specs=[ANY] * n_in, out_specs=[ANY] * n_out, out_shape=job.out_shapes,
        input_output_aliases={i: i for i in range(n_in)} if job.inplace else {},
        scratch_shapes=job.sems)(*job.ins)


class Carried:
    def __init__(self, job, n_in, n_out, n_scratch, steps):
        self.job, self.n_in, self.n_out, self.n_scratch, self.steps = job, n_in, n_out, n_scratch, steps
        self.nji = len(job.ins) if job else 0
        self.njo = len(job.out_shapes) if job else 0

    def in_specs(self):
        return [ANY] * self.nji

    def out_specs(self):
        return [ANY] * self.njo

    def out_shapes(self):
        return list(self.job.out_shapes) if self.job else []

    def aliases(self):
        if not (self.job and self.job.inplace):
            return {}
        return {self.n_in + i: self.n_out + i for i in range(self.nji)}

    def sems(self):
        return list(self.job.sems) if self.job else []

    def args(self):
        return list(self.job.ins) if self.job else []

    def split(self, refs):
        a = self.n_in
        b = a + self.nji
        c = b + self.n_out
        d = c + self.njo
        e = d + self.n_scratch
        return refs[:a] + refs[b:c] + refs[d:e], (refs[a:b], refs[c:d], refs[e:])

    def phase(self, phase, step, jrefs):
        if self.job is None:
            return
        at = {0: 0, 1: max(self.steps - 2, 0), 2: self.steps - 1}[phase]

        @pl.when(step == at)
        def _():
            self.job.run(phase, *jrefs)


def gather_job(placed, names):
    shapes = []
    for n in names:
        R, C = placed[n].shape
        shapes.append((R, C // 4) if n in COL_SHARDED else (R // 4, C))
    start, mid, finish, sems = _gather_jobs(names, shapes)
    arrays = [placed[n] for n in names]
    return CommJob(arrays, [SDS(a.shape, a.dtype) for a in arrays], True,
                   lambda i, o, s: start(o, s), lambda i, o, s: finish(o, s), sems, mid=lambda i, o, s: mid(o, s))


def pair_exchange_job(grads, names):
    shapes = [grads[n].shape for n in names]

    def copies(ins, outs, sems):
        x, y, c, _ = _place()
        return [pltpu.make_async_remote_copy(
            src_ref=ins[w].at[:, pl.ds((1 - c) * (shapes[w][1] // 2), shapes[w][1] // 2), :], dst_ref=outs[w],
            send_sem=sems[0].at[w], recv_sem=sems[1].at[w], device_id=(x, y, 1 - c), device_id_type=MESH)
            for w in range(len(names))]

    def start(ins, outs, sems):
        for cp in copies(ins, outs, sems):
            cp.start()

    def finish(ins, outs, sems):
        for cp in copies(ins, outs, sems):
            cp.wait()

    return CommJob([grads[n] for n in names], [SDS((4, s[1] // 2, s[2]), F32) for s in shapes], False, start, finish,
                   [pltpu.SemaphoreType.DMA((len(names),))] * 2)


def chip_exchange_job(p16, names):
    shapes = [p16[n].shape for n in names]

    def copies(ins, outs, sems):
        x, y, c, chips = _place()
        return [pltpu.make_async_remote_copy(
            src_ref=ins[w].at[2 * chip[0] + chip[1]], dst_ref=outs[w].at[j],
            send_sem=sems[0].at[w, j], recv_sem=sems[1].at[w, j], device_id=(*chip, c), device_id_type=MESH)
            for w in range(len(names)) for j, chip in enumerate(chips)]

    def start(ins, outs, sems):
        for cp in copies(ins, outs, sems):
            cp.start()

    def finish(ins, outs, sems):
        for cp in copies(ins, outs, sems):
            cp.wait()

    return CommJob([p16[n] for n in names], [SDS((3,) + tuple(s[1:]), BF16) for s in shapes], False, start, finish,
                   [pltpu.SemaphoreType.DMA((len(names), 3))] * 2)


def half_exchange_job(full, names):
    shapes = [full[n].shape for n in names]

    def copies(outs, sems):
        x, y, c, _ = _place()
        cps = []
        for w in range(len(names)):
            Rh = shapes[w][0] // 2
            rows = outs[w].at[pl.ds(c * Rh, Rh), :]
            other = outs[w].at[pl.ds((1 - c) * Rh, Rh), :]
            cps.append((pltpu.make_async_remote_copy(src_ref=rows, dst_ref=rows, send_sem=sems[0].at[w], recv_sem=sems[1].at[w],
                                                     device_id=(x, y, 1 - c), device_id_type=MESH),
                        pltpu.make_async_remote_copy(src_ref=other, dst_ref=other, send_sem=sems[0].at[w], recv_sem=sems[1].at[w],
                                                     device_id=(x, y, 1 - c), device_id_type=MESH)))
        return cps

    def start(ins, outs, sems):
        for send, _ in copies(outs, sems):
            send.start()

    def finish(ins, outs, sems):
        for send, recv in copies(outs, sems):
            recv.wait_recv()
            send.wait_send()

    arrays = [full[n] for n in names]
    return CommJob(arrays, [SDS(a.shape, a.dtype) for a in arrays], True, start, finish,
                   [pltpu.SemaphoreType.DMA((len(names),))] * 2)


def allreduce_small(v, name):
    rows = v.shape[0]

    def body(v_ref, o_ref, buf, send, recv):
        x, y, c, _ = _place()
        me = 4 * x + 2 * y + c
        buf[0] = v_ref[...]
        sends = []
        for k in range(1, 8):
            peer = (x ^ (k >> 2), y ^ ((k >> 1) & 1), c ^ (k & 1))
            cp = pltpu.make_async_remote_copy(src_ref=v_ref, dst_ref=buf.at[k], send_sem=send.at[k - 1], recv_sem=recv.at[k - 1],
                                              device_id=peer, device_id_type=MESH)
            cp.start()
            sends.append(cp)
        for cp in sends:
            cp.wait_recv()
        acc = buf[me]
        for d in range(1, 8):
            acc = acc + buf[d ^ me]
        o_ref[...] = acc
        for cp in sends:
            cp.wait_send()

    vm = pl.BlockSpec(memory_space=pltpu.VMEM)
    return pl.pallas_call(
        body, name=name, in_specs=[vm], out_specs=vm, out_shape=SDS((rows, 128), F32),
        scratch_shapes=[pltpu.VMEM((8, rows, 128), F32), pltpu.SemaphoreType.DMA((7,)), pltpu.SemaphoreType.DMA((7,))],
    )(v)


def pair_sum(g, xh, c_arr, name):
    _, R, C = g.shape
    Rh = R // 2

    def body(c_ref, g_ref, x_ref, p32_ref, p16_ref):
        s = g_ref[...] + x_ref[...]
        p32_ref[...] = s
        p16_ref[...] = s.astype(BF16)

    same = pl.BlockSpec((None, Rh, C), lambda s, c: (s, 0, 0))
    grid_spec = pltpu.PrefetchScalarGridSpec(
        num_scalar_prefetch=1, grid=(4,),
        in_specs=[pl.BlockSpec((None, Rh, C), lambda s, c: (s, c[0], 0)), same], out_specs=[same, same])
    return pl.pallas_call(body, grid_spec=grid_spec, name=name, out_shape=[SDS((4, Rh, C), F32), SDS((4, Rh, C), BF16)],
                          compiler_params=_cp("parallel"))(c_arr, g, xh)


def chip_sum(p32, y, sel_arr, name):
    _, Rh, C = p32.shape

    def body(s_ref, p_ref, y_ref, o_ref):
        o_ref[...] = ((p_ref[...] + y_ref[0].astype(F32)) + y_ref[1].astype(F32)) + y_ref[2].astype(F32)

    grid_spec = pltpu.PrefetchScalarGridSpec(
        num_scalar_prefetch=1, grid=(1,),
        in_specs=[pl.BlockSpec((None, Rh, C), lambda i, s: (s[0], 0, 0)), pl.BlockSpec((3, Rh, C), lambda i, s: (0, 0, 0))],
        out_specs=pl.BlockSpec((Rh, C), lambda i, s: (s[1], 0)))
    return pl.pallas_call(body, grid_spec=grid_spec, name=name, out_shape=SDS((2 * Rh, C), F32),
                          compiler_params=_cp("arbitrary"))(sel_arr, p32, y)


class StepComm:
    FIRST = ("w_in",)
    REST = tuple(n for n in BIG if n != "w_in")

    def __init__(self, placed, c_arr, sel_arr):
        self.placed, self.c_arr, self.sel_arr = placed, c_arr, sel_arr
        self.full, self.p32 = {}, {}

    def gather_first(self):
        return dict(zip(self.FIRST, run_job("gather_first", gather_job(self.placed, self.FIRST))))

    def gather_rest_job(self):
        return gather_job(self.placed, self.REST)

    def gathered_rest(self, landed):
        return dict(zip(self.REST, landed))

    def _pair(self, gb, names, tag):
        g = {n: (gb[n] if gb[n].ndim == 3 else gb[n].reshape(4, gb[n].shape[0] // 4, gb[n].shape[1])) for n in names}
        xh = dict(zip(names, run_job("pair_exchange_" + tag, pair_exchange_job(g, names))))
        p16 = {}
        for n in names:
            self.p32[n], p16[n] = pair_sum(g[n], xh[n], self.c_arr, "pair_sum_" + n)
        return p16

    def _chip_sums(self, names, landed):
        for n, y in zip(names, landed):
            self.full[n] = chip_sum(self.p32[n], y, self.sel_arr, "chip_sum_" + n)

    def reduce_early_job(self, gb):
        return chip_exchange_job(self._pair(gb, self.REST, "a"), self.REST)

    def reduce_early_landed(self, landed):
        self._chip_sums(self.REST, landed)

    def reduce_finish(self, gb):
        p16 = self._pair(gb, self.FIRST, "b")
        self._chip_sums(self.FIRST, run_job("chip_exchange_b", chip_exchange_job(p16, self.FIRST)))
        return dict(zip(BIG, run_job("half_exchange", half_exchange_job(self.full, BIG))))


SMALL = ("rel_bias", "g_mix", "w_short_conv", "g_attn_out", "g_conv_out", "g_xattn", "g_mem", "g_ffn",
         "w_ffn_conv", "b_ffn_conv", "g_final")
SMALL_FULL = {"rel_bias": (8, 32), "g_mix": (1, 1024), "w_short_conv": (3, 512), "g_attn_out": (1, 512), "g_conv_out": (1, 512),
              "g_xattn": (1, 1024), "g_mem": (1, 1024), "g_ffn": (1, 1024), "w_ffn_conv": (3, 5632), "b_ffn_conv": (1, 5632),
              "g_final": (1024,)}
SMALL_SHARDED = ("w_short_conv", "w_ffn_conv")


def _pack(parts):
    flat = jnp.concatenate([p.reshape(-1) for p in parts])
    rows = -(-flat.shape[0] // 1024) * 8
    return jnp.pad(flat, (0, rows * 128 - flat.shape[0])).reshape(rows, 128)


def _unpack(packed, names):
    flat, out, off = packed.reshape(-1), {}, 0
    for n in names:
        size = int(np.prod(SMALL_FULL[n]))
        out[n] = flat[off:off + size].reshape(SMALL_FULL[n])
        off += size
    return out


def kernel(x, mem, rel_bias, g_mix, w_in, w_short_conv, g_attn_out, g_conv_out, w_out, g_xattn, g_mem, w_xq, w_xk, w_xv, w_xo, g_ffn, w_up, w_ffn_conv, b_ffn_conv, w_down, g_final, loss_target, m_rel_bias, m_g_mix, m_w_in, m_w_short_conv, m_g_attn_out, m_g_conv_out, m_w_out, m_g_xattn, m_g_mem, m_w_xq, m_w_xk, m_w_xv, m_w_xo, m_g_ffn, m_w_up, m_w_ffn_conv, m_b_ffn_conv, m_w_down, m_g_final, v_rel_bias, v_g_mix, v_w_in, v_w_short_conv, v_g_attn_out, v_g_conv_out, v_w_out, v_g_xattn, v_g_mem, v_w_xq, v_w_xk, v_w_xv, v_w_xo, v_g_ffn, v_w_up, v_w_ffn_conv, v_b_ffn_conv, v_w_down, v_g_final):
    names = ("rel_bias", "g_mix", "w_in", "w_short_conv", "g_attn_out", "g_conv_out", "w_out", "g_xattn", "g_mem", "w_xq",
             "w_xk", "w_xv", "w_xo", "g_ffn", "w_up", "w_ffn_conv", "b_ffn_conv", "w_down", "g_final")
    W = dict(zip(names, (rel_bias, g_mix, w_in, w_short_conv, g_attn_out, g_conv_out, w_out, g_xattn, g_mem, w_xq, w_xk, w_xv,
                         w_xo, g_ffn, w_up, w_ffn_conv, b_ffn_conv, w_down, g_final)))
    M = dict(zip(names, (m_rel_bias, m_g_mix, m_w_in, m_w_short_conv, m_g_attn_out, m_g_conv_out, m_w_out, m_g_xattn, m_g_mem,
                         m_w_xq, m_w_xk, m_w_xv, m_w_xo, m_g_ffn, m_w_up, m_w_ffn_conv, m_b_ffn_conv, m_w_down, m_g_final)))
    V = dict(zip(names, (v_rel_bias, v_g_mix, v_w_in, v_w_short_conv, v_g_attn_out, v_g_conv_out, v_w_out, v_g_xattn, v_g_mem,
                         v_w_xq, v_w_xk, v_w_xv, v_w_xo, v_g_ffn, v_w_up, v_w_ffn_conv, v_b_ffn_conv, v_w_down, v_g_final)))
    xi, yi, ci = lax.axis_index("x"), lax.axis_index("y"), lax.axis_index("c")
    mine = 2 * xi + yi
    c_arr = jnp.reshape(ci, (1,)).astype(jnp.int32)
    mine_arr = jnp.reshape(mine, (1,)).astype(jnp.int32)

    sel_arr = jnp.stack([mine, ci]).astype(jnp.int32)
    comm = StepComm({n: place_shard(W[n][0], mine_arr, n) for n in BIG}, c_arr, sel_arr)
    first = jnp.where(ci == 0, 1.0, 0.0).astype(F32)
    emb = []
    for n in SMALL_SHARDED:
        shard = W[n][0]
        full = jnp.zeros(SMALL_FULL[n], F32)
        emb.append(lax.dynamic_update_slice(full, shard * first, (0, mine * shard.shape[1])))
    conv_w = allreduce_small(_pack(emb), "gather_small")
    conv_w = _unpack(conv_w, SMALL_SHARDED)
    ws = {n: W[n] for n in SMALL if n not in SMALL_SHARDED}
    ws.update(conv_w)

    loss, grad_x, gfull, gs = local_step(x[0], mem[0], loss_target[0], None, ws, comm)
    loss = lax.psum(loss[0, 0], ("x", "y", "c"))

    gsm = _unpack(allreduce_small(_pack([gs[n] for n in SMALL]), "reduce_small"), SMALL)

    grads, delta, new_m, new_v = {}, {}, {}, {}
    for n in BIG:
        grads[n] = gfull[n][None]
        d, nm, nv = adamw(W[n][0], gfull[n], M[n][0], V[n][0], "adamw_" + n)
        delta[n], new_m[n], new_v[n] = d[None], nm[None], nv[None]
    for n in SMALL_SHARDED:
        wid = W[n].shape[2]
        g = lax.dynamic_slice(gsm[n], (0, mine * wid), (3, wid))
        grads[n] = g[None]
        d, nm, nv = adamw(W[n][0], g, M[n][0], V[n][0], "adamw_" + n)
        delta[n], new_m[n], new_v[n] = d[None], nm[None], nv[None]
    rep = [n for n in SMALL if n not in SMALL_SHARDED]
    d, nm, nv = adamw(_pack([W[n] for n in rep]), _pack([gsm[n] for n in rep]), _pack([M[n] for n in rep]),
                      _pack([V[n] for n in rep]), "adamw_small")

    def unpack_rep(p):
        flat, out, off = p.reshape(-1), {}, 0
        for n in rep:
            size = int(np.prod(SMALL_FULL[n]))
            out[n] = flat[off:off + size].reshape(SMALL_FULL[n])
            off += size
        return out

    d, nm, nv = unpack_rep(d), unpack_rep(nm), unpack_rep(nv)
    for n in rep:
        grads[n], delta[n], new_m[n], new_v[n] = gsm[n], d[n], nm[n], nv[n]
    return (loss, grad_x[None], *[grads[n] for n in names], *[delta[n] for n in names],
            *[new_m[n] for n in names], *[new_v[n] for n in names])
```

```python
import functools
import math

import numpy as np
import jax
import jax.numpy as jnp
from jax import lax
from jax.experimental import pallas as pl
from jax.experimental.pallas import tpu as pltpu

F32 = jnp.float32
BF16 = jnp.bfloat16
SDS = jax.ShapeDtypeStruct
MESH = pl.DeviceIdType.MESH

D_MODEL = 1024
ATTN_W = 512
N_HEADS = 8
BLK = 128
PATTERNS = ((128, 1), (512, 4), (2048, 16))
N_BUCKETS = 32
D_FF = 2816
N_MEM_HEADS = 4
MEM_HD = 256
EPS = 1e-6
NEG = -1e30
VMEM_LIMIT = 56 * 1024 * 1024

ADAM_LR, ADAM_B1, ADAM_B2, ADAM_EPS, ADAM_WD, ADAM_STEP = 0.001, 0.9, 0.999, 1e-08, 0.01, 10


def _cp(*sem):
    return pltpu.CompilerParams(dimension_semantics=sem, vmem_limit_bytes=VMEM_LIMIT)


def _dot(a, b):
    return jnp.dot(a, b, preferred_element_type=F32)


def _dot_nt(a, b):
    return lax.dot_general(a, b, (((1,), (1,)), ((), ())), preferred_element_type=F32)


def _dot_tn(a, b):
    return lax.dot_general(a, b, (((0,), (0,)), ((), ())), preferred_element_type=F32)


def _rsum(x):
    return jnp.sum(x, axis=1, keepdims=True)


def rmsnorm_fwd(x, g, name):
    S, Dm = x.shape
    tm = min(S, 512)

    def body(x_ref, g_ref, o_ref):
        xv = x_ref[...]
        r = lax.rsqrt(jnp.mean(xv * xv, axis=1, keepdims=True) + EPS)
        o_ref[...] = (xv * r * g_ref[...]).astype(o_ref.dtype)

    return pl.pallas_call(
        body, grid=(S // tm,), name=name,
        in_specs=[pl.BlockSpec((tm, Dm), lambda i: (i, 0)), pl.BlockSpec((1, Dm), lambda i: (0, 0))],
        out_specs=pl.BlockSpec((tm, Dm), lambda i: (i, 0)),
        out_shape=SDS((S, Dm), BF16), compiler_params=_cp("parallel"))(x, g)


def rmsnorm_bwd(x, g, dh, res, name):
    S, Dm = x.shape
    tm = min(S, 512)
    want_dx = res is not None

    def body(*refs):
        if want_dx:
            x_ref, g_ref, dh_ref, res_ref, dx_ref, dg_ref = refs
        else:
            x_ref, g_ref, dh_ref, dg_ref = refs
        i = pl.program_id(0)
        xv = x_ref[...]
        r = lax.rsqrt(jnp.mean(xv * xv, axis=1, keepdims=True) + EPS)
        xh = xv * r
        dh = dh_ref[...].astype(F32)
        if want_dx:
            gd = dh * g_ref[...]
            dx_ref[...] = res_ref[...] + r * (gd - xh * jnp.mean(gd * xh, axis=1, keepdims=True))

        @pl.when(i == 0)
        def _():
            dg_ref[...] = jnp.zeros_like(dg_ref)

        dg_ref[...] += jnp.sum(dh * xh, axis=0, keepdims=True)

    row = pl.BlockSpec((tm, Dm), lambda i: (i, 0))
    vec = pl.BlockSpec((1, Dm), lambda i: (0, 0))
    if want_dx:
        return pl.pallas_call(
            body, grid=(S // tm,), name=name, in_specs=[row, vec, row, row], out_specs=[row, vec],
            out_shape=[SDS((S, Dm), F32), SDS((1, Dm), F32)], compiler_params=_cp("arbitrary"))(x, g, dh, res)
    return pl.pallas_call(
        body, grid=(S // tm,), name=name, in_specs=[row, vec, row], out_specs=vec,
        out_shape=SDS((1, Dm), F32), compiler_params=_cp("arbitrary"))(x, g, dh)


def _col_spec(bw, tn, rows, row_of, col_of):
    if bw is None:
        return pl.BlockSpec((rows, tn), lambda *g: (row_of(*g), col_of(*g)))
    if tn % bw == 0:
        return pl.BlockSpec((tn // bw, rows, bw), lambda *g: (col_of(*g), row_of(*g), 0))
    per = bw // tn
    return pl.BlockSpec((None, rows, tn), lambda *g: (col_of(*g) // per, row_of(*g), col_of(*g) % per))


def _read_cols(ref, bw, tn):
    if bw is None or tn % bw != 0:
        return ref[...]
    if tn == bw:
        return ref[0]
    return jnp.concatenate([ref[q] for q in range(tn // bw)], axis=1)


def _write_cols(ref, bw, tn, val):
    if bw is None or tn % bw != 0:
        ref[...] = val.astype(ref.dtype)
    else:
        for q in range(tn // bw):
            ref[q] = val[:, q * bw:(q + 1) * bw].astype(ref.dtype)


def mm_nn(a, b, name, *, res=None, out_dtype=F32, out_bw=None, tm=1024, tn=512):
    M, K = a.shape
    N = b.shape[1]
    tm = min(tm, M)

    def body(*refs):
        if res is None:
            a_ref, b_ref, o_ref = refs
        else:
            a_ref, b_ref, r_ref, o_ref = refs
        acc = _dot(a_ref[...].astype(BF16), b_ref[...])
        if res is not None:
            acc = acc + r_ref[...]
        _write_cols(o_ref, out_bw, tn, acc)

    ri, ci = (lambda i, j: i), (lambda i, j: j)
    in_specs = [pl.BlockSpec((tm, K), lambda i, j: (i, 0)), pl.BlockSpec((K, tn), lambda i, j: (0, j))]
    args = [a, b]
    if res is not None:
        in_specs.append(pl.BlockSpec((tm, tn), lambda i, j: (i, j)))
        args.append(res)
    oshape = (M, N) if out_bw is None else (N // out_bw, M, out_bw)
    return pl.pallas_call(
        body, grid=(M // tm, N // tn), name=name, in_specs=in_specs,
        out_specs=_col_spec(out_bw, tn, tm, ri, ci), out_shape=SDS(oshape, out_dtype),
        compiler_params=_cp("parallel", "parallel"))(*args)


def mm_nt(a, a_bw, b, name, *, res=None, out_dtype=F32, tm=1024, tn=512):
    if a_bw is None:
        M, K = a.shape
    else:
        M, K = a.shape[1], a.shape[0] * a_bw
    N = b.shape[0]
    tm = min(tm, M)

    def body(*refs):
        if res is None:
            a_ref, b_ref, o_ref = refs
        else:
            a_ref, b_ref, r_ref, o_ref = refs
        av = _read_cols(a_ref, a_bw, K).astype(BF16)
        acc = _dot_nt(av, b_ref[...])
        if res is not None:
            acc = acc + r_ref[...]
        o_ref[...] = acc.astype(o_ref.dtype)

    in_specs = [_col_spec(a_bw, K, tm, lambda i, j: i, lambda i, j: 0), pl.BlockSpec((tn, K), lambda i, j: (j, 0))]
    args = [a, b]
    if res is not None:
        in_specs.append(pl.BlockSpec((tm, tn), lambda i, j: (i, j)))
        args.append(res)
    return pl.pallas_call(
        body, grid=(M // tm, N // tn), name=name, in_specs=in_specs,
        out_specs=pl.BlockSpec((tm, tn), lambda i, j: (i, j)), out_shape=SDS((M, N), out_dtype),
        compiler_params=_cp("parallel", "parallel"))(*args)


def mm_tn(a, b, b_bw, name, *, shards=None, tm=1024, tn=512, ts=512):
    S, Ka = a.shape
    N = b.shape[1] if b_bw is None else b.shape[0] * b_bw
    ts = min(ts, S)
    tm = min(tm, Ka)

    def body(a_ref, b_ref, o_ref):
        @pl.when(pl.program_id(2) == 0)
        def _():
            o_ref[...] = jnp.zeros_like(o_ref)

        bv = _read_cols(b_ref, b_bw, tn).astype(BF16)
        o_ref[...] += _dot_tn(a_ref[...].astype(BF16), bv)

    in_specs = [pl.BlockSpec((ts, tm), lambda i, j, k: (k, i)),
                _col_spec(b_bw, tn, ts, lambda i, j, k: k, lambda i, j, k: j)]
    if shards is None:
        out_spec, oshape = pl.BlockSpec((tm, tn), lambda i, j, k: (i, j)), (Ka, N)
    else:
        per = (N // shards) // tn
        out_spec = pl.BlockSpec((None, tm, tn), lambda i, j, k: (j // per, i, j % per))
        oshape = (shards, Ka, N // shards)
    return pl.pallas_call(
        body, grid=(Ka // tm, N // tn, S // ts), name=name, in_specs=in_specs, out_specs=out_spec,
        out_shape=SDS(oshape, F32), compiler_params=_cp("parallel", "parallel", "arbitrary"))(a, b)


def _bucket_tables():
    out = np.zeros((3, 2, BLK, BLK), np.int32)
    qi = np.arange(BLK)[:, None]
    for p, (win, dil) in enumerate(PATTERNS):
        w = win // dil
        assert w == BLK
        for half in range(2):
            kj = np.arange(BLK)[None, :] + half * BLK
            steps = qi + w - kj
            valid = (steps >= 0) & (steps <= w)
            dist = np.clip(steps, 0, w) * dil
            dd = np.maximum(dist, 1).astype(np.float32)
            large = 16 + (np.log(dd / np.float32(16)) / np.float32(math.log(2048 / 16)) * np.float32(16)).astype(np.int32)
            large = np.minimum(large, N_BUCKETS - 1)
            out[p, half] = np.where(valid, np.where(dist < 16, dist, large), -1)
    return out


def bias_tables(rel_bias):
    bkt = jnp.asarray(_bucket_tables())

    def body(rb_ref, bkt_ref, o_ref):
        h = pl.program_id(1)
        for half in range(2):
            bk = bkt_ref[half]
            acc = jnp.full((BLK, BLK), NEG, F32)
            for b in range(N_BUCKETS):
                acc = jnp.where(bk == b, rb_ref[h, b], acc)
            o_ref[half] = acc

    return pl.pallas_call(
        body, grid=(3, N_HEADS), name="bias_tables",
        in_specs=[pl.BlockSpec(memory_space=pltpu.SMEM),
                  pl.BlockSpec((None, 2, BLK, BLK), lambda p, h: (p, 0, 0, 0))],
        out_specs=pl.BlockSpec((None, None, 2, BLK, BLK), lambda p, h: (p, h, 0, 0, 0)),
        out_shape=SDS((3, N_HEADS, 2, BLK, BLK), F32), compiler_params=_cp("parallel", "parallel"))(rel_bias, bkt)


def bias_tables_bwd(ds_sum):
    bkt = jnp.asarray(_bucket_tables())

    def body(ds_ref, bkt_ref, o_ref):
        h = pl.program_id(0)

        @pl.when(h == 0)
        def _():
            o_ref[...] = jnp.zeros_like(o_ref)

        rows = lax.broadcasted_iota(jnp.int32, (N_HEADS, N_BUCKETS), 0)
        cols = lax.broadcasted_iota(jnp.int32, (N_HEADS, N_BUCKETS), 1)
        acc = jnp.zeros((N_HEADS, N_BUCKETS), F32)
        for b in range(N_BUCKETS):
            tot = jnp.zeros((1, 1), F32)
            for p in range(3):
                for half in range(2):
                    t = jnp.where(bkt_ref[p, half] == b, ds_ref[p, half], 0.0)
                    tot = tot + jnp.sum(_rsum(t), axis=0, keepdims=True)
            acc = acc + jnp.where((rows == h) & (cols == b), tot, 0.0)
        o_ref[...] += acc

    return pl.pallas_call(
        body, grid=(N_HEADS,), name="bias_tables_bwd",
        in_specs=[pl.BlockSpec((3, None, 2, BLK, BLK), lambda h: (0, h, 0, 0, 0)),
                  pl.BlockSpec((3, 2, BLK, BLK), lambda h: (0, 0, 0, 0))],
        out_specs=pl.BlockSpec((N_HEADS, N_BUCKETS), lambda h: (0, 0)),
        out_shape=SDS((N_HEADS, N_BUCKETS), F32), compiler_params=_cp("arbitrary"))(ds_sum, bkt)


def _rows(start, dil):
    if dil == 1:
        return pl.ds(pl.multiple_of(start, BLK), BLK)
    return pl.ds(start, BLK, stride=dil)


GRP = 4


def _group_rows(i, dil):
    t = lax.broadcasted_iota(jnp.int32, (GRP, 1, 1), 0)
    if dil == 1:
        first = i * GRP
        cur = [(first + j) * BLK for j in range(GRP)]
        prv = [jnp.maximum(first + j - 1, 0) * BLK for j in range(GRP)]
        pen = jnp.where(first + t > 0, 0.0, NEG)
    else:
        per_b = dil // GRP
        b, r0 = i // per_b, (i % per_b) * GRP
        cur = [b * (BLK * dil) + r0 + j for j in range(GRP)]
        prv = [jnp.maximum(b - 1, 0) * (BLK * dil) + r0 + j for j in range(GRP)]
        pen = jnp.where(b + 0 * t > 0, 0.0, NEG)
    return [_rows(s, dil) for s in cur], [_rows(s, dil) for s in prv], pen


def _load_group(ref, rows):
    return jnp.stack([ref[r, :] for r in rows])


def _bdot_nt(a, b):
    return lax.dot_general(a, b, (((2,), (2,)), ((0,), (0,))), preferred_element_type=F32)


def _bdot(a, b):
    return lax.dot_general(a, b, (((2,), (1,)), ((0,), (0,))), preferred_element_type=F32)


def _bdot_tn(a, b):
    return lax.dot_general(a, b, (((1,), (1,)), ((0,), (0,))), preferred_element_type=F32)


def _lsum(x):
    return jnp.sum(x, axis=-1, keepdims=True)


def attn_fwd(qkv, bias, job=None):
    S = qkv.shape[2]
    nblk = S // BLK
    car = Carried(job, 2, 2, 1, 4)

    def body(*refs):
        (qkv_ref, bias_ref, o_ref, lse_ref, l_ref), jrefs = car.split(refs)
        car.phase(0, pl.program_id(0), jrefs)
        lane = lax.broadcasted_iota(jnp.int32, (GRP, BLK, BLK), 2)
        lo = lane < 64
        masks = (lo, jnp.logical_not(lo))
        q_ref, k_ref, v_ref = qkv_ref.at[0], qkv_ref.at[1], qkv_ref.at[2]
        for p, (_, dil) in enumerate(PATTERNS):
            def step(i, carry, p=p, dil=dil):
                rc, rp, pen = _group_rows(i, dil)
                q2 = _load_group(q_ref, rc) * 0.125
                kc = _load_group(k_ref, rc).astype(BF16)
                kp = _load_group(k_ref, rp).astype(BF16)
                vc = _load_group(v_ref, rc)
                vp = _load_group(v_ref, rp)
                if p > 0:
                    acc_old, m_old, l_old = _load_group(o_ref, rc), _load_group(lse_ref, rc), _load_group(l_ref, rc)
                pv = jnp.zeros((GRP, BLK, BLK), F32)
                m_new, l_new, alpha = [], [], []
                for h in range(2):
                    qh = jnp.where(masks[h], q2, 0.0).astype(BF16)
                    sp = _bdot_nt(qh, kp) + bias_ref[p, h, 0][None] + pen
                    sc = _bdot_nt(qh, kc) + bias_ref[p, h, 1][None]
                    mn = jnp.max(jnp.maximum(sp, sc), axis=-1, keepdims=True)
                    if p > 0:
                        mo = m_old[:, :, 64 * h:64 * h + 1]
                        mn = jnp.maximum(mn, mo)
                        al = jnp.exp(mo - mn)
                        alpha.append(al)
                    pp = jnp.exp(sp - mn)
                    pc = jnp.exp(sc - mn)
                    ln = _lsum(pp + pc)
                    if p > 0:
                        ln = ln + al * l_old[:, :, 64 * h:64 * h + 1]
                    vhp = jnp.where(masks[h], vp, 0.0).astype(BF16)
                    vhc = jnp.where(masks[h], vc, 0.0).astype(BF16)
                    pv = pv + _bdot(pp.astype(BF16), vhp) + _bdot(pc.astype(BF16), vhc)
                    m_new.append(mn)
                    l_new.append(ln)
                if p > 0:
                    pv = pv + acc_old * jnp.where(lo, alpha[0], alpha[1])
                m_t = jnp.where(lo, m_new[0], m_new[1])
                l_t = jnp.where(lo, l_new[0], l_new[1])
                for j, r in enumerate(rc):
                    o_ref[r, :] = pv[j]
                    lse_ref[r, :] = m_t[j]
                    l_ref[r, :] = l_t[j]
                return carry

            lax.fori_loop(0, nblk // GRP, step, 0)

        def fin(i, carry):
            rows = pl.ds(pl.multiple_of(i * 512, 512), 512)
            lv = l_ref[rows, :]
            o_ref[rows, :] = o_ref[rows, :] / lv
            lse_ref[rows, :] = lse_ref[rows, :] + jnp.log(lv)
            return carry

        lax.fori_loop(0, S // 512, fin, 0)
        car.phase(1, pl.program_id(0), jrefs)
        car.phase(2, pl.program_id(0), jrefs)

    outs = pl.pallas_call(
        body, grid=(4,), name="attn_fwd",
        in_specs=[pl.BlockSpec((3, None, S, BLK), lambda g: (0, g, 0, 0)),
                  pl.BlockSpec((3, 2, 2, BLK, BLK), lambda g: (0, g, 0, 0, 0))] + car.in_specs(),
        out_specs=[pl.BlockSpec((None, S, BLK), lambda g: (g, 0, 0)),
                   pl.BlockSpec((None, S, BLK), lambda g: (g, 0, 0))] + car.out_specs(),
        out_shape=[SDS((4, S, BLK), F32), SDS((4, S, BLK), F32)] + car.out_shapes(),
        input_output_aliases=car.aliases(),
        scratch_shapes=[pltpu.VMEM((S, BLK), F32)] + car.sems(), compiler_params=_cp("arbitrary"))(qkv, bias, *car.args())
    return outs[0], outs[1], outs[2:]


def attn_bwd(qkv, bias, o, lse, do, job=None):
    S = qkv.shape[2]
    nblk = S // BLK
    car = Carried(job, 5, 2, 0, 4)

    def body(*refs):
        (qkv_ref, bias_ref, o_ref, lse_ref, do_ref, dqkv_ref, ds_ref), jrefs = car.split(refs)
        car.phase(0, pl.program_id(0), jrefs)
        lane = lax.broadcasted_iota(jnp.int32, (GRP, BLK, BLK), 2)
        lo = lane < 64
        masks = (lo, jnp.logical_not(lo))
        dqkv_ref[...] = jnp.zeros_like(dqkv_ref)
        ds_ref[...] = jnp.zeros_like(ds_ref)
        q_ref, k_ref, v_ref = qkv_ref.at[0], qkv_ref.at[1], qkv_ref.at[2]
        dq_ref, dk_ref, dv_ref = dqkv_ref.at[0], dqkv_ref.at[1], dqkv_ref.at[2]
        for p, (_, dil) in enumerate(PATTERNS):
            def step(i, carry, p=p, dil=dil):
                rc, rp, pen = _group_rows(i, dil)
                q2 = _load_group(q_ref, rc) * 0.125
                kc = _load_group(k_ref, rc).astype(BF16)
                kp = _load_group(k_ref, rp).astype(BF16)
                vc = _load_group(v_ref, rc).astype(BF16)
                vp = _load_group(v_ref, rp).astype(BF16)
                dot = _load_group(do_ref, rc)
                lset = _load_group(lse_ref, rc)
                prod = dot * _load_group(o_ref, rc)
                zero = jnp.zeros((GRP, BLK, BLK), F32)
                dq, dkc, dkp, dvc, dvp = zero, zero, zero, zero, zero
                for h in range(2):
                    qh = jnp.where(masks[h], q2, 0.0).astype(BF16)
                    doh = jnp.where(masks[h], dot, 0.0).astype(BF16)
                    delta = _lsum(jnp.where(masks[h], prod, 0.0))
                    lh = lset[:, :, 64 * h:64 * h + 1]
                    pp = jnp.exp(_bdot_nt(qh, kp) + bias_ref[p, h, 0][None] + pen - lh)
                    pc = jnp.exp(_bdot_nt(qh, kc) + bias_ref[p, h, 1][None] - lh)
                    dsp = pp * (_bdot_nt(doh, vp) - delta)
                    dsc = pc * (_bdot_nt(doh, vc) - delta)
                    ds_ref[p, h, 0] += jnp.sum(dsp, axis=0)
                    ds_ref[p, h, 1] += jnp.sum(dsc, axis=0)
                    dspb, dscb = dsp.astype(BF16), dsc.astype(BF16)
                    dq = dq + jnp.where(masks[h], _bdot(dspb, kp) + _bdot(dscb, kc), 0.0)
                    dkp = dkp + _bdot_tn(dspb, qh)
                    dkc = dkc + _bdot_tn(dscb, qh)
                    dvp = dvp + _bdot_tn(pp.astype(BF16), doh)
                    dvc = dvc + _bdot_tn(pc.astype(BF16), doh)
                for j in range(GRP):
                    dq_ref[rc[j], :] += dq[j] * 0.125
                    dk_ref[rp[j], :] += dkp[j]
                    dk_ref[rc[j], :] += dkc[j]
                    dv_ref[rp[j], :] += dvp[j]
                    dv_ref[rc[j], :] += dvc[j]
                return carry

            lax.fori_loop(0, nblk // GRP, step, 0)
        car.phase(1, pl.program_id(0), jrefs)
        car.phase(2, pl.program_id(0), jrefs)

    blk = pl.BlockSpec((None, S, BLK), lambda g: (g, 0, 0))
    outs = pl.pallas_call(
        body, grid=(4,), name="attn_bwd",
        in_specs=[pl.BlockSpec((3, None, S, BLK), lambda g: (0, g, 0, 0)),
                  pl.BlockSpec((3, 2, 2, BLK, BLK), lambda g: (0, g, 0, 0, 0)), blk, blk, blk] + car.in_specs(),
        out_specs=[pl.BlockSpec((None, 3, None, S, BLK), lambda g: (0, 0, g, 0, 0)),
                   pl.BlockSpec((3, 2, 2, BLK, BLK), lambda g: (0, g, 0, 0, 0))] + car.out_specs(),
        out_shape=[SDS((2, 3, 4, S, BLK), F32), SDS((3, N_HEADS, 2, BLK, BLK), F32)] + car.out_shapes(),
        input_output_aliases=car.aliases(), scratch_shapes=car.sems(),
        compiler_params=_cp("arbitrary"))(qkv, bias, o, lse, do, *car.args())
    return outs[0], outs[1], outs[2:]


def _shift_down(u, k, halo):
    n = u.shape[0]
    row = lax.broadcasted_iota(jnp.int32, u.shape, 0)
    out = pltpu.roll(u, k, 0)
    hn = halo.shape[0]
    for j in range(k):
        out = jnp.where(row == j, halo[hn - k + j:hn - k + j + 1, :], out)
    return out


def _shift_up(u, k, halo):
    n = u.shape[0]
    row = lax.broadcasted_iota(jnp.int32, u.shape, 0)
    out = pltpu.roll(u, n - k, 0)
    for j in range(k):
        out = jnp.where(row == n - k + j, halo[j:j + 1, :], out)
    return out


def _conv3(u, halo, w0, w1, w2):
    return _shift_down(u, 2, halo) * w0 + _shift_down(u, 1, halo) * w1 + u * w2


def _conv3_bwd_in(d, halo, w0, w1, w2):
    return d * w2 + _shift_up(d, 1, halo) * w1 + _shift_up(d, 2, halo) * w0


def _colsum(x):
    return jnp.sum(x, axis=0, keepdims=True)


def _mixer_specs(S, tm):
    conv = pl.BlockSpec((None, 12, tm, BLK), lambda i: (1, 0, i, 0))
    halo = pl.BlockSpec((None, 12, 8, BLK), lambda i: (1, 0, jnp.maximum(i * (tm // 8) - 1, 0), 0))
    ob = pl.BlockSpec((4, tm, BLK), lambda i: (0, i, 0))
    return conv, halo, ob


def _mixer_recompute(i, o_ref, pr_ref, ph_ref, w_ref):
    ob = [o_ref[q] for q in range(4)]
    gb = [pr_ref[q] for q in range(4)]
    gc = [pr_ref[4 + q] for q in range(4)]
    xi = [pr_ref[8 + q] for q in range(4)]
    keep = jnp.where(i > 0, 1.0, 0.0)
    u = [gc[q] * xi[q] for q in range(4)]
    hu = [ph_ref[4 + q] * ph_ref[8 + q] * keep for q in range(4)]
    w = [[w_ref[k:k + 1, q * BLK:(q + 1) * BLK] for k in range(3)] for q in range(4)]
    cv = [_conv3(u[q], hu[q], *w[q]) for q in range(4)]
    return ob, gb, gc, xi, u, hu, cv, w


def _rms_blocks(blocks):
    ss = sum(_rsum(b * b) for b in blocks)
    return lax.rsqrt(ss / (BLK * len(blocks)) + EPS)


def mixer_post_fwd(o, proj, w_sc, g_a, g_c):
    S = o.shape[1]
    tm = 256

    def body(o_ref, pr_ref, ph_ref, w_ref, ga_ref, gc_ref, m_ref):
        i = pl.program_id(0)
        ob, gb, _, _, _, _, cv, _ = _mixer_recompute(i, o_ref, pr_ref, ph_ref, w_ref)
        conv = [gb[q] * cv[q] for q in range(4)]
        ra, rc = _rms_blocks(ob), _rms_blocks(conv)
        for q in range(4):
            sl = slice(q * BLK, (q + 1) * BLK)
            m_ref[:, q * BLK:(q + 1) * BLK] = (ob[q] * ra * ga_ref[:, sl]).astype(BF16)
            m_ref[:, ATTN_W + q * BLK:ATTN_W + (q + 1) * BLK] = (conv[q] * rc * gc_ref[:, sl]).astype(BF16)

    conv_s, halo_s, ob_s = _mixer_specs(S, tm)
    full = lambda r, c: pl.BlockSpec((r, c), lambda i: (0, 0))
    return pl.pallas_call(
        body, grid=(S // tm,), name="mixer_post_fwd",
        in_specs=[ob_s, conv_s, halo_s, full(3, 512), full(1, 512), full(1, 512)],
        out_specs=pl.BlockSpec((tm, D_MODEL), lambda i: (i, 0)), out_shape=SDS((S, D_MODEL), BF16),
        compiler_params=_cp("parallel"))(o, proj, proj, w_sc, g_a, g_c)


def mixer_post_bwd_a(dmixed, o, proj, w_sc, g_a, g_c):
    S = o.shape[1]
    tm = 256

    def body(dm_ref, o_ref, pr_ref, ph_ref, w_ref, ga_ref, gc_ref, do_ref, dgb_ref, dcv_ref, dga_ref, dgc_ref):
        i = pl.program_id(0)
        ob, gb, _, _, _, _, cv, _ = _mixer_recompute(i, o_ref, pr_ref, ph_ref, w_ref)
        conv = [gb[q] * cv[q] for q in range(4)]
        ra, rc = _rms_blocks(ob), _rms_blocks(conv)

        @pl.when(i == 0)
        def _():
            dga_ref[...] = jnp.zeros_like(dga_ref)
            dgc_ref[...] = jnp.zeros_like(dgc_ref)

        for blocks, r, g_ref, off, dg_ref, is_attn in ((ob, ra, ga_ref, 0, dga_ref, True), (conv, rc, gc_ref, ATTN_W, dgc_ref, False)):
            xh = [blocks[q] * r for q in range(4)]
            dm = [dm_ref[:, off + q * BLK:off + (q + 1) * BLK].astype(F32) for q in range(4)]
            gd = [dm[q] * g_ref[:, q * BLK:(q + 1) * BLK] for q in range(4)]
            mean = sum(_rsum(gd[q] * xh[q]) for q in range(4)) / (4 * BLK)
            for q in range(4):
                dg_ref[:, q * BLK:(q + 1) * BLK] += _colsum(dm[q] * xh[q])
                dx = r * (gd[q] - xh[q] * mean)
                if is_attn:
                    do_ref[q] = dx
                else:
                    dgb_ref[q] = dx * cv[q]
                    dcv_ref[q] = dx * gb[q]

    conv_s, halo_s, ob_s = _mixer_specs(S, tm)
    full = lambda r, c: pl.BlockSpec((r, c), lambda i: (0, 0))
    return pl.pallas_call(
        body, grid=(S // tm,), name="mixer_post_bwd_a",
        in_specs=[pl.BlockSpec((tm, D_MODEL), lambda i: (i, 0)), ob_s, conv_s, halo_s, full(3, 512), full(1, 512), full(1, 512)],
        out_specs=[ob_s, ob_s, ob_s, full(1, 512), full(1, 512)],
        out_shape=[SDS((4, S, BLK), F32)] * 3 + [SDS((1, 512), F32)] * 2,
        compiler_params=_cp("arbitrary"))(dmixed, o, proj, proj, w_sc, g_a, g_c)


def mixer_post_bwd_b(dproj, dgb, dcv, proj, w_sc):
    S = proj.shape[2]
    tm = 256
    last = S // 8 - 1

    def body(dp_in, dgb_ref, dcv_ref, dn_ref, pr_ref, ph_ref, w_ref, dp_ref, dw_ref):
        i = pl.program_id(0)
        keep_prev = jnp.where(i > 0, 1.0, 0.0)
        keep_next = jnp.where(i < pl.num_programs(0) - 1, 1.0, 0.0)

        @pl.when(i == 0)
        def _():
            dw_ref[...] = jnp.zeros_like(dw_ref)

        for q in range(4):
            sl = slice(q * BLK, (q + 1) * BLK)
            gc, xi = pr_ref[4 + q], pr_ref[8 + q]
            u = gc * xi
            hu = ph_ref[4 + q] * ph_ref[8 + q] * keep_prev
            dcv = dcv_ref[q]
            w0, w1, w2 = (w_ref[k:k + 1, sl] for k in range(3))
            du = _conv3_bwd_in(dcv, dn_ref[q] * keep_next, w0, w1, w2)
            dw_ref[0:1, sl] += _colsum(dcv * _shift_down(u, 2, hu))
            dw_ref[1:2, sl] += _colsum(dcv * _shift_down(u, 1, hu))
            dw_ref[2:3, sl] += _colsum(dcv * u)
            dp_ref[q] = dgb_ref[q]
            dp_ref[4 + q] = du * xi
            dp_ref[8 + q] = du * gc

    conv_s, halo_s, ob_s = _mixer_specs(S, tm)
    nxt = pl.BlockSpec((4, 8, BLK), lambda i: (0, jnp.minimum((i + 1) * (tm // 8), last), 0))
    full = lambda r, c: pl.BlockSpec((r, c), lambda i: (0, 0))
    return pl.pallas_call(
        body, grid=(S // tm,), name="mixer_post_bwd_b",
        in_specs=[pl.BlockSpec(memory_space=pl.ANY), ob_s, ob_s, nxt, conv_s, halo_s, full(3, 512)],
        out_specs=[conv_s, full(3, 512)],
        out_shape=[SDS(dproj.shape, F32), SDS((3, 512), F32)],
        input_output_aliases={0: 0}, compiler_params=_cp("arbitrary"))(dproj, dgb, dcv, dcv, proj, proj, w_sc)


def xattn_fwd(q, k, v):
    S = q.shape[0]
    tm = 512
    scale = MEM_HD ** -0.5

    def body(q_ref, k_ref, v_ref, o_ref):
        for h in range(N_MEM_HEADS):
            sl = slice(h * MEM_HD, (h + 1) * MEM_HD)
            s = _dot_nt(q_ref[:, sl], k_ref[:, sl]) * scale
            p = jnp.exp(s - jnp.max(s, axis=1, keepdims=True))
            p = p / _rsum(p)
            o_ref[:, sl] = _dot(p.astype(BF16), v_ref[:, sl]).astype(BF16)

    row = pl.BlockSpec((tm, D_MODEL), lambda i: (i, 0))
    kv = pl.BlockSpec(k.shape, lambda i: (0, 0))
    return pl.pallas_call(body, grid=(S // tm,), name="xattn_fwd", in_specs=[row, kv, kv], out_specs=row,
                          out_shape=SDS((S, D_MODEL), BF16), compiler_params=_cp("parallel"))(q, k, v)


def xattn_bwd(q, k, v, do):
    S = q.shape[0]
    tm = 512
    scale = MEM_HD ** -0.5

    def body(q_ref, k_ref, v_ref, do_ref, dq_ref, dk_ref, dv_ref):
        @pl.when(pl.program_id(0) == 0)
        def _():
            dk_ref[...] = jnp.zeros_like(dk_ref)
            dv_ref[...] = jnp.zeros_like(dv_ref)

        for h in range(N_MEM_HEADS):
            sl = slice(h * MEM_HD, (h + 1) * MEM_HD)
            qh, kh, vh, doh = q_ref[:, sl], k_ref[:, sl], v_ref[:, sl], do_ref[:, sl]
            s = _dot_nt(qh, kh) * scale
            p = jnp.exp(s - jnp.max(s, axis=1, keepdims=True))
            p = p / _rsum(p)
            dp = _dot_nt(doh, vh)
            ds = (p * (dp - _rsum(p * dp)) * scale).astype(BF16)
            dq_ref[:, sl] = _dot(ds, kh).astype(BF16)
            dk_ref[:, sl] += _dot_tn(ds, qh)
            dv_ref[:, sl] += _dot_tn(p.astype(BF16), doh)

    row = pl.BlockSpec((tm, D_MODEL), lambda i: (i, 0))
    kv = pl.BlockSpec(k.shape, lambda i: (0, 0))
    return pl.pallas_call(body, grid=(S // tm,), name="xattn_bwd", in_specs=[row, kv, kv, row], out_specs=[row, kv, kv],
                          out_shape=[SDS((S, D_MODEL), BF16), SDS(k.shape, F32), SDS(k.shape, F32)],
                          compiler_params=_cp("arbitrary"))(q, k, v, do)


FFN_TM, FFN_TC = 512, 256


def _ffn_specs(S, order):
    tm, tc = FFN_TM, FFN_TC
    ij = (lambda a, b: (a, b)) if order == "ij" else (lambda a, b: (b, a))
    blk = pl.BlockSpec((2, tm, tc), lambda a, b: (0,) + ij(a, b))
    prev = pl.BlockSpec((2, 16, tc), lambda a, b: (0, jnp.maximum(ij(a, b)[0] * (tm // 16) - 1, 0), ij(a, b)[1]))
    nxt = pl.BlockSpec((2, 16, tc), lambda a, b: (0, jnp.minimum((ij(a, b)[0] + 1) * (tm // 16), S // 16 - 1), ij(a, b)[1]))
    wsp = pl.BlockSpec((2, 3, tc), lambda a, b: (0, 0, ij(a, b)[1]))
    bsp = pl.BlockSpec((2, 1, tc), lambda a, b: (0, 0, ij(a, b)[1]))
    act = pl.BlockSpec((tm, tc), lambda a, b: ij(a, b))
    return blk, prev, nxt, wsp, bsp, act


def _ffn_up(i, up_ref, uh_ref, w_ref, b_ref):
    keep = jnp.where(i > 0, 1.0, 0.0)
    out = []
    for half in range(2):
        u = up_ref[half].astype(F32)
        hu = uh_ref[half].astype(F32) * keep
        w0, w1, w2 = (w_ref[half, k:k + 1, :] for k in range(3))
        out.append((_conv3(u, hu, w0, w1, w2) + b_ref[half], u, hu))
    return out


def ffn_act_fwd(up_pre, w_fc, b_fc):
    S = up_pre.shape[1]

    def body(up_ref, uh_ref, w_ref, b_ref, a_ref):
        (g, _, _), (v, _, _) = _ffn_up(pl.program_id(0), up_ref, uh_ref, w_ref, b_ref)
        a_ref[...] = (g * jax.nn.sigmoid(g) * v).astype(BF16)

    blk, prev, _, wsp, bsp, act = _ffn_specs(S, "ij")
    return pl.pallas_call(body, grid=(S // FFN_TM, D_FF // FFN_TC), name="ffn_act_fwd",
                          in_specs=[blk, prev, wsp, bsp], out_specs=act, out_shape=SDS((S, D_FF), BF16),
                          compiler_params=_cp("parallel", "parallel"))(up_pre, up_pre, w_fc, b_fc)


def ffn_act_bwd_a(dact, up_pre, w_fc, b_fc):
    S = up_pre.shape[1]

    def body(da_ref, up_ref, uh_ref, w_ref, b_ref, du_ref, db_ref):
        i = pl.program_id(1)
        (g, _, _), (v, _, _) = _ffn_up(i, up_ref, uh_ref, w_ref, b_ref)
        da = da_ref[...].astype(F32)
        sg = jax.nn.sigmoid(g)
        dg = da * v * (sg * (1.0 + g * (1.0 - sg)))
        dv = da * g * sg

        @pl.when(i == 0)
        def _():
            db_ref[...] = jnp.zeros_like(db_ref)

        du_ref[0] = dg.astype(BF16)
        du_ref[1] = dv.astype(BF16)
        db_ref[0] += _colsum(dg)
        db_ref[1] += _colsum(dv)

    blk, prev, _, wsp, bsp, act = _ffn_specs(S, "ji")
    return pl.pallas_call(body, grid=(D_FF // FFN_TC, S // FFN_TM), name="ffn_act_bwd_a",
                          in_specs=[act, blk, prev, wsp, bsp], out_specs=[blk, bsp],
                          out_shape=[SDS((2, S, D_FF), BF16), SDS((2, 1, D_FF), F32)],
                          compiler_params=_cp("parallel", "arbitrary"))(dact, up_pre, up_pre, w_fc, b_fc)


def ffn_act_bwd_b(d_up, up_pre, w_fc):
    S = up_pre.shape[1]

    def body(du_ref, dn_ref, up_ref, uh_ref, w_ref, dp_ref, dw_ref):
        i = pl.program_id(1)
        keep_prev = jnp.where(i > 0, 1.0, 0.0)
        keep_next = jnp.where(i < pl.num_programs(1) - 1, 1.0, 0.0)

        @pl.when(i == 0)
        def _():
            dw_ref[...] = jnp.zeros_like(dw_ref)

        for half in range(2):
            d = du_ref[half].astype(F32)
            dn = dn_ref[half].astype(F32) * keep_next
            u = up_ref[half].astype(F32)
            hu = uh_ref[half].astype(F32) * keep_prev
            w0, w1, w2 = (w_ref[half, k:k + 1, :] for k in range(3))
            dp_ref[half] = _conv3_bwd_in(d, dn, w0, w1, w2).astype(BF16)
            dw_ref[half, 0:1, :] += _colsum(d * _shift_down(u, 2, hu))
            dw_ref[half, 1:2, :] += _colsum(d * _shift_down(u, 1, hu))
            dw_ref[half, 2:3, :] += _colsum(d * u)

    blk, prev, nxt, wsp, _, _ = _ffn_specs(S, "ji")
    return pl.pallas_call(body, grid=(D_FF // FFN_TC, S // FFN_TM), name="ffn_act_bwd_b",
                          in_specs=[blk, nxt, blk, prev, wsp], out_specs=[blk, wsp],
                          out_shape=[SDS((2, S, D_FF), BF16), SDS((2, 3, D_FF), F32)],
                          compiler_params=_cp("parallel", "arbitrary"))(d_up, d_up, up_pre, up_pre, w_fc)


def final_loss(x3, g, target):
    S, Dm = x3.shape
    tm = 512

    def body(x_ref, g_ref, t_ref, loss_ref, dx_ref, dg_ref):
        i = pl.program_id(0)
        xv = x_ref[...]
        r = lax.rsqrt(jnp.mean(xv * xv, axis=1, keepdims=True) + EPS)
        xh = xv * r
        err = xh * g_ref[...] - t_ref[...]
        dy = err / Dm
        gd = dy * g_ref[...]
        dx_ref[...] = r * (gd - xh * jnp.mean(gd * xh, axis=1, keepdims=True))

        @pl.when(i == 0)
        def _():
            dg_ref[...] = jnp.zeros_like(dg_ref)
            loss_ref[...] = jnp.zeros_like(loss_ref)

        dg_ref[...] += _colsum(dy * xh)
        loss_ref[...] += 0.5 * _colsum(jnp.mean(err * err, axis=1, keepdims=True))

    row = pl.BlockSpec((tm, Dm), lambda i: (i, 0))
    vec = pl.BlockSpec((1, Dm), lambda i: (0, 0))
    one = pl.BlockSpec((1, 1), lambda i: (0, 0))
    return pl.pallas_call(body, grid=(S // tm,), name="final_loss", in_specs=[row, vec, row], out_specs=[one, row, vec],
                          out_shape=[SDS((1, 1), F32), SDS((S, Dm), F32), SDS((1, Dm), F32)],
                          compiler_params=_cp("arbitrary"))(x3, g, target)


def local_step(x, mem, target, wb, ws, comm=None):
    S = x.shape[0]
    if comm is not None:
        wb = comm.gather_first()
    assert S % 2048 == 0
    w_fc = ws["w_ffn_conv"].reshape(3, 2, D_FF).transpose(1, 0, 2)
    b_fc = ws["b_ffn_conv"].reshape(2, 1, D_FF)

    bias = bias_tables(ws["rel_bias"])
    h1 = rmsnorm_fwd(x, ws["g_mix"], "norm_mix")
    proj = mm_nn(h1, wb["w_in"], "proj", out_bw=BLK, tn=768)
    qkv = proj.reshape(2, 3, 4, S, BLK)[0]
    o, lse, landed = attn_fwd(qkv, bias, job=None if comm is None else comm.gather_rest_job())
    if comm is not None:
        wb.update(comm.gathered_rest(landed))
    proj4 = proj.reshape(2, 12, S, BLK)
    mixed = mixer_post_fwd(o, proj4, ws["w_short_conv"], ws["g_attn_out"], ws["g_conv_out"])
    x1 = mm_nn(mixed, wb["w_out"], "out_proj", res=x)
    h2 = rmsnorm_fwd(x1, ws["g_xattn"], "norm_xattn")
    mem_n = rmsnorm_fwd(mem, ws["g_mem"], "norm_mem")
    xq = mm_nn(h2, wb["w_xq"], "xq", out_dtype=BF16)
    xk = mm_nn(mem_n, wb["w_xk"], "xk", out_dtype=BF16, tn=1024)
    xv = mm_nn(mem_n, wb["w_xv"], "xv", out_dtype=BF16, tn=1024)
    xo = xattn_fwd(xq, xk, xv)
    x2 = mm_nn(xo, wb["w_xo"], "xo_proj", res=x1)
    h3 = rmsnorm_fwd(x2, ws["g_ffn"], "norm_ffn")
    up_pre = mm_nn(h3, wb["w_up"], "up_proj", out_dtype=BF16, out_bw=D_FF, tn=1408)
    act = ffn_act_fwd(up_pre, w_fc, b_fc)
    x3 = mm_nn(act, wb["w_down"], "down_proj", res=x2, tm=512)
    loss, dx3, dg_final = final_loss(x3, ws["g_final"].reshape(1, -1), target)

    gb, gs = {}, {"g_final": dg_final}
    gb["w_down"] = mm_tn(act, dx3, None, "dw_down", tm=1408, tn=1024)
    dact = mm_nt(dx3, None, wb["w_down"], "d_act", out_dtype=BF16, tn=1408)
    d_up, db_fc = ffn_act_bwd_a(dact, up_pre, w_fc, b_fc)
    d_up_pre, dw_fc = ffn_act_bwd_b(d_up, up_pre, w_fc)
    gs["b_ffn_conv"] = db_fc.reshape(1, 2 * D_FF)
    gs["w_ffn_conv"] = dw_fc.transpose(1, 0, 2).reshape(3, 2 * D_FF)
    gb["w_up"] = mm_tn(h3, d_up_pre, D_FF, "dw_up", shards=4, tn=1408)
    dh3 = mm_nt(d_up_pre, D_FF, wb["w_up"], "d_h3", tm=512)
    dx2, gs["g_ffn"] = rmsnorm_bwd(x2, ws["g_ffn"], dh3, dx3, "norm_ffn_bwd")
    gb["w_xo"] = mm_tn(xo, dx2, None, "dw_xo")
    dxo = mm_nt(dx2, None, wb["w_xo"], "d_xo", out_dtype=BF16)
    dxq, dxk, dxv = xattn_bwd(xq, xk, xv, dxo)
    gb["w_xq"] = mm_tn(h2, dxq, None, "dw_xq")
    gb["w_xk"] = mm_tn(mem_n, dxk, None, "dw_xk", tn=1024)
    gb["w_xv"] = mm_tn(mem_n, dxv, None, "dw_xv", tn=1024)
    dh2 = mm_nt(dxq, None, wb["w_xq"], "d_h2")
    dmem_n = mm_nt(dxk, None, wb["w_xk"], "d_memk", tn=1024)
    dmem_n = mm_nt(dxv, None, wb["w_xv"], "d_memv", res=dmem_n, tn=1024)
    gs["g_mem"] = rmsnorm_bwd(mem, ws["g_mem"], dmem_n, None, "norm_mem_bwd")
    dx1, gs["g_xattn"] = rmsnorm_bwd(x1, ws["g_xattn"], dh2, dx2, "norm_xattn_bwd")
    gb["w_out"] = mm_tn(mixed, dx1, None, "dw_out")
    dmixed = mm_nt(dx1, None, wb["w_out"], "d_mixed")
    do, dgb, dcv, gs["g_attn_out"], gs["g_conv_out"] = mixer_post_bwd_a(
        dmixed, o, proj4, ws["w_short_conv"], ws["g_attn_out"], ws["g_conv_out"])
    dproj, ds_sum, landed = attn_bwd(qkv, bias, o, lse, do, job=None if comm is None else comm.reduce_early_job(gb))
    if comm is not None:
        comm.reduce_early_landed(landed)
    gs["rel_bias"] = bias_tables_bwd(ds_sum)
    dproj, gs["w_short_conv"] = mixer_post_bwd_b(dproj.reshape(2, 12, S, BLK), dgb, dcv, proj4, ws["w_short_conv"])
    dproj = dproj.reshape(24, S, BLK)
    gb["w_in"] = mm_tn(h1, dproj, BLK, "dw_in", shards=4, tn=768)
    dh1 = mm_nt(dproj, BLK, wb["w_in"], "d_h1", tm=512)
    grad_x, gs["g_mix"] = rmsnorm_bwd(x, ws["g_mix"], dh1, dx1, "norm_mix_bwd")
    if comm is not None:
        gb = comm.reduce_finish(gb)
    return loss, grad_x, gb, gs


def adamw(w, g, m, v, name):
    R, C = w.shape
    tr = R
    for cand in (256, 352):
        if R > cand and R % cand == 0:
            tr = cand
            break

    def body(w_ref, g_ref, m_ref, v_ref, d_ref, nm_ref, nv_ref):
        gv = g_ref[...]
        mn = ADAM_B1 * m_ref[...] + (1.0 - ADAM_B1) * gv
        vn = ADAM_B2 * v_ref[...] + (1.0 - ADAM_B2) * (gv * gv)
        m_hat = mn / (1.0 - ADAM_B1 ** ADAM_STEP)
        v_hat = vn / (1.0 - ADAM_B2 ** ADAM_STEP)
        d_ref[...] = -ADAM_LR * (m_hat / (jnp.sqrt(v_hat) + ADAM_EPS) + ADAM_WD * w_ref[...])
        nm_ref[...] = mn
        nv_ref[...] = vn

    blk = pl.BlockSpec((tr, C), lambda i: (i, 0))
    return pl.pallas_call(body, grid=(R // tr,), name=name, in_specs=[blk] * 4, out_specs=[blk] * 3,
                          out_shape=[SDS((R, C), F32)] * 3, compiler_params=_cp("parallel"))(w, g, m, v)


BIG = ("w_in", "w_out", "w_xq", "w_xk", "w_xv", "w_xo", "w_up", "w_down")
COL_SHARDED = ("w_in", "w_up")
N_BIG = len(BIG)
ANY = pl.BlockSpec(memory_space=pl.ANY)


def _place():
    x, y, c = lax.axis_index("x"), lax.axis_index("y"), lax.axis_index("c")
    chips = [(1 - x, y), (x, 1 - y), (1 - x, 1 - y)]
    return x, y, c, chips


def _window(full, name, R, C, shard, half):
    r0, nr = (0, R) if half is None else (half * (R // 2), R // 2)
    if name in COL_SHARDED:
        return full.at[pl.ds(r0, nr), pl.ds(shard * C, C)]
    return full.at[pl.ds(shard * R + r0, nr), :]


def place_shard(w, mine_arr, n):
    R, C = w.shape
    col = n in COL_SHARDED

    def body(s_ref, w_ref, o_ref):
        o_ref[...] = w_ref[...].astype(BF16)

    grid_spec = pltpu.PrefetchScalarGridSpec(
        num_scalar_prefetch=1, grid=(1,), in_specs=[pl.BlockSpec((R, C), lambda i, s: (0, 0))],
        out_specs=pl.BlockSpec((R, C), (lambda i, s: (0, s[0])) if col else (lambda i, s: (s[0], 0))))
    return pl.pallas_call(body, grid_spec=grid_spec, name="place_" + n,
                          out_shape=SDS((R, 4 * C) if col else (4 * R, C), BF16),
                          compiler_params=_cp("arbitrary"))(mine_arr, w)


def _gather_jobs(names, shapes):
    nw = len(names)

    def start(full, sems):
        send, recv, fsend, frecv = sems
        x, y, c, chips = _place()
        mine = 2 * x + y
        for w, n in enumerate(names):
            R, C = shapes[w]
            own = _window(full[w], n, R, C, mine, c)
            for j, chip in enumerate(chips):
                pltpu.make_async_remote_copy(src_ref=own, dst_ref=own, send_sem=send.at[w, j], recv_sem=recv.at[w, j],
                                             device_id=(*chip, c), device_id_type=MESH).start()

    def mid(full, sems):
        send, recv, fsend, frecv = sems
        x, y, c, chips = _place()
        sib = (x, y, 1 - c)
        for w, n in enumerate(names):
            R, C = shapes[w]
            for j, chip in enumerate(chips):
                landed = _window(full[w], n, R, C, 2 * chip[0] + chip[1], c)
                pltpu.make_async_remote_copy(src_ref=landed, dst_ref=landed, send_sem=send.at[w, j], recv_sem=recv.at[w, j],
                                             device_id=(*chip, c), device_id_type=MESH).wait_recv()
                pltpu.make_async_remote_copy(src_ref=landed, dst_ref=landed, send_sem=fsend.at[w, j],
                                             recv_sem=frecv.at[w, j], device_id=sib, device_id_type=MESH).start()

    def finish(full, sems):
        send, recv, fsend, frecv = sems
        x, y, c, chips = _place()
        mine = 2 * x + y
        sib = (x, y, 1 - c)
        for w, n in enumerate(names):
            R, C = shapes[w]
            own = _window(full[w], n, R, C, mine, c)
            for j, chip in enumerate(chips):
                landed = _window(full[w], n, R, C, 2 * chip[0] + chip[1], c)
                other = _window(full[w], n, R, C, 2 * chip[0] + chip[1], 1 - c)
                pltpu.make_async_remote_copy(src_ref=other, dst_ref=other, send_sem=fsend.at[w, j], recv_sem=frecv.at[w, j],
                                             device_id=sib, device_id_type=MESH).wait_recv()
                pltpu.make_async_remote_copy(src_ref=own, dst_ref=own, send_sem=send.at[w, j], recv_sem=recv.at[w, j],
                                             device_id=(*chip, c), device_id_type=MESH).wait_send()
                pltpu.make_async_remote_copy(src_ref=landed, dst_ref=landed, send_sem=fsend.at[w, j],
                                             recv_sem=frecv.at[w, j], device_id=sib, device_id_type=MESH).wait_send()

    return start, mid, finish, [pltpu.SemaphoreType.DMA((nw, 3))] * 4


class CommJob:
    def __init__(self, ins, out_shapes, inplace, start, finish, sems, mid=None):
        self.ins, self.out_shapes, self.inplace = list(ins), list(out_shapes), inplace
        self.start, self.mid, self.finish, self.sems = start, mid, finish, list(sems)

    def run(self, phase, ins, outs, sems):
        if phase == 0:
            self.start(ins, outs, sems)
        elif phase == 1:
            if self.mid is not None:
                self.mid(ins, outs, sems)
        else:
            self.finish(ins, outs, sems)


def run_job(name, job):
    n_in, n_out = len(job.ins), len(job.out_shapes)

    def body(*refs):
        ins, outs, sm = refs[:n_in], refs[n_in:n_in + n_out], refs[n_in + n_out:]
        for phase in range(3):
            job.run(phase, ins, outs, sm)

    return pl.pallas_call(
        body, name=name, in_specs=[ANY] * n_in, out_specs=[ANY] * n_out, out_shape=job.out_shapes,
        input_output_aliases={i: i for i in range(n_in)} if job.inplace else {},
        scratch_shapes=job.sems)(*job.ins)


class Carried:
    def __init__(self, job, n_in, n_out, n_scratch, steps):
        self.job, self.n_in, self.n_out, self.n_scratch, self.steps = job, n_in, n_out, n_scratch, steps
        self.nji = len(job.ins) if job else 0
        self.njo = len(job.out_shapes) if job else 0

    def in_specs(self):
        return [ANY] * self.nji

    def out_specs(self):
        return [ANY] * self.njo

    def out_shapes(self):
        return list(self.job.out_shapes) if self.job else []

    def aliases(self):
        if not (self.job and self.job.inplace):
            return {}
        return {self.n_in + i: self.n_out + i for i in range(self.nji)}

    def sems(self):
        return list(self.job.sems) if self.job else []

    def args(self):
        return list(self.job.ins) if self.job else []

    def split(self, refs):
        a = self.n_in
        b = a + self.nji
        c = b + self.n_out
        d = c + self.njo
        e = d + self.n_scratch
        return refs[:a] + refs[b:c] + refs[d:e], (refs[a:b], refs[c:d], refs[e:])

    def phase(self, phase, step, jrefs):
        if self.job is None:
            return
        at = {0: 0, 1: max(self.steps - 2, 0), 2: self.steps - 1}[phase]

        @pl.when(step == at)
        def _():
            self.job.run(phase, *jrefs)


def gather_job(placed, names):
    shapes = []
    for n in names:
        R, C = placed[n].shape
        shapes.append((R, C // 4) if n in COL_SHARDED else (R // 4, C))
    start, mid, finish, sems = _gather_jobs(names, shapes)
    arrays = [placed[n] for n in names]
    return CommJob(arrays, [SDS(a.shape, a.dtype) for a in arrays], True,
                   lambda i, o, s: start(o, s), lambda i, o, s: finish(o, s), sems, mid=lambda i, o, s: mid(o, s))


def pair_exchange_job(grads, names):
    shapes = [grads[n].shape for n in names]

    def copies(ins, outs, sems):
        x, y, c, _ = _place()
        return [pltpu.make_async_remote_copy(
            src_ref=ins[w].at[:, pl.ds((1 - c) * (shapes[w][1] // 2), shapes[w][1] // 2), :], dst_ref=outs[w],
            send_sem=sems[0].at[w], recv_sem=sems[1].at[w], device_id=(x, y, 1 - c), device_id_type=MESH)
            for w in range(len(names))]

    def start(ins, outs, sems):
        for cp in copies(ins, outs, sems):
            cp.start()

    def finish(ins, outs, sems):
        for cp in copies(ins, outs, sems):
            cp.wait()

    return CommJob([grads[n] for n in names], [SDS((4, s[1] // 2, s[2]), F32) for s in shapes], False, start, finish,
                   [pltpu.SemaphoreType.DMA((len(names),))] * 2)


def chip_exchange_job(p16, names):
    shapes = [p16[n].shape for n in names]

    def copies(ins, outs, sems):
        x, y, c, chips = _place()
        return [pltpu.make_async_remote_copy(
            src_ref=ins[w].at[2 * chip[0] + chip[1]], dst_ref=outs[w].at[j],
            send_sem=sems[0].at[w, j], recv_sem=sems[1].at[w, j], device_id=(*chip, c), device_id_type=MESH)
            for w in range(len(names)) for j, chip in enumerate(chips)]

    def start(ins, outs, sems):
        for cp in copies(ins, outs, sems):
            cp.start()

    def finish(ins, outs, sems):
        for cp in copies(ins, outs, sems):
            cp.wait()

    return CommJob([p16[n] for n in names], [SDS((3,) + tuple(s[1:]), BF16) for s in shapes], False, start, finish,
                   [pltpu.SemaphoreType.DMA((len(names), 3))] * 2)


def half_exchange_job(full, names):
    shapes = [full[n].shape for n in names]

    def copies(outs, sems):
        x, y, c, _ = _place()
        cps = []
        for w in range(len(names)):
            Rh = shapes[w][0] // 2
            rows = outs[w].at[pl.ds(c * Rh, Rh), :]
            other = outs[w].at[pl.ds((1 - c) * Rh, Rh), :]
            cps.append((pltpu.make_async_remote_copy(src_ref=rows, dst_ref=rows, send_sem=sems[0].at[w], recv_sem=sems[1].at[w],
                                                     device_id=(x, y, 1 - c), device_id_type=MESH),
                        pltpu.make_async_remote_copy(src_ref=other, dst_ref=other, send_sem=sems[0].at[w], recv_sem=sems[1].at[w],
                                                     device_id=(x, y, 1 - c), device_id_type=MESH)))
        return cps

    def start(ins, outs, sems):
        for send, _ in copies(outs, sems):
            send.start()

    def finish(ins, outs, sems):
        for send, recv in copies(outs, sems):
            recv.wait_recv()
            send.wait_send()

    arrays = [full[n] for n in names]
    return CommJob(arrays, [SDS(a.shape, a.dtype) for a in arrays], True, start, finish,
                   [pltpu.SemaphoreType.DMA((len(names),))] * 2)


def allreduce_small(v, name):
    rows = v.shape[0]

    def body(v_ref, o_ref, buf, send, recv):
        x, y, c, _ = _place()
        me = 4 * x + 2 * y + c
        buf[0] = v_ref[...]
        sends = []
        for k in range(1, 8):
            peer = (x ^ (k >> 2), y ^ ((k >> 1) & 1), c ^ (k & 1))
            cp = pltpu.make_async_remote_copy(src_ref=v_ref, dst_ref=buf.at[k], send_sem=send.at[k - 1], recv_sem=recv.at[k - 1],
                                              device_id=peer, device_id_type=MESH)
            cp.start()
            sends.append(cp)
        for cp in sends:
            cp.wait_recv()
        acc = buf[me]
        for d in range(1, 8):
            acc = acc + buf[d ^ me]
        o_ref[...] = acc
        for cp in sends:
            cp.wait_send()

    vm = pl.BlockSpec(memory_space=pltpu.VMEM)
    return pl.pallas_call(
        body, name=name, in_specs=[vm], out_specs=vm, out_shape=SDS((rows, 128), F32),
        scratch_shapes=[pltpu.VMEM((8, rows, 128), F32), pltpu.SemaphoreType.DMA((7,)), pltpu.SemaphoreType.DMA((7,))],
    )(v)


def pair_sum(g, xh, c_arr, name):
    _, R, C = g.shape
    Rh = R // 2

    def body(c_ref, g_ref, x_ref, p32_ref, p16_ref):
        s = g_ref[...] + x_ref[...]
        p32_ref[...] = s
        p16_ref[...] = s.astype(BF16)

    same = pl.BlockSpec((None, Rh, C), lambda s, c: (s, 0, 0))
    grid_spec = pltpu.PrefetchScalarGridSpec(
        num_scalar_prefetch=1, grid=(4,),
        in_specs=[pl.BlockSpec((None, Rh, C), lambda s, c: (s, c[0], 0)), same], out_specs=[same, same])
    return pl.pallas_call(body, grid_spec=grid_spec, name=name, out_shape=[SDS((4, Rh, C), F32), SDS((4, Rh, C), BF16)],
                          compiler_params=_cp("parallel"))(c_arr, g, xh)


def chip_sum(p32, y, sel_arr, name):
    _, Rh, C = p32.shape

    def body(s_ref, p_ref, y_ref, o_ref):
        o_ref[...] = ((p_ref[...] + y_ref[0].astype(F32)) + y_ref[1].astype(F32)) + y_ref[2].astype(F32)

    grid_spec = pltpu.PrefetchScalarGridSpec(
        num_scalar_prefetch=1, grid=(1,),
        in_specs=[pl.BlockSpec((None, Rh, C), lambda i, s: (s[0], 0, 0)), pl.BlockSpec((3, Rh, C), lambda i, s: (0, 0, 0))],
        out_specs=pl.BlockSpec((Rh, C), lambda i, s: (s[1], 0)))
    return pl.pallas_call(body, grid_spec=grid_spec, name=name, out_shape=SDS((2 * Rh, C), F32),
                          compiler_params=_cp("arbitrary"))(sel_arr, p32, y)


class StepComm:
    FIRST = ("w_in",)
    REST = tuple(n for n in BIG if n != "w_in")

    def __init__(self, placed, c_arr, sel_arr):
        self.placed, self.c_arr, self.sel_arr = placed, c_arr, sel_arr
        self.full, self.p32 = {}, {}

    def gather_first(self):
        return dict(zip(self.FIRST, run_job("gather_first", gather_job(self.placed, self.FIRST))))

    def gather_rest_job(self):
        return gather_job(self.placed, self.REST)

    def gathered_rest(self, landed):
        return dict(zip(self.REST, landed))

    def _pair(self, gb, names, tag):
        g = {n: (gb[n] if gb[n].ndim == 3 else gb[n].reshape(4, gb[n].shape[0] // 4, gb[n].shape[1])) for n in names}
        xh = dict(zip(names, run_job("pair_exchange_" + tag, pair_exchange_job(g, names))))
        p16 = {}
        for n in names:
            self.p32[n], p16[n] = pair_sum(g[n], xh[n], self.c_arr, "pair_sum_" + n)
        return p16

    def _chip_sums(self, names, landed):
        for n, y in zip(names, landed):
            self.full[n] = chip_sum(self.p32[n], y, self.sel_arr, "chip_sum_" + n)

    def reduce_early_job(self, gb):
        return chip_exchange_job(self._pair(gb, self.REST, "a"), self.REST)

    def reduce_early_landed(self, landed):
        self._chip_sums(self.REST, landed)

    def reduce_finish(self, gb):
        p16 = self._pair(gb, self.FIRST, "b")
        self._chip_sums(self.FIRST, run_job("chip_exchange_b", chip_exchange_job(p16, self.FIRST)))
        return dict(zip(BIG, run_job("half_exchange", half_exchange_job(self.full, BIG))))


SMALL = ("rel_bias", "g_mix", "w_short_conv", "g_attn_out", "g_conv_out", "g_xattn", "g_mem", "g_ffn",
         "w_ffn_conv", "b_ffn_conv", "g_final")
SMALL_FULL = {"rel_bias": (8, 32), "g_mix": (1, 1024), "w_short_conv": (3, 512), "g_attn_out": (1, 512), "g_conv_out": (1, 512),
              "g_xattn": (1, 1024), "g_mem": (1, 1024), "g_ffn": (1, 1024), "w_ffn_conv": (3, 5632), "b_ffn_conv": (1, 5632),
              "g_final": (1024,)}
SMALL_SHARDED = ("w_short_conv", "w_ffn_conv")


def _pack(parts):
    flat = jnp.concatenate([p.reshape(-1) for p in parts])
    rows = -(-flat.shape[0] // 1024) * 8
    return jnp.pad(flat, (0, rows * 128 - flat.shape[0])).reshape(rows, 128)


def _unpack(packed, names):
    flat, out, off = packed.reshape(-1), {}, 0
    for n in names:
        size = int(np.prod(SMALL_FULL[n]))
        out[n] = flat[off:off + size].reshape(SMALL_FULL[n])
        off += size
    return out


def kernel(x, mem, rel_bias, g_mix, w_in, w_short_conv, g_attn_out, g_conv_out, w_out, g_xattn, g_mem, w_xq, w_xk, w_xv, w_xo, g_ffn, w_up, w_ffn_conv, b_ffn_conv, w_down, g_final, loss_target, m_rel_bias, m_g_mix, m_w_in, m_w_short_conv, m_g_attn_out, m_g_conv_out, m_w_out, m_g_xattn, m_g_mem, m_w_xq, m_w_xk, m_w_xv, m_w_xo, m_g_ffn, m_w_up, m_w_ffn_conv, m_b_ffn_conv, m_w_down, m_g_final, v_rel_bias, v_g_mix, v_w_in, v_w_short_conv, v_g_attn_out, v_g_conv_out, v_w_out, v_g_xattn, v_g_mem, v_w_xq, v_w_xk, v_w_xv, v_w_xo, v_g_ffn, v_w_up, v_w_ffn_conv, v_b_ffn_conv, v_w_down, v_g_final):
    names = ("rel_bias", "g_mix", "w_in", "w_short_conv", "g_attn_out", "g_conv_out", "w_out", "g_xattn", "g_mem", "w_xq",
             "w_xk", "w_xv", "w_xo", "g_ffn", "w_up", "w_ffn_conv", "b_ffn_conv", "w_down", "g_final")
    W = dict(zip(names, (rel_bias, g_mix, w_in, w_short_conv, g_attn_out, g_conv_out, w_out, g_xattn, g_mem, w_xq, w_xk, w_xv,
                         w_xo, g_ffn, w_up, w_ffn_conv, b_ffn_conv, w_down, g_final)))
    M = dict(zip(names, (m_rel_bias, m_g_mix, m_w_in, m_w_short_conv, m_g_attn_out, m_g_conv_out, m_w_out, m_g_xattn, m_g_mem,
                         m_w_xq, m_w_xk, m_w_xv, m_w_xo, m_g_ffn, m_w_up, m_w_ffn_conv, m_b_ffn_conv, m_w_down, m_g_final)))
    V = dict(zip(names, (v_rel_bias, v_g_mix, v_w_in, v_w_short_conv, v_g_attn_out, v_g_conv_out, v_w_out, v_g_xattn, v_g_mem,
                         v_w_xq, v_w_xk, v_w_xv, v_w_xo, v_g_ffn, v_w_up, v_w_ffn_conv, v_b_ffn_conv, v_w_down, v_g_final)))
    xi, yi, ci = lax.axis_index("x"), lax.axis_index("y"), lax.axis_index("c")
    mine = 2 * xi + yi
    c_arr = jnp.reshape(ci, (1,)).astype(jnp.int32)
    mine_arr = jnp.reshape(mine, (1,)).astype(jnp.int32)

    sel_arr = jnp.stack([mine, ci]).astype(jnp.int32)
    comm = StepComm({n: place_shard(W[n][0], mine_arr, n) for n in BIG}, c_arr, sel_arr)
    first = jnp.where(ci == 0, 1.0, 0.0).astype(F32)
    emb = []
    for n in SMALL_SHARDED:
        shard = W[n][0]
        full = jnp.zeros(SMALL_FULL[n], F32)
        emb.append(lax.dynamic_update_slice(full, shard * first, (0, mine * shard.shape[1])))
    conv_w = allreduce_small(_pack(emb), "gather_small")
    conv_w = _unpack(conv_w, SMALL_SHARDED)
    ws = {n: W[n] for n in SMALL if n not in SMALL_SHARDED}
    ws.update(conv_w)

    loss, grad_x, gfull, gs = local_step(x[0], mem[0], loss_target[0], None, ws, comm)
    loss = lax.psum(loss[0, 0], ("x", "y", "c"))

    gsm = _unpack(allreduce_small(_pack([gs[n] for n in SMALL]), "reduce_small"), SMALL)

    grads, delta, new_m, new_v = {}, {}, {}, {}
    for n in BIG:
        grads[n] = gfull[n][None]
        d, nm, nv = adamw(W[n][0], gfull[n], M[n][0], V[n][0], "adamw_" + n)
        delta[n], new_m[n], new_v[n] = d[None], nm[None], nv[None]
    for n in SMALL_SHARDED:
        wid = W[n].shape[2]
        g = lax.dynamic_slice(gsm[n], (0, mine * wid), (3, wid))
        grads[n] = g[None]
        d, nm, nv = adamw(W[n][0], g, M[n][0], V[n][0], "adamw_" + n)
        delta[n], new_m[n], new_v[n] = d[None], nm[None], nv[None]
    rep = [n for n in SMALL if n not in SMALL_SHARDED]
    d, nm, nv = adamw(_pack([W[n] for n in rep]), _pack([gsm[n] for n in rep]), _pack([M[n] for n in rep]),
                      _pack([V[n] for n in rep]), "adamw_small")

    def unpack_rep(p):
        flat, out, off = p.reshape(-1), {}, 0
        for n in rep:
            size = int(np.prod(SMALL_FULL[n]))
            out[n] = flat[off:off + size].reshape(SMALL_FULL[n])
            off += size
        return out

    d, nm, nv = unpack_rep(d), unpack_rep(nm), unpack_rep(nv)
    for n in rep:
        grads[n], delta[n], new_m[n], new_v[n] = gsm[n], d[n], nm[n], nv[n]
    return (loss, grad_x[None], *[grads[n] for n in names], *[delta[n] for n in names],
            *[new_m[n] for n in names], *[new_v[n] for n in names])
```

```python
import functools
import math

import numpy as np
import jax
import jax.numpy as jnp
from jax import lax
from jax.experimental import pallas as pl
from jax.experimental.pallas import tpu as pltpu

F32 = jnp.float32
BF16 = jnp.bfloat16
SDS = jax.ShapeDtypeStruct
MESH = pl.DeviceIdType.MESH

D_MODEL = 1024
ATTN_W = 512
N_HEADS = 8
BLK = 128
PATTERNS = ((128, 1), (512, 4), (2048, 16))
N_BUCKETS = 32
D_FF = 2816
N_MEM_HEADS = 4
MEM_HD = 256
EPS = 1e-6
NEG = -1e30
VMEM_LIMIT = 56 * 1024 * 1024

ADAM_LR, ADAM_B1, ADAM_B2, ADAM_EPS, ADAM_WD, ADAM_STEP = 0.001, 0.9, 0.999, 1e-08, 0.01, 10


def _cp(*sem):
    return pltpu.CompilerParams(dimension_semantics=sem, vmem_limit_bytes=VMEM_LIMIT)


def _dot(a, b):
    return jnp.dot(a, b, preferred_element_type=F32)


def _dot_nt(a, b):
    return lax.dot_general(a, b, (((1,), (1,)), ((), ())), preferred_element_type=F32)


def _dot_tn(a, b):
    return lax.dot_general(a, b, (((0,), (0,)), ((), ())), preferred_element_type=F32)


def _rsum(x):
    return jnp.sum(x, axis=1, keepdims=True)


def rmsnorm_fwd(x, g, name):
    S, Dm = x.shape
    tm = min(S, 512)

    def body(x_ref, g_ref, o_ref):
        xv = x_ref[...]
        r = lax.rsqrt(jnp.mean(xv * xv, axis=1, keepdims=True) + EPS)
        o_ref[...] = (xv * r * g_ref[...]).astype(o_ref.dtype)

    return pl.pallas_call(
        body, grid=(S // tm,), name=name,
        in_specs=[pl.BlockSpec((tm, Dm), lambda i: (i, 0)), pl.BlockSpec((1, Dm), lambda i: (0, 0))],
        out_specs=pl.BlockSpec((tm, Dm), lambda i: (i, 0)),
        out_shape=SDS((S, Dm), BF16), compiler_params=_cp("parallel"))(x, g)


def rmsnorm_bwd(x, g, dh, res, name, job=None):
    S, Dm = x.shape
    tm = min(S, 512)
    want_dx = res is not None

    def body(*refs):
        if want_dx:
            x_ref, g_ref, dh_ref, res_ref, dx_ref, dg_ref = refs
        else:
            x_ref, g_ref, dh_ref, dg_ref = refs
        i = pl.program_id(0)
        xv = x_ref[...]
        r = lax.rsqrt(jnp.mean(xv * xv, axis=1, keepdims=True) + EPS)
        xh = xv * r
        dh = dh_ref[...].astype(F32)
        if want_dx:
            gd = dh * g_ref[...]
            dx_ref[...] = res_ref[...] + r * (gd - xh * jnp.mean(gd * xh, axis=1, keepdims=True))

        @pl.when(i == 0)
        def _():
            dg_ref[...] = jnp.zeros_like(dg_ref)

        dg_ref[...] += jnp.sum(dh * xh, axis=0, keepdims=True)

    row = pl.BlockSpec((tm, Dm), lambda i: (i, 0))
    vec = pl.BlockSpec((1, Dm), lambda i: (0, 0))
    if want_dx:
        return _pcall(body, [x, g, dh, res], grid=(S // tm,), name=name, in_specs=[row, vec, row, row], out_specs=[row, vec],
                      out_shape=[SDS((S, Dm), F32), SDS((1, Dm), F32)], sem=("arbitrary",), job=job)
    return pl.pallas_call(
        body, grid=(S // tm,), name=name, in_specs=[row, vec, row], out_specs=vec,
        out_shape=SDS((1, Dm), F32), compiler_params=_cp("arbitrary"))(x, g, dh)


def _col_spec(bw, tn, rows, row_of, col_of):
    if bw is None:
        return pl.BlockSpec((rows, tn), lambda *g: (row_of(*g), col_of(*g)))
    if tn % bw == 0:
        return pl.BlockSpec((tn // bw, rows, bw), lambda *g: (col_of(*g), row_of(*g), 0))
    per = bw // tn
    return pl.BlockSpec((None, rows, tn), lambda *g: (col_of(*g) // per, row_of(*g), col_of(*g) % per))


def _read_cols(ref, bw, tn):
    if bw is None or tn % bw != 0:
        return ref[...]
    if tn == bw:
        return ref[0]
    return jnp.concatenate([ref[q] for q in range(tn // bw)], axis=1)


def _write_cols(ref, bw, tn, val):
    if bw is None or tn % bw != 0:
        ref[...] = val.astype(ref.dtype)
    else:
        for q in range(tn // bw):
            ref[q] = val[:, q * bw:(q + 1) * bw].astype(ref.dtype)


def mm_nn(a, b, name, *, res=None, out_dtype=F32, out_bw=None, tm=1024, tn=512, job=None):
    M, K = a.shape
    N = b.shape[1]
    tm = min(tm, M)

    def body(*refs):
        if res is None:
            a_ref, b_ref, o_ref = refs
        else:
            a_ref, b_ref, r_ref, o_ref = refs
        acc = _dot(a_ref[...].astype(BF16), b_ref[...])
        if res is not None:
            acc = acc + r_ref[...]
        _write_cols(o_ref, out_bw, tn, acc)

    ri, ci = (lambda i, j: i), (lambda i, j: j)
    in_specs = [pl.BlockSpec((tm, K), lambda i, j: (i, 0)), pl.BlockSpec((K, tn), lambda i, j: (0, j))]
    args = [a, b]
    if res is not None:
        in_specs.append(pl.BlockSpec((tm, tn), lambda i, j: (i, j)))
        args.append(res)
    oshape = (M, N) if out_bw is None else (N // out_bw, M, out_bw)
    return _pcall(body, args, grid=(M // tm, N // tn), name=name, in_specs=in_specs,
                  out_specs=[_col_spec(out_bw, tn, tm, ri, ci)], out_shape=[SDS(oshape, out_dtype)],
                  sem=("parallel", "parallel"), job=job)[0]


def mm_nt(a, a_bw, b, name, *, res=None, out_dtype=F32, tm=1024, tn=512, job=None):
    if a_bw is None:
        M, K = a.shape
    else:
        M, K = a.shape[1], a.shape[0] * a_bw
    N = b.shape[0]
    tm = min(tm, M)

    def body(*refs):
        if res is None:
            a_ref, b_ref, o_ref = refs
        else:
            a_ref, b_ref, r_ref, o_ref = refs
        av = _read_cols(a_ref, a_bw, K).astype(BF16)
        acc = _dot_nt(av, b_ref[...])
        if res is not None:
            acc = acc + r_ref[...]
        o_ref[...] = acc.astype(o_ref.dtype)

    in_specs = [_col_spec(a_bw, K, tm, lambda i, j: i, lambda i, j: 0), pl.BlockSpec((tn, K), lambda i, j: (j, 0))]
    args = [a, b]
    if res is not None:
        in_specs.append(pl.BlockSpec((tm, tn), lambda i, j: (i, j)))
        args.append(res)
    return _pcall(body, args, grid=(M // tm, N // tn), name=name, in_specs=in_specs,
                  out_specs=[pl.BlockSpec((tm, tn), lambda i, j: (i, j))], out_shape=[SDS((M, N), out_dtype)],
                  sem=("parallel", "parallel"), job=job)[0]


def mm_tn(a, b, b_bw, name, *, shards=None, tm=1024, tn=512, ts=512, job=None):
    S, Ka = a.shape
    N = b.shape[1] if b_bw is None else b.shape[0] * b_bw
    ts = min(ts, S)
    tm = min(tm, Ka)

    def body(a_ref, b_ref, o_ref):
        @pl.when(pl.program_id(2) == 0)
        def _():
            o_ref[...] = jnp.zeros_like(o_ref)

        bv = _read_cols(b_ref, b_bw, tn).astype(BF16)
        o_ref[...] += _dot_tn(a_ref[...].astype(BF16), bv)

    in_specs = [pl.BlockSpec((ts, tm), lambda i, j, k: (k, i)),
                _col_spec(b_bw, tn, ts, lambda i, j, k: k, lambda i, j, k: j)]
    if shards is None:
        out_spec, oshape = pl.BlockSpec((tm, tn), lambda i, j, k: (i, j)), (Ka, N)
    else:
        per = (N // shards) // tn
        out_spec = pl.BlockSpec((None, tm, tn), lambda i, j, k: (j // per, i, j % per))
        oshape = (shards, Ka, N // shards)
    return _pcall(body, [a, b], grid=(Ka // tm, N // tn, S // ts), name=name, in_specs=in_specs, out_specs=[out_spec],
                  out_shape=[SDS(oshape, F32)], sem=("parallel", "parallel", "arbitrary"), job=job)[0]


def _bucket_tables():
    out = np.zeros((3, 2, BLK, BLK), np.int32)
    qi = np.arange(BLK)[:, None]
    for p, (win, dil) in enumerate(PATTERNS):
        w = win // dil
        assert w == BLK
        for half in range(2):
            kj = np.arange(BLK)[None, :] + half * BLK
            steps = qi + w - kj
            valid = (steps >= 0) & (steps <= w)
            dist = np.clip(steps, 0, w) * dil
            dd = np.maximum(dist, 1).astype(np.float32)
            large = 16 + (np.log(dd / np.float32(16)) / np.float32(math.log(2048 / 16)) * np.float32(16)).astype(np.int32)
            large = np.minimum(large, N_BUCKETS - 1)
            out[p, half] = np.where(valid, np.where(dist < 16, dist, large), -1)
    return out


def bias_tables(rel_bias):
    bkt = jnp.asarray(_bucket_tables())

    def body(rb_ref, bkt_ref, o_ref):
        h = pl.program_id(1)
        for half in range(2):
            bk = bkt_ref[half]
            acc = jnp.full((BLK, BLK), NEG, F32)
            for b in range(N_BUCKETS):
                acc = jnp.where(bk == b, rb_ref[h, b], acc)
            o_ref[half] = acc

    return pl.pallas_call(
        body, grid=(3, N_HEADS), name="bias_tables",
        in_specs=[pl.BlockSpec(memory_space=pltpu.SMEM),
                  pl.BlockSpec((None, 2, BLK, BLK), lambda p, h: (p, 0, 0, 0))],
        out_specs=pl.BlockSpec((None, None, 2, BLK, BLK), lambda p, h: (p, h, 0, 0, 0)),
        out_shape=SDS((3, N_HEADS, 2, BLK, BLK), F32), compiler_params=_cp("parallel", "parallel"))(rel_bias, bkt)


def bias_tables_bwd(ds_sum):
    bkt = jnp.asarray(_bucket_tables())

    def body(ds_ref, bkt_ref, o_ref):
        h = pl.program_id(0)

        @pl.when(h == 0)
        def _():
            o_ref[...] = jnp.zeros_like(o_ref)

        rows = lax.broadcasted_iota(jnp.int32, (N_HEADS, N_BUCKETS), 0)
        cols = lax.broadcasted_iota(jnp.int32, (N_HEADS, N_BUCKETS), 1)
        acc = jnp.zeros((N_HEADS, N_BUCKETS), F32)
        for b in range(N_BUCKETS):
            tot = jnp.zeros((1, 1), F32)
            for p in range(3):
                for half in range(2):
                    t = jnp.where(bkt_ref[p, half] == b, ds_ref[p, half], 0.0)
                    tot = tot + jnp.sum(_rsum(t), axis=0, keepdims=True)
            acc = acc + jnp.where((rows == h) & (cols == b), tot, 0.0)
        o_ref[...] += acc

    return pl.pallas_call(
        body, grid=(N_HEADS,), name="bias_tables_bwd",
        in_specs=[pl.BlockSpec((3, None, 2, BLK, BLK), lambda h: (0, h, 0, 0, 0)),
                  pl.BlockSpec((3, 2, BLK, BLK), lambda h: (0, 0, 0, 0))],
        out_specs=pl.BlockSpec((N_HEADS, N_BUCKETS), lambda h: (0, 0)),
        out_shape=SDS((N_HEADS, N_BUCKETS), F32), compiler_params=_cp("arbitrary"))(ds_sum, bkt)


def _rows(start, dil):
    if dil == 1:
        return pl.ds(pl.multiple_of(start, BLK), BLK)
    return pl.ds(start, BLK, stride=dil)


GRP = 4


def _group_rows(i, dil):
    t = lax.broadcasted_iota(jnp.int32, (GRP, 1, 1), 0)
    if dil == 1:
        first = i * GRP
        cur = [(first + j) * BLK for j in range(GRP)]
        prv = [jnp.maximum(first + j - 1, 0) * BLK for j in range(GRP)]
        pen = jnp.where(first + t > 0, 0.0, NEG)
    else:
        per_b = dil // GRP
        b, r0 = i // per_b, (i % per_b) * GRP
        cur = [b * (BLK * dil) + r0 + j for j in range(GRP)]
        prv = [jnp.maximum(b - 1, 0) * (BLK * dil) + r0 + j for j in range(GRP)]
        pen = jnp.where(b + 0 * t > 0, 0.0, NEG)
    return [_rows(s, dil) for s in cur], [_rows(s, dil) for s in prv], pen


def _load_group(ref, rows):
    return jnp.stack([ref[r, :] for r in rows])


def _bdot_nt(a, b):
    return lax.dot_general(a, b, (((2,), (2,)), ((0,), (0,))), preferred_element_type=F32)


def _bdot(a, b):
    return lax.dot_general(a, b, (((2,), (1,)), ((0,), (0,))), preferred_element_type=F32)


def _bdot_tn(a, b):
    return lax.dot_general(a, b, (((1,), (1,)), ((0,), (0,))), preferred_element_type=F32)


def _lsum(x):
    return jnp.sum(x, axis=-1, keepdims=True)


def attn_fwd(qkv, bias, job=None):
    S = qkv.shape[3]
    nblk = S // BLK
    car = Carried(job, 2, 2, 1, 4)

    def body(*refs):
        (qkv_ref, bias_ref, o_ref, lse_ref, l_ref), jrefs = car.split(refs)
        car.phase(0, pl.program_id(0), jrefs)
        lane = lax.broadcasted_iota(jnp.int32, (GRP, BLK, BLK), 2)
        lo = lane < 64
        masks = (lo, jnp.logical_not(lo))
        q_ref, k_ref, v_ref = qkv_ref.at[0], qkv_ref.at[1], qkv_ref.at[2]
        for p, (_, dil) in enumerate(PATTERNS):
            def step(i, carry, p=p, dil=dil):
                rc, rp, pen = _group_rows(i, dil)
                q2 = _load_group(q_ref, rc) * 0.125
                kc = _load_group(k_ref, rc).astype(BF16)
                kp = _load_group(k_ref, rp).astype(BF16)
                vc = _load_group(v_ref, rc)
                vp = _load_group(v_ref, rp)
                if p > 0:
                    acc_old, m_old, l_old = _load_group(o_ref, rc), _load_group(lse_ref, rc), _load_group(l_ref, rc)
                pv = jnp.zeros((GRP, BLK, BLK), F32)
                m_new, l_new, alpha = [], [], []
                for h in range(2):
                    qh = jnp.where(masks[h], q2, 0.0).astype(BF16)
                    sp = _bdot_nt(qh, kp) + bias_ref[p, h, 0][None] + pen
                    sc = _bdot_nt(qh, kc) + bias_ref[p, h, 1][None]
                    mn = jnp.max(jnp.maximum(sp, sc), axis=-1, keepdims=True)
                    if p > 0:
                        mo = m_old[:, :, 64 * h:64 * h + 1]
                        mn = jnp.maximum(mn, mo)
                        al = jnp.exp(mo - mn)
                        alpha.append(al)
                    pp = jnp.exp(sp - mn)
                    pc = jnp.exp(sc - mn)
                    ln = _lsum(pp + pc)
                    if p > 0:
                        ln = ln + al * l_old[:, :, 64 * h:64 * h + 1]
                    vhp = jnp.where(masks[h], vp, 0.0).astype(BF16)
                    vhc = jnp.where(masks[h], vc, 0.0).astype(BF16)
                    pv = pv + _bdot(pp.astype(BF16), vhp) + _bdot(pc.astype(BF16), vhc)
                    m_new.append(mn)
                    l_new.append(ln)
                if p > 0:
                    pv = pv + acc_old * jnp.where(lo, alpha[0], alpha[1])
                m_t = jnp.where(lo, m_new[0], m_new[1])
                l_t = jnp.where(lo, l_new[0], l_new[1])
                for j, r in enumerate(rc):
                    o_ref[r, :] = pv[j]
                    lse_ref[r, :] = m_t[j]
                    l_ref[r, :] = l_t[j]
                return carry

            lax.fori_loop(0, nblk // GRP, step, 0)

        def fin(i, carry):
            rows = pl.ds(pl.multiple_of(i * 512, 512), 512)
            lv = l_ref[rows, :]
            o_ref[rows, :] = o_ref[rows, :] / lv
            lse_ref[rows, :] = lse_ref[rows, :] + jnp.log(lv)
            return carry

        lax.fori_loop(0, S // 512, fin, 0)
        car.phase(1, pl.program_id(0), jrefs)
        car.phase(2, pl.program_id(0), jrefs)

    outs = pl.pallas_call(
        body, grid=(4,), name="attn_fwd",
        in_specs=[pl.BlockSpec((None, 3, None, S, BLK), lambda g: (0, 0, g, 0, 0)),
                  pl.BlockSpec((3, 2, 2, BLK, BLK), lambda g: (0, g, 0, 0, 0))] + car.in_specs(),
        out_specs=[pl.BlockSpec((None, S, BLK), lambda g: (g, 0, 0)),
                   pl.BlockSpec((None, S, BLK), lambda g: (g, 0, 0))] + car.out_specs(),
        out_shape=[SDS((4, S, BLK), F32), SDS((4, S, BLK), F32)] + car.out_shapes(),
        input_output_aliases=car.aliases(),
        scratch_shapes=[pltpu.VMEM((S, BLK), F32)] + car.sems(), compiler_params=_cp("arbitrary"))(qkv, bias, *car.args())
    if job is not None:
        job.landed = list(outs[2:])
    return outs[0], outs[1]


def attn_bwd(qkv, bias, o, lse, do, job=None):
    S = qkv.shape[3]
    nblk = S // BLK
    car = Carried(job, 5, 2, 0, 4)

    def body(*refs):
        (qkv_ref, bias_ref, o_ref, lse_ref, do_ref, dqkv_ref, ds_ref), jrefs = car.split(refs)
        car.phase(0, pl.program_id(0), jrefs)
        lane = lax.broadcasted_iota(jnp.int32, (GRP, BLK, BLK), 2)
        lo = lane < 64
        masks = (lo, jnp.logical_not(lo))
        dqkv_ref[...] = jnp.zeros_like(dqkv_ref)
        ds_ref[...] = jnp.zeros_like(ds_ref)
        q_ref, k_ref, v_ref = qkv_ref.at[0], qkv_ref.at[1], qkv_ref.at[2]
        dq_ref, dk_ref, dv_ref = dqkv_ref.at[0], dqkv_ref.at[1], dqkv_ref.at[2]
        for p, (_, dil) in enumerate(PATTERNS):
            def step(i, carry, p=p, dil=dil):
                rc, rp, pen = _group_rows(i, dil)
                q2 = _load_group(q_ref, rc) * 0.125
                kc = _load_group(k_ref, rc).astype(BF16)
                kp = _load_group(k_ref, rp).astype(BF16)
                vc = _load_group(v_ref, rc).astype(BF16)
                vp = _load_group(v_ref, rp).astype(BF16)
                dot = _load_group(do_ref, rc)
                lset = _load_group(lse_ref, rc)
                prod = dot * _load_group(o_ref, rc)
                zero = jnp.zeros((GRP, BLK, BLK), F32)
                dq, dkc, dkp, dvc, dvp = zero, zero, zero, zero, zero
                for h in range(2):
                    qh = jnp.where(masks[h], q2, 0.0).astype(BF16)
                    doh = jnp.where(masks[h], dot, 0.0).astype(BF16)
                    delta = _lsum(jnp.where(masks[h], prod, 0.0))
                    lh = lset[:, :, 64 * h:64 * h + 1]
                    pp = jnp.exp(_bdot_nt(qh, kp) + bias_ref[p, h, 0][None] + pen - lh)
                    pc = jnp.exp(_bdot_nt(qh, kc) + bias_ref[p, h, 1][None] - lh)
                    dsp = pp * (_bdot_nt(doh, vp) - delta)
                    dsc = pc * (_bdot_nt(doh, vc) - delta)
                    ds_ref[p, h, 0] += jnp.sum(dsp, axis=0)
                    ds_ref[p, h, 1] += jnp.sum(dsc, axis=0)
                    dspb, dscb = dsp.astype(BF16), dsc.astype(BF16)
                    dq = dq + jnp.where(masks[h], _bdot(dspb, kp) + _bdot(dscb, kc), 0.0)
                    dkp = dkp + _bdot_tn(dspb, qh)
                    dkc = dkc + _bdot_tn(dscb, qh)
                    dvp = dvp + _bdot_tn(pp.astype(BF16), doh)
                    dvc = dvc + _bdot_tn(pc.astype(BF16), doh)
                for j in range(GRP):
                    dq_ref[rc[j], :] += dq[j] * 0.125
                    dk_ref[rp[j], :] += dkp[j]
                    dk_ref[rc[j], :] += dkc[j]
                    dv_ref[rp[j], :] += dvp[j]
                    dv_ref[rc[j], :] += dvc[j]
                return carry

            lax.fori_loop(0, nblk // GRP, step, 0)
        car.phase(1, pl.program_id(0), jrefs)
        car.phase(2, pl.program_id(0), jrefs)

    blk = pl.BlockSpec((None, S, BLK), lambda g: (g, 0, 0))
    outs = pl.pallas_call(
        body, grid=(4,), name="attn_bwd",
        in_specs=[pl.BlockSpec((None, 3, None, S, BLK), lambda g: (0, 0, g, 0, 0)),
                  pl.BlockSpec((3, 2, 2, BLK, BLK), lambda g: (0, g, 0, 0, 0)), blk, blk, blk] + car.in_specs(),
        out_specs=[pl.BlockSpec((None, 3, None, S, BLK), lambda g: (0, 0, g, 0, 0)),
                   pl.BlockSpec((3, 2, 2, BLK, BLK), lambda g: (0, g, 0, 0, 0))] + car.out_specs(),
        out_shape=[SDS((2, 3, 4, S, BLK), F32), SDS((3, N_HEADS, 2, BLK, BLK), F32)] + car.out_shapes(),
        input_output_aliases=car.aliases(), scratch_shapes=car.sems(),
        compiler_params=_cp("arbitrary"))(qkv, bias, o, lse, do, *car.args())
    if job is not None:
        job.landed = list(outs[2:])
    return outs[0], outs[1]


def _shift_down(u, k, halo):
    n = u.shape[0]
    row = lax.broadcasted_iota(jnp.int32, u.shape, 0)
    out = pltpu.roll(u, k, 0)
    hn = halo.shape[0]
    for j in range(k):
        out = jnp.where(row == j, halo[hn - k + j:hn - k + j + 1, :], out)
    return out


def _shift_up(u, k, halo):
    n = u.shape[0]
    row = lax.broadcasted_iota(jnp.int32, u.shape, 0)
    out = pltpu.roll(u, n - k, 0)
    for j in range(k):
        out = jnp.where(row == n - k + j, halo[j:j + 1, :], out)
    return out


def _conv3(u, halo, w0, w1, w2):
    return _shift_down(u, 2, halo) * w0 + _shift_down(u, 1, halo) * w1 + u * w2


def _colsum(x):
    return jnp.sum(x, axis=0, keepdims=True)


def _mixer_specs(S, tm):
    conv = pl.BlockSpec((None, 12, tm, BLK), lambda i: (1, 0, i, 0))
    halo = pl.BlockSpec((None, 12, 8, BLK), lambda i: (1, 0, jnp.maximum(i * (tm // 8) - 1, 0), 0))
    ob = pl.BlockSpec((4, tm, BLK), lambda i: (0, i, 0))
    return conv, halo, ob


def _mixer_recompute(i, o_ref, pr_ref, ph_ref, w_ref):
    ob = [o_ref[q] for q in range(4)]
    gb = [pr_ref[q] for q in range(4)]
    gc = [pr_ref[4 + q] for q in range(4)]
    xi = [pr_ref[8 + q] for q in range(4)]
    keep = jnp.where(i > 0, 1.0, 0.0)
    u = [gc[q] * xi[q] for q in range(4)]
    hu = [ph_ref[4 + q] * ph_ref[8 + q] * keep for q in range(4)]
    w = [[w_ref[k:k + 1, q * BLK:(q + 1) * BLK] for k in range(3)] for q in range(4)]
    cv = [_conv3(u[q], hu[q], *w[q]) for q in range(4)]
    return ob, gb, gc, xi, u, hu, cv, w


def _rms_blocks(blocks):
    ss = sum(_rsum(b * b) for b in blocks)
    return lax.rsqrt(ss / (BLK * len(blocks)) + EPS)


def mixer_post_fwd(o, proj, w_sc, g_a, g_c):
    S = o.shape[1]
    tm = 256

    def body(o_ref, pr_ref, ph_ref, w_ref, ga_ref, gc_ref, m_ref):
        i = pl.program_id(0)
        ob, gb, _, _, _, _, cv, _ = _mixer_recompute(i, o_ref, pr_ref, ph_ref, w_ref)
        conv = [gb[q] * cv[q] for q in range(4)]
        ra, rc = _rms_blocks(ob), _rms_blocks(conv)
        for q in range(4):
            sl = slice(q * BLK, (q + 1) * BLK)
            m_ref[:, q * BLK:(q + 1) * BLK] = (ob[q] * ra * ga_ref[:, sl]).astype(BF16)
            m_ref[:, ATTN_W + q * BLK:ATTN_W + (q + 1) * BLK] = (conv[q] * rc * gc_ref[:, sl]).astype(BF16)

    conv_s, halo_s, ob_s = _mixer_specs(S, tm)
    full = lambda r, c: pl.BlockSpec((r, c), lambda i: (0, 0))
    return pl.pallas_call(
        body, grid=(S // tm,), name="mixer_post_fwd",
        in_specs=[ob_s, conv_s, halo_s, full(3, 512), full(1, 512), full(1, 512)],
        out_specs=pl.BlockSpec((tm, D_MODEL), lambda i: (i, 0)), out_shape=SDS((S, D_MODEL), BF16),
        compiler_params=_cp("parallel"))(o, proj, proj, w_sc, g_a, g_c)


def mixer_post_bwd_a(dmixed, o, proj, w_sc, g_a, g_c, job=None):
    S = o.shape[1]
    tm = 256

    def body(dm_ref, o_ref, pr_ref, ph_ref, w_ref, ga_ref, gc_ref, do_ref, dgb_ref, dcv_ref, dga_ref, dgc_ref):
        i = pl.program_id(0)
        ob, gb, _, _, _, _, cv, _ = _mixer_recompute(i, o_ref, pr_ref, ph_ref, w_ref)
        conv = [gb[q] * cv[q] for q in range(4)]
        ra, rc = _rms_blocks(ob), _rms_blocks(conv)

        @pl.when(i == 0)
        def _():
            dga_ref[...] = jnp.zeros_like(dga_ref)
            dgc_ref[...] = jnp.zeros_like(dgc_ref)

        for blocks, r, g_ref, off, dg_ref, is_attn in ((ob, ra, ga_ref, 0, dga_ref, True), (conv, rc, gc_ref, ATTN_W, dgc_ref, False)):
            xh = [blocks[q] * r for q in range(4)]
            dm = [dm_ref[:, off + q * BLK:off + (q + 1) * BLK].astype(F32) for q in range(4)]
            gd = [dm[q] * g_ref[:, q * BLK:(q + 1) * BLK] for q in range(4)]
            mean = sum(_rsum(gd[q] * xh[q]) for q in range(4)) / (4 * BLK)
            for q in range(4):
                dg_ref[:, q * BLK:(q + 1) * BLK] += _colsum(dm[q] * xh[q])
                dx = r * (gd[q] - xh[q] * mean)
                if is_attn:
                    do_ref[q] = dx
                else:
                    dgb_ref[q] = dx * cv[q]
                    dcv_ref[q] = dx * gb[q]

    conv_s, halo_s, ob_s = _mixer_specs(S, tm)
    full = lambda r, c: pl.BlockSpec((r, c), lambda i: (0, 0))
    return _pcall(
        body, [dmixed, o, proj, proj, w_sc, g_a, g_c], grid=(S // tm,), name="mixer_post_bwd_a",
        in_specs=[pl.BlockSpec((tm, D_MODEL), lambda i: (i, 0)), ob_s, conv_s, halo_s, full(3, 512), full(1, 512), full(1, 512)],
        out_specs=[ob_s, ob_s, ob_s, full(1, 512), full(1, 512)],
        out_shape=[SDS((4, S, BLK), F32)] * 3 + [SDS((1, 512), F32)] * 2, sem=("arbitrary",), job=job)


def mixer_post_bwd_b(dproj, dgb, dcv, proj, w_sc):
    S = proj.shape[2]
    tm = 256
    last = S // 8 - 1

    def body(dp_in, dgb_ref, dcv_ref, dn_ref, pr_ref, w_ref, dp_ref, dw_ref):
        i = pl.program_id(0)
        keep_next = jnp.where(i < pl.num_programs(0) - 1, 1.0, 0.0)

        @pl.when(i == 0)
        def _():
            dw_ref[...] = jnp.zeros_like(dw_ref)

        for q in range(4):
            sl = slice(q * BLK, (q + 1) * BLK)
            gc, xi = pr_ref[4 + q], pr_ref[8 + q]
            u = gc * xi
            dcv = dcv_ref[q]
            dn = dn_ref[q] * keep_next
            d1, d2 = _shift_up(dcv, 1, dn), _shift_up(dcv, 2, dn)
            w0, w1, w2 = (w_ref[k:k + 1, sl] for k in range(3))
            du = dcv * w2 + d1 * w1 + d2 * w0
            dw_ref[0:1, sl] += _colsum(d2 * u)
            dw_ref[1:2, sl] += _colsum(d1 * u)
            dw_ref[2:3, sl] += _colsum(dcv * u)
            dp_ref[q] = dgb_ref[q]
            dp_ref[4 + q] = du * xi
            dp_ref[8 + q] = du * gc

    conv_s, _, ob_s = _mixer_specs(S, tm)
    nxt = pl.BlockSpec((4, 8, BLK), lambda i: (0, jnp.minimum((i + 1) * (tm // 8), last), 0))
    full = lambda r, c: pl.BlockSpec((r, c), lambda i: (0, 0))
    return pl.pallas_call(
        body, grid=(S // tm,), name="mixer_post_bwd_b",
        in_specs=[pl.BlockSpec(memory_space=pl.ANY), ob_s, ob_s, nxt, conv_s, full(3, 512)],
        out_specs=[conv_s, full(3, 512)],
        out_shape=[SDS(dproj.shape, F32), SDS((3, 512), F32)],
        input_output_aliases={0: 0}, compiler_params=_cp("arbitrary"))(dproj, dgb, dcv, dcv, proj, w_sc)


def xattn_fwd(q, k, v):
    S = q.shape[0]
    tm = 512
    scale = MEM_HD ** -0.5

    def body(q_ref, k_ref, v_ref, o_ref):
        for h in range(N_MEM_HEADS):
            sl = slice(h * MEM_HD, (h + 1) * MEM_HD)
            s = _dot_nt(q_ref[:, sl], k_ref[:, sl]) * scale
            p = jnp.exp(s - jnp.max(s, axis=1, keepdims=True))
            p = p / _rsum(p)
            o_ref[:, sl] = _dot(p.astype(BF16), v_ref[:, sl]).astype(BF16)

    row = pl.BlockSpec((tm, D_MODEL), lambda i: (i, 0))
    kv = pl.BlockSpec(k.shape, lambda i: (0, 0))
    return pl.pallas_call(body, grid=(S // tm,), name="xattn_fwd", in_specs=[row, kv, kv], out_specs=row,
                          out_shape=SDS((S, D_MODEL), BF16), compiler_params=_cp("parallel"))(q, k, v)


def xattn_bwd(q, k, v, do):
    S = q.shape[0]
    tm = 512
    scale = MEM_HD ** -0.5

    def body(q_ref, k_ref, v_ref, do_ref, dq_ref, dk_ref, dv_ref):
        @pl.when(pl.program_id(0) == 0)
        def _():
            dk_ref[...] = jnp.zeros_like(dk_ref)
            dv_ref[...] = jnp.zeros_like(dv_ref)

        for h in range(N_MEM_HEADS):
            sl = slice(h * MEM_HD, (h + 1) * MEM_HD)
            qh, kh, vh, doh = q_ref[:, sl], k_ref[:, sl], v_ref[:, sl], do_ref[:, sl]
            s = _dot_nt(qh, kh) * scale
            p = jnp.exp(s - jnp.max(s, axis=1, keepdims=True))
            p = p / _rsum(p)
            dp = _dot_nt(doh, vh)
            ds = (p * (dp - _rsum(p * dp)) * scale).astype(BF16)
            dq_ref[:, sl] = _dot(ds, kh).astype(BF16)
            dk_ref[:, sl] += _dot_tn(ds, qh)
            dv_ref[:, sl] += _dot_tn(p.astype(BF16), doh)

    row = pl.BlockSpec((tm, D_MODEL), lambda i: (i, 0))
    kv = pl.BlockSpec(k.shape, lambda i: (0, 0))
    return pl.pallas_call(body, grid=(S // tm,), name="xattn_bwd", in_specs=[row, kv, kv, row], out_specs=[row, kv, kv],
                          out_shape=[SDS((S, D_MODEL), BF16), SDS(k.shape, F32), SDS(k.shape, F32)],
                          compiler_params=_cp("arbitrary"))(q, k, v, do)


FFN_TM, FFN_TC = 512, 256


def _ffn_specs(S, order):
    tm, tc = FFN_TM, FFN_TC
    ij = (lambda a, b: (a, b)) if order == "ij" else (lambda a, b: (b, a))
    blk = pl.BlockSpec((2, tm, tc), lambda a, b: (0,) + ij(a, b))
    prev = pl.BlockSpec((2, 16, tc), lambda a, b: (0, jnp.maximum(ij(a, b)[0] * (tm // 16) - 1, 0), ij(a, b)[1]))
    nxt = pl.BlockSpec((2, 16, tc), lambda a, b: (0, jnp.minimum((ij(a, b)[0] + 1) * (tm // 16), S // 16 - 1), ij(a, b)[1]))
    wsp = pl.BlockSpec((2, 3, tc), lambda a, b: (0, 0, ij(a, b)[1]))
    bsp = pl.BlockSpec((2, 1, tc), lambda a, b: (0, 0, ij(a, b)[1]))
    act = pl.BlockSpec((tm, tc), lambda a, b: ij(a, b))
    return blk, prev, nxt, wsp, bsp, act


def _ffn_up(i, up_ref, uh_ref, w_ref, b_ref):
    keep = jnp.where(i > 0, 1.0, 0.0)
    out = []
    for half in range(2):
        u = up_ref[half].astype(F32)
        hu = uh_ref[half].astype(F32) * keep
        w0, w1, w2 = (w_ref[half, k:k + 1, :] for k in range(3))
        out.append(_conv3(u, hu, w0, w1, w2) + b_ref[half])
    return out


def ffn_act_fwd(up_pre, w_fc, b_fc):
    S = up_pre.shape[1]

    def body(up_ref, uh_ref, w_ref, b_ref, a_ref):
        g, v = _ffn_up(pl.program_id(0), up_ref, uh_ref, w_ref, b_ref)
        a_ref[...] = (g * jax.nn.sigmoid(g) * v).astype(BF16)

    blk, prev, _, wsp, bsp, act = _ffn_specs(S, "ij")
    return pl.pallas_call(body, grid=(S // FFN_TM, D_FF // FFN_TC), name="ffn_act_fwd",
                          in_specs=[blk, prev, wsp, bsp], out_specs=act, out_shape=SDS((S, D_FF), BF16),
                          compiler_params=_cp("parallel", "parallel"))(up_pre, up_pre, w_fc, b_fc)


def ffn_act_bwd_a(dact, up_pre, w_fc, b_fc, job=None):
    S = up_pre.shape[1]

    def body(da_ref, up_ref, uh_ref, w_ref, b_ref, du_ref, db_ref):
        i = pl.program_id(1)
        g, v = _ffn_up(i, up_ref, uh_ref, w_ref, b_ref)
        da = da_ref[...].astype(F32)
        sg = jax.nn.sigmoid(g)
        dg = da * v * (sg * (1.0 + g * (1.0 - sg)))
        dv = da * g * sg

        @pl.when(i == 0)
        def _():
            db_ref[...] = jnp.zeros_like(db_ref)

        du_ref[0] = dg.astype(BF16)
        du_ref[1] = dv.astype(BF16)
        db_ref[0] += _colsum(dg)
        db_ref[1] += _colsum(dv)

    blk, prev, _, wsp, bsp, act = _ffn_specs(S, "ji")
    return _pcall(body, [dact, up_pre, up_pre, w_fc, b_fc], grid=(D_FF // FFN_TC, S // FFN_TM), name="ffn_act_bwd_a",
                  in_specs=[act, blk, prev, wsp, bsp], out_specs=[blk, bsp],
                  out_shape=[SDS((2, S, D_FF), BF16), SDS((2, 1, D_FF), F32)], sem=("parallel", "arbitrary"), job=job)


def ffn_act_bwd_b(d_up, up_pre, w_fc):
    S = up_pre.shape[1]

    def body(du_ref, dn_ref, up_ref, w_ref, dp_ref, dw_ref):
        i = pl.program_id(1)
        keep_next = jnp.where(i < pl.num_programs(1) - 1, 1.0, 0.0)

        @pl.when(i == 0)
        def _():
            dw_ref[...] = jnp.zeros_like(dw_ref)

        for half in range(2):
            d = du_ref[half].astype(F32)
            dn = dn_ref[half].astype(F32) * keep_next
            u = up_ref[half].astype(F32)
            d1, d2 = _shift_up(d, 1, dn), _shift_up(d, 2, dn)
            w0, w1, w2 = (w_ref[half, k:k + 1, :] for k in range(3))
            dp_ref[half] = (d * w2 + d1 * w1 + d2 * w0).astype(BF16)
            dw_ref[half, 0:1, :] += _colsum(d2 * u)
            dw_ref[half, 1:2, :] += _colsum(d1 * u)
            dw_ref[half, 2:3, :] += _colsum(d * u)

    blk, _, nxt, wsp, _, _ = _ffn_specs(S, "ji")
    return pl.pallas_call(body, grid=(D_FF // FFN_TC, S // FFN_TM), name="ffn_act_bwd_b",
                          in_specs=[blk, nxt, blk, wsp], out_specs=[blk, wsp],
                          out_shape=[SDS((2, S, D_FF), BF16), SDS((2, 3, D_FF), F32)],
                          compiler_params=_cp("parallel", "arbitrary"))(d_up, d_up, up_pre, w_fc)


def final_loss(x3, g, target):
    S, Dm = x3.shape
    tm = 512

    def body(x_ref, g_ref, t_ref, loss_ref, dx_ref, dg_ref):
        i = pl.program_id(0)
        xv = x_ref[...]
        r = lax.rsqrt(jnp.mean(xv * xv, axis=1, keepdims=True) + EPS)
        xh = xv * r
        err = xh * g_ref[...] - t_ref[...]
        dy = err / Dm
        gd = dy * g_ref[...]
        dx_ref[...] = r * (gd - xh * jnp.mean(gd * xh, axis=1, keepdims=True))

        @pl.when(i == 0)
        def _():
            dg_ref[...] = jnp.zeros_like(dg_ref)
            loss_ref[...] = jnp.zeros_like(loss_ref)

        dg_ref[...] += _colsum(dy * xh)
        loss_ref[...] += 0.5 * _colsum(jnp.mean(err * err, axis=1, keepdims=True))

    row = pl.BlockSpec((tm, Dm), lambda i: (i, 0))
    vec = pl.BlockSpec((1, Dm), lambda i: (0, 0))
    one = pl.BlockSpec((1, 1), lambda i: (0, 0))
    return pl.pallas_call(body, grid=(S // tm,), name="final_loss", in_specs=[row, vec, row], out_specs=[one, row, vec],
                          out_shape=[SDS((1, 1), F32), SDS((S, Dm), F32), SDS((1, Dm), F32)],
                          compiler_params=_cp("arbitrary"))(x3, g, target)


def local_step(x, mem, target, wb, ws, comm=None):
    S = x.shape[0]
    assert S % 2048 == 0
    if comm is None:
        comm = NoComm()
    else:
        wb = comm.gather_first()
    w_fc = ws["w_ffn_conv"].reshape(3, 2, D_FF).transpose(1, 0, 2)
    b_fc = ws["b_ffn_conv"].reshape(2, 1, D_FF)

    bias = bias_tables(ws["rel_bias"])
    h1 = rmsnorm_fwd(x, ws["g_mix"], "norm_mix")
    proj = mm_nn(h1, wb["w_in"], "proj", out_bw=BLK, tn=768, job=comm.job("proj"))
    comm.landed("proj", wb)
    qkv = proj.reshape(2, 3, 4, S, BLK)
    o, lse = attn_fwd(qkv, bias, job=comm.job("attn_fwd"))
    comm.landed("attn_fwd", wb)
    proj4 = proj.reshape(2, 12, S, BLK)
    mixed = mixer_post_fwd(o, proj4, ws["w_short_conv"], ws["g_attn_out"], ws["g_conv_out"])
    x1 = mm_nn(mixed, wb["w_out"], "out_proj", res=x)
    h2 = rmsnorm_fwd(x1, ws["g_xattn"], "norm_xattn")
    mem_n = rmsnorm_fwd(mem, ws["g_mem"], "norm_mem")
    xq = mm_nn(h2, wb["w_xq"], "xq", out_dtype=BF16)
    xk = mm_nn(mem_n, wb["w_xk"], "xk", out_dtype=BF16, tn=1024)
    xv = mm_nn(mem_n, wb["w_xv"], "xv", out_dtype=BF16, tn=1024)
    xo = xattn_fwd(xq, xk, xv)
    x2 = mm_nn(xo, wb["w_xo"], "xo_proj", res=x1)
    h3 = rmsnorm_fwd(x2, ws["g_ffn"], "norm_ffn")
    up_pre = mm_nn(h3, wb["w_up"], "up_proj", out_dtype=BF16, out_bw=D_FF, tn=1408)
    act = ffn_act_fwd(up_pre, w_fc, b_fc)
    x3 = mm_nn(act, wb["w_down"], "down_proj", res=x2, tm=512)
    loss, dx3, dg_final = final_loss(x3, ws["g_final"].reshape(1, -1), target)

    gb, gs = {}, {"g_final": dg_final}
    gb["w_down"] = mm_tn(act, dx3, None, "dw_down", tm=1408, tn=1024)
    dact = mm_nt(dx3, None, wb["w_down"], "d_act", out_dtype=BF16, tn=1408, job=comm.job("d_act", gb))
    comm.landed("d_act")
    d_up, db_fc = ffn_act_bwd_a(dact, up_pre, w_fc, b_fc, job=comm.job("ffn_act_bwd_a"))
    comm.landed("ffn_act_bwd_a")
    d_up_pre, dw_fc = ffn_act_bwd_b(d_up, up_pre, w_fc)
    gs["b_ffn_conv"] = db_fc.reshape(1, 2 * D_FF)
    gs["w_ffn_conv"] = dw_fc.transpose(1, 0, 2).reshape(3, 2 * D_FF)
    gb["w_up"] = mm_tn(h3, d_up_pre, D_FF, "dw_up", shards=4, tn=1408)
    dh3 = mm_nt(d_up_pre, D_FF, wb["w_up"], "d_h3", tm=512, job=comm.job("d_h3", gb))
    comm.landed("d_h3")
    dx2, gs["g_ffn"] = rmsnorm_bwd(x2, ws["g_ffn"], dh3, dx3, "norm_ffn_bwd")
    gb["w_xo"] = mm_tn(xo, dx2, None, "dw_xo")
    dxo = mm_nt(dx2, None, wb["w_xo"], "d_xo", out_dtype=BF16)
    dxq, dxk, dxv = xattn_bwd(xq, xk, xv, dxo)
    gb["w_xq"] = mm_tn(h2, dxq, None, "dw_xq")
    gb["w_xk"] = mm_tn(mem_n, dxk, None, "dw_xk", tn=1024)
    gb["w_xv"] = mm_tn(mem_n, dxv, None, "dw_xv", tn=1024)
    dh2 = mm_nt(dxq, None, wb["w_xq"], "d_h2")
    dmem_n = mm_nt(dxk, None, wb["w_xk"], "d_memk", tn=1024)
    dmem_n = mm_nt(dxv, None, wb["w_xv"], "d_memv", res=dmem_n, tn=1024)
    gs["g_mem"] = rmsnorm_bwd(mem, ws["g_mem"], dmem_n, None, "norm_mem_bwd")
    dx1, gs["g_xattn"] = rmsnorm_bwd(x1, ws["g_xattn"], dh2, dx2, "norm_xattn_bwd")
    gb["w_out"] = mm_tn(mixed, dx1, None, "dw_out")
    dmixed = mm_nt(dx1, None, wb["w_out"], "d_mixed")
    do, dgb, dcv, gs["g_attn_out"], gs["g_conv_out"] = mixer_post_bwd_a(
        dmixed, o, proj4, ws["w_short_conv"], ws["g_attn_out"], ws["g_conv_out"], job=comm.job("mixer_post_bwd_a", gb))
    comm.landed("mixer_post_bwd_a")
    dproj, ds_sum = attn_bwd(qkv, bias, o, lse, do, job=comm.job("attn_bwd"))
    comm.landed("attn_bwd")
    gs["rel_bias"] = bias_tables_bwd(ds_sum)
    dproj, gs["w_short_conv"] = mixer_post_bwd_b(dproj.reshape(2, 12, S, BLK), dgb, dcv, proj4, ws["w_short_conv"])
    dproj = dproj.reshape(24, S, BLK)
    gb["w_in"] = mm_tn(h1, dproj, BLK, "dw_in", shards=4, tn=768, job=comm.job("dw_in"))
    comm.landed("dw_in")
    dh1 = mm_nt(dproj, BLK, wb["w_in"], "d_h1", tm=512, job=comm.job("d_h1", gb))
    comm.landed("d_h1")
    grad_x, gs["g_mix"] = rmsnorm_bwd(x, ws["g_mix"], dh1, dx1, "norm_mix_bwd", job=comm.job("norm_mix_bwd"))
    comm.landed("norm_mix_bwd")
    return loss, grad_x, comm.finish(gb), gs


def adamw(w, g, m, v, name):
    R, C = w.shape
    tr = R
    for cand in (256, 352):
        if R > cand and R % cand == 0:
            tr = cand
            break

    def body(w_ref, g_ref, m_ref, v_ref, d_ref, nm_ref, nv_ref):
        gv = g_ref[...]
        mn = ADAM_B1 * m_ref[...] + (1.0 - ADAM_B1) * gv
        vn = ADAM_B2 * v_ref[...] + (1.0 - ADAM_B2) * (gv * gv)
        m_hat = mn / (1.0 - ADAM_B1 ** ADAM_STEP)
        v_hat = vn / (1.0 - ADAM_B2 ** ADAM_STEP)
        d_ref[...] = -ADAM_LR * (m_hat / (jnp.sqrt(v_hat) + ADAM_EPS) + ADAM_WD * w_ref[...])
        nm_ref[...] = mn
        nv_ref[...] = vn

    blk = pl.BlockSpec((tr, C), lambda i: (i, 0))
    return pl.pallas_call(body, grid=(R // tr,), name=name, in_specs=[blk] * 4, out_specs=[blk] * 3,
                          out_shape=[SDS((R, C), F32)] * 3, compiler_params=_cp("parallel"))(w, g, m, v)


BIG = ("w_in", "w_out", "w_xq", "w_xk", "w_xv", "w_xo", "w_up", "w_down")
COL_SHARDED = ("w_in", "w_up")
N_BIG = len(BIG)
ANY = pl.BlockSpec(memory_space=pl.ANY)


def _place():
    x, y, c = lax.axis_index("x"), lax.axis_index("y"), lax.axis_index("c")
    chips = [(1 - x, y), (x, 1 - y), (1 - x, 1 - y)]
    return x, y, c, chips


def _window(full, name, R, C, shard, half):
    r0, nr = (0, R) if half is None else (half * (R // 2), R // 2)
    if name in COL_SHARDED:
        return full.at[pl.ds(r0, nr), pl.ds(shard * C, C)]
    return full.at[pl.ds(shard * R + r0, nr), :]


def place_shard(w, mine_arr, n):
    R, C = w.shape
    col = n in COL_SHARDED

    def body(s_ref, w_ref, o_ref):
        o_ref[...] = w_ref[...].astype(BF16)

    grid_spec = pltpu.PrefetchScalarGridSpec(
        num_scalar_prefetch=1, grid=(1,), in_specs=[pl.BlockSpec((R, C), lambda i, s: (0, 0))],
        out_specs=pl.BlockSpec((R, C), (lambda i, s: (0, s[0])) if col else (lambda i, s: (s[0], 0))))
    return pl.pallas_call(body, grid_spec=grid_spec, name="place_" + n,
                          out_shape=SDS((R, 4 * C) if col else (4 * R, C), BF16),
                          compiler_params=_cp("arbitrary"))(mine_arr, w)


def _gather_jobs(names, shapes):
    nw = len(names)

    def start(full, sems):
        send, recv, fsend, frecv = sems
        x, y, c, chips = _place()
        mine = 2 * x + y
        for w, n in enumerate(names):
            R, C = shapes[w]
            own = _window(full[w], n, R, C, mine, c)
            for j, chip in enumerate(chips):
                pltpu.make_async_remote_copy(src_ref=own, dst_ref=own, send_sem=send.at[w, j], recv_sem=recv.at[w, j],
                                             device_id=(*chip, c), device_id_type=MESH).start()

    def mid(full, sems):
        send, recv, fsend, frecv = sems
        x, y, c, chips = _place()
        sib = (x, y, 1 - c)
        for w, n in enumerate(names):
            R, C = shapes[w]
            for j, chip in enumerate(chips):
                landed = _window(full[w], n, R, C, 2 * chip[0] + chip[1], c)
                pltpu.make_async_remote_copy(src_ref=landed, dst_ref=landed, send_sem=send.at[w, j], recv_sem=recv.at[w, j],
                                             device_id=(*chip, c), device_id_type=MESH).wait_recv()
                pltpu.make_async_remote_copy(src_ref=landed, dst_ref=landed, send_sem=fsend.at[w, j],
                                             recv_sem=frecv.at[w, j], device_id=sib, device_id_type=MESH).start()

    def finish(full, sems):
        send, recv, fsend, frecv = sems
        x, y, c, chips = _place()
        mine = 2 * x + y
        sib = (x, y, 1 - c)
        for w, n in enumerate(names):
            R, C = shapes[w]
            own = _window(full[w], n, R, C, mine, c)
            for j, chip in enumerate(chips):
                landed = _window(full[w], n, R, C, 2 * chip[0] + chip[1], c)
                other = _window(full[w], n, R, C, 2 * chip[0] + chip[1], 1 - c)
                pltpu.make_async_remote_copy(src_ref=other, dst_ref=other, send_sem=fsend.at[w, j], recv_sem=frecv.at[w, j],
                                             device_id=sib, device_id_type=MESH).wait_recv()
                pltpu.make_async_remote_copy(src_ref=own, dst_ref=own, send_sem=send.at[w, j], recv_sem=recv.at[w, j],
                                             device_id=(*chip, c), device_id_type=MESH).wait_send()
                pltpu.make_async_remote_copy(src_ref=landed, dst_ref=landed, send_sem=fsend.at[w, j],
                                             recv_sem=frecv.at[w, j], device_id=sib, device_id_type=MESH).wait_send()

    return start, mid, finish, [pltpu.SemaphoreType.DMA((nw, 3))] * 4


class CommJob:
    def __init__(self, ins, out_shapes, inplace, start, finish, sems, mid=None):
        self.ins, self.out_shapes, self.inplace = list(ins), list(out_shapes), inplace
        self.start, self.mid, self.finish, self.sems = start, mid, finish, list(sems)

    def run(self, phase, ins, outs, sems):
        if phase == 0:
            self.start(ins, outs, sems)
        elif phase == 1:
            if self.mid is not None:
                self.mid(ins, outs, sems)
        else:
            self.finish(ins, outs, sems)


def run_job(name, job):
    n_in, n_out = len(job.ins), len(job.out_shapes)

    def body(*refs):
        ins, outs, sm = refs[:n_in], refs[n_in:n_in + n_out], refs[n_in + n_out:]
        for phase in range(3):
            job.run(phase, ins, outs, sm)

    return pl.pallas_call(
        body, name=name, in_specs=[ANY] * n_in, out_specs=[ANY] * n_out, out_shape=job.out_shapes,
        input_output_aliases={i: i for i in range(n_in)} if job.inplace else {},
        scratch_shapes=job.sems)(*job.ins)


class Carried:
    def __init__(self, job, n_in, n_out, n_scratch, steps):
        self.job, self.n_in, self.n_out, self.n_scratch, self.steps = job, n_in, n_out, n_scratch, steps
        self.nji = len(job.ins) if job else 0
        self.njo = len(job.out_shapes) if job else 0

    def in_specs(self):
        return [ANY] * self.nji

    def out_specs(self):
        return [ANY] * self.njo

    def out_shapes(self):
        return list(self.job.out_shapes) if self.job else []

    def aliases(self):
        if not (self.job and self.job.inplace):
            return {}
        return {self.n_in + i: self.n_out + i for i in range(self.nji)}

    def sems(self):
        return list(self.job.sems) if self.job else []

    def args(self):
        return list(self.job.ins) if self.job else []

    def split(self, refs):
        a = self.n_in
        b = a + self.nji
        c = b + self.n_out
        d = c + self.njo
        e = d + self.n_scratch
        return refs[:a] + refs[b:c] + refs[d:e], (refs[a:b], refs[c:d], refs[e:])

    def phase(self, phase, step, jrefs):
        if self.job is None:
            return
        at = {0: 0, 1: max(self.steps - 2, 0), 2: self.steps - 1}[phase]

        @pl.when(step == at)
        def _():
            self.job.run(phase, *jrefs)


def _pcall(body, args, *, grid, in_specs, out_specs, out_shape, name, sem, scratch=(), aliases=None, job=None):
    n_in, n_out = len(args), len(out_shape)
    if job is None:
        return pl.pallas_call(
            body, grid=grid, in_specs=list(in_specs), out_specs=list(out_specs), out_shape=list(out_shape), name=name,
            scratch_shapes=list(scratch), input_output_aliases=dict(aliases or {}), compiler_params=_cp(*sem))(*args)
    total = int(np.prod(grid))
    car = Carried(job, n_in, n_out, len(scratch), total)

    def wrapped(*refs):
        main, jrefs = car.split(refs)
        lin = pl.program_id(0)
        for ax in range(1, len(grid)):
            lin = lin * grid[ax] + pl.program_id(ax)
        car.phase(0, lin, jrefs)
        body(*main)
        car.phase(1, lin, jrefs)
        car.phase(2, lin, jrefs)

    res = pl.pallas_call(
        wrapped, grid=grid, in_specs=list(in_specs) + car.in_specs(), out_specs=list(out_specs) + car.out_specs(),
        out_shape=list(out_shape) + car.out_shapes(), name=name, scratch_shapes=list(scratch) + car.sems(),
        input_output_aliases={**dict(aliases or {}), **car.aliases()},
        compiler_params=_cp(*(["arbitrary"] * len(grid))))(*args, *car.args())
    job.landed = list(res[n_out:])
    return list(res[:n_out])


def gather_job(placed, names):
    shapes = []
    for n in names:
        R, C = placed[n].shape
        shapes.append((R, C // 4) if n in COL_SHARDED else (R // 4, C))
    start, mid, finish, sems = _gather_jobs(names, shapes)
    arrays = [placed[n] for n in names]
    return CommJob(arrays, [SDS(a.shape, a.dtype) for a in arrays], True,
                   lambda i, o, s: start(o, s), lambda i, o, s: finish(o, s), sems, mid=lambda i, o, s: mid(o, s))


def pair_exchange_job(grads, names):
    shapes = [grads[n].shape for n in names]

    def copies(ins, outs, sems):
        x, y, c, _ = _place()
        return [pltpu.make_async_remote_copy(
            src_ref=ins[w].at[:, pl.ds((1 - c) * (shapes[w][1] // 2), shapes[w][1] // 2), :], dst_ref=outs[w],
            send_sem=sems[0].at[w], recv_sem=sems[1].at[w], device_id=(x, y, 1 - c), device_id_type=MESH)
            for w in range(len(names))]

    def start(ins, outs, sems):
        for cp in copies(ins, outs, sems):
            cp.start()

    def finish(ins, outs, sems):
        for cp in copies(ins, outs, sems):
            cp.wait()

    return CommJob([grads[n] for n in names], [SDS((4, s[1] // 2, s[2]), F32) for s in shapes], False, start, finish,
                   [pltpu.SemaphoreType.DMA((len(names),))] * 2)


def chip_exchange_job(p16, names):
    shapes = [p16[n].shape for n in names]

    def copies(ins, outs, sems):
        x, y, c, chips = _place()
        return [pltpu.make_async_remote_copy(
            src_ref=ins[w].at[2 * chip[0] + chip[1]], dst_ref=outs[w].at[j],
            send_sem=sems[0].at[w, j], recv_sem=sems[1].at[w, j], device_id=(*chip, c), device_id_type=MESH)
            for w in range(len(names)) for j, chip in enumerate(chips)]

    def start(ins, outs, sems):
        for cp in copies(ins, outs, sems):
            cp.start()

    def finish(ins, outs, sems):
        for cp in copies(ins, outs, sems):
            cp.wait()

    return CommJob([p16[n] for n in names], [SDS((3,) + tuple(s[1:]), BF16) for s in shapes], False, start, finish,
                   [pltpu.SemaphoreType.DMA((len(names), 3))] * 2)


def half_exchange_job(full, names):
    shapes = [full[n].shape for n in names]

    def copies(outs, sems):
        x, y, c, _ = _place()
        cps = []
        for w in range(len(names)):
            Rh = shapes[w][0] // 2
            rows = outs[w].at[pl.ds(c * Rh, Rh), :]
            other = outs[w].at[pl.ds((1 - c) * Rh, Rh), :]
            cps.append((pltpu.make_async_remote_copy(src_ref=rows, dst_ref=rows, send_sem=sems[0].at[w], recv_sem=sems[1].at[w],
                                                     device_id=(x, y, 1 - c), device_id_type=MESH),
                        pltpu.make_async_remote_copy(src_ref=other, dst_ref=other, send_sem=sems[0].at[w], recv_sem=sems[1].at[w],
                                                     device_id=(x, y, 1 - c), device_id_type=MESH)))
        return cps

    def start(ins, outs, sems):
        for send, _ in copies(outs, sems):
            send.start()

    def finish(ins, outs, sems):
        for send, recv in copies(outs, sems):
            recv.wait_recv()
            send.wait_send()

    arrays = [full[n] for n in names]
    return CommJob(arrays, [SDS(a.shape, a.dtype) for a in arrays], True, start, finish,
                   [pltpu.SemaphoreType.DMA((len(names),))] * 2)


def allreduce_small(v, name):
    rows = v.shape[0]

    def body(v_ref, o_ref, buf, send, recv):
        x, y, c, _ = _place()
        me = 4 * x + 2 * y + c
        buf[0] = v_ref[...]
        sends = []
        for k in range(1, 8):
            peer = (x ^ (k >> 2), y ^ ((k >> 1) & 1), c ^ (k & 1))
            cp = pltpu.make_async_remote_copy(src_ref=v_ref, dst_ref=buf.at[k], send_sem=send.at[k - 1], recv_sem=recv.at[k - 1],
                                              device_id=peer, device_id_type=MESH)
            cp.start()
            sends.append(cp)
        for cp in sends:
            cp.wait_recv()
        acc = buf[me]
        for d in range(1, 8):
            acc = acc + buf[d ^ me]
        o_ref[...] = acc
        for cp in sends:
            cp.wait_send()

    vm = pl.BlockSpec(memory_space=pltpu.VMEM)
    return pl.pallas_call(
        body, name=name, in_specs=[vm], out_specs=vm, out_shape=SDS((rows, 128), F32),
        scratch_shapes=[pltpu.VMEM((8, rows, 128), F32), pltpu.SemaphoreType.DMA((7,)), pltpu.SemaphoreType.DMA((7,))],
    )(v)


def pair_sum(g, xh, c_arr, name):
    _, R, C = g.shape
    Rh = R // 2

    def body(c_ref, g_ref, x_ref, p32_ref, p16_ref):
        s = g_ref[...] + x_ref[...]
        p32_ref[...] = s
        p16_ref[...] = s.astype(BF16)

    same = pl.BlockSpec((None, Rh, C), lambda s, c: (s, 0, 0))
    grid_spec = pltpu.PrefetchScalarGridSpec(
        num_scalar_prefetch=1, grid=(4,),
        in_specs=[pl.BlockSpec((None, Rh, C), lambda s, c: (s, c[0], 0)), same], out_specs=[same, same])
    return pl.pallas_call(body, grid_spec=grid_spec, name=name, out_shape=[SDS((4, Rh, C), F32), SDS((4, Rh, C), BF16)],
                          compiler_params=_cp("parallel"))(c_arr, g, xh)


def chip_sum(p32, y, sel_arr, name):
    _, Rh, C = p32.shape

    def body(s_ref, p_ref, y_ref, o_ref):
        o_ref[...] = ((p_ref[...] + y_ref[0].astype(F32)) + y_ref[1].astype(F32)) + y_ref[2].astype(F32)

    grid_spec = pltpu.PrefetchScalarGridSpec(
        num_scalar_prefetch=1, grid=(1,),
        in_specs=[pl.BlockSpec((None, Rh, C), lambda i, s: (s[0], 0, 0)), pl.BlockSpec((3, Rh, C), lambda i, s: (0, 0, 0))],
        out_specs=pl.BlockSpec((Rh, C), lambda i, s: (s[1], 0)))
    return pl.pallas_call(body, grid_spec=grid_spec, name=name, out_shape=SDS((2 * Rh, C), F32),
                          compiler_params=_cp("arbitrary"))(sel_arr, p32, y)


class StepComm:
    FIRST = ("w_in",)
    SQUARE = ("w_out", "w_xq", "w_xk", "w_xv", "w_xo")
    FFN = ("w_up", "w_down")
    PLAN = {
        "proj": ("gather", SQUARE), "attn_fwd": ("gather", FFN),
        "d_act": ("pair", ("w_down",)), "ffn_act_bwd_a": ("chip", ("w_down",)),
        "d_h3": ("pair", ("w_up",)), "mixer_post_bwd_a": ("pair", SQUARE), "attn_bwd": ("chip", ("w_up",) + SQUARE),
        "dw_in": ("half", FFN + SQUARE), "d_h1": ("pair", FIRST), "norm_mix_bwd": ("chip", FIRST),
    }

    def __init__(self, placed, c_arr, sel_arr):
        self.placed, self.c_arr, self.sel_arr = placed, c_arr, sel_arr
        self.g, self.p32, self.p16, self.full, self.done, self.jobs = {}, {}, {}, {}, {}, {}

    def gather_first(self):
        return dict(zip(self.FIRST, run_job("gather_first", gather_job(self.placed, self.FIRST))))

    def job(self, tag, gb=None):
        kind, names = self.PLAN[tag]
        if kind == "gather":
            j = gather_job(self.placed, names)
        elif kind == "pair":
            for n in names:
                self.g[n] = gb[n] if gb[n].ndim == 3 else gb[n].reshape(4, gb[n].shape[0] // 4, gb[n].shape[1])
            j = pair_exchange_job(self.g, names)
        elif kind == "chip":
            j = chip_exchange_job(self.p16, names)
        else:
            j = half_exchange_job(self.full, names)
        self.jobs[tag] = j
        return j

    def landed(self, tag, wb=None):
        kind, names = self.PLAN[tag]
        for n, a in zip(names, self.jobs[tag].landed):
            if kind == "gather":
                wb[n] = a
            elif kind == "pair":
                self.p32[n], self.p16[n] = pair_sum(self.g[n], a, self.c_arr, "pair_sum_" + n)
            elif kind == "chip":
                self.full[n] = chip_sum(self.p32[n], a, self.sel_arr, "chip_sum_" + n)
            else:
                self.done[n] = a

    def finish(self, gb):
        last = run_job("half_exchange", half_exchange_job(self.full, self.FIRST))
        self.done.update(zip(self.FIRST, last))
        return self.done


class NoComm:
    def job(self, tag, gb=None):
        return None

    def landed(self, tag, wb=None):
        pass

    def finish(self, gb):
        return gb


SMALL = ("rel_bias", "g_mix", "w_short_conv", "g_attn_out", "g_conv_out", "g_xattn", "g_mem", "g_ffn",
         "w_ffn_conv", "b_ffn_conv", "g_final")
SMALL_FULL = {"rel_bias": (8, 32), "g_mix": (1, 1024), "w_short_conv": (3, 512), "g_attn_out": (1, 512), "g_conv_out": (1, 512),
              "g_xattn": (1, 1024), "g_mem": (1, 1024), "g_ffn": (1, 1024), "w_ffn_conv": (3, 5632), "b_ffn_conv": (1, 5632),
              "g_final": (1024,)}
SMALL_SHARDED = ("w_short_conv", "w_ffn_conv")


def _pack(parts):
    flat = jnp.concatenate([p.reshape(-1) for p in parts])
    rows = -(-flat.shape[0] // 1024) * 8
    return jnp.pad(flat, (0, rows * 128 - flat.shape[0])).reshape(rows, 128)


def _unpack(packed, names):
    flat, out, off = packed.reshape(-1), {}, 0
    for n in names:
        size = int(np.prod(SMALL_FULL[n]))
        out[n] = flat[off:off + size].reshape(SMALL_FULL[n])
        off += size
    return out


def kernel(x, mem, rel_bias, g_mix, w_in, w_short_conv, g_attn_out, g_conv_out, w_out, g_xattn, g_mem, w_xq, w_xk, w_xv, w_xo, g_ffn, w_up, w_ffn_conv, b_ffn_conv, w_down, g_final, loss_target, m_rel_bias, m_g_mix, m_w_in, m_w_short_conv, m_g_attn_out, m_g_conv_out, m_w_out, m_g_xattn, m_g_mem, m_w_xq, m_w_xk, m_w_xv, m_w_xo, m_g_ffn, m_w_up, m_w_ffn_conv, m_b_ffn_conv, m_w_down, m_g_final, v_rel_bias, v_g_mix, v_w_in, v_w_short_conv, v_g_attn_out, v_g_conv_out, v_w_out, v_g_xattn, v_g_mem, v_w_xq, v_w_xk, v_w_xv, v_w_xo, v_g_ffn, v_w_up, v_w_ffn_conv, v_b_ffn_conv, v_w_down, v_g_final):
    names = ("rel_bias", "g_mix", "w_in", "w_short_conv", "g_attn_out", "g_conv_out", "w_out", "g_xattn", "g_mem", "w_xq",
             "w_xk", "w_xv", "w_xo", "g_ffn", "w_up", "w_ffn_conv", "b_ffn_conv", "w_down", "g_final")
    W = dict(zip(names, (rel_bias, g_mix, w_in, w_short_conv, g_attn_out, g_conv_out, w_out, g_xattn, g_mem, w_xq, w_xk, w_xv,
                         w_xo, g_ffn, w_up, w_ffn_conv, b_ffn_conv, w_down, g_final)))
    M = dict(zip(names, (m_rel_bias, m_g_mix, m_w_in, m_w_short_conv, m_g_attn_out, m_g_conv_out, m_w_out, m_g_xattn, m_g_mem,
                         m_w_xq, m_w_xk, m_w_xv, m_w_xo, m_g_ffn, m_w_up, m_w_ffn_conv, m_b_ffn_conv, m_w_down, m_g_final)))
    V = dict(zip(names, (v_rel_bias, v_g_mix, v_w_in, v_w_short_conv, v_g_attn_out, v_g_conv_out, v_w_out, v_g_xattn, v_g_mem,
                         v_w_xq, v_w_xk, v_w_xv, v_w_xo, v_g_ffn, v_w_up, v_w_ffn_conv, v_b_ffn_conv, v_w_down, v_g_final)))
    xi, yi, ci = lax.axis_index("x"), lax.axis_index("y"), lax.axis_index("c")
    mine = 2 * xi + yi
    c_arr = jnp.reshape(ci, (1,)).astype(jnp.int32)
    mine_arr = jnp.reshape(mine, (1,)).astype(jnp.int32)

    sel_arr = jnp.stack([mine, ci]).astype(jnp.int32)
    comm = StepComm({n: place_shard(W[n][0], mine_arr, n) for n in BIG}, c_arr, sel_arr)
    first = jnp.where(ci == 0, 1.0, 0.0).astype(F32)
    emb = []
    for n in SMALL_SHARDED:
        shard = W[n][0]
        full = jnp.zeros(SMALL_FULL[n], F32)
        emb.append(lax.dynamic_update_slice(full, shard * first, (0, mine * shard.shape[1])))
    conv_w = allreduce_small(_pack(emb), "gather_small")
    conv_w = _unpack(conv_w, SMALL_SHARDED)
    ws = {n: W[n] for n in SMALL if n not in SMALL_SHARDED}
    ws.update(conv_w)

    loss, grad_x, gfull, gs = local_step(x[0], mem[0], loss_target[0], None, ws, comm)
    loss = lax.psum(loss[0, 0], ("x", "y", "c"))

    gsm = _unpack(allreduce_small(_pack([gs[n] for n in SMALL]), "reduce_small"), SMALL)

    grads, delta, new_m, new_v = {}, {}, {}, {}
    for n in BIG:
        grads[n] = gfull[n][None]
        d, nm, nv = adamw(W[n][0], gfull[n], M[n][0], V[n][0], "adamw_" + n)
        delta[n], new_m[n], new_v[n] = d[None], nm[None], nv[None]
    for n in SMALL_SHARDED:
        wid = W[n].shape[2]
        g = lax.dynamic_slice(gsm[n], (0, mine * wid), (3, wid))
        grads[n] = g[None]
        d, nm, nv = adamw(W[n][0], g, M[n][0], V[n][0], "adamw_" + n)
        delta[n], new_m[n], new_v[n] = d[None], nm[None], nv[None]
    rep = [n for n in SMALL if n not in SMALL_SHARDED]
    d, nm, nv = adamw(_pack([W[n] for n in rep]), _pack([gsm[n] for n in rep]), _pack([M[n] for n in rep]),
                      _pack([V[n] for n in rep]), "adamw_small")

    def unpack_rep(p):
        flat, out, off = p.reshape(-1), {}, 0
        for n in rep:
            size = int(np.prod(SMALL_FULL[n]))
            out[n] = flat[off:off + size].reshape(SMALL_FULL[n])
            off += size
        return out

    d, nm, nv = unpack_rep(d), unpack_rep(nm), unpack_rep(nv)
    for n in rep:
        grads[n], delta[n], new_m[n], new_v[n] = gsm[n], d[n], nm[n], nv[n]
    return (loss, grad_x[None], *[grads[n] for n in names], *[delta[n] for n in names],
            *[new_m[n] for n in names], *[new_v[n] for n in names])
```

```python
import functools
import math

import numpy as np
import jax
import jax.numpy as jnp
from jax import lax
from jax.experimental import pallas as pl
from jax.experimental.pallas import tpu as pltpu

F32 = jnp.float32
BF16 = jnp.bfloat16
SDS = jax.ShapeDtypeStruct
MESH = pl.DeviceIdType.MESH

D_MODEL = 1024
ATTN_W = 512
N_HEADS = 8
BLK = 128
PATTERNS = ((128, 1), (512, 4), (2048, 16))
N_BUCKETS = 32
D_FF = 2816
N_MEM_HEADS = 4
MEM_HD = 256
EPS = 1e-6
NEG = -1e30
VMEM_LIMIT = 56 * 1024 * 1024

ADAM_LR, ADAM_B1, ADAM_B2, ADAM_EPS, ADAM_WD, ADAM_STEP = 0.001, 0.9, 0.999, 1e-08, 0.01, 10


def _cp(*sem):
    return pltpu.CompilerParams(dimension_semantics=sem, vmem_limit_bytes=VMEM_LIMIT)


def _dot(a, b):
    return jnp.dot(a, b, preferred_element_type=F32)


def _dot_nt(a, b):
    return lax.dot_general(a, b, (((1,), (1,)), ((), ())), preferred_element_type=F32)


def _dot_tn(a, b):
    return lax.dot_general(a, b, (((0,), (0,)), ((), ())), preferred_element_type=F32)


def _rsum(x):
    return jnp.sum(x, axis=1, keepdims=True)


def rmsnorm_fwd(x, g, name):
    S, Dm = x.shape
    tm = min(S, 512)

    def body(x_ref, g_ref, o_ref):
        xv = x_ref[...]
        r = lax.rsqrt(jnp.mean(xv * xv, axis=1, keepdims=True) + EPS)
        o_ref[...] = (xv * r * g_ref[...]).astype(o_ref.dtype)

    return pl.pallas_call(
        body, grid=(S // tm,), name=name,
        in_specs=[pl.BlockSpec((tm, Dm), lambda i: (i, 0)), pl.BlockSpec((1, Dm), lambda i: (0, 0))],
        out_specs=pl.BlockSpec((tm, Dm), lambda i: (i, 0)),
        out_shape=SDS((S, Dm), BF16), compiler_params=_cp("parallel"))(x, g)


def rmsnorm_bwd(x, g, dh, res, name, job=None):
    S, Dm = x.shape
    tm = min(S, 512)
    want_dx = res is not None

    def body(*refs):
        if want_dx:
            x_ref, g_ref, dh_ref, res_ref, dx_ref, dg_ref = refs
        else:
            x_ref, g_ref, dh_ref, dg_ref = refs
        i = pl.program_id(0)
        xv = x_ref[...]
        r = lax.rsqrt(jnp.mean(xv * xv, axis=1, keepdims=True) + EPS)
        xh = xv * r
        dh = dh_ref[...].astype(F32)
        if want_dx:
            gd = dh * g_ref[...]
            dx_ref[...] = res_ref[...] + r * (gd - xh * jnp.mean(gd * xh, axis=1, keepdims=True))

        @pl.when(i == 0)
        def _():
            dg_ref[...] = jnp.zeros_like(dg_ref)

        dg_ref[...] += jnp.sum(dh * xh, axis=0, keepdims=True)

    row = pl.BlockSpec((tm, Dm), lambda i: (i, 0))
    vec = pl.BlockSpec((1, Dm), lambda i: (0, 0))
    if want_dx:
        return _pcall(body, [x, g, dh, res], grid=(S // tm,), name=name, in_specs=[row, vec, row, row], out_specs=[row, vec],
                      out_shape=[SDS((S, Dm), F32), SDS((1, Dm), F32)], sem=("arbitrary",), job=job)
    return pl.pallas_call(
        body, grid=(S // tm,), name=name, in_specs=[row, vec, row], out_specs=vec,
        out_shape=SDS((1, Dm), F32), compiler_params=_cp("arbitrary"))(x, g, dh)


def _col_spec(bw, tn, rows, row_of, col_of):
    if bw is None:
        return pl.BlockSpec((rows, tn), lambda *g: (row_of(*g), col_of(*g)))
    if tn % bw == 0:
        return pl.BlockSpec((tn // bw, rows, bw), lambda *g: (col_of(*g), row_of(*g), 0))
    per = bw // tn
    return pl.BlockSpec((None, rows, tn), lambda *g: (col_of(*g) // per, row_of(*g), col_of(*g) % per))


def _read_cols(ref, bw, tn):
    if bw is None or tn % bw != 0:
        return ref[...]
    if tn == bw:
        return ref[0]
    return jnp.concatenate([ref[q] for q in range(tn // bw)], axis=1)


def _write_cols(ref, bw, tn, val):
    if bw is None or tn % bw != 0:
        ref[...] = val.astype(ref.dtype)
    else:
        for q in range(tn // bw):
            ref[q] = val[:, q * bw:(q + 1) * bw].astype(ref.dtype)


def mm_nn(a, b, name, *, res=None, out_dtype=F32, out_bw=None, tm=1024, tn=512, job=None):
    M, K = a.shape
    N = b.shape[1]
    tm = min(tm, M)

    def body(*refs):
        if res is None:
            a_ref, b_ref, o_ref = refs
        else:
            a_ref, b_ref, r_ref, o_ref = refs
        acc = _dot(a_ref[...].astype(BF16), b_ref[...])
        if res is not None:
            acc = acc + r_ref[...]
        _write_cols(o_ref, out_bw, tn, acc)

    ri, ci = (lambda i, j: i), (lambda i, j: j)
    in_specs = [pl.BlockSpec((tm, K), lambda i, j: (i, 0)), pl.BlockSpec((K, tn), lambda i, j: (0, j))]
    args = [a, b]
    if res is not None:
        in_specs.append(pl.BlockSpec((tm, tn), lambda i, j: (i, j)))
        args.append(res)
    oshape = (M, N) if out_bw is None else (N // out_bw, M, out_bw)
    return _pcall(body, args, grid=(M // tm, N // tn), name=name, in_specs=in_specs,
                  out_specs=[_col_spec(out_bw, tn, tm, ri, ci)], out_shape=[SDS(oshape, out_dtype)],
                  sem=("parallel", "parallel"), job=job)[0]


def mm_nt(a, a_bw, b, name, *, res=None, out_dtype=F32, tm=1024, tn=512, job=None):
    if a_bw is None:
        M, K = a.shape
    else:
        M, K = a.shape[1], a.shape[0] * a_bw
    N = b.shape[0]
    tm = min(tm, M)

    def body(*refs):
        if res is None:
            a_ref, b_ref, o_ref = refs
        else:
            a_ref, b_ref, r_ref, o_ref = refs
        av = _read_cols(a_ref, a_bw, K).astype(BF16)
        acc = _dot_nt(av, b_ref[...])
        if res is not None:
            acc = acc + r_ref[...]
        o_ref[...] = acc.astype(o_ref.dtype)

    in_specs = [_col_spec(a_bw, K, tm, lambda i, j: i, lambda i, j: 0), pl.BlockSpec((tn, K), lambda i, j: (j, 0))]
    args = [a, b]
    if res is not None:
        in_specs.append(pl.BlockSpec((tm, tn), lambda i, j: (i, j)))
        args.append(res)
    return _pcall(body, args, grid=(M // tm, N // tn), name=name, in_specs=in_specs,
                  out_specs=[pl.BlockSpec((tm, tn), lambda i, j: (i, j))], out_shape=[SDS((M, N), out_dtype)],
                  sem=("parallel", "parallel"), job=job)[0]


def mm_tn(a, b, b_bw, name, *, shards=None, tm=1024, tn=512, ts=512, job=None):
    S, Ka = a.shape
    N = b.shape[1] if b_bw is None else b.shape[0] * b_bw
    ts = min(ts, S)
    tm = min(tm, Ka)

    def body(a_ref, b_ref, o_ref):
        @pl.when(pl.program_id(2) == 0)
        def _():
            o_ref[...] = jnp.zeros_like(o_ref)

        bv = _read_cols(b_ref, b_bw, tn).astype(BF16)
        o_ref[...] += _dot_tn(a_ref[...].astype(BF16), bv)

    in_specs = [pl.BlockSpec((ts, tm), lambda i, j, k: (k, i)),
                _col_spec(b_bw, tn, ts, lambda i, j, k: k, lambda i, j, k: j)]
    if shards is None:
        out_spec, oshape = pl.BlockSpec((tm, tn), lambda i, j, k: (i, j)), (Ka, N)
    else:
        per = (N // shards) // tn
        out_spec = pl.BlockSpec((None, tm, tn), lambda i, j, k: (j // per, i, j % per))
        oshape = (shards, Ka, N // shards)
    return _pcall(body, [a, b], grid=(Ka // tm, N // tn, S // ts), name=name, in_specs=in_specs, out_specs=[out_spec],
                  out_shape=[SDS(oshape, F32)], sem=("parallel", "parallel", "arbitrary"), job=job)[0]


def _bucket_tables():
    out = np.zeros((3, 2, BLK, BLK), np.int32)
    qi = np.arange(BLK)[:, None]
    for p, (win, dil) in enumerate(PATTERNS):
        w = win // dil
        assert w == BLK
        for half in range(2):
            kj = np.arange(BLK)[None, :] + half * BLK
            steps = qi + w - kj
            valid = (steps >= 0) & (steps <= w)
            dist = np.clip(steps, 0, w) * dil
            dd = np.maximum(dist, 1).astype(np.float32)
            large = 16 + (np.log(dd / np.float32(16)) / np.float32(math.log(2048 / 16)) * np.float32(16)).astype(np.int32)
            large = np.minimum(large, N_BUCKETS - 1)
            out[p, half] = np.where(valid, np.where(dist < 16, dist, large), -1)
    return out


def bias_tables(rel_bias):
    bkt = jnp.asarray(_bucket_tables())

    def body(rb_ref, bkt_ref, o_ref):
        h = pl.program_id(1)
        for half in range(2):
            bk = bkt_ref[half]
            acc = jnp.full((BLK, BLK), NEG, F32)
            for b in range(N_BUCKETS):
                acc = jnp.where(bk == b, rb_ref[h, b], acc)
            o_ref[half] = acc

    return pl.pallas_call(
        body, grid=(3, N_HEADS), name="bias_tables",
        in_specs=[pl.BlockSpec(memory_space=pltpu.SMEM),
                  pl.BlockSpec((None, 2, BLK, BLK), lambda p, h: (p, 0, 0, 0))],
        out_specs=pl.BlockSpec((None, None, 2, BLK, BLK), lambda p, h: (p, h, 0, 0, 0)),
        out_shape=SDS((3, N_HEADS, 2, BLK, BLK), F32), compiler_params=_cp("parallel", "parallel"))(rel_bias, bkt)


def bias_tables_bwd(ds_sum):
    bkt = jnp.asarray(_bucket_tables())

    def body(ds_ref, bkt_ref, o_ref):
        h = pl.program_id(0)

        @pl.when(h == 0)
        def _():
            o_ref[...] = jnp.zeros_like(o_ref)

        rows = lax.broadcasted_iota(jnp.int32, (N_HEADS, N_BUCKETS), 0)
        cols = lax.broadcasted_iota(jnp.int32, (N_HEADS, N_BUCKETS), 1)
        acc = jnp.zeros((N_HEADS, N_BUCKETS), F32)
        for b in range(N_BUCKETS):
            tot = jnp.zeros((1, 1), F32)
            for p in range(3):
                for half in range(2):
                    t = jnp.where(bkt_ref[p, half] == b, ds_ref[p, half], 0.0)
                    tot = tot + jnp.sum(_rsum(t), axis=0, keepdims=True)
            acc = acc + jnp.where((rows == h) & (cols == b), tot, 0.0)
        o_ref[...] += acc

    return pl.pallas_call(
        body, grid=(N_HEADS,), name="bias_tables_bwd",
        in_specs=[pl.BlockSpec((3, None, 2, BLK, BLK), lambda h: (0, h, 0, 0, 0)),
                  pl.BlockSpec((3, 2, BLK, BLK), lambda h: (0, 0, 0, 0))],
        out_specs=pl.BlockSpec((N_HEADS, N_BUCKETS), lambda h: (0, 0)),
        out_shape=SDS((N_HEADS, N_BUCKETS), F32), compiler_params=_cp("arbitrary"))(ds_sum, bkt)


def _rows(start, dil):
    if dil == 1:
        return pl.ds(pl.multiple_of(start, BLK), BLK)
    return pl.ds(start, BLK, stride=dil)


GRP = 4


def _group_rows(i, dil):
    t = lax.broadcasted_iota(jnp.int32, (GRP, 1, 1), 0)
    if dil == 1:
        first = i * GRP
        cur = [(first + j) * BLK for j in range(GRP)]
        prv = [jnp.maximum(first + j - 1, 0) * BLK for j in range(GRP)]
        pen = jnp.where(first + t > 0, 0.0, NEG)
    else:
        per_b = dil // GRP
        b, r0 = i // per_b, (i % per_b) * GRP
        cur = [b * (BLK * dil) + r0 + j for j in range(GRP)]
        prv = [jnp.maximum(b - 1, 0) * (BLK * dil) + r0 + j for j in range(GRP)]
        pen = jnp.where(b + 0 * t > 0, 0.0, NEG)
    return [_rows(s, dil) for s in cur], [_rows(s, dil) for s in prv], pen


def _load_group(ref, rows):
    return jnp.stack([ref[r, :] for r in rows])


def _bdot_nt(a, b):
    return lax.dot_general(a, b, (((2,), (2,)), ((0,), (0,))), preferred_element_type=F32)


def _bdot(a, b):
    return lax.dot_general(a, b, (((2,), (1,)), ((0,), (0,))), preferred_element_type=F32)


def _bdot_tn(a, b):
    return lax.dot_general(a, b, (((1,), (1,)), ((0,), (0,))), preferred_element_type=F32)


def _lsum(x):
    return jnp.sum(x, axis=-1, keepdims=True)


def attn_fwd(qkv, bias, job=None):
    S = qkv.shape[3]
    nblk = S // BLK
    car = Carried(job, 2, 2, 1, 4)

    def body(*refs):
        (qkv_ref, bias_ref, o_ref, lse_ref, l_ref), jrefs = car.split(refs)
        car.phase(0, pl.program_id(0), jrefs)
        lane = lax.broadcasted_iota(jnp.int32, (GRP, BLK, BLK), 2)
        lo = lane < 64
        masks = (lo, jnp.logical_not(lo))
        q_ref, k_ref, v_ref = qkv_ref.at[0], qkv_ref.at[1], qkv_ref.at[2]
        for p, (_, dil) in enumerate(PATTERNS):
            def step(i, carry, p=p, dil=dil):
                rc, rp, pen = _group_rows(i, dil)
                q2 = _load_group(q_ref, rc) * 0.125
                kc = _load_group(k_ref, rc).astype(BF16)
                kp = _load_group(k_ref, rp).astype(BF16)
                vc = _load_group(v_ref, rc)
                vp = _load_group(v_ref, rp)
                if p > 0:
                    acc_old, m_old, l_old = _load_group(o_ref, rc), _load_group(lse_ref, rc), _load_group(l_ref, rc)
                pv = jnp.zeros((GRP, BLK, BLK), F32)
                m_new, l_new, alpha = [], [], []
                for h in range(2):
                    qh = jnp.where(masks[h], q2, 0.0).astype(BF16)
                    sp = _bdot_nt(qh, kp) + bias_ref[p, h, 0][None] + pen
                    sc = _bdot_nt(qh, kc) + bias_ref[p, h, 1][None]
                    mn = jnp.max(jnp.maximum(sp, sc), axis=-1, keepdims=True)
                    if p > 0:
                        mo = m_old[:, :, 64 * h:64 * h + 1]
                        mn = jnp.maximum(mn, mo)
                        al = jnp.exp(mo - mn)
                        alpha.append(al)
                    pp = jnp.exp(sp - mn)
                    pc = jnp.exp(sc - mn)
                    ln = _lsum(pp + pc)
                    if p > 0:
                        ln = ln + al * l_old[:, :, 64 * h:64 * h + 1]
                    vhp = jnp.where(masks[h], vp, 0.0).astype(BF16)
                    vhc = jnp.where(masks[h], vc, 0.0).astype(BF16)
                    pv = pv + _bdot(pp.astype(BF16), vhp) + _bdot(pc.astype(BF16), vhc)
                    m_new.append(mn)
                    l_new.append(ln)
                if p > 0:
                    pv = pv + acc_old * jnp.where(lo, alpha[0], alpha[1])
                m_t = jnp.where(lo, m_new[0], m_new[1])
                l_t = jnp.where(lo, l_new[0], l_new[1])
                for j, r in enumerate(rc):
                    o_ref[r, :] = pv[j]
                    lse_ref[r, :] = m_t[j]
                    l_ref[r, :] = l_t[j]
                return carry

            lax.fori_loop(0, nblk // GRP, step, 0)

        def fin(i, carry):
            rows = pl.ds(pl.multiple_of(i * 512, 512), 512)
            lv = l_ref[rows, :]
            o_ref[rows, :] = o_ref[rows, :] / lv
            lse_ref[rows, :] = lse_ref[rows, :] + jnp.log(lv)
            return carry

        lax.fori_loop(0, S // 512, fin, 0)
        car.phase(1, pl.program_id(0), jrefs)
        car.phase(2, pl.program_id(0), jrefs)

    outs = pl.pallas_call(
        body, grid=(4,), name="attn_fwd",
        in_specs=[pl.BlockSpec((None, 3, None, S, BLK), lambda g: (0, 0, g, 0, 0)),
                  pl.BlockSpec((3, 2, 2, BLK, BLK), lambda g: (0, g, 0, 0, 0))] + car.in_specs(),
        out_specs=[pl.BlockSpec((None, S, BLK), lambda g: (g, 0, 0)),
                   pl.BlockSpec((None, S, BLK), lambda g: (g, 0, 0))] + car.out_specs(),
        out_shape=[SDS((4, S, BLK), F32), SDS((4, S, BLK), F32)] + car.out_shapes(),
        input_output_aliases=car.aliases(),
        scratch_shapes=[pltpu.VMEM((S, BLK), F32)] + car.sems(), compiler_params=_cp("arbitrary"))(qkv, bias, *car.args())
    if job is not None:
        job.landed = list(outs[2:])
    return outs[0], outs[1]


def attn_bwd(qkv, bias, o, lse, do, job=None):
    S = qkv.shape[3]
    nblk = S // BLK
    car = Carried(job, 5, 2, 0, 4)

    def body(*refs):
        (qkv_ref, bias_ref, o_ref, lse_ref, do_ref, dqkv_ref, ds_ref), jrefs = car.split(refs)
        car.phase(0, pl.program_id(0), jrefs)
        lane = lax.broadcasted_iota(jnp.int32, (GRP, BLK, BLK), 2)
        lo = lane < 64
        masks = (lo, jnp.logical_not(lo))
        dqkv_ref[...] = jnp.zeros_like(dqkv_ref)
        ds_ref[...] = jnp.zeros_like(ds_ref)
        q_ref, k_ref, v_ref = qkv_ref.at[0], qkv_ref.at[1], qkv_ref.at[2]
        dq_ref, dk_ref, dv_ref = dqkv_ref.at[0], dqkv_ref.at[1], dqkv_ref.at[2]
        for p, (_, dil) in enumerate(PATTERNS):
            def step(i, carry, p=p, dil=dil):
                rc, rp, pen = _group_rows(i, dil)
                q2 = _load_group(q_ref, rc) * 0.125
                kc = _load_group(k_ref, rc).astype(BF16)
                kp = _load_group(k_ref, rp).astype(BF16)
                vc = _load_group(v_ref, rc).astype(BF16)
                vp = _load_group(v_ref, rp).astype(BF16)
                dot = _load_group(do_ref, rc)
                lset = _load_group(lse_ref, rc)
                prod = dot * _load_group(o_ref, rc)
                zero = jnp.zeros((GRP, BLK, BLK), F32)
                dq, dkc, dkp, dvc, dvp = zero, zero, zero, zero, zero
                for h in range(2):
                    qh = jnp.where(masks[h], q2, 0.0).astype(BF16)
                    doh = jnp.where(masks[h], dot, 0.0).astype(BF16)
                    delta = _lsum(jnp.where(masks[h], prod, 0.0))
                    lh = lset[:, :, 64 * h:64 * h + 1]
                    pp = jnp.exp(_bdot_nt(qh, kp) + bias_ref[p, h, 0][None] + pen - lh)
                    pc = jnp.exp(_bdot_nt(qh, kc) + bias_ref[p, h, 1][None] - lh)
                    dsp = pp * (_bdot_nt(doh, vp) - delta)
                    dsc = pc * (_bdot_nt(doh, vc) - delta)
                    ds_ref[p, h, 0] += jnp.sum(dsp, axis=0)
                    ds_ref[p, h, 1] += jnp.sum(dsc, axis=0)
                    dspb, dscb = dsp.astype(BF16), dsc.astype(BF16)
                    dq = dq + jnp.where(masks[h], _bdot(dspb, kp) + _bdot(dscb, kc), 0.0)
                    dkp = dkp + _bdot_tn(dspb, qh)
                    dkc = dkc + _bdot_tn(dscb, qh)
                    dvp = dvp + _bdot_tn(pp.astype(BF16), doh)
                    dvc = dvc + _bdot_tn(pc.astype(BF16), doh)
                for j in range(GRP):
                    dq_ref[rc[j], :] += dq[j] * 0.125
                    dk_ref[rp[j], :] += dkp[j]
                    dk_ref[rc[j], :] += dkc[j]
                    dv_ref[rp[j], :] += dvp[j]
                    dv_ref[rc[j], :] += dvc[j]
                return carry

            lax.fori_loop(0, nblk // GRP, step, 0)
        car.phase(1, pl.program_id(0), jrefs)
        car.phase(2, pl.program_id(0), jrefs)

    blk = pl.BlockSpec((None, S, BLK), lambda g: (g, 0, 0))
    outs = pl.pallas_call(
        body, grid=(4,), name="attn_bwd",
        in_specs=[pl.BlockSpec((None, 3, None, S, BLK), lambda g: (0, 0, g, 0, 0)),
                  pl.BlockSpec((3, 2, 2, BLK, BLK), lambda g: (0, g, 0, 0, 0)), blk, blk, blk] + car.in_specs(),
        out_specs=[pl.BlockSpec((None, 3, None, S, BLK), lambda g: (0, 0, g, 0, 0)),
                   pl.BlockSpec((3, 2, 2, BLK, BLK), lambda g: (0, g, 0, 0, 0))] + car.out_specs(),
        out_shape=[SDS((2, 3, 4, S, BLK), F32), SDS((3, N_HEADS, 2, BLK, BLK), F32)] + car.out_shapes(),
        input_output_aliases=car.aliases(), scratch_shapes=car.sems(),
        compiler_params=_cp("arbitrary"))(qkv, bias, o, lse, do, *car.args())
    if job is not None:
        job.landed = list(outs[2:])
    return outs[0], outs[1]


def _shift_down(u, k, halo):
    n = u.shape[0]
    row = lax.broadcasted_iota(jnp.int32, u.shape, 0)
    out = pltpu.roll(u, k, 0)
    hn = halo.shape[0]
    for j in range(k):
        out = jnp.where(row == j, halo[hn - k + j:hn - k + j + 1, :], out)
    return out


def _shift_up(u, k, halo):
    n = u.shape[0]
    row = lax.broadcasted_iota(jnp.int32, u.shape, 0)
    out = pltpu.roll(u, n - k, 0)
    for j in range(k):
        out = jnp.where(row == n - k + j, halo[j:j + 1, :], out)
    return out


def _conv3(u, halo, w0, w1, w2):
    return _shift_down(u, 2, halo) * w0 + _shift_down(u, 1, halo) * w1 + u * w2


def _colsum(x):
    return jnp.sum(x, axis=0, keepdims=True)


def _mixer_specs(S, tm):
    conv = pl.BlockSpec((None, 12, tm, BLK), lambda i: (1, 0, i, 0))
    halo = pl.BlockSpec((None, 12, 8, BLK), lambda i: (1, 0, jnp.maximum(i * (tm // 8) - 1, 0), 0))
    ob = pl.BlockSpec((4, tm, BLK), lambda i: (0, i, 0))
    return conv, halo, ob


def _mixer_recompute(i, o_ref, pr_ref, ph_ref, w_ref):
    ob = [o_ref[q] for q in range(4)]
    gb = [pr_ref[q] for q in range(4)]
    gc = [pr_ref[4 + q] for q in range(4)]
    xi = [pr_ref[8 + q] for q in range(4)]
    keep = jnp.where(i > 0, 1.0, 0.0)
    u = [gc[q] * xi[q] for q in range(4)]
    hu = [ph_ref[4 + q] * ph_ref[8 + q] * keep for q in range(4)]
    w = [[w_ref[k:k + 1, q * BLK:(q + 1) * BLK] for k in range(3)] for q in range(4)]
    cv = [_conv3(u[q], hu[q], *w[q]) for q in range(4)]
    return ob, gb, gc, xi, u, hu, cv, w


def _rms_blocks(blocks):
    ss = sum(_rsum(b * b) for b in blocks)
    return lax.rsqrt(ss / (BLK * len(blocks)) + EPS)


def mixer_post_fwd(o, proj, w_sc, g_a, g_c):
    S = o.shape[1]
    tm = 256

    def body(o_ref, pr_ref, ph_ref, w_ref, ga_ref, gc_ref, m_ref):
        i = pl.program_id(0)
        ob, gb, _, _, _, _, cv, _ = _mixer_recompute(i, o_ref, pr_ref, ph_ref, w_ref)
        conv = [gb[q] * cv[q] for q in range(4)]
        ra, rc = _rms_blocks(ob), _rms_blocks(conv)
        for q in range(4):
            sl = slice(q * BLK, (q + 1) * BLK)
            m_ref[:, q * BLK:(q + 1) * BLK] = (ob[q] * ra * ga_ref[:, sl]).astype(BF16)
            m_ref[:, ATTN_W + q * BLK:ATTN_W + (q + 1) * BLK] = (conv[q] * rc * gc_ref[:, sl]).astype(BF16)

    conv_s, halo_s, ob_s = _mixer_specs(S, tm)
    full = lambda r, c: pl.BlockSpec((r, c), lambda i: (0, 0))
    return pl.pallas_call(
        body, grid=(S // tm,), name="mixer_post_fwd",
        in_specs=[ob_s, conv_s, halo_s, full(3, 512), full(1, 512), full(1, 512)],
        out_specs=pl.BlockSpec((tm, D_MODEL), lambda i: (i, 0)), out_shape=SDS((S, D_MODEL), BF16),
        compiler_params=_cp("parallel"))(o, proj, proj, w_sc, g_a, g_c)


def mixer_post_bwd_a(dmixed, o, proj, w_sc, g_a, g_c, job=None):
    S = o.shape[1]
    tm = 256

    def body(dm_ref, o_ref, pr_ref, ph_ref, w_ref, ga_ref, gc_ref, do_ref, dgb_ref, dcv_ref, dga_ref, dgc_ref):
        i = pl.program_id(0)
        ob, gb, _, _, _, _, cv, _ = _mixer_recompute(i, o_ref, pr_ref, ph_ref, w_ref)
        conv = [gb[q] * cv[q] for q in range(4)]
        ra, rc = _rms_blocks(ob), _rms_blocks(conv)

        @pl.when(i == 0)
        def _():
            dga_ref[...] = jnp.zeros_like(dga_ref)
            dgc_ref[...] = jnp.zeros_like(dgc_ref)

        for blocks, r, g_ref, off, dg_ref, is_attn in ((ob, ra, ga_ref, 0, dga_ref, True), (conv, rc, gc_ref, ATTN_W, dgc_ref, False)):
            xh = [blocks[q] * r for q in range(4)]
            dm = [dm_ref[:, off + q * BLK:off + (q + 1) * BLK].astype(F32) for q in range(4)]
            gd = [dm[q] * g_ref[:, q * BLK:(q + 1) * BLK] for q in range(4)]
            mean = sum(_rsum(gd[q] * xh[q]) for q in range(4)) / (4 * BLK)
            for q in range(4):
                dg_ref[:, q * BLK:(q + 1) * BLK] += _colsum(dm[q] * xh[q])
                dx = r * (gd[q] - xh[q] * mean)
                if is_attn:
                    do_ref[q] = dx
                else:
                    dgb_ref[q] = dx * cv[q]
                    dcv_ref[q] = dx * gb[q]

    conv_s, halo_s, ob_s = _mixer_specs(S, tm)
    full = lambda r, c: pl.BlockSpec((r, c), lambda i: (0, 0))
    return _pcall(
        body, [dmixed, o, proj, proj, w_sc, g_a, g_c], grid=(S // tm,), name="mixer_post_bwd_a",
        in_specs=[pl.BlockSpec((tm, D_MODEL), lambda i: (i, 0)), ob_s, conv_s, halo_s, full(3, 512), full(1, 512), full(1, 512)],
        out_specs=[ob_s, ob_s, ob_s, full(1, 512), full(1, 512)],
        out_shape=[SDS((4, S, BLK), F32)] * 3 + [SDS((1, 512), F32)] * 2, sem=("arbitrary",), job=job)


def mixer_post_bwd_b(dproj, dgb, dcv, proj, w_sc):
    S = proj.shape[2]
    tm = 256
    last = S // 8 - 1

    def body(dp_in, dgb_ref, dcv_ref, dn_ref, pr_ref, w_ref, dp_ref, dw_ref):
        i = pl.program_id(0)
        keep_next = jnp.where(i < pl.num_programs(0) - 1, 1.0, 0.0)

        @pl.when(i == 0)
        def _():
            dw_ref[...] = jnp.zeros_like(dw_ref)

        for q in range(4):
            sl = slice(q * BLK, (q + 1) * BLK)
            gc, xi = pr_ref[4 + q], pr_ref[8 + q]
            u = gc * xi
            dcv = dcv_ref[q]
            dn = dn_ref[q] * keep_next
            d1, d2 = _shift_up(dcv, 1, dn), _shift_up(dcv, 2, dn)
            w0, w1, w2 = (w_ref[k:k + 1, sl] for k in range(3))
            du = dcv * w2 + d1 * w1 + d2 * w0
            dw_ref[0:1, sl] += _colsum(d2 * u)
            dw_ref[1:2, sl] += _colsum(d1 * u)
            dw_ref[2:3, sl] += _colsum(dcv * u)
            dp_ref[q] = dgb_ref[q]
            dp_ref[4 + q] = du * xi
            dp_ref[8 + q] = du * gc

    conv_s, _, ob_s = _mixer_specs(S, tm)
    nxt = pl.BlockSpec((4, 8, BLK), lambda i: (0, jnp.minimum((i + 1) * (tm // 8), last), 0))
    full = lambda r, c: pl.BlockSpec((r, c), lambda i: (0, 0))
    return pl.pallas_call(
        body, grid=(S // tm,), name="mixer_post_bwd_b",
        in_specs=[pl.BlockSpec(memory_space=pl.ANY), ob_s, ob_s, nxt, conv_s, full(3, 512)],
        out_specs=[conv_s, full(3, 512)],
        out_shape=[SDS(dproj.shape, F32), SDS((3, 512), F32)],
        input_output_aliases={0: 0}, compiler_params=_cp("arbitrary"))(dproj, dgb, dcv, dcv, proj, w_sc)


def xattn_fwd(q, k, v):
    S = q.shape[0]
    tm = 512
    scale = MEM_HD ** -0.5

    def body(q_ref, k_ref, v_ref, o_ref):
        for h in range(N_MEM_HEADS):
            sl = slice(h * MEM_HD, (h + 1) * MEM_HD)
            s = _dot_nt(q_ref[:, sl], k_ref[:, sl]) * scale
            p = jnp.exp(s - jnp.max(s, axis=1, keepdims=True))
            p = p / _rsum(p)
            o_ref[:, sl] = _dot(p.astype(BF16), v_ref[:, sl]).astype(BF16)

    row = pl.BlockSpec((tm, D_MODEL), lambda i: (i, 0))
    kv = pl.BlockSpec(k.shape, lambda i: (0, 0))
    return pl.pallas_call(body, grid=(S // tm,), name="xattn_fwd", in_specs=[row, kv, kv], out_specs=row,
                          out_shape=SDS((S, D_MODEL), BF16), compiler_params=_cp("parallel"))(q, k, v)


def xattn_bwd(q, k, v, do):
    S = q.shape[0]
    tm = 512
    scale = MEM_HD ** -0.5

    def body(q_ref, k_ref, v_ref, do_ref, dq_ref, dk_ref, dv_ref):
        @pl.when(pl.program_id(0) == 0)
        def _():
            dk_ref[...] = jnp.zeros_like(dk_ref)
            dv_ref[...] = jnp.zeros_like(dv_ref)

        for h in range(N_MEM_HEADS):
            sl = slice(h * MEM_HD, (h + 1) * MEM_HD)
            qh, kh, vh, doh = q_ref[:, sl], k_ref[:, sl], v_ref[:, sl], do_ref[:, sl]
            s = _dot_nt(qh, kh) * scale
            p = jnp.exp(s - jnp.max(s, axis=1, keepdims=True))
            p = p / _rsum(p)
            dp = _dot_nt(doh, vh)
            ds = (p * (dp - _rsum(p * dp)) * scale).astype(BF16)
            dq_ref[:, sl] = _dot(ds, kh).astype(BF16)
            dk_ref[:, sl] += _dot_tn(ds, qh)
            dv_ref[:, sl] += _dot_tn(p.astype(BF16), doh)

    row = pl.BlockSpec((tm, D_MODEL), lambda i: (i, 0))
    kv = pl.BlockSpec(k.shape, lambda i: (0, 0))
    return pl.pallas_call(body, grid=(S // tm,), name="xattn_bwd", in_specs=[row, kv, kv, row], out_specs=[row, kv, kv],
                          out_shape=[SDS((S, D_MODEL), BF16), SDS(k.shape, F32), SDS(k.shape, F32)],
                          compiler_params=_cp("arbitrary"))(q, k, v, do)


FFN_TM, FFN_TC = 512, 256


def _ffn_specs(S, order):
    tm, tc = FFN_TM, FFN_TC
    ij = (lambda a, b: (a, b)) if order == "ij" else (lambda a, b: (b, a))
    blk = pl.BlockSpec((2, tm, tc), lambda a, b: (0,) + ij(a, b))
    prev = pl.BlockSpec((2, 16, tc), lambda a, b: (0, jnp.maximum(ij(a, b)[0] * (tm // 16) - 1, 0), ij(a, b)[1]))
    nxt = pl.BlockSpec((2, 16, tc), lambda a, b: (0, jnp.minimum((ij(a, b)[0] + 1) * (tm // 16), S // 16 - 1), ij(a, b)[1]))
    wsp = pl.BlockSpec((2, 3, tc), lambda a, b: (0, 0, ij(a, b)[1]))
    bsp = pl.BlockSpec((2, 1, tc), lambda a, b: (0, 0, ij(a, b)[1]))
    act = pl.BlockSpec((tm, tc), lambda a, b: ij(a, b))
    return blk, prev, nxt, wsp, bsp, act


def _ffn_up(i, up_ref, uh_ref, w_ref, b_ref):
    keep = jnp.where(i > 0, 1.0, 0.0)
    out = []
    for half in range(2):
        u = up_ref[half].astype(F32)
        hu = uh_ref[half].astype(F32) * keep
        w0, w1, w2 = (w_ref[half, k:k + 1, :] for k in range(3))
        out.append(_conv3(u, hu, w0, w1, w2) + b_ref[half])
    return out


def ffn_act_fwd(up_pre, w_fc, b_fc):
    S = up_pre.shape[1]

    def body(up_ref, uh_ref, w_ref, b_ref, a_ref, gv_ref):
        g, v = _ffn_up(pl.program_id(0), up_ref, uh_ref, w_ref, b_ref)
        a_ref[...] = (g * jax.nn.sigmoid(g) * v).astype(BF16)
        gv_ref[0] = g.astype(BF16)
        gv_ref[1] = v.astype(BF16)

    blk, prev, _, wsp, bsp, act = _ffn_specs(S, "ij")
    return pl.pallas_call(body, grid=(S // FFN_TM, D_FF // FFN_TC), name="ffn_act_fwd",
                          in_specs=[blk, prev, wsp, bsp], out_specs=[act, blk],
                          out_shape=[SDS((S, D_FF), BF16), SDS((2, S, D_FF), BF16)],
                          compiler_params=_cp("parallel", "parallel"))(up_pre, up_pre, w_fc, b_fc)


def ffn_act_bwd_a(dact, up, job=None):
    S = up.shape[1]

    def body(da_ref, up_ref, du_ref, db_ref):
        i = pl.program_id(1)
        g, v = up_ref[0].astype(F32), up_ref[1].astype(F32)
        da = da_ref[...].astype(F32)
        sg = jax.nn.sigmoid(g)
        dg = da * v * (sg * (1.0 + g * (1.0 - sg)))
        dv = da * g * sg

        @pl.when(i == 0)
        def _():
            db_ref[...] = jnp.zeros_like(db_ref)

        du_ref[0] = dg.astype(BF16)
        du_ref[1] = dv.astype(BF16)
        db_ref[0] += _colsum(dg)
        db_ref[1] += _colsum(dv)

    blk, _, _, _, bsp, act = _ffn_specs(S, "ji")
    return _pcall(body, [dact, up], grid=(D_FF // FFN_TC, S // FFN_TM), name="ffn_act_bwd_a",
                  in_specs=[act, blk], out_specs=[blk, bsp],
                  out_shape=[SDS((2, S, D_FF), BF16), SDS((2, 1, D_FF), F32)], sem=("parallel", "arbitrary"), job=job)


def ffn_act_bwd_b(d_up, up_pre, w_fc):
    S = up_pre.shape[1]

    def body(du_ref, dn_ref, up_ref, w_ref, dp_ref, dw_ref):
        i = pl.program_id(1)
        keep_next = jnp.where(i < pl.num_programs(1) - 1, 1.0, 0.0)

        @pl.when(i == 0)
        def _():
            dw_ref[...] = jnp.zeros_like(dw_ref)

        for half in range(2):
            d = du_ref[half].astype(F32)
            dn = dn_ref[half].astype(F32) * keep_next
            u = up_ref[half].astype(F32)
            d1, d2 = _shift_up(d, 1, dn), _shift_up(d, 2, dn)
            w0, w1, w2 = (w_ref[half, k:k + 1, :] for k in range(3))
            dp_ref[half] = (d * w2 + d1 * w1 + d2 * w0).astype(BF16)
            dw_ref[half, 0:1, :] += _colsum(d2 * u)
            dw_ref[half, 1:2, :] += _colsum(d1 * u)
            dw_ref[half, 2:3, :] += _colsum(d * u)

    blk, _, nxt, wsp, _, _ = _ffn_specs(S, "ji")
    return pl.pallas_call(body, grid=(D_FF // FFN_TC, S // FFN_TM), name="ffn_act_bwd_b",
                          in_specs=[blk, nxt, blk, wsp], out_specs=[blk, wsp],
                          out_shape=[SDS((2, S, D_FF), BF16), SDS((2, 3, D_FF), F32)],
                          compiler_params=_cp("parallel", "arbitrary"))(d_up, d_up, up_pre, w_fc)


def final_loss(x3, g, target):
    S, Dm = x3.shape
    tm = 512

    def body(x_ref, g_ref, t_ref, loss_ref, dx_ref, dg_ref):
        i = pl.program_id(0)
        xv = x_ref[...]
        r = lax.rsqrt(jnp.mean(xv * xv, axis=1, keepdims=True) + EPS)
        xh = xv * r
        err = xh * g_ref[...] - t_ref[...]
        dy = err / Dm
        gd = dy * g_ref[...]
        dx_ref[...] = r * (gd - xh * jnp.mean(gd * xh, axis=1, keepdims=True))

        @pl.when(i == 0)
        def _():
            dg_ref[...] = jnp.zeros_like(dg_ref)
            loss_ref[...] = jnp.zeros_like(loss_ref)

        dg_ref[...] += _colsum(dy * xh)
        loss_ref[...] += 0.5 * _colsum(jnp.mean(err * err, axis=1, keepdims=True))

    row = pl.BlockSpec((tm, Dm), lambda i: (i, 0))
    vec = pl.BlockSpec((1, Dm), lambda i: (0, 0))
    one = pl.BlockSpec((1, 1), lambda i: (0, 0))
    return pl.pallas_call(body, grid=(S // tm,), name="final_loss", in_specs=[row, vec, row], out_specs=[one, row, vec],
                          out_shape=[SDS((1, 1), F32), SDS((S, Dm), F32), SDS((1, Dm), F32)],
                          compiler_params=_cp("arbitrary"))(x3, g, target)


def local_step(x, mem, target, wb, ws, comm=None):
    S = x.shape[0]
    assert S % 2048 == 0
    if comm is None:
        comm = NoComm()
    else:
        wb = comm.gather_first()
    w_fc = ws["w_ffn_conv"].reshape(3, 2, D_FF).transpose(1, 0, 2)
    b_fc = ws["b_ffn_conv"].reshape(2, 1, D_FF)

    bias = bias_tables(ws["rel_bias"])
    h1 = rmsnorm_fwd(x, ws["g_mix"], "norm_mix")
    proj = mm_nn(h1, wb["w_in"], "proj", out_bw=BLK, tn=768, job=comm.job("proj"))
    comm.landed("proj", wb)
    qkv = proj.reshape(2, 3, 4, S, BLK)
    o, lse = attn_fwd(qkv, bias, job=comm.job("attn_fwd"))
    comm.landed("attn_fwd", wb)
    proj4 = proj.reshape(2, 12, S, BLK)
    mixed = mixer_post_fwd(o, proj4, ws["w_short_conv"], ws["g_attn_out"], ws["g_conv_out"])
    x1 = mm_nn(mixed, wb["w_out"], "out_proj", res=x)
    h2 = rmsnorm_fwd(x1, ws["g_xattn"], "norm_xattn")
    mem_n = rmsnorm_fwd(mem, ws["g_mem"], "norm_mem")
    xq = mm_nn(h2, wb["w_xq"], "xq", out_dtype=BF16)
    xk = mm_nn(mem_n, wb["w_xk"], "xk", out_dtype=BF16, tn=1024)
    xv = mm_nn(mem_n, wb["w_xv"], "xv", out_dtype=BF16, tn=1024)
    xo = xattn_fwd(xq, xk, xv)
    x2 = mm_nn(xo, wb["w_xo"], "xo_proj", res=x1)
    h3 = rmsnorm_fwd(x2, ws["g_ffn"], "norm_ffn")
    up_pre = mm_nn(h3, wb["w_up"], "up_proj", out_dtype=BF16, out_bw=D_FF, tn=1408)
    act, up = ffn_act_fwd(up_pre, w_fc, b_fc)
    x3 = mm_nn(act, wb["w_down"], "down_proj", res=x2, tm=512)
    loss, dx3, dg_final = final_loss(x3, ws["g_final"].reshape(1, -1), target)

    gb, gs = {}, {"g_final": dg_final}
    gb["w_down"] = mm_tn(act, dx3, None, "dw_down", tm=1408, tn=1024)
    dact = mm_nt(dx3, None, wb["w_down"], "d_act", out_dtype=BF16, tn=1408)
    d_up, db_fc = ffn_act_bwd_a(dact, up)
    d_up_pre, dw_fc = ffn_act_bwd_b(d_up, up_pre, w_fc)
    gs["b_ffn_conv"] = db_fc.reshape(1, 2 * D_FF)
    gs["w_ffn_conv"] = dw_fc.transpose(1, 0, 2).reshape(3, 2 * D_FF)
    gb["w_up"] = mm_tn(h3, d_up_pre, D_FF, "dw_up", shards=4, tn=1408)
    dh3 = mm_nt(d_up_pre, D_FF, wb["w_up"], "d_h3", tm=512, job=comm.job("d_h3", gb))
    comm.landed("d_h3")
    dx2, gs["g_ffn"] = rmsnorm_bwd(x2, ws["g_ffn"], dh3, dx3, "norm_ffn_bwd")
    gb["w_xo"] = mm_tn(xo, dx2, None, "dw_xo")
    dxo = mm_nt(dx2, None, wb["w_xo"], "d_xo", out_dtype=BF16)
    dxq, dxk, dxv = xattn_bwd(xq, xk, xv, dxo)
    gb["w_xq"] = mm_tn(h2, dxq, None, "dw_xq")
    gb["w_xk"] = mm_tn(mem_n, dxk, None, "dw_xk", tn=1024)
    gb["w_xv"] = mm_tn(mem_n, dxv, None, "dw_xv", tn=1024)
    dh2 = mm_nt(dxq, None, wb["w_xq"], "d_h2")
    dmem_n = mm_nt(dxk, None, wb["w_xk"], "d_memk", tn=1024)
    dmem_n = mm_nt(dxv, None, wb["w_xv"], "d_memv", res=dmem_n, tn=1024)
    gs["g_mem"] = rmsnorm_bwd(mem, ws["g_mem"], dmem_n, None, "norm_mem_bwd")
    dx1, gs["g_xattn"] = rmsnorm_bwd(x1, ws["g_xattn"], dh2, dx2, "norm_xattn_bwd")
    gb["w_out"] = mm_tn(mixed, dx1, None, "dw_out")
    dmixed = mm_nt(dx1, None, wb["w_out"], "d_mixed")
    do, dgb, dcv, gs["g_attn_out"], gs["g_conv_out"] = mixer_post_bwd_a(
        dmixed, o, proj4, ws["w_short_conv"], ws["g_attn_out"], ws["g_conv_out"], job=comm.job("mixer_post_bwd_a", gb))
    comm.landed("mixer_post_bwd_a")
    dproj, ds_sum = attn_bwd(qkv, bias, o, lse, do, job=comm.job("attn_bwd"))
    comm.landed("attn_bwd")
    gs["rel_bias"] = bias_tables_bwd(ds_sum)
    dproj, gs["w_short_conv"] = mixer_post_bwd_b(dproj.reshape(2, 12, S, BLK), dgb, dcv, proj4, ws["w_short_conv"])
    dproj = dproj.reshape(24, S, BLK)
    gb["w_in"] = mm_tn(h1, dproj, BLK, "dw_in", shards=4, tn=768, job=comm.job("dw_in"))
    comm.landed("dw_in")
    dh1 = mm_nt(dproj, BLK, wb["w_in"], "d_h1", tm=512, job=comm.job("d_h1", gb))
    comm.landed("d_h1")
    grad_x, gs["g_mix"] = rmsnorm_bwd(x, ws["g_mix"], dh1, dx1, "norm_mix_bwd", job=comm.job("norm_mix_bwd"))
    comm.landed("norm_mix_bwd")
    return loss, grad_x, comm.finish(gb), gs


def adamw(w, g, m, v, name):
    R, C = w.shape
    tr = R
    for cand in (256, 352):
        if R > cand and R % cand == 0:
            tr = cand
            break

    def body(w_ref, g_ref, m_ref, v_ref, d_ref, nm_ref, nv_ref):
        gv = g_ref[...]
        mn = ADAM_B1 * m_ref[...] + (1.0 - ADAM_B1) * gv
        vn = ADAM_B2 * v_ref[...] + (1.0 - ADAM_B2) * (gv * gv)
        m_hat = mn / (1.0 - ADAM_B1 ** ADAM_STEP)
        v_hat = vn / (1.0 - ADAM_B2 ** ADAM_STEP)
        d_ref[...] = -ADAM_LR * (m_hat / (jnp.sqrt(v_hat) + ADAM_EPS) + ADAM_WD * w_ref[...])
        nm_ref[...] = mn
        nv_ref[...] = vn

    blk = pl.BlockSpec((tr, C), lambda i: (i, 0))
    return pl.pallas_call(body, grid=(R // tr,), name=name, in_specs=[blk] * 4, out_specs=[blk] * 3,
                          out_shape=[SDS((R, C), F32)] * 3, compiler_params=_cp("parallel"))(w, g, m, v)


BIG = ("w_in", "w_out", "w_xq", "w_xk", "w_xv", "w_xo", "w_up", "w_down")
COL_SHARDED = ("w_in", "w_up")
N_BIG = len(BIG)
ANY = pl.BlockSpec(memory_space=pl.ANY)


def _place():
    x, y, c = lax.axis_index("x"), lax.axis_index("y"), lax.axis_index("c")
    chips = [(1 - x, y), (x, 1 - y), (1 - x, 1 - y)]
    return x, y, c, chips


def _window(full, name, R, C, shard, half):
    r0, nr = (0, R) if half is None else (half * (R // 2), R // 2)
    if name in COL_SHARDED:
        return full.at[pl.ds(r0, nr), pl.ds(shard * C, C)]
    return full.at[pl.ds(shard * R + r0, nr), :]


def place_shard(w, mine_arr, n):
    R, C = w.shape
    col = n in COL_SHARDED

    def body(s_ref, w_ref, o_ref):
        o_ref[...] = w_ref[...].astype(BF16)

    grid_spec = pltpu.PrefetchScalarGridSpec(
        num_scalar_prefetch=1, grid=(1,), in_specs=[pl.BlockSpec((R, C), lambda i, s: (0, 0))],
        out_specs=pl.BlockSpec((R, C), (lambda i, s: (0, s[0])) if col else (lambda i, s: (s[0], 0))))
    return pl.pallas_call(body, grid_spec=grid_spec, name="place_" + n,
                          out_shape=SDS((R, 4 * C) if col else (4 * R, C), BF16),
                          compiler_params=_cp("arbitrary"))(mine_arr, w)


def _gather_jobs(names, shapes):
    nw = len(names)

    def start(full, sems):
        send, recv, fsend, frecv = sems
        x, y, c, chips = _place()
        mine = 2 * x + y
        for w, n in enumerate(names):
            R, C = shapes[w]
            own = _window(full[w], n, R, C, mine, c)
            for j, chip in enumerate(chips):
                pltpu.make_async_remote_copy(src_ref=own, dst_ref=own, send_sem=send.at[w, j], recv_sem=recv.at[w, j],
                                             device_id=(*chip, c), device_id_type=MESH).start()

    def mid(full, sems):
        send, recv, fsend, frecv = sems
        x, y, c, chips = _place()
        sib = (x, y, 1 - c)
        for w, n in enumerate(names):
            R, C = shapes[w]
            for j, chip in enumerate(chips):
                landed = _window(full[w], n, R, C, 2 * chip[0] + chip[1], c)
                pltpu.make_async_remote_copy(src_ref=landed, dst_ref=landed, send_sem=send.at[w, j], recv_sem=recv.at[w, j],
                                             device_id=(*chip, c), device_id_type=MESH).wait_recv()
                pltpu.make_async_remote_copy(src_ref=landed, dst_ref=landed, send_sem=fsend.at[w, j],
                                             recv_sem=frecv.at[w, j], device_id=sib, device_id_type=MESH).start()

    def finish(full, sems):
        send, recv, fsend, frecv = sems
        x, y, c, chips = _place()
        mine = 2 * x + y
        sib = (x, y, 1 - c)
        for w, n in enumerate(names):
            R, C = shapes[w]
            own = _window(full[w], n, R, C, mine, c)
            for j, chip in enumerate(chips):
                landed = _window(full[w], n, R, C, 2 * chip[0] + chip[1], c)
                other = _window(full[w], n, R, C, 2 * chip[0] + chip[1], 1 - c)
                pltpu.make_async_remote_copy(src_ref=other, dst_ref=other, send_sem=fsend.at[w, j], recv_sem=frecv.at[w, j],
                                             device_id=sib, device_id_type=MESH).wait_recv()
                pltpu.make_async_remote_copy(src_ref=own, dst_ref=own, send_sem=send.at[w, j], recv_sem=recv.at[w, j],
                                             device_id=(*chip, c), device_id_type=MESH).wait_send()
                pltpu.make_async_remote_copy(src_ref=landed, dst_ref=landed, send_sem=fsend.at[w, j],
                                             recv_sem=frecv.at[w, j], device_id=sib, device_id_type=MESH).wait_send()

    return start, mid, finish, [pltpu.SemaphoreType.DMA((nw, 3))] * 4


class CommJob:
    def __init__(self, ins, out_shapes, inplace, start, finish, sems, mid=None):
        self.ins, self.out_shapes, self.inplace = list(ins), list(out_shapes), inplace
        self.start, self.mid, self.finish, self.sems = start, mid, finish, list(sems)

    def run(self, phase, ins, outs, sems):
        if phase == 0:
            self.start(ins, outs, sems)
        elif phase == 1:
            if self.mid is not None:
                self.mid(ins, outs, sems)
        else:
            self.finish(ins, outs, sems)


def run_job(name, job):
    n_in, n_out = len(job.ins), len(job.out_shapes)

    def body(*refs):
        ins, outs, sm = refs[:n_in], refs[n_in:n_in + n_out], refs[n_in + n_out:]
        for phase in range(3):
            job.run(phase, ins, outs, sm)

    return pl.pallas_call(
        body, name=name, in_specs=[ANY] * n_in, out_specs=[ANY] * n_out, out_shape=job.out_shapes,
        input_output_aliases={i: i for i in range(n_in)} if job.inplace else {},
        scratch_shapes=job.sems)(*job.ins)


class Carried:
    def __init__(self, job, n_in, n_out, n_scratch, steps):
        self.job, self.n_in, self.n_out, self.n_scratch, self.steps = job, n_in, n_out, n_scratch, steps
        self.nji = len(job.ins) if job else 0
        self.njo = len(job.out_shapes) if job else 0

    def in_specs(self):
        return [ANY] * self.nji

    def out_specs(self):
        return [ANY] * self.njo

    def out_shapes(self):
        return list(self.job.out_shapes) if self.job else []

    def aliases(self):
        if not (self.job and self.job.inplace):
            return {}
        return {self.n_in + i: self.n_out + i for i in range(self.nji)}

    def sems(self):
        return list(self.job.sems) if self.job else []

    def args(self):
        return list(self.job.ins) if self.job else []

    def split(self, refs):
        a = self.n_in
        b = a + self.nji
        c = b + self.n_out
        d = c + self.njo
        e = d + self.n_scratch
        return refs[:a] + refs[b:c] + refs[d:e], (refs[a:b], refs[c:d], refs[e:])

    def phase(self, phase, step, jrefs):
        if self.job is None:
            return
        at = {0: 0, 1: max(self.steps - 2, 0), 2: self.steps - 1}[phase]

        @pl.when(step == at)
        def _():
            self.job.run(phase, *jrefs)


def _pcall(body, args, *, grid, in_specs, out_specs, out_shape, name, sem, scratch=(), aliases=None, job=None):
    n_in, n_out = len(args), len(out_shape)
    if job is None:
        return pl.pallas_call(
            body, grid=grid, in_specs=list(in_specs), out_specs=list(out_specs), out_shape=list(out_shape), name=name,
            scratch_shapes=list(scratch), input_output_aliases=dict(aliases or {}), compiler_params=_cp(*sem))(*args)
    total = int(np.prod(grid))
    car = Carried(job, n_in, n_out, len(scratch), total)

    def wrapped(*refs):
        main, jrefs = car.split(refs)
        lin = pl.program_id(0)
        for ax in range(1, len(grid)):
            lin = lin * grid[ax] + pl.program_id(ax)
        car.phase(0, lin, jrefs)
        body(*main)
        car.phase(1, lin, jrefs)
        car.phase(2, lin, jrefs)

    res = pl.pallas_call(
        wrapped, grid=grid, in_specs=list(in_specs) + car.in_specs(), out_specs=list(out_specs) + car.out_specs(),
        out_shape=list(out_shape) + car.out_shapes(), name=name, scratch_shapes=list(scratch) + car.sems(),
        input_output_aliases={**dict(aliases or {}), **car.aliases()},
        compiler_params=_cp(*(["arbitrary"] * len(grid))))(*args, *car.args())
    job.landed = list(res[n_out:])
    return list(res[:n_out])


def gather_job(placed, names):
    shapes = []
    for n in names:
        R, C = placed[n].shape
        shapes.append((R, C // 4) if n in COL_SHARDED else (R // 4, C))
    start, mid, finish, sems = _gather_jobs(names, shapes)
    arrays = [placed[n] for n in names]
    return CommJob(arrays, [SDS(a.shape, a.dtype) for a in arrays], True,
                   lambda i, o, s: start(o, s), lambda i, o, s: finish(o, s), sems, mid=lambda i, o, s: mid(o, s))


def pair_exchange_job(grads, names):
    shapes = [grads[n].shape for n in names]

    def copies(ins, outs, sems):
        x, y, c, _ = _place()
        return [pltpu.make_async_remote_copy(
            src_ref=ins[w].at[:, pl.ds((1 - c) * (shapes[w][1] // 2), shapes[w][1] // 2), :], dst_ref=outs[w],
            send_sem=sems[0].at[w], recv_sem=sems[1].at[w], device_id=(x, y, 1 - c), device_id_type=MESH)
            for w in range(len(names))]

    def start(ins, outs, sems):
        for cp in copies(ins, outs, sems):
            cp.start()

    def finish(ins, outs, sems):
        for cp in copies(ins, outs, sems):
            cp.wait()

    return CommJob([grads[n] for n in names], [SDS((4, s[1] // 2, s[2]), F32) for s in shapes], False, start, finish,
                   [pltpu.SemaphoreType.DMA((len(names),))] * 2)


def chip_exchange_job(p16, names):
    shapes = [p16[n].shape for n in names]

    def copies(ins, outs, sems):
        x, y, c, chips = _place()
        return [pltpu.make_async_remote_copy(
            src_ref=ins[w].at[2 * chip[0] + chip[1]], dst_ref=outs[w].at[j],
            send_sem=sems[0].at[w, j], recv_sem=sems[1].at[w, j], device_id=(*chip, c), device_id_type=MESH)
            for w in range(len(names)) for j, chip in enumerate(chips)]

    def start(ins, outs, sems):
        for cp in copies(ins, outs, sems):
            cp.start()

    def finish(ins, outs, sems):
        for cp in copies(ins, outs, sems):
            cp.wait()

    return CommJob([p16[n] for n in names], [SDS((3,) + tuple(s[1:]), BF16) for s in shapes], False, start, finish,
                   [pltpu.SemaphoreType.DMA((len(names), 3))] * 2)


def half_exchange_job(full, names):
    shapes = [full[n].shape for n in names]

    def copies(outs, sems):
        x, y, c, _ = _place()
        cps = []
        for w in range(len(names)):
            Rh = shapes[w][0] // 2
            rows = outs[w].at[pl.ds(c * Rh, Rh), :]
            other = outs[w].at[pl.ds((1 - c) * Rh, Rh), :]
            cps.append((pltpu.make_async_remote_copy(src_ref=rows, dst_ref=rows, send_sem=sems[0].at[w], recv_sem=sems[1].at[w],
                                                     device_id=(x, y, 1 - c), device_id_type=MESH),
                        pltpu.make_async_remote_copy(src_ref=other, dst_ref=other, send_sem=sems[0].at[w], recv_sem=sems[1].at[w],
                                                     device_id=(x, y, 1 - c), device_id_type=MESH)))
        return cps

    def start(ins, outs, sems):
        for send, _ in copies(outs, sems):
            send.start()

    def finish(ins, outs, sems):
        for send, recv in copies(outs, sems):
            recv.wait_recv()
            send.wait_send()

    arrays = [full[n] for n in names]
    return CommJob(arrays, [SDS(a.shape, a.dtype) for a in arrays], True, start, finish,
                   [pltpu.SemaphoreType.DMA((len(names),))] * 2)


def allreduce_small(v, name):
    rows = v.shape[0]

    def body(v_ref, o_ref, buf, send, recv):
        x, y, c, _ = _place()
        me = 4 * x + 2 * y + c
        buf[0] = v_ref[...]
        sends = []
        for k in range(1, 8):
            peer = (x ^ (k >> 2), y ^ ((k >> 1) & 1), c ^ (k & 1))
            cp = pltpu.make_async_remote_copy(src_ref=v_ref, dst_ref=buf.at[k], send_sem=send.at[k - 1], recv_sem=recv.at[k - 1],
                                              device_id=peer, device_id_type=MESH)
            cp.start()
            sends.append(cp)
        for cp in sends:
            cp.wait_recv()
        acc = buf[me]
        for d in range(1, 8):
            acc = acc + buf[d ^ me]
        o_ref[...] = acc
        for cp in sends:
            cp.wait_send()

    vm = pl.BlockSpec(memory_space=pltpu.VMEM)
    return pl.pallas_call(
        body, name=name, in_specs=[vm], out_specs=vm, out_shape=SDS((rows, 128), F32),
        scratch_shapes=[pltpu.VMEM((8, rows, 128), F32), pltpu.SemaphoreType.DMA((7,)), pltpu.SemaphoreType.DMA((7,))],
    )(v)


def pair_sum(g, xh, sel_arr, name):
    _, R, C = g.shape
    Rh = R // 2

    def body(sel_ref, g_ref, x_ref, p32_ref, p16_ref):
        s = g_ref[...] + x_ref[...]
        p16_ref[...] = s.astype(BF16)

        @pl.when(pl.program_id(0) == sel_ref[0])
        def _():
            p32_ref[...] = s

    same = pl.BlockSpec((None, Rh, C), lambda s, sel: (s, 0, 0))
    grid_spec = pltpu.PrefetchScalarGridSpec(
        num_scalar_prefetch=1, grid=(4,),
        in_specs=[pl.BlockSpec((None, Rh, C), lambda s, sel: (s, sel[1], 0)), same],
        out_specs=[pl.BlockSpec((Rh, C), lambda s, sel: (0, 0)), same])
    return pl.pallas_call(body, grid_spec=grid_spec, name=name, out_shape=[SDS((Rh, C), F32), SDS((4, Rh, C), BF16)],
                          compiler_params=_cp("arbitrary"))(sel_arr, g, xh)


def chip_sum(p32, y, sel_arr, name):
    Rh, C = p32.shape

    def body(s_ref, p_ref, y_ref, o_ref):
        o_ref[...] = ((p_ref[...] + y_ref[0].astype(F32)) + y_ref[1].astype(F32)) + y_ref[2].astype(F32)

    grid_spec = pltpu.PrefetchScalarGridSpec(
        num_scalar_prefetch=1, grid=(1,),
        in_specs=[pl.BlockSpec((Rh, C), lambda i, s: (0, 0)), pl.BlockSpec((3, Rh, C), lambda i, s: (0, 0, 0))],
        out_specs=pl.BlockSpec((Rh, C), lambda i, s: (s[1], 0)))
    return pl.pallas_call(body, grid_spec=grid_spec, name=name, out_shape=SDS((2 * Rh, C), F32),
                          compiler_params=_cp("arbitrary"))(sel_arr, p32, y)


class StepComm:
    FIRST = ("w_in",)
    SQUARE = ("w_out", "w_xq", "w_xk", "w_xv", "w_xo")
    FFN = ("w_up", "w_down")
    PLAN = {
        "proj": ("gather", SQUARE), "attn_fwd": ("gather", FFN),
        "d_h3": ("pair", FFN), "mixer_post_bwd_a": ("pair", SQUARE), "attn_bwd": ("chip", FFN + SQUARE),
        "dw_in": ("half", FFN + SQUARE), "d_h1": ("pair", FIRST), "norm_mix_bwd": ("chip", FIRST),
    }

    def __init__(self, placed, sel_arr):
        self.placed, self.sel_arr = placed, sel_arr
        self.g, self.p32, self.p16, self.full, self.done, self.jobs = {}, {}, {}, {}, {}, {}

    def gather_first(self):
        return dict(zip(self.FIRST, run_job("gather_first", gather_job(self.placed, self.FIRST))))

    def job(self, tag, gb=None):
        kind, names = self.PLAN[tag]
        if kind == "gather":
            j = gather_job(self.placed, names)
        elif kind == "pair":
            for n in names:
                self.g[n] = gb[n] if gb[n].ndim == 3 else gb[n].reshape(4, gb[n].shape[0] // 4, gb[n].shape[1])
            j = pair_exchange_job(self.g, names)
        elif kind == "chip":
            j = chip_exchange_job(self.p16, names)
        else:
            j = half_exchange_job(self.full, names)
        self.jobs[tag] = j
        return j

    def landed(self, tag, wb=None):
        kind, names = self.PLAN[tag]
        for n, a in zip(names, self.jobs[tag].landed):
            if kind == "gather":
                wb[n] = a
            elif kind == "pair":
                self.p32[n], self.p16[n] = pair_sum(self.g[n], a, self.sel_arr, "pair_sum_" + n)
            elif kind == "chip":
                self.full[n] = chip_sum(self.p32[n], a, self.sel_arr, "chip_sum_" + n)
            else:
                self.done[n] = a

    def finish(self, gb):
        last = run_job("half_exchange", half_exchange_job(self.full, self.FIRST))
        self.done.update(zip(self.FIRST, last))
        return self.done


class NoComm:
    def job(self, tag, gb=None):
        return None

    def landed(self, tag, wb=None):
        pass

    def finish(self, gb):
        return gb


SMALL = ("rel_bias", "g_mix", "w_short_conv", "g_attn_out", "g_conv_out", "g_xattn", "g_mem", "g_ffn",
         "w_ffn_conv", "b_ffn_conv", "g_final")
SMALL_FULL = {"rel_bias": (8, 32), "g_mix": (1, 1024), "w_short_conv": (3, 512), "g_attn_out": (1, 512), "g_conv_out": (1, 512),
              "g_xattn": (1, 1024), "g_mem": (1, 1024), "g_ffn": (1, 1024), "w_ffn_conv": (3, 5632), "b_ffn_conv": (1, 5632),
              "g_final": (1024,)}
SMALL_SHARDED = ("w_short_conv", "w_ffn_conv")


def _pack(parts):
    flat = jnp.concatenate([p.reshape(-1) for p in parts])
    rows = -(-flat.shape[0] // 1024) * 8
    return jnp.pad(flat, (0, rows * 128 - flat.shape[0])).reshape(rows, 128)


def _unpack(packed, names):
    flat, out, off = packed.reshape(-1), {}, 0
    for n in names:
        size = int(np.prod(SMALL_FULL[n]))
        out[n] = flat[off:off + size].reshape(SMALL_FULL[n])
        off += size
    return out


def kernel(x, mem, rel_bias, g_mix, w_in, w_short_conv, g_attn_out, g_conv_out, w_out, g_xattn, g_mem, w_xq, w_xk, w_xv, w_xo, g_ffn, w_up, w_ffn_conv, b_ffn_conv, w_down, g_final, loss_target, m_rel_bias, m_g_mix, m_w_in, m_w_short_conv, m_g_attn_out, m_g_conv_out, m_w_out, m_g_xattn, m_g_mem, m_w_xq, m_w_xk, m_w_xv, m_w_xo, m_g_ffn, m_w_up, m_w_ffn_conv, m_b_ffn_conv, m_w_down, m_g_final, v_rel_bias, v_g_mix, v_w_in, v_w_short_conv, v_g_attn_out, v_g_conv_out, v_w_out, v_g_xattn, v_g_mem, v_w_xq, v_w_xk, v_w_xv, v_w_xo, v_g_ffn, v_w_up, v_w_ffn_conv, v_b_ffn_conv, v_w_down, v_g_final):
    names = ("rel_bias", "g_mix", "w_in", "w_short_conv", "g_attn_out", "g_conv_out", "w_out", "g_xattn", "g_mem", "w_xq",
             "w_xk", "w_xv", "w_xo", "g_ffn", "w_up", "w_ffn_conv", "b_ffn_conv", "w_down", "g_final")
    W = dict(zip(names, (rel_bias, g_mix, w_in, w_short_conv, g_attn_out, g_conv_out, w_out, g_xattn, g_mem, w_xq, w_xk, w_xv,
                         w_xo, g_ffn, w_up, w_ffn_conv, b_ffn_conv, w_down, g_final)))
    M = dict(zip(names, (m_rel_bias, m_g_mix, m_w_in, m_w_short_conv, m_g_attn_out, m_g_conv_out, m_w_out, m_g_xattn, m_g_mem,
                         m_w_xq, m_w_xk, m_w_xv, m_w_xo, m_g_ffn, m_w_up, m_w_ffn_conv, m_b_ffn_conv, m_w_down, m_g_final)))
    V = dict(zip(names, (v_rel_bias, v_g_mix, v_w_in, v_w_short_conv, v_g_attn_out, v_g_conv_out, v_w_out, v_g_xattn, v_g_mem,
                         v_w_xq, v_w_xk, v_w_xv, v_w_xo, v_g_ffn, v_w_up, v_w_ffn_conv, v_b_ffn_conv, v_w_down, v_g_final)))
    xi, yi, ci = lax.axis_index("x"), lax.axis_index("y"), lax.axis_index("c")
    mine = 2 * xi + yi
    mine_arr = jnp.reshape(mine, (1,)).astype(jnp.int32)
    sel_arr = jnp.stack([mine, ci]).astype(jnp.int32)
    comm = StepComm({n: place_shard(W[n][0], mine_arr, n) for n in BIG}, sel_arr)
    first = jnp.where(ci == 0, 1.0, 0.0).astype(F32)
    emb = []
    for n in SMALL_SHARDED:
        shard = W[n][0]
        full = jnp.zeros(SMALL_FULL[n], F32)
        emb.append(lax.dynamic_update_slice(full, shard * first, (0, mine * shard.shape[1])))
    conv_w = allreduce_small(_pack(emb), "gather_small")
    conv_w = _unpack(conv_w, SMALL_SHARDED)
    ws = {n: W[n] for n in SMALL if n not in SMALL_SHARDED}
    ws.update(conv_w)

    loss, grad_x, gfull, gs = local_step(x[0], mem[0], loss_target[0], None, ws, comm)

    reduced = allreduce_small(_pack([gs[n] for n in SMALL] + [loss]), "reduce_small")
    gsm = _unpack(reduced, SMALL)
    loss = reduced.reshape(-1)[sum(int(np.prod(SMALL_FULL[n])) for n in SMALL)]

    grads, delta, new_m, new_v = {}, {}, {}, {}
    for n in BIG:
        grads[n] = gfull[n][None]
        d, nm, nv = adamw(W[n][0], gfull[n], M[n][0], V[n][0], "adamw_" + n)
        delta[n], new_m[n], new_v[n] = d[None], nm[None], nv[None]
    for n in SMALL_SHARDED:
        wid = W[n].shape[2]
        g = lax.dynamic_slice(gsm[n], (0, mine * wid), (3, wid))
        grads[n] = g[None]
        d, nm, nv = adamw(W[n][0], g, M[n][0], V[n][0], "adamw_" + n)
        delta[n], new_m[n], new_v[n] = d[None], nm[None], nv[None]
    rep = [n for n in SMALL if n not in SMALL_SHARDED]
    d, nm, nv = adamw(_pack([W[n] for n in rep]), _pack([gsm[n] for n in rep]), _pack([M[n] for n in rep]),
                      _pack([V[n] for n in rep]), "adamw_small")

    def unpack_rep(p):
        flat, out, off = p.reshape(-1), {}, 0
        for n in rep:
            size = int(np.prod(SMALL_FULL[n]))
            out[n] = flat[off:off + size].reshape(SMALL_FULL[n])
            off += size
        return out

    d, nm, nv = unpack_rep(d), unpack_rep(nm), unpack_rep(nv)
    for n in rep:
        grads[n], delta[n], new_m[n], new_v[n] = gsm[n], d[n], nm[n], nv[n]
    return (loss, grad_x[None], *[grads[n] for n in names], *[delta[n] for n in names],
            *[new_m[n] for n in names], *[new_v[n] for n in names])
```

```python
import functools
import math

import numpy as np
import jax
import jax.numpy as jnp
from jax import lax
from jax.experimental import pallas as pl
from jax.experimental.pallas import tpu as pltpu

F32 = jnp.float32
BF16 = jnp.bfloat16
SDS = jax.ShapeDtypeStruct
MESH = pl.DeviceIdType.MESH

D_MODEL = 1024
ATTN_W = 512
N_HEADS = 8
BLK = 128
PATTERNS = ((128, 1), (512, 4), (2048, 16))
N_BUCKETS = 32
D_FF = 2816
N_MEM_HEADS = 4
MEM_HD = 256
EPS = 1e-6
NEG = -1e30
VMEM_LIMIT = 56 * 1024 * 1024

ADAM_LR, ADAM_B1, ADAM_B2, ADAM_EPS, ADAM_WD, ADAM_STEP = 0.001, 0.9, 0.999, 1e-08, 0.01, 10


def _cp(*sem):
    return pltpu.CompilerParams(dimension_semantics=sem, vmem_limit_bytes=VMEM_LIMIT)


def _dot(a, b):
    return jnp.dot(a, b, preferred_element_type=F32)


def _dot_nt(a, b):
    return lax.dot_general(a, b, (((1,), (1,)), ((), ())), preferred_element_type=F32)


def _dot_tn(a, b):
    return lax.dot_general(a, b, (((0,), (0,)), ((), ())), preferred_element_type=F32)


def _rsum(x):
    return jnp.sum(x, axis=1, keepdims=True)


def rmsnorm_fwd(x, g, name):
    S, Dm = x.shape
    tm = min(S, 512)

    def body(x_ref, g_ref, o_ref):
        xv = x_ref[...]
        r = lax.rsqrt(jnp.mean(xv * xv, axis=1, keepdims=True) + EPS)
        o_ref[...] = (xv * r * g_ref[...]).astype(o_ref.dtype)

    return pl.pallas_call(
        body, grid=(S // tm,), name=name,
        in_specs=[pl.BlockSpec((tm, Dm), lambda i: (i, 0)), pl.BlockSpec((1, Dm), lambda i: (0, 0))],
        out_specs=pl.BlockSpec((tm, Dm), lambda i: (i, 0)),
        out_shape=SDS((S, Dm), BF16), compiler_params=_cp("parallel"))(x, g)


def rmsnorm_bwd(x, g, dh, res, name, job=None):
    S, Dm = x.shape
    tm = min(S, 512)
    want_dx = res is not None

    def body(*refs):
        if want_dx:
            x_ref, g_ref, dh_ref, res_ref, dx_ref, dg_ref = refs
        else:
            x_ref, g_ref, dh_ref, dg_ref = refs
        i = pl.program_id(0)
        xv = x_ref[...]
        r = lax.rsqrt(jnp.mean(xv * xv, axis=1, keepdims=True) + EPS)
        xh = xv * r
        dh = dh_ref[...].astype(F32)
        if want_dx:
            gd = dh * g_ref[...]
            dx_ref[...] = res_ref[...] + r * (gd - xh * jnp.mean(gd * xh, axis=1, keepdims=True))

        @pl.when(i == 0)
        def _():
            dg_ref[...] = jnp.zeros_like(dg_ref)

        dg_ref[...] += jnp.sum(dh * xh, axis=0, keepdims=True)

    row = pl.BlockSpec((tm, Dm), lambda i: (i, 0))
    vec = pl.BlockSpec((1, Dm), lambda i: (0, 0))
    if want_dx:
        return _pcall(body, [x, g, dh, res], grid=(S // tm,), name=name, in_specs=[row, vec, row, row], out_specs=[row, vec],
                      out_shape=[SDS((S, Dm), F32), SDS((1, Dm), F32)], sem=("arbitrary",), job=job)
    return pl.pallas_call(
        body, grid=(S // tm,), name=name, in_specs=[row, vec, row], out_specs=vec,
        out_shape=SDS((1, Dm), F32), compiler_params=_cp("arbitrary"))(x, g, dh)


def _col_spec(bw, tn, rows, row_of, col_of):
    if bw is None:
        return pl.BlockSpec((rows, tn), lambda *g: (row_of(*g), col_of(*g)))
    if tn % bw == 0:
        return pl.BlockSpec((tn // bw, rows, bw), lambda *g: (col_of(*g), row_of(*g), 0))
    per = bw // tn
    return pl.BlockSpec((None, rows, tn), lambda *g: (col_of(*g) // per, row_of(*g), col_of(*g) % per))


def _read_cols(ref, bw, tn):
    if bw is None or tn % bw != 0:
        return ref[...]
    if tn == bw:
        return ref[0]
    return jnp.concatenate([ref[q] for q in range(tn // bw)], axis=1)


def _write_cols(ref, bw, tn, val):
    if bw is None or tn % bw != 0:
        ref[...] = val.astype(ref.dtype)
    else:
        for q in range(tn // bw):
            ref[q] = val[:, q * bw:(q + 1) * bw].astype(ref.dtype)


def mm_nn(a, b, name, *, res=None, out_dtype=F32, out_bw=None, tm=1024, tn=512, job=None):
    M, K = a.shape
    N = b.shape[1]
    tm = min(tm, M)

    def body(*refs):
        if res is None:
            a_ref, b_ref, o_ref = refs
        else:
            a_ref, b_ref, r_ref, o_ref = refs
        acc = _dot(a_ref[...].astype(BF16), b_ref[...])
        if res is not None:
            acc = acc + r_ref[...]
        _write_cols(o_ref, out_bw, tn, acc)

    ri, ci = (lambda i, j: i), (lambda i, j: j)
    in_specs = [pl.BlockSpec((tm, K), lambda i, j: (i, 0)), pl.BlockSpec((K, tn), lambda i, j: (0, j))]
    args = [a, b]
    if res is not None:
        in_specs.append(pl.BlockSpec((tm, tn), lambda i, j: (i, j)))
        args.append(res)
    oshape = (M, N) if out_bw is None else (N // out_bw, M, out_bw)
    return _pcall(body, args, grid=(M // tm, N // tn), name=name, in_specs=in_specs,
                  out_specs=[_col_spec(out_bw, tn, tm, ri, ci)], out_shape=[SDS(oshape, out_dtype)],
                  sem=("parallel", "parallel"), job=job)[0]


def mm_nt(a, a_bw, b, name, *, res=None, out_dtype=F32, tm=1024, tn=512, job=None):
    if a_bw is None:
        M, K = a.shape
    else:
        M, K = a.shape[1], a.shape[0] * a_bw
    N = b.shape[0]
    tm = min(tm, M)

    def body(*refs):
        if res is None:
            a_ref, b_ref, o_ref = refs
        else:
            a_ref, b_ref, r_ref, o_ref = refs
        av = _read_cols(a_ref, a_bw, K).astype(BF16)
        acc = _dot_nt(av, b_ref[...])
        if res is not None:
            acc = acc + r_ref[...]
        o_ref[...] = acc.astype(o_ref.dtype)

    in_specs = [_col_spec(a_bw, K, tm, lambda i, j: i, lambda i, j: 0), pl.BlockSpec((tn, K), lambda i, j: (j, 0))]
    args = [a, b]
    if res is not None:
        in_specs.append(pl.BlockSpec((tm, tn), lambda i, j: (i, j)))
        args.append(res)
    return _pcall(body, args, grid=(M // tm, N // tn), name=name, in_specs=in_specs,
                  out_specs=[pl.BlockSpec((tm, tn), lambda i, j: (i, j))], out_shape=[SDS((M, N), out_dtype)],
                  sem=("parallel", "parallel"), job=job)[0]


def mm_nt_norm_bwd(a, a_bw, b, x, g, res, name, *, tm=512, job=None):
    if a_bw is None:
        M, K = a.shape
    else:
        M, K = a.shape[1], a.shape[0] * a_bw
    N = b.shape[0]

    def body(a_ref, b_ref, x_ref, g_ref, r_ref, dx_ref, dg_ref):
        dh = _dot_nt(_read_cols(a_ref, a_bw, K).astype(BF16), b_ref[...])
        xv = x_ref[...]
        r = lax.rsqrt(jnp.mean(xv * xv, axis=1, keepdims=True) + EPS)
        xh = xv * r
        gd = dh * g_ref[...]
        dx_ref[...] = r_ref[...] + r * (gd - xh * jnp.mean(gd * xh, axis=1, keepdims=True))

        @pl.when(pl.program_id(0) == 0)
        def _():
            dg_ref[...] = jnp.zeros_like(dg_ref)

        dg_ref[...] += jnp.sum(dh * xh, axis=0, keepdims=True)

    row = pl.BlockSpec((tm, N), lambda i: (i, 0))
    vec = pl.BlockSpec((1, N), lambda i: (0, 0))
    return _pcall(body, [a, b, x, g, res], grid=(M // tm,), name=name,
                  in_specs=[_col_spec(a_bw, K, tm, lambda i: i, lambda i: 0), pl.BlockSpec((N, K), lambda i: (0, 0)), row, vec, row],
                  out_specs=[row, vec], out_shape=[SDS((M, N), F32), SDS((1, N), F32)], sem=("arbitrary",), job=job)


def mm_tn(a, b, b_bw, name, *, shards=None, tm=1024, tn=512, ts=512, job=None):
    S, Ka = a.shape
    N = b.shape[1] if b_bw is None else b.shape[0] * b_bw
    ts = min(ts, S)
    tm = min(tm, Ka)

    def body(a_ref, b_ref, o_ref):
        @pl.when(pl.program_id(2) == 0)
        def _():
            o_ref[...] = jnp.zeros_like(o_ref)

        bv = _read_cols(b_ref, b_bw, tn).astype(BF16)
        o_ref[...] += _dot_tn(a_ref[...].astype(BF16), bv)

    in_specs = [pl.BlockSpec((ts, tm), lambda i, j, k: (k, i)),
                _col_spec(b_bw, tn, ts, lambda i, j, k: k, lambda i, j, k: j)]
    if shards is None:
        out_spec, oshape = pl.BlockSpec((tm, tn), lambda i, j, k: (i, j)), (Ka, N)
    else:
        per = (N // shards) // tn
        out_spec = pl.BlockSpec((None, tm, tn), lambda i, j, k: (j // per, i, j % per))
        oshape = (shards, Ka, N // shards)
    return _pcall(body, [a, b], grid=(Ka // tm, N // tn, S // ts), name=name, in_specs=in_specs, out_specs=[out_spec],
                  out_shape=[SDS(oshape, F32)], sem=("parallel", "parallel", "arbitrary"), job=job)[0]


def _bucket_tables():
    out = np.zeros((3, 2, BLK, BLK), np.int32)
    qi = np.arange(BLK)[:, None]
    for p, (win, dil) in enumerate(PATTERNS):
        w = win // dil
        assert w == BLK
        for half in range(2):
            kj = np.arange(BLK)[None, :] + half * BLK
            steps = qi + w - kj
            valid = (steps >= 0) & (steps <= w)
            dist = np.clip(steps, 0, w) * dil
            dd = np.maximum(dist, 1).astype(np.float32)
            large = 16 + (np.log(dd / np.float32(16)) / np.float32(math.log(2048 / 16)) * np.float32(16)).astype(np.int32)
            large = np.minimum(large, N_BUCKETS - 1)
            out[p, half] = np.where(valid, np.where(dist < 16, dist, large), -1)
    return out


def bias_tables(rel_bias):
    bkt = jnp.asarray(_bucket_tables())

    def body(rb_ref, bkt_ref, o_ref):
        h = pl.program_id(1)
        for half in range(2):
            bk = bkt_ref[half]
            acc = jnp.full((BLK, BLK), NEG, F32)
            for b in range(N_BUCKETS):
                acc = jnp.where(bk == b, rb_ref[h, b], acc)
            o_ref[half] = acc

    return pl.pallas_call(
        body, grid=(3, N_HEADS), name="bias_tables",
        in_specs=[pl.BlockSpec(memory_space=pltpu.SMEM),
                  pl.BlockSpec((None, 2, BLK, BLK), lambda p, h: (p, 0, 0, 0))],
        out_specs=pl.BlockSpec((None, None, 2, BLK, BLK), lambda p, h: (p, h, 0, 0, 0)),
        out_shape=SDS((3, N_HEADS, 2, BLK, BLK), F32), compiler_params=_cp("parallel", "parallel"))(rel_bias, bkt)


def bias_tables_bwd(ds_sum):
    bkt = jnp.asarray(_bucket_tables())

    def body(ds_ref, bkt_ref, o_ref):
        h = pl.program_id(0)

        @pl.when(h == 0)
        def _():
            o_ref[...] = jnp.zeros_like(o_ref)

        rows = lax.broadcasted_iota(jnp.int32, (N_HEADS, N_BUCKETS), 0)
        cols = lax.broadcasted_iota(jnp.int32, (N_HEADS, N_BUCKETS), 1)
        acc = jnp.zeros((N_HEADS, N_BUCKETS), F32)
        for b in range(N_BUCKETS):
            tot = jnp.zeros((1, 1), F32)
            for p in range(3):
                for half in range(2):
                    t = jnp.where(bkt_ref[p, half] == b, ds_ref[p, half], 0.0)
                    tot = tot + jnp.sum(_rsum(t), axis=0, keepdims=True)
            acc = acc + jnp.where((rows == h) & (cols == b), tot, 0.0)
        o_ref[...] += acc

    return pl.pallas_call(
        body, grid=(N_HEADS,), name="bias_tables_bwd",
        in_specs=[pl.BlockSpec((3, None, 2, BLK, BLK), lambda h: (0, h, 0, 0, 0)),
                  pl.BlockSpec((3, 2, BLK, BLK), lambda h: (0, 0, 0, 0))],
        out_specs=pl.BlockSpec((N_HEADS, N_BUCKETS), lambda h: (0, 0)),
        out_shape=SDS((N_HEADS, N_BUCKETS), F32), compiler_params=_cp("arbitrary"))(ds_sum, bkt)


def _rows(start, dil):
    if dil == 1:
        return pl.ds(pl.multiple_of(start, BLK), BLK)
    return pl.ds(start, BLK, stride=dil)


GRP = 4


def _group_rows(i, dil):
    t = lax.broadcasted_iota(jnp.int32, (GRP, 1, 1), 0)
    if dil == 1:
        first = i * GRP
        cur = [(first + j) * BLK for j in range(GRP)]
        prv = [jnp.maximum(first + j - 1, 0) * BLK for j in range(GRP)]
        pen = jnp.where(first + t > 0, 0.0, NEG)
    else:
        per_b = dil // GRP
        b, r0 = i // per_b, (i % per_b) * GRP
        cur = [b * (BLK * dil) + r0 + j for j in range(GRP)]
        prv = [jnp.maximum(b - 1, 0) * (BLK * dil) + r0 + j for j in range(GRP)]
        pen = jnp.where(b + 0 * t > 0, 0.0, NEG)
    return [_rows(s, dil) for s in cur], [_rows(s, dil) for s in prv], pen


def _load_group(ref, rows):
    return jnp.stack([ref[r, :] for r in rows])


def _bdot_nt(a, b):
    return lax.dot_general(a, b, (((2,), (2,)), ((0,), (0,))), preferred_element_type=F32)


def _bdot(a, b):
    return lax.dot_general(a, b, (((2,), (1,)), ((0,), (0,))), preferred_element_type=F32)


def _bdot_tn(a, b):
    return lax.dot_general(a, b, (((1,), (1,)), ((0,), (0,))), preferred_element_type=F32)


def _lsum(x):
    return jnp.sum(x, axis=-1, keepdims=True)


def attn_fwd(qkv, bias, job=None):
    S = qkv.shape[3]
    nblk = S // BLK
    car = Carried(job, 2, 2, 1, 4)

    def body(*refs):
        (qkv_ref, bias_ref, o_ref, lse_ref, l_ref), jrefs = car.split(refs)
        car.phase(0, pl.program_id(0), jrefs)
        lane = lax.broadcasted_iota(jnp.int32, (GRP, BLK, BLK), 2)
        lo = lane < 64
        masks = (lo, jnp.logical_not(lo))
        q_ref, k_ref, v_ref = qkv_ref.at[0], qkv_ref.at[1], qkv_ref.at[2]
        for p, (_, dil) in enumerate(PATTERNS):
            def step(i, carry, p=p, dil=dil):
                rc, rp, pen = _group_rows(i, dil)
                q2 = _load_group(q_ref, rc) * 0.125
                kc = _load_group(k_ref, rc).astype(BF16)
                kp = _load_group(k_ref, rp).astype(BF16)
                vc = _load_group(v_ref, rc)
                vp = _load_group(v_ref, rp)
                if p > 0:
                    acc_old, m_old, l_old = _load_group(o_ref, rc), _load_group(lse_ref, rc), _load_group(l_ref, rc)
                pv = jnp.zeros((GRP, BLK, BLK), F32)
                m_new, l_new, alpha = [], [], []
                for h in range(2):
                    qh = jnp.where(masks[h], q2, 0.0).astype(BF16)
                    sp = _bdot_nt(qh, kp) + bias_ref[p, h, 0][None] + pen
                    sc = _bdot_nt(qh, kc) + bias_ref[p, h, 1][None]
                    mn = jnp.max(jnp.maximum(sp, sc), axis=-1, keepdims=True)
                    if p > 0:
                        mo = m_old[:, :, 64 * h:64 * h + 1]
                        mn = jnp.maximum(mn, mo)
                        al = jnp.exp(mo - mn)
                        alpha.append(al)
                    pp = jnp.exp(sp - mn)
                    pc = jnp.exp(sc - mn)
                    ln = _lsum(pp + pc)
                    if p > 0:
                        ln = ln + al * l_old[:, :, 64 * h:64 * h + 1]
                    vhp = jnp.where(masks[h], vp, 0.0).astype(BF16)
                    vhc = jnp.where(masks[h], vc, 0.0).astype(BF16)
                    pv = pv + _bdot(pp.astype(BF16), vhp) + _bdot(pc.astype(BF16), vhc)
                    m_new.append(mn)
                    l_new.append(ln)
                if p > 0:
                    pv = pv + acc_old * jnp.where(lo, alpha[0], alpha[1])
                m_t = jnp.where(lo, m_new[0], m_new[1])
                l_t = jnp.where(lo, l_new[0], l_new[1])
                for j, r in enumerate(rc):
                    o_ref[r, :] = pv[j]
                    lse_ref[r, :] = m_t[j]
                    l_ref[r, :] = l_t[j]
                return carry

            lax.fori_loop(0, nblk // GRP, step, 0)

        def fin(i, carry):
            rows = pl.ds(pl.multiple_of(i * 512, 512), 512)
            lv = l_ref[rows, :]
            o_ref[rows, :] = o_ref[rows, :] / lv
            lse_ref[rows, :] = lse_ref[rows, :] + jnp.log(lv)
            return carry

        lax.fori_loop(0, S // 512, fin, 0)
        car.phase(1, pl.program_id(0), jrefs)
        car.phase(2, pl.program_id(0), jrefs)

    outs = pl.pallas_call(
        body, grid=(4,), name="attn_fwd",
        in_specs=[pl.BlockSpec((None, 3, None, S, BLK), lambda g: (0, 0, g, 0, 0)),
                  pl.BlockSpec((3, 2, 2, BLK, BLK), lambda g: (0, g, 0, 0, 0))] + car.in_specs(),
        out_specs=[pl.BlockSpec((None, S, BLK), lambda g: (g, 0, 0)),
                   pl.BlockSpec((None, S, BLK), lambda g: (g, 0, 0))] + car.out_specs(),
        out_shape=[SDS((4, S, BLK), F32), SDS((4, S, BLK), F32)] + car.out_shapes(),
        input_output_aliases=car.aliases(),
        scratch_shapes=[pltpu.VMEM((S, BLK), F32)] + car.sems(), compiler_params=_cp("arbitrary"))(qkv, bias, *car.args())
    if job is not None:
        job.landed = list(outs[2:])
    return outs[0], outs[1]


def attn_bwd(qkv, bias, o, lse, do, job=None):
    S = qkv.shape[3]
    nblk = S // BLK
    car = Carried(job, 5, 2, 0, 4)

    def body(*refs):
        (qkv_ref, bias_ref, o_ref, lse_ref, do_ref, dqkv_ref, ds_ref), jrefs = car.split(refs)
        car.phase(0, pl.program_id(0), jrefs)
        lane = lax.broadcasted_iota(jnp.int32, (GRP, BLK, BLK), 2)
        lo = lane < 64
        masks = (lo, jnp.logical_not(lo))
        dqkv_ref[...] = jnp.zeros_like(dqkv_ref)
        ds_ref[...] = jnp.zeros_like(ds_ref)
        q_ref, k_ref, v_ref = qkv_ref.at[0], qkv_ref.at[1], qkv_ref.at[2]
        dq_ref, dk_ref, dv_ref = dqkv_ref.at[0], dqkv_ref.at[1], dqkv_ref.at[2]
        for p, (_, dil) in enumerate(PATTERNS):
            def step(i, carry, p=p, dil=dil):
                rc, rp, pen = _group_rows(i, dil)
                q2 = _load_group(q_ref, rc) * 0.125
                kc = _load_group(k_ref, rc).astype(BF16)
                kp = _load_group(k_ref, rp).astype(BF16)
                vc = _load_group(v_ref, rc).astype(BF16)
                vp = _load_group(v_ref, rp).astype(BF16)
                dot = _load_group(do_ref, rc)
                lset = _load_group(lse_ref, rc)
                prod = dot * _load_group(o_ref, rc)
                zero = jnp.zeros((GRP, BLK, BLK), F32)
                dq, dkc, dkp, dvc, dvp = zero, zero, zero, zero, zero
                for h in range(2):
                    qh = jnp.where(masks[h], q2, 0.0).astype(BF16)
                    doh = jnp.where(masks[h], dot, 0.0).astype(BF16)
                    delta = _lsum(jnp.where(masks[h], prod, 0.0))
                    lh = lset[:, :, 64 * h:64 * h + 1]
                    pp = jnp.exp(_bdot_nt(qh, kp) + bias_ref[p, h, 0][None] + pen - lh)
                    pc = jnp.exp(_bdot_nt(qh, kc) + bias_ref[p, h, 1][None] - lh)
                    dsp = pp * (_bdot_nt(doh, vp) - delta)
                    dsc = pc * (_bdot_nt(doh, vc) - delta)
                    ds_ref[p, h, 0] += jnp.sum(dsp, axis=0)
                    ds_ref[p, h, 1] += jnp.sum(dsc, axis=0)
                    dspb, dscb = dsp.astype(BF16), dsc.astype(BF16)
                    dq = dq + jnp.where(masks[h], _bdot(dspb, kp) + _bdot(dscb, kc), 0.0)
                    dkp = dkp + _bdot_tn(dspb, qh)
                    dkc = dkc + _bdot_tn(dscb, qh)
                    dvp = dvp + _bdot_tn(pp.astype(BF16), doh)
                    dvc = dvc + _bdot_tn(pc.astype(BF16), doh)
                for j in range(GRP):
                    dq_ref[rc[j], :] += dq[j] * 0.125
                    dk_ref[rp[j], :] += dkp[j]
                    dk_ref[rc[j], :] += dkc[j]
                    dv_ref[rp[j], :] += dvp[j]
                    dv_ref[rc[j], :] += dvc[j]
                return carry

            lax.fori_loop(0, nblk // GRP, step, 0)
        car.phase(1, pl.program_id(0), jrefs)
        car.phase(2, pl.program_id(0), jrefs)

    blk = pl.BlockSpec((None, S, BLK), lambda g: (g, 0, 0))
    outs = pl.pallas_call(
        body, grid=(4,), name="attn_bwd",
        in_specs=[pl.BlockSpec((None, 3, None, S, BLK), lambda g: (0, 0, g, 0, 0)),
                  pl.BlockSpec((3, 2, 2, BLK, BLK), lambda g: (0, g, 0, 0, 0)), blk, blk, blk] + car.in_specs(),
        out_specs=[pl.BlockSpec((None, 3, None, S, BLK), lambda g: (0, 0, g, 0, 0)),
                   pl.BlockSpec((3, 2, 2, BLK, BLK), lambda g: (0, g, 0, 0, 0))] + car.out_specs(),
        out_shape=[SDS((2, 3, 4, S, BLK), F32), SDS((3, N_HEADS, 2, BLK, BLK), F32)] + car.out_shapes(),
        input_output_aliases=car.aliases(), scratch_shapes=car.sems(),
        compiler_params=_cp("arbitrary"))(qkv, bias, o, lse, do, *car.args())
    if job is not None:
        job.landed = list(outs[2:])
    return outs[0], outs[1]


def _shift_down(u, k, halo):
    n = u.shape[0]
    row = lax.broadcasted_iota(jnp.int32, u.shape, 0)
    out = pltpu.roll(u, k, 0)
    hn = halo.shape[0]
    for j in range(k):
        out = jnp.where(row == j, halo[hn - k + j:hn - k + j + 1, :], out)
    return out


def _shift_up(u, k, halo):
    n = u.shape[0]
    row = lax.broadcasted_iota(jnp.int32, u.shape, 0)
    out = pltpu.roll(u, n - k, 0)
    for j in range(k):
        out = jnp.where(row == n - k + j, halo[j:j + 1, :], out)
    return out


def _conv3(u, halo, w0, w1, w2):
    return _shift_down(u, 2, halo) * w0 + _shift_down(u, 1, halo) * w1 + u * w2


def _colsum(x):
    return jnp.sum(x, axis=0, keepdims=True)


def _mixer_specs(S, tm):
    conv = pl.BlockSpec((None, 12, tm, BLK), lambda i: (1, 0, i, 0))
    halo = pl.BlockSpec((None, 12, 8, BLK), lambda i: (1, 0, jnp.maximum(i * (tm // 8) - 1, 0), 0))
    ob = pl.BlockSpec((4, tm, BLK), lambda i: (0, i, 0))
    return conv, halo, ob


def _mixer_recompute(i, o_ref, pr_ref, ph_ref, w_ref):
    ob = [o_ref[q] for q in range(4)]
    gb = [pr_ref[q] for q in range(4)]
    gc = [pr_ref[4 + q] for q in range(4)]
    xi = [pr_ref[8 + q] for q in range(4)]
    keep = jnp.where(i > 0, 1.0, 0.0)
    u = [gc[q] * xi[q] for q in range(4)]
    hu = [ph_ref[4 + q] * ph_ref[8 + q] * keep for q in range(4)]
    w = [[w_ref[k:k + 1, q * BLK:(q + 1) * BLK] for k in range(3)] for q in range(4)]
    cv = [_conv3(u[q], hu[q], *w[q]) for q in range(4)]
    return ob, gb, gc, xi, u, hu, cv, w


def _rms_blocks(blocks):
    ss = sum(_rsum(b * b) for b in blocks)
    return lax.rsqrt(ss / (BLK * len(blocks)) + EPS)


def mixer_post_fwd(o, proj, w_sc, g_a, g_c):
    S = o.shape[1]
    tm = 256

    def body(o_ref, pr_ref, ph_ref, w_ref, ga_ref, gc_ref, m_ref):
        i = pl.program_id(0)
        ob, gb, _, _, _, _, cv, _ = _mixer_recompute(i, o_ref, pr_ref, ph_ref, w_ref)
        conv = [gb[q] * cv[q] for q in range(4)]
        ra, rc = _rms_blocks(ob), _rms_blocks(conv)
        for q in range(4):
            sl = slice(q * BLK, (q + 1) * BLK)
            m_ref[:, q * BLK:(q + 1) * BLK] = (ob[q] * ra * ga_ref[:, sl]).astype(BF16)
            m_ref[:, ATTN_W + q * BLK:ATTN_W + (q + 1) * BLK] = (conv[q] * rc * gc_ref[:, sl]).astype(BF16)

    conv_s, halo_s, ob_s = _mixer_specs(S, tm)
    full = lambda r, c: pl.BlockSpec((r, c), lambda i: (0, 0))
    return pl.pallas_call(
        body, grid=(S // tm,), name="mixer_post_fwd",
        in_specs=[ob_s, conv_s, halo_s, full(3, 512), full(1, 512), full(1, 512)],
        out_specs=pl.BlockSpec((tm, D_MODEL), lambda i: (i, 0)), out_shape=SDS((S, D_MODEL), BF16),
        compiler_params=_cp("parallel"))(o, proj, proj, w_sc, g_a, g_c)


def mixer_post_bwd_a(dmixed, o, proj, w_sc, g_a, g_c, job=None):
    S = o.shape[1]
    tm = 256

    def body(dm_ref, o_ref, pr_ref, ph_ref, w_ref, ga_ref, gc_ref, do_ref, dgb_ref, dcv_ref, dga_ref, dgc_ref):
        i = pl.program_id(0)
        ob, gb, _, _, _, _, cv, _ = _mixer_recompute(i, o_ref, pr_ref, ph_ref, w_ref)
        conv = [gb[q] * cv[q] for q in range(4)]
        ra, rc = _rms_blocks(ob), _rms_blocks(conv)

        @pl.when(i == 0)
        def _():
            dga_ref[...] = jnp.zeros_like(dga_ref)
            dgc_ref[...] = jnp.zeros_like(dgc_ref)

        for blocks, r, g_ref, off, dg_ref, is_attn in ((ob, ra, ga_ref, 0, dga_ref, True), (conv, rc, gc_ref, ATTN_W, dgc_ref, False)):
            xh = [blocks[q] * r for q in range(4)]
            dm = [dm_ref[:, off + q * BLK:off + (q + 1) * BLK].astype(F32) for q in range(4)]
            gd = [dm[q] * g_ref[:, q * BLK:(q + 1) * BLK] for q in range(4)]
            mean = sum(_rsum(gd[q] * xh[q]) for q in range(4)) / (4 * BLK)
            for q in range(4):
                dg_ref[:, q * BLK:(q + 1) * BLK] += _colsum(dm[q] * xh[q])
                dx = r * (gd[q] - xh[q] * mean)
                if is_attn:
                    do_ref[q] = dx
                else:
                    dgb_ref[q] = dx * cv[q]
                    dcv_ref[q] = dx * gb[q]

    conv_s, halo_s, ob_s = _mixer_specs(S, tm)
    full = lambda r, c: pl.BlockSpec((r, c), lambda i: (0, 0))
    return _pcall(
        body, [dmixed, o, proj, proj, w_sc, g_a, g_c], grid=(S // tm,), name="mixer_post_bwd_a",
        in_specs=[pl.BlockSpec((tm, D_MODEL), lambda i: (i, 0)), ob_s, conv_s, halo_s, full(3, 512), full(1, 512), full(1, 512)],
        out_specs=[ob_s, ob_s, ob_s, full(1, 512), full(1, 512)],
        out_shape=[SDS((4, S, BLK), F32)] * 3 + [SDS((1, 512), F32)] * 2, sem=("arbitrary",), job=job)


def mixer_post_bwd_b(dproj, dgb, dcv, proj, w_sc):
    S = proj.shape[2]
    tm = 256
    last = S // 8 - 1

    def body(dp_in, dgb_ref, dcv_ref, dn_ref, pr_ref, w_ref, dp_ref, dw_ref):
        i = pl.program_id(0)
        keep_next = jnp.where(i < pl.num_programs(0) - 1, 1.0, 0.0)

        @pl.when(i == 0)
        def _():
            dw_ref[...] = jnp.zeros_like(dw_ref)

        for q in range(4):
            sl = slice(q * BLK, (q + 1) * BLK)
            gc, xi = pr_ref[4 + q], pr_ref[8 + q]
            u = gc * xi
            dcv = dcv_ref[q]
            dn = dn_ref[q] * keep_next
            d1, d2 = _shift_up(dcv, 1, dn), _shift_up(dcv, 2, dn)
            w0, w1, w2 = (w_ref[k:k + 1, sl] for k in range(3))
            du = dcv * w2 + d1 * w1 + d2 * w0
            dw_ref[0:1, sl] += _colsum(d2 * u)
            dw_ref[1:2, sl] += _colsum(d1 * u)
            dw_ref[2:3, sl] += _colsum(dcv * u)
            dp_ref[q] = dgb_ref[q]
            dp_ref[4 + q] = du * xi
            dp_ref[8 + q] = du * gc

    conv_s, _, ob_s = _mixer_specs(S, tm)
    nxt = pl.BlockSpec((4, 8, BLK), lambda i: (0, jnp.minimum((i + 1) * (tm // 8), last), 0))
    full = lambda r, c: pl.BlockSpec((r, c), lambda i: (0, 0))
    return pl.pallas_call(
        body, grid=(S // tm,), name="mixer_post_bwd_b",
        in_specs=[pl.BlockSpec(memory_space=pl.ANY), ob_s, ob_s, nxt, conv_s, full(3, 512)],
        out_specs=[conv_s, full(3, 512)],
        out_shape=[SDS(dproj.shape, F32), SDS((3, 512), F32)],
        input_output_aliases={0: 0}, compiler_params=_cp("arbitrary"))(dproj, dgb, dcv, dcv, proj, w_sc)


def xattn_fwd(q, k, v):
    S = q.shape[0]
    tm = 512
    scale = MEM_HD ** -0.5

    def body(q_ref, k_ref, v_ref, o_ref):
        for h in range(N_MEM_HEADS):
            sl = slice(h * MEM_HD, (h + 1) * MEM_HD)
            s = _dot_nt(q_ref[:, sl], k_ref[:, sl]) * scale
            p = jnp.exp(s - jnp.max(s, axis=1, keepdims=True))
            p = p / _rsum(p)
            o_ref[:, sl] = _dot(p.astype(BF16), v_ref[:, sl]).astype(BF16)

    row = pl.BlockSpec((tm, D_MODEL), lambda i: (i, 0))
    kv = pl.BlockSpec(k.shape, lambda i: (0, 0))
    return pl.pallas_call(body, grid=(S // tm,), name="xattn_fwd", in_specs=[row, kv, kv], out_specs=row,
                          out_shape=SDS((S, D_MODEL), BF16), compiler_params=_cp("parallel"))(q, k, v)


def xattn_bwd(q, k, v, do):
    S = q.shape[0]
    tm = 512
    scale = MEM_HD ** -0.5

    def body(q_ref, k_ref, v_ref, do_ref, dq_ref, dk_ref, dv_ref):
        @pl.when(pl.program_id(0) == 0)
        def _():
            dk_ref[...] = jnp.zeros_like(dk_ref)
            dv_ref[...] = jnp.zeros_like(dv_ref)

        for h in range(N_MEM_HEADS):
            sl = slice(h * MEM_HD, (h + 1) * MEM_HD)
            qh, kh, vh, doh = q_ref[:, sl], k_ref[:, sl], v_ref[:, sl], do_ref[:, sl]
            s = _dot_nt(qh, kh) * scale
            p = jnp.exp(s - jnp.max(s, axis=1, keepdims=True))
            p = p / _rsum(p)
            dp = _dot_nt(doh, vh)
            ds = (p * (dp - _rsum(p * dp)) * scale).astype(BF16)
            dq_ref[:, sl] = _dot(ds, kh).astype(BF16)
            dk_ref[:, sl] += _dot_tn(ds, qh)
            dv_ref[:, sl] += _dot_tn(p.astype(BF16), doh)

    row = pl.BlockSpec((tm, D_MODEL), lambda i: (i, 0))
    kv = pl.BlockSpec(k.shape, lambda i: (0, 0))
    return pl.pallas_call(body, grid=(S // tm,), name="xattn_bwd", in_specs=[row, kv, kv, row], out_specs=[row, kv, kv],
                          out_shape=[SDS((S, D_MODEL), BF16), SDS(k.shape, F32), SDS(k.shape, F32)],
                          compiler_params=_cp("arbitrary"))(q, k, v, do)


FFN_TM, FFN_TC = 512, 256


def _ffn_specs(S, order):
    tm, tc = FFN_TM, FFN_TC
    ij = (lambda a, b: (a, b)) if order == "ij" else (lambda a, b: (b, a))
    blk = pl.BlockSpec((2, tm, tc), lambda a, b: (0,) + ij(a, b))
    prev = pl.BlockSpec((2, 16, tc), lambda a, b: (0, jnp.maximum(ij(a, b)[0] * (tm // 16) - 1, 0), ij(a, b)[1]))
    nxt = pl.BlockSpec((2, 16, tc), lambda a, b: (0, jnp.minimum((ij(a, b)[0] + 1) * (tm // 16), S // 16 - 1), ij(a, b)[1]))
    wsp = pl.BlockSpec((2, 3, tc), lambda a, b: (0, 0, ij(a, b)[1]))
    bsp = pl.BlockSpec((2, 1, tc), lambda a, b: (0, 0, ij(a, b)[1]))
    act = pl.BlockSpec((tm, tc), lambda a, b: ij(a, b))
    return blk, prev, nxt, wsp, bsp, act


def _ffn_up(i, up_ref, uh_ref, w_ref, b_ref):
    keep = jnp.where(i > 0, 1.0, 0.0)
    out = []
    for half in range(2):
        u = up_ref[half].astype(F32)
        hu = uh_ref[half].astype(F32) * keep
        w0, w1, w2 = (w_ref[half, k:k + 1, :] for k in range(3))
        out.append(_conv3(u, hu, w0, w1, w2) + b_ref[half])
    return out


def ffn_act_fwd(up_pre, w_fc, b_fc):
    S = up_pre.shape[1]

    def body(up_ref, uh_ref, w_ref, b_ref, a_ref, gv_ref):
        g, v = _ffn_up(pl.program_id(0), up_ref, uh_ref, w_ref, b_ref)
        a_ref[...] = (g * jax.nn.sigmoid(g) * v).astype(BF16)
        gv_ref[0] = g.astype(BF16)
        gv_ref[1] = v.astype(BF16)

    blk, prev, _, wsp, bsp, act = _ffn_specs(S, "ij")
    return pl.pallas_call(body, grid=(S // FFN_TM, D_FF // FFN_TC), name="ffn_act_fwd",
                          in_specs=[blk, prev, wsp, bsp], out_specs=[act, blk],
                          out_shape=[SDS((S, D_FF), BF16), SDS((2, S, D_FF), BF16)],
                          compiler_params=_cp("parallel", "parallel"))(up_pre, up_pre, w_fc, b_fc)


def ffn_act_bwd_a(dact, up, job=None):
    S = up.shape[1]

    def body(da_ref, up_ref, du_ref, db_ref):
        i = pl.program_id(1)
        g, v = up_ref[0].astype(F32), up_ref[1].astype(F32)
        da = da_ref[...].astype(F32)
        sg = jax.nn.sigmoid(g)
        dg = da * v * (sg * (1.0 + g * (1.0 - sg)))
        dv = da * g * sg

        @pl.when(i == 0)
        def _():
            db_ref[...] = jnp.zeros_like(db_ref)

        du_ref[0] = dg.astype(BF16)
        du_ref[1] = dv.astype(BF16)
        db_ref[0] += _colsum(dg)
        db_ref[1] += _colsum(dv)

    blk, _, _, _, bsp, act = _ffn_specs(S, "ji")
    return _pcall(body, [dact, up], grid=(D_FF // FFN_TC, S // FFN_TM), name="ffn_act_bwd_a",
                  in_specs=[act, blk], out_specs=[blk, bsp],
                  out_shape=[SDS((2, S, D_FF), BF16), SDS((2, 1, D_FF), F32)], sem=("parallel", "arbitrary"), job=job)


def ffn_act_bwd_b(d_up, up_pre, w_fc):
    S = up_pre.shape[1]

    def body(du_ref, dn_ref, up_ref, w_ref, dp_ref, dw_ref):
        i = pl.program_id(1)
        keep_next = jnp.where(i < pl.num_programs(1) - 1, 1.0, 0.0)

        @pl.when(i == 0)
        def _():
            dw_ref[...] = jnp.zeros_like(dw_ref)

        for half in range(2):
            d = du_ref[half].astype(F32)
            dn = dn_ref[half].astype(F32) * keep_next
            u = up_ref[half].astype(F32)
            d1, d2 = _shift_up(d, 1, dn), _shift_up(d, 2, dn)
            w0, w1, w2 = (w_ref[half, k:k + 1, :] for k in range(3))
            dp_ref[half] = (d * w2 + d1 * w1 + d2 * w0).astype(BF16)
            dw_ref[half, 0:1, :] += _colsum(d2 * u)
            dw_ref[half, 1:2, :] += _colsum(d1 * u)
            dw_ref[half, 2:3, :] += _colsum(d * u)

    blk, _, nxt, wsp, _, _ = _ffn_specs(S, "ji")
    return pl.pallas_call(body, grid=(D_FF // FFN_TC, S // FFN_TM), name="ffn_act_bwd_b",
                          in_specs=[blk, nxt, blk, wsp], out_specs=[blk, wsp],
                          out_shape=[SDS((2, S, D_FF), BF16), SDS((2, 3, D_FF), F32)],
                          compiler_params=_cp("parallel", "arbitrary"))(d_up, d_up, up_pre, w_fc)


def final_loss(x3, g, target):
    S, Dm = x3.shape
    tm = 512

    def body(x_ref, g_ref, t_ref, loss_ref, dx_ref, dg_ref):
        i = pl.program_id(0)
        xv = x_ref[...]
        r = lax.rsqrt(jnp.mean(xv * xv, axis=1, keepdims=True) + EPS)
        xh = xv * r
        err = xh * g_ref[...] - t_ref[...]
        dy = err / Dm
        gd = dy * g_ref[...]
        dx_ref[...] = r * (gd - xh * jnp.mean(gd * xh, axis=1, keepdims=True))

        @pl.when(i == 0)
        def _():
            dg_ref[...] = jnp.zeros_like(dg_ref)
            loss_ref[...] = jnp.zeros_like(loss_ref)

        dg_ref[...] += _colsum(dy * xh)
        loss_ref[...] += 0.5 * _colsum(jnp.mean(err * err, axis=1, keepdims=True))

    row = pl.BlockSpec((tm, Dm), lambda i: (i, 0))
    vec = pl.BlockSpec((1, Dm), lambda i: (0, 0))
    one = pl.BlockSpec((1, 1), lambda i: (0, 0))
    return pl.pallas_call(body, grid=(S // tm,), name="final_loss", in_specs=[row, vec, row], out_specs=[one, row, vec],
                          out_shape=[SDS((1, 1), F32), SDS((S, Dm), F32), SDS((1, Dm), F32)],
                          compiler_params=_cp("arbitrary"))(x3, g, target)


def local_step(x, mem, target, wb, ws, comm=None):
    S = x.shape[0]
    assert S % 2048 == 0
    if comm is None:
        comm = NoComm()
    else:
        wb = comm.gather_first()
    w_fc = ws["w_ffn_conv"].reshape(3, 2, D_FF).transpose(1, 0, 2)
    b_fc = ws["b_ffn_conv"].reshape(2, 1, D_FF)

    bias = bias_tables(ws["rel_bias"])
    h1 = rmsnorm_fwd(x, ws["g_mix"], "norm_mix")
    proj = mm_nn(h1, wb["w_in"], "proj", out_bw=BLK, tn=768, job=comm.job("proj"))
    comm.landed("proj", wb)
    qkv = proj.reshape(2, 3, 4, S, BLK)
    o, lse = attn_fwd(qkv, bias, job=comm.job("attn_fwd"))
    comm.landed("attn_fwd", wb)
    proj4 = proj.reshape(2, 12, S, BLK)
    mixed = mixer_post_fwd(o, proj4, ws["w_short_conv"], ws["g_attn_out"], ws["g_conv_out"])
    x1 = mm_nn(mixed, wb["w_out"], "out_proj", res=x)
    h2 = rmsnorm_fwd(x1, ws["g_xattn"], "norm_xattn")
    mem_n = rmsnorm_fwd(mem, ws["g_mem"], "norm_mem")
    xq = mm_nn(h2, wb["w_xq"], "xq", out_dtype=BF16)
    xk = mm_nn(mem_n, wb["w_xk"], "xk", out_dtype=BF16, tn=1024)
    xv = mm_nn(mem_n, wb["w_xv"], "xv", out_dtype=BF16, tn=1024)
    xo = xattn_fwd(xq, xk, xv)
    x2 = mm_nn(xo, wb["w_xo"], "xo_proj", res=x1)
    h3 = rmsnorm_fwd(x2, ws["g_ffn"], "norm_ffn")
    up_pre = mm_nn(h3, wb["w_up"], "up_proj", out_dtype=BF16, out_bw=D_FF, tn=1408)
    act, up = ffn_act_fwd(up_pre, w_fc, b_fc)
    x3 = mm_nn(act, wb["w_down"], "down_proj", res=x2, tm=512)
    loss, dx3, dg_final = final_loss(x3, ws["g_final"].reshape(1, -1), target)

    gb, gs = {}, {"g_final": dg_final}
    gb["w_down"] = mm_tn(act, dx3, None, "dw_down", tm=1408, tn=1024)
    dact = mm_nt(dx3, None, wb["w_down"], "d_act", out_dtype=BF16, tn=1408)
    d_up, db_fc = ffn_act_bwd_a(dact, up)
    d_up_pre, dw_fc = ffn_act_bwd_b(d_up, up_pre, w_fc)
    gs["b_ffn_conv"] = db_fc.reshape(1, 2 * D_FF)
    gs["w_ffn_conv"] = dw_fc.transpose(1, 0, 2).reshape(3, 2 * D_FF)
    gb["w_up"] = mm_tn(h3, d_up_pre, D_FF, "dw_up", shards=4, tn=1408)
    dx2, gs["g_ffn"] = mm_nt_norm_bwd(d_up_pre, D_FF, wb["w_up"], x2, ws["g_ffn"], dx3, "d_h3", tm=256,
                                      job=comm.job("d_h3", gb))
    comm.landed("d_h3")
    gb["w_xo"] = mm_tn(xo, dx2, None, "dw_xo")
    dxo = mm_nt(dx2, None, wb["w_xo"], "d_xo", out_dtype=BF16)
    dxq, dxk, dxv = xattn_bwd(xq, xk, xv, dxo)
    gb["w_xq"] = mm_tn(h2, dxq, None, "dw_xq")
    gb["w_xk"] = mm_tn(mem_n, dxk, None, "dw_xk", tn=1024)
    gb["w_xv"] = mm_tn(mem_n, dxv, None, "dw_xv", tn=1024)
    dmem_n = mm_nt(dxk, None, wb["w_xk"], "d_memk", tn=1024)
    dmem_n = mm_nt(dxv, None, wb["w_xv"], "d_memv", res=dmem_n, tn=1024)
    gs["g_mem"] = rmsnorm_bwd(mem, ws["g_mem"], dmem_n, None, "norm_mem_bwd")
    dx1, gs["g_xattn"] = mm_nt_norm_bwd(dxq, None, wb["w_xq"], x1, ws["g_xattn"], dx2, "d_h2")
    gb["w_out"] = mm_tn(mixed, dx1, None, "dw_out")
    dmixed = mm_nt(dx1, None, wb["w_out"], "d_mixed")
    do, dgb, dcv, gs["g_attn_out"], gs["g_conv_out"] = mixer_post_bwd_a(
        dmixed, o, proj4, ws["w_short_conv"], ws["g_attn_out"], ws["g_conv_out"], job=comm.job("mixer_post_bwd_a", gb))
    comm.landed("mixer_post_bwd_a")
    dproj, ds_sum = attn_bwd(qkv, bias, o, lse, do, job=comm.job("attn_bwd"))
    comm.landed("attn_bwd")
    gs["rel_bias"] = bias_tables_bwd(ds_sum)
    dproj, gs["w_short_conv"] = mixer_post_bwd_b(dproj.reshape(2, 12, S, BLK), dgb, dcv, proj4, ws["w_short_conv"])
    dproj = dproj.reshape(24, S, BLK)
    gb["w_in"] = mm_tn(h1, dproj, BLK, "dw_in", shards=4, tn=768, job=comm.job("dw_in"))
    comm.landed("dw_in")
    grad_x, gs["g_mix"] = mm_nt_norm_bwd(dproj, BLK, wb["w_in"], x, ws["g_mix"], dx1, "d_h1", tm=256,
                                         job=comm.job("d_h1", gb))
    comm.landed("d_h1")
    return loss, grad_x, comm.finish(gb), gs


def adamw(w, g, m, v, name):
    R, C = w.shape
    tr = R
    for cand in (256, 352):
        if R > cand and R % cand == 0:
            tr = cand
            break

    def body(w_ref, g_ref, m_ref, v_ref, d_ref, nm_ref, nv_ref):
        gv = g_ref[...]
        mn = ADAM_B1 * m_ref[...] + (1.0 - ADAM_B1) * gv
        vn = ADAM_B2 * v_ref[...] + (1.0 - ADAM_B2) * (gv * gv)
        m_hat = mn / (1.0 - ADAM_B1 ** ADAM_STEP)
        v_hat = vn / (1.0 - ADAM_B2 ** ADAM_STEP)
        d_ref[...] = -ADAM_LR * (m_hat / (jnp.sqrt(v_hat) + ADAM_EPS) + ADAM_WD * w_ref[...])
        nm_ref[...] = mn
        nv_ref[...] = vn

    blk = pl.BlockSpec((tr, C), lambda i: (i, 0))
    return pl.pallas_call(body, grid=(R // tr,), name=name, in_specs=[blk] * 4, out_specs=[blk] * 3,
                          out_shape=[SDS((R, C), F32)] * 3, compiler_params=_cp("parallel"))(w, g, m, v)


BIG = ("w_in", "w_out", "w_xq", "w_xk", "w_xv", "w_xo", "w_up", "w_down")
COL_SHARDED = ("w_in", "w_up")
N_BIG = len(BIG)
ANY = pl.BlockSpec(memory_space=pl.ANY)


def _place():
    x, y, c = lax.axis_index("x"), lax.axis_index("y"), lax.axis_index("c")
    chips = [(1 - x, y), (x, 1 - y), (1 - x, 1 - y)]
    return x, y, c, chips


def _window(full, name, R, C, shard, half):
    r0, nr = (0, R) if half is None else (half * (R // 2), R // 2)
    if name in COL_SHARDED:
        return full.at[pl.ds(r0, nr), pl.ds(shard * C, C)]
    return full.at[pl.ds(shard * R + r0, nr), :]


def place_shard(w, mine_arr, n):
    R, C = w.shape
    col = n in COL_SHARDED

    def body(s_ref, w_ref, o_ref):
        o_ref[...] = w_ref[...].astype(BF16)

    grid_spec = pltpu.PrefetchScalarGridSpec(
        num_scalar_prefetch=1, grid=(1,), in_specs=[pl.BlockSpec((R, C), lambda i, s: (0, 0))],
        out_specs=pl.BlockSpec((R, C), (lambda i, s: (0, s[0])) if col else (lambda i, s: (s[0], 0))))
    return pl.pallas_call(body, grid_spec=grid_spec, name="place_" + n,
                          out_shape=SDS((R, 4 * C) if col else (4 * R, C), BF16),
                          compiler_params=_cp("arbitrary"))(mine_arr, w)


def _gather_jobs(names, shapes):
    nw = len(names)

    def start(full, sems):
        send, recv, fsend, frecv = sems
        x, y, c, chips = _place()
        mine = 2 * x + y
        for w, n in enumerate(names):
            R, C = shapes[w]
            own = _window(full[w], n, R, C, mine, c)
            for j, chip in enumerate(chips):
                pltpu.make_async_remote_copy(src_ref=own, dst_ref=own, send_sem=send.at[w, j], recv_sem=recv.at[w, j],
                                             device_id=(*chip, c), device_id_type=MESH).start()

    def mid(full, sems):
        send, recv, fsend, frecv = sems
        x, y, c, chips = _place()
        sib = (x, y, 1 - c)
        for w, n in enumerate(names):
            R, C = shapes[w]
            for j, chip in enumerate(chips):
                landed = _window(full[w], n, R, C, 2 * chip[0] + chip[1], c)
                pltpu.make_async_remote_copy(src_ref=landed, dst_ref=landed, send_sem=send.at[w, j], recv_sem=recv.at[w, j],
                                             device_id=(*chip, c), device_id_type=MESH).wait_recv()
                pltpu.make_async_remote_copy(src_ref=landed, dst_ref=landed, send_sem=fsend.at[w, j],
                                             recv_sem=frecv.at[w, j], device_id=sib, device_id_type=MESH).start()

    def finish(full, sems):
        send, recv, fsend, frecv = sems
        x, y, c, chips = _place()
        mine = 2 * x + y
        sib = (x, y, 1 - c)
        for w, n in enumerate(names):
            R, C = shapes[w]
            own = _window(full[w], n, R, C, mine, c)
            for j, chip in enumerate(chips):
                landed = _window(full[w], n, R, C, 2 * chip[0] + chip[1], c)
                other = _window(full[w], n, R, C, 2 * chip[0] + chip[1], 1 - c)
                pltpu.make_async_remote_copy(src_ref=other, dst_ref=other, send_sem=fsend.at[w, j], recv_sem=frecv.at[w, j],
                                             device_id=sib, device_id_type=MESH).wait_recv()
                pltpu.make_async_remote_copy(src_ref=own, dst_ref=own, send_sem=send.at[w, j], recv_sem=recv.at[w, j],
                                             device_id=(*chip, c), device_id_type=MESH).wait_send()
                pltpu.make_async_remote_copy(src_ref=landed, dst_ref=landed, send_sem=fsend.at[w, j],
                                             recv_sem=frecv.at[w, j], device_id=sib, device_id_type=MESH).wait_send()

    return start, mid, finish, [pltpu.SemaphoreType.DMA((nw, 3))] * 4


class CommJob:
    def __init__(self, ins, out_shapes, inplace, start, finish, sems, mid=None):
        self.ins, self.out_shapes, self.inplace = list(ins), list(out_shapes), inplace
        self.start, self.mid, self.finish, self.sems = start, mid, finish, list(sems)

    def run(self, phase, ins, outs, sems):
        if phase == 0:
            self.start(ins, outs, sems)
        elif phase == 1:
            if self.mid is not None:
                self.mid(ins, outs, sems)
        else:
            self.finish(ins, outs, sems)


def run_job(name, job):
    n_in, n_out = len(job.ins), len(job.out_shapes)

    def body(*refs):
        ins, outs, sm = refs[:n_in], refs[n_in:n_in + n_out], refs[n_in + n_out:]
        for phase in range(3):
            job.run(phase, ins, outs, sm)

    job.landed = list(pl.pallas_call(
        body, name=name, in_specs=[ANY] * n_in, out_specs=[ANY] * n_out, out_shape=job.out_shapes,
        input_output_aliases={i: i for i in range(n_in)} if job.inplace else {},
        scratch_shapes=job.sems)(*job.ins))
    return job.landed


class Carried:
    def __init__(self, job, n_in, n_out, n_scratch, steps):
        self.job, self.n_in, self.n_out, self.n_scratch, self.steps = job, n_in, n_out, n_scratch, steps
        self.nji = len(job.ins) if job else 0
        self.njo = len(job.out_shapes) if job else 0

    def in_specs(self):
        return [ANY] * self.nji

    def out_specs(self):
        return [ANY] * self.njo

    def out_shapes(self):
        return list(self.job.out_shapes) if self.job else []

    def aliases(self):
        if not (self.job and self.job.inplace):
            return {}
        return {self.n_in + i: self.n_out + i for i in range(self.nji)}

    def sems(self):
        return list(self.job.sems) if self.job else []

    def args(self):
        return list(self.job.ins) if self.job else []

    def split(self, refs):
        a = self.n_in
        b = a + self.nji
        c = b + self.n_out
        d = c + self.njo
        e = d + self.n_scratch
        return refs[:a] + refs[b:c] + refs[d:e], (refs[a:b], refs[c:d], refs[e:])

    def phase(self, phase, step, jrefs):
        if self.job is None:
            return
        at = {0: 0, 1: max(self.steps - 2, 0), 2: self.steps - 1}[phase]

        @pl.when(step == at)
        def _():
            self.job.run(phase, *jrefs)


def _pcall(body, args, *, grid, in_specs, out_specs, out_shape, name, sem, scratch=(), aliases=None, job=None):
    n_in, n_out = len(args), len(out_shape)
    if job is None:
        return pl.pallas_call(
            body, grid=grid, in_specs=list(in_specs), out_specs=list(out_specs), out_shape=list(out_shape), name=name,
            scratch_shapes=list(scratch), input_output_aliases=dict(aliases or {}), compiler_params=_cp(*sem))(*args)
    total = int(np.prod(grid))
    car = Carried(job, n_in, n_out, len(scratch), total)

    def wrapped(*refs):
        main, jrefs = car.split(refs)
        lin = pl.program_id(0)
        for ax in range(1, len(grid)):
            lin = lin * grid[ax] + pl.program_id(ax)
        car.phase(0, lin, jrefs)
        body(*main)
        car.phase(1, lin, jrefs)
        car.phase(2, lin, jrefs)

    res = pl.pallas_call(
        wrapped, grid=grid, in_specs=list(in_specs) + car.in_specs(), out_specs=list(out_specs) + car.out_specs(),
        out_shape=list(out_shape) + car.out_shapes(), name=name, scratch_shapes=list(scratch) + car.sems(),
        input_output_aliases={**dict(aliases or {}), **car.aliases()},
        compiler_params=_cp(*(["arbitrary"] * len(grid))))(*args, *car.args())
    job.landed = list(res[n_out:])
    return list(res[:n_out])


def gather_job(placed, names):
    shapes = []
    for n in names:
        R, C = placed[n].shape
        shapes.append((R, C // 4) if n in COL_SHARDED else (R // 4, C))
    start, mid, finish, sems = _gather_jobs(names, shapes)
    arrays = [placed[n] for n in names]
    return CommJob(arrays, [SDS(a.shape, a.dtype) for a in arrays], True,
                   lambda i, o, s: start(o, s), lambda i, o, s: finish(o, s), sems, mid=lambda i, o, s: mid(o, s))


def pair_exchange_job(grads, names):
    shapes = [grads[n].shape for n in names]

    def copies(ins, outs, sems):
        x, y, c, _ = _place()
        return [pltpu.make_async_remote_copy(
            src_ref=ins[w].at[:, pl.ds((1 - c) * (shapes[w][1] // 2), shapes[w][1] // 2), :], dst_ref=outs[w],
            send_sem=sems[0].at[w], recv_sem=sems[1].at[w], device_id=(x, y, 1 - c), device_id_type=MESH)
            for w in range(len(names))]

    def start(ins, outs, sems):
        for cp in copies(ins, outs, sems):
            cp.start()

    def finish(ins, outs, sems):
        for cp in copies(ins, outs, sems):
            cp.wait()

    return CommJob([grads[n] for n in names], [SDS((4, s[1] // 2, s[2]), F32) for s in shapes], False, start, finish,
                   [pltpu.SemaphoreType.DMA((len(names),))] * 2)


def chip_exchange_job(p16, names):
    shapes = [p16[n].shape for n in names]

    def copies(ins, outs, sems):
        x, y, c, chips = _place()
        return [pltpu.make_async_remote_copy(
            src_ref=ins[w].at[2 * chip[0] + chip[1]], dst_ref=outs[w].at[j],
            send_sem=sems[0].at[w, j], recv_sem=sems[1].at[w, j], device_id=(*chip, c), device_id_type=MESH)
            for w in range(len(names)) for j, chip in enumerate(chips)]

    def start(ins, outs, sems):
        for cp in copies(ins, outs, sems):
            cp.start()

    def finish(ins, outs, sems):
        for cp in copies(ins, outs, sems):
            cp.wait()

    return CommJob([p16[n] for n in names], [SDS((3,) + tuple(s[1:]), BF16) for s in shapes], False, start, finish,
                   [pltpu.SemaphoreType.DMA((len(names), 3))] * 2)


def half_exchange_job(full, names):
    shapes = [full[n].shape for n in names]

    def copies(outs, sems):
        x, y, c, _ = _place()
        cps = []
        for w in range(len(names)):
            Rh = shapes[w][0] // 2
            rows = outs[w].at[pl.ds(c * Rh, Rh), :]
            other = outs[w].at[pl.ds((1 - c) * Rh, Rh), :]
            cps.append((pltpu.make_async_remote_copy(src_ref=rows, dst_ref=rows, send_sem=sems[0].at[w], recv_sem=sems[1].at[w],
                                                     device_id=(x, y, 1 - c), device_id_type=MESH),
                        pltpu.make_async_remote_copy(src_ref=other, dst_ref=other, send_sem=sems[0].at[w], recv_sem=sems[1].at[w],
                                                     device_id=(x, y, 1 - c), device_id_type=MESH)))
        return cps

    def start(ins, outs, sems):
        for send, _ in copies(outs, sems):
            send.start()

    def finish(ins, outs, sems):
        for send, recv in copies(outs, sems):
            recv.wait_recv()
            send.wait_send()

    arrays = [full[n] for n in names]
    return CommJob(arrays, [SDS(a.shape, a.dtype) for a in arrays], True, start, finish,
                   [pltpu.SemaphoreType.DMA((len(names),))] * 2)


def allreduce_small(parts, name):
    n = len(parts)

    def body(*refs):
        ins, outs, bufs = refs[:n], refs[n:2 * n], refs[2 * n:3 * n]
        send, recv = refs[3 * n:]
        x, y, c, _ = _place()
        me = 4 * x + 2 * y + c
        sends = []
        for p in range(n):
            for k in range(1, 8):
                peer = (x ^ (k >> 2), y ^ ((k >> 1) & 1), c ^ (k & 1))
                cp = pltpu.make_async_remote_copy(src_ref=ins[p], dst_ref=bufs[p].at[k], send_sem=send.at[p, k - 1],
                                                  recv_sem=recv.at[p, k - 1], device_id=peer, device_id_type=MESH)
                cp.start()
                sends.append(cp)
            bufs[p][0] = ins[p][...]
        for cp in sends:
            cp.wait_recv()
        for p in range(n):
            acc = bufs[p][me]
            for d in range(1, 8):
                acc = acc + bufs[p][d ^ me]
            outs[p][...] = acc
        for cp in sends:
            cp.wait_send()

    vm = pl.BlockSpec(memory_space=pltpu.VMEM)
    return pl.pallas_call(
        body, name=name, in_specs=[vm] * n, out_specs=[vm] * n, out_shape=[SDS(a.shape, F32) for a in parts],
        scratch_shapes=[pltpu.VMEM((8,) + tuple(a.shape), F32) for a in parts] + [pltpu.SemaphoreType.DMA((n, 7))] * 2,
    )(*parts)


def adamw_multi(ws, gs, ms, vs, name):
    n = len(ws)

    def body(*refs):
        for p in range(n):
            w_ref, g_ref, m_ref, v_ref = (refs[q * n + p] for q in range(4))
            d_ref, nm_ref, nv_ref = (refs[(4 + q) * n + p] for q in range(3))
            gv = g_ref[...]
            mn = ADAM_B1 * m_ref[...] + (1.0 - ADAM_B1) * gv
            vn = ADAM_B2 * v_ref[...] + (1.0 - ADAM_B2) * (gv * gv)
            m_hat = mn / (1.0 - ADAM_B1 ** ADAM_STEP)
            v_hat = vn / (1.0 - ADAM_B2 ** ADAM_STEP)
            d_ref[...] = -ADAM_LR * (m_hat / (jnp.sqrt(v_hat) + ADAM_EPS) + ADAM_WD * w_ref[...])
            nm_ref[...] = mn
            nv_ref[...] = vn

    vm = pl.BlockSpec(memory_space=pltpu.VMEM)
    outs = pl.pallas_call(body, name=name, in_specs=[vm] * (4 * n), out_specs=[vm] * (3 * n),
                          out_shape=[SDS(a.shape, F32) for a in ws] * 3)(*ws, *gs, *ms, *vs)
    return outs[:n], outs[n:2 * n], outs[2 * n:]


def pair_sum(g, xh, sel_arr, name):
    _, R, C = g.shape
    Rh = R // 2

    def body(sel_ref, g_ref, x_ref, p32_ref, p16_ref):
        s = g_ref[...] + x_ref[...]
        p16_ref[...] = s.astype(BF16)

        @pl.when(pl.program_id(0) == sel_ref[0])
        def _():
            p32_ref[...] = s

    same = pl.BlockSpec((None, Rh, C), lambda s, sel: (s, 0, 0))
    grid_spec = pltpu.PrefetchScalarGridSpec(
        num_scalar_prefetch=1, grid=(4,),
        in_specs=[pl.BlockSpec((None, Rh, C), lambda s, sel: (s, sel[1], 0)), same],
        out_specs=[pl.BlockSpec((Rh, C), lambda s, sel: (0, 0)), same])
    return pl.pallas_call(body, grid_spec=grid_spec, name=name, out_shape=[SDS((Rh, C), F32), SDS((4, Rh, C), BF16)],
                          compiler_params=_cp("arbitrary"))(sel_arr, g, xh)


def chip_sum(p32, y, sel_arr, name):
    Rh, C = p32.shape

    def body(s_ref, p_ref, y_ref, o_ref):
        o_ref[...] = ((p_ref[...] + y_ref[0].astype(F32)) + y_ref[1].astype(F32)) + y_ref[2].astype(F32)

    grid_spec = pltpu.PrefetchScalarGridSpec(
        num_scalar_prefetch=1, grid=(1,),
        in_specs=[pl.BlockSpec((Rh, C), lambda i, s: (0, 0)), pl.BlockSpec((3, Rh, C), lambda i, s: (0, 0, 0))],
        out_specs=pl.BlockSpec((Rh, C), lambda i, s: (s[1], 0)))
    return pl.pallas_call(body, grid_spec=grid_spec, name=name, out_shape=SDS((2 * Rh, C), F32),
                          compiler_params=_cp("arbitrary"))(sel_arr, p32, y)


class StepComm:
    FIRST = ("w_in",)
    SQUARE = ("w_out", "w_xq", "w_xk", "w_xv", "w_xo")
    FFN = ("w_up", "w_down")
    PLAN = {
        "proj": ("gather", SQUARE), "attn_fwd": ("gather", FFN),
        "d_h3": ("pair", FFN), "mixer_post_bwd_a": ("pair", SQUARE), "attn_bwd": ("chip", FFN + SQUARE),
        "dw_in": ("half", FFN + SQUARE), "d_h1": ("pair", FIRST), "end_chip": ("chip", FIRST),
    }

    def __init__(self, placed, sel_arr):
        self.placed, self.sel_arr = placed, sel_arr
        self.g, self.p32, self.p16, self.full, self.done, self.jobs = {}, {}, {}, {}, {}, {}

    def gather_first(self):
        return dict(zip(self.FIRST, run_job("gather_first", gather_job(self.placed, self.FIRST))))

    def job(self, tag, gb=None):
        kind, names = self.PLAN[tag]
        if kind == "gather":
            j = gather_job(self.placed, names)
        elif kind == "pair":
            for n in names:
                self.g[n] = gb[n] if gb[n].ndim == 3 else gb[n].reshape(4, gb[n].shape[0] // 4, gb[n].shape[1])
            j = pair_exchange_job(self.g, names)
        elif kind == "chip":
            j = chip_exchange_job(self.p16, names)
        else:
            j = half_exchange_job(self.full, names)
        self.jobs[tag] = j
        return j

    def landed(self, tag, wb=None):
        kind, names = self.PLAN[tag]
        for n, a in zip(names, self.jobs[tag].landed):
            if kind == "gather":
                wb[n] = a
            elif kind == "pair":
                self.p32[n], self.p16[n] = pair_sum(self.g[n], a, self.sel_arr, "pair_sum_" + n)
            elif kind == "chip":
                self.full[n] = chip_sum(self.p32[n], a, self.sel_arr, "chip_sum_" + n)
            else:
                self.done[n] = a

    def finish(self, gb):
        run_job("chip_exchange_last", self.job("end_chip"))
        self.landed("end_chip")
        last = run_job("half_exchange", half_exchange_job(self.full, self.FIRST))
        self.done.update(zip(self.FIRST, last))
        return self.done


class NoComm:
    def job(self, tag, gb=None):
        return None

    def landed(self, tag, wb=None):
        pass

    def finish(self, gb):
        return gb


SMALL = ("rel_bias", "g_mix", "w_short_conv", "g_attn_out", "g_conv_out", "g_xattn", "g_mem", "g_ffn",
         "w_ffn_conv", "b_ffn_conv", "g_final")
SMALL_FULL = {"rel_bias": (8, 32), "g_mix": (1, 1024), "w_short_conv": (3, 512), "g_attn_out": (1, 512), "g_conv_out": (1, 512),
              "g_xattn": (1, 1024), "g_mem": (1, 1024), "g_ffn": (1, 1024), "w_ffn_conv": (3, 5632), "b_ffn_conv": (1, 5632),
              "g_final": (1024,)}
SMALL_SHARDED = ("w_short_conv", "w_ffn_conv")


def _pack(parts):
    flat = jnp.concatenate([p.reshape(-1) for p in parts])
    rows = -(-flat.shape[0] // 1024) * 8
    return jnp.pad(flat, (0, rows * 128 - flat.shape[0])).reshape(rows, 128)


def _unpack(packed, names):
    flat, out, off = packed.reshape(-1), {}, 0
    for n in names:
        size = int(np.prod(SMALL_FULL[n]))
        out[n] = flat[off:off + size].reshape(SMALL_FULL[n])
        off += size
    return out


def kernel(x, mem, rel_bias, g_mix, w_in, w_short_conv, g_attn_out, g_conv_out, w_out, g_xattn, g_mem, w_xq, w_xk, w_xv, w_xo, g_ffn, w_up, w_ffn_conv, b_ffn_conv, w_down, g_final, loss_target, m_rel_bias, m_g_mix, m_w_in, m_w_short_conv, m_g_attn_out, m_g_conv_out, m_w_out, m_g_xattn, m_g_mem, m_w_xq, m_w_xk, m_w_xv, m_w_xo, m_g_ffn, m_w_up, m_w_ffn_conv, m_b_ffn_conv, m_w_down, m_g_final, v_rel_bias, v_g_mix, v_w_in, v_w_short_conv, v_g_attn_out, v_g_conv_out, v_w_out, v_g_xattn, v_g_mem, v_w_xq, v_w_xk, v_w_xv, v_w_xo, v_g_ffn, v_w_up, v_w_ffn_conv, v_b_ffn_conv, v_w_down, v_g_final):
    names = ("rel_bias", "g_mix", "w_in", "w_short_conv", "g_attn_out", "g_conv_out", "w_out", "g_xattn", "g_mem", "w_xq",
             "w_xk", "w_xv", "w_xo", "g_ffn", "w_up", "w_ffn_conv", "b_ffn_conv", "w_down", "g_final")
    W = dict(zip(names, (rel_bias, g_mix, w_in, w_short_conv, g_attn_out, g_conv_out, w_out, g_xattn, g_mem, w_xq, w_xk, w_xv,
                         w_xo, g_ffn, w_up, w_ffn_conv, b_ffn_conv, w_down, g_final)))
    M = dict(zip(names, (m_rel_bias, m_g_mix, m_w_in, m_w_short_conv, m_g_attn_out, m_g_conv_out, m_w_out, m_g_xattn, m_g_mem,
                         m_w_xq, m_w_xk, m_w_xv, m_w_xo, m_g_ffn, m_w_up, m_w_ffn_conv, m_b_ffn_conv, m_w_down, m_g_final)))
    V = dict(zip(names, (v_rel_bias, v_g_mix, v_w_in, v_w_short_conv, v_g_attn_out, v_g_conv_out, v_w_out, v_g_xattn, v_g_mem,
                         v_w_xq, v_w_xk, v_w_xv, v_w_xo, v_g_ffn, v_w_up, v_w_ffn_conv, v_b_ffn_conv, v_w_down, v_g_final)))
    xi, yi, ci = lax.axis_index("x"), lax.axis_index("y"), lax.axis_index("c")
    mine = 2 * xi + yi
    mine_arr = jnp.reshape(mine, (1,)).astype(jnp.int32)
    sel_arr = jnp.stack([mine, ci]).astype(jnp.int32)
    comm = StepComm({n: place_shard(W[n][0], mine_arr, n) for n in BIG}, sel_arr)
    first = jnp.where(ci == 0, 1.0, 0.0).astype(F32)
    emb = []
    for n in SMALL_SHARDED:
        shard = W[n][0]
        full = jnp.zeros(SMALL_FULL[n], F32)
        emb.append(lax.dynamic_update_slice(full, shard * first, (0, mine * shard.shape[1])))
    ws = {n: W[n] for n in SMALL if n not in SMALL_SHARDED}
    ws.update(zip(SMALL_SHARDED, allreduce_small(emb, "gather_small")))

    loss, grad_x, gfull, gs = local_step(x[0], mem[0], loss_target[0], None, ws, comm)

    def as2d(a):
        return a.reshape(1, -1) if a.ndim == 1 else a

    reduced = allreduce_small([as2d(gs[n]) for n in SMALL] + [loss], "reduce_small")
    gsm = dict(zip(SMALL, reduced[:-1]))
    loss = reduced[-1][0, 0]

    grads, delta, new_m, new_v = {}, {}, {}, {}
    for n in BIG:
        grads[n] = gfull[n][None]
        d, nm, nv = adamw(W[n][0], gfull[n], M[n][0], V[n][0], "adamw_" + n)
        delta[n], new_m[n], new_v[n] = d[None], nm[None], nv[None]
    for n in SMALL_SHARDED:
        wid = W[n].shape[2]
        gsm[n] = lax.dynamic_slice(gsm[n], (0, mine * wid), (3, wid))

    def own(a, n):
        return a[0] if n in SMALL_SHARDED else as2d(a)

    d, nm, nv = adamw_multi([own(W[n], n) for n in SMALL], [gsm[n] for n in SMALL], [own(M[n], n) for n in SMALL],
                            [own(V[n], n) for n in SMALL], "adamw_small")
    for i, n in enumerate(SMALL):
        shape = W[n].shape
        grads[n], delta[n], new_m[n], new_v[n] = (a.reshape(shape) for a in (gsm[n], d[i], nm[i], nv[i]))
    return (loss, grad_x[None], *[grads[n] for n in names], *[delta[n] for n in names],
            *[new_m[n] for n in names], *[new_v[n] for n in names])
```

```python
import functools
import math

import numpy as np
import jax
import jax.numpy as jnp
from jax import lax
from jax.experimental import pallas as pl
from jax.experimental.pallas import tpu as pltpu

F32 = jnp.float32
BF16 = jnp.bfloat16
SDS = jax.ShapeDtypeStruct
MESH = pl.DeviceIdType.MESH

D_MODEL = 1024
ATTN_W = 512
N_HEADS = 8
BLK = 128
PATTERNS = ((128, 1), (512, 4), (2048, 16))
N_BUCKETS = 32
D_FF = 2816
N_MEM_HEADS = 4
MEM_HD = 256
EPS = 1e-6
NEG = -1e30
VMEM_LIMIT = 56 * 1024 * 1024

ADAM_LR, ADAM_B1, ADAM_B2, ADAM_EPS, ADAM_WD, ADAM_STEP = 0.001, 0.9, 0.999, 1e-08, 0.01, 10


def _cp(*sem):
    return pltpu.CompilerParams(dimension_semantics=sem, vmem_limit_bytes=VMEM_LIMIT)


def _dot(a, b):
    return jnp.dot(a, b, preferred_element_type=F32)


def _dot_nt(a, b):
    return lax.dot_general(a, b, (((1,), (1,)), ((), ())), preferred_element_type=F32)


def _dot_tn(a, b):
    return lax.dot_general(a, b, (((0,), (0,)), ((), ())), preferred_element_type=F32)


def _rsum(x):
    return jnp.sum(x, axis=1, keepdims=True)


def rmsnorm_fwd(x, g, name):
    S, Dm = x.shape
    tm = min(S, 512)

    def body(x_ref, g_ref, o_ref):
        xv = x_ref[...]
        r = lax.rsqrt(jnp.mean(xv * xv, axis=1, keepdims=True) + EPS)
        o_ref[...] = (xv * r * g_ref[...]).astype(o_ref.dtype)

    return pl.pallas_call(
        body, grid=(S // tm,), name=name,
        in_specs=[pl.BlockSpec((tm, Dm), lambda i: (i, 0)), pl.BlockSpec((1, Dm), lambda i: (0, 0))],
        out_specs=pl.BlockSpec((tm, Dm), lambda i: (i, 0)),
        out_shape=SDS((S, Dm), BF16), compiler_params=_cp("parallel"))(x, g)


def rmsnorm_bwd(x, g, dh, res, name, job=None):
    S, Dm = x.shape
    tm = min(S, 512)
    want_dx = res is not None

    def body(*refs):
        if want_dx:
            x_ref, g_ref, dh_ref, res_ref, dx_ref, dg_ref = refs
        else:
            x_ref, g_ref, dh_ref, dg_ref = refs
        i = pl.program_id(0)
        xv = x_ref[...]
        r = lax.rsqrt(jnp.mean(xv * xv, axis=1, keepdims=True) + EPS)
        xh = xv * r
        dh = dh_ref[...].astype(F32)
        if want_dx:
            gd = dh * g_ref[...]
            dx_ref[...] = res_ref[...] + r * (gd - xh * jnp.mean(gd * xh, axis=1, keepdims=True))

        @pl.when(i == 0)
        def _():
            dg_ref[...] = jnp.zeros_like(dg_ref)

        dg_ref[...] += jnp.sum(dh * xh, axis=0, keepdims=True)

    row = pl.BlockSpec((tm, Dm), lambda i: (i, 0))
    vec = pl.BlockSpec((1, Dm), lambda i: (0, 0))
    if want_dx:
        return _pcall(body, [x, g, dh, res], grid=(S // tm,), name=name, in_specs=[row, vec, row, row], out_specs=[row, vec],
                      out_shape=[SDS((S, Dm), F32), SDS((1, Dm), F32)], sem=("arbitrary",), job=job)
    return pl.pallas_call(
        body, grid=(S // tm,), name=name, in_specs=[row, vec, row], out_specs=vec,
        out_shape=SDS((1, Dm), F32), compiler_params=_cp("arbitrary"))(x, g, dh)


def _col_spec(bw, tn, rows, row_of, col_of):
    if bw is None:
        return pl.BlockSpec((rows, tn), lambda *g: (row_of(*g), col_of(*g)))
    if tn % bw == 0:
        return pl.BlockSpec((tn // bw, rows, bw), lambda *g: (col_of(*g), row_of(*g), 0))
    per = bw // tn
    return pl.BlockSpec((None, rows, tn), lambda *g: (col_of(*g) // per, row_of(*g), col_of(*g) % per))


def _read_cols(ref, bw, tn):
    if bw is None or tn % bw != 0:
        return ref[...]
    if tn == bw:
        return ref[0]
    return jnp.concatenate([ref[q] for q in range(tn // bw)], axis=1)


def _write_cols(ref, bw, tn, val):
    if bw is None or tn % bw != 0:
        ref[...] = val.astype(ref.dtype)
    else:
        for q in range(tn // bw):
            ref[q] = val[:, q * bw:(q + 1) * bw].astype(ref.dtype)


def mm_nn(a, b, name, *, res=None, out_dtype=F32, out_bw=None, tm=1024, tn=512, job=None):
    M, K = a.shape
    N = b.shape[1]
    tm = min(tm, M)

    def body(*refs):
        if res is None:
            a_ref, b_ref, o_ref = refs
        else:
            a_ref, b_ref, r_ref, o_ref = refs
        acc = _dot(a_ref[...].astype(BF16), b_ref[...])
        if res is not None:
            acc = acc + r_ref[...]
        _write_cols(o_ref, out_bw, tn, acc)

    ri, ci = (lambda i, j: i), (lambda i, j: j)
    in_specs = [pl.BlockSpec((tm, K), lambda i, j: (i, 0)), pl.BlockSpec((K, tn), lambda i, j: (0, j))]
    args = [a, b]
    if res is not None:
        in_specs.append(pl.BlockSpec((tm, tn), lambda i, j: (i, j)))
        args.append(res)
    oshape = (M, N) if out_bw is None else (N // out_bw, M, out_bw)
    return _pcall(body, args, grid=(M // tm, N // tn), name=name, in_specs=in_specs,
                  out_specs=[_col_spec(out_bw, tn, tm, ri, ci)], out_shape=[SDS(oshape, out_dtype)],
                  sem=("parallel", "parallel"), job=job)[0]


def mm_nt(a, a_bw, b, name, *, res=None, out_dtype=F32, tm=1024, tn=512, job=None):
    if a_bw is None:
        M, K = a.shape
    else:
        M, K = a.shape[1], a.shape[0] * a_bw
    N = b.shape[0]
    tm = min(tm, M)

    def body(*refs):
        if res is None:
            a_ref, b_ref, o_ref = refs
        else:
            a_ref, b_ref, r_ref, o_ref = refs
        av = _read_cols(a_ref, a_bw, K).astype(BF16)
        acc = _dot_nt(av, b_ref[...])
        if res is not None:
            acc = acc + r_ref[...]
        o_ref[...] = acc.astype(o_ref.dtype)

    in_specs = [_col_spec(a_bw, K, tm, lambda i, j: i, lambda i, j: 0), pl.BlockSpec((tn, K), lambda i, j: (j, 0))]
    args = [a, b]
    if res is not None:
        in_specs.append(pl.BlockSpec((tm, tn), lambda i, j: (i, j)))
        args.append(res)
    return _pcall(body, args, grid=(M // tm, N // tn), name=name, in_specs=in_specs,
                  out_specs=[pl.BlockSpec((tm, tn), lambda i, j: (i, j))], out_shape=[SDS((M, N), out_dtype)],
                  sem=("parallel", "parallel"), job=job)[0]


def mm_nt_norm_bwd(a, a_bw, b, x, g, res, name, *, tm=512, job=None):
    if a_bw is None:
        M, K = a.shape
    else:
        M, K = a.shape[1], a.shape[0] * a_bw
    N = b.shape[0]

    def body(a_ref, b_ref, x_ref, g_ref, r_ref, dx_ref, dg_ref):
        dh = _dot_nt(_read_cols(a_ref, a_bw, K).astype(BF16), b_ref[...])
        xv = x_ref[...]
        r = lax.rsqrt(jnp.mean(xv * xv, axis=1, keepdims=True) + EPS)
        xh = xv * r
        gd = dh * g_ref[...]
        dx_ref[...] = r_ref[...] + r * (gd - xh * jnp.mean(gd * xh, axis=1, keepdims=True))

        @pl.when(pl.program_id(0) == 0)
        def _():
            dg_ref[...] = jnp.zeros_like(dg_ref)

        dg_ref[...] += jnp.sum(dh * xh, axis=0, keepdims=True)

    row = pl.BlockSpec((tm, N), lambda i: (i, 0))
    vec = pl.BlockSpec((1, N), lambda i: (0, 0))
    return _pcall(body, [a, b, x, g, res], grid=(M // tm,), name=name,
                  in_specs=[_col_spec(a_bw, K, tm, lambda i: i, lambda i: 0), pl.BlockSpec((N, K), lambda i: (0, 0)), row, vec, row],
                  out_specs=[row, vec], out_shape=[SDS((M, N), F32), SDS((1, N), F32)], sem=("arbitrary",), job=job)


def mm_tn(a, b, b_bw, name, *, shards=None, tm=1024, tn=512, ts=512, job=None):
    S, Ka = a.shape
    N = b.shape[1] if b_bw is None else b.shape[0] * b_bw
    ts = min(ts, S)
    tm = min(tm, Ka)

    def body(a_ref, b_ref, o_ref):
        @pl.when(pl.program_id(2) == 0)
        def _():
            o_ref[...] = jnp.zeros_like(o_ref)

        bv = _read_cols(b_ref, b_bw, tn).astype(BF16)
        o_ref[...] += _dot_tn(a_ref[...].astype(BF16), bv)

    in_specs = [pl.BlockSpec((ts, tm), lambda i, j, k: (k, i)),
                _col_spec(b_bw, tn, ts, lambda i, j, k: k, lambda i, j, k: j)]
    if shards is None:
        out_spec, oshape = pl.BlockSpec((tm, tn), lambda i, j, k: (i, j)), (Ka, N)
    else:
        per = (N // shards) // tn
        out_spec = pl.BlockSpec((None, tm, tn), lambda i, j, k: (j // per, i, j % per))
        oshape = (shards, Ka, N // shards)
    return _pcall(body, [a, b], grid=(Ka // tm, N // tn, S // ts), name=name, in_specs=in_specs, out_specs=[out_spec],
                  out_shape=[SDS(oshape, F32)], sem=("parallel", "parallel", "arbitrary"), job=job)[0]


def _bucket_tables():
    out = np.zeros((3, 2, BLK, BLK), np.int32)
    qi = np.arange(BLK)[:, None]
    for p, (win, dil) in enumerate(PATTERNS):
        w = win // dil
        assert w == BLK
        for half in range(2):
            kj = np.arange(BLK)[None, :] + half * BLK
            steps = qi + w - kj
            valid = (steps >= 0) & (steps <= w)
            dist = np.clip(steps, 0, w) * dil
            dd = np.maximum(dist, 1).astype(np.float32)
            large = 16 + (np.log(dd / np.float32(16)) / np.float32(math.log(2048 / 16)) * np.float32(16)).astype(np.int32)
            large = np.minimum(large, N_BUCKETS - 1)
            out[p, half] = np.where(valid, np.where(dist < 16, dist, large), -1)
    return out


def bias_tables(rel_bias):
    bkt = jnp.asarray(_bucket_tables())

    def body(rb_ref, bkt_ref, o_ref):
        h = pl.program_id(1)
        for half in range(2):
            bk = bkt_ref[half]
            acc = jnp.full((BLK, BLK), NEG, F32)
            for b in range(N_BUCKETS):
                acc = jnp.where(bk == b, rb_ref[h, b], acc)
            o_ref[half] = acc

    return pl.pallas_call(
        body, grid=(3, N_HEADS), name="bias_tables",
        in_specs=[pl.BlockSpec(memory_space=pltpu.SMEM),
                  pl.BlockSpec((None, 2, BLK, BLK), lambda p, h: (p, 0, 0, 0))],
        out_specs=pl.BlockSpec((None, None, 2, BLK, BLK), lambda p, h: (p, h, 0, 0, 0)),
        out_shape=SDS((3, N_HEADS, 2, BLK, BLK), F32), compiler_params=_cp("parallel", "parallel"))(rel_bias, bkt)


def bias_tables_bwd(ds_sum):
    bkt = jnp.asarray(_bucket_tables())

    def body(ds_ref, bkt_ref, o_ref):
        h = pl.program_id(0)

        @pl.when(h == 0)
        def _():
            o_ref[...] = jnp.zeros_like(o_ref)

        rows = lax.broadcasted_iota(jnp.int32, (N_HEADS, N_BUCKETS), 0)
        cols = lax.broadcasted_iota(jnp.int32, (N_HEADS, N_BUCKETS), 1)
        acc = jnp.zeros((N_HEADS, N_BUCKETS), F32)
        for b in range(N_BUCKETS):
            tot = jnp.zeros((1, 1), F32)
            for p in range(3):
                for half in range(2):
                    t = jnp.where(bkt_ref[p, half] == b, ds_ref[p, half], 0.0)
                    tot = tot + jnp.sum(_rsum(t), axis=0, keepdims=True)
            acc = acc + jnp.where((rows == h) & (cols == b), tot, 0.0)
        o_ref[...] += acc

    return pl.pallas_call(
        body, grid=(N_HEADS,), name="bias_tables_bwd",
        in_specs=[pl.BlockSpec((3, None, 2, BLK, BLK), lambda h: (0, h, 0, 0, 0)),
                  pl.BlockSpec((3, 2, BLK, BLK), lambda h: (0, 0, 0, 0))],
        out_specs=pl.BlockSpec((N_HEADS, N_BUCKETS), lambda h: (0, 0)),
        out_shape=SDS((N_HEADS, N_BUCKETS), F32), compiler_params=_cp("arbitrary"))(ds_sum, bkt)


def _rows(start, dil):
    if dil == 1:
        return pl.ds(pl.multiple_of(start, BLK), BLK)
    return pl.ds(start, BLK, stride=dil)


GRP = 8


def _group_rows(i, dil):
    t = lax.broadcasted_iota(jnp.int32, (GRP, 1, 1), 0)
    n0 = i * GRP
    b0, r0 = n0 // dil, n0 % dil
    cur, prv = [], []
    for j in range(GRP):
        b, r = (b0 + j // dil, j % dil) if GRP > dil else (b0, r0 + j)
        cur.append(b * (BLK * dil) + r)
        prv.append(jnp.maximum(b - 1, 0) * (BLK * dil) + r)
    pen = jnp.where(b0 + (r0 + t) // dil > 0, 0.0, NEG)
    return [_rows(s, dil) for s in cur], [_rows(s, dil) for s in prv], pen


def _load_group(ref, rows):
    return jnp.stack([ref[r, :] for r in rows])


def _bdot_nt(a, b):
    return lax.dot_general(a, b, (((2,), (2,)), ((0,), (0,))), preferred_element_type=F32)


def _bdot(a, b):
    return lax.dot_general(a, b, (((2,), (1,)), ((0,), (0,))), preferred_element_type=F32)


def _bdot_tn(a, b):
    return lax.dot_general(a, b, (((1,), (1,)), ((0,), (0,))), preferred_element_type=F32)


def _lsum(x):
    return jnp.sum(x, axis=-1, keepdims=True)


def attn_fwd(qkv, bias, job=None):
    S = qkv.shape[3]
    nblk = S // BLK
    car = Carried(job, 2, 2, 1, 4)

    def body(*refs):
        (qkv_ref, bias_ref, o_ref, lse_ref, l_ref), jrefs = car.split(refs)
        car.phase(0, pl.program_id(0), jrefs)
        lane = lax.broadcasted_iota(jnp.int32, (GRP, BLK, BLK), 2)
        lo = lane < 64
        masks = (lo, jnp.logical_not(lo))
        q_ref, k_ref, v_ref = qkv_ref.at[0], qkv_ref.at[1], qkv_ref.at[2]
        for p, (_, dil) in enumerate(PATTERNS):
            def step(i, carry, p=p, dil=dil):
                rc, rp, pen = _group_rows(i, dil)
                q2 = _load_group(q_ref, rc) * 0.125
                kc = _load_group(k_ref, rc).astype(BF16)
                kp = _load_group(k_ref, rp).astype(BF16)
                vc = _load_group(v_ref, rc)
                vp = _load_group(v_ref, rp)
                if p > 0:
                    acc_old, m_old, l_old = _load_group(o_ref, rc), _load_group(lse_ref, rc), _load_group(l_ref, rc)
                pv = jnp.zeros((GRP, BLK, BLK), F32)
                m_new, l_new, alpha = [], [], []
                for h in range(2):
                    qh = jnp.where(masks[h], q2, 0.0).astype(BF16)
                    sp = _bdot_nt(qh, kp) + bias_ref[p, h, 0][None] + pen
                    sc = _bdot_nt(qh, kc) + bias_ref[p, h, 1][None]
                    mn = jnp.max(jnp.maximum(sp, sc), axis=-1, keepdims=True)
                    if p > 0:
                        mo = m_old[:, :, 64 * h:64 * h + 1]
                        mn = jnp.maximum(mn, mo)
                        al = jnp.exp(mo - mn)
                        alpha.append(al)
                    pp = jnp.exp(sp - mn)
                    pc = jnp.exp(sc - mn)
                    ln = _lsum(pp + pc)
                    if p > 0:
                        ln = ln + al * l_old[:, :, 64 * h:64 * h + 1]
                    vhp = jnp.where(masks[h], vp, 0.0).astype(BF16)
                    vhc = jnp.where(masks[h], vc, 0.0).astype(BF16)
                    pv = pv + _bdot(pp.astype(BF16), vhp) + _bdot(pc.astype(BF16), vhc)
                    m_new.append(mn)
                    l_new.append(ln)
                if p > 0:
                    pv = pv + acc_old * jnp.where(lo, alpha[0], alpha[1])
                m_t = jnp.where(lo, m_new[0], m_new[1])
                l_t = jnp.where(lo, l_new[0], l_new[1])
                for j, r in enumerate(rc):
                    o_ref[r, :] = pv[j]
                    lse_ref[r, :] = m_t[j]
                    l_ref[r, :] = l_t[j]
                return carry

            lax.fori_loop(0, nblk // GRP, step, 0)

        def fin(i, carry):
            rows = pl.ds(pl.multiple_of(i * 512, 512), 512)
            lv = l_ref[rows, :]
            o_ref[rows, :] = o_ref[rows, :] / lv
            lse_ref[rows, :] = lse_ref[rows, :] + jnp.log(lv)
            return carry

        lax.fori_loop(0, S // 512, fin, 0)
        car.phase(1, pl.program_id(0), jrefs)
        car.phase(2, pl.program_id(0), jrefs)

    outs = pl.pallas_call(
        body, grid=(4,), name="attn_fwd",
        in_specs=[pl.BlockSpec((None, 3, None, S, BLK), lambda g: (0, 0, g, 0, 0)),
                  pl.BlockSpec((3, 2, 2, BLK, BLK), lambda g: (0, g, 0, 0, 0))] + car.in_specs(),
        out_specs=[pl.BlockSpec((None, S, BLK), lambda g: (g, 0, 0)),
                   pl.BlockSpec((None, S, BLK), lambda g: (g, 0, 0))] + car.out_specs(),
        out_shape=[SDS((4, S, BLK), F32), SDS((4, S, BLK), F32)] + car.out_shapes(),
        input_output_aliases=car.aliases(),
        scratch_shapes=[pltpu.VMEM((S, BLK), F32)] + car.sems(), compiler_params=_cp("arbitrary"))(qkv, bias, *car.args())
    if job is not None:
        job.landed = list(outs[2:])
    return outs[0], outs[1]


def attn_bwd(qkv, bias, o, lse, do, job=None):
    S = qkv.shape[3]
    nblk = S // BLK
    car = Carried(job, 5, 2, 0, 4)

    def body(*refs):
        (qkv_ref, bias_ref, o_ref, lse_ref, do_ref, dqkv_ref, ds_ref), jrefs = car.split(refs)
        car.phase(0, pl.program_id(0), jrefs)
        lane = lax.broadcasted_iota(jnp.int32, (GRP, BLK, BLK), 2)
        lo = lane < 64
        masks = (lo, jnp.logical_not(lo))
        dqkv_ref[...] = jnp.zeros_like(dqkv_ref)
        ds_ref[...] = jnp.zeros_like(ds_ref)
        q_ref, k_ref, v_ref = qkv_ref.at[0], qkv_ref.at[1], qkv_ref.at[2]
        dq_ref, dk_ref, dv_ref = dqkv_ref.at[0], dqkv_ref.at[1], dqkv_ref.at[2]
        for p, (_, dil) in enumerate(PATTERNS):
            def step(i, carry, p=p, dil=dil):
                rc, rp, pen = _group_rows(i, dil)
                q2 = _load_group(q_ref, rc) * 0.125
                kc = _load_group(k_ref, rc).astype(BF16)
                kp = _load_group(k_ref, rp).astype(BF16)
                vc = _load_group(v_ref, rc).astype(BF16)
                vp = _load_group(v_ref, rp).astype(BF16)
                dot = _load_group(do_ref, rc)
                lset = _load_group(lse_ref, rc)
                prod = dot * _load_group(o_ref, rc)
                zero = jnp.zeros((GRP, BLK, BLK), F32)
                dq, dkc, dkp, dvc, dvp = zero, zero, zero, zero, zero
                for h in range(2):
                    qh = jnp.where(masks[h], q2, 0.0).astype(BF16)
                    doh = jnp.where(masks[h], dot, 0.0).astype(BF16)
                    delta = _lsum(jnp.where(masks[h], prod, 0.0))
                    lh = lset[:, :, 64 * h:64 * h + 1]
                    pp = jnp.exp(_bdot_nt(qh, kp) + bias_ref[p, h, 0][None] + pen - lh)
                    pc = jnp.exp(_bdot_nt(qh, kc) + bias_ref[p, h, 1][None] - lh)
                    dsp = pp * (_bdot_nt(doh, vp) - delta)
                    dsc = pc * (_bdot_nt(doh, vc) - delta)
                    ds_ref[p, h, 0] += jnp.sum(dsp, axis=0)
                    ds_ref[p, h, 1] += jnp.sum(dsc, axis=0)
                    dspb, dscb = dsp.astype(BF16), dsc.astype(BF16)
                    dq = dq + jnp.where(masks[h], _bdot(dspb, kp) + _bdot(dscb, kc), 0.0)
                    dkp = dkp + _bdot_tn(dspb, qh)
                    dkc = dkc + _bdot_tn(dscb, qh)
                    dvp = dvp + _bdot_tn(pp.astype(BF16), doh)
                    dvc = dvc + _bdot_tn(pc.astype(BF16), doh)
                for j in range(GRP):
                    dq_ref[rc[j], :] += dq[j] * 0.125
                    dk_ref[rp[j], :] += dkp[j]
                    dk_ref[rc[j], :] += dkc[j]
                    dv_ref[rp[j], :] += dvp[j]
                    dv_ref[rc[j], :] += dvc[j]
                return carry

            lax.fori_loop(0, nblk // GRP, step, 0)
        car.phase(1, pl.program_id(0), jrefs)
        car.phase(2, pl.program_id(0), jrefs)

    blk = pl.BlockSpec((None, S, BLK), lambda g: (g, 0, 0))
    outs = pl.pallas_call(
        body, grid=(4,), name="attn_bwd",
        in_specs=[pl.BlockSpec((None, 3, None, S, BLK), lambda g: (0, 0, g, 0, 0)),
                  pl.BlockSpec((3, 2, 2, BLK, BLK), lambda g: (0, g, 0, 0, 0)), blk, blk, blk] + car.in_specs(),
        out_specs=[pl.BlockSpec((None, 3, None, S, BLK), lambda g: (0, 0, g, 0, 0)),
                   pl.BlockSpec((3, 2, 2, BLK, BLK), lambda g: (0, g, 0, 0, 0))] + car.out_specs(),
        out_shape=[SDS((2, 3, 4, S, BLK), F32), SDS((3, N_HEADS, 2, BLK, BLK), F32)] + car.out_shapes(),
        input_output_aliases=car.aliases(), scratch_shapes=car.sems(),
        compiler_params=_cp("arbitrary"))(qkv, bias, o, lse, do, *car.args())
    if job is not None:
        job.landed = list(outs[2:])
    return outs[0], outs[1]


def _shift_down(u, k, halo):
    n = u.shape[0]
    row = lax.broadcasted_iota(jnp.int32, u.shape, 0)
    out = pltpu.roll(u, k, 0)
    hn = halo.shape[0]
    for j in range(k):
        out = jnp.where(row == j, halo[hn - k + j:hn - k + j + 1, :], out)
    return out


def _shift_up(u, k, halo):
    n = u.shape[0]
    row = lax.broadcasted_iota(jnp.int32, u.shape, 0)
    out = pltpu.roll(u, n - k, 0)
    for j in range(k):
        out = jnp.where(row == n - k + j, halo[j:j + 1, :], out)
    return out


def _conv3(u, halo, w0, w1, w2):
    return _shift_down(u, 2, halo) * w0 + _shift_down(u, 1, halo) * w1 + u * w2


def _colsum(x):
    return jnp.sum(x, axis=0, keepdims=True)


def _mixer_specs(S, tm):
    conv = pl.BlockSpec((None, 12, tm, BLK), lambda i: (1, 0, i, 0))
    halo = pl.BlockSpec((None, 12, 8, BLK), lambda i: (1, 0, jnp.maximum(i * (tm // 8) - 1, 0), 0))
    ob = pl.BlockSpec((4, tm, BLK), lambda i: (0, i, 0))
    return conv, halo, ob


def _mixer_recompute(i, o_ref, pr_ref, ph_ref, w_ref):
    ob = [o_ref[q] for q in range(4)]
    gb = [pr_ref[q] for q in range(4)]
    gc = [pr_ref[4 + q] for q in range(4)]
    xi = [pr_ref[8 + q] for q in range(4)]
    keep = jnp.where(i > 0, 1.0, 0.0)
    u = [gc[q] * xi[q] for q in range(4)]
    hu = [ph_ref[4 + q] * ph_ref[8 + q] * keep for q in range(4)]
    w = [[w_ref[k:k + 1, q * BLK:(q + 1) * BLK] for k in range(3)] for q in range(4)]
    cv = [_conv3(u[q], hu[q], *w[q]) for q in range(4)]
    return ob, gb, gc, xi, u, hu, cv, w


def _rms_blocks(blocks):
    ss = sum(_rsum(b * b) for b in blocks)
    return lax.rsqrt(ss / (BLK * len(blocks)) + EPS)


def mixer_post_fwd(o, proj, w_sc, g_a, g_c):
    S = o.shape[1]
    tm = 256

    def body(o_ref, pr_ref, ph_ref, w_ref, ga_ref, gc_ref, m_ref):
        i = pl.program_id(0)
        ob, gb, _, _, _, _, cv, _ = _mixer_recompute(i, o_ref, pr_ref, ph_ref, w_ref)
        conv = [gb[q] * cv[q] for q in range(4)]
        ra, rc = _rms_blocks(ob), _rms_blocks(conv)
        for q in range(4):
            sl = slice(q * BLK, (q + 1) * BLK)
            m_ref[:, q * BLK:(q + 1) * BLK] = (ob[q] * ra * ga_ref[:, sl]).astype(BF16)
            m_ref[:, ATTN_W + q * BLK:ATTN_W + (q + 1) * BLK] = (conv[q] * rc * gc_ref[:, sl]).astype(BF16)

    conv_s, halo_s, ob_s = _mixer_specs(S, tm)
    full = lambda r, c: pl.BlockSpec((r, c), lambda i: (0, 0))
    return pl.pallas_call(
        body, grid=(S // tm,), name="mixer_post_fwd",
        in_specs=[ob_s, conv_s, halo_s, full(3, 512), full(1, 512), full(1, 512)],
        out_specs=pl.BlockSpec((tm, D_MODEL), lambda i: (i, 0)), out_shape=SDS((S, D_MODEL), BF16),
        compiler_params=_cp("parallel"))(o, proj, proj, w_sc, g_a, g_c)


def mixer_post_bwd_a(dmixed, o, proj, w_sc, g_a, g_c, job=None):
    S = o.shape[1]
    tm = 256

    def body(dm_ref, o_ref, pr_ref, ph_ref, w_ref, ga_ref, gc_ref, do_ref, dgb_ref, dcv_ref, dga_ref, dgc_ref):
        i = pl.program_id(0)
        ob, gb, _, _, _, _, cv, _ = _mixer_recompute(i, o_ref, pr_ref, ph_ref, w_ref)
        conv = [gb[q] * cv[q] for q in range(4)]
        ra, rc = _rms_blocks(ob), _rms_blocks(conv)

        @pl.when(i == 0)
        def _():
            dga_ref[...] = jnp.zeros_like(dga_ref)
            dgc_ref[...] = jnp.zeros_like(dgc_ref)

        for blocks, r, g_ref, off, dg_ref, is_attn in ((ob, ra, ga_ref, 0, dga_ref, True), (conv, rc, gc_ref, ATTN_W, dgc_ref, False)):
            xh = [blocks[q] * r for q in range(4)]
            dm = [dm_ref[:, off + q * BLK:off + (q + 1) * BLK].astype(F32) for q in range(4)]
            gd = [dm[q] * g_ref[:, q * BLK:(q + 1) * BLK] for q in range(4)]
            mean = sum(_rsum(gd[q] * xh[q]) for q in range(4)) / (4 * BLK)
            for q in range(4):
                dg_ref[:, q * BLK:(q + 1) * BLK] += _colsum(dm[q] * xh[q])
                dx = r * (gd[q] - xh[q] * mean)
                if is_attn:
                    do_ref[q] = dx
                else:
                    dgb_ref[q] = dx * cv[q]
                    dcv_ref[q] = dx * gb[q]

    conv_s, halo_s, ob_s = _mixer_specs(S, tm)
    full = lambda r, c: pl.BlockSpec((r, c), lambda i: (0, 0))
    return _pcall(
        body, [dmixed, o, proj, proj, w_sc, g_a, g_c], grid=(S // tm,), name="mixer_post_bwd_a",
        in_specs=[pl.BlockSpec((tm, D_MODEL), lambda i: (i, 0)), ob_s, conv_s, halo_s, full(3, 512), full(1, 512), full(1, 512)],
        out_specs=[ob_s, ob_s, ob_s, full(1, 512), full(1, 512)],
        out_shape=[SDS((4, S, BLK), F32)] * 3 + [SDS((1, 512), F32)] * 2, sem=("arbitrary",), job=job)


def mixer_post_bwd_b(dproj, dgb, dcv, proj, w_sc):
    S = proj.shape[2]
    tm = 256
    last = S // 8 - 1

    def body(dp_in, dgb_ref, dcv_ref, dn_ref, pr_ref, w_ref, dp_ref, dw_ref):
        i = pl.program_id(0)
        keep_next = jnp.where(i < pl.num_programs(0) - 1, 1.0, 0.0)

        @pl.when(i == 0)
        def _():
            dw_ref[...] = jnp.zeros_like(dw_ref)

        for q in range(4):
            sl = slice(q * BLK, (q + 1) * BLK)
            gc, xi = pr_ref[4 + q], pr_ref[8 + q]
            u = gc * xi
            dcv = dcv_ref[q]
            dn = dn_ref[q] * keep_next
            d1, d2 = _shift_up(dcv, 1, dn), _shift_up(dcv, 2, dn)
            w0, w1, w2 = (w_ref[k:k + 1, sl] for k in range(3))
            du = dcv * w2 + d1 * w1 + d2 * w0
            dw_ref[0:1, sl] += _colsum(d2 * u)
            dw_ref[1:2, sl] += _colsum(d1 * u)
            dw_ref[2:3, sl] += _colsum(dcv * u)
            dp_ref[q] = dgb_ref[q]
            dp_ref[4 + q] = du * xi
            dp_ref[8 + q] = du * gc

    conv_s, _, ob_s = _mixer_specs(S, tm)
    nxt = pl.BlockSpec((4, 8, BLK), lambda i: (0, jnp.minimum((i + 1) * (tm // 8), last), 0))
    full = lambda r, c: pl.BlockSpec((r, c), lambda i: (0, 0))
    return pl.pallas_call(
        body, grid=(S // tm,), name="mixer_post_bwd_b",
        in_specs=[pl.BlockSpec(memory_space=pl.ANY), ob_s, ob_s, nxt, conv_s, full(3, 512)],
        out_specs=[conv_s, full(3, 512)],
        out_shape=[SDS(dproj.shape, F32), SDS((3, 512), F32)],
        input_output_aliases={0: 0}, compiler_params=_cp("arbitrary"))(dproj, dgb, dcv, dcv, proj, w_sc)


def xattn_fwd(q, k, v):
    S = q.shape[0]
    tm = 512
    scale = MEM_HD ** -0.5

    def body(q_ref, k_ref, v_ref, o_ref):
        for h in range(N_MEM_HEADS):
            sl = slice(h * MEM_HD, (h + 1) * MEM_HD)
            s = _dot_nt(q_ref[:, sl], k_ref[:, sl]) * scale
            p = jnp.exp(s - jnp.max(s, axis=1, keepdims=True))
            p = p / _rsum(p)
            o_ref[:, sl] = _dot(p.astype(BF16), v_ref[:, sl]).astype(BF16)

    row = pl.BlockSpec((tm, D_MODEL), lambda i: (i, 0))
    kv = pl.BlockSpec(k.shape, lambda i: (0, 0))
    return pl.pallas_call(body, grid=(S // tm,), name="xattn_fwd", in_specs=[row, kv, kv], out_specs=row,
                          out_shape=SDS((S, D_MODEL), BF16), compiler_params=_cp("parallel"))(q, k, v)


def xattn_bwd(q, k, v, do):
    S = q.shape[0]
    tm = 512
    scale = MEM_HD ** -0.5

    def body(q_ref, k_ref, v_ref, do_ref, dq_ref, dk_ref, dv_ref):
        @pl.when(pl.program_id(0) == 0)
        def _():
            dk_ref[...] = jnp.zeros_like(dk_ref)
            dv_ref[...] = jnp.zeros_like(dv_ref)

        for h in range(N_MEM_HEADS):
            sl = slice(h * MEM_HD, (h + 1) * MEM_HD)
            qh, kh, vh, doh = q_ref[:, sl], k_ref[:, sl], v_ref[:, sl], do_ref[:, sl]
            s = _dot_nt(qh, kh) * scale
            p = jnp.exp(s - jnp.max(s, axis=1, keepdims=True))
            p = p / _rsum(p)
            dp = _dot_nt(doh, vh)
            ds = (p * (dp - _rsum(p * dp)) * scale).astype(BF16)
            dq_ref[:, sl] = _dot(ds, kh).astype(BF16)
            dk_ref[:, sl] += _dot_tn(ds, qh)
            dv_ref[:, sl] += _dot_tn(p.astype(BF16), doh)

    row = pl.BlockSpec((tm, D_MODEL), lambda i: (i, 0))
    kv = pl.BlockSpec(k.shape, lambda i: (0, 0))
    return pl.pallas_call(body, grid=(S // tm,), name="xattn_bwd", in_specs=[row, kv, kv, row], out_specs=[row, kv, kv],
                          out_shape=[SDS((S, D_MODEL), BF16), SDS(k.shape, F32), SDS(k.shape, F32)],
                          compiler_params=_cp("arbitrary"))(q, k, v, do)


FFN_TM, FFN_TC = 256, 1408


def _ffn_specs(S, order):
    tm, tc = FFN_TM, FFN_TC
    ij = (lambda a, b: (a, b)) if order == "ij" else (lambda a, b: (b, a))
    blk = pl.BlockSpec((2, tm, tc), lambda a, b: (0,) + ij(a, b))
    prev = pl.BlockSpec((2, 16, tc), lambda a, b: (0, jnp.maximum(ij(a, b)[0] * (tm // 16) - 1, 0), ij(a, b)[1]))
    nxt = pl.BlockSpec((2, 16, tc), lambda a, b: (0, jnp.minimum((ij(a, b)[0] + 1) * (tm // 16), S // 16 - 1), ij(a, b)[1]))
    wsp = pl.BlockSpec((2, 3, tc), lambda a, b: (0, 0, ij(a, b)[1]))
    bsp = pl.BlockSpec((2, 1, tc), lambda a, b: (0, 0, ij(a, b)[1]))
    act = pl.BlockSpec((tm, tc), lambda a, b: ij(a, b))
    return blk, prev, nxt, wsp, bsp, act


def _ffn_up(i, up_ref, uh_ref, w_ref, b_ref):
    keep = jnp.where(i > 0, 1.0, 0.0)
    out = []
    for half in range(2):
        u = up_ref[half].astype(F32)
        hu = uh_ref[half].astype(F32) * keep
        w0, w1, w2 = (w_ref[half, k:k + 1, :] for k in range(3))
        out.append(_conv3(u, hu, w0, w1, w2) + b_ref[half])
    return out


def ffn_act_fwd(up_pre, w_fc, b_fc):
    S = up_pre.shape[1]

    def body(up_ref, uh_ref, w_ref, b_ref, a_ref, gv_ref):
        g, v = _ffn_up(pl.program_id(0), up_ref, uh_ref, w_ref, b_ref)
        a_ref[...] = (g * jax.nn.sigmoid(g) * v).astype(BF16)
        gv_ref[0] = g.astype(BF16)
        gv_ref[1] = v.astype(BF16)

    blk, prev, _, wsp, bsp, act = _ffn_specs(S, "ij")
    return pl.pallas_call(body, grid=(S // FFN_TM, D_FF // FFN_TC), name="ffn_act_fwd",
                          in_specs=[blk, prev, wsp, bsp], out_specs=[act, blk],
                          out_shape=[SDS((S, D_FF), BF16), SDS((2, S, D_FF), BF16)],
                          compiler_params=_cp("parallel", "parallel"))(up_pre, up_pre, w_fc, b_fc)


def ffn_act_bwd_a(dact, up, job=None):
    S = up.shape[1]

    def body(da_ref, up_ref, du_ref, db_ref):
        i = pl.program_id(1)
        g, v = up_ref[0].astype(F32), up_ref[1].astype(F32)
        da = da_ref[...].astype(F32)
        sg = jax.nn.sigmoid(g)
        dg = da * v * (sg * (1.0 + g * (1.0 - sg)))
        dv = da * g * sg

        @pl.when(i == 0)
        def _():
            db_ref[...] = jnp.zeros_like(db_ref)

        du_ref[0] = dg.astype(BF16)
        du_ref[1] = dv.astype(BF16)
        db_ref[0] += _colsum(dg)
        db_ref[1] += _colsum(dv)

    blk, _, _, _, bsp, act = _ffn_specs(S, "ji")
    return _pcall(body, [dact, up], grid=(D_FF // FFN_TC, S // FFN_TM), name="ffn_act_bwd_a",
                  in_specs=[act, blk], out_specs=[blk, bsp],
                  out_shape=[SDS((2, S, D_FF), BF16), SDS((2, 1, D_FF), F32)], sem=("parallel", "arbitrary"), job=job)


def ffn_act_bwd_b(d_up, up_pre, w_fc):
    S = up_pre.shape[1]

    def body(du_ref, dn_ref, up_ref, w_ref, dp_ref, dw_ref):
        i = pl.program_id(1)
        keep_next = jnp.where(i < pl.num_programs(1) - 1, 1.0, 0.0)

        @pl.when(i == 0)
        def _():
            dw_ref[...] = jnp.zeros_like(dw_ref)

        for half in range(2):
            d = du_ref[half].astype(F32)
            dn = dn_ref[half].astype(F32) * keep_next
            u = up_ref[half].astype(F32)
            d1, d2 = _shift_up(d, 1, dn), _shift_up(d, 2, dn)
            w0, w1, w2 = (w_ref[half, k:k + 1, :] for k in range(3))
            dp_ref[half] = (d * w2 + d1 * w1 + d2 * w0).astype(BF16)
            dw_ref[half, 0:1, :] += _colsum(d2 * u)
            dw_ref[half, 1:2, :] += _colsum(d1 * u)
            dw_ref[half, 2:3, :] += _colsum(d * u)

    blk, _, nxt, wsp, _, _ = _ffn_specs(S, "ji")
    return pl.pallas_call(body, grid=(D_FF // FFN_TC, S // FFN_TM), name="ffn_act_bwd_b",
                          in_specs=[blk, nxt, blk, wsp], out_specs=[blk, wsp],
                          out_shape=[SDS((2, S, D_FF), BF16), SDS((2, 3, D_FF), F32)],
                          compiler_params=_cp("parallel", "arbitrary"))(d_up, d_up, up_pre, w_fc)


def final_loss(x3, g, target):
    S, Dm = x3.shape
    tm = 512

    def body(x_ref, g_ref, t_ref, loss_ref, dx_ref, dg_ref):
        i = pl.program_id(0)
        xv = x_ref[...]
        r = lax.rsqrt(jnp.mean(xv * xv, axis=1, keepdims=True) + EPS)
        xh = xv * r
        err = xh * g_ref[...] - t_ref[...]
        dy = err / Dm
        gd = dy * g_ref[...]
        dx_ref[...] = r * (gd - xh * jnp.mean(gd * xh, axis=1, keepdims=True))

        @pl.when(i == 0)
        def _():
            dg_ref[...] = jnp.zeros_like(dg_ref)
            loss_ref[...] = jnp.zeros_like(loss_ref)

        dg_ref[...] += _colsum(dy * xh)
        loss_ref[...] += 0.5 * _colsum(jnp.mean(err * err, axis=1, keepdims=True))

    row = pl.BlockSpec((tm, Dm), lambda i: (i, 0))
    vec = pl.BlockSpec((1, Dm), lambda i: (0, 0))
    one = pl.BlockSpec((1, 1), lambda i: (0, 0))
    return pl.pallas_call(body, grid=(S // tm,), name="final_loss", in_specs=[row, vec, row], out_specs=[one, row, vec],
                          out_shape=[SDS((1, 1), F32), SDS((S, Dm), F32), SDS((1, Dm), F32)],
                          compiler_params=_cp("arbitrary"))(x3, g, target)


def local_step(x, mem, target, wb, ws, comm=None):
    S = x.shape[0]
    assert S % 2048 == 0
    if comm is None:
        comm = NoComm()
    else:
        wb = comm.gather_first()
    w_fc = ws["w_ffn_conv"].reshape(3, 2, D_FF).transpose(1, 0, 2)
    b_fc = ws["b_ffn_conv"].reshape(2, 1, D_FF)

    bias = bias_tables(ws["rel_bias"])
    h1 = rmsnorm_fwd(x, ws["g_mix"], "norm_mix")
    proj = mm_nn(h1, wb["w_in"], "proj", out_bw=BLK, tn=768, job=comm.job("proj"))
    comm.landed("proj", wb)
    qkv = proj.reshape(2, 3, 4, S, BLK)
    o, lse = attn_fwd(qkv, bias, job=comm.job("attn_fwd"))
    comm.landed("attn_fwd", wb)
    proj4 = proj.reshape(2, 12, S, BLK)
    mixed = mixer_post_fwd(o, proj4, ws["w_short_conv"], ws["g_attn_out"], ws["g_conv_out"])
    x1 = mm_nn(mixed, wb["w_out"], "out_proj", res=x)
    h2 = rmsnorm_fwd(x1, ws["g_xattn"], "norm_xattn")
    mem_n = rmsnorm_fwd(mem, ws["g_mem"], "norm_mem")
    xq = mm_nn(h2, wb["w_xq"], "xq", out_dtype=BF16)
    xk = mm_nn(mem_n, wb["w_xk"], "xk", out_dtype=BF16, tn=1024)
    xv = mm_nn(mem_n, wb["w_xv"], "xv", out_dtype=BF16, tn=1024)
    xo = xattn_fwd(xq, xk, xv)
    x2 = mm_nn(xo, wb["w_xo"], "xo_proj", res=x1)
    h3 = rmsnorm_fwd(x2, ws["g_ffn"], "norm_ffn")
    up_pre = mm_nn(h3, wb["w_up"], "up_proj", out_dtype=BF16, out_bw=D_FF, tn=1408)
    act, up = ffn_act_fwd(up_pre, w_fc, b_fc)
    x3 = mm_nn(act, wb["w_down"], "down_proj", res=x2, tm=512)
    loss, dx3, dg_final = final_loss(x3, ws["g_final"].reshape(1, -1), target)

    gb, gs = {}, {"g_final": dg_final}
    gb["w_down"] = mm_tn(act, dx3, None, "dw_down", tm=1408, tn=1024)
    dact = mm_nt(dx3, None, wb["w_down"], "d_act", out_dtype=BF16, tn=1408)
    d_up, db_fc = ffn_act_bwd_a(dact, up)
    d_up_pre, dw_fc = ffn_act_bwd_b(d_up, up_pre, w_fc)
    gs["b_ffn_conv"] = db_fc.reshape(1, 2 * D_FF)
    gs["w_ffn_conv"] = dw_fc.transpose(1, 0, 2).reshape(3, 2 * D_FF)
    gb["w_up"] = mm_tn(h3, d_up_pre, D_FF, "dw_up", shards=4, tn=1408)
    dx2, gs["g_ffn"] = mm_nt_norm_bwd(d_up_pre, D_FF, wb["w_up"], x2, ws["g_ffn"], dx3, "d_h3", tm=256,
                                      job=comm.job("d_h3", gb))
    comm.landed("d_h3")
    gb["w_xo"] = mm_tn(xo, dx2, None, "dw_xo")
    dxo = mm_nt(dx2, None, wb["w_xo"], "d_xo", out_dtype=BF16)
    dxq, dxk, dxv = xattn_bwd(xq, xk, xv, dxo)
    gb["w_xq"] = mm_tn(h2, dxq, None, "dw_xq")
    gb["w_xk"] = mm_tn(mem_n, dxk, None, "dw_xk", tn=1024)
    gb["w_xv"] = mm_tn(mem_n, dxv, None, "dw_xv", tn=1024)
    dmem_n = mm_nt(dxk, None, wb["w_xk"], "d_memk", tn=1024)
    dmem_n = mm_nt(dxv, None, wb["w_xv"], "d_memv", res=dmem_n, tn=1024)
    gs["g_mem"] = rmsnorm_bwd(mem, ws["g_mem"], dmem_n, None, "norm_mem_bwd")
    dx1, gs["g_xattn"] = mm_nt_norm_bwd(dxq, None, wb["w_xq"], x1, ws["g_xattn"], dx2, "d_h2")
    gb["w_out"] = mm_tn(mixed, dx1, None, "dw_out")
    dmixed = mm_nt(dx1, None, wb["w_out"], "d_mixed")
    do, dgb, dcv, gs["g_attn_out"], gs["g_conv_out"] = mixer_post_bwd_a(
        dmixed, o, proj4, ws["w_short_conv"], ws["g_attn_out"], ws["g_conv_out"], job=comm.job("mixer_post_bwd_a", gb))
    comm.landed("mixer_post_bwd_a")
    dproj, ds_sum = attn_bwd(qkv, bias, o, lse, do, job=comm.job("attn_bwd"))
    comm.landed("attn_bwd")
    gs["rel_bias"] = bias_tables_bwd(ds_sum)
    dproj, gs["w_short_conv"] = mixer_post_bwd_b(dproj.reshape(2, 12, S, BLK), dgb, dcv, proj4, ws["w_short_conv"])
    dproj = dproj.reshape(24, S, BLK)
    gb["w_in"] = mm_tn(h1, dproj, BLK, "dw_in", shards=4, tn=768, job=comm.job("dw_in"))
    comm.landed("dw_in")
    grad_x, gs["g_mix"] = mm_nt_norm_bwd(dproj, BLK, wb["w_in"], x, ws["g_mix"], dx1, "d_h1", tm=256,
                                         job=comm.job("d_h1", gb))
    comm.landed("d_h1")
    return loss, grad_x, comm.finish(gb), gs


def adamw(w, g, m, v, name):
    R, C = w.shape
    tr = R
    for cand in (256, 352):
        if R > cand and R % cand == 0:
            tr = cand
            break

    def body(w_ref, g_ref, m_ref, v_ref, d_ref, nm_ref, nv_ref):
        gv = g_ref[...]
        mn = ADAM_B1 * m_ref[...] + (1.0 - ADAM_B1) * gv
        vn = ADAM_B2 * v_ref[...] + (1.0 - ADAM_B2) * (gv * gv)
        m_hat = mn / (1.0 - ADAM_B1 ** ADAM_STEP)
        v_hat = vn / (1.0 - ADAM_B2 ** ADAM_STEP)
        d_ref[...] = -ADAM_LR * (m_hat / (jnp.sqrt(v_hat) + ADAM_EPS) + ADAM_WD * w_ref[...])
        nm_ref[...] = mn
        nv_ref[...] = vn

    blk = pl.BlockSpec((tr, C), lambda i: (i, 0))
    return pl.pallas_call(body, grid=(R // tr,), name=name, in_specs=[blk] * 4, out_specs=[blk] * 3,
                          out_shape=[SDS((R, C), F32)] * 3, compiler_params=_cp("parallel"))(w, g, m, v)


BIG = ("w_in", "w_out", "w_xq", "w_xk", "w_xv", "w_xo", "w_up", "w_down")
COL_SHARDED = ("w_in", "w_up")
N_BIG = len(BIG)
ANY = pl.BlockSpec(memory_space=pl.ANY)


def _place():
    x, y, c = lax.axis_index("x"), lax.axis_index("y"), lax.axis_index("c")
    chips = [(1 - x, y), (x, 1 - y), (1 - x, 1 - y)]
    return x, y, c, chips


def _window(full, name, R, C, shard, half):
    r0, nr = (0, R) if half is None else (half * (R // 2), R // 2)
    if name in COL_SHARDED:
        return full.at[pl.ds(r0, nr), pl.ds(shard * C, C)]
    return full.at[pl.ds(shard * R + r0, nr), :]


def place_shard(w, mine_arr, n):
    R, C = w.shape
    col = n in COL_SHARDED

    def body(s_ref, w_ref, o_ref):
        o_ref[...] = w_ref[...].astype(BF16)

    grid_spec = pltpu.PrefetchScalarGridSpec(
        num_scalar_prefetch=1, grid=(1,), in_specs=[pl.BlockSpec((R, C), lambda i, s: (0, 0))],
        out_specs=pl.BlockSpec((R, C), (lambda i, s: (0, s[0])) if col else (lambda i, s: (s[0], 0))))
    return pl.pallas_call(body, grid_spec=grid_spec, name="place_" + n,
                          out_shape=SDS((R, 4 * C) if col else (4 * R, C), BF16),
                          compiler_params=_cp("arbitrary"))(mine_arr, w)


def _gather_jobs(names, shapes):
    nw = len(names)

    def start(full, sems):
        send, recv, fsend, frecv = sems
        x, y, c, chips = _place()
        mine = 2 * x + y
        for w, n in enumerate(names):
            R, C = shapes[w]
            own = _window(full[w], n, R, C, mine, c)
            for j, chip in enumerate(chips):
                pltpu.make_async_remote_copy(src_ref=own, dst_ref=own, send_sem=send.at[w, j], recv_sem=recv.at[w, j],
                                             device_id=(*chip, c), device_id_type=MESH).start()

    def mid(full, sems):
        send, recv, fsend, frecv = sems
        x, y, c, chips = _place()
        sib = (x, y, 1 - c)
        for w, n in enumerate(names):
            R, C = shapes[w]
            for j, chip in enumerate(chips):
                landed = _window(full[w], n, R, C, 2 * chip[0] + chip[1], c)
                pltpu.make_async_remote_copy(src_ref=landed, dst_ref=landed, send_sem=send.at[w, j], recv_sem=recv.at[w, j],
                                             device_id=(*chip, c), device_id_type=MESH).wait_recv()
                pltpu.make_async_remote_copy(src_ref=landed, dst_ref=landed, send_sem=fsend.at[w, j],
                                             recv_sem=frecv.at[w, j], device_id=sib, device_id_type=MESH).start()

    def finish(full, sems):
        send, recv, fsend, frecv = sems
        x, y, c, chips = _place()
        mine = 2 * x + y
        sib = (x, y, 1 - c)
        for w, n in enumerate(names):
            R, C = shapes[w]
            own = _window(full[w], n, R, C, mine, c)
            for j, chip in enumerate(chips):
                landed = _window(full[w], n, R, C, 2 * chip[0] + chip[1], c)
                other = _window(full[w], n, R, C, 2 * chip[0] + chip[1], 1 - c)
                pltpu.make_async_remote_copy(src_ref=other, dst_ref=other, send_sem=fsend.at[w, j], recv_sem=frecv.at[w, j],
                                             device_id=sib, device_id_type=MESH).wait_recv()
                pltpu.make_async_remote_copy(src_ref=own, dst_ref=own, send_sem=send.at[w, j], recv_sem=recv.at[w, j],
                                             device_id=(*chip, c), device_id_type=MESH).wait_send()
                pltpu.make_async_remote_copy(src_ref=landed, dst_ref=landed, send_sem=fsend.at[w, j],
                                             recv_sem=frecv.at[w, j], device_id=sib, device_id_type=MESH).wait_send()

    return start, mid, finish, [pltpu.SemaphoreType.DMA((nw, 3))] * 4


class CommJob:
    def __init__(self, ins, out_shapes, inplace, start, finish, sems, mid=None):
        self.ins, self.out_shapes, self.inplace = list(ins), list(out_shapes), inplace
        self.start, self.mid, self.finish, self.sems = start, mid, finish, list(sems)

    def run(self, phase, ins, outs, sems):
        if phase == 0:
            self.start(ins, outs, sems)
        elif phase == 1:
            if self.mid is not None:
                self.mid(ins, outs, sems)
        else:
            self.finish(ins, outs, sems)


def run_job(name, job):
    n_in, n_out = len(job.ins), len(job.out_shapes)

    def body(*refs):
        ins, outs, sm = refs[:n_in], refs[n_in:n_in + n_out], refs[n_in + n_out:]
        for phase in range(3):
            job.run(phase, ins, outs, sm)

    job.landed = list(pl.pallas_call(
        body, name=name, in_specs=[ANY] * n_in, out_specs=[ANY] * n_out, out_shape=job.out_shapes,
        input_output_aliases={i: i for i in range(n_in)} if job.inplace else {},
        scratch_shapes=job.sems)(*job.ins))
    return job.landed


class Carried:
    def __init__(self, job, n_in, n_out, n_scratch, steps):
        self.job, self.n_in, self.n_out, self.n_scratch, self.steps = job, n_in, n_out, n_scratch, steps
        self.nji = len(job.ins) if job else 0
        self.njo = len(job.out_shapes) if job else 0

    def in_specs(self):
        return [ANY] * self.nji

    def out_specs(self):
        return [ANY] * self.njo

    def out_shapes(self):
        return list(self.job.out_shapes) if self.job else []

    def aliases(self):
        if not (self.job and self.job.inplace):
            return {}
        return {self.n_in + i: self.n_out + i for i in range(self.nji)}

    def sems(self):
        return list(self.job.sems) if self.job else []

    def args(self):
        return list(self.job.ins) if self.job else []

    def split(self, refs):
        a = self.n_in
        b = a + self.nji
        c = b + self.n_out
        d = c + self.njo
        e = d + self.n_scratch
        return refs[:a] + refs[b:c] + refs[d:e], (refs[a:b], refs[c:d], refs[e:])

    def phase(self, phase, step, jrefs):
        if self.job is None:
            return
        at = {0: 0, 1: max(self.steps - 2, 0), 2: self.steps - 1}[phase]

        @pl.when(step == at)
        def _():
            self.job.run(phase, *jrefs)


def _pcall(body, args, *, grid, in_specs, out_specs, out_shape, name, sem, scratch=(), aliases=None, job=None):
    n_in, n_out = len(args), len(out_shape)
    if job is None:
        return pl.pallas_call(
            body, grid=grid, in_specs=list(in_specs), out_specs=list(out_specs), out_shape=list(out_shape), name=name,
            scratch_shapes=list(scratch), input_output_aliases=dict(aliases or {}), compiler_params=_cp(*sem))(*args)
    total = int(np.prod(grid))
    car = Carried(job, n_in, n_out, len(scratch), total)

    def wrapped(*refs):
        main, jrefs = car.split(refs)
        lin = pl.program_id(0)
        for ax in range(1, len(grid)):
            lin = lin * grid[ax] + pl.program_id(ax)
        car.phase(0, lin, jrefs)
        body(*main)
        car.phase(1, lin, jrefs)
        car.phase(2, lin, jrefs)

    res = pl.pallas_call(
        wrapped, grid=grid, in_specs=list(in_specs) + car.in_specs(), out_specs=list(out_specs) + car.out_specs(),
        out_shape=list(out_shape) + car.out_shapes(), name=name, scratch_shapes=list(scratch) + car.sems(),
        input_output_aliases={**dict(aliases or {}), **car.aliases()},
        compiler_params=_cp(*(["arbitrary"] * len(grid))))(*args, *car.args())
    job.landed = list(res[n_out:])
    return list(res[:n_out])


def gather_job(placed, names):
    shapes = []
    for n in names:
        R, C = placed[n].shape
        shapes.append((R, C // 4) if n in COL_SHARDED else (R // 4, C))
    start, mid, finish, sems = _gather_jobs(names, shapes)
    arrays = [placed[n] for n in names]
    return CommJob(arrays, [SDS(a.shape, a.dtype) for a in arrays], True,
                   lambda i, o, s: start(o, s), lambda i, o, s: finish(o, s), sems, mid=lambda i, o, s: mid(o, s))


def pair_exchange_job(grads, names):
    shapes = [grads[n].shape for n in names]

    def copies(ins, outs, sems):
        x, y, c, _ = _place()
        return [pltpu.make_async_remote_copy(
            src_ref=ins[w].at[:, pl.ds((1 - c) * (shapes[w][1] // 2), shapes[w][1] // 2), :], dst_ref=outs[w],
            send_sem=sems[0].at[w], recv_sem=sems[1].at[w], device_id=(x, y, 1 - c), device_id_type=MESH)
            for w in range(len(names))]

    def start(ins, outs, sems):
        for cp in copies(ins, outs, sems):
            cp.start()

    def finish(ins, outs, sems):
        for cp in copies(ins, outs, sems):
            cp.wait()

    return CommJob([grads[n] for n in names], [SDS((4, s[1] // 2, s[2]), F32) for s in shapes], False, start, finish,
                   [pltpu.SemaphoreType.DMA((len(names),))] * 2)


def chip_exchange_job(p16, names):
    shapes = [p16[n].shape for n in names]

    def copies(ins, outs, sems):
        x, y, c, chips = _place()
        return [pltpu.make_async_remote_copy(
            src_ref=ins[w].at[2 * chip[0] + chip[1]], dst_ref=outs[w].at[j],
            send_sem=sems[0].at[w, j], recv_sem=sems[1].at[w, j], device_id=(*chip, c), device_id_type=MESH)
            for w in range(len(names)) for j, chip in enumerate(chips)]

    def start(ins, outs, sems):
        for cp in copies(ins, outs, sems):
            cp.start()

    def finish(ins, outs, sems):
        for cp in copies(ins, outs, sems):
            cp.wait()

    return CommJob([p16[n] for n in names], [SDS((3,) + tuple(s[1:]), BF16) for s in shapes], False, start, finish,
                   [pltpu.SemaphoreType.DMA((len(names), 3))] * 2)


def half_exchange_job(full, names):
    shapes = [full[n].shape for n in names]

    def copies(outs, sems):
        x, y, c, _ = _place()
        cps = []
        for w in range(len(names)):
            Rh = shapes[w][0] // 2
            rows = outs[w].at[pl.ds(c * Rh, Rh), :]
            other = outs[w].at[pl.ds((1 - c) * Rh, Rh), :]
            cps.append((pltpu.make_async_remote_copy(src_ref=rows, dst_ref=rows, send_sem=sems[0].at[w], recv_sem=sems[1].at[w],
                                                     device_id=(x, y, 1 - c), device_id_type=MESH),
                        pltpu.make_async_remote_copy(src_ref=other, dst_ref=other, send_sem=sems[0].at[w], recv_sem=sems[1].at[w],
                                                     device_id=(x, y, 1 - c), device_id_type=MESH)))
        return cps

    def start(ins, outs, sems):
        for send, _ in copies(outs, sems):
            send.start()

    def finish(ins, outs, sems):
        for send, recv in copies(outs, sems):
            recv.wait_recv()
            send.wait_send()

    arrays = [full[n] for n in names]
    return CommJob(arrays, [SDS(a.shape, a.dtype) for a in arrays], True, start, finish,
                   [pltpu.SemaphoreType.DMA((len(names),))] * 2)


def allreduce_small(parts, name):
    n = len(parts)

    def body(*refs):
        ins, outs, bufs = refs[:n], refs[n:2 * n], refs[2 * n:3 * n]
        send, recv = refs[3 * n:]
        x, y, c, _ = _place()
        me = 4 * x + 2 * y + c
        sends = []
        for p in range(n):
            for k in range(1, 8):
                peer = (x ^ (k >> 2), y ^ ((k >> 1) & 1), c ^ (k & 1))
                cp = pltpu.make_async_remote_copy(src_ref=ins[p], dst_ref=bufs[p].at[k], send_sem=send.at[p, k - 1],
                                                  recv_sem=recv.at[p, k - 1], device_id=peer, device_id_type=MESH)
                cp.start()
                sends.append(cp)
            bufs[p][0] = ins[p][...]
        for cp in sends:
            cp.wait_recv()
        for p in range(n):
            acc = bufs[p][me]
            for d in range(1, 8):
                acc = acc + bufs[p][d ^ me]
            outs[p][...] = acc
        for cp in sends:
            cp.wait_send()

    vm = pl.BlockSpec(memory_space=pltpu.VMEM)
    return pl.pallas_call(
        body, name=name, in_specs=[vm] * n, out_specs=[vm] * n, out_shape=[SDS(a.shape, F32) for a in parts],
        scratch_shapes=[pltpu.VMEM((8,) + tuple(a.shape), F32) for a in parts] + [pltpu.SemaphoreType.DMA((n, 7))] * 2,
    )(*parts)


def adamw_multi(ws, gs, ms, vs, name):
    n = len(ws)

    def body(*refs):
        for p in range(n):
            w_ref, g_ref, m_ref, v_ref = (refs[q * n + p] for q in range(4))
            d_ref, nm_ref, nv_ref = (refs[(4 + q) * n + p] for q in range(3))
            gv = g_ref[...]
            mn = ADAM_B1 * m_ref[...] + (1.0 - ADAM_B1) * gv
            vn = ADAM_B2 * v_ref[...] + (1.0 - ADAM_B2) * (gv * gv)
            m_hat = mn / (1.0 - ADAM_B1 ** ADAM_STEP)
            v_hat = vn / (1.0 - ADAM_B2 ** ADAM_STEP)
            d_ref[...] = -ADAM_LR * (m_hat / (jnp.sqrt(v_hat) + ADAM_EPS) + ADAM_WD * w_ref[...])
            nm_ref[...] = mn
            nv_ref[...] = vn

    vm = pl.BlockSpec(memory_space=pltpu.VMEM)
    outs = pl.pallas_call(body, name=name, in_specs=[vm] * (4 * n), out_specs=[vm] * (3 * n),
                          out_shape=[SDS(a.shape, F32) for a in ws] * 3)(*ws, *gs, *ms, *vs)
    return outs[:n], outs[n:2 * n], outs[2 * n:]


def pair_sum(g, xh, sel_arr, name):
    _, R, C = g.shape
    Rh = R // 2

    def body(sel_ref, g_ref, x_ref, p32_ref, p16_ref):
        s = g_ref[...] + x_ref[...]
        p16_ref[...] = s.astype(BF16)

        @pl.when(pl.program_id(0) == sel_ref[0])
        def _():
            p32_ref[...] = s

    same = pl.BlockSpec((None, Rh, C), lambda s, sel: (s, 0, 0))
    grid_spec = pltpu.PrefetchScalarGridSpec(
        num_scalar_prefetch=1, grid=(4,),
        in_specs=[pl.BlockSpec((None, Rh, C), lambda s, sel: (s, sel[1], 0)), same],
        out_specs=[pl.BlockSpec((Rh, C), lambda s, sel: (0, 0)), same])
    return pl.pallas_call(body, grid_spec=grid_spec, name=name, out_shape=[SDS((Rh, C), F32), SDS((4, Rh, C), BF16)],
                          compiler_params=_cp("arbitrary"))(sel_arr, g, xh)


def chip_sum(p32, y, sel_arr, name):
    Rh, C = p32.shape

    def body(s_ref, p_ref, y_ref, o_ref):
        o_ref[...] = ((p_ref[...] + y_ref[0].astype(F32)) + y_ref[1].astype(F32)) + y_ref[2].astype(F32)

    grid_spec = pltpu.PrefetchScalarGridSpec(
        num_scalar_prefetch=1, grid=(1,),
        in_specs=[pl.BlockSpec((Rh, C), lambda i, s: (0, 0)), pl.BlockSpec((3, Rh, C), lambda i, s: (0, 0, 0))],
        out_specs=pl.BlockSpec((Rh, C), lambda i, s: (s[1], 0)))
    return pl.pallas_call(body, grid_spec=grid_spec, name=name, out_shape=SDS((2 * Rh, C), F32),
                          compiler_params=_cp("arbitrary"))(sel_arr, p32, y)


class StepComm:
    FIRST = ("w_in",)
    SQUARE = ("w_out", "w_xq", "w_xk", "w_xv", "w_xo")
    FFN = ("w_up", "w_down")
    PLAN = {
        "proj": ("gather", SQUARE), "attn_fwd": ("gather", FFN),
        "d_h3": ("pair", FFN), "mixer_post_bwd_a": ("pair", SQUARE), "attn_bwd": ("chip", FFN + SQUARE),
        "dw_in": ("half", FFN + SQUARE), "d_h1": ("pair", FIRST), "end_chip": ("chip", FIRST),
    }

    def __init__(self, placed, sel_arr):
        self.placed, self.sel_arr = placed, sel_arr
        self.g, self.p32, self.p16, self.full, self.done, self.jobs = {}, {}, {}, {}, {}, {}

    def gather_first(self):
        return dict(zip(self.FIRST, run_job("gather_first", gather_job(self.placed, self.FIRST))))

    def job(self, tag, gb=None):
        kind, names = self.PLAN[tag]
        if kind == "gather":
            j = gather_job(self.placed, names)
        elif kind == "pair":
            for n in names:
                self.g[n] = gb[n] if gb[n].ndim == 3 else gb[n].reshape(4, gb[n].shape[0] // 4, gb[n].shape[1])
            j = pair_exchange_job(self.g, names)
        elif kind == "chip":
            j = chip_exchange_job(self.p16, names)
        else:
            j = half_exchange_job(self.full, names)
        self.jobs[tag] = j
        return j

    def landed(self, tag, wb=None):
        kind, names = self.PLAN[tag]
        for n, a in zip(names, self.jobs[tag].landed):
            if kind == "gather":
                wb[n] = a
            elif kind == "pair":
                self.p32[n], self.p16[n] = pair_sum(self.g[n], a, self.sel_arr, "pair_sum_" + n)
            elif kind == "chip":
                self.full[n] = chip_sum(self.p32[n], a, self.sel_arr, "chip_sum_" + n)
            else:
                self.done[n] = a

    def finish(self, gb):
        run_job("chip_exchange_last", self.job("end_chip"))
        self.landed("end_chip")
        last = run_job("half_exchange", half_exchange_job(self.full, self.FIRST))
        self.done.update(zip(self.FIRST, last))
        return self.done


class NoComm:
    def job(self, tag, gb=None):
        return None

    def landed(self, tag, wb=None):
        pass

    def finish(self, gb):
        return gb


SMALL = ("rel_bias", "g_mix", "w_short_conv", "g_attn_out", "g_conv_out", "g_xattn", "g_mem", "g_ffn",
         "w_ffn_conv", "b_ffn_conv", "g_final")
SMALL_FULL = {"rel_bias": (8, 32), "g_mix": (1, 1024), "w_short_conv": (3, 512), "g_attn_out": (1, 512), "g_conv_out": (1, 512),
              "g_xattn": (1, 1024), "g_mem": (1, 1024), "g_ffn": (1, 1024), "w_ffn_conv": (3, 5632), "b_ffn_conv": (1, 5632),
              "g_final": (1024,)}
SMALL_SHARDED = ("w_short_conv", "w_ffn_conv")


def _pack(parts):
    flat = jnp.concatenate([p.reshape(-1) for p in parts])
    rows = -(-flat.shape[0] // 1024) * 8
    return jnp.pad(flat, (0, rows * 128 - flat.shape[0])).reshape(rows, 128)


def _unpack(packed, names):
    flat, out, off = packed.reshape(-1), {}, 0
    for n in names:
        size = int(np.prod(SMALL_FULL[n]))
        out[n] = flat[off:off + size].reshape(SMALL_FULL[n])
        off += size
    return out


def kernel(x, mem, rel_bias, g_mix, w_in, w_short_conv, g_attn_out, g_conv_out, w_out, g_xattn, g_mem, w_xq, w_xk, w_xv, w_xo, g_ffn, w_up, w_ffn_conv, b_ffn_conv, w_down, g_final, loss_target, m_rel_bias, m_g_mix, m_w_in, m_w_short_conv, m_g_attn_out, m_g_conv_out, m_w_out, m_g_xattn, m_g_mem, m_w_xq, m_w_xk, m_w_xv, m_w_xo, m_g_ffn, m_w_up, m_w_ffn_conv, m_b_ffn_conv, m_w_down, m_g_final, v_rel_bias, v_g_mix, v_w_in, v_w_short_conv, v_g_attn_out, v_g_conv_out, v_w_out, v_g_xattn, v_g_mem, v_w_xq, v_w_xk, v_w_xv, v_w_xo, v_g_ffn, v_w_up, v_w_ffn_conv, v_b_ffn_conv, v_w_down, v_g_final):
    names = ("rel_bias", "g_mix", "w_in", "w_short_conv", "g_attn_out", "g_conv_out", "w_out", "g_xattn", "g_mem", "w_xq",
             "w_xk", "w_xv", "w_xo", "g_ffn", "w_up", "w_ffn_conv", "b_ffn_conv", "w_down", "g_final")
    W = dict(zip(names, (rel_bias, g_mix, w_in, w_short_conv, g_attn_out, g_conv_out, w_out, g_xattn, g_mem, w_xq, w_xk, w_xv,
                         w_xo, g_ffn, w_up, w_ffn_conv, b_ffn_conv, w_down, g_final)))
    M = dict(zip(names, (m_rel_bias, m_g_mix, m_w_in, m_w_short_conv, m_g_attn_out, m_g_conv_out, m_w_out, m_g_xattn, m_g_mem,
                         m_w_xq, m_w_xk, m_w_xv, m_w_xo, m_g_ffn, m_w_up, m_w_ffn_conv, m_b_ffn_conv, m_w_down, m_g_final)))
    V = dict(zip(names, (v_rel_bias, v_g_mix, v_w_in, v_w_short_conv, v_g_attn_out, v_g_conv_out, v_w_out, v_g_xattn, v_g_mem,
                         v_w_xq, v_w_xk, v_w_xv, v_w_xo, v_g_ffn, v_w_up, v_w_ffn_conv, v_b_ffn_conv, v_w_down, v_g_final)))
    xi, yi, ci = lax.axis_index("x"), lax.axis_index("y"), lax.axis_index("c")
    mine = 2 * xi + yi
    mine_arr = jnp.reshape(mine, (1,)).astype(jnp.int32)
    sel_arr = jnp.stack([mine, ci]).astype(jnp.int32)
    comm = StepComm({n: place_shard(W[n][0], mine_arr, n) for n in BIG}, sel_arr)
    first = jnp.where(ci == 0, 1.0, 0.0).astype(F32)
    emb = []
    for n in SMALL_SHARDED:
        shard = W[n][0]
        full = jnp.zeros(SMALL_FULL[n], F32)
        emb.append(lax.dynamic_update_slice(full, shard * first, (0, mine * shard.shape[1])))
    ws = {n: W[n] for n in SMALL if n not in SMALL_SHARDED}
    ws.update(zip(SMALL_SHARDED, allreduce_small(emb, "gather_small")))

    loss, grad_x, gfull, gs = local_step(x[0], mem[0], loss_target[0], None, ws, comm)

    def as2d(a):
        return a.reshape(1, -1) if a.ndim == 1 else a

    reduced = allreduce_small([as2d(gs[n]) for n in SMALL] + [loss], "reduce_small")
    gsm = dict(zip(SMALL, reduced[:-1]))
    loss = reduced[-1][0, 0]

    grads, delta, new_m, new_v = {}, {}, {}, {}
    for n in BIG:
        grads[n] = gfull[n][None]
        d, nm, nv = adamw(W[n][0], gfull[n], M[n][0], V[n][0], "adamw_" + n)
        delta[n], new_m[n], new_v[n] = d[None], nm[None], nv[None]
    for n in SMALL_SHARDED:
        wid = W[n].shape[2]
        gsm[n] = lax.dynamic_slice(gsm[n], (0, mine * wid), (3, wid))

    def own(a, n):
        return a[0] if n in SMALL_SHARDED else as2d(a)

    d, nm, nv = adamw_multi([own(W[n], n) for n in SMALL], [gsm[n] for n in SMALL], [own(M[n], n) for n in SMALL],
                            [own(V[n], n) for n in SMALL], "adamw_small")
    for i, n in enumerate(SMALL):
        shape = W[n].shape
        grads[n], delta[n], new_m[n], new_v[n] = (a.reshape(shape) for a in (gsm[n], d[i], nm[i], nv[i]))
    return (loss, grad_x[None], *[grads[n] for n in names], *[delta[n] for n in names],
            *[new_m[n] for n in names], *[new_v[n] for n in names])
```

```python
import functools
import math

import numpy as np
import jax
import jax.numpy as jnp
from jax import lax
from jax.experimental import pallas as pl
from jax.experimental.pallas import tpu as pltpu

F32 = jnp.float32
BF16 = jnp.bfloat16
SDS = jax.ShapeDtypeStruct
MESH = pl.DeviceIdType.MESH

D_MODEL = 1024
ATTN_W = 512
N_HEADS = 8
BLK = 128
PATTERNS = ((128, 1), (512, 4), (2048, 16))
N_BUCKETS = 32
D_FF = 2816
N_MEM_HEADS = 4
MEM_HD = 256
EPS = 1e-6
NEG = -1e30
VMEM_LIMIT = 56 * 1024 * 1024

ADAM_LR, ADAM_B1, ADAM_B2, ADAM_EPS, ADAM_WD, ADAM_STEP = 0.001, 0.9, 0.999, 1e-08, 0.01, 10


def _cp(*sem):
    return pltpu.CompilerParams(dimension_semantics=sem, vmem_limit_bytes=VMEM_LIMIT)


def _dot(a, b):
    return jnp.dot(a, b, preferred_element_type=F32)


def _dot_nt(a, b):
    return lax.dot_general(a, b, (((1,), (1,)), ((), ())), preferred_element_type=F32)


def _dot_tn(a, b):
    return lax.dot_general(a, b, (((0,), (0,)), ((), ())), preferred_element_type=F32)


def _rsum(x):
    return jnp.sum(x, axis=1, keepdims=True)


def rmsnorm_fwd(x, g, name):
    S, Dm = x.shape
    tm = min(S, 512)

    def body(x_ref, g_ref, o_ref):
        xv = x_ref[...]
        r = lax.rsqrt(jnp.mean(xv * xv, axis=1, keepdims=True) + EPS)
        o_ref[...] = (xv * r * g_ref[...]).astype(o_ref.dtype)

    return pl.pallas_call(
        body, grid=(S // tm,), name=name,
        in_specs=[pl.BlockSpec((tm, Dm), lambda i: (i, 0)), pl.BlockSpec((1, Dm), lambda i: (0, 0))],
        out_specs=pl.BlockSpec((tm, Dm), lambda i: (i, 0)),
        out_shape=SDS((S, Dm), BF16), compiler_params=_cp("parallel"))(x, g)


def rmsnorm_bwd(x, g, dh, res, name, job=None):
    S, Dm = x.shape
    tm = min(S, 512)
    want_dx = res is not None

    def body(*refs):
        if want_dx:
            x_ref, g_ref, dh_ref, res_ref, dx_ref, dg_ref = refs
        else:
            x_ref, g_ref, dh_ref, dg_ref = refs
        i = pl.program_id(0)
        xv = x_ref[...]
        r = lax.rsqrt(jnp.mean(xv * xv, axis=1, keepdims=True) + EPS)
        xh = xv * r
        dh = dh_ref[...].astype(F32)
        if want_dx:
            gd = dh * g_ref[...]
            dx_ref[...] = res_ref[...] + r * (gd - xh * jnp.mean(gd * xh, axis=1, keepdims=True))

        @pl.when(i == 0)
        def _():
            dg_ref[...] = jnp.zeros_like(dg_ref)

        dg_ref[...] += jnp.sum(dh * xh, axis=0, keepdims=True)

    row = pl.BlockSpec((tm, Dm), lambda i: (i, 0))
    vec = pl.BlockSpec((1, Dm), lambda i: (0, 0))
    if want_dx:
        return _pcall(body, [x, g, dh, res], grid=(S // tm,), name=name, in_specs=[row, vec, row, row], out_specs=[row, vec],
                      out_shape=[SDS((S, Dm), F32), SDS((1, Dm), F32)], sem=("arbitrary",), job=job)
    return pl.pallas_call(
        body, grid=(S // tm,), name=name, in_specs=[row, vec, row], out_specs=vec,
        out_shape=SDS((1, Dm), F32), compiler_params=_cp("arbitrary"))(x, g, dh)


def _col_spec(bw, tn, rows, row_of, col_of):
    if bw is None:
        return pl.BlockSpec((rows, tn), lambda *g: (row_of(*g), col_of(*g)))
    if tn % bw == 0:
        return pl.BlockSpec((tn // bw, rows, bw), lambda *g: (col_of(*g), row_of(*g), 0))
    per = bw // tn
    return pl.BlockSpec((None, rows, tn), lambda *g: (col_of(*g) // per, row_of(*g), col_of(*g) % per))


def _read_cols(ref, bw, tn):
    if bw is None or tn % bw != 0:
        return ref[...]
    if tn == bw:
        return ref[0]
    return jnp.concatenate([ref[q] for q in range(tn // bw)], axis=1)


def _write_cols(ref, bw, tn, val):
    if bw is None or tn % bw != 0:
        ref[...] = val.astype(ref.dtype)
    else:
        for q in range(tn // bw):
            ref[q] = val[:, q * bw:(q + 1) * bw].astype(ref.dtype)


def mm_nn(a, b, name, *, res=None, out_dtype=F32, out_bw=None, tm=1024, tn=512, job=None):
    M, K = a.shape
    N = b.shape[1]
    tm = min(tm, M)

    def body(*refs):
        if res is None:
            a_ref, b_ref, o_ref = refs
        else:
            a_ref, b_ref, r_ref, o_ref = refs
        acc = _dot(a_ref[...].astype(BF16), b_ref[...])
        if res is not None:
            acc = acc + r_ref[...]
        _write_cols(o_ref, out_bw, tn, acc)

    ri, ci = (lambda i, j: i), (lambda i, j: j)
    in_specs = [pl.BlockSpec((tm, K), lambda i, j: (i, 0)), pl.BlockSpec((K, tn), lambda i, j: (0, j))]
    args = [a, b]
    if res is not None:
        in_specs.append(pl.BlockSpec((tm, tn), lambda i, j: (i, j)))
        args.append(res)
    oshape = (M, N) if out_bw is None else (N // out_bw, M, out_bw)
    return _pcall(body, args, grid=(M // tm, N // tn), name=name, in_specs=in_specs,
                  out_specs=[_col_spec(out_bw, tn, tm, ri, ci)], out_shape=[SDS(oshape, out_dtype)],
                  sem=("parallel", "parallel"), job=job)[0]


def mm_nt(a, a_bw, b, name, *, res=None, out_dtype=F32, tm=1024, tn=512, job=None):
    if a_bw is None:
        M, K = a.shape
    else:
        M, K = a.shape[1], a.shape[0] * a_bw
    N = b.shape[0]
    tm = min(tm, M)

    def body(*refs):
        if res is None:
            a_ref, b_ref, o_ref = refs
        else:
            a_ref, b_ref, r_ref, o_ref = refs
        av = _read_cols(a_ref, a_bw, K).astype(BF16)
        acc = _dot_nt(av, b_ref[...])
        if res is not None:
            acc = acc + r_ref[...]
        o_ref[...] = acc.astype(o_ref.dtype)

    in_specs = [_col_spec(a_bw, K, tm, lambda i, j: i, lambda i, j: 0), pl.BlockSpec((tn, K), lambda i, j: (j, 0))]
    args = [a, b]
    if res is not None:
        in_specs.append(pl.BlockSpec((tm, tn), lambda i, j: (i, j)))
        args.append(res)
    return _pcall(body, args, grid=(M // tm, N // tn), name=name, in_specs=in_specs,
                  out_specs=[pl.BlockSpec((tm, tn), lambda i, j: (i, j))], out_shape=[SDS((M, N), out_dtype)],
                  sem=("parallel", "parallel"), job=job)[0]


def mm_nt_norm_bwd(a, a_bw, b, x, g, res, name, *, tm=512, job=None):
    if a_bw is None:
        M, K = a.shape
    else:
        M, K = a.shape[1], a.shape[0] * a_bw
    N = b.shape[0]

    def body(a_ref, b_ref, x_ref, g_ref, r_ref, dx_ref, dg_ref):
        dh = _dot_nt(_read_cols(a_ref, a_bw, K).astype(BF16), b_ref[...])
        xv = x_ref[...]
        r = lax.rsqrt(jnp.mean(xv * xv, axis=1, keepdims=True) + EPS)
        xh = xv * r
        gd = dh * g_ref[...]
        dx_ref[...] = r_ref[...] + r * (gd - xh * jnp.mean(gd * xh, axis=1, keepdims=True))

        @pl.when(pl.program_id(0) == 0)
        def _():
            dg_ref[...] = jnp.zeros_like(dg_ref)

        dg_ref[...] += jnp.sum(dh * xh, axis=0, keepdims=True)

    row = pl.BlockSpec((tm, N), lambda i: (i, 0))
    vec = pl.BlockSpec((1, N), lambda i: (0, 0))
    return _pcall(body, [a, b, x, g, res], grid=(M // tm,), name=name,
                  in_specs=[_col_spec(a_bw, K, tm, lambda i: i, lambda i: 0), pl.BlockSpec((N, K), lambda i: (0, 0)), row, vec, row],
                  out_specs=[row, vec], out_shape=[SDS((M, N), F32), SDS((1, N), F32)], sem=("arbitrary",), job=job)


def mm_tn(a, b, b_bw, name, *, shards=None, tm=1024, tn=1024, ts=1024, job=None):
    S, Ka = a.shape
    N = b.shape[1] if b_bw is None else b.shape[0] * b_bw
    ts = min(ts, S)
    tm = min(tm, Ka)

    def body(a_ref, b_ref, o_ref):
        @pl.when(pl.program_id(2) == 0)
        def _():
            o_ref[...] = jnp.zeros_like(o_ref)

        bv = _read_cols(b_ref, b_bw, tn).astype(BF16)
        o_ref[...] += _dot_tn(a_ref[...].astype(BF16), bv)

    in_specs = [pl.BlockSpec((ts, tm), lambda i, j, k: (k, i)),
                _col_spec(b_bw, tn, ts, lambda i, j, k: k, lambda i, j, k: j)]
    if shards is None:
        out_spec, oshape = pl.BlockSpec((tm, tn), lambda i, j, k: (i, j)), (Ka, N)
    else:
        per = (N // shards) // tn
        out_spec = pl.BlockSpec((None, tm, tn), lambda i, j, k: (j // per, i, j % per))
        oshape = (shards, Ka, N // shards)
    return _pcall(body, [a, b], grid=(Ka // tm, N // tn, S // ts), name=name, in_specs=in_specs, out_specs=[out_spec],
                  out_shape=[SDS(oshape, F32)], sem=("parallel", "parallel", "arbitrary"), job=job)[0]


def _bucket_tables():
    out = np.zeros((3, 2, BLK, BLK), np.int32)
    qi = np.arange(BLK)[:, None]
    for p, (win, dil) in enumerate(PATTERNS):
        w = win // dil
        assert w == BLK
        for half in range(2):
            kj = np.arange(BLK)[None, :] + half * BLK
            steps = qi + w - kj
            valid = (steps >= 0) & (steps <= w)
            dist = np.clip(steps, 0, w) * dil
            dd = np.maximum(dist, 1).astype(np.float32)
            large = 16 + (np.log(dd / np.float32(16)) / np.float32(math.log(2048 / 16)) * np.float32(16)).astype(np.int32)
            large = np.minimum(large, N_BUCKETS - 1)
            out[p, half] = np.where(valid, np.where(dist < 16, dist, large), -1)
    return out


def bias_tables(rel_bias):
    bkt = jnp.asarray(_bucket_tables())

    def body(rb_ref, bkt_ref, o_ref):
        h = pl.program_id(1)
        for half in range(2):
            bk = bkt_ref[half]
            acc = jnp.full((BLK, BLK), NEG, F32)
            for b in range(N_BUCKETS):
                acc = jnp.where(bk == b, rb_ref[h, b], acc)
            o_ref[half] = acc

    return pl.pallas_call(
        body, grid=(3, N_HEADS), name="bias_tables",
        in_specs=[pl.BlockSpec(memory_space=pltpu.SMEM),
                  pl.BlockSpec((None, 2, BLK, BLK), lambda p, h: (p, 0, 0, 0))],
        out_specs=pl.BlockSpec((None, None, 2, BLK, BLK), lambda p, h: (p, h, 0, 0, 0)),
        out_shape=SDS((3, N_HEADS, 2, BLK, BLK), F32), compiler_params=_cp("parallel", "parallel"))(rel_bias, bkt)


def bias_tables_bwd(ds_sum):
    bkt = jnp.asarray(_bucket_tables())

    def body(ds_ref, bkt_ref, o_ref):
        h = pl.program_id(0)

        @pl.when(h == 0)
        def _():
            o_ref[...] = jnp.zeros_like(o_ref)

        rows = lax.broadcasted_iota(jnp.int32, (N_HEADS, N_BUCKETS), 0)
        cols = lax.broadcasted_iota(jnp.int32, (N_HEADS, N_BUCKETS), 1)
        acc = jnp.zeros((N_HEADS, N_BUCKETS), F32)
        for b in range(N_BUCKETS):
            tot = jnp.zeros((1, 1), F32)
            for p in range(3):
                for half in range(2):
                    t = jnp.where(bkt_ref[p, half] == b, ds_ref[p, half], 0.0)
                    tot = tot + jnp.sum(_rsum(t), axis=0, keepdims=True)
            acc = acc + jnp.where((rows == h) & (cols == b), tot, 0.0)
        o_ref[...] += acc

    return pl.pallas_call(
        body, grid=(N_HEADS,), name="bias_tables_bwd",
        in_specs=[pl.BlockSpec((3, None, 2, BLK, BLK), lambda h: (0, h, 0, 0, 0)),
                  pl.BlockSpec((3, 2, BLK, BLK), lambda h: (0, 0, 0, 0))],
        out_specs=pl.BlockSpec((N_HEADS, N_BUCKETS), lambda h: (0, 0)),
        out_shape=SDS((N_HEADS, N_BUCKETS), F32), compiler_params=_cp("arbitrary"))(ds_sum, bkt)


def _rows(start, dil):
    if dil == 1:
        return pl.ds(pl.multiple_of(start, BLK), BLK)
    return pl.ds(start, BLK, stride=dil)


GRP = 8


def _group_rows(i, dil):
    t = lax.broadcasted_iota(jnp.int32, (GRP, 1, 1), 0)
    n0 = i * GRP
    b0, r0 = n0 // dil, n0 % dil
    cur, prv = [], []
    for j in range(GRP):
        b, r = (b0 + j // dil, j % dil) if GRP > dil else (b0, r0 + j)
        cur.append(b * (BLK * dil) + r)
        prv.append(jnp.maximum(b - 1, 0) * (BLK * dil) + r)
    pen = jnp.where(b0 + (r0 + t) // dil > 0, 0.0, NEG)
    return [_rows(s, dil) for s in cur], [_rows(s, dil) for s in prv], pen


def _load_group(ref, rows):
    return jnp.stack([ref[r, :] for r in rows])


def _bdot_nt(a, b):
    return lax.dot_general(a, b, (((2,), (2,)), ((0,), (0,))), preferred_element_type=F32)


def _bdot(a, b):
    return lax.dot_general(a, b, (((2,), (1,)), ((0,), (0,))), preferred_element_type=F32)


def _bdot_tn(a, b):
    return lax.dot_general(a, b, (((1,), (1,)), ((0,), (0,))), preferred_element_type=F32)


def _lsum(x):
    return jnp.sum(x, axis=-1, keepdims=True)


def _widen(src, dst):
    S = src.shape[1]

    def chunk(i, carry):
        rows = pl.ds(pl.multiple_of(i * 512, 512), 512)
        for a in range(3):
            dst[a, rows, :] = src[a, rows, :].astype(F32)
        return carry

    lax.fori_loop(0, S // 512, chunk, 0)


def attn_fwd(qkv, bias, job=None):
    S = qkv.shape[3]
    nblk = S // BLK
    car = Carried(job, 2, 2, 2, 4)

    def body(*refs):
        (qkv_in, bias_ref, o_ref, lse_ref, l_ref, qkv_ref), jrefs = car.split(refs)
        car.phase(0, pl.program_id(0), jrefs)
        _widen(qkv_in, qkv_ref)
        lane = lax.broadcasted_iota(jnp.int32, (GRP, BLK, BLK), 2)
        lo = lane < 64
        masks = (lo, jnp.logical_not(lo))
        q_ref, k_ref, v_ref = qkv_ref.at[0], qkv_ref.at[1], qkv_ref.at[2]
        for p, (_, dil) in enumerate(PATTERNS):
            def step(i, carry, p=p, dil=dil):
                rc, rp, pen = _group_rows(i, dil)
                q2 = _load_group(q_ref, rc) * 0.125
                kc = _load_group(k_ref, rc).astype(BF16)
                kp = _load_group(k_ref, rp).astype(BF16)
                vc = _load_group(v_ref, rc)
                vp = _load_group(v_ref, rp)
                if p > 0:
                    acc_old, m_old, l_old = _load_group(o_ref, rc), _load_group(lse_ref, rc), _load_group(l_ref, rc)
                pv = jnp.zeros((GRP, BLK, BLK), F32)
                m_new, l_new, alpha = [], [], []
                for h in range(2):
                    qh = jnp.where(masks[h], q2, 0.0).astype(BF16)
                    sp = _bdot_nt(qh, kp) + bias_ref[p, h, 0][None] + pen
                    sc = _bdot_nt(qh, kc) + bias_ref[p, h, 1][None]
                    mn = jnp.max(jnp.maximum(sp, sc), axis=-1, keepdims=True)
                    if p > 0:
                        mo = m_old[:, :, 64 * h:64 * h + 1]
                        mn = jnp.maximum(mn, mo)
                        al = jnp.exp(mo - mn)
                        alpha.append(al)
                    pp = jnp.exp(sp - mn)
                    pc = jnp.exp(sc - mn)
                    ln = _lsum(pp + pc)
                    if p > 0:
                        ln = ln + al * l_old[:, :, 64 * h:64 * h + 1]
                    vhp = jnp.where(masks[h], vp, 0.0).astype(BF16)
                    vhc = jnp.where(masks[h], vc, 0.0).astype(BF16)
                    pv = pv + _bdot(pp.astype(BF16), vhp) + _bdot(pc.astype(BF16), vhc)
                    m_new.append(mn)
                    l_new.append(ln)
                if p > 0:
                    pv = pv + acc_old * jnp.where(lo, alpha[0], alpha[1])
                m_t = jnp.where(lo, m_new[0], m_new[1])
                l_t = jnp.where(lo, l_new[0], l_new[1])
                for j, r in enumerate(rc):
                    o_ref[r, :] = pv[j]
                    lse_ref[r, :] = m_t[j]
                    l_ref[r, :] = l_t[j]
                return carry

            lax.fori_loop(0, nblk // GRP, step, 0)

        def fin(i, carry):
            rows = pl.ds(pl.multiple_of(i * 512, 512), 512)
            lv = l_ref[rows, :]
            o_ref[rows, :] = o_ref[rows, :] / lv
            lse_ref[rows, :] = lse_ref[rows, :] + jnp.log(lv)
            return carry

        lax.fori_loop(0, S // 512, fin, 0)
        car.phase(1, pl.program_id(0), jrefs)
        car.phase(2, pl.program_id(0), jrefs)

    outs = pl.pallas_call(
        body, grid=(4,), name="attn_fwd",
        in_specs=[pl.BlockSpec((None, 3, None, S, BLK), lambda g: (0, 0, g, 0, 0)),
                  pl.BlockSpec((3, 2, 2, BLK, BLK), lambda g: (0, g, 0, 0, 0))] + car.in_specs(),
        out_specs=[pl.BlockSpec((None, S, BLK), lambda g: (g, 0, 0)),
                   pl.BlockSpec((None, S, BLK), lambda g: (g, 0, 0))] + car.out_specs(),
        out_shape=[SDS((4, S, BLK), F32), SDS((4, S, BLK), F32)] + car.out_shapes(),
        input_output_aliases=car.aliases(),
        scratch_shapes=[pltpu.VMEM((S, BLK), F32), pltpu.VMEM((3, S, BLK), F32)] + car.sems(),
        compiler_params=_cp("arbitrary"))(qkv, bias, *car.args())
    if job is not None:
        job.landed = list(outs[2:])
    return outs[0], outs[1]


def attn_bwd(qkv, bias, o, lse, do, job=None):
    S = qkv.shape[3]
    nblk = S // BLK
    car = Carried(job, 5, 2, 2, 4)

    def body(*refs):
        (qkv_in, bias_ref, o_ref, lse_ref, do_ref, dqkv_out, ds_ref, qkv_ref, dqkv_ref), jrefs = car.split(refs)
        car.phase(0, pl.program_id(0), jrefs)
        _widen(qkv_in, qkv_ref)
        lane = lax.broadcasted_iota(jnp.int32, (GRP, BLK, BLK), 2)
        lo = lane < 64
        masks = (lo, jnp.logical_not(lo))
        dqkv_ref[...] = jnp.zeros_like(dqkv_ref)
        ds_ref[...] = jnp.zeros_like(ds_ref)
        q_ref, k_ref, v_ref = qkv_ref.at[0], qkv_ref.at[1], qkv_ref.at[2]
        dq_ref, dk_ref, dv_ref = dqkv_ref.at[0], dqkv_ref.at[1], dqkv_ref.at[2]
        for p, (_, dil) in enumerate(PATTERNS):
            def step(i, carry, p=p, dil=dil):
                rc, rp, pen = _group_rows(i, dil)
                q2 = _load_group(q_ref, rc) * 0.125
                kc = _load_group(k_ref, rc).astype(BF16)
                kp = _load_group(k_ref, rp).astype(BF16)
                vc = _load_group(v_ref, rc).astype(BF16)
                vp = _load_group(v_ref, rp).astype(BF16)
                dot = _load_group(do_ref, rc)
                lset = _load_group(lse_ref, rc)
                prod = dot * _load_group(o_ref, rc)
                zero = jnp.zeros((GRP, BLK, BLK), F32)
                dq, dkc, dkp, dvc, dvp = zero, zero, zero, zero, zero
                for h in range(2):
                    qh = jnp.where(masks[h], q2, 0.0).astype(BF16)
                    doh = jnp.where(masks[h], dot, 0.0).astype(BF16)
                    delta = _lsum(jnp.where(masks[h], prod, 0.0))
                    lh = lset[:, :, 64 * h:64 * h + 1]
                    pp = jnp.exp(_bdot_nt(qh, kp) + bias_ref[p, h, 0][None] + pen - lh)
                    pc = jnp.exp(_bdot_nt(qh, kc) + bias_ref[p, h, 1][None] - lh)
                    dsp = pp * (_bdot_nt(doh, vp) - delta)
                    dsc = pc * (_bdot_nt(doh, vc) - delta)
                    ds_ref[p, h, 0] += jnp.sum(dsp, axis=0)
                    ds_ref[p, h, 1] += jnp.sum(dsc, axis=0)
                    dspb, dscb = dsp.astype(BF16), dsc.astype(BF16)
                    dq = dq + jnp.where(masks[h], _bdot(dspb, kp) + _bdot(dscb, kc), 0.0)
                    dkp = dkp + _bdot_tn(dspb, qh)
                    dkc = dkc + _bdot_tn(dscb, qh)
                    dvp = dvp + _bdot_tn(pp.astype(BF16), doh)
                    dvc = dvc + _bdot_tn(pc.astype(BF16), doh)
                for j in range(GRP):
                    dq_ref[rc[j], :] += dq[j] * 0.125
                    dk_ref[rp[j], :] += dkp[j]
                    dk_ref[rc[j], :] += dkc[j]
                    dv_ref[rp[j], :] += dvp[j]
                    dv_ref[rc[j], :] += dvc[j]
                return carry

            lax.fori_loop(0, nblk // GRP, step, 0)

        def narrow(i, carry):
            rows = pl.ds(pl.multiple_of(i * 512, 512), 512)
            for a in range(3):
                dqkv_out[a, rows, :] = dqkv_ref[a, rows, :].astype(BF16)
            return carry

        lax.fori_loop(0, S // 512, narrow, 0)
        car.phase(1, pl.program_id(0), jrefs)
        car.phase(2, pl.program_id(0), jrefs)

    blk = pl.BlockSpec((None, S, BLK), lambda g: (g, 0, 0))
    outs = pl.pallas_call(
        body, grid=(4,), name="attn_bwd",
        in_specs=[pl.BlockSpec((None, 3, None, S, BLK), lambda g: (0, 0, g, 0, 0)),
                  pl.BlockSpec((3, 2, 2, BLK, BLK), lambda g: (0, g, 0, 0, 0)), blk, blk, blk] + car.in_specs(),
        out_specs=[pl.BlockSpec((None, 3, None, S, BLK), lambda g: (0, 0, g, 0, 0)),
                   pl.BlockSpec((3, 2, 2, BLK, BLK), lambda g: (0, g, 0, 0, 0))] + car.out_specs(),
        out_shape=[SDS((2, 3, 4, S, BLK), BF16), SDS((3, N_HEADS, 2, BLK, BLK), F32)] + car.out_shapes(),
        input_output_aliases=car.aliases(),
        scratch_shapes=[pltpu.VMEM((3, S, BLK), F32), pltpu.VMEM((3, S, BLK), F32)] + car.sems(),
        compiler_params=_cp("arbitrary"))(qkv, bias, o, lse, do, *car.args())
    if job is not None:
        job.landed = list(outs[2:])
    return outs[0], outs[1]


def _shift_down(u, k, halo):
    n = u.shape[0]
    row = lax.broadcasted_iota(jnp.int32, u.shape, 0)
    out = pltpu.roll(u, k, 0)
    hn = halo.shape[0]
    for j in range(k):
        out = jnp.where(row == j, halo[hn - k + j:hn - k + j + 1, :], out)
    return out


def _shift_up(u, k, halo):
    n = u.shape[0]
    row = lax.broadcasted_iota(jnp.int32, u.shape, 0)
    out = pltpu.roll(u, n - k, 0)
    for j in range(k):
        out = jnp.where(row == n - k + j, halo[j:j + 1, :], out)
    return out


def _conv3(u, halo, w0, w1, w2):
    return _shift_down(u, 2, halo) * w0 + _shift_down(u, 1, halo) * w1 + u * w2


def _colsum(x):
    return jnp.sum(x, axis=0, keepdims=True)


def _mixer_specs(S, tm):
    conv = pl.BlockSpec((None, 12, tm, BLK), lambda i: (1, 0, i, 0))
    halo = pl.BlockSpec((None, 12, 16, BLK), lambda i: (1, 0, jnp.maximum(i * (tm // 16) - 1, 0), 0))
    ob = pl.BlockSpec((4, tm, BLK), lambda i: (0, i, 0))
    return conv, halo, ob


def _mixer_recompute(i, o_ref, pr_ref, ph_ref, w_ref):
    ob = [o_ref[q] for q in range(4)]
    gb = [pr_ref[q].astype(F32) for q in range(4)]
    gc = [pr_ref[4 + q].astype(F32) for q in range(4)]
    xi = [pr_ref[8 + q].astype(F32) for q in range(4)]
    keep = jnp.where(i > 0, 1.0, 0.0)
    u = [gc[q] * xi[q] for q in range(4)]
    hu = [ph_ref[4 + q].astype(F32) * ph_ref[8 + q].astype(F32) * keep for q in range(4)]
    w = [[w_ref[k:k + 1, q * BLK:(q + 1) * BLK] for k in range(3)] for q in range(4)]
    cv = [_conv3(u[q], hu[q], *w[q]) for q in range(4)]
    return ob, gb, gc, xi, u, hu, cv, w


def _rms_blocks(blocks):
    ss = sum(_rsum(b * b) for b in blocks)
    return lax.rsqrt(ss / (BLK * len(blocks)) + EPS)


def mixer_post_fwd(o, proj, w_sc, g_a, g_c):
    S = o.shape[1]
    tm = 512

    def body(o_ref, pr_ref, ph_ref, w_ref, ga_ref, gc_ref, m_ref):
        i = pl.program_id(0)
        ob, gb, _, _, _, _, cv, _ = _mixer_recompute(i, o_ref, pr_ref, ph_ref, w_ref)
        conv = [gb[q] * cv[q] for q in range(4)]
        ra, rc = _rms_blocks(ob), _rms_blocks(conv)
        for q in range(4):
            sl = slice(q * BLK, (q + 1) * BLK)
            m_ref[:, q * BLK:(q + 1) * BLK] = (ob[q] * ra * ga_ref[:, sl]).astype(BF16)
            m_ref[:, ATTN_W + q * BLK:ATTN_W + (q + 1) * BLK] = (conv[q] * rc * gc_ref[:, sl]).astype(BF16)

    conv_s, halo_s, ob_s = _mixer_specs(S, tm)
    full = lambda r, c: pl.BlockSpec((r, c), lambda i: (0, 0))
    return pl.pallas_call(
        body, grid=(S // tm,), name="mixer_post_fwd",
        in_specs=[ob_s, conv_s, halo_s, full(3, 512), full(1, 512), full(1, 512)],
        out_specs=pl.BlockSpec((tm, D_MODEL), lambda i: (i, 0)), out_shape=SDS((S, D_MODEL), BF16),
        compiler_params=_cp("parallel"))(o, proj, proj, w_sc, g_a, g_c)


def mixer_post_bwd_a(dmixed, o, proj, w_sc, g_a, g_c, job=None):
    S = o.shape[1]
    tm = 512

    def body(dm_ref, o_ref, pr_ref, ph_ref, w_ref, ga_ref, gc_ref, do_ref, dgb_ref, dcv_ref, dga_ref, dgc_ref):
        i = pl.program_id(0)
        ob, gb, _, _, _, _, cv, _ = _mixer_recompute(i, o_ref, pr_ref, ph_ref, w_ref)
        conv = [gb[q] * cv[q] for q in range(4)]
        ra, rc = _rms_blocks(ob), _rms_blocks(conv)

        @pl.when(i == 0)
        def _():
            dga_ref[...] = jnp.zeros_like(dga_ref)
            dgc_ref[...] = jnp.zeros_like(dgc_ref)

        for blocks, r, g_ref, off, dg_ref, is_attn in ((ob, ra, ga_ref, 0, dga_ref, True), (conv, rc, gc_ref, ATTN_W, dgc_ref, False)):
            xh = [blocks[q] * r for q in range(4)]
            dm = [dm_ref[:, off + q * BLK:off + (q + 1) * BLK].astype(F32) for q in range(4)]
            gd = [dm[q] * g_ref[:, q * BLK:(q + 1) * BLK] for q in range(4)]
            mean = sum(_rsum(gd[q] * xh[q]) for q in range(4)) / (4 * BLK)
            for q in range(4):
                dg_ref[:, q * BLK:(q + 1) * BLK] += _colsum(dm[q] * xh[q])
                dx = r * (gd[q] - xh[q] * mean)
                if is_attn:
                    do_ref[q] = dx
                else:
                    dgb_ref[q] = dx * cv[q]
                    dcv_ref[q] = dx * gb[q]

    conv_s, halo_s, ob_s = _mixer_specs(S, tm)
    full = lambda r, c: pl.BlockSpec((r, c), lambda i: (0, 0))
    return _pcall(
        body, [dmixed, o, proj, proj, w_sc, g_a, g_c], grid=(S // tm,), name="mixer_post_bwd_a",
        in_specs=[pl.BlockSpec((tm, D_MODEL), lambda i: (i, 0)), ob_s, conv_s, halo_s, full(3, 512), full(1, 512), full(1, 512)],
        out_specs=[ob_s, ob_s, ob_s, full(1, 512), full(1, 512)],
        out_shape=[SDS((4, S, BLK), F32)] * 3 + [SDS((1, 512), F32)] * 2, sem=("arbitrary",), job=job)


def mixer_post_bwd_b(dproj, dgb, dcv, proj, w_sc):
    S = proj.shape[2]
    tm = 512
    last = S // 8 - 1

    def body(dp_in, dgb_ref, dcv_ref, dn_ref, pr_ref, w_ref, dp_ref, dw_ref):
        i = pl.program_id(0)
        keep_next = jnp.where(i < pl.num_programs(0) - 1, 1.0, 0.0)

        @pl.when(i == 0)
        def _():
            dw_ref[...] = jnp.zeros_like(dw_ref)

        for q in range(4):
            sl = slice(q * BLK, (q + 1) * BLK)
            gc, xi = pr_ref[4 + q].astype(F32), pr_ref[8 + q].astype(F32)
            u = gc * xi
            dcv = dcv_ref[q]
            dn = dn_ref[q] * keep_next
            d1, d2 = _shift_up(dcv, 1, dn), _shift_up(dcv, 2, dn)
            w0, w1, w2 = (w_ref[k:k + 1, sl] for k in range(3))
            du = dcv * w2 + d1 * w1 + d2 * w0
            dw_ref[0:1, sl] += _colsum(d2 * u)
            dw_ref[1:2, sl] += _colsum(d1 * u)
            dw_ref[2:3, sl] += _colsum(dcv * u)
            dp_ref[q] = dgb_ref[q].astype(BF16)
            dp_ref[4 + q] = (du * xi).astype(BF16)
            dp_ref[8 + q] = (du * gc).astype(BF16)

    conv_s, _, ob_s = _mixer_specs(S, tm)
    nxt = pl.BlockSpec((4, 8, BLK), lambda i: (0, jnp.minimum((i + 1) * (tm // 8), last), 0))
    full = lambda r, c: pl.BlockSpec((r, c), lambda i: (0, 0))
    return pl.pallas_call(
        body, grid=(S // tm,), name="mixer_post_bwd_b",
        in_specs=[pl.BlockSpec(memory_space=pl.ANY), ob_s, ob_s, nxt, conv_s, full(3, 512)],
        out_specs=[conv_s, full(3, 512)],
        out_shape=[SDS(dproj.shape, BF16), SDS((3, 512), F32)],
        input_output_aliases={0: 0}, compiler_params=_cp("arbitrary"))(dproj, dgb, dcv, dcv, proj, w_sc)


def xattn_fwd(q, k, v):
    S = q.shape[0]
    tm = 512
    scale = MEM_HD ** -0.5

    def body(q_ref, k_ref, v_ref, o_ref):
        for h in range(N_MEM_HEADS):
            sl = slice(h * MEM_HD, (h + 1) * MEM_HD)
            s = _dot_nt(q_ref[:, sl], k_ref[:, sl]) * scale
            p = jnp.exp(s - jnp.max(s, axis=1, keepdims=True))
            p = p / _rsum(p)
            o_ref[:, sl] = _dot(p.astype(BF16), v_ref[:, sl]).astype(BF16)

    row = pl.BlockSpec((tm, D_MODEL), lambda i: (i, 0))
    kv = pl.BlockSpec(k.shape, lambda i: (0, 0))
    return pl.pallas_call(body, grid=(S // tm,), name="xattn_fwd", in_specs=[row, kv, kv], out_specs=row,
                          out_shape=SDS((S, D_MODEL), BF16), compiler_params=_cp("parallel"))(q, k, v)


def xattn_bwd(q, k, v, do):
    S = q.shape[0]
    tm = 512
    scale = MEM_HD ** -0.5

    def body(q_ref, k_ref, v_ref, do_ref, dq_ref, dk_ref, dv_ref):
        @pl.when(pl.program_id(0) == 0)
        def _():
            dk_ref[...] = jnp.zeros_like(dk_ref)
            dv_ref[...] = jnp.zeros_like(dv_ref)

        for h in range(N_MEM_HEADS):
            sl = slice(h * MEM_HD, (h + 1) * MEM_HD)
            qh, kh, vh, doh = q_ref[:, sl], k_ref[:, sl], v_ref[:, sl], do_ref[:, sl]
            s = _dot_nt(qh, kh) * scale
            p = jnp.exp(s - jnp.max(s, axis=1, keepdims=True))
            p = p / _rsum(p)
            dp = _dot_nt(doh, vh)
            ds = (p * (dp - _rsum(p * dp)) * scale).astype(BF16)
            dq_ref[:, sl] = _dot(ds, kh).astype(BF16)
            dk_ref[:, sl] += _dot_tn(ds, qh)
            dv_ref[:, sl] += _dot_tn(p.astype(BF16), doh)

    row = pl.BlockSpec((tm, D_MODEL), lambda i: (i, 0))
    kv = pl.BlockSpec(k.shape, lambda i: (0, 0))
    return pl.pallas_call(body, grid=(S // tm,), name="xattn_bwd", in_specs=[row, kv, kv, row], out_specs=[row, kv, kv],
                          out_shape=[SDS((S, D_MODEL), BF16), SDS(k.shape, F32), SDS(k.shape, F32)],
                          compiler_params=_cp("arbitrary"))(q, k, v, do)


FFN_TM, FFN_TC = 256, 1408


def _ffn_specs(S, order):
    tm, tc = FFN_TM, FFN_TC
    ij = (lambda a, b: (a, b)) if order == "ij" else (lambda a, b: (b, a))
    blk = pl.BlockSpec((2, tm, tc), lambda a, b: (0,) + ij(a, b))
    prev = pl.BlockSpec((2, 16, tc), lambda a, b: (0, jnp.maximum(ij(a, b)[0] * (tm // 16) - 1, 0), ij(a, b)[1]))
    nxt = pl.BlockSpec((2, 16, tc), lambda a, b: (0, jnp.minimum((ij(a, b)[0] + 1) * (tm // 16), S // 16 - 1), ij(a, b)[1]))
    wsp = pl.BlockSpec((2, 3, tc), lambda a, b: (0, 0, ij(a, b)[1]))
    bsp = pl.BlockSpec((2, 1, tc), lambda a, b: (0, 0, ij(a, b)[1]))
    act = pl.BlockSpec((tm, tc), lambda a, b: ij(a, b))
    return blk, prev, nxt, wsp, bsp, act


def _ffn_up(i, up_ref, uh_ref, w_ref, b_ref):
    keep = jnp.where(i > 0, 1.0, 0.0)
    out = []
    for half in range(2):
        u = up_ref[half].astype(F32)
        hu = uh_ref[half].astype(F32) * keep
        w0, w1, w2 = (w_ref[half, k:k + 1, :] for k in range(3))
        out.append(_conv3(u, hu, w0, w1, w2) + b_ref[half])
    return out


def ffn_act_fwd(up_pre, w_fc, b_fc):
    S = up_pre.shape[1]

    def body(up_ref, uh_ref, w_ref, b_ref, a_ref, gv_ref):
        g, v = _ffn_up(pl.program_id(0), up_ref, uh_ref, w_ref, b_ref)
        a_ref[...] = (g * jax.nn.sigmoid(g) * v).astype(BF16)
        gv_ref[0] = g.astype(BF16)
        gv_ref[1] = v.astype(BF16)

    blk, prev, _, wsp, bsp, act = _ffn_specs(S, "ij")
    return pl.pallas_call(body, grid=(S // FFN_TM, D_FF // FFN_TC), name="ffn_act_fwd",
                          in_specs=[blk, prev, wsp, bsp], out_specs=[act, blk],
                          out_shape=[SDS((S, D_FF), BF16), SDS((2, S, D_FF), BF16)],
                          compiler_params=_cp("parallel", "parallel"))(up_pre, up_pre, w_fc, b_fc)


def ffn_act_bwd_a(dact, up, job=None):
    S = up.shape[1]

    def body(da_ref, up_ref, du_ref, db_ref):
        i = pl.program_id(1)
        g, v = up_ref[0].astype(F32), up_ref[1].astype(F32)
        da = da_ref[...].astype(F32)
        sg = jax.nn.sigmoid(g)
        dg = da * v * (sg * (1.0 + g * (1.0 - sg)))
        dv = da * g * sg

        @pl.when(i == 0)
        def _():
            db_ref[...] = jnp.zeros_like(db_ref)

        du_ref[0] = dg.astype(BF16)
        du_ref[1] = dv.astype(BF16)
        db_ref[0] += _colsum(dg)
        db_ref[1] += _colsum(dv)

    blk, _, _, _, bsp, act = _ffn_specs(S, "ji")
    return _pcall(body, [dact, up], grid=(D_FF // FFN_TC, S // FFN_TM), name="ffn_act_bwd_a",
                  in_specs=[act, blk], out_specs=[blk, bsp],
                  out_shape=[SDS((2, S, D_FF), BF16), SDS((2, 1, D_FF), F32)], sem=("parallel", "arbitrary"), job=job)


def ffn_act_bwd_b(d_up, up_pre, w_fc):
    S = up_pre.shape[1]

    def body(du_ref, dn_ref, up_ref, w_ref, dp_ref, dw_ref):
        i = pl.program_id(1)
        keep_next = jnp.where(i < pl.num_programs(1) - 1, 1.0, 0.0)

        @pl.when(i == 0)
        def _():
            dw_ref[...] = jnp.zeros_like(dw_ref)

        for half in range(2):
            d = du_ref[half].astype(F32)
            dn = dn_ref[half].astype(F32) * keep_next
            u = up_ref[half].astype(F32)
            d1, d2 = _shift_up(d, 1, dn), _shift_up(d, 2, dn)
            w0, w1, w2 = (w_ref[half, k:k + 1, :] for k in range(3))
            dp_ref[half] = (d * w2 + d1 * w1 + d2 * w0).astype(BF16)
            dw_ref[half, 0:1, :] += _colsum(d2 * u)
            dw_ref[half, 1:2, :] += _colsum(d1 * u)
            dw_ref[half, 2:3, :] += _colsum(d * u)

    blk, _, nxt, wsp, _, _ = _ffn_specs(S, "ji")
    return pl.pallas_call(body, grid=(D_FF // FFN_TC, S // FFN_TM), name="ffn_act_bwd_b",
                          in_specs=[blk, nxt, blk, wsp], out_specs=[blk, wsp],
                          out_shape=[SDS((2, S, D_FF), BF16), SDS((2, 3, D_FF), F32)],
                          compiler_params=_cp("parallel", "arbitrary"))(d_up, d_up, up_pre, w_fc)


def final_loss(x3, g, target):
    S, Dm = x3.shape
    tm = 512

    def body(x_ref, g_ref, t_ref, loss_ref, dx_ref, dg_ref):
        i = pl.program_id(0)
        xv = x_ref[...]
        r = lax.rsqrt(jnp.mean(xv * xv, axis=1, keepdims=True) + EPS)
        xh = xv * r
        err = xh * g_ref[...] - t_ref[...]
        dy = err / Dm
        gd = dy * g_ref[...]
        dx_ref[...] = r * (gd - xh * jnp.mean(gd * xh, axis=1, keepdims=True))

        @pl.when(i == 0)
        def _():
            dg_ref[...] = jnp.zeros_like(dg_ref)
            loss_ref[...] = jnp.zeros_like(loss_ref)

        dg_ref[...] += _colsum(dy * xh)
        loss_ref[...] += 0.5 * _colsum(jnp.mean(err * err, axis=1, keepdims=True))

    row = pl.BlockSpec((tm, Dm), lambda i: (i, 0))
    vec = pl.BlockSpec((1, Dm), lambda i: (0, 0))
    one = pl.BlockSpec((1, 1), lambda i: (0, 0))
    return pl.pallas_call(body, grid=(S // tm,), name="final_loss", in_specs=[row, vec, row], out_specs=[one, row, vec],
                          out_shape=[SDS((1, 1), F32), SDS((S, Dm), F32), SDS((1, Dm), F32)],
                          compiler_params=_cp("arbitrary"))(x3, g, target)


def local_step(x, mem, target, wb, ws, comm=None):
    S = x.shape[0]
    assert S % 2048 == 0
    if comm is None:
        comm = NoComm()
    else:
        wb = comm.gather_first()
    w_fc = ws["w_ffn_conv"].reshape(3, 2, D_FF).transpose(1, 0, 2)
    b_fc = ws["b_ffn_conv"].reshape(2, 1, D_FF)

    bias = bias_tables(ws["rel_bias"])
    h1 = rmsnorm_fwd(x, ws["g_mix"], "norm_mix")
    proj = mm_nn(h1, wb["w_in"], "proj", out_dtype=BF16, out_bw=BLK, tn=768, job=comm.job("proj"))
    comm.landed("proj", wb)
    qkv = proj.reshape(2, 3, 4, S, BLK)
    o, lse = attn_fwd(qkv, bias, job=comm.job("attn_fwd"))
    comm.landed("attn_fwd", wb)
    proj4 = proj.reshape(2, 12, S, BLK)
    mixed = mixer_post_fwd(o, proj4, ws["w_short_conv"], ws["g_attn_out"], ws["g_conv_out"])
    x1 = mm_nn(mixed, wb["w_out"], "out_proj", res=x)
    h2 = rmsnorm_fwd(x1, ws["g_xattn"], "norm_xattn")
    mem_n = rmsnorm_fwd(mem, ws["g_mem"], "norm_mem")
    xq = mm_nn(h2, wb["w_xq"], "xq", out_dtype=BF16)
    xk = mm_nn(mem_n, wb["w_xk"], "xk", out_dtype=BF16, tn=1024)
    xv = mm_nn(mem_n, wb["w_xv"], "xv", out_dtype=BF16, tn=1024)
    xo = xattn_fwd(xq, xk, xv)
    x2 = mm_nn(xo, wb["w_xo"], "xo_proj", res=x1)
    h3 = rmsnorm_fwd(x2, ws["g_ffn"], "norm_ffn")
    up_pre = mm_nn(h3, wb["w_up"], "up_proj", out_dtype=BF16, out_bw=D_FF, tn=1408)
    act, up = ffn_act_fwd(up_pre, w_fc, b_fc)
    x3 = mm_nn(act, wb["w_down"], "down_proj", res=x2)
    loss, dx3, dg_final = final_loss(x3, ws["g_final"].reshape(1, -1), target)

    gb, gs = {}, {"g_final": dg_final}
    gb["w_down"] = mm_tn(act, dx3, None, "dw_down", tm=1408, tn=1024)
    dact = mm_nt(dx3, None, wb["w_down"], "d_act", out_dtype=BF16, tn=1408)
    d_up, db_fc = ffn_act_bwd_a(dact, up)
    d_up_pre, dw_fc = ffn_act_bwd_b(d_up, up_pre, w_fc)
    gs["b_ffn_conv"] = db_fc.reshape(1, 2 * D_FF)
    gs["w_ffn_conv"] = dw_fc.transpose(1, 0, 2).reshape(3, 2 * D_FF)
    gb["w_up"] = mm_tn(h3, d_up_pre, D_FF, "dw_up", shards=4, tn=1408)
    dx2, gs["g_ffn"] = mm_nt_norm_bwd(d_up_pre, D_FF, wb["w_up"], x2, ws["g_ffn"], dx3, "d_h3", tm=256,
                                      job=comm.job("d_h3", gb))
    comm.landed("d_h3")
    gb["w_xo"] = mm_tn(xo, dx2, None, "dw_xo")
    dxo = mm_nt(dx2, None, wb["w_xo"], "d_xo", out_dtype=BF16)
    dxq, dxk, dxv = xattn_bwd(xq, xk, xv, dxo)
    gb["w_xq"] = mm_tn(h2, dxq, None, "dw_xq")
    gb["w_xk"] = mm_tn(mem_n, dxk, None, "dw_xk", tn=1024)
    gb["w_xv"] = mm_tn(mem_n, dxv, None, "dw_xv", tn=1024)
    dmem_n = mm_nt(dxk, None, wb["w_xk"], "d_memk", tn=1024)
    dmem_n = mm_nt(dxv, None, wb["w_xv"], "d_memv", res=dmem_n, tn=1024)
    gs["g_mem"] = rmsnorm_bwd(mem, ws["g_mem"], dmem_n, None, "norm_mem_bwd")
    dx1, gs["g_xattn"] = mm_nt_norm_bwd(dxq, None, wb["w_xq"], x1, ws["g_xattn"], dx2, "d_h2")
    gb["w_out"] = mm_tn(mixed, dx1, None, "dw_out")
    dmixed = mm_nt(dx1, None, wb["w_out"], "d_mixed")
    do, dgb, dcv, gs["g_attn_out"], gs["g_conv_out"] = mixer_post_bwd_a(
        dmixed, o, proj4, ws["w_short_conv"], ws["g_attn_out"], ws["g_conv_out"], job=comm.job("mixer_post_bwd_a", gb))
    comm.landed("mixer_post_bwd_a")
    dproj, ds_sum = attn_bwd(qkv, bias, o, lse, do, job=comm.job("attn_bwd"))
    comm.landed("attn_bwd")
    gs["rel_bias"] = bias_tables_bwd(ds_sum)
    dproj, gs["w_short_conv"] = mixer_post_bwd_b(dproj.reshape(2, 12, S, BLK), dgb, dcv, proj4, ws["w_short_conv"])
    dproj = dproj.reshape(24, S, BLK)
    gb["w_in"] = mm_tn(h1, dproj, BLK, "dw_in", shards=4, tn=768, job=comm.job("dw_in"))
    comm.landed("dw_in")
    grad_x, gs["g_mix"] = mm_nt_norm_bwd(dproj, BLK, wb["w_in"], x, ws["g_mix"], dx1, "d_h1", tm=256,
                                         job=comm.job("d_h1", gb))
    comm.landed("d_h1")
    return loss, grad_x, comm.finish(gb), gs


def adamw(w, g, m, v, name):
    R, C = w.shape
    tr = R
    for cand in (256, 352):
        if R > cand and R % cand == 0:
            tr = cand
            break

    def body(w_ref, g_ref, m_ref, v_ref, d_ref, nm_ref, nv_ref):
        gv = g_ref[...]
        mn = ADAM_B1 * m_ref[...] + (1.0 - ADAM_B1) * gv
        vn = ADAM_B2 * v_ref[...] + (1.0 - ADAM_B2) * (gv * gv)
        m_hat = mn / (1.0 - ADAM_B1 ** ADAM_STEP)
        v_hat = vn / (1.0 - ADAM_B2 ** ADAM_STEP)
        d_ref[...] = -ADAM_LR * (m_hat / (jnp.sqrt(v_hat) + ADAM_EPS) + ADAM_WD * w_ref[...])
        nm_ref[...] = mn
        nv_ref[...] = vn

    blk = pl.BlockSpec((tr, C), lambda i: (i, 0))
    return pl.pallas_call(body, grid=(R // tr,), name=name, in_specs=[blk] * 4, out_specs=[blk] * 3,
                          out_shape=[SDS((R, C), F32)] * 3, compiler_params=_cp("parallel"))(w, g, m, v)


BIG = ("w_in", "w_out", "w_xq", "w_xk", "w_xv", "w_xo", "w_up", "w_down")
COL_SHARDED = ("w_in", "w_up")
N_BIG = len(BIG)
ANY = pl.BlockSpec(memory_space=pl.ANY)


def _place():
    x, y, c = lax.axis_index("x"), lax.axis_index("y"), lax.axis_index("c")
    chips = [(1 - x, y), (x, 1 - y), (1 - x, 1 - y)]
    return x, y, c, chips


def _window(full, name, R, C, shard, half):
    r0, nr = (0, R) if half is None else (half * (R // 2), R // 2)
    if name in COL_SHARDED:
        return full.at[pl.ds(r0, nr), pl.ds(shard * C, C)]
    return full.at[pl.ds(shard * R + r0, nr), :]


def place_shard(w, mine_arr, n):
    R, C = w.shape
    col = n in COL_SHARDED

    def body(s_ref, w_ref, o_ref):
        o_ref[...] = w_ref[...].astype(BF16)

    grid_spec = pltpu.PrefetchScalarGridSpec(
        num_scalar_prefetch=1, grid=(1,), in_specs=[pl.BlockSpec((R, C), lambda i, s: (0, 0))],
        out_specs=pl.BlockSpec((R, C), (lambda i, s: (0, s[0])) if col else (lambda i, s: (s[0], 0))))
    return pl.pallas_call(body, grid_spec=grid_spec, name="place_" + n,
                          out_shape=SDS((R, 4 * C) if col else (4 * R, C), BF16),
                          compiler_params=_cp("arbitrary"))(mine_arr, w)


def _gather_jobs(names, shapes):
    nw = len(names)

    def start(full, sems):
        send, recv, fsend, frecv = sems
        x, y, c, chips = _place()
        mine = 2 * x + y
        for w, n in enumerate(names):
            R, C = shapes[w]
            own = _window(full[w], n, R, C, mine, c)
            for j, chip in enumerate(chips):
                pltpu.make_async_remote_copy(src_ref=own, dst_ref=own, send_sem=send.at[w, j], recv_sem=recv.at[w, j],
                                             device_id=(*chip, c), device_id_type=MESH).start()

    def mid(full, sems):
        send, recv, fsend, frecv = sems
        x, y, c, chips = _place()
        sib = (x, y, 1 - c)
        for w, n in enumerate(names):
            R, C = shapes[w]
            for j, chip in enumerate(chips):
                landed = _window(full[w], n, R, C, 2 * chip[0] + chip[1], c)
                pltpu.make_async_remote_copy(src_ref=landed, dst_ref=landed, send_sem=send.at[w, j], recv_sem=recv.at[w, j],
                                             device_id=(*chip, c), device_id_type=MESH).wait_recv()
                pltpu.make_async_remote_copy(src_ref=landed, dst_ref=landed, send_sem=fsend.at[w, j],
                                             recv_sem=frecv.at[w, j], device_id=sib, device_id_type=MESH).start()

    def finish(full, sems):
        send, recv, fsend, frecv = sems
        x, y, c, chips = _place()
        mine = 2 * x + y
        sib = (x, y, 1 - c)
        for w, n in enumerate(names):
            R, C = shapes[w]
            own = _window(full[w], n, R, C, mine, c)
            for j, chip in enumerate(chips):
                landed = _window(full[w], n, R, C, 2 * chip[0] + chip[1], c)
                other = _window(full[w], n, R, C, 2 * chip[0] + chip[1], 1 - c)
                pltpu.make_async_remote_copy(src_ref=other, dst_ref=other, send_sem=fsend.at[w, j], recv_sem=frecv.at[w, j],
                                             device_id=sib, device_id_type=MESH).wait_recv()
                pltpu.make_async_remote_copy(src_ref=own, dst_ref=own, send_sem=send.at[w, j], recv_sem=recv.at[w, j],
                                             device_id=(*chip, c), device_id_type=MESH).wait_send()
                pltpu.make_async_remote_copy(src_ref=landed, dst_ref=landed, send_sem=fsend.at[w, j],
                                             recv_sem=frecv.at[w, j], device_id=sib, device_id_type=MESH).wait_send()

    return start, mid, finish, [pltpu.SemaphoreType.DMA((nw, 3))] * 4


class CommJob:
    def __init__(self, ins, out_shapes, inplace, start, finish, sems, mid=None):
        self.ins, self.out_shapes, self.inplace = list(ins), list(out_shapes), inplace
        self.start, self.mid, self.finish, self.sems = start, mid, finish, list(sems)

    def run(self, phase, ins, outs, sems):
        if phase == 0:
            self.start(ins, outs, sems)
        elif phase == 1:
            if self.mid is not None:
                self.mid(ins, outs, sems)
        else:
            self.finish(ins, outs, sems)


def run_job(name, job):
    n_in, n_out = len(job.ins), len(job.out_shapes)

    def body(*refs):
        ins, outs, sm = refs[:n_in], refs[n_in:n_in + n_out], refs[n_in + n_out:]
        for phase in range(3):
            job.run(phase, ins, outs, sm)

    job.landed = list(pl.pallas_call(
        body, name=name, in_specs=[ANY] * n_in, out_specs=[ANY] * n_out, out_shape=job.out_shapes,
        input_output_aliases={i: i for i in range(n_in)} if job.inplace else {},
        scratch_shapes=job.sems)(*job.ins))
    return job.landed


class Carried:
    def __init__(self, job, n_in, n_out, n_scratch, steps):
        self.job, self.n_in, self.n_out, self.n_scratch, self.steps = job, n_in, n_out, n_scratch, steps
        self.nji = len(job.ins) if job else 0
        self.njo = len(job.out_shapes) if job else 0

    def in_specs(self):
        return [ANY] * self.nji

    def out_specs(self):
        return [ANY] * self.njo

    def out_shapes(self):
        return list(self.job.out_shapes) if self.job else []

    def aliases(self):
        if not (self.job and self.job.inplace):
            return {}
        return {self.n_in + i: self.n_out + i for i in range(self.nji)}

    def sems(self):
        return list(self.job.sems) if self.job else []

    def args(self):
        return list(self.job.ins) if self.job else []

    def split(self, refs):
        a = self.n_in
        b = a + self.nji
        c = b + self.n_out
        d = c + self.njo
        e = d + self.n_scratch
        return refs[:a] + refs[b:c] + refs[d:e], (refs[a:b], refs[c:d], refs[e:])

    def phase(self, phase, step, jrefs):
        if self.job is None:
            return
        at = {0: 0, 1: max(self.steps - 2, 0), 2: self.steps - 1}[phase]

        @pl.when(step == at)
        def _():
            self.job.run(phase, *jrefs)


def _pcall(body, args, *, grid, in_specs, out_specs, out_shape, name, sem, scratch=(), aliases=None, job=None):
    n_in, n_out = len(args), len(out_shape)
    if job is None:
        return pl.pallas_call(
            body, grid=grid, in_specs=list(in_specs), out_specs=list(out_specs), out_shape=list(out_shape), name=name,
            scratch_shapes=list(scratch), input_output_aliases=dict(aliases or {}), compiler_params=_cp(*sem))(*args)
    total = int(np.prod(grid))
    car = Carried(job, n_in, n_out, len(scratch), total)

    def wrapped(*refs):
        main, jrefs = car.split(refs)
        lin = pl.program_id(0)
        for ax in range(1, len(grid)):
            lin = lin * grid[ax] + pl.program_id(ax)
        car.phase(0, lin, jrefs)
        body(*main)
        car.phase(1, lin, jrefs)
        car.phase(2, lin, jrefs)

    res = pl.pallas_call(
        wrapped, grid=grid, in_specs=list(in_specs) + car.in_specs(), out_specs=list(out_specs) + car.out_specs(),
        out_shape=list(out_shape) + car.out_shapes(), name=name, scratch_shapes=list(scratch) + car.sems(),
        input_output_aliases={**dict(aliases or {}), **car.aliases()},
        compiler_params=_cp(*(["arbitrary"] * len(grid))))(*args, *car.args())
    job.landed = list(res[n_out:])
    return list(res[:n_out])


def gather_job(placed, names):
    shapes = []
    for n in names:
        R, C = placed[n].shape
        shapes.append((R, C // 4) if n in COL_SHARDED else (R // 4, C))
    start, mid, finish, sems = _gather_jobs(names, shapes)
    arrays = [placed[n] for n in names]
    return CommJob(arrays, [SDS(a.shape, a.dtype) for a in arrays], True,
                   lambda i, o, s: start(o, s), lambda i, o, s: finish(o, s), sems, mid=lambda i, o, s: mid(o, s))


def pair_exchange_job(grads, names):
    shapes = [grads[n].shape for n in names]

    def copies(ins, outs, sems):
        x, y, c, _ = _place()
        return [pltpu.make_async_remote_copy(
            src_ref=ins[w].at[:, pl.ds((1 - c) * (shapes[w][1] // 2), shapes[w][1] // 2), :], dst_ref=outs[w],
            send_sem=sems[0].at[w], recv_sem=sems[1].at[w], device_id=(x, y, 1 - c), device_id_type=MESH)
            for w in range(len(names))]

    def start(ins, outs, sems):
        for cp in copies(ins, outs, sems):
            cp.start()

    def finish(ins, outs, sems):
        for cp in copies(ins, outs, sems):
            cp.wait()

    return CommJob([grads[n] for n in names], [SDS((4, s[1] // 2, s[2]), F32) for s in shapes], False, start, finish,
                   [pltpu.SemaphoreType.DMA((len(names),))] * 2)


def chip_exchange_job(p16, names):
    shapes = [p16[n].shape for n in names]

    def copies(ins, outs, sems):
        x, y, c, chips = _place()
        return [pltpu.make_async_remote_copy(
            src_ref=ins[w].at[2 * chip[0] + chip[1]], dst_ref=outs[w].at[j],
            send_sem=sems[0].at[w, j], recv_sem=sems[1].at[w, j], device_id=(*chip, c), device_id_type=MESH)
            for w in range(len(names)) for j, chip in enumerate(chips)]

    def start(ins, outs, sems):
        for cp in copies(ins, outs, sems):
            cp.start()

    def finish(ins, outs, sems):
        for cp in copies(ins, outs, sems):
            cp.wait()

    return CommJob([p16[n] for n in names], [SDS((3,) + tuple(s[1:]), BF16) for s in shapes], False, start, finish,
                   [pltpu.SemaphoreType.DMA((len(names), 3))] * 2)


def half_exchange_job(full, names):
    shapes = [full[n].shape for n in names]

    def copies(outs, sems):
        x, y, c, _ = _place()
        cps = []
        for w in range(len(names)):
            Rh = shapes[w][0] // 2
            rows = outs[w].at[pl.ds(c * Rh, Rh), :]
            other = outs[w].at[pl.ds((1 - c) * Rh, Rh), :]
            cps.append((pltpu.make_async_remote_copy(src_ref=rows, dst_ref=rows, send_sem=sems[0].at[w], recv_sem=sems[1].at[w],
                                                     device_id=(x, y, 1 - c), device_id_type=MESH),
                        pltpu.make_async_remote_copy(src_ref=other, dst_ref=other, send_sem=sems[0].at[w], recv_sem=sems[1].at[w],
                                                     device_id=(x, y, 1 - c), device_id_type=MESH)))
        return cps

    def start(ins, outs, sems):
        for send, _ in copies(outs, sems):
            send.start()

    def finish(ins, outs, sems):
        for send, recv in copies(outs, sems):
            recv.wait_recv()
            send.wait_send()

    arrays = [full[n] for n in names]
    return CommJob(arrays, [SDS(a.shape, a.dtype) for a in arrays], True, start, finish,
                   [pltpu.SemaphoreType.DMA((len(names),))] * 2)


def allreduce_small(parts, name):
    n = len(parts)

    def body(*refs):
        ins, outs, bufs = refs[:n], refs[n:2 * n], refs[2 * n:3 * n]
        send, recv = refs[3 * n:]
        x, y, c, _ = _place()
        me = 4 * x + 2 * y + c
        sends = []
        for p in range(n):
            for k in range(1, 8):
                peer = (x ^ (k >> 2), y ^ ((k >> 1) & 1), c ^ (k & 1))
                cp = pltpu.make_async_remote_copy(src_ref=ins[p], dst_ref=bufs[p].at[k], send_sem=send.at[p, k - 1],
                                                  recv_sem=recv.at[p, k - 1], device_id=peer, device_id_type=MESH)
                cp.start()
                sends.append(cp)
            bufs[p][0] = ins[p][...]
        for cp in sends:
            cp.wait_recv()
        for p in range(n):
            acc = bufs[p][me]
            for d in range(1, 8):
                acc = acc + bufs[p][d ^ me]
            outs[p][...] = acc
        for cp in sends:
            cp.wait_send()

    vm = pl.BlockSpec(memory_space=pltpu.VMEM)
    return pl.pallas_call(
        body, name=name, in_specs=[vm] * n, out_specs=[vm] * n, out_shape=[SDS(a.shape, F32) for a in parts],
        scratch_shapes=[pltpu.VMEM((8,) + tuple(a.shape), F32) for a in parts] + [pltpu.SemaphoreType.DMA((n, 7))] * 2,
    )(*parts)


def adamw_multi(ws, gs, ms, vs, name):
    n = len(ws)

    def body(*refs):
        for p in range(n):
            w_ref, g_ref, m_ref, v_ref = (refs[q * n + p] for q in range(4))
            d_ref, nm_ref, nv_ref = (refs[(4 + q) * n + p] for q in range(3))
            gv = g_ref[...]
            mn = ADAM_B1 * m_ref[...] + (1.0 - ADAM_B1) * gv
            vn = ADAM_B2 * v_ref[...] + (1.0 - ADAM_B2) * (gv * gv)
            m_hat = mn / (1.0 - ADAM_B1 ** ADAM_STEP)
            v_hat = vn / (1.0 - ADAM_B2 ** ADAM_STEP)
            d_ref[...] = -ADAM_LR * (m_hat / (jnp.sqrt(v_hat) + ADAM_EPS) + ADAM_WD * w_ref[...])
            nm_ref[...] = mn
            nv_ref[...] = vn

    vm = pl.BlockSpec(memory_space=pltpu.VMEM)
    outs = pl.pallas_call(body, name=name, in_specs=[vm] * (4 * n), out_specs=[vm] * (3 * n),
                          out_shape=[SDS(a.shape, F32) for a in ws] * 3)(*ws, *gs, *ms, *vs)
    return outs[:n], outs[n:2 * n], outs[2 * n:]


def pair_sum(g, xh, sel_arr, name):
    _, R, C = g.shape
    Rh = R // 2

    def body(sel_ref, g_ref, x_ref, p32_ref, p16_ref):
        s = g_ref[...] + x_ref[...]
        p16_ref[...] = s.astype(BF16)

        @pl.when(pl.program_id(0) == sel_ref[0])
        def _():
            p32_ref[...] = s

    same = pl.BlockSpec((None, Rh, C), lambda s, sel: (s, 0, 0))
    grid_spec = pltpu.PrefetchScalarGridSpec(
        num_scalar_prefetch=1, grid=(4,),
        in_specs=[pl.BlockSpec((None, Rh, C), lambda s, sel: (s, sel[1], 0)), same],
        out_specs=[pl.BlockSpec((Rh, C), lambda s, sel: (0, 0)), same])
    return pl.pallas_call(body, grid_spec=grid_spec, name=name, out_shape=[SDS((Rh, C), F32), SDS((4, Rh, C), BF16)],
                          compiler_params=_cp("arbitrary"))(sel_arr, g, xh)


def chip_sum(p32, y, sel_arr, name):
    Rh, C = p32.shape

    def body(s_ref, p_ref, y_ref, o_ref):
        o_ref[...] = ((p_ref[...] + y_ref[0].astype(F32)) + y_ref[1].astype(F32)) + y_ref[2].astype(F32)

    grid_spec = pltpu.PrefetchScalarGridSpec(
        num_scalar_prefetch=1, grid=(1,),
        in_specs=[pl.BlockSpec((Rh, C), lambda i, s: (0, 0)), pl.BlockSpec((3, Rh, C), lambda i, s: (0, 0, 0))],
        out_specs=pl.BlockSpec((Rh, C), lambda i, s: (s[1], 0)))
    return pl.pallas_call(body, grid_spec=grid_spec, name=name, out_shape=SDS((2 * Rh, C), F32),
                          compiler_params=_cp("arbitrary"))(sel_arr, p32, y)


class StepComm:
    FIRST = ("w_in",)
    SQUARE = ("w_out", "w_xq", "w_xk", "w_xv", "w_xo")
    FFN = ("w_up", "w_down")
    PLAN = {
        "proj": ("gather", SQUARE), "attn_fwd": ("gather", FFN),
        "d_h3": ("pair", FFN), "mixer_post_bwd_a": ("pair", SQUARE), "attn_bwd": ("chip", FFN + SQUARE),
        "dw_in": ("half", FFN + SQUARE), "d_h1": ("pair", FIRST), "end_chip": ("chip", FIRST),
    }

    def __init__(self, placed, sel_arr):
        self.placed, self.sel_arr = placed, sel_arr
        self.g, self.p32, self.p16, self.full, self.done, self.jobs = {}, {}, {}, {}, {}, {}

    def gather_first(self):
        return dict(zip(self.FIRST, run_job("gather_first", gather_job(self.placed, self.FIRST))))

    def job(self, tag, gb=None):
        kind, names = self.PLAN[tag]
        if kind == "gather":
            j = gather_job(self.placed, names)
        elif kind == "pair":
            for n in names:
                self.g[n] = gb[n] if gb[n].ndim == 3 else gb[n].reshape(4, gb[n].shape[0] // 4, gb[n].shape[1])
            j = pair_exchange_job(self.g, names)
        elif kind == "chip":
            j = chip_exchange_job(self.p16, names)
        else:
            j = half_exchange_job(self.full, names)
        self.jobs[tag] = j
        return j

    def landed(self, tag, wb=None):
        kind, names = self.PLAN[tag]
        for n, a in zip(names, self.jobs[tag].landed):
            if kind == "gather":
                wb[n] = a
            elif kind == "pair":
                self.p32[n], self.p16[n] = pair_sum(self.g[n], a, self.sel_arr, "pair_sum_" + n)
            elif kind == "chip":
                self.full[n] = chip_sum(self.p32[n], a, self.sel_arr, "chip_sum_" + n)
            else:
                self.done[n] = a

    def finish(self, gb):
        run_job("chip_exchange_last", self.job("end_chip"))
        self.landed("end_chip")
        last = run_job("half_exchange", half_exchange_job(self.full, self.FIRST))
        self.done.update(zip(self.FIRST, last))
        return self.done


class NoComm:
    def job(self, tag, gb=None):
        return None

    def landed(self, tag, wb=None):
        pass

    def finish(self, gb):
        return gb


SMALL = ("rel_bias", "g_mix", "w_short_conv", "g_attn_out", "g_conv_out", "g_xattn", "g_mem", "g_ffn",
         "w_ffn_conv", "b_ffn_conv", "g_final")
SMALL_FULL = {"rel_bias": (8, 32), "g_mix": (1, 1024), "w_short_conv": (3, 512), "g_attn_out": (1, 512), "g_conv_out": (1, 512),
              "g_xattn": (1, 1024), "g_mem": (1, 1024), "g_ffn": (1, 1024), "w_ffn_conv": (3, 5632), "b_ffn_conv": (1, 5632),
              "g_final": (1024,)}
SMALL_SHARDED = ("w_short_conv", "w_ffn_conv")


def _pack(parts):
    flat = jnp.concatenate([p.reshape(-1) for p in parts])
    rows = -(-flat.shape[0] // 1024) * 8
    return jnp.pad(flat, (0, rows * 128 - flat.shape[0])).reshape(rows, 128)


def _unpack(packed, names):
    flat, out, off = packed.reshape(-1), {}, 0
    for n in names:
        size = int(np.prod(SMALL_FULL[n]))
        out[n] = flat[off:off + size].reshape(SMALL_FULL[n])
        off += size
    return out


def kernel(x, mem, rel_bias, g_mix, w_in, w_short_conv, g_attn_out, g_conv_out, w_out, g_xattn, g_mem, w_xq, w_xk, w_xv, w_xo, g_ffn, w_up, w_ffn_conv, b_ffn_conv, w_down, g_final, loss_target, m_rel_bias, m_g_mix, m_w_in, m_w_short_conv, m_g_attn_out, m_g_conv_out, m_w_out, m_g_xattn, m_g_mem, m_w_xq, m_w_xk, m_w_xv, m_w_xo, m_g_ffn, m_w_up, m_w_ffn_conv, m_b_ffn_conv, m_w_down, m_g_final, v_rel_bias, v_g_mix, v_w_in, v_w_short_conv, v_g_attn_out, v_g_conv_out, v_w_out, v_g_xattn, v_g_mem, v_w_xq, v_w_xk, v_w_xv, v_w_xo, v_g_ffn, v_w_up, v_w_ffn_conv, v_b_ffn_conv, v_w_down, v_g_final):
    names = ("rel_bias", "g_mix", "w_in", "w_short_conv", "g_attn_out", "g_conv_out", "w_out", "g_xattn", "g_mem", "w_xq",
             "w_xk", "w_xv", "w_xo", "g_ffn", "w_up", "w_ffn_conv", "b_ffn_conv", "w_down", "g_final")
    W = dict(zip(names, (rel_bias, g_mix, w_in, w_short_conv, g_attn_out, g_conv_out, w_out, g_xattn, g_mem, w_xq, w_xk, w_xv,
                         w_xo, g_ffn, w_up, w_ffn_conv, b_ffn_conv, w_down, g_final)))
    M = dict(zip(names, (m_rel_bias, m_g_mix, m_w_in, m_w_short_conv, m_g_attn_out, m_g_conv_out, m_w_out, m_g_xattn, m_g_mem,
                         m_w_xq, m_w_xk, m_w_xv, m_w_xo, m_g_ffn, m_w_up, m_w_ffn_conv, m_b_ffn_conv, m_w_down, m_g_final)))
    V = dict(zip(names, (v_rel_bias, v_g_mix, v_w_in, v_w_short_conv, v_g_attn_out, v_g_conv_out, v_w_out, v_g_xattn, v_g_mem,
                         v_w_xq, v_w_xk, v_w_xv, v_w_xo, v_g_ffn, v_w_up, v_w_ffn_conv, v_b_ffn_conv, v_w_down, v_g_final)))
    xi, yi, ci = lax.axis_index("x"), lax.axis_index("y"), lax.axis_index("c")
    mine = 2 * xi + yi
    mine_arr = jnp.reshape(mine, (1,)).astype(jnp.int32)
    sel_arr = jnp.stack([mine, ci]).astype(jnp.int32)
    comm = StepComm({n: place_shard(W[n][0], mine_arr, n) for n in BIG}, sel_arr)
    first = jnp.where(ci == 0, 1.0, 0.0).astype(F32)
    emb = []
    for n in SMALL_SHARDED:
        shard = W[n][0]
        full = jnp.zeros(SMALL_FULL[n], F32)
        emb.append(lax.dynamic_update_slice(full, shard * first, (0, mine * shard.shape[1])))
    ws = {n: W[n] for n in SMALL if n not in SMALL_SHARDED}
    ws.update(zip(SMALL_SHARDED, allreduce_small(emb, "gather_small")))

    loss, grad_x, gfull, gs = local_step(x[0], mem[0], loss_target[0], None, ws, comm)

    def as2d(a):
        return a.reshape(1, -1) if a.ndim == 1 else a

    reduced = allreduce_small([as2d(gs[n]) for n in SMALL] + [loss], "reduce_small")
    gsm = dict(zip(SMALL, reduced[:-1]))
    loss = reduced[-1][0, 0]

    grads, delta, new_m, new_v = {}, {}, {}, {}
    for n in BIG:
        grads[n] = gfull[n][None]
        d, nm, nv = adamw(W[n][0], gfull[n], M[n][0], V[n][0], "adamw_" + n)
        delta[n], new_m[n], new_v[n] = d[None], nm[None], nv[None]
    for n in SMALL_SHARDED:
        wid = W[n].shape[2]
        gsm[n] = lax.dynamic_slice(gsm[n], (0, mine * wid), (3, wid))

    def own(a, n):
        return a[0] if n in SMALL_SHARDED else as2d(a)

    d, nm, nv = adamw_multi([own(W[n], n) for n in SMALL], [gsm[n] for n in SMALL], [own(M[n], n) for n in SMALL],
                            [own(V[n], n) for n in SMALL], "adamw_small")
    for i, n in enumerate(SMALL):
        shape = W[n].shape
        grads[n], delta[n], new_m[n], new_v[n] = (a.reshape(shape) for a in (gsm[n], d[i], nm[i], nv[i]))
    return (loss, grad_x[None], *[grads[n] for n in names], *[delta[n] for n in names],
            *[new_m[n] for n in names], *[new_v[n] for n in names])
```

```python
import functools
import math

import numpy as np
import jax
import jax.numpy as jnp
from jax import lax
from jax.experimental import pallas as pl
from jax.experimental.pallas import tpu as pltpu

F32 = jnp.float32
BF16 = jnp.bfloat16
SDS = jax.ShapeDtypeStruct
MESH = pl.DeviceIdType.MESH

D_MODEL = 1024
ATTN_W = 512
N_HEADS = 8
BLK = 128
PATTERNS = ((128, 1), (512, 4), (2048, 16))
N_BUCKETS = 32
D_FF = 2816
N_MEM_HEADS = 4
MEM_HD = 256
EPS = 1e-6
NEG = -1e30
VMEM_LIMIT = 56 * 1024 * 1024

ADAM_LR, ADAM_B1, ADAM_B2, ADAM_EPS, ADAM_WD, ADAM_STEP = 0.001, 0.9, 0.999, 1e-08, 0.01, 10


def _cp(*sem):
    return pltpu.CompilerParams(dimension_semantics=sem, vmem_limit_bytes=VMEM_LIMIT)


def _dot(a, b):
    return jnp.dot(a, b, preferred_element_type=F32)


def _dot_nt(a, b):
    return lax.dot_general(a, b, (((1,), (1,)), ((), ())), preferred_element_type=F32)


def _dot_tn(a, b):
    return lax.dot_general(a, b, (((0,), (0,)), ((), ())), preferred_element_type=F32)


def _rsum(x):
    return jnp.sum(x, axis=1, keepdims=True)


def rmsnorm_fwd(x, g, name):
    S, Dm = x.shape
    tm = min(S, 512)

    def body(x_ref, g_ref, o_ref):
        xv = x_ref[...]
        r = lax.rsqrt(jnp.mean(xv * xv, axis=1, keepdims=True) + EPS)
        o_ref[...] = (xv * r * g_ref[...]).astype(o_ref.dtype)

    return pl.pallas_call(
        body, grid=(S // tm,), name=name,
        in_specs=[pl.BlockSpec((tm, Dm), lambda i: (i, 0)), pl.BlockSpec((1, Dm), lambda i: (0, 0))],
        out_specs=pl.BlockSpec((tm, Dm), lambda i: (i, 0)),
        out_shape=SDS((S, Dm), BF16), compiler_params=_cp("parallel"))(x, g)


def rmsnorm_bwd(x, g, dh, res, name, job=None):
    S, Dm = x.shape
    tm = min(S, 512)
    want_dx = res is not None

    def body(*refs):
        if want_dx:
            x_ref, g_ref, dh_ref, res_ref, dx_ref, dg_ref = refs
        else:
            x_ref, g_ref, dh_ref, dg_ref = refs
        i = pl.program_id(0)
        xv = x_ref[...]
        r = lax.rsqrt(jnp.mean(xv * xv, axis=1, keepdims=True) + EPS)
        xh = xv * r
        dh = dh_ref[...].astype(F32)
        if want_dx:
            gd = dh * g_ref[...]
            dx_ref[...] = res_ref[...] + r * (gd - xh * jnp.mean(gd * xh, axis=1, keepdims=True))

        @pl.when(i == 0)
        def _():
            dg_ref[...] = jnp.zeros_like(dg_ref)

        dg_ref[...] += jnp.sum(dh * xh, axis=0, keepdims=True)

    row = pl.BlockSpec((tm, Dm), lambda i: (i, 0))
    vec = pl.BlockSpec((1, Dm), lambda i: (0, 0))
    if want_dx:
        return _pcall(body, [x, g, dh, res], grid=(S // tm,), name=name, in_specs=[row, vec, row, row], out_specs=[row, vec],
                      out_shape=[SDS((S, Dm), F32), SDS((1, Dm), F32)], sem=("arbitrary",), job=job)
    return pl.pallas_call(
        body, grid=(S // tm,), name=name, in_specs=[row, vec, row], out_specs=vec,
        out_shape=SDS((1, Dm), F32), compiler_params=_cp("arbitrary"))(x, g, dh)


def _col_spec(bw, tn, rows, row_of, col_of):
    if bw is None:
        return pl.BlockSpec((rows, tn), lambda *g: (row_of(*g), col_of(*g)))
    if tn % bw == 0:
        return pl.BlockSpec((tn // bw, rows, bw), lambda *g: (col_of(*g), row_of(*g), 0))
    per = bw // tn
    return pl.BlockSpec((None, rows, tn), lambda *g: (col_of(*g) // per, row_of(*g), col_of(*g) % per))


def _read_cols(ref, bw, tn):
    if bw is None or tn % bw != 0:
        return ref[...]
    if tn == bw:
        return ref[0]
    return jnp.concatenate([ref[q] for q in range(tn // bw)], axis=1)


def _write_cols(ref, bw, tn, val):
    if bw is None or tn % bw != 0:
        ref[...] = val.astype(ref.dtype)
    else:
        for q in range(tn // bw):
            ref[q] = val[:, q * bw:(q + 1) * bw].astype(ref.dtype)


def mm_nn(a, b, name, *, res=None, out_dtype=F32, out_bw=None, tm=1024, tn=512, job=None):
    M, K = a.shape
    N = b.shape[1]
    tm = min(tm, M)

    def body(*refs):
        if res is None:
            a_ref, b_ref, o_ref = refs
        else:
            a_ref, b_ref, r_ref, o_ref = refs
        acc = _dot(a_ref[...].astype(BF16), b_ref[...])
        if res is not None:
            acc = acc + r_ref[...]
        _write_cols(o_ref, out_bw, tn, acc)

    ri, ci = (lambda i, j: i), (lambda i, j: j)
    in_specs = [pl.BlockSpec((tm, K), lambda i, j: (i, 0)), pl.BlockSpec((K, tn), lambda i, j: (0, j))]
    args = [a, b]
    if res is not None:
        in_specs.append(pl.BlockSpec((tm, tn), lambda i, j: (i, j)))
        args.append(res)
    oshape = (M, N) if out_bw is None else (N // out_bw, M, out_bw)
    return _pcall(body, args, grid=(M // tm, N // tn), name=name, in_specs=in_specs,
                  out_specs=[_col_spec(out_bw, tn, tm, ri, ci)], out_shape=[SDS(oshape, out_dtype)],
                  sem=("parallel", "parallel"), job=job)[0]


def mm_nt(a, a_bw, b, name, *, res=None, out_dtype=F32, tm=1024, tn=512, job=None):
    if a_bw is None:
        M, K = a.shape
    else:
        M, K = a.shape[1], a.shape[0] * a_bw
    N = b.shape[0]
    tm = min(tm, M)

    def body(*refs):
        if res is None:
            a_ref, b_ref, o_ref = refs
        else:
            a_ref, b_ref, r_ref, o_ref = refs
        av = _read_cols(a_ref, a_bw, K).astype(BF16)
        acc = _dot_nt(av, b_ref[...])
        if res is not None:
            acc = acc + r_ref[...]
        o_ref[...] = acc.astype(o_ref.dtype)

    in_specs = [_col_spec(a_bw, K, tm, lambda i, j: i, lambda i, j: 0), pl.BlockSpec((tn, K), lambda i, j: (j, 0))]
    args = [a, b]
    if res is not None:
        in_specs.append(pl.BlockSpec((tm, tn), lambda i, j: (i, j)))
        args.append(res)
    return _pcall(body, args, grid=(M // tm, N // tn), name=name, in_specs=in_specs,
                  out_specs=[pl.BlockSpec((tm, tn), lambda i, j: (i, j))], out_shape=[SDS((M, N), out_dtype)],
                  sem=("parallel", "parallel"), job=job)[0]


def mm_nt_norm_bwd(a, a_bw, b, x, g, res, name, *, tm=512, job=None):
    if a_bw is None:
        M, K = a.shape
    else:
        M, K = a.shape[1], a.shape[0] * a_bw
    N = b.shape[0]

    def body(a_ref, b_ref, x_ref, g_ref, r_ref, dx_ref, dg_ref):
        dh = _dot_nt(_read_cols(a_ref, a_bw, K).astype(BF16), b_ref[...])
        xv = x_ref[...]
        r = lax.rsqrt(jnp.mean(xv * xv, axis=1, keepdims=True) + EPS)
        xh = xv * r
        gd = dh * g_ref[...]
        dx_ref[...] = r_ref[...] + r * (gd - xh * jnp.mean(gd * xh, axis=1, keepdims=True))

        @pl.when(pl.program_id(0) == 0)
        def _():
            dg_ref[...] = jnp.zeros_like(dg_ref)

        dg_ref[...] += jnp.sum(dh * xh, axis=0, keepdims=True)

    row = pl.BlockSpec((tm, N), lambda i: (i, 0))
    vec = pl.BlockSpec((1, N), lambda i: (0, 0))
    return _pcall(body, [a, b, x, g, res], grid=(M // tm,), name=name,
                  in_specs=[_col_spec(a_bw, K, tm, lambda i: i, lambda i: 0), pl.BlockSpec((N, K), lambda i: (0, 0)), row, vec, row],
                  out_specs=[row, vec], out_shape=[SDS((M, N), F32), SDS((1, N), F32)], sem=("arbitrary",), job=job)


def mm_tn(a, b, b_bw, name, *, shards=None, tm=1024, tn=1024, ts=1024, job=None):
    S, Ka = a.shape
    N = b.shape[1] if b_bw is None else b.shape[0] * b_bw
    ts = min(ts, S)
    tm = min(tm, Ka)

    def body(a_ref, b_ref, o_ref):
        @pl.when(pl.program_id(2) == 0)
        def _():
            o_ref[...] = jnp.zeros_like(o_ref)

        bv = _read_cols(b_ref, b_bw, tn).astype(BF16)
        o_ref[...] += _dot_tn(a_ref[...].astype(BF16), bv)

    in_specs = [pl.BlockSpec((ts, tm), lambda i, j, k: (k, i)),
                _col_spec(b_bw, tn, ts, lambda i, j, k: k, lambda i, j, k: j)]
    if shards is None:
        out_spec, oshape = pl.BlockSpec((tm, tn), lambda i, j, k: (i, j)), (Ka, N)
    else:
        per = (N // shards) // tn
        out_spec = pl.BlockSpec((None, tm, tn), lambda i, j, k: (j // per, i, j % per))
        oshape = (shards, Ka, N // shards)
    return _pcall(body, [a, b], grid=(Ka // tm, N // tn, S // ts), name=name, in_specs=in_specs, out_specs=[out_spec],
                  out_shape=[SDS(oshape, F32)], sem=("parallel", "parallel", "arbitrary"), job=job)[0]


def _bucket_tables():
    out = np.zeros((3, 2, BLK, BLK), np.int32)
    qi = np.arange(BLK)[:, None]
    for p, (win, dil) in enumerate(PATTERNS):
        w = win // dil
        assert w == BLK
        for half in range(2):
            kj = np.arange(BLK)[None, :] + half * BLK
            steps = qi + w - kj
            valid = (steps >= 0) & (steps <= w)
            dist = np.clip(steps, 0, w) * dil
            dd = np.maximum(dist, 1).astype(np.float32)
            large = 16 + (np.log(dd / np.float32(16)) / np.float32(math.log(2048 / 16)) * np.float32(16)).astype(np.int32)
            large = np.minimum(large, N_BUCKETS - 1)
            out[p, half] = np.where(valid, np.where(dist < 16, dist, large), -1)
    return out


def bias_tables(rel_bias):
    bkt = jnp.asarray(_bucket_tables())

    def body(rb_ref, bkt_ref, o_ref):
        h = pl.program_id(1)
        for half in range(2):
            bk = bkt_ref[half]
            acc = jnp.full((BLK, BLK), NEG, F32)
            for b in range(N_BUCKETS):
                acc = jnp.where(bk == b, rb_ref[h, b], acc)
            o_ref[half] = acc

    return pl.pallas_call(
        body, grid=(3, N_HEADS), name="bias_tables",
        in_specs=[pl.BlockSpec(memory_space=pltpu.SMEM),
                  pl.BlockSpec((None, 2, BLK, BLK), lambda p, h: (p, 0, 0, 0))],
        out_specs=pl.BlockSpec((None, None, 2, BLK, BLK), lambda p, h: (p, h, 0, 0, 0)),
        out_shape=SDS((3, N_HEADS, 2, BLK, BLK), F32), compiler_params=_cp("parallel", "parallel"))(rel_bias, bkt)


def bias_tables_bwd(ds_sum):
    bkt = jnp.asarray(_bucket_tables())

    def body(ds_ref, bkt_ref, o_ref):
        h = pl.program_id(0)

        @pl.when(h == 0)
        def _():
            o_ref[...] = jnp.zeros_like(o_ref)

        rows = lax.broadcasted_iota(jnp.int32, (N_HEADS, N_BUCKETS), 0)
        cols = lax.broadcasted_iota(jnp.int32, (N_HEADS, N_BUCKETS), 1)
        acc = jnp.zeros((N_HEADS, N_BUCKETS), F32)
        for b in range(N_BUCKETS):
            tot = jnp.zeros((1, 1), F32)
            for p in range(3):
                for half in range(2):
                    t = jnp.where(bkt_ref[p, half] == b, ds_ref[p, half], 0.0)
                    tot = tot + jnp.sum(_rsum(t), axis=0, keepdims=True)
            acc = acc + jnp.where((rows == h) & (cols == b), tot, 0.0)
        o_ref[...] += acc

    return pl.pallas_call(
        body, grid=(N_HEADS,), name="bias_tables_bwd",
        in_specs=[pl.BlockSpec((3, None, 2, BLK, BLK), lambda h: (0, h, 0, 0, 0)),
                  pl.BlockSpec((3, 2, BLK, BLK), lambda h: (0, 0, 0, 0))],
        out_specs=pl.BlockSpec((N_HEADS, N_BUCKETS), lambda h: (0, 0)),
        out_shape=SDS((N_HEADS, N_BUCKETS), F32), compiler_params=_cp("arbitrary"))(ds_sum, bkt)


def _rows(start, dil):
    if dil == 1:
        return pl.ds(pl.multiple_of(start, BLK), BLK)
    return pl.ds(start, BLK, stride=dil)


GRP = 8


def _group_rows(i, dil):
    t = lax.broadcasted_iota(jnp.int32, (GRP, 1, 1), 0)
    n0 = i * GRP
    b0, r0 = n0 // dil, n0 % dil
    cur, prv = [], []
    for j in range(GRP):
        b, r = (b0 + j // dil, j % dil) if GRP > dil else (b0, r0 + j)
        cur.append(b * (BLK * dil) + r)
        prv.append(jnp.maximum(b - 1, 0) * (BLK * dil) + r)
    pen = jnp.where(b0 + (r0 + t) // dil > 0, 0.0, NEG)
    return [_rows(s, dil) for s in cur], [_rows(s, dil) for s in prv], pen


def _load_group(ref, rows):
    return jnp.stack([ref[r, :] for r in rows])


def _bdot_nt(a, b):
    return lax.dot_general(a, b, (((2,), (2,)), ((0,), (0,))), preferred_element_type=F32)


def _bdot(a, b):
    return lax.dot_general(a, b, (((2,), (1,)), ((0,), (0,))), preferred_element_type=F32)


def _bdot_tn(a, b):
    return lax.dot_general(a, b, (((1,), (1,)), ((0,), (0,))), preferred_element_type=F32)


def _lsum(x):
    return jnp.sum(x, axis=-1, keepdims=True)


def _widen(src, dst):
    S = src.shape[1]

    def chunk(i, carry):
        rows = pl.ds(pl.multiple_of(i * 512, 512), 512)
        for a in range(3):
            dst[a, rows, :] = src[a, rows, :].astype(F32)
        return carry

    lax.fori_loop(0, S // 512, chunk, 0)


def attn_fwd(qkv, bias, job=None):
    S = qkv.shape[3]
    nblk = S // BLK
    car = Carried(job, 2, 2, 1, 4)

    def body(*refs):
        (qkv_in, bias_ref, o_ref, lse_ref, qkv_ref), jrefs = car.split(refs)
        car.phase(0, pl.program_id(0), jrefs)
        _widen(qkv_in, qkv_ref)
        lane = lax.broadcasted_iota(jnp.int32, (GRP, BLK, BLK), 2)
        lo = lane < 64
        masks = (lo, jnp.logical_not(lo))
        q_ref, k_ref, v_ref = qkv_ref.at[0], qkv_ref.at[1], qkv_ref.at[2]
        for p, (_, dil) in enumerate(PATTERNS):
            def step(i, carry, p=p, dil=dil):
                rc, rp, pen = _group_rows(i, dil)
                q2 = _load_group(q_ref, rc) * 0.125
                kc = _load_group(k_ref, rc).astype(BF16)
                kp = _load_group(k_ref, rp).astype(BF16)
                vc = _load_group(v_ref, rc)
                vp = _load_group(v_ref, rp)
                if p > 0:
                    acc_old, ml_old = _load_group(o_ref, rc), _load_group(lse_ref, rc)
                pv = jnp.zeros((GRP, BLK, BLK), F32)
                m_new, l_new, alpha = [], [], []
                for h in range(2):
                    qh = jnp.where(masks[h], q2, 0.0).astype(BF16)
                    sp = _bdot_nt(qh, kp) + bias_ref[p, h, 0][None] + pen
                    sc = _bdot_nt(qh, kc) + bias_ref[p, h, 1][None]
                    mn = jnp.max(jnp.maximum(sp, sc), axis=-1, keepdims=True)
                    if p > 0:
                        mo = ml_old[:, :, 64 * h:64 * h + 1]
                        mn = jnp.maximum(mn, mo)
                        al = jnp.exp(mo - mn)
                        alpha.append(al)
                    pp = jnp.exp(sp - mn)
                    pc = jnp.exp(sc - mn)
                    ln = _lsum(pp + pc)
                    if p > 0:
                        ln = ln + al * ml_old[:, :, 64 * h + 32:64 * h + 33]
                    vhp = jnp.where(masks[h], vp, 0.0).astype(BF16)
                    vhc = jnp.where(masks[h], vc, 0.0).astype(BF16)
                    pv = pv + _bdot(pp.astype(BF16), vhp) + _bdot(pc.astype(BF16), vhc)
                    m_new.append(mn)
                    l_new.append(ln)
                if p > 0:
                    pv = pv + acc_old * jnp.where(lo, alpha[0], alpha[1])
                ml_t = jnp.where(lo, jnp.where(lane < 32, m_new[0], l_new[0]), jnp.where(lane < 96, m_new[1], l_new[1]))
                for j, r in enumerate(rc):
                    o_ref[r, :] = pv[j]
                    lse_ref[r, :] = ml_t[j]
                return carry

            lax.fori_loop(0, nblk // GRP, step, 0)

        def fin(i, carry):
            rows = pl.ds(pl.multiple_of(i * 512, 512), 512)
            ml = lse_ref[rows, :]
            lo2 = lax.broadcasted_iota(jnp.int32, ml.shape, 1) < 64
            o_ref[rows, :] = o_ref[rows, :] / jnp.where(lo2, ml[:, 32:33], ml[:, 96:97])
            lse_ref[rows, :] = jnp.where(lo2, ml[:, 0:1] + jnp.log(ml[:, 32:33]), ml[:, 64:65] + jnp.log(ml[:, 96:97]))
            return carry

        lax.fori_loop(0, S // 512, fin, 0)
        car.phase(1, pl.program_id(0), jrefs)
        car.phase(2, pl.program_id(0), jrefs)

    outs = pl.pallas_call(
        body, grid=(4,), name="attn_fwd",
        in_specs=[pl.BlockSpec((None, 3, None, S, BLK), lambda g: (0, 0, g, 0, 0)),
                  pl.BlockSpec((3, 2, 2, BLK, BLK), lambda g: (0, g, 0, 0, 0))] + car.in_specs(),
        out_specs=[pl.BlockSpec((None, S, BLK), lambda g: (g, 0, 0)),
                   pl.BlockSpec((None, S, BLK), lambda g: (g, 0, 0))] + car.out_specs(),
        out_shape=[SDS((4, S, BLK), F32), SDS((4, S, BLK), F32)] + car.out_shapes(),
        input_output_aliases=car.aliases(),
        scratch_shapes=[pltpu.VMEM((3, S, BLK), F32)] + car.sems(),
        compiler_params=_cp("arbitrary"))(qkv, bias, *car.args())
    if job is not None:
        job.landed = list(outs[2:])
    return outs[0], outs[1]


def attn_bwd(qkv, bias, o, lse, do, job=None):
    S = qkv.shape[3]
    nblk = S // BLK
    car = Carried(job, 5, 2, 2, 4)

    def body(*refs):
        (qkv_in, bias_ref, o_ref, lse_ref, do_ref, dqkv_out, ds_ref, qkv_ref, dqkv_ref), jrefs = car.split(refs)
        car.phase(0, pl.program_id(0), jrefs)
        _widen(qkv_in, qkv_ref)
        lane = lax.broadcasted_iota(jnp.int32, (GRP, BLK, BLK), 2)
        lo = lane < 64
        masks = (lo, jnp.logical_not(lo))
        dqkv_ref[...] = jnp.zeros_like(dqkv_ref)
        ds_ref[...] = jnp.zeros_like(ds_ref)
        q_ref, k_ref, v_ref = qkv_ref.at[0], qkv_ref.at[1], qkv_ref.at[2]
        dq_ref, dk_ref, dv_ref = dqkv_ref.at[0], dqkv_ref.at[1], dqkv_ref.at[2]
        for p, (_, dil) in enumerate(PATTERNS):
            def step(i, carry, p=p, dil=dil):
                rc, rp, pen = _group_rows(i, dil)
                q2 = _load_group(q_ref, rc) * 0.125
                kc = _load_group(k_ref, rc).astype(BF16)
                kp = _load_group(k_ref, rp).astype(BF16)
                vc = _load_group(v_ref, rc).astype(BF16)
                vp = _load_group(v_ref, rp).astype(BF16)
                dot = _load_group(do_ref, rc)
                lset = _load_group(lse_ref, rc)
                prod = dot * _load_group(o_ref, rc)
                zero = jnp.zeros((GRP, BLK, BLK), F32)
                dq, dkc, dkp, dvc, dvp = zero, zero, zero, zero, zero
                for h in range(2):
                    qh = jnp.where(masks[h], q2, 0.0).astype(BF16)
                    doh = jnp.where(masks[h], dot, 0.0).astype(BF16)
                    delta = _lsum(jnp.where(masks[h], prod, 0.0))
                    lh = lset[:, :, 64 * h:64 * h + 1]
                    pp = jnp.exp(_bdot_nt(qh, kp) + bias_ref[p, h, 0][None] + pen - lh)
                    pc = jnp.exp(_bdot_nt(qh, kc) + bias_ref[p, h, 1][None] - lh)
                    dsp = pp * (_bdot_nt(doh, vp) - delta)
                    dsc = pc * (_bdot_nt(doh, vc) - delta)
                    ds_ref[p, h, 0] += jnp.sum(dsp, axis=0)
                    ds_ref[p, h, 1] += jnp.sum(dsc, axis=0)
                    dspb, dscb = dsp.astype(BF16), dsc.astype(BF16)
                    dq = dq + jnp.where(masks[h], _bdot(dspb, kp) + _bdot(dscb, kc), 0.0)
                    dkp = dkp + _bdot_tn(dspb, qh)
                    dkc = dkc + _bdot_tn(dscb, qh)
                    dvp = dvp + _bdot_tn(pp.astype(BF16), doh)
                    dvc = dvc + _bdot_tn(pc.astype(BF16), doh)
                for j in range(GRP):
                    dq_ref[rc[j], :] += dq[j] * 0.125
                    dk_ref[rp[j], :] += dkp[j]
                    dk_ref[rc[j], :] += dkc[j]
                    dv_ref[rp[j], :] += dvp[j]
                    dv_ref[rc[j], :] += dvc[j]
                return carry

            lax.fori_loop(0, nblk // GRP, step, 0)

        def narrow(i, carry):
            rows = pl.ds(pl.multiple_of(i * 512, 512), 512)
            for a in range(3):
                dqkv_out[a, rows, :] = dqkv_ref[a, rows, :].astype(BF16)
            return carry

        lax.fori_loop(0, S // 512, narrow, 0)
        car.phase(1, pl.program_id(0), jrefs)
        car.phase(2, pl.program_id(0), jrefs)

    blk = pl.BlockSpec((None, S, BLK), lambda g: (g, 0, 0))
    outs = pl.pallas_call(
        body, grid=(4,), name="attn_bwd",
        in_specs=[pl.BlockSpec((None, 3, None, S, BLK), lambda g: (0, 0, g, 0, 0)),
                  pl.BlockSpec((3, 2, 2, BLK, BLK), lambda g: (0, g, 0, 0, 0)), blk, blk, blk] + car.in_specs(),
        out_specs=[pl.BlockSpec((None, 3, None, S, BLK), lambda g: (0, 0, g, 0, 0)),
                   pl.BlockSpec((3, 2, 2, BLK, BLK), lambda g: (0, g, 0, 0, 0))] + car.out_specs(),
        out_shape=[SDS((2, 3, 4, S, BLK), BF16), SDS((3, N_HEADS, 2, BLK, BLK), F32)] + car.out_shapes(),
        input_output_aliases=car.aliases(),
        scratch_shapes=[pltpu.VMEM((3, S, BLK), F32), pltpu.VMEM((3, S, BLK), F32)] + car.sems(),
        compiler_params=_cp("arbitrary"))(qkv, bias, o, lse, do, *car.args())
    if job is not None:
        job.landed = list(outs[2:])
    return outs[0], outs[1]


def _shift_down(u, k, halo):
    n = u.shape[0]
    row = lax.broadcasted_iota(jnp.int32, u.shape, 0)
    out = pltpu.roll(u, k, 0)
    hn = halo.shape[0]
    for j in range(k):
        out = jnp.where(row == j, halo[hn - k + j:hn - k + j + 1, :], out)
    return out


def _shift_up(u, k, halo):
    n = u.shape[0]
    row = lax.broadcasted_iota(jnp.int32, u.shape, 0)
    out = pltpu.roll(u, n - k, 0)
    for j in range(k):
        out = jnp.where(row == n - k + j, halo[j:j + 1, :], out)
    return out


def _conv3(u, halo, w0, w1, w2):
    return _shift_down(u, 2, halo) * w0 + _shift_down(u, 1, halo) * w1 + u * w2


def _colsum(x):
    return jnp.sum(x, axis=0, keepdims=True)


def _mixer_specs(S, tm):
    conv = pl.BlockSpec((None, 12, tm, BLK), lambda i: (1, 0, i, 0))
    halo = pl.BlockSpec((None, 12, 16, BLK), lambda i: (1, 0, jnp.maximum(i * (tm // 16) - 1, 0), 0))
    ob = pl.BlockSpec((4, tm, BLK), lambda i: (0, i, 0))
    return conv, halo, ob


def _mixer_recompute(i, o_ref, pr_ref, ph_ref, w_ref):
    ob = [o_ref[q] for q in range(4)]
    gb = [pr_ref[q].astype(F32) for q in range(4)]
    gc = [pr_ref[4 + q].astype(F32) for q in range(4)]
    xi = [pr_ref[8 + q].astype(F32) for q in range(4)]
    keep = jnp.where(i > 0, 1.0, 0.0)
    u = [gc[q] * xi[q] for q in range(4)]
    hu = [ph_ref[4 + q].astype(F32) * ph_ref[8 + q].astype(F32) * keep for q in range(4)]
    w = [[w_ref[k:k + 1, q * BLK:(q + 1) * BLK] for k in range(3)] for q in range(4)]
    cv = [_conv3(u[q], hu[q], *w[q]) for q in range(4)]
    return ob, gb, gc, xi, u, hu, cv, w


def _rms_blocks(blocks):
    ss = sum(_rsum(b * b) for b in blocks)
    return lax.rsqrt(ss / (BLK * len(blocks)) + EPS)


def mixer_post_fwd(o, proj, w_sc, g_a, g_c):
    S = o.shape[1]
    tm = 512

    def body(o_ref, pr_ref, ph_ref, w_ref, ga_ref, gc_ref, m_ref):
        i = pl.program_id(0)
        ob, gb, _, _, _, _, cv, _ = _mixer_recompute(i, o_ref, pr_ref, ph_ref, w_ref)
        conv = [gb[q] * cv[q] for q in range(4)]
        ra, rc = _rms_blocks(ob), _rms_blocks(conv)
        for q in range(4):
            sl = slice(q * BLK, (q + 1) * BLK)
            m_ref[:, q * BLK:(q + 1) * BLK] = (ob[q] * ra * ga_ref[:, sl]).astype(BF16)
            m_ref[:, ATTN_W + q * BLK:ATTN_W + (q + 1) * BLK] = (conv[q] * rc * gc_ref[:, sl]).astype(BF16)

    conv_s, halo_s, ob_s = _mixer_specs(S, tm)
    full = lambda r, c: pl.BlockSpec((r, c), lambda i: (0, 0))
    return pl.pallas_call(
        body, grid=(S // tm,), name="mixer_post_fwd",
        in_specs=[ob_s, conv_s, halo_s, full(3, 512), full(1, 512), full(1, 512)],
        out_specs=pl.BlockSpec((tm, D_MODEL), lambda i: (i, 0)), out_shape=SDS((S, D_MODEL), BF16),
        compiler_params=_cp("parallel"))(o, proj, proj, w_sc, g_a, g_c)


def mixer_post_bwd_a(dmixed, o, proj, w_sc, g_a, g_c, job=None):
    S = o.shape[1]
    tm = 512

    def body(dm_ref, o_ref, pr_ref, ph_ref, w_ref, ga_ref, gc_ref, do_ref, dgb_ref, dcv_ref, dga_ref, dgc_ref):
        i = pl.program_id(0)
        ob, gb, _, _, _, _, cv, _ = _mixer_recompute(i, o_ref, pr_ref, ph_ref, w_ref)
        conv = [gb[q] * cv[q] for q in range(4)]
        ra, rc = _rms_blocks(ob), _rms_blocks(conv)

        @pl.when(i == 0)
        def _():
            dga_ref[...] = jnp.zeros_like(dga_ref)
            dgc_ref[...] = jnp.zeros_like(dgc_ref)

        for blocks, r, g_ref, off, dg_ref, is_attn in ((ob, ra, ga_ref, 0, dga_ref, True), (conv, rc, gc_ref, ATTN_W, dgc_ref, False)):
            xh = [blocks[q] * r for q in range(4)]
            dm = [dm_ref[:, off + q * BLK:off + (q + 1) * BLK].astype(F32) for q in range(4)]
            gd = [dm[q] * g_ref[:, q * BLK:(q + 1) * BLK] for q in range(4)]
            mean = sum(_rsum(gd[q] * xh[q]) for q in range(4)) / (4 * BLK)
            for q in range(4):
                dg_ref[:, q * BLK:(q + 1) * BLK] += _colsum(dm[q] * xh[q])
                dx = r * (gd[q] - xh[q] * mean)
                if is_attn:
                    do_ref[q] = dx
                else:
                    dgb_ref[q] = dx * cv[q]
                    dcv_ref[q] = dx * gb[q]

    conv_s, halo_s, ob_s = _mixer_specs(S, tm)
    full = lambda r, c: pl.BlockSpec((r, c), lambda i: (0, 0))
    return _pcall(
        body, [dmixed, o, proj, proj, w_sc, g_a, g_c], grid=(S // tm,), name="mixer_post_bwd_a",
        in_specs=[pl.BlockSpec((tm, D_MODEL), lambda i: (i, 0)), ob_s, conv_s, halo_s, full(3, 512), full(1, 512), full(1, 512)],
        out_specs=[ob_s, ob_s, ob_s, full(1, 512), full(1, 512)],
        out_shape=[SDS((4, S, BLK), F32)] * 3 + [SDS((1, 512), F32)] * 2, sem=("arbitrary",), job=job)


def mixer_post_bwd_b(dproj, dgb, dcv, proj, w_sc):
    S = proj.shape[2]
    tm = 512
    last = S // 8 - 1

    def body(dp_in, dgb_ref, dcv_ref, dn_ref, pr_ref, w_ref, dp_ref, dw_ref):
        i = pl.program_id(0)
        keep_next = jnp.where(i < pl.num_programs(0) - 1, 1.0, 0.0)

        @pl.when(i == 0)
        def _():
            dw_ref[...] = jnp.zeros_like(dw_ref)

        for q in range(4):
            sl = slice(q * BLK, (q + 1) * BLK)
            gc, xi = pr_ref[4 + q].astype(F32), pr_ref[8 + q].astype(F32)
            u = gc * xi
            dcv = dcv_ref[q]
            dn = dn_ref[q] * keep_next
            d1, d2 = _shift_up(dcv, 1, dn), _shift_up(dcv, 2, dn)
            w0, w1, w2 = (w_ref[k:k + 1, sl] for k in range(3))
            du = dcv * w2 + d1 * w1 + d2 * w0
            dw_ref[0:1, sl] += _colsum(d2 * u)
            dw_ref[1:2, sl] += _colsum(d1 * u)
            dw_ref[2:3, sl] += _colsum(dcv * u)
            dp_ref[q] = dgb_ref[q].astype(BF16)
            dp_ref[4 + q] = (du * xi).astype(BF16)
            dp_ref[8 + q] = (du * gc).astype(BF16)

    conv_s, _, ob_s = _mixer_specs(S, tm)
    nxt = pl.BlockSpec((4, 8, BLK), lambda i: (0, jnp.minimum((i + 1) * (tm // 8), last), 0))
    full = lambda r, c: pl.BlockSpec((r, c), lambda i: (0, 0))
    return pl.pallas_call(
        body, grid=(S // tm,), name="mixer_post_bwd_b",
        in_specs=[pl.BlockSpec(memory_space=pl.ANY), ob_s, ob_s, nxt, conv_s, full(3, 512)],
        out_specs=[conv_s, full(3, 512)],
        out_shape=[SDS(dproj.shape, BF16), SDS((3, 512), F32)],
        input_output_aliases={0: 0}, compiler_params=_cp("arbitrary"))(dproj, dgb, dcv, dcv, proj, w_sc)


def xattn_fwd(q, k, v):
    S = q.shape[0]
    tm = 512
    scale = MEM_HD ** -0.5

    def body(q_ref, k_ref, v_ref, o_ref):
        for h in range(N_MEM_HEADS):
            sl = slice(h * MEM_HD, (h + 1) * MEM_HD)
            s = _dot_nt(q_ref[:, sl], k_ref[:, sl]) * scale
            p = jnp.exp(s - jnp.max(s, axis=1, keepdims=True))
            p = p / _rsum(p)
            o_ref[:, sl] = _dot(p.astype(BF16), v_ref[:, sl]).astype(BF16)

    row = pl.BlockSpec((tm, D_MODEL), lambda i: (i, 0))
    kv = pl.BlockSpec(k.shape, lambda i: (0, 0))
    return pl.pallas_call(body, grid=(S // tm,), name="xattn_fwd", in_specs=[row, kv, kv], out_specs=row,
                          out_shape=SDS((S, D_MODEL), BF16), compiler_params=_cp("parallel"))(q, k, v)


def xattn_bwd(q, k, v, do):
    S = q.shape[0]
    tm = 512
    scale = MEM_HD ** -0.5

    def body(q_ref, k_ref, v_ref, do_ref, dq_ref, dk_ref, dv_ref):
        @pl.when(pl.program_id(0) == 0)
        def _():
            dk_ref[...] = jnp.zeros_like(dk_ref)
            dv_ref[...] = jnp.zeros_like(dv_ref)

        for h in range(N_MEM_HEADS):
            sl = slice(h * MEM_HD, (h + 1) * MEM_HD)
            qh, kh, vh, doh = q_ref[:, sl], k_ref[:, sl], v_ref[:, sl], do_ref[:, sl]
            s = _dot_nt(qh, kh) * scale
            p = jnp.exp(s - jnp.max(s, axis=1, keepdims=True))
            p = p / _rsum(p)
            dp = _dot_nt(doh, vh)
            ds = (p * (dp - _rsum(p * dp)) * scale).astype(BF16)
            dq_ref[:, sl] = _dot(ds, kh).astype(BF16)
            dk_ref[:, sl] += _dot_tn(ds, qh)
            dv_ref[:, sl] += _dot_tn(p.astype(BF16), doh)

    row = pl.BlockSpec((tm, D_MODEL), lambda i: (i, 0))
    kv = pl.BlockSpec(k.shape, lambda i: (0, 0))
    return pl.pallas_call(body, grid=(S // tm,), name="xattn_bwd", in_specs=[row, kv, kv, row], out_specs=[row, kv, kv],
                          out_shape=[SDS((S, D_MODEL), BF16), SDS(k.shape, F32), SDS(k.shape, F32)],
                          compiler_params=_cp("arbitrary"))(q, k, v, do)


FFN_TM, FFN_TC = 256, 1408


def _ffn_specs(S, order):
    tm, tc = FFN_TM, FFN_TC
    ij = (lambda a, b: (a, b)) if order == "ij" else (lambda a, b: (b, a))
    blk = pl.BlockSpec((2, tm, tc), lambda a, b: (0,) + ij(a, b))
    prev = pl.BlockSpec((2, 16, tc), lambda a, b: (0, jnp.maximum(ij(a, b)[0] * (tm // 16) - 1, 0), ij(a, b)[1]))
    nxt = pl.BlockSpec((2, 16, tc), lambda a, b: (0, jnp.minimum((ij(a, b)[0] + 1) * (tm // 16), S // 16 - 1), ij(a, b)[1]))
    wsp = pl.BlockSpec((2, 3, tc), lambda a, b: (0, 0, ij(a, b)[1]))
    bsp = pl.BlockSpec((2, 1, tc), lambda a, b: (0, 0, ij(a, b)[1]))
    act = pl.BlockSpec((tm, tc), lambda a, b: ij(a, b))
    return blk, prev, nxt, wsp, bsp, act


def _ffn_up(i, up_ref, uh_ref, w_ref, b_ref):
    keep = jnp.where(i > 0, 1.0, 0.0)
    out = []
    for half in range(2):
        u = up_ref[half].astype(F32)
        hu = uh_ref[half].astype(F32) * keep
        w0, w1, w2 = (w_ref[half, k:k + 1, :] for k in range(3))
        out.append(_conv3(u, hu, w0, w1, w2) + b_ref[half])
    return out


def ffn_act_fwd(up_pre, w_fc, b_fc):
    S = up_pre.shape[1]

    def body(up_ref, uh_ref, w_ref, b_ref, a_ref, gv_ref):
        g, v = _ffn_up(pl.program_id(0), up_ref, uh_ref, w_ref, b_ref)
        a_ref[...] = (g * jax.nn.sigmoid(g) * v).astype(BF16)
        gv_ref[0] = g.astype(BF16)
        gv_ref[1] = v.astype(BF16)

    blk, prev, _, wsp, bsp, act = _ffn_specs(S, "ij")
    return pl.pallas_call(body, grid=(S // FFN_TM, D_FF // FFN_TC), name="ffn_act_fwd",
                          in_specs=[blk, prev, wsp, bsp], out_specs=[act, blk],
                          out_shape=[SDS((S, D_FF), BF16), SDS((2, S, D_FF), BF16)],
                          compiler_params=_cp("parallel", "parallel"))(up_pre, up_pre, w_fc, b_fc)


def ffn_act_bwd_a(dact, up, job=None):
    S = up.shape[1]

    def body(da_ref, up_ref, du_ref, db_ref):
        i = pl.program_id(1)
        g, v = up_ref[0].astype(F32), up_ref[1].astype(F32)
        da = da_ref[...].astype(F32)
        sg = jax.nn.sigmoid(g)
        dg = da * v * (sg * (1.0 + g * (1.0 - sg)))
        dv = da * g * sg

        @pl.when(i == 0)
        def _():
            db_ref[...] = jnp.zeros_like(db_ref)

        du_ref[0] = dg.astype(BF16)
        du_ref[1] = dv.astype(BF16)
        db_ref[0] += _colsum(dg)
        db_ref[1] += _colsum(dv)

    blk, _, _, _, bsp, act = _ffn_specs(S, "ji")
    return _pcall(body, [dact, up], grid=(D_FF // FFN_TC, S // FFN_TM), name="ffn_act_bwd_a",
                  in_specs=[act, blk], out_specs=[blk, bsp],
                  out_shape=[SDS((2, S, D_FF), BF16), SDS((2, 1, D_FF), F32)], sem=("parallel", "arbitrary"), job=job)


def ffn_act_bwd_b(d_up, up_pre, w_fc):
    S = up_pre.shape[1]

    def body(du_ref, dn_ref, up_ref, w_ref, dp_ref, dw_ref):
        i = pl.program_id(1)
        keep_next = jnp.where(i < pl.num_programs(1) - 1, 1.0, 0.0)

        @pl.when(i == 0)
        def _():
            dw_ref[...] = jnp.zeros_like(dw_ref)

        for half in range(2):
            d = du_ref[half].astype(F32)
            dn = dn_ref[half].astype(F32) * keep_next
            u = up_ref[half].astype(F32)
            d1, d2 = _shift_up(d, 1, dn), _shift_up(d, 2, dn)
            w0, w1, w2 = (w_ref[half, k:k + 1, :] for k in range(3))
            dp_ref[half] = (d * w2 + d1 * w1 + d2 * w0).astype(BF16)
            dw_ref[half, 0:1, :] += _colsum(d2 * u)
            dw_ref[half, 1:2, :] += _colsum(d1 * u)
            dw_ref[half, 2:3, :] += _colsum(d * u)

    blk, _, nxt, wsp, _, _ = _ffn_specs(S, "ji")
    return pl.pallas_call(body, grid=(D_FF // FFN_TC, S // FFN_TM), name="ffn_act_bwd_b",
                          in_specs=[blk, nxt, blk, wsp], out_specs=[blk, wsp],
                          out_shape=[SDS((2, S, D_FF), BF16), SDS((2, 3, D_FF), F32)],
                          compiler_params=_cp("parallel", "arbitrary"))(d_up, d_up, up_pre, w_fc)


def final_loss(x3, g, target):
    S, Dm = x3.shape
    tm = 512

    def body(x_ref, g_ref, t_ref, loss_ref, dx_ref, dg_ref):
        i = pl.program_id(0)
        xv = x_ref[...]
        r = lax.rsqrt(jnp.mean(xv * xv, axis=1, keepdims=True) + EPS)
        xh = xv * r
        err = xh * g_ref[...] - t_ref[...]
        dy = err / Dm
        gd = dy * g_ref[...]
        dx_ref[...] = r * (gd - xh * jnp.mean(gd * xh, axis=1, keepdims=True))

        @pl.when(i == 0)
        def _():
            dg_ref[...] = jnp.zeros_like(dg_ref)
            loss_ref[...] = jnp.zeros_like(loss_ref)

        dg_ref[...] += _colsum(dy * xh)
        loss_ref[...] += 0.5 * _colsum(jnp.mean(err * err, axis=1, keepdims=True))

    row = pl.BlockSpec((tm, Dm), lambda i: (i, 0))
    vec = pl.BlockSpec((1, Dm), lambda i: (0, 0))
    one = pl.BlockSpec((1, 1), lambda i: (0, 0))
    return pl.pallas_call(body, grid=(S // tm,), name="final_loss", in_specs=[row, vec, row], out_specs=[one, row, vec],
                          out_shape=[SDS((1, 1), F32), SDS((S, Dm), F32), SDS((1, Dm), F32)],
                          compiler_params=_cp("arbitrary"))(x3, g, target)


def local_step(x, mem, target, wb, ws, comm=None):
    S = x.shape[0]
    assert S % 2048 == 0
    if comm is None:
        comm = NoComm()
    else:
        wb = comm.gather_first()
    w_fc = ws["w_ffn_conv"].reshape(3, 2, D_FF).transpose(1, 0, 2)
    b_fc = ws["b_ffn_conv"].reshape(2, 1, D_FF)

    bias = bias_tables(ws["rel_bias"])
    h1 = rmsnorm_fwd(x, ws["g_mix"], "norm_mix")
    proj = mm_nn(h1, wb["w_in"], "proj", out_dtype=BF16, out_bw=BLK, tn=768, job=comm.job("proj"))
    comm.landed("proj", wb)
    qkv = proj.reshape(2, 3, 4, S, BLK)
    o, lse = attn_fwd(qkv, bias, job=comm.job("attn_fwd"))
    comm.landed("attn_fwd", wb)
    proj4 = proj.reshape(2, 12, S, BLK)
    mixed = mixer_post_fwd(o, proj4, ws["w_short_conv"], ws["g_attn_out"], ws["g_conv_out"])
    x1 = mm_nn(mixed, wb["w_out"], "out_proj", res=x)
    h2 = rmsnorm_fwd(x1, ws["g_xattn"], "norm_xattn")
    mem_n = rmsnorm_fwd(mem, ws["g_mem"], "norm_mem")
    xq = mm_nn(h2, wb["w_xq"], "xq", out_dtype=BF16)
    xk = mm_nn(mem_n, wb["w_xk"], "xk", out_dtype=BF16, tn=1024)
    xv = mm_nn(mem_n, wb["w_xv"], "xv", out_dtype=BF16, tn=1024)
    xo = xattn_fwd(xq, xk, xv)
    x2 = mm_nn(xo, wb["w_xo"], "xo_proj", res=x1)
    h3 = rmsnorm_fwd(x2, ws["g_ffn"], "norm_ffn")
    up_pre = mm_nn(h3, wb["w_up"], "up_proj", out_dtype=BF16, out_bw=D_FF, tn=1408)
    act, up = ffn_act_fwd(up_pre, w_fc, b_fc)
    x3 = mm_nn(act, wb["w_down"], "down_proj", res=x2)
    loss, dx3, dg_final = final_loss(x3, ws["g_final"].reshape(1, -1), target)

    gb, gs = {}, {"g_final": dg_final}
    gb["w_down"] = mm_tn(act, dx3, None, "dw_down", tm=1408, tn=1024)
    dact = mm_nt(dx3, None, wb["w_down"], "d_act", out_dtype=BF16, tn=1408)
    d_up, db_fc = ffn_act_bwd_a(dact, up)
    d_up_pre, dw_fc = ffn_act_bwd_b(d_up, up_pre, w_fc)
    gs["b_ffn_conv"] = db_fc.reshape(1, 2 * D_FF)
    gs["w_ffn_conv"] = dw_fc.transpose(1, 0, 2).reshape(3, 2 * D_FF)
    gb["w_up"] = mm_tn(h3, d_up_pre, D_FF, "dw_up", shards=4, tn=1408)
    dx2, gs["g_ffn"] = mm_nt_norm_bwd(d_up_pre, D_FF, wb["w_up"], x2, ws["g_ffn"], dx3, "d_h3", tm=256,
                                      job=comm.job("d_h3", gb))
    comm.landed("d_h3")
    gb["w_xo"] = mm_tn(xo, dx2, None, "dw_xo")
    dxo = mm_nt(dx2, None, wb["w_xo"], "d_xo", out_dtype=BF16)
    dxq, dxk, dxv = xattn_bwd(xq, xk, xv, dxo)
    gb["w_xq"] = mm_tn(h2, dxq, None, "dw_xq")
    gb["w_xk"] = mm_tn(mem_n, dxk, None, "dw_xk", tn=1024)
    gb["w_xv"] = mm_tn(mem_n, dxv, None, "dw_xv", tn=1024)
    dmem_n = mm_nt(dxk, None, wb["w_xk"], "d_memk", tn=1024)
    dmem_n = mm_nt(dxv, None, wb["w_xv"], "d_memv", res=dmem_n, tn=1024)
    gs["g_mem"] = rmsnorm_bwd(mem, ws["g_mem"], dmem_n, None, "norm_mem_bwd")
    dx1, gs["g_xattn"] = mm_nt_norm_bwd(dxq, None, wb["w_xq"], x1, ws["g_xattn"], dx2, "d_h2")
    gb["w_out"] = mm_tn(mixed, dx1, None, "dw_out")
    dmixed = mm_nt(dx1, None, wb["w_out"], "d_mixed")
    do, dgb, dcv, gs["g_attn_out"], gs["g_conv_out"] = mixer_post_bwd_a(
        dmixed, o, proj4, ws["w_short_conv"], ws["g_attn_out"], ws["g_conv_out"], job=comm.job("mixer_post_bwd_a", gb))
    comm.landed("mixer_post_bwd_a")
    dproj, ds_sum = attn_bwd(qkv, bias, o, lse, do, job=comm.job("attn_bwd"))
    comm.landed("attn_bwd")
    gs["rel_bias"] = bias_tables_bwd(ds_sum)
    dproj, gs["w_short_conv"] = mixer_post_bwd_b(dproj.reshape(2, 12, S, BLK), dgb, dcv, proj4, ws["w_short_conv"])
    dproj = dproj.reshape(24, S, BLK)
    gb["w_in"] = mm_tn(h1, dproj, BLK, "dw_in", shards=4, tn=768, job=comm.job("dw_in"))
    comm.landed("dw_in")
    grad_x, gs["g_mix"] = mm_nt_norm_bwd(dproj, BLK, wb["w_in"], x, ws["g_mix"], dx1, "d_h1", tm=256,
                                         job=comm.job("d_h1", gb))
    comm.landed("d_h1")
    return loss, grad_x, comm.finish(gb), gs


def adamw(w, g, m, v, name, job=None):
    R, C = w.shape
    tr = R
    for cand in (256, 352):
        if R > cand and R % cand == 0:
            tr = cand
            break

    def body(w_ref, g_ref, m_ref, v_ref, d_ref, nm_ref, nv_ref):
        gv = g_ref[...]
        mn = ADAM_B1 * m_ref[...] + (1.0 - ADAM_B1) * gv
        vn = ADAM_B2 * v_ref[...] + (1.0 - ADAM_B2) * (gv * gv)
        m_hat = mn / (1.0 - ADAM_B1 ** ADAM_STEP)
        v_hat = vn / (1.0 - ADAM_B2 ** ADAM_STEP)
        d_ref[...] = -ADAM_LR * (m_hat / (jnp.sqrt(v_hat) + ADAM_EPS) + ADAM_WD * w_ref[...])
        nm_ref[...] = mn
        nv_ref[...] = vn

    blk = pl.BlockSpec((tr, C), lambda i: (i, 0))
    return _pcall(body, [w, g, m, v], grid=(R // tr,), name=name, in_specs=[blk] * 4, out_specs=[blk] * 3,
                  out_shape=[SDS((R, C), F32)] * 3, sem=("parallel",), job=job)


BIG = ("w_in", "w_out", "w_xq", "w_xk", "w_xv", "w_xo", "w_up", "w_down")
COL_SHARDED = ("w_in", "w_up")
N_BIG = len(BIG)
ANY = pl.BlockSpec(memory_space=pl.ANY)


def _place():
    x, y, c = lax.axis_index("x"), lax.axis_index("y"), lax.axis_index("c")
    chips = [(1 - x, y), (x, 1 - y), (1 - x, 1 - y)]
    return x, y, c, chips


def _window(full, name, R, C, shard, half):
    r0, nr = (0, R) if half is None else (half * (R // 2), R // 2)
    if name in COL_SHARDED:
        return full.at[pl.ds(r0, nr), pl.ds(shard * C, C)]
    return full.at[pl.ds(shard * R + r0, nr), :]


def place_shard(w, mine_arr, n):
    R, C = w.shape
    col = n in COL_SHARDED

    def body(s_ref, w_ref, o_ref):
        o_ref[...] = w_ref[...].astype(BF16)

    grid_spec = pltpu.PrefetchScalarGridSpec(
        num_scalar_prefetch=1, grid=(1,), in_specs=[pl.BlockSpec((R, C), lambda i, s: (0, 0))],
        out_specs=pl.BlockSpec((R, C), (lambda i, s: (0, s[0])) if col else (lambda i, s: (s[0], 0))))
    return pl.pallas_call(body, grid_spec=grid_spec, name="place_" + n,
                          out_shape=SDS((R, 4 * C) if col else (4 * R, C), BF16),
                          compiler_params=_cp("arbitrary"))(mine_arr, w)


def _gather_jobs(names, shapes):
    nw = len(names)

    def start(full, sems):
        send, recv, fsend, frecv = sems
        x, y, c, chips = _place()
        mine = 2 * x + y
        for w, n in enumerate(names):
            R, C = shapes[w]
            own = _window(full[w], n, R, C, mine, c)
            for j, chip in enumerate(chips):
                pltpu.make_async_remote_copy(src_ref=own, dst_ref=own, send_sem=send.at[w, j], recv_sem=recv.at[w, j],
                                             device_id=(*chip, c), device_id_type=MESH).start()

    def mid(full, sems):
        send, recv, fsend, frecv = sems
        x, y, c, chips = _place()
        sib = (x, y, 1 - c)
        for w, n in enumerate(names):
            R, C = shapes[w]
            for j, chip in enumerate(chips):
                landed = _window(full[w], n, R, C, 2 * chip[0] + chip[1], c)
                pltpu.make_async_remote_copy(src_ref=landed, dst_ref=landed, send_sem=send.at[w, j], recv_sem=recv.at[w, j],
                                             device_id=(*chip, c), device_id_type=MESH).wait_recv()
                pltpu.make_async_remote_copy(src_ref=landed, dst_ref=landed, send_sem=fsend.at[w, j],
                                             recv_sem=frecv.at[w, j], device_id=sib, device_id_type=MESH).start()

    def finish(full, sems):
        send, recv, fsend, frecv = sems
        x, y, c, chips = _place()
        mine = 2 * x + y
        sib = (x, y, 1 - c)
        for w, n in enumerate(names):
            R, C = shapes[w]
            own = _window(full[w], n, R, C, mine, c)
            for j, chip in enumerate(chips):
                landed = _window(full[w], n, R, C, 2 * chip[0] + chip[1], c)
                other = _window(full[w], n, R, C, 2 * chip[0] + chip[1], 1 - c)
                pltpu.make_async_remote_copy(src_ref=other, dst_ref=other, send_sem=fsend.at[w, j], recv_sem=frecv.at[w, j],
                                             device_id=sib, device_id_type=MESH).wait_recv()
                pltpu.make_async_remote_copy(src_ref=own, dst_ref=own, send_sem=send.at[w, j], recv_sem=recv.at[w, j],
                                             device_id=(*chip, c), device_id_type=MESH).wait_send()
                pltpu.make_async_remote_copy(src_ref=landed, dst_ref=landed, send_sem=fsend.at[w, j],
                                             recv_sem=frecv.at[w, j], device_id=sib, device_id_type=MESH).wait_send()

    return start, mid, finish, [pltpu.SemaphoreType.DMA((nw, 3))] * 4


class CommJob:
    def __init__(self, ins, out_shapes, inplace, start, finish, sems, mid=None):
        self.ins, self.out_shapes, self.inplace = list(ins), list(out_shapes), inplace
        self.start, self.mid, self.finish, self.sems = start, mid, finish, list(sems)

    def run(self, phase, ins, outs, sems):
        if phase == 0:
            self.start(ins, outs, sems)
        elif phase == 1:
            if self.mid is not None:
                self.mid(ins, outs, sems)
        else:
            self.finish(ins, outs, sems)


def run_job(name, job):
    n_in, n_out = len(job.ins), len(job.out_shapes)

    def body(*refs):
        ins, outs, sm = refs[:n_in], refs[n_in:n_in + n_out], refs[n_in + n_out:]
        for phase in range(3):
            job.run(phase, ins, outs, sm)

    job.landed = list(pl.pallas_call(
        body, name=name, in_specs=[ANY] * n_in, out_specs=[ANY] * n_out, out_shape=job.out_shapes,
        input_output_aliases={i: i for i in range(n_in)} if job.inplace else {},
        scratch_shapes=job.sems)(*job.ins))
    return job.landed


class Carried:
    def __init__(self, job, n_in, n_out, n_scratch, steps):
        self.job, self.n_in, self.n_out, self.n_scratch, self.steps = job, n_in, n_out, n_scratch, steps
        self.nji = len(job.ins) if job else 0
        self.njo = len(job.out_shapes) if job else 0

    def in_specs(self):
        return [ANY] * self.nji

    def out_specs(self):
        return [ANY] * self.njo

    def out_shapes(self):
        return list(self.job.out_shapes) if self.job else []

    def aliases(self):
        if not (self.job and self.job.inplace):
            return {}
        return {self.n_in + i: self.n_out + i for i in range(self.nji)}

    def sems(self):
        return list(self.job.sems) if self.job else []

    def args(self):
        return list(self.job.ins) if self.job else []

    def split(self, refs):
        a = self.n_in
        b = a + self.nji
        c = b + self.n_out
        d = c + self.njo
        e = d + self.n_scratch
        return refs[:a] + refs[b:c] + refs[d:e], (refs[a:b], refs[c:d], refs[e:])

    def phase(self, phase, step, jrefs):
        if self.job is None:
            return
        at = {0: 0, 1: max(self.steps - 2, 0), 2: self.steps - 1}[phase]

        @pl.when(step == at)
        def _():
            self.job.run(phase, *jrefs)


def _pcall(body, args, *, grid, in_specs, out_specs, out_shape, name, sem, scratch=(), aliases=None, job=None):
    n_in, n_out = len(args), len(out_shape)
    if job is None:
        return pl.pallas_call(
            body, grid=grid, in_specs=list(in_specs), out_specs=list(out_specs), out_shape=list(out_shape), name=name,
            scratch_shapes=list(scratch), input_output_aliases=dict(aliases or {}), compiler_params=_cp(*sem))(*args)
    total = int(np.prod(grid))
    car = Carried(job, n_in, n_out, len(scratch), total)

    def wrapped(*refs):
        main, jrefs = car.split(refs)
        lin = pl.program_id(0)
        for ax in range(1, len(grid)):
            lin = lin * grid[ax] + pl.program_id(ax)
        car.phase(0, lin, jrefs)
        body(*main)
        car.phase(1, lin, jrefs)
        car.phase(2, lin, jrefs)

    res = pl.pallas_call(
        wrapped, grid=grid, in_specs=list(in_specs) + car.in_specs(), out_specs=list(out_specs) + car.out_specs(),
        out_shape=list(out_shape) + car.out_shapes(), name=name, scratch_shapes=list(scratch) + car.sems(),
        input_output_aliases={**dict(aliases or {}), **car.aliases()},
        compiler_params=_cp(*(["arbitrary"] * len(grid))))(*args, *car.args())
    job.landed = list(res[n_out:])
    return list(res[:n_out])


def gather_job(placed, names):
    shapes = []
    for n in names:
        R, C = placed[n].shape
        shapes.append((R, C // 4) if n in COL_SHARDED else (R // 4, C))
    start, mid, finish, sems = _gather_jobs(names, shapes)
    arrays = [placed[n] for n in names]
    return CommJob(arrays, [SDS(a.shape, a.dtype) for a in arrays], True,
                   lambda i, o, s: start(o, s), lambda i, o, s: finish(o, s), sems, mid=lambda i, o, s: mid(o, s))


def pair_exchange_job(grads, names):
    shapes = [grads[n].shape for n in names]

    def copies(ins, outs, sems):
        x, y, c, _ = _place()
        return [pltpu.make_async_remote_copy(
            src_ref=ins[w].at[:, pl.ds((1 - c) * (shapes[w][1] // 2), shapes[w][1] // 2), :], dst_ref=outs[w],
            send_sem=sems[0].at[w], recv_sem=sems[1].at[w], device_id=(x, y, 1 - c), device_id_type=MESH)
            for w in range(len(names))]

    def start(ins, outs, sems):
        for cp in copies(ins, outs, sems):
            cp.start()

    def finish(ins, outs, sems):
        for cp in copies(ins, outs, sems):
            cp.wait()

    return CommJob([grads[n] for n in names], [SDS((4, s[1] // 2, s[2]), F32) for s in shapes], False, start, finish,
                   [pltpu.SemaphoreType.DMA((len(names),))] * 2)


def chip_exchange_job(p16, names):
    shapes = [p16[n].shape for n in names]

    def copies(ins, outs, sems):
        x, y, c, chips = _place()
        return [pltpu.make_async_remote_copy(
            src_ref=ins[w].at[2 * chip[0] + chip[1]], dst_ref=outs[w].at[j],
            send_sem=sems[0].at[w, j], recv_sem=sems[1].at[w, j], device_id=(*chip, c), device_id_type=MESH)
            for w in range(len(names)) for j, chip in enumerate(chips)]

    def start(ins, outs, sems):
        for cp in copies(ins, outs, sems):
            cp.start()

    def finish(ins, outs, sems):
        for cp in copies(ins, outs, sems):
            cp.wait()

    return CommJob([p16[n] for n in names], [SDS((3,) + tuple(s[1:]), BF16) for s in shapes], False, start, finish,
                   [pltpu.SemaphoreType.DMA((len(names), 3))] * 2)


def half_exchange_job(full, names):
    shapes = [full[n].shape for n in names]

    def copies(outs, sems):
        x, y, c, _ = _place()
        cps = []
        for w in range(len(names)):
            Rh = shapes[w][0] // 2
            rows = outs[w].at[pl.ds(c * Rh, Rh), :]
            other = outs[w].at[pl.ds((1 - c) * Rh, Rh), :]
            cps.append((pltpu.make_async_remote_copy(src_ref=rows, dst_ref=rows, send_sem=sems[0].at[w], recv_sem=sems[1].at[w],
                                                     device_id=(x, y, 1 - c), device_id_type=MESH),
                        pltpu.make_async_remote_copy(src_ref=other, dst_ref=other, send_sem=sems[0].at[w], recv_sem=sems[1].at[w],
                                                     device_id=(x, y, 1 - c), device_id_type=MESH)))
        return cps

    def start(ins, outs, sems):
        for send, _ in copies(outs, sems):
            send.start()

    def finish(ins, outs, sems):
        for send, recv in copies(outs, sems):
            recv.wait_recv()
            send.wait_send()

    arrays = [full[n] for n in names]
    return CommJob(arrays, [SDS(a.shape, a.dtype) for a in arrays], True, start, finish,
                   [pltpu.SemaphoreType.DMA((len(names),))] * 2)


def allreduce_small(parts, name):
    n = len(parts)

    def body(*refs):
        ins, outs, bufs = refs[:n], refs[n:2 * n], refs[2 * n:3 * n]
        send, recv = refs[3 * n:]
        x, y, c, _ = _place()
        me = 4 * x + 2 * y + c
        sends = []
        for p in range(n):
            for k in range(1, 8):
                peer = (x ^ (k >> 2), y ^ ((k >> 1) & 1), c ^ (k & 1))
                cp = pltpu.make_async_remote_copy(src_ref=ins[p], dst_ref=bufs[p].at[k], send_sem=send.at[p, k - 1],
                                                  recv_sem=recv.at[p, k - 1], device_id=peer, device_id_type=MESH)
                cp.start()
                sends.append(cp)
            bufs[p][0] = ins[p][...]
        for cp in sends:
            cp.wait_recv()
        for p in range(n):
            acc = bufs[p][me]
            for d in range(1, 8):
                acc = acc + bufs[p][d ^ me]
            outs[p][...] = acc
        for cp in sends:
            cp.wait_send()

    vm = pl.BlockSpec(memory_space=pltpu.VMEM)
    return pl.pallas_call(
        body, name=name, in_specs=[vm] * n, out_specs=[vm] * n, out_shape=[SDS(a.shape, F32) for a in parts],
        scratch_shapes=[pltpu.VMEM((8,) + tuple(a.shape), F32) for a in parts] + [pltpu.SemaphoreType.DMA((n, 7))] * 2,
    )(*parts)


def adamw_multi(ws, gs, ms, vs, name):
    n = len(ws)

    def body(*refs):
        for p in range(n):
            w_ref, g_ref, m_ref, v_ref = (refs[q * n + p] for q in range(4))
            d_ref, nm_ref, nv_ref = (refs[(4 + q) * n + p] for q in range(3))
            gv = g_ref[...]
            mn = ADAM_B1 * m_ref[...] + (1.0 - ADAM_B1) * gv
            vn = ADAM_B2 * v_ref[...] + (1.0 - ADAM_B2) * (gv * gv)
            m_hat = mn / (1.0 - ADAM_B1 ** ADAM_STEP)
            v_hat = vn / (1.0 - ADAM_B2 ** ADAM_STEP)
            d_ref[...] = -ADAM_LR * (m_hat / (jnp.sqrt(v_hat) + ADAM_EPS) + ADAM_WD * w_ref[...])
            nm_ref[...] = mn
            nv_ref[...] = vn

    vm = pl.BlockSpec(memory_space=pltpu.VMEM)
    outs = pl.pallas_call(body, name=name, in_specs=[vm] * (4 * n), out_specs=[vm] * (3 * n),
                          out_shape=[SDS(a.shape, F32) for a in ws] * 3)(*ws, *gs, *ms, *vs)
    return outs[:n], outs[n:2 * n], outs[2 * n:]


def pair_sum(g, xh, sel_arr, name):
    _, R, C = g.shape
    Rh = R // 2

    def body(sel_ref, g_ref, x_ref, p32_ref, p16_ref):
        s = g_ref[...] + x_ref[...]
        p16_ref[...] = s.astype(BF16)

        @pl.when(pl.program_id(0) == sel_ref[0])
        def _():
            p32_ref[...] = s

    same = pl.BlockSpec((None, Rh, C), lambda s, sel: (s, 0, 0))
    grid_spec = pltpu.PrefetchScalarGridSpec(
        num_scalar_prefetch=1, grid=(4,),
        in_specs=[pl.BlockSpec((None, Rh, C), lambda s, sel: (s, sel[1], 0)), same],
        out_specs=[pl.BlockSpec((Rh, C), lambda s, sel: (0, 0)), same])
    return pl.pallas_call(body, grid_spec=grid_spec, name=name, out_shape=[SDS((Rh, C), F32), SDS((4, Rh, C), BF16)],
                          compiler_params=_cp("arbitrary"))(sel_arr, g, xh)


def chip_sum(p32, y, sel_arr, name):
    Rh, C = p32.shape

    def body(s_ref, p_ref, y_ref, o_ref):
        o_ref[...] = ((p_ref[...] + y_ref[0].astype(F32)) + y_ref[1].astype(F32)) + y_ref[2].astype(F32)

    grid_spec = pltpu.PrefetchScalarGridSpec(
        num_scalar_prefetch=1, grid=(1,),
        in_specs=[pl.BlockSpec((Rh, C), lambda i, s: (0, 0)), pl.BlockSpec((3, Rh, C), lambda i, s: (0, 0, 0))],
        out_specs=pl.BlockSpec((Rh, C), lambda i, s: (s[1], 0)))
    return pl.pallas_call(body, grid_spec=grid_spec, name=name, out_shape=SDS((2 * Rh, C), F32),
                          compiler_params=_cp("arbitrary"))(sel_arr, p32, y)


class StepComm:
    FIRST = ("w_in",)
    SQUARE = ("w_out", "w_xq", "w_xk", "w_xv", "w_xo")
    FFN = ("w_up", "w_down")
    PLAN = {
        "proj": ("gather", SQUARE[:2]), "attn_fwd": ("gather", SQUARE[2:] + FFN),
        "d_h3": ("pair", FFN), "mixer_post_bwd_a": ("pair", SQUARE), "attn_bwd": ("chip", FFN + SQUARE),
        "dw_in": ("half", FFN + SQUARE), "d_h1": ("pair", FIRST), "end_chip": ("chip", FIRST),
    }

    def __init__(self, placed, sel_arr):
        self.placed, self.sel_arr = placed, sel_arr
        self.g, self.p32, self.p16, self.full, self.done, self.jobs = {}, {}, {}, {}, {}, {}

    def gather_first(self):
        return dict(zip(self.FIRST, run_job("gather_first", gather_job(self.placed, self.FIRST))))

    def job(self, tag, gb=None):
        kind, names = self.PLAN[tag]
        if kind == "gather":
            j = gather_job(self.placed, names)
        elif kind == "pair":
            for n in names:
                self.g[n] = gb[n] if gb[n].ndim == 3 else gb[n].reshape(4, gb[n].shape[0] // 4, gb[n].shape[1])
            j = pair_exchange_job(self.g, names)
        elif kind == "chip":
            j = chip_exchange_job(self.p16, names)
        else:
            j = half_exchange_job(self.full, names)
        self.jobs[tag] = j
        return j

    def landed(self, tag, wb=None):
        kind, names = self.PLAN[tag]
        for n, a in zip(names, self.jobs[tag].landed):
            if kind == "gather":
                wb[n] = a
            elif kind == "pair":
                self.p32[n], self.p16[n] = pair_sum(self.g[n], a, self.sel_arr, "pair_sum_" + n)
            elif kind == "chip":
                self.full[n] = chip_sum(self.p32[n], a, self.sel_arr, "chip_sum_" + n)
            else:
                self.done[n] = a

    def finish(self, gb):
        return self.done

    def last_job(self):
        return self.job("end_chip")

    def last_landed(self):
        self.landed("end_chip")
        last = run_job("half_exchange", half_exchange_job(self.full, self.FIRST))
        self.done.update(zip(self.FIRST, last))


class NoComm:
    def job(self, tag, gb=None):
        return None

    def landed(self, tag, wb=None):
        pass

    def finish(self, gb):
        return gb


SMALL = ("rel_bias", "g_mix", "w_short_conv", "g_attn_out", "g_conv_out", "g_xattn", "g_mem", "g_ffn",
         "w_ffn_conv", "b_ffn_conv", "g_final")
SMALL_FULL = {"rel_bias": (8, 32), "g_mix": (1, 1024), "w_short_conv": (3, 512), "g_attn_out": (1, 512), "g_conv_out": (1, 512),
              "g_xattn": (1, 1024), "g_mem": (1, 1024), "g_ffn": (1, 1024), "w_ffn_conv": (3, 5632), "b_ffn_conv": (1, 5632),
              "g_final": (1024,)}
SMALL_SHARDED = ("w_short_conv", "w_ffn_conv")


def _pack(parts):
    flat = jnp.concatenate([p.reshape(-1) for p in parts])
    rows = -(-flat.shape[0] // 1024) * 8
    return jnp.pad(flat, (0, rows * 128 - flat.shape[0])).reshape(rows, 128)


def _unpack(packed, names):
    flat, out, off = packed.reshape(-1), {}, 0
    for n in names:
        size = int(np.prod(SMALL_FULL[n]))
        out[n] = flat[off:off + size].reshape(SMALL_FULL[n])
        off += size
    return out


def kernel(x, mem, rel_bias, g_mix, w_in, w_short_conv, g_attn_out, g_conv_out, w_out, g_xattn, g_mem, w_xq, w_xk, w_xv, w_xo, g_ffn, w_up, w_ffn_conv, b_ffn_conv, w_down, g_final, loss_target, m_rel_bias, m_g_mix, m_w_in, m_w_short_conv, m_g_attn_out, m_g_conv_out, m_w_out, m_g_xattn, m_g_mem, m_w_xq, m_w_xk, m_w_xv, m_w_xo, m_g_ffn, m_w_up, m_w_ffn_conv, m_b_ffn_conv, m_w_down, m_g_final, v_rel_bias, v_g_mix, v_w_in, v_w_short_conv, v_g_attn_out, v_g_conv_out, v_w_out, v_g_xattn, v_g_mem, v_w_xq, v_w_xk, v_w_xv, v_w_xo, v_g_ffn, v_w_up, v_w_ffn_conv, v_b_ffn_conv, v_w_down, v_g_final):
    names = ("rel_bias", "g_mix", "w_in", "w_short_conv", "g_attn_out", "g_conv_out", "w_out", "g_xattn", "g_mem", "w_xq",
             "w_xk", "w_xv", "w_xo", "g_ffn", "w_up", "w_ffn_conv", "b_ffn_conv", "w_down", "g_final")
    W = dict(zip(names, (rel_bias, g_mix, w_in, w_short_conv, g_attn_out, g_conv_out, w_out, g_xattn, g_mem, w_xq, w_xk, w_xv,
                         w_xo, g_ffn, w_up, w_ffn_conv, b_ffn_conv, w_down, g_final)))
    M = dict(zip(names, (m_rel_bias, m_g_mix, m_w_in, m_w_short_conv, m_g_attn_out, m_g_conv_out, m_w_out, m_g_xattn, m_g_mem,
                         m_w_xq, m_w_xk, m_w_xv, m_w_xo, m_g_ffn, m_w_up, m_w_ffn_conv, m_b_ffn_conv, m_w_down, m_g_final)))
    V = dict(zip(names, (v_rel_bias, v_g_mix, v_w_in, v_w_short_conv, v_g_attn_out, v_g_conv_out, v_w_out, v_g_xattn, v_g_mem,
                         v_w_xq, v_w_xk, v_w_xv, v_w_xo, v_g_ffn, v_w_up, v_w_ffn_conv, v_b_ffn_conv, v_w_down, v_g_final)))
    xi, yi, ci = lax.axis_index("x"), lax.axis_index("y"), lax.axis_index("c")
    mine = 2 * xi + yi
    mine_arr = jnp.reshape(mine, (1,)).astype(jnp.int32)
    sel_arr = jnp.stack([mine, ci]).astype(jnp.int32)
    comm = StepComm({n: place_shard(W[n][0], mine_arr, n) for n in BIG}, sel_arr)
    first = jnp.where(ci == 0, 1.0, 0.0).astype(F32)
    emb = []
    for n in SMALL_SHARDED:
        shard = W[n][0]
        full = jnp.zeros(SMALL_FULL[n], F32)
        emb.append(lax.dynamic_update_slice(full, shard * first, (0, mine * shard.shape[1])))
    ws = {n: W[n] for n in SMALL if n not in SMALL_SHARDED}
    ws.update(zip(SMALL_SHARDED, allreduce_small(emb, "gather_small")))

    loss, grad_x, gfull, gs = local_step(x[0], mem[0], loss_target[0], None, ws, comm)

    def as2d(a):
        return a.reshape(1, -1) if a.ndim == 1 else a

    reduced = allreduce_small([as2d(gs[n]) for n in SMALL] + [loss], "reduce_small")
    gsm = dict(zip(SMALL, reduced[:-1]))
    loss = reduced[-1][0, 0]

    grads, delta, new_m, new_v = {}, {}, {}, {}
    for n in ("w_up",) + tuple(n for n in BIG if n != "w_up"):
        first_update = n == "w_up"
        d, nm, nv = adamw(W[n][0], gfull[n], M[n][0], V[n][0], "adamw_" + n, job=comm.last_job() if first_update else None)
        if first_update:
            comm.last_landed()
        grads[n], delta[n], new_m[n], new_v[n] = gfull[n][None], d[None], nm[None], nv[None]
    for n in SMALL_SHARDED:
        wid = W[n].shape[2]
        gsm[n] = lax.dynamic_slice(gsm[n], (0, mine * wid), (3, wid))

    def own(a, n):
        return a[0] if n in SMALL_SHARDED else as2d(a)

    d, nm, nv = adamw_multi([own(W[n], n) for n in SMALL], [gsm[n] for n in SMALL], [own(M[n], n) for n in SMALL],
                            [own(V[n], n) for n in SMALL], "adamw_small")
    for i, n in enumerate(SMALL):
        shape = W[n].shape
        grads[n], delta[n], new_m[n], new_v[n] = (a.reshape(shape) for a in (gsm[n], d[i], nm[i], nv[i]))
    return (loss, grad_x[None], *[grads[n] for n in names], *[delta[n] for n in names],
            *[new_m[n] for n in names], *[new_v[n] for n in names])
```

```python
import functools
import math

import numpy as np
import jax
import jax.numpy as jnp
from jax import lax
from jax.experimental import pallas as pl
from jax.experimental.pallas import tpu as pltpu

F32 = jnp.float32
BF16 = jnp.bfloat16
SDS = jax.ShapeDtypeStruct
MESH = pl.DeviceIdType.MESH

D_MODEL = 1024
ATTN_W = 512
N_HEADS = 8
BLK = 128
PATTERNS = ((128, 1), (512, 4), (2048, 16))
N_BUCKETS = 32
D_FF = 2816
N_MEM_HEADS = 4
MEM_HD = 256
EPS = 1e-6
NEG = -1e30
VMEM_LIMIT = 56 * 1024 * 1024

ADAM_LR, ADAM_B1, ADAM_B2, ADAM_EPS, ADAM_WD, ADAM_STEP = 0.001, 0.9, 0.999, 1e-08, 0.01, 10


def _cp(*sem):
    return pltpu.CompilerParams(dimension_semantics=sem, vmem_limit_bytes=VMEM_LIMIT)


def _dot(a, b):
    return jnp.dot(a, b, preferred_element_type=F32)


def _dot_nt(a, b):
    return lax.dot_general(a, b, (((1,), (1,)), ((), ())), preferred_element_type=F32)


def _dot_tn(a, b):
    return lax.dot_general(a, b, (((0,), (0,)), ((), ())), preferred_element_type=F32)


def _rsum(x):
    return jnp.sum(x, axis=1, keepdims=True)


def rmsnorm_fwd(x, g, name):
    S, Dm = x.shape
    tm = min(S, 512)

    def body(x_ref, g_ref, o_ref):
        xv = x_ref[...]
        r = lax.rsqrt(jnp.mean(xv * xv, axis=1, keepdims=True) + EPS)
        o_ref[...] = (xv * r * g_ref[...]).astype(o_ref.dtype)

    return pl.pallas_call(
        body, grid=(S // tm,), name=name,
        in_specs=[pl.BlockSpec((tm, Dm), lambda i: (i, 0)), pl.BlockSpec((1, Dm), lambda i: (0, 0))],
        out_specs=pl.BlockSpec((tm, Dm), lambda i: (i, 0)),
        out_shape=SDS((S, Dm), BF16), compiler_params=_cp("parallel"))(x, g)


def rmsnorm_bwd(x, g, dh, res, name, job=None):
    S, Dm = x.shape
    tm = min(S, 512)
    want_dx = res is not None

    def body(*refs):
        if want_dx:
            x_ref, g_ref, dh_ref, res_ref, dx_ref, dg_ref = refs
        else:
            x_ref, g_ref, dh_ref, dg_ref = refs
        i = pl.program_id(0)
        xv = x_ref[...]
        r = lax.rsqrt(jnp.mean(xv * xv, axis=1, keepdims=True) + EPS)
        xh = xv * r
        dh = dh_ref[...].astype(F32)
        if want_dx:
            gd = dh * g_ref[...]
            dx_ref[...] = res_ref[...] + r * (gd - xh * jnp.mean(gd * xh, axis=1, keepdims=True))

        @pl.when(i == 0)
        def _():
            dg_ref[...] = jnp.zeros_like(dg_ref)

        dg_ref[...] += jnp.sum(dh * xh, axis=0, keepdims=True)

    row = pl.BlockSpec((tm, Dm), lambda i: (i, 0))
    vec = pl.BlockSpec((1, Dm), lambda i: (0, 0))
    if want_dx:
        return _pcall(body, [x, g, dh, res], grid=(S // tm,), name=name, in_specs=[row, vec, row, row], out_specs=[row, vec],
                      out_shape=[SDS((S, Dm), F32), SDS((1, Dm), F32)], sem=("arbitrary",), job=job)
    return pl.pallas_call(
        body, grid=(S // tm,), name=name, in_specs=[row, vec, row], out_specs=vec,
        out_shape=SDS((1, Dm), F32), compiler_params=_cp("arbitrary"))(x, g, dh)


def _col_spec(bw, tn, rows, row_of, col_of):
    if bw is None:
        return pl.BlockSpec((rows, tn), lambda *g: (row_of(*g), col_of(*g)))
    if tn % bw == 0:
        return pl.BlockSpec((tn // bw, rows, bw), lambda *g: (col_of(*g), row_of(*g), 0))
    per = bw // tn
    return pl.BlockSpec((None, rows, tn), lambda *g: (col_of(*g) // per, row_of(*g), col_of(*g) % per))


def _read_cols(ref, bw, tn):
    if bw is None or tn % bw != 0:
        return ref[...]
    if tn == bw:
        return ref[0]
    return jnp.concatenate([ref[q] for q in range(tn // bw)], axis=1)


def _write_cols(ref, bw, tn, val):
    if bw is None or tn % bw != 0:
        ref[...] = val.astype(ref.dtype)
    else:
        for q in range(tn // bw):
            ref[q] = val[:, q * bw:(q + 1) * bw].astype(ref.dtype)


def mm_nn(a, b, name, *, res=None, out_dtype=F32, out_bw=None, tm=1024, tn=512, job=None):
    M, K = a.shape
    N = b.shape[1]
    tm = min(tm, M)

    def body(*refs):
        if res is None:
            a_ref, b_ref, o_ref = refs
        else:
            a_ref, b_ref, r_ref, o_ref = refs
        acc = _dot(a_ref[...].astype(BF16), b_ref[...])
        if res is not None:
            acc = acc + r_ref[...]
        _write_cols(o_ref, out_bw, tn, acc)

    ri, ci = (lambda i, j: i), (lambda i, j: j)
    in_specs = [pl.BlockSpec((tm, K), lambda i, j: (i, 0)), pl.BlockSpec((K, tn), lambda i, j: (0, j))]
    args = [a, b]
    if res is not None:
        in_specs.append(pl.BlockSpec((tm, tn), lambda i, j: (i, j)))
        args.append(res)
    oshape = (M, N) if out_bw is None else (N // out_bw, M, out_bw)
    return _pcall(body, args, grid=(M // tm, N // tn), name=name, in_specs=in_specs,
                  out_specs=[_col_spec(out_bw, tn, tm, ri, ci)], out_shape=[SDS(oshape, out_dtype)],
                  sem=("parallel", "parallel"), job=job)[0]


def mm_nt(a, a_bw, b, name, *, res=None, out_dtype=F32, tm=1024, tn=512, job=None):
    if a_bw is None:
        M, K = a.shape
    else:
        M, K = a.shape[1], a.shape[0] * a_bw
    N = b.shape[0]
    tm = min(tm, M)

    def body(*refs):
        if res is None:
            a_ref, b_ref, o_ref = refs
        else:
            a_ref, b_ref, r_ref, o_ref = refs
        av = _read_cols(a_ref, a_bw, K).astype(BF16)
        acc = _dot_nt(av, b_ref[...])
        if res is not None:
            acc = acc + r_ref[...]
        o_ref[...] = acc.astype(o_ref.dtype)

    in_specs = [_col_spec(a_bw, K, tm, lambda i, j: i, lambda i, j: 0), pl.BlockSpec((tn, K), lambda i, j: (j, 0))]
    args = [a, b]
    if res is not None:
        in_specs.append(pl.BlockSpec((tm, tn), lambda i, j: (i, j)))
        args.append(res)
    return _pcall(body, args, grid=(M // tm, N // tn), name=name, in_specs=in_specs,
                  out_specs=[pl.BlockSpec((tm, tn), lambda i, j: (i, j))], out_shape=[SDS((M, N), out_dtype)],
                  sem=("parallel", "parallel"), job=job)[0]


def mm_nt_norm_bwd(a, a_bw, b, x, g, res, name, *, tm=512, job=None):
    if a_bw is None:
        M, K = a.shape
    else:
        M, K = a.shape[1], a.shape[0] * a_bw
    N = b.shape[0]

    def body(a_ref, b_ref, x_ref, g_ref, r_ref, dx_ref, dg_ref):
        dh = _dot_nt(_read_cols(a_ref, a_bw, K).astype(BF16), b_ref[...])
        xv = x_ref[...]
        r = lax.rsqrt(jnp.mean(xv * xv, axis=1, keepdims=True) + EPS)
        xh = xv * r
        gd = dh * g_ref[...]
        dx_ref[...] = r_ref[...] + r * (gd - xh * jnp.mean(gd * xh, axis=1, keepdims=True))

        @pl.when(pl.program_id(0) == 0)
        def _():
            dg_ref[...] = jnp.zeros_like(dg_ref)

        dg_ref[...] += jnp.sum(dh * xh, axis=0, keepdims=True)

    row = pl.BlockSpec((tm, N), lambda i: (i, 0))
    vec = pl.BlockSpec((1, N), lambda i: (0, 0))
    return _pcall(body, [a, b, x, g, res], grid=(M // tm,), name=name,
                  in_specs=[_col_spec(a_bw, K, tm, lambda i: i, lambda i: 0), pl.BlockSpec((N, K), lambda i: (0, 0)), row, vec, row],
                  out_specs=[row, vec], out_shape=[SDS((M, N), F32), SDS((1, N), F32)], sem=("arbitrary",), job=job)


def mm_tn(a, b, b_bw, name, *, shards=None, tm=1024, tn=1024, ts=1024, job=None):
    S, Ka = a.shape
    N = b.shape[1] if b_bw is None else b.shape[0] * b_bw
    ts = min(ts, S)
    tm = min(tm, Ka)

    def body(a_ref, b_ref, o_ref):
        @pl.when(pl.program_id(2) == 0)
        def _():
            o_ref[...] = jnp.zeros_like(o_ref)

        bv = _read_cols(b_ref, b_bw, tn).astype(BF16)
        o_ref[...] += _dot_tn(a_ref[...].astype(BF16), bv)

    in_specs = [pl.BlockSpec((ts, tm), lambda i, j, k: (k, i)),
                _col_spec(b_bw, tn, ts, lambda i, j, k: k, lambda i, j, k: j)]
    if shards is None:
        out_spec, oshape = pl.BlockSpec((tm, tn), lambda i, j, k: (i, j)), (Ka, N)
    else:
        per = (N // shards) // tn
        out_spec = pl.BlockSpec((None, tm, tn), lambda i, j, k: (j // per, i, j % per))
        oshape = (shards, Ka, N // shards)
    return _pcall(body, [a, b], grid=(Ka // tm, N // tn, S // ts), name=name, in_specs=in_specs, out_specs=[out_spec],
                  out_shape=[SDS(oshape, F32)], sem=("parallel", "parallel", "arbitrary"), job=job)[0]


def _bucket_tables():
    out = np.zeros((3, 2, BLK, BLK), np.int32)
    qi = np.arange(BLK)[:, None]
    for p, (win, dil) in enumerate(PATTERNS):
        w = win // dil
        assert w == BLK
        for half in range(2):
            kj = np.arange(BLK)[None, :] + half * BLK
            steps = qi + w - kj
            valid = (steps >= 0) & (steps <= w)
            dist = np.clip(steps, 0, w) * dil
            dd = np.maximum(dist, 1).astype(np.float32)
            large = 16 + (np.log(dd / np.float32(16)) / np.float32(math.log(2048 / 16)) * np.float32(16)).astype(np.int32)
            large = np.minimum(large, N_BUCKETS - 1)
            out[p, half] = np.where(valid, np.where(dist < 16, dist, large), -1)
    return out


def bias_tables(rel_bias):
    bkt = jnp.asarray(_bucket_tables())

    def body(rb_ref, bkt_ref, o_ref):
        h = pl.program_id(1)
        for half in range(2):
            bk = bkt_ref[half]
            acc = jnp.full((BLK, BLK), NEG, F32)
            for b in range(N_BUCKETS):
                acc = jnp.where(bk == b, rb_ref[h, b], acc)
            o_ref[half] = acc

    return pl.pallas_call(
        body, grid=(3, N_HEADS), name="bias_tables",
        in_specs=[pl.BlockSpec(memory_space=pltpu.SMEM),
                  pl.BlockSpec((None, 2, BLK, BLK), lambda p, h: (p, 0, 0, 0))],
        out_specs=pl.BlockSpec((None, None, 2, BLK, BLK), lambda p, h: (p, h, 0, 0, 0)),
        out_shape=SDS((3, N_HEADS, 2, BLK, BLK), F32), compiler_params=_cp("parallel", "parallel"))(rel_bias, bkt)


def bias_tables_bwd(ds_sum):
    bkt = jnp.asarray(_bucket_tables())

    def body(ds_ref, bkt_ref, o_ref):
        h = pl.program_id(0)

        @pl.when(h == 0)
        def _():
            o_ref[...] = jnp.zeros_like(o_ref)

        rows = lax.broadcasted_iota(jnp.int32, (N_HEADS, N_BUCKETS), 0)
        cols = lax.broadcasted_iota(jnp.int32, (N_HEADS, N_BUCKETS), 1)
        acc = jnp.zeros((N_HEADS, N_BUCKETS), F32)
        for b in range(N_BUCKETS):
            tot = jnp.zeros((1, 1), F32)
            for p in range(3):
                for half in range(2):
                    t = jnp.where(bkt_ref[p, half] == b, ds_ref[p, half], 0.0)
                    tot = tot + jnp.sum(_rsum(t), axis=0, keepdims=True)
            acc = acc + jnp.where((rows == h) & (cols == b), tot, 0.0)
        o_ref[...] += acc

    return pl.pallas_call(
        body, grid=(N_HEADS,), name="bias_tables_bwd",
        in_specs=[pl.BlockSpec((3, None, 2, BLK, BLK), lambda h: (0, h, 0, 0, 0)),
                  pl.BlockSpec((3, 2, BLK, BLK), lambda h: (0, 0, 0, 0))],
        out_specs=pl.BlockSpec((N_HEADS, N_BUCKETS), lambda h: (0, 0)),
        out_shape=SDS((N_HEADS, N_BUCKETS), F32), compiler_params=_cp("arbitrary"))(ds_sum, bkt)


def _rows(start, dil):
    if dil == 1:
        return pl.ds(pl.multiple_of(start, BLK), BLK)
    return pl.ds(start, BLK, stride=dil)


GRP = 8


def _group_rows(i, dil, S):
    nb = S // (BLK * dil)
    run = min(nb, GRP)
    chunks = nb // run
    res0, b0 = (i // chunks) * (GRP // run), (i % chunks) * run
    cur = [(b0 + j % run) * (BLK * dil) + res0 + j // run for j in range(GRP)]
    t = lax.broadcasted_iota(jnp.int32, (GRP, 1, 1), 0)
    if run == nb:
        before, pen = None, jnp.where(t % run == 0, NEG, 0.0)
    else:
        before = _rows(jnp.maximum(b0 - 1, 0) * (BLK * dil) + res0, dil)
        pen = jnp.where((t == 0) & (b0 == 0), NEG, 0.0)
    return [_rows(s, dil) for s in cur], before, pen


def _load_group(ref, rows):
    return jnp.stack([ref[r, :] for r in rows])


def _with_prev(ref, before, cur_blocks):
    first = cur_blocks[:1] if before is None else ref[before, :][None].astype(cur_blocks.dtype)
    return jnp.concatenate([first, cur_blocks[:-1]], axis=0)


def _bdot_nt(a, b):
    return lax.dot_general(a, b, (((2,), (2,)), ((0,), (0,))), preferred_element_type=F32)


def _bdot(a, b):
    return lax.dot_general(a, b, (((2,), (1,)), ((0,), (0,))), preferred_element_type=F32)


def _bdot_tn(a, b):
    return lax.dot_general(a, b, (((1,), (1,)), ((0,), (0,))), preferred_element_type=F32)


def _lsum(x):
    return jnp.sum(x, axis=-1, keepdims=True)


def _widen(src, dst):
    S = src.shape[1]

    def chunk(i, carry):
        rows = pl.ds(pl.multiple_of(i * 512, 512), 512)
        for a in range(3):
            dst[a, rows, :] = src[a, rows, :].astype(F32)
        return carry

    lax.fori_loop(0, S // 512, chunk, 0)


def attn_fwd(qkv, bias, job=None):
    S = qkv.shape[3]
    nblk = S // BLK
    car = Carried(job, 2, 2, 1, 4)

    def body(*refs):
        (qkv_in, bias_ref, o_ref, lse_ref, qkv_ref), jrefs = car.split(refs)
        car.phase(0, pl.program_id(0), jrefs)
        _widen(qkv_in, qkv_ref)
        lane = lax.broadcasted_iota(jnp.int32, (GRP, BLK, BLK), 2)
        lo = lane < 64
        masks = (lo, jnp.logical_not(lo))
        q_ref, k_ref, v_ref = qkv_ref.at[0], qkv_ref.at[1], qkv_ref.at[2]
        for p, (_, dil) in enumerate(PATTERNS):
            def step(i, carry, p=p, dil=dil):
                rc, before, pen = _group_rows(i, dil, S)
                q2 = _load_group(q_ref, rc) * 0.125
                kc = _load_group(k_ref, rc).astype(BF16)
                kp = _with_prev(k_ref, before, kc)
                vc = _load_group(v_ref, rc)
                vp = _with_prev(v_ref, before, vc)
                if p > 0:
                    acc_old, ml_old = _load_group(o_ref, rc), _load_group(lse_ref, rc)
                pv = jnp.zeros((GRP, BLK, BLK), F32)
                m_new, l_new, alpha = [], [], []
                for h in range(2):
                    qh = jnp.where(masks[h], q2, 0.0).astype(BF16)
                    sp = _bdot_nt(qh, kp) + bias_ref[p, h, 0][None] + pen
                    sc = _bdot_nt(qh, kc) + bias_ref[p, h, 1][None]
                    mn = jnp.max(jnp.maximum(sp, sc), axis=-1, keepdims=True)
                    if p > 0:
                        mo = ml_old[:, :, 64 * h:64 * h + 1]
                        mn = jnp.maximum(mn, mo)
                        al = jnp.exp(mo - mn)
                        alpha.append(al)
                    pp = jnp.exp(sp - mn)
                    pc = jnp.exp(sc - mn)
                    ln = _lsum(pp + pc)
                    if p > 0:
                        ln = ln + al * ml_old[:, :, 64 * h + 32:64 * h + 33]
                    vhp = jnp.where(masks[h], vp, 0.0).astype(BF16)
                    vhc = jnp.where(masks[h], vc, 0.0).astype(BF16)
                    pv = pv + _bdot(pp.astype(BF16), vhp) + _bdot(pc.astype(BF16), vhc)
                    m_new.append(mn)
                    l_new.append(ln)
                if p > 0:
                    pv = pv + acc_old * jnp.where(lo, alpha[0], alpha[1])
                ml_t = jnp.where(lo, jnp.where(lane < 32, m_new[0], l_new[0]), jnp.where(lane < 96, m_new[1], l_new[1]))
                for j, r in enumerate(rc):
                    o_ref[r, :] = pv[j]
                    lse_ref[r, :] = ml_t[j]
                return carry

            lax.fori_loop(0, nblk // GRP, step, 0)

        def fin(i, carry):
            rows = pl.ds(pl.multiple_of(i * 512, 512), 512)
            ml = lse_ref[rows, :]
            is_m = (lax.broadcasted_iota(jnp.int32, ml.shape, 1) & 32) == 0
            den = jnp.where(is_m, pltpu.roll(ml, 96, 1), ml)
            o_ref[rows, :] = o_ref[rows, :] / den
            lse_ref[rows, :] = ml + jnp.log(den)
            return carry

        lax.fori_loop(0, S // 512, fin, 0)
        car.phase(1, pl.program_id(0), jrefs)
        car.phase(2, pl.program_id(0), jrefs)

    outs = pl.pallas_call(
        body, grid=(4,), name="attn_fwd",
        in_specs=[pl.BlockSpec((None, 3, None, S, BLK), lambda g: (0, 0, g, 0, 0)),
                  pl.BlockSpec((3, 2, 2, BLK, BLK), lambda g: (0, g, 0, 0, 0))] + car.in_specs(),
        out_specs=[pl.BlockSpec((None, S, BLK), lambda g: (g, 0, 0)),
                   pl.BlockSpec((None, S, BLK), lambda g: (g, 0, 0))] + car.out_specs(),
        out_shape=[SDS((4, S, BLK), F32), SDS((4, S, BLK), F32)] + car.out_shapes(),
        input_output_aliases=car.aliases(),
        scratch_shapes=[pltpu.VMEM((3, S, BLK), F32)] + car.sems(),
        compiler_params=_cp("arbitrary"))(qkv, bias, *car.args())
    if job is not None:
        job.landed = list(outs[2:])
    return outs[0], outs[1]


def attn_bwd(qkv, bias, o, lse, do, job=None):
    S = qkv.shape[3]
    nblk = S // BLK
    car = Carried(job, 5, 2, 2, 4)

    def body(*refs):
        (qkv_in, bias_ref, o_ref, lse_ref, do_ref, dqkv_out, ds_ref, qkv_ref, dqkv_ref), jrefs = car.split(refs)
        car.phase(0, pl.program_id(0), jrefs)
        _widen(qkv_in, qkv_ref)
        lane = lax.broadcasted_iota(jnp.int32, (GRP, BLK, BLK), 2)
        lo = lane < 64
        masks = (lo, jnp.logical_not(lo))
        dqkv_ref[...] = jnp.zeros_like(dqkv_ref)
        ds_ref[...] = jnp.zeros_like(ds_ref)
        q_ref, k_ref, v_ref = qkv_ref.at[0], qkv_ref.at[1], qkv_ref.at[2]
        dq_ref, dk_ref, dv_ref = dqkv_ref.at[0], dqkv_ref.at[1], dqkv_ref.at[2]
        for p, (_, dil) in enumerate(PATTERNS):
            def step(i, carry, p=p, dil=dil):
                rc, before, pen = _group_rows(i, dil, S)
                q2 = _load_group(q_ref, rc) * 0.125
                kc = _load_group(k_ref, rc).astype(BF16)
                kp = _with_prev(k_ref, before, kc)
                vc = _load_group(v_ref, rc).astype(BF16)
                vp = _with_prev(v_ref, before, vc)
                dot = _load_group(do_ref, rc)
                lset = _load_group(lse_ref, rc)
                prod = dot * _load_group(o_ref, rc)
                zero = jnp.zeros((GRP, BLK, BLK), F32)
                dq, dkc, dkp, dvc, dvp = zero, zero, zero, zero, zero
                for h in range(2):
                    qh = jnp.where(masks[h], q2, 0.0).astype(BF16)
                    doh = jnp.where(masks[h], dot, 0.0).astype(BF16)
                    delta = _lsum(jnp.where(masks[h], prod, 0.0))
                    lh = lset[:, :, 64 * h:64 * h + 1]
                    pp = jnp.exp(_bdot_nt(qh, kp) + bias_ref[p, h, 0][None] + pen - lh)
                    pc = jnp.exp(_bdot_nt(qh, kc) + bias_ref[p, h, 1][None] - lh)
                    dsp = pp * (_bdot_nt(doh, vp) - delta)
                    dsc = pc * (_bdot_nt(doh, vc) - delta)
                    ds_ref[p, h, 0] += jnp.sum(dsp, axis=0)
                    ds_ref[p, h, 1] += jnp.sum(dsc, axis=0)
                    dspb, dscb = dsp.astype(BF16), dsc.astype(BF16)
                    dq = dq + jnp.where(masks[h], _bdot(dspb, kp) + _bdot(dscb, kc), 0.0)
                    dkp = dkp + _bdot_tn(dspb, qh)
                    dkc = dkc + _bdot_tn(dscb, qh)
                    dvp = dvp + _bdot_tn(pp.astype(BF16), doh)
                    dvc = dvc + _bdot_tn(pc.astype(BF16), doh)
                none = jnp.zeros((1, BLK, BLK), F32)
                dkc = dkc + jnp.concatenate([dkp[1:], none], axis=0)
                dvc = dvc + jnp.concatenate([dvp[1:], none], axis=0)
                for j in range(GRP):
                    dq_ref[rc[j], :] += dq[j] * 0.125
                    dk_ref[rc[j], :] += dkc[j]
                    dv_ref[rc[j], :] += dvc[j]
                if before is not None:
                    dk_ref[before, :] += dkp[0]
                    dv_ref[before, :] += dvp[0]
                return carry

            lax.fori_loop(0, nblk // GRP, step, 0)

        def narrow(i, carry):
            rows = pl.ds(pl.multiple_of(i * 512, 512), 512)
            for a in range(3):
                dqkv_out[a, rows, :] = dqkv_ref[a, rows, :].astype(BF16)
            return carry

        lax.fori_loop(0, S // 512, narrow, 0)
        car.phase(1, pl.program_id(0), jrefs)
        car.phase(2, pl.program_id(0), jrefs)

    blk = pl.BlockSpec((None, S, BLK), lambda g: (g, 0, 0))
    outs = pl.pallas_call(
        body, grid=(4,), name="attn_bwd",
        in_specs=[pl.BlockSpec((None, 3, None, S, BLK), lambda g: (0, 0, g, 0, 0)),
                  pl.BlockSpec((3, 2, 2, BLK, BLK), lambda g: (0, g, 0, 0, 0)), blk, blk, blk] + car.in_specs(),
        out_specs=[pl.BlockSpec((None, 3, None, S, BLK), lambda g: (0, 0, g, 0, 0)),
                   pl.BlockSpec((3, 2, 2, BLK, BLK), lambda g: (0, g, 0, 0, 0))] + car.out_specs(),
        out_shape=[SDS((2, 3, 4, S, BLK), BF16), SDS((3, N_HEADS, 2, BLK, BLK), F32)] + car.out_shapes(),
        input_output_aliases=car.aliases(),
        scratch_shapes=[pltpu.VMEM((3, S, BLK), F32), pltpu.VMEM((3, S, BLK), F32)] + car.sems(),
        compiler_params=_cp("arbitrary"))(qkv, bias, o, lse, do, *car.args())
    if job is not None:
        job.landed = list(outs[2:])
    return outs[0], outs[1]


def _shift_down(u, k, halo):
    n = u.shape[0]
    row = lax.broadcasted_iota(jnp.int32, u.shape, 0)
    out = pltpu.roll(u, k, 0)
    hn = halo.shape[0]
    for j in range(k):
        out = jnp.where(row == j, halo[hn - k + j:hn - k + j + 1, :], out)
    return out


def _shift_up(u, k, halo):
    n = u.shape[0]
    row = lax.broadcasted_iota(jnp.int32, u.shape, 0)
    out = pltpu.roll(u, n - k, 0)
    for j in range(k):
        out = jnp.where(row == n - k + j, halo[j:j + 1, :], out)
    return out


def _conv3(u, halo, w0, w1, w2):
    return _shift_down(u, 2, halo) * w0 + _shift_down(u, 1, halo) * w1 + u * w2


def _colsum(x):
    return jnp.sum(x, axis=0, keepdims=True)


def _mixer_specs(S, tm):
    conv = pl.BlockSpec((None, 12, tm, BLK), lambda i: (1, 0, i, 0))
    halo = pl.BlockSpec((None, 12, 16, BLK), lambda i: (1, 0, jnp.maximum(i * (tm // 16) - 1, 0), 0))
    ob = pl.BlockSpec((4, tm, BLK), lambda i: (0, i, 0))
    return conv, halo, ob


def _mixer_recompute(i, o_ref, pr_ref, ph_ref, w_ref):
    ob = [o_ref[q] for q in range(4)]
    gb = [pr_ref[q].astype(F32) for q in range(4)]
    gc = [pr_ref[4 + q].astype(F32) for q in range(4)]
    xi = [pr_ref[8 + q].astype(F32) for q in range(4)]
    keep = jnp.where(i > 0, 1.0, 0.0)
    u = [gc[q] * xi[q] for q in range(4)]
    hu = [ph_ref[4 + q].astype(F32) * ph_ref[8 + q].astype(F32) * keep for q in range(4)]
    w = [[w_ref[k:k + 1, q * BLK:(q + 1) * BLK] for k in range(3)] for q in range(4)]
    cv = [_conv3(u[q], hu[q], *w[q]) for q in range(4)]
    return ob, gb, gc, xi, u, hu, cv, w


def _rms_blocks(blocks):
    ss = sum(_rsum(b * b) for b in blocks)
    return lax.rsqrt(ss / (BLK * len(blocks)) + EPS)


def mixer_post_fwd(o, proj, w_sc, g_a, g_c):
    S = o.shape[1]
    tm = 512

    def body(o_ref, pr_ref, ph_ref, w_ref, ga_ref, gc_ref, m_ref):
        i = pl.program_id(0)
        ob, gb, _, _, _, _, cv, _ = _mixer_recompute(i, o_ref, pr_ref, ph_ref, w_ref)
        conv = [gb[q] * cv[q] for q in range(4)]
        ra, rc = _rms_blocks(ob), _rms_blocks(conv)
        for q in range(4):
            sl = slice(q * BLK, (q + 1) * BLK)
            m_ref[:, q * BLK:(q + 1) * BLK] = (ob[q] * ra * ga_ref[:, sl]).astype(BF16)
            m_ref[:, ATTN_W + q * BLK:ATTN_W + (q + 1) * BLK] = (conv[q] * rc * gc_ref[:, sl]).astype(BF16)

    conv_s, halo_s, ob_s = _mixer_specs(S, tm)
    full = lambda r, c: pl.BlockSpec((r, c), lambda i: (0, 0))
    return pl.pallas_call(
        body, grid=(S // tm,), name="mixer_post_fwd",
        in_specs=[ob_s, conv_s, halo_s, full(3, 512), full(1, 512), full(1, 512)],
        out_specs=pl.BlockSpec((tm, D_MODEL), lambda i: (i, 0)), out_shape=SDS((S, D_MODEL), BF16),
        compiler_params=_cp("parallel"))(o, proj, proj, w_sc, g_a, g_c)


def mixer_post_bwd_a(dmixed, o, proj, w_sc, g_a, g_c, job=None):
    S = o.shape[1]
    tm = 512

    def body(dm_ref, o_ref, pr_ref, ph_ref, w_ref, ga_ref, gc_ref, do_ref, dgb_ref, dcv_ref, dga_ref, dgc_ref):
        i = pl.program_id(0)
        ob, gb, _, _, _, _, cv, _ = _mixer_recompute(i, o_ref, pr_ref, ph_ref, w_ref)
        conv = [gb[q] * cv[q] for q in range(4)]
        ra, rc = _rms_blocks(ob), _rms_blocks(conv)

        @pl.when(i == 0)
        def _():
            dga_ref[...] = jnp.zeros_like(dga_ref)
            dgc_ref[...] = jnp.zeros_like(dgc_ref)

        for blocks, r, g_ref, off, dg_ref, is_attn in ((ob, ra, ga_ref, 0, dga_ref, True), (conv, rc, gc_ref, ATTN_W, dgc_ref, False)):
            xh = [blocks[q] * r for q in range(4)]
            dm = [dm_ref[:, off + q * BLK:off + (q + 1) * BLK].astype(F32) for q in range(4)]
            gd = [dm[q] * g_ref[:, q * BLK:(q + 1) * BLK] for q in range(4)]
            mean = sum(_rsum(gd[q] * xh[q]) for q in range(4)) / (4 * BLK)
            for q in range(4):
                dg_ref[:, q * BLK:(q + 1) * BLK] += _colsum(dm[q] * xh[q])
                dx = r * (gd[q] - xh[q] * mean)
                if is_attn:
                    do_ref[q] = dx
                else:
                    dgb_ref[q] = dx * cv[q]
                    dcv_ref[q] = dx * gb[q]

    conv_s, halo_s, ob_s = _mixer_specs(S, tm)
    full = lambda r, c: pl.BlockSpec((r, c), lambda i: (0, 0))
    return _pcall(
        body, [dmixed, o, proj, proj, w_sc, g_a, g_c], grid=(S // tm,), name="mixer_post_bwd_a",
        in_specs=[pl.BlockSpec((tm, D_MODEL), lambda i: (i, 0)), ob_s, conv_s, halo_s, full(3, 512), full(1, 512), full(1, 512)],
        out_specs=[ob_s, ob_s, ob_s, full(1, 512), full(1, 512)],
        out_shape=[SDS((4, S, BLK), F32)] * 3 + [SDS((1, 512), F32)] * 2, sem=("arbitrary",), job=job)


def mixer_post_bwd_b(dproj, dgb, dcv, proj, w_sc):
    S = proj.shape[2]
    tm = 512
    last = S // 8 - 1

    def body(dp_in, dgb_ref, dcv_ref, dn_ref, pr_ref, w_ref, dp_ref, dw_ref):
        i = pl.program_id(0)
        keep_next = jnp.where(i < pl.num_programs(0) - 1, 1.0, 0.0)

        @pl.when(i == 0)
        def _():
            dw_ref[...] = jnp.zeros_like(dw_ref)

        for q in range(4):
            sl = slice(q * BLK, (q + 1) * BLK)
            gc, xi = pr_ref[4 + q].astype(F32), pr_ref[8 + q].astype(F32)
            u = gc * xi
            dcv = dcv_ref[q]
            dn = dn_ref[q] * keep_next
            d1, d2 = _shift_up(dcv, 1, dn), _shift_up(dcv, 2, dn)
            w0, w1, w2 = (w_ref[k:k + 1, sl] for k in range(3))
            du = dcv * w2 + d1 * w1 + d2 * w0
            dw_ref[0:1, sl] += _colsum(d2 * u)
            dw_ref[1:2, sl] += _colsum(d1 * u)
            dw_ref[2:3, sl] += _colsum(dcv * u)
            dp_ref[q] = dgb_ref[q].astype(BF16)
            dp_ref[4 + q] = (du * xi).astype(BF16)
            dp_ref[8 + q] = (du * gc).astype(BF16)

    conv_s, _, ob_s = _mixer_specs(S, tm)
    nxt = pl.BlockSpec((4, 8, BLK), lambda i: (0, jnp.minimum((i + 1) * (tm // 8), last), 0))
    full = lambda r, c: pl.BlockSpec((r, c), lambda i: (0, 0))
    return pl.pallas_call(
        body, grid=(S // tm,), name="mixer_post_bwd_b",
        in_specs=[pl.BlockSpec(memory_space=pl.ANY), ob_s, ob_s, nxt, conv_s, full(3, 512)],
        out_specs=[conv_s, full(3, 512)],
        out_shape=[SDS(dproj.shape, BF16), SDS((3, 512), F32)],
        input_output_aliases={0: 0}, compiler_params=_cp("arbitrary"))(dproj, dgb, dcv, dcv, proj, w_sc)


def xattn_fwd(q, k, v):
    S = q.shape[0]
    tm = 512
    scale = MEM_HD ** -0.5

    def body(q_ref, k_ref, v_ref, o_ref):
        for h in range(N_MEM_HEADS):
            sl = slice(h * MEM_HD, (h + 1) * MEM_HD)
            s = _dot_nt(q_ref[:, sl], k_ref[:, sl]) * scale
            p = jnp.exp(s - jnp.max(s, axis=1, keepdims=True))
            p = p / _rsum(p)
            o_ref[:, sl] = _dot(p.astype(BF16), v_ref[:, sl]).astype(BF16)

    row = pl.BlockSpec((tm, D_MODEL), lambda i: (i, 0))
    kv = pl.BlockSpec(k.shape, lambda i: (0, 0))
    return pl.pallas_call(body, grid=(S // tm,), name="xattn_fwd", in_specs=[row, kv, kv], out_specs=row,
                          out_shape=SDS((S, D_MODEL), BF16), compiler_params=_cp("parallel"))(q, k, v)


def xattn_bwd(q, k, v, do):
    S = q.shape[0]
    tm = 512
    scale = MEM_HD ** -0.5

    def body(q_ref, k_ref, v_ref, do_ref, dq_ref, dk_ref, dv_ref):
        @pl.when(pl.program_id(0) == 0)
        def _():
            dk_ref[...] = jnp.zeros_like(dk_ref)
            dv_ref[...] = jnp.zeros_like(dv_ref)

        for h in range(N_MEM_HEADS):
            sl = slice(h * MEM_HD, (h + 1) * MEM_HD)
            qh, kh, vh, doh = q_ref[:, sl], k_ref[:, sl], v_ref[:, sl], do_ref[:, sl]
            s = _dot_nt(qh, kh) * scale
            p = jnp.exp(s - jnp.max(s, axis=1, keepdims=True))
            p = p / _rsum(p)
            dp = _dot_nt(doh, vh)
            ds = (p * (dp - _rsum(p * dp)) * scale).astype(BF16)
            dq_ref[:, sl] = _dot(ds, kh).astype(BF16)
            dk_ref[:, sl] += _dot_tn(ds, qh)
            dv_ref[:, sl] += _dot_tn(p.astype(BF16), doh)

    row = pl.BlockSpec((tm, D_MODEL), lambda i: (i, 0))
    kv = pl.BlockSpec(k.shape, lambda i: (0, 0))
    return pl.pallas_call(body, grid=(S // tm,), name="xattn_bwd", in_specs=[row, kv, kv, row], out_specs=[row, kv, kv],
                          out_shape=[SDS((S, D_MODEL), BF16), SDS(k.shape, F32), SDS(k.shape, F32)],
                          compiler_params=_cp("arbitrary"))(q, k, v, do)


FFN_TM, FFN_TC = 256, 1408


def _ffn_specs(S, order):
    tm, tc = FFN_TM, FFN_TC
    ij = (lambda a, b: (a, b)) if order == "ij" else (lambda a, b: (b, a))
    blk = pl.BlockSpec((2, tm, tc), lambda a, b: (0,) + ij(a, b))
    prev = pl.BlockSpec((2, 16, tc), lambda a, b: (0, jnp.maximum(ij(a, b)[0] * (tm // 16) - 1, 0), ij(a, b)[1]))
    nxt = pl.BlockSpec((2, 16, tc), lambda a, b: (0, jnp.minimum((ij(a, b)[0] + 1) * (tm // 16), S // 16 - 1), ij(a, b)[1]))
    wsp = pl.BlockSpec((2, 3, tc), lambda a, b: (0, 0, ij(a, b)[1]))
    bsp = pl.BlockSpec((2, 1, tc), lambda a, b: (0, 0, ij(a, b)[1]))
    act = pl.BlockSpec((tm, tc), lambda a, b: ij(a, b))
    return blk, prev, nxt, wsp, bsp, act


def _ffn_up(i, up_ref, uh_ref, w_ref, b_ref):
    keep = jnp.where(i > 0, 1.0, 0.0)
    out = []
    for half in range(2):
        u = up_ref[half].astype(F32)
        hu = uh_ref[half].astype(F32) * keep
        w0, w1, w2 = (w_ref[half, k:k + 1, :] for k in range(3))
        out.append(_conv3(u, hu, w0, w1, w2) + b_ref[half])
    return out


def ffn_act_fwd(up_pre, w_fc, b_fc):
    S = up_pre.shape[1]

    def body(up_ref, uh_ref, w_ref, b_ref, a_ref, gv_ref):
        g, v = _ffn_up(pl.program_id(0), up_ref, uh_ref, w_ref, b_ref)
        a_ref[...] = (g * jax.nn.sigmoid(g) * v).astype(BF16)
        gv_ref[0] = g.astype(BF16)
        gv_ref[1] = v.astype(BF16)

    blk, prev, _, wsp, bsp, act = _ffn_specs(S, "ij")
    return pl.pallas_call(body, grid=(S // FFN_TM, D_FF // FFN_TC), name="ffn_act_fwd",
                          in_specs=[blk, prev, wsp, bsp], out_specs=[act, blk],
                          out_shape=[SDS((S, D_FF), BF16), SDS((2, S, D_FF), BF16)],
                          compiler_params=_cp("parallel", "parallel"))(up_pre, up_pre, w_fc, b_fc)


def ffn_act_bwd_a(dact, up, job=None):
    S = up.shape[1]

    def body(da_ref, up_ref, du_ref, db_ref):
        i = pl.program_id(1)
        g, v = up_ref[0].astype(F32), up_ref[1].astype(F32)
        da = da_ref[...].astype(F32)
        sg = jax.nn.sigmoid(g)
        dg = da * v * (sg * (1.0 + g * (1.0 - sg)))
        dv = da * g * sg

        @pl.when(i == 0)
        def _():
            db_ref[...] = jnp.zeros_like(db_ref)

        du_ref[0] = dg.astype(BF16)
        du_ref[1] = dv.astype(BF16)
        db_ref[0] += _colsum(dg)
        db_ref[1] += _colsum(dv)

    blk, _, _, _, bsp, act = _ffn_specs(S, "ji")
    return _pcall(body, [dact, up], grid=(D_FF // FFN_TC, S // FFN_TM), name="ffn_act_bwd_a",
                  in_specs=[act, blk], out_specs=[blk, bsp],
                  out_shape=[SDS((2, S, D_FF), BF16), SDS((2, 1, D_FF), F32)], sem=("parallel", "arbitrary"), job=job)


def ffn_act_bwd_b(d_up, up_pre, w_fc):
    S = up_pre.shape[1]

    def body(du_ref, dn_ref, up_ref, w_ref, dp_ref, dw_ref):
        i = pl.program_id(1)
        keep_next = jnp.where(i < pl.num_programs(1) - 1, 1.0, 0.0)

        @pl.when(i == 0)
        def _():
            dw_ref[...] = jnp.zeros_like(dw_ref)

        for half in range(2):
            d = du_ref[half].astype(F32)
            dn = dn_ref[half].astype(F32) * keep_next
            u = up_ref[half].astype(F32)
            d1, d2 = _shift_up(d, 1, dn), _shift_up(d, 2, dn)
            w0, w1, w2 = (w_ref[half, k:k + 1, :] for k in range(3))
            dp_ref[half] = (d * w2 + d1 * w1 + d2 * w0).astype(BF16)
            dw_ref[half, 0:1, :] += _colsum(d2 * u)
            dw_ref[half, 1:2, :] += _colsum(d1 * u)
            dw_ref[half, 2:3, :] += _colsum(d * u)

    blk, _, nxt, wsp, _, _ = _ffn_specs(S, "ji")
    return pl.pallas_call(body, grid=(D_FF // FFN_TC, S // FFN_TM), name="ffn_act_bwd_b",
                          in_specs=[blk, nxt, blk, wsp], out_specs=[blk, wsp],
                          out_shape=[SDS((2, S, D_FF), BF16), SDS((2, 3, D_FF), F32)],
                          compiler_params=_cp("parallel", "arbitrary"))(d_up, d_up, up_pre, w_fc)


def final_loss(x3, g, target):
    S, Dm = x3.shape
    tm = 512

    def body(x_ref, g_ref, t_ref, loss_ref, dx_ref, dg_ref):
        i = pl.program_id(0)
        xv = x_ref[...]
        r = lax.rsqrt(jnp.mean(xv * xv, axis=1, keepdims=True) + EPS)
        xh = xv * r
        err = xh * g_ref[...] - t_ref[...]
        dy = err / Dm
        gd = dy * g_ref[...]
        dx_ref[...] = r * (gd - xh * jnp.mean(gd * xh, axis=1, keepdims=True))

        @pl.when(i == 0)
        def _():
            dg_ref[...] = jnp.zeros_like(dg_ref)
            loss_ref[...] = jnp.zeros_like(loss_ref)

        dg_ref[...] += _colsum(dy * xh)
        loss_ref[...] += 0.5 * _colsum(jnp.mean(err * err, axis=1, keepdims=True))

    row = pl.BlockSpec((tm, Dm), lambda i: (i, 0))
    vec = pl.BlockSpec((1, Dm), lambda i: (0, 0))
    one = pl.BlockSpec((1, 1), lambda i: (0, 0))
    return pl.pallas_call(body, grid=(S // tm,), name="final_loss", in_specs=[row, vec, row], out_specs=[one, row, vec],
                          out_shape=[SDS((1, 1), F32), SDS((S, Dm), F32), SDS((1, Dm), F32)],
                          compiler_params=_cp("arbitrary"))(x3, g, target)


def local_step(x, mem, target, wb, ws, comm=None):
    S = x.shape[0]
    assert S % 2048 == 0
    if comm is None:
        comm = NoComm()
    else:
        wb = comm.gather_first()
    w_fc = ws["w_ffn_conv"].reshape(3, 2, D_FF).transpose(1, 0, 2)
    b_fc = ws["b_ffn_conv"].reshape(2, 1, D_FF)

    bias = bias_tables(ws["rel_bias"])
    h1 = rmsnorm_fwd(x, ws["g_mix"], "norm_mix")
    proj = mm_nn(h1, wb["w_in"], "proj", out_dtype=BF16, out_bw=BLK, tn=768, job=comm.job("proj"))
    comm.landed("proj", wb)
    qkv = proj.reshape(2, 3, 4, S, BLK)
    o, lse = attn_fwd(qkv, bias, job=comm.job("attn_fwd"))
    comm.landed("attn_fwd", wb)
    proj4 = proj.reshape(2, 12, S, BLK)
    mixed = mixer_post_fwd(o, proj4, ws["w_short_conv"], ws["g_attn_out"], ws["g_conv_out"])
    x1 = mm_nn(mixed, wb["w_out"], "out_proj", res=x)
    h2 = rmsnorm_fwd(x1, ws["g_xattn"], "norm_xattn")
    mem_n = rmsnorm_fwd(mem, ws["g_mem"], "norm_mem")
    xq = mm_nn(h2, wb["w_xq"], "xq", out_dtype=BF16)
    xk = mm_nn(mem_n, wb["w_xk"], "xk", out_dtype=BF16, tn=1024)
    xv = mm_nn(mem_n, wb["w_xv"], "xv", out_dtype=BF16, tn=1024)
    xo = xattn_fwd(xq, xk, xv)
    x2 = mm_nn(xo, wb["w_xo"], "xo_proj", res=x1)
    h3 = rmsnorm_fwd(x2, ws["g_ffn"], "norm_ffn")
    up_pre = mm_nn(h3, wb["w_up"], "up_proj", out_dtype=BF16, out_bw=D_FF, tn=1408)
    act, up = ffn_act_fwd(up_pre, w_fc, b_fc)
    x3 = mm_nn(act, wb["w_down"], "down_proj", res=x2)
    loss, dx3, dg_final = final_loss(x3, ws["g_final"].reshape(1, -1), target)

    gb, gs = {}, {"g_final": dg_final}
    gb["w_down"] = mm_tn(act, dx3, None, "dw_down", tm=1408, tn=1024)
    dact = mm_nt(dx3, None, wb["w_down"], "d_act", out_dtype=BF16, tn=1408)
    d_up, db_fc = ffn_act_bwd_a(dact, up)
    d_up_pre, dw_fc = ffn_act_bwd_b(d_up, up_pre, w_fc)
    gs["b_ffn_conv"] = db_fc.reshape(1, 2 * D_FF)
    gs["w_ffn_conv"] = dw_fc.transpose(1, 0, 2).reshape(3, 2 * D_FF)
    gb["w_up"] = mm_tn(h3, d_up_pre, D_FF, "dw_up", shards=4, tn=1408)
    dx2, gs["g_ffn"] = mm_nt_norm_bwd(d_up_pre, D_FF, wb["w_up"], x2, ws["g_ffn"], dx3, "d_h3", tm=256,
                                      job=comm.job("d_h3", gb))
    comm.landed("d_h3")
    gb["w_xo"] = mm_tn(xo, dx2, None, "dw_xo")
    dxo = mm_nt(dx2, None, wb["w_xo"], "d_xo", out_dtype=BF16)
    dxq, dxk, dxv = xattn_bwd(xq, xk, xv, dxo)
    gb["w_xq"] = mm_tn(h2, dxq, None, "dw_xq")
    gb["w_xk"] = mm_tn(mem_n, dxk, None, "dw_xk", tn=1024)
    gb["w_xv"] = mm_tn(mem_n, dxv, None, "dw_xv", tn=1024)
    dmem_n = mm_nt(dxk, None, wb["w_xk"], "d_memk", tn=1024)
    dmem_n = mm_nt(dxv, None, wb["w_xv"], "d_memv", res=dmem_n, tn=1024)
    gs["g_mem"] = rmsnorm_bwd(mem, ws["g_mem"], dmem_n, None, "norm_mem_bwd")
    dx1, gs["g_xattn"] = mm_nt_norm_bwd(dxq, None, wb["w_xq"], x1, ws["g_xattn"], dx2, "d_h2")
    gb["w_out"] = mm_tn(mixed, dx1, None, "dw_out")
    dmixed = mm_nt(dx1, None, wb["w_out"], "d_mixed")
    do, dgb, dcv, gs["g_attn_out"], gs["g_conv_out"] = mixer_post_bwd_a(
        dmixed, o, proj4, ws["w_short_conv"], ws["g_attn_out"], ws["g_conv_out"], job=comm.job("mixer_post_bwd_a", gb))
    comm.landed("mixer_post_bwd_a")
    dproj, ds_sum = attn_bwd(qkv, bias, o, lse, do, job=comm.job("attn_bwd"))
    comm.landed("attn_bwd")
    gs["rel_bias"] = bias_tables_bwd(ds_sum)
    dproj, gs["w_short_conv"] = mixer_post_bwd_b(dproj.reshape(2, 12, S, BLK), dgb, dcv, proj4, ws["w_short_conv"])
    dproj = dproj.reshape(24, S, BLK)
    gb["w_in"] = mm_tn(h1, dproj, BLK, "dw_in", shards=4, tn=768, job=comm.job("dw_in"))
    comm.landed("dw_in")
    grad_x, gs["g_mix"] = mm_nt_norm_bwd(dproj, BLK, wb["w_in"], x, ws["g_mix"], dx1, "d_h1", tm=256,
                                         job=comm.job("d_h1", gb))
    comm.landed("d_h1")
    return loss, grad_x, comm.finish(gb), gs


def adamw(w, g, m, v, name, job=None):
    R, C = w.shape
    tr = R
    for cand in (256, 352):
        if R > cand and R % cand == 0:
            tr = cand
            break

    def body(w_ref, g_ref, m_ref, v_ref, d_ref, nm_ref, nv_ref):
        gv = g_ref[...]
        mn = ADAM_B1 * m_ref[...] + (1.0 - ADAM_B1) * gv
        vn = ADAM_B2 * v_ref[...] + (1.0 - ADAM_B2) * (gv * gv)
        m_hat = mn / (1.0 - ADAM_B1 ** ADAM_STEP)
        v_hat = vn / (1.0 - ADAM_B2 ** ADAM_STEP)
        d_ref[...] = -ADAM_LR * (m_hat / (jnp.sqrt(v_hat) + ADAM_EPS) + ADAM_WD * w_ref[...])
        nm_ref[...] = mn
        nv_ref[...] = vn

    blk = pl.BlockSpec((tr, C), lambda i: (i, 0))
    return _pcall(body, [w, g, m, v], grid=(R // tr,), name=name, in_specs=[blk] * 4, out_specs=[blk] * 3,
                  out_shape=[SDS((R, C), F32)] * 3, sem=("parallel",), job=job)


BIG = ("w_in", "w_out", "w_xq", "w_xk", "w_xv", "w_xo", "w_up", "w_down")
COL_SHARDED = ("w_in", "w_up")
N_BIG = len(BIG)
ANY = pl.BlockSpec(memory_space=pl.ANY)


def _place():
    x, y, c = lax.axis_index("x"), lax.axis_index("y"), lax.axis_index("c")
    chips = [(1 - x, y), (x, 1 - y), (1 - x, 1 - y)]
    return x, y, c, chips


def _window(full, name, R, C, shard, half):
    r0, nr = (0, R) if half is None else (half * (R // 2), R // 2)
    if name in COL_SHARDED:
        return full.at[pl.ds(r0, nr), pl.ds(shard * C, C)]
    return full.at[pl.ds(shard * R + r0, nr), :]


def place_shard(w, mine_arr, n):
    R, C = w.shape
    col = n in COL_SHARDED

    def body(s_ref, w_ref, o_ref):
        o_ref[...] = w_ref[...].astype(BF16)

    grid_spec = pltpu.PrefetchScalarGridSpec(
        num_scalar_prefetch=1, grid=(1,), in_specs=[pl.BlockSpec((R, C), lambda i, s: (0, 0))],
        out_specs=pl.BlockSpec((R, C), (lambda i, s: (0, s[0])) if col else (lambda i, s: (s[0], 0))))
    return pl.pallas_call(body, grid_spec=grid_spec, name="place_" + n,
                          out_shape=SDS((R, 4 * C) if col else (4 * R, C), BF16),
                          compiler_params=_cp("arbitrary"))(mine_arr, w)


def _gather_jobs(names, shapes):
    nw = len(names)

    def start(full, sems):
        send, recv, fsend, frecv = sems
        x, y, c, chips = _place()
        mine = 2 * x + y
        for w, n in enumerate(names):
            R, C = shapes[w]
            own = _window(full[w], n, R, C, mine, c)
            for j, chip in enumerate(chips):
                pltpu.make_async_remote_copy(src_ref=own, dst_ref=own, send_sem=send.at[w, j], recv_sem=recv.at[w, j],
                                             device_id=(*chip, c), device_id_type=MESH).start()

    def mid(full, sems):
        send, recv, fsend, frecv = sems
        x, y, c, chips = _place()
        sib = (x, y, 1 - c)
        for w, n in enumerate(names):
            R, C = shapes[w]
            for j, chip in enumerate(chips):
                landed = _window(full[w], n, R, C, 2 * chip[0] + chip[1], c)
                pltpu.make_async_remote_copy(src_ref=landed, dst_ref=landed, send_sem=send.at[w, j], recv_sem=recv.at[w, j],
                                             device_id=(*chip, c), device_id_type=MESH).wait_recv()
                pltpu.make_async_remote_copy(src_ref=landed, dst_ref=landed, send_sem=fsend.at[w, j],
                                             recv_sem=frecv.at[w, j], device_id=sib, device_id_type=MESH).start()

    def finish(full, sems):
        send, recv, fsend, frecv = sems
        x, y, c, chips = _place()
        mine = 2 * x + y
        sib = (x, y, 1 - c)
        for w, n in enumerate(names):
            R, C = shapes[w]
            own = _window(full[w], n, R, C, mine, c)
            for j, chip in enumerate(chips):
                landed = _window(full[w], n, R, C, 2 * chip[0] + chip[1], c)
                other = _window(full[w], n, R, C, 2 * chip[0] + chip[1], 1 - c)
                pltpu.make_async_remote_copy(src_ref=other, dst_ref=other, send_sem=fsend.at[w, j], recv_sem=frecv.at[w, j],
                                             device_id=sib, device_id_type=MESH).wait_recv()
                pltpu.make_async_remote_copy(src_ref=own, dst_ref=own, send_sem=send.at[w, j], recv_sem=recv.at[w, j],
                                             device_id=(*chip, c), device_id_type=MESH).wait_send()
                pltpu.make_async_remote_copy(src_ref=landed, dst_ref=landed, send_sem=fsend.at[w, j],
                                             recv_sem=frecv.at[w, j], device_id=sib, device_id_type=MESH).wait_send()

    return start, mid, finish, [pltpu.SemaphoreType.DMA((nw, 3))] * 4


class CommJob:
    def __init__(self, ins, out_shapes, inplace, start, finish, sems, mid=None):
        self.ins, self.out_shapes, self.inplace = list(ins), list(out_shapes), inplace
        self.start, self.mid, self.finish, self.sems = start, mid, finish, list(sems)

    def run(self, phase, ins, outs, sems):
        if phase == 0:
            self.start(ins, outs, sems)
        elif phase == 1:
            if self.mid is not None:
                self.mid(ins, outs, sems)
        else:
            self.finish(ins, outs, sems)


def run_job(name, job):
    n_in, n_out = len(job.ins), len(job.out_shapes)

    def body(*refs):
        ins, outs, sm = refs[:n_in], refs[n_in:n_in + n_out], refs[n_in + n_out:]
        for phase in range(3):
            job.run(phase, ins, outs, sm)

    job.landed = list(pl.pallas_call(
        body, name=name, in_specs=[ANY] * n_in, out_specs=[ANY] * n_out, out_shape=job.out_shapes,
        input_output_aliases={i: i for i in range(n_in)} if job.inplace else {},
        scratch_shapes=job.sems)(*job.ins))
    return job.landed


class Carried:
    def __init__(self, job, n_in, n_out, n_scratch, steps):
        self.job, self.n_in, self.n_out, self.n_scratch, self.steps = job, n_in, n_out, n_scratch, steps
        self.nji = len(job.ins) if job else 0
        self.njo = len(job.out_shapes) if job else 0

    def in_specs(self):
        return [ANY] * self.nji

    def out_specs(self):
        return [ANY] * self.njo

    def out_shapes(self):
        return list(self.job.out_shapes) if self.job else []

    def aliases(self):
        if not (self.job and self.job.inplace):
            return {}
        return {self.n_in + i: self.n_out + i for i in range(self.nji)}

    def sems(self):
        return list(self.job.sems) if self.job else []

    def args(self):
        return list(self.job.ins) if self.job else []

    def split(self, refs):
        a = self.n_in
        b = a + self.nji
        c = b + self.n_out
        d = c + self.njo
        e = d + self.n_scratch
        return refs[:a] + refs[b:c] + refs[d:e], (refs[a:b], refs[c:d], refs[e:])

    def phase(self, phase, step, jrefs):
        if self.job is None:
            return
        at = {0: 0, 1: max(self.steps - 2, 0), 2: self.steps - 1}[phase]

        @pl.when(step == at)
        def _():
            self.job.run(phase, *jrefs)


def _pcall(body, args, *, grid, in_specs, out_specs, out_shape, name, sem, scratch=(), aliases=None, job=None):
    n_in, n_out = len(args), len(out_shape)
    if job is None:
        return pl.pallas_call(
            body, grid=grid, in_specs=list(in_specs), out_specs=list(out_specs), out_shape=list(out_shape), name=name,
            scratch_shapes=list(scratch), input_output_aliases=dict(aliases or {}), compiler_params=_cp(*sem))(*args)
    total = int(np.prod(grid))
    car = Carried(job, n_in, n_out, len(scratch), total)

    def wrapped(*refs):
        main, jrefs = car.split(refs)
        lin = pl.program_id(0)
        for ax in range(1, len(grid)):
            lin = lin * grid[ax] + pl.program_id(ax)
        car.phase(0, lin, jrefs)
        body(*main)
        car.phase(1, lin, jrefs)
        car.phase(2, lin, jrefs)

    res = pl.pallas_call(
        wrapped, grid=grid, in_specs=list(in_specs) + car.in_specs(), out_specs=list(out_specs) + car.out_specs(),
        out_shape=list(out_shape) + car.out_shapes(), name=name, scratch_shapes=list(scratch) + car.sems(),
        input_output_aliases={**dict(aliases or {}), **car.aliases()},
        compiler_params=_cp(*(["arbitrary"] * len(grid))))(*args, *car.args())
    job.landed = list(res[n_out:])
    return list(res[:n_out])


def gather_job(placed, names):
    shapes = []
    for n in names:
        R, C = placed[n].shape
        shapes.append((R, C // 4) if n in COL_SHARDED else (R // 4, C))
    start, mid, finish, sems = _gather_jobs(names, shapes)
    arrays = [placed[n] for n in names]
    return CommJob(arrays, [SDS(a.shape, a.dtype) for a in arrays], True,
                   lambda i, o, s: start(o, s), lambda i, o, s: finish(o, s), sems, mid=lambda i, o, s: mid(o, s))


def pair_exchange_job(grads, names):
    shapes = [grads[n].shape for n in names]

    def copies(ins, outs, sems):
        x, y, c, _ = _place()
        return [pltpu.make_async_remote_copy(
            src_ref=ins[w].at[:, pl.ds((1 - c) * (shapes[w][1] // 2), shapes[w][1] // 2), :], dst_ref=outs[w],
            send_sem=sems[0].at[w], recv_sem=sems[1].at[w], device_id=(x, y, 1 - c), device_id_type=MESH)
            for w in range(len(names))]

    def start(ins, outs, sems):
        for cp in copies(ins, outs, sems):
            cp.start()

    def finish(ins, outs, sems):
        for cp in copies(ins, outs, sems):
            cp.wait()

    return CommJob([grads[n] for n in names], [SDS((4, s[1] // 2, s[2]), F32) for s in shapes], False, start, finish,
                   [pltpu.SemaphoreType.DMA((len(names),))] * 2)


def chip_exchange_job(p16, names):
    shapes = [p16[n].shape for n in names]

    def copies(ins, outs, sems):
        x, y, c, chips = _place()
        return [pltpu.make_async_remote_copy(
            src_ref=ins[w].at[2 * chip[0] + chip[1]], dst_ref=outs[w].at[j],
            send_sem=sems[0].at[w, j], recv_sem=sems[1].at[w, j], device_id=(*chip, c), device_id_type=MESH)
            for w in range(len(names)) for j, chip in enumerate(chips)]

    def start(ins, outs, sems):
        for cp in copies(ins, outs, sems):
            cp.start()

    def finish(ins, outs, sems):
        for cp in copies(ins, outs, sems):
            cp.wait()

    return CommJob([p16[n] for n in names], [SDS((3,) + tuple(s[1:]), BF16) for s in shapes], False, start, finish,
                   [pltpu.SemaphoreType.DMA((len(names), 3))] * 2)


def half_exchange_job(full, names):
    shapes = [full[n].shape for n in names]

    def copies(outs, sems):
        x, y, c, _ = _place()
        cps = []
        for w in range(len(names)):
            Rh = shapes[w][0] // 2
            rows = outs[w].at[pl.ds(c * Rh, Rh), :]
            other = outs[w].at[pl.ds((1 - c) * Rh, Rh), :]
            cps.append((pltpu.make_async_remote_copy(src_ref=rows, dst_ref=rows, send_sem=sems[0].at[w], recv_sem=sems[1].at[w],
                                                     device_id=(x, y, 1 - c), device_id_type=MESH),
                        pltpu.make_async_remote_copy(src_ref=other, dst_ref=other, send_sem=sems[0].at[w], recv_sem=sems[1].at[w],
                                                     device_id=(x, y, 1 - c), device_id_type=MESH)))
        return cps

    def start(ins, outs, sems):
        for send, _ in copies(outs, sems):
            send.start()

    def finish(ins, outs, sems):
        for send, recv in copies(outs, sems):
            recv.wait_recv()
            send.wait_send()

    arrays = [full[n] for n in names]
    return CommJob(arrays, [SDS(a.shape, a.dtype) for a in arrays], True, start, finish,
                   [pltpu.SemaphoreType.DMA((len(names),))] * 2)


def allreduce_small(parts, name):
    n = len(parts)

    def body(*refs):
        ins, outs, bufs = refs[:n], refs[n:2 * n], refs[2 * n:3 * n]
        send, recv = refs[3 * n:]
        x, y, c, _ = _place()
        me = 4 * x + 2 * y + c
        sends = []
        for p in range(n):
            for k in range(1, 8):
                peer = (x ^ (k >> 2), y ^ ((k >> 1) & 1), c ^ (k & 1))
                cp = pltpu.make_async_remote_copy(src_ref=ins[p], dst_ref=bufs[p].at[k], send_sem=send.at[p, k - 1],
                                                  recv_sem=recv.at[p, k - 1], device_id=peer, device_id_type=MESH)
                cp.start()
                sends.append(cp)
            bufs[p][0] = ins[p][...]
        for cp in sends:
            cp.wait_recv()
        for p in range(n):
            acc = bufs[p][me]
            for d in range(1, 8):
                acc = acc + bufs[p][d ^ me]
            outs[p][...] = acc
        for cp in sends:
            cp.wait_send()

    vm = pl.BlockSpec(memory_space=pltpu.VMEM)
    return pl.pallas_call(
        body, name=name, in_specs=[vm] * n, out_specs=[vm] * n, out_shape=[SDS(a.shape, F32) for a in parts],
        scratch_shapes=[pltpu.VMEM((8,) + tuple(a.shape), F32) for a in parts] + [pltpu.SemaphoreType.DMA((n, 7))] * 2,
    )(*parts)


def adamw_multi(ws, gs, ms, vs, name):
    n = len(ws)

    def body(*refs):
        for p in range(n):
            w_ref, g_ref, m_ref, v_ref = (refs[q * n + p] for q in range(4))
            d_ref, nm_ref, nv_ref = (refs[(4 + q) * n + p] for q in range(3))
            gv = g_ref[...]
            mn = ADAM_B1 * m_ref[...] + (1.0 - ADAM_B1) * gv
            vn = ADAM_B2 * v_ref[...] + (1.0 - ADAM_B2) * (gv * gv)
            m_hat = mn / (1.0 - ADAM_B1 ** ADAM_STEP)
            v_hat = vn / (1.0 - ADAM_B2 ** ADAM_STEP)
            d_ref[...] = -ADAM_LR * (m_hat / (jnp.sqrt(v_hat) + ADAM_EPS) + ADAM_WD * w_ref[...])
            nm_ref[...] = mn
            nv_ref[...] = vn

    vm = pl.BlockSpec(memory_space=pltpu.VMEM)
    outs = pl.pallas_call(body, name=name, in_specs=[vm] * (4 * n), out_specs=[vm] * (3 * n),
                          out_shape=[SDS(a.shape, F32) for a in ws] * 3)(*ws, *gs, *ms, *vs)
    return outs[:n], outs[n:2 * n], outs[2 * n:]


def pair_sum(g, xh, sel_arr, name):
    _, R, C = g.shape
    Rh = R // 2

    def body(sel_ref, g_ref, x_ref, p32_ref, p16_ref):
        s = g_ref[...] + x_ref[...]
        p16_ref[...] = s.astype(BF16)

        @pl.when(pl.program_id(0) == sel_ref[0])
        def _():
            p32_ref[...] = s

    same = pl.BlockSpec((None, Rh, C), lambda s, sel: (s, 0, 0))
    grid_spec = pltpu.PrefetchScalarGridSpec(
        num_scalar_prefetch=1, grid=(4,),
        in_specs=[pl.BlockSpec((None, Rh, C), lambda s, sel: (s, sel[1], 0)), same],
        out_specs=[pl.BlockSpec((Rh, C), lambda s, sel: (0, 0)), same])
    return pl.pallas_call(body, grid_spec=grid_spec, name=name, out_shape=[SDS((Rh, C), F32), SDS((4, Rh, C), BF16)],
                          compiler_params=_cp("arbitrary"))(sel_arr, g, xh)


def chip_sum(p32, y, sel_arr, name):
    Rh, C = p32.shape

    def body(s_ref, p_ref, y_ref, o_ref):
        o_ref[...] = ((p_ref[...] + y_ref[0].astype(F32)) + y_ref[1].astype(F32)) + y_ref[2].astype(F32)

    grid_spec = pltpu.PrefetchScalarGridSpec(
        num_scalar_prefetch=1, grid=(1,),
        in_specs=[pl.BlockSpec((Rh, C), lambda i, s: (0, 0)), pl.BlockSpec((3, Rh, C), lambda i, s: (0, 0, 0))],
        out_specs=pl.BlockSpec((Rh, C), lambda i, s: (s[1], 0)))
    return pl.pallas_call(body, grid_spec=grid_spec, name=name, out_shape=SDS((2 * Rh, C), F32),
                          compiler_params=_cp("arbitrary"))(sel_arr, p32, y)


class StepComm:
    FIRST = ("w_in",)
    SQUARE = ("w_out", "w_xq", "w_xk", "w_xv", "w_xo")
    FFN = ("w_up", "w_down")
    PLAN = {
        "proj": ("gather", SQUARE[:4]), "attn_fwd": ("gather", SQUARE[4:] + FFN),
        "d_h3": ("pair", FFN), "mixer_post_bwd_a": ("pair", SQUARE), "attn_bwd": ("chip", FFN + SQUARE),
        "dw_in": ("half", FFN + SQUARE), "d_h1": ("pair", FIRST), "end_chip": ("chip", FIRST),
    }

    def __init__(self, placed, sel_arr):
        self.placed, self.sel_arr = placed, sel_arr
        self.g, self.p32, self.p16, self.full, self.done, self.jobs = {}, {}, {}, {}, {}, {}

    def gather_first(self):
        return dict(zip(self.FIRST, run_job("gather_first", gather_job(self.placed, self.FIRST))))

    def job(self, tag, gb=None):
        kind, names = self.PLAN[tag]
        if kind == "gather":
            j = gather_job(self.placed, names)
        elif kind == "pair":
            for n in names:
                self.g[n] = gb[n] if gb[n].ndim == 3 else gb[n].reshape(4, gb[n].shape[0] // 4, gb[n].shape[1])
            j = pair_exchange_job(self.g, names)
        elif kind == "chip":
            j = chip_exchange_job(self.p16, names)
        else:
            j = half_exchange_job(self.full, names)
        self.jobs[tag] = j
        return j

    def landed(self, tag, wb=None):
        kind, names = self.PLAN[tag]
        for n, a in zip(names, self.jobs[tag].landed):
            if kind == "gather":
                wb[n] = a
            elif kind == "pair":
                self.p32[n], self.p16[n] = pair_sum(self.g[n], a, self.sel_arr, "pair_sum_" + n)
            elif kind == "chip":
                self.full[n] = chip_sum(self.p32[n], a, self.sel_arr, "chip_sum_" + n)
            else:
                self.done[n] = a

    def finish(self, gb):
        return self.done

    def last_job(self):
        return self.job("end_chip")

    def last_landed(self):
        self.landed("end_chip")
        last = run_job("half_exchange", half_exchange_job(self.full, self.FIRST))
        self.done.update(zip(self.FIRST, last))


class NoComm:
    def job(self, tag, gb=None):
        return None

    def landed(self, tag, wb=None):
        pass

    def finish(self, gb):
        return gb


SMALL = ("rel_bias", "g_mix", "w_short_conv", "g_attn_out", "g_conv_out", "g_xattn", "g_mem", "g_ffn",
         "w_ffn_conv", "b_ffn_conv", "g_final")
SMALL_FULL = {"rel_bias": (8, 32), "g_mix": (1, 1024), "w_short_conv": (3, 512), "g_attn_out": (1, 512), "g_conv_out": (1, 512),
              "g_xattn": (1, 1024), "g_mem": (1, 1024), "g_ffn": (1, 1024), "w_ffn_conv": (3, 5632), "b_ffn_conv": (1, 5632),
              "g_final": (1024,)}
SMALL_SHARDED = ("w_short_conv", "w_ffn_conv")


def _pack(parts):
    flat = jnp.concatenate([p.reshape(-1) for p in parts])
    rows = -(-flat.shape[0] // 1024) * 8
    return jnp.pad(flat, (0, rows * 128 - flat.shape[0])).reshape(rows, 128)


def _unpack(packed, names):
    flat, out, off = packed.reshape(-1), {}, 0
    for n in names:
        size = int(np.prod(SMALL_FULL[n]))
        out[n] = flat[off:off + size].reshape(SMALL_FULL[n])
        off += size
    return out


def kernel(x, mem, rel_bias, g_mix, w_in, w_short_conv, g_attn_out, g_conv_out, w_out, g_xattn, g_mem, w_xq, w_xk, w_xv, w_xo, g_ffn, w_up, w_ffn_conv, b_ffn_conv, w_down, g_final, loss_target, m_rel_bias, m_g_mix, m_w_in, m_w_short_conv, m_g_attn_out, m_g_conv_out, m_w_out, m_g_xattn, m_g_mem, m_w_xq, m_w_xk, m_w_xv, m_w_xo, m_g_ffn, m_w_up, m_w_ffn_conv, m_b_ffn_conv, m_w_down, m_g_final, v_rel_bias, v_g_mix, v_w_in, v_w_short_conv, v_g_attn_out, v_g_conv_out, v_w_out, v_g_xattn, v_g_mem, v_w_xq, v_w_xk, v_w_xv, v_w_xo, v_g_ffn, v_w_up, v_w_ffn_conv, v_b_ffn_conv, v_w_down, v_g_final):
    names = ("rel_bias", "g_mix", "w_in", "w_short_conv", "g_attn_out", "g_conv_out", "w_out", "g_xattn", "g_mem", "w_xq",
             "w_xk", "w_xv", "w_xo", "g_ffn", "w_up", "w_ffn_conv", "b_ffn_conv", "w_down", "g_final")
    W = dict(zip(names, (rel_bias, g_mix, w_in, w_short_conv, g_attn_out, g_conv_out, w_out, g_xattn, g_mem, w_xq, w_xk, w_xv,
                         w_xo, g_ffn, w_up, w_ffn_conv, b_ffn_conv, w_down, g_final)))
    M = dict(zip(names, (m_rel_bias, m_g_mix, m_w_in, m_w_short_conv, m_g_attn_out, m_g_conv_out, m_w_out, m_g_xattn, m_g_mem,
                         m_w_xq, m_w_xk, m_w_xv, m_w_xo, m_g_ffn, m_w_up, m_w_ffn_conv, m_b_ffn_conv, m_w_down, m_g_final)))
    V = dict(zip(names, (v_rel_bias, v_g_mix, v_w_in, v_w_short_conv, v_g_attn_out, v_g_conv_out, v_w_out, v_g_xattn, v_g_mem,
                         v_w_xq, v_w_xk, v_w_xv, v_w_xo, v_g_ffn, v_w_up, v_w_ffn_conv, v_b_ffn_conv, v_w_down, v_g_final)))
    xi, yi, ci = lax.axis_index("x"), lax.axis_index("y"), lax.axis_index("c")
    mine = 2 * xi + yi
    mine_arr = jnp.reshape(mine, (1,)).astype(jnp.int32)
    sel_arr = jnp.stack([mine, ci]).astype(jnp.int32)
    comm = StepComm({n: place_shard(W[n][0], mine_arr, n) for n in BIG}, sel_arr)
    first = jnp.where(ci == 0, 1.0, 0.0).astype(F32)
    emb = []
    for n in SMALL_SHARDED:
        shard = W[n][0]
        full = jnp.zeros(SMALL_FULL[n], F32)
        emb.append(lax.dynamic_update_slice(full, shard * first, (0, mine * shard.shape[1])))
    ws = {n: W[n] for n in SMALL if n not in SMALL_SHARDED}
    ws.update(zip(SMALL_SHARDED, allreduce_small(emb, "gather_small")))

    loss, grad_x, gfull, gs = local_step(x[0], mem[0], loss_target[0], None, ws, comm)

    def as2d(a):
        return a.reshape(1, -1) if a.ndim == 1 else a

    reduced = allreduce_small([as2d(gs[n]) for n in SMALL] + [loss], "reduce_small")
    gsm = dict(zip(SMALL, reduced[:-1]))
    loss = reduced[-1][0, 0]

    grads, delta, new_m, new_v = {}, {}, {}, {}
    for n in ("w_up",) + tuple(n for n in BIG if n != "w_up"):
        first_update = n == "w_up"
        d, nm, nv = adamw(W[n][0], gfull[n], M[n][0], V[n][0], "adamw_" + n, job=comm.last_job() if first_update else None)
        if first_update:
            comm.last_landed()
        grads[n], delta[n], new_m[n], new_v[n] = gfull[n][None], d[None], nm[None], nv[None]
    for n in SMALL_SHARDED:
        wid = W[n].shape[2]
        gsm[n] = lax.dynamic_slice(gsm[n], (0, mine * wid), (3, wid))

    def own(a, n):
        return a[0] if n in SMALL_SHARDED else as2d(a)

    d, nm, nv = adamw_multi([own(W[n], n) for n in SMALL], [gsm[n] for n in SMALL], [own(M[n], n) for n in SMALL],
                            [own(V[n], n) for n in SMALL], "adamw_small")
    for i, n in enumerate(SMALL):
        shape = W[n].shape
        grads[n], delta[n], new_m[n], new_v[n] = (a.reshape(shape) for a in (gsm[n], d[i], nm[i], nv[i]))
    return (loss, grad_x[None], *[grads[n] for n in names], *[delta[n] for n in names],
            *[new_m[n] for n in names], *[new_v[n] for n in names])
```

```python
import functools
import math

import numpy as np
import jax
import jax.numpy as jnp
from jax import lax
from jax.experimental import pallas as pl
from jax.experimental.pallas import tpu as pltpu

F32 = jnp.float32
BF16 = jnp.bfloat16
SDS = jax.ShapeDtypeStruct
MESH = pl.DeviceIdType.MESH

D_MODEL = 1024
ATTN_W = 512
N_HEADS = 8
BLK = 128
PATTERNS = ((128, 1), (512, 4), (2048, 16))
N_BUCKETS = 32
D_FF = 2816
N_MEM_HEADS = 4
MEM_HD = 256
EPS = 1e-6
NEG = -1e30
VMEM_LIMIT = 56 * 1024 * 1024

ADAM_LR, ADAM_B1, ADAM_B2, ADAM_EPS, ADAM_WD, ADAM_STEP = 0.001, 0.9, 0.999, 1e-08, 0.01, 10


def _cp(*sem):
    return pltpu.CompilerParams(dimension_semantics=sem, vmem_limit_bytes=VMEM_LIMIT)


def _dot(a, b):
    return jnp.dot(a, b, preferred_element_type=F32)


def _dot_nt(a, b):
    return lax.dot_general(a, b, (((1,), (1,)), ((), ())), preferred_element_type=F32)


def _dot_tn(a, b):
    return lax.dot_general(a, b, (((0,), (0,)), ((), ())), preferred_element_type=F32)


def _rsum(x):
    return jnp.sum(x, axis=1, keepdims=True)


def rmsnorm_fwd(x, g, name):
    S, Dm = x.shape
    tm = min(S, 512)

    def body(x_ref, g_ref, o_ref):
        xv = x_ref[...]
        r = lax.rsqrt(jnp.mean(xv * xv, axis=1, keepdims=True) + EPS)
        o_ref[...] = (xv * r * g_ref[...]).astype(o_ref.dtype)

    return pl.pallas_call(
        body, grid=(S // tm,), name=name,
        in_specs=[pl.BlockSpec((tm, Dm), lambda i: (i, 0)), pl.BlockSpec((1, Dm), lambda i: (0, 0))],
        out_specs=pl.BlockSpec((tm, Dm), lambda i: (i, 0)),
        out_shape=SDS((S, Dm), BF16), compiler_params=_cp("parallel"))(x, g)


def rmsnorm_bwd(x, g, dh, res, name, job=None):
    S, Dm = x.shape
    tm = min(S, 512)
    want_dx = res is not None

    def body(*refs):
        if want_dx:
            x_ref, g_ref, dh_ref, res_ref, dx_ref, dg_ref = refs
        else:
            x_ref, g_ref, dh_ref, dg_ref = refs
        i = pl.program_id(0)
        xv = x_ref[...]
        r = lax.rsqrt(jnp.mean(xv * xv, axis=1, keepdims=True) + EPS)
        xh = xv * r
        dh = dh_ref[...].astype(F32)
        if want_dx:
            gd = dh * g_ref[...]
            dx_ref[...] = res_ref[...] + r * (gd - xh * jnp.mean(gd * xh, axis=1, keepdims=True))

        @pl.when(i == 0)
        def _():
            dg_ref[...] = jnp.zeros_like(dg_ref)

        dg_ref[...] += jnp.sum(dh * xh, axis=0, keepdims=True)

    row = pl.BlockSpec((tm, Dm), lambda i: (i, 0))
    vec = pl.BlockSpec((1, Dm), lambda i: (0, 0))
    if want_dx:
        return _pcall(body, [x, g, dh, res], grid=(S // tm,), name=name, in_specs=[row, vec, row, row], out_specs=[row, vec],
                      out_shape=[SDS((S, Dm), F32), SDS((1, Dm), F32)], sem=("arbitrary",), job=job)
    return pl.pallas_call(
        body, grid=(S // tm,), name=name, in_specs=[row, vec, row], out_specs=vec,
        out_shape=SDS((1, Dm), F32), compiler_params=_cp("arbitrary"))(x, g, dh)


def _col_spec(bw, tn, rows, row_of, col_of):
    if bw is None:
        return pl.BlockSpec((rows, tn), lambda *g: (row_of(*g), col_of(*g)))
    if tn % bw == 0:
        return pl.BlockSpec((tn // bw, rows, bw), lambda *g: (col_of(*g), row_of(*g), 0))
    per = bw // tn
    return pl.BlockSpec((None, rows, tn), lambda *g: (col_of(*g) // per, row_of(*g), col_of(*g) % per))


def _read_cols(ref, bw, tn):
    if bw is None or tn % bw != 0:
        return ref[...]
    if tn == bw:
        return ref[0]
    return jnp.concatenate([ref[q] for q in range(tn // bw)], axis=1)


def _write_cols(ref, bw, tn, val):
    if bw is None or tn % bw != 0:
        ref[...] = val.astype(ref.dtype)
    else:
        for q in range(tn // bw):
            ref[q] = val[:, q * bw:(q + 1) * bw].astype(ref.dtype)


def mm_nn(a, b, name, *, res=None, out_dtype=F32, out_bw=None, tm=1024, tn=512, job=None):
    M, K = a.shape
    N = b.shape[1]
    tm = min(tm, M)

    def body(*refs):
        if res is None:
            a_ref, b_ref, o_ref = refs
        else:
            a_ref, b_ref, r_ref, o_ref = refs
        acc = _dot(a_ref[...].astype(BF16), b_ref[...])
        if res is not None:
            acc = acc + r_ref[...]
        _write_cols(o_ref, out_bw, tn, acc)

    ri, ci = (lambda i, j: i), (lambda i, j: j)
    in_specs = [pl.BlockSpec((tm, K), lambda i, j: (i, 0)), pl.BlockSpec((K, tn), lambda i, j: (0, j))]
    args = [a, b]
    if res is not None:
        in_specs.append(pl.BlockSpec((tm, tn), lambda i, j: (i, j)))
        args.append(res)
    oshape = (M, N) if out_bw is None else (N // out_bw, M, out_bw)
    return _pcall(body, args, grid=(M // tm, N // tn), name=name, in_specs=in_specs,
                  out_specs=[_col_spec(out_bw, tn, tm, ri, ci)], out_shape=[SDS(oshape, out_dtype)],
                  sem=("parallel", "parallel"), job=job)[0]


def mm_nt(a, a_bw, b, name, *, res=None, out_dtype=F32, tm=1024, tn=512, job=None):
    if a_bw is None:
        M, K = a.shape
    else:
        M, K = a.shape[1], a.shape[0] * a_bw
    N = b.shape[0]
    tm = min(tm, M)

    def body(*refs):
        if res is None:
            a_ref, b_ref, o_ref = refs
        else:
            a_ref, b_ref, r_ref, o_ref = refs
        av = _read_cols(a_ref, a_bw, K).astype(BF16)
        acc = _dot_nt(av, b_ref[...])
        if res is not None:
            acc = acc + r_ref[...]
        o_ref[...] = acc.astype(o_ref.dtype)

    in_specs = [_col_spec(a_bw, K, tm, lambda i, j: i, lambda i, j: 0), pl.BlockSpec((tn, K), lambda i, j: (j, 0))]
    args = [a, b]
    if res is not None:
        in_specs.append(pl.BlockSpec((tm, tn), lambda i, j: (i, j)))
        args.append(res)
    return _pcall(body, args, grid=(M // tm, N // tn), name=name, in_specs=in_specs,
                  out_specs=[pl.BlockSpec((tm, tn), lambda i, j: (i, j))], out_shape=[SDS((M, N), out_dtype)],
                  sem=("parallel", "parallel"), job=job)[0]


def mm_nt_norm_bwd(a, a_bw, b, x, g, res, name, *, tm=512, job=None):
    if a_bw is None:
        M, K = a.shape
    else:
        M, K = a.shape[1], a.shape[0] * a_bw
    N = b.shape[0]

    def body(a_ref, b_ref, x_ref, g_ref, r_ref, dx_ref, dg_ref):
        dh = _dot_nt(_read_cols(a_ref, a_bw, K).astype(BF16), b_ref[...])
        xv = x_ref[...]
        r = lax.rsqrt(jnp.mean(xv * xv, axis=1, keepdims=True) + EPS)
        xh = xv * r
        gd = dh * g_ref[...]
        dx_ref[...] = r_ref[...] + r * (gd - xh * jnp.mean(gd * xh, axis=1, keepdims=True))

        @pl.when(pl.program_id(0) == 0)
        def _():
            dg_ref[...] = jnp.zeros_like(dg_ref)

        dg_ref[...] += jnp.sum(dh * xh, axis=0, keepdims=True)

    row = pl.BlockSpec((tm, N), lambda i: (i, 0))
    vec = pl.BlockSpec((1, N), lambda i: (0, 0))
    return _pcall(body, [a, b, x, g, res], grid=(M // tm,), name=name,
                  in_specs=[_col_spec(a_bw, K, tm, lambda i: i, lambda i: 0), pl.BlockSpec((N, K), lambda i: (0, 0)), row, vec, row],
                  out_specs=[row, vec], out_shape=[SDS((M, N), F32), SDS((1, N), F32)], sem=("arbitrary",), job=job)


def mm_tn(a, b, b_bw, name, *, shards=None, tm=1024, tn=1024, ts=1024, job=None):
    S, Ka = a.shape
    N = b.shape[1] if b_bw is None else b.shape[0] * b_bw
    ts = min(ts, S)
    tm = min(tm, Ka)

    def body(a_ref, b_ref, o_ref):
        @pl.when(pl.program_id(2) == 0)
        def _():
            o_ref[...] = jnp.zeros_like(o_ref)

        bv = _read_cols(b_ref, b_bw, tn).astype(BF16)
        o_ref[...] += _dot_tn(a_ref[...].astype(BF16), bv)

    in_specs = [pl.BlockSpec((ts, tm), lambda i, j, k: (k, i)),
                _col_spec(b_bw, tn, ts, lambda i, j, k: k, lambda i, j, k: j)]
    if shards is None:
        out_spec, oshape = pl.BlockSpec((tm, tn), lambda i, j, k: (i, j)), (Ka, N)
    else:
        per = (N // shards) // tn
        out_spec = pl.BlockSpec((None, tm, tn), lambda i, j, k: (j // per, i, j % per))
        oshape = (shards, Ka, N // shards)
    return _pcall(body, [a, b], grid=(Ka // tm, N // tn, S // ts), name=name, in_specs=in_specs, out_specs=[out_spec],
                  out_shape=[SDS(oshape, F32)], sem=("parallel", "parallel", "arbitrary"), job=job)[0]


def _bucket_tables():
    out = np.zeros((3, 2, BLK, BLK), np.int32)
    qi = np.arange(BLK)[:, None]
    for p, (win, dil) in enumerate(PATTERNS):
        w = win // dil
        assert w == BLK
        for half in range(2):
            kj = np.arange(BLK)[None, :] + half * BLK
            steps = qi + w - kj
            valid = (steps >= 0) & (steps <= w)
            dist = np.clip(steps, 0, w) * dil
            dd = np.maximum(dist, 1).astype(np.float32)
            large = 16 + (np.log(dd / np.float32(16)) / np.float32(math.log(2048 / 16)) * np.float32(16)).astype(np.int32)
            large = np.minimum(large, N_BUCKETS - 1)
            out[p, half] = np.where(valid, np.where(dist < 16, dist, large), -1)
    return out


def bias_tables(rel_bias):
    bkt = jnp.asarray(_bucket_tables())

    def body(rb_ref, bkt_ref, o_ref):
        h = pl.program_id(1)
        for half in range(2):
            bk = bkt_ref[half]
            acc = jnp.full((BLK, BLK), NEG, F32)
            for b in range(N_BUCKETS):
                acc = jnp.where(bk == b, rb_ref[h, b], acc)
            o_ref[half] = acc

    return pl.pallas_call(
        body, grid=(3, N_HEADS), name="bias_tables",
        in_specs=[pl.BlockSpec(memory_space=pltpu.SMEM),
                  pl.BlockSpec((None, 2, BLK, BLK), lambda p, h: (p, 0, 0, 0))],
        out_specs=pl.BlockSpec((None, None, 2, BLK, BLK), lambda p, h: (p, h, 0, 0, 0)),
        out_shape=SDS((3, N_HEADS, 2, BLK, BLK), F32), compiler_params=_cp("parallel", "parallel"))(rel_bias, bkt)


def bias_tables_bwd(ds_sum):
    bkt = jnp.asarray(_bucket_tables())

    def body(ds_ref, bkt_ref, o_ref):
        h = pl.program_id(0)

        @pl.when(h == 0)
        def _():
            o_ref[...] = jnp.zeros_like(o_ref)

        rows = lax.broadcasted_iota(jnp.int32, (N_HEADS, N_BUCKETS), 0)
        cols = lax.broadcasted_iota(jnp.int32, (N_HEADS, N_BUCKETS), 1)
        acc = jnp.zeros((N_HEADS, N_BUCKETS), F32)
        for b in range(N_BUCKETS):
            t = jnp.zeros((BLK, BLK), F32)
            for p in range(3):
                for half in range(2):
                    t = t + jnp.where(bkt_ref[p, half] == b, ds_ref[p, half], 0.0)
            tot = jnp.sum(_rsum(t), axis=0, keepdims=True)
            acc = acc + jnp.where((rows == h) & (cols == b), tot, 0.0)
        o_ref[...] += acc

    return pl.pallas_call(
        body, grid=(N_HEADS,), name="bias_tables_bwd",
        in_specs=[pl.BlockSpec((3, None, 2, BLK, BLK), lambda h: (0, h, 0, 0, 0)),
                  pl.BlockSpec((3, 2, BLK, BLK), lambda h: (0, 0, 0, 0))],
        out_specs=pl.BlockSpec((N_HEADS, N_BUCKETS), lambda h: (0, 0)),
        out_shape=SDS((N_HEADS, N_BUCKETS), F32), compiler_params=_cp("arbitrary"))(ds_sum, bkt)


def _rows(start, dil):
    if dil == 1:
        return pl.ds(pl.multiple_of(start, BLK), BLK)
    return pl.ds(start, BLK, stride=dil)


GRP = 8


def _group_rows(i, dil, S):
    nb = S // (BLK * dil)
    run = min(nb, GRP)
    chunks = nb // run
    res0, b0 = (i // chunks) * (GRP // run), (i % chunks) * run
    cur = [(b0 + j % run) * (BLK * dil) + res0 + j // run for j in range(GRP)]
    t = lax.broadcasted_iota(jnp.int32, (GRP, 1, 1), 0)
    if run == nb:
        before, pen = None, jnp.where(t % run == 0, NEG, 0.0)
    else:
        before = _rows(jnp.maximum(b0 - 1, 0) * (BLK * dil) + res0, dil)
        pen = jnp.where((t == 0) & (b0 == 0), NEG, 0.0)
    return [_rows(s, dil) for s in cur], before, pen


def _load_group(ref, rows):
    return jnp.stack([ref[r, :] for r in rows])


def _with_prev(ref, before, cur_blocks):
    first = cur_blocks[:1] if before is None else ref[before, :][None].astype(cur_blocks.dtype)
    return jnp.concatenate([first, cur_blocks[:-1]], axis=0)


def _bdot_nt(a, b):
    return lax.dot_general(a, b, (((2,), (2,)), ((0,), (0,))), preferred_element_type=F32)


def _bdot(a, b):
    return lax.dot_general(a, b, (((2,), (1,)), ((0,), (0,))), preferred_element_type=F32)


def _bdot_tn(a, b):
    return lax.dot_general(a, b, (((1,), (1,)), ((0,), (0,))), preferred_element_type=F32)


def _lsum(x):
    return jnp.sum(x, axis=-1, keepdims=True)


def _widen(src, dst):
    S = src.shape[1]

    def chunk(i, carry):
        rows = pl.ds(pl.multiple_of(i * 512, 512), 512)
        for a in range(3):
            dst[a, rows, :] = src[a, rows, :].astype(F32)
        return carry

    lax.fori_loop(0, S // 512, chunk, 0)


def attn_fwd(qkv, bias, job=None):
    S = qkv.shape[3]
    nblk = S // BLK
    car = Carried(job, 2, 2, 1, 4)

    def body(*refs):
        (qkv_in, bias_ref, o_ref, lse_ref, qkv_ref), jrefs = car.split(refs)
        car.phase(0, pl.program_id(0), jrefs)
        _widen(qkv_in, qkv_ref)
        lane = lax.broadcasted_iota(jnp.int32, (GRP, BLK, BLK), 2)
        lo = lane < 64
        masks = (lo, jnp.logical_not(lo))
        q_ref, k_ref, v_ref = qkv_ref.at[0], qkv_ref.at[1], qkv_ref.at[2]
        for p, (_, dil) in enumerate(PATTERNS):
            def step(i, carry, p=p, dil=dil):
                rc, before, pen = _group_rows(i, dil, S)
                q2 = _load_group(q_ref, rc) * 0.125
                kc = _load_group(k_ref, rc).astype(BF16)
                kp = _with_prev(k_ref, before, kc)
                vc = _load_group(v_ref, rc)
                vp = _with_prev(v_ref, before, vc)
                if p > 0:
                    acc_old, ml_old = _load_group(o_ref, rc), _load_group(lse_ref, rc)
                pv = jnp.zeros((GRP, BLK, BLK), F32)
                m_new, l_new, alpha = [], [], []
                for h in range(2):
                    qh = jnp.where(masks[h], q2, 0.0).astype(BF16)
                    sp = _bdot_nt(qh, kp) + bias_ref[p, h, 0][None] + pen
                    sc = _bdot_nt(qh, kc) + bias_ref[p, h, 1][None]
                    mn = jnp.max(jnp.maximum(sp, sc), axis=-1, keepdims=True)
                    if p > 0:
                        mo = ml_old[:, :, 64 * h:64 * h + 1]
                        mn = jnp.maximum(mn, mo)
                        al = jnp.exp(mo - mn)
                        alpha.append(al)
                    pp = jnp.exp(sp - mn)
                    pc = jnp.exp(sc - mn)
                    ln = _lsum(pp + pc)
                    if p > 0:
                        ln = ln + al * ml_old[:, :, 64 * h + 32:64 * h + 33]
                    vhp = jnp.where(masks[h], vp, 0.0).astype(BF16)
                    vhc = jnp.where(masks[h], vc, 0.0).astype(BF16)
                    pv = pv + _bdot(pp.astype(BF16), vhp) + _bdot(pc.astype(BF16), vhc)
                    m_new.append(mn)
                    l_new.append(ln)
                if p > 0:
                    pv = pv + acc_old * jnp.where(lo, alpha[0], alpha[1])
                ml_t = jnp.where(lo, jnp.where(lane < 32, m_new[0], l_new[0]), jnp.where(lane < 96, m_new[1], l_new[1]))
                for j, r in enumerate(rc):
                    o_ref[r, :] = pv[j]
                    lse_ref[r, :] = ml_t[j]
                return carry

            lax.fori_loop(0, nblk // GRP, step, 0)

        def fin(i, carry):
            rows = pl.ds(pl.multiple_of(i * 512, 512), 512)
            ml = lse_ref[rows, :]
            is_m = (lax.broadcasted_iota(jnp.int32, ml.shape, 1) & 32) == 0
            den = jnp.where(is_m, pltpu.roll(ml, 96, 1), ml)
            o_ref[rows, :] = o_ref[rows, :] / den
            lse_ref[rows, :] = ml + jnp.log(den)
            return carry

        lax.fori_loop(0, S // 512, fin, 0)
        car.phase(1, pl.program_id(0), jrefs)
        car.phase(2, pl.program_id(0), jrefs)

    outs = pl.pallas_call(
        body, grid=(4,), name="attn_fwd",
        in_specs=[pl.BlockSpec((None, 3, None, S, BLK), lambda g: (0, 0, g, 0, 0)),
                  pl.BlockSpec((3, 2, 2, BLK, BLK), lambda g: (0, g, 0, 0, 0))] + car.in_specs(),
        out_specs=[pl.BlockSpec((None, S, BLK), lambda g: (g, 0, 0)),
                   pl.BlockSpec((None, S, BLK), lambda g: (g, 0, 0))] + car.out_specs(),
        out_shape=[SDS((4, S, BLK), F32), SDS((4, S, BLK), F32)] + car.out_shapes(),
        input_output_aliases=car.aliases(),
        scratch_shapes=[pltpu.VMEM((3, S, BLK), F32)] + car.sems(),
        compiler_params=_cp("arbitrary"))(qkv, bias, *car.args())
    if job is not None:
        job.landed = list(outs[2:])
    return outs[0], outs[1]


def attn_bwd(qkv, bias, o, lse, do, job=None):
    S = qkv.shape[3]
    nblk = S // BLK
    car = Carried(job, 5, 2, 2, 4)

    def body(*refs):
        (qkv_in, bias_ref, o_ref, lse_ref, do_ref, dqkv_out, ds_ref, qkv_ref, dqkv_ref), jrefs = car.split(refs)
        car.phase(0, pl.program_id(0), jrefs)
        _widen(qkv_in, qkv_ref)
        lane = lax.broadcasted_iota(jnp.int32, (GRP, BLK, BLK), 2)
        lo = lane < 64
        masks = (lo, jnp.logical_not(lo))
        dqkv_ref[...] = jnp.zeros_like(dqkv_ref)
        ds_ref[...] = jnp.zeros_like(ds_ref)
        q_ref, k_ref, v_ref = qkv_ref.at[0], qkv_ref.at[1], qkv_ref.at[2]
        dq_ref, dk_ref, dv_ref = dqkv_ref.at[0], dqkv_ref.at[1], dqkv_ref.at[2]
        for p, (_, dil) in enumerate(PATTERNS):
            def step(i, carry, p=p, dil=dil):
                rc, before, pen = _group_rows(i, dil, S)
                q2 = _load_group(q_ref, rc) * 0.125
                kc = _load_group(k_ref, rc).astype(BF16)
                kp = _with_prev(k_ref, before, kc)
                vc = _load_group(v_ref, rc).astype(BF16)
                vp = _with_prev(v_ref, before, vc)
                dot = _load_group(do_ref, rc)
                lset = _load_group(lse_ref, rc)
                prod = dot * _load_group(o_ref, rc)
                zero = jnp.zeros((GRP, BLK, BLK), F32)
                dq, dkc, dkp, dvc, dvp = zero, zero, zero, zero, zero
                for h in range(2):
                    qh = jnp.where(masks[h], q2, 0.0).astype(BF16)
                    doh = jnp.where(masks[h], dot, 0.0).astype(BF16)
                    delta = _lsum(jnp.where(masks[h], prod, 0.0))
                    lh = lset[:, :, 64 * h:64 * h + 1]
                    pp = jnp.exp(_bdot_nt(qh, kp) + bias_ref[p, h, 0][None] + pen - lh)
                    pc = jnp.exp(_bdot_nt(qh, kc) + bias_ref[p, h, 1][None] - lh)
                    dsp = pp * (_bdot_nt(doh, vp) - delta)
                    dsc = pc * (_bdot_nt(doh, vc) - delta)
                    ds_ref[p, h, 0] += jnp.sum(dsp, axis=0)
                    ds_ref[p, h, 1] += jnp.sum(dsc, axis=0)
                    dspb, dscb = dsp.astype(BF16), dsc.astype(BF16)
                    dq = dq + jnp.where(masks[h], _bdot(dspb, kp) + _bdot(dscb, kc), 0.0)
                    dkp = dkp + _bdot_tn(dspb, qh)
                    dkc = dkc + _bdot_tn(dscb, qh)
                    dvp = dvp + _bdot_tn(pp.astype(BF16), doh)
                    dvc = dvc + _bdot_tn(pc.astype(BF16), doh)
                none = jnp.zeros((1, BLK, BLK), F32)
                dkc = dkc + jnp.concatenate([dkp[1:], none], axis=0)
                dvc = dvc + jnp.concatenate([dvp[1:], none], axis=0)
                for j in range(GRP):
                    dq_ref[rc[j], :] += dq[j] * 0.125
                    dk_ref[rc[j], :] += dkc[j]
                    dv_ref[rc[j], :] += dvc[j]
                if before is not None:
                    dk_ref[before, :] += dkp[0]
                    dv_ref[before, :] += dvp[0]
                return carry

            lax.fori_loop(0, nblk // GRP, step, 0)

        def narrow(i, carry):
            rows = pl.ds(pl.multiple_of(i * 512, 512), 512)
            for a in range(3):
                dqkv_out[a, rows, :] = dqkv_ref[a, rows, :].astype(BF16)
            return carry

        lax.fori_loop(0, S // 512, narrow, 0)
        car.phase(1, pl.program_id(0), jrefs)
        car.phase(2, pl.program_id(0), jrefs)

    blk = pl.BlockSpec((None, S, BLK), lambda g: (g, 0, 0))
    outs = pl.pallas_call(
        body, grid=(4,), name="attn_bwd",
        in_specs=[pl.BlockSpec((None, 3, None, S, BLK), lambda g: (0, 0, g, 0, 0)),
                  pl.BlockSpec((3, 2, 2, BLK, BLK), lambda g: (0, g, 0, 0, 0)), blk, blk, blk] + car.in_specs(),
        out_specs=[pl.BlockSpec((None, 3, None, S, BLK), lambda g: (0, 0, g, 0, 0)),
                   pl.BlockSpec((3, 2, 2, BLK, BLK), lambda g: (0, g, 0, 0, 0))] + car.out_specs(),
        out_shape=[SDS((2, 3, 4, S, BLK), BF16), SDS((3, N_HEADS, 2, BLK, BLK), F32)] + car.out_shapes(),
        input_output_aliases=car.aliases(),
        scratch_shapes=[pltpu.VMEM((3, S, BLK), F32), pltpu.VMEM((3, S, BLK), F32)] + car.sems(),
        compiler_params=_cp("arbitrary"))(qkv, bias, o, lse, do, *car.args())
    if job is not None:
        job.landed = list(outs[2:])
    return outs[0], outs[1]


def _shift_down(u, k, halo):
    n = u.shape[0]
    row = lax.broadcasted_iota(jnp.int32, u.shape, 0)
    out = pltpu.roll(u, k, 0)
    hn = halo.shape[0]
    for j in range(k):
        out = jnp.where(row == j, halo[hn - k + j:hn - k + j + 1, :], out)
    return out


def _shift_up(u, k, halo):
    n = u.shape[0]
    row = lax.broadcasted_iota(jnp.int32, u.shape, 0)
    out = pltpu.roll(u, n - k, 0)
    for j in range(k):
        out = jnp.where(row == n - k + j, halo[j:j + 1, :], out)
    return out


def _conv3(u, halo, w0, w1, w2):
    return _shift_down(u, 2, halo) * w0 + _shift_down(u, 1, halo) * w1 + u * w2


def _colsum(x):
    return jnp.sum(x, axis=0, keepdims=True)


def _mixer_specs(S, tm):
    conv = pl.BlockSpec((None, 12, tm, BLK), lambda i: (1, 0, i, 0))
    halo = pl.BlockSpec((None, 12, 16, BLK), lambda i: (1, 0, jnp.maximum(i * (tm // 16) - 1, 0), 0))
    ob = pl.BlockSpec((4, tm, BLK), lambda i: (0, i, 0))
    return conv, halo, ob


def _mixer_recompute(i, o_ref, pr_ref, ph_ref, w_ref):
    ob = [o_ref[q] for q in range(4)]
    gb = [pr_ref[q].astype(F32) for q in range(4)]
    gc = [pr_ref[4 + q].astype(F32) for q in range(4)]
    xi = [pr_ref[8 + q].astype(F32) for q in range(4)]
    keep = jnp.where(i > 0, 1.0, 0.0)
    u = [gc[q] * xi[q] for q in range(4)]
    hu = [ph_ref[4 + q].astype(F32) * ph_ref[8 + q].astype(F32) * keep for q in range(4)]
    w = [[w_ref[k:k + 1, q * BLK:(q + 1) * BLK] for k in range(3)] for q in range(4)]
    cv = [_conv3(u[q], hu[q], *w[q]) for q in range(4)]
    return ob, gb, gc, xi, u, hu, cv, w


def _rms_blocks(blocks):
    ss = sum(_rsum(b * b) for b in blocks)
    return lax.rsqrt(ss / (BLK * len(blocks)) + EPS)


def mixer_post_fwd(o, proj, w_sc, g_a, g_c):
    S = o.shape[1]
    tm = 512

    def body(o_ref, pr_ref, ph_ref, w_ref, ga_ref, gc_ref, m_ref):
        i = pl.program_id(0)
        ob, gb, _, _, _, _, cv, _ = _mixer_recompute(i, o_ref, pr_ref, ph_ref, w_ref)
        conv = [gb[q] * cv[q] for q in range(4)]
        ra, rc = _rms_blocks(ob), _rms_blocks(conv)
        for q in range(4):
            sl = slice(q * BLK, (q + 1) * BLK)
            m_ref[:, q * BLK:(q + 1) * BLK] = (ob[q] * ra * ga_ref[:, sl]).astype(BF16)
            m_ref[:, ATTN_W + q * BLK:ATTN_W + (q + 1) * BLK] = (conv[q] * rc * gc_ref[:, sl]).astype(BF16)

    conv_s, halo_s, ob_s = _mixer_specs(S, tm)
    full = lambda r, c: pl.BlockSpec((r, c), lambda i: (0, 0))
    return pl.pallas_call(
        body, grid=(S // tm,), name="mixer_post_fwd",
        in_specs=[ob_s, conv_s, halo_s, full(3, 512), full(1, 512), full(1, 512)],
        out_specs=pl.BlockSpec((tm, D_MODEL), lambda i: (i, 0)), out_shape=SDS((S, D_MODEL), BF16),
        compiler_params=_cp("parallel"))(o, proj, proj, w_sc, g_a, g_c)


def mixer_post_bwd_a(dmixed, o, proj, w_sc, g_a, g_c, job=None):
    S = o.shape[1]
    tm = 512

    def body(dm_ref, o_ref, pr_ref, ph_ref, w_ref, ga_ref, gc_ref, do_ref, dgb_ref, dcv_ref, dga_ref, dgc_ref):
        i = pl.program_id(0)
        ob, gb, _, _, _, _, cv, _ = _mixer_recompute(i, o_ref, pr_ref, ph_ref, w_ref)
        conv = [gb[q] * cv[q] for q in range(4)]
        ra, rc = _rms_blocks(ob), _rms_blocks(conv)

        @pl.when(i == 0)
        def _():
            dga_ref[...] = jnp.zeros_like(dga_ref)
            dgc_ref[...] = jnp.zeros_like(dgc_ref)

        for blocks, r, g_ref, off, dg_ref, is_attn in ((ob, ra, ga_ref, 0, dga_ref, True), (conv, rc, gc_ref, ATTN_W, dgc_ref, False)):
            xh = [blocks[q] * r for q in range(4)]
            dm = [dm_ref[:, off + q * BLK:off + (q + 1) * BLK].astype(F32) for q in range(4)]
            gd = [dm[q] * g_ref[:, q * BLK:(q + 1) * BLK] for q in range(4)]
            mean = sum(_rsum(gd[q] * xh[q]) for q in range(4)) / (4 * BLK)
            for q in range(4):
                dg_ref[:, q * BLK:(q + 1) * BLK] += _colsum(dm[q] * xh[q])
                dx = r * (gd[q] - xh[q] * mean)
                if is_attn:
                    do_ref[q] = dx
                else:
                    dgb_ref[q] = dx * cv[q]
                    dcv_ref[q] = dx * gb[q]

    conv_s, halo_s, ob_s = _mixer_specs(S, tm)
    full = lambda r, c: pl.BlockSpec((r, c), lambda i: (0, 0))
    return _pcall(
        body, [dmixed, o, proj, proj, w_sc, g_a, g_c], grid=(S // tm,), name="mixer_post_bwd_a",
        in_specs=[pl.BlockSpec((tm, D_MODEL), lambda i: (i, 0)), ob_s, conv_s, halo_s, full(3, 512), full(1, 512), full(1, 512)],
        out_specs=[ob_s, ob_s, ob_s, full(1, 512), full(1, 512)],
        out_shape=[SDS((4, S, BLK), F32)] * 3 + [SDS((1, 512), F32)] * 2, sem=("arbitrary",), job=job)


def mixer_post_bwd_b(dproj, dgb, dcv, proj, w_sc):
    S = proj.shape[2]
    tm = 512
    last = S // 8 - 1

    def body(dp_in, dgb_ref, dcv_ref, dn_ref, pr_ref, w_ref, dp_ref, dw_ref):
        i = pl.program_id(0)
        keep_next = jnp.where(i < pl.num_programs(0) - 1, 1.0, 0.0)

        @pl.when(i == 0)
        def _():
            dw_ref[...] = jnp.zeros_like(dw_ref)

        for q in range(4):
            sl = slice(q * BLK, (q + 1) * BLK)
            gc, xi = pr_ref[4 + q].astype(F32), pr_ref[8 + q].astype(F32)
            u = gc * xi
            dcv = dcv_ref[q]
            dn = dn_ref[q] * keep_next
            d1, d2 = _shift_up(dcv, 1, dn), _shift_up(dcv, 2, dn)
            w0, w1, w2 = (w_ref[k:k + 1, sl] for k in range(3))
            du = dcv * w2 + d1 * w1 + d2 * w0
            dw_ref[0:1, sl] += _colsum(d2 * u)
            dw_ref[1:2, sl] += _colsum(d1 * u)
            dw_ref[2:3, sl] += _colsum(dcv * u)
            dp_ref[q] = dgb_ref[q].astype(BF16)
            dp_ref[4 + q] = (du * xi).astype(BF16)
            dp_ref[8 + q] = (du * gc).astype(BF16)

    conv_s, _, ob_s = _mixer_specs(S, tm)
    nxt = pl.BlockSpec((4, 8, BLK), lambda i: (0, jnp.minimum((i + 1) * (tm // 8), last), 0))
    full = lambda r, c: pl.BlockSpec((r, c), lambda i: (0, 0))
    return pl.pallas_call(
        body, grid=(S // tm,), name="mixer_post_bwd_b",
        in_specs=[pl.BlockSpec(memory_space=pl.ANY), ob_s, ob_s, nxt, conv_s, full(3, 512)],
        out_specs=[conv_s, full(3, 512)],
        out_shape=[SDS(dproj.shape, BF16), SDS((3, 512), F32)],
        input_output_aliases={0: 0}, compiler_params=_cp("arbitrary"))(dproj, dgb, dcv, dcv, proj, w_sc)


def xattn_fwd(q, k, v):
    S = q.shape[0]
    tm = 512
    scale = MEM_HD ** -0.5

    def body(q_ref, k_ref, v_ref, o_ref):
        for h in range(N_MEM_HEADS):
            sl = slice(h * MEM_HD, (h + 1) * MEM_HD)
            s = _dot_nt(q_ref[:, sl], k_ref[:, sl]) * scale
            p = jnp.exp(s - jnp.max(s, axis=1, keepdims=True))
            p = p / _rsum(p)
            o_ref[:, sl] = _dot(p.astype(BF16), v_ref[:, sl]).astype(BF16)

    row = pl.BlockSpec((tm, D_MODEL), lambda i: (i, 0))
    kv = pl.BlockSpec(k.shape, lambda i: (0, 0))
    return pl.pallas_call(body, grid=(S // tm,), name="xattn_fwd", in_specs=[row, kv, kv], out_specs=row,
                          out_shape=SDS((S, D_MODEL), BF16), compiler_params=_cp("parallel"))(q, k, v)


def xattn_bwd(q, k, v, do):
    S = q.shape[0]
    tm = 512
    scale = MEM_HD ** -0.5

    def body(q_ref, k_ref, v_ref, do_ref, dq_ref, dk_ref, dv_ref):
        @pl.when(pl.program_id(0) == 0)
        def _():
            dk_ref[...] = jnp.zeros_like(dk_ref)
            dv_ref[...] = jnp.zeros_like(dv_ref)

        for h in range(N_MEM_HEADS):
            sl = slice(h * MEM_HD, (h + 1) * MEM_HD)
            qh, kh, vh, doh = q_ref[:, sl], k_ref[:, sl], v_ref[:, sl], do_ref[:, sl]
            s = _dot_nt(qh, kh) * scale
            p = jnp.exp(s - jnp.max(s, axis=1, keepdims=True))
            p = p / _rsum(p)
            dp = _dot_nt(doh, vh)
            ds = (p * (dp - _rsum(p * dp)) * scale).astype(BF16)
            dq_ref[:, sl] = _dot(ds, kh).astype(BF16)
            dk_ref[:, sl] += _dot_tn(ds, qh)
            dv_ref[:, sl] += _dot_tn(p.astype(BF16), doh)

    row = pl.BlockSpec((tm, D_MODEL), lambda i: (i, 0))
    kv = pl.BlockSpec(k.shape, lambda i: (0, 0))
    return pl.pallas_call(body, grid=(S // tm,), name="xattn_bwd", in_specs=[row, kv, kv, row], out_specs=[row, kv, kv],
                          out_shape=[SDS((S, D_MODEL), BF16), SDS(k.shape, F32), SDS(k.shape, F32)],
                          compiler_params=_cp("arbitrary"))(q, k, v, do)


FFN_TM, FFN_TC = 256, 1408


def _ffn_specs(S, order):
    tm, tc = FFN_TM, FFN_TC
    ij = (lambda a, b: (a, b)) if order == "ij" else (lambda a, b: (b, a))
    blk = pl.BlockSpec((2, tm, tc), lambda a, b: (0,) + ij(a, b))
    prev = pl.BlockSpec((2, 16, tc), lambda a, b: (0, jnp.maximum(ij(a, b)[0] * (tm // 16) - 1, 0), ij(a, b)[1]))
    nxt = pl.BlockSpec((2, 16, tc), lambda a, b: (0, jnp.minimum((ij(a, b)[0] + 1) * (tm // 16), S // 16 - 1), ij(a, b)[1]))
    wsp = pl.BlockSpec((2, 3, tc), lambda a, b: (0, 0, ij(a, b)[1]))
    bsp = pl.BlockSpec((2, 1, tc), lambda a, b: (0, 0, ij(a, b)[1]))
    act = pl.BlockSpec((tm, tc), lambda a, b: ij(a, b))
    return blk, prev, nxt, wsp, bsp, act


def _ffn_up(i, up_ref, uh_ref, w_ref, b_ref):
    keep = jnp.where(i > 0, 1.0, 0.0)
    out = []
    for half in range(2):
        u = up_ref[half].astype(F32)
        hu = uh_ref[half].astype(F32) * keep
        w0, w1, w2 = (w_ref[half, k:k + 1, :] for k in range(3))
        out.append(_conv3(u, hu, w0, w1, w2) + b_ref[half])
    return out


def ffn_act_fwd(up_pre, w_fc, b_fc):
    S = up_pre.shape[1]

    def body(up_ref, uh_ref, w_ref, b_ref, a_ref, gv_ref):
        g, v = _ffn_up(pl.program_id(0), up_ref, uh_ref, w_ref, b_ref)
        a_ref[...] = (g * jax.nn.sigmoid(g) * v).astype(BF16)
        gv_ref[0] = g.astype(BF16)
        gv_ref[1] = v.astype(BF16)

    blk, prev, _, wsp, bsp, act = _ffn_specs(S, "ij")
    return pl.pallas_call(body, grid=(S // FFN_TM, D_FF // FFN_TC), name="ffn_act_fwd",
                          in_specs=[blk, prev, wsp, bsp], out_specs=[act, blk],
                          out_shape=[SDS((S, D_FF), BF16), SDS((2, S, D_FF), BF16)],
                          compiler_params=_cp("parallel", "parallel"))(up_pre, up_pre, w_fc, b_fc)


def ffn_act_bwd(dact, up, up_pre, w_fc):
    S = up.shape[1]

    def act_grads(da, g, v):
        sg = jax.nn.sigmoid(g)
        return da * v * (sg * (1.0 + g * (1.0 - sg))), da * g * sg

    def body(da_ref, dan_ref, gv_ref, gvn_ref, up_ref, w_ref, dp_ref, dw_ref, db_ref):
        i = pl.program_id(1)
        keep_next = jnp.where(i < pl.num_programs(1) - 1, 1.0, 0.0)

        @pl.when(i == 0)
        def _():
            dw_ref[...] = jnp.zeros_like(dw_ref)
            db_ref[...] = jnp.zeros_like(db_ref)

        here = act_grads(da_ref[...].astype(F32), gv_ref[0].astype(F32), gv_ref[1].astype(F32))
        after = act_grads(dan_ref[...].astype(F32) * keep_next, gvn_ref[0].astype(F32), gvn_ref[1].astype(F32))
        for half in range(2):
            d, dn = here[half], after[half]
            u = up_ref[half].astype(F32)
            d1, d2 = _shift_up(d, 1, dn), _shift_up(d, 2, dn)
            w0, w1, w2 = (w_ref[half, k:k + 1, :] for k in range(3))
            dp_ref[half] = (d * w2 + d1 * w1 + d2 * w0).astype(BF16)
            dw_ref[half, 0:1, :] += _colsum(d2 * u)
            dw_ref[half, 1:2, :] += _colsum(d1 * u)
            dw_ref[half, 2:3, :] += _colsum(d * u)
            db_ref[half] += _colsum(d)

    tm, tc = FFN_TM, FFN_TC
    blk, _, nxt, wsp, bsp, act = _ffn_specs(S, "ji")
    act_next = pl.BlockSpec((16, tc), lambda j, i: (jnp.minimum((i + 1) * (tm // 16), S // 16 - 1), j))
    return pl.pallas_call(body, grid=(D_FF // tc, S // tm), name="ffn_act_bwd",
                          in_specs=[act, act_next, blk, nxt, blk, wsp], out_specs=[blk, wsp, bsp],
                          out_shape=[SDS((2, S, D_FF), BF16), SDS((2, 3, D_FF), F32), SDS((2, 1, D_FF), F32)],
                          compiler_params=_cp("parallel", "arbitrary"))(dact, dact, up, up, up_pre, w_fc)


def final_loss(x3, g, target):
    S, Dm = x3.shape
    tm = 512

    def body(x_ref, g_ref, t_ref, loss_ref, dx_ref, dg_ref):
        i = pl.program_id(0)
        xv = x_ref[...]
        r = lax.rsqrt(jnp.mean(xv * xv, axis=1, keepdims=True) + EPS)
        xh = xv * r
        err = xh * g_ref[...] - t_ref[...]
        dy = err / Dm
        gd = dy * g_ref[...]
        dx_ref[...] = r * (gd - xh * jnp.mean(gd * xh, axis=1, keepdims=True))

        @pl.when(i == 0)
        def _():
            dg_ref[...] = jnp.zeros_like(dg_ref)
            loss_ref[...] = jnp.zeros_like(loss_ref)

        dg_ref[...] += _colsum(dy * xh)
        loss_ref[...] += 0.5 * _colsum(jnp.mean(err * err, axis=1, keepdims=True))

    row = pl.BlockSpec((tm, Dm), lambda i: (i, 0))
    vec = pl.BlockSpec((1, Dm), lambda i: (0, 0))
    one = pl.BlockSpec((1, 1), lambda i: (0, 0))
    return pl.pallas_call(body, grid=(S // tm,), name="final_loss", in_specs=[row, vec, row], out_specs=[one, row, vec],
                          out_shape=[SDS((1, 1), F32), SDS((S, Dm), F32), SDS((1, Dm), F32)],
                          compiler_params=_cp("arbitrary"))(x3, g, target)


def local_step(x, mem, target, wb, ws, comm=None):
    S = x.shape[0]
    assert S % 2048 == 0
    if comm is None:
        comm = NoComm()
    else:
        wb = comm.gather_first()
    w_fc = ws["w_ffn_conv"].reshape(3, 2, D_FF).transpose(1, 0, 2)
    b_fc = ws["b_ffn_conv"].reshape(2, 1, D_FF)

    bias = bias_tables(ws["rel_bias"])
    h1 = rmsnorm_fwd(x, ws["g_mix"], "norm_mix")
    proj = mm_nn(h1, wb["w_in"], "proj", out_dtype=BF16, out_bw=BLK, tn=768, job=comm.job("proj"))
    comm.landed("proj", wb)
    qkv = proj.reshape(2, 3, 4, S, BLK)
    o, lse = attn_fwd(qkv, bias, job=comm.job("attn_fwd"))
    comm.landed("attn_fwd", wb)
    proj4 = proj.reshape(2, 12, S, BLK)
    mixed = mixer_post_fwd(o, proj4, ws["w_short_conv"], ws["g_attn_out"], ws["g_conv_out"])
    x1 = mm_nn(mixed, wb["w_out"], "out_proj", res=x)
    h2 = rmsnorm_fwd(x1, ws["g_xattn"], "norm_xattn")
    mem_n = rmsnorm_fwd(mem, ws["g_mem"], "norm_mem")
    xq = mm_nn(h2, wb["w_xq"], "xq", out_dtype=BF16)
    xk = mm_nn(mem_n, wb["w_xk"], "xk", out_dtype=BF16, tn=1024)
    xv = mm_nn(mem_n, wb["w_xv"], "xv", out_dtype=BF16, tn=1024)
    xo = xattn_fwd(xq, xk, xv)
    x2 = mm_nn(xo, wb["w_xo"], "xo_proj", res=x1)
    h3 = rmsnorm_fwd(x2, ws["g_ffn"], "norm_ffn")
    up_pre = mm_nn(h3, wb["w_up"], "up_proj", out_dtype=BF16, out_bw=D_FF, tn=1408)
    act, up = ffn_act_fwd(up_pre, w_fc, b_fc)
    x3 = mm_nn(act, wb["w_down"], "down_proj", res=x2)
    loss, dx3, dg_final = final_loss(x3, ws["g_final"].reshape(1, -1), target)

    gb, gs = {}, {"g_final": dg_final}
    gb["w_down"] = mm_tn(act, dx3, None, "dw_down", tm=1408, tn=1024)
    dact = mm_nt(dx3, None, wb["w_down"], "d_act", out_dtype=BF16, tn=1408)
    d_up_pre, dw_fc, db_fc = ffn_act_bwd(dact, up, up_pre, w_fc)
    gs["b_ffn_conv"] = db_fc.reshape(1, 2 * D_FF)
    gs["w_ffn_conv"] = dw_fc.transpose(1, 0, 2).reshape(3, 2 * D_FF)
    gb["w_up"] = mm_tn(h3, d_up_pre, D_FF, "dw_up", shards=4, tn=1408)
    dx2, gs["g_ffn"] = mm_nt_norm_bwd(d_up_pre, D_FF, wb["w_up"], x2, ws["g_ffn"], dx3, "d_h3", tm=256,
                                      job=comm.job("d_h3", gb))
    comm.landed("d_h3")
    gb["w_xo"] = mm_tn(xo, dx2, None, "dw_xo")
    dxo = mm_nt(dx2, None, wb["w_xo"], "d_xo", out_dtype=BF16)
    dxq, dxk, dxv = xattn_bwd(xq, xk, xv, dxo)
    gb["w_xq"] = mm_tn(h2, dxq, None, "dw_xq")
    gb["w_xk"] = mm_tn(mem_n, dxk, None, "dw_xk", tn=1024)
    gb["w_xv"] = mm_tn(mem_n, dxv, None, "dw_xv", tn=1024)
    dmem_n = mm_nt(dxk, None, wb["w_xk"], "d_memk", tn=1024)
    dmem_n = mm_nt(dxv, None, wb["w_xv"], "d_memv", res=dmem_n, tn=1024)
    gs["g_mem"] = rmsnorm_bwd(mem, ws["g_mem"], dmem_n, None, "norm_mem_bwd")
    dx1, gs["g_xattn"] = mm_nt_norm_bwd(dxq, None, wb["w_xq"], x1, ws["g_xattn"], dx2, "d_h2")
    gb["w_out"] = mm_tn(mixed, dx1, None, "dw_out")
    dmixed = mm_nt(dx1, None, wb["w_out"], "d_mixed")
    do, dgb, dcv, gs["g_attn_out"], gs["g_conv_out"] = mixer_post_bwd_a(
        dmixed, o, proj4, ws["w_short_conv"], ws["g_attn_out"], ws["g_conv_out"], job=comm.job("mixer_post_bwd_a", gb))
    comm.landed("mixer_post_bwd_a")
    dproj, ds_sum = attn_bwd(qkv, bias, o, lse, do, job=comm.job("attn_bwd"))
    comm.landed("attn_bwd")
    gs["rel_bias"] = bias_tables_bwd(ds_sum)
    dproj, gs["w_short_conv"] = mixer_post_bwd_b(dproj.reshape(2, 12, S, BLK), dgb, dcv, proj4, ws["w_short_conv"])
    dproj = dproj.reshape(24, S, BLK)
    gb["w_in"] = mm_tn(h1, dproj, BLK, "dw_in", shards=4, tn=768, job=comm.job("dw_in"))
    comm.landed("dw_in")
    grad_x, gs["g_mix"] = mm_nt_norm_bwd(dproj, BLK, wb["w_in"], x, ws["g_mix"], dx1, "d_h1", tm=256,
                                         job=comm.job("d_h1", gb))
    comm.landed("d_h1")
    return loss, grad_x, comm.finish(gb), gs


def adamw(w, g, m, v, name, job=None):
    R, C = w.shape
    tr = R
    for cand in (256, 352):
        if R > cand and R % cand == 0:
            tr = cand
            break

    def body(w_ref, g_ref, m_ref, v_ref, d_ref, nm_ref, nv_ref):
        gv = g_ref[...]
        mn = ADAM_B1 * m_ref[...] + (1.0 - ADAM_B1) * gv
        vn = ADAM_B2 * v_ref[...] + (1.0 - ADAM_B2) * (gv * gv)
        m_hat = mn / (1.0 - ADAM_B1 ** ADAM_STEP)
        v_hat = vn / (1.0 - ADAM_B2 ** ADAM_STEP)
        d_ref[...] = -ADAM_LR * (m_hat / (jnp.sqrt(v_hat) + ADAM_EPS) + ADAM_WD * w_ref[...])
        nm_ref[...] = mn
        nv_ref[...] = vn

    blk = pl.BlockSpec((tr, C), lambda i: (i, 0))
    return _pcall(body, [w, g, m, v], grid=(R // tr,), name=name, in_specs=[blk] * 4, out_specs=[blk] * 3,
                  out_shape=[SDS((R, C), F32)] * 3, sem=("parallel",), job=job)


BIG = ("w_in", "w_out", "w_xq", "w_xk", "w_xv", "w_xo", "w_up", "w_down")
COL_SHARDED = ("w_in", "w_up")
N_BIG = len(BIG)
ANY = pl.BlockSpec(memory_space=pl.ANY)


def _place():
    x, y, c = lax.axis_index("x"), lax.axis_index("y"), lax.axis_index("c")
    chips = [(1 - x, y), (x, 1 - y), (1 - x, 1 - y)]
    return x, y, c, chips


def _window(full, name, R, C, shard, half):
    r0, nr = (0, R) if half is None else (half * (R // 2), R // 2)
    if name in COL_SHARDED:
        return full.at[pl.ds(r0, nr), pl.ds(shard * C, C)]
    return full.at[pl.ds(shard * R + r0, nr), :]


def place_shard(w, mine_arr, n):
    R, C = w.shape
    col = n in COL_SHARDED

    def body(s_ref, w_ref, o_ref):
        o_ref[...] = w_ref[...].astype(BF16)

    grid_spec = pltpu.PrefetchScalarGridSpec(
        num_scalar_prefetch=1, grid=(1,), in_specs=[pl.BlockSpec((R, C), lambda i, s: (0, 0))],
        out_specs=pl.BlockSpec((R, C), (lambda i, s: (0, s[0])) if col else (lambda i, s: (s[0], 0))))
    return pl.pallas_call(body, grid_spec=grid_spec, name="place_" + n,
                          out_shape=SDS((R, 4 * C) if col else (4 * R, C), BF16),
                          compiler_params=_cp("arbitrary"))(mine_arr, w)


def _gather_jobs(names, shapes):
    nw = len(names)

    def start(full, sems):
        send, recv, fsend, frecv = sems
        x, y, c, chips = _place()
        mine = 2 * x + y
        for w, n in enumerate(names):
            R, C = shapes[w]
            own = _window(full[w], n, R, C, mine, c)
            for j, chip in enumerate(chips):
                pltpu.make_async_remote_copy(src_ref=own, dst_ref=own, send_sem=send.at[w, j], recv_sem=recv.at[w, j],
                                             device_id=(*chip, c), device_id_type=MESH).start()

    def mid(full, sems):
        send, recv, fsend, frecv = sems
        x, y, c, chips = _place()
        sib = (x, y, 1 - c)
        for w, n in enumerate(names):
            R, C = shapes[w]
            for j, chip in enumerate(chips):
                landed = _window(full[w], n, R, C, 2 * chip[0] + chip[1], c)
                pltpu.make_async_remote_copy(src_ref=landed, dst_ref=landed, send_sem=send.at[w, j], recv_sem=recv.at[w, j],
                                             device_id=(*chip, c), device_id_type=MESH).wait_recv()
                pltpu.make_async_remote_copy(src_ref=landed, dst_ref=landed, send_sem=fsend.at[w, j],
                                             recv_sem=frecv.at[w, j], device_id=sib, device_id_type=MESH).start()

    def finish(full, sems):
        send, recv, fsend, frecv = sems
        x, y, c, chips = _place()
        mine = 2 * x + y
        sib = (x, y, 1 - c)
        for w, n in enumerate(names):
            R, C = shapes[w]
            own = _window(full[w], n, R, C, mine, c)
            for j, chip in enumerate(chips):
                landed = _window(full[w], n, R, C, 2 * chip[0] + chip[1], c)
                other = _window(full[w], n, R, C, 2 * chip[0] + chip[1], 1 - c)
                pltpu.make_async_remote_copy(src_ref=other, dst_ref=other, send_sem=fsend.at[w, j], recv_sem=frecv.at[w, j],
                                             device_id=sib, device_id_type=MESH).wait_recv()
                pltpu.make_async_remote_copy(src_ref=own, dst_ref=own, send_sem=send.at[w, j], recv_sem=recv.at[w, j],
                                             device_id=(*chip, c), device_id_type=MESH).wait_send()
                pltpu.make_async_remote_copy(src_ref=landed, dst_ref=landed, send_sem=fsend.at[w, j],
                                             recv_sem=frecv.at[w, j], device_id=sib, device_id_type=MESH).wait_send()

    return start, mid, finish, [pltpu.SemaphoreType.DMA((nw, 3))] * 4


class CommJob:
    def __init__(self, ins, out_shapes, inplace, start, finish, sems, mid=None):
        self.ins, self.out_shapes, self.inplace = list(ins), list(out_shapes), inplace
        self.start, self.mid, self.finish, self.sems = start, mid, finish, list(sems)

    def run(self, phase, ins, outs, sems):
        if phase == 0:
            self.start(ins, outs, sems)
        elif phase == 1:
            if self.mid is not None:
                self.mid(ins, outs, sems)
        else:
            self.finish(ins, outs, sems)


def run_job(name, job):
    n_in, n_out = len(job.ins), len(job.out_shapes)

    def body(*refs):
        ins, outs, sm = refs[:n_in], refs[n_in:n_in + n_out], refs[n_in + n_out:]
        for phase in range(3):
            job.run(phase, ins, outs, sm)

    job.landed = list(pl.pallas_call(
        body, name=name, in_specs=[ANY] * n_in, out_specs=[ANY] * n_out, out_shape=job.out_shapes,
        input_output_aliases={i: i for i in range(n_in)} if job.inplace else {},
        scratch_shapes=job.sems)(*job.ins))
    return job.landed


class Carried:
    def __init__(self, job, n_in, n_out, n_scratch, steps):
        self.job, self.n_in, self.n_out, self.n_scratch, self.steps = job, n_in, n_out, n_scratch, steps
        self.nji = len(job.ins) if job else 0
        self.njo = len(job.out_shapes) if job else 0

    def in_specs(self):
        return [ANY] * self.nji

    def out_specs(self):
        return [ANY] * self.njo

    def out_shapes(self):
        return list(self.job.out_shapes) if self.job else []

    def aliases(self):
        if not (self.job and self.job.inplace):
            return {}
        return {self.n_in + i: self.n_out + i for i in range(self.nji)}

    def sems(self):
        return list(self.job.sems) if self.job else []

    def args(self):
        return list(self.job.ins) if self.job else []

    def split(self, refs):
        a = self.n_in
        b = a + self.nji
        c = b + self.n_out
        d = c + self.njo
        e = d + self.n_scratch
        return refs[:a] + refs[b:c] + refs[d:e], (refs[a:b], refs[c:d], refs[e:])

    def phase(self, phase, step, jrefs):
        if self.job is None:
            return
        at = {0: 0, 1: max(self.steps - 2, 0), 2: self.steps - 1}[phase]

        @pl.when(step == at)
        def _():
            self.job.run(phase, *jrefs)


def _pcall(body, args, *, grid, in_specs, out_specs, out_shape, name, sem, scratch=(), aliases=None, job=None):
    n_in, n_out = len(args), len(out_shape)
    if job is None:
        return pl.pallas_call(
            body, grid=grid, in_specs=list(in_specs), out_specs=list(out_specs), out_shape=list(out_shape), name=name,
            scratch_shapes=list(scratch), input_output_aliases=dict(aliases or {}), compiler_params=_cp(*sem))(*args)
    total = int(np.prod(grid))
    car = Carried(job, n_in, n_out, len(scratch), total)

    def wrapped(*refs):
        main, jrefs = car.split(refs)
        lin = pl.program_id(0)
        for ax in range(1, len(grid)):
            lin = lin * grid[ax] + pl.program_id(ax)
        car.phase(0, lin, jrefs)
        body(*main)
        car.phase(1, lin, jrefs)
        car.phase(2, lin, jrefs)

    res = pl.pallas_call(
        wrapped, grid=grid, in_specs=list(in_specs) + car.in_specs(), out_specs=list(out_specs) + car.out_specs(),
        out_shape=list(out_shape) + car.out_shapes(), name=name, scratch_shapes=list(scratch) + car.sems(),
        input_output_aliases={**dict(aliases or {}), **car.aliases()},
        compiler_params=_cp(*(["arbitrary"] * len(grid))))(*args, *car.args())
    job.landed = list(res[n_out:])
    return list(res[:n_out])


def gather_job(placed, names):
    shapes = []
    for n in names:
        R, C = placed[n].shape
        shapes.append((R, C // 4) if n in COL_SHARDED else (R // 4, C))
    start, mid, finish, sems = _gather_jobs(names, shapes)
    arrays = [placed[n] for n in names]
    return CommJob(arrays, [SDS(a.shape, a.dtype) for a in arrays], True,
                   lambda i, o, s: start(o, s), lambda i, o, s: finish(o, s), sems, mid=lambda i, o, s: mid(o, s))


def pair_exchange_job(grads, names):
    shapes = [grads[n].shape for n in names]

    def copies(ins, outs, sems):
        x, y, c, _ = _place()
        return [pltpu.make_async_remote_copy(
            src_ref=ins[w].at[:, pl.ds((1 - c) * (shapes[w][1] // 2), shapes[w][1] // 2), :], dst_ref=outs[w],
            send_sem=sems[0].at[w], recv_sem=sems[1].at[w], device_id=(x, y, 1 - c), device_id_type=MESH)
            for w in range(len(names))]

    def start(ins, outs, sems):
        for cp in copies(ins, outs, sems):
            cp.start()

    def finish(ins, outs, sems):
        for cp in copies(ins, outs, sems):
            cp.wait()

    return CommJob([grads[n] for n in names], [SDS((4, s[1] // 2, s[2]), F32) for s in shapes], False, start, finish,
                   [pltpu.SemaphoreType.DMA((len(names),))] * 2)


def chip_exchange_job(p16, names):
    shapes = [p16[n].shape for n in names]

    def copies(ins, outs, sems):
        x, y, c, chips = _place()
        return [pltpu.make_async_remote_copy(
            src_ref=ins[w].at[2 * chip[0] + chip[1]], dst_ref=outs[w].at[j],
            send_sem=sems[0].at[w, j], recv_sem=sems[1].at[w, j], device_id=(*chip, c), device_id_type=MESH)
            for w in range(len(names)) for j, chip in enumerate(chips)]

    def start(ins, outs, sems):
        for cp in copies(ins, outs, sems):
            cp.start()

    def finish(ins, outs, sems):
        for cp in copies(ins, outs, sems):
            cp.wait()

    return CommJob([p16[n] for n in names], [SDS((3,) + tuple(s[1:]), BF16) for s in shapes], False, start, finish,
                   [pltpu.SemaphoreType.DMA((len(names), 3))] * 2)


def half_exchange_job(full, names):
    shapes = [full[n].shape for n in names]

    def copies(outs, sems):
        x, y, c, _ = _place()
        cps = []
        for w in range(len(names)):
            Rh = shapes[w][0] // 2
            rows = outs[w].at[pl.ds(c * Rh, Rh), :]
            other = outs[w].at[pl.ds((1 - c) * Rh, Rh), :]
            cps.append((pltpu.make_async_remote_copy(src_ref=rows, dst_ref=rows, send_sem=sems[0].at[w], recv_sem=sems[1].at[w],
                                                     device_id=(x, y, 1 - c), device_id_type=MESH),
                        pltpu.make_async_remote_copy(src_ref=other, dst_ref=other, send_sem=sems[0].at[w], recv_sem=sems[1].at[w],
                                                     device_id=(x, y, 1 - c), device_id_type=MESH)))
        return cps

    def start(ins, outs, sems):
        for send, _ in copies(outs, sems):
            send.start()

    def finish(ins, outs, sems):
        for send, recv in copies(outs, sems):
            recv.wait_recv()
            send.wait_send()

    arrays = [full[n] for n in names]
    return CommJob(arrays, [SDS(a.shape, a.dtype) for a in arrays], True, start, finish,
                   [pltpu.SemaphoreType.DMA((len(names),))] * 2)


def allreduce_small(parts, name, job=None):
    n = len(parts)
    nji = len(job.ins) if job else 0
    njo = len(job.out_shapes) if job else 0

    def body(*refs):
        ins, jins = refs[:n], refs[n:n + nji]
        outs, jouts = refs[n + nji:2 * n + nji], refs[2 * n + nji:2 * n + nji + njo]
        bufs = refs[2 * n + nji + njo:3 * n + nji + njo]
        send, recv = refs[3 * n + nji + njo:3 * n + nji + njo + 2]
        jsems = refs[3 * n + nji + njo + 2:]
        x, y, c, _ = _place()
        me = 4 * x + 2 * y + c
        if job is not None:
            job.run(0, jins, jouts, jsems)
        sends = []
        for p in range(n):
            for k in range(1, 8):
                peer = (x ^ (k >> 2), y ^ ((k >> 1) & 1), c ^ (k & 1))
                cp = pltpu.make_async_remote_copy(src_ref=ins[p], dst_ref=bufs[p].at[k], send_sem=send.at[p, k - 1],
                                                  recv_sem=recv.at[p, k - 1], device_id=peer, device_id_type=MESH)
                cp.start()
                sends.append(cp)
            bufs[p][0] = ins[p][...]
        for cp in sends:
            cp.wait_recv()
        for p in range(n):
            acc = bufs[p][me]
            for d in range(1, 8):
                acc = acc + bufs[p][d ^ me]
            outs[p][...] = acc
        for cp in sends:
            cp.wait_send()
        if job is not None:
            job.run(1, jins, jouts, jsems)
            job.run(2, jins, jouts, jsems)

    vm = pl.BlockSpec(memory_space=pltpu.VMEM)
    res = pl.pallas_call(
        body, name=name, in_specs=[vm] * n + [ANY] * nji, out_specs=[vm] * n + [ANY] * njo,
        out_shape=[SDS(a.shape, F32) for a in parts] + (list(job.out_shapes) if job else []),
        input_output_aliases={n + i: n + i for i in range(nji)} if (job and job.inplace) else {},
        scratch_shapes=[pltpu.VMEM((8,) + tuple(a.shape), F32) for a in parts] + [pltpu.SemaphoreType.DMA((n, 7))] * 2
        + (list(job.sems) if job else []),
    )(*parts, *(job.ins if job else []))
    if job is not None:
        job.landed = list(res[n:])
    return list(res[:n])


def adamw_multi(ws, gs, ms, vs, name):
    n = len(ws)

    def body(*refs):
        for p in range(n):
            w_ref, g_ref, m_ref, v_ref = (refs[q * n + p] for q in range(4))
            d_ref, nm_ref, nv_ref = (refs[(4 + q) * n + p] for q in range(3))
            gv = g_ref[...]
            mn = ADAM_B1 * m_ref[...] + (1.0 - ADAM_B1) * gv
            vn = ADAM_B2 * v_ref[...] + (1.0 - ADAM_B2) * (gv * gv)
            m_hat = mn / (1.0 - ADAM_B1 ** ADAM_STEP)
            v_hat = vn / (1.0 - ADAM_B2 ** ADAM_STEP)
            d_ref[...] = -ADAM_LR * (m_hat / (jnp.sqrt(v_hat) + ADAM_EPS) + ADAM_WD * w_ref[...])
            nm_ref[...] = mn
            nv_ref[...] = vn

    vm = pl.BlockSpec(memory_space=pltpu.VMEM)
    outs = pl.pallas_call(body, name=name, in_specs=[vm] * (4 * n), out_specs=[vm] * (3 * n),
                          out_shape=[SDS(a.shape, F32) for a in ws] * 3)(*ws, *gs, *ms, *vs)
    return outs[:n], outs[n:2 * n], outs[2 * n:]


def pair_sum(g, xh, sel_arr, name):
    _, R, C = g.shape
    Rh = R // 2

    def body(sel_ref, g_ref, x_ref, p32_ref, p16_ref):
        s = g_ref[...] + x_ref[...]
        p16_ref[...] = s.astype(BF16)

        @pl.when(pl.program_id(0) == sel_ref[0])
        def _():
            p32_ref[...] = s

    same = pl.BlockSpec((None, Rh, C), lambda s, sel: (s, 0, 0))
    grid_spec = pltpu.PrefetchScalarGridSpec(
        num_scalar_prefetch=1, grid=(4,),
        in_specs=[pl.BlockSpec((None, Rh, C), lambda s, sel: (s, sel[1], 0)), same],
        out_specs=[pl.BlockSpec((Rh, C), lambda s, sel: (0, 0)), same])
    return pl.pallas_call(body, grid_spec=grid_spec, name=name, out_shape=[SDS((Rh, C), F32), SDS((4, Rh, C), BF16)],
                          compiler_params=_cp("arbitrary"))(sel_arr, g, xh)


def chip_sum(p32, y, sel_arr, name):
    Rh, C = p32.shape

    def body(s_ref, p_ref, y_ref, o_ref):
        o_ref[...] = ((p_ref[...] + y_ref[0].astype(F32)) + y_ref[1].astype(F32)) + y_ref[2].astype(F32)

    grid_spec = pltpu.PrefetchScalarGridSpec(
        num_scalar_prefetch=1, grid=(1,),
        in_specs=[pl.BlockSpec((Rh, C), lambda i, s: (0, 0)), pl.BlockSpec((3, Rh, C), lambda i, s: (0, 0, 0))],
        out_specs=pl.BlockSpec((Rh, C), lambda i, s: (s[1], 0)))
    return pl.pallas_call(body, grid_spec=grid_spec, name=name, out_shape=SDS((2 * Rh, C), F32),
                          compiler_params=_cp("arbitrary"))(sel_arr, p32, y)


class StepComm:
    FIRST = ("w_in",)
    SQUARE = ("w_out", "w_xq", "w_xk", "w_xv", "w_xo")
    FFN = ("w_up", "w_down")
    PLAN = {
        "proj": ("gather", SQUARE[:4]), "attn_fwd": ("gather", SQUARE[4:] + FFN),
        "d_h3": ("pair", FFN), "mixer_post_bwd_a": ("pair", SQUARE), "attn_bwd": ("chip", FFN + SQUARE),
        "dw_in": ("half", FFN + SQUARE), "d_h1": ("pair", FIRST), "end_chip": ("chip", FIRST),
    }

    def __init__(self, placed, sel_arr):
        self.placed, self.sel_arr = placed, sel_arr
        self.g, self.p32, self.p16, self.full, self.done, self.jobs = {}, {}, {}, {}, {}, {}

    def first_job(self):
        self.jobs["first"] = gather_job(self.placed, self.FIRST)
        return self.jobs["first"]

    def gather_first(self):
        return dict(zip(self.FIRST, self.jobs["first"].landed))

    def job(self, tag, gb=None):
        kind, names = self.PLAN[tag]
        if kind == "gather":
            j = gather_job(self.placed, names)
        elif kind == "pair":
            for n in names:
                self.g[n] = gb[n] if gb[n].ndim == 3 else gb[n].reshape(4, gb[n].shape[0] // 4, gb[n].shape[1])
            j = pair_exchange_job(self.g, names)
        elif kind == "chip":
            j = chip_exchange_job(self.p16, names)
        else:
            j = half_exchange_job(self.full, names)
        self.jobs[tag] = j
        return j

    def landed(self, tag, wb=None):
        kind, names = self.PLAN[tag]
        for n, a in zip(names, self.jobs[tag].landed):
            if kind == "gather":
                wb[n] = a
            elif kind == "pair":
                self.p32[n], self.p16[n] = pair_sum(self.g[n], a, self.sel_arr, "pair_sum_" + n)
            elif kind == "chip":
                self.full[n] = chip_sum(self.p32[n], a, self.sel_arr, "chip_sum_" + n)
            else:
                self.done[n] = a

    def finish(self, gb):
        return self.done

    def last_job(self):
        return self.job("end_chip")

    def last_landed(self):
        self.landed("end_chip")
        self.jobs["end_half"] = half_exchange_job(self.full, self.FIRST)
        return self.jobs["end_half"]

    def take_last(self):
        self.done.update(zip(self.FIRST, self.jobs["end_half"].landed))


class NoComm:
    def job(self, tag, gb=None):
        return None

    def landed(self, tag, wb=None):
        pass

    def finish(self, gb):
        return gb


SMALL = ("rel_bias", "g_mix", "w_short_conv", "g_attn_out", "g_conv_out", "g_xattn", "g_mem", "g_ffn",
         "w_ffn_conv", "b_ffn_conv", "g_final")
SMALL_FULL = {"rel_bias": (8, 32), "g_mix": (1, 1024), "w_short_conv": (3, 512), "g_attn_out": (1, 512), "g_conv_out": (1, 512),
              "g_xattn": (1, 1024), "g_mem": (1, 1024), "g_ffn": (1, 1024), "w_ffn_conv": (3, 5632), "b_ffn_conv": (1, 5632),
              "g_final": (1024,)}
SMALL_SHARDED = ("w_short_conv", "w_ffn_conv")


def _pack(parts):
    flat = jnp.concatenate([p.reshape(-1) for p in parts])
    rows = -(-flat.shape[0] // 1024) * 8
    return jnp.pad(flat, (0, rows * 128 - flat.shape[0])).reshape(rows, 128)


def _unpack(packed, names):
    flat, out, off = packed.reshape(-1), {}, 0
    for n in names:
        size = int(np.prod(SMALL_FULL[n]))
        out[n] = flat[off:off + size].reshape(SMALL_FULL[n])
        off += size
    return out


def kernel(x, mem, rel_bias, g_mix, w_in, w_short_conv, g_attn_out, g_conv_out, w_out, g_xattn, g_mem, w_xq, w_xk, w_xv, w_xo, g_ffn, w_up, w_ffn_conv, b_ffn_conv, w_down, g_final, loss_target, m_rel_bias, m_g_mix, m_w_in, m_w_short_conv, m_g_attn_out, m_g_conv_out, m_w_out, m_g_xattn, m_g_mem, m_w_xq, m_w_xk, m_w_xv, m_w_xo, m_g_ffn, m_w_up, m_w_ffn_conv, m_b_ffn_conv, m_w_down, m_g_final, v_rel_bias, v_g_mix, v_w_in, v_w_short_conv, v_g_attn_out, v_g_conv_out, v_w_out, v_g_xattn, v_g_mem, v_w_xq, v_w_xk, v_w_xv, v_w_xo, v_g_ffn, v_w_up, v_w_ffn_conv, v_b_ffn_conv, v_w_down, v_g_final):
    names = ("rel_bias", "g_mix", "w_in", "w_short_conv", "g_attn_out", "g_conv_out", "w_out", "g_xattn", "g_mem", "w_xq",
             "w_xk", "w_xv", "w_xo", "g_ffn", "w_up", "w_ffn_conv", "b_ffn_conv", "w_down", "g_final")
    W = dict(zip(names, (rel_bias, g_mix, w_in, w_short_conv, g_attn_out, g_conv_out, w_out, g_xattn, g_mem, w_xq, w_xk, w_xv,
                         w_xo, g_ffn, w_up, w_ffn_conv, b_ffn_conv, w_down, g_final)))
    M = dict(zip(names, (m_rel_bias, m_g_mix, m_w_in, m_w_short_conv, m_g_attn_out, m_g_conv_out, m_w_out, m_g_xattn, m_g_mem,
                         m_w_xq, m_w_xk, m_w_xv, m_w_xo, m_g_ffn, m_w_up, m_w_ffn_conv, m_b_ffn_conv, m_w_down, m_g_final)))
    V = dict(zip(names, (v_rel_bias, v_g_mix, v_w_in, v_w_short_conv, v_g_attn_out, v_g_conv_out, v_w_out, v_g_xattn, v_g_mem,
                         v_w_xq, v_w_xk, v_w_xv, v_w_xo, v_g_ffn, v_w_up, v_w_ffn_conv, v_b_ffn_conv, v_w_down, v_g_final)))
    xi, yi, ci = lax.axis_index("x"), lax.axis_index("y"), lax.axis_index("c")
    mine = 2 * xi + yi
    mine_arr = jnp.reshape(mine, (1,)).astype(jnp.int32)
    sel_arr = jnp.stack([mine, ci]).astype(jnp.int32)
    comm = StepComm({n: place_shard(W[n][0], mine_arr, n) for n in BIG}, sel_arr)
    first = jnp.where(ci == 0, 1.0, 0.0).astype(F32)
    emb = []
    for n in SMALL_SHARDED:
        shard = W[n][0]
        full = jnp.zeros(SMALL_FULL[n], F32)
        emb.append(lax.dynamic_update_slice(full, shard * first, (0, mine * shard.shape[1])))
    ws = {n: W[n] for n in SMALL if n not in SMALL_SHARDED}
    ws.update(zip(SMALL_SHARDED, allreduce_small(emb, "gather_small", job=comm.first_job())))

    loss, grad_x, gfull, gs = local_step(x[0], mem[0], loss_target[0], None, ws, comm)

    def as2d(a):
        return a.reshape(1, -1) if a.ndim == 1 else a

    grads, delta, new_m, new_v = {}, {}, {}, {}
    for n in ("w_up",) + tuple(n for n in BIG if n != "w_up"):
        first_update = n == "w_up"
        d, nm, nv = adamw(W[n][0], gfull[n], M[n][0], V[n][0], "adamw_" + n, job=comm.last_job() if first_update else None)
        if first_update:
            reduced = allreduce_small([as2d(gs[n]) for n in SMALL] + [loss], "reduce_small", job=comm.last_landed())
            comm.take_last()
            gsm = dict(zip(SMALL, reduced[:-1]))
            loss = reduced[-1][0, 0]
        grads[n], delta[n], new_m[n], new_v[n] = gfull[n][None], d[None], nm[None], nv[None]
    for n in SMALL_SHARDED:
        wid = W[n].shape[2]
        gsm[n] = lax.dynamic_slice(gsm[n], (0, mine * wid), (3, wid))

    def own(a, n):
        return a[0] if n in SMALL_SHARDED else as2d(a)

    d, nm, nv = adamw_multi([own(W[n], n) for n in SMALL], [gsm[n] for n in SMALL], [own(M[n], n) for n in SMALL],
                            [own(V[n], n) for n in SMALL], "adamw_small")
    for i, n in enumerate(SMALL):
        shape = W[n].shape
        grads[n], delta[n], new_m[n], new_v[n] = (a.reshape(shape) for a in (gsm[n], d[i], nm[i], nv[i]))
    return (loss, grad_x[None], *[grads[n] for n in names], *[delta[n] for n in names],
            *[new_m[n] for n in names], *[new_v[n] for n in names])
```

```python
import functools
import math

import numpy as np
import jax
import jax.numpy as jnp
from jax import lax
from jax.experimental import pallas as pl
from jax.experimental.pallas import tpu as pltpu

F32 = jnp.float32
BF16 = jnp.bfloat16
SDS = jax.ShapeDtypeStruct
MESH = pl.DeviceIdType.MESH

D_MODEL = 1024
ATTN_W = 512
N_HEADS = 8
BLK = 128
PATTERNS = ((128, 1), (512, 4), (2048, 16))
N_BUCKETS = 32
D_FF = 2816
N_MEM_HEADS = 4
MEM_HD = 256
EPS = 1e-6
NEG = -1e30
VMEM_LIMIT = 56 * 1024 * 1024

ADAM_LR, ADAM_B1, ADAM_B2, ADAM_EPS, ADAM_WD, ADAM_STEP = 0.001, 0.9, 0.999, 1e-08, 0.01, 10


def _cp(*sem):
    return pltpu.CompilerParams(dimension_semantics=sem, vmem_limit_bytes=VMEM_LIMIT)


def _dot(a, b):
    return jnp.dot(a, b, preferred_element_type=F32)


def _dot_nt(a, b):
    return lax.dot_general(a, b, (((1,), (1,)), ((), ())), preferred_element_type=F32)


def _dot_tn(a, b):
    return lax.dot_general(a, b, (((0,), (0,)), ((), ())), preferred_element_type=F32)


def _rsum(x):
    return jnp.sum(x, axis=1, keepdims=True)


def rmsnorm_fwd(x, g, name):
    S, Dm = x.shape
    tm = min(S, 512)

    def body(x_ref, g_ref, o_ref):
        xv = x_ref[...]
        r = lax.rsqrt(jnp.mean(xv * xv, axis=1, keepdims=True) + EPS)
        o_ref[...] = (xv * r * g_ref[...]).astype(o_ref.dtype)

    return pl.pallas_call(
        body, grid=(S // tm,), name=name,
        in_specs=[pl.BlockSpec((tm, Dm), lambda i: (i, 0)), pl.BlockSpec((1, Dm), lambda i: (0, 0))],
        out_specs=pl.BlockSpec((tm, Dm), lambda i: (i, 0)),
        out_shape=SDS((S, Dm), BF16), compiler_params=_cp("parallel"))(x, g)


def rmsnorm_bwd(x, g, dh, res, name, job=None):
    S, Dm = x.shape
    tm = min(S, 512)
    want_dx = res is not None

    def body(*refs):
        if want_dx:
            x_ref, g_ref, dh_ref, res_ref, dx_ref, dg_ref = refs
        else:
            x_ref, g_ref, dh_ref, dg_ref = refs
        i = pl.program_id(0)
        xv = x_ref[...]
        r = lax.rsqrt(jnp.mean(xv * xv, axis=1, keepdims=True) + EPS)
        xh = xv * r
        dh = dh_ref[...].astype(F32)
        if want_dx:
            gd = dh * g_ref[...]
            dx_ref[...] = res_ref[...] + r * (gd - xh * jnp.mean(gd * xh, axis=1, keepdims=True))

        @pl.when(i == 0)
        def _():
            dg_ref[...] = jnp.zeros_like(dg_ref)

        dg_ref[...] += jnp.sum(dh * xh, axis=0, keepdims=True)

    row = pl.BlockSpec((tm, Dm), lambda i: (i, 0))
    vec = pl.BlockSpec((1, Dm), lambda i: (0, 0))
    if want_dx:
        return _pcall(body, [x, g, dh, res], grid=(S // tm,), name=name, in_specs=[row, vec, row, row], out_specs=[row, vec],
                      out_shape=[SDS((S, Dm), F32), SDS((1, Dm), F32)], sem=("arbitrary",), job=job)
    return pl.pallas_call(
        body, grid=(S // tm,), name=name, in_specs=[row, vec, row], out_specs=vec,
        out_shape=SDS((1, Dm), F32), compiler_params=_cp("arbitrary"))(x, g, dh)


def _col_spec(bw, tn, rows, row_of, col_of):
    if bw is None:
        return pl.BlockSpec((rows, tn), lambda *g: (row_of(*g), col_of(*g)))
    if tn % bw == 0:
        return pl.BlockSpec((tn // bw, rows, bw), lambda *g: (col_of(*g), row_of(*g), 0))
    per = bw // tn
    return pl.BlockSpec((None, rows, tn), lambda *g: (col_of(*g) // per, row_of(*g), col_of(*g) % per))


def _read_cols(ref, bw, tn):
    if bw is None or tn % bw != 0:
        return ref[...]
    if tn == bw:
        return ref[0]
    return jnp.concatenate([ref[q] for q in range(tn // bw)], axis=1)


def _write_cols(ref, bw, tn, val):
    if bw is None or tn % bw != 0:
        ref[...] = val.astype(ref.dtype)
    else:
        for q in range(tn // bw):
            ref[q] = val[:, q * bw:(q + 1) * bw].astype(ref.dtype)


def mm_nn(a, b, name, *, res=None, out_dtype=F32, out_bw=None, tm=1024, tn=512, job=None):
    M, K = a.shape
    N = b.shape[1]
    tm = min(tm, M)

    def body(*refs):
        if res is None:
            a_ref, b_ref, o_ref = refs
        else:
            a_ref, b_ref, r_ref, o_ref = refs
        acc = _dot(a_ref[...].astype(BF16), b_ref[...])
        if res is not None:
            acc = acc + r_ref[...]
        _write_cols(o_ref, out_bw, tn, acc)

    ri, ci = (lambda i, j: i), (lambda i, j: j)
    in_specs = [pl.BlockSpec((tm, K), lambda i, j: (i, 0)), pl.BlockSpec((K, tn), lambda i, j: (0, j))]
    args = [a, b]
    if res is not None:
        in_specs.append(pl.BlockSpec((tm, tn), lambda i, j: (i, j)))
        args.append(res)
    oshape = (M, N) if out_bw is None else (N // out_bw, M, out_bw)
    return _pcall(body, args, grid=(M // tm, N // tn), name=name, in_specs=in_specs,
                  out_specs=[_col_spec(out_bw, tn, tm, ri, ci)], out_shape=[SDS(oshape, out_dtype)],
                  sem=("parallel", "parallel"), job=job)[0]


def mm_nt(a, a_bw, b, name, *, res=None, out_dtype=F32, tm=1024, tn=512, job=None):
    if a_bw is None:
        M, K = a.shape
    else:
        M, K = a.shape[1], a.shape[0] * a_bw
    N = b.shape[0]
    tm = min(tm, M)

    def body(*refs):
        if res is None:
            a_ref, b_ref, o_ref = refs
        else:
            a_ref, b_ref, r_ref, o_ref = refs
        av = _read_cols(a_ref, a_bw, K).astype(BF16)
        acc = _dot_nt(av, b_ref[...])
        if res is not None:
            acc = acc + r_ref[...]
        o_ref[...] = acc.astype(o_ref.dtype)

    in_specs = [_col_spec(a_bw, K, tm, lambda i, j: i, lambda i, j: 0), pl.BlockSpec((tn, K), lambda i, j: (j, 0))]
    args = [a, b]
    if res is not None:
        in_specs.append(pl.BlockSpec((tm, tn), lambda i, j: (i, j)))
        args.append(res)
    return _pcall(body, args, grid=(M // tm, N // tn), name=name, in_specs=in_specs,
                  out_specs=[pl.BlockSpec((tm, tn), lambda i, j: (i, j))], out_shape=[SDS((M, N), out_dtype)],
                  sem=("parallel", "parallel"), job=job)[0]


def mm_nt_norm_bwd(a, a_bw, b, x, g, res, name, *, tm=512, job=None):
    if a_bw is None:
        M, K = a.shape
    else:
        M, K = a.shape[1], a.shape[0] * a_bw
    N = b.shape[0]

    def body(a_ref, b_ref, x_ref, g_ref, r_ref, dx_ref, dg_ref):
        dh = _dot_nt(_read_cols(a_ref, a_bw, K).astype(BF16), b_ref[...])
        xv = x_ref[...]
        r = lax.rsqrt(jnp.mean(xv * xv, axis=1, keepdims=True) + EPS)
        xh = xv * r
        gd = dh * g_ref[...]
        dx_ref[...] = r_ref[...] + r * (gd - xh * jnp.mean(gd * xh, axis=1, keepdims=True))

        @pl.when(pl.program_id(0) == 0)
        def _():
            dg_ref[...] = jnp.zeros_like(dg_ref)

        dg_ref[...] += jnp.sum(dh * xh, axis=0, keepdims=True)

    row = pl.BlockSpec((tm, N), lambda i: (i, 0))
    vec = pl.BlockSpec((1, N), lambda i: (0, 0))
    return _pcall(body, [a, b, x, g, res], grid=(M // tm,), name=name,
                  in_specs=[_col_spec(a_bw, K, tm, lambda i: i, lambda i: 0), pl.BlockSpec((N, K), lambda i: (0, 0)), row, vec, row],
                  out_specs=[row, vec], out_shape=[SDS((M, N), F32), SDS((1, N), F32)], sem=("arbitrary",), job=job)


def mm_tn(a, b, b_bw, name, *, shards=None, tm=1024, tn=1024, ts=1024, job=None):
    S, Ka = a.shape
    N = b.shape[1] if b_bw is None else b.shape[0] * b_bw
    ts = min(ts, S)
    tm = min(tm, Ka)

    def body(a_ref, b_ref, o_ref):
        @pl.when(pl.program_id(2) == 0)
        def _():
            o_ref[...] = jnp.zeros_like(o_ref)

        bv = _read_cols(b_ref, b_bw, tn).astype(BF16)
        o_ref[...] += _dot_tn(a_ref[...].astype(BF16), bv)

    in_specs = [pl.BlockSpec((ts, tm), lambda i, j, k: (k, i)),
                _col_spec(b_bw, tn, ts, lambda i, j, k: k, lambda i, j, k: j)]
    if shards is None:
        out_spec, oshape = pl.BlockSpec((tm, tn), lambda i, j, k: (i, j)), (Ka, N)
    else:
        per = (N // shards) // tn
        out_spec = pl.BlockSpec((None, tm, tn), lambda i, j, k: (j // per, i, j % per))
        oshape = (shards, Ka, N // shards)
    return _pcall(body, [a, b], grid=(Ka // tm, N // tn, S // ts), name=name, in_specs=in_specs, out_specs=[out_spec],
                  out_shape=[SDS(oshape, F32)], sem=("parallel", "parallel", "arbitrary"), job=job)[0]


def _bucket_tables():
    out = np.zeros((3, 2, BLK, BLK), np.int32)
    qi = np.arange(BLK)[:, None]
    for p, (win, dil) in enumerate(PATTERNS):
        w = win // dil
        assert w == BLK
        for half in range(2):
            kj = np.arange(BLK)[None, :] + half * BLK
            steps = qi + w - kj
            valid = (steps >= 0) & (steps <= w)
            dist = np.clip(steps, 0, w) * dil
            dd = np.maximum(dist, 1).astype(np.float32)
            large = 16 + (np.log(dd / np.float32(16)) / np.float32(math.log(2048 / 16)) * np.float32(16)).astype(np.int32)
            large = np.minimum(large, N_BUCKETS - 1)
            out[p, half] = np.where(valid, np.where(dist < 16, dist, large), -1)
    return out


def bias_tables(rel_bias):
    bkt = jnp.asarray(_bucket_tables())

    def body(rb_ref, bkt_ref, o_ref):
        h = pl.program_id(1)
        for half in range(2):
            bk = bkt_ref[half]
            acc = jnp.full((BLK, BLK), NEG, F32)
            for b in range(N_BUCKETS):
                acc = jnp.where(bk == b, rb_ref[h, b], acc)
            o_ref[half] = acc

    return pl.pallas_call(
        body, grid=(3, N_HEADS), name="bias_tables",
        in_specs=[pl.BlockSpec(memory_space=pltpu.SMEM),
                  pl.BlockSpec((None, 2, BLK, BLK), lambda p, h: (p, 0, 0, 0))],
        out_specs=pl.BlockSpec((None, None, 2, BLK, BLK), lambda p, h: (p, h, 0, 0, 0)),
        out_shape=SDS((3, N_HEADS, 2, BLK, BLK), F32), compiler_params=_cp("parallel", "parallel"))(rel_bias, bkt)


def bias_tables_bwd(ds_sum):
    bkt = jnp.asarray(_bucket_tables())

    def body(ds_ref, bkt_ref, o_ref):
        h = pl.program_id(0)

        @pl.when(h == 0)
        def _():
            o_ref[...] = jnp.zeros_like(o_ref)

        rows = lax.broadcasted_iota(jnp.int32, (N_HEADS, N_BUCKETS), 0)
        cols = lax.broadcasted_iota(jnp.int32, (N_HEADS, N_BUCKETS), 1)
        acc = jnp.zeros((N_HEADS, N_BUCKETS), F32)
        for b in range(N_BUCKETS):
            t = jnp.zeros((BLK, BLK), F32)
            for p in range(3):
                for half in range(2):
                    t = t + jnp.where(bkt_ref[p, half] == b, ds_ref[p, half], 0.0)
            tot = jnp.sum(_rsum(t), axis=0, keepdims=True)
            acc = acc + jnp.where((rows == h) & (cols == b), tot, 0.0)
        o_ref[...] += acc

    return pl.pallas_call(
        body, grid=(N_HEADS,), name="bias_tables_bwd",
        in_specs=[pl.BlockSpec((3, None, 2, BLK, BLK), lambda h: (0, h, 0, 0, 0)),
                  pl.BlockSpec((3, 2, BLK, BLK), lambda h: (0, 0, 0, 0))],
        out_specs=pl.BlockSpec((N_HEADS, N_BUCKETS), lambda h: (0, 0)),
        out_shape=SDS((N_HEADS, N_BUCKETS), F32), compiler_params=_cp("arbitrary"))(ds_sum, bkt)


def _rows(start, dil):
    if dil == 1:
        return pl.ds(pl.multiple_of(start, BLK), BLK)
    return pl.ds(start, BLK, stride=dil)


GRP = 8


def _group_rows(i, dil, S):
    nb = S // (BLK * dil)
    run = min(nb, GRP)
    chunks = nb // run
    res0, b0 = (i // chunks) * (GRP // run), (i % chunks) * run
    cur = [(b0 + j % run) * (BLK * dil) + res0 + j // run for j in range(GRP)]
    t = lax.broadcasted_iota(jnp.int32, (GRP, 1, 1), 0)
    if run == nb:
        before, pen = None, jnp.where(t % run == 0, NEG, 0.0)
    else:
        before = _rows(jnp.maximum(b0 - 1, 0) * (BLK * dil) + res0, dil)
        pen = jnp.where((t == 0) & (b0 == 0), NEG, 0.0)
    return [_rows(s, dil) for s in cur], before, pen


def _load_group(ref, rows):
    return jnp.stack([ref[r, :] for r in rows])


def _with_prev(ref, before, cur_blocks):
    first = cur_blocks[:1] if before is None else ref[before, :][None].astype(cur_blocks.dtype)
    return jnp.concatenate([first, cur_blocks[:-1]], axis=0)


def _bdot_nt(a, b):
    return lax.dot_general(a, b, (((2,), (2,)), ((0,), (0,))), preferred_element_type=F32)


def _bdot(a, b):
    return lax.dot_general(a, b, (((2,), (1,)), ((0,), (0,))), preferred_element_type=F32)


def _bdot_tn(a, b):
    return lax.dot_general(a, b, (((1,), (1,)), ((0,), (0,))), preferred_element_type=F32)


def _lsum(x):
    return jnp.sum(x, axis=-1, keepdims=True)


def _widen(src, dst):
    S = src.shape[1]

    def chunk(i, carry):
        rows = pl.ds(pl.multiple_of(i * 512, 512), 512)
        for a in range(3):
            dst[a, rows, :] = src[a, rows, :].astype(F32)
        return carry

    lax.fori_loop(0, S // 512, chunk, 0)


def attn_fwd(qkv, bias, job=None):
    S = qkv.shape[3]
    nblk = S // BLK
    car = Carried(job, 2, 2, 1, 4)

    def body(*refs):
        (qkv_in, bias_ref, o_ref, lse_ref, qkv_ref), jrefs = car.split(refs)
        car.phase(0, pl.program_id(0), jrefs)
        _widen(qkv_in, qkv_ref)
        lane = lax.broadcasted_iota(jnp.int32, (GRP, BLK, BLK), 2)
        lo = lane < 64
        masks = (lo, jnp.logical_not(lo))
        q_ref, k_ref, v_ref = qkv_ref.at[0], qkv_ref.at[1], qkv_ref.at[2]
        for p, (_, dil) in enumerate(PATTERNS):
            def step(i, carry, p=p, dil=dil):
                rc, before, pen = _group_rows(i, dil, S)
                q2 = _load_group(q_ref, rc) * 0.125
                kc = _load_group(k_ref, rc).astype(BF16)
                kp = _with_prev(k_ref, before, kc)
                vc = _load_group(v_ref, rc)
                vp = _with_prev(v_ref, before, vc)
                if p > 0:
                    acc_old, ml_old = _load_group(o_ref, rc), _load_group(lse_ref, rc)
                pv = jnp.zeros((GRP, BLK, BLK), F32)
                m_new, l_new, alpha = [], [], []
                for h in range(2):
                    qh = jnp.where(masks[h], q2, 0.0).astype(BF16)
                    sp = _bdot_nt(qh, kp) + bias_ref[p, h, 0][None] + pen
                    sc = _bdot_nt(qh, kc) + bias_ref[p, h, 1][None]
                    mn = jnp.max(jnp.maximum(sp, sc), axis=-1, keepdims=True)
                    if p > 0:
                        mo = ml_old[:, :, 64 * h:64 * h + 1]
                        mn = jnp.maximum(mn, mo)
                        al = jnp.exp(mo - mn)
                        alpha.append(al)
                    pp = jnp.exp(sp - mn)
                    pc = jnp.exp(sc - mn)
                    ln = _lsum(pp + pc)
                    if p > 0:
                        ln = ln + al * ml_old[:, :, 64 * h + 32:64 * h + 33]
                    vhp = jnp.where(masks[h], vp, 0.0).astype(BF16)
                    vhc = jnp.where(masks[h], vc, 0.0).astype(BF16)
                    pv = pv + _bdot(pp.astype(BF16), vhp) + _bdot(pc.astype(BF16), vhc)
                    m_new.append(mn)
                    l_new.append(ln)
                if p > 0:
                    pv = pv + acc_old * jnp.where(lo, alpha[0], alpha[1])
                ml_t = jnp.where(lo, jnp.where(lane < 32, m_new[0], l_new[0]), jnp.where(lane < 96, m_new[1], l_new[1]))
                for j, r in enumerate(rc):
                    o_ref[r, :] = pv[j]
                    lse_ref[r, :] = ml_t[j]
                return carry

            lax.fori_loop(0, nblk // GRP, step, 0)

        def fin(i, carry):
            rows = pl.ds(pl.multiple_of(i * 512, 512), 512)
            ml = lse_ref[rows, :]
            is_m = (lax.broadcasted_iota(jnp.int32, ml.shape, 1) & 32) == 0
            den = jnp.where(is_m, pltpu.roll(ml, 96, 1), ml)
            o_ref[rows, :] = o_ref[rows, :] / den
            lse_ref[rows, :] = ml + jnp.log(den)
            return carry

        lax.fori_loop(0, S // 512, fin, 0)
        car.phase(1, pl.program_id(0), jrefs)
        car.phase(2, pl.program_id(0), jrefs)

    outs = pl.pallas_call(
        body, grid=(4,), name="attn_fwd",
        in_specs=[pl.BlockSpec((None, 3, None, S, BLK), lambda g: (0, 0, g, 0, 0)),
                  pl.BlockSpec((3, 2, 2, BLK, BLK), lambda g: (0, g, 0, 0, 0))] + car.in_specs(),
        out_specs=[pl.BlockSpec((None, S, BLK), lambda g: (g, 0, 0)),
                   pl.BlockSpec((None, S, BLK), lambda g: (g, 0, 0))] + car.out_specs(),
        out_shape=[SDS((4, S, BLK), F32), SDS((4, S, BLK), F32)] + car.out_shapes(),
        input_output_aliases=car.aliases(),
        scratch_shapes=[pltpu.VMEM((3, S, BLK), F32)] + car.sems(),
        compiler_params=_cp("arbitrary"))(qkv, bias, *car.args())
    if job is not None:
        job.landed = list(outs[2:])
    return outs[0], outs[1]


def attn_bwd(qkv, bias, o, lse, do, job=None):
    S = qkv.shape[3]
    nblk = S // BLK
    car = Carried(job, 5, 2, 2, 4)

    def body(*refs):
        (qkv_in, bias_ref, o_ref, lse_ref, do_ref, dqkv_out, ds_ref, qkv_ref, dqkv_ref), jrefs = car.split(refs)
        car.phase(0, pl.program_id(0), jrefs)
        _widen(qkv_in, qkv_ref)
        lane = lax.broadcasted_iota(jnp.int32, (GRP, BLK, BLK), 2)
        lo = lane < 64
        masks = (lo, jnp.logical_not(lo))
        dqkv_ref[...] = jnp.zeros_like(dqkv_ref)
        ds_ref[...] = jnp.zeros_like(ds_ref)
        q_ref, k_ref, v_ref = qkv_ref.at[0], qkv_ref.at[1], qkv_ref.at[2]
        dq_ref, dk_ref, dv_ref = dqkv_ref.at[0], dqkv_ref.at[1], dqkv_ref.at[2]
        for p, (_, dil) in enumerate(PATTERNS):
            def step(i, carry, p=p, dil=dil):
                rc, before, pen = _group_rows(i, dil, S)
                q2 = _load_group(q_ref, rc) * 0.125
                kc = _load_group(k_ref, rc).astype(BF16)
                kp = _with_prev(k_ref, before, kc)
                vc = _load_group(v_ref, rc).astype(BF16)
                vp = _with_prev(v_ref, before, vc)
                dot = _load_group(do_ref, rc)
                lset = _load_group(lse_ref, rc)
                prod = dot * _load_group(o_ref, rc)
                zero = jnp.zeros((GRP, BLK, BLK), F32)
                dq, dkc, dkp, dvc, dvp = zero, zero, zero, zero, zero
                for h in range(2):
                    qh = jnp.where(masks[h], q2, 0.0).astype(BF16)
                    doh = jnp.where(masks[h], dot, 0.0).astype(BF16)
                    delta = _lsum(jnp.where(masks[h], prod, 0.0))
                    lh = lset[:, :, 64 * h:64 * h + 1]
                    pp = jnp.exp(_bdot_nt(qh, kp) + bias_ref[p, h, 0][None] + pen - lh)
                    pc = jnp.exp(_bdot_nt(qh, kc) + bias_ref[p, h, 1][None] - lh)
                    dsp = pp * (_bdot_nt(doh, vp) - delta)
                    dsc = pc * (_bdot_nt(doh, vc) - delta)
                    ds_ref[p, h, 0] += jnp.sum(dsp, axis=0)
                    ds_ref[p, h, 1] += jnp.sum(dsc, axis=0)
                    dspb, dscb = dsp.astype(BF16), dsc.astype(BF16)
                    dq = dq + jnp.where(masks[h], _bdot(dspb, kp) + _bdot(dscb, kc), 0.0)
                    dkp = dkp + _bdot_tn(dspb, qh)
                    dkc = dkc + _bdot_tn(dscb, qh)
                    dvp = dvp + _bdot_tn(pp.astype(BF16), doh)
                    dvc = dvc + _bdot_tn(pc.astype(BF16), doh)
                none = jnp.zeros((1, BLK, BLK), F32)
                dkc = dkc + jnp.concatenate([dkp[1:], none], axis=0)
                dvc = dvc + jnp.concatenate([dvp[1:], none], axis=0)
                for j in range(GRP):
                    dq_ref[rc[j], :] += dq[j] * 0.125
                    dk_ref[rc[j], :] += dkc[j]
                    dv_ref[rc[j], :] += dvc[j]
                if before is not None:
                    dk_ref[before, :] += dkp[0]
                    dv_ref[before, :] += dvp[0]
                return carry

            lax.fori_loop(0, nblk // GRP, step, 0)

        def narrow(i, carry):
            rows = pl.ds(pl.multiple_of(i * 512, 512), 512)
            for a in range(3):
                dqkv_out[a, rows, :] = dqkv_ref[a, rows, :].astype(BF16)
            return carry

        lax.fori_loop(0, S // 512, narrow, 0)
        car.phase(1, pl.program_id(0), jrefs)
        car.phase(2, pl.program_id(0), jrefs)

    blk = pl.BlockSpec((None, S, BLK), lambda g: (g, 0, 0))
    outs = pl.pallas_call(
        body, grid=(4,), name="attn_bwd",
        in_specs=[pl.BlockSpec((None, 3, None, S, BLK), lambda g: (0, 0, g, 0, 0)),
                  pl.BlockSpec((3, 2, 2, BLK, BLK), lambda g: (0, g, 0, 0, 0)), blk, blk, blk] + car.in_specs(),
        out_specs=[pl.BlockSpec((None, 3, None, S, BLK), lambda g: (0, 0, g, 0, 0)),
                   pl.BlockSpec((3, 2, 2, BLK, BLK), lambda g: (0, g, 0, 0, 0))] + car.out_specs(),
        out_shape=[SDS((2, 3, 4, S, BLK), BF16), SDS((3, N_HEADS, 2, BLK, BLK), F32)] + car.out_shapes(),
        input_output_aliases=car.aliases(),
        scratch_shapes=[pltpu.VMEM((3, S, BLK), F32), pltpu.VMEM((3, S, BLK), F32)] + car.sems(),
        compiler_params=_cp("arbitrary"))(qkv, bias, o, lse, do, *car.args())
    if job is not None:
        job.landed = list(outs[2:])
    return outs[0], outs[1]


def _shift_down(u, k, halo):
    n = u.shape[0]
    row = lax.broadcasted_iota(jnp.int32, u.shape, 0)
    out = pltpu.roll(u, k, 0)
    hn = halo.shape[0]
    for j in range(k):
        out = jnp.where(row == j, halo[hn - k + j:hn - k + j + 1, :], out)
    return out


def _shift_up(u, k, halo):
    n = u.shape[0]
    row = lax.broadcasted_iota(jnp.int32, u.shape, 0)
    out = pltpu.roll(u, n - k, 0)
    for j in range(k):
        out = jnp.where(row == n - k + j, halo[j:j + 1, :], out)
    return out


def _conv3(u, halo, w0, w1, w2):
    return _shift_down(u, 2, halo) * w0 + _shift_down(u, 1, halo) * w1 + u * w2


def _colsum(x):
    return jnp.sum(x, axis=0, keepdims=True)


def _mixer_specs(S, tm):
    conv = pl.BlockSpec((None, 12, tm, BLK), lambda i: (1, 0, i, 0))
    halo = pl.BlockSpec((None, 12, 16, BLK), lambda i: (1, 0, jnp.maximum(i * (tm // 16) - 1, 0), 0))
    ob = pl.BlockSpec((4, tm, BLK), lambda i: (0, i, 0))
    return conv, halo, ob


def _mixer_recompute(i, o_ref, pr_ref, ph_ref, w_ref):
    ob = [o_ref[q] for q in range(4)]
    gb = [pr_ref[q].astype(F32) for q in range(4)]
    gc = [pr_ref[4 + q].astype(F32) for q in range(4)]
    xi = [pr_ref[8 + q].astype(F32) for q in range(4)]
    keep = jnp.where(i > 0, 1.0, 0.0)
    u = [gc[q] * xi[q] for q in range(4)]
    hu = [ph_ref[4 + q].astype(F32) * ph_ref[8 + q].astype(F32) * keep for q in range(4)]
    w = [[w_ref[k:k + 1, q * BLK:(q + 1) * BLK] for k in range(3)] for q in range(4)]
    cv = [_conv3(u[q], hu[q], *w[q]) for q in range(4)]
    return ob, gb, gc, xi, u, hu, cv, w


def _rms_blocks(blocks):
    ss = sum(_rsum(b * b) for b in blocks)
    return lax.rsqrt(ss / (BLK * len(blocks)) + EPS)


def mixer_post_fwd(o, proj, w_sc, g_a, g_c):
    S = o.shape[1]
    tm = 512

    def body(o_ref, pr_ref, ph_ref, w_ref, ga_ref, gc_ref, m_ref):
        i = pl.program_id(0)
        ob, gb, _, _, _, _, cv, _ = _mixer_recompute(i, o_ref, pr_ref, ph_ref, w_ref)
        conv = [gb[q] * cv[q] for q in range(4)]
        ra, rc = _rms_blocks(ob), _rms_blocks(conv)
        for q in range(4):
            sl = slice(q * BLK, (q + 1) * BLK)
            m_ref[:, q * BLK:(q + 1) * BLK] = (ob[q] * ra * ga_ref[:, sl]).astype(BF16)
            m_ref[:, ATTN_W + q * BLK:ATTN_W + (q + 1) * BLK] = (conv[q] * rc * gc_ref[:, sl]).astype(BF16)

    conv_s, halo_s, ob_s = _mixer_specs(S, tm)
    full = lambda r, c: pl.BlockSpec((r, c), lambda i: (0, 0))
    return pl.pallas_call(
        body, grid=(S // tm,), name="mixer_post_fwd",
        in_specs=[ob_s, conv_s, halo_s, full(3, 512), full(1, 512), full(1, 512)],
        out_specs=pl.BlockSpec((tm, D_MODEL), lambda i: (i, 0)), out_shape=SDS((S, D_MODEL), BF16),
        compiler_params=_cp("parallel"))(o, proj, proj, w_sc, g_a, g_c)


def mixer_post_bwd_a(dmixed, o, proj, w_sc, g_a, g_c, job=None):
    S = o.shape[1]
    tm = 512

    def body(dm_ref, o_ref, pr_ref, ph_ref, w_ref, ga_ref, gc_ref, do_ref, dgb_ref, dcv_ref, dga_ref, dgc_ref):
        i = pl.program_id(0)
        ob, gb, _, _, _, _, cv, _ = _mixer_recompute(i, o_ref, pr_ref, ph_ref, w_ref)
        conv = [gb[q] * cv[q] for q in range(4)]
        ra, rc = _rms_blocks(ob), _rms_blocks(conv)

        @pl.when(i == 0)
        def _():
            dga_ref[...] = jnp.zeros_like(dga_ref)
            dgc_ref[...] = jnp.zeros_like(dgc_ref)

        for blocks, r, g_ref, off, dg_ref, is_attn in ((ob, ra, ga_ref, 0, dga_ref, True), (conv, rc, gc_ref, ATTN_W, dgc_ref, False)):
            xh = [blocks[q] * r for q in range(4)]
            dm = [dm_ref[:, off + q * BLK:off + (q + 1) * BLK].astype(F32) for q in range(4)]
            gd = [dm[q] * g_ref[:, q * BLK:(q + 1) * BLK] for q in range(4)]
            mean = sum(_rsum(gd[q] * xh[q]) for q in range(4)) / (4 * BLK)
            for q in range(4):
                dg_ref[:, q * BLK:(q + 1) * BLK] += _colsum(dm[q] * xh[q])
                dx = r * (gd[q] - xh[q] * mean)
                if is_attn:
                    do_ref[q] = dx
                else:
                    dgb_ref[q] = dx * cv[q]
                    dcv_ref[q] = dx * gb[q]

    conv_s, halo_s, ob_s = _mixer_specs(S, tm)
    full = lambda r, c: pl.BlockSpec((r, c), lambda i: (0, 0))
    return _pcall(
        body, [dmixed, o, proj, proj, w_sc, g_a, g_c], grid=(S // tm,), name="mixer_post_bwd_a",
        in_specs=[pl.BlockSpec((tm, D_MODEL), lambda i: (i, 0)), ob_s, conv_s, halo_s, full(3, 512), full(1, 512), full(1, 512)],
        out_specs=[ob_s, ob_s, ob_s, full(1, 512), full(1, 512)],
        out_shape=[SDS((4, S, BLK), F32)] * 3 + [SDS((1, 512), F32)] * 2, sem=("arbitrary",), job=job)


def mixer_post_bwd_b(dproj, dgb, dcv, proj, w_sc):
    S = proj.shape[2]
    tm = 512
    last = S // 8 - 1

    def body(dp_in, dgb_ref, dcv_ref, dn_ref, pr_ref, w_ref, dp_ref, dw_ref):
        i = pl.program_id(0)
        keep_next = jnp.where(i < pl.num_programs(0) - 1, 1.0, 0.0)

        @pl.when(i == 0)
        def _():
            dw_ref[...] = jnp.zeros_like(dw_ref)

        for q in range(4):
            sl = slice(q * BLK, (q + 1) * BLK)
            gc, xi = pr_ref[4 + q].astype(F32), pr_ref[8 + q].astype(F32)
            u = gc * xi
            dcv = dcv_ref[q]
            dn = dn_ref[q] * keep_next
            d1, d2 = _shift_up(dcv, 1, dn), _shift_up(dcv, 2, dn)
            w0, w1, w2 = (w_ref[k:k + 1, sl] for k in range(3))
            du = dcv * w2 + d1 * w1 + d2 * w0
            dw_ref[0:1, sl] += _colsum(d2 * u)
            dw_ref[1:2, sl] += _colsum(d1 * u)
            dw_ref[2:3, sl] += _colsum(dcv * u)
            dp_ref[q] = dgb_ref[q].astype(BF16)
            dp_ref[4 + q] = (du * xi).astype(BF16)
            dp_ref[8 + q] = (du * gc).astype(BF16)

    conv_s, _, ob_s = _mixer_specs(S, tm)
    nxt = pl.BlockSpec((4, 8, BLK), lambda i: (0, jnp.minimum((i + 1) * (tm // 8), last), 0))
    full = lambda r, c: pl.BlockSpec((r, c), lambda i: (0, 0))
    return pl.pallas_call(
        body, grid=(S // tm,), name="mixer_post_bwd_b",
        in_specs=[pl.BlockSpec(memory_space=pl.ANY), ob_s, ob_s, nxt, conv_s, full(3, 512)],
        out_specs=[conv_s, full(3, 512)],
        out_shape=[SDS(dproj.shape, BF16), SDS((3, 512), F32)],
        input_output_aliases={0: 0}, compiler_params=_cp("arbitrary"))(dproj, dgb, dcv, dcv, proj, w_sc)


def xattn_fwd(q, k, v):
    S = q.shape[0]
    tm = 512
    scale = MEM_HD ** -0.5

    def body(q_ref, k_ref, v_ref, o_ref):
        for h in range(N_MEM_HEADS):
            sl = slice(h * MEM_HD, (h + 1) * MEM_HD)
            s = _dot_nt(q_ref[:, sl], k_ref[:, sl]) * scale
            p = jnp.exp(s - jnp.max(s, axis=1, keepdims=True))
            p = p / _rsum(p)
            o_ref[:, sl] = _dot(p.astype(BF16), v_ref[:, sl]).astype(BF16)

    row = pl.BlockSpec((tm, D_MODEL), lambda i: (i, 0))
    kv = pl.BlockSpec(k.shape, lambda i: (0, 0))
    return pl.pallas_call(body, grid=(S // tm,), name="xattn_fwd", in_specs=[row, kv, kv], out_specs=row,
                          out_shape=SDS((S, D_MODEL), BF16), compiler_params=_cp("parallel"))(q, k, v)


def xattn_bwd(q, k, v, do):
    S = q.shape[0]
    tm = 512
    scale = MEM_HD ** -0.5

    def body(q_ref, k_ref, v_ref, do_ref, dq_ref, dk_ref, dv_ref):
        @pl.when(pl.program_id(0) == 0)
        def _():
            dk_ref[...] = jnp.zeros_like(dk_ref)
            dv_ref[...] = jnp.zeros_like(dv_ref)

        for h in range(N_MEM_HEADS):
            sl = slice(h * MEM_HD, (h + 1) * MEM_HD)
            qh, kh, vh, doh = q_ref[:, sl], k_ref[:, sl], v_ref[:, sl], do_ref[:, sl]
            s = _dot_nt(qh, kh) * scale
            p = jnp.exp(s - jnp.max(s, axis=1, keepdims=True))
            p = p / _rsum(p)
            dp = _dot_nt(doh, vh)
            ds = (p * (dp - _rsum(p * dp)) * scale).astype(BF16)
            dq_ref[:, sl] = _dot(ds, kh).astype(BF16)
            dk_ref[:, sl] += _dot_tn(ds, qh)
            dv_ref[:, sl] += _dot_tn(p.astype(BF16), doh)

    row = pl.BlockSpec((tm, D_MODEL), lambda i: (i, 0))
    kv = pl.BlockSpec(k.shape, lambda i: (0, 0))
    return pl.pallas_call(body, grid=(S // tm,), name="xattn_bwd", in_specs=[row, kv, kv, row], out_specs=[row, kv, kv],
                          out_shape=[SDS((S, D_MODEL), BF16), SDS(k.shape, F32), SDS(k.shape, F32)],
                          compiler_params=_cp("arbitrary"))(q, k, v, do)


FFN_TM, FFN_TC = 256, 1408


def _ffn_specs(S, order):
    tm, tc = FFN_TM, FFN_TC
    ij = (lambda a, b: (a, b)) if order == "ij" else (lambda a, b: (b, a))
    blk = pl.BlockSpec((2, tm, tc), lambda a, b: (0,) + ij(a, b))
    prev = pl.BlockSpec((2, 16, tc), lambda a, b: (0, jnp.maximum(ij(a, b)[0] * (tm // 16) - 1, 0), ij(a, b)[1]))
    nxt = pl.BlockSpec((2, 16, tc), lambda a, b: (0, jnp.minimum((ij(a, b)[0] + 1) * (tm // 16), S // 16 - 1), ij(a, b)[1]))
    wsp = pl.BlockSpec((2, 3, tc), lambda a, b: (0, 0, ij(a, b)[1]))
    bsp = pl.BlockSpec((2, 1, tc), lambda a, b: (0, 0, ij(a, b)[1]))
    act = pl.BlockSpec((tm, tc), lambda a, b: ij(a, b))
    return blk, prev, nxt, wsp, bsp, act


def _ffn_up(i, up_ref, uh_ref, w_ref, b_ref):
    keep = jnp.where(i > 0, 1.0, 0.0)
    out = []
    for half in range(2):
        u = up_ref[half].astype(F32)
        hu = uh_ref[half].astype(F32) * keep
        w0, w1, w2 = (w_ref[half, k:k + 1, :] for k in range(3))
        out.append(_conv3(u, hu, w0, w1, w2) + b_ref[half])
    return out


def ffn_act_fwd(up_pre, w_fc, b_fc):
    S = up_pre.shape[1]

    def body(up_ref, uh_ref, w_ref, b_ref, a_ref, gv_ref):
        g, v = _ffn_up(pl.program_id(0), up_ref, uh_ref, w_ref, b_ref)
        a_ref[...] = (g * jax.nn.sigmoid(g) * v).astype(BF16)
        gv_ref[0] = g.astype(BF16)
        gv_ref[1] = v.astype(BF16)

    blk, prev, _, wsp, bsp, act = _ffn_specs(S, "ij")
    return pl.pallas_call(body, grid=(S // FFN_TM, D_FF // FFN_TC), name="ffn_act_fwd",
                          in_specs=[blk, prev, wsp, bsp], out_specs=[act, blk],
                          out_shape=[SDS((S, D_FF), BF16), SDS((2, S, D_FF), BF16)],
                          compiler_params=_cp("parallel", "parallel"))(up_pre, up_pre, w_fc, b_fc)


def ffn_act_bwd(dact, up, up_pre, w_fc):
    S = up.shape[1]

    def act_grads(da, g, v):
        sg = jax.nn.sigmoid(g)
        return da * v * (sg * (1.0 + g * (1.0 - sg))), da * g * sg

    def body(da_ref, dan_ref, gv_ref, gvn_ref, up_ref, w_ref, dp_ref, dw_ref, db_ref):
        i = pl.program_id(1)
        keep_next = jnp.where(i < pl.num_programs(1) - 1, 1.0, 0.0)

        @pl.when(i == 0)
        def _():
            dw_ref[...] = jnp.zeros_like(dw_ref)
            db_ref[...] = jnp.zeros_like(db_ref)

        here = act_grads(da_ref[...].astype(F32), gv_ref[0].astype(F32), gv_ref[1].astype(F32))
        after = act_grads(dan_ref[...].astype(F32) * keep_next, gvn_ref[0].astype(F32), gvn_ref[1].astype(F32))
        for half in range(2):
            d, dn = here[half], after[half]
            u = up_ref[half].astype(F32)
            d1, d2 = _shift_up(d, 1, dn), _shift_up(d, 2, dn)
            w0, w1, w2 = (w_ref[half, k:k + 1, :] for k in range(3))
            dp_ref[half] = (d * w2 + d1 * w1 + d2 * w0).astype(BF16)
            dw_ref[half, 0:1, :] += _colsum(d2 * u)
            dw_ref[half, 1:2, :] += _colsum(d1 * u)
            dw_ref[half, 2:3, :] += _colsum(d * u)
            db_ref[half] += _colsum(d)

    tm, tc = FFN_TM, FFN_TC
    blk, _, nxt, wsp, bsp, act = _ffn_specs(S, "ji")
    act_next = pl.BlockSpec((16, tc), lambda j, i: (jnp.minimum((i + 1) * (tm // 16), S // 16 - 1), j))
    return pl.pallas_call(body, grid=(D_FF // tc, S // tm), name="ffn_act_bwd",
                          in_specs=[act, act_next, blk, nxt, blk, wsp], out_specs=[blk, wsp, bsp],
                          out_shape=[SDS((2, S, D_FF), BF16), SDS((2, 3, D_FF), F32), SDS((2, 1, D_FF), F32)],
                          compiler_params=_cp("parallel", "arbitrary"))(dact, dact, up, up, up_pre, w_fc)


def final_loss(x3, g, target):
    S, Dm = x3.shape
    tm = 512

    def body(x_ref, g_ref, t_ref, loss_ref, dx_ref, dg_ref):
        i = pl.program_id(0)
        xv = x_ref[...]
        r = lax.rsqrt(jnp.mean(xv * xv, axis=1, keepdims=True) + EPS)
        xh = xv * r
        err = xh * g_ref[...] - t_ref[...]
        dy = err / Dm
        gd = dy * g_ref[...]
        dx_ref[...] = r * (gd - xh * jnp.mean(gd * xh, axis=1, keepdims=True))

        @pl.when(i == 0)
        def _():
            dg_ref[...] = jnp.zeros_like(dg_ref)
            loss_ref[...] = jnp.zeros_like(loss_ref)

        dg_ref[...] += _colsum(dy * xh)
        loss_ref[...] += 0.5 * _colsum(jnp.mean(err * err, axis=1, keepdims=True))

    row = pl.BlockSpec((tm, Dm), lambda i: (i, 0))
    vec = pl.BlockSpec((1, Dm), lambda i: (0, 0))
    one = pl.BlockSpec((1, 1), lambda i: (0, 0))
    return pl.pallas_call(body, grid=(S // tm,), name="final_loss", in_specs=[row, vec, row], out_specs=[one, row, vec],
                          out_shape=[SDS((1, 1), F32), SDS((S, Dm), F32), SDS((1, Dm), F32)],
                          compiler_params=_cp("arbitrary"))(x3, g, target)


def local_step(x, mem, target, wb, ws, comm=None):
    S = x.shape[0]
    assert S % 2048 == 0
    if comm is None:
        comm = NoComm()
    else:
        wb = comm.gather_first()
    w_fc = ws["w_ffn_conv"].reshape(3, 2, D_FF).transpose(1, 0, 2)
    b_fc = ws["b_ffn_conv"].reshape(2, 1, D_FF)

    bias = bias_tables(ws["rel_bias"])
    h1 = rmsnorm_fwd(x, ws["g_mix"], "norm_mix")
    proj = mm_nn(h1, wb["w_in"], "proj", out_dtype=BF16, out_bw=BLK, tn=768, job=comm.job("proj"))
    comm.landed("proj", wb)
    qkv = proj.reshape(2, 3, 4, S, BLK)
    o, lse = attn_fwd(qkv, bias, job=comm.job("attn_fwd"))
    comm.landed("attn_fwd", wb)
    proj4 = proj.reshape(2, 12, S, BLK)
    mixed = mixer_post_fwd(o, proj4, ws["w_short_conv"], ws["g_attn_out"], ws["g_conv_out"])
    x1 = mm_nn(mixed, wb["w_out"], "out_proj", res=x)
    h2 = rmsnorm_fwd(x1, ws["g_xattn"], "norm_xattn")
    mem_n = rmsnorm_fwd(mem, ws["g_mem"], "norm_mem")
    xq = mm_nn(h2, wb["w_xq"], "xq", out_dtype=BF16)
    xk = mm_nn(mem_n, wb["w_xk"], "xk", out_dtype=BF16, tn=1024)
    xv = mm_nn(mem_n, wb["w_xv"], "xv", out_dtype=BF16, tn=1024)
    xo = xattn_fwd(xq, xk, xv)
    x2 = mm_nn(xo, wb["w_xo"], "xo_proj", res=x1)
    h3 = rmsnorm_fwd(x2, ws["g_ffn"], "norm_ffn")
    up_pre = mm_nn(h3, wb["w_up"], "up_proj", out_dtype=BF16, out_bw=D_FF, tn=1408)
    act, up = ffn_act_fwd(up_pre, w_fc, b_fc)
    x3 = mm_nn(act, wb["w_down"], "down_proj", res=x2)
    loss, dx3, dg_final = final_loss(x3, ws["g_final"].reshape(1, -1), target)

    gb, gs = {}, {"g_final": dg_final}
    gb["w_down"] = mm_tn(act, dx3, None, "dw_down", tm=1408, tn=1024)
    dact = mm_nt(dx3, None, wb["w_down"], "d_act", out_dtype=BF16, tn=1408)
    d_up_pre, dw_fc, db_fc = ffn_act_bwd(dact, up, up_pre, w_fc)
    gs["b_ffn_conv"] = db_fc.reshape(1, 2 * D_FF)
    gs["w_ffn_conv"] = dw_fc.transpose(1, 0, 2).reshape(3, 2 * D_FF)
    gb["w_up"] = mm_tn(h3, d_up_pre, D_FF, "dw_up", shards=4, tn=1408)
    dx2, gs["g_ffn"] = mm_nt_norm_bwd(d_up_pre, D_FF, wb["w_up"], x2, ws["g_ffn"], dx3, "d_h3", tm=256,
                                      job=comm.job("d_h3", gb))
    comm.landed("d_h3")
    gb["w_xo"] = mm_tn(xo, dx2, None, "dw_xo")
    dxo = mm_nt(dx2, None, wb["w_xo"], "d_xo", out_dtype=BF16)
    dxq, dxk, dxv = xattn_bwd(xq, xk, xv, dxo)
    gb["w_xq"] = mm_tn(h2, dxq, None, "dw_xq")
    gb["w_xk"] = mm_tn(mem_n, dxk, None, "dw_xk", tn=1024)
    gb["w_xv"] = mm_tn(mem_n, dxv, None, "dw_xv", tn=1024)
    dmem_n = mm_nt(dxk, None, wb["w_xk"], "d_memk", tn=1024)
    dmem_n = mm_nt(dxv, None, wb["w_xv"], "d_memv", res=dmem_n, tn=1024)
    gs["g_mem"] = rmsnorm_bwd(mem, ws["g_mem"], dmem_n, None, "norm_mem_bwd")
    dx1, gs["g_xattn"] = mm_nt_norm_bwd(dxq, None, wb["w_xq"], x1, ws["g_xattn"], dx2, "d_h2")
    gb["w_out"] = mm_tn(mixed, dx1, None, "dw_out")
    dmixed = mm_nt(dx1, None, wb["w_out"], "d_mixed")
    do, dgb, dcv, gs["g_attn_out"], gs["g_conv_out"] = mixer_post_bwd_a(
        dmixed, o, proj4, ws["w_short_conv"], ws["g_attn_out"], ws["g_conv_out"], job=comm.job("mixer_post_bwd_a", gb))
    comm.landed("mixer_post_bwd_a")
    dproj, ds_sum = attn_bwd(qkv, bias, o, lse, do, job=comm.job("attn_bwd"))
    comm.landed("attn_bwd")
    gs["rel_bias"] = bias_tables_bwd(ds_sum)
    dproj, gs["w_short_conv"] = mixer_post_bwd_b(dproj.reshape(2, 12, S, BLK), dgb, dcv, proj4, ws["w_short_conv"])
    dproj = dproj.reshape(24, S, BLK)
    gb["w_in"] = mm_tn(h1, dproj, BLK, "dw_in", shards=4, tn=768, job=comm.job("dw_in"))
    comm.landed("dw_in")
    comm.first_update(gb)
    grad_x, gs["g_mix"] = mm_nt_norm_bwd(dproj, BLK, wb["w_in"], x, ws["g_mix"], dx1, "d_h1", tm=256,
                                         job=comm.job("d_h1", gb))
    comm.landed("d_h1")
    return loss, grad_x, comm.finish(gb), gs


def adamw(w, g, m, v, name, job=None):
    R, C = w.shape
    tr = R
    for cand in (256, 352):
        if R > cand and R % cand == 0:
            tr = cand
            break

    def body(w_ref, g_ref, m_ref, v_ref, d_ref, nm_ref, nv_ref):
        gv = g_ref[...]
        mn = ADAM_B1 * m_ref[...] + (1.0 - ADAM_B1) * gv
        vn = ADAM_B2 * v_ref[...] + (1.0 - ADAM_B2) * (gv * gv)
        m_hat = mn / (1.0 - ADAM_B1 ** ADAM_STEP)
        v_hat = vn / (1.0 - ADAM_B2 ** ADAM_STEP)
        d_ref[...] = -ADAM_LR * (m_hat / (jnp.sqrt(v_hat) + ADAM_EPS) + ADAM_WD * w_ref[...])
        nm_ref[...] = mn
        nv_ref[...] = vn

    blk = pl.BlockSpec((tr, C), lambda i: (i, 0))
    return _pcall(body, [w, g, m, v], grid=(R // tr,), name=name, in_specs=[blk] * 4, out_specs=[blk] * 3,
                  out_shape=[SDS((R, C), F32)] * 3, sem=("parallel",), job=job)


BIG = ("w_in", "w_out", "w_xq", "w_xk", "w_xv", "w_xo", "w_up", "w_down")
COL_SHARDED = ("w_in", "w_up")
N_BIG = len(BIG)
ANY = pl.BlockSpec(memory_space=pl.ANY)


def _place():
    x, y, c = lax.axis_index("x"), lax.axis_index("y"), lax.axis_index("c")
    chips = [(1 - x, y), (x, 1 - y), (1 - x, 1 - y)]
    return x, y, c, chips


def _window(full, name, R, C, shard, half):
    r0, nr = (0, R) if half is None else (half * (R // 2), R // 2)
    if name in COL_SHARDED:
        return full.at[pl.ds(r0, nr), pl.ds(shard * C, C)]
    return full.at[pl.ds(shard * R + r0, nr), :]


def place_shard(w, mine_arr, n):
    R, C = w.shape
    col = n in COL_SHARDED

    def body(s_ref, w_ref, o_ref):
        o_ref[...] = w_ref[...].astype(BF16)

    grid_spec = pltpu.PrefetchScalarGridSpec(
        num_scalar_prefetch=1, grid=(1,), in_specs=[pl.BlockSpec((R, C), lambda i, s: (0, 0))],
        out_specs=pl.BlockSpec((R, C), (lambda i, s: (0, s[0])) if col else (lambda i, s: (s[0], 0))))
    return pl.pallas_call(body, grid_spec=grid_spec, name="place_" + n,
                          out_shape=SDS((R, 4 * C) if col else (4 * R, C), BF16),
                          compiler_params=_cp("arbitrary"))(mine_arr, w)


def _gather_jobs(names, shapes):
    nw = len(names)

    def start(full, sems):
        send, recv, fsend, frecv = sems
        x, y, c, chips = _place()
        mine = 2 * x + y
        for w, n in enumerate(names):
            R, C = shapes[w]
            own = _window(full[w], n, R, C, mine, c)
            for j, chip in enumerate(chips):
                pltpu.make_async_remote_copy(src_ref=own, dst_ref=own, send_sem=send.at[w, j], recv_sem=recv.at[w, j],
                                             device_id=(*chip, c), device_id_type=MESH).start()

    def mid(full, sems):
        send, recv, fsend, frecv = sems
        x, y, c, chips = _place()
        sib = (x, y, 1 - c)
        for w, n in enumerate(names):
            R, C = shapes[w]
            for j, chip in enumerate(chips):
                landed = _window(full[w], n, R, C, 2 * chip[0] + chip[1], c)
                pltpu.make_async_remote_copy(src_ref=landed, dst_ref=landed, send_sem=send.at[w, j], recv_sem=recv.at[w, j],
                                             device_id=(*chip, c), device_id_type=MESH).wait_recv()
                pltpu.make_async_remote_copy(src_ref=landed, dst_ref=landed, send_sem=fsend.at[w, j],
                                             recv_sem=frecv.at[w, j], device_id=sib, device_id_type=MESH).start()

    def finish(full, sems):
        send, recv, fsend, frecv = sems
        x, y, c, chips = _place()
        mine = 2 * x + y
        sib = (x, y, 1 - c)
        for w, n in enumerate(names):
            R, C = shapes[w]
            own = _window(full[w], n, R, C, mine, c)
            for j, chip in enumerate(chips):
                landed = _window(full[w], n, R, C, 2 * chip[0] + chip[1], c)
                other = _window(full[w], n, R, C, 2 * chip[0] + chip[1], 1 - c)
                pltpu.make_async_remote_copy(src_ref=other, dst_ref=other, send_sem=fsend.at[w, j], recv_sem=frecv.at[w, j],
                                             device_id=sib, device_id_type=MESH).wait_recv()
                pltpu.make_async_remote_copy(src_ref=own, dst_ref=own, send_sem=send.at[w, j], recv_sem=recv.at[w, j],
                                             device_id=(*chip, c), device_id_type=MESH).wait_send()
                pltpu.make_async_remote_copy(src_ref=landed, dst_ref=landed, send_sem=fsend.at[w, j],
                                             recv_sem=frecv.at[w, j], device_id=sib, device_id_type=MESH).wait_send()

    return start, mid, finish, [pltpu.SemaphoreType.DMA((nw, 3))] * 4


class CommJob:
    def __init__(self, ins, out_shapes, inplace, start, finish, sems, mid=None):
        self.ins, self.out_shapes, self.inplace = list(ins), list(out_shapes), inplace
        self.start, self.mid, self.finish, self.sems = start, mid, finish, list(sems)

    def run(self, phase, ins, outs, sems):
        if phase == 0:
            self.start(ins, outs, sems)
        elif phase == 1:
            if self.mid is not None:
                self.mid(ins, outs, sems)
        else:
            self.finish(ins, outs, sems)


def run_job(name, job):
    n_in, n_out = len(job.ins), len(job.out_shapes)

    def body(*refs):
        ins, outs, sm = refs[:n_in], refs[n_in:n_in + n_out], refs[n_in + n_out:]
        for phase in range(3):
            job.run(phase, ins, outs, sm)

    job.landed = list(pl.pallas_call(
        body, name=name, in_specs=[ANY] * n_in, out_specs=[ANY] * n_out, out_shape=job.out_shapes,
        input_output_aliases={i: i for i in range(n_in)} if job.inplace else {},
        scratch_shapes=job.sems)(*job.ins))
    return job.landed


class Carried:
    def __init__(self, job, n_in, n_out, n_scratch, steps):
        self.job, self.n_in, self.n_out, self.n_scratch, self.steps = job, n_in, n_out, n_scratch, steps
        self.nji = len(job.ins) if job else 0
        self.njo = len(job.out_shapes) if job else 0

    def in_specs(self):
        return [ANY] * self.nji

    def out_specs(self):
        return [ANY] * self.njo

    def out_shapes(self):
        return list(self.job.out_shapes) if self.job else []

    def aliases(self):
        if not (self.job and self.job.inplace):
            return {}
        return {self.n_in + i: self.n_out + i for i in range(self.nji)}

    def sems(self):
        return list(self.job.sems) if self.job else []

    def args(self):
        return list(self.job.ins) if self.job else []

    def split(self, refs):
        a = self.n_in
        b = a + self.nji
        c = b + self.n_out
        d = c + self.njo
        e = d + self.n_scratch
        return refs[:a] + refs[b:c] + refs[d:e], (refs[a:b], refs[c:d], refs[e:])

    def phase(self, phase, step, jrefs):
        if self.job is None:
            return
        at = {0: 0, 1: max(self.steps - 2, 0), 2: self.steps - 1}[phase]

        @pl.when(step == at)
        def _():
            self.job.run(phase, *jrefs)


def _pcall(body, args, *, grid, in_specs, out_specs, out_shape, name, sem, scratch=(), aliases=None, job=None):
    n_in, n_out = len(args), len(out_shape)
    if job is None:
        return pl.pallas_call(
            body, grid=grid, in_specs=list(in_specs), out_specs=list(out_specs), out_shape=list(out_shape), name=name,
            scratch_shapes=list(scratch), input_output_aliases=dict(aliases or {}), compiler_params=_cp(*sem))(*args)
    total = int(np.prod(grid))
    car = Carried(job, n_in, n_out, len(scratch), total)

    def wrapped(*refs):
        main, jrefs = car.split(refs)
        lin = pl.program_id(0)
        for ax in range(1, len(grid)):
            lin = lin * grid[ax] + pl.program_id(ax)
        car.phase(0, lin, jrefs)
        body(*main)
        car.phase(1, lin, jrefs)
        car.phase(2, lin, jrefs)

    res = pl.pallas_call(
        wrapped, grid=grid, in_specs=list(in_specs) + car.in_specs(), out_specs=list(out_specs) + car.out_specs(),
        out_shape=list(out_shape) + car.out_shapes(), name=name, scratch_shapes=list(scratch) + car.sems(),
        input_output_aliases={**dict(aliases or {}), **car.aliases()},
        compiler_params=_cp(*(["arbitrary"] * len(grid))))(*args, *car.args())
    job.landed = list(res[n_out:])
    return list(res[:n_out])


def gather_job(placed, names):
    shapes = []
    for n in names:
        R, C = placed[n].shape
        shapes.append((R, C // 4) if n in COL_SHARDED else (R // 4, C))
    start, mid, finish, sems = _gather_jobs(names, shapes)
    arrays = [placed[n] for n in names]
    return CommJob(arrays, [SDS(a.shape, a.dtype) for a in arrays], True,
                   lambda i, o, s: start(o, s), lambda i, o, s: finish(o, s), sems, mid=lambda i, o, s: mid(o, s))


def pair_exchange_job(grads, names):
    shapes = [grads[n].shape for n in names]

    def copies(ins, outs, sems):
        x, y, c, _ = _place()
        return [pltpu.make_async_remote_copy(
            src_ref=ins[w].at[:, pl.ds((1 - c) * (shapes[w][1] // 2), shapes[w][1] // 2), :], dst_ref=outs[w],
            send_sem=sems[0].at[w], recv_sem=sems[1].at[w], device_id=(x, y, 1 - c), device_id_type=MESH)
            for w in range(len(names))]

    def start(ins, outs, sems):
        for cp in copies(ins, outs, sems):
            cp.start()

    def finish(ins, outs, sems):
        for cp in copies(ins, outs, sems):
            cp.wait()

    return CommJob([grads[n] for n in names], [SDS((4, s[1] // 2, s[2]), F32) for s in shapes], False, start, finish,
                   [pltpu.SemaphoreType.DMA((len(names),))] * 2)


def chip_exchange_job(p16, names):
    shapes = [p16[n].shape for n in names]

    def copies(ins, outs, sems):
        x, y, c, chips = _place()
        return [pltpu.make_async_remote_copy(
            src_ref=ins[w].at[2 * chip[0] + chip[1]], dst_ref=outs[w].at[j],
            send_sem=sems[0].at[w, j], recv_sem=sems[1].at[w, j], device_id=(*chip, c), device_id_type=MESH)
            for w in range(len(names)) for j, chip in enumerate(chips)]

    def start(ins, outs, sems):
        for cp in copies(ins, outs, sems):
            cp.start()

    def finish(ins, outs, sems):
        for cp in copies(ins, outs, sems):
            cp.wait()

    return CommJob([p16[n] for n in names], [SDS((3,) + tuple(s[1:]), BF16) for s in shapes], False, start, finish,
                   [pltpu.SemaphoreType.DMA((len(names), 3))] * 2)


def half_exchange_job(full, names):
    shapes = [full[n].shape for n in names]

    def copies(outs, sems):
        x, y, c, _ = _place()
        cps = []
        for w in range(len(names)):
            Rh = shapes[w][0] // 2
            rows = outs[w].at[pl.ds(c * Rh, Rh), :]
            other = outs[w].at[pl.ds((1 - c) * Rh, Rh), :]
            cps.append((pltpu.make_async_remote_copy(src_ref=rows, dst_ref=rows, send_sem=sems[0].at[w], recv_sem=sems[1].at[w],
                                                     device_id=(x, y, 1 - c), device_id_type=MESH),
                        pltpu.make_async_remote_copy(src_ref=other, dst_ref=other, send_sem=sems[0].at[w], recv_sem=sems[1].at[w],
                                                     device_id=(x, y, 1 - c), device_id_type=MESH)))
        return cps

    def start(ins, outs, sems):
        for send, _ in copies(outs, sems):
            send.start()

    def finish(ins, outs, sems):
        for send, recv in copies(outs, sems):
            recv.wait_recv()
            send.wait_send()

    arrays = [full[n] for n in names]
    return CommJob(arrays, [SDS(a.shape, a.dtype) for a in arrays], True, start, finish,
                   [pltpu.SemaphoreType.DMA((len(names),))] * 2)


def allreduce_small(parts, name, job=None):
    n = len(parts)
    nji = len(job.ins) if job else 0
    njo = len(job.out_shapes) if job else 0

    def body(*refs):
        ins, jins = refs[:n], refs[n:n + nji]
        outs, jouts = refs[n + nji:2 * n + nji], refs[2 * n + nji:2 * n + nji + njo]
        bufs = refs[2 * n + nji + njo:3 * n + nji + njo]
        send, recv = refs[3 * n + nji + njo:3 * n + nji + njo + 2]
        jsems = refs[3 * n + nji + njo + 2:]
        x, y, c, _ = _place()
        me = 4 * x + 2 * y + c
        if job is not None:
            job.run(0, jins, jouts, jsems)
        sends = []
        for p in range(n):
            for k in range(1, 8):
                peer = (x ^ (k >> 2), y ^ ((k >> 1) & 1), c ^ (k & 1))
                cp = pltpu.make_async_remote_copy(src_ref=ins[p], dst_ref=bufs[p].at[k], send_sem=send.at[p, k - 1],
                                                  recv_sem=recv.at[p, k - 1], device_id=peer, device_id_type=MESH)
                cp.start()
                sends.append(cp)
            bufs[p][0] = ins[p][...]
        for cp in sends:
            cp.wait_recv()
        for p in range(n):
            acc = bufs[p][me]
            for d in range(1, 8):
                acc = acc + bufs[p][d ^ me]
            outs[p][...] = acc
        for cp in sends:
            cp.wait_send()
        if job is not None:
            job.run(1, jins, jouts, jsems)
            job.run(2, jins, jouts, jsems)

    vm = pl.BlockSpec(memory_space=pltpu.VMEM)
    res = pl.pallas_call(
        body, name=name, in_specs=[vm] * n + [ANY] * nji, out_specs=[vm] * n + [ANY] * njo,
        out_shape=[SDS(a.shape, F32) for a in parts] + (list(job.out_shapes) if job else []),
        input_output_aliases={n + i: n + i for i in range(nji)} if (job and job.inplace) else {},
        scratch_shapes=[pltpu.VMEM((8,) + tuple(a.shape), F32) for a in parts] + [pltpu.SemaphoreType.DMA((n, 7))] * 2
        + (list(job.sems) if job else []),
    )(*parts, *(job.ins if job else []))
    if job is not None:
        job.landed = list(res[n:])
    return list(res[:n])


def adamw_multi(ws, gs, ms, vs, name):
    n = len(ws)

    def body(*refs):
        for p in range(n):
            w_ref, g_ref, m_ref, v_ref = (refs[q * n + p] for q in range(4))
            d_ref, nm_ref, nv_ref = (refs[(4 + q) * n + p] for q in range(3))
            gv = g_ref[...]
            mn = ADAM_B1 * m_ref[...] + (1.0 - ADAM_B1) * gv
            vn = ADAM_B2 * v_ref[...] + (1.0 - ADAM_B2) * (gv * gv)
            m_hat = mn / (1.0 - ADAM_B1 ** ADAM_STEP)
            v_hat = vn / (1.0 - ADAM_B2 ** ADAM_STEP)
            d_ref[...] = -ADAM_LR * (m_hat / (jnp.sqrt(v_hat) + ADAM_EPS) + ADAM_WD * w_ref[...])
            nm_ref[...] = mn
            nv_ref[...] = vn

    vm = pl.BlockSpec(memory_space=pltpu.VMEM)
    outs = pl.pallas_call(body, name=name, in_specs=[vm] * (4 * n), out_specs=[vm] * (3 * n),
                          out_shape=[SDS(a.shape, F32) for a in ws] * 3)(*ws, *gs, *ms, *vs)
    return outs[:n], outs[n:2 * n], outs[2 * n:]


def pair_sum(g, xh, sel_arr, name):
    _, R, C = g.shape
    Rh = R // 2

    def body(sel_ref, g_ref, x_ref, p32_ref, p16_ref):
        s = g_ref[...] + x_ref[...]
        p16_ref[...] = s.astype(BF16)

        @pl.when(pl.program_id(0) == sel_ref[0])
        def _():
            p32_ref[...] = s

    same = pl.BlockSpec((None, Rh, C), lambda s, sel: (s, 0, 0))
    grid_spec = pltpu.PrefetchScalarGridSpec(
        num_scalar_prefetch=1, grid=(4,),
        in_specs=[pl.BlockSpec((None, Rh, C), lambda s, sel: (s, sel[1], 0)), same],
        out_specs=[pl.BlockSpec((Rh, C), lambda s, sel: (0, 0)), same])
    return pl.pallas_call(body, grid_spec=grid_spec, name=name, out_shape=[SDS((Rh, C), F32), SDS((4, Rh, C), BF16)],
                          compiler_params=_cp("arbitrary"))(sel_arr, g, xh)


def chip_sum(p32, y, sel_arr, name):
    Rh, C = p32.shape

    def body(s_ref, p_ref, y_ref, o_ref):
        o_ref[...] = ((p_ref[...] + y_ref[0].astype(F32)) + y_ref[1].astype(F32)) + y_ref[2].astype(F32)

    grid_spec = pltpu.PrefetchScalarGridSpec(
        num_scalar_prefetch=1, grid=(1,),
        in_specs=[pl.BlockSpec((Rh, C), lambda i, s: (0, 0)), pl.BlockSpec((3, Rh, C), lambda i, s: (0, 0, 0))],
        out_specs=pl.BlockSpec((Rh, C), lambda i, s: (s[1], 0)))
    return pl.pallas_call(body, grid_spec=grid_spec, name=name, out_shape=SDS((2 * Rh, C), F32),
                          compiler_params=_cp("arbitrary"))(sel_arr, p32, y)


class StepComm:
    FIRST = ("w_in",)
    SQUARE = ("w_out", "w_xq", "w_xk", "w_xv", "w_xo")
    FFN = ("w_up", "w_down")
    PLAN = {
        "proj": ("gather", SQUARE[:4]), "attn_fwd": ("gather", SQUARE[4:] + FFN),
        "d_h3": ("pair", FFN), "mixer_post_bwd_a": ("pair", SQUARE), "attn_bwd": ("chip", FFN + SQUARE),
        "dw_in": ("half", FFN + SQUARE), "first_update": ("pair", FIRST), "d_h1": ("chip", FIRST),
    }

    def __init__(self, placed, sel_arr):
        self.placed, self.sel_arr = placed, sel_arr
        self.g, self.p32, self.p16, self.full, self.done, self.jobs = {}, {}, {}, {}, {}, {}

    def first_job(self):
        self.jobs["first"] = gather_job(self.placed, self.FIRST)
        return self.jobs["first"]

    def gather_first(self):
        return dict(zip(self.FIRST, self.jobs["first"].landed))

    def job(self, tag, gb=None):
        kind, names = self.PLAN[tag]
        if kind == "gather":
            j = gather_job(self.placed, names)
        elif kind == "pair":
            for n in names:
                self.g[n] = gb[n] if gb[n].ndim == 3 else gb[n].reshape(4, gb[n].shape[0] // 4, gb[n].shape[1])
            j = pair_exchange_job(self.g, names)
        elif kind == "chip":
            j = chip_exchange_job(self.p16, names)
        else:
            j = half_exchange_job(self.full, names)
        self.jobs[tag] = j
        return j

    def landed(self, tag, wb=None):
        kind, names = self.PLAN[tag]
        for n, a in zip(names, self.jobs[tag].landed):
            if kind == "gather":
                wb[n] = a
            elif kind == "pair":
                self.p32[n], self.p16[n] = pair_sum(self.g[n], a, self.sel_arr, "pair_sum_" + n)
            elif kind == "chip":
                self.full[n] = chip_sum(self.p32[n], a, self.sel_arr, "chip_sum_" + n)
            else:
                self.done[n] = a

    def first_update(self, gb):
        self.first_result = self.update_fn(self.job("first_update", gb))
        self.landed("first_update")

    def finish(self, gb):
        return self.done

    def last_job(self):
        self.jobs["end_half"] = half_exchange_job(self.full, self.FIRST)
        return self.jobs["end_half"]

    def take_last(self):
        self.done.update(zip(self.FIRST, self.jobs["end_half"].landed))


class NoComm:
    def job(self, tag, gb=None):
        return None

    def landed(self, tag, wb=None):
        pass

    def first_update(self, gb):
        pass

    def finish(self, gb):
        return gb


SMALL = ("rel_bias", "g_mix", "w_short_conv", "g_attn_out", "g_conv_out", "g_xattn", "g_mem", "g_ffn",
         "w_ffn_conv", "b_ffn_conv", "g_final")
SMALL_FULL = {"rel_bias": (8, 32), "g_mix": (1, 1024), "w_short_conv": (3, 512), "g_attn_out": (1, 512), "g_conv_out": (1, 512),
              "g_xattn": (1, 1024), "g_mem": (1, 1024), "g_ffn": (1, 1024), "w_ffn_conv": (3, 5632), "b_ffn_conv": (1, 5632),
              "g_final": (1024,)}
SMALL_SHARDED = ("w_short_conv", "w_ffn_conv")


def _pack(parts):
    flat = jnp.concatenate([p.reshape(-1) for p in parts])
    rows = -(-flat.shape[0] // 1024) * 8
    return jnp.pad(flat, (0, rows * 128 - flat.shape[0])).reshape(rows, 128)


def _unpack(packed, names):
    flat, out, off = packed.reshape(-1), {}, 0
    for n in names:
        size = int(np.prod(SMALL_FULL[n]))
        out[n] = flat[off:off + size].reshape(SMALL_FULL[n])
        off += size
    return out


def kernel(x, mem, rel_bias, g_mix, w_in, w_short_conv, g_attn_out, g_conv_out, w_out, g_xattn, g_mem, w_xq, w_xk, w_xv, w_xo, g_ffn, w_up, w_ffn_conv, b_ffn_conv, w_down, g_final, loss_target, m_rel_bias, m_g_mix, m_w_in, m_w_short_conv, m_g_attn_out, m_g_conv_out, m_w_out, m_g_xattn, m_g_mem, m_w_xq, m_w_xk, m_w_xv, m_w_xo, m_g_ffn, m_w_up, m_w_ffn_conv, m_b_ffn_conv, m_w_down, m_g_final, v_rel_bias, v_g_mix, v_w_in, v_w_short_conv, v_g_attn_out, v_g_conv_out, v_w_out, v_g_xattn, v_g_mem, v_w_xq, v_w_xk, v_w_xv, v_w_xo, v_g_ffn, v_w_up, v_w_ffn_conv, v_b_ffn_conv, v_w_down, v_g_final):
    names = ("rel_bias", "g_mix", "w_in", "w_short_conv", "g_attn_out", "g_conv_out", "w_out", "g_xattn", "g_mem", "w_xq",
             "w_xk", "w_xv", "w_xo", "g_ffn", "w_up", "w_ffn_conv", "b_ffn_conv", "w_down", "g_final")
    W = dict(zip(names, (rel_bias, g_mix, w_in, w_short_conv, g_attn_out, g_conv_out, w_out, g_xattn, g_mem, w_xq, w_xk, w_xv,
                         w_xo, g_ffn, w_up, w_ffn_conv, b_ffn_conv, w_down, g_final)))
    M = dict(zip(names, (m_rel_bias, m_g_mix, m_w_in, m_w_short_conv, m_g_attn_out, m_g_conv_out, m_w_out, m_g_xattn, m_g_mem,
                         m_w_xq, m_w_xk, m_w_xv, m_w_xo, m_g_ffn, m_w_up, m_w_ffn_conv, m_b_ffn_conv, m_w_down, m_g_final)))
    V = dict(zip(names, (v_rel_bias, v_g_mix, v_w_in, v_w_short_conv, v_g_attn_out, v_g_conv_out, v_w_out, v_g_xattn, v_g_mem,
                         v_w_xq, v_w_xk, v_w_xv, v_w_xo, v_g_ffn, v_w_up, v_w_ffn_conv, v_b_ffn_conv, v_w_down, v_g_final)))
    xi, yi, ci = lax.axis_index("x"), lax.axis_index("y"), lax.axis_index("c")
    mine = 2 * xi + yi
    mine_arr = jnp.reshape(mine, (1,)).astype(jnp.int32)
    sel_arr = jnp.stack([mine, ci]).astype(jnp.int32)
    comm = StepComm({n: place_shard(W[n][0], mine_arr, n) for n in BIG}, sel_arr)
    first = jnp.where(ci == 0, 1.0, 0.0).astype(F32)
    emb = []
    for n in SMALL_SHARDED:
        shard = W[n][0]
        full = jnp.zeros(SMALL_FULL[n], F32)
        emb.append(lax.dynamic_update_slice(full, shard * first, (0, mine * shard.shape[1])))
    ws = {n: W[n] for n in SMALL if n not in SMALL_SHARDED}
    ws.update(zip(SMALL_SHARDED, allreduce_small(emb, "gather_small", job=comm.first_job())))

    comm.update_fn = lambda job: adamw(W["w_up"][0], comm.done["w_up"], M["w_up"][0], V["w_up"][0], "adamw_w_up", job=job)
    loss, grad_x, gfull, gs = local_step(x[0], mem[0], loss_target[0], None, ws, comm)

    def as2d(a):
        return a.reshape(1, -1) if a.ndim == 1 else a

    reduced = allreduce_small([as2d(gs[n]) for n in SMALL] + [loss], "reduce_small", job=comm.last_job())
    comm.take_last()
    gsm = dict(zip(SMALL, reduced[:-1]))
    loss = reduced[-1][0, 0]

    grads, delta, new_m, new_v = {}, {}, {}, {}
    for n in BIG:
        d, nm, nv = comm.first_result if n == "w_up" else adamw(W[n][0], gfull[n], M[n][0], V[n][0], "adamw_" + n)
        grads[n], delta[n], new_m[n], new_v[n] = gfull[n][None], d[None], nm[None], nv[None]
    for n in SMALL_SHARDED:
        wid = W[n].shape[2]
        gsm[n] = lax.dynamic_slice(gsm[n], (0, mine * wid), (3, wid))

    def own(a, n):
        return a[0] if n in SMALL_SHARDED else as2d(a)

    d, nm, nv = adamw_multi([own(W[n], n) for n in SMALL], [gsm[n] for n in SMALL], [own(M[n], n) for n in SMALL],
                            [own(V[n], n) for n in SMALL], "adamw_small")
    for i, n in enumerate(SMALL):
        shape = W[n].shape
        grads[n], delta[n], new_m[n], new_v[n] = (a.reshape(shape) for a in (gsm[n], d[i], nm[i], nv[i]))
    return (loss, grad_x[None], *[grads[n] for n in names], *[delta[n] for n in names],
            *[new_m[n] for n in names], *[new_v[n] for n in names])
```

```python
import functools
import math

import numpy as np
import jax
import jax.numpy as jnp
from jax import lax
from jax.experimental import pallas as pl
from jax.experimental.pallas import tpu as pltpu

F32 = jnp.float32
BF16 = jnp.bfloat16
SDS = jax.ShapeDtypeStruct
MESH = pl.DeviceIdType.MESH

D_MODEL = 1024
ATTN_W = 512
N_HEADS = 8
BLK = 128
PATTERNS = ((128, 1), (512, 4), (2048, 16))
N_BUCKETS = 32
D_FF = 2816
N_MEM_HEADS = 4
MEM_HD = 256
EPS = 1e-6
NEG = -1e30
VMEM_LIMIT = 56 * 1024 * 1024

ADAM_LR, ADAM_B1, ADAM_B2, ADAM_EPS, ADAM_WD, ADAM_STEP = 0.001, 0.9, 0.999, 1e-08, 0.01, 10


def _cp(*sem):
    return pltpu.CompilerParams(dimension_semantics=sem, vmem_limit_bytes=VMEM_LIMIT)


def _dot(a, b):
    return jnp.dot(a, b, preferred_element_type=F32)


def _dot_nt(a, b):
    return lax.dot_general(a, b, (((1,), (1,)), ((), ())), preferred_element_type=F32)


def _dot_tn(a, b):
    return lax.dot_general(a, b, (((0,), (0,)), ((), ())), preferred_element_type=F32)


def _rsum(x):
    return jnp.sum(x, axis=1, keepdims=True)


def rmsnorm_fwd(x, g, name, job=None):
    S, Dm = x.shape
    tm = min(S, 512)

    def body(x_ref, g_ref, o_ref):
        xv = x_ref[...]
        r = lax.rsqrt(jnp.mean(xv * xv, axis=1, keepdims=True) + EPS)
        o_ref[...] = (xv * r * g_ref[...]).astype(o_ref.dtype)

    return _pcall(body, [x, g], grid=(S // tm,), name=name,
                  in_specs=[pl.BlockSpec((tm, Dm), lambda i: (i, 0)), pl.BlockSpec((1, Dm), lambda i: (0, 0))],
                  out_specs=[pl.BlockSpec((tm, Dm), lambda i: (i, 0))], out_shape=[SDS((S, Dm), BF16)],
                  sem=("parallel",), job=job)[0]


def rmsnorm_bwd(x, g, dh, res, name, job=None):
    S, Dm = x.shape
    tm = min(S, 512)
    want_dx = res is not None

    def body(*refs):
        if want_dx:
            x_ref, g_ref, dh_ref, res_ref, dx_ref, dg_ref = refs
        else:
            x_ref, g_ref, dh_ref, dg_ref = refs
        i = pl.program_id(0)
        xv = x_ref[...]
        r = lax.rsqrt(jnp.mean(xv * xv, axis=1, keepdims=True) + EPS)
        xh = xv * r
        dh = dh_ref[...].astype(F32)
        if want_dx:
            gd = dh * g_ref[...]
            dx_ref[...] = res_ref[...] + r * (gd - xh * jnp.mean(gd * xh, axis=1, keepdims=True))

        @pl.when(i == 0)
        def _():
            dg_ref[...] = jnp.zeros_like(dg_ref)

        dg_ref[...] += jnp.sum(dh * xh, axis=0, keepdims=True)

    row = pl.BlockSpec((tm, Dm), lambda i: (i, 0))
    vec = pl.BlockSpec((1, Dm), lambda i: (0, 0))
    if want_dx:
        return _pcall(body, [x, g, dh, res], grid=(S // tm,), name=name, in_specs=[row, vec, row, row], out_specs=[row, vec],
                      out_shape=[SDS((S, Dm), F32), SDS((1, Dm), F32)], sem=("arbitrary",), job=job)
    return pl.pallas_call(
        body, grid=(S // tm,), name=name, in_specs=[row, vec, row], out_specs=vec,
        out_shape=SDS((1, Dm), F32), compiler_params=_cp("arbitrary"))(x, g, dh)


def _col_spec(bw, tn, rows, row_of, col_of):
    if bw is None:
        return pl.BlockSpec((rows, tn), lambda *g: (row_of(*g), col_of(*g)))
    if tn % bw == 0:
        return pl.BlockSpec((tn // bw, rows, bw), lambda *g: (col_of(*g), row_of(*g), 0))
    per = bw // tn
    return pl.BlockSpec((None, rows, tn), lambda *g: (col_of(*g) // per, row_of(*g), col_of(*g) % per))


def _read_cols(ref, bw, tn):
    if bw is None or tn % bw != 0:
        return ref[...]
    if tn == bw:
        return ref[0]
    return jnp.concatenate([ref[q] for q in range(tn // bw)], axis=1)


def _write_cols(ref, bw, tn, val):
    if bw is None or tn % bw != 0:
        ref[...] = val.astype(ref.dtype)
    else:
        for q in range(tn // bw):
            ref[q] = val[:, q * bw:(q + 1) * bw].astype(ref.dtype)


def mm_nn(a, b, name, *, res=None, out_dtype=F32, out_bw=None, tm=1024, tn=512, job=None):
    M, K = a.shape
    N = b.shape[1]
    tm = min(tm, M)

    def body(*refs):
        if res is None:
            a_ref, b_ref, o_ref = refs
        else:
            a_ref, b_ref, r_ref, o_ref = refs
        acc = _dot(a_ref[...].astype(BF16), b_ref[...])
        if res is not None:
            acc = acc + r_ref[...]
        _write_cols(o_ref, out_bw, tn, acc)

    ri, ci = (lambda i, j: i), (lambda i, j: j)
    in_specs = [pl.BlockSpec((tm, K), lambda i, j: (i, 0)), pl.BlockSpec((K, tn), lambda i, j: (0, j))]
    args = [a, b]
    if res is not None:
        in_specs.append(pl.BlockSpec((tm, tn), lambda i, j: (i, j)))
        args.append(res)
    oshape = (M, N) if out_bw is None else (N // out_bw, M, out_bw)
    return _pcall(body, args, grid=(M // tm, N // tn), name=name, in_specs=in_specs,
                  out_specs=[_col_spec(out_bw, tn, tm, ri, ci)], out_shape=[SDS(oshape, out_dtype)],
                  sem=("parallel", "parallel"), job=job)[0]


def mm_nt(a, a_bw, b, name, *, res=None, out_dtype=F32, tm=1024, tn=512, job=None):
    if a_bw is None:
        M, K = a.shape
    else:
        M, K = a.shape[1], a.shape[0] * a_bw
    N = b.shape[0]
    tm = min(tm, M)

    def body(*refs):
        if res is None:
            a_ref, b_ref, o_ref = refs
        else:
            a_ref, b_ref, r_ref, o_ref = refs
        av = _read_cols(a_ref, a_bw, K).astype(BF16)
        acc = _dot_nt(av, b_ref[...])
        if res is not None:
            acc = acc + r_ref[...]
        o_ref[...] = acc.astype(o_ref.dtype)

    in_specs = [_col_spec(a_bw, K, tm, lambda i, j: i, lambda i, j: 0), pl.BlockSpec((tn, K), lambda i, j: (j, 0))]
    args = [a, b]
    if res is not None:
        in_specs.append(pl.BlockSpec((tm, tn), lambda i, j: (i, j)))
        args.append(res)
    return _pcall(body, args, grid=(M // tm, N // tn), name=name, in_specs=in_specs,
                  out_specs=[pl.BlockSpec((tm, tn), lambda i, j: (i, j))], out_shape=[SDS((M, N), out_dtype)],
                  sem=("parallel", "parallel"), job=job)[0]


def mm_nt_norm_bwd(a, a_bw, b, x, g, res, name, *, tm=512, job=None):
    if a_bw is None:
        M, K = a.shape
    else:
        M, K = a.shape[1], a.shape[0] * a_bw
    N = b.shape[0]

    def body(a_ref, b_ref, x_ref, g_ref, r_ref, dx_ref, dg_ref):
        dh = _dot_nt(_read_cols(a_ref, a_bw, K).astype(BF16), b_ref[...])
        xv = x_ref[...]
        r = lax.rsqrt(jnp.mean(xv * xv, axis=1, keepdims=True) + EPS)
        xh = xv * r
        gd = dh * g_ref[...]
        dx_ref[...] = r_ref[...] + r * (gd - xh * jnp.mean(gd * xh, axis=1, keepdims=True))

        @pl.when(pl.program_id(0) == 0)
        def _():
            dg_ref[...] = jnp.zeros_like(dg_ref)

        dg_ref[...] += jnp.sum(dh * xh, axis=0, keepdims=True)

    row = pl.BlockSpec((tm, N), lambda i: (i, 0))
    vec = pl.BlockSpec((1, N), lambda i: (0, 0))
    return _pcall(body, [a, b, x, g, res], grid=(M // tm,), name=name,
                  in_specs=[_col_spec(a_bw, K, tm, lambda i: i, lambda i: 0), pl.BlockSpec((N, K), lambda i: (0, 0)), row, vec, row],
                  out_specs=[row, vec], out_shape=[SDS((M, N), F32), SDS((1, N), F32)], sem=("arbitrary",), job=job)


def mm_tn(a, b, b_bw, name, *, shards=None, tm=1024, tn=1024, ts=1024, job=None):
    S, Ka = a.shape
    N = b.shape[1] if b_bw is None else b.shape[0] * b_bw
    ts = min(ts, S)
    tm = min(tm, Ka)

    def body(a_ref, b_ref, o_ref):
        @pl.when(pl.program_id(2) == 0)
        def _():
            o_ref[...] = jnp.zeros_like(o_ref)

        bv = _read_cols(b_ref, b_bw, tn).astype(BF16)
        o_ref[...] += _dot_tn(a_ref[...].astype(BF16), bv)

    in_specs = [pl.BlockSpec((ts, tm), lambda i, j, k: (k, i)),
                _col_spec(b_bw, tn, ts, lambda i, j, k: k, lambda i, j, k: j)]
    if shards is None:
        out_spec, oshape = pl.BlockSpec((tm, tn), lambda i, j, k: (i, j)), (Ka, N)
    else:
        per = (N // shards) // tn
        out_spec = pl.BlockSpec((None, tm, tn), lambda i, j, k: (j // per, i, j % per))
        oshape = (shards, Ka, N // shards)
    return _pcall(body, [a, b], grid=(Ka // tm, N // tn, S // ts), name=name, in_specs=in_specs, out_specs=[out_spec],
                  out_shape=[SDS(oshape, F32)], sem=("parallel", "parallel", "arbitrary"), job=job)[0]


def _bucket_tables():
    out = np.zeros((3, 2, BLK, BLK), np.int32)
    qi = np.arange(BLK)[:, None]
    for p, (win, dil) in enumerate(PATTERNS):
        w = win // dil
        assert w == BLK
        for half in range(2):
            kj = np.arange(BLK)[None, :] + half * BLK
            steps = qi + w - kj
            valid = (steps >= 0) & (steps <= w)
            dist = np.clip(steps, 0, w) * dil
            dd = np.maximum(dist, 1).astype(np.float32)
            large = 16 + (np.log(dd / np.float32(16)) / np.float32(math.log(2048 / 16)) * np.float32(16)).astype(np.int32)
            large = np.minimum(large, N_BUCKETS - 1)
            out[p, half] = np.where(valid, np.where(dist < 16, dist, large), -1)
    return out


def bias_tables(rel_bias):
    bkt = jnp.asarray(_bucket_tables())

    def body(rb_ref, bkt_ref, o_ref):
        h = pl.program_id(1)
        for half in range(2):
            bk = bkt_ref[half]
            acc = jnp.full((BLK, BLK), NEG, F32)
            for b in range(N_BUCKETS):
                acc = jnp.where(bk == b, rb_ref[h, b], acc)
            o_ref[half] = acc

    return pl.pallas_call(
        body, grid=(3, N_HEADS), name="bias_tables",
        in_specs=[pl.BlockSpec(memory_space=pltpu.SMEM),
                  pl.BlockSpec((None, 2, BLK, BLK), lambda p, h: (p, 0, 0, 0))],
        out_specs=pl.BlockSpec((None, None, 2, BLK, BLK), lambda p, h: (p, h, 0, 0, 0)),
        out_shape=SDS((3, N_HEADS, 2, BLK, BLK), F32), compiler_params=_cp("parallel", "parallel"))(rel_bias, bkt)


def bias_tables_bwd(ds_sum):
    bkt = jnp.asarray(_bucket_tables())

    def body(ds_ref, bkt_ref, o_ref):
        h = pl.program_id(0)

        @pl.when(h == 0)
        def _():
            o_ref[...] = jnp.zeros_like(o_ref)

        rows = lax.broadcasted_iota(jnp.int32, (N_HEADS, N_BUCKETS), 0)
        cols = lax.broadcasted_iota(jnp.int32, (N_HEADS, N_BUCKETS), 1)
        acc = jnp.zeros((N_HEADS, N_BUCKETS), F32)
        for b in range(N_BUCKETS):
            t = jnp.zeros((BLK, BLK), F32)
            for p in range(3):
                for half in range(2):
                    t = t + jnp.where(bkt_ref[p, half] == b, ds_ref[p, half], 0.0)
            tot = jnp.sum(_rsum(t), axis=0, keepdims=True)
            acc = acc + jnp.where((rows == h) & (cols == b), tot, 0.0)
        o_ref[...] += acc

    return pl.pallas_call(
        body, grid=(N_HEADS,), name="bias_tables_bwd",
        in_specs=[pl.BlockSpec((3, None, 2, BLK, BLK), lambda h: (0, h, 0, 0, 0)),
                  pl.BlockSpec((3, 2, BLK, BLK), lambda h: (0, 0, 0, 0))],
        out_specs=pl.BlockSpec((N_HEADS, N_BUCKETS), lambda h: (0, 0)),
        out_shape=SDS((N_HEADS, N_BUCKETS), F32), compiler_params=_cp("arbitrary"))(ds_sum, bkt)


def _rows(start, dil):
    if dil == 1:
        return pl.ds(pl.multiple_of(start, BLK), BLK)
    return pl.ds(start, BLK, stride=dil)


GRP = 8


def _group_rows(i, dil, S):
    nb = S // (BLK * dil)
    run = min(nb, GRP)
    chunks = nb // run
    res0, b0 = (i // chunks) * (GRP // run), (i % chunks) * run
    cur = [(b0 + j % run) * (BLK * dil) + res0 + j // run for j in range(GRP)]
    t = lax.broadcasted_iota(jnp.int32, (GRP, 1, 1), 0)
    if run == nb:
        before, pen = None, jnp.where(t % run == 0, NEG, 0.0)
    else:
        before = _rows(jnp.maximum(b0 - 1, 0) * (BLK * dil) + res0, dil)
        pen = jnp.where((t == 0) & (b0 == 0), NEG, 0.0)
    return [_rows(s, dil) for s in cur], before, pen


def _load_group(ref, rows):
    return jnp.stack([ref[r, :] for r in rows])


def _with_prev(ref, before, cur_blocks):
    first = cur_blocks[:1] if before is None else ref[before, :][None].astype(cur_blocks.dtype)
    return jnp.concatenate([first, cur_blocks[:-1]], axis=0)


def _bdot_nt(a, b):
    return lax.dot_general(a, b, (((2,), (2,)), ((0,), (0,))), preferred_element_type=F32)


def _bdot(a, b):
    return lax.dot_general(a, b, (((2,), (1,)), ((0,), (0,))), preferred_element_type=F32)


def _bdot_tn(a, b):
    return lax.dot_general(a, b, (((1,), (1,)), ((0,), (0,))), preferred_element_type=F32)


def _lsum(x):
    return jnp.sum(x, axis=-1, keepdims=True)


def _widen(src, dst):
    S = src.shape[1]

    def chunk(i, carry):
        rows = pl.ds(pl.multiple_of(i * 512, 512), 512)
        for a in range(3):
            dst[a, rows, :] = src[a, rows, :].astype(F32)
        return carry

    lax.fori_loop(0, S // 512, chunk, 0)


def attn_fwd(qkv, bias, job=None):
    S = qkv.shape[3]
    nblk = S // BLK
    car = Carried(job, 2, 2, 1, 4)

    def body(*refs):
        (qkv_in, bias_ref, o_ref, lse_ref, qkv_ref), jrefs = car.split(refs)
        car.phase(0, pl.program_id(0), jrefs)
        _widen(qkv_in, qkv_ref)
        lane = lax.broadcasted_iota(jnp.int32, (GRP, BLK, BLK), 2)
        lo = lane < 64
        masks = (lo, jnp.logical_not(lo))
        q_ref, k_ref, v_ref = qkv_ref.at[0], qkv_ref.at[1], qkv_ref.at[2]
        for p, (_, dil) in enumerate(PATTERNS):
            def step(i, carry, p=p, dil=dil):
                rc, before, pen = _group_rows(i, dil, S)
                q2 = _load_group(q_ref, rc) * 0.125
                kc = _load_group(k_ref, rc).astype(BF16)
                kp = _with_prev(k_ref, before, kc)
                vc = _load_group(v_ref, rc)
                vp = _with_prev(v_ref, before, vc)
                if p > 0:
                    acc_old, ml_old = _load_group(o_ref, rc), _load_group(lse_ref, rc)
                pv = jnp.zeros((GRP, BLK, BLK), F32)
                m_new, l_new, alpha = [], [], []
                for h in range(2):
                    qh = jnp.where(masks[h], q2, 0.0).astype(BF16)
                    sp = _bdot_nt(qh, kp) + bias_ref[p, h, 0][None] + pen
                    sc = _bdot_nt(qh, kc) + bias_ref[p, h, 1][None]
                    mn = jnp.max(jnp.maximum(sp, sc), axis=-1, keepdims=True)
                    if p > 0:
                        mo = ml_old[:, :, 64 * h:64 * h + 1]
                        mn = jnp.maximum(mn, mo)
                        al = jnp.exp(mo - mn)
                        alpha.append(al)
                    pp = jnp.exp(sp - mn)
                    pc = jnp.exp(sc - mn)
                    ln = _lsum(pp + pc)
                    if p > 0:
                        ln = ln + al * ml_old[:, :, 64 * h + 32:64 * h + 33]
                    vhp = jnp.where(masks[h], vp, 0.0).astype(BF16)
                    vhc = jnp.where(masks[h], vc, 0.0).astype(BF16)
                    pv = pv + _bdot(pp.astype(BF16), vhp) + _bdot(pc.astype(BF16), vhc)
                    m_new.append(mn)
                    l_new.append(ln)
                if p > 0:
                    pv = pv + acc_old * jnp.where(lo, alpha[0], alpha[1])
                ml_t = jnp.where(lo, jnp.where(lane < 32, m_new[0], l_new[0]), jnp.where(lane < 96, m_new[1], l_new[1]))
                for j, r in enumerate(rc):
                    o_ref[r, :] = pv[j]
                    lse_ref[r, :] = ml_t[j]
                return carry

            lax.fori_loop(0, nblk // GRP, step, 0)

        def fin(i, carry):
            rows = pl.ds(pl.multiple_of(i * 512, 512), 512)
            ml = lse_ref[rows, :]
            is_m = (lax.broadcasted_iota(jnp.int32, ml.shape, 1) & 32) == 0
            den = jnp.where(is_m, pltpu.roll(ml, 96, 1), ml)
            o_ref[rows, :] = o_ref[rows, :] / den
            lse_ref[rows, :] = ml + jnp.log(den)
            return carry

        lax.fori_loop(0, S // 512, fin, 0)
        car.phase(1, pl.program_id(0), jrefs)
        car.phase(2, pl.program_id(0), jrefs)

    outs = pl.pallas_call(
        body, grid=(4,), name="attn_fwd",
        in_specs=[pl.BlockSpec((None, 3, None, S, BLK), lambda g: (0, 0, g, 0, 0)),
                  pl.BlockSpec((3, 2, 2, BLK, BLK), lambda g: (0, g, 0, 0, 0))] + car.in_specs(),
        out_specs=[pl.BlockSpec((None, S, BLK), lambda g: (g, 0, 0)),
                   pl.BlockSpec((None, S, BLK), lambda g: (g, 0, 0))] + car.out_specs(),
        out_shape=[SDS((4, S, BLK), F32), SDS((4, S, BLK), F32)] + car.out_shapes(),
        input_output_aliases=car.aliases(),
        scratch_shapes=[pltpu.VMEM((3, S, BLK), F32)] + car.sems(),
        compiler_params=_cp("arbitrary"))(qkv, bias, *car.args())
    if job is not None:
        job.landed = list(outs[2:])
    return outs[0], outs[1]


def attn_bwd(qkv, bias, o, lse, do, job=None):
    S = qkv.shape[3]
    nblk = S // BLK
    car = Carried(job, 5, 2, 2, 4)

    def body(*refs):
        (qkv_in, bias_ref, o_ref, lse_ref, do_ref, dqkv_out, ds_ref, qkv_ref, dqkv_ref), jrefs = car.split(refs)
        car.phase(0, pl.program_id(0), jrefs)
        _widen(qkv_in, qkv_ref)
        lane = lax.broadcasted_iota(jnp.int32, (GRP, BLK, BLK), 2)
        lo = lane < 64
        masks = (lo, jnp.logical_not(lo))
        dqkv_ref[...] = jnp.zeros_like(dqkv_ref)
        ds_ref[...] = jnp.zeros_like(ds_ref)
        q_ref, k_ref, v_ref = qkv_ref.at[0], qkv_ref.at[1], qkv_ref.at[2]
        dq_ref, dk_ref, dv_ref = dqkv_ref.at[0], dqkv_ref.at[1], dqkv_ref.at[2]
        for p, (_, dil) in enumerate(PATTERNS):
            def step(i, carry, p=p, dil=dil):
                rc, before, pen = _group_rows(i, dil, S)
                q2 = _load_group(q_ref, rc) * 0.125
                kc = _load_group(k_ref, rc).astype(BF16)
                kp = _with_prev(k_ref, before, kc)
                vc = _load_group(v_ref, rc).astype(BF16)
                vp = _with_prev(v_ref, before, vc)
                dot = _load_group(do_ref, rc)
                lset = _load_group(lse_ref, rc)
                prod = dot * _load_group(o_ref, rc)
                zero = jnp.zeros((GRP, BLK, BLK), F32)
                dq, dkc, dkp, dvc, dvp = zero, zero, zero, zero, zero
                for h in range(2):
                    qh = jnp.where(masks[h], q2, 0.0).astype(BF16)
                    doh = jnp.where(masks[h], dot, 0.0).astype(BF16)
                    delta = _lsum(jnp.where(masks[h], prod, 0.0))
                    lh = lset[:, :, 64 * h:64 * h + 1]
                    pp = jnp.exp(_bdot_nt(qh, kp) + bias_ref[p, h, 0][None] + pen - lh)
                    pc = jnp.exp(_bdot_nt(qh, kc) + bias_ref[p, h, 1][None] - lh)
                    dsp = pp * (_bdot_nt(doh, vp) - delta)
                    dsc = pc * (_bdot_nt(doh, vc) - delta)
                    ds_ref[p, h, 0] += jnp.sum(dsp, axis=0)
                    ds_ref[p, h, 1] += jnp.sum(dsc, axis=0)
                    dspb, dscb = dsp.astype(BF16), dsc.astype(BF16)
                    dq = dq + jnp.where(masks[h], _bdot(dspb, kp) + _bdot(dscb, kc), 0.0)
                    dkp = dkp + _bdot_tn(dspb, qh)
                    dkc = dkc + _bdot_tn(dscb, qh)
                    dvp = dvp + _bdot_tn(pp.astype(BF16), doh)
                    dvc = dvc + _bdot_tn(pc.astype(BF16), doh)
                none = jnp.zeros((1, BLK, BLK), F32)
                dkc = dkc + jnp.concatenate([dkp[1:], none], axis=0)
                dvc = dvc + jnp.concatenate([dvp[1:], none], axis=0)
                for j in range(GRP):
                    dq_ref[rc[j], :] += dq[j] * 0.125
                    dk_ref[rc[j], :] += dkc[j]
                    dv_ref[rc[j], :] += dvc[j]
                if before is not None:
                    dk_ref[before, :] += dkp[0]
                    dv_ref[before, :] += dvp[0]
                return carry

            lax.fori_loop(0, nblk // GRP, step, 0)

        def narrow(i, carry):
            rows = pl.ds(pl.multiple_of(i * 512, 512), 512)
            for a in range(3):
                dqkv_out[a, rows, :] = dqkv_ref[a, rows, :].astype(BF16)
            return carry

        lax.fori_loop(0, S // 512, narrow, 0)
        car.phase(1, pl.program_id(0), jrefs)
        car.phase(2, pl.program_id(0), jrefs)

    blk = pl.BlockSpec((None, S, BLK), lambda g: (g, 0, 0))
    outs = pl.pallas_call(
        body, grid=(4,), name="attn_bwd",
        in_specs=[pl.BlockSpec((None, 3, None, S, BLK), lambda g: (0, 0, g, 0, 0)),
                  pl.BlockSpec((3, 2, 2, BLK, BLK), lambda g: (0, g, 0, 0, 0)), blk, blk, blk] + car.in_specs(),
        out_specs=[pl.BlockSpec((None, 3, None, S, BLK), lambda g: (0, 0, g, 0, 0)),
                   pl.BlockSpec((3, 2, 2, BLK, BLK), lambda g: (0, g, 0, 0, 0))] + car.out_specs(),
        out_shape=[SDS((2, 3, 4, S, BLK), BF16), SDS((3, N_HEADS, 2, BLK, BLK), F32)] + car.out_shapes(),
        input_output_aliases=car.aliases(),
        scratch_shapes=[pltpu.VMEM((3, S, BLK), F32), pltpu.VMEM((3, S, BLK), F32)] + car.sems(),
        compiler_params=_cp("arbitrary"))(qkv, bias, o, lse, do, *car.args())
    if job is not None:
        job.landed = list(outs[2:])
    return outs[0], outs[1]


def _shift_down(u, k, halo):
    n = u.shape[0]
    row = lax.broadcasted_iota(jnp.int32, u.shape, 0)
    out = pltpu.roll(u, k, 0)
    hn = halo.shape[0]
    for j in range(k):
        out = jnp.where(row == j, halo[hn - k + j:hn - k + j + 1, :], out)
    return out


def _shift_up(u, k, halo):
    n = u.shape[0]
    row = lax.broadcasted_iota(jnp.int32, u.shape, 0)
    out = pltpu.roll(u, n - k, 0)
    for j in range(k):
        out = jnp.where(row == n - k + j, halo[j:j + 1, :], out)
    return out


def _conv3(u, halo, w0, w1, w2):
    return _shift_down(u, 2, halo) * w0 + _shift_down(u, 1, halo) * w1 + u * w2


def _colsum(x):
    return jnp.sum(x, axis=0, keepdims=True)


def _mixer_specs(S, tm):
    conv = pl.BlockSpec((None, 12, tm, BLK), lambda i: (1, 0, i, 0))
    halo = pl.BlockSpec((None, 12, 16, BLK), lambda i: (1, 0, jnp.maximum(i * (tm // 16) - 1, 0), 0))
    ob = pl.BlockSpec((4, tm, BLK), lambda i: (0, i, 0))
    return conv, halo, ob


def _mixer_recompute(i, o_ref, pr_ref, ph_ref, w_ref):
    ob = [o_ref[q] for q in range(4)]
    gb = [pr_ref[q].astype(F32) for q in range(4)]
    gc = [pr_ref[4 + q].astype(F32) for q in range(4)]
    xi = [pr_ref[8 + q].astype(F32) for q in range(4)]
    keep = jnp.where(i > 0, 1.0, 0.0)
    u = [gc[q] * xi[q] for q in range(4)]
    hu = [ph_ref[4 + q].astype(F32) * ph_ref[8 + q].astype(F32) * keep for q in range(4)]
    w = [[w_ref[k:k + 1, q * BLK:(q + 1) * BLK] for k in range(3)] for q in range(4)]
    cv = [_conv3(u[q], hu[q], *w[q]) for q in range(4)]
    return ob, gb, gc, xi, u, hu, cv, w


def _rms_blocks(blocks):
    ss = sum(_rsum(b * b) for b in blocks)
    return lax.rsqrt(ss / (BLK * len(blocks)) + EPS)


def mixer_post_fwd(o, proj, w_sc, g_a, g_c):
    S = o.shape[1]
    tm = 512

    def body(o_ref, pr_ref, ph_ref, w_ref, ga_ref, gc_ref, m_ref):
        i = pl.program_id(0)
        ob, gb, _, _, _, _, cv, _ = _mixer_recompute(i, o_ref, pr_ref, ph_ref, w_ref)
        conv = [gb[q] * cv[q] for q in range(4)]
        ra, rc = _rms_blocks(ob), _rms_blocks(conv)
        for q in range(4):
            sl = slice(q * BLK, (q + 1) * BLK)
            m_ref[:, q * BLK:(q + 1) * BLK] = (ob[q] * ra * ga_ref[:, sl]).astype(BF16)
            m_ref[:, ATTN_W + q * BLK:ATTN_W + (q + 1) * BLK] = (conv[q] * rc * gc_ref[:, sl]).astype(BF16)

    conv_s, halo_s, ob_s = _mixer_specs(S, tm)
    full = lambda r, c: pl.BlockSpec((r, c), lambda i: (0, 0))
    return pl.pallas_call(
        body, grid=(S // tm,), name="mixer_post_fwd",
        in_specs=[ob_s, conv_s, halo_s, full(3, 512), full(1, 512), full(1, 512)],
        out_specs=pl.BlockSpec((tm, D_MODEL), lambda i: (i, 0)), out_shape=SDS((S, D_MODEL), BF16),
        compiler_params=_cp("parallel"))(o, proj, proj, w_sc, g_a, g_c)


def mixer_post_bwd_a(dmixed, o, proj, w_sc, g_a, g_c, job=None):
    S = o.shape[1]
    tm = 512

    def body(dm_ref, o_ref, pr_ref, ph_ref, w_ref, ga_ref, gc_ref, do_ref, dgb_ref, dcv_ref, dga_ref, dgc_ref):
        i = pl.program_id(0)
        ob, gb, _, _, _, _, cv, _ = _mixer_recompute(i, o_ref, pr_ref, ph_ref, w_ref)
        conv = [gb[q] * cv[q] for q in range(4)]
        ra, rc = _rms_blocks(ob), _rms_blocks(conv)

        @pl.when(i == 0)
        def _():
            dga_ref[...] = jnp.zeros_like(dga_ref)
            dgc_ref[...] = jnp.zeros_like(dgc_ref)

        for blocks, r, g_ref, off, dg_ref, is_attn in ((ob, ra, ga_ref, 0, dga_ref, True), (conv, rc, gc_ref, ATTN_W, dgc_ref, False)):
            xh = [blocks[q] * r for q in range(4)]
            dm = [dm_ref[:, off + q * BLK:off + (q + 1) * BLK].astype(F32) for q in range(4)]
            gd = [dm[q] * g_ref[:, q * BLK:(q + 1) * BLK] for q in range(4)]
            mean = sum(_rsum(gd[q] * xh[q]) for q in range(4)) / (4 * BLK)
            for q in range(4):
                dg_ref[:, q * BLK:(q + 1) * BLK] += _colsum(dm[q] * xh[q])
                dx = r * (gd[q] - xh[q] * mean)
                if is_attn:
                    do_ref[q] = dx
                else:
                    dgb_ref[q] = dx * cv[q]
                    dcv_ref[q] = dx * gb[q]

    conv_s, halo_s, ob_s = _mixer_specs(S, tm)
    full = lambda r, c: pl.BlockSpec((r, c), lambda i: (0, 0))
    return _pcall(
        body, [dmixed, o, proj, proj, w_sc, g_a, g_c], grid=(S // tm,), name="mixer_post_bwd_a",
        in_specs=[pl.BlockSpec((tm, D_MODEL), lambda i: (i, 0)), ob_s, conv_s, halo_s, full(3, 512), full(1, 512), full(1, 512)],
        out_specs=[ob_s, ob_s, ob_s, full(1, 512), full(1, 512)],
        out_shape=[SDS((4, S, BLK), F32)] * 3 + [SDS((1, 512), F32)] * 2, sem=("arbitrary",), job=job)


def mixer_post_bwd_b(dproj, dgb, dcv, proj, w_sc):
    S = proj.shape[2]
    tm = 512
    last = S // 8 - 1

    def body(dp_in, dgb_ref, dcv_ref, dn_ref, pr_ref, w_ref, dp_ref, dw_ref):
        i = pl.program_id(0)
        keep_next = jnp.where(i < pl.num_programs(0) - 1, 1.0, 0.0)

        @pl.when(i == 0)
        def _():
            dw_ref[...] = jnp.zeros_like(dw_ref)

        for q in range(4):
            sl = slice(q * BLK, (q + 1) * BLK)
            gc, xi = pr_ref[4 + q].astype(F32), pr_ref[8 + q].astype(F32)
            u = gc * xi
            dcv = dcv_ref[q]
            dn = dn_ref[q] * keep_next
            d1, d2 = _shift_up(dcv, 1, dn), _shift_up(dcv, 2, dn)
            w0, w1, w2 = (w_ref[k:k + 1, sl] for k in range(3))
            du = dcv * w2 + d1 * w1 + d2 * w0
            dw_ref[0:1, sl] += _colsum(d2 * u)
            dw_ref[1:2, sl] += _colsum(d1 * u)
            dw_ref[2:3, sl] += _colsum(dcv * u)
            dp_ref[q] = dgb_ref[q].astype(BF16)
            dp_ref[4 + q] = (du * xi).astype(BF16)
            dp_ref[8 + q] = (du * gc).astype(BF16)

    conv_s, _, ob_s = _mixer_specs(S, tm)
    nxt = pl.BlockSpec((4, 8, BLK), lambda i: (0, jnp.minimum((i + 1) * (tm // 8), last), 0))
    full = lambda r, c: pl.BlockSpec((r, c), lambda i: (0, 0))
    return pl.pallas_call(
        body, grid=(S // tm,), name="mixer_post_bwd_b",
        in_specs=[pl.BlockSpec(memory_space=pl.ANY), ob_s, ob_s, nxt, conv_s, full(3, 512)],
        out_specs=[conv_s, full(3, 512)],
        out_shape=[SDS(dproj.shape, BF16), SDS((3, 512), F32)],
        input_output_aliases={0: 0}, compiler_params=_cp("arbitrary"))(dproj, dgb, dcv, dcv, proj, w_sc)


def xattn_fwd(q, k, v):
    S = q.shape[0]
    tm = 512
    scale = MEM_HD ** -0.5

    def body(q_ref, k_ref, v_ref, o_ref):
        for h in range(N_MEM_HEADS):
            sl = slice(h * MEM_HD, (h + 1) * MEM_HD)
            s = _dot_nt(q_ref[:, sl], k_ref[:, sl]) * scale
            p = jnp.exp(s - jnp.max(s, axis=1, keepdims=True))
            p = p / _rsum(p)
            o_ref[:, sl] = _dot(p.astype(BF16), v_ref[:, sl]).astype(BF16)

    row = pl.BlockSpec((tm, D_MODEL), lambda i: (i, 0))
    kv = pl.BlockSpec(k.shape, lambda i: (0, 0))
    return pl.pallas_call(body, grid=(S // tm,), name="xattn_fwd", in_specs=[row, kv, kv], out_specs=row,
                          out_shape=SDS((S, D_MODEL), BF16), compiler_params=_cp("parallel"))(q, k, v)


def xattn_bwd(q, k, v, do):
    S = q.shape[0]
    tm = 512
    scale = MEM_HD ** -0.5

    def body(q_ref, k_ref, v_ref, do_ref, dq_ref, dk_ref, dv_ref):
        @pl.when(pl.program_id(0) == 0)
        def _():
            dk_ref[...] = jnp.zeros_like(dk_ref)
            dv_ref[...] = jnp.zeros_like(dv_ref)

        for h in range(N_MEM_HEADS):
            sl = slice(h * MEM_HD, (h + 1) * MEM_HD)
            qh, kh, vh, doh = q_ref[:, sl], k_ref[:, sl], v_ref[:, sl], do_ref[:, sl]
            s = _dot_nt(qh, kh) * scale
            p = jnp.exp(s - jnp.max(s, axis=1, keepdims=True))
            p = p / _rsum(p)
            dp = _dot_nt(doh, vh)
            ds = (p * (dp - _rsum(p * dp)) * scale).astype(BF16)
            dq_ref[:, sl] = _dot(ds, kh).astype(BF16)
            dk_ref[:, sl] += _dot_tn(ds, qh)
            dv_ref[:, sl] += _dot_tn(p.astype(BF16), doh)

    row = pl.BlockSpec((tm, D_MODEL), lambda i: (i, 0))
    kv = pl.BlockSpec(k.shape, lambda i: (0, 0))
    return pl.pallas_call(body, grid=(S // tm,), name="xattn_bwd", in_specs=[row, kv, kv, row], out_specs=[row, kv, kv],
                          out_shape=[SDS((S, D_MODEL), BF16), SDS(k.shape, F32), SDS(k.shape, F32)],
                          compiler_params=_cp("arbitrary"))(q, k, v, do)


FFN_TM, FFN_TC = 256, 1408


def _ffn_specs(S, order):
    tm, tc = FFN_TM, FFN_TC
    ij = (lambda a, b: (a, b)) if order == "ij" else (lambda a, b: (b, a))
    blk = pl.BlockSpec((2, tm, tc), lambda a, b: (0,) + ij(a, b))
    prev = pl.BlockSpec((2, 16, tc), lambda a, b: (0, jnp.maximum(ij(a, b)[0] * (tm // 16) - 1, 0), ij(a, b)[1]))
    nxt = pl.BlockSpec((2, 16, tc), lambda a, b: (0, jnp.minimum((ij(a, b)[0] + 1) * (tm // 16), S // 16 - 1), ij(a, b)[1]))
    wsp = pl.BlockSpec((2, 3, tc), lambda a, b: (0, 0, ij(a, b)[1]))
    bsp = pl.BlockSpec((2, 1, tc), lambda a, b: (0, 0, ij(a, b)[1]))
    act = pl.BlockSpec((tm, tc), lambda a, b: ij(a, b))
    return blk, prev, nxt, wsp, bsp, act


def _ffn_up(i, up_ref, uh_ref, w_ref, b_ref):
    keep = jnp.where(i > 0, 1.0, 0.0)
    out = []
    for half in range(2):
        u = up_ref[half].astype(F32)
        hu = uh_ref[half].astype(F32) * keep
        w0, w1, w2 = (w_ref[half, k:k + 1, :] for k in range(3))
        out.append(_conv3(u, hu, w0, w1, w2) + b_ref[half])
    return out


def ffn_act_fwd(up_pre, w_fc, b_fc):
    S = up_pre.shape[1]

    def body(up_ref, uh_ref, w_ref, b_ref, a_ref, gv_ref):
        g, v = _ffn_up(pl.program_id(0), up_ref, uh_ref, w_ref, b_ref)
        a_ref[...] = (g * jax.nn.sigmoid(g) * v).astype(BF16)
        gv_ref[0] = g.astype(BF16)
        gv_ref[1] = v.astype(BF16)

    blk, prev, _, wsp, bsp, act = _ffn_specs(S, "ij")
    return pl.pallas_call(body, grid=(S // FFN_TM, D_FF // FFN_TC), name="ffn_act_fwd",
                          in_specs=[blk, prev, wsp, bsp], out_specs=[act, blk],
                          out_shape=[SDS((S, D_FF), BF16), SDS((2, S, D_FF), BF16)],
                          compiler_params=_cp("parallel", "parallel"))(up_pre, up_pre, w_fc, b_fc)


def ffn_act_bwd(dact, up, up_pre, w_fc):
    S = up.shape[1]

    def act_grads(da, g, v):
        sg = jax.nn.sigmoid(g)
        return da * v * (sg * (1.0 + g * (1.0 - sg))), da * g * sg

    def body(da_ref, dan_ref, gv_ref, gvn_ref, up_ref, w_ref, dp_ref, dw_ref, db_ref):
        i = pl.program_id(1)
        keep_next = jnp.where(i < pl.num_programs(1) - 1, 1.0, 0.0)

        @pl.when(i == 0)
        def _():
            dw_ref[...] = jnp.zeros_like(dw_ref)
            db_ref[...] = jnp.zeros_like(db_ref)

        here = act_grads(da_ref[...].astype(F32), gv_ref[0].astype(F32), gv_ref[1].astype(F32))
        after = act_grads(dan_ref[...].astype(F32) * keep_next, gvn_ref[0].astype(F32), gvn_ref[1].astype(F32))
        for half in range(2):
            d, dn = here[half], after[half]
            u = up_ref[half].astype(F32)
            d1, d2 = _shift_up(d, 1, dn), _shift_up(d, 2, dn)
            w0, w1, w2 = (w_ref[half, k:k + 1, :] for k in range(3))
            dp_ref[half] = (d * w2 + d1 * w1 + d2 * w0).astype(BF16)
            dw_ref[half, 0:1, :] += _colsum(d2 * u)
            dw_ref[half, 1:2, :] += _colsum(d1 * u)
            dw_ref[half, 2:3, :] += _colsum(d * u)
            db_ref[half] += _colsum(d)

    tm, tc = FFN_TM, FFN_TC
    blk, _, nxt, wsp, bsp, act = _ffn_specs(S, "ji")
    act_next = pl.BlockSpec((16, tc), lambda j, i: (jnp.minimum((i + 1) * (tm // 16), S // 16 - 1), j))
    return pl.pallas_call(body, grid=(D_FF // tc, S // tm), name="ffn_act_bwd",
                          in_specs=[act, act_next, blk, nxt, blk, wsp], out_specs=[blk, wsp, bsp],
                          out_shape=[SDS((2, S, D_FF), BF16), SDS((2, 3, D_FF), F32), SDS((2, 1, D_FF), F32)],
                          compiler_params=_cp("parallel", "arbitrary"))(dact, dact, up, up, up_pre, w_fc)


def final_loss(x3, g, target):
    S, Dm = x3.shape
    tm = 512

    def body(x_ref, g_ref, t_ref, loss_ref, dx_ref, dg_ref):
        i = pl.program_id(0)
        xv = x_ref[...]
        r = lax.rsqrt(jnp.mean(xv * xv, axis=1, keepdims=True) + EPS)
        xh = xv * r
        err = xh * g_ref[...] - t_ref[...]
        dy = err / Dm
        gd = dy * g_ref[...]
        dx_ref[...] = r * (gd - xh * jnp.mean(gd * xh, axis=1, keepdims=True))

        @pl.when(i == 0)
        def _():
            dg_ref[...] = jnp.zeros_like(dg_ref)
            loss_ref[...] = jnp.zeros_like(loss_ref)

        dg_ref[...] += _colsum(dy * xh)
        loss_ref[...] += 0.5 * _colsum(jnp.mean(err * err, axis=1, keepdims=True))

    row = pl.BlockSpec((tm, Dm), lambda i: (i, 0))
    vec = pl.BlockSpec((1, Dm), lambda i: (0, 0))
    one = pl.BlockSpec((1, 1), lambda i: (0, 0))
    return pl.pallas_call(body, grid=(S // tm,), name="final_loss", in_specs=[row, vec, row], out_specs=[one, row, vec],
                          out_shape=[SDS((1, 1), F32), SDS((S, Dm), F32), SDS((1, Dm), F32)],
                          compiler_params=_cp("arbitrary"))(x3, g, target)


def local_step(x, mem, target, wb, ws, comm=None):
    S = x.shape[0]
    assert S % 2048 == 0
    if comm is None:
        comm = NoComm()
    else:
        wb = {}
    ws = dict(ws)

    bias = bias_tables(ws["rel_bias"])
    h1 = rmsnorm_fwd(x, ws["g_mix"], "norm_mix", job=comm.job("norm_mix"))
    comm.landed("norm_mix", wb, ws)
    w_fc = ws["w_ffn_conv"].reshape(3, 2, D_FF).transpose(1, 0, 2)
    b_fc = ws["b_ffn_conv"].reshape(2, 1, D_FF)
    proj = mm_nn(h1, wb["w_in"], "proj", out_dtype=BF16, out_bw=BLK, tn=768, job=comm.job("proj"))
    comm.landed("proj", wb)
    qkv = proj.reshape(2, 3, 4, S, BLK)
    o, lse = attn_fwd(qkv, bias, job=comm.job("attn_fwd"))
    comm.landed("attn_fwd", wb)
    proj4 = proj.reshape(2, 12, S, BLK)
    mixed = mixer_post_fwd(o, proj4, ws["w_short_conv"], ws["g_attn_out"], ws["g_conv_out"])
    x1 = mm_nn(mixed, wb["w_out"], "out_proj", res=x)
    h2 = rmsnorm_fwd(x1, ws["g_xattn"], "norm_xattn")
    mem_n = rmsnorm_fwd(mem, ws["g_mem"], "norm_mem")
    xq = mm_nn(h2, wb["w_xq"], "xq", out_dtype=BF16)
    xk = mm_nn(mem_n, wb["w_xk"], "xk", out_dtype=BF16, tn=1024)
    xv = mm_nn(mem_n, wb["w_xv"], "xv", out_dtype=BF16, tn=1024)
    xo = xattn_fwd(xq, xk, xv)
    x2 = mm_nn(xo, wb["w_xo"], "xo_proj", res=x1)
    h3 = rmsnorm_fwd(x2, ws["g_ffn"], "norm_ffn")
    up_pre = mm_nn(h3, wb["w_up"], "up_proj", out_dtype=BF16, out_bw=D_FF, tn=1408)
    act, up = ffn_act_fwd(up_pre, w_fc, b_fc)
    x3 = mm_nn(act, wb["w_down"], "down_proj", res=x2)
    loss, dx3, dg_final = final_loss(x3, ws["g_final"].reshape(1, -1), target)

    gb, gs = {}, {"g_final": dg_final}
    gb["w_down"] = mm_tn(act, dx3, None, "dw_down", tm=1408, tn=1024)
    dact = mm_nt(dx3, None, wb["w_down"], "d_act", out_dtype=BF16, tn=1408)
    d_up_pre, dw_fc, db_fc = ffn_act_bwd(dact, up, up_pre, w_fc)
    gs["b_ffn_conv"] = db_fc.reshape(1, 2 * D_FF)
    gs["w_ffn_conv"] = dw_fc.transpose(1, 0, 2).reshape(3, 2 * D_FF)
    gb["w_up"] = mm_tn(h3, d_up_pre, D_FF, "dw_up", shards=4, tn=1408)
    dx2, gs["g_ffn"] = mm_nt_norm_bwd(d_up_pre, D_FF, wb["w_up"], x2, ws["g_ffn"], dx3, "d_h3", tm=256,
                                      job=comm.job("d_h3", gb))
    comm.landed("d_h3")
    gb["w_xo"] = mm_tn(xo, dx2, None, "dw_xo")
    dxo = mm_nt(dx2, None, wb["w_xo"], "d_xo", out_dtype=BF16)
    dxq, dxk, dxv = xattn_bwd(xq, xk, xv, dxo)
    gb["w_xq"] = mm_tn(h2, dxq, None, "dw_xq")
    gb["w_xk"] = mm_tn(mem_n, dxk, None, "dw_xk", tn=1024)
    gb["w_xv"] = mm_tn(mem_n, dxv, None, "dw_xv", tn=1024)
    dmem_n = mm_nt(dxk, None, wb["w_xk"], "d_memk", tn=1024)
    dmem_n = mm_nt(dxv, None, wb["w_xv"], "d_memv", res=dmem_n, tn=1024)
    gs["g_mem"] = rmsnorm_bwd(mem, ws["g_mem"], dmem_n, None, "norm_mem_bwd")
    dx1, gs["g_xattn"] = mm_nt_norm_bwd(dxq, None, wb["w_xq"], x1, ws["g_xattn"], dx2, "d_h2")
    gb["w_out"] = mm_tn(mixed, dx1, None, "dw_out")
    dmixed = mm_nt(dx1, None, wb["w_out"], "d_mixed")
    do, dgb, dcv, gs["g_attn_out"], gs["g_conv_out"] = mixer_post_bwd_a(
        dmixed, o, proj4, ws["w_short_conv"], ws["g_attn_out"], ws["g_conv_out"], job=comm.job("mixer_post_bwd_a", gb))
    comm.landed("mixer_post_bwd_a")
    dproj, ds_sum = attn_bwd(qkv, bias, o, lse, do, job=comm.job("attn_bwd"))
    comm.landed("attn_bwd")
    gs["rel_bias"] = bias_tables_bwd(ds_sum)
    dproj, gs["w_short_conv"] = mixer_post_bwd_b(dproj.reshape(2, 12, S, BLK), dgb, dcv, proj4, ws["w_short_conv"])
    dproj = dproj.reshape(24, S, BLK)
    gb["w_in"] = mm_tn(h1, dproj, BLK, "dw_in", shards=4, tn=768, job=comm.job("dw_in"))
    comm.landed("dw_in")
    comm.first_update(gb)
    grad_x, gs["g_mix"] = mm_nt_norm_bwd(dproj, BLK, wb["w_in"], x, ws["g_mix"], dx1, "d_h1", tm=256,
                                         job=comm.job("d_h1", gb))
    comm.landed("d_h1")
    return loss, grad_x, comm.finish(gb), gs


def adamw(w, g, m, v, name, job=None):
    R, C = w.shape
    tr = R
    for cand in (256, 352):
        if R > cand and R % cand == 0:
            tr = cand
            break

    def body(w_ref, g_ref, m_ref, v_ref, d_ref, nm_ref, nv_ref):
        gv = g_ref[...]
        mn = ADAM_B1 * m_ref[...] + (1.0 - ADAM_B1) * gv
        vn = ADAM_B2 * v_ref[...] + (1.0 - ADAM_B2) * (gv * gv)
        m_hat = mn / (1.0 - ADAM_B1 ** ADAM_STEP)
        v_hat = vn / (1.0 - ADAM_B2 ** ADAM_STEP)
        d_ref[...] = -ADAM_LR * (m_hat / (jnp.sqrt(v_hat) + ADAM_EPS) + ADAM_WD * w_ref[...])
        nm_ref[...] = mn
        nv_ref[...] = vn

    blk = pl.BlockSpec((tr, C), lambda i: (i, 0))
    return _pcall(body, [w, g, m, v], grid=(R // tr,), name=name, in_specs=[blk] * 4, out_specs=[blk] * 3,
                  out_shape=[SDS((R, C), F32)] * 3, sem=("parallel",), job=job)


BIG = ("w_in", "w_out", "w_xq", "w_xk", "w_xv", "w_xo", "w_up", "w_down")
COL_SHARDED = ("w_in", "w_up")
N_BIG = len(BIG)
ANY = pl.BlockSpec(memory_space=pl.ANY)


def _place():
    x, y, c = lax.axis_index("x"), lax.axis_index("y"), lax.axis_index("c")
    chips = [(1 - x, y), (x, 1 - y), (1 - x, 1 - y)]
    return x, y, c, chips


def _window(full, name, R, C, shard, half):
    r0, nr = (0, R) if half is None else (half * (R // 2), R // 2)
    if name in COL_SHARDED:
        return full.at[pl.ds(r0, nr), pl.ds(shard * C, C)]
    return full.at[pl.ds(shard * R + r0, nr), :]


def place_shard(w, mine_arr, n):
    R, C = w.shape
    col = n in COL_SHARDED

    def body(s_ref, w_ref, o_ref):
        o_ref[...] = w_ref[...].astype(BF16)

    grid_spec = pltpu.PrefetchScalarGridSpec(
        num_scalar_prefetch=1, grid=(1,), in_specs=[pl.BlockSpec((R, C), lambda i, s: (0, 0))],
        out_specs=pl.BlockSpec((R, C), (lambda i, s: (0, s[0])) if col else (lambda i, s: (s[0], 0))))
    return pl.pallas_call(body, grid_spec=grid_spec, name="place_" + n,
                          out_shape=SDS((R, 4 * C) if col else (4 * R, C), BF16),
                          compiler_params=_cp("arbitrary"))(mine_arr, w)


def _gather_jobs(names, shapes):
    nw = len(names)

    def start(full, sems):
        send, recv, fsend, frecv = sems
        x, y, c, chips = _place()
        mine = 2 * x + y
        for w, n in enumerate(names):
            R, C = shapes[w]
            own = _window(full[w], n, R, C, mine, c)
            for j, chip in enumerate(chips):
                pltpu.make_async_remote_copy(src_ref=own, dst_ref=own, send_sem=send.at[w, j], recv_sem=recv.at[w, j],
                                             device_id=(*chip, c), device_id_type=MESH).start()

    def mid(full, sems):
        send, recv, fsend, frecv = sems
        x, y, c, chips = _place()
        sib = (x, y, 1 - c)
        for w, n in enumerate(names):
            R, C = shapes[w]
            for j, chip in enumerate(chips):
                landed = _window(full[w], n, R, C, 2 * chip[0] + chip[1], c)
                pltpu.make_async_remote_copy(src_ref=landed, dst_ref=landed, send_sem=send.at[w, j], recv_sem=recv.at[w, j],
                                             device_id=(*chip, c), device_id_type=MESH).wait_recv()
                pltpu.make_async_remote_copy(src_ref=landed, dst_ref=landed, send_sem=fsend.at[w, j],
                                             recv_sem=frecv.at[w, j], device_id=sib, device_id_type=MESH).start()

    def finish(full, sems):
        send, recv, fsend, frecv = sems
        x, y, c, chips = _place()
        mine = 2 * x + y
        sib = (x, y, 1 - c)
        for w, n in enumerate(names):
            R, C = shapes[w]
            own = _window(full[w], n, R, C, mine, c)
            for j, chip in enumerate(chips):
                landed = _window(full[w], n, R, C, 2 * chip[0] + chip[1], c)
                other = _window(full[w], n, R, C, 2 * chip[0] + chip[1], 1 - c)
                pltpu.make_async_remote_copy(src_ref=other, dst_ref=other, send_sem=fsend.at[w, j], recv_sem=frecv.at[w, j],
                                             device_id=sib, device_id_type=MESH).wait_recv()
                pltpu.make_async_remote_copy(src_ref=own, dst_ref=own, send_sem=send.at[w, j], recv_sem=recv.at[w, j],
                                             device_id=(*chip, c), device_id_type=MESH).wait_send()
                pltpu.make_async_remote_copy(src_ref=landed, dst_ref=landed, send_sem=fsend.at[w, j],
                                             recv_sem=frecv.at[w, j], device_id=sib, device_id_type=MESH).wait_send()

    return start, mid, finish, [pltpu.SemaphoreType.DMA((nw, 3))] * 4


class CommJob:
    def __init__(self, ins, out_shapes, inplace, start, finish, sems, mid=None):
        self.ins, self.out_shapes, self.inplace = list(ins), list(out_shapes), inplace
        self.start, self.mid, self.finish, self.sems = start, mid, finish, list(sems)

    def run(self, phase, ins, outs, sems):
        if phase == 0:
            self.start(ins, outs, sems)
        elif phase == 1:
            if self.mid is not None:
                self.mid(ins, outs, sems)
        else:
            self.finish(ins, outs, sems)


def run_job(name, job):
    n_in, n_out = len(job.ins), len(job.out_shapes)

    def body(*refs):
        ins, outs, sm = refs[:n_in], refs[n_in:n_in + n_out], refs[n_in + n_out:]
        for phase in range(3):
            job.run(phase, ins, outs, sm)

    job.landed = list(pl.pallas_call(
        body, name=name, in_specs=[ANY] * n_in, out_specs=[ANY] * n_out, out_shape=job.out_shapes,
        input_output_aliases={i: i for i in range(n_in)} if job.inplace else {},
        scratch_shapes=job.sems)(*job.ins))
    return job.landed


class Carried:
    def __init__(self, job, n_in, n_out, n_scratch, steps):
        self.job, self.n_in, self.n_out, self.n_scratch, self.steps = job, n_in, n_out, n_scratch, steps
        self.nji = len(job.ins) if job else 0
        self.njo = len(job.out_shapes) if job else 0

    def in_specs(self):
        return [ANY] * self.nji

    def out_specs(self):
        return [ANY] * self.njo

    def out_shapes(self):
        return list(self.job.out_shapes) if self.job else []

    def aliases(self):
        if not (self.job and self.job.inplace):
            return {}
        return {self.n_in + i: self.n_out + i for i in range(self.nji)}

    def sems(self):
        return list(self.job.sems) if self.job else []

    def args(self):
        return list(self.job.ins) if self.job else []

    def split(self, refs):
        a = self.n_in
        b = a + self.nji
        c = b + self.n_out
        d = c + self.njo
        e = d + self.n_scratch
        return refs[:a] + refs[b:c] + refs[d:e], (refs[a:b], refs[c:d], refs[e:])

    def phase(self, phase, step, jrefs):
        if self.job is None:
            return
        at = {0: 0, 1: max(self.steps - 2, 0), 2: self.steps - 1}[phase]

        @pl.when(step == at)
        def _():
            self.job.run(phase, *jrefs)


def _pcall(body, args, *, grid, in_specs, out_specs, out_shape, name, sem, scratch=(), aliases=None, job=None):
    n_in, n_out = len(args), len(out_shape)
    if job is None:
        return pl.pallas_call(
            body, grid=grid, in_specs=list(in_specs), out_specs=list(out_specs), out_shape=list(out_shape), name=name,
            scratch_shapes=list(scratch), input_output_aliases=dict(aliases or {}), compiler_params=_cp(*sem))(*args)
    total = int(np.prod(grid))
    car = Carried(job, n_in, n_out, len(scratch), total)

    def wrapped(*refs):
        main, jrefs = car.split(refs)
        lin = pl.program_id(0)
        for ax in range(1, len(grid)):
            lin = lin * grid[ax] + pl.program_id(ax)
        car.phase(0, lin, jrefs)
        body(*main)
        car.phase(1, lin, jrefs)
        car.phase(2, lin, jrefs)

    res = pl.pallas_call(
        wrapped, grid=grid, in_specs=list(in_specs) + car.in_specs(), out_specs=list(out_specs) + car.out_specs(),
        out_shape=list(out_shape) + car.out_shapes(), name=name, scratch_shapes=list(scratch) + car.sems(),
        input_output_aliases={**dict(aliases or {}), **car.aliases()},
        compiler_params=_cp(*(["arbitrary"] * len(grid))))(*args, *car.args())
    job.landed = list(res[n_out:])
    return list(res[:n_out])


def gather_job(placed, names):
    shapes = []
    for n in names:
        R, C = placed[n].shape
        shapes.append((R, C // 4) if n in COL_SHARDED else (R // 4, C))
    start, mid, finish, sems = _gather_jobs(names, shapes)
    arrays = [placed[n] for n in names]
    return CommJob(arrays, [SDS(a.shape, a.dtype) for a in arrays], True,
                   lambda i, o, s: start(o, s), lambda i, o, s: finish(o, s), sems, mid=lambda i, o, s: mid(o, s))


def gather_rows_job(placed):
    widths = [a.shape[1] // 4 for a in placed]

    def copies(outs, sems):
        x, y, c, chips = _place()
        mine = 2 * x + y
        cps = []
        for w, ref in enumerate(outs):
            own = ref.at[:, pl.ds(mine * widths[w], widths[w])]
            for j, chip in enumerate(chips):
                theirs = ref.at[:, pl.ds((2 * chip[0] + chip[1]) * widths[w], widths[w])]
                kw = dict(send_sem=sems[0].at[w, j], recv_sem=sems[1].at[w, j], device_id=(*chip, c), device_id_type=MESH)
                cps.append((pltpu.make_async_remote_copy(src_ref=own, dst_ref=own, **kw),
                            pltpu.make_async_remote_copy(src_ref=theirs, dst_ref=theirs, **kw)))
        return cps

    def start(ins, outs, sems):
        for send, _ in copies(outs, sems):
            send.start()

    def finish(ins, outs, sems):
        for send, recv in copies(outs, sems):
            recv.wait_recv()
            send.wait_send()

    return CommJob(placed, [SDS(a.shape, a.dtype) for a in placed], True, start, finish,
                   [pltpu.SemaphoreType.DMA((len(placed), 3))] * 2)


def merge_jobs(a, b):
    assert a.inplace and b.inplace
    na, nb_, nsa = len(a.ins), len(b.ins), len(a.sems)

    def phase(which):
        def run(ins, outs, sems):
            for job, i, o, s in ((a, ins[:na], outs[:na], sems[:nsa]), (b, ins[na:], outs[na:], sems[nsa:])):
                fn = getattr(job, which)
                if fn is not None:
                    fn(i, o, s)
        return run

    merged = CommJob(a.ins + b.ins, a.out_shapes + b.out_shapes, True, phase("start"), phase("finish"), a.sems + b.sems,
                     mid=phase("mid"))
    merged.parts = (a, b, na)
    return merged


def split_landed(merged):
    a, b, na = merged.parts
    a.landed, b.landed = merged.landed[:na], merged.landed[na:]


def pair_exchange_job(grads, names):
    shapes = [grads[n].shape for n in names]

    def copies(ins, outs, sems):
        x, y, c, _ = _place()
        return [pltpu.make_async_remote_copy(
            src_ref=ins[w].at[:, pl.ds((1 - c) * (shapes[w][1] // 2), shapes[w][1] // 2), :], dst_ref=outs[w],
            send_sem=sems[0].at[w], recv_sem=sems[1].at[w], device_id=(x, y, 1 - c), device_id_type=MESH)
            for w in range(len(names))]

    def start(ins, outs, sems):
        for cp in copies(ins, outs, sems):
            cp.start()

    def finish(ins, outs, sems):
        for cp in copies(ins, outs, sems):
            cp.wait()

    return CommJob([grads[n] for n in names], [SDS((4, s[1] // 2, s[2]), F32) for s in shapes], False, start, finish,
                   [pltpu.SemaphoreType.DMA((len(names),))] * 2)


def chip_exchange_job(p16, names):
    shapes = [p16[n].shape for n in names]

    def copies(ins, outs, sems):
        x, y, c, chips = _place()
        return [pltpu.make_async_remote_copy(
            src_ref=ins[w].at[2 * chip[0] + chip[1]], dst_ref=outs[w].at[j],
            send_sem=sems[0].at[w, j], recv_sem=sems[1].at[w, j], device_id=(*chip, c), device_id_type=MESH)
            for w in range(len(names)) for j, chip in enumerate(chips)]

    def start(ins, outs, sems):
        for cp in copies(ins, outs, sems):
            cp.start()

    def finish(ins, outs, sems):
        for cp in copies(ins, outs, sems):
            cp.wait()

    return CommJob([p16[n] for n in names], [SDS((3,) + tuple(s[1:]), BF16) for s in shapes], False, start, finish,
                   [pltpu.SemaphoreType.DMA((len(names), 3))] * 2)


def half_exchange_job(full, names):
    shapes = [full[n].shape for n in names]

    def copies(outs, sems):
        x, y, c, _ = _place()
        cps = []
        for w in range(len(names)):
            Rh = shapes[w][0] // 2
            rows = outs[w].at[pl.ds(c * Rh, Rh), :]
            other = outs[w].at[pl.ds((1 - c) * Rh, Rh), :]
            cps.append((pltpu.make_async_remote_copy(src_ref=rows, dst_ref=rows, send_sem=sems[0].at[w], recv_sem=sems[1].at[w],
                                                     device_id=(x, y, 1 - c), device_id_type=MESH),
                        pltpu.make_async_remote_copy(src_ref=other, dst_ref=other, send_sem=sems[0].at[w], recv_sem=sems[1].at[w],
                                                     device_id=(x, y, 1 - c), device_id_type=MESH)))
        return cps

    def start(ins, outs, sems):
        for send, _ in copies(outs, sems):
            send.start()

    def finish(ins, outs, sems):
        for send, recv in copies(outs, sems):
            recv.wait_recv()
            send.wait_send()

    arrays = [full[n] for n in names]
    return CommJob(arrays, [SDS(a.shape, a.dtype) for a in arrays], True, start, finish,
                   [pltpu.SemaphoreType.DMA((len(names),))] * 2)


def allreduce_small(parts, name, job=None):
    n = len(parts)
    nji = len(job.ins) if job else 0
    njo = len(job.out_shapes) if job else 0

    def body(*refs):
        ins, jins = refs[:n], refs[n:n + nji]
        outs, jouts = refs[n + nji:2 * n + nji], refs[2 * n + nji:2 * n + nji + njo]
        bufs = refs[2 * n + nji + njo:3 * n + nji + njo]
        send, recv = refs[3 * n + nji + njo:3 * n + nji + njo + 2]
        jsems = refs[3 * n + nji + njo + 2:]
        x, y, c, _ = _place()
        me = 4 * x + 2 * y + c
        if job is not None:
            job.run(0, jins, jouts, jsems)
        sends = []
        for p in range(n):
            for k in range(1, 8):
                peer = (x ^ (k >> 2), y ^ ((k >> 1) & 1), c ^ (k & 1))
                cp = pltpu.make_async_remote_copy(src_ref=ins[p], dst_ref=bufs[p].at[k], send_sem=send.at[p, k - 1],
                                                  recv_sem=recv.at[p, k - 1], device_id=peer, device_id_type=MESH)
                cp.start()
                sends.append(cp)
            bufs[p][0] = ins[p][...]
        for cp in sends:
            cp.wait_recv()
        for p in range(n):
            acc = bufs[p][me]
            for d in range(1, 8):
                acc = acc + bufs[p][d ^ me]
            outs[p][...] = acc
        for cp in sends:
            cp.wait_send()
        if job is not None:
            job.run(1, jins, jouts, jsems)
            job.run(2, jins, jouts, jsems)

    vm = pl.BlockSpec(memory_space=pltpu.VMEM)
    res = pl.pallas_call(
        body, name=name, in_specs=[vm] * n + [ANY] * nji, out_specs=[vm] * n + [ANY] * njo,
        out_shape=[SDS(a.shape, F32) for a in parts] + (list(job.out_shapes) if job else []),
        input_output_aliases={n + i: n + i for i in range(nji)} if (job and job.inplace) else {},
        scratch_shapes=[pltpu.VMEM((8,) + tuple(a.shape), F32) for a in parts] + [pltpu.SemaphoreType.DMA((n, 7))] * 2
        + (list(job.sems) if job else []),
    )(*parts, *(job.ins if job else []))
    if job is not None:
        job.landed = list(res[n:])
    return list(res[:n])


def adamw_multi(ws, gs, ms, vs, name):
    n = len(ws)

    def body(*refs):
        for p in range(n):
            w_ref, g_ref, m_ref, v_ref = (refs[q * n + p] for q in range(4))
            d_ref, nm_ref, nv_ref = (refs[(4 + q) * n + p] for q in range(3))
            gv = g_ref[...]
            mn = ADAM_B1 * m_ref[...] + (1.0 - ADAM_B1) * gv
            vn = ADAM_B2 * v_ref[...] + (1.0 - ADAM_B2) * (gv * gv)
            m_hat = mn / (1.0 - ADAM_B1 ** ADAM_STEP)
            v_hat = vn / (1.0 - ADAM_B2 ** ADAM_STEP)
            d_ref[...] = -ADAM_LR * (m_hat / (jnp.sqrt(v_hat) + ADAM_EPS) + ADAM_WD * w_ref[...])
            nm_ref[...] = mn
            nv_ref[...] = vn

    vm = pl.BlockSpec(memory_space=pltpu.VMEM)
    outs = pl.pallas_call(body, name=name, in_specs=[vm] * (4 * n), out_specs=[vm] * (3 * n),
                          out_shape=[SDS(a.shape, F32) for a in ws] * 3)(*ws, *gs, *ms, *vs)
    return outs[:n], outs[n:2 * n], outs[2 * n:]


def pair_sum(g, xh, sel_arr, name):
    _, R, C = g.shape
    Rh = R // 2

    def body(sel_ref, g_ref, x_ref, p32_ref, p16_ref):
        s = g_ref[...] + x_ref[...]
        p16_ref[...] = s.astype(BF16)

        @pl.when(pl.program_id(0) == sel_ref[0])
        def _():
            p32_ref[...] = s

    same = pl.BlockSpec((None, Rh, C), lambda s, sel: (s, 0, 0))
    grid_spec = pltpu.PrefetchScalarGridSpec(
        num_scalar_prefetch=1, grid=(4,),
        in_specs=[pl.BlockSpec((None, Rh, C), lambda s, sel: (s, sel[1], 0)), same],
        out_specs=[pl.BlockSpec((Rh, C), lambda s, sel: (0, 0)), same])
    return pl.pallas_call(body, grid_spec=grid_spec, name=name, out_shape=[SDS((Rh, C), F32), SDS((4, Rh, C), BF16)],
                          compiler_params=_cp("arbitrary"))(sel_arr, g, xh)


def chip_sum(p32, y, sel_arr, name):
    Rh, C = p32.shape

    def body(s_ref, p_ref, y_ref, o_ref):
        o_ref[...] = ((p_ref[...] + y_ref[0].astype(F32)) + y_ref[1].astype(F32)) + y_ref[2].astype(F32)

    grid_spec = pltpu.PrefetchScalarGridSpec(
        num_scalar_prefetch=1, grid=(1,),
        in_specs=[pl.BlockSpec((Rh, C), lambda i, s: (0, 0)), pl.BlockSpec((3, Rh, C), lambda i, s: (0, 0, 0))],
        out_specs=pl.BlockSpec((Rh, C), lambda i, s: (s[1], 0)))
    return pl.pallas_call(body, grid_spec=grid_spec, name=name, out_shape=SDS((2 * Rh, C), F32),
                          compiler_params=_cp("arbitrary"))(sel_arr, p32, y)


class StepComm:
    FIRST = ("w_in",)
    SQUARE = ("w_out", "w_xq", "w_xk", "w_xv", "w_xo")
    FFN = ("w_up", "w_down")
    PLAN = {
        "proj": ("gather", SQUARE[:4]), "attn_fwd": ("gather", SQUARE[4:] + FFN),
        "d_h3": ("pair", FFN), "mixer_post_bwd_a": ("pair", SQUARE), "attn_bwd": ("chip", FFN + SQUARE),
        "dw_in": ("half", FFN + SQUARE), "first_update": ("pair", FIRST), "d_h1": ("chip", FIRST),
    }

    def __init__(self, placed, conv_placed, sel_arr):
        self.placed, self.conv_placed, self.sel_arr = placed, conv_placed, sel_arr
        self.g, self.p32, self.p16, self.full, self.done, self.jobs = {}, {}, {}, {}, {}, {}

    def job(self, tag, gb=None):
        if tag == "norm_mix":
            self.jobs[tag] = merge_jobs(gather_job(self.placed, self.FIRST), gather_rows_job(self.conv_placed))
            return self.jobs[tag]
        kind, names = self.PLAN[tag]
        if kind == "gather":
            j = gather_job(self.placed, names)
        elif kind == "pair":
            for n in names:
                self.g[n] = gb[n] if gb[n].ndim == 3 else gb[n].reshape(4, gb[n].shape[0] // 4, gb[n].shape[1])
            j = pair_exchange_job(self.g, names)
        elif kind == "chip":
            j = chip_exchange_job(self.p16, names)
        else:
            j = half_exchange_job(self.full, names)
        self.jobs[tag] = j
        return j

    def landed(self, tag, wb=None, ws=None):
        if tag == "norm_mix":
            split_landed(self.jobs[tag])
            first, conv, _ = self.jobs[tag].parts
            wb.update(zip(self.FIRST, first.landed))
            ws.update((n, a[:3]) for n, a in zip(SMALL_SHARDED, conv.landed))
            return
        kind, names = self.PLAN[tag]
        for n, a in zip(names, self.jobs[tag].landed):
            if kind == "gather":
                wb[n] = a
            elif kind == "pair":
                self.p32[n], self.p16[n] = pair_sum(self.g[n], a, self.sel_arr, "pair_sum_" + n)
            elif kind == "chip":
                self.full[n] = chip_sum(self.p32[n], a, self.sel_arr, "chip_sum_" + n)
            else:
                self.done[n] = a

    def first_update(self, gb):
        self.first_result = self.update_fn(self.job("first_update", gb))
        self.landed("first_update")

    def finish(self, gb):
        return self.done

    def last_job(self):
        self.jobs["end_half"] = half_exchange_job(self.full, self.FIRST)
        return self.jobs["end_half"]

    def take_last(self):
        self.done.update(zip(self.FIRST, self.jobs["end_half"].landed))


class NoComm:
    def job(self, tag, gb=None):
        return None

    def landed(self, tag, wb=None, ws=None):
        pass

    def first_update(self, gb):
        pass

    def finish(self, gb):
        return gb


SMALL = ("rel_bias", "g_mix", "w_short_conv", "g_attn_out", "g_conv_out", "g_xattn", "g_mem", "g_ffn",
         "w_ffn_conv", "b_ffn_conv", "g_final")
SMALL_FULL = {"rel_bias": (8, 32), "g_mix": (1, 1024), "w_short_conv": (3, 512), "g_attn_out": (1, 512), "g_conv_out": (1, 512),
              "g_xattn": (1, 1024), "g_mem": (1, 1024), "g_ffn": (1, 1024), "w_ffn_conv": (3, 5632), "b_ffn_conv": (1, 5632),
              "g_final": (1024,)}
SMALL_SHARDED = ("w_short_conv", "w_ffn_conv")


def _pack(parts):
    flat = jnp.concatenate([p.reshape(-1) for p in parts])
    rows = -(-flat.shape[0] // 1024) * 8
    return jnp.pad(flat, (0, rows * 128 - flat.shape[0])).reshape(rows, 128)


def _unpack(packed, names):
    flat, out, off = packed.reshape(-1), {}, 0
    for n in names:
        size = int(np.prod(SMALL_FULL[n]))
        out[n] = flat[off:off + size].reshape(SMALL_FULL[n])
        off += size
    return out


def kernel(x, mem, rel_bias, g_mix, w_in, w_short_conv, g_attn_out, g_conv_out, w_out, g_xattn, g_mem, w_xq, w_xk, w_xv, w_xo, g_ffn, w_up, w_ffn_conv, b_ffn_conv, w_down, g_final, loss_target, m_rel_bias, m_g_mix, m_w_in, m_w_short_conv, m_g_attn_out, m_g_conv_out, m_w_out, m_g_xattn, m_g_mem, m_w_xq, m_w_xk, m_w_xv, m_w_xo, m_g_ffn, m_w_up, m_w_ffn_conv, m_b_ffn_conv, m_w_down, m_g_final, v_rel_bias, v_g_mix, v_w_in, v_w_short_conv, v_g_attn_out, v_g_conv_out, v_w_out, v_g_xattn, v_g_mem, v_w_xq, v_w_xk, v_w_xv, v_w_xo, v_g_ffn, v_w_up, v_w_ffn_conv, v_b_ffn_conv, v_w_down, v_g_final):
    names = ("rel_bias", "g_mix", "w_in", "w_short_conv", "g_attn_out", "g_conv_out", "w_out", "g_xattn", "g_mem", "w_xq",
             "w_xk", "w_xv", "w_xo", "g_ffn", "w_up", "w_ffn_conv", "b_ffn_conv", "w_down", "g_final")
    W = dict(zip(names, (rel_bias, g_mix, w_in, w_short_conv, g_attn_out, g_conv_out, w_out, g_xattn, g_mem, w_xq, w_xk, w_xv,
                         w_xo, g_ffn, w_up, w_ffn_conv, b_ffn_conv, w_down, g_final)))
    M = dict(zip(names, (m_rel_bias, m_g_mix, m_w_in, m_w_short_conv, m_g_attn_out, m_g_conv_out, m_w_out, m_g_xattn, m_g_mem,
                         m_w_xq, m_w_xk, m_w_xv, m_w_xo, m_g_ffn, m_w_up, m_w_ffn_conv, m_b_ffn_conv, m_w_down, m_g_final)))
    V = dict(zip(names, (v_rel_bias, v_g_mix, v_w_in, v_w_short_conv, v_g_attn_out, v_g_conv_out, v_w_out, v_g_xattn, v_g_mem,
                         v_w_xq, v_w_xk, v_w_xv, v_w_xo, v_g_ffn, v_w_up, v_w_ffn_conv, v_b_ffn_conv, v_w_down, v_g_final)))
    xi, yi, ci = lax.axis_index("x"), lax.axis_index("y"), lax.axis_index("c")
    mine = 2 * xi + yi
    mine_arr = jnp.reshape(mine, (1,)).astype(jnp.int32)
    sel_arr = jnp.stack([mine, ci]).astype(jnp.int32)
    conv_placed = []
    for n in SMALL_SHARDED:
        shard = W[n][0]
        conv_placed.append(lax.dynamic_update_slice(jnp.zeros((8, SMALL_FULL[n][1]), F32), shard, (0, mine * shard.shape[1])))
    comm = StepComm({n: place_shard(W[n][0], mine_arr, n) for n in BIG}, conv_placed, sel_arr)
    ws = {n: W[n] for n in SMALL if n not in SMALL_SHARDED}

    comm.update_fn = lambda job: adamw(W["w_up"][0], comm.done["w_up"], M["w_up"][0], V["w_up"][0], "adamw_w_up", job=job)
    loss, grad_x, gfull, gs = local_step(x[0], mem[0], loss_target[0], None, ws, comm)

    def as2d(a):
        return a.reshape(1, -1) if a.ndim == 1 else a

    reduced = allreduce_small([as2d(gs[n]) for n in SMALL] + [loss], "reduce_small", job=comm.last_job())
    comm.take_last()
    gsm = dict(zip(SMALL, reduced[:-1]))
    loss = reduced[-1][0, 0]

    grads, delta, new_m, new_v = {}, {}, {}, {}
    for n in BIG:
        d, nm, nv = comm.first_result if n == "w_up" else adamw(W[n][0], gfull[n], M[n][0], V[n][0], "adamw_" + n)
        grads[n], delta[n], new_m[n], new_v[n] = gfull[n][None], d[None], nm[None], nv[None]
    for n in SMALL_SHARDED:
        wid = W[n].shape[2]
        gsm[n] = lax.dynamic_slice(gsm[n], (0, mine * wid), (3, wid))

    def own(a, n):
        return a[0] if n in SMALL_SHARDED else as2d(a)

    d, nm, nv = adamw_multi([own(W[n], n) for n in SMALL], [gsm[n] for n in SMALL], [own(M[n], n) for n in SMALL],
                            [own(V[n], n) for n in SMALL], "adamw_small")
    for i, n in enumerate(SMALL):
        shape = W[n].shape
        grads[n], delta[n], new_m[n], new_v[n] = (a.reshape(shape) for a in (gsm[n], d[i], nm[i], nv[i]))
    return (loss, grad_x[None], *[grads[n] for n in names], *[delta[n] for n in names],
            *[new_m[n] for n in names], *[new_v[n] for n in names])
```

```python
import functools
import math

import numpy as np
import jax
import jax.numpy as jnp
from jax import lax
from jax.experimental import pallas as pl
from jax.experimental.pallas import tpu as pltpu

F32 = jnp.float32
BF16 = jnp.bfloat16
SDS = jax.ShapeDtypeStruct
MESH = pl.DeviceIdType.MESH

D_MODEL = 1024
ATTN_W = 512
N_HEADS = 8
BLK = 128
PATTERNS = ((128, 1), (512, 4), (2048, 16))
N_BUCKETS = 32
D_FF = 2816
N_MEM_HEADS = 4
MEM_HD = 256
EPS = 1e-6
NEG = -1e30
VMEM_LIMIT = 56 * 1024 * 1024

ADAM_LR, ADAM_B1, ADAM_B2, ADAM_EPS, ADAM_WD, ADAM_STEP = 0.001, 0.9, 0.999, 1e-08, 0.01, 10


def _cp(*sem):
    return pltpu.CompilerParams(dimension_semantics=sem, vmem_limit_bytes=VMEM_LIMIT)


def _dot(a, b):
    return jnp.dot(a, b, preferred_element_type=F32)


def _dot_nt(a, b):
    return lax.dot_general(a, b, (((1,), (1,)), ((), ())), preferred_element_type=F32)


def _dot_tn(a, b):
    return lax.dot_general(a, b, (((0,), (0,)), ((), ())), preferred_element_type=F32)


def _rsum(x):
    return jnp.sum(x, axis=1, keepdims=True)


def rmsnorm_fwd(x, g, name, job=None):
    S, Dm = x.shape
    tm = min(S, 512)

    def body(x_ref, g_ref, o_ref):
        xv = x_ref[...]
        r = lax.rsqrt(jnp.mean(xv * xv, axis=1, keepdims=True) + EPS)
        o_ref[...] = (xv * r * g_ref[...]).astype(o_ref.dtype)

    return _pcall(body, [x, g], grid=(S // tm,), name=name,
                  in_specs=[pl.BlockSpec((tm, Dm), lambda i: (i, 0)), pl.BlockSpec((1, Dm), lambda i: (0, 0))],
                  out_specs=[pl.BlockSpec((tm, Dm), lambda i: (i, 0))], out_shape=[SDS((S, Dm), BF16)],
                  sem=("parallel",), job=job)[0]


def rmsnorm_bwd(x, g, dh, res, name, job=None):
    S, Dm = x.shape
    tm = min(S, 512)
    want_dx = res is not None

    def body(*refs):
        if want_dx:
            x_ref, g_ref, dh_ref, res_ref, dx_ref, dg_ref = refs
        else:
            x_ref, g_ref, dh_ref, dg_ref = refs
        i = pl.program_id(0)
        xv = x_ref[...]
        r = lax.rsqrt(jnp.mean(xv * xv, axis=1, keepdims=True) + EPS)
        xh = xv * r
        dh = dh_ref[...].astype(F32)
        if want_dx:
            gd = dh * g_ref[...]
            dx_ref[...] = res_ref[...] + r * (gd - xh * jnp.mean(gd * xh, axis=1, keepdims=True))

        @pl.when(i == 0)
        def _():
            dg_ref[...] = jnp.zeros_like(dg_ref)

        dg_ref[...] += jnp.sum(dh * xh, axis=0, keepdims=True)

    row = pl.BlockSpec((tm, Dm), lambda i: (i, 0))
    vec = pl.BlockSpec((1, Dm), lambda i: (0, 0))
    if want_dx:
        return _pcall(body, [x, g, dh, res], grid=(S // tm,), name=name, in_specs=[row, vec, row, row], out_specs=[row, vec],
                      out_shape=[SDS((S, Dm), F32), SDS((1, Dm), F32)], sem=("arbitrary",), job=job)
    return pl.pallas_call(
        body, grid=(S // tm,), name=name, in_specs=[row, vec, row], out_specs=vec,
        out_shape=SDS((1, Dm), F32), compiler_params=_cp("arbitrary"))(x, g, dh)


def _col_spec(bw, tn, rows, row_of, col_of):
    if bw is None:
        return pl.BlockSpec((rows, tn), lambda *g: (row_of(*g), col_of(*g)))
    if tn % bw == 0:
        return pl.BlockSpec((tn // bw, rows, bw), lambda *g: (col_of(*g), row_of(*g), 0))
    per = bw // tn
    return pl.BlockSpec((None, rows, tn), lambda *g: (col_of(*g) // per, row_of(*g), col_of(*g) % per))


def _read_cols(ref, bw, tn):
    if bw is None or tn % bw != 0:
        return ref[...]
    if tn == bw:
        return ref[0]
    return jnp.concatenate([ref[q] for q in range(tn // bw)], axis=1)


def _write_cols(ref, bw, tn, val):
    if bw is None or tn % bw != 0:
        ref[...] = val.astype(ref.dtype)
    else:
        for q in range(tn // bw):
            ref[q] = val[:, q * bw:(q + 1) * bw].astype(ref.dtype)


def mm_nn(a, b, name, *, res=None, out_dtype=F32, out_bw=None, tm=1024, tn=512, job=None):
    M, K = a.shape
    N = b.shape[1]
    tm = min(tm, M)

    def body(*refs):
        if res is None:
            a_ref, b_ref, o_ref = refs
        else:
            a_ref, b_ref, r_ref, o_ref = refs
        acc = _dot(a_ref[...].astype(BF16), b_ref[...])
        if res is not None:
            acc = acc + r_ref[...]
        _write_cols(o_ref, out_bw, tn, acc)

    ri, ci = (lambda i, j: i), (lambda i, j: j)
    in_specs = [pl.BlockSpec((tm, K), lambda i, j: (i, 0)), pl.BlockSpec((K, tn), lambda i, j: (0, j))]
    args = [a, b]
    if res is not None:
        in_specs.append(pl.BlockSpec((tm, tn), lambda i, j: (i, j)))
        args.append(res)
    oshape = (M, N) if out_bw is None else (N // out_bw, M, out_bw)
    return _pcall(body, args, grid=(M // tm, N // tn), name=name, in_specs=in_specs,
                  out_specs=[_col_spec(out_bw, tn, tm, ri, ci)], out_shape=[SDS(oshape, out_dtype)],
                  sem=("parallel", "parallel"), job=job)[0]


def mm_nn_rows(a, b, res, g, name, *, target=None, tm=512):
    M, K = a.shape
    N = b.shape[1]

    def body(*refs):
        a_ref, b_ref, r_ref, g_ref = refs[:4]
        xv = r_ref[...] + _dot(a_ref[...].astype(BF16), b_ref[...])
        r = lax.rsqrt(jnp.mean(xv * xv, axis=1, keepdims=True) + EPS)
        xh = xv * r
        if target is None:
            x_ref, h_ref = refs[4:]
            x_ref[...] = xv
            h_ref[...] = (xh * g_ref[...]).astype(BF16)
            return
        t_ref, loss_ref, dx_ref, dg_ref = refs[4:]
        err = xh * g_ref[...] - t_ref[...]
        dy = err / N
        gd = dy * g_ref[...]
        dx_ref[...] = r * (gd - xh * jnp.mean(gd * xh, axis=1, keepdims=True))

        @pl.when(pl.program_id(0) == 0)
        def _():
            dg_ref[...] = jnp.zeros_like(dg_ref)
            loss_ref[...] = jnp.zeros_like(loss_ref)

        dg_ref[...] += _colsum(dy * xh)
        loss_ref[...] += 0.5 * _colsum(jnp.mean(err * err, axis=1, keepdims=True))

    row = pl.BlockSpec((tm, N), lambda i: (i, 0))
    vec = pl.BlockSpec((1, N), lambda i: (0, 0))
    in_specs = [pl.BlockSpec((tm, K), lambda i: (i, 0)), pl.BlockSpec((K, N), lambda i: (0, 0)), row, vec]
    if target is None:
        return pl.pallas_call(body, grid=(M // tm,), name=name, in_specs=in_specs, out_specs=[row, row],
                              out_shape=[SDS((M, N), F32), SDS((M, N), BF16)], compiler_params=_cp("parallel"))(a, b, res, g)
    one = pl.BlockSpec((1, 1), lambda i: (0, 0))
    return pl.pallas_call(body, grid=(M // tm,), name=name, in_specs=in_specs + [row], out_specs=[one, row, vec],
                          out_shape=[SDS((1, 1), F32), SDS((M, N), F32), SDS((1, N), F32)],
                          compiler_params=_cp("arbitrary"))(a, b, res, g, target)


def mm_nt(a, a_bw, b, name, *, res=None, out_dtype=F32, tm=1024, tn=512, job=None):
    if a_bw is None:
        M, K = a.shape
    else:
        M, K = a.shape[1], a.shape[0] * a_bw
    N = b.shape[0]
    tm = min(tm, M)

    def body(*refs):
        if res is None:
            a_ref, b_ref, o_ref = refs
        else:
            a_ref, b_ref, r_ref, o_ref = refs
        av = _read_cols(a_ref, a_bw, K).astype(BF16)
        acc = _dot_nt(av, b_ref[...])
        if res is not None:
            acc = acc + r_ref[...]
        o_ref[...] = acc.astype(o_ref.dtype)

    in_specs = [_col_spec(a_bw, K, tm, lambda i, j: i, lambda i, j: 0), pl.BlockSpec((tn, K), lambda i, j: (j, 0))]
    args = [a, b]
    if res is not None:
        in_specs.append(pl.BlockSpec((tm, tn), lambda i, j: (i, j)))
        args.append(res)
    return _pcall(body, args, grid=(M // tm, N // tn), name=name, in_specs=in_specs,
                  out_specs=[pl.BlockSpec((tm, tn), lambda i, j: (i, j))], out_shape=[SDS((M, N), out_dtype)],
                  sem=("parallel", "parallel"), job=job)[0]


def mm_nt_norm_bwd(a, a_bw, b, x, g, res, name, *, tm=512, job=None):
    if a_bw is None:
        M, K = a.shape
    else:
        M, K = a.shape[1], a.shape[0] * a_bw
    N = b.shape[0]

    def body(a_ref, b_ref, x_ref, g_ref, r_ref, dx_ref, dg_ref):
        dh = _dot_nt(_read_cols(a_ref, a_bw, K).astype(BF16), b_ref[...])
        xv = x_ref[...]
        r = lax.rsqrt(jnp.mean(xv * xv, axis=1, keepdims=True) + EPS)
        xh = xv * r
        gd = dh * g_ref[...]
        dx_ref[...] = r_ref[...] + r * (gd - xh * jnp.mean(gd * xh, axis=1, keepdims=True))

        @pl.when(pl.program_id(0) == 0)
        def _():
            dg_ref[...] = jnp.zeros_like(dg_ref)

        dg_ref[...] += jnp.sum(dh * xh, axis=0, keepdims=True)

    row = pl.BlockSpec((tm, N), lambda i: (i, 0))
    vec = pl.BlockSpec((1, N), lambda i: (0, 0))
    return _pcall(body, [a, b, x, g, res], grid=(M // tm,), name=name,
                  in_specs=[_col_spec(a_bw, K, tm, lambda i: i, lambda i: 0), pl.BlockSpec((N, K), lambda i: (0, 0)), row, vec, row],
                  out_specs=[row, vec], out_shape=[SDS((M, N), F32), SDS((1, N), F32)], sem=("arbitrary",), job=job)


def mm_tn(a, b, b_bw, name, *, shards=None, tm=1024, tn=1024, ts=1024, job=None):
    S, Ka = a.shape
    N = b.shape[1] if b_bw is None else b.shape[0] * b_bw
    ts = min(ts, S)
    tm = min(tm, Ka)

    def body(a_ref, b_ref, o_ref):
        @pl.when(pl.program_id(2) == 0)
        def _():
            o_ref[...] = jnp.zeros_like(o_ref)

        bv = _read_cols(b_ref, b_bw, tn).astype(BF16)
        o_ref[...] += _dot_tn(a_ref[...].astype(BF16), bv)

    in_specs = [pl.BlockSpec((ts, tm), lambda i, j, k: (k, i)),
                _col_spec(b_bw, tn, ts, lambda i, j, k: k, lambda i, j, k: j)]
    if shards is None:
        out_spec, oshape = pl.BlockSpec((tm, tn), lambda i, j, k: (i, j)), (Ka, N)
    else:
        per = (N // shards) // tn
        out_spec = pl.BlockSpec((None, tm, tn), lambda i, j, k: (j // per, i, j % per))
        oshape = (shards, Ka, N // shards)
    return _pcall(body, [a, b], grid=(Ka // tm, N // tn, S // ts), name=name, in_specs=in_specs, out_specs=[out_spec],
                  out_shape=[SDS(oshape, F32)], sem=("parallel", "parallel", "arbitrary"), job=job)[0]


def _bucket_tables():
    out = np.zeros((3, 2, BLK, BLK), np.int32)
    qi = np.arange(BLK)[:, None]
    for p, (win, dil) in enumerate(PATTERNS):
        w = win // dil
        assert w == BLK
        for half in range(2):
            kj = np.arange(BLK)[None, :] + half * BLK
            steps = qi + w - kj
            valid = (steps >= 0) & (steps <= w)
            dist = np.clip(steps, 0, w) * dil
            dd = np.maximum(dist, 1).astype(np.float32)
            large = 16 + (np.log(dd / np.float32(16)) / np.float32(math.log(2048 / 16)) * np.float32(16)).astype(np.int32)
            large = np.minimum(large, N_BUCKETS - 1)
            out[p, half] = np.where(valid, np.where(dist < 16, dist, large), -1)
    return out


def bias_tables(rel_bias):
    bkt = jnp.asarray(_bucket_tables())

    def body(rb_ref, bkt_ref, o_ref):
        h = pl.program_id(1)
        for half in range(2):
            bk = bkt_ref[half]
            acc = jnp.full((BLK, BLK), NEG, F32)
            for b in range(N_BUCKETS):
                acc = jnp.where(bk == b, rb_ref[h, b], acc)
            o_ref[half] = acc

    return pl.pallas_call(
        body, grid=(3, N_HEADS), name="bias_tables",
        in_specs=[pl.BlockSpec(memory_space=pltpu.SMEM),
                  pl.BlockSpec((None, 2, BLK, BLK), lambda p, h: (p, 0, 0, 0))],
        out_specs=pl.BlockSpec((None, None, 2, BLK, BLK), lambda p, h: (p, h, 0, 0, 0)),
        out_shape=SDS((3, N_HEADS, 2, BLK, BLK), F32), compiler_params=_cp("parallel", "parallel"))(rel_bias, bkt)


def bias_tables_bwd(ds_sum):
    bkt = jnp.asarray(_bucket_tables())

    def body(ds_ref, bkt_ref, o_ref):
        h = pl.program_id(0)

        @pl.when(h == 0)
        def _():
            o_ref[...] = jnp.zeros_like(o_ref)

        rows = lax.broadcasted_iota(jnp.int32, (N_HEADS, N_BUCKETS), 0)
        cols = lax.broadcasted_iota(jnp.int32, (N_HEADS, N_BUCKETS), 1)
        acc = jnp.zeros((N_HEADS, N_BUCKETS), F32)
        for b in range(N_BUCKETS):
            t = jnp.zeros((BLK, BLK), F32)
            for p in range(3):
                for half in range(2):
                    t = t + jnp.where(bkt_ref[p, half] == b, ds_ref[p, half], 0.0)
            tot = jnp.sum(_rsum(t), axis=0, keepdims=True)
            acc = acc + jnp.where((rows == h) & (cols == b), tot, 0.0)
        o_ref[...] += acc

    return pl.pallas_call(
        body, grid=(N_HEADS,), name="bias_tables_bwd",
        in_specs=[pl.BlockSpec((3, None, 2, BLK, BLK), lambda h: (0, h, 0, 0, 0)),
                  pl.BlockSpec((3, 2, BLK, BLK), lambda h: (0, 0, 0, 0))],
        out_specs=pl.BlockSpec((N_HEADS, N_BUCKETS), lambda h: (0, 0)),
        out_shape=SDS((N_HEADS, N_BUCKETS), F32), compiler_params=_cp("arbitrary"))(ds_sum, bkt)


def _rows(start, dil):
    if dil == 1:
        return pl.ds(pl.multiple_of(start, BLK), BLK)
    return pl.ds(start, BLK, stride=dil)


GRP = 8


def _group_rows(i, dil, S):
    nb = S // (BLK * dil)
    run = min(nb, GRP)
    chunks = nb // run
    res0, b0 = (i // chunks) * (GRP // run), (i % chunks) * run
    cur = [(b0 + j % run) * (BLK * dil) + res0 + j // run for j in range(GRP)]
    t = lax.broadcasted_iota(jnp.int32, (GRP, 1, 1), 0)
    if run == nb:
        before, pen = None, jnp.where(t % run == 0, NEG, 0.0)
    else:
        before = _rows(jnp.maximum(b0 - 1, 0) * (BLK * dil) + res0, dil)
        pen = jnp.where((t == 0) & (b0 == 0), NEG, 0.0)
    return [_rows(s, dil) for s in cur], before, pen


def _load_group(ref, rows):
    return jnp.stack([ref[r, :] for r in rows])


def _with_prev(ref, before, cur_blocks):
    first = cur_blocks[:1] if before is None else ref[before, :][None].astype(cur_blocks.dtype)
    return jnp.concatenate([first, cur_blocks[:-1]], axis=0)


def _bdot_nt(a, b):
    return lax.dot_general(a, b, (((2,), (2,)), ((0,), (0,))), preferred_element_type=F32)


def _bdot(a, b):
    return lax.dot_general(a, b, (((2,), (1,)), ((0,), (0,))), preferred_element_type=F32)


def _bdot_tn(a, b):
    return lax.dot_general(a, b, (((1,), (1,)), ((0,), (0,))), preferred_element_type=F32)


def _lsum(x):
    return jnp.sum(x, axis=-1, keepdims=True)


def _widen(src, dst):
    S = src.shape[1]

    def chunk(i, carry):
        rows = pl.ds(pl.multiple_of(i * 512, 512), 512)
        for a in range(3):
            dst[a, rows, :] = src[a, rows, :].astype(F32)
        return carry

    lax.fori_loop(0, S // 512, chunk, 0)


def attn_fwd(qkv, bias, job=None):
    S = qkv.shape[3]
    nblk = S // BLK
    car = Carried(job, 2, 2, 1, 4)

    def body(*refs):
        (qkv_in, bias_ref, o_ref, lse_ref, qkv_ref), jrefs = car.split(refs)
        car.phase(0, pl.program_id(0), jrefs)
        _widen(qkv_in, qkv_ref)
        lane = lax.broadcasted_iota(jnp.int32, (GRP, BLK, BLK), 2)
        lo = lane < 64
        masks = (lo, jnp.logical_not(lo))
        q_ref, k_ref, v_ref = qkv_ref.at[0], qkv_ref.at[1], qkv_ref.at[2]
        for p, (_, dil) in enumerate(PATTERNS):
            def step(i, carry, p=p, dil=dil):
                rc, before, pen = _group_rows(i, dil, S)
                q2 = _load_group(q_ref, rc) * 0.125
                kc = _load_group(k_ref, rc).astype(BF16)
                kp = _with_prev(k_ref, before, kc)
                vc = _load_group(v_ref, rc)
                vp = _with_prev(v_ref, before, vc)
                if p > 0:
                    acc_old, ml_old = _load_group(o_ref, rc), _load_group(lse_ref, rc)
                pv = jnp.zeros((GRP, BLK, BLK), F32)
                m_new, l_new, alpha = [], [], []
                for h in range(2):
                    qh = jnp.where(masks[h], q2, 0.0).astype(BF16)
                    sp = _bdot_nt(qh, kp) + bias_ref[p, h, 0][None] + pen
                    sc = _bdot_nt(qh, kc) + bias_ref[p, h, 1][None]
                    mn = jnp.max(jnp.maximum(sp, sc), axis=-1, keepdims=True)
                    if p > 0:
                        mo = ml_old[:, :, 64 * h:64 * h + 1]
                        mn = jnp.maximum(mn, mo)
                        al = jnp.exp(mo - mn)
                        alpha.append(al)
                    pp = jnp.exp(sp - mn)
                    pc = jnp.exp(sc - mn)
                    ln = _lsum(pp + pc)
                    if p > 0:
                        ln = ln + al * ml_old[:, :, 64 * h + 32:64 * h + 33]
                    vhp = jnp.where(masks[h], vp, 0.0).astype(BF16)
                    vhc = jnp.where(masks[h], vc, 0.0).astype(BF16)
                    pv = pv + _bdot(pp.astype(BF16), vhp) + _bdot(pc.astype(BF16), vhc)
                    m_new.append(mn)
                    l_new.append(ln)
                if p > 0:
                    pv = pv + acc_old * jnp.where(lo, alpha[0], alpha[1])
                ml_t = jnp.where(lo, jnp.where(lane < 32, m_new[0], l_new[0]), jnp.where(lane < 96, m_new[1], l_new[1]))
                for j, r in enumerate(rc):
                    o_ref[r, :] = pv[j]
                    lse_ref[r, :] = ml_t[j]
                return carry

            lax.fori_loop(0, nblk // GRP, step, 0)

        def fin(i, carry):
            rows = pl.ds(pl.multiple_of(i * 512, 512), 512)
            ml = lse_ref[rows, :]
            is_m = (lax.broadcasted_iota(jnp.int32, ml.shape, 1) & 32) == 0
            den = jnp.where(is_m, pltpu.roll(ml, 96, 1), ml)
            o_ref[rows, :] = o_ref[rows, :] / den
            lse_ref[rows, :] = ml + jnp.log(den)
            return carry

        lax.fori_loop(0, S // 512, fin, 0)
        car.phase(1, pl.program_id(0), jrefs)
        car.phase(2, pl.program_id(0), jrefs)

    outs = pl.pallas_call(
        body, grid=(4,), name="attn_fwd",
        in_specs=[pl.BlockSpec((None, 3, None, S, BLK), lambda g: (0, 0, g, 0, 0)),
                  pl.BlockSpec((3, 2, 2, BLK, BLK), lambda g: (0, g, 0, 0, 0))] + car.in_specs(),
        out_specs=[pl.BlockSpec((None, S, BLK), lambda g: (g, 0, 0)),
                   pl.BlockSpec((None, S, BLK), lambda g: (g, 0, 0))] + car.out_specs(),
        out_shape=[SDS((4, S, BLK), F32), SDS((4, S, BLK), F32)] + car.out_shapes(),
        input_output_aliases=car.aliases(),
        scratch_shapes=[pltpu.VMEM((3, S, BLK), F32)] + car.sems(),
        compiler_params=_cp("arbitrary"))(qkv, bias, *car.args())
    if job is not None:
        job.landed = list(outs[2:])
    return outs[0], outs[1]


def attn_bwd(qkv, bias, o, lse, do, job=None):
    S = qkv.shape[3]
    nblk = S // BLK
    car = Carried(job, 5, 2, 2, 4)

    def body(*refs):
        (qkv_in, bias_ref, o_ref, lse_ref, do_ref, dqkv_out, ds_ref, qkv_ref, dqkv_ref), jrefs = car.split(refs)
        car.phase(0, pl.program_id(0), jrefs)
        _widen(qkv_in, qkv_ref)
        lane = lax.broadcasted_iota(jnp.int32, (GRP, BLK, BLK), 2)
        lo = lane < 64
        masks = (lo, jnp.logical_not(lo))
        dqkv_ref[...] = jnp.zeros_like(dqkv_ref)
        ds_ref[...] = jnp.zeros_like(ds_ref)
        q_ref, k_ref, v_ref = qkv_ref.at[0], qkv_ref.at[1], qkv_ref.at[2]
        dq_ref, dk_ref, dv_ref = dqkv_ref.at[0], dqkv_ref.at[1], dqkv_ref.at[2]
        for p, (_, dil) in enumerate(PATTERNS):
            def step(i, carry, p=p, dil=dil):
                rc, before, pen = _group_rows(i, dil, S)
                q2 = _load_group(q_ref, rc) * 0.125
                kc = _load_group(k_ref, rc).astype(BF16)
                kp = _with_prev(k_ref, before, kc)
                vc = _load_group(v_ref, rc).astype(BF16)
                vp = _with_prev(v_ref, before, vc)
                dot = _load_group(do_ref, rc)
                lset = _load_group(lse_ref, rc)
                prod = dot * _load_group(o_ref, rc)
                zero = jnp.zeros((GRP, BLK, BLK), F32)
                dq, dkc, dkp, dvc, dvp = zero, zero, zero, zero, zero
                for h in range(2):
                    qh = jnp.where(masks[h], q2, 0.0).astype(BF16)
                    doh = jnp.where(masks[h], dot, 0.0).astype(BF16)
                    delta = _lsum(jnp.where(masks[h], prod, 0.0))
                    lh = lset[:, :, 64 * h:64 * h + 1]
                    pp = jnp.exp(_bdot_nt(qh, kp) + bias_ref[p, h, 0][None] + pen - lh)
                    pc = jnp.exp(_bdot_nt(qh, kc) + bias_ref[p, h, 1][None] - lh)
                    dsp = pp * (_bdot_nt(doh, vp) - delta)
                    dsc = pc * (_bdot_nt(doh, vc) - delta)
                    ds_ref[p, h, 0] += jnp.sum(dsp, axis=0)
                    ds_ref[p, h, 1] += jnp.sum(dsc, axis=0)
                    dspb, dscb = dsp.astype(BF16), dsc.astype(BF16)
                    dq = dq + jnp.where(masks[h], _bdot(dspb, kp) + _bdot(dscb, kc), 0.0)
                    dkp = dkp + _bdot_tn(dspb, qh)
                    dkc = dkc + _bdot_tn(dscb, qh)
                    dvp = dvp + _bdot_tn(pp.astype(BF16), doh)
                    dvc = dvc + _bdot_tn(pc.astype(BF16), doh)
                none = jnp.zeros((1, BLK, BLK), F32)
                dkc = dkc + jnp.concatenate([dkp[1:], none], axis=0)
                dvc = dvc + jnp.concatenate([dvp[1:], none], axis=0)
                for j in range(GRP):
                    dq_ref[rc[j], :] += dq[j] * 0.125
                    dk_ref[rc[j], :] += dkc[j]
                    dv_ref[rc[j], :] += dvc[j]
                if before is not None:
                    dk_ref[before, :] += dkp[0]
                    dv_ref[before, :] += dvp[0]
                return carry

            lax.fori_loop(0, nblk // GRP, step, 0)

        def narrow(i, carry):
            rows = pl.ds(pl.multiple_of(i * 512, 512), 512)
            for a in range(3):
                dqkv_out[a, rows, :] = dqkv_ref[a, rows, :].astype(BF16)
            return carry

        lax.fori_loop(0, S // 512, narrow, 0)
        car.phase(1, pl.program_id(0), jrefs)
        car.phase(2, pl.program_id(0), jrefs)

    blk = pl.BlockSpec((None, S, BLK), lambda g: (g, 0, 0))
    outs = pl.pallas_call(
        body, grid=(4,), name="attn_bwd",
        in_specs=[pl.BlockSpec((None, 3, None, S, BLK), lambda g: (0, 0, g, 0, 0)),
                  pl.BlockSpec((3, 2, 2, BLK, BLK), lambda g: (0, g, 0, 0, 0)), blk, blk, blk] + car.in_specs(),
        out_specs=[pl.BlockSpec((None, 3, None, S, BLK), lambda g: (0, 0, g, 0, 0)),
                   pl.BlockSpec((3, 2, 2, BLK, BLK), lambda g: (0, g, 0, 0, 0))] + car.out_specs(),
        out_shape=[SDS((2, 3, 4, S, BLK), BF16), SDS((3, N_HEADS, 2, BLK, BLK), F32)] + car.out_shapes(),
        input_output_aliases=car.aliases(),
        scratch_shapes=[pltpu.VMEM((3, S, BLK), F32), pltpu.VMEM((3, S, BLK), F32)] + car.sems(),
        compiler_params=_cp("arbitrary"))(qkv, bias, o, lse, do, *car.args())
    if job is not None:
        job.landed = list(outs[2:])
    return outs[0], outs[1]


def _shift_down(u, k, halo):
    n = u.shape[0]
    row = lax.broadcasted_iota(jnp.int32, u.shape, 0)
    out = pltpu.roll(u, k, 0)
    hn = halo.shape[0]
    for j in range(k):
        out = jnp.where(row == j, halo[hn - k + j:hn - k + j + 1, :], out)
    return out


def _shift_up(u, k, halo):
    n = u.shape[0]
    row = lax.broadcasted_iota(jnp.int32, u.shape, 0)
    out = pltpu.roll(u, n - k, 0)
    for j in range(k):
        out = jnp.where(row == n - k + j, halo[j:j + 1, :], out)
    return out


def _conv3(u, halo, w0, w1, w2):
    return _shift_down(u, 2, halo) * w0 + _shift_down(u, 1, halo) * w1 + u * w2


def _colsum(x):
    return jnp.sum(x, axis=0, keepdims=True)


def _mixer_specs(S, tm):
    conv = pl.BlockSpec((None, 12, tm, BLK), lambda i: (1, 0, i, 0))
    halo = pl.BlockSpec((None, 12, 16, BLK), lambda i: (1, 0, jnp.maximum(i * (tm // 16) - 1, 0), 0))
    ob = pl.BlockSpec((4, tm, BLK), lambda i: (0, i, 0))
    return conv, halo, ob


def _mixer_recompute(i, o_ref, pr_ref, ph_ref, w_ref):
    ob = [o_ref[q] for q in range(4)]
    gb = [pr_ref[q].astype(F32) for q in range(4)]
    gc = [pr_ref[4 + q].astype(F32) for q in range(4)]
    xi = [pr_ref[8 + q].astype(F32) for q in range(4)]
    keep = jnp.where(i > 0, 1.0, 0.0)
    u = [gc[q] * xi[q] for q in range(4)]
    hu = [ph_ref[4 + q].astype(F32) * ph_ref[8 + q].astype(F32) * keep for q in range(4)]
    w = [[w_ref[k:k + 1, q * BLK:(q + 1) * BLK] for k in range(3)] for q in range(4)]
    cv = [_conv3(u[q], hu[q], *w[q]) for q in range(4)]
    return ob, gb, gc, xi, u, hu, cv, w


def _rms_blocks(blocks):
    ss = sum(_rsum(b * b) for b in blocks)
    return lax.rsqrt(ss / (BLK * len(blocks)) + EPS)


def mixer_post_fwd(o, proj, w_sc, g_a, g_c):
    S = o.shape[1]
    tm = 512

    def body(o_ref, pr_ref, ph_ref, w_ref, ga_ref, gc_ref, m_ref):
        i = pl.program_id(0)
        ob, gb, _, _, _, _, cv, _ = _mixer_recompute(i, o_ref, pr_ref, ph_ref, w_ref)
        conv = [gb[q] * cv[q] for q in range(4)]
        ra, rc = _rms_blocks(ob), _rms_blocks(conv)
        for q in range(4):
            sl = slice(q * BLK, (q + 1) * BLK)
            m_ref[:, q * BLK:(q + 1) * BLK] = (ob[q] * ra * ga_ref[:, sl]).astype(BF16)
            m_ref[:, ATTN_W + q * BLK:ATTN_W + (q + 1) * BLK] = (conv[q] * rc * gc_ref[:, sl]).astype(BF16)

    conv_s, halo_s, ob_s = _mixer_specs(S, tm)
    full = lambda r, c: pl.BlockSpec((r, c), lambda i: (0, 0))
    return pl.pallas_call(
        body, grid=(S // tm,), name="mixer_post_fwd",
        in_specs=[ob_s, conv_s, halo_s, full(3, 512), full(1, 512), full(1, 512)],
        out_specs=pl.BlockSpec((tm, D_MODEL), lambda i: (i, 0)), out_shape=SDS((S, D_MODEL), BF16),
        compiler_params=_cp("parallel"))(o, proj, proj, w_sc, g_a, g_c)


def mixer_post_bwd_a(dmixed, o, proj, w_sc, g_a, g_c, job=None):
    S = o.shape[1]
    tm = 512

    def body(dm_ref, o_ref, pr_ref, ph_ref, w_ref, ga_ref, gc_ref, do_ref, dgb_ref, dcv_ref, dga_ref, dgc_ref):
        i = pl.program_id(0)
        ob, gb, _, _, _, _, cv, _ = _mixer_recompute(i, o_ref, pr_ref, ph_ref, w_ref)
        conv = [gb[q] * cv[q] for q in range(4)]
        ra, rc = _rms_blocks(ob), _rms_blocks(conv)

        @pl.when(i == 0)
        def _():
            dga_ref[...] = jnp.zeros_like(dga_ref)
            dgc_ref[...] = jnp.zeros_like(dgc_ref)

        for blocks, r, g_ref, off, dg_ref, is_attn in ((ob, ra, ga_ref, 0, dga_ref, True), (conv, rc, gc_ref, ATTN_W, dgc_ref, False)):
            xh = [blocks[q] * r for q in range(4)]
            dm = [dm_ref[:, off + q * BLK:off + (q + 1) * BLK].astype(F32) for q in range(4)]
            gd = [dm[q] * g_ref[:, q * BLK:(q + 1) * BLK] for q in range(4)]
            mean = sum(_rsum(gd[q] * xh[q]) for q in range(4)) / (4 * BLK)
            for q in range(4):
                dg_ref[:, q * BLK:(q + 1) * BLK] += _colsum(dm[q] * xh[q])
                dx = r * (gd[q] - xh[q] * mean)
                if is_attn:
                    do_ref[q] = dx
                else:
                    dgb_ref[q] = dx * cv[q]
                    dcv_ref[q] = dx * gb[q]

    conv_s, halo_s, ob_s = _mixer_specs(S, tm)
    full = lambda r, c: pl.BlockSpec((r, c), lambda i: (0, 0))
    return _pcall(
        body, [dmixed, o, proj, proj, w_sc, g_a, g_c], grid=(S // tm,), name="mixer_post_bwd_a",
        in_specs=[pl.BlockSpec((tm, D_MODEL), lambda i: (i, 0)), ob_s, conv_s, halo_s, full(3, 512), full(1, 512), full(1, 512)],
        out_specs=[ob_s, ob_s, ob_s, full(1, 512), full(1, 512)],
        out_shape=[SDS((4, S, BLK), F32)] * 3 + [SDS((1, 512), F32)] * 2, sem=("arbitrary",), job=job)


def mixer_post_bwd_b(dproj, dgb, dcv, proj, w_sc):
    S = proj.shape[2]
    tm = 512
    last = S // 8 - 1

    def body(dp_in, dgb_ref, dcv_ref, dn_ref, pr_ref, w_ref, dp_ref, dw_ref):
        i = pl.program_id(0)
        keep_next = jnp.where(i < pl.num_programs(0) - 1, 1.0, 0.0)

        @pl.when(i == 0)
        def _():
            dw_ref[...] = jnp.zeros_like(dw_ref)

        for q in range(4):
            sl = slice(q * BLK, (q + 1) * BLK)
            gc, xi = pr_ref[4 + q].astype(F32), pr_ref[8 + q].astype(F32)
            u = gc * xi
            dcv = dcv_ref[q]
            dn = dn_ref[q] * keep_next
            d1, d2 = _shift_up(dcv, 1, dn), _shift_up(dcv, 2, dn)
            w0, w1, w2 = (w_ref[k:k + 1, sl] for k in range(3))
            du = dcv * w2 + d1 * w1 + d2 * w0
            dw_ref[0:1, sl] += _colsum(d2 * u)
            dw_ref[1:2, sl] += _colsum(d1 * u)
            dw_ref[2:3, sl] += _colsum(dcv * u)
            dp_ref[q] = dgb_ref[q].astype(BF16)
            dp_ref[4 + q] = (du * xi).astype(BF16)
            dp_ref[8 + q] = (du * gc).astype(BF16)

    conv_s, _, ob_s = _mixer_specs(S, tm)
    nxt = pl.BlockSpec((4, 8, BLK), lambda i: (0, jnp.minimum((i + 1) * (tm // 8), last), 0))
    full = lambda r, c: pl.BlockSpec((r, c), lambda i: (0, 0))
    return pl.pallas_call(
        body, grid=(S // tm,), name="mixer_post_bwd_b",
        in_specs=[pl.BlockSpec(memory_space=pl.ANY), ob_s, ob_s, nxt, conv_s, full(3, 512)],
        out_specs=[conv_s, full(3, 512)],
        out_shape=[SDS(dproj.shape, BF16), SDS((3, 512), F32)],
        input_output_aliases={0: 0}, compiler_params=_cp("arbitrary"))(dproj, dgb, dcv, dcv, proj, w_sc)


def xattn_fwd(q, k, v):
    S = q.shape[0]
    tm = 512
    scale = MEM_HD ** -0.5

    def body(q_ref, k_ref, v_ref, o_ref):
        for h in range(N_MEM_HEADS):
            sl = slice(h * MEM_HD, (h + 1) * MEM_HD)
            s = _dot_nt(q_ref[:, sl], k_ref[:, sl]) * scale
            p = jnp.exp(s - jnp.max(s, axis=1, keepdims=True))
            p = p / _rsum(p)
            o_ref[:, sl] = _dot(p.astype(BF16), v_ref[:, sl]).astype(BF16)

    row = pl.BlockSpec((tm, D_MODEL), lambda i: (i, 0))
    kv = pl.BlockSpec(k.shape, lambda i: (0, 0))
    return pl.pallas_call(body, grid=(S // tm,), name="xattn_fwd", in_specs=[row, kv, kv], out_specs=row,
                          out_shape=SDS((S, D_MODEL), BF16), compiler_params=_cp("parallel"))(q, k, v)


def xattn_bwd(q, k, v, do):
    S = q.shape[0]
    tm = 512
    scale = MEM_HD ** -0.5

    def body(q_ref, k_ref, v_ref, do_ref, dq_ref, dk_ref, dv_ref):
        @pl.when(pl.program_id(0) == 0)
        def _():
            dk_ref[...] = jnp.zeros_like(dk_ref)
            dv_ref[...] = jnp.zeros_like(dv_ref)

        for h in range(N_MEM_HEADS):
            sl = slice(h * MEM_HD, (h + 1) * MEM_HD)
            qh, kh, vh, doh = q_ref[:, sl], k_ref[:, sl], v_ref[:, sl], do_ref[:, sl]
            s = _dot_nt(qh, kh) * scale
            p = jnp.exp(s - jnp.max(s, axis=1, keepdims=True))
            p = p / _rsum(p)
            dp = _dot_nt(doh, vh)
            ds = (p * (dp - _rsum(p * dp)) * scale).astype(BF16)
            dq_ref[:, sl] = _dot(ds, kh).astype(BF16)
            dk_ref[:, sl] += _dot_tn(ds, qh)
            dv_ref[:, sl] += _dot_tn(p.astype(BF16), doh)

    row = pl.BlockSpec((tm, D_MODEL), lambda i: (i, 0))
    kv = pl.BlockSpec(k.shape, lambda i: (0, 0))
    return pl.pallas_call(body, grid=(S // tm,), name="xattn_bwd", in_specs=[row, kv, kv, row], out_specs=[row, kv, kv],
                          out_shape=[SDS((S, D_MODEL), BF16), SDS(k.shape, F32), SDS(k.shape, F32)],
                          compiler_params=_cp("arbitrary"))(q, k, v, do)


FFN_TM, FFN_TC = 256, 1408


def _ffn_specs(S, order):
    tm, tc = FFN_TM, FFN_TC
    ij = (lambda a, b: (a, b)) if order == "ij" else (lambda a, b: (b, a))
    blk = pl.BlockSpec((2, tm, tc), lambda a, b: (0,) + ij(a, b))
    prev = pl.BlockSpec((2, 16, tc), lambda a, b: (0, jnp.maximum(ij(a, b)[0] * (tm // 16) - 1, 0), ij(a, b)[1]))
    nxt = pl.BlockSpec((2, 16, tc), lambda a, b: (0, jnp.minimum((ij(a, b)[0] + 1) * (tm // 16), S // 16 - 1), ij(a, b)[1]))
    wsp = pl.BlockSpec((2, 3, tc), lambda a, b: (0, 0, ij(a, b)[1]))
    bsp = pl.BlockSpec((2, 1, tc), lambda a, b: (0, 0, ij(a, b)[1]))
    act = pl.BlockSpec((tm, tc), lambda a, b: ij(a, b))
    return blk, prev, nxt, wsp, bsp, act


def _ffn_up(i, up_ref, uh_ref, w_ref, b_ref):
    keep = jnp.where(i > 0, 1.0, 0.0)
    out = []
    for half in range(2):
        u = up_ref[half].astype(F32)
        hu = uh_ref[half].astype(F32) * keep
        w0, w1, w2 = (w_ref[half, k:k + 1, :] for k in range(3))
        out.append(_conv3(u, hu, w0, w1, w2) + b_ref[half])
    return out


def ffn_act_fwd(up_pre, w_fc, b_fc):
    S = up_pre.shape[1]

    def body(up_ref, uh_ref, w_ref, b_ref, a_ref, gv_ref):
        g, v = _ffn_up(pl.program_id(0), up_ref, uh_ref, w_ref, b_ref)
        a_ref[...] = (g * jax.nn.sigmoid(g) * v).astype(BF16)
        gv_ref[0] = g.astype(BF16)
        gv_ref[1] = v.astype(BF16)

    blk, prev, _, wsp, bsp, act = _ffn_specs(S, "ij")
    return pl.pallas_call(body, grid=(S // FFN_TM, D_FF // FFN_TC), name="ffn_act_fwd",
                          in_specs=[blk, prev, wsp, bsp], out_specs=[act, blk],
                          out_shape=[SDS((S, D_FF), BF16), SDS((2, S, D_FF), BF16)],
                          compiler_params=_cp("parallel", "parallel"))(up_pre, up_pre, w_fc, b_fc)


def ffn_act_bwd(dact, up, up_pre, w_fc):
    S = up.shape[1]

    def act_grads(da, g, v):
        sg = jax.nn.sigmoid(g)
        return da * v * (sg * (1.0 + g * (1.0 - sg))), da * g * sg

    def body(da_ref, dan_ref, gv_ref, gvn_ref, up_ref, w_ref, dp_ref, dw_ref, db_ref):
        i = pl.program_id(1)
        keep_next = jnp.where(i < pl.num_programs(1) - 1, 1.0, 0.0)

        @pl.when(i == 0)
        def _():
            dw_ref[...] = jnp.zeros_like(dw_ref)
            db_ref[...] = jnp.zeros_like(db_ref)

        here = act_grads(da_ref[...].astype(F32), gv_ref[0].astype(F32), gv_ref[1].astype(F32))
        after = act_grads(dan_ref[...].astype(F32) * keep_next, gvn_ref[0].astype(F32), gvn_ref[1].astype(F32))
        for half in range(2):
            d, dn = here[half], after[half]
            u = up_ref[half].astype(F32)
            d1, d2 = _shift_up(d, 1, dn), _shift_up(d, 2, dn)
            w0, w1, w2 = (w_ref[half, k:k + 1, :] for k in range(3))
            dp_ref[half] = (d * w2 + d1 * w1 + d2 * w0).astype(BF16)
            dw_ref[half, 0:1, :] += _colsum(d2 * u)
            dw_ref[half, 1:2, :] += _colsum(d1 * u)
            dw_ref[half, 2:3, :] += _colsum(d * u)
            db_ref[half] += _colsum(d)

    tm, tc = FFN_TM, FFN_TC
    blk, _, nxt, wsp, bsp, act = _ffn_specs(S, "ji")
    act_next = pl.BlockSpec((16, tc), lambda j, i: (jnp.minimum((i + 1) * (tm // 16), S // 16 - 1), j))
    return pl.pallas_call(body, grid=(D_FF // tc, S // tm), name="ffn_act_bwd",
                          in_specs=[act, act_next, blk, nxt, blk, wsp], out_specs=[blk, wsp, bsp],
                          out_shape=[SDS((2, S, D_FF), BF16), SDS((2, 3, D_FF), F32), SDS((2, 1, D_FF), F32)],
                          compiler_params=_cp("parallel", "arbitrary"))(dact, dact, up, up, up_pre, w_fc)


def local_step(x, mem, target, wb, ws, comm=None):
    S = x.shape[0]
    assert S % 2048 == 0
    if comm is None:
        comm = NoComm()
    else:
        wb = {}
    ws = dict(ws)

    bias = bias_tables(ws["rel_bias"])
    h1 = rmsnorm_fwd(x, ws["g_mix"], "norm_mix", job=comm.job("norm_mix"))
    comm.landed("norm_mix", wb, ws)
    w_fc = ws["w_ffn_conv"].reshape(3, 2, D_FF).transpose(1, 0, 2)
    b_fc = ws["b_ffn_conv"].reshape(2, 1, D_FF)
    proj = mm_nn(h1, wb["w_in"], "proj", out_dtype=BF16, out_bw=BLK, tn=768, job=comm.job("proj"))
    comm.landed("proj", wb)
    qkv = proj.reshape(2, 3, 4, S, BLK)
    o, lse = attn_fwd(qkv, bias, job=comm.job("attn_fwd"))
    comm.landed("attn_fwd", wb)
    proj4 = proj.reshape(2, 12, S, BLK)
    mixed = mixer_post_fwd(o, proj4, ws["w_short_conv"], ws["g_attn_out"], ws["g_conv_out"])
    x1, h2 = mm_nn_rows(mixed, wb["w_out"], x, ws["g_xattn"], "out_proj")
    mem_n = rmsnorm_fwd(mem, ws["g_mem"], "norm_mem")
    xq = mm_nn(h2, wb["w_xq"], "xq", out_dtype=BF16)
    xk = mm_nn(mem_n, wb["w_xk"], "xk", out_dtype=BF16, tn=1024)
    xv = mm_nn(mem_n, wb["w_xv"], "xv", out_dtype=BF16, tn=1024)
    xo = xattn_fwd(xq, xk, xv)
    x2, h3 = mm_nn_rows(xo, wb["w_xo"], x1, ws["g_ffn"], "xo_proj")
    up_pre = mm_nn(h3, wb["w_up"], "up_proj", out_dtype=BF16, out_bw=D_FF, tn=1408)
    act, up = ffn_act_fwd(up_pre, w_fc, b_fc)
    loss, dx3, dg_final = mm_nn_rows(act, wb["w_down"], x2, ws["g_final"].reshape(1, -1), "down_proj", target=target)

    gb, gs = {}, {"g_final": dg_final}
    gb["w_down"] = mm_tn(act, dx3, None, "dw_down", tm=1408, tn=1024)
    dact = mm_nt(dx3, None, wb["w_down"], "d_act", out_dtype=BF16, tn=1408)
    d_up_pre, dw_fc, db_fc = ffn_act_bwd(dact, up, up_pre, w_fc)
    gs["b_ffn_conv"] = db_fc.reshape(1, 2 * D_FF)
    gs["w_ffn_conv"] = dw_fc.transpose(1, 0, 2).reshape(3, 2 * D_FF)
    gb["w_up"] = mm_tn(h3, d_up_pre, D_FF, "dw_up", shards=4, tn=1408)
    dx2, gs["g_ffn"] = mm_nt_norm_bwd(d_up_pre, D_FF, wb["w_up"], x2, ws["g_ffn"], dx3, "d_h3", tm=256,
                                      job=comm.job("d_h3", gb))
    comm.landed("d_h3")
    gb["w_xo"] = mm_tn(xo, dx2, None, "dw_xo")
    dxo = mm_nt(dx2, None, wb["w_xo"], "d_xo", out_dtype=BF16)
    dxq, dxk, dxv = xattn_bwd(xq, xk, xv, dxo)
    gb["w_xq"] = mm_tn(h2, dxq, None, "dw_xq")
    gb["w_xk"] = mm_tn(mem_n, dxk, None, "dw_xk", tn=1024)
    gb["w_xv"] = mm_tn(mem_n, dxv, None, "dw_xv", tn=1024)
    dmem_n = mm_nt(dxk, None, wb["w_xk"], "d_memk", tn=1024)
    dmem_n = mm_nt(dxv, None, wb["w_xv"], "d_memv", res=dmem_n, tn=1024)
    gs["g_mem"] = rmsnorm_bwd(mem, ws["g_mem"], dmem_n, None, "norm_mem_bwd")
    dx1, gs["g_xattn"] = mm_nt_norm_bwd(dxq, None, wb["w_xq"], x1, ws["g_xattn"], dx2, "d_h2")
    gb["w_out"] = mm_tn(mixed, dx1, None, "dw_out")
    dmixed = mm_nt(dx1, None, wb["w_out"], "d_mixed")
    do, dgb, dcv, gs["g_attn_out"], gs["g_conv_out"] = mixer_post_bwd_a(
        dmixed, o, proj4, ws["w_short_conv"], ws["g_attn_out"], ws["g_conv_out"], job=comm.job("mixer_post_bwd_a", gb))
    comm.landed("mixer_post_bwd_a")
    dproj, ds_sum = attn_bwd(qkv, bias, o, lse, do, job=comm.job("attn_bwd"))
    comm.landed("attn_bwd")
    gs["rel_bias"] = bias_tables_bwd(ds_sum)
    dproj, gs["w_short_conv"] = mixer_post_bwd_b(dproj.reshape(2, 12, S, BLK), dgb, dcv, proj4, ws["w_short_conv"])
    dproj = dproj.reshape(24, S, BLK)
    gb["w_in"] = mm_tn(h1, dproj, BLK, "dw_in", shards=4, tn=768, job=comm.job("dw_in"))
    comm.landed("dw_in")
    comm.first_update(gb)
    grad_x, gs["g_mix"] = mm_nt_norm_bwd(dproj, BLK, wb["w_in"], x, ws["g_mix"], dx1, "d_h1", tm=256,
                                         job=comm.job("d_h1", gb))
    comm.landed("d_h1")
    return loss, grad_x, comm.finish(gb), gs


def adamw(w, g, m, v, name, job=None):
    R, C = w.shape
    tr = R
    for cand in (256, 352):
        if R > cand and R % cand == 0:
            tr = cand
            break

    def body(w_ref, g_ref, m_ref, v_ref, d_ref, nm_ref, nv_ref):
        gv = g_ref[...]
        mn = ADAM_B1 * m_ref[...] + (1.0 - ADAM_B1) * gv
        vn = ADAM_B2 * v_ref[...] + (1.0 - ADAM_B2) * (gv * gv)
        m_hat = mn / (1.0 - ADAM_B1 ** ADAM_STEP)
        v_hat = vn / (1.0 - ADAM_B2 ** ADAM_STEP)
        d_ref[...] = -ADAM_LR * (m_hat / (jnp.sqrt(v_hat) + ADAM_EPS) + ADAM_WD * w_ref[...])
        nm_ref[...] = mn
        nv_ref[...] = vn

    blk = pl.BlockSpec((tr, C), lambda i: (i, 0))
    return _pcall(body, [w, g, m, v], grid=(R // tr,), name=name, in_specs=[blk] * 4, out_specs=[blk] * 3,
                  out_shape=[SDS((R, C), F32)] * 3, sem=("parallel",), job=job)


BIG = ("w_in", "w_out", "w_xq", "w_xk", "w_xv", "w_xo", "w_up", "w_down")
COL_SHARDED = ("w_in", "w_up")
N_BIG = len(BIG)
ANY = pl.BlockSpec(memory_space=pl.ANY)


def _place():
    x, y, c = lax.axis_index("x"), lax.axis_index("y"), lax.axis_index("c")
    chips = [(1 - x, y), (x, 1 - y), (1 - x, 1 - y)]
    return x, y, c, chips


def _window(full, name, R, C, shard, half):
    r0, nr = (0, R) if half is None else (half * (R // 2), R // 2)
    if name in COL_SHARDED:
        return full.at[pl.ds(r0, nr), pl.ds(shard * C, C)]
    return full.at[pl.ds(shard * R + r0, nr), :]


def place_shard(w, mine_arr, n):
    R, C = w.shape
    col = n in COL_SHARDED

    def body(s_ref, w_ref, o_ref):
        o_ref[...] = w_ref[...].astype(BF16)

    grid_spec = pltpu.PrefetchScalarGridSpec(
        num_scalar_prefetch=1, grid=(1,), in_specs=[pl.BlockSpec((R, C), lambda i, s: (0, 0))],
        out_specs=pl.BlockSpec((R, C), (lambda i, s: (0, s[0])) if col else (lambda i, s: (s[0], 0))))
    return pl.pallas_call(body, grid_spec=grid_spec, name="place_" + n,
                          out_shape=SDS((R, 4 * C) if col else (4 * R, C), BF16),
                          compiler_params=_cp("arbitrary"))(mine_arr, w)


def _gather_jobs(names, shapes):
    nw = len(names)

    def start(full, sems):
        send, recv, fsend, frecv = sems
        x, y, c, chips = _place()
        mine = 2 * x + y
        for w, n in enumerate(names):
            R, C = shapes[w]
            own = _window(full[w], n, R, C, mine, c)
            for j, chip in enumerate(chips):
                pltpu.make_async_remote_copy(src_ref=own, dst_ref=own, send_sem=send.at[w, j], recv_sem=recv.at[w, j],
                                             device_id=(*chip, c), device_id_type=MESH).start()

    def mid(full, sems):
        send, recv, fsend, frecv = sems
        x, y, c, chips = _place()
        sib = (x, y, 1 - c)
        for w, n in enumerate(names):
            R, C = shapes[w]
            for j, chip in enumerate(chips):
                landed = _window(full[w], n, R, C, 2 * chip[0] + chip[1], c)
                pltpu.make_async_remote_copy(src_ref=landed, dst_ref=landed, send_sem=send.at[w, j], recv_sem=recv.at[w, j],
                                             device_id=(*chip, c), device_id_type=MESH).wait_recv()
                pltpu.make_async_remote_copy(src_ref=landed, dst_ref=landed, send_sem=fsend.at[w, j],
                                             recv_sem=frecv.at[w, j], device_id=sib, device_id_type=MESH).start()

    def finish(full, sems):
        send, recv, fsend, frecv = sems
        x, y, c, chips = _place()
        mine = 2 * x + y
        sib = (x, y, 1 - c)
        for w, n in enumerate(names):
            R, C = shapes[w]
            own = _window(full[w], n, R, C, mine, c)
            for j, chip in enumerate(chips):
                landed = _window(full[w], n, R, C, 2 * chip[0] + chip[1], c)
                other = _window(full[w], n, R, C, 2 * chip[0] + chip[1], 1 - c)
                pltpu.make_async_remote_copy(src_ref=other, dst_ref=other, send_sem=fsend.at[w, j], recv_sem=frecv.at[w, j],
                                             device_id=sib, device_id_type=MESH).wait_recv()
                pltpu.make_async_remote_copy(src_ref=own, dst_ref=own, send_sem=send.at[w, j], recv_sem=recv.at[w, j],
                                             device_id=(*chip, c), device_id_type=MESH).wait_send()
                pltpu.make_async_remote_copy(src_ref=landed, dst_ref=landed, send_sem=fsend.at[w, j],
                                             recv_sem=frecv.at[w, j], device_id=sib, device_id_type=MESH).wait_send()

    return start, mid, finish, [pltpu.SemaphoreType.DMA((nw, 3))] * 4


class CommJob:
    def __init__(self, ins, out_shapes, inplace, start, finish, sems, mid=None):
        self.ins, self.out_shapes, self.inplace = list(ins), list(out_shapes), inplace
        self.start, self.mid, self.finish, self.sems = start, mid, finish, list(sems)

    def run(self, phase, ins, outs, sems):
        if phase == 0:
            self.start(ins, outs, sems)
        elif phase == 1:
            if self.mid is not None:
                self.mid(ins, outs, sems)
        else:
            self.finish(ins, outs, sems)


def run_job(name, job):
    n_in, n_out = len(job.ins), len(job.out_shapes)

    def body(*refs):
        ins, outs, sm = refs[:n_in], refs[n_in:n_in + n_out], refs[n_in + n_out:]
        for phase in range(3):
            job.run(phase, ins, outs, sm)

    job.landed = list(pl.pallas_call(
        body, name=name, in_specs=[ANY] * n_in, out_specs=[ANY] * n_out, out_shape=job.out_shapes,
        input_output_aliases={i: i for i in range(n_in)} if job.inplace else {},
        scratch_shapes=job.sems)(*job.ins))
    return job.landed


class Carried:
    def __init__(self, job, n_in, n_out, n_scratch, steps):
        self.job, self.n_in, self.n_out, self.n_scratch, self.steps = job, n_in, n_out, n_scratch, steps
        self.nji = len(job.ins) if job else 0
        self.njo = len(job.out_shapes) if job else 0

    def in_specs(self):
        return [ANY] * self.nji

    def out_specs(self):
        return [ANY] * self.njo

    def out_shapes(self):
        return list(self.job.out_shapes) if self.job else []

    def aliases(self):
        if not (self.job and self.job.inplace):
            return {}
        return {self.n_in + i: self.n_out + i for i in range(self.nji)}

    def sems(self):
        return list(self.job.sems) if self.job else []

    def args(self):
        return list(self.job.ins) if self.job else []

    def split(self, refs):
        a = self.n_in
        b = a + self.nji
        c = b + self.n_out
        d = c + self.njo
        e = d + self.n_scratch
        return refs[:a] + refs[b:c] + refs[d:e], (refs[a:b], refs[c:d], refs[e:])

    def phase(self, phase, step, jrefs):
        if self.job is None:
            return
        at = {0: 0, 1: max(self.steps - 2, 0), 2: self.steps - 1}[phase]

        @pl.when(step == at)
        def _():
            self.job.run(phase, *jrefs)


def _pcall(body, args, *, grid, in_specs, out_specs, out_shape, name, sem, scratch=(), aliases=None, job=None):
    n_in, n_out = len(args), len(out_shape)
    if job is None:
        return pl.pallas_call(
            body, grid=grid, in_specs=list(in_specs), out_specs=list(out_specs), out_shape=list(out_shape), name=name,
            scratch_shapes=list(scratch), input_output_aliases=dict(aliases or {}), compiler_params=_cp(*sem))(*args)
    total = int(np.prod(grid))
    car = Carried(job, n_in, n_out, len(scratch), total)

    def wrapped(*refs):
        main, jrefs = car.split(refs)
        lin = pl.program_id(0)
        for ax in range(1, len(grid)):
            lin = lin * grid[ax] + pl.program_id(ax)
        car.phase(0, lin, jrefs)
        body(*main)
        car.phase(1, lin, jrefs)
        car.phase(2, lin, jrefs)

    res = pl.pallas_call(
        wrapped, grid=grid, in_specs=list(in_specs) + car.in_specs(), out_specs=list(out_specs) + car.out_specs(),
        out_shape=list(out_shape) + car.out_shapes(), name=name, scratch_shapes=list(scratch) + car.sems(),
        input_output_aliases={**dict(aliases or {}), **car.aliases()},
        compiler_params=_cp(*(["arbitrary"] * len(grid))))(*args, *car.args())
    job.landed = list(res[n_out:])
    return list(res[:n_out])


def gather_job(placed, names):
    shapes = []
    for n in names:
        R, C = placed[n].shape
        shapes.append((R, C // 4) if n in COL_SHARDED else (R // 4, C))
    start, mid, finish, sems = _gather_jobs(names, shapes)
    arrays = [placed[n] for n in names]
    return CommJob(arrays, [SDS(a.shape, a.dtype) for a in arrays], True,
                   lambda i, o, s: start(o, s), lambda i, o, s: finish(o, s), sems, mid=lambda i, o, s: mid(o, s))


def gather_rows_job(placed):
    widths = [a.shape[1] // 4 for a in placed]

    def copies(outs, sems):
        x, y, c, chips = _place()
        mine = 2 * x + y
        cps = []
        for w, ref in enumerate(outs):
            own = ref.at[:, pl.ds(mine * widths[w], widths[w])]
            for j, chip in enumerate(chips):
                theirs = ref.at[:, pl.ds((2 * chip[0] + chip[1]) * widths[w], widths[w])]
                kw = dict(send_sem=sems[0].at[w, j], recv_sem=sems[1].at[w, j], device_id=(*chip, c), device_id_type=MESH)
                cps.append((pltpu.make_async_remote_copy(src_ref=own, dst_ref=own, **kw),
                            pltpu.make_async_remote_copy(src_ref=theirs, dst_ref=theirs, **kw)))
        return cps

    def start(ins, outs, sems):
        for send, _ in copies(outs, sems):
            send.start()

    def finish(ins, outs, sems):
        for send, recv in copies(outs, sems):
            recv.wait_recv()
            send.wait_send()

    return CommJob(placed, [SDS(a.shape, a.dtype) for a in placed], True, start, finish,
                   [pltpu.SemaphoreType.DMA((len(placed), 3))] * 2)


def merge_jobs(a, b):
    assert a.inplace and b.inplace
    na, nb_, nsa = len(a.ins), len(b.ins), len(a.sems)

    def phase(which):
        def run(ins, outs, sems):
            for job, i, o, s in ((a, ins[:na], outs[:na], sems[:nsa]), (b, ins[na:], outs[na:], sems[nsa:])):
                fn = getattr(job, which)
                if fn is not None:
                    fn(i, o, s)
        return run

    merged = CommJob(a.ins + b.ins, a.out_shapes + b.out_shapes, True, phase("start"), phase("finish"), a.sems + b.sems,
                     mid=phase("mid"))
    merged.parts = (a, b, na)
    return merged


def split_landed(merged):
    a, b, na = merged.parts
    a.landed, b.landed = merged.landed[:na], merged.landed[na:]


def pair_exchange_job(grads, names):
    shapes = [grads[n].shape for n in names]

    def copies(ins, outs, sems):
        x, y, c, _ = _place()
        return [pltpu.make_async_remote_copy(
            src_ref=ins[w].at[:, pl.ds((1 - c) * (shapes[w][1] // 2), shapes[w][1] // 2), :], dst_ref=outs[w],
            send_sem=sems[0].at[w], recv_sem=sems[1].at[w], device_id=(x, y, 1 - c), device_id_type=MESH)
            for w in range(len(names))]

    def start(ins, outs, sems):
        for cp in copies(ins, outs, sems):
            cp.start()

    def finish(ins, outs, sems):
        for cp in copies(ins, outs, sems):
            cp.wait()

    return CommJob([grads[n] for n in names], [SDS((4, s[1] // 2, s[2]), F32) for s in shapes], False, start, finish,
                   [pltpu.SemaphoreType.DMA((len(names),))] * 2)


def chip_exchange_job(p16, names):
    shapes = [p16[n].shape for n in names]

    def copies(ins, outs, sems):
        x, y, c, chips = _place()
        return [pltpu.make_async_remote_copy(
            src_ref=ins[w].at[2 * chip[0] + chip[1]], dst_ref=outs[w].at[j],
            send_sem=sems[0].at[w, j], recv_sem=sems[1].at[w, j], device_id=(*chip, c), device_id_type=MESH)
            for w in range(len(names)) for j, chip in enumerate(chips)]

    def start(ins, outs, sems):
        for cp in copies(ins, outs, sems):
            cp.start()

    def finish(ins, outs, sems):
        for cp in copies(ins, outs, sems):
            cp.wait()

    return CommJob([p16[n] for n in names], [SDS((3,) + tuple(s[1:]), BF16) for s in shapes], False, start, finish,
                   [pltpu.SemaphoreType.DMA((len(names), 3))] * 2)


def half_exchange_job(full, names):
    shapes = [full[n].shape for n in names]

    def copies(outs, sems):
        x, y, c, _ = _place()
        cps = []
        for w in range(len(names)):
            Rh = shapes[w][0] // 2
            rows = outs[w].at[pl.ds(c * Rh, Rh), :]
            other = outs[w].at[pl.ds((1 - c) * Rh, Rh), :]
            cps.append((pltpu.make_async_remote_copy(src_ref=rows, dst_ref=rows, send_sem=sems[0].at[w], recv_sem=sems[1].at[w],
                                                     device_id=(x, y, 1 - c), device_id_type=MESH),
                        pltpu.make_async_remote_copy(src_ref=other, dst_ref=other, send_sem=sems[0].at[w], recv_sem=sems[1].at[w],
                                                     device_id=(x, y, 1 - c), device_id_type=MESH)))
        return cps

    def start(ins, outs, sems):
        for send, _ in copies(outs, sems):
            send.start()

    def finish(ins, outs, sems):
        for send, recv in copies(outs, sems):
            recv.wait_recv()
            send.wait_send()

    arrays = [full[n] for n in names]
    return CommJob(arrays, [SDS(a.shape, a.dtype) for a in arrays], True, start, finish,
                   [pltpu.SemaphoreType.DMA((len(names),))] * 2)


def allreduce_small(parts, name, job=None):
    n = len(parts)
    nji = len(job.ins) if job else 0
    njo = len(job.out_shapes) if job else 0

    def body(*refs):
        ins, jins = refs[:n], refs[n:n + nji]
        outs, jouts = refs[n + nji:2 * n + nji], refs[2 * n + nji:2 * n + nji + njo]
        bufs = refs[2 * n + nji + njo:3 * n + nji + njo]
        send, recv = refs[3 * n + nji + njo:3 * n + nji + njo + 2]
        jsems = refs[3 * n + nji + njo + 2:]
        x, y, c, _ = _place()
        me = 4 * x + 2 * y + c
        if job is not None:
            job.run(0, jins, jouts, jsems)
        sends = []
        for p in range(n):
            for k in range(1, 8):
                peer = (x ^ (k >> 2), y ^ ((k >> 1) & 1), c ^ (k & 1))
                cp = pltpu.make_async_remote_copy(src_ref=ins[p], dst_ref=bufs[p].at[k], send_sem=send.at[p, k - 1],
                                                  recv_sem=recv.at[p, k - 1], device_id=peer, device_id_type=MESH)
                cp.start()
                sends.append(cp)
            bufs[p][0] = ins[p][...]
        for cp in sends:
            cp.wait_recv()
        for p in range(n):
            acc = bufs[p][me]
            for d in range(1, 8):
                acc = acc + bufs[p][d ^ me]
            outs[p][...] = acc
        for cp in sends:
            cp.wait_send()
        if job is not None:
            job.run(1, jins, jouts, jsems)
            job.run(2, jins, jouts, jsems)

    vm = pl.BlockSpec(memory_space=pltpu.VMEM)
    res = pl.pallas_call(
        body, name=name, in_specs=[vm] * n + [ANY] * nji, out_specs=[vm] * n + [ANY] * njo,
        out_shape=[SDS(a.shape, F32) for a in parts] + (list(job.out_shapes) if job else []),
        input_output_aliases={n + i: n + i for i in range(nji)} if (job and job.inplace) else {},
        scratch_shapes=[pltpu.VMEM((8,) + tuple(a.shape), F32) for a in parts] + [pltpu.SemaphoreType.DMA((n, 7))] * 2
        + (list(job.sems) if job else []),
    )(*parts, *(job.ins if job else []))
    if job is not None:
        job.landed = list(res[n:])
    return list(res[:n])


def adamw_multi(ws, gs, ms, vs, name):
    n = len(ws)

    def body(*refs):
        for p in range(n):
            w_ref, g_ref, m_ref, v_ref = (refs[q * n + p] for q in range(4))
            d_ref, nm_ref, nv_ref = (refs[(4 + q) * n + p] for q in range(3))
            gv = g_ref[...]
            mn = ADAM_B1 * m_ref[...] + (1.0 - ADAM_B1) * gv
            vn = ADAM_B2 * v_ref[...] + (1.0 - ADAM_B2) * (gv * gv)
            m_hat = mn / (1.0 - ADAM_B1 ** ADAM_STEP)
            v_hat = vn / (1.0 - ADAM_B2 ** ADAM_STEP)
            d_ref[...] = -ADAM_LR * (m_hat / (jnp.sqrt(v_hat) + ADAM_EPS) + ADAM_WD * w_ref[...])
            nm_ref[...] = mn
            nv_ref[...] = vn

    vm = pl.BlockSpec(memory_space=pltpu.VMEM)
    outs = pl.pallas_call(body, name=name, in_specs=[vm] * (4 * n), out_specs=[vm] * (3 * n),
                          out_shape=[SDS(a.shape, F32) for a in ws] * 3)(*ws, *gs, *ms, *vs)
    return outs[:n], outs[n:2 * n], outs[2 * n:]


def pair_sum(g, xh, sel_arr, name):
    _, R, C = g.shape
    Rh = R // 2

    def body(sel_ref, g_ref, x_ref, p32_ref, p16_ref):
        s = g_ref[...] + x_ref[...]
        p16_ref[...] = s.astype(BF16)

        @pl.when(pl.program_id(0) == sel_ref[0])
        def _():
            p32_ref[...] = s

    same = pl.BlockSpec((None, Rh, C), lambda s, sel: (s, 0, 0))
    grid_spec = pltpu.PrefetchScalarGridSpec(
        num_scalar_prefetch=1, grid=(4,),
        in_specs=[pl.BlockSpec((None, Rh, C), lambda s, sel: (s, sel[1], 0)), same],
        out_specs=[pl.BlockSpec((Rh, C), lambda s, sel: (0, 0)), same])
    return pl.pallas_call(body, grid_spec=grid_spec, name=name, out_shape=[SDS((Rh, C), F32), SDS((4, Rh, C), BF16)],
                          compiler_params=_cp("arbitrary"))(sel_arr, g, xh)


def chip_sum(p32, y, sel_arr, name):
    Rh, C = p32.shape

    def body(s_ref, p_ref, y_ref, o_ref):
        o_ref[...] = ((p_ref[...] + y_ref[0].astype(F32)) + y_ref[1].astype(F32)) + y_ref[2].astype(F32)

    grid_spec = pltpu.PrefetchScalarGridSpec(
        num_scalar_prefetch=1, grid=(1,),
        in_specs=[pl.BlockSpec((Rh, C), lambda i, s: (0, 0)), pl.BlockSpec((3, Rh, C), lambda i, s: (0, 0, 0))],
        out_specs=pl.BlockSpec((Rh, C), lambda i, s: (s[1], 0)))
    return pl.pallas_call(body, grid_spec=grid_spec, name=name, out_shape=SDS((2 * Rh, C), F32),
                          compiler_params=_cp("arbitrary"))(sel_arr, p32, y)


class StepComm:
    FIRST = ("w_in",)
    SQUARE = ("w_out", "w_xq", "w_xk", "w_xv", "w_xo")
    FFN = ("w_up", "w_down")
    PLAN = {
        "proj": ("gather", SQUARE[:3]), "attn_fwd": ("gather", SQUARE[3:] + FFN),
        "d_h3": ("pair", FFN), "mixer_post_bwd_a": ("pair", SQUARE), "attn_bwd": ("chip", FFN + SQUARE),
        "dw_in": ("half", FFN + SQUARE), "first_update": ("pair", FIRST), "d_h1": ("chip", FIRST),
    }

    def __init__(self, placed, conv_placed, sel_arr):
        self.placed, self.conv_placed, self.sel_arr = placed, conv_placed, sel_arr
        self.g, self.p32, self.p16, self.full, self.done, self.jobs = {}, {}, {}, {}, {}, {}

    def job(self, tag, gb=None):
        if tag == "norm_mix":
            self.jobs[tag] = merge_jobs(gather_job(self.placed, self.FIRST), gather_rows_job(self.conv_placed))
            return self.jobs[tag]
        kind, names = self.PLAN[tag]
        if kind == "gather":
            j = gather_job(self.placed, names)
        elif kind == "pair":
            for n in names:
                self.g[n] = gb[n] if gb[n].ndim == 3 else gb[n].reshape(4, gb[n].shape[0] // 4, gb[n].shape[1])
            j = pair_exchange_job(self.g, names)
        elif kind == "chip":
            j = chip_exchange_job(self.p16, names)
        else:
            j = half_exchange_job(self.full, names)
        self.jobs[tag] = j
        return j

    def landed(self, tag, wb=None, ws=None):
        if tag == "norm_mix":
            split_landed(self.jobs[tag])
            first, conv, _ = self.jobs[tag].parts
            wb.update(zip(self.FIRST, first.landed))
            ws.update((n, a[:3]) for n, a in zip(SMALL_SHARDED, conv.landed))
            return
        kind, names = self.PLAN[tag]
        for n, a in zip(names, self.jobs[tag].landed):
            if kind == "gather":
                wb[n] = a
            elif kind == "pair":
                self.p32[n], self.p16[n] = pair_sum(self.g[n], a, self.sel_arr, "pair_sum_" + n)
            elif kind == "chip":
                self.full[n] = chip_sum(self.p32[n], a, self.sel_arr, "chip_sum_" + n)
            else:
                self.done[n] = a

    def first_update(self, gb):
        self.first_result = self.update_fn(self.job("first_update", gb))
        self.landed("first_update")

    def finish(self, gb):
        return self.done

    def last_job(self):
        self.jobs["end_half"] = half_exchange_job(self.full, self.FIRST)
        return self.jobs["end_half"]

    def take_last(self):
        self.done.update(zip(self.FIRST, self.jobs["end_half"].landed))


class NoComm:
    def job(self, tag, gb=None):
        return None

    def landed(self, tag, wb=None, ws=None):
        pass

    def first_update(self, gb):
        pass

    def finish(self, gb):
        return gb


SMALL = ("rel_bias", "g_mix", "w_short_conv", "g_attn_out", "g_conv_out", "g_xattn", "g_mem", "g_ffn",
         "w_ffn_conv", "b_ffn_conv", "g_final")
SMALL_FULL = {"rel_bias": (8, 32), "g_mix": (1, 1024), "w_short_conv": (3, 512), "g_attn_out": (1, 512), "g_conv_out": (1, 512),
              "g_xattn": (1, 1024), "g_mem": (1, 1024), "g_ffn": (1, 1024), "w_ffn_conv": (3, 5632), "b_ffn_conv": (1, 5632),
              "g_final": (1024,)}
SMALL_SHARDED = ("w_short_conv", "w_ffn_conv")


def _pack(parts):
    flat = jnp.concatenate([p.reshape(-1) for p in parts])
    rows = -(-flat.shape[0] // 1024) * 8
    return jnp.pad(flat, (0, rows * 128 - flat.shape[0])).reshape(rows, 128)


def _unpack(packed, names):
    flat, out, off = packed.reshape(-1), {}, 0
    for n in names:
        size = int(np.prod(SMALL_FULL[n]))
        out[n] = flat[off:off + size].reshape(SMALL_FULL[n])
        off += size
    return out


def kernel(x, mem, rel_bias, g_mix, w_in, w_short_conv, g_attn_out, g_conv_out, w_out, g_xattn, g_mem, w_xq, w_xk, w_xv, w_xo, g_ffn, w_up, w_ffn_conv, b_ffn_conv, w_down, g_final, loss_target, m_rel_bias, m_g_mix, m_w_in, m_w_short_conv, m_g_attn_out, m_g_conv_out, m_w_out, m_g_xattn, m_g_mem, m_w_xq, m_w_xk, m_w_xv, m_w_xo, m_g_ffn, m_w_up, m_w_ffn_conv, m_b_ffn_conv, m_w_down, m_g_final, v_rel_bias, v_g_mix, v_w_in, v_w_short_conv, v_g_attn_out, v_g_conv_out, v_w_out, v_g_xattn, v_g_mem, v_w_xq, v_w_xk, v_w_xv, v_w_xo, v_g_ffn, v_w_up, v_w_ffn_conv, v_b_ffn_conv, v_w_down, v_g_final):
    names = ("rel_bias", "g_mix", "w_in", "w_short_conv", "g_attn_out", "g_conv_out", "w_out", "g_xattn", "g_mem", "w_xq",
             "w_xk", "w_xv", "w_xo", "g_ffn", "w_up", "w_ffn_conv", "b_ffn_conv", "w_down", "g_final")
    W = dict(zip(names, (rel_bias, g_mix, w_in, w_short_conv, g_attn_out, g_conv_out, w_out, g_xattn, g_mem, w_xq, w_xk, w_xv,
                         w_xo, g_ffn, w_up, w_ffn_conv, b_ffn_conv, w_down, g_final)))
    M = dict(zip(names, (m_rel_bias, m_g_mix, m_w_in, m_w_short_conv, m_g_attn_out, m_g_conv_out, m_w_out, m_g_xattn, m_g_mem,
                         m_w_xq, m_w_xk, m_w_xv, m_w_xo, m_g_ffn, m_w_up, m_w_ffn_conv, m_b_ffn_conv, m_w_down, m_g_final)))
    V = dict(zip(names, (v_rel_bias, v_g_mix, v_w_in, v_w_short_conv, v_g_attn_out, v_g_conv_out, v_w_out, v_g_xattn, v_g_mem,
                         v_w_xq, v_w_xk, v_w_xv, v_w_xo, v_g_ffn, v_w_up, v_w_ffn_conv, v_b_ffn_conv, v_w_down, v_g_final)))
    xi, yi, ci = lax.axis_index("x"), lax.axis_index("y"), lax.axis_index("c")
    mine = 2 * xi + yi
    mine_arr = jnp.reshape(mine, (1,)).astype(jnp.int32)
    sel_arr = jnp.stack([mine, ci]).astype(jnp.int32)
    conv_placed = []
    for n in SMALL_SHARDED:
        shard = W[n][0]
        conv_placed.append(lax.dynamic_update_slice(jnp.zeros((8, SMALL_FULL[n][1]), F32), shard, (0, mine * shard.shape[1])))
    comm = StepComm({n: place_shard(W[n][0], mine_arr, n) for n in BIG}, conv_placed, sel_arr)
    ws = {n: W[n] for n in SMALL if n not in SMALL_SHARDED}

    comm.update_fn = lambda job: adamw(W["w_up"][0], comm.done["w_up"], M["w_up"][0], V["w_up"][0], "adamw_w_up", job=job)
    loss, grad_x, gfull, gs = local_step(x[0], mem[0], loss_target[0], None, ws, comm)

    def as2d(a):
        return a.reshape(1, -1) if a.ndim == 1 else a

    reduced = allreduce_small([as2d(gs[n]) for n in SMALL] + [loss], "reduce_small", job=comm.last_job())
    comm.take_last()
    gsm = dict(zip(SMALL, reduced[:-1]))
    loss = reduced[-1][0, 0]

    grads, delta, new_m, new_v = {}, {}, {}, {}
    for n in BIG:
        d, nm, nv = comm.first_result if n == "w_up" else adamw(W[n][0], gfull[n], M[n][0], V[n][0], "adamw_" + n)
        grads[n], delta[n], new_m[n], new_v[n] = gfull[n][None], d[None], nm[None], nv[None]
    for n in SMALL_SHARDED:
        wid = W[n].shape[2]
        gsm[n] = lax.dynamic_slice(gsm[n], (0, mine * wid), (3, wid))

    def own(a, n):
        return a[0] if n in SMALL_SHARDED else as2d(a)

    d, nm, nv = adamw_multi([own(W[n], n) for n in SMALL], [gsm[n] for n in SMALL], [own(M[n], n) for n in SMALL],
                            [own(V[n], n) for n in SMALL], "adamw_small")
    for i, n in enumerate(SMALL):
        shape = W[n].shape
        grads[n], delta[n], new_m[n], new_v[n] = (a.reshape(shape) for a in (gsm[n], d[i], nm[i], nv[i]))
    return (loss, grad_x[None], *[grads[n] for n in names], *[delta[n] for n in names],
            *[new_m[n] for n in names], *[new_v[n] for n in names])
```

```python
import functools
import math

import numpy as np
import jax
import jax.numpy as jnp
from jax import lax
from jax.experimental import pallas as pl
from jax.experimental.pallas import tpu as pltpu

F32 = jnp.float32
BF16 = jnp.bfloat16
SDS = jax.ShapeDtypeStruct
MESH = pl.DeviceIdType.MESH

D_MODEL = 1024
ATTN_W = 512
N_HEADS = 8
BLK = 128
PATTERNS = ((128, 1), (512, 4), (2048, 16))
N_BUCKETS = 32
D_FF = 2816
N_MEM_HEADS = 4
MEM_HD = 256
EPS = 1e-6
NEG = -1e30
VMEM_LIMIT = 56 * 1024 * 1024

ADAM_LR, ADAM_B1, ADAM_B2, ADAM_EPS, ADAM_WD, ADAM_STEP = 0.001, 0.9, 0.999, 1e-08, 0.01, 10


def _cp(*sem):
    return pltpu.CompilerParams(dimension_semantics=sem, vmem_limit_bytes=VMEM_LIMIT)


def _dot(a, b):
    return jnp.dot(a, b, preferred_element_type=F32)


def _dot_nt(a, b):
    return lax.dot_general(a, b, (((1,), (1,)), ((), ())), preferred_element_type=F32)


def _dot_tn(a, b):
    return lax.dot_general(a, b, (((0,), (0,)), ((), ())), preferred_element_type=F32)


def _rsum(x):
    return jnp.sum(x, axis=1, keepdims=True)


def rmsnorm_fwd(x, g, name, job=None):
    S, Dm = x.shape
    tm = min(S, 512)

    def body(x_ref, g_ref, o_ref):
        xv = x_ref[...]
        r = lax.rsqrt(jnp.mean(xv * xv, axis=1, keepdims=True) + EPS)
        o_ref[...] = (xv * r * g_ref[...]).astype(o_ref.dtype)

    return _pcall(body, [x, g], grid=(S // tm,), name=name,
                  in_specs=[pl.BlockSpec((tm, Dm), lambda i: (i, 0)), pl.BlockSpec((1, Dm), lambda i: (0, 0))],
                  out_specs=[pl.BlockSpec((tm, Dm), lambda i: (i, 0))], out_shape=[SDS((S, Dm), BF16)],
                  sem=("parallel",), job=job)[0]


def rmsnorm_bwd(x, g, dh, res, name, job=None):
    S, Dm = x.shape
    tm = min(S, 512)
    want_dx = res is not None

    def body(*refs):
        if want_dx:
            x_ref, g_ref, dh_ref, res_ref, dx_ref, dg_ref = refs
        else:
            x_ref, g_ref, dh_ref, dg_ref = refs
        i = pl.program_id(0)
        xv = x_ref[...]
        r = lax.rsqrt(jnp.mean(xv * xv, axis=1, keepdims=True) + EPS)
        xh = xv * r
        dh = dh_ref[...].astype(F32)
        if want_dx:
            gd = dh * g_ref[...]
            dx_ref[...] = res_ref[...] + r * (gd - xh * jnp.mean(gd * xh, axis=1, keepdims=True))

        @pl.when(i == 0)
        def _():
            dg_ref[...] = jnp.zeros_like(dg_ref)

        dg_ref[...] += jnp.sum(dh * xh, axis=0, keepdims=True)

    row = pl.BlockSpec((tm, Dm), lambda i: (i, 0))
    vec = pl.BlockSpec((1, Dm), lambda i: (0, 0))
    if want_dx:
        return _pcall(body, [x, g, dh, res], grid=(S // tm,), name=name, in_specs=[row, vec, row, row], out_specs=[row, vec],
                      out_shape=[SDS((S, Dm), F32), SDS((1, Dm), F32)], sem=("arbitrary",), job=job)
    return pl.pallas_call(
        body, grid=(S // tm,), name=name, in_specs=[row, vec, row], out_specs=vec,
        out_shape=SDS((1, Dm), F32), compiler_params=_cp("arbitrary"))(x, g, dh)


def _col_spec(bw, tn, rows, row_of, col_of):
    if bw is None:
        return pl.BlockSpec((rows, tn), lambda *g: (row_of(*g), col_of(*g)))
    if tn % bw == 0:
        return pl.BlockSpec((tn // bw, rows, bw), lambda *g: (col_of(*g), row_of(*g), 0))
    per = bw // tn
    return pl.BlockSpec((None, rows, tn), lambda *g: (col_of(*g) // per, row_of(*g), col_of(*g) % per))


def _read_cols(ref, bw, tn):
    if bw is None or tn % bw != 0:
        return ref[...]
    if tn == bw:
        return ref[0]
    return jnp.concatenate([ref[q] for q in range(tn // bw)], axis=1)


def _write_cols(ref, bw, tn, val):
    if bw is None or tn % bw != 0:
        ref[...] = val.astype(ref.dtype)
    else:
        for q in range(tn // bw):
            ref[q] = val[:, q * bw:(q + 1) * bw].astype(ref.dtype)


def mm_nn(a, b, name, *, res=None, out_dtype=F32, out_bw=None, tm=1024, tn=512, job=None):
    M, K = a.shape
    N = b.shape[1]
    tm = min(tm, M)

    def body(*refs):
        if res is None:
            a_ref, b_ref, o_ref = refs
        else:
            a_ref, b_ref, r_ref, o_ref = refs
        acc = _dot(a_ref[...].astype(BF16), b_ref[...])
        if res is not None:
            acc = acc + r_ref[...]
        _write_cols(o_ref, out_bw, tn, acc)

    ri, ci = (lambda i, j: i), (lambda i, j: j)
    in_specs = [pl.BlockSpec((tm, K), lambda i, j: (i, 0)), pl.BlockSpec((K, tn), lambda i, j: (0, j))]
    args = [a, b]
    if res is not None:
        in_specs.append(pl.BlockSpec((tm, tn), lambda i, j: (i, j)))
        args.append(res)
    oshape = (M, N) if out_bw is None else (N // out_bw, M, out_bw)
    return _pcall(body, args, grid=(M // tm, N // tn), name=name, in_specs=in_specs,
                  out_specs=[_col_spec(out_bw, tn, tm, ri, ci)], out_shape=[SDS(oshape, out_dtype)],
                  sem=("parallel", "parallel"), job=job)[0]


def mm_nn_rows(a, b, res, g, name, *, target=None, tm=512):
    M, K = a.shape
    N = b.shape[1]

    def body(*refs):
        a_ref, b_ref, r_ref, g_ref = refs[:4]
        xv = r_ref[...] + _dot(a_ref[...].astype(BF16), b_ref[...])
        r = lax.rsqrt(jnp.mean(xv * xv, axis=1, keepdims=True) + EPS)
        xh = xv * r
        if target is None:
            x_ref, h_ref = refs[4:]
            x_ref[...] = xv
            h_ref[...] = (xh * g_ref[...]).astype(BF16)
            return
        t_ref, loss_ref, dx_ref, dg_ref = refs[4:]
        err = xh * g_ref[...] - t_ref[...]
        dy = err / N
        gd = dy * g_ref[...]
        dx_ref[...] = r * (gd - xh * jnp.mean(gd * xh, axis=1, keepdims=True))

        @pl.when(pl.program_id(0) == 0)
        def _():
            dg_ref[...] = jnp.zeros_like(dg_ref)
            loss_ref[...] = jnp.zeros_like(loss_ref)

        dg_ref[...] += _colsum(dy * xh)
        loss_ref[...] += 0.5 * _colsum(jnp.mean(err * err, axis=1, keepdims=True))

    row = pl.BlockSpec((tm, N), lambda i: (i, 0))
    vec = pl.BlockSpec((1, N), lambda i: (0, 0))
    in_specs = [pl.BlockSpec((tm, K), lambda i: (i, 0)), pl.BlockSpec((K, N), lambda i: (0, 0)), row, vec]
    if target is None:
        return pl.pallas_call(body, grid=(M // tm,), name=name, in_specs=in_specs, out_specs=[row, row],
                              out_shape=[SDS((M, N), F32), SDS((M, N), BF16)], compiler_params=_cp("parallel"))(a, b, res, g)
    one = pl.BlockSpec((1, 1), lambda i: (0, 0))
    return pl.pallas_call(body, grid=(M // tm,), name=name, in_specs=in_specs + [row], out_specs=[one, row, vec],
                          out_shape=[SDS((1, 1), F32), SDS((M, N), F32), SDS((1, N), F32)],
                          compiler_params=_cp("arbitrary"))(a, b, res, g, target)


def mm_nt(a, a_bw, b, name, *, res=None, out_dtype=F32, tm=1024, tn=512, job=None):
    if a_bw is None:
        M, K = a.shape
    else:
        M, K = a.shape[1], a.shape[0] * a_bw
    N = b.shape[0]
    tm = min(tm, M)

    def body(*refs):
        if res is None:
            a_ref, b_ref, o_ref = refs
        else:
            a_ref, b_ref, r_ref, o_ref = refs
        av = _read_cols(a_ref, a_bw, K).astype(BF16)
        acc = _dot_nt(av, b_ref[...])
        if res is not None:
            acc = acc + r_ref[...]
        o_ref[...] = acc.astype(o_ref.dtype)

    in_specs = [_col_spec(a_bw, K, tm, lambda i, j: i, lambda i, j: 0), pl.BlockSpec((tn, K), lambda i, j: (j, 0))]
    args = [a, b]
    if res is not None:
        in_specs.append(pl.BlockSpec((tm, tn), lambda i, j: (i, j)))
        args.append(res)
    return _pcall(body, args, grid=(M // tm, N // tn), name=name, in_specs=in_specs,
                  out_specs=[pl.BlockSpec((tm, tn), lambda i, j: (i, j))], out_shape=[SDS((M, N), out_dtype)],
                  sem=("parallel", "parallel"), job=job)[0]


def mm_nt_norm_bwd(a, a_bw, b, x, g, res, name, *, tm=512, job=None):
    if a_bw is None:
        M, K = a.shape
    else:
        M, K = a.shape[1], a.shape[0] * a_bw
    N = b.shape[0]

    def body(a_ref, b_ref, x_ref, g_ref, r_ref, dx_ref, dg_ref):
        dh = _dot_nt(_read_cols(a_ref, a_bw, K).astype(BF16), b_ref[...])
        xv = x_ref[...]
        r = lax.rsqrt(jnp.mean(xv * xv, axis=1, keepdims=True) + EPS)
        xh = xv * r
        gd = dh * g_ref[...]
        dx_ref[...] = r_ref[...] + r * (gd - xh * jnp.mean(gd * xh, axis=1, keepdims=True))

        @pl.when(pl.program_id(0) == 0)
        def _():
            dg_ref[...] = jnp.zeros_like(dg_ref)

        dg_ref[...] += jnp.sum(dh * xh, axis=0, keepdims=True)

    row = pl.BlockSpec((tm, N), lambda i: (i, 0))
    vec = pl.BlockSpec((1, N), lambda i: (0, 0))
    return _pcall(body, [a, b, x, g, res], grid=(M // tm,), name=name,
                  in_specs=[_col_spec(a_bw, K, tm, lambda i: i, lambda i: 0), pl.BlockSpec((N, K), lambda i: (0, 0)), row, vec, row],
                  out_specs=[row, vec], out_shape=[SDS((M, N), F32), SDS((1, N), F32)], sem=("arbitrary",), job=job)


def mm_tn(a, b, b_bw, name, *, shards=None, tm=1024, tn=1024, ts=1024, job=None):
    S, Ka = a.shape
    N = b.shape[1] if b_bw is None else b.shape[0] * b_bw
    ts = min(ts, S)
    tm = min(tm, Ka)

    def body(a_ref, b_ref, o_ref):
        @pl.when(pl.program_id(2) == 0)
        def _():
            o_ref[...] = jnp.zeros_like(o_ref)

        bv = _read_cols(b_ref, b_bw, tn).astype(BF16)
        o_ref[...] += _dot_tn(a_ref[...].astype(BF16), bv)

    in_specs = [pl.BlockSpec((ts, tm), lambda i, j, k: (k, i)),
                _col_spec(b_bw, tn, ts, lambda i, j, k: k, lambda i, j, k: j)]
    if shards is None:
        out_spec, oshape = pl.BlockSpec((tm, tn), lambda i, j, k: (i, j)), (Ka, N)
    else:
        per = (N // shards) // tn
        out_spec = pl.BlockSpec((None, tm, tn), lambda i, j, k: (j // per, i, j % per))
        oshape = (shards, Ka, N // shards)
    return _pcall(body, [a, b], grid=(Ka // tm, N // tn, S // ts), name=name, in_specs=in_specs, out_specs=[out_spec],
                  out_shape=[SDS(oshape, F32)], sem=("parallel", "parallel", "arbitrary"), job=job)[0]


KEYS = 2 * BLK


def _bucket_tables():
    out = np.zeros((3, BLK, KEYS), np.int32)
    qi = np.arange(BLK)[:, None]
    kj = np.arange(KEYS)[None, :]
    for p, (win, dil) in enumerate(PATTERNS):
        w = win // dil
        assert w == BLK
        steps = qi + w - kj
        valid = (steps >= 0) & (steps <= w)
        dist = np.clip(steps, 0, w) * dil
        dd = np.maximum(dist, 1).astype(np.float32)
        large = 16 + (np.log(dd / np.float32(16)) / np.float32(math.log(2048 / 16)) * np.float32(16)).astype(np.int32)
        large = np.minimum(large, N_BUCKETS - 1)
        out[p] = np.where(valid, np.where(dist < 16, dist, large), -1)
    return out


def bias_tables(rel_bias):
    bkt = jnp.asarray(_bucket_tables())

    def body(rb_ref, bkt_ref, o_ref):
        h = pl.program_id(1)
        bk = bkt_ref[...]
        acc = jnp.full((BLK, KEYS), NEG, F32)
        for b in range(N_BUCKETS):
            acc = jnp.where(bk == b, rb_ref[h, b], acc)
        o_ref[...] = acc

    return pl.pallas_call(
        body, grid=(3, N_HEADS), name="bias_tables",
        in_specs=[pl.BlockSpec(memory_space=pltpu.SMEM),
                  pl.BlockSpec((None, BLK, KEYS), lambda p, h: (p, 0, 0))],
        out_specs=pl.BlockSpec((None, None, BLK, KEYS), lambda p, h: (p, h, 0, 0)),
        out_shape=SDS((3, N_HEADS, BLK, KEYS), F32), compiler_params=_cp("parallel", "parallel"))(rel_bias, bkt)


def bias_tables_bwd(ds_sum):
    bkt = jnp.asarray(_bucket_tables())

    def body(ds_ref, bkt_ref, o_ref):
        h = pl.program_id(0)

        @pl.when(h == 0)
        def _():
            o_ref[...] = jnp.zeros_like(o_ref)

        rows = lax.broadcasted_iota(jnp.int32, (N_HEADS, N_BUCKETS), 0)
        cols = lax.broadcasted_iota(jnp.int32, (N_HEADS, N_BUCKETS), 1)
        acc = jnp.zeros((N_HEADS, N_BUCKETS), F32)
        for b in range(N_BUCKETS):
            t = jnp.zeros((BLK, KEYS), F32)
            for p in range(3):
                t = t + jnp.where(bkt_ref[p] == b, ds_ref[p], 0.0)
            tot = jnp.sum(_rsum(t), axis=0, keepdims=True)
            acc = acc + jnp.where((rows == h) & (cols == b), tot, 0.0)
        o_ref[...] += acc

    return pl.pallas_call(
        body, grid=(N_HEADS,), name="bias_tables_bwd",
        in_specs=[pl.BlockSpec((3, None, BLK, KEYS), lambda h: (0, h, 0, 0)),
                  pl.BlockSpec((3, BLK, KEYS), lambda h: (0, 0, 0))],
        out_specs=pl.BlockSpec((N_HEADS, N_BUCKETS), lambda h: (0, 0)),
        out_shape=SDS((N_HEADS, N_BUCKETS), F32), compiler_params=_cp("arbitrary"))(ds_sum, bkt)


def _rows(start, dil):
    if dil == 1:
        return pl.ds(pl.multiple_of(start, BLK), BLK)
    return pl.ds(start, BLK, stride=dil)


GRP = 8


def _group_rows(i, dil, S):
    nb = S // (BLK * dil)
    run = min(nb, GRP)
    chunks = nb // run
    res0, b0 = (i // chunks) * (GRP // run), (i % chunks) * run
    cur = [(b0 + j % run) * (BLK * dil) + res0 + j // run for j in range(GRP)]
    t = lax.broadcasted_iota(jnp.int32, (GRP, 1, 1), 0)
    if run == nb:
        before, pen = None, jnp.where(t % run == 0, NEG, 0.0)
    else:
        before = _rows(jnp.maximum(b0 - 1, 0) * (BLK * dil) + res0, dil)
        pen = jnp.where((t == 0) & (b0 == 0), NEG, 0.0)
    return [_rows(s, dil) for s in cur], before, pen


def _load_group(ref, rows):
    return jnp.stack([ref[r, :] for r in rows])


def _with_prev(ref, before, cur_blocks):
    first = cur_blocks[:1] if before is None else ref[before, :][None].astype(cur_blocks.dtype)
    return jnp.concatenate([first, cur_blocks[:-1]], axis=0)


def _bdot_nt(a, b):
    return lax.dot_general(a, b, (((2,), (2,)), ((0,), (0,))), preferred_element_type=F32)


def _bdot(a, b):
    return lax.dot_general(a, b, (((2,), (1,)), ((0,), (0,))), preferred_element_type=F32)


def _bdot_tn(a, b):
    return lax.dot_general(a, b, (((1,), (1,)), ((0,), (0,))), preferred_element_type=F32)


def _lsum(x):
    return jnp.sum(x, axis=-1, keepdims=True)


def _widen(src, dst):
    S = src.shape[1]

    def chunk(i, carry):
        rows = pl.ds(pl.multiple_of(i * 512, 512), 512)
        for a in range(3):
            dst[a, rows, :] = src[a, rows, :].astype(F32)
        return carry

    lax.fori_loop(0, S // 512, chunk, 0)


def attn_fwd(qkv, bias, job=None):
    S = qkv.shape[3]
    nblk = S // BLK
    car = Carried(job, 2, 2, 1, 4)

    def body(*refs):
        (qkv_in, bias_ref, o_ref, lse_ref, qkv_ref), jrefs = car.split(refs)
        car.phase(0, pl.program_id(0), jrefs)
        _widen(qkv_in, qkv_ref)
        lane = lax.broadcasted_iota(jnp.int32, (GRP, BLK, BLK), 2)
        lo = lane < 64
        masks = (lo, jnp.logical_not(lo))
        lov = lax.broadcasted_iota(jnp.int32, (GRP, KEYS, BLK), 2) < 64
        vmasks = (lov, jnp.logical_not(lov))
        prev_keys = lax.broadcasted_iota(jnp.int32, (1, 1, KEYS), 2) < BLK
        q_ref, k_ref, v_ref = qkv_ref.at[0], qkv_ref.at[1], qkv_ref.at[2]
        for p, (_, dil) in enumerate(PATTERNS):
            def step(i, carry, p=p, dil=dil):
                rc, before, pen = _group_rows(i, dil, S)
                q2 = _load_group(q_ref, rc) * 0.125
                kc = _load_group(k_ref, rc).astype(BF16)
                keys = jnp.concatenate([_with_prev(k_ref, before, kc), kc], axis=1)
                vc = _load_group(v_ref, rc)
                vals = jnp.concatenate([_with_prev(v_ref, before, vc), vc], axis=1)
                pen = jnp.where(prev_keys, pen, 0.0)
                if p > 0:
                    acc_old, ml_old = _load_group(o_ref, rc), _load_group(lse_ref, rc)
                pv = jnp.zeros((GRP, BLK, BLK), F32)
                m_new, l_new, alpha = [], [], []
                for h in range(2):
                    qh = jnp.where(masks[h], q2, 0.0).astype(BF16)
                    s = _bdot_nt(qh, keys) + bias_ref[p, h][None] + pen
                    mn = jnp.max(s, axis=-1, keepdims=True)
                    if p > 0:
                        mo = ml_old[:, :, 64 * h:64 * h + 1]
                        mn = jnp.maximum(mn, mo)
                        al = jnp.exp(mo - mn)
                        alpha.append(al)
                    pr = jnp.exp(s - mn)
                    ln = _lsum(pr)
                    if p > 0:
                        ln = ln + al * ml_old[:, :, 64 * h + 32:64 * h + 33]
                    pv = pv + _bdot(pr.astype(BF16), jnp.where(vmasks[h], vals, 0.0).astype(BF16))
                    m_new.append(mn)
                    l_new.append(ln)
                if p > 0:
                    pv = pv + acc_old * jnp.where(lo, alpha[0], alpha[1])
                ml_t = jnp.where(lo, jnp.where(lane < 32, m_new[0], l_new[0]), jnp.where(lane < 96, m_new[1], l_new[1]))
                for j, r in enumerate(rc):
                    o_ref[r, :] = pv[j]
                    lse_ref[r, :] = ml_t[j]
                return carry

            lax.fori_loop(0, nblk // GRP, step, 0)

        def fin(i, carry):
            rows = pl.ds(pl.multiple_of(i * 512, 512), 512)
            ml = lse_ref[rows, :]
            is_m = (lax.broadcasted_iota(jnp.int32, ml.shape, 1) & 32) == 0
            den = jnp.where(is_m, pltpu.roll(ml, 96, 1), ml)
            o_ref[rows, :] = o_ref[rows, :] / den
            lse_ref[rows, :] = ml + jnp.log(den)
            return carry

        lax.fori_loop(0, S // 512, fin, 0)
        car.phase(1, pl.program_id(0), jrefs)
        car.phase(2, pl.program_id(0), jrefs)

    outs = pl.pallas_call(
        body, grid=(4,), name="attn_fwd",
        in_specs=[pl.BlockSpec((None, 3, None, S, BLK), lambda g: (0, 0, g, 0, 0)),
                  pl.BlockSpec((3, 2, BLK, KEYS), lambda g: (0, g, 0, 0))] + car.in_specs(),
        out_specs=[pl.BlockSpec((None, S, BLK), lambda g: (g, 0, 0)),
                   pl.BlockSpec((None, S, BLK), lambda g: (g, 0, 0))] + car.out_specs(),
        out_shape=[SDS((4, S, BLK), F32), SDS((4, S, BLK), F32)] + car.out_shapes(),
        input_output_aliases=car.aliases(),
        scratch_shapes=[pltpu.VMEM((3, S, BLK), F32)] + car.sems(),
        compiler_params=_cp("arbitrary"))(qkv, bias, *car.args())
    if job is not None:
        job.landed = list(outs[2:])
    return outs[0], outs[1]


def attn_bwd(qkv, bias, o, lse, do, job=None):
    S = qkv.shape[3]
    nblk = S // BLK
    car = Carried(job, 5, 2, 2, 4)

    def body(*refs):
        (qkv_in, bias_ref, o_ref, lse_ref, do_ref, dqkv_out, ds_ref, qkv_ref, dqkv_ref), jrefs = car.split(refs)
        car.phase(0, pl.program_id(0), jrefs)
        _widen(qkv_in, qkv_ref)
        lane = lax.broadcasted_iota(jnp.int32, (GRP, BLK, BLK), 2)
        lo = lane < 64
        masks = (lo, jnp.logical_not(lo))
        lov = lax.broadcasted_iota(jnp.int32, (GRP, KEYS, BLK), 2) < 64
        vmasks = (lov, jnp.logical_not(lov))
        prev_keys = lax.broadcasted_iota(jnp.int32, (1, 1, KEYS), 2) < BLK
        dqkv_ref[...] = jnp.zeros_like(dqkv_ref)
        ds_ref[...] = jnp.zeros_like(ds_ref)
        q_ref, k_ref, v_ref = qkv_ref.at[0], qkv_ref.at[1], qkv_ref.at[2]
        dq_ref, dk_ref, dv_ref = dqkv_ref.at[0], dqkv_ref.at[1], dqkv_ref.at[2]
        for p, (_, dil) in enumerate(PATTERNS):
            def step(i, carry, p=p, dil=dil):
                rc, before, pen = _group_rows(i, dil, S)
                q2 = _load_group(q_ref, rc) * 0.125
                kc = _load_group(k_ref, rc).astype(BF16)
                keys = jnp.concatenate([_with_prev(k_ref, before, kc), kc], axis=1)
                vc = _load_group(v_ref, rc).astype(BF16)
                vals = jnp.concatenate([_with_prev(v_ref, before, vc), vc], axis=1)
                pen = jnp.where(prev_keys, pen, 0.0)
                dot = _load_group(do_ref, rc)
                lset = _load_group(lse_ref, rc)
                prod = dot * _load_group(o_ref, rc)
                dq = jnp.zeros((GRP, BLK, BLK), F32)
                dkeys = jnp.zeros((GRP, KEYS, BLK), F32)
                dvals = jnp.zeros((GRP, KEYS, BLK), F32)
                for h in range(2):
                    qh = jnp.where(masks[h], q2, 0.0).astype(BF16)
                    doh = jnp.where(masks[h], dot, 0.0).astype(BF16)
                    delta = _lsum(jnp.where(masks[h], prod, 0.0))
                    lh = lset[:, :, 64 * h:64 * h + 1]
                    pr = jnp.exp(_bdot_nt(qh, keys) + bias_ref[p, h][None] + pen - lh)
                    ds = pr * (_bdot_nt(doh, vals) - delta)
                    ds_ref[p, h] += jnp.sum(ds, axis=0)
                    dsb = ds.astype(BF16)
                    dq = dq + jnp.where(masks[h], _bdot(dsb, keys), 0.0)
                    dkeys = dkeys + _bdot_tn(dsb, qh)
                    dvals = dvals + _bdot_tn(pr.astype(BF16), doh)
                dkp, dkc, dvp, dvc = dkeys[:, :BLK], dkeys[:, BLK:], dvals[:, :BLK], dvals[:, BLK:]
                none = jnp.zeros((1, BLK, BLK), F32)
                dkc = dkc + jnp.concatenate([dkp[1:], none], axis=0)
                dvc = dvc + jnp.concatenate([dvp[1:], none], axis=0)
                for j in range(GRP):
                    dq_ref[rc[j], :] += dq[j] * 0.125
                    dk_ref[rc[j], :] += dkc[j]
                    dv_ref[rc[j], :] += dvc[j]
                if before is not None:
                    dk_ref[before, :] += dkp[0]
                    dv_ref[before, :] += dvp[0]
                return carry

            lax.fori_loop(0, nblk // GRP, step, 0)

        def narrow(i, carry):
            rows = pl.ds(pl.multiple_of(i * 512, 512), 512)
            for a in range(3):
                dqkv_out[a, rows, :] = dqkv_ref[a, rows, :].astype(BF16)
            return carry

        lax.fori_loop(0, S // 512, narrow, 0)
        car.phase(1, pl.program_id(0), jrefs)
        car.phase(2, pl.program_id(0), jrefs)

    blk = pl.BlockSpec((None, S, BLK), lambda g: (g, 0, 0))
    outs = pl.pallas_call(
        body, grid=(4,), name="attn_bwd",
        in_specs=[pl.BlockSpec((None, 3, None, S, BLK), lambda g: (0, 0, g, 0, 0)),
                  pl.BlockSpec((3, 2, BLK, KEYS), lambda g: (0, g, 0, 0)), blk, blk, blk] + car.in_specs(),
        out_specs=[pl.BlockSpec((None, 3, None, S, BLK), lambda g: (0, 0, g, 0, 0)),
                   pl.BlockSpec((3, 2, BLK, KEYS), lambda g: (0, g, 0, 0))] + car.out_specs(),
        out_shape=[SDS((2, 3, 4, S, BLK), BF16), SDS((3, N_HEADS, BLK, KEYS), F32)] + car.out_shapes(),
        input_output_aliases=car.aliases(),
        scratch_shapes=[pltpu.VMEM((3, S, BLK), F32), pltpu.VMEM((3, S, BLK), F32)] + car.sems(),
        compiler_params=_cp("arbitrary"))(qkv, bias, o, lse, do, *car.args())
    if job is not None:
        job.landed = list(outs[2:])
    return outs[0], outs[1]


def _shift_down(u, k, halo):
    n = u.shape[0]
    row = lax.broadcasted_iota(jnp.int32, u.shape, 0)
    out = pltpu.roll(u, k, 0)
    hn = halo.shape[0]
    for j in range(k):
        out = jnp.where(row == j, halo[hn - k + j:hn - k + j + 1, :], out)
    return out


def _shift_up(u, k, halo):
    n = u.shape[0]
    row = lax.broadcasted_iota(jnp.int32, u.shape, 0)
    out = pltpu.roll(u, n - k, 0)
    for j in range(k):
        out = jnp.where(row == n - k + j, halo[j:j + 1, :], out)
    return out


def _conv3(u, halo, w0, w1, w2):
    return _shift_down(u, 2, halo) * w0 + _shift_down(u, 1, halo) * w1 + u * w2


def _colsum(x):
    return jnp.sum(x, axis=0, keepdims=True)


def _mixer_specs(S, tm):
    conv = pl.BlockSpec((None, 12, tm, BLK), lambda i: (1, 0, i, 0))
    halo = pl.BlockSpec((None, 12, 16, BLK), lambda i: (1, 0, jnp.maximum(i * (tm // 16) - 1, 0), 0))
    ob = pl.BlockSpec((4, tm, BLK), lambda i: (0, i, 0))
    return conv, halo, ob


def _mixer_recompute(i, o_ref, pr_ref, ph_ref, w_ref):
    ob = [o_ref[q] for q in range(4)]
    gb = [pr_ref[q].astype(F32) for q in range(4)]
    gc = [pr_ref[4 + q].astype(F32) for q in range(4)]
    xi = [pr_ref[8 + q].astype(F32) for q in range(4)]
    keep = jnp.where(i > 0, 1.0, 0.0)
    u = [gc[q] * xi[q] for q in range(4)]
    hu = [ph_ref[4 + q].astype(F32) * ph_ref[8 + q].astype(F32) * keep for q in range(4)]
    w = [[w_ref[k:k + 1, q * BLK:(q + 1) * BLK] for k in range(3)] for q in range(4)]
    cv = [_conv3(u[q], hu[q], *w[q]) for q in range(4)]
    return ob, gb, gc, xi, u, hu, cv, w


def _rms_blocks(blocks):
    ss = sum(_rsum(b * b) for b in blocks)
    return lax.rsqrt(ss / (BLK * len(blocks)) + EPS)


def mixer_post_fwd(o, proj, w_sc, g_a, g_c):
    S = o.shape[1]
    tm = 512

    def body(o_ref, pr_ref, ph_ref, w_ref, ga_ref, gc_ref, m_ref):
        i = pl.program_id(0)
        ob, gb, _, _, _, _, cv, _ = _mixer_recompute(i, o_ref, pr_ref, ph_ref, w_ref)
        conv = [gb[q] * cv[q] for q in range(4)]
        ra, rc = _rms_blocks(ob), _rms_blocks(conv)
        for q in range(4):
            sl = slice(q * BLK, (q + 1) * BLK)
            m_ref[:, q * BLK:(q + 1) * BLK] = (ob[q] * ra * ga_ref[:, sl]).astype(BF16)
            m_ref[:, ATTN_W + q * BLK:ATTN_W + (q + 1) * BLK] = (conv[q] * rc * gc_ref[:, sl]).astype(BF16)

    conv_s, halo_s, ob_s = _mixer_specs(S, tm)
    full = lambda r, c: pl.BlockSpec((r, c), lambda i: (0, 0))
    return pl.pallas_call(
        body, grid=(S // tm,), name="mixer_post_fwd",
        in_specs=[ob_s, conv_s, halo_s, full(3, 512), full(1, 512), full(1, 512)],
        out_specs=pl.BlockSpec((tm, D_MODEL), lambda i: (i, 0)), out_shape=SDS((S, D_MODEL), BF16),
        compiler_params=_cp("parallel"))(o, proj, proj, w_sc, g_a, g_c)


def mixer_post_bwd_a(dmixed, o, proj, w_sc, g_a, g_c, job=None):
    S = o.shape[1]
    tm = 512

    def body(dm_ref, o_ref, pr_ref, ph_ref, w_ref, ga_ref, gc_ref, do_ref, dgb_ref, dcv_ref, dga_ref, dgc_ref):
        i = pl.program_id(0)
        ob, gb, _, _, _, _, cv, _ = _mixer_recompute(i, o_ref, pr_ref, ph_ref, w_ref)
        conv = [gb[q] * cv[q] for q in range(4)]
        ra, rc = _rms_blocks(ob), _rms_blocks(conv)

        @pl.when(i == 0)
        def _():
            dga_ref[...] = jnp.zeros_like(dga_ref)
            dgc_ref[...] = jnp.zeros_like(dgc_ref)

        for blocks, r, g_ref, off, dg_ref, is_attn in ((ob, ra, ga_ref, 0, dga_ref, True), (conv, rc, gc_ref, ATTN_W, dgc_ref, False)):
            xh = [blocks[q] * r for q in range(4)]
            dm = [dm_ref[:, off + q * BLK:off + (q + 1) * BLK].astype(F32) for q in range(4)]
            gd = [dm[q] * g_ref[:, q * BLK:(q + 1) * BLK] for q in range(4)]
            mean = sum(_rsum(gd[q] * xh[q]) for q in range(4)) / (4 * BLK)
            for q in range(4):
                dg_ref[:, q * BLK:(q + 1) * BLK] += _colsum(dm[q] * xh[q])
                dx = r * (gd[q] - xh[q] * mean)
                if is_attn:
                    do_ref[q] = dx
                else:
                    dgb_ref[q] = dx * cv[q]
                    dcv_ref[q] = dx * gb[q]

    conv_s, halo_s, ob_s = _mixer_specs(S, tm)
    full = lambda r, c: pl.BlockSpec((r, c), lambda i: (0, 0))
    return _pcall(
        body, [dmixed, o, proj, proj, w_sc, g_a, g_c], grid=(S // tm,), name="mixer_post_bwd_a",
        in_specs=[pl.BlockSpec((tm, D_MODEL), lambda i: (i, 0)), ob_s, conv_s, halo_s, full(3, 512), full(1, 512), full(1, 512)],
        out_specs=[ob_s, ob_s, ob_s, full(1, 512), full(1, 512)],
        out_shape=[SDS((4, S, BLK), F32)] * 3 + [SDS((1, 512), F32)] * 2, sem=("arbitrary",), job=job)


def mixer_post_bwd_b(dproj, dgb, dcv, proj, w_sc):
    S = proj.shape[2]
    tm = 512
    last = S // 8 - 1

    def body(dp_in, dgb_ref, dcv_ref, dn_ref, pr_ref, w_ref, dp_ref, dw_ref):
        i = pl.program_id(0)
        keep_next = jnp.where(i < pl.num_programs(0) - 1, 1.0, 0.0)

        @pl.when(i == 0)
        def _():
            dw_ref[...] = jnp.zeros_like(dw_ref)

        for q in range(4):
            sl = slice(q * BLK, (q + 1) * BLK)
            gc, xi = pr_ref[4 + q].astype(F32), pr_ref[8 + q].astype(F32)
            u = gc * xi
            dcv = dcv_ref[q]
            dn = dn_ref[q] * keep_next
            d1, d2 = _shift_up(dcv, 1, dn), _shift_up(dcv, 2, dn)
            w0, w1, w2 = (w_ref[k:k + 1, sl] for k in range(3))
            du = dcv * w2 + d1 * w1 + d2 * w0
            dw_ref[0:1, sl] += _colsum(d2 * u)
            dw_ref[1:2, sl] += _colsum(d1 * u)
            dw_ref[2:3, sl] += _colsum(dcv * u)
            dp_ref[q] = dgb_ref[q].astype(BF16)
            dp_ref[4 + q] = (du * xi).astype(BF16)
            dp_ref[8 + q] = (du * gc).astype(BF16)

    conv_s, _, ob_s = _mixer_specs(S, tm)
    nxt = pl.BlockSpec((4, 8, BLK), lambda i: (0, jnp.minimum((i + 1) * (tm // 8), last), 0))
    full = lambda r, c: pl.BlockSpec((r, c), lambda i: (0, 0))
    return pl.pallas_call(
        body, grid=(S // tm,), name="mixer_post_bwd_b",
        in_specs=[pl.BlockSpec(memory_space=pl.ANY), ob_s, ob_s, nxt, conv_s, full(3, 512)],
        out_specs=[conv_s, full(3, 512)],
        out_shape=[SDS(dproj.shape, BF16), SDS((3, 512), F32)],
        input_output_aliases={0: 0}, compiler_params=_cp("arbitrary"))(dproj, dgb, dcv, dcv, proj, w_sc)


def xattn_fwd(q, k, v):
    S = q.shape[0]
    tm = 512
    scale = MEM_HD ** -0.5

    def body(q_ref, k_ref, v_ref, o_ref):
        for h in range(N_MEM_HEADS):
            sl = slice(h * MEM_HD, (h + 1) * MEM_HD)
            s = _dot_nt(q_ref[:, sl], k_ref[:, sl]) * scale
            p = jnp.exp(s - jnp.max(s, axis=1, keepdims=True))
            p = p / _rsum(p)
            o_ref[:, sl] = _dot(p.astype(BF16), v_ref[:, sl]).astype(BF16)

    row = pl.BlockSpec((tm, D_MODEL), lambda i: (i, 0))
    kv = pl.BlockSpec(k.shape, lambda i: (0, 0))
    return pl.pallas_call(body, grid=(S // tm,), name="xattn_fwd", in_specs=[row, kv, kv], out_specs=row,
                          out_shape=SDS((S, D_MODEL), BF16), compiler_params=_cp("parallel"))(q, k, v)


def xattn_bwd(q, k, v, do):
    S = q.shape[0]
    tm = 512
    scale = MEM_HD ** -0.5

    def body(q_ref, k_ref, v_ref, do_ref, dq_ref, dk_ref, dv_ref):
        @pl.when(pl.program_id(0) == 0)
        def _():
            dk_ref[...] = jnp.zeros_like(dk_ref)
            dv_ref[...] = jnp.zeros_like(dv_ref)

        for h in range(N_MEM_HEADS):
            sl = slice(h * MEM_HD, (h + 1) * MEM_HD)
            qh, kh, vh, doh = q_ref[:, sl], k_ref[:, sl], v_ref[:, sl], do_ref[:, sl]
            s = _dot_nt(qh, kh) * scale
            p = jnp.exp(s - jnp.max(s, axis=1, keepdims=True))
            p = p / _rsum(p)
            dp = _dot_nt(doh, vh)
            ds = (p * (dp - _rsum(p * dp)) * scale).astype(BF16)
            dq_ref[:, sl] = _dot(ds, kh).astype(BF16)
            dk_ref[:, sl] += _dot_tn(ds, qh)
            dv_ref[:, sl] += _dot_tn(p.astype(BF16), doh)

    row = pl.BlockSpec((tm, D_MODEL), lambda i: (i, 0))
    kv = pl.BlockSpec(k.shape, lambda i: (0, 0))
    return pl.pallas_call(body, grid=(S // tm,), name="xattn_bwd", in_specs=[row, kv, kv, row], out_specs=[row, kv, kv],
                          out_shape=[SDS((S, D_MODEL), BF16), SDS(k.shape, F32), SDS(k.shape, F32)],
                          compiler_params=_cp("arbitrary"))(q, k, v, do)


FFN_TM, FFN_TC = 256, 1408


def _ffn_specs(S, order):
    tm, tc = FFN_TM, FFN_TC
    ij = (lambda a, b: (a, b)) if order == "ij" else (lambda a, b: (b, a))
    blk = pl.BlockSpec((2, tm, tc), lambda a, b: (0,) + ij(a, b))
    prev = pl.BlockSpec((2, 16, tc), lambda a, b: (0, jnp.maximum(ij(a, b)[0] * (tm // 16) - 1, 0), ij(a, b)[1]))
    nxt = pl.BlockSpec((2, 16, tc), lambda a, b: (0, jnp.minimum((ij(a, b)[0] + 1) * (tm // 16), S // 16 - 1), ij(a, b)[1]))
    wsp = pl.BlockSpec((2, 3, tc), lambda a, b: (0, 0, ij(a, b)[1]))
    bsp = pl.BlockSpec((2, 1, tc), lambda a, b: (0, 0, ij(a, b)[1]))
    act = pl.BlockSpec((tm, tc), lambda a, b: ij(a, b))
    return blk, prev, nxt, wsp, bsp, act


def _ffn_up(i, up_ref, uh_ref, w_ref, b_ref):
    keep = jnp.where(i > 0, 1.0, 0.0)
    out = []
    for half in range(2):
        u = up_ref[half].astype(F32)
        hu = uh_ref[half].astype(F32) * keep
        w0, w1, w2 = (w_ref[half, k:k + 1, :] for k in range(3))
        out.append(_conv3(u, hu, w0, w1, w2) + b_ref[half])
    return out


def ffn_act_fwd(up_pre, w_fc, b_fc):
    S = up_pre.shape[1]

    def body(up_ref, uh_ref, w_ref, b_ref, a_ref, gv_ref):
        g, v = _ffn_up(pl.program_id(0), up_ref, uh_ref, w_ref, b_ref)
        a_ref[...] = (g * jax.nn.sigmoid(g) * v).astype(BF16)
        gv_ref[0] = g.astype(BF16)
        gv_ref[1] = v.astype(BF16)

    blk, prev, _, wsp, bsp, act = _ffn_specs(S, "ij")
    return pl.pallas_call(body, grid=(S // FFN_TM, D_FF // FFN_TC), name="ffn_act_fwd",
                          in_specs=[blk, prev, wsp, bsp], out_specs=[act, blk],
                          out_shape=[SDS((S, D_FF), BF16), SDS((2, S, D_FF), BF16)],
                          compiler_params=_cp("parallel", "parallel"))(up_pre, up_pre, w_fc, b_fc)


def ffn_act_bwd(dact, up, up_pre, w_fc):
    S = up.shape[1]

    def act_grads(da, g, v):
        sg = jax.nn.sigmoid(g)
        return da * v * (sg * (1.0 + g * (1.0 - sg))), da * g * sg

    def body(da_ref, dan_ref, gv_ref, gvn_ref, up_ref, w_ref, dp_ref, dw_ref, db_ref):
        i = pl.program_id(1)
        keep_next = jnp.where(i < pl.num_programs(1) - 1, 1.0, 0.0)

        @pl.when(i == 0)
        def _():
            dw_ref[...] = jnp.zeros_like(dw_ref)
            db_ref[...] = jnp.zeros_like(db_ref)

        here = act_grads(da_ref[...].astype(F32), gv_ref[0].astype(F32), gv_ref[1].astype(F32))
        after = act_grads(dan_ref[...].astype(F32) * keep_next, gvn_ref[0].astype(F32), gvn_ref[1].astype(F32))
        for half in range(2):
            d, dn = here[half], after[half]
            u = up_ref[half].astype(F32)
            d1, d2 = _shift_up(d, 1, dn), _shift_up(d, 2, dn)
            w0, w1, w2 = (w_ref[half, k:k + 1, :] for k in range(3))
            dp_ref[half] = (d * w2 + d1 * w1 + d2 * w0).astype(BF16)
            dw_ref[half, 0:1, :] += _colsum(d2 * u)
            dw_ref[half, 1:2, :] += _colsum(d1 * u)
            dw_ref[half, 2:3, :] += _colsum(d * u)
            db_ref[half] += _colsum(d)

    tm, tc = FFN_TM, FFN_TC
    blk, _, nxt, wsp, bsp, act = _ffn_specs(S, "ji")
    act_next = pl.BlockSpec((16, tc), lambda j, i: (jnp.minimum((i + 1) * (tm // 16), S // 16 - 1), j))
    return pl.pallas_call(body, grid=(D_FF // tc, S // tm), name="ffn_act_bwd",
                          in_specs=[act, act_next, blk, nxt, blk, wsp], out_specs=[blk, wsp, bsp],
                          out_shape=[SDS((2, S, D_FF), BF16), SDS((2, 3, D_FF), F32), SDS((2, 1, D_FF), F32)],
                          compiler_params=_cp("parallel", "arbitrary"))(dact, dact, up, up, up_pre, w_fc)


def local_step(x, mem, target, wb, ws, comm=None):
    S = x.shape[0]
    assert S % 2048 == 0
    if comm is None:
        comm = NoComm()
    else:
        wb = {}
    ws = dict(ws)

    bias = bias_tables(ws["rel_bias"])
    h1 = rmsnorm_fwd(x, ws["g_mix"], "norm_mix", job=comm.job("norm_mix"))
    comm.landed("norm_mix", wb, ws)
    w_fc = ws["w_ffn_conv"].reshape(3, 2, D_FF).transpose(1, 0, 2)
    b_fc = ws["b_ffn_conv"].reshape(2, 1, D_FF)
    proj = mm_nn(h1, wb["w_in"], "proj", out_dtype=BF16, out_bw=BLK, tn=768, job=comm.job("proj"))
    comm.landed("proj", wb)
    qkv = proj.reshape(2, 3, 4, S, BLK)
    o, lse = attn_fwd(qkv, bias, job=comm.job("attn_fwd"))
    comm.landed("attn_fwd", wb)
    proj4 = proj.reshape(2, 12, S, BLK)
    mixed = mixer_post_fwd(o, proj4, ws["w_short_conv"], ws["g_attn_out"], ws["g_conv_out"])
    x1, h2 = mm_nn_rows(mixed, wb["w_out"], x, ws["g_xattn"], "out_proj")
    mem_n = rmsnorm_fwd(mem, ws["g_mem"], "norm_mem")
    xq = mm_nn(h2, wb["w_xq"], "xq", out_dtype=BF16)
    xk = mm_nn(mem_n, wb["w_xk"], "xk", out_dtype=BF16, tn=1024)
    xv = mm_nn(mem_n, wb["w_xv"], "xv", out_dtype=BF16, tn=1024)
    xo = xattn_fwd(xq, xk, xv)
    x2, h3 = mm_nn_rows(xo, wb["w_xo"], x1, ws["g_ffn"], "xo_proj")
    up_pre = mm_nn(h3, wb["w_up"], "up_proj", out_dtype=BF16, out_bw=D_FF, tn=1408)
    act, up = ffn_act_fwd(up_pre, w_fc, b_fc)
    loss, dx3, dg_final = mm_nn_rows(act, wb["w_down"], x2, ws["g_final"].reshape(1, -1), "down_proj", target=target)

    gb, gs = {}, {"g_final": dg_final}
    gb["w_down"] = mm_tn(act, dx3, None, "dw_down", tm=1408, tn=1024)
    dact = mm_nt(dx3, None, wb["w_down"], "d_act", out_dtype=BF16, tn=1408)
    d_up_pre, dw_fc, db_fc = ffn_act_bwd(dact, up, up_pre, w_fc)
    gs["b_ffn_conv"] = db_fc.reshape(1, 2 * D_FF)
    gs["w_ffn_conv"] = dw_fc.transpose(1, 0, 2).reshape(3, 2 * D_FF)
    gb["w_up"] = mm_tn(h3, d_up_pre, D_FF, "dw_up", shards=4, tn=1408)
    dx2, gs["g_ffn"] = mm_nt_norm_bwd(d_up_pre, D_FF, wb["w_up"], x2, ws["g_ffn"], dx3, "d_h3", tm=256,
                                      job=comm.job("d_h3", gb))
    comm.landed("d_h3")
    gb["w_xo"] = mm_tn(xo, dx2, None, "dw_xo")
    dxo = mm_nt(dx2, None, wb["w_xo"], "d_xo", out_dtype=BF16)
    dxq, dxk, dxv = xattn_bwd(xq, xk, xv, dxo)
    gb["w_xq"] = mm_tn(h2, dxq, None, "dw_xq")
    gb["w_xk"] = mm_tn(mem_n, dxk, None, "dw_xk", tn=1024)
    gb["w_xv"] = mm_tn(mem_n, dxv, None, "dw_xv", tn=1024)
    dmem_n = mm_nt(dxk, None, wb["w_xk"], "d_memk", tn=1024)
    dmem_n = mm_nt(dxv, None, wb["w_xv"], "d_memv", res=dmem_n, tn=1024)
    gs["g_mem"] = rmsnorm_bwd(mem, ws["g_mem"], dmem_n, None, "norm_mem_bwd")
    dx1, gs["g_xattn"] = mm_nt_norm_bwd(dxq, None, wb["w_xq"], x1, ws["g_xattn"], dx2, "d_h2")
    gb["w_out"] = mm_tn(mixed, dx1, None, "dw_out")
    dmixed = mm_nt(dx1, None, wb["w_out"], "d_mixed")
    do, dgb, dcv, gs["g_attn_out"], gs["g_conv_out"] = mixer_post_bwd_a(
        dmixed, o, proj4, ws["w_short_conv"], ws["g_attn_out"], ws["g_conv_out"], job=comm.job("mixer_post_bwd_a", gb))
    comm.landed("mixer_post_bwd_a")
    dproj, ds_sum = attn_bwd(qkv, bias, o, lse, do, job=comm.job("attn_bwd"))
    comm.landed("attn_bwd")
    gs["rel_bias"] = bias_tables_bwd(ds_sum)
    dproj, gs["w_short_conv"] = mixer_post_bwd_b(dproj.reshape(2, 12, S, BLK), dgb, dcv, proj4, ws["w_short_conv"])
    dproj = dproj.reshape(24, S, BLK)
    gb["w_in"] = mm_tn(h1, dproj, BLK, "dw_in", shards=4, tn=768, job=comm.job("dw_in"))
    comm.landed("dw_in")
    comm.first_update(gb)
    grad_x, gs["g_mix"] = mm_nt_norm_bwd(dproj, BLK, wb["w_in"], x, ws["g_mix"], dx1, "d_h1", tm=256,
                                         job=comm.job("d_h1", gb))
    comm.landed("d_h1")
    return loss, grad_x, comm.finish(gb), gs


def adamw(w, g, m, v, name, job=None):
    R, C = w.shape
    tr = R
    for cand in (256, 352):
        if R > cand and R % cand == 0:
            tr = cand
            break

    def body(w_ref, g_ref, m_ref, v_ref, d_ref, nm_ref, nv_ref):
        gv = g_ref[...]
        mn = ADAM_B1 * m_ref[...] + (1.0 - ADAM_B1) * gv
        vn = ADAM_B2 * v_ref[...] + (1.0 - ADAM_B2) * (gv * gv)
        m_hat = mn / (1.0 - ADAM_B1 ** ADAM_STEP)
        v_hat = vn / (1.0 - ADAM_B2 ** ADAM_STEP)
        d_ref[...] = -ADAM_LR * (m_hat / (jnp.sqrt(v_hat) + ADAM_EPS) + ADAM_WD * w_ref[...])
        nm_ref[...] = mn
        nv_ref[...] = vn

    blk = pl.BlockSpec((tr, C), lambda i: (i, 0))
    return _pcall(body, [w, g, m, v], grid=(R // tr,), name=name, in_specs=[blk] * 4, out_specs=[blk] * 3,
                  out_shape=[SDS((R, C), F32)] * 3, sem=("parallel",), job=job)


BIG = ("w_in", "w_out", "w_xq", "w_xk", "w_xv", "w_xo", "w_up", "w_down")
COL_SHARDED = ("w_in", "w_up")
N_BIG = len(BIG)
ANY = pl.BlockSpec(memory_space=pl.ANY)


def _place():
    x, y, c = lax.axis_index("x"), lax.axis_index("y"), lax.axis_index("c")
    chips = [(1 - x, y), (x, 1 - y), (1 - x, 1 - y)]
    return x, y, c, chips


def _window(full, name, R, C, shard, half):
    r0, nr = (0, R) if half is None else (half * (R // 2), R // 2)
    if name in COL_SHARDED:
        return full.at[pl.ds(r0, nr), pl.ds(shard * C, C)]
    return full.at[pl.ds(shard * R + r0, nr), :]


def place_shard(w, mine_arr, n):
    R, C = w.shape
    col = n in COL_SHARDED

    def body(s_ref, w_ref, o_ref):
        o_ref[...] = w_ref[...].astype(BF16)

    grid_spec = pltpu.PrefetchScalarGridSpec(
        num_scalar_prefetch=1, grid=(1,), in_specs=[pl.BlockSpec((R, C), lambda i, s: (0, 0))],
        out_specs=pl.BlockSpec((R, C), (lambda i, s: (0, s[0])) if col else (lambda i, s: (s[0], 0))))
    return pl.pallas_call(body, grid_spec=grid_spec, name="place_" + n,
                          out_shape=SDS((R, 4 * C) if col else (4 * R, C), BF16),
                          compiler_params=_cp("arbitrary"))(mine_arr, w)


def _gather_jobs(names, shapes):
    nw = len(names)

    def start(full, sems):
        send, recv, fsend, frecv = sems
        x, y, c, chips = _place()
        mine = 2 * x + y
        for w, n in enumerate(names):
            R, C = shapes[w]
            own = _window(full[w], n, R, C, mine, c)
            for j, chip in enumerate(chips):
                pltpu.make_async_remote_copy(src_ref=own, dst_ref=own, send_sem=send.at[w, j], recv_sem=recv.at[w, j],
                                             device_id=(*chip, c), device_id_type=MESH).start()

    def mid(full, sems, lo=0, hi=nw):
        send, recv, fsend, frecv = sems
        x, y, c, chips = _place()
        sib = (x, y, 1 - c)
        for w, n in list(enumerate(names))[lo:hi]:
            R, C = shapes[w]
            for j, chip in enumerate(chips):
                landed = _window(full[w], n, R, C, 2 * chip[0] + chip[1], c)
                pltpu.make_async_remote_copy(src_ref=landed, dst_ref=landed, send_sem=send.at[w, j], recv_sem=recv.at[w, j],
                                             device_id=(*chip, c), device_id_type=MESH).wait_recv()
                pltpu.make_async_remote_copy(src_ref=landed, dst_ref=landed, send_sem=fsend.at[w, j],
                                             recv_sem=frecv.at[w, j], device_id=sib, device_id_type=MESH).start()

    def finish(full, sems):
        send, recv, fsend, frecv = sems
        x, y, c, chips = _place()
        mine = 2 * x + y
        sib = (x, y, 1 - c)
        for w, n in enumerate(names):
            R, C = shapes[w]
            own = _window(full[w], n, R, C, mine, c)
            for j, chip in enumerate(chips):
                landed = _window(full[w], n, R, C, 2 * chip[0] + chip[1], c)
                other = _window(full[w], n, R, C, 2 * chip[0] + chip[1], 1 - c)
                pltpu.make_async_remote_copy(src_ref=other, dst_ref=other, send_sem=fsend.at[w, j], recv_sem=frecv.at[w, j],
                                             device_id=sib, device_id_type=MESH).wait_recv()
                pltpu.make_async_remote_copy(src_ref=own, dst_ref=own, send_sem=send.at[w, j], recv_sem=recv.at[w, j],
                                             device_id=(*chip, c), device_id_type=MESH).wait_send()
                pltpu.make_async_remote_copy(src_ref=landed, dst_ref=landed, send_sem=fsend.at[w, j],
                                             recv_sem=frecv.at[w, j], device_id=sib, device_id_type=MESH).wait_send()

    return start, mid, finish, [pltpu.SemaphoreType.DMA((nw, 3))] * 4


class CommJob:
    def __init__(self, ins, out_shapes, inplace, start, finish, sems, mid=None):
        self.ins, self.out_shapes, self.inplace = list(ins), list(out_shapes), inplace
        self.start, self.mid, self.finish, self.sems = start, mid, finish, list(sems)

    def run(self, phase, ins, outs, sems):
        if phase == 0:
            self.start(ins, outs, sems)
        elif phase == 1:
            if self.mid is not None:
                self.mid(ins, outs, sems)
        else:
            self.finish(ins, outs, sems)


def run_job(name, job):
    n_in, n_out = len(job.ins), len(job.out_shapes)

    def body(*refs):
        ins, outs, sm = refs[:n_in], refs[n_in:n_in + n_out], refs[n_in + n_out:]
        for phase in range(3):
            job.run(phase, ins, outs, sm)

    job.landed = list(pl.pallas_call(
        body, name=name, in_specs=[ANY] * n_in, out_specs=[ANY] * n_out, out_shape=job.out_shapes,
        input_output_aliases={i: i for i in range(n_in)} if job.inplace else {},
        scratch_shapes=job.sems)(*job.ins))
    return job.landed


class Carried:
    def __init__(self, job, n_in, n_out, n_scratch, steps):
        self.job, self.n_in, self.n_out, self.n_scratch, self.steps = job, n_in, n_out, n_scratch, steps
        self.nji = len(job.ins) if job else 0
        self.njo = len(job.out_shapes) if job else 0

    def in_specs(self):
        return [ANY] * self.nji

    def out_specs(self):
        return [ANY] * self.njo

    def out_shapes(self):
        return list(self.job.out_shapes) if self.job else []

    def aliases(self):
        if not (self.job and self.job.inplace):
            return {}
        return {self.n_in + i: self.n_out + i for i in range(self.nji)}

    def sems(self):
        return list(self.job.sems) if self.job else []

    def args(self):
        return list(self.job.ins) if self.job else []

    def split(self, refs):
        a = self.n_in
        b = a + self.nji
        c = b + self.n_out
        d = c + self.njo
        e = d + self.n_scratch
        return refs[:a] + refs[b:c] + refs[d:e], (refs[a:b], refs[c:d], refs[e:])

    def phase(self, phase, step, jrefs):
        if self.job is None:
            return
        at = {0: 0, 1: max(self.steps - 2, 0), 2: self.steps - 1}[phase]

        @pl.when(step == at)
        def _():
            self.job.run(phase, *jrefs)


def _pcall(body, args, *, grid, in_specs, out_specs, out_shape, name, sem, scratch=(), aliases=None, job=None):
    n_in, n_out = len(args), len(out_shape)
    if job is None:
        return pl.pallas_call(
            body, grid=grid, in_specs=list(in_specs), out_specs=list(out_specs), out_shape=list(out_shape), name=name,
            scratch_shapes=list(scratch), input_output_aliases=dict(aliases or {}), compiler_params=_cp(*sem))(*args)
    total = int(np.prod(grid))
    car = Carried(job, n_in, n_out, len(scratch), total)

    def wrapped(*refs):
        main, jrefs = car.split(refs)
        lin = pl.program_id(0)
        for ax in range(1, len(grid)):
            lin = lin * grid[ax] + pl.program_id(ax)
        car.phase(0, lin, jrefs)
        body(*main)
        car.phase(1, lin, jrefs)
        car.phase(2, lin, jrefs)

    res = pl.pallas_call(
        wrapped, grid=grid, in_specs=list(in_specs) + car.in_specs(), out_specs=list(out_specs) + car.out_specs(),
        out_shape=list(out_shape) + car.out_shapes(), name=name, scratch_shapes=list(scratch) + car.sems(),
        input_output_aliases={**dict(aliases or {}), **car.aliases()},
        compiler_params=_cp(*(["arbitrary"] * len(grid))))(*args, *car.args())
    job.landed = list(res[n_out:])
    return list(res[:n_out])


def gather_job(placed, names):
    shapes = []
    for n in names:
        R, C = placed[n].shape
        shapes.append((R, C // 4) if n in COL_SHARDED else (R // 4, C))
    start, mid, finish, sems = _gather_jobs(names, shapes)
    arrays = [placed[n] for n in names]
    early = max(len(names) - 1, 1)

    def last(i, o, s):
        mid(o, s, early, len(names))
        finish(o, s)

    return CommJob(arrays, [SDS(a.shape, a.dtype) for a in arrays], True,
                   lambda i, o, s: start(o, s), last, sems, mid=lambda i, o, s: mid(o, s, 0, early))


def gather_rows_job(placed):
    widths = [a.shape[1] // 4 for a in placed]

    def copies(outs, sems):
        x, y, c, chips = _place()
        mine = 2 * x + y
        cps = []
        for w, ref in enumerate(outs):
            own = ref.at[:, pl.ds(mine * widths[w], widths[w])]
            for j, chip in enumerate(chips):
                theirs = ref.at[:, pl.ds((2 * chip[0] + chip[1]) * widths[w], widths[w])]
                kw = dict(send_sem=sems[0].at[w, j], recv_sem=sems[1].at[w, j], device_id=(*chip, c), device_id_type=MESH)
                cps.append((pltpu.make_async_remote_copy(src_ref=own, dst_ref=own, **kw),
                            pltpu.make_async_remote_copy(src_ref=theirs, dst_ref=theirs, **kw)))
        return cps

    def start(ins, outs, sems):
        for send, _ in copies(outs, sems):
            send.start()

    def finish(ins, outs, sems):
        for send, recv in copies(outs, sems):
            recv.wait_recv()
            send.wait_send()

    return CommJob(placed, [SDS(a.shape, a.dtype) for a in placed], True, start, finish,
                   [pltpu.SemaphoreType.DMA((len(placed), 3))] * 2)


def merge_jobs(a, b):
    assert a.inplace and b.inplace
    na, nb_, nsa = len(a.ins), len(b.ins), len(a.sems)

    def phase(which):
        def run(ins, outs, sems):
            for job, i, o, s in ((a, ins[:na], outs[:na], sems[:nsa]), (b, ins[na:], outs[na:], sems[nsa:])):
                fn = getattr(job, which)
                if fn is not None:
                    fn(i, o, s)
        return run

    merged = CommJob(a.ins + b.ins, a.out_shapes + b.out_shapes, True, phase("start"), phase("finish"), a.sems + b.sems,
                     mid=phase("mid"))
    merged.parts = (a, b, na)
    return merged


def split_landed(merged):
    a, b, na = merged.parts
    a.landed, b.landed = merged.landed[:na], merged.landed[na:]


def pair_exchange_job(grads, names):
    shapes = [grads[n].shape for n in names]

    def copies(ins, outs, sems):
        x, y, c, _ = _place()
        return [pltpu.make_async_remote_copy(
            src_ref=ins[w].at[:, pl.ds((1 - c) * (shapes[w][1] // 2), shapes[w][1] // 2), :], dst_ref=outs[w],
            send_sem=sems[0].at[w], recv_sem=sems[1].at[w], device_id=(x, y, 1 - c), device_id_type=MESH)
            for w in range(len(names))]

    def start(ins, outs, sems):
        for cp in copies(ins, outs, sems):
            cp.start()

    def finish(ins, outs, sems):
        for cp in copies(ins, outs, sems):
            cp.wait()

    return CommJob([grads[n] for n in names], [SDS((4, s[1] // 2, s[2]), F32) for s in shapes], False, start, finish,
                   [pltpu.SemaphoreType.DMA((len(names),))] * 2)


def chip_exchange_job(p16, names):
    shapes = [p16[n].shape for n in names]

    def copies(ins, outs, sems):
        x, y, c, chips = _place()
        return [pltpu.make_async_remote_copy(
            src_ref=ins[w].at[2 * chip[0] + chip[1]], dst_ref=outs[w].at[j],
            send_sem=sems[0].at[w, j], recv_sem=sems[1].at[w, j], device_id=(*chip, c), device_id_type=MESH)
            for w in range(len(names)) for j, chip in enumerate(chips)]

    def start(ins, outs, sems):
        for cp in copies(ins, outs, sems):
            cp.start()

    def finish(ins, outs, sems):
        for cp in copies(ins, outs, sems):
            cp.wait()

    return CommJob([p16[n] for n in names], [SDS((3,) + tuple(s[1:]), BF16) for s in shapes], False, start, finish,
                   [pltpu.SemaphoreType.DMA((len(names), 3))] * 2)


def half_exchange_job(full, names):
    shapes = [full[n].shape for n in names]

    def copies(outs, sems):
        x, y, c, _ = _place()
        cps = []
        for w in range(len(names)):
            Rh = shapes[w][0] // 2
            rows = outs[w].at[pl.ds(c * Rh, Rh), :]
            other = outs[w].at[pl.ds((1 - c) * Rh, Rh), :]
            cps.append((pltpu.make_async_remote_copy(src_ref=rows, dst_ref=rows, send_sem=sems[0].at[w], recv_sem=sems[1].at[w],
                                                     device_id=(x, y, 1 - c), device_id_type=MESH),
                        pltpu.make_async_remote_copy(src_ref=other, dst_ref=other, send_sem=sems[0].at[w], recv_sem=sems[1].at[w],
                                                     device_id=(x, y, 1 - c), device_id_type=MESH)))
        return cps

    def start(ins, outs, sems):
        for send, _ in copies(outs, sems):
            send.start()

    def finish(ins, outs, sems):
        for send, recv in copies(outs, sems):
            recv.wait_recv()
            send.wait_send()

    arrays = [full[n] for n in names]
    return CommJob(arrays, [SDS(a.shape, a.dtype) for a in arrays], True, start, finish,
                   [pltpu.SemaphoreType.DMA((len(names),))] * 2)


def allreduce_small(parts, name, job=None):
    n = len(parts)
    nji = len(job.ins) if job else 0
    njo = len(job.out_shapes) if job else 0

    def body(*refs):
        ins, jins = refs[:n], refs[n:n + nji]
        outs, jouts = refs[n + nji:2 * n + nji], refs[2 * n + nji:2 * n + nji + njo]
        bufs = refs[2 * n + nji + njo:3 * n + nji + njo]
        send, recv = refs[3 * n + nji + njo:3 * n + nji + njo + 2]
        jsems = refs[3 * n + nji + njo + 2:]
        x, y, c, _ = _place()
        me = 4 * x + 2 * y + c
        if job is not None:
            job.run(0, jins, jouts, jsems)
        sends = []
        for p in range(n):
            for k in range(1, 8):
                peer = (x ^ (k >> 2), y ^ ((k >> 1) & 1), c ^ (k & 1))
                cp = pltpu.make_async_remote_copy(src_ref=ins[p], dst_ref=bufs[p].at[k], send_sem=send.at[p, k - 1],
                                                  recv_sem=recv.at[p, k - 1], device_id=peer, device_id_type=MESH)
                cp.start()
                sends.append(cp)
            bufs[p][0] = ins[p][...]
        for cp in sends:
            cp.wait_recv()
        for p in range(n):
            acc = bufs[p][me]
            for d in range(1, 8):
                acc = acc + bufs[p][d ^ me]
            outs[p][...] = acc
        for cp in sends:
            cp.wait_send()
        if job is not None:
            job.run(1, jins, jouts, jsems)
            job.run(2, jins, jouts, jsems)

    vm = pl.BlockSpec(memory_space=pltpu.VMEM)
    res = pl.pallas_call(
        body, name=name, in_specs=[vm] * n + [ANY] * nji, out_specs=[vm] * n + [ANY] * njo,
        out_shape=[SDS(a.shape, F32) for a in parts] + (list(job.out_shapes) if job else []),
        input_output_aliases={n + i: n + i for i in range(nji)} if (job and job.inplace) else {},
        scratch_shapes=[pltpu.VMEM((8,) + tuple(a.shape), F32) for a in parts] + [pltpu.SemaphoreType.DMA((n, 7))] * 2
        + (list(job.sems) if job else []),
    )(*parts, *(job.ins if job else []))
    if job is not None:
        job.landed = list(res[n:])
    return list(res[:n])


def adamw_multi(ws, gs, ms, vs, name):
    n = len(ws)

    def body(*refs):
        for p in range(n):
            w_ref, g_ref, m_ref, v_ref = (refs[q * n + p] for q in range(4))
            d_ref, nm_ref, nv_ref = (refs[(4 + q) * n + p] for q in range(3))
            gv = g_ref[...]
            mn = ADAM_B1 * m_ref[...] + (1.0 - ADAM_B1) * gv
            vn = ADAM_B2 * v_ref[...] + (1.0 - ADAM_B2) * (gv * gv)
            m_hat = mn / (1.0 - ADAM_B1 ** ADAM_STEP)
            v_hat = vn / (1.0 - ADAM_B2 ** ADAM_STEP)
            d_ref[...] = -ADAM_LR * (m_hat / (jnp.sqrt(v_hat) + ADAM_EPS) + ADAM_WD * w_ref[...])
            nm_ref[...] = mn
            nv_ref[...] = vn

    vm = pl.BlockSpec(memory_space=pltpu.VMEM)
    outs = pl.pallas_call(body, name=name, in_specs=[vm] * (4 * n), out_specs=[vm] * (3 * n),
                          out_shape=[SDS(a.shape, F32) for a in ws] * 3)(*ws, *gs, *ms, *vs)
    return outs[:n], outs[n:2 * n], outs[2 * n:]


def pair_sum(g, xh, sel_arr, name):
    _, R, C = g.shape
    Rh = R // 2

    def body(sel_ref, g_ref, x_ref, p32_ref, p16_ref):
        s = g_ref[...] + x_ref[...]
        p16_ref[...] = s.astype(BF16)

        @pl.when(pl.program_id(0) == sel_ref[0])
        def _():
            p32_ref[...] = s

    same = pl.BlockSpec((None, Rh, C), lambda s, sel: (s, 0, 0))
    grid_spec = pltpu.PrefetchScalarGridSpec(
        num_scalar_prefetch=1, grid=(4,),
        in_specs=[pl.BlockSpec((None, Rh, C), lambda s, sel: (s, sel[1], 0)), same],
        out_specs=[pl.BlockSpec((Rh, C), lambda s, sel: (0, 0)), same])
    return pl.pallas_call(body, grid_spec=grid_spec, name=name, out_shape=[SDS((Rh, C), F32), SDS((4, Rh, C), BF16)],
                          compiler_params=_cp("arbitrary"))(sel_arr, g, xh)


def chip_sum(p32, y, sel_arr, name):
    Rh, C = p32.shape

    def body(s_ref, p_ref, y_ref, o_ref):
        o_ref[...] = ((p_ref[...] + y_ref[0].astype(F32)) + y_ref[1].astype(F32)) + y_ref[2].astype(F32)

    grid_spec = pltpu.PrefetchScalarGridSpec(
        num_scalar_prefetch=1, grid=(1,),
        in_specs=[pl.BlockSpec((Rh, C), lambda i, s: (0, 0)), pl.BlockSpec((3, Rh, C), lambda i, s: (0, 0, 0))],
        out_specs=pl.BlockSpec((Rh, C), lambda i, s: (s[1], 0)))
    return pl.pallas_call(body, grid_spec=grid_spec, name=name, out_shape=SDS((2 * Rh, C), F32),
                          compiler_params=_cp("arbitrary"))(sel_arr, p32, y)


class StepComm:
    FIRST = ("w_in",)
    SQUARE = ("w_out", "w_xq", "w_xk", "w_xv", "w_xo")
    FFN = ("w_up", "w_down")
    PLAN = {
        "proj": ("gather", SQUARE[:3]), "attn_fwd": ("gather", SQUARE[3:] + FFN),
        "d_h3": ("pair", FFN), "mixer_post_bwd_a": ("pair", SQUARE), "attn_bwd": ("chip", FFN + SQUARE),
        "dw_in": ("half", FFN + SQUARE), "first_update": ("pair", FIRST), "d_h1": ("chip", FIRST),
    }

    def __init__(self, placed, conv_placed, sel_arr):
        self.placed, self.conv_placed, self.sel_arr = placed, conv_placed, sel_arr
        self.g, self.p32, self.p16, self.full, self.done, self.jobs = {}, {}, {}, {}, {}, {}

    def job(self, tag, gb=None):
        if tag == "norm_mix":
            self.jobs[tag] = merge_jobs(gather_job(self.placed, self.FIRST), gather_rows_job(self.conv_placed))
            return self.jobs[tag]
        kind, names = self.PLAN[tag]
        if kind == "gather":
            j = gather_job(self.placed, names)
        elif kind == "pair":
            for n in names:
                self.g[n] = gb[n] if gb[n].ndim == 3 else gb[n].reshape(4, gb[n].shape[0] // 4, gb[n].shape[1])
            j = pair_exchange_job(self.g, names)
        elif kind == "chip":
            j = chip_exchange_job(self.p16, names)
        else:
            j = half_exchange_job(self.full, names)
        self.jobs[tag] = j
        return j

    def landed(self, tag, wb=None, ws=None):
        if tag == "norm_mix":
            split_landed(self.jobs[tag])
            first, conv, _ = self.jobs[tag].parts
            wb.update(zip(self.FIRST, first.landed))
            ws.update((n, a[:3]) for n, a in zip(SMALL_SHARDED, conv.landed))
            return
        kind, names = self.PLAN[tag]
        for n, a in zip(names, self.jobs[tag].landed):
            if kind == "gather":
                wb[n] = a
            elif kind == "pair":
                self.p32[n], self.p16[n] = pair_sum(self.g[n], a, self.sel_arr, "pair_sum_" + n)
            elif kind == "chip":
                self.full[n] = chip_sum(self.p32[n], a, self.sel_arr, "chip_sum_" + n)
            else:
                self.done[n] = a

    def first_update(self, gb):
        self.first_result = self.update_fn(self.job("first_update", gb))
        self.landed("first_update")

    def finish(self, gb):
        return self.done

    def last_job(self):
        self.jobs["end_half"] = half_exchange_job(self.full, self.FIRST)
        return self.jobs["end_half"]

    def take_last(self):
        self.done.update(zip(self.FIRST, self.jobs["end_half"].landed))


class NoComm:
    def job(self, tag, gb=None):
        return None

    def landed(self, tag, wb=None, ws=None):
        pass

    def first_update(self, gb):
        pass

    def finish(self, gb):
        return gb


SMALL = ("rel_bias", "g_mix", "w_short_conv", "g_attn_out", "g_conv_out", "g_xattn", "g_mem", "g_ffn",
         "w_ffn_conv", "b_ffn_conv", "g_final")
SMALL_FULL = {"rel_bias": (8, 32), "g_mix": (1, 1024), "w_short_conv": (3, 512), "g_attn_out": (1, 512), "g_conv_out": (1, 512),
              "g_xattn": (1, 1024), "g_mem": (1, 1024), "g_ffn": (1, 1024), "w_ffn_conv": (3, 5632), "b_ffn_conv": (1, 5632),
              "g_final": (1024,)}
SMALL_SHARDED = ("w_short_conv", "w_ffn_conv")


def _pack(parts):
    flat = jnp.concatenate([p.reshape(-1) for p in parts])
    rows = -(-flat.shape[0] // 1024) * 8
    return jnp.pad(flat, (0, rows * 128 - flat.shape[0])).reshape(rows, 128)


def _unpack(packed, names):
    flat, out, off = packed.reshape(-1), {}, 0
    for n in names:
        size = int(np.prod(SMALL_FULL[n]))
        out[n] = flat[off:off + size].reshape(SMALL_FULL[n])
        off += size
    return out


def kernel(x, mem, rel_bias, g_mix, w_in, w_short_conv, g_attn_out, g_conv_out, w_out, g_xattn, g_mem, w_xq, w_xk, w_xv, w_xo, g_ffn, w_up, w_ffn_conv, b_ffn_conv, w_down, g_final, loss_target, m_rel_bias, m_g_mix, m_w_in, m_w_short_conv, m_g_attn_out, m_g_conv_out, m_w_out, m_g_xattn, m_g_mem, m_w_xq, m_w_xk, m_w_xv, m_w_xo, m_g_ffn, m_w_up, m_w_ffn_conv, m_b_ffn_conv, m_w_down, m_g_final, v_rel_bias, v_g_mix, v_w_in, v_w_short_conv, v_g_attn_out, v_g_conv_out, v_w_out, v_g_xattn, v_g_mem, v_w_xq, v_w_xk, v_w_xv, v_w_xo, v_g_ffn, v_w_up, v_w_ffn_conv, v_b_ffn_conv, v_w_down, v_g_final):
    names = ("rel_bias", "g_mix", "w_in", "w_short_conv", "g_attn_out", "g_conv_out", "w_out", "g_xattn", "g_mem", "w_xq",
             "w_xk", "w_xv", "w_xo", "g_ffn", "w_up", "w_ffn_conv", "b_ffn_conv", "w_down", "g_final")
    W = dict(zip(names, (rel_bias, g_mix, w_in, w_short_conv, g_attn_out, g_conv_out, w_out, g_xattn, g_mem, w_xq, w_xk, w_xv,
                         w_xo, g_ffn, w_up, w_ffn_conv, b_ffn_conv, w_down, g_final)))
    M = dict(zip(names, (m_rel_bias, m_g_mix, m_w_in, m_w_short_conv, m_g_attn_out, m_g_conv_out, m_w_out, m_g_xattn, m_g_mem,
                         m_w_xq, m_w_xk, m_w_xv, m_w_xo, m_g_ffn, m_w_up, m_w_ffn_conv, m_b_ffn_conv, m_w_down, m_g_final)))
    V = dict(zip(names, (v_rel_bias, v_g_mix, v_w_in, v_w_short_conv, v_g_attn_out, v_g_conv_out, v_w_out, v_g_xattn, v_g_mem,
                         v_w_xq, v_w_xk, v_w_xv, v_w_xo, v_g_ffn, v_w_up, v_w_ffn_conv, v_b_ffn_conv, v_w_down, v_g_final)))
    xi, yi, ci = lax.axis_index("x"), lax.axis_index("y"), lax.axis_index("c")
    mine = 2 * xi + yi
    mine_arr = jnp.reshape(mine, (1,)).astype(jnp.int32)
    sel_arr = jnp.stack([mine, ci]).astype(jnp.int32)
    conv_placed = []
    for n in SMALL_SHARDED:
        shard = W[n][0]
        conv_placed.append(lax.dynamic_update_slice(jnp.zeros((8, SMALL_FULL[n][1]), F32), shard, (0, mine * shard.shape[1])))
    comm = StepComm({n: place_shard(W[n][0], mine_arr, n) for n in BIG}, conv_placed, sel_arr)
    ws = {n: W[n] for n in SMALL if n not in SMALL_SHARDED}

    comm.update_fn = lambda job: adamw(W["w_up"][0], comm.done["w_up"], M["w_up"][0], V["w_up"][0], "adamw_w_up", job=job)
    loss, grad_x, gfull, gs = local_step(x[0], mem[0], loss_target[0], None, ws, comm)

    def as2d(a):
        return a.reshape(1, -1) if a.ndim == 1 else a

    reduced = allreduce_small([as2d(gs[n]) for n in SMALL] + [loss], "reduce_small", job=comm.last_job())
    comm.take_last()
    gsm = dict(zip(SMALL, reduced[:-1]))
    loss = reduced[-1][0, 0]

    grads, delta, new_m, new_v = {}, {}, {}, {}
    for n in BIG:
        d, nm, nv = comm.first_result if n == "w_up" else adamw(W[n][0], gfull[n], M[n][0], V[n][0], "adamw_" + n)
        grads[n], delta[n], new_m[n], new_v[n] = gfull[n][None], d[None], nm[None], nv[None]
    for n in SMALL_SHARDED:
        wid = W[n].shape[2]
        gsm[n] = lax.dynamic_slice(gsm[n], (0, mine * wid), (3, wid))

    def own(a, n):
        return a[0] if n in SMALL_SHARDED else as2d(a)

    d, nm, nv = adamw_multi([own(W[n], n) for n in SMALL], [gsm[n] for n in SMALL], [own(M[n], n) for n in SMALL],
                            [own(V[n], n) for n in SMALL], "adamw_small")
    for i, n in enumerate(SMALL):
        shape = W[n].shape
        grads[n], delta[n], new_m[n], new_v[n] = (a.reshape(shape) for a in (gsm[n], d[i], nm[i], nv[i]))
    return (loss, grad_x[None], *[grads[n] for n in names], *[delta[n] for n in names],
            *[new_m[n] for n in names], *[new_v[n] for n in names])
```

```python
import functools
import math

import numpy as np
import jax
import jax.numpy as jnp
from jax import lax
from jax.experimental import pallas as pl
from jax.experimental.pallas import tpu as pltpu

F32 = jnp.float32
BF16 = jnp.bfloat16
SDS = jax.ShapeDtypeStruct
MESH = pl.DeviceIdType.MESH

D_MODEL = 1024
ATTN_W = 512
N_HEADS = 8
BLK = 128
PATTERNS = ((128, 1), (512, 4), (2048, 16))
N_BUCKETS = 32
D_FF = 2816
N_MEM_HEADS = 4
MEM_HD = 256
EPS = 1e-6
NEG = -1e30
VMEM_LIMIT = 56 * 1024 * 1024

ADAM_LR, ADAM_B1, ADAM_B2, ADAM_EPS, ADAM_WD, ADAM_STEP = 0.001, 0.9, 0.999, 1e-08, 0.01, 10


def _cp(*sem):
    return pltpu.CompilerParams(dimension_semantics=sem, vmem_limit_bytes=VMEM_LIMIT)


def _dot(a, b):
    return jnp.dot(a, b, preferred_element_type=F32)


def _dot_nt(a, b):
    return lax.dot_general(a, b, (((1,), (1,)), ((), ())), preferred_element_type=F32)


def _dot_tn(a, b):
    return lax.dot_general(a, b, (((0,), (0,)), ((), ())), preferred_element_type=F32)


def _rsum(x):
    return jnp.sum(x, axis=1, keepdims=True)


def rmsnorm_fwd(x, g, name, job=None):
    S, Dm = x.shape
    tm = min(S, 512)

    def body(x_ref, g_ref, o_ref):
        xv = x_ref[...]
        r = lax.rsqrt(jnp.mean(xv * xv, axis=1, keepdims=True) + EPS)
        o_ref[...] = (xv * r * g_ref[...]).astype(o_ref.dtype)

    return _pcall(body, [x, g], grid=(S // tm,), name=name,
                  in_specs=[pl.BlockSpec((tm, Dm), lambda i: (i, 0)), pl.BlockSpec((1, Dm), lambda i: (0, 0))],
                  out_specs=[pl.BlockSpec((tm, Dm), lambda i: (i, 0))], out_shape=[SDS((S, Dm), BF16)],
                  sem=("parallel",), job=job)[0]


def rmsnorm_bwd(x, g, dh, res, name, job=None):
    S, Dm = x.shape
    tm = min(S, 512)
    want_dx = res is not None

    def body(*refs):
        if want_dx:
            x_ref, g_ref, dh_ref, res_ref, dx_ref, dg_ref = refs
        else:
            x_ref, g_ref, dh_ref, dg_ref = refs
        i = pl.program_id(0)
        xv = x_ref[...]
        r = lax.rsqrt(jnp.mean(xv * xv, axis=1, keepdims=True) + EPS)
        xh = xv * r
        dh = dh_ref[...].astype(F32)
        if want_dx:
            gd = dh * g_ref[...]
            dx_ref[...] = res_ref[...] + r * (gd - xh * jnp.mean(gd * xh, axis=1, keepdims=True))

        @pl.when(i == 0)
        def _():
            dg_ref[...] = jnp.zeros_like(dg_ref)

        dg_ref[...] += jnp.sum(dh * xh, axis=0, keepdims=True)

    row = pl.BlockSpec((tm, Dm), lambda i: (i, 0))
    vec = pl.BlockSpec((1, Dm), lambda i: (0, 0))
    if want_dx:
        return _pcall(body, [x, g, dh, res], grid=(S // tm,), name=name, in_specs=[row, vec, row, row], out_specs=[row, vec],
                      out_shape=[SDS((S, Dm), F32), SDS((1, Dm), F32)], sem=("arbitrary",), job=job)
    return pl.pallas_call(
        body, grid=(S // tm,), name=name, in_specs=[row, vec, row], out_specs=vec,
        out_shape=SDS((1, Dm), F32), compiler_params=_cp("arbitrary"))(x, g, dh)


def _col_spec(bw, tn, rows, row_of, col_of):
    if bw is None:
        return pl.BlockSpec((rows, tn), lambda *g: (row_of(*g), col_of(*g)))
    if tn % bw == 0:
        return pl.BlockSpec((tn // bw, rows, bw), lambda *g: (col_of(*g), row_of(*g), 0))
    per = bw // tn
    return pl.BlockSpec((None, rows, tn), lambda *g: (col_of(*g) // per, row_of(*g), col_of(*g) % per))


def _read_cols(ref, bw, tn):
    if bw is None or tn % bw != 0:
        return ref[...]
    if tn == bw:
        return ref[0]
    return jnp.concatenate([ref[q] for q in range(tn // bw)], axis=1)


def _write_cols(ref, bw, tn, val):
    if bw is None or tn % bw != 0:
        ref[...] = val.astype(ref.dtype)
    else:
        for q in range(tn // bw):
            ref[q] = val[:, q * bw:(q + 1) * bw].astype(ref.dtype)


def mm_nn(a, b, name, *, res=None, out_dtype=F32, out_bw=None, tm=1024, tn=512, job=None):
    M, K = a.shape
    N = b.shape[1]
    tm = min(tm, M)

    def body(*refs):
        if res is None:
            a_ref, b_ref, o_ref = refs
        else:
            a_ref, b_ref, r_ref, o_ref = refs
        acc = _dot(a_ref[...].astype(BF16), b_ref[...])
        if res is not None:
            acc = acc + r_ref[...]
        _write_cols(o_ref, out_bw, tn, acc)

    ri, ci = (lambda i, j: i), (lambda i, j: j)
    in_specs = [pl.BlockSpec((tm, K), lambda i, j: (i, 0)), pl.BlockSpec((K, tn), lambda i, j: (0, j))]
    args = [a, b]
    if res is not None:
        in_specs.append(pl.BlockSpec((tm, tn), lambda i, j: (i, j)))
        args.append(res)
    oshape = (M, N) if out_bw is None else (N // out_bw, M, out_bw)
    return _pcall(body, args, grid=(M // tm, N // tn), name=name, in_specs=in_specs,
                  out_specs=[_col_spec(out_bw, tn, tm, ri, ci)], out_shape=[SDS(oshape, out_dtype)],
                  sem=("parallel", "parallel"), job=job)[0]


def mm_nn_rows(a, b, res, g, name, *, target=None, tm=512):
    M, K = a.shape
    N = b.shape[1]

    def body(*refs):
        a_ref, b_ref, r_ref, g_ref = refs[:4]
        xv = r_ref[...] + _dot(a_ref[...].astype(BF16), b_ref[...])
        r = lax.rsqrt(jnp.mean(xv * xv, axis=1, keepdims=True) + EPS)
        xh = xv * r
        if target is None:
            x_ref, h_ref = refs[4:]
            x_ref[...] = xv
            h_ref[...] = (xh * g_ref[...]).astype(BF16)
            return
        t_ref, loss_ref, dx_ref, dg_ref = refs[4:]
        err = xh * g_ref[...] - t_ref[...]
        dy = err / N
        gd = dy * g_ref[...]
        dx_ref[...] = r * (gd - xh * jnp.mean(gd * xh, axis=1, keepdims=True))

        @pl.when(pl.program_id(0) == 0)
        def _():
            dg_ref[...] = jnp.zeros_like(dg_ref)
            loss_ref[...] = jnp.zeros_like(loss_ref)

        dg_ref[...] += _colsum(dy * xh)
        loss_ref[...] += 0.5 * _colsum(jnp.mean(err * err, axis=1, keepdims=True))

    row = pl.BlockSpec((tm, N), lambda i: (i, 0))
    vec = pl.BlockSpec((1, N), lambda i: (0, 0))
    in_specs = [pl.BlockSpec((tm, K), lambda i: (i, 0)), pl.BlockSpec((K, N), lambda i: (0, 0)), row, vec]
    if target is None:
        return pl.pallas_call(body, grid=(M // tm,), name=name, in_specs=in_specs, out_specs=[row, row],
                              out_shape=[SDS((M, N), F32), SDS((M, N), BF16)], compiler_params=_cp("parallel"))(a, b, res, g)
    one = pl.BlockSpec((1, 1), lambda i: (0, 0))
    return pl.pallas_call(body, grid=(M // tm,), name=name, in_specs=in_specs + [row], out_specs=[one, row, vec],
                          out_shape=[SDS((1, 1), F32), SDS((M, N), F32), SDS((1, N), F32)],
                          compiler_params=_cp("arbitrary"))(a, b, res, g, target)


def mm_nt(a, a_bw, b, name, *, res=None, out_dtype=F32, tm=1024, tn=512, job=None):
    if a_bw is None:
        M, K = a.shape
    else:
        M, K = a.shape[1], a.shape[0] * a_bw
    N = b.shape[0]
    tm = min(tm, M)

    def body(*refs):
        if res is None:
            a_ref, b_ref, o_ref = refs
        else:
            a_ref, b_ref, r_ref, o_ref = refs
        av = _read_cols(a_ref, a_bw, K).astype(BF16)
        acc = _dot_nt(av, b_ref[...])
        if res is not None:
            acc = acc + r_ref[...]
        o_ref[...] = acc.astype(o_ref.dtype)

    in_specs = [_col_spec(a_bw, K, tm, lambda i, j: i, lambda i, j: 0), pl.BlockSpec((tn, K), lambda i, j: (j, 0))]
    args = [a, b]
    if res is not None:
        in_specs.append(pl.BlockSpec((tm, tn), lambda i, j: (i, j)))
        args.append(res)
    return _pcall(body, args, grid=(M // tm, N // tn), name=name, in_specs=in_specs,
                  out_specs=[pl.BlockSpec((tm, tn), lambda i, j: (i, j))], out_shape=[SDS((M, N), out_dtype)],
                  sem=("parallel", "parallel"), job=job)[0]


def mm_nt_norm_bwd(a, a_bw, b, x, g, res, name, *, tm=512, job=None):
    if a_bw is None:
        M, K = a.shape
    else:
        M, K = a.shape[1], a.shape[0] * a_bw
    N = b.shape[0]

    def body(a_ref, b_ref, x_ref, g_ref, r_ref, dx_ref, dg_ref):
        dh = _dot_nt(_read_cols(a_ref, a_bw, K).astype(BF16), b_ref[...])
        xv = x_ref[...]
        r = lax.rsqrt(jnp.mean(xv * xv, axis=1, keepdims=True) + EPS)
        xh = xv * r
        gd = dh * g_ref[...]
        dx_ref[...] = r_ref[...] + r * (gd - xh * jnp.mean(gd * xh, axis=1, keepdims=True))

        @pl.when(pl.program_id(0) == 0)
        def _():
            dg_ref[...] = jnp.zeros_like(dg_ref)

        dg_ref[...] += jnp.sum(dh * xh, axis=0, keepdims=True)

    row = pl.BlockSpec((tm, N), lambda i: (i, 0))
    vec = pl.BlockSpec((1, N), lambda i: (0, 0))
    return _pcall(body, [a, b, x, g, res], grid=(M // tm,), name=name,
                  in_specs=[_col_spec(a_bw, K, tm, lambda i: i, lambda i: 0), pl.BlockSpec((N, K), lambda i: (0, 0)), row, vec, row],
                  out_specs=[row, vec], out_shape=[SDS((M, N), F32), SDS((1, N), F32)], sem=("arbitrary",), job=job)


def mm_tn(a, b, b_bw, name, *, shards=None, tm=1024, tn=1024, ts=1024, job=None):
    S, Ka = a.shape
    N = b.shape[1] if b_bw is None else b.shape[0] * b_bw
    ts = min(ts, S)
    tm = min(tm, Ka)

    def body(a_ref, b_ref, o_ref):
        @pl.when(pl.program_id(2) == 0)
        def _():
            o_ref[...] = jnp.zeros_like(o_ref)

        bv = _read_cols(b_ref, b_bw, tn).astype(BF16)
        o_ref[...] += _dot_tn(a_ref[...].astype(BF16), bv)

    in_specs = [pl.BlockSpec((ts, tm), lambda i, j, k: (k, i)),
                _col_spec(b_bw, tn, ts, lambda i, j, k: k, lambda i, j, k: j)]
    if shards is None:
        out_spec, oshape = pl.BlockSpec((tm, tn), lambda i, j, k: (i, j)), (Ka, N)
    else:
        per = (N // shards) // tn
        out_spec = pl.BlockSpec((None, tm, tn), lambda i, j, k: (j // per, i, j % per))
        oshape = (shards, Ka, N // shards)
    return _pcall(body, [a, b], grid=(Ka // tm, N // tn, S // ts), name=name, in_specs=in_specs, out_specs=[out_spec],
                  out_shape=[SDS(oshape, F32)], sem=("parallel", "parallel", "arbitrary"), job=job)[0]


KEYS = 2 * BLK


def _bucket_tables():
    out = np.zeros((3, BLK, KEYS), np.int32)
    qi = np.arange(BLK)[:, None]
    kj = np.arange(KEYS)[None, :]
    for p, (win, dil) in enumerate(PATTERNS):
        w = win // dil
        assert w == BLK
        steps = qi + w - kj
        valid = (steps >= 0) & (steps <= w)
        dist = np.clip(steps, 0, w) * dil
        dd = np.maximum(dist, 1).astype(np.float32)
        large = 16 + (np.log(dd / np.float32(16)) / np.float32(math.log(2048 / 16)) * np.float32(16)).astype(np.int32)
        large = np.minimum(large, N_BUCKETS - 1)
        out[p] = np.where(valid, np.where(dist < 16, dist, large), -1)
    return out


def bias_tables(rel_bias):
    bkt = jnp.asarray(_bucket_tables())

    def body(rb_ref, bkt_ref, o_ref):
        h = pl.program_id(1)
        bk = bkt_ref[...]
        acc = jnp.full((BLK, KEYS), NEG, F32)
        for b in range(N_BUCKETS):
            acc = jnp.where(bk == b, rb_ref[h, b], acc)
        o_ref[...] = acc

    return pl.pallas_call(
        body, grid=(3, N_HEADS), name="bias_tables",
        in_specs=[pl.BlockSpec(memory_space=pltpu.SMEM),
                  pl.BlockSpec((None, BLK, KEYS), lambda p, h: (p, 0, 0))],
        out_specs=pl.BlockSpec((None, None, BLK, KEYS), lambda p, h: (p, h, 0, 0)),
        out_shape=SDS((3, N_HEADS, BLK, KEYS), F32), compiler_params=_cp("parallel", "parallel"))(rel_bias, bkt)


def bias_tables_bwd(ds_sum):
    bkt = jnp.asarray(_bucket_tables())

    def body(ds_ref, bkt_ref, o_ref):
        h = pl.program_id(0)

        @pl.when(h == 0)
        def _():
            o_ref[...] = jnp.zeros_like(o_ref)

        rows = lax.broadcasted_iota(jnp.int32, (N_HEADS, N_BUCKETS), 0)
        cols = lax.broadcasted_iota(jnp.int32, (N_HEADS, N_BUCKETS), 1)
        acc = jnp.zeros((N_HEADS, N_BUCKETS), F32)
        for b in range(N_BUCKETS):
            t = jnp.zeros((BLK, KEYS), F32)
            for p in range(3):
                t = t + jnp.where(bkt_ref[p] == b, ds_ref[p], 0.0)
            tot = jnp.sum(_rsum(t), axis=0, keepdims=True)
            acc = acc + jnp.where((rows == h) & (cols == b), tot, 0.0)
        o_ref[...] += acc

    return pl.pallas_call(
        body, grid=(N_HEADS,), name="bias_tables_bwd",
        in_specs=[pl.BlockSpec((3, None, BLK, KEYS), lambda h: (0, h, 0, 0)),
                  pl.BlockSpec((3, BLK, KEYS), lambda h: (0, 0, 0))],
        out_specs=pl.BlockSpec((N_HEADS, N_BUCKETS), lambda h: (0, 0)),
        out_shape=SDS((N_HEADS, N_BUCKETS), F32), compiler_params=_cp("arbitrary"))(ds_sum, bkt)


def _rows(start, dil):
    if dil == 1:
        return pl.ds(pl.multiple_of(start, BLK), BLK)
    return pl.ds(start, BLK, stride=dil)


GRP = 8


def _group_rows(i, dil, S):
    nb = S // (BLK * dil)
    run = min(nb, GRP)
    chunks = nb // run
    res0, b0 = (i // chunks) * (GRP // run), (i % chunks) * run
    cur = [(b0 + j % run) * (BLK * dil) + res0 + j // run for j in range(GRP)]
    t = lax.broadcasted_iota(jnp.int32, (GRP, 1, 1), 0)
    if run == nb:
        before, pen = None, jnp.where(t % run == 0, NEG, 0.0)
    else:
        before = _rows(jnp.maximum(b0 - 1, 0) * (BLK * dil) + res0, dil)
        pen = jnp.where((t == 0) & (b0 == 0), NEG, 0.0)
    return [_rows(s, dil) for s in cur], before, pen


def _load_group(ref, rows):
    return jnp.stack([ref[r, :] for r in rows])


def _with_prev(ref, before, cur_blocks):
    first = cur_blocks[:1] if before is None else ref[before, :][None].astype(cur_blocks.dtype)
    return jnp.concatenate([first, cur_blocks[:-1]], axis=0)


def _bdot_nt(a, b):
    return lax.dot_general(a, b, (((2,), (2,)), ((0,), (0,))), preferred_element_type=F32)


def _bdot(a, b):
    return lax.dot_general(a, b, (((2,), (1,)), ((0,), (0,))), preferred_element_type=F32)


def _bdot_tn(a, b):
    return lax.dot_general(a, b, (((1,), (1,)), ((0,), (0,))), preferred_element_type=F32)


def _lsum(x):
    return jnp.sum(x, axis=-1, keepdims=True)


def _widen(src, dst):
    S = src.shape[1]

    def chunk(i, carry):
        rows = pl.ds(pl.multiple_of(i * 512, 512), 512)
        for a in range(3):
            dst[a, rows, :] = src[a, rows, :].astype(F32)
        return carry

    lax.fori_loop(0, S // 512, chunk, 0)


def attn_fwd(qkv, bias, job=None):
    S = qkv.shape[3]
    nblk = S // BLK
    car = Carried(job, 2, 2, 1, 4)

    def body(*refs):
        (qkv_in, bias_ref, o_ref, lse_ref, qkv_ref), jrefs = car.split(refs)
        car.phase(0, pl.program_id(0), jrefs)
        _widen(qkv_in, qkv_ref)
        lane = lax.broadcasted_iota(jnp.int32, (GRP, BLK, BLK), 2)
        lo = lane < 64
        masks = (lo, jnp.logical_not(lo))
        lov = lax.broadcasted_iota(jnp.int32, (GRP, KEYS, BLK), 2) < 64
        vmasks = (lov, jnp.logical_not(lov))
        prev_keys = lax.broadcasted_iota(jnp.int32, (1, 1, KEYS), 2) < BLK
        q_ref, k_ref, v_ref = qkv_ref.at[0], qkv_ref.at[1], qkv_ref.at[2]
        for p, (_, dil) in enumerate(PATTERNS):
            def step(i, carry, p=p, dil=dil):
                rc, before, pen = _group_rows(i, dil, S)
                q2 = _load_group(q_ref, rc) * 0.125
                kc = _load_group(k_ref, rc).astype(BF16)
                keys = jnp.concatenate([_with_prev(k_ref, before, kc), kc], axis=1)
                vc = _load_group(v_ref, rc)
                vals = jnp.concatenate([_with_prev(v_ref, before, vc), vc], axis=1)
                pen = jnp.where(prev_keys, pen, 0.0)
                if p > 0:
                    acc_old, ml_old = _load_group(o_ref, rc), _load_group(lse_ref, rc)
                pv = jnp.zeros((GRP, BLK, BLK), F32)
                m_new, l_new, alpha = [], [], []
                for h in range(2):
                    qh = jnp.where(masks[h], q2, 0.0).astype(BF16)
                    s = _bdot_nt(qh, keys) + bias_ref[p, h][None] + pen
                    mn = jnp.max(s, axis=-1, keepdims=True)
                    if p > 0:
                        mo = ml_old[:, :, 64 * h:64 * h + 1]
                        mn = jnp.maximum(mn, mo)
                        al = jnp.exp(mo - mn)
                        alpha.append(al)
                    pr = jnp.exp(s - mn)
                    ln = _lsum(pr)
                    if p > 0:
                        ln = ln + al * ml_old[:, :, 64 * h + 32:64 * h + 33]
                    pv = pv + _bdot(pr.astype(BF16), jnp.where(vmasks[h], vals, 0.0).astype(BF16))
                    m_new.append(mn)
                    l_new.append(ln)
                if p > 0:
                    pv = pv + acc_old * jnp.where(lo, alpha[0], alpha[1])
                ml_t = jnp.where(lo, jnp.where(lane < 32, m_new[0], l_new[0]), jnp.where(lane < 96, m_new[1], l_new[1]))
                for j, r in enumerate(rc):
                    o_ref[r, :] = pv[j]
                    lse_ref[r, :] = ml_t[j]
                return carry

            lax.fori_loop(0, nblk // GRP, step, 0)

        def fin(i, carry):
            rows = pl.ds(pl.multiple_of(i * 512, 512), 512)
            ml = lse_ref[rows, :]
            is_m = (lax.broadcasted_iota(jnp.int32, ml.shape, 1) & 32) == 0
            den = jnp.where(is_m, pltpu.roll(ml, 96, 1), ml)
            o_ref[rows, :] = o_ref[rows, :] / den
            lse_ref[rows, :] = ml + jnp.log(den)
            return carry

        lax.fori_loop(0, S // 512, fin, 0)
        car.phase(1, pl.program_id(0), jrefs)
        car.phase(2, pl.program_id(0), jrefs)

    outs = pl.pallas_call(
        body, grid=(4,), name="attn_fwd",
        in_specs=[pl.BlockSpec((None, 3, None, S, BLK), lambda g: (0, 0, g, 0, 0)),
                  pl.BlockSpec((3, 2, BLK, KEYS), lambda g: (0, g, 0, 0))] + car.in_specs(),
        out_specs=[pl.BlockSpec((None, S, BLK), lambda g: (g, 0, 0)),
                   pl.BlockSpec((None, S, BLK), lambda g: (g, 0, 0))] + car.out_specs(),
        out_shape=[SDS((4, S, BLK), F32), SDS((4, S, BLK), F32)] + car.out_shapes(),
        input_output_aliases=car.aliases(),
        scratch_shapes=[pltpu.VMEM((3, S, BLK), F32)] + car.sems(),
        compiler_params=_cp("arbitrary"))(qkv, bias, *car.args())
    if job is not None:
        job.landed = list(outs[2:])
    return outs[0], outs[1]


def attn_bwd(qkv, bias, o, lse, do, job=None):
    S = qkv.shape[3]
    nblk = S // BLK
    car = Carried(job, 5, 2, 2, 4)

    def body(*refs):
        (qkv_in, bias_ref, o_ref, lse_ref, do_ref, dqkv_out, ds_ref, qkv_ref, dqkv_ref), jrefs = car.split(refs)
        car.phase(0, pl.program_id(0), jrefs)
        _widen(qkv_in, qkv_ref)
        lane = lax.broadcasted_iota(jnp.int32, (GRP, BLK, BLK), 2)
        lo = lane < 64
        masks = (lo, jnp.logical_not(lo))
        lov = lax.broadcasted_iota(jnp.int32, (GRP, KEYS, BLK), 2) < 64
        vmasks = (lov, jnp.logical_not(lov))
        prev_keys = lax.broadcasted_iota(jnp.int32, (1, 1, KEYS), 2) < BLK
        dqkv_ref[...] = jnp.zeros_like(dqkv_ref)
        ds_ref[...] = jnp.zeros_like(ds_ref)
        q_ref, k_ref, v_ref = qkv_ref.at[0], qkv_ref.at[1], qkv_ref.at[2]
        dq_ref, dk_ref, dv_ref = dqkv_ref.at[0], dqkv_ref.at[1], dqkv_ref.at[2]
        for p, (_, dil) in enumerate(PATTERNS):
            def step(i, carry, p=p, dil=dil):
                rc, before, pen = _group_rows(i, dil, S)
                q2 = _load_group(q_ref, rc) * 0.125
                kc = _load_group(k_ref, rc).astype(BF16)
                keys = jnp.concatenate([_with_prev(k_ref, before, kc), kc], axis=1)
                vc = _load_group(v_ref, rc).astype(BF16)
                vals = jnp.concatenate([_with_prev(v_ref, before, vc), vc], axis=1)
                pen = jnp.where(prev_keys, pen, 0.0)
                dot = _load_group(do_ref, rc)
                lset = _load_group(lse_ref, rc)
                prod = dot * _load_group(o_ref, rc)
                dq = jnp.zeros((GRP, BLK, BLK), F32)
                dkeys = jnp.zeros((GRP, KEYS, BLK), F32)
                dvals = jnp.zeros((GRP, KEYS, BLK), F32)
                for h in range(2):
                    qh = jnp.where(masks[h], q2, 0.0).astype(BF16)
                    doh = jnp.where(masks[h], dot, 0.0).astype(BF16)
                    delta = _lsum(jnp.where(masks[h], prod, 0.0))
                    lh = lset[:, :, 64 * h:64 * h + 1]
                    pr = jnp.exp(_bdot_nt(qh, keys) + bias_ref[p, h][None] + pen - lh)
                    ds = pr * (_bdot_nt(doh, vals) - delta)
                    ds_ref[p, h] += jnp.sum(ds, axis=0)
                    dsb = ds.astype(BF16)
                    dq = dq + jnp.where(masks[h], _bdot(dsb, keys), 0.0)
                    dkeys = dkeys + _bdot_tn(dsb, qh)
                    dvals = dvals + _bdot_tn(pr.astype(BF16), doh)
                dkp, dkc, dvp, dvc = dkeys[:, :BLK], dkeys[:, BLK:], dvals[:, :BLK], dvals[:, BLK:]
                none = jnp.zeros((1, BLK, BLK), F32)
                dkc = dkc + jnp.concatenate([dkp[1:], none], axis=0)
                dvc = dvc + jnp.concatenate([dvp[1:], none], axis=0)
                for j in range(GRP):
                    dq_ref[rc[j], :] += dq[j] * 0.125
                    dk_ref[rc[j], :] += dkc[j]
                    dv_ref[rc[j], :] += dvc[j]
                if before is not None:
                    dk_ref[before, :] += dkp[0]
                    dv_ref[before, :] += dvp[0]
                return carry

            lax.fori_loop(0, nblk // GRP, step, 0)

        def narrow(i, carry):
            rows = pl.ds(pl.multiple_of(i * 512, 512), 512)
            for a in range(3):
                dqkv_out[a, rows, :] = dqkv_ref[a, rows, :].astype(BF16)
            return carry

        lax.fori_loop(0, S // 512, narrow, 0)
        car.phase(1, pl.program_id(0), jrefs)
        car.phase(2, pl.program_id(0), jrefs)

    blk = pl.BlockSpec((None, S, BLK), lambda g: (g, 0, 0))
    outs = pl.pallas_call(
        body, grid=(4,), name="attn_bwd",
        in_specs=[pl.BlockSpec((None, 3, None, S, BLK), lambda g: (0, 0, g, 0, 0)),
                  pl.BlockSpec((3, 2, BLK, KEYS), lambda g: (0, g, 0, 0)), blk, blk, blk] + car.in_specs(),
        out_specs=[pl.BlockSpec((None, 3, None, S, BLK), lambda g: (0, 0, g, 0, 0)),
                   pl.BlockSpec((3, 2, BLK, KEYS), lambda g: (0, g, 0, 0))] + car.out_specs(),
        out_shape=[SDS((2, 3, 4, S, BLK), BF16), SDS((3, N_HEADS, BLK, KEYS), F32)] + car.out_shapes(),
        input_output_aliases=car.aliases(),
        scratch_shapes=[pltpu.VMEM((3, S, BLK), F32), pltpu.VMEM((3, S, BLK), F32)] + car.sems(),
        compiler_params=_cp("arbitrary"))(qkv, bias, o, lse, do, *car.args())
    if job is not None:
        job.landed = list(outs[2:])
    return outs[0], outs[1]


def _shift_down(u, k, halo):
    n = u.shape[0]
    row = lax.broadcasted_iota(jnp.int32, u.shape, 0)
    out = pltpu.roll(u, k, 0)
    hn = halo.shape[0]
    for j in range(k):
        out = jnp.where(row == j, halo[hn - k + j:hn - k + j + 1, :], out)
    return out


def _shift_up(u, k, halo):
    n = u.shape[0]
    row = lax.broadcasted_iota(jnp.int32, u.shape, 0)
    out = pltpu.roll(u, n - k, 0)
    for j in range(k):
        out = jnp.where(row == n - k + j, halo[j:j + 1, :], out)
    return out


def _conv3(u, halo, w0, w1, w2):
    return _shift_down(u, 2, halo) * w0 + _shift_down(u, 1, halo) * w1 + u * w2


def _colsum(x):
    return jnp.sum(x, axis=0, keepdims=True)


def _mixer_specs(S, tm):
    conv = pl.BlockSpec((None, 12, tm, BLK), lambda i: (1, 0, i, 0))
    halo = pl.BlockSpec((None, 12, 16, BLK), lambda i: (1, 0, jnp.maximum(i * (tm // 16) - 1, 0), 0))
    ob = pl.BlockSpec((4, tm, BLK), lambda i: (0, i, 0))
    return conv, halo, ob


def _mixer_recompute(i, o_ref, pr_ref, ph_ref, w_ref):
    ob = [o_ref[q] for q in range(4)]
    gb = [pr_ref[q].astype(F32) for q in range(4)]
    gc = [pr_ref[4 + q].astype(F32) for q in range(4)]
    xi = [pr_ref[8 + q].astype(F32) for q in range(4)]
    keep = jnp.where(i > 0, 1.0, 0.0)
    u = [gc[q] * xi[q] for q in range(4)]
    hu = [ph_ref[4 + q].astype(F32) * ph_ref[8 + q].astype(F32) * keep for q in range(4)]
    w = [[w_ref[k:k + 1, q * BLK:(q + 1) * BLK] for k in range(3)] for q in range(4)]
    cv = [_conv3(u[q], hu[q], *w[q]) for q in range(4)]
    return ob, gb, gc, xi, u, hu, cv, w


def _rms_blocks(blocks):
    ss = sum(_rsum(b * b) for b in blocks)
    return lax.rsqrt(ss / (BLK * len(blocks)) + EPS)


def mixer_post_fwd(o, proj, w_sc, g_a, g_c):
    S = o.shape[1]
    tm = 512

    def body(o_ref, pr_ref, ph_ref, w_ref, ga_ref, gc_ref, m_ref):
        i = pl.program_id(0)
        ob, gb, _, _, _, _, cv, _ = _mixer_recompute(i, o_ref, pr_ref, ph_ref, w_ref)
        conv = [gb[q] * cv[q] for q in range(4)]
        ra, rc = _rms_blocks(ob), _rms_blocks(conv)
        for q in range(4):
            sl = slice(q * BLK, (q + 1) * BLK)
            m_ref[:, q * BLK:(q + 1) * BLK] = (ob[q] * ra * ga_ref[:, sl]).astype(BF16)
            m_ref[:, ATTN_W + q * BLK:ATTN_W + (q + 1) * BLK] = (conv[q] * rc * gc_ref[:, sl]).astype(BF16)

    conv_s, halo_s, ob_s = _mixer_specs(S, tm)
    full = lambda r, c: pl.BlockSpec((r, c), lambda i: (0, 0))
    return pl.pallas_call(
        body, grid=(S // tm,), name="mixer_post_fwd",
        in_specs=[ob_s, conv_s, halo_s, full(3, 512), full(1, 512), full(1, 512)],
        out_specs=pl.BlockSpec((tm, D_MODEL), lambda i: (i, 0)), out_shape=SDS((S, D_MODEL), BF16),
        compiler_params=_cp("parallel"))(o, proj, proj, w_sc, g_a, g_c)


def mixer_post_bwd_a(dmixed, o, proj, w_sc, g_a, g_c, job=None):
    S = o.shape[1]
    tm = 512

    def body(dm_ref, o_ref, pr_ref, ph_ref, w_ref, ga_ref, gc_ref, do_ref, dgb_ref, dcv_ref, dga_ref, dgc_ref):
        i = pl.program_id(0)
        ob, gb, _, _, _, _, cv, _ = _mixer_recompute(i, o_ref, pr_ref, ph_ref, w_ref)
        conv = [gb[q] * cv[q] for q in range(4)]
        ra, rc = _rms_blocks(ob), _rms_blocks(conv)

        @pl.when(i == 0)
        def _():
            dga_ref[...] = jnp.zeros_like(dga_ref)
            dgc_ref[...] = jnp.zeros_like(dgc_ref)

        for blocks, r, g_ref, off, dg_ref, is_attn in ((ob, ra, ga_ref, 0, dga_ref, True), (conv, rc, gc_ref, ATTN_W, dgc_ref, False)):
            xh = [blocks[q] * r for q in range(4)]
            dm = [dm_ref[:, off + q * BLK:off + (q + 1) * BLK].astype(F32) for q in range(4)]
            gd = [dm[q] * g_ref[:, q * BLK:(q + 1) * BLK] for q in range(4)]
            mean = sum(_rsum(gd[q] * xh[q]) for q in range(4)) / (4 * BLK)
            for q in range(4):
                dg_ref[:, q * BLK:(q + 1) * BLK] += _colsum(dm[q] * xh[q])
                dx = r * (gd[q] - xh[q] * mean)
                if is_attn:
                    do_ref[q] = dx
                else:
                    dgb_ref[q] = dx * cv[q]
                    dcv_ref[q] = dx * gb[q]

    conv_s, halo_s, ob_s = _mixer_specs(S, tm)
    full = lambda r, c: pl.BlockSpec((r, c), lambda i: (0, 0))
    return _pcall(
        body, [dmixed, o, proj, proj, w_sc, g_a, g_c], grid=(S // tm,), name="mixer_post_bwd_a",
        in_specs=[pl.BlockSpec((tm, D_MODEL), lambda i: (i, 0)), ob_s, conv_s, halo_s, full(3, 512), full(1, 512), full(1, 512)],
        out_specs=[ob_s, ob_s, ob_s, full(1, 512), full(1, 512)],
        out_shape=[SDS((4, S, BLK), F32)] * 3 + [SDS((1, 512), F32)] * 2, sem=("arbitrary",), job=job)


def mixer_post_bwd_b(dproj, dgb, dcv, proj, w_sc):
    S = proj.shape[2]
    tm = 512
    last = S // 8 - 1

    def body(dp_in, dgb_ref, dcv_ref, dn_ref, pr_ref, w_ref, dp_ref, dw_ref):
        i = pl.program_id(0)
        keep_next = jnp.where(i < pl.num_programs(0) - 1, 1.0, 0.0)

        @pl.when(i == 0)
        def _():
            dw_ref[...] = jnp.zeros_like(dw_ref)

        for q in range(4):
            sl = slice(q * BLK, (q + 1) * BLK)
            gc, xi = pr_ref[4 + q].astype(F32), pr_ref[8 + q].astype(F32)
            u = gc * xi
            dcv = dcv_ref[q]
            dn = dn_ref[q] * keep_next
            d1, d2 = _shift_up(dcv, 1, dn), _shift_up(dcv, 2, dn)
            w0, w1, w2 = (w_ref[k:k + 1, sl] for k in range(3))
            du = dcv * w2 + d1 * w1 + d2 * w0
            dw_ref[0:1, sl] += _colsum(d2 * u)
            dw_ref[1:2, sl] += _colsum(d1 * u)
            dw_ref[2:3, sl] += _colsum(dcv * u)
            dp_ref[q] = dgb_ref[q].astype(BF16)
            dp_ref[4 + q] = (du * xi).astype(BF16)
            dp_ref[8 + q] = (du * gc).astype(BF16)

    conv_s, _, ob_s = _mixer_specs(S, tm)
    nxt = pl.BlockSpec((4, 8, BLK), lambda i: (0, jnp.minimum((i + 1) * (tm // 8), last), 0))
    full = lambda r, c: pl.BlockSpec((r, c), lambda i: (0, 0))
    return pl.pallas_call(
        body, grid=(S // tm,), name="mixer_post_bwd_b",
        in_specs=[pl.BlockSpec(memory_space=pl.ANY), ob_s, ob_s, nxt, conv_s, full(3, 512)],
        out_specs=[conv_s, full(3, 512)],
        out_shape=[SDS(dproj.shape, BF16), SDS((3, 512), F32)],
        input_output_aliases={0: 0}, compiler_params=_cp("arbitrary"))(dproj, dgb, dcv, dcv, proj, w_sc)


def xattn_fwd(q, k, v):
    S = q.shape[0]
    tm = 512
    scale = MEM_HD ** -0.5

    def body(q_ref, k_ref, v_ref, o_ref):
        for h in range(N_MEM_HEADS):
            sl = slice(h * MEM_HD, (h + 1) * MEM_HD)
            s = _dot_nt(q_ref[:, sl], k_ref[:, sl]) * scale
            p = jnp.exp(s - jnp.max(s, axis=1, keepdims=True))
            p = p / _rsum(p)
            o_ref[:, sl] = _dot(p.astype(BF16), v_ref[:, sl]).astype(BF16)

    row = pl.BlockSpec((tm, D_MODEL), lambda i: (i, 0))
    kv = pl.BlockSpec(k.shape, lambda i: (0, 0))
    return pl.pallas_call(body, grid=(S // tm,), name="xattn_fwd", in_specs=[row, kv, kv], out_specs=row,
                          out_shape=SDS((S, D_MODEL), BF16), compiler_params=_cp("parallel"))(q, k, v)


def xattn_bwd(q, k, v, do):
    S = q.shape[0]
    tm = 512
    scale = MEM_HD ** -0.5

    def body(q_ref, k_ref, v_ref, do_ref, dq_ref, dk_ref, dv_ref):
        @pl.when(pl.program_id(0) == 0)
        def _():
            dk_ref[...] = jnp.zeros_like(dk_ref)
            dv_ref[...] = jnp.zeros_like(dv_ref)

        for h in range(N_MEM_HEADS):
            sl = slice(h * MEM_HD, (h + 1) * MEM_HD)
            qh, kh, vh, doh = q_ref[:, sl], k_ref[:, sl], v_ref[:, sl], do_ref[:, sl]
            s = _dot_nt(qh, kh) * scale
            p = jnp.exp(s - jnp.max(s, axis=1, keepdims=True))
            p = p / _rsum(p)
            dp = _dot_nt(doh, vh)
            ds = (p * (dp - _rsum(p * dp)) * scale).astype(BF16)
            dq_ref[:, sl] = _dot(ds, kh).astype(BF16)
            dk_ref[:, sl] += _dot_tn(ds, qh)
            dv_ref[:, sl] += _dot_tn(p.astype(BF16), doh)

    row = pl.BlockSpec((tm, D_MODEL), lambda i: (i, 0))
    kv = pl.BlockSpec(k.shape, lambda i: (0, 0))
    return pl.pallas_call(body, grid=(S // tm,), name="xattn_bwd", in_specs=[row, kv, kv, row], out_specs=[row, kv, kv],
                          out_shape=[SDS((S, D_MODEL), BF16), SDS(k.shape, F32), SDS(k.shape, F32)],
                          compiler_params=_cp("arbitrary"))(q, k, v, do)


FFN_TM, FFN_TC = 256, 1408


def _ffn_specs(S, order):
    tm, tc = FFN_TM, FFN_TC
    ij = (lambda a, b: (a, b)) if order == "ij" else (lambda a, b: (b, a))
    blk = pl.BlockSpec((2, tm, tc), lambda a, b: (0,) + ij(a, b))
    prev = pl.BlockSpec((2, 16, tc), lambda a, b: (0, jnp.maximum(ij(a, b)[0] * (tm // 16) - 1, 0), ij(a, b)[1]))
    nxt = pl.BlockSpec((2, 16, tc), lambda a, b: (0, jnp.minimum((ij(a, b)[0] + 1) * (tm // 16), S // 16 - 1), ij(a, b)[1]))
    wsp = pl.BlockSpec((2, 3, tc), lambda a, b: (0, 0, ij(a, b)[1]))
    bsp = pl.BlockSpec((2, 1, tc), lambda a, b: (0, 0, ij(a, b)[1]))
    act = pl.BlockSpec((tm, tc), lambda a, b: ij(a, b))
    return blk, prev, nxt, wsp, bsp, act


def _ffn_up(i, up_ref, uh_ref, w_ref, b_ref):
    keep = jnp.where(i > 0, 1.0, 0.0)
    out = []
    for half in range(2):
        u = up_ref[half].astype(F32)
        hu = uh_ref[half].astype(F32) * keep
        w0, w1, w2 = (w_ref[half, k:k + 1, :] for k in range(3))
        out.append(_conv3(u, hu, w0, w1, w2) + b_ref[half])
    return out


def ffn_act_fwd(up_pre, w_fc, b_fc):
    S = up_pre.shape[1]

    def body(up_ref, uh_ref, w_ref, b_ref, a_ref, gv_ref):
        g, v = _ffn_up(pl.program_id(0), up_ref, uh_ref, w_ref, b_ref)
        a_ref[...] = (g * jax.nn.sigmoid(g) * v).astype(BF16)
        gv_ref[0] = g.astype(BF16)
        gv_ref[1] = v.astype(BF16)

    blk, prev, _, wsp, bsp, act = _ffn_specs(S, "ij")
    return pl.pallas_call(body, grid=(S // FFN_TM, D_FF // FFN_TC), name="ffn_act_fwd",
                          in_specs=[blk, prev, wsp, bsp], out_specs=[act, blk],
                          out_shape=[SDS((S, D_FF), BF16), SDS((2, S, D_FF), BF16)],
                          compiler_params=_cp("parallel", "parallel"))(up_pre, up_pre, w_fc, b_fc)


def up_act_fwd(h, w_up, w_fc, b_fc):
    S, K = h.shape
    tm, tc = FFN_TM, FFN_TC
    nj = D_FF // tc

    def body(h_ref, wg_ref, wv_ref, w_ref, b_ref, pre_ref, gv_ref, a_ref, halo):
        i = pl.program_id(1)
        hv = h_ref[...]
        res = []
        for half, wt_ref in enumerate((wg_ref, wv_ref)):
            u = _dot(hv, wt_ref[...])
            hu = jnp.where(i > 0, halo[half], 0.0)
            halo[half] = u[tm - 8:, :]
            w0, w1, w2 = (w_ref[half, k:k + 1, :] for k in range(3))
            pre_ref[half] = u.astype(BF16)
            res.append(_conv3(u, hu, w0, w1, w2) + b_ref[half])
        g, v = res
        gv_ref[0] = g.astype(BF16)
        gv_ref[1] = v.astype(BF16)
        a_ref[...] = (g * jax.nn.sigmoid(g) * v).astype(BF16)

    blk = pl.BlockSpec((2, tm, tc), lambda j, i: (0, i, j))
    return pl.pallas_call(
        body, grid=(nj, S // tm), name="up_act_fwd",
        in_specs=[pl.BlockSpec((tm, K), lambda j, i: (i, 0)), pl.BlockSpec((K, tc), lambda j, i: (0, j)),
                  pl.BlockSpec((K, tc), lambda j, i: (0, j + nj)), pl.BlockSpec((2, 3, tc), lambda j, i: (0, 0, j)),
                  pl.BlockSpec((2, 1, tc), lambda j, i: (0, 0, j))],
        out_specs=[blk, blk, pl.BlockSpec((tm, tc), lambda j, i: (i, j))],
        out_shape=[SDS((2, S, D_FF), BF16), SDS((2, S, D_FF), BF16), SDS((S, D_FF), BF16)],
        scratch_shapes=[pltpu.VMEM((2, 8, tc), F32)],
        compiler_params=_cp("parallel", "arbitrary"))(h, w_up, w_up, w_fc, b_fc)


def ffn_act_bwd(dx, w_down, up, up_pre, w_fc):
    S = up.shape[1]

    def act_grads(da, g, v):
        sg = jax.nn.sigmoid(g)
        return da * v * (sg * (1.0 + g * (1.0 - sg))), da * g * sg

    def body(dx_ref, dxn_ref, wd_ref, gv_ref, gvn_ref, up_ref, w_ref, dp_ref, dw_ref, db_ref):
        i = pl.program_id(1)
        keep_next = jnp.where(i < pl.num_programs(1) - 1, 1.0, 0.0)

        @pl.when(i == 0)
        def _():
            dw_ref[...] = jnp.zeros_like(dw_ref)
            db_ref[...] = jnp.zeros_like(db_ref)

        wd = wd_ref[...]
        da = _dot_nt(dx_ref[...].astype(BF16), wd)
        dan = _dot_nt(dxn_ref[...].astype(BF16), wd) * keep_next
        here = act_grads(da, gv_ref[0].astype(F32), gv_ref[1].astype(F32))
        after = act_grads(dan, gvn_ref[0].astype(F32), gvn_ref[1].astype(F32))
        for half in range(2):
            d, dn = here[half], after[half]
            u = up_ref[half].astype(F32)
            d1, d2 = _shift_up(d, 1, dn), _shift_up(d, 2, dn)
            w0, w1, w2 = (w_ref[half, k:k + 1, :] for k in range(3))
            dp_ref[half] = (d * w2 + d1 * w1 + d2 * w0).astype(BF16)
            dw_ref[half, 0:1, :] += _colsum(d2 * u)
            dw_ref[half, 1:2, :] += _colsum(d1 * u)
            dw_ref[half, 2:3, :] += _colsum(d * u)
            db_ref[half] += _colsum(d)

    tm, tc = FFN_TM, FFN_TC
    blk, _, nxt, wsp, bsp, _ = _ffn_specs(S, "ji")
    K = dx.shape[1]
    rows = pl.BlockSpec((tm, K), lambda j, i: (i, 0))
    rows_next = pl.BlockSpec((16, K), lambda j, i: (jnp.minimum((i + 1) * (tm // 16), S // 16 - 1), 0))
    return pl.pallas_call(body, grid=(D_FF // tc, S // tm), name="ffn_act_bwd",
                          in_specs=[rows, rows_next, pl.BlockSpec((tc, K), lambda j, i: (j, 0)), blk, nxt, blk, wsp],
                          out_specs=[blk, wsp, bsp],
                          out_shape=[SDS((2, S, D_FF), BF16), SDS((2, 3, D_FF), F32), SDS((2, 1, D_FF), F32)],
                          compiler_params=_cp("parallel", "arbitrary"))(dx, dx, w_down, up, up, up_pre, w_fc)


def local_step(x, mem, target, wb, ws, comm=None):
    S = x.shape[0]
    assert S % 2048 == 0
    if comm is None:
        comm = NoComm()
    else:
        wb = {}
    ws = dict(ws)

    bias = bias_tables(ws["rel_bias"])
    h1 = rmsnorm_fwd(x, ws["g_mix"], "norm_mix", job=comm.job("norm_mix"))
    comm.landed("norm_mix", wb, ws)
    w_fc = ws["w_ffn_conv"].reshape(3, 2, D_FF).transpose(1, 0, 2)
    b_fc = ws["b_ffn_conv"].reshape(2, 1, D_FF)
    proj = mm_nn(h1, wb["w_in"], "proj", out_dtype=BF16, out_bw=BLK, tn=768, job=comm.job("proj"))
    comm.landed("proj", wb)
    qkv = proj.reshape(2, 3, 4, S, BLK)
    o, lse = attn_fwd(qkv, bias, job=comm.job("attn_fwd"))
    comm.landed("attn_fwd", wb)
    proj4 = proj.reshape(2, 12, S, BLK)
    mixed = mixer_post_fwd(o, proj4, ws["w_short_conv"], ws["g_attn_out"], ws["g_conv_out"])
    x1, h2 = mm_nn_rows(mixed, wb["w_out"], x, ws["g_xattn"], "out_proj")
    mem_n = rmsnorm_fwd(mem, ws["g_mem"], "norm_mem")
    xq = mm_nn(h2, wb["w_xq"], "xq", out_dtype=BF16)
    xk = mm_nn(mem_n, wb["w_xk"], "xk", out_dtype=BF16, tn=1024)
    xv = mm_nn(mem_n, wb["w_xv"], "xv", out_dtype=BF16, tn=1024)
    xo = xattn_fwd(xq, xk, xv)
    x2, h3 = mm_nn_rows(xo, wb["w_xo"], x1, ws["g_ffn"], "xo_proj")
    up_pre, up, act = up_act_fwd(h3, wb["w_up"], w_fc, b_fc)
    loss, dx3, dg_final = mm_nn_rows(act, wb["w_down"], x2, ws["g_final"].reshape(1, -1), "down_proj", target=target)

    gb, gs = {}, {"g_final": dg_final}
    gb["w_down"] = mm_tn(act, dx3, None, "dw_down", tm=1408, tn=1024)
    d_up_pre, dw_fc, db_fc = ffn_act_bwd(dx3, wb["w_down"], up, up_pre, w_fc)
    gs["b_ffn_conv"] = db_fc.reshape(1, 2 * D_FF)
    gs["w_ffn_conv"] = dw_fc.transpose(1, 0, 2).reshape(3, 2 * D_FF)
    gb["w_up"] = mm_tn(h3, d_up_pre, D_FF, "dw_up", shards=4, tn=1408)
    dx2, gs["g_ffn"] = mm_nt_norm_bwd(d_up_pre, D_FF, wb["w_up"], x2, ws["g_ffn"], dx3, "d_h3", tm=256,
                                      job=comm.job("d_h3", gb))
    comm.landed("d_h3")
    gb["w_xo"] = mm_tn(xo, dx2, None, "dw_xo")
    dxo = mm_nt(dx2, None, wb["w_xo"], "d_xo", out_dtype=BF16)
    dxq, dxk, dxv = xattn_bwd(xq, xk, xv, dxo)
    gb["w_xq"] = mm_tn(h2, dxq, None, "dw_xq")
    gb["w_xk"] = mm_tn(mem_n, dxk, None, "dw_xk", tn=1024)
    gb["w_xv"] = mm_tn(mem_n, dxv, None, "dw_xv", tn=1024)
    dmem_n = mm_nt(dxk, None, wb["w_xk"], "d_memk", tn=1024)
    dmem_n = mm_nt(dxv, None, wb["w_xv"], "d_memv", res=dmem_n, tn=1024)
    gs["g_mem"] = rmsnorm_bwd(mem, ws["g_mem"], dmem_n, None, "norm_mem_bwd")
    dx1, gs["g_xattn"] = mm_nt_norm_bwd(dxq, None, wb["w_xq"], x1, ws["g_xattn"], dx2, "d_h2")
    gb["w_out"] = mm_tn(mixed, dx1, None, "dw_out")
    dmixed = mm_nt(dx1, None, wb["w_out"], "d_mixed")
    do, dgb, dcv, gs["g_attn_out"], gs["g_conv_out"] = mixer_post_bwd_a(
        dmixed, o, proj4, ws["w_short_conv"], ws["g_attn_out"], ws["g_conv_out"], job=comm.job("mixer_post_bwd_a", gb))
    comm.landed("mixer_post_bwd_a")
    dproj, ds_sum = attn_bwd(qkv, bias, o, lse, do, job=comm.job("attn_bwd"))
    comm.landed("attn_bwd")
    gs["rel_bias"] = bias_tables_bwd(ds_sum)
    dproj, gs["w_short_conv"] = mixer_post_bwd_b(dproj.reshape(2, 12, S, BLK), dgb, dcv, proj4, ws["w_short_conv"])
    dproj = dproj.reshape(24, S, BLK)
    gb["w_in"] = mm_tn(h1, dproj, BLK, "dw_in", shards=4, tn=768, job=comm.job("dw_in"))
    comm.landed("dw_in")
    comm.first_update(gb)
    grad_x, gs["g_mix"] = mm_nt_norm_bwd(dproj, BLK, wb["w_in"], x, ws["g_mix"], dx1, "d_h1", tm=256,
                                         job=comm.job("d_h1", gb))
    comm.landed("d_h1")
    return loss, grad_x, comm.finish(gb), gs


def adamw(w, g, m, v, name, job=None):
    R, C = w.shape
    tr = R
    for cand in (256, 352):
        if R > cand and R % cand == 0:
            tr = cand
            break

    def body(w_ref, g_ref, m_ref, v_ref, d_ref, nm_ref, nv_ref):
        gv = g_ref[...]
        mn = ADAM_B1 * m_ref[...] + (1.0 - ADAM_B1) * gv
        vn = ADAM_B2 * v_ref[...] + (1.0 - ADAM_B2) * (gv * gv)
        m_hat = mn / (1.0 - ADAM_B1 ** ADAM_STEP)
        v_hat = vn / (1.0 - ADAM_B2 ** ADAM_STEP)
        d_ref[...] = -ADAM_LR * (m_hat / (jnp.sqrt(v_hat) + ADAM_EPS) + ADAM_WD * w_ref[...])
        nm_ref[...] = mn
        nv_ref[...] = vn

    blk = pl.BlockSpec((tr, C), lambda i: (i, 0))
    return _pcall(body, [w, g, m, v], grid=(R // tr,), name=name, in_specs=[blk] * 4, out_specs=[blk] * 3,
                  out_shape=[SDS((R, C), F32)] * 3, sem=("parallel",), job=job)


BIG = ("w_in", "w_out", "w_xq", "w_xk", "w_xv", "w_xo", "w_up", "w_down")
COL_SHARDED = ("w_in", "w_up")
N_BIG = len(BIG)
ANY = pl.BlockSpec(memory_space=pl.ANY)


def _place():
    x, y, c = lax.axis_index("x"), lax.axis_index("y"), lax.axis_index("c")
    chips = [(1 - x, y), (x, 1 - y), (1 - x, 1 - y)]
    return x, y, c, chips


def _window(full, name, R, C, shard, half):
    r0, nr = (0, R) if half is None else (half * (R // 2), R // 2)
    if name in COL_SHARDED:
        return full.at[pl.ds(r0, nr), pl.ds(shard * C, C)]
    return full.at[pl.ds(shard * R + r0, nr), :]


def place_shard(w, mine_arr, n):
    R, C = w.shape
    col = n in COL_SHARDED

    def body(s_ref, w_ref, o_ref):
        o_ref[...] = w_ref[...].astype(BF16)

    grid_spec = pltpu.PrefetchScalarGridSpec(
        num_scalar_prefetch=1, grid=(1,), in_specs=[pl.BlockSpec((R, C), lambda i, s: (0, 0))],
        out_specs=pl.BlockSpec((R, C), (lambda i, s: (0, s[0])) if col else (lambda i, s: (s[0], 0))))
    return pl.pallas_call(body, grid_spec=grid_spec, name="place_" + n,
                          out_shape=SDS((R, 4 * C) if col else (4 * R, C), BF16),
                          compiler_params=_cp("arbitrary"))(mine_arr, w)


def _gather_jobs(names, shapes):
    nw = len(names)

    def start(full, sems):
        send, recv, fsend, frecv = sems
        x, y, c, chips = _place()
        mine = 2 * x + y
        for w, n in enumerate(names):
            R, C = shapes[w]
            own = _window(full[w], n, R, C, mine, c)
            for j, chip in enumerate(chips):
                pltpu.make_async_remote_copy(src_ref=own, dst_ref=own, send_sem=send.at[w, j], recv_sem=recv.at[w, j],
                                             device_id=(*chip, c), device_id_type=MESH).start()

    def mid(full, sems, lo=0, hi=nw):
        send, recv, fsend, frecv = sems
        x, y, c, chips = _place()
        sib = (x, y, 1 - c)
        for w, n in list(enumerate(names))[lo:hi]:
            R, C = shapes[w]
            for j, chip in enumerate(chips):
                landed = _window(full[w], n, R, C, 2 * chip[0] + chip[1], c)
                pltpu.make_async_remote_copy(src_ref=landed, dst_ref=landed, send_sem=send.at[w, j], recv_sem=recv.at[w, j],
                                             device_id=(*chip, c), device_id_type=MESH).wait_recv()
                pltpu.make_async_remote_copy(src_ref=landed, dst_ref=landed, send_sem=fsend.at[w, j],
                                             recv_sem=frecv.at[w, j], device_id=sib, device_id_type=MESH).start()

    def finish(full, sems):
        send, recv, fsend, frecv = sems
        x, y, c, chips = _place()
        mine = 2 * x + y
        sib = (x, y, 1 - c)
        for w, n in enumerate(names):
            R, C = shapes[w]
            own = _window(full[w], n, R, C, mine, c)
            for j, chip in enumerate(chips):
                landed = _window(full[w], n, R, C, 2 * chip[0] + chip[1], c)
                other = _window(full[w], n, R, C, 2 * chip[0] + chip[1], 1 - c)
                pltpu.make_async_remote_copy(src_ref=other, dst_ref=other, send_sem=fsend.at[w, j], recv_sem=frecv.at[w, j],
                                             device_id=sib, device_id_type=MESH).wait_recv()
                pltpu.make_async_remote_copy(src_ref=own, dst_ref=own, send_sem=send.at[w, j], recv_sem=recv.at[w, j],
                                             device_id=(*chip, c), device_id_type=MESH).wait_send()
                pltpu.make_async_remote_copy(src_ref=landed, dst_ref=landed, send_sem=fsend.at[w, j],
                                             recv_sem=frecv.at[w, j], device_id=sib, device_id_type=MESH).wait_send()

    return start, mid, finish, [pltpu.SemaphoreType.DMA((nw, 3))] * 4


class CommJob:
    def __init__(self, ins, out_shapes, inplace, start, finish, sems, mid=None):
        self.ins, self.out_shapes, self.inplace = list(ins), list(out_shapes), inplace
        self.start, self.mid, self.finish, self.sems = start, mid, finish, list(sems)

    def run(self, phase, ins, outs, sems):
        if phase == 0:
            self.start(ins, outs, sems)
        elif phase == 1:
            if self.mid is not None:
                self.mid(ins, outs, sems)
        else:
            self.finish(ins, outs, sems)


def run_job(name, job):
    n_in, n_out = len(job.ins), len(job.out_shapes)

    def body(*refs):
        ins, outs, sm = refs[:n_in], refs[n_in:n_in + n_out], refs[n_in + n_out:]
        for phase in range(3):
            job.run(phase, ins, outs, sm)

    job.landed = list(pl.pallas_call(
        body, name=name, in_specs=[ANY] * n_in, out_specs=[ANY] * n_out, out_shape=job.out_shapes,
        input_output_aliases={i: i for i in range(n_in)} if job.inplace else {},
        scratch_shapes=job.sems)(*job.ins))
    return job.landed


class Carried:
    def __init__(self, job, n_in, n_out, n_scratch, steps):
        self.job, self.n_in, self.n_out, self.n_scratch, self.steps = job, n_in, n_out, n_scratch, steps
        self.nji = len(job.ins) if job else 0
        self.njo = len(job.out_shapes) if job else 0

    def in_specs(self):
        return [ANY] * self.nji

    def out_specs(self):
        return [ANY] * self.njo

    def out_shapes(self):
        return list(self.job.out_shapes) if self.job else []

    def aliases(self):
        if not (self.job and self.job.inplace):
            return {}
        return {self.n_in + i: self.n_out + i for i in range(self.nji)}

    def sems(self):
        return list(self.job.sems) if self.job else []

    def args(self):
        return list(self.job.ins) if self.job else []

    def split(self, refs):
        a = self.n_in
        b = a + self.nji
        c = b + self.n_out
        d = c + self.njo
        e = d + self.n_scratch
        return refs[:a] + refs[b:c] + refs[d:e], (refs[a:b], refs[c:d], refs[e:])

    def phase(self, phase, step, jrefs):
        if self.job is None:
            return
        at = {0: 0, 1: max(self.steps - 2, 0), 2: self.steps - 1}[phase]

        @pl.when(step == at)
        def _():
            self.job.run(phase, *jrefs)


def _pcall(body, args, *, grid, in_specs, out_specs, out_shape, name, sem, scratch=(), aliases=None, job=None):
    n_in, n_out = len(args), len(out_shape)
    if job is None:
        return pl.pallas_call(
            body, grid=grid, in_specs=list(in_specs), out_specs=list(out_specs), out_shape=list(out_shape), name=name,
            scratch_shapes=list(scratch), input_output_aliases=dict(aliases or {}), compiler_params=_cp(*sem))(*args)
    total = int(np.prod(grid))
    car = Carried(job, n_in, n_out, len(scratch), total)

    def wrapped(*refs):
        main, jrefs = car.split(refs)
        lin = pl.program_id(0)
        for ax in range(1, len(grid)):
            lin = lin * grid[ax] + pl.program_id(ax)
        car.phase(0, lin, jrefs)
        body(*main)
        car.phase(1, lin, jrefs)
        car.phase(2, lin, jrefs)

    res = pl.pallas_call(
        wrapped, grid=grid, in_specs=list(in_specs) + car.in_specs(), out_specs=list(out_specs) + car.out_specs(),
        out_shape=list(out_shape) + car.out_shapes(), name=name, scratch_shapes=list(scratch) + car.sems(),
        input_output_aliases={**dict(aliases or {}), **car.aliases()},
        compiler_params=_cp(*(["arbitrary"] * len(grid))))(*args, *car.args())
    job.landed = list(res[n_out:])
    return list(res[:n_out])


def gather_job(placed, names):
    shapes = []
    for n in names:
        R, C = placed[n].shape
        shapes.append((R, C // 4) if n in COL_SHARDED else (R // 4, C))
    start, mid, finish, sems = _gather_jobs(names, shapes)
    arrays = [placed[n] for n in names]
    early = max(len(names) - 1, 1)

    def last(i, o, s):
        mid(o, s, early, len(names))
        finish(o, s)

    return CommJob(arrays, [SDS(a.shape, a.dtype) for a in arrays], True,
                   lambda i, o, s: start(o, s), last, sems, mid=lambda i, o, s: mid(o, s, 0, early))


def gather_rows_job(placed):
    widths = [a.shape[1] // 4 for a in placed]

    def copies(outs, sems):
        x, y, c, chips = _place()
        mine = 2 * x + y
        cps = []
        for w, ref in enumerate(outs):
            own = ref.at[:, pl.ds(mine * widths[w], widths[w])]
            for j, chip in enumerate(chips):
                theirs = ref.at[:, pl.ds((2 * chip[0] + chip[1]) * widths[w], widths[w])]
                kw = dict(send_sem=sems[0].at[w, j], recv_sem=sems[1].at[w, j], device_id=(*chip, c), device_id_type=MESH)
                cps.append((pltpu.make_async_remote_copy(src_ref=own, dst_ref=own, **kw),
                            pltpu.make_async_remote_copy(src_ref=theirs, dst_ref=theirs, **kw)))
        return cps

    def start(ins, outs, sems):
        for send, _ in copies(outs, sems):
            send.start()

    def finish(ins, outs, sems):
        for send, recv in copies(outs, sems):
            recv.wait_recv()
            send.wait_send()

    return CommJob(placed, [SDS(a.shape, a.dtype) for a in placed], True, start, finish,
                   [pltpu.SemaphoreType.DMA((len(placed), 3))] * 2)


def merge_jobs(a, b):
    assert a.inplace and b.inplace
    na, nb_, nsa = len(a.ins), len(b.ins), len(a.sems)

    def phase(which):
        def run(ins, outs, sems):
            for job, i, o, s in ((a, ins[:na], outs[:na], sems[:nsa]), (b, ins[na:], outs[na:], sems[nsa:])):
                fn = getattr(job, which)
                if fn is not None:
                    fn(i, o, s)
        return run

    merged = CommJob(a.ins + b.ins, a.out_shapes + b.out_shapes, True, phase("start"), phase("finish"), a.sems + b.sems,
                     mid=phase("mid"))
    merged.parts = (a, b, na)
    return merged


def split_landed(merged):
    a, b, na = merged.parts
    a.landed, b.landed = merged.landed[:na], merged.landed[na:]


def pair_exchange_job(grads, names):
    shapes = [grads[n].shape for n in names]

    def copies(ins, outs, sems):
        x, y, c, _ = _place()
        return [pltpu.make_async_remote_copy(
            src_ref=ins[w].at[:, pl.ds((1 - c) * (shapes[w][1] // 2), shapes[w][1] // 2), :], dst_ref=outs[w],
            send_sem=sems[0].at[w], recv_sem=sems[1].at[w], device_id=(x, y, 1 - c), device_id_type=MESH)
            for w in range(len(names))]

    def start(ins, outs, sems):
        for cp in copies(ins, outs, sems):
            cp.start()

    def finish(ins, outs, sems):
        for cp in copies(ins, outs, sems):
            cp.wait()

    return CommJob([grads[n] for n in names], [SDS((4, s[1] // 2, s[2]), F32) for s in shapes], False, start, finish,
                   [pltpu.SemaphoreType.DMA((len(names),))] * 2)


def chip_exchange_job(p16, names):
    shapes = [p16[n].shape for n in names]

    def copies(ins, outs, sems):
        x, y, c, chips = _place()
        return [pltpu.make_async_remote_copy(
            src_ref=ins[w].at[2 * chip[0] + chip[1]], dst_ref=outs[w].at[j],
            send_sem=sems[0].at[w, j], recv_sem=sems[1].at[w, j], device_id=(*chip, c), device_id_type=MESH)
            for w in range(len(names)) for j, chip in enumerate(chips)]

    def start(ins, outs, sems):
        for cp in copies(ins, outs, sems):
            cp.start()

    def finish(ins, outs, sems):
        for cp in copies(ins, outs, sems):
            cp.wait()

    return CommJob([p16[n] for n in names], [SDS((3,) + tuple(s[1:]), BF16) for s in shapes], False, start, finish,
                   [pltpu.SemaphoreType.DMA((len(names), 3))] * 2)


def half_exchange_job(full, names):
    shapes = [full[n].shape for n in names]

    def copies(outs, sems):
        x, y, c, _ = _place()
        cps = []
        for w in range(len(names)):
            Rh = shapes[w][0] // 2
            rows = outs[w].at[pl.ds(c * Rh, Rh), :]
            other = outs[w].at[pl.ds((1 - c) * Rh, Rh), :]
            cps.append((pltpu.make_async_remote_copy(src_ref=rows, dst_ref=rows, send_sem=sems[0].at[w], recv_sem=sems[1].at[w],
                                                     device_id=(x, y, 1 - c), device_id_type=MESH),
                        pltpu.make_async_remote_copy(src_ref=other, dst_ref=other, send_sem=sems[0].at[w], recv_sem=sems[1].at[w],
                                                     device_id=(x, y, 1 - c), device_id_type=MESH)))
        return cps

    def start(ins, outs, sems):
        for send, _ in copies(outs, sems):
            send.start()

    def finish(ins, outs, sems):
        for send, recv in copies(outs, sems):
            recv.wait_recv()
            send.wait_send()

    arrays = [full[n] for n in names]
    return CommJob(arrays, [SDS(a.shape, a.dtype) for a in arrays], True, start, finish,
                   [pltpu.SemaphoreType.DMA((len(names),))] * 2)


def allreduce_small(parts, name, job=None):
    n = len(parts)
    nji = len(job.ins) if job else 0
    njo = len(job.out_shapes) if job else 0

    def body(*refs):
        ins, jins = refs[:n], refs[n:n + nji]
        outs, jouts = refs[n + nji:2 * n + nji], refs[2 * n + nji:2 * n + nji + njo]
        bufs = refs[2 * n + nji + njo:3 * n + nji + njo]
        send, recv = refs[3 * n + nji + njo:3 * n + nji + njo + 2]
        jsems = refs[3 * n + nji + njo + 2:]
        x, y, c, _ = _place()
        me = 4 * x + 2 * y + c
        if job is not None:
            job.run(0, jins, jouts, jsems)
        sends = []
        for p in range(n):
            for k in range(1, 8):
                peer = (x ^ (k >> 2), y ^ ((k >> 1) & 1), c ^ (k & 1))
                cp = pltpu.make_async_remote_copy(src_ref=ins[p], dst_ref=bufs[p].at[k], send_sem=send.at[p, k - 1],
                                                  recv_sem=recv.at[p, k - 1], device_id=peer, device_id_type=MESH)
                cp.start()
                sends.append(cp)
            bufs[p][0] = ins[p][...]
        for cp in sends:
            cp.wait_recv()
        for p in range(n):
            acc = bufs[p][me]
            for d in range(1, 8):
                acc = acc + bufs[p][d ^ me]
            outs[p][...] = acc
        for cp in sends:
            cp.wait_send()
        if job is not None:
            job.run(1, jins, jouts, jsems)
            job.run(2, jins, jouts, jsems)

    vm = pl.BlockSpec(memory_space=pltpu.VMEM)
    res = pl.pallas_call(
        body, name=name, in_specs=[vm] * n + [ANY] * nji, out_specs=[vm] * n + [ANY] * njo,
        out_shape=[SDS(a.shape, F32) for a in parts] + (list(job.out_shapes) if job else []),
        input_output_aliases={n + i: n + i for i in range(nji)} if (job and job.inplace) else {},
        scratch_shapes=[pltpu.VMEM((8,) + tuple(a.shape), F32) for a in parts] + [pltpu.SemaphoreType.DMA((n, 7))] * 2
        + (list(job.sems) if job else []),
    )(*parts, *(job.ins if job else []))
    if job is not None:
        job.landed = list(res[n:])
    return list(res[:n])


def adamw_multi(ws, gs, ms, vs, name):
    n = len(ws)

    def body(*refs):
        for p in range(n):
            w_ref, g_ref, m_ref, v_ref = (refs[q * n + p] for q in range(4))
            d_ref, nm_ref, nv_ref = (refs[(4 + q) * n + p] for q in range(3))
            gv = g_ref[...]
            mn = ADAM_B1 * m_ref[...] + (1.0 - ADAM_B1) * gv
            vn = ADAM_B2 * v_ref[...] + (1.0 - ADAM_B2) * (gv * gv)
            m_hat = mn / (1.0 - ADAM_B1 ** ADAM_STEP)
            v_hat = vn / (1.0 - ADAM_B2 ** ADAM_STEP)
            d_ref[...] = -ADAM_LR * (m_hat / (jnp.sqrt(v_hat) + ADAM_EPS) + ADAM_WD * w_ref[...])
            nm_ref[...] = mn
            nv_ref[...] = vn

    vm = pl.BlockSpec(memory_space=pltpu.VMEM)
    outs = pl.pallas_call(body, name=name, in_specs=[vm] * (4 * n), out_specs=[vm] * (3 * n),
                          out_shape=[SDS(a.shape, F32) for a in ws] * 3)(*ws, *gs, *ms, *vs)
    return outs[:n], outs[n:2 * n], outs[2 * n:]


def pair_sum(g, xh, sel_arr, name):
    _, R, C = g.shape
    Rh = R // 2

    def body(sel_ref, g_ref, x_ref, p32_ref, p16_ref):
        s = g_ref[...] + x_ref[...]
        p16_ref[...] = s.astype(BF16)

        @pl.when(pl.program_id(0) == sel_ref[0])
        def _():
            p32_ref[...] = s

    same = pl.BlockSpec((None, Rh, C), lambda s, sel: (s, 0, 0))
    grid_spec = pltpu.PrefetchScalarGridSpec(
        num_scalar_prefetch=1, grid=(4,),
        in_specs=[pl.BlockSpec((None, Rh, C), lambda s, sel: (s, sel[1], 0)), same],
        out_specs=[pl.BlockSpec((Rh, C), lambda s, sel: (0, 0)), same])
    return pl.pallas_call(body, grid_spec=grid_spec, name=name, out_shape=[SDS((Rh, C), F32), SDS((4, Rh, C), BF16)],
                          compiler_params=_cp("arbitrary"))(sel_arr, g, xh)


def chip_sum(p32, y, sel_arr, name):
    Rh, C = p32.shape

    def body(s_ref, p_ref, y_ref, o_ref):
        o_ref[...] = ((p_ref[...] + y_ref[0].astype(F32)) + y_ref[1].astype(F32)) + y_ref[2].astype(F32)

    grid_spec = pltpu.PrefetchScalarGridSpec(
        num_scalar_prefetch=1, grid=(1,),
        in_specs=[pl.BlockSpec((Rh, C), lambda i, s: (0, 0)), pl.BlockSpec((3, Rh, C), lambda i, s: (0, 0, 0))],
        out_specs=pl.BlockSpec((Rh, C), lambda i, s: (s[1], 0)))
    return pl.pallas_call(body, grid_spec=grid_spec, name=name, out_shape=SDS((2 * Rh, C), F32),
                          compiler_params=_cp("arbitrary"))(sel_arr, p32, y)


class StepComm:
    FIRST = ("w_in",)
    SQUARE = ("w_out", "w_xq", "w_xk", "w_xv", "w_xo")
    FFN = ("w_up", "w_down")
    PLAN = {
        "proj": ("gather", SQUARE[:3]), "attn_fwd": ("gather", SQUARE[3:] + FFN),
        "d_h3": ("pair", FFN), "mixer_post_bwd_a": ("pair", SQUARE), "attn_bwd": ("chip", FFN + SQUARE),
        "dw_in": ("half", FFN + SQUARE), "first_update": ("pair", FIRST), "d_h1": ("chip", FIRST),
    }

    def __init__(self, placed, conv_placed, sel_arr):
        self.placed, self.conv_placed, self.sel_arr = placed, conv_placed, sel_arr
        self.g, self.p32, self.p16, self.full, self.done, self.jobs = {}, {}, {}, {}, {}, {}

    def job(self, tag, gb=None):
        if tag == "norm_mix":
            self.jobs[tag] = merge_jobs(gather_job(self.placed, self.FIRST), gather_rows_job(self.conv_placed))
            return self.jobs[tag]
        kind, names = self.PLAN[tag]
        if kind == "gather":
            j = gather_job(self.placed, names)
        elif kind == "pair":
            for n in names:
                self.g[n] = gb[n] if gb[n].ndim == 3 else gb[n].reshape(4, gb[n].shape[0] // 4, gb[n].shape[1])
            j = pair_exchange_job(self.g, names)
        elif kind == "chip":
            j = chip_exchange_job(self.p16, names)
        else:
            j = half_exchange_job(self.full, names)
        self.jobs[tag] = j
        return j

    def landed(self, tag, wb=None, ws=None):
        if tag == "norm_mix":
            split_landed(self.jobs[tag])
            first, conv, _ = self.jobs[tag].parts
            wb.update(zip(self.FIRST, first.landed))
            ws.update((n, a[:3]) for n, a in zip(SMALL_SHARDED, conv.landed))
            return
        kind, names = self.PLAN[tag]
        for n, a in zip(names, self.jobs[tag].landed):
            if kind == "gather":
                wb[n] = a
            elif kind == "pair":
                self.p32[n], self.p16[n] = pair_sum(self.g[n], a, self.sel_arr, "pair_sum_" + n)
            elif kind == "chip":
                self.full[n] = chip_sum(self.p32[n], a, self.sel_arr, "chip_sum_" + n)
            else:
                self.done[n] = a

    def first_update(self, gb):
        self.first_result = self.update_fn(self.job("first_update", gb))
        self.landed("first_update")

    def finish(self, gb):
        return self.done

    def last_job(self):
        self.jobs["end_half"] = half_exchange_job(self.full, self.FIRST)
        return self.jobs["end_half"]

    def take_last(self):
        self.done.update(zip(self.FIRST, self.jobs["end_half"].landed))


class NoComm:
    def job(self, tag, gb=None):
        return None

    def landed(self, tag, wb=None, ws=None):
        pass

    def first_update(self, gb):
        pass

    def finish(self, gb):
        return gb


SMALL = ("rel_bias", "g_mix", "w_short_conv", "g_attn_out", "g_conv_out", "g_xattn", "g_mem", "g_ffn",
         "w_ffn_conv", "b_ffn_conv", "g_final")
SMALL_FULL = {"rel_bias": (8, 32), "g_mix": (1, 1024), "w_short_conv": (3, 512), "g_attn_out": (1, 512), "g_conv_out": (1, 512),
              "g_xattn": (1, 1024), "g_mem": (1, 1024), "g_ffn": (1, 1024), "w_ffn_conv": (3, 5632), "b_ffn_conv": (1, 5632),
              "g_final": (1024,)}
SMALL_SHARDED = ("w_short_conv", "w_ffn_conv")


def _pack(parts):
    flat = jnp.concatenate([p.reshape(-1) for p in parts])
    rows = -(-flat.shape[0] // 1024) * 8
    return jnp.pad(flat, (0, rows * 128 - flat.shape[0])).reshape(rows, 128)


def _unpack(packed, names):
    flat, out, off = packed.reshape(-1), {}, 0
    for n in names:
        size = int(np.prod(SMALL_FULL[n]))
        out[n] = flat[off:off + size].reshape(SMALL_FULL[n])
        off += size
    return out


def kernel(x, mem, rel_bias, g_mix, w_in, w_short_conv, g_attn_out, g_conv_out, w_out, g_xattn, g_mem, w_xq, w_xk, w_xv, w_xo, g_ffn, w_up, w_ffn_conv, b_ffn_conv, w_down, g_final, loss_target, m_rel_bias, m_g_mix, m_w_in, m_w_short_conv, m_g_attn_out, m_g_conv_out, m_w_out, m_g_xattn, m_g_mem, m_w_xq, m_w_xk, m_w_xv, m_w_xo, m_g_ffn, m_w_up, m_w_ffn_conv, m_b_ffn_conv, m_w_down, m_g_final, v_rel_bias, v_g_mix, v_w_in, v_w_short_conv, v_g_attn_out, v_g_conv_out, v_w_out, v_g_xattn, v_g_mem, v_w_xq, v_w_xk, v_w_xv, v_w_xo, v_g_ffn, v_w_up, v_w_ffn_conv, v_b_ffn_conv, v_w_down, v_g_final):
    names = ("rel_bias", "g_mix", "w_in", "w_short_conv", "g_attn_out", "g_conv_out", "w_out", "g_xattn", "g_mem", "w_xq",
             "w_xk", "w_xv", "w_xo", "g_ffn", "w_up", "w_ffn_conv", "b_ffn_conv", "w_down", "g_final")
    W = dict(zip(names, (rel_bias, g_mix, w_in, w_short_conv, g_attn_out, g_conv_out, w_out, g_xattn, g_mem, w_xq, w_xk, w_xv,
                         w_xo, g_ffn, w_up, w_ffn_conv, b_ffn_conv, w_down, g_final)))
    M = dict(zip(names, (m_rel_bias, m_g_mix, m_w_in, m_w_short_conv, m_g_attn_out, m_g_conv_out, m_w_out, m_g_xattn, m_g_mem,
                         m_w_xq, m_w_xk, m_w_xv, m_w_xo, m_g_ffn, m_w_up, m_w_ffn_conv, m_b_ffn_conv, m_w_down, m_g_final)))
    V = dict(zip(names, (v_rel_bias, v_g_mix, v_w_in, v_w_short_conv, v_g_attn_out, v_g_conv_out, v_w_out, v_g_xattn, v_g_mem,
                         v_w_xq, v_w_xk, v_w_xv, v_w_xo, v_g_ffn, v_w_up, v_w_ffn_conv, v_b_ffn_conv, v_w_down, v_g_final)))
    xi, yi, ci = lax.axis_index("x"), lax.axis_index("y"), lax.axis_index("c")
    mine = 2 * xi + yi
    mine_arr = jnp.reshape(mine, (1,)).astype(jnp.int32)
    sel_arr = jnp.stack([mine, ci]).astype(jnp.int32)
    conv_placed = []
    for n in SMALL_SHARDED:
        shard = W[n][0]
        conv_placed.append(lax.dynamic_update_slice(jnp.zeros((8, SMALL_FULL[n][1]), F32), shard, (0, mine * shard.shape[1])))
    comm = StepComm({n: place_shard(W[n][0], mine_arr, n) for n in BIG}, conv_placed, sel_arr)
    ws = {n: W[n] for n in SMALL if n not in SMALL_SHARDED}

    comm.update_fn = lambda job: adamw(W["w_up"][0], comm.done["w_up"], M["w_up"][0], V["w_up"][0], "adamw_w_up", job=job)
    loss, grad_x, gfull, gs = local_step(x[0], mem[0], loss_target[0], None, ws, comm)

    def as2d(a):
        return a.reshape(1, -1) if a.ndim == 1 else a

    reduced = allreduce_small([as2d(gs[n]) for n in SMALL] + [loss], "reduce_small", job=comm.last_job())
    comm.take_last()
    gsm = dict(zip(SMALL, reduced[:-1]))
    loss = reduced[-1][0, 0]

    grads, delta, new_m, new_v = {}, {}, {}, {}
    for n in BIG:
        d, nm, nv = comm.first_result if n == "w_up" else adamw(W[n][0], gfull[n], M[n][0], V[n][0], "adamw_" + n)
        grads[n], delta[n], new_m[n], new_v[n] = gfull[n][None], d[None], nm[None], nv[None]
    for n in SMALL_SHARDED:
        wid = W[n].shape[2]
        gsm[n] = lax.dynamic_slice(gsm[n], (0, mine * wid), (3, wid))

    def own(a, n):
        return a[0] if n in SMALL_SHARDED else as2d(a)

    d, nm, nv = adamw_multi([own(W[n], n) for n in SMALL], [gsm[n] for n in SMALL], [own(M[n], n) for n in SMALL],
                            [own(V[n], n) for n in SMALL], "adamw_small")
    for i, n in enumerate(SMALL):
        shape = W[n].shape
        grads[n], delta[n], new_m[n], new_v[n] = (a.reshape(shape) for a in (gsm[n], d[i], nm[i], nv[i]))
    return (loss, grad_x[None], *[grads[n] for n in names], *[delta[n] for n in names],
            *[new_m[n] for n in names], *[new_v[n] for n in names])
```

```python
import functools
import math

import numpy as np
import jax
import jax.numpy as jnp
from jax import lax
from jax.experimental import pallas as pl
from jax.experimental.pallas import tpu as pltpu

F32 = jnp.float32
BF16 = jnp.bfloat16
SDS = jax.ShapeDtypeStruct
MESH = pl.DeviceIdType.MESH

D_MODEL = 1024
ATTN_W = 512
N_HEADS = 8
BLK = 128
PATTERNS = ((128, 1), (512, 4), (2048, 16))
N_BUCKETS = 32
D_FF = 2816
N_MEM_HEADS = 4
MEM_HD = 256
EPS = 1e-6
NEG = -1e30
VMEM_LIMIT = 56 * 1024 * 1024

ADAM_LR, ADAM_B1, ADAM_B2, ADAM_EPS, ADAM_WD, ADAM_STEP = 0.001, 0.9, 0.999, 1e-08, 0.01, 10


def _cp(*sem):
    return pltpu.CompilerParams(dimension_semantics=sem, vmem_limit_bytes=VMEM_LIMIT)


def _dot(a, b):
    return jnp.dot(a, b, preferred_element_type=F32)


def _dot_nt(a, b):
    return lax.dot_general(a, b, (((1,), (1,)), ((), ())), preferred_element_type=F32)


def _dot_tn(a, b):
    return lax.dot_general(a, b, (((0,), (0,)), ((), ())), preferred_element_type=F32)


def _rsum(x):
    return jnp.sum(x, axis=1, keepdims=True)


def rmsnorm_fwd(x, g, name, job=None):
    S, Dm = x.shape
    tm = min(S, 512)

    def body(x_ref, g_ref, o_ref):
        xv = x_ref[...]
        r = lax.rsqrt(jnp.mean(xv * xv, axis=1, keepdims=True) + EPS)
        o_ref[...] = (xv * r * g_ref[...]).astype(o_ref.dtype)

    return _pcall(body, [x, g], grid=(S // tm,), name=name,
                  in_specs=[pl.BlockSpec((tm, Dm), lambda i: (i, 0)), pl.BlockSpec((1, Dm), lambda i: (0, 0))],
                  out_specs=[pl.BlockSpec((tm, Dm), lambda i: (i, 0))], out_shape=[SDS((S, Dm), BF16)],
                  sem=("parallel",), job=job)[0]


def rmsnorm_bwd(x, g, dh, res, name, job=None):
    S, Dm = x.shape
    tm = min(S, 512)
    want_dx = res is not None

    def body(*refs):
        if want_dx:
            x_ref, g_ref, dh_ref, res_ref, dx_ref, dg_ref = refs
        else:
            x_ref, g_ref, dh_ref, dg_ref = refs
        i = pl.program_id(0)
        xv = x_ref[...]
        r = lax.rsqrt(jnp.mean(xv * xv, axis=1, keepdims=True) + EPS)
        xh = xv * r
        dh = dh_ref[...].astype(F32)
        if want_dx:
            gd = dh * g_ref[...]
            dx_ref[...] = res_ref[...] + r * (gd - xh * jnp.mean(gd * xh, axis=1, keepdims=True))

        @pl.when(i == 0)
        def _():
            dg_ref[...] = jnp.zeros_like(dg_ref)

        dg_ref[...] += jnp.sum(dh * xh, axis=0, keepdims=True)

    row = pl.BlockSpec((tm, Dm), lambda i: (i, 0))
    vec = pl.BlockSpec((1, Dm), lambda i: (0, 0))
    if want_dx:
        return _pcall(body, [x, g, dh, res], grid=(S // tm,), name=name, in_specs=[row, vec, row, row], out_specs=[row, vec],
                      out_shape=[SDS((S, Dm), F32), SDS((1, Dm), F32)], sem=("arbitrary",), job=job)
    return pl.pallas_call(
        body, grid=(S // tm,), name=name, in_specs=[row, vec, row], out_specs=vec,
        out_shape=SDS((1, Dm), F32), compiler_params=_cp("arbitrary"))(x, g, dh)


def _col_spec(bw, tn, rows, row_of, col_of):
    if bw is None:
        return pl.BlockSpec((rows, tn), lambda *g: (row_of(*g), col_of(*g)))
    if tn % bw == 0:
        return pl.BlockSpec((tn // bw, rows, bw), lambda *g: (col_of(*g), row_of(*g), 0))
    per = bw // tn
    return pl.BlockSpec((None, rows, tn), lambda *g: (col_of(*g) // per, row_of(*g), col_of(*g) % per))


def _read_cols(ref, bw, tn):
    if bw is None or tn % bw != 0:
        return ref[...]
    if tn == bw:
        return ref[0]
    return jnp.concatenate([ref[q] for q in range(tn // bw)], axis=1)


def _write_cols(ref, bw, tn, val):
    if bw is None or tn % bw != 0:
        ref[...] = val.astype(ref.dtype)
    else:
        for q in range(tn // bw):
            ref[q] = val[:, q * bw:(q + 1) * bw].astype(ref.dtype)


def mm_nn(a, b, name, *, res=None, out_dtype=F32, out_bw=None, tm=1024, tn=512, job=None):
    M, K = a.shape
    N = b.shape[1]
    tm = min(tm, M)

    def body(*refs):
        if res is None:
            a_ref, b_ref, o_ref = refs
        else:
            a_ref, b_ref, r_ref, o_ref = refs
        acc = _dot(a_ref[...].astype(BF16), b_ref[...])
        if res is not None:
            acc = acc + r_ref[...]
        _write_cols(o_ref, out_bw, tn, acc)

    ri, ci = (lambda i, j: i), (lambda i, j: j)
    in_specs = [pl.BlockSpec((tm, K), lambda i, j: (i, 0)), pl.BlockSpec((K, tn), lambda i, j: (0, j))]
    args = [a, b]
    if res is not None:
        in_specs.append(pl.BlockSpec((tm, tn), lambda i, j: (i, j)))
        args.append(res)
    oshape = (M, N) if out_bw is None else (N // out_bw, M, out_bw)
    return _pcall(body, args, grid=(M // tm, N // tn), name=name, in_specs=in_specs,
                  out_specs=[_col_spec(out_bw, tn, tm, ri, ci)], out_shape=[SDS(oshape, out_dtype)],
                  sem=("parallel", "parallel"), job=job)[0]


def mm_nn_rows(a, b, res, g, name, *, target=None, tm=512):
    M, K = a.shape
    N = b.shape[1]

    def body(*refs):
        a_ref, b_ref, r_ref, g_ref = refs[:4]
        xv = r_ref[...] + _dot(a_ref[...].astype(BF16), b_ref[...])
        r = lax.rsqrt(jnp.mean(xv * xv, axis=1, keepdims=True) + EPS)
        xh = xv * r
        if target is None:
            x_ref, h_ref = refs[4:]
            x_ref[...] = xv
            h_ref[...] = (xh * g_ref[...]).astype(BF16)
            return
        t_ref, loss_ref, dx_ref, dg_ref = refs[4:]
        err = xh * g_ref[...] - t_ref[...]
        dy = err / N
        gd = dy * g_ref[...]
        dx_ref[...] = r * (gd - xh * jnp.mean(gd * xh, axis=1, keepdims=True))

        @pl.when(pl.program_id(0) == 0)
        def _():
            dg_ref[...] = jnp.zeros_like(dg_ref)
            loss_ref[...] = jnp.zeros_like(loss_ref)

        dg_ref[...] += _colsum(dy * xh)
        loss_ref[...] += 0.5 * _colsum(jnp.mean(err * err, axis=1, keepdims=True))

    row = pl.BlockSpec((tm, N), lambda i: (i, 0))
    vec = pl.BlockSpec((1, N), lambda i: (0, 0))
    in_specs = [pl.BlockSpec((tm, K), lambda i: (i, 0)), pl.BlockSpec((K, N), lambda i: (0, 0)), row, vec]
    if target is None:
        return pl.pallas_call(body, grid=(M // tm,), name=name, in_specs=in_specs, out_specs=[row, row],
                              out_shape=[SDS((M, N), F32), SDS((M, N), BF16)], compiler_params=_cp("parallel"))(a, b, res, g)
    one = pl.BlockSpec((1, 1), lambda i: (0, 0))
    return pl.pallas_call(body, grid=(M // tm,), name=name, in_specs=in_specs + [row], out_specs=[one, row, vec],
                          out_shape=[SDS((1, 1), F32), SDS((M, N), F32), SDS((1, N), F32)],
                          compiler_params=_cp("arbitrary"))(a, b, res, g, target)


def mm_nt(a, a_bw, b, name, *, res=None, out_dtype=F32, tm=1024, tn=512, job=None):
    if a_bw is None:
        M, K = a.shape
    else:
        M, K = a.shape[1], a.shape[0] * a_bw
    N = b.shape[0]
    tm = min(tm, M)

    def body(*refs):
        if res is None:
            a_ref, b_ref, o_ref = refs
        else:
            a_ref, b_ref, r_ref, o_ref = refs
        av = _read_cols(a_ref, a_bw, K).astype(BF16)
        acc = _dot_nt(av, b_ref[...])
        if res is not None:
            acc = acc + r_ref[...]
        o_ref[...] = acc.astype(o_ref.dtype)

    in_specs = [_col_spec(a_bw, K, tm, lambda i, j: i, lambda i, j: 0), pl.BlockSpec((tn, K), lambda i, j: (j, 0))]
    args = [a, b]
    if res is not None:
        in_specs.append(pl.BlockSpec((tm, tn), lambda i, j: (i, j)))
        args.append(res)
    return _pcall(body, args, grid=(M // tm, N // tn), name=name, in_specs=in_specs,
                  out_specs=[pl.BlockSpec((tm, tn), lambda i, j: (i, j))], out_shape=[SDS((M, N), out_dtype)],
                  sem=("parallel", "parallel"), job=job)[0]


def mm_nt_norm_bwd(a, a_bw, b, x, g, res, name, *, tm=512, job=None):
    if a_bw is None:
        M, K = a.shape
    else:
        M, K = a.shape[1], a.shape[0] * a_bw
    N = b.shape[0]

    def body(a_ref, b_ref, x_ref, g_ref, r_ref, dx_ref, dg_ref):
        dh = _dot_nt(_read_cols(a_ref, a_bw, K).astype(BF16), b_ref[...])
        xv = x_ref[...]
        r = lax.rsqrt(jnp.mean(xv * xv, axis=1, keepdims=True) + EPS)
        xh = xv * r
        gd = dh * g_ref[...]
        dx_ref[...] = r_ref[...] + r * (gd - xh * jnp.mean(gd * xh, axis=1, keepdims=True))

        @pl.when(pl.program_id(0) == 0)
        def _():
            dg_ref[...] = jnp.zeros_like(dg_ref)

        dg_ref[...] += jnp.sum(dh * xh, axis=0, keepdims=True)

    row = pl.BlockSpec((tm, N), lambda i: (i, 0))
    vec = pl.BlockSpec((1, N), lambda i: (0, 0))
    return _pcall(body, [a, b, x, g, res], grid=(M // tm,), name=name,
                  in_specs=[_col_spec(a_bw, K, tm, lambda i: i, lambda i: 0), pl.BlockSpec((N, K), lambda i: (0, 0)), row, vec, row],
                  out_specs=[row, vec], out_shape=[SDS((M, N), F32), SDS((1, N), F32)], sem=("arbitrary",), job=job)


def mm_tn(a, b, b_bw, name, *, shards=None, tm=1024, tn=1024, ts=1024, job=None):
    S, Ka = a.shape
    N = b.shape[1] if b_bw is None else b.shape[0] * b_bw
    ts = min(ts, S)
    tm = min(tm, Ka)

    def body(a_ref, b_ref, o_ref):
        @pl.when(pl.program_id(2) == 0)
        def _():
            o_ref[...] = jnp.zeros_like(o_ref)

        bv = _read_cols(b_ref, b_bw, tn).astype(BF16)
        o_ref[...] += _dot_tn(a_ref[...].astype(BF16), bv)

    in_specs = [pl.BlockSpec((ts, tm), lambda i, j, k: (k, i)),
                _col_spec(b_bw, tn, ts, lambda i, j, k: k, lambda i, j, k: j)]
    if shards is None:
        out_spec, oshape = pl.BlockSpec((tm, tn), lambda i, j, k: (i, j)), (Ka, N)
    else:
        per = (N // shards) // tn
        out_spec = pl.BlockSpec((None, tm, tn), lambda i, j, k: (j // per, i, j % per))
        oshape = (shards, Ka, N // shards)
    return _pcall(body, [a, b], grid=(Ka // tm, N // tn, S // ts), name=name, in_specs=in_specs, out_specs=[out_spec],
                  out_shape=[SDS(oshape, F32)], sem=("parallel", "parallel", "arbitrary"), job=job)[0]


KEYS = 2 * BLK


def _bucket_tables():
    out = np.zeros((3, BLK, KEYS), np.int32)
    qi = np.arange(BLK)[:, None]
    kj = np.arange(KEYS)[None, :]
    for p, (win, dil) in enumerate(PATTERNS):
        w = win // dil
        assert w == BLK
        steps = qi + w - kj
        valid = (steps >= 0) & (steps <= w)
        dist = np.clip(steps, 0, w) * dil
        dd = np.maximum(dist, 1).astype(np.float32)
        large = 16 + (np.log(dd / np.float32(16)) / np.float32(math.log(2048 / 16)) * np.float32(16)).astype(np.int32)
        large = np.minimum(large, N_BUCKETS - 1)
        out[p] = np.where(valid, np.where(dist < 16, dist, large), -1)
    return out


def bias_tables(rel_bias):
    bkt = jnp.asarray(_bucket_tables())

    def body(rb_ref, bkt_ref, o_ref):
        h = pl.program_id(1)
        bk = bkt_ref[...]
        acc = jnp.full((BLK, KEYS), NEG, F32)
        for b in range(N_BUCKETS):
            acc = jnp.where(bk == b, rb_ref[h, b], acc)
        o_ref[...] = acc

    return pl.pallas_call(
        body, grid=(3, N_HEADS), name="bias_tables",
        in_specs=[pl.BlockSpec(memory_space=pltpu.SMEM),
                  pl.BlockSpec((None, BLK, KEYS), lambda p, h: (p, 0, 0))],
        out_specs=pl.BlockSpec((None, None, BLK, KEYS), lambda p, h: (p, h, 0, 0)),
        out_shape=SDS((3, N_HEADS, BLK, KEYS), F32), compiler_params=_cp("parallel", "parallel"))(rel_bias, bkt)


def bias_tables_bwd(ds_sum):
    bkt = jnp.asarray(_bucket_tables())

    def body(ds_ref, bkt_ref, o_ref):
        h = pl.program_id(0)

        @pl.when(h == 0)
        def _():
            o_ref[...] = jnp.zeros_like(o_ref)

        rows = lax.broadcasted_iota(jnp.int32, (N_HEADS, N_BUCKETS), 0)
        cols = lax.broadcasted_iota(jnp.int32, (N_HEADS, N_BUCKETS), 1)
        acc = jnp.zeros((N_HEADS, N_BUCKETS), F32)
        for b in range(N_BUCKETS):
            t = jnp.zeros((BLK, KEYS), F32)
            for p in range(3):
                t = t + jnp.where(bkt_ref[p] == b, ds_ref[p], 0.0)
            tot = jnp.sum(_rsum(t), axis=0, keepdims=True)
            acc = acc + jnp.where((rows == h) & (cols == b), tot, 0.0)
        o_ref[...] += acc

    return pl.pallas_call(
        body, grid=(N_HEADS,), name="bias_tables_bwd",
        in_specs=[pl.BlockSpec((3, None, BLK, KEYS), lambda h: (0, h, 0, 0)),
                  pl.BlockSpec((3, BLK, KEYS), lambda h: (0, 0, 0))],
        out_specs=pl.BlockSpec((N_HEADS, N_BUCKETS), lambda h: (0, 0)),
        out_shape=SDS((N_HEADS, N_BUCKETS), F32), compiler_params=_cp("arbitrary"))(ds_sum, bkt)


def _rows(start, dil):
    if dil == 1:
        return pl.ds(pl.multiple_of(start, BLK), BLK)
    return pl.ds(start, BLK, stride=dil)


GRP = 8


def _group_rows(i, dil, S):
    nb = S // (BLK * dil)
    run = min(nb, GRP)
    chunks = nb // run
    res0, b0 = (i // chunks) * (GRP // run), (i % chunks) * run
    cur = [(b0 + j % run) * (BLK * dil) + res0 + j // run for j in range(GRP)]
    t = lax.broadcasted_iota(jnp.int32, (GRP, 1, 1), 0)
    if run == nb:
        before, pen = None, jnp.where(t % run == 0, NEG, 0.0)
    else:
        before = _rows(jnp.maximum(b0 - 1, 0) * (BLK * dil) + res0, dil)
        pen = jnp.where((t == 0) & (b0 == 0), NEG, 0.0)
    return [_rows(s, dil) for s in cur], before, pen


def _load_group(ref, rows):
    return jnp.stack([ref[r, :] for r in rows])


def _with_prev(ref, before, cur_blocks):
    first = cur_blocks[:1] if before is None else ref[before, :][None].astype(cur_blocks.dtype)
    return jnp.concatenate([first, cur_blocks[:-1]], axis=0)


def _bdot_nt(a, b):
    return lax.dot_general(a, b, (((2,), (2,)), ((0,), (0,))), preferred_element_type=F32)


def _bdot(a, b):
    return lax.dot_general(a, b, (((2,), (1,)), ((0,), (0,))), preferred_element_type=F32)


def _bdot_tn(a, b):
    return lax.dot_general(a, b, (((1,), (1,)), ((0,), (0,))), preferred_element_type=F32)


def _lsum(x):
    return jnp.sum(x, axis=-1, keepdims=True)


def _widen(src, dst):
    S = src.shape[1]

    def chunk(i, carry):
        rows = pl.ds(pl.multiple_of(i * 512, 512), 512)
        for a in range(3):
            dst[a, rows, :] = src[a, rows, :].astype(F32)
        return carry

    lax.fori_loop(0, S // 512, chunk, 0)


def attn_fwd(qkv, bias, job=None):
    S = qkv.shape[3]
    nblk = S // BLK
    car = Carried(job, 2, 2, 1, 4)

    def body(*refs):
        (qkv_in, bias_ref, o_ref, lse_ref, qkv_ref), jrefs = car.split(refs)
        car.phase(0, pl.program_id(0), jrefs)
        _widen(qkv_in, qkv_ref)
        lane = lax.broadcasted_iota(jnp.int32, (GRP, BLK, BLK), 2)
        lo = lane < 64
        masks = (lo, jnp.logical_not(lo))
        lov = lax.broadcasted_iota(jnp.int32, (GRP, KEYS, BLK), 2) < 64
        vmasks = (lov, jnp.logical_not(lov))
        prev_keys = lax.broadcasted_iota(jnp.int32, (1, 1, KEYS), 2) < BLK
        q_ref, k_ref, v_ref = qkv_ref.at[0], qkv_ref.at[1], qkv_ref.at[2]
        for p, (_, dil) in enumerate(PATTERNS):
            def step(i, carry, p=p, dil=dil):
                rc, before, pen = _group_rows(i, dil, S)
                q2 = _load_group(q_ref, rc) * 0.125
                kc = _load_group(k_ref, rc).astype(BF16)
                keys = jnp.concatenate([_with_prev(k_ref, before, kc), kc], axis=1)
                vc = _load_group(v_ref, rc)
                vals = jnp.concatenate([_with_prev(v_ref, before, vc), vc], axis=1)
                pen = jnp.where(prev_keys, pen, 0.0)
                if p > 0:
                    acc_old, ml_old = _load_group(o_ref, rc), _load_group(lse_ref, rc)
                pv = jnp.zeros((GRP, BLK, BLK), F32)
                m_new, l_new, alpha = [], [], []
                for h in range(2):
                    qh = jnp.where(masks[h], q2, 0.0).astype(BF16)
                    s = _bdot_nt(qh, keys) + bias_ref[p, h][None] + pen
                    mn = jnp.max(s, axis=-1, keepdims=True)
                    if p > 0:
                        mo = ml_old[:, :, 64 * h:64 * h + 1]
                        mn = jnp.maximum(mn, mo)
                        al = jnp.exp(mo - mn)
                        alpha.append(al)
                    pr = jnp.exp(s - mn)
                    ln = _lsum(pr)
                    if p > 0:
                        ln = ln + al * ml_old[:, :, 64 * h + 32:64 * h + 33]
                    pv = pv + _bdot(pr.astype(BF16), jnp.where(vmasks[h], vals, 0.0).astype(BF16))
                    m_new.append(mn)
                    l_new.append(ln)
                if p > 0:
                    pv = pv + acc_old * jnp.where(lo, alpha[0], alpha[1])
                ml_t = jnp.where(lo, jnp.where(lane < 32, m_new[0], l_new[0]), jnp.where(lane < 96, m_new[1], l_new[1]))
                for j, r in enumerate(rc):
                    o_ref[r, :] = pv[j]
                    lse_ref[r, :] = ml_t[j]
                return carry

            lax.fori_loop(0, nblk // GRP, step, 0)

        def fin(i, carry):
            rows = pl.ds(pl.multiple_of(i * 512, 512), 512)
            ml = lse_ref[rows, :]
            is_m = (lax.broadcasted_iota(jnp.int32, ml.shape, 1) & 32) == 0
            den = jnp.where(is_m, pltpu.roll(ml, 96, 1), ml)
            o_ref[rows, :] = o_ref[rows, :] / den
            lse_ref[rows, :] = ml + jnp.log(den)
            return carry

        lax.fori_loop(0, S // 512, fin, 0)
        car.phase(1, pl.program_id(0), jrefs)
        car.phase(2, pl.program_id(0), jrefs)

    outs = pl.pallas_call(
        body, grid=(4,), name="attn_fwd",
        in_specs=[pl.BlockSpec((None, 3, None, S, BLK), lambda g: (0, 0, g, 0, 0)),
                  pl.BlockSpec((3, 2, BLK, KEYS), lambda g: (0, g, 0, 0))] + car.in_specs(),
        out_specs=[pl.BlockSpec((None, S, BLK), lambda g: (g, 0, 0)),
                   pl.BlockSpec((None, S, BLK), lambda g: (g, 0, 0))] + car.out_specs(),
        out_shape=[SDS((4, S, BLK), F32), SDS((4, S, BLK), F32)] + car.out_shapes(),
        input_output_aliases=car.aliases(),
        scratch_shapes=[pltpu.VMEM((3, S, BLK), F32)] + car.sems(),
        compiler_params=_cp("arbitrary"))(qkv, bias, *car.args())
    if job is not None:
        job.landed = list(outs[2:])
    return outs[0], outs[1]


def attn_bwd(qkv, bias, o, lse, do, job=None):
    S = qkv.shape[3]
    nblk = S // BLK
    car = Carried(job, 5, 2, 2, 4)

    def body(*refs):
        (qkv_in, bias_ref, o_ref, lse_ref, do_ref, dqkv_out, ds_ref, qkv_ref, dqkv_ref), jrefs = car.split(refs)
        car.phase(0, pl.program_id(0), jrefs)
        _widen(qkv_in, qkv_ref)
        lane = lax.broadcasted_iota(jnp.int32, (GRP, BLK, BLK), 2)
        lo = lane < 64
        masks = (lo, jnp.logical_not(lo))
        lov = lax.broadcasted_iota(jnp.int32, (GRP, KEYS, BLK), 2) < 64
        vmasks = (lov, jnp.logical_not(lov))
        prev_keys = lax.broadcasted_iota(jnp.int32, (1, 1, KEYS), 2) < BLK
        dqkv_ref[...] = jnp.zeros_like(dqkv_ref)
        ds_ref[...] = jnp.zeros_like(ds_ref)
        q_ref, k_ref, v_ref = qkv_ref.at[0], qkv_ref.at[1], qkv_ref.at[2]
        dq_ref, dk_ref, dv_ref = dqkv_ref.at[0], dqkv_ref.at[1], dqkv_ref.at[2]
        for p, (_, dil) in enumerate(PATTERNS):
            def step(i, carry, p=p, dil=dil):
                rc, before, pen = _group_rows(i, dil, S)
                q2 = _load_group(q_ref, rc) * 0.125
                kc = _load_group(k_ref, rc).astype(BF16)
                keys = jnp.concatenate([_with_prev(k_ref, before, kc), kc], axis=1)
                vc = _load_group(v_ref, rc).astype(BF16)
                vals = jnp.concatenate([_with_prev(v_ref, before, vc), vc], axis=1)
                pen = jnp.where(prev_keys, pen, 0.0)
                dot = _load_group(do_ref, rc)
                lset = _load_group(lse_ref, rc)
                prod = dot * _load_group(o_ref, rc)
                dq = jnp.zeros((GRP, BLK, BLK), F32)
                dkeys = jnp.zeros((GRP, KEYS, BLK), F32)
                dvals = jnp.zeros((GRP, KEYS, BLK), F32)
                for h in range(2):
                    qh = jnp.where(masks[h], q2, 0.0).astype(BF16)
                    doh = jnp.where(masks[h], dot, 0.0).astype(BF16)
                    delta = _lsum(jnp.where(masks[h], prod, 0.0))
                    lh = lset[:, :, 64 * h:64 * h + 1]
                    pr = jnp.exp(_bdot_nt(qh, keys) + bias_ref[p, h][None] + pen - lh)
                    ds = pr * (_bdot_nt(doh, vals) - delta)
                    ds_ref[p, h] += jnp.sum(ds, axis=0)
                    dsb = ds.astype(BF16)
                    dq = dq + jnp.where(masks[h], _bdot(dsb, keys), 0.0)
                    dkeys = dkeys + _bdot_tn(dsb, qh)
                    dvals = dvals + _bdot_tn(pr.astype(BF16), doh)
                dkp, dkc, dvp, dvc = dkeys[:, :BLK], dkeys[:, BLK:], dvals[:, :BLK], dvals[:, BLK:]
                none = jnp.zeros((1, BLK, BLK), F32)
                dkc = dkc + jnp.concatenate([dkp[1:], none], axis=0)
                dvc = dvc + jnp.concatenate([dvp[1:], none], axis=0)
                for j in range(GRP):
                    dq_ref[rc[j], :] += dq[j] * 0.125
                    dk_ref[rc[j], :] += dkc[j]
                    dv_ref[rc[j], :] += dvc[j]
                if before is not None:
                    dk_ref[before, :] += dkp[0]
                    dv_ref[before, :] += dvp[0]
                return carry

            lax.fori_loop(0, nblk // GRP, step, 0)

        def narrow(i, carry):
            rows = pl.ds(pl.multiple_of(i * 512, 512), 512)
            for a in range(3):
                dqkv_out[a, rows, :] = dqkv_ref[a, rows, :].astype(BF16)
            return carry

        lax.fori_loop(0, S // 512, narrow, 0)
        car.phase(1, pl.program_id(0), jrefs)
        car.phase(2, pl.program_id(0), jrefs)

    blk = pl.BlockSpec((None, S, BLK), lambda g: (g, 0, 0))
    outs = pl.pallas_call(
        body, grid=(4,), name="attn_bwd",
        in_specs=[pl.BlockSpec((None, 3, None, S, BLK), lambda g: (0, 0, g, 0, 0)),
                  pl.BlockSpec((3, 2, BLK, KEYS), lambda g: (0, g, 0, 0)), blk, blk, blk] + car.in_specs(),
        out_specs=[pl.BlockSpec((None, 3, None, S, BLK), lambda g: (0, 0, g, 0, 0)),
                   pl.BlockSpec((3, 2, BLK, KEYS), lambda g: (0, g, 0, 0))] + car.out_specs(),
        out_shape=[SDS((2, 3, 4, S, BLK), BF16), SDS((3, N_HEADS, BLK, KEYS), F32)] + car.out_shapes(),
        input_output_aliases=car.aliases(),
        scratch_shapes=[pltpu.VMEM((3, S, BLK), F32), pltpu.VMEM((3, S, BLK), F32)] + car.sems(),
        compiler_params=_cp("arbitrary"))(qkv, bias, o, lse, do, *car.args())
    if job is not None:
        job.landed = list(outs[2:])
    return outs[0], outs[1]


def _shift_down(u, k, halo):
    n = u.shape[0]
    row = lax.broadcasted_iota(jnp.int32, u.shape, 0)
    out = pltpu.roll(u, k, 0)
    hn = halo.shape[0]
    for j in range(k):
        out = jnp.where(row == j, halo[hn - k + j:hn - k + j + 1, :], out)
    return out


def _shift_up(u, k, halo):
    n = u.shape[0]
    row = lax.broadcasted_iota(jnp.int32, u.shape, 0)
    out = pltpu.roll(u, n - k, 0)
    for j in range(k):
        out = jnp.where(row == n - k + j, halo[j:j + 1, :], out)
    return out


def _conv3(u, halo, w0, w1, w2):
    return _shift_down(u, 2, halo) * w0 + _shift_down(u, 1, halo) * w1 + u * w2


def _colsum(x):
    return jnp.sum(x, axis=0, keepdims=True)


def _mixer_specs(S, tm):
    conv = pl.BlockSpec((None, 12, tm, BLK), lambda i: (1, 0, i, 0))
    halo = pl.BlockSpec((None, 12, 16, BLK), lambda i: (1, 0, jnp.maximum(i * (tm // 16) - 1, 0), 0))
    ob = pl.BlockSpec((4, tm, BLK), lambda i: (0, i, 0))
    return conv, halo, ob


def _mixer_recompute(i, o_ref, pr_ref, ph_ref, w_ref):
    ob = [o_ref[q] for q in range(4)]
    gb = [pr_ref[q].astype(F32) for q in range(4)]
    gc = [pr_ref[4 + q].astype(F32) for q in range(4)]
    xi = [pr_ref[8 + q].astype(F32) for q in range(4)]
    keep = jnp.where(i > 0, 1.0, 0.0)
    u = [gc[q] * xi[q] for q in range(4)]
    hu = [ph_ref[4 + q].astype(F32) * ph_ref[8 + q].astype(F32) * keep for q in range(4)]
    w = [[w_ref[k:k + 1, q * BLK:(q + 1) * BLK] for k in range(3)] for q in range(4)]
    cv = [_conv3(u[q], hu[q], *w[q]) for q in range(4)]
    return ob, gb, gc, xi, u, hu, cv, w


def _rms_blocks(blocks):
    ss = sum(_rsum(b * b) for b in blocks)
    return lax.rsqrt(ss / (BLK * len(blocks)) + EPS)


def mixer_post_fwd(o, proj, w_sc, g_a, g_c):
    S = o.shape[1]
    tm = 512

    def body(o_ref, pr_ref, ph_ref, w_ref, ga_ref, gc_ref, m_ref):
        i = pl.program_id(0)
        ob, gb, _, _, _, _, cv, _ = _mixer_recompute(i, o_ref, pr_ref, ph_ref, w_ref)
        conv = [gb[q] * cv[q] for q in range(4)]
        ra, rc = _rms_blocks(ob), _rms_blocks(conv)
        for q in range(4):
            sl = slice(q * BLK, (q + 1) * BLK)
            m_ref[:, q * BLK:(q + 1) * BLK] = (ob[q] * ra * ga_ref[:, sl]).astype(BF16)
            m_ref[:, ATTN_W + q * BLK:ATTN_W + (q + 1) * BLK] = (conv[q] * rc * gc_ref[:, sl]).astype(BF16)

    conv_s, halo_s, ob_s = _mixer_specs(S, tm)
    full = lambda r, c: pl.BlockSpec((r, c), lambda i: (0, 0))
    return pl.pallas_call(
        body, grid=(S // tm,), name="mixer_post_fwd",
        in_specs=[ob_s, conv_s, halo_s, full(3, 512), full(1, 512), full(1, 512)],
        out_specs=pl.BlockSpec((tm, D_MODEL), lambda i: (i, 0)), out_shape=SDS((S, D_MODEL), BF16),
        compiler_params=_cp("parallel"))(o, proj, proj, w_sc, g_a, g_c)


def mixer_post_bwd_a(dmixed, o, proj, w_sc, g_a, g_c, job=None):
    S = o.shape[1]
    tm = 512

    def body(dm_ref, o_ref, pr_ref, ph_ref, w_ref, ga_ref, gc_ref, do_ref, dgb_ref, dcv_ref, dga_ref, dgc_ref):
        i = pl.program_id(0)
        ob, gb, _, _, _, _, cv, _ = _mixer_recompute(i, o_ref, pr_ref, ph_ref, w_ref)
        conv = [gb[q] * cv[q] for q in range(4)]
        ra, rc = _rms_blocks(ob), _rms_blocks(conv)

        @pl.when(i == 0)
        def _():
            dga_ref[...] = jnp.zeros_like(dga_ref)
            dgc_ref[...] = jnp.zeros_like(dgc_ref)

        for blocks, r, g_ref, off, dg_ref, is_attn in ((ob, ra, ga_ref, 0, dga_ref, True), (conv, rc, gc_ref, ATTN_W, dgc_ref, False)):
            xh = [blocks[q] * r for q in range(4)]
            dm = [dm_ref[:, off + q * BLK:off + (q + 1) * BLK].astype(F32) for q in range(4)]
            gd = [dm[q] * g_ref[:, q * BLK:(q + 1) * BLK] for q in range(4)]
            mean = sum(_rsum(gd[q] * xh[q]) for q in range(4)) / (4 * BLK)
            for q in range(4):
                dg_ref[:, q * BLK:(q + 1) * BLK] += _colsum(dm[q] * xh[q])
                dx = r * (gd[q] - xh[q] * mean)
                if is_attn:
                    do_ref[q] = dx
                else:
                    dgb_ref[q] = dx * cv[q]
                    dcv_ref[q] = dx * gb[q]

    conv_s, halo_s, ob_s = _mixer_specs(S, tm)
    full = lambda r, c: pl.BlockSpec((r, c), lambda i: (0, 0))
    return _pcall(
        body, [dmixed, o, proj, proj, w_sc, g_a, g_c], grid=(S // tm,), name="mixer_post_bwd_a",
        in_specs=[pl.BlockSpec((tm, D_MODEL), lambda i: (i, 0)), ob_s, conv_s, halo_s, full(3, 512), full(1, 512), full(1, 512)],
        out_specs=[ob_s, ob_s, ob_s, full(1, 512), full(1, 512)],
        out_shape=[SDS((4, S, BLK), F32)] * 3 + [SDS((1, 512), F32)] * 2, sem=("arbitrary",), job=job)


def mixer_post_bwd_b(dproj, dgb, dcv, proj, w_sc):
    S = proj.shape[2]
    tm = 512
    last = S // 8 - 1

    def body(dp_in, dgb_ref, dcv_ref, dn_ref, pr_ref, w_ref, dp_ref, dw_ref):
        i = pl.program_id(0)
        keep_next = jnp.where(i < pl.num_programs(0) - 1, 1.0, 0.0)

        @pl.when(i == 0)
        def _():
            dw_ref[...] = jnp.zeros_like(dw_ref)

        for q in range(4):
            sl = slice(q * BLK, (q + 1) * BLK)
            gc, xi = pr_ref[4 + q].astype(F32), pr_ref[8 + q].astype(F32)
            u = gc * xi
            dcv = dcv_ref[q]
            dn = dn_ref[q] * keep_next
            d1, d2 = _shift_up(dcv, 1, dn), _shift_up(dcv, 2, dn)
            w0, w1, w2 = (w_ref[k:k + 1, sl] for k in range(3))
            du = dcv * w2 + d1 * w1 + d2 * w0
            dw_ref[0:1, sl] += _colsum(d2 * u)
            dw_ref[1:2, sl] += _colsum(d1 * u)
            dw_ref[2:3, sl] += _colsum(dcv * u)
            dp_ref[q] = dgb_ref[q].astype(BF16)
            dp_ref[4 + q] = (du * xi).astype(BF16)
            dp_ref[8 + q] = (du * gc).astype(BF16)

    conv_s, _, ob_s = _mixer_specs(S, tm)
    nxt = pl.BlockSpec((4, 8, BLK), lambda i: (0, jnp.minimum((i + 1) * (tm // 8), last), 0))
    full = lambda r, c: pl.BlockSpec((r, c), lambda i: (0, 0))
    return pl.pallas_call(
        body, grid=(S // tm,), name="mixer_post_bwd_b",
        in_specs=[pl.BlockSpec(memory_space=pl.ANY), ob_s, ob_s, nxt, conv_s, full(3, 512)],
        out_specs=[conv_s, full(3, 512)],
        out_shape=[SDS(dproj.shape, BF16), SDS((3, 512), F32)],
        input_output_aliases={0: 0}, compiler_params=_cp("arbitrary"))(dproj, dgb, dcv, dcv, proj, w_sc)


def xattn_fwd(q, k, v):
    S = q.shape[0]
    tm = 512
    scale = MEM_HD ** -0.5

    def body(q_ref, k_ref, v_ref, o_ref):
        for h in range(N_MEM_HEADS):
            sl = slice(h * MEM_HD, (h + 1) * MEM_HD)
            s = _dot_nt(q_ref[:, sl], k_ref[:, sl]) * scale
            p = jnp.exp(s - jnp.max(s, axis=1, keepdims=True))
            p = p / _rsum(p)
            o_ref[:, sl] = _dot(p.astype(BF16), v_ref[:, sl]).astype(BF16)

    row = pl.BlockSpec((tm, D_MODEL), lambda i: (i, 0))
    kv = pl.BlockSpec(k.shape, lambda i: (0, 0))
    return pl.pallas_call(body, grid=(S // tm,), name="xattn_fwd", in_specs=[row, kv, kv], out_specs=row,
                          out_shape=SDS((S, D_MODEL), BF16), compiler_params=_cp("parallel"))(q, k, v)


def xattn_bwd(q, k, v, do):
    S = q.shape[0]
    tm = 512
    scale = MEM_HD ** -0.5

    def body(q_ref, k_ref, v_ref, do_ref, dq_ref, dk_ref, dv_ref):
        @pl.when(pl.program_id(0) == 0)
        def _():
            dk_ref[...] = jnp.zeros_like(dk_ref)
            dv_ref[...] = jnp.zeros_like(dv_ref)

        for h in range(N_MEM_HEADS):
            sl = slice(h * MEM_HD, (h + 1) * MEM_HD)
            qh, kh, vh, doh = q_ref[:, sl], k_ref[:, sl], v_ref[:, sl], do_ref[:, sl]
            s = _dot_nt(qh, kh) * scale
            p = jnp.exp(s - jnp.max(s, axis=1, keepdims=True))
            p = p / _rsum(p)
            dp = _dot_nt(doh, vh)
            ds = (p * (dp - _rsum(p * dp)) * scale).astype(BF16)
            dq_ref[:, sl] = _dot(ds, kh).astype(BF16)
            dk_ref[:, sl] += _dot_tn(ds, qh)
            dv_ref[:, sl] += _dot_tn(p.astype(BF16), doh)

    row = pl.BlockSpec((tm, D_MODEL), lambda i: (i, 0))
    kv = pl.BlockSpec(k.shape, lambda i: (0, 0))
    return pl.pallas_call(body, grid=(S // tm,), name="xattn_bwd", in_specs=[row, kv, kv, row], out_specs=[row, kv, kv],
                          out_shape=[SDS((S, D_MODEL), BF16), SDS(k.shape, F32), SDS(k.shape, F32)],
                          compiler_params=_cp("arbitrary"))(q, k, v, do)


FFN_TM, FFN_TC = 256, 1408


def _ffn_specs(S, order):
    tm, tc = FFN_TM, FFN_TC
    ij = (lambda a, b: (a, b)) if order == "ij" else (lambda a, b: (b, a))
    blk = pl.BlockSpec((2, tm, tc), lambda a, b: (0,) + ij(a, b))
    prev = pl.BlockSpec((2, 16, tc), lambda a, b: (0, jnp.maximum(ij(a, b)[0] * (tm // 16) - 1, 0), ij(a, b)[1]))
    nxt = pl.BlockSpec((2, 16, tc), lambda a, b: (0, jnp.minimum((ij(a, b)[0] + 1) * (tm // 16), S // 16 - 1), ij(a, b)[1]))
    wsp = pl.BlockSpec((2, 3, tc), lambda a, b: (0, 0, ij(a, b)[1]))
    bsp = pl.BlockSpec((2, 1, tc), lambda a, b: (0, 0, ij(a, b)[1]))
    act = pl.BlockSpec((tm, tc), lambda a, b: ij(a, b))
    return blk, prev, nxt, wsp, bsp, act


def _ffn_up(i, up_ref, uh_ref, w_ref, b_ref):
    keep = jnp.where(i > 0, 1.0, 0.0)
    out = []
    for half in range(2):
        u = up_ref[half].astype(F32)
        hu = uh_ref[half].astype(F32) * keep
        w0, w1, w2 = (w_ref[half, k:k + 1, :] for k in range(3))
        out.append(_conv3(u, hu, w0, w1, w2) + b_ref[half])
    return out


def ffn_act_fwd(up_pre, w_fc, b_fc):
    S = up_pre.shape[1]

    def body(up_ref, uh_ref, w_ref, b_ref, a_ref, gv_ref):
        g, v = _ffn_up(pl.program_id(0), up_ref, uh_ref, w_ref, b_ref)
        a_ref[...] = (g * jax.nn.sigmoid(g) * v).astype(BF16)
        gv_ref[0] = g.astype(BF16)
        gv_ref[1] = v.astype(BF16)

    blk, prev, _, wsp, bsp, act = _ffn_specs(S, "ij")
    return pl.pallas_call(body, grid=(S // FFN_TM, D_FF // FFN_TC), name="ffn_act_fwd",
                          in_specs=[blk, prev, wsp, bsp], out_specs=[act, blk],
                          out_shape=[SDS((S, D_FF), BF16), SDS((2, S, D_FF), BF16)],
                          compiler_params=_cp("parallel", "parallel"))(up_pre, up_pre, w_fc, b_fc)


def up_act_fwd(h, w_up, w_fc, b_fc):
    S, K = h.shape
    tm, tc = FFN_TM, FFN_TC
    nj = D_FF // tc

    def body(h_ref, wg_ref, wv_ref, w_ref, b_ref, pre_ref, gv_ref, a_ref, halo):
        i = pl.program_id(1)
        hv = h_ref[...]
        res = []
        for half, wt_ref in enumerate((wg_ref, wv_ref)):
            u = _dot(hv, wt_ref[...])
            hu = jnp.where(i > 0, halo[half], 0.0)
            halo[half] = u[tm - 8:, :]
            w0, w1, w2 = (w_ref[half, k:k + 1, :] for k in range(3))
            pre_ref[half] = u.astype(BF16)
            res.append(_conv3(u, hu, w0, w1, w2) + b_ref[half])
        g, v = res
        gv_ref[0] = g.astype(BF16)
        gv_ref[1] = v.astype(BF16)
        a_ref[...] = (g * jax.nn.sigmoid(g) * v).astype(BF16)

    blk = pl.BlockSpec((2, tm, tc), lambda j, i: (0, i, j))
    return pl.pallas_call(
        body, grid=(nj, S // tm), name="up_act_fwd",
        in_specs=[pl.BlockSpec((tm, K), lambda j, i: (i, 0)), pl.BlockSpec((K, tc), lambda j, i: (0, j)),
                  pl.BlockSpec((K, tc), lambda j, i: (0, j + nj)), pl.BlockSpec((2, 3, tc), lambda j, i: (0, 0, j)),
                  pl.BlockSpec((2, 1, tc), lambda j, i: (0, 0, j))],
        out_specs=[blk, blk, pl.BlockSpec((tm, tc), lambda j, i: (i, j))],
        out_shape=[SDS((2, S, D_FF), BF16), SDS((2, S, D_FF), BF16), SDS((S, D_FF), BF16)],
        scratch_shapes=[pltpu.VMEM((2, 8, tc), F32)],
        compiler_params=_cp("parallel", "arbitrary"))(h, w_up, w_up, w_fc, b_fc)


def ffn_act_bwd(dx, w_down, up, up_pre, w_fc):
    S = up.shape[1]

    def act_grads(da, g, v):
        sg = jax.nn.sigmoid(g)
        return da * v * (sg * (1.0 + g * (1.0 - sg))), da * g * sg

    def body(dx_ref, dxn_ref, wd_ref, gv_ref, gvn_ref, up_ref, w_ref, dp_ref, dw_ref, db_ref):
        i = pl.program_id(1)
        keep_next = jnp.where(i < pl.num_programs(1) - 1, 1.0, 0.0)

        @pl.when(i == 0)
        def _():
            dw_ref[...] = jnp.zeros_like(dw_ref)
            db_ref[...] = jnp.zeros_like(db_ref)

        wd = wd_ref[...]
        da = _dot_nt(dx_ref[...].astype(BF16), wd)
        dan = _dot_nt(dxn_ref[...].astype(BF16), wd) * keep_next
        here = act_grads(da, gv_ref[0].astype(F32), gv_ref[1].astype(F32))
        after = act_grads(dan, gvn_ref[0].astype(F32), gvn_ref[1].astype(F32))
        for half in range(2):
            d, dn = here[half], after[half]
            u = up_ref[half].astype(F32)
            d1, d2 = _shift_up(d, 1, dn), _shift_up(d, 2, dn)
            w0, w1, w2 = (w_ref[half, k:k + 1, :] for k in range(3))
            dp_ref[half] = (d * w2 + d1 * w1 + d2 * w0).astype(BF16)
            dw_ref[half, 0:1, :] += _colsum(d2 * u)
            dw_ref[half, 1:2, :] += _colsum(d1 * u)
            dw_ref[half, 2:3, :] += _colsum(d * u)
            db_ref[half] += _colsum(d)

    tm, tc = FFN_TM, FFN_TC
    blk, _, nxt, wsp, bsp, _ = _ffn_specs(S, "ji")
    K = dx.shape[1]
    rows = pl.BlockSpec((tm, K), lambda j, i: (i, 0))
    rows_next = pl.BlockSpec((16, K), lambda j, i: (jnp.minimum((i + 1) * (tm // 16), S // 16 - 1), 0))
    return pl.pallas_call(body, grid=(D_FF // tc, S // tm), name="ffn_act_bwd",
                          in_specs=[rows, rows_next, pl.BlockSpec((tc, K), lambda j, i: (j, 0)), blk, nxt, blk, wsp],
                          out_specs=[blk, wsp, bsp],
                          out_shape=[SDS((2, S, D_FF), BF16), SDS((2, 3, D_FF), F32), SDS((2, 1, D_FF), F32)],
                          compiler_params=_cp("parallel", "arbitrary"))(dx, dx, w_down, up, up, up_pre, w_fc)


def local_step(x, mem, target, wb, ws, comm=None):
    S = x.shape[0]
    assert S % 2048 == 0
    if comm is None:
        comm = NoComm()
    else:
        wb = {}
    ws = dict(ws)

    bias = bias_tables(ws["rel_bias"])
    h1 = rmsnorm_fwd(x, ws["g_mix"], "norm_mix", job=comm.job("norm_mix"))
    comm.landed("norm_mix", wb, ws)
    w_fc = ws["w_ffn_conv"].reshape(3, 2, D_FF).transpose(1, 0, 2)
    b_fc = ws["b_ffn_conv"].reshape(2, 1, D_FF)
    proj = mm_nn(h1, wb["w_in"], "proj", out_dtype=BF16, out_bw=BLK, tn=768, job=comm.job("proj"))
    comm.landed("proj", wb)
    qkv = proj.reshape(2, 3, 4, S, BLK)
    o, lse = attn_fwd(qkv, bias, job=comm.job("attn_fwd"))
    comm.landed("attn_fwd", wb)
    proj4 = proj.reshape(2, 12, S, BLK)
    mixed = mixer_post_fwd(o, proj4, ws["w_short_conv"], ws["g_attn_out"], ws["g_conv_out"])
    x1, h2 = mm_nn_rows(mixed, wb["w_out"], x, ws["g_xattn"], "out_proj")
    mem_n = rmsnorm_fwd(mem, ws["g_mem"], "norm_mem")
    xq = mm_nn(h2, wb["w_xq"], "xq", out_dtype=BF16)
    xk = mm_nn(mem_n, wb["w_xk"], "xk", out_dtype=BF16, tn=1024)
    xv = mm_nn(mem_n, wb["w_xv"], "xv", out_dtype=BF16, tn=1024)
    xo = xattn_fwd(xq, xk, xv)
    x2, h3 = mm_nn_rows(xo, wb["w_xo"], x1, ws["g_ffn"], "xo_proj")
    up_pre, up, act = up_act_fwd(h3, wb["w_up"], w_fc, b_fc)
    loss, dx3, dg_final = mm_nn_rows(act, wb["w_down"], x2, ws["g_final"].reshape(1, -1), "down_proj", target=target)

    gb, gs = {}, {"g_final": dg_final}
    gb["w_down"] = mm_tn(act, dx3, None, "dw_down", tm=1408, tn=1024)
    d_up_pre, dw_fc, db_fc = ffn_act_bwd(dx3, wb["w_down"], up, up_pre, w_fc)
    gs["b_ffn_conv"] = db_fc.reshape(1, 2 * D_FF)
    gs["w_ffn_conv"] = dw_fc.transpose(1, 0, 2).reshape(3, 2 * D_FF)
    gb["w_up"] = mm_tn(h3, d_up_pre, D_FF, "dw_up", shards=4, tn=1408)
    dx2, gs["g_ffn"] = mm_nt_norm_bwd(d_up_pre, D_FF, wb["w_up"], x2, ws["g_ffn"], dx3, "d_h3", tm=256,
                                      job=comm.job("d_h3", gb))
    comm.landed("d_h3")
    gb["w_xo"] = mm_tn(xo, dx2, None, "dw_xo")
    dxo = mm_nt(dx2, None, wb["w_xo"], "d_xo", out_dtype=BF16)
    dxq, dxk, dxv = xattn_bwd(xq, xk, xv, dxo)
    gb["w_xq"] = mm_tn(h2, dxq, None, "dw_xq")
    gb["w_xk"] = mm_tn(mem_n, dxk, None, "dw_xk", tn=1024)
    gb["w_xv"] = mm_tn(mem_n, dxv, None, "dw_xv", tn=1024)
    dmem_n = mm_nt(dxk, None, wb["w_xk"], "d_memk", tn=1024)
    dmem_n = mm_nt(dxv, None, wb["w_xv"], "d_memv", res=dmem_n, tn=1024)
    gs["g_mem"] = rmsnorm_bwd(mem, ws["g_mem"], dmem_n, None, "norm_mem_bwd")
    dx1, gs["g_xattn"] = mm_nt_norm_bwd(dxq, None, wb["w_xq"], x1, ws["g_xattn"], dx2, "d_h2")
    gb["w_out"] = mm_tn(mixed, dx1, None, "dw_out")
    dmixed = mm_nt(dx1, None, wb["w_out"], "d_mixed")
    do, dgb, dcv, gs["g_attn_out"], gs["g_conv_out"] = mixer_post_bwd_a(
        dmixed, o, proj4, ws["w_short_conv"], ws["g_attn_out"], ws["g_conv_out"], job=comm.job("mixer_post_bwd_a", gb))
    comm.landed("mixer_post_bwd_a")
    dproj, ds_sum = attn_bwd(qkv, bias, o, lse, do, job=comm.job("attn_bwd"))
    comm.landed("attn_bwd")
    gs["rel_bias"] = bias_tables_bwd(ds_sum)
    dproj, gs["w_short_conv"] = mixer_post_bwd_b(dproj.reshape(2, 12, S, BLK), dgb, dcv, proj4, ws["w_short_conv"])
    dproj = dproj.reshape(24, S, BLK)
    gb["w_in"] = mm_tn(h1, dproj, BLK, "dw_in", shards=4, tn=768, job=comm.job("dw_in"))
    comm.landed("dw_in")
    comm.first_update(gb)
    grad_x, gs["g_mix"] = mm_nt_norm_bwd(dproj, BLK, wb["w_in"], x, ws["g_mix"], dx1, "d_h1", tm=256,
                                         job=comm.job("d_h1", gb))
    comm.landed("d_h1")
    return loss, grad_x, comm.finish(gb), gs


def _adam_update(w_ref, g_ref, m_ref, v_ref, d_ref, nm_ref, nv_ref):
    gv = g_ref[...]
    mn = ADAM_B1 * m_ref[...] + (1.0 - ADAM_B1) * gv
    vn = ADAM_B2 * v_ref[...] + (1.0 - ADAM_B2) * (gv * gv)
    m_hat = mn / (1.0 - ADAM_B1 ** ADAM_STEP)
    v_hat = vn / (1.0 - ADAM_B2 ** ADAM_STEP)
    d_ref[...] = -ADAM_LR * (m_hat / (jnp.sqrt(v_hat) + ADAM_EPS) + ADAM_WD * w_ref[...])
    nm_ref[...] = mn
    nv_ref[...] = vn


def adamw(ws, gs, ms, vs, name, job=None):
    n = len(ws)
    R, C = ws[0].shape
    tr = R // 4

    def body(*refs):
        for p in range(n):
            _adam_update(*(refs[q * n + p] for q in range(7)))

    blk = pl.BlockSpec((tr, C), lambda i: (i, 0))
    outs = _pcall(body, [*ws, *gs, *ms, *vs], grid=(4,), name=name, in_specs=[blk] * (4 * n), out_specs=[blk] * (3 * n),
                  out_shape=[SDS((R, C), F32)] * (3 * n), sem=("parallel",), job=job)
    return outs[:n], outs[n:2 * n], outs[2 * n:]


BIG = ("w_in", "w_out", "w_xq", "w_xk", "w_xv", "w_xo", "w_up", "w_down")
COL_SHARDED = ("w_in", "w_up")
N_BIG = len(BIG)
ANY = pl.BlockSpec(memory_space=pl.ANY)


def _place():
    x, y, c = lax.axis_index("x"), lax.axis_index("y"), lax.axis_index("c")
    chips = [(1 - x, y), (x, 1 - y), (1 - x, 1 - y)]
    return x, y, c, chips


def _window(full, name, R, C, shard, half):
    r0, nr = (0, R) if half is None else (half * (R // 2), R // 2)
    if name in COL_SHARDED:
        return full.at[pl.ds(r0, nr), pl.ds(shard * C, C)]
    return full.at[pl.ds(shard * R + r0, nr), :]


def same_shape_groups(names, arrays):
    groups = {}
    for n in names:
        groups.setdefault(tuple(arrays[n].shape), []).append(n)
    return list(groups.values())


def place_shards(ws, mine_arr, col, name):
    n = len(ws)
    R, C = ws[0].shape

    def body(s_ref, *refs):
        for p in range(n):
            refs[n + p][...] = refs[p][...].astype(BF16)

    grid_spec = pltpu.PrefetchScalarGridSpec(
        num_scalar_prefetch=1, grid=(1,), in_specs=[pl.BlockSpec((R, C), lambda i, s: (0, 0))] * n,
        out_specs=[pl.BlockSpec((R, C), (lambda i, s: (0, s[0])) if col else (lambda i, s: (s[0], 0)))] * n)
    return pl.pallas_call(body, grid_spec=grid_spec, name=name,
                          out_shape=[SDS((R, 4 * C) if col else (4 * R, C), BF16)] * n,
                          compiler_params=_cp("arbitrary"))(mine_arr, *ws)


def _gather_jobs(names, shapes):
    nw = len(names)

    def start(full, sems):
        send, recv, fsend, frecv = sems
        x, y, c, chips = _place()
        mine = 2 * x + y
        for w, n in enumerate(names):
            R, C = shapes[w]
            own = _window(full[w], n, R, C, mine, c)
            for j, chip in enumerate(chips):
                pltpu.make_async_remote_copy(src_ref=own, dst_ref=own, send_sem=send.at[w, j], recv_sem=recv.at[w, j],
                                             device_id=(*chip, c), device_id_type=MESH).start()

    def mid(full, sems, lo=0, hi=nw):
        send, recv, fsend, frecv = sems
        x, y, c, chips = _place()
        sib = (x, y, 1 - c)
        for w, n in list(enumerate(names))[lo:hi]:
            R, C = shapes[w]
            for j, chip in enumerate(chips):
                landed = _window(full[w], n, R, C, 2 * chip[0] + chip[1], c)
                pltpu.make_async_remote_copy(src_ref=landed, dst_ref=landed, send_sem=send.at[w, j], recv_sem=recv.at[w, j],
                                             device_id=(*chip, c), device_id_type=MESH).wait_recv()
                pltpu.make_async_remote_copy(src_ref=landed, dst_ref=landed, send_sem=fsend.at[w, j],
                                             recv_sem=frecv.at[w, j], device_id=sib, device_id_type=MESH).start()

    def finish(full, sems):
        send, recv, fsend, frecv = sems
        x, y, c, chips = _place()
        mine = 2 * x + y
        sib = (x, y, 1 - c)
        for w, n in enumerate(names):
            R, C = shapes[w]
            own = _window(full[w], n, R, C, mine, c)
            for j, chip in enumerate(chips):
                landed = _window(full[w], n, R, C, 2 * chip[0] + chip[1], c)
                other = _window(full[w], n, R, C, 2 * chip[0] + chip[1], 1 - c)
                pltpu.make_async_remote_copy(src_ref=other, dst_ref=other, send_sem=fsend.at[w, j], recv_sem=frecv.at[w, j],
                                             device_id=sib, device_id_type=MESH).wait_recv()
                pltpu.make_async_remote_copy(src_ref=own, dst_ref=own, send_sem=send.at[w, j], recv_sem=recv.at[w, j],
                                             device_id=(*chip, c), device_id_type=MESH).wait_send()
                pltpu.make_async_remote_copy(src_ref=landed, dst_ref=landed, send_sem=fsend.at[w, j],
                                             recv_sem=frecv.at[w, j], device_id=sib, device_id_type=MESH).wait_send()

    return start, mid, finish, [pltpu.SemaphoreType.DMA((nw, 3))] * 4


class CommJob:
    def __init__(self, ins, out_shapes, inplace, start, finish, sems, mid=None):
        self.ins, self.out_shapes, self.inplace = list(ins), list(out_shapes), inplace
        self.start, self.mid, self.finish, self.sems = start, mid, finish, list(sems)

    def run(self, phase, ins, outs, sems):
        if phase == 0:
            self.start(ins, outs, sems)
        elif phase == 1:
            if self.mid is not None:
                self.mid(ins, outs, sems)
        else:
            self.finish(ins, outs, sems)


def run_job(name, job):
    n_in, n_out = len(job.ins), len(job.out_shapes)

    def body(*refs):
        ins, outs, sm = refs[:n_in], refs[n_in:n_in + n_out], refs[n_in + n_out:]
        for phase in range(3):
            job.run(phase, ins, outs, sm)

    job.landed = list(pl.pallas_call(
        body, name=name, in_specs=[ANY] * n_in, out_specs=[ANY] * n_out, out_shape=job.out_shapes,
        input_output_aliases={i: i for i in range(n_in)} if job.inplace else {},
        scratch_shapes=job.sems)(*job.ins))
    return job.landed


class Carried:
    def __init__(self, job, n_in, n_out, n_scratch, steps):
        self.job, self.n_in, self.n_out, self.n_scratch, self.steps = job, n_in, n_out, n_scratch, steps
        self.nji = len(job.ins) if job else 0
        self.njo = len(job.out_shapes) if job else 0

    def in_specs(self):
        return [ANY] * self.nji

    def out_specs(self):
        return [ANY] * self.njo

    def out_shapes(self):
        return list(self.job.out_shapes) if self.job else []

    def aliases(self):
        if not (self.job and self.job.inplace):
            return {}
        return {self.n_in + i: self.n_out + i for i in range(self.nji)}

    def sems(self):
        return list(self.job.sems) if self.job else []

    def args(self):
        return list(self.job.ins) if self.job else []

    def split(self, refs):
        a = self.n_in
        b = a + self.nji
        c = b + self.n_out
        d = c + self.njo
        e = d + self.n_scratch
        return refs[:a] + refs[b:c] + refs[d:e], (refs[a:b], refs[c:d], refs[e:])

    def phase(self, phase, step, jrefs):
        if self.job is None:
            return
        at = {0: 0, 1: max(self.steps - 2, 0), 2: self.steps - 1}[phase]

        @pl.when(step == at)
        def _():
            self.job.run(phase, *jrefs)


def _pcall(body, args, *, grid, in_specs, out_specs, out_shape, name, sem, scratch=(), aliases=None, job=None):
    n_in, n_out = len(args), len(out_shape)
    if job is None:
        return pl.pallas_call(
            body, grid=grid, in_specs=list(in_specs), out_specs=list(out_specs), out_shape=list(out_shape), name=name,
            scratch_shapes=list(scratch), input_output_aliases=dict(aliases or {}), compiler_params=_cp(*sem))(*args)
    total = int(np.prod(grid))
    car = Carried(job, n_in, n_out, len(scratch), total)

    def wrapped(*refs):
        main, jrefs = car.split(refs)
        lin = pl.program_id(0)
        for ax in range(1, len(grid)):
            lin = lin * grid[ax] + pl.program_id(ax)
        car.phase(0, lin, jrefs)
        body(*main)
        car.phase(1, lin, jrefs)
        car.phase(2, lin, jrefs)

    res = pl.pallas_call(
        wrapped, grid=grid, in_specs=list(in_specs) + car.in_specs(), out_specs=list(out_specs) + car.out_specs(),
        out_shape=list(out_shape) + car.out_shapes(), name=name, scratch_shapes=list(scratch) + car.sems(),
        input_output_aliases={**dict(aliases or {}), **car.aliases()},
        compiler_params=_cp(*(["arbitrary"] * len(grid))))(*args, *car.args())
    job.landed = list(res[n_out:])
    return list(res[:n_out])


def gather_job(placed, names):
    shapes = []
    for n in names:
        R, C = placed[n].shape
        shapes.append((R, C // 4) if n in COL_SHARDED else (R // 4, C))
    start, mid, finish, sems = _gather_jobs(names, shapes)
    arrays = [placed[n] for n in names]
    early = max(len(names) - 1, 1)

    def last(i, o, s):
        mid(o, s, early, len(names))
        finish(o, s)

    return CommJob(arrays, [SDS(a.shape, a.dtype) for a in arrays], True,
                   lambda i, o, s: start(o, s), last, sems, mid=lambda i, o, s: mid(o, s, 0, early))


def gather_rows_job(placed):
    widths = [a.shape[1] // 4 for a in placed]

    def copies(outs, sems):
        x, y, c, chips = _place()
        mine = 2 * x + y
        cps = []
        for w, ref in enumerate(outs):
            own = ref.at[:, pl.ds(mine * widths[w], widths[w])]
            for j, chip in enumerate(chips):
                theirs = ref.at[:, pl.ds((2 * chip[0] + chip[1]) * widths[w], widths[w])]
                kw = dict(send_sem=sems[0].at[w, j], recv_sem=sems[1].at[w, j], device_id=(*chip, c), device_id_type=MESH)
                cps.append((pltpu.make_async_remote_copy(src_ref=own, dst_ref=own, **kw),
                            pltpu.make_async_remote_copy(src_ref=theirs, dst_ref=theirs, **kw)))
        return cps

    def start(ins, outs, sems):
        for send, _ in copies(outs, sems):
            send.start()

    def finish(ins, outs, sems):
        for send, recv in copies(outs, sems):
            recv.wait_recv()
            send.wait_send()

    return CommJob(placed, [SDS(a.shape, a.dtype) for a in placed], True, start, finish,
                   [pltpu.SemaphoreType.DMA((len(placed), 3))] * 2)


def merge_jobs(a, b):
    assert a.inplace and b.inplace
    na, nb_, nsa = len(a.ins), len(b.ins), len(a.sems)

    def phase(which):
        def run(ins, outs, sems):
            for job, i, o, s in ((a, ins[:na], outs[:na], sems[:nsa]), (b, ins[na:], outs[na:], sems[nsa:])):
                fn = getattr(job, which)
                if fn is not None:
                    fn(i, o, s)
        return run

    merged = CommJob(a.ins + b.ins, a.out_shapes + b.out_shapes, True, phase("start"), phase("finish"), a.sems + b.sems,
                     mid=phase("mid"))
    merged.parts = (a, b, na)
    return merged


def split_landed(merged):
    a, b, na = merged.parts
    a.landed, b.landed = merged.landed[:na], merged.landed[na:]


def pair_exchange_job(grads, names):
    shapes = [grads[n].shape for n in names]

    def copies(ins, outs, sems):
        x, y, c, _ = _place()
        return [pltpu.make_async_remote_copy(
            src_ref=ins[w].at[:, pl.ds((1 - c) * (shapes[w][1] // 2), shapes[w][1] // 2), :], dst_ref=outs[w],
            send_sem=sems[0].at[w], recv_sem=sems[1].at[w], device_id=(x, y, 1 - c), device_id_type=MESH)
            for w in range(len(names))]

    def start(ins, outs, sems):
        for cp in copies(ins, outs, sems):
            cp.start()

    def finish(ins, outs, sems):
        for cp in copies(ins, outs, sems):
            cp.wait()

    return CommJob([grads[n] for n in names], [SDS((4, s[1] // 2, s[2]), F32) for s in shapes], False, start, finish,
                   [pltpu.SemaphoreType.DMA((len(names),))] * 2)


def chip_exchange_job(p16, names):
    shapes = [p16[n].shape for n in names]

    def copies(ins, outs, sems):
        x, y, c, chips = _place()
        return [pltpu.make_async_remote_copy(
            src_ref=ins[w].at[2 * chip[0] + chip[1]], dst_ref=outs[w].at[j],
            send_sem=sems[0].at[w, j], recv_sem=sems[1].at[w, j], device_id=(*chip, c), device_id_type=MESH)
            for w in range(len(names)) for j, chip in enumerate(chips)]

    def start(ins, outs, sems):
        for cp in copies(ins, outs, sems):
            cp.start()

    def finish(ins, outs, sems):
        for cp in copies(ins, outs, sems):
            cp.wait()

    return CommJob([p16[n] for n in names], [SDS((3,) + tuple(s[1:]), BF16) for s in shapes], False, start, finish,
                   [pltpu.SemaphoreType.DMA((len(names), 3))] * 2)


def half_exchange_job(full, names):
    shapes = [full[n].shape for n in names]

    def copies(outs, sems):
        x, y, c, _ = _place()
        cps = []
        for w in range(len(names)):
            Rh = shapes[w][0] // 2
            rows = outs[w].at[pl.ds(c * Rh, Rh), :]
            other = outs[w].at[pl.ds((1 - c) * Rh, Rh), :]
            cps.append((pltpu.make_async_remote_copy(src_ref=rows, dst_ref=rows, send_sem=sems[0].at[w], recv_sem=sems[1].at[w],
                                                     device_id=(x, y, 1 - c), device_id_type=MESH),
                        pltpu.make_async_remote_copy(src_ref=other, dst_ref=other, send_sem=sems[0].at[w], recv_sem=sems[1].at[w],
                                                     device_id=(x, y, 1 - c), device_id_type=MESH)))
        return cps

    def start(ins, outs, sems):
        for send, _ in copies(outs, sems):
            send.start()

    def finish(ins, outs, sems):
        for send, recv in copies(outs, sems):
            recv.wait_recv()
            send.wait_send()

    arrays = [full[n] for n in names]
    return CommJob(arrays, [SDS(a.shape, a.dtype) for a in arrays], True, start, finish,
                   [pltpu.SemaphoreType.DMA((len(names),))] * 2)


def allreduce_small(parts, name, job=None):
    n = len(parts)
    nji = len(job.ins) if job else 0
    njo = len(job.out_shapes) if job else 0

    def body(*refs):
        ins, jins = refs[:n], refs[n:n + nji]
        outs, jouts = refs[n + nji:2 * n + nji], refs[2 * n + nji:2 * n + nji + njo]
        bufs = refs[2 * n + nji + njo:3 * n + nji + njo]
        send, recv = refs[3 * n + nji + njo:3 * n + nji + njo + 2]
        jsems = refs[3 * n + nji + njo + 2:]
        x, y, c, _ = _place()
        me = 4 * x + 2 * y + c
        if job is not None:
            job.run(0, jins, jouts, jsems)
        sends = []
        for p in range(n):
            for k in range(1, 8):
                peer = (x ^ (k >> 2), y ^ ((k >> 1) & 1), c ^ (k & 1))
                cp = pltpu.make_async_remote_copy(src_ref=ins[p], dst_ref=bufs[p].at[k], send_sem=send.at[p, k - 1],
                                                  recv_sem=recv.at[p, k - 1], device_id=peer, device_id_type=MESH)
                cp.start()
                sends.append(cp)
            bufs[p][0] = ins[p][...]
        for cp in sends:
            cp.wait_recv()
        for p in range(n):
            acc = bufs[p][me]
            for d in range(1, 8):
                acc = acc + bufs[p][d ^ me]
            outs[p][...] = acc
        for cp in sends:
            cp.wait_send()
        if job is not None:
            job.run(1, jins, jouts, jsems)
            job.run(2, jins, jouts, jsems)

    vm = pl.BlockSpec(memory_space=pltpu.VMEM)
    res = pl.pallas_call(
        body, name=name, in_specs=[vm] * n + [ANY] * nji, out_specs=[vm] * n + [ANY] * njo,
        out_shape=[SDS(a.shape, F32) for a in parts] + (list(job.out_shapes) if job else []),
        input_output_aliases={n + i: n + i for i in range(nji)} if (job and job.inplace) else {},
        scratch_shapes=[pltpu.VMEM((8,) + tuple(a.shape), F32) for a in parts] + [pltpu.SemaphoreType.DMA((n, 7))] * 2
        + (list(job.sems) if job else []),
    )(*parts, *(job.ins if job else []))
    if job is not None:
        job.landed = list(res[n:])
    return list(res[:n])


def adamw_multi(ws, gs, ms, vs, name):
    n = len(ws)

    def body(*refs):
        for p in range(n):
            _adam_update(*(refs[q * n + p] for q in range(7)))

    vm = pl.BlockSpec(memory_space=pltpu.VMEM)
    outs = pl.pallas_call(body, name=name, in_specs=[vm] * (4 * n), out_specs=[vm] * (3 * n),
                          out_shape=[SDS(a.shape, F32) for a in ws] * 3)(*ws, *gs, *ms, *vs)
    return outs[:n], outs[n:2 * n], outs[2 * n:]


def pair_sum(gs, xhs, sel_arr, name):
    n = len(gs)
    _, R, C = gs[0].shape
    Rh = R // 2

    def body(sel_ref, *refs):
        for p in range(n):
            g_ref, x_ref, p32_ref, p16_ref = (refs[q * n + p] for q in range(4))
            s = g_ref[...] + x_ref[...]
            p16_ref[...] = s.astype(BF16)

            @pl.when(pl.program_id(0) == sel_ref[0])
            def _():
                p32_ref[...] = s

    same = pl.BlockSpec((None, Rh, C), lambda s, sel: (s, 0, 0))
    grid_spec = pltpu.PrefetchScalarGridSpec(
        num_scalar_prefetch=1, grid=(4,),
        in_specs=[pl.BlockSpec((None, Rh, C), lambda s, sel: (s, sel[1], 0))] * n + [same] * n,
        out_specs=[pl.BlockSpec((Rh, C), lambda s, sel: (0, 0))] * n + [same] * n)
    outs = pl.pallas_call(body, grid_spec=grid_spec, name=name,
                          out_shape=[SDS((Rh, C), F32)] * n + [SDS((4, Rh, C), BF16)] * n,
                          compiler_params=_cp("arbitrary"))(sel_arr, *gs, *xhs)
    return outs[:n], outs[n:]


def chip_sum(p32s, ys, sel_arr, name):
    n = len(p32s)
    Rh, C = p32s[0].shape

    def body(s_ref, *refs):
        for p in range(n):
            p_ref, y_ref, o_ref = (refs[q * n + p] for q in range(3))
            o_ref[...] = ((p_ref[...] + y_ref[0].astype(F32)) + y_ref[1].astype(F32)) + y_ref[2].astype(F32)

    grid_spec = pltpu.PrefetchScalarGridSpec(
        num_scalar_prefetch=1, grid=(1,),
        in_specs=[pl.BlockSpec((Rh, C), lambda i, s: (0, 0))] * n + [pl.BlockSpec((3, Rh, C), lambda i, s: (0, 0, 0))] * n,
        out_specs=[pl.BlockSpec((Rh, C), lambda i, s: (s[1], 0))] * n)
    return pl.pallas_call(body, grid_spec=grid_spec, name=name, out_shape=[SDS((2 * Rh, C), F32)] * n,
                          compiler_params=_cp("arbitrary"))(sel_arr, *p32s, *ys)


class StepComm:
    FIRST = ("w_in",)
    SQUARE = ("w_out", "w_xq", "w_xk", "w_xv", "w_xo")
    FFN = ("w_up", "w_down")
    PLAN = {
        "proj": ("gather", SQUARE[:3]), "attn_fwd": ("gather", SQUARE[3:] + FFN),
        "d_h3": ("pair", FFN), "mixer_post_bwd_a": ("pair", SQUARE), "attn_bwd": ("chip", FFN + SQUARE),
        "dw_in": ("half", FFN + SQUARE), "first_update": ("pair", FIRST), "d_h1": ("chip", FIRST),
    }

    def __init__(self, placed, conv_placed, sel_arr):
        self.placed, self.conv_placed, self.sel_arr = placed, conv_placed, sel_arr
        self.g, self.p32, self.p16, self.full, self.done, self.jobs = {}, {}, {}, {}, {}, {}

    def job(self, tag, gb=None):
        if tag == "norm_mix":
            self.jobs[tag] = merge_jobs(gather_job(self.placed, self.FIRST), gather_rows_job(self.conv_placed))
            return self.jobs[tag]
        kind, names = self.PLAN[tag]
        if kind == "gather":
            j = gather_job(self.placed, names)
        elif kind == "pair":
            for n in names:
                self.g[n] = gb[n] if gb[n].ndim == 3 else gb[n].reshape(4, gb[n].shape[0] // 4, gb[n].shape[1])
            j = pair_exchange_job(self.g, names)
        elif kind == "chip":
            j = chip_exchange_job(self.p16, names)
        else:
            j = half_exchange_job(self.full, names)
        self.jobs[tag] = j
        return j

    def landed(self, tag, wb=None, ws=None):
        if tag == "norm_mix":
            split_landed(self.jobs[tag])
            first, conv, _ = self.jobs[tag].parts
            wb.update(zip(self.FIRST, first.landed))
            ws.update((n, a[:3]) for n, a in zip(SMALL_SHARDED, conv.landed))
            return
        kind, names = self.PLAN[tag]
        got = dict(zip(names, self.jobs[tag].landed))
        if kind == "gather":
            wb.update(got)
        elif kind == "half":
            self.done.update(got)
        for group in same_shape_groups(names, got):
            if kind == "pair":
                p32s, p16s = pair_sum([self.g[n] for n in group], [got[n] for n in group], self.sel_arr, "pair_sum_" + group[0])
                self.p32.update(zip(group, p32s))
                self.p16.update(zip(group, p16s))
            elif kind == "chip":
                fulls = chip_sum([self.p32[n] for n in group], [got[n] for n in group], self.sel_arr, "chip_sum_" + group[0])
                self.full.update(zip(group, fulls))

    def first_update(self, gb):
        self.first_result = self.update_fn(self.job("first_update", gb))
        self.landed("first_update")

    def finish(self, gb):
        return self.done

    def last_job(self):
        self.jobs["end_half"] = half_exchange_job(self.full, self.FIRST)
        return self.jobs["end_half"]

    def take_last(self):
        self.done.update(zip(self.FIRST, self.jobs["end_half"].landed))


class NoComm:
    def job(self, tag, gb=None):
        return None

    def landed(self, tag, wb=None, ws=None):
        pass

    def first_update(self, gb):
        pass

    def finish(self, gb):
        return gb


SMALL = ("rel_bias", "g_mix", "w_short_conv", "g_attn_out", "g_conv_out", "g_xattn", "g_mem", "g_ffn",
         "w_ffn_conv", "b_ffn_conv", "g_final")
SMALL_FULL = {"rel_bias": (8, 32), "g_mix": (1, 1024), "w_short_conv": (3, 512), "g_attn_out": (1, 512), "g_conv_out": (1, 512),
              "g_xattn": (1, 1024), "g_mem": (1, 1024), "g_ffn": (1, 1024), "w_ffn_conv": (3, 5632), "b_ffn_conv": (1, 5632),
              "g_final": (1024,)}
SMALL_SHARDED = ("w_short_conv", "w_ffn_conv")


def _pack(parts):
    flat = jnp.concatenate([p.reshape(-1) for p in parts])
    rows = -(-flat.shape[0] // 1024) * 8
    return jnp.pad(flat, (0, rows * 128 - flat.shape[0])).reshape(rows, 128)


def _unpack(packed, names):
    flat, out, off = packed.reshape(-1), {}, 0
    for n in names:
        size = int(np.prod(SMALL_FULL[n]))
        out[n] = flat[off:off + size].reshape(SMALL_FULL[n])
        off += size
    return out


def kernel(x, mem, rel_bias, g_mix, w_in, w_short_conv, g_attn_out, g_conv_out, w_out, g_xattn, g_mem, w_xq, w_xk, w_xv, w_xo, g_ffn, w_up, w_ffn_conv, b_ffn_conv, w_down, g_final, loss_target, m_rel_bias, m_g_mix, m_w_in, m_w_short_conv, m_g_attn_out, m_g_conv_out, m_w_out, m_g_xattn, m_g_mem, m_w_xq, m_w_xk, m_w_xv, m_w_xo, m_g_ffn, m_w_up, m_w_ffn_conv, m_b_ffn_conv, m_w_down, m_g_final, v_rel_bias, v_g_mix, v_w_in, v_w_short_conv, v_g_attn_out, v_g_conv_out, v_w_out, v_g_xattn, v_g_mem, v_w_xq, v_w_xk, v_w_xv, v_w_xo, v_g_ffn, v_w_up, v_w_ffn_conv, v_b_ffn_conv, v_w_down, v_g_final):
    names = ("rel_bias", "g_mix", "w_in", "w_short_conv", "g_attn_out", "g_conv_out", "w_out", "g_xattn", "g_mem", "w_xq",
             "w_xk", "w_xv", "w_xo", "g_ffn", "w_up", "w_ffn_conv", "b_ffn_conv", "w_down", "g_final")
    W = dict(zip(names, (rel_bias, g_mix, w_in, w_short_conv, g_attn_out, g_conv_out, w_out, g_xattn, g_mem, w_xq, w_xk, w_xv,
                         w_xo, g_ffn, w_up, w_ffn_conv, b_ffn_conv, w_down, g_final)))
    M = dict(zip(names, (m_rel_bias, m_g_mix, m_w_in, m_w_short_conv, m_g_attn_out, m_g_conv_out, m_w_out, m_g_xattn, m_g_mem,
                         m_w_xq, m_w_xk, m_w_xv, m_w_xo, m_g_ffn, m_w_up, m_w_ffn_conv, m_b_ffn_conv, m_w_down, m_g_final)))
    V = dict(zip(names, (v_rel_bias, v_g_mix, v_w_in, v_w_short_conv, v_g_attn_out, v_g_conv_out, v_w_out, v_g_xattn, v_g_mem,
                         v_w_xq, v_w_xk, v_w_xv, v_w_xo, v_g_ffn, v_w_up, v_w_ffn_conv, v_b_ffn_conv, v_w_down, v_g_final)))
    xi, yi, ci = lax.axis_index("x"), lax.axis_index("y"), lax.axis_index("c")
    mine = 2 * xi + yi
    mine_arr = jnp.reshape(mine, (1,)).astype(jnp.int32)
    sel_arr = jnp.stack([mine, ci]).astype(jnp.int32)
    conv_placed = []
    for n in SMALL_SHARDED:
        shard = W[n][0]
        conv_placed.append(lax.dynamic_update_slice(jnp.zeros((8, SMALL_FULL[n][1]), F32), shard, (0, mine * shard.shape[1])))
    shards = {n: W[n][0] for n in BIG}
    placed = {}
    for group in same_shape_groups(BIG, shards):
        outs = place_shards([shards[n] for n in group], mine_arr, group[0] in COL_SHARDED, "place_" + group[0])
        placed.update(zip(group, outs))
    comm = StepComm(placed, conv_placed, sel_arr)
    ws = {n: W[n] for n in SMALL if n not in SMALL_SHARDED}

    def update(group, job=None):
        return adamw([W[n][0] for n in group], [comm.done[n] for n in group], [M[n][0] for n in group],
                     [V[n][0] for n in group], "adamw_" + group[0], job=job)

    comm.update_fn = lambda job: update(["w_up"], job)
    loss, grad_x, gfull, gs = local_step(x[0], mem[0], loss_target[0], None, ws, comm)

    def as2d(a):
        return a.reshape(1, -1) if a.ndim == 1 else a

    reduced = allreduce_small([as2d(gs[n]) for n in SMALL] + [loss], "reduce_small", job=comm.last_job())
    comm.take_last()
    gsm = dict(zip(SMALL, reduced[:-1]))
    loss = reduced[-1][0, 0]

    grads, delta, new_m, new_v = {}, {}, {}, {}
    for group in same_shape_groups(BIG, gfull):
        ds, nms, nvs = comm.first_result if group == ["w_up"] else update(group)
        for n, d, nm, nv in zip(group, ds, nms, nvs):
            grads[n], delta[n], new_m[n], new_v[n] = gfull[n][None], d[None], nm[None], nv[None]
    for n in SMALL_SHARDED:
        wid = W[n].shape[2]
        gsm[n] = lax.dynamic_slice(gsm[n], (0, mine * wid), (3, wid))

    def own(a, n):
        return a[0] if n in SMALL_SHARDED else as2d(a)

    d, nm, nv = adamw_multi([own(W[n], n) for n in SMALL], [gsm[n] for n in SMALL], [own(M[n], n) for n in SMALL],
                            [own(V[n], n) for n in SMALL], "adamw_small")
    for i, n in enumerate(SMALL):
        shape = W[n].shape
        grads[n], delta[n], new_m[n], new_v[n] = (a.reshape(shape) for a in (gsm[n], d[i], nm[i], nv[i]))
    return (loss, grad_x[None], *[grads[n] for n in names], *[delta[n] for n in names],
            *[new_m[n] for n in names], *[new_v[n] for n in names])
```

```python
import math

import numpy as np
import jax
import jax.numpy as jnp
from jax import lax
from jax.experimental import pallas as pl
from jax.experimental.pallas import tpu as pltpu

F32 = jnp.float32
BF16 = jnp.bfloat16
SDS = jax.ShapeDtypeStruct
MESH = pl.DeviceIdType.MESH

D_MODEL = 1024
ATTN_W = 512
N_HEADS = 8
HEAD_DIM = 64
HALF_HEAD = HEAD_DIM // 2
Q_SCALE = HEAD_DIM ** -0.5
BLK = 128
PATTERNS = ((128, 1), (512, 4), (2048, 16))
N_BUCKETS = 32
BUCKET_MAX_DIST = 2048
D_FF = 2816
N_MEM_HEADS = 4
MEM_HD = 256
EPS = 1e-6
NEG = -1e30
VMEM_LIMIT = 56 * 1024 * 1024

ADAM_LR, ADAM_B1, ADAM_B2, ADAM_EPS, ADAM_WD, ADAM_STEP = 0.001, 0.9, 0.999, 1e-08, 0.01, 10


def _cp(*sem):
    return pltpu.CompilerParams(dimension_semantics=sem, vmem_limit_bytes=VMEM_LIMIT)


def _dot(a, b):
    return jnp.dot(a, b, preferred_element_type=F32)


def _dot_nt(a, b):
    return lax.dot_general(a, b, (((1,), (1,)), ((), ())), preferred_element_type=F32)


def _dot_tn(a, b):
    return lax.dot_general(a, b, (((0,), (0,)), ((), ())), preferred_element_type=F32)


def _rsum(x):
    return jnp.sum(x, axis=1, keepdims=True)


def rmsnorm_fwd(x, g, name, job=None):
    S, Dm = x.shape
    tm = min(S, 512)

    def body(x_ref, g_ref, o_ref):
        xv = x_ref[...]
        r = lax.rsqrt(jnp.mean(xv * xv, axis=1, keepdims=True) + EPS)
        o_ref[...] = (xv * r * g_ref[...]).astype(o_ref.dtype)

    return _pcall(body, [x, g], grid=(S // tm,), name=name,
                  in_specs=[pl.BlockSpec((tm, Dm), lambda i: (i, 0)), pl.BlockSpec((1, Dm), lambda i: (0, 0))],
                  out_specs=[pl.BlockSpec((tm, Dm), lambda i: (i, 0))], out_shape=[SDS((S, Dm), BF16)],
                  sem=("parallel",), job=job)[0]


def rmsnorm_gain_grad(x, dh, name):
    S, Dm = x.shape
    tm = min(S, 512)

    def body(x_ref, dh_ref, dg_ref):
        xv = x_ref[...]
        xh = xv * lax.rsqrt(jnp.mean(xv * xv, axis=1, keepdims=True) + EPS)

        @pl.when(pl.program_id(0) == 0)
        def _():
            dg_ref[...] = jnp.zeros_like(dg_ref)

        dg_ref[...] += jnp.sum(dh_ref[...].astype(F32) * xh, axis=0, keepdims=True)

    row = pl.BlockSpec((tm, Dm), lambda i: (i, 0))
    return pl.pallas_call(
        body, grid=(S // tm,), name=name, in_specs=[row, row], out_specs=pl.BlockSpec((1, Dm), lambda i: (0, 0)),
        out_shape=SDS((1, Dm), F32), compiler_params=_cp("arbitrary"))(x, dh)


def _col_spec(bw, tn, rows, row_of, col_of):
    if bw is None:
        return pl.BlockSpec((rows, tn), lambda *g: (row_of(*g), col_of(*g)))
    if tn % bw == 0:
        return pl.BlockSpec((tn // bw, rows, bw), lambda *g: (col_of(*g), row_of(*g), 0))
    per = bw // tn
    return pl.BlockSpec((None, rows, tn), lambda *g: (col_of(*g) // per, row_of(*g), col_of(*g) % per))


def _read_cols(ref, bw, tn):
    if bw is None or tn % bw != 0:
        return ref[...]
    if tn == bw:
        return ref[0]
    return jnp.concatenate([ref[q] for q in range(tn // bw)], axis=1)


def _write_cols(ref, bw, tn, val):
    if bw is None or tn % bw != 0:
        ref[...] = val.astype(ref.dtype)
    else:
        for q in range(tn // bw):
            ref[q] = val[:, q * bw:(q + 1) * bw].astype(ref.dtype)


def mm_nn(a, b, name, *, out_dtype=F32, out_bw=None, tm=1024, tn=512, job=None):
    M, K = a.shape
    N = b.shape[1]
    tm = min(tm, M)

    def body(a_ref, b_ref, o_ref):
        _write_cols(o_ref, out_bw, tn, _dot(a_ref[...].astype(BF16), b_ref[...]))

    ri, ci = (lambda i, j: i), (lambda i, j: j)
    in_specs = [pl.BlockSpec((tm, K), lambda i, j: (i, 0)), pl.BlockSpec((K, tn), lambda i, j: (0, j))]
    oshape = (M, N) if out_bw is None else (N // out_bw, M, out_bw)
    return _pcall(body, [a, b], grid=(M // tm, N // tn), name=name, in_specs=in_specs,
                  out_specs=[_col_spec(out_bw, tn, tm, ri, ci)], out_shape=[SDS(oshape, out_dtype)],
                  sem=("parallel", "parallel"), job=job)[0]


def mm_nn_rows(a, b, res, g, name, *, target=None, tm=512):
    M, K = a.shape
    N = b.shape[1]

    def body(*refs):
        a_ref, b_ref, r_ref, g_ref = refs[:4]
        xv = r_ref[...] + _dot(a_ref[...].astype(BF16), b_ref[...])
        r = lax.rsqrt(jnp.mean(xv * xv, axis=1, keepdims=True) + EPS)
        xh = xv * r
        if target is None:
            x_ref, h_ref = refs[4:]
            x_ref[...] = xv
            h_ref[...] = (xh * g_ref[...]).astype(BF16)
            return
        t_ref, loss_ref, dx_ref, dg_ref = refs[4:]
        err = xh * g_ref[...] - t_ref[...]
        dy = err / N
        gd = dy * g_ref[...]
        dx_ref[...] = r * (gd - xh * jnp.mean(gd * xh, axis=1, keepdims=True))

        @pl.when(pl.program_id(0) == 0)
        def _():
            dg_ref[...] = jnp.zeros_like(dg_ref)
            loss_ref[...] = jnp.zeros_like(loss_ref)

        dg_ref[...] += _colsum(dy * xh)
        loss_ref[...] += 0.5 * _colsum(jnp.mean(err * err, axis=1, keepdims=True))

    row = pl.BlockSpec((tm, N), lambda i: (i, 0))
    vec = pl.BlockSpec((1, N), lambda i: (0, 0))
    in_specs = [pl.BlockSpec((tm, K), lambda i: (i, 0)), pl.BlockSpec((K, N), lambda i: (0, 0)), row, vec]
    if target is None:
        return pl.pallas_call(body, grid=(M // tm,), name=name, in_specs=in_specs, out_specs=[row, row],
                              out_shape=[SDS((M, N), F32), SDS((M, N), BF16)], compiler_params=_cp("parallel"))(a, b, res, g)
    one = pl.BlockSpec((1, 1), lambda i: (0, 0))
    return pl.pallas_call(body, grid=(M // tm,), name=name, in_specs=in_specs + [row], out_specs=[one, row, vec],
                          out_shape=[SDS((1, 1), F32), SDS((M, N), F32), SDS((1, N), F32)],
                          compiler_params=_cp("arbitrary"))(a, b, res, g, target)


def mm_nt(a, a_bw, b, name, *, res=None, out_dtype=F32, tm=1024, tn=512, job=None):
    if a_bw is None:
        M, K = a.shape
    else:
        M, K = a.shape[1], a.shape[0] * a_bw
    N = b.shape[0]
    tm = min(tm, M)

    def body(*refs):
        if res is None:
            a_ref, b_ref, o_ref = refs
        else:
            a_ref, b_ref, r_ref, o_ref = refs
        av = _read_cols(a_ref, a_bw, K).astype(BF16)
        acc = _dot_nt(av, b_ref[...])
        if res is not None:
            acc = acc + r_ref[...]
        o_ref[...] = acc.astype(o_ref.dtype)

    in_specs = [_col_spec(a_bw, K, tm, lambda i, j: i, lambda i, j: 0), pl.BlockSpec((tn, K), lambda i, j: (j, 0))]
    args = [a, b]
    if res is not None:
        in_specs.append(pl.BlockSpec((tm, tn), lambda i, j: (i, j)))
        args.append(res)
    return _pcall(body, args, grid=(M // tm, N // tn), name=name, in_specs=in_specs,
                  out_specs=[pl.BlockSpec((tm, tn), lambda i, j: (i, j))], out_shape=[SDS((M, N), out_dtype)],
                  sem=("parallel", "parallel"), job=job)[0]


def mm_nt_norm_bwd(a, a_bw, b, x, g, res, name, *, tm=512, job=None):
    if a_bw is None:
        M, K = a.shape
    else:
        M, K = a.shape[1], a.shape[0] * a_bw
    N = b.shape[0]

    def body(a_ref, b_ref, x_ref, g_ref, r_ref, dx_ref, dg_ref):
        dh = _dot_nt(_read_cols(a_ref, a_bw, K).astype(BF16), b_ref[...])
        xv = x_ref[...]
        r = lax.rsqrt(jnp.mean(xv * xv, axis=1, keepdims=True) + EPS)
        xh = xv * r
        gd = dh * g_ref[...]
        dx_ref[...] = r_ref[...] + r * (gd - xh * jnp.mean(gd * xh, axis=1, keepdims=True))

        @pl.when(pl.program_id(0) == 0)
        def _():
            dg_ref[...] = jnp.zeros_like(dg_ref)

        dg_ref[...] += jnp.sum(dh * xh, axis=0, keepdims=True)

    row = pl.BlockSpec((tm, N), lambda i: (i, 0))
    vec = pl.BlockSpec((1, N), lambda i: (0, 0))
    return _pcall(body, [a, b, x, g, res], grid=(M // tm,), name=name,
                  in_specs=[_col_spec(a_bw, K, tm, lambda i: i, lambda i: 0), pl.BlockSpec((N, K), lambda i: (0, 0)), row, vec, row],
                  out_specs=[row, vec], out_shape=[SDS((M, N), F32), SDS((1, N), F32)], sem=("arbitrary",), job=job)


def mm_tn(a, b, b_bw, name, *, shards=None, tm=1024, tn=1024, ts=1024, job=None):
    S, Ka = a.shape
    N = b.shape[1] if b_bw is None else b.shape[0] * b_bw
    ts = min(ts, S)
    tm = min(tm, Ka)

    def body(a_ref, b_ref, o_ref):
        @pl.when(pl.program_id(2) == 0)
        def _():
            o_ref[...] = jnp.zeros_like(o_ref)

        bv = _read_cols(b_ref, b_bw, tn).astype(BF16)
        o_ref[...] += _dot_tn(a_ref[...].astype(BF16), bv)

    in_specs = [pl.BlockSpec((ts, tm), lambda i, j, k: (k, i)),
                _col_spec(b_bw, tn, ts, lambda i, j, k: k, lambda i, j, k: j)]
    if shards is None:
        out_spec, oshape = pl.BlockSpec((tm, tn), lambda i, j, k: (i, j)), (Ka, N)
    else:
        per = (N // shards) // tn
        out_spec = pl.BlockSpec((None, tm, tn), lambda i, j, k: (j // per, i, j % per))
        oshape = (shards, Ka, N // shards)
    return _pcall(body, [a, b], grid=(Ka // tm, N // tn, S // ts), name=name, in_specs=in_specs, out_specs=[out_spec],
                  out_shape=[SDS(oshape, F32)], sem=("parallel", "parallel", "arbitrary"), job=job)[0]


KEYS = 2 * BLK


def _bucket_tables():
    out = np.zeros((3, BLK, KEYS), np.int32)
    qi = np.arange(BLK)[:, None]
    kj = np.arange(KEYS)[None, :]
    for p, (win, dil) in enumerate(PATTERNS):
        w = win // dil
        assert w == BLK
        steps = qi + w - kj
        valid = (steps >= 0) & (steps <= w)
        dist = np.clip(steps, 0, w) * dil
        dd = np.maximum(dist, 1).astype(np.float32)
        exact = N_BUCKETS // 2
        large = exact + (np.log(dd / np.float32(exact)) / np.float32(math.log(BUCKET_MAX_DIST / exact))
                         * np.float32(N_BUCKETS - exact)).astype(np.int32)
        large = np.minimum(large, N_BUCKETS - 1)
        out[p] = np.where(valid, np.where(dist < exact, dist, large), -1)
    return out


def bias_tables(rel_bias):
    bkt = jnp.asarray(_bucket_tables())

    def body(rb_ref, bkt_ref, o_ref):
        h = pl.program_id(1)
        bk = bkt_ref[...]
        acc = jnp.full((BLK, KEYS), NEG, F32)
        for b in range(N_BUCKETS):
            acc = jnp.where(bk == b, rb_ref[h, b], acc)
        o_ref[...] = acc

    return pl.pallas_call(
        body, grid=(3, N_HEADS), name="bias_tables",
        in_specs=[pl.BlockSpec(memory_space=pltpu.SMEM),
                  pl.BlockSpec((None, BLK, KEYS), lambda p, h: (p, 0, 0))],
        out_specs=pl.BlockSpec((None, None, BLK, KEYS), lambda p, h: (p, h, 0, 0)),
        out_shape=SDS((3, N_HEADS, BLK, KEYS), F32), compiler_params=_cp("parallel", "parallel"))(rel_bias, bkt)


def bias_tables_bwd(ds_sum):
    bkt = jnp.asarray(_bucket_tables())

    def body(ds_ref, bkt_ref, o_ref):
        h = pl.program_id(0)

        @pl.when(h == 0)
        def _():
            o_ref[...] = jnp.zeros_like(o_ref)

        rows = lax.broadcasted_iota(jnp.int32, (N_HEADS, N_BUCKETS), 0)
        cols = lax.broadcasted_iota(jnp.int32, (N_HEADS, N_BUCKETS), 1)
        acc = jnp.zeros((N_HEADS, N_BUCKETS), F32)
        for b in range(N_BUCKETS):
            t = jnp.zeros((BLK, KEYS), F32)
            for p in range(3):
                t = t + jnp.where(bkt_ref[p] == b, ds_ref[p], 0.0)
            tot = jnp.sum(_rsum(t), axis=0, keepdims=True)
            acc = acc + jnp.where((rows == h) & (cols == b), tot, 0.0)
        o_ref[...] += acc

    return pl.pallas_call(
        body, grid=(N_HEADS,), name="bias_tables_bwd",
        in_specs=[pl.BlockSpec((3, None, BLK, KEYS), lambda h: (0, h, 0, 0)),
                  pl.BlockSpec((3, BLK, KEYS), lambda h: (0, 0, 0))],
        out_specs=pl.BlockSpec((N_HEADS, N_BUCKETS), lambda h: (0, 0)),
        out_shape=SDS((N_HEADS, N_BUCKETS), F32), compiler_params=_cp("arbitrary"))(ds_sum, bkt)


def _rows(start, dil):
    if dil == 1:
        return pl.ds(pl.multiple_of(start, BLK), BLK)
    return pl.ds(start, BLK, stride=dil)


GRP = 8


def _group_rows(i, dil, S):
    nb = S // (BLK * dil)
    run = min(nb, GRP)
    chunks = nb // run
    res0, b0 = (i // chunks) * (GRP // run), (i % chunks) * run
    cur = [(b0 + j % run) * (BLK * dil) + res0 + j // run for j in range(GRP)]
    t = lax.broadcasted_iota(jnp.int32, (GRP, 1, 1), 0)
    if run == nb:
        before, pen = None, jnp.where(t % run == 0, NEG, 0.0)
    else:
        before = _rows(jnp.maximum(b0 - 1, 0) * (BLK * dil) + res0, dil)
        pen = jnp.where((t == 0) & (b0 == 0), NEG, 0.0)
    return [_rows(s, dil) for s in cur], before, pen


def _load_group(ref, rows):
    return jnp.stack([ref[r, :] for r in rows])


def _with_prev(ref, before, cur_blocks):
    first = cur_blocks[:1] if before is None else ref[before, :][None].astype(cur_blocks.dtype)
    return jnp.concatenate([first, cur_blocks[:-1]], axis=0)


def _bdot_nt(a, b):
    return lax.dot_general(a, b, (((2,), (2,)), ((0,), (0,))), preferred_element_type=F32)


def _bdot(a, b):
    return lax.dot_general(a, b, (((2,), (1,)), ((0,), (0,))), preferred_element_type=F32)


def _bdot_tn(a, b):
    return lax.dot_general(a, b, (((1,), (1,)), ((0,), (0,))), preferred_element_type=F32)


def _lsum(x):
    return jnp.sum(x, axis=-1, keepdims=True)


def _widen(src, dst):
    S = src.shape[1]

    def chunk(i, carry):
        rows = pl.ds(pl.multiple_of(i * 512, 512), 512)
        for a in range(3):
            dst[a, rows, :] = src[a, rows, :].astype(F32)
        return carry

    lax.fori_loop(0, S // 512, chunk, 0)


def attn_fwd(qkv, bias, job=None):
    S = qkv.shape[3]
    nblk = S // BLK
    car = Carried(job, 2, 2, 1, 4)

    def body(*refs):
        (qkv_in, bias_ref, o_ref, lse_ref, qkv_ref), jrefs = car.split(refs)
        car.phase(0, pl.program_id(0), jrefs)
        _widen(qkv_in, qkv_ref)
        lane = lax.broadcasted_iota(jnp.int32, (GRP, BLK, BLK), 2)
        lo = lane < HEAD_DIM
        masks = (lo, jnp.logical_not(lo))
        lov = lax.broadcasted_iota(jnp.int32, (GRP, KEYS, BLK), 2) < HEAD_DIM
        vmasks = (lov, jnp.logical_not(lov))
        prev_keys = lax.broadcasted_iota(jnp.int32, (1, 1, KEYS), 2) < BLK
        q_ref, k_ref, v_ref = qkv_ref.at[0], qkv_ref.at[1], qkv_ref.at[2]
        for p, (_, dil) in enumerate(PATTERNS):
            def step(i, carry, p=p, dil=dil):
                rc, before, pen = _group_rows(i, dil, S)
                q2 = _load_group(q_ref, rc) * Q_SCALE
                kc = _load_group(k_ref, rc).astype(BF16)
                keys = jnp.concatenate([_with_prev(k_ref, before, kc), kc], axis=1)
                vc = _load_group(v_ref, rc)
                vals = jnp.concatenate([_with_prev(v_ref, before, vc), vc], axis=1)
                pen = jnp.where(prev_keys, pen, 0.0)
                if p > 0:
                    acc_old, ml_old = _load_group(o_ref, rc), _load_group(lse_ref, rc)
                pv = jnp.zeros((GRP, BLK, BLK), F32)
                m_new, l_new, alpha = [], [], []
                for h in range(2):
                    qh = jnp.where(masks[h], q2, 0.0).astype(BF16)
                    s = _bdot_nt(qh, keys) + bias_ref[p, h][None] + pen
                    mn = jnp.max(s, axis=-1, keepdims=True)
                    if p > 0:
                        mo = ml_old[:, :, HEAD_DIM * h:HEAD_DIM * h + 1]
                        mn = jnp.maximum(mn, mo)
                        al = jnp.exp(mo - mn)
                        alpha.append(al)
                    pr = jnp.exp(s - mn)
                    ln = _lsum(pr)
                    if p > 0:
                        ln = ln + al * ml_old[:, :, HEAD_DIM * h + HALF_HEAD:HEAD_DIM * h + HALF_HEAD + 1]
                    pv = pv + _bdot(pr.astype(BF16), jnp.where(vmasks[h], vals, 0.0).astype(BF16))
                    m_new.append(mn)
                    l_new.append(ln)
                if p > 0:
                    pv = pv + acc_old * jnp.where(lo, alpha[0], alpha[1])
                ml_t = jnp.where(lo, jnp.where(lane < HALF_HEAD, m_new[0], l_new[0]),
                                 jnp.where(lane < HEAD_DIM + HALF_HEAD, m_new[1], l_new[1]))
                for j, r in enumerate(rc):
                    o_ref[r, :] = pv[j]
                    lse_ref[r, :] = ml_t[j]
                return carry

            lax.fori_loop(0, nblk // GRP, step, 0)

        def fin(i, carry):
            rows = pl.ds(pl.multiple_of(i * 512, 512), 512)
            ml = lse_ref[rows, :]
            is_m = (lax.broadcasted_iota(jnp.int32, ml.shape, 1) & HALF_HEAD) == 0
            den = jnp.where(is_m, pltpu.roll(ml, BLK - HALF_HEAD, 1), ml)
            o_ref[rows, :] = o_ref[rows, :] / den
            lse_ref[rows, :] = ml + jnp.log(den)
            return carry

        lax.fori_loop(0, S // 512, fin, 0)
        car.phase(1, pl.program_id(0), jrefs)
        car.phase(2, pl.program_id(0), jrefs)

    outs = pl.pallas_call(
        body, grid=(4,), name="attn_fwd",
        in_specs=[pl.BlockSpec((None, 3, None, S, BLK), lambda g: (0, 0, g, 0, 0)),
                  pl.BlockSpec((3, 2, BLK, KEYS), lambda g: (0, g, 0, 0))] + car.in_specs(),
        out_specs=[pl.BlockSpec((None, S, BLK), lambda g: (g, 0, 0)),
                   pl.BlockSpec((None, S, BLK), lambda g: (g, 0, 0))] + car.out_specs(),
        out_shape=[SDS((4, S, BLK), F32), SDS((4, S, BLK), F32)] + car.out_shapes(),
        input_output_aliases=car.aliases(),
        scratch_shapes=[pltpu.VMEM((3, S, BLK), F32)] + car.sems(),
        compiler_params=_cp("arbitrary"))(qkv, bias, *car.args())
    if job is not None:
        job.landed = list(outs[2:])
    return outs[0], outs[1]


def attn_bwd(qkv, bias, o, lse, do, job=None):
    S = qkv.shape[3]
    nblk = S // BLK
    car = Carried(job, 5, 2, 2, 4)

    def body(*refs):
        (qkv_in, bias_ref, o_ref, lse_ref, do_ref, dqkv_out, ds_ref, qkv_ref, dqkv_ref), jrefs = car.split(refs)
        car.phase(0, pl.program_id(0), jrefs)
        _widen(qkv_in, qkv_ref)
        lane = lax.broadcasted_iota(jnp.int32, (GRP, BLK, BLK), 2)
        lo = lane < HEAD_DIM
        masks = (lo, jnp.logical_not(lo))
        lov = lax.broadcasted_iota(jnp.int32, (GRP, KEYS, BLK), 2) < HEAD_DIM
        vmasks = (lov, jnp.logical_not(lov))
        prev_keys = lax.broadcasted_iota(jnp.int32, (1, 1, KEYS), 2) < BLK
        dqkv_ref[...] = jnp.zeros_like(dqkv_ref)
        ds_ref[...] = jnp.zeros_like(ds_ref)
        q_ref, k_ref, v_ref = qkv_ref.at[0], qkv_ref.at[1], qkv_ref.at[2]
        dq_ref, dk_ref, dv_ref = dqkv_ref.at[0], dqkv_ref.at[1], dqkv_ref.at[2]
        for p, (_, dil) in enumerate(PATTERNS):
            def step(i, carry, p=p, dil=dil):
                rc, before, pen = _group_rows(i, dil, S)
                q2 = _load_group(q_ref, rc) * Q_SCALE
                kc = _load_group(k_ref, rc).astype(BF16)
                keys = jnp.concatenate([_with_prev(k_ref, before, kc), kc], axis=1)
                vc = _load_group(v_ref, rc).astype(BF16)
                vals = jnp.concatenate([_with_prev(v_ref, before, vc), vc], axis=1)
                pen = jnp.where(prev_keys, pen, 0.0)
                dot = _load_group(do_ref, rc)
                lset = _load_group(lse_ref, rc)
                prod = dot * _load_group(o_ref, rc)
                dq = jnp.zeros((GRP, BLK, BLK), F32)
                dkeys = jnp.zeros((GRP, KEYS, BLK), F32)
                dvals = jnp.zeros((GRP, KEYS, BLK), F32)
                for h in range(2):
                    qh = jnp.where(masks[h], q2, 0.0).astype(BF16)
                    doh = jnp.where(masks[h], dot, 0.0).astype(BF16)
                    delta = _lsum(jnp.where(masks[h], prod, 0.0))
                    lh = lset[:, :, HEAD_DIM * h:HEAD_DIM * h + 1]
                    pr = jnp.exp(_bdot_nt(qh, keys) + bias_ref[p, h][None] + pen - lh)
                    ds = pr * (_bdot_nt(doh, vals) - delta)
                    ds_ref[p, h] += jnp.sum(ds, axis=0)
                    dsb = ds.astype(BF16)
                    dq = dq + jnp.where(masks[h], _bdot(dsb, keys), 0.0)
                    dkeys = dkeys + _bdot_tn(dsb, qh)
                    dvals = dvals + _bdot_tn(pr.astype(BF16), doh)
                dkp, dkc, dvp, dvc = dkeys[:, :BLK], dkeys[:, BLK:], dvals[:, :BLK], dvals[:, BLK:]
                none = jnp.zeros((1, BLK, BLK), F32)
                dkc = dkc + jnp.concatenate([dkp[1:], none], axis=0)
                dvc = dvc + jnp.concatenate([dvp[1:], none], axis=0)
                for j in range(GRP):
                    dq_ref[rc[j], :] += dq[j] * Q_SCALE
                    dk_ref[rc[j], :] += dkc[j]
                    dv_ref[rc[j], :] += dvc[j]
                if before is not None:
                    dk_ref[before, :] += dkp[0]
                    dv_ref[before, :] += dvp[0]
                return carry

            lax.fori_loop(0, nblk // GRP, step, 0)

        def narrow(i, carry):
            rows = pl.ds(pl.multiple_of(i * 512, 512), 512)
            for a in range(3):
                dqkv_out[a, rows, :] = dqkv_ref[a, rows, :].astype(BF16)
            return carry

        lax.fori_loop(0, S // 512, narrow, 0)
        car.phase(1, pl.program_id(0), jrefs)
        car.phase(2, pl.program_id(0), jrefs)

    blk = pl.BlockSpec((None, S, BLK), lambda g: (g, 0, 0))
    outs = pl.pallas_call(
        body, grid=(4,), name="attn_bwd",
        in_specs=[pl.BlockSpec((None, 3, None, S, BLK), lambda g: (0, 0, g, 0, 0)),
                  pl.BlockSpec((3, 2, BLK, KEYS), lambda g: (0, g, 0, 0)), blk, blk, blk] + car.in_specs(),
        out_specs=[pl.BlockSpec((None, 3, None, S, BLK), lambda g: (0, 0, g, 0, 0)),
                   pl.BlockSpec((3, 2, BLK, KEYS), lambda g: (0, g, 0, 0))] + car.out_specs(),
        out_shape=[SDS((2, 3, 4, S, BLK), BF16), SDS((3, N_HEADS, BLK, KEYS), F32)] + car.out_shapes(),
        input_output_aliases=car.aliases(),
        scratch_shapes=[pltpu.VMEM((3, S, BLK), F32), pltpu.VMEM((3, S, BLK), F32)] + car.sems(),
        compiler_params=_cp("arbitrary"))(qkv, bias, o, lse, do, *car.args())
    if job is not None:
        job.landed = list(outs[2:])
    return outs[0], outs[1]


def _shift_down(u, k, halo):
    n = u.shape[0]
    row = lax.broadcasted_iota(jnp.int32, u.shape, 0)
    out = pltpu.roll(u, k, 0)
    hn = halo.shape[0]
    for j in range(k):
        out = jnp.where(row == j, halo[hn - k + j:hn - k + j + 1, :], out)
    return out


def _shift_up(u, k, halo):
    n = u.shape[0]
    row = lax.broadcasted_iota(jnp.int32, u.shape, 0)
    out = pltpu.roll(u, n - k, 0)
    for j in range(k):
        out = jnp.where(row == n - k + j, halo[j:j + 1, :], out)
    return out


def _conv3(u, halo, w0, w1, w2):
    return _shift_down(u, 2, halo) * w0 + _shift_down(u, 1, halo) * w1 + u * w2


def _colsum(x):
    return jnp.sum(x, axis=0, keepdims=True)


def _mixer_specs(S, tm):
    conv = pl.BlockSpec((None, 12, tm, BLK), lambda i: (1, 0, i, 0))
    halo = pl.BlockSpec((None, 12, 16, BLK), lambda i: (1, 0, jnp.maximum(i * (tm // 16) - 1, 0), 0))
    ob = pl.BlockSpec((4, tm, BLK), lambda i: (0, i, 0))
    return conv, halo, ob


def _mixer_recompute(i, o_ref, pr_ref, ph_ref, w_ref):
    ob = [o_ref[q] for q in range(4)]
    gb = [pr_ref[q].astype(F32) for q in range(4)]
    gc = [pr_ref[4 + q].astype(F32) for q in range(4)]
    xi = [pr_ref[8 + q].astype(F32) for q in range(4)]
    keep = jnp.where(i > 0, 1.0, 0.0)
    u = [gc[q] * xi[q] for q in range(4)]
    hu = [ph_ref[4 + q].astype(F32) * ph_ref[8 + q].astype(F32) * keep for q in range(4)]
    w = [[w_ref[k:k + 1, q * BLK:(q + 1) * BLK] for k in range(3)] for q in range(4)]
    cv = [_conv3(u[q], hu[q], *w[q]) for q in range(4)]
    return ob, gb, gc, xi, u, hu, cv, w


def _rms_blocks(blocks):
    ss = sum(_rsum(b * b) for b in blocks)
    return lax.rsqrt(ss / (BLK * len(blocks)) + EPS)


def mixer_post_fwd(o, proj, w_sc, g_a, g_c):
    S = o.shape[1]
    tm = 512

    def body(o_ref, pr_ref, ph_ref, w_ref, ga_ref, gc_ref, m_ref):
        i = pl.program_id(0)
        ob, gb, _, _, _, _, cv, _ = _mixer_recompute(i, o_ref, pr_ref, ph_ref, w_ref)
        conv = [gb[q] * cv[q] for q in range(4)]
        ra, rc = _rms_blocks(ob), _rms_blocks(conv)
        for q in range(4):
            sl = slice(q * BLK, (q + 1) * BLK)
            m_ref[:, q * BLK:(q + 1) * BLK] = (ob[q] * ra * ga_ref[:, sl]).astype(BF16)
            m_ref[:, ATTN_W + q * BLK:ATTN_W + (q + 1) * BLK] = (conv[q] * rc * gc_ref[:, sl]).astype(BF16)

    conv_s, halo_s, ob_s = _mixer_specs(S, tm)
    full = lambda r, c: pl.BlockSpec((r, c), lambda i: (0, 0))
    return pl.pallas_call(
        body, grid=(S // tm,), name="mixer_post_fwd",
        in_specs=[ob_s, conv_s, halo_s, full(3, 512), full(1, 512), full(1, 512)],
        out_specs=pl.BlockSpec((tm, D_MODEL), lambda i: (i, 0)), out_shape=SDS((S, D_MODEL), BF16),
        compiler_params=_cp("parallel"))(o, proj, proj, w_sc, g_a, g_c)


def mixer_post_bwd_a(dmixed, o, proj, w_sc, g_a, g_c, job=None):
    S = o.shape[1]
    tm = 512

    def body(dm_ref, o_ref, pr_ref, ph_ref, w_ref, ga_ref, gc_ref, do_ref, dgb_ref, dcv_ref, dga_ref, dgc_ref):
        i = pl.program_id(0)
        ob, gb, _, _, _, _, cv, _ = _mixer_recompute(i, o_ref, pr_ref, ph_ref, w_ref)
        conv = [gb[q] * cv[q] for q in range(4)]
        ra, rc = _rms_blocks(ob), _rms_blocks(conv)

        @pl.when(i == 0)
        def _():
            dga_ref[...] = jnp.zeros_like(dga_ref)
            dgc_ref[...] = jnp.zeros_like(dgc_ref)

        for blocks, r, g_ref, off, dg_ref, is_attn in ((ob, ra, ga_ref, 0, dga_ref, True), (conv, rc, gc_ref, ATTN_W, dgc_ref, False)):
            xh = [blocks[q] * r for q in range(4)]
            dm = [dm_ref[:, off + q * BLK:off + (q + 1) * BLK].astype(F32) for q in range(4)]
            gd = [dm[q] * g_ref[:, q * BLK:(q + 1) * BLK] for q in range(4)]
            mean = sum(_rsum(gd[q] * xh[q]) for q in range(4)) / (4 * BLK)
            for q in range(4):
                dg_ref[:, q * BLK:(q + 1) * BLK] += _colsum(dm[q] * xh[q])
                dx = r * (gd[q] - xh[q] * mean)
                if is_attn:
                    do_ref[q] = dx
                else:
                    dgb_ref[q] = dx * cv[q]
                    dcv_ref[q] = dx * gb[q]

    conv_s, halo_s, ob_s = _mixer_specs(S, tm)
    full = lambda r, c: pl.BlockSpec((r, c), lambda i: (0, 0))
    return _pcall(
        body, [dmixed, o, proj, proj, w_sc, g_a, g_c], grid=(S // tm,), name="mixer_post_bwd_a",
        in_specs=[pl.BlockSpec((tm, D_MODEL), lambda i: (i, 0)), ob_s, conv_s, halo_s, full(3, 512), full(1, 512), full(1, 512)],
        out_specs=[ob_s, ob_s, ob_s, full(1, 512), full(1, 512)],
        out_shape=[SDS((4, S, BLK), F32)] * 3 + [SDS((1, 512), F32)] * 2, sem=("arbitrary",), job=job)


def mixer_post_bwd_b(dproj, dgb, dcv, proj, w_sc):
    S = proj.shape[2]
    tm = 512
    last = S // 8 - 1

    def body(dp_in, dgb_ref, dcv_ref, dn_ref, pr_ref, w_ref, dp_ref, dw_ref):
        i = pl.program_id(0)
        keep_next = jnp.where(i < pl.num_programs(0) - 1, 1.0, 0.0)

        @pl.when(i == 0)
        def _():
            dw_ref[...] = jnp.zeros_like(dw_ref)

        for q in range(4):
            sl = slice(q * BLK, (q + 1) * BLK)
            gc, xi = pr_ref[4 + q].astype(F32), pr_ref[8 + q].astype(F32)
            u = gc * xi
            dcv = dcv_ref[q]
            dn = dn_ref[q] * keep_next
            d1, d2 = _shift_up(dcv, 1, dn), _shift_up(dcv, 2, dn)
            w0, w1, w2 = (w_ref[k:k + 1, sl] for k in range(3))
            du = dcv * w2 + d1 * w1 + d2 * w0
            dw_ref[0:1, sl] += _colsum(d2 * u)
            dw_ref[1:2, sl] += _colsum(d1 * u)
            dw_ref[2:3, sl] += _colsum(dcv * u)
            dp_ref[q] = dgb_ref[q].astype(BF16)
            dp_ref[4 + q] = (du * xi).astype(BF16)
            dp_ref[8 + q] = (du * gc).astype(BF16)

    conv_s, _, ob_s = _mixer_specs(S, tm)
    nxt = pl.BlockSpec((4, 8, BLK), lambda i: (0, jnp.minimum((i + 1) * (tm // 8), last), 0))
    full = lambda r, c: pl.BlockSpec((r, c), lambda i: (0, 0))
    return pl.pallas_call(
        body, grid=(S // tm,), name="mixer_post_bwd_b",
        in_specs=[pl.BlockSpec(memory_space=pl.ANY), ob_s, ob_s, nxt, conv_s, full(3, 512)],
        out_specs=[conv_s, full(3, 512)],
        out_shape=[SDS(dproj.shape, BF16), SDS((3, 512), F32)],
        input_output_aliases={0: 0}, compiler_params=_cp("arbitrary"))(dproj, dgb, dcv, dcv, proj, w_sc)


def xattn_fwd(q, k, v):
    S = q.shape[0]
    tm = 512
    scale = MEM_HD ** -0.5

    def body(q_ref, k_ref, v_ref, o_ref):
        for h in range(N_MEM_HEADS):
            sl = slice(h * MEM_HD, (h + 1) * MEM_HD)
            s = _dot_nt(q_ref[:, sl], k_ref[:, sl]) * scale
            p = jnp.exp(s - jnp.max(s, axis=1, keepdims=True))
            p = p / _rsum(p)
            o_ref[:, sl] = _dot(p.astype(BF16), v_ref[:, sl]).astype(BF16)

    row = pl.BlockSpec((tm, D_MODEL), lambda i: (i, 0))
    kv = pl.BlockSpec(k.shape, lambda i: (0, 0))
    return pl.pallas_call(body, grid=(S // tm,), name="xattn_fwd", in_specs=[row, kv, kv], out_specs=row,
                          out_shape=SDS((S, D_MODEL), BF16), compiler_params=_cp("parallel"))(q, k, v)


def xattn_bwd(q, k, v, do):
    S = q.shape[0]
    tm = 512
    scale = MEM_HD ** -0.5

    def body(q_ref, k_ref, v_ref, do_ref, dq_ref, dk_ref, dv_ref):
        @pl.when(pl.program_id(0) == 0)
        def _():
            dk_ref[...] = jnp.zeros_like(dk_ref)
            dv_ref[...] = jnp.zeros_like(dv_ref)

        for h in range(N_MEM_HEADS):
            sl = slice(h * MEM_HD, (h + 1) * MEM_HD)
            qh, kh, vh, doh = q_ref[:, sl], k_ref[:, sl], v_ref[:, sl], do_ref[:, sl]
            s = _dot_nt(qh, kh) * scale
            p = jnp.exp(s - jnp.max(s, axis=1, keepdims=True))
            p = p / _rsum(p)
            dp = _dot_nt(doh, vh)
            ds = (p * (dp - _rsum(p * dp)) * scale).astype(BF16)
            dq_ref[:, sl] = _dot(ds, kh).astype(BF16)
            dk_ref[:, sl] += _dot_tn(ds, qh)
            dv_ref[:, sl] += _dot_tn(p.astype(BF16), doh)

    row = pl.BlockSpec((tm, D_MODEL), lambda i: (i, 0))
    kv = pl.BlockSpec(k.shape, lambda i: (0, 0))
    return pl.pallas_call(body, grid=(S // tm,), name="xattn_bwd", in_specs=[row, kv, kv, row], out_specs=[row, kv, kv],
                          out_shape=[SDS((S, D_MODEL), BF16), SDS(k.shape, F32), SDS(k.shape, F32)],
                          compiler_params=_cp("arbitrary"))(q, k, v, do)


FFN_TM, FFN_TC = 256, 1408


def _ffn_specs(S):
    tm, tc = FFN_TM, FFN_TC
    blk = pl.BlockSpec((2, tm, tc), lambda j, i: (0, i, j))
    nxt = pl.BlockSpec((2, 16, tc), lambda j, i: (0, jnp.minimum((i + 1) * (tm // 16), S // 16 - 1), j))
    wsp = pl.BlockSpec((2, 3, tc), lambda j, i: (0, 0, j))
    bsp = pl.BlockSpec((2, 1, tc), lambda j, i: (0, 0, j))
    return blk, nxt, wsp, bsp


def up_act_fwd(h, w_up, w_fc, b_fc):
    S, K = h.shape
    tm, tc = FFN_TM, FFN_TC
    nj = D_FF // tc

    def body(h_ref, wg_ref, wv_ref, w_ref, b_ref, pre_ref, gv_ref, a_ref, halo):
        i = pl.program_id(1)
        hv = h_ref[...]
        res = []
        for half, wt_ref in enumerate((wg_ref, wv_ref)):
            u = _dot(hv, wt_ref[...])
            hu = jnp.where(i > 0, halo[half], 0.0)
            halo[half] = u[tm - 8:, :]
            w0, w1, w2 = (w_ref[half, k:k + 1, :] for k in range(3))
            pre_ref[half] = u.astype(BF16)
            res.append(_conv3(u, hu, w0, w1, w2) + b_ref[half])
        g, v = res
        gv_ref[0] = g.astype(BF16)
        gv_ref[1] = v.astype(BF16)
        a_ref[...] = (g * jax.nn.sigmoid(g) * v).astype(BF16)

    blk = pl.BlockSpec((2, tm, tc), lambda j, i: (0, i, j))
    return pl.pallas_call(
        body, grid=(nj, S // tm), name="up_act_fwd",
        in_specs=[pl.BlockSpec((tm, K), lambda j, i: (i, 0)), pl.BlockSpec((K, tc), lambda j, i: (0, j)),
                  pl.BlockSpec((K, tc), lambda j, i: (0, j + nj)), pl.BlockSpec((2, 3, tc), lambda j, i: (0, 0, j)),
                  pl.BlockSpec((2, 1, tc), lambda j, i: (0, 0, j))],
        out_specs=[blk, blk, pl.BlockSpec((tm, tc), lambda j, i: (i, j))],
        out_shape=[SDS((2, S, D_FF), BF16), SDS((2, S, D_FF), BF16), SDS((S, D_FF), BF16)],
        scratch_shapes=[pltpu.VMEM((2, 8, tc), F32)],
        compiler_params=_cp("parallel", "arbitrary"))(h, w_up, w_up, w_fc, b_fc)


def ffn_act_bwd(dx, w_down, up, up_pre, w_fc):
    S = up.shape[1]

    def act_grads(da, g, v):
        sg = jax.nn.sigmoid(g)
        return da * v * (sg * (1.0 + g * (1.0 - sg))), da * g * sg

    def body(dx_ref, dxn_ref, wd_ref, gv_ref, gvn_ref, up_ref, w_ref, dp_ref, dw_ref, db_ref):
        i = pl.program_id(1)
        keep_next = jnp.where(i < pl.num_programs(1) - 1, 1.0, 0.0)

        @pl.when(i == 0)
        def _():
            dw_ref[...] = jnp.zeros_like(dw_ref)
            db_ref[...] = jnp.zeros_like(db_ref)

        wd = wd_ref[...]
        da = _dot_nt(dx_ref[...].astype(BF16), wd)
        dan = _dot_nt(dxn_ref[...].astype(BF16), wd) * keep_next
        here = act_grads(da, gv_ref[0].astype(F32), gv_ref[1].astype(F32))
        after = act_grads(dan, gvn_ref[0].astype(F32), gvn_ref[1].astype(F32))
        for half in range(2):
            d, dn = here[half], after[half]
            u = up_ref[half].astype(F32)
            d1, d2 = _shift_up(d, 1, dn), _shift_up(d, 2, dn)
            w0, w1, w2 = (w_ref[half, k:k + 1, :] for k in range(3))
            dp_ref[half] = (d * w2 + d1 * w1 + d2 * w0).astype(BF16)
            dw_ref[half, 0:1, :] += _colsum(d2 * u)
            dw_ref[half, 1:2, :] += _colsum(d1 * u)
            dw_ref[half, 2:3, :] += _colsum(d * u)
            db_ref[half] += _colsum(d)

    tm, tc = FFN_TM, FFN_TC
    blk, nxt, wsp, bsp = _ffn_specs(S)
    K = dx.shape[1]
    rows = pl.BlockSpec((tm, K), lambda j, i: (i, 0))
    rows_next = pl.BlockSpec((16, K), lambda j, i: (jnp.minimum((i + 1) * (tm // 16), S // 16 - 1), 0))
    return pl.pallas_call(body, grid=(D_FF // tc, S // tm), name="ffn_act_bwd",
                          in_specs=[rows, rows_next, pl.BlockSpec((tc, K), lambda j, i: (j, 0)), blk, nxt, blk, wsp],
                          out_specs=[blk, wsp, bsp],
                          out_shape=[SDS((2, S, D_FF), BF16), SDS((2, 3, D_FF), F32), SDS((2, 1, D_FF), F32)],
                          compiler_params=_cp("parallel", "arbitrary"))(dx, dx, w_down, up, up, up_pre, w_fc)


def local_step(x, mem, target, wb, ws, comm=None):
    S = x.shape[0]
    assert S % 2048 == 0
    if comm is None:
        comm = NoComm()
    else:
        wb = {}
    ws = dict(ws)

    bias = bias_tables(ws["rel_bias"])
    h1 = rmsnorm_fwd(x, ws["g_mix"], "norm_mix", job=comm.job("norm_mix"))
    comm.landed("norm_mix", wb, ws)
    w_fc = ws["w_ffn_conv"].reshape(3, 2, D_FF).transpose(1, 0, 2)
    b_fc = ws["b_ffn_conv"].reshape(2, 1, D_FF)
    proj = mm_nn(h1, wb["w_in"], "proj", out_dtype=BF16, out_bw=BLK, tn=768, job=comm.job("proj"))
    comm.landed("proj", wb)
    qkv = proj.reshape(2, 3, 4, S, BLK)
    o, lse = attn_fwd(qkv, bias, job=comm.job("attn_fwd"))
    comm.landed("attn_fwd", wb)
    proj4 = proj.reshape(2, 12, S, BLK)
    mixed = mixer_post_fwd(o, proj4, ws["w_short_conv"], ws["g_attn_out"], ws["g_conv_out"])
    x1, h2 = mm_nn_rows(mixed, wb["w_out"], x, ws["g_xattn"], "out_proj")
    mem_n = rmsnorm_fwd(mem, ws["g_mem"], "norm_mem")
    xq = mm_nn(h2, wb["w_xq"], "xq", out_dtype=BF16)
    xk = mm_nn(mem_n, wb["w_xk"], "xk", out_dtype=BF16, tn=1024)
    xv = mm_nn(mem_n, wb["w_xv"], "xv", out_dtype=BF16, tn=1024)
    xo = xattn_fwd(xq, xk, xv)
    x2, h3 = mm_nn_rows(xo, wb["w_xo"], x1, ws["g_ffn"], "xo_proj")
    up_pre, up, act = up_act_fwd(h3, wb["w_up"], w_fc, b_fc)
    loss, dx3, dg_final = mm_nn_rows(act, wb["w_down"], x2, ws["g_final"].reshape(1, -1), "down_proj", target=target)

    gb, gs = {}, {"g_final": dg_final}
    gb["w_down"] = mm_tn(act, dx3, None, "dw_down", tm=1408, tn=1024)
    d_up_pre, dw_fc, db_fc = ffn_act_bwd(dx3, wb["w_down"], up, up_pre, w_fc)
    gs["b_ffn_conv"] = db_fc.reshape(1, 2 * D_FF)
    gs["w_ffn_conv"] = dw_fc.transpose(1, 0, 2).reshape(3, 2 * D_FF)
    gb["w_up"] = mm_tn(h3, d_up_pre, D_FF, "dw_up", shards=4, tn=1408)
    dx2, gs["g_ffn"] = mm_nt_norm_bwd(d_up_pre, D_FF, wb["w_up"], x2, ws["g_ffn"], dx3, "d_h3", tm=256,
                                      job=comm.job("d_h3", gb))
    comm.landed("d_h3")
    gb["w_xo"] = mm_tn(xo, dx2, None, "dw_xo")
    dxo = mm_nt(dx2, None, wb["w_xo"], "d_xo", out_dtype=BF16)
    dxq, dxk, dxv = xattn_bwd(xq, xk, xv, dxo)
    gb["w_xq"] = mm_tn(h2, dxq, None, "dw_xq")
    gb["w_xk"] = mm_tn(mem_n, dxk, None, "dw_xk", tn=1024)
    gb["w_xv"] = mm_tn(mem_n, dxv, None, "dw_xv", tn=1024)
    dmem_n = mm_nt(dxk, None, wb["w_xk"], "d_memk", tn=1024)
    dmem_n = mm_nt(dxv, None, wb["w_xv"], "d_memv", res=dmem_n, tn=1024)
    gs["g_mem"] = rmsnorm_gain_grad(mem, dmem_n, "norm_mem_bwd")
    dx1, gs["g_xattn"] = mm_nt_norm_bwd(dxq, None, wb["w_xq"], x1, ws["g_xattn"], dx2, "d_h2")
    gb["w_out"] = mm_tn(mixed, dx1, None, "dw_out")
    dmixed = mm_nt(dx1, None, wb["w_out"], "d_mixed")
    do, dgb, dcv, gs["g_attn_out"], gs["g_conv_out"] = mixer_post_bwd_a(
        dmixed, o, proj4, ws["w_short_conv"], ws["g_attn_out"], ws["g_conv_out"], job=comm.job("mixer_post_bwd_a", gb))
    comm.landed("mixer_post_bwd_a")
    dproj, ds_sum = attn_bwd(qkv, bias, o, lse, do, job=comm.job("attn_bwd"))
    comm.landed("attn_bwd")
    gs["rel_bias"] = bias_tables_bwd(ds_sum)
    dproj, gs["w_short_conv"] = mixer_post_bwd_b(dproj.reshape(2, 12, S, BLK), dgb, dcv, proj4, ws["w_short_conv"])
    dproj = dproj.reshape(24, S, BLK)
    gb["w_in"] = mm_tn(h1, dproj, BLK, "dw_in", shards=4, tn=768, job=comm.job("dw_in"))
    comm.landed("dw_in")
    comm.first_update(gb)
    grad_x, gs["g_mix"] = mm_nt_norm_bwd(dproj, BLK, wb["w_in"], x, ws["g_mix"], dx1, "d_h1", tm=256,
                                         job=comm.job("d_h1", gb))
    comm.landed("d_h1")
    return loss, grad_x, comm.finish(gb), gs


def _adam_update(w_ref, g_ref, m_ref, v_ref, d_ref, nm_ref, nv_ref):
    gv = g_ref[...]
    mn = ADAM_B1 * m_ref[...] + (1.0 - ADAM_B1) * gv
    vn = ADAM_B2 * v_ref[...] + (1.0 - ADAM_B2) * (gv * gv)
    m_hat = mn / (1.0 - ADAM_B1 ** ADAM_STEP)
    v_hat = vn / (1.0 - ADAM_B2 ** ADAM_STEP)
    d_ref[...] = -ADAM_LR * (m_hat / (jnp.sqrt(v_hat) + ADAM_EPS) + ADAM_WD * w_ref[...])
    nm_ref[...] = mn
    nv_ref[...] = vn


def adamw(ws, gs, ms, vs, name, job=None):
    n = len(ws)
    R, C = ws[0].shape
    tr = R // 4

    def body(*refs):
        for p in range(n):
            _adam_update(*(refs[q * n + p] for q in range(7)))

    blk = pl.BlockSpec((tr, C), lambda i: (i, 0))
    outs = _pcall(body, [*ws, *gs, *ms, *vs], grid=(4,), name=name, in_specs=[blk] * (4 * n), out_specs=[blk] * (3 * n),
                  out_shape=[SDS((R, C), F32)] * (3 * n), sem=("parallel",), job=job)
    return outs[:n], outs[n:2 * n], outs[2 * n:]


BIG = ("w_in", "w_out", "w_xq", "w_xk", "w_xv", "w_xo", "w_up", "w_down")
COL_SHARDED = ("w_in", "w_up")
ANY = pl.BlockSpec(memory_space=pl.ANY)


def _place():
    x, y, c = lax.axis_index("x"), lax.axis_index("y"), lax.axis_index("c")
    chips = [(1 - x, y), (x, 1 - y), (1 - x, 1 - y)]
    return x, y, c, chips


def _window(full, name, R, C, shard, half):
    r0, nr = (0, R) if half is None else (half * (R // 2), R // 2)
    if name in COL_SHARDED:
        return full.at[pl.ds(r0, nr), pl.ds(shard * C, C)]
    return full.at[pl.ds(shard * R + r0, nr), :]


def same_shape_groups(names, arrays):
    groups = {}
    for n in names:
        groups.setdefault(tuple(arrays[n].shape), []).append(n)
    return list(groups.values())


def place_shards(ws, mine_arr, col, name):
    n = len(ws)
    R, C = ws[0].shape

    def body(s_ref, *refs):
        for p in range(n):
            refs[n + p][...] = refs[p][...].astype(BF16)

    grid_spec = pltpu.PrefetchScalarGridSpec(
        num_scalar_prefetch=1, grid=(1,), in_specs=[pl.BlockSpec((R, C), lambda i, s: (0, 0))] * n,
        out_specs=[pl.BlockSpec((R, C), (lambda i, s: (0, s[0])) if col else (lambda i, s: (s[0], 0)))] * n)
    return pl.pallas_call(body, grid_spec=grid_spec, name=name,
                          out_shape=[SDS((R, 4 * C) if col else (4 * R, C), BF16)] * n,
                          compiler_params=_cp("arbitrary"))(mine_arr, *ws)


def _gather_jobs(names, shapes):
    nw = len(names)

    def start(full, sems):
        send, recv, fsend, frecv = sems
        x, y, c, chips = _place()
        mine = 2 * x + y
        for w, n in enumerate(names):
            R, C = shapes[w]
            own = _window(full[w], n, R, C, mine, c)
            for j, chip in enumerate(chips):
                pltpu.make_async_remote_copy(src_ref=own, dst_ref=own, send_sem=send.at[w, j], recv_sem=recv.at[w, j],
                                             device_id=(*chip, c), device_id_type=MESH).start()

    def mid(full, sems, lo=0, hi=nw):
        send, recv, fsend, frecv = sems
        x, y, c, chips = _place()
        sib = (x, y, 1 - c)
        for w, n in list(enumerate(names))[lo:hi]:
            R, C = shapes[w]
            for j, chip in enumerate(chips):
                landed = _window(full[w], n, R, C, 2 * chip[0] + chip[1], c)
                pltpu.make_async_remote_copy(src_ref=landed, dst_ref=landed, send_sem=send.at[w, j], recv_sem=recv.at[w, j],
                                             device_id=(*chip, c), device_id_type=MESH).wait_recv()
                pltpu.make_async_remote_copy(src_ref=landed, dst_ref=landed, send_sem=fsend.at[w, j],
                                             recv_sem=frecv.at[w, j], device_id=sib, device_id_type=MESH).start()

    def finish(full, sems):
        send, recv, fsend, frecv = sems
        x, y, c, chips = _place()
        mine = 2 * x + y
        sib = (x, y, 1 - c)
        for w, n in enumerate(names):
            R, C = shapes[w]
            own = _window(full[w], n, R, C, mine, c)
            for j, chip in enumerate(chips):
                landed = _window(full[w], n, R, C, 2 * chip[0] + chip[1], c)
                other = _window(full[w], n, R, C, 2 * chip[0] + chip[1], 1 - c)
                pltpu.make_async_remote_copy(src_ref=other, dst_ref=other, send_sem=fsend.at[w, j], recv_sem=frecv.at[w, j],
                                             device_id=sib, device_id_type=MESH).wait_recv()
                pltpu.make_async_remote_copy(src_ref=own, dst_ref=own, send_sem=send.at[w, j], recv_sem=recv.at[w, j],
                                             device_id=(*chip, c), device_id_type=MESH).wait_send()
                pltpu.make_async_remote_copy(src_ref=landed, dst_ref=landed, send_sem=fsend.at[w, j],
                                             recv_sem=frecv.at[w, j], device_id=sib, device_id_type=MESH).wait_send()

    return start, mid, finish, [pltpu.SemaphoreType.DMA((nw, 3))] * 4


class CommJob:
    def __init__(self, ins, out_shapes, inplace, start, finish, sems, mid=None):
        self.ins, self.out_shapes, self.inplace = list(ins), list(out_shapes), inplace
        self.start, self.mid, self.finish, self.sems = start, mid, finish, list(sems)

    def run(self, phase, ins, outs, sems):
        if phase == 0:
            self.start(ins, outs, sems)
        elif phase == 1:
            if self.mid is not None:
                self.mid(ins, outs, sems)
        else:
            self.finish(ins, outs, sems)


class Carried:
    def __init__(self, job, n_in, n_out, n_scratch, steps):
        self.job, self.n_in, self.n_out, self.n_scratch, self.steps = job, n_in, n_out, n_scratch, steps
        self.nji = len(job.ins) if job else 0
        self.njo = len(job.out_shapes) if job else 0

    def in_specs(self):
        return [ANY] * self.nji

    def out_specs(self):
        return [ANY] * self.njo

    def out_shapes(self):
        return list(self.job.out_shapes) if self.job else []

    def aliases(self):
        if not (self.job and self.job.inplace):
            return {}
        return {self.n_in + i: self.n_out + i for i in range(self.nji)}

    def sems(self):
        return list(self.job.sems) if self.job else []

    def args(self):
        return list(self.job.ins) if self.job else []

    def split(self, refs):
        a = self.n_in
        b = a + self.nji
        c = b + self.n_out
        d = c + self.njo
        e = d + self.n_scratch
        return refs[:a] + refs[b:c] + refs[d:e], (refs[a:b], refs[c:d], refs[e:])

    def phase(self, phase, step, jrefs):
        if self.job is None:
            return
        at = {0: 0, 1: max(self.steps - 2, 0), 2: self.steps - 1}[phase]

        @pl.when(step == at)
        def _():
            self.job.run(phase, *jrefs)


def _pcall(body, args, *, grid, in_specs, out_specs, out_shape, name, sem, scratch=(), aliases=None, job=None):
    n_in, n_out = len(args), len(out_shape)
    if job is None:
        return pl.pallas_call(
            body, grid=grid, in_specs=list(in_specs), out_specs=list(out_specs), out_shape=list(out_shape), name=name,
            scratch_shapes=list(scratch), input_output_aliases=dict(aliases or {}), compiler_params=_cp(*sem))(*args)
    total = int(np.prod(grid))
    car = Carried(job, n_in, n_out, len(scratch), total)

    def wrapped(*refs):
        main, jrefs = car.split(refs)
        lin = pl.program_id(0)
        for ax in range(1, len(grid)):
            lin = lin * grid[ax] + pl.program_id(ax)
        car.phase(0, lin, jrefs)
        body(*main)
        car.phase(1, lin, jrefs)
        car.phase(2, lin, jrefs)

    res = pl.pallas_call(
        wrapped, grid=grid, in_specs=list(in_specs) + car.in_specs(), out_specs=list(out_specs) + car.out_specs(),
        out_shape=list(out_shape) + car.out_shapes(), name=name, scratch_shapes=list(scratch) + car.sems(),
        input_output_aliases={**dict(aliases or {}), **car.aliases()},
        compiler_params=_cp(*(["arbitrary"] * len(grid))))(*args, *car.args())
    job.landed = list(res[n_out:])
    return list(res[:n_out])


def gather_job(placed, names):
    shapes = []
    for n in names:
        R, C = placed[n].shape
        shapes.append((R, C // 4) if n in COL_SHARDED else (R // 4, C))
    start, mid, finish, sems = _gather_jobs(names, shapes)
    arrays = [placed[n] for n in names]
    early = max(len(names) - 1, 1)

    def last(i, o, s):
        mid(o, s, early, len(names))
        finish(o, s)

    return CommJob(arrays, [SDS(a.shape, a.dtype) for a in arrays], True,
                   lambda i, o, s: start(o, s), last, sems, mid=lambda i, o, s: mid(o, s, 0, early))


def gather_rows_job(placed):
    widths = [a.shape[1] // 4 for a in placed]

    def copies(outs, sems):
        x, y, c, chips = _place()
        mine = 2 * x + y
        cps = []
        for w, ref in enumerate(outs):
            own = ref.at[:, pl.ds(mine * widths[w], widths[w])]
            for j, chip in enumerate(chips):
                theirs = ref.at[:, pl.ds((2 * chip[0] + chip[1]) * widths[w], widths[w])]
                kw = dict(send_sem=sems[0].at[w, j], recv_sem=sems[1].at[w, j], device_id=(*chip, c), device_id_type=MESH)
                cps.append((pltpu.make_async_remote_copy(src_ref=own, dst_ref=own, **kw),
                            pltpu.make_async_remote_copy(src_ref=theirs, dst_ref=theirs, **kw)))
        return cps

    def start(ins, outs, sems):
        for send, _ in copies(outs, sems):
            send.start()

    def finish(ins, outs, sems):
        for send, recv in copies(outs, sems):
            recv.wait_recv()
            send.wait_send()

    return CommJob(placed, [SDS(a.shape, a.dtype) for a in placed], True, start, finish,
                   [pltpu.SemaphoreType.DMA((len(placed), 3))] * 2)


def merge_jobs(a, b):
    assert a.inplace and b.inplace
    na, nb_, nsa = len(a.ins), len(b.ins), len(a.sems)

    def phase(which):
        def run(ins, outs, sems):
            for job, i, o, s in ((a, ins[:na], outs[:na], sems[:nsa]), (b, ins[na:], outs[na:], sems[nsa:])):
                fn = getattr(job, which)
                if fn is not None:
                    fn(i, o, s)
        return run

    merged = CommJob(a.ins + b.ins, a.out_shapes + b.out_shapes, True, phase("start"), phase("finish"), a.sems + b.sems,
                     mid=phase("mid"))
    merged.parts = (a, b, na)
    return merged


def split_landed(merged):
    a, b, na = merged.parts
    a.landed, b.landed = merged.landed[:na], merged.landed[na:]


def pair_exchange_job(grads, names):
    shapes = [grads[n].shape for n in names]

    def copies(ins, outs, sems):
        x, y, c, _ = _place()
        return [pltpu.make_async_remote_copy(
            src_ref=ins[w].at[:, pl.ds((1 - c) * (shapes[w][1] // 2), shapes[w][1] // 2), :], dst_ref=outs[w],
            send_sem=sems[0].at[w], recv_sem=sems[1].at[w], device_id=(x, y, 1 - c), device_id_type=MESH)
            for w in range(len(names))]

    def start(ins, outs, sems):
        for cp in copies(ins, outs, sems):
            cp.start()

    def finish(ins, outs, sems):
        for cp in copies(ins, outs, sems):
            cp.wait()

    return CommJob([grads[n] for n in names], [SDS((4, s[1] // 2, s[2]), F32) for s in shapes], False, start, finish,
                   [pltpu.SemaphoreType.DMA((len(names),))] * 2)


def chip_exchange_job(p16, names):
    shapes = [p16[n].shape for n in names]

    def copies(ins, outs, sems):
        x, y, c, chips = _place()
        return [pltpu.make_async_remote_copy(
            src_ref=ins[w].at[2 * chip[0] + chip[1]], dst_ref=outs[w].at[j],
            send_sem=sems[0].at[w, j], recv_sem=sems[1].at[w, j], device_id=(*chip, c), device_id_type=MESH)
            for w in range(len(names)) for j, chip in enumerate(chips)]

    def start(ins, outs, sems):
        for cp in copies(ins, outs, sems):
            cp.start()

    def finish(ins, outs, sems):
        for cp in copies(ins, outs, sems):
            cp.wait()

    return CommJob([p16[n] for n in names], [SDS((3,) + tuple(s[1:]), BF16) for s in shapes], False, start, finish,
                   [pltpu.SemaphoreType.DMA((len(names), 3))] * 2)


def half_exchange_job(full, names):
    shapes = [full[n].shape for n in names]

    def copies(outs, sems):
        x, y, c, _ = _place()
        cps = []
        for w in range(len(names)):
            Rh = shapes[w][0] // 2
            rows = outs[w].at[pl.ds(c * Rh, Rh), :]
            other = outs[w].at[pl.ds((1 - c) * Rh, Rh), :]
            cps.append((pltpu.make_async_remote_copy(src_ref=rows, dst_ref=rows, send_sem=sems[0].at[w], recv_sem=sems[1].at[w],
                                                     device_id=(x, y, 1 - c), device_id_type=MESH),
                        pltpu.make_async_remote_copy(src_ref=other, dst_ref=other, send_sem=sems[0].at[w], recv_sem=sems[1].at[w],
                                                     device_id=(x, y, 1 - c), device_id_type=MESH)))
        return cps

    def start(ins, outs, sems):
        for send, _ in copies(outs, sems):
            send.start()

    def finish(ins, outs, sems):
        for send, recv in copies(outs, sems):
            recv.wait_recv()
            send.wait_send()

    arrays = [full[n] for n in names]
    return CommJob(arrays, [SDS(a.shape, a.dtype) for a in arrays], True, start, finish,
                   [pltpu.SemaphoreType.DMA((len(names),))] * 2)


def allreduce_small(parts, name, job=None):
    n = len(parts)
    nji = len(job.ins) if job else 0
    njo = len(job.out_shapes) if job else 0

    def body(*refs):
        ins, jins = refs[:n], refs[n:n + nji]
        outs, jouts = refs[n + nji:2 * n + nji], refs[2 * n + nji:2 * n + nji + njo]
        bufs = refs[2 * n + nji + njo:3 * n + nji + njo]
        send, recv = refs[3 * n + nji + njo:3 * n + nji + njo + 2]
        jsems = refs[3 * n + nji + njo + 2:]
        x, y, c, _ = _place()
        me = 4 * x + 2 * y + c
        if job is not None:
            job.run(0, jins, jouts, jsems)
        sends = []
        for p in range(n):
            for k in range(1, 8):
                peer = (x ^ (k >> 2), y ^ ((k >> 1) & 1), c ^ (k & 1))
                cp = pltpu.make_async_remote_copy(src_ref=ins[p], dst_ref=bufs[p].at[k], send_sem=send.at[p, k - 1],
                                                  recv_sem=recv.at[p, k - 1], device_id=peer, device_id_type=MESH)
                cp.start()
                sends.append(cp)
            bufs[p][0] = ins[p][...]
        for cp in sends:
            cp.wait_recv()
        for p in range(n):
            acc = bufs[p][me]
            for d in range(1, 8):
                acc = acc + bufs[p][d ^ me]
            outs[p][...] = acc
        for cp in sends:
            cp.wait_send()
        if job is not None:
            job.run(1, jins, jouts, jsems)
            job.run(2, jins, jouts, jsems)

    vm = pl.BlockSpec(memory_space=pltpu.VMEM)
    res = pl.pallas_call(
        body, name=name, in_specs=[vm] * n + [ANY] * nji, out_specs=[vm] * n + [ANY] * njo,
        out_shape=[SDS(a.shape, F32) for a in parts] + (list(job.out_shapes) if job else []),
        input_output_aliases={n + i: n + i for i in range(nji)} if (job and job.inplace) else {},
        scratch_shapes=[pltpu.VMEM((8,) + tuple(a.shape), F32) for a in parts] + [pltpu.SemaphoreType.DMA((n, 7))] * 2
        + (list(job.sems) if job else []),
    )(*parts, *(job.ins if job else []))
    if job is not None:
        job.landed = list(res[n:])
    return list(res[:n])


def adamw_multi(ws, gs, ms, vs, name):
    n = len(ws)

    def body(*refs):
        for p in range(n):
            _adam_update(*(refs[q * n + p] for q in range(7)))

    vm = pl.BlockSpec(memory_space=pltpu.VMEM)
    outs = pl.pallas_call(body, name=name, in_specs=[vm] * (4 * n), out_specs=[vm] * (3 * n),
                          out_shape=[SDS(a.shape, F32) for a in ws] * 3)(*ws, *gs, *ms, *vs)
    return outs[:n], outs[n:2 * n], outs[2 * n:]


def pair_sum(gs, xhs, sel_arr, name):
    n = len(gs)
    _, R, C = gs[0].shape
    Rh = R // 2

    def body(sel_ref, *refs):
        for p in range(n):
            g_ref, x_ref, p32_ref, p16_ref = (refs[q * n + p] for q in range(4))
            s = g_ref[...] + x_ref[...]
            p16_ref[...] = s.astype(BF16)

            @pl.when(pl.program_id(0) == sel_ref[0])
            def _():
                p32_ref[...] = s

    same = pl.BlockSpec((None, Rh, C), lambda s, sel: (s, 0, 0))
    grid_spec = pltpu.PrefetchScalarGridSpec(
        num_scalar_prefetch=1, grid=(4,),
        in_specs=[pl.BlockSpec((None, Rh, C), lambda s, sel: (s, sel[1], 0))] * n + [same] * n,
        out_specs=[pl.BlockSpec((Rh, C), lambda s, sel: (0, 0))] * n + [same] * n)
    outs = pl.pallas_call(body, grid_spec=grid_spec, name=name,
                          out_shape=[SDS((Rh, C), F32)] * n + [SDS((4, Rh, C), BF16)] * n,
                          compiler_params=_cp("arbitrary"))(sel_arr, *gs, *xhs)
    return outs[:n], outs[n:]


def chip_sum(p32s, ys, sel_arr, name):
    n = len(p32s)
    Rh, C = p32s[0].shape

    def body(s_ref, *refs):
        for p in range(n):
            p_ref, y_ref, o_ref = (refs[q * n + p] for q in range(3))
            o_ref[...] = ((p_ref[...] + y_ref[0].astype(F32)) + y_ref[1].astype(F32)) + y_ref[2].astype(F32)

    grid_spec = pltpu.PrefetchScalarGridSpec(
        num_scalar_prefetch=1, grid=(1,),
        in_specs=[pl.BlockSpec((Rh, C), lambda i, s: (0, 0))] * n + [pl.BlockSpec((3, Rh, C), lambda i, s: (0, 0, 0))] * n,
        out_specs=[pl.BlockSpec((Rh, C), lambda i, s: (s[1], 0))] * n)
    return pl.pallas_call(body, grid_spec=grid_spec, name=name, out_shape=[SDS((2 * Rh, C), F32)] * n,
                          compiler_params=_cp("arbitrary"))(sel_arr, *p32s, *ys)


class StepComm:
    FIRST = ("w_in",)
    SQUARE = ("w_out", "w_xq", "w_xk", "w_xv", "w_xo")
    FFN = ("w_up", "w_down")
    PLAN = {
        "proj": ("gather", SQUARE[:3]), "attn_fwd": ("gather", SQUARE[3:] + FFN),
        "d_h3": ("pair", FFN), "mixer_post_bwd_a": ("pair", SQUARE), "attn_bwd": ("chip", FFN + SQUARE),
        "dw_in": ("half", FFN + SQUARE), "first_update": ("pair", FIRST), "d_h1": ("chip", FIRST),
    }

    def __init__(self, placed, conv_placed, sel_arr):
        self.placed, self.conv_placed, self.sel_arr = placed, conv_placed, sel_arr
        self.g, self.p32, self.p16, self.full, self.done, self.jobs = {}, {}, {}, {}, {}, {}

    def job(self, tag, gb=None):
        if tag == "norm_mix":
            self.jobs[tag] = merge_jobs(gather_job(self.placed, self.FIRST), gather_rows_job(self.conv_placed))
            return self.jobs[tag]
        kind, names = self.PLAN[tag]
        if kind == "gather":
            j = gather_job(self.placed, names)
        elif kind == "pair":
            for n in names:
                self.g[n] = gb[n] if gb[n].ndim == 3 else gb[n].reshape(4, gb[n].shape[0] // 4, gb[n].shape[1])
            j = pair_exchange_job(self.g, names)
        elif kind == "chip":
            j = chip_exchange_job(self.p16, names)
        else:
            j = half_exchange_job(self.full, names)
        self.jobs[tag] = j
        return j

    def landed(self, tag, wb=None, ws=None):
        if tag == "norm_mix":
            split_landed(self.jobs[tag])
            first, conv, _ = self.jobs[tag].parts
            wb.update(zip(self.FIRST, first.landed))
            ws.update((n, a[:3]) for n, a in zip(SMALL_SHARDED, conv.landed))
            return
        kind, names = self.PLAN[tag]
        got = dict(zip(names, self.jobs[tag].landed))
        if kind == "gather":
            wb.update(got)
        elif kind == "half":
            self.done.update(got)
        for group in same_shape_groups(names, got):
            if kind == "pair":
                p32s, p16s = pair_sum([self.g[n] for n in group], [got[n] for n in group], self.sel_arr, "pair_sum_" + group[0])
                self.p32.update(zip(group, p32s))
                self.p16.update(zip(group, p16s))
            elif kind == "chip":
                fulls = chip_sum([self.p32[n] for n in group], [got[n] for n in group], self.sel_arr, "chip_sum_" + group[0])
                self.full.update(zip(group, fulls))

    def first_update(self, gb):
        self.first_result = self.update_fn(self.job("first_update", gb))
        self.landed("first_update")

    def finish(self, gb):
        return self.done

    def last_job(self):
        self.jobs["end_half"] = half_exchange_job(self.full, self.FIRST)
        return self.jobs["end_half"]

    def take_last(self):
        self.done.update(zip(self.FIRST, self.jobs["end_half"].landed))


class NoComm:
    def job(self, tag, gb=None):
        return None

    def landed(self, tag, wb=None, ws=None):
        pass

    def first_update(self, gb):
        pass

    def finish(self, gb):
        return gb


SMALL = ("rel_bias", "g_mix", "w_short_conv", "g_attn_out", "g_conv_out", "g_xattn", "g_mem", "g_ffn",
         "w_ffn_conv", "b_ffn_conv", "g_final")
SMALL_SHARDED = ("w_short_conv", "w_ffn_conv")


def kernel(x, mem, rel_bias, g_mix, w_in, w_short_conv, g_attn_out, g_conv_out, w_out, g_xattn, g_mem, w_xq, w_xk, w_xv, w_xo, g_ffn, w_up, w_ffn_conv, b_ffn_conv, w_down, g_final, loss_target, m_rel_bias, m_g_mix, m_w_in, m_w_short_conv, m_g_attn_out, m_g_conv_out, m_w_out, m_g_xattn, m_g_mem, m_w_xq, m_w_xk, m_w_xv, m_w_xo, m_g_ffn, m_w_up, m_w_ffn_conv, m_b_ffn_conv, m_w_down, m_g_final, v_rel_bias, v_g_mix, v_w_in, v_w_short_conv, v_g_attn_out, v_g_conv_out, v_w_out, v_g_xattn, v_g_mem, v_w_xq, v_w_xk, v_w_xv, v_w_xo, v_g_ffn, v_w_up, v_w_ffn_conv, v_b_ffn_conv, v_w_down, v_g_final):
    names = ("rel_bias", "g_mix", "w_in", "w_short_conv", "g_attn_out", "g_conv_out", "w_out", "g_xattn", "g_mem", "w_xq",
             "w_xk", "w_xv", "w_xo", "g_ffn", "w_up", "w_ffn_conv", "b_ffn_conv", "w_down", "g_final")
    W = dict(zip(names, (rel_bias, g_mix, w_in, w_short_conv, g_attn_out, g_conv_out, w_out, g_xattn, g_mem, w_xq, w_xk, w_xv,
                         w_xo, g_ffn, w_up, w_ffn_conv, b_ffn_conv, w_down, g_final)))
    M = dict(zip(names, (m_rel_bias, m_g_mix, m_w_in, m_w_short_conv, m_g_attn_out, m_g_conv_out, m_w_out, m_g_xattn, m_g_mem,
                         m_w_xq, m_w_xk, m_w_xv, m_w_xo, m_g_ffn, m_w_up, m_w_ffn_conv, m_b_ffn_conv, m_w_down, m_g_final)))
    V = dict(zip(names, (v_rel_bias, v_g_mix, v_w_in, v_w_short_conv, v_g_attn_out, v_g_conv_out, v_w_out, v_g_xattn, v_g_mem,
                         v_w_xq, v_w_xk, v_w_xv, v_w_xo, v_g_ffn, v_w_up, v_w_ffn_conv, v_b_ffn_conv, v_w_down, v_g_final)))
    xi, yi, ci = lax.axis_index("x"), lax.axis_index("y"), lax.axis_index("c")
    mine = 2 * xi + yi
    mine_arr = jnp.reshape(mine, (1,)).astype(jnp.int32)
    sel_arr = jnp.stack([mine, ci]).astype(jnp.int32)
    conv_placed = []
    for n in SMALL_SHARDED:
        shard = W[n][0]
        conv_placed.append(lax.dynamic_update_slice(jnp.zeros((8, 4 * shard.shape[1]), F32), shard, (0, mine * shard.shape[1])))
    shards = {n: W[n][0] for n in BIG}
    placed = {}
    for group in same_shape_groups(BIG, shards):
        outs = place_shards([shards[n] for n in group], mine_arr, group[0] in COL_SHARDED, "place_" + group[0])
        placed.update(zip(group, outs))
    comm = StepComm(placed, conv_placed, sel_arr)
    ws = {n: W[n] for n in SMALL if n not in SMALL_SHARDED}

    def update(group, job=None):
        return adamw([W[n][0] for n in group], [comm.done[n] for n in group], [M[n][0] for n in group],
                     [V[n][0] for n in group], "adamw_" + group[0], job=job)

    comm.update_fn = lambda job: update(["w_up"], job)
    loss, grad_x, gfull, gs = local_step(x[0], mem[0], loss_target[0], None, ws, comm)

    def as2d(a):
        return a.reshape(1, -1) if a.ndim == 1 else a

    reduced = allreduce_small([as2d(gs[n]) for n in SMALL] + [loss], "reduce_small", job=comm.last_job())
    comm.take_last()
    gsm = dict(zip(SMALL, reduced[:-1]))
    loss = reduced[-1][0, 0]

    grads, delta, new_m, new_v = {}, {}, {}, {}
    for group in same_shape_groups(BIG, gfull):
        ds, nms, nvs = comm.first_result if group == ["w_up"] else update(group)
        for n, d, nm, nv in zip(group, ds, nms, nvs):
            grads[n], delta[n], new_m[n], new_v[n] = gfull[n][None], d[None], nm[None], nv[None]
    for n in SMALL_SHARDED:
        wid = W[n].shape[2]
        gsm[n] = lax.dynamic_slice(gsm[n], (0, mine * wid), (3, wid))

    def own(a, n):
        return a[0] if n in SMALL_SHARDED else as2d(a)

    d, nm, nv = adamw_multi([own(W[n], n) for n in SMALL], [gsm[n] for n in SMALL], [own(M[n], n) for n in SMALL],
                            [own(V[n], n) for n in SMALL], "adamw_small")
    for i, n in enumerate(SMALL):
        shape = W[n].shape
        grads[n], delta[n], new_m[n], new_v[n] = (a.reshape(shape) for a in (gsm[n], d[i], nm[i], nv[i]))
    return (loss, grad_x[None], *[grads[n] for n in names], *[delta[n] for n in names],
            *[new_m[n] for n in names], *[new_v[n] for n in names])
```

```python
import math

import numpy as np
import jax
import jax.numpy as jnp
from jax import lax
from jax.experimental import pallas as pl
from jax.experimental.pallas import tpu as pltpu

F32 = jnp.float32
BF16 = jnp.bfloat16
SDS = jax.ShapeDtypeStruct
MESH = pl.DeviceIdType.MESH

D_MODEL = 1024
ATTN_W = 512
N_HEADS = 8
HEAD_DIM = 64
HALF_HEAD = HEAD_DIM // 2
Q_SCALE = HEAD_DIM ** -0.5
BLK = 128
PATTERNS = ((128, 1), (512, 4), (2048, 16))
N_BUCKETS = 32
BUCKET_MAX_DIST = 2048
D_FF = 2816
N_MEM_HEADS = 4
MEM_HD = 256
EPS = 1e-6
NEG = -1e30
VMEM_LIMIT = 56 * 1024 * 1024

ADAM_LR, ADAM_B1, ADAM_B2, ADAM_EPS, ADAM_WD, ADAM_STEP = 0.001, 0.9, 0.999, 1e-08, 0.01, 10


def _cp(*sem):
    return pltpu.CompilerParams(dimension_semantics=sem, vmem_limit_bytes=VMEM_LIMIT)


def _dot(a, b):
    return jnp.dot(a, b, preferred_element_type=F32)


def _dot_nt(a, b):
    return lax.dot_general(a, b, (((1,), (1,)), ((), ())), preferred_element_type=F32)


def _dot_tn(a, b):
    return lax.dot_general(a, b, (((0,), (0,)), ((), ())), preferred_element_type=F32)


def _rsum(x):
    return jnp.sum(x, axis=1, keepdims=True)


def rmsnorm_fwd(x, g, name, job=None):
    S, Dm = x.shape
    tm = min(S, 512)

    def body(x_ref, g_ref, o_ref):
        xv = x_ref[...]
        r = lax.rsqrt(jnp.mean(xv * xv, axis=1, keepdims=True) + EPS)
        o_ref[...] = (xv * r * g_ref[...]).astype(o_ref.dtype)

    return _pcall(body, [x, g], grid=(S // tm,), name=name,
                  in_specs=[pl.BlockSpec((tm, Dm), lambda i: (i, 0)), pl.BlockSpec((1, Dm), lambda i: (0, 0))],
                  out_specs=[pl.BlockSpec((tm, Dm), lambda i: (i, 0))], out_shape=[SDS((S, Dm), BF16)],
                  sem=("parallel",), job=job)[0]


def rmsnorm_gain_grad(x, dh, name):
    S, Dm = x.shape
    tm = min(S, 512)

    def body(x_ref, dh_ref, dg_ref):
        xv = x_ref[...]
        xh = xv * lax.rsqrt(jnp.mean(xv * xv, axis=1, keepdims=True) + EPS)

        @pl.when(pl.program_id(0) == 0)
        def _():
            dg_ref[...] = jnp.zeros_like(dg_ref)

        dg_ref[...] += jnp.sum(dh_ref[...].astype(F32) * xh, axis=0, keepdims=True)

    row = pl.BlockSpec((tm, Dm), lambda i: (i, 0))
    return pl.pallas_call(
        body, grid=(S // tm,), name=name, in_specs=[row, row], out_specs=pl.BlockSpec((1, Dm), lambda i: (0, 0)),
        out_shape=SDS((1, Dm), F32), compiler_params=_cp("arbitrary"))(x, dh)


def _col_spec(bw, tn, rows, row_of, col_of):
    if bw is None:
        return pl.BlockSpec((rows, tn), lambda *g: (row_of(*g), col_of(*g)))
    if tn % bw == 0:
        return pl.BlockSpec((tn // bw, rows, bw), lambda *g: (col_of(*g), row_of(*g), 0))
    per = bw // tn
    return pl.BlockSpec((None, rows, tn), lambda *g: (col_of(*g) // per, row_of(*g), col_of(*g) % per))


def _read_cols(ref, bw, tn):
    if bw is None or tn % bw != 0:
        return ref[...]
    if tn == bw:
        return ref[0]
    return jnp.concatenate([ref[q] for q in range(tn // bw)], axis=1)


def _write_cols(ref, bw, tn, val):
    if bw is None or tn % bw != 0:
        ref[...] = val.astype(ref.dtype)
    else:
        for q in range(tn // bw):
            ref[q] = val[:, q * bw:(q + 1) * bw].astype(ref.dtype)


def mm_nn(a, b, name, *, out_dtype=F32, out_bw=None, tm=1024, tn=512, job=None):
    M, K = a.shape
    N = b.shape[1]
    tm = min(tm, M)

    def body(a_ref, b_ref, o_ref):
        _write_cols(o_ref, out_bw, tn, _dot(a_ref[...].astype(BF16), b_ref[...]))

    ri, ci = (lambda i, j: i), (lambda i, j: j)
    in_specs = [pl.BlockSpec((tm, K), lambda i, j: (i, 0)), pl.BlockSpec((K, tn), lambda i, j: (0, j))]
    oshape = (M, N) if out_bw is None else (N // out_bw, M, out_bw)
    return _pcall(body, [a, b], grid=(M // tm, N // tn), name=name, in_specs=in_specs,
                  out_specs=[_col_spec(out_bw, tn, tm, ri, ci)], out_shape=[SDS(oshape, out_dtype)],
                  sem=("parallel", "parallel"), job=job)[0]


def mm_nn_rows(a, b, res, g, name, *, target=None, tm=512):
    M, K = a.shape
    N = b.shape[1]

    def body(*refs):
        a_ref, b_ref, r_ref, g_ref = refs[:4]
        xv = r_ref[...] + _dot(a_ref[...].astype(BF16), b_ref[...])
        r = lax.rsqrt(jnp.mean(xv * xv, axis=1, keepdims=True) + EPS)
        xh = xv * r
        if target is None:
            x_ref, h_ref = refs[4:]
            x_ref[...] = xv
            h_ref[...] = (xh * g_ref[...]).astype(BF16)
            return
        t_ref, loss_ref, dx_ref, dg_ref = refs[4:]
        err = xh * g_ref[...] - t_ref[...]
        dy = err / N
        gd = dy * g_ref[...]
        dx_ref[...] = r * (gd - xh * jnp.mean(gd * xh, axis=1, keepdims=True))

        @pl.when(pl.program_id(0) == 0)
        def _():
            dg_ref[...] = jnp.zeros_like(dg_ref)
            loss_ref[...] = jnp.zeros_like(loss_ref)

        dg_ref[...] += _colsum(dy * xh)
        loss_ref[...] += 0.5 * _colsum(jnp.mean(err * err, axis=1, keepdims=True))

    row = pl.BlockSpec((tm, N), lambda i: (i, 0))
    vec = pl.BlockSpec((1, N), lambda i: (0, 0))
    in_specs = [pl.BlockSpec((tm, K), lambda i: (i, 0)), pl.BlockSpec((K, N), lambda i: (0, 0)), row, vec]
    if target is None:
        return pl.pallas_call(body, grid=(M // tm,), name=name, in_specs=in_specs, out_specs=[row, row],
                              out_shape=[SDS((M, N), F32), SDS((M, N), BF16)], compiler_params=_cp("parallel"))(a, b, res, g)
    one = pl.BlockSpec((1, 1), lambda i: (0, 0))
    return pl.pallas_call(body, grid=(M // tm,), name=name, in_specs=in_specs + [row], out_specs=[one, row, vec],
                          out_shape=[SDS((1, 1), F32), SDS((M, N), F32), SDS((1, N), F32)],
                          compiler_params=_cp("arbitrary"))(a, b, res, g, target)


def mm_nt(a, a_bw, b, name, *, res=None, out_dtype=F32, tm=1024, tn=512, job=None):
    if a_bw is None:
        M, K = a.shape
    else:
        M, K = a.shape[1], a.shape[0] * a_bw
    N = b.shape[0]
    tm = min(tm, M)

    def body(*refs):
        if res is None:
            a_ref, b_ref, o_ref = refs
        else:
            a_ref, b_ref, r_ref, o_ref = refs
        av = _read_cols(a_ref, a_bw, K).astype(BF16)
        acc = _dot_nt(av, b_ref[...])
        if res is not None:
            acc = acc + r_ref[...]
        o_ref[...] = acc.astype(o_ref.dtype)

    in_specs = [_col_spec(a_bw, K, tm, lambda i, j: i, lambda i, j: 0), pl.BlockSpec((tn, K), lambda i, j: (j, 0))]
    args = [a, b]
    if res is not None:
        in_specs.append(pl.BlockSpec((tm, tn), lambda i, j: (i, j)))
        args.append(res)
    return _pcall(body, args, grid=(M // tm, N // tn), name=name, in_specs=in_specs,
                  out_specs=[pl.BlockSpec((tm, tn), lambda i, j: (i, j))], out_shape=[SDS((M, N), out_dtype)],
                  sem=("parallel", "parallel"), job=job)[0]


def mm_nt_norm_bwd(a, a_bw, b, x, g, res, name, *, tm=512, job=None):
    if a_bw is None:
        M, K = a.shape
    else:
        M, K = a.shape[1], a.shape[0] * a_bw
    N = b.shape[0]

    def body(a_ref, b_ref, x_ref, g_ref, r_ref, dx_ref, dg_ref):
        dh = _dot_nt(_read_cols(a_ref, a_bw, K).astype(BF16), b_ref[...])
        xv = x_ref[...]
        r = lax.rsqrt(jnp.mean(xv * xv, axis=1, keepdims=True) + EPS)
        xh = xv * r
        gd = dh * g_ref[...]
        dx_ref[...] = r_ref[...] + r * (gd - xh * jnp.mean(gd * xh, axis=1, keepdims=True))

        @pl.when(pl.program_id(0) == 0)
        def _():
            dg_ref[...] = jnp.zeros_like(dg_ref)

        dg_ref[...] += jnp.sum(dh * xh, axis=0, keepdims=True)

    row = pl.BlockSpec((tm, N), lambda i: (i, 0))
    vec = pl.BlockSpec((1, N), lambda i: (0, 0))
    return _pcall(body, [a, b, x, g, res], grid=(M // tm,), name=name,
                  in_specs=[_col_spec(a_bw, K, tm, lambda i: i, lambda i: 0), pl.BlockSpec((N, K), lambda i: (0, 0)), row, vec, row],
                  out_specs=[row, vec], out_shape=[SDS((M, N), F32), SDS((1, N), F32)], sem=("arbitrary",), job=job)


def mm_tn(a, b, b_bw, name, *, shards=None, tm=1024, tn=1024, ts=1024, job=None):
    S, Ka = a.shape
    N = b.shape[1] if b_bw is None else b.shape[0] * b_bw
    ts = min(ts, S)
    tm = min(tm, Ka)

    def body(a_ref, b_ref, o_ref):
        @pl.when(pl.program_id(2) == 0)
        def _():
            o_ref[...] = jnp.zeros_like(o_ref)

        bv = _read_cols(b_ref, b_bw, tn).astype(BF16)
        o_ref[...] += _dot_tn(a_ref[...].astype(BF16), bv)

    in_specs = [pl.BlockSpec((ts, tm), lambda i, j, k: (k, i)),
                _col_spec(b_bw, tn, ts, lambda i, j, k: k, lambda i, j, k: j)]
    if shards is None:
        out_spec, oshape = pl.BlockSpec((tm, tn), lambda i, j, k: (i, j)), (Ka, N)
    else:
        per = (N // shards) // tn
        out_spec = pl.BlockSpec((None, tm, tn), lambda i, j, k: (j // per, i, j % per))
        oshape = (shards, Ka, N // shards)
    return _pcall(body, [a, b], grid=(Ka // tm, N // tn, S // ts), name=name, in_specs=in_specs, out_specs=[out_spec],
                  out_shape=[SDS(oshape, F32)], sem=("parallel", "parallel", "arbitrary"), job=job)[0]


KEYS = 2 * BLK


def _bucket_tables():
    out = np.zeros((3, BLK, KEYS), np.int32)
    qi = np.arange(BLK)[:, None]
    kj = np.arange(KEYS)[None, :]
    for p, (win, dil) in enumerate(PATTERNS):
        w = win // dil
        assert w == BLK
        steps = qi + w - kj
        valid = (steps >= 0) & (steps <= w)
        dist = np.clip(steps, 0, w) * dil
        dd = np.maximum(dist, 1).astype(np.float32)
        exact = N_BUCKETS // 2
        large = exact + (np.log(dd / np.float32(exact)) / np.float32(math.log(BUCKET_MAX_DIST / exact))
                         * np.float32(N_BUCKETS - exact)).astype(np.int32)
        large = np.minimum(large, N_BUCKETS - 1)
        out[p] = np.where(valid, np.where(dist < exact, dist, large), -1)
    return out


def bias_tables(rel_bias):
    bkt = jnp.asarray(_bucket_tables())

    def body(rb_ref, bkt_ref, o_ref):
        bk = bkt_ref[...]
        for h in range(N_HEADS):
            acc = jnp.full((BLK, KEYS), NEG, F32)
            for b in range(N_BUCKETS):
                acc = jnp.where(bk == b, rb_ref[h, b], acc)
            o_ref[h] = acc

    return pl.pallas_call(
        body, grid=(3,), name="bias_tables",
        in_specs=[pl.BlockSpec(memory_space=pltpu.SMEM),
                  pl.BlockSpec((None, BLK, KEYS), lambda p: (p, 0, 0))],
        out_specs=pl.BlockSpec((None, N_HEADS, BLK, KEYS), lambda p: (p, 0, 0, 0)),
        out_shape=SDS((3, N_HEADS, BLK, KEYS), F32), compiler_params=_cp("parallel"))(rel_bias, bkt)


def bias_tables_bwd(ds_sum):
    bkt = jnp.asarray(_bucket_tables())

    def body(ds_ref, bkt_ref, o_ref):
        h = pl.program_id(0)

        @pl.when(h == 0)
        def _():
            o_ref[...] = jnp.zeros_like(o_ref)

        rows = lax.broadcasted_iota(jnp.int32, (N_HEADS, N_BUCKETS), 0)
        cols = lax.broadcasted_iota(jnp.int32, (N_HEADS, N_BUCKETS), 1)
        acc = jnp.zeros((N_HEADS, N_BUCKETS), F32)
        for b in range(N_BUCKETS):
            t = jnp.zeros((BLK, KEYS), F32)
            for p in range(3):
                t = t + jnp.where(bkt_ref[p] == b, ds_ref[p], 0.0)
            tot = jnp.sum(_rsum(t), axis=0, keepdims=True)
            acc = acc + jnp.where((rows == h) & (cols == b), tot, 0.0)
        o_ref[...] += acc

    return pl.pallas_call(
        body, grid=(N_HEADS,), name="bias_tables_bwd",
        in_specs=[pl.BlockSpec((3, None, BLK, KEYS), lambda h: (0, h, 0, 0)),
                  pl.BlockSpec((3, BLK, KEYS), lambda h: (0, 0, 0))],
        out_specs=pl.BlockSpec((N_HEADS, N_BUCKETS), lambda h: (0, 0)),
        out_shape=SDS((N_HEADS, N_BUCKETS), F32), compiler_params=_cp("arbitrary"))(ds_sum, bkt)


def _rows(start, dil):
    if dil == 1:
        return pl.ds(pl.multiple_of(start, BLK), BLK)
    return pl.ds(start, BLK, stride=dil)


GRP = 8


def _group_rows(i, dil, S):
    nb = S // (BLK * dil)
    run = min(nb, GRP)
    chunks = nb // run
    res0, b0 = (i // chunks) * (GRP // run), (i % chunks) * run
    cur = [(b0 + j % run) * (BLK * dil) + res0 + j // run for j in range(GRP)]
    t = lax.broadcasted_iota(jnp.int32, (GRP, 1, 1), 0)
    if run == nb:
        before, pen = None, jnp.where(t % run == 0, NEG, 0.0)
    else:
        before = _rows(jnp.maximum(b0 - 1, 0) * (BLK * dil) + res0, dil)
        pen = jnp.where((t == 0) & (b0 == 0), NEG, 0.0)
    return [_rows(s, dil) for s in cur], before, pen


def _load_group(ref, rows):
    return jnp.stack([ref[r, :] for r in rows])


def _with_prev(ref, before, cur_blocks):
    first = cur_blocks[:1] if before is None else ref[before, :][None].astype(cur_blocks.dtype)
    return jnp.concatenate([first, cur_blocks[:-1]], axis=0)


def _bdot_nt(a, b):
    return lax.dot_general(a, b, (((2,), (2,)), ((0,), (0,))), preferred_element_type=F32)


def _bdot(a, b):
    return lax.dot_general(a, b, (((2,), (1,)), ((0,), (0,))), preferred_element_type=F32)


def _bdot_tn(a, b):
    return lax.dot_general(a, b, (((1,), (1,)), ((0,), (0,))), preferred_element_type=F32)


def _lsum(x):
    return jnp.sum(x, axis=-1, keepdims=True)


def _widen(src, dst):
    S = src.shape[1]

    def chunk(i, carry):
        rows = pl.ds(pl.multiple_of(i * 512, 512), 512)
        for a in range(3):
            dst[a, rows, :] = src[a, rows, :].astype(F32)
        return carry

    lax.fori_loop(0, S // 512, chunk, 0)


def attn_fwd(qkv, bias, job=None):
    S = qkv.shape[3]
    nblk = S // BLK
    car = Carried(job, 2, 2, 1, 4)

    def body(*refs):
        (qkv_in, bias_ref, o_ref, lse_ref, qkv_ref), jrefs = car.split(refs)
        car.phase(0, pl.program_id(0), jrefs)
        _widen(qkv_in, qkv_ref)
        lane = lax.broadcasted_iota(jnp.int32, (GRP, BLK, BLK), 2)
        lo = lane < HEAD_DIM
        masks = (lo, jnp.logical_not(lo))
        lov = lax.broadcasted_iota(jnp.int32, (GRP, KEYS, BLK), 2) < HEAD_DIM
        vmasks = (lov, jnp.logical_not(lov))
        prev_keys = lax.broadcasted_iota(jnp.int32, (1, 1, KEYS), 2) < BLK
        q_ref, k_ref, v_ref = qkv_ref.at[0], qkv_ref.at[1], qkv_ref.at[2]
        for p, (_, dil) in enumerate(PATTERNS):
            def step(i, carry, p=p, dil=dil):
                rc, before, pen = _group_rows(i, dil, S)
                q2 = _load_group(q_ref, rc) * Q_SCALE
                kc = _load_group(k_ref, rc).astype(BF16)
                keys = jnp.concatenate([_with_prev(k_ref, before, kc), kc], axis=1)
                vc = _load_group(v_ref, rc)
                vals = jnp.concatenate([_with_prev(v_ref, before, vc), vc], axis=1)
                pen = jnp.where(prev_keys, pen, 0.0)
                if p > 0:
                    acc_old, ml_old = _load_group(o_ref, rc), _load_group(lse_ref, rc)
                pv = jnp.zeros((GRP, BLK, BLK), F32)
                m_new, l_new, alpha = [], [], []
                for h in range(2):
                    qh = jnp.where(masks[h], q2, 0.0).astype(BF16)
                    s = _bdot_nt(qh, keys) + bias_ref[p, h][None] + pen
                    mn = jnp.max(s, axis=-1, keepdims=True)
                    if p > 0:
                        mo = ml_old[:, :, HEAD_DIM * h:HEAD_DIM * h + 1]
                        mn = jnp.maximum(mn, mo)
                        al = jnp.exp(mo - mn)
                        alpha.append(al)
                    pr = jnp.exp(s - mn)
                    ln = _lsum(pr)
                    if p > 0:
                        ln = ln + al * ml_old[:, :, HEAD_DIM * h + HALF_HEAD:HEAD_DIM * h + HALF_HEAD + 1]
                    pv = pv + _bdot(pr.astype(BF16), jnp.where(vmasks[h], vals, 0.0).astype(BF16))
                    m_new.append(mn)
                    l_new.append(ln)
                if p > 0:
                    pv = pv + acc_old * jnp.where(lo, alpha[0], alpha[1])
                ml_t = jnp.where(lo, jnp.where(lane < HALF_HEAD, m_new[0], l_new[0]),
                                 jnp.where(lane < HEAD_DIM + HALF_HEAD, m_new[1], l_new[1]))
                for j, r in enumerate(rc):
                    o_ref[r, :] = pv[j]
                    lse_ref[r, :] = ml_t[j]
                return carry

            lax.fori_loop(0, nblk // GRP, step, 0)

        def fin(i, carry):
            rows = pl.ds(pl.multiple_of(i * 512, 512), 512)
            ml = lse_ref[rows, :]
            is_m = (lax.broadcasted_iota(jnp.int32, ml.shape, 1) & HALF_HEAD) == 0
            den = jnp.where(is_m, pltpu.roll(ml, BLK - HALF_HEAD, 1), ml)
            o_ref[rows, :] = o_ref[rows, :] / den
            lse_ref[rows, :] = ml + jnp.log(den)
            return carry

        lax.fori_loop(0, S // 512, fin, 0)
        car.phase(1, pl.program_id(0), jrefs)
        car.phase(2, pl.program_id(0), jrefs)

    outs = pl.pallas_call(
        body, grid=(4,), name="attn_fwd",
        in_specs=[pl.BlockSpec((None, 3, None, S, BLK), lambda g: (0, 0, g, 0, 0)),
                  pl.BlockSpec((3, 2, BLK, KEYS), lambda g: (0, g, 0, 0))] + car.in_specs(),
        out_specs=[pl.BlockSpec((None, S, BLK), lambda g: (g, 0, 0)),
                   pl.BlockSpec((None, S, BLK), lambda g: (g, 0, 0))] + car.out_specs(),
        out_shape=[SDS((4, S, BLK), F32), SDS((4, S, BLK), F32)] + car.out_shapes(),
        input_output_aliases=car.aliases(),
        scratch_shapes=[pltpu.VMEM((3, S, BLK), F32)] + car.sems(),
        compiler_params=_cp("arbitrary"))(qkv, bias, *car.args())
    if job is not None:
        job.landed = list(outs[2:])
    return outs[0], outs[1]


def attn_bwd(qkv, bias, o, lse, do, job=None):
    S = qkv.shape[3]
    nblk = S // BLK
    car = Carried(job, 5, 2, 2, 4)

    def body(*refs):
        (qkv_in, bias_ref, o_ref, lse_ref, do_ref, dqkv_out, ds_ref, qkv_ref, dqkv_ref), jrefs = car.split(refs)
        car.phase(0, pl.program_id(0), jrefs)
        _widen(qkv_in, qkv_ref)
        lane = lax.broadcasted_iota(jnp.int32, (GRP, BLK, BLK), 2)
        lo = lane < HEAD_DIM
        masks = (lo, jnp.logical_not(lo))
        lov = lax.broadcasted_iota(jnp.int32, (GRP, KEYS, BLK), 2) < HEAD_DIM
        vmasks = (lov, jnp.logical_not(lov))
        prev_keys = lax.broadcasted_iota(jnp.int32, (1, 1, KEYS), 2) < BLK
        dqkv_ref[...] = jnp.zeros_like(dqkv_ref)
        ds_ref[...] = jnp.zeros_like(ds_ref)
        q_ref, k_ref, v_ref = qkv_ref.at[0], qkv_ref.at[1], qkv_ref.at[2]
        dq_ref, dk_ref, dv_ref = dqkv_ref.at[0], dqkv_ref.at[1], dqkv_ref.at[2]
        for p, (_, dil) in enumerate(PATTERNS):
            def step(i, carry, p=p, dil=dil):
                rc, before, pen = _group_rows(i, dil, S)
                q2 = _load_group(q_ref, rc) * Q_SCALE
                kc = _load_group(k_ref, rc).astype(BF16)
                keys = jnp.concatenate([_with_prev(k_ref, before, kc), kc], axis=1)
                vc = _load_group(v_ref, rc).astype(BF16)
                vals = jnp.concatenate([_with_prev(v_ref, before, vc), vc], axis=1)
                pen = jnp.where(prev_keys, pen, 0.0)
                dot = _load_group(do_ref, rc)
                lset = _load_group(lse_ref, rc)
                prod = dot * _load_group(o_ref, rc)
                dq = jnp.zeros((GRP, BLK, BLK), F32)
                dkeys = jnp.zeros((GRP, KEYS, BLK), F32)
                dvals = jnp.zeros((GRP, KEYS, BLK), F32)
                for h in range(2):
                    qh = jnp.where(masks[h], q2, 0.0).astype(BF16)
                    doh = jnp.where(masks[h], dot, 0.0).astype(BF16)
                    delta = _lsum(jnp.where(masks[h], prod, 0.0))
                    lh = lset[:, :, HEAD_DIM * h:HEAD_DIM * h + 1]
                    pr = jnp.exp(_bdot_nt(qh, keys) + bias_ref[p, h][None] + pen - lh)
                    ds = pr * (_bdot_nt(doh, vals) - delta)
                    ds_ref[p, h] += jnp.sum(ds, axis=0)
                    dsb = ds.astype(BF16)
                    dq = dq + jnp.where(masks[h], _bdot(dsb, keys), 0.0)
                    dkeys = dkeys + _bdot_tn(dsb, qh)
                    dvals = dvals + _bdot_tn(pr.astype(BF16), doh)
                dkp, dkc, dvp, dvc = dkeys[:, :BLK], dkeys[:, BLK:], dvals[:, :BLK], dvals[:, BLK:]
                none = jnp.zeros((1, BLK, BLK), F32)
                dkc = dkc + jnp.concatenate([dkp[1:], none], axis=0)
                dvc = dvc + jnp.concatenate([dvp[1:], none], axis=0)
                for j in range(GRP):
                    dq_ref[rc[j], :] += dq[j] * Q_SCALE
                    dk_ref[rc[j], :] += dkc[j]
                    dv_ref[rc[j], :] += dvc[j]
                if before is not None:
                    dk_ref[before, :] += dkp[0]
                    dv_ref[before, :] += dvp[0]
                return carry

            lax.fori_loop(0, nblk // GRP, step, 0)

        def narrow(i, carry):
            rows = pl.ds(pl.multiple_of(i * 512, 512), 512)
            for a in range(3):
                dqkv_out[a, rows, :] = dqkv_ref[a, rows, :].astype(BF16)
            return carry

        lax.fori_loop(0, S // 512, narrow, 0)
        car.phase(1, pl.program_id(0), jrefs)
        car.phase(2, pl.program_id(0), jrefs)

    blk = pl.BlockSpec((None, S, BLK), lambda g: (g, 0, 0))
    outs = pl.pallas_call(
        body, grid=(4,), name="attn_bwd",
        in_specs=[pl.BlockSpec((None, 3, None, S, BLK), lambda g: (0, 0, g, 0, 0)),
                  pl.BlockSpec((3, 2, BLK, KEYS), lambda g: (0, g, 0, 0)), blk, blk, blk] + car.in_specs(),
        out_specs=[pl.BlockSpec((None, 3, None, S, BLK), lambda g: (0, 0, g, 0, 0)),
                   pl.BlockSpec((3, 2, BLK, KEYS), lambda g: (0, g, 0, 0))] + car.out_specs(),
        out_shape=[SDS((2, 3, 4, S, BLK), BF16), SDS((3, N_HEADS, BLK, KEYS), F32)] + car.out_shapes(),
        input_output_aliases=car.aliases(),
        scratch_shapes=[pltpu.VMEM((3, S, BLK), F32), pltpu.VMEM((3, S, BLK), F32)] + car.sems(),
        compiler_params=_cp("arbitrary"))(qkv, bias, o, lse, do, *car.args())
    if job is not None:
        job.landed = list(outs[2:])
    return outs[0], outs[1]


def _shift_down(u, k, halo):
    n = u.shape[0]
    row = lax.broadcasted_iota(jnp.int32, u.shape, 0)
    out = pltpu.roll(u, k, 0)
    hn = halo.shape[0]
    for j in range(k):
        out = jnp.where(row == j, halo[hn - k + j:hn - k + j + 1, :], out)
    return out


def _shift_up(u, k, halo):
    n = u.shape[0]
    row = lax.broadcasted_iota(jnp.int32, u.shape, 0)
    out = pltpu.roll(u, n - k, 0)
    for j in range(k):
        out = jnp.where(row == n - k + j, halo[j:j + 1, :], out)
    return out


def _conv3(u, halo, w0, w1, w2):
    return _shift_down(u, 2, halo) * w0 + _shift_down(u, 1, halo) * w1 + u * w2


def _colsum(x):
    return jnp.sum(x, axis=0, keepdims=True)


def _mixer_specs(S, tm):
    conv = pl.BlockSpec((None, 12, tm, BLK), lambda i: (1, 0, i, 0))
    halo = pl.BlockSpec((None, 12, 16, BLK), lambda i: (1, 0, jnp.maximum(i * (tm // 16) - 1, 0), 0))
    ob = pl.BlockSpec((4, tm, BLK), lambda i: (0, i, 0))
    return conv, halo, ob


def _mixer_recompute(i, o_ref, pr_ref, ph_ref, w_ref):
    ob = [o_ref[q] for q in range(4)]
    gb = [pr_ref[q].astype(F32) for q in range(4)]
    gc = [pr_ref[4 + q].astype(F32) for q in range(4)]
    xi = [pr_ref[8 + q].astype(F32) for q in range(4)]
    keep = jnp.where(i > 0, 1.0, 0.0)
    u = [gc[q] * xi[q] for q in range(4)]
    hu = [ph_ref[4 + q].astype(F32) * ph_ref[8 + q].astype(F32) * keep for q in range(4)]
    w = [[w_ref[k:k + 1, q * BLK:(q + 1) * BLK] for k in range(3)] for q in range(4)]
    cv = [_conv3(u[q], hu[q], *w[q]) for q in range(4)]
    return ob, gb, gc, xi, u, hu, cv, w


def _rms_blocks(blocks):
    ss = sum(_rsum(b * b) for b in blocks)
    return lax.rsqrt(ss / (BLK * len(blocks)) + EPS)


def mixer_post_fwd(o, proj, w_sc, g_a, g_c):
    S = o.shape[1]
    tm = 512

    def body(o_ref, pr_ref, ph_ref, w_ref, ga_ref, gc_ref, m_ref):
        i = pl.program_id(0)
        ob, gb, _, _, _, _, cv, _ = _mixer_recompute(i, o_ref, pr_ref, ph_ref, w_ref)
        conv = [gb[q] * cv[q] for q in range(4)]
        ra, rc = _rms_blocks(ob), _rms_blocks(conv)
        for q in range(4):
            sl = slice(q * BLK, (q + 1) * BLK)
            m_ref[:, q * BLK:(q + 1) * BLK] = (ob[q] * ra * ga_ref[:, sl]).astype(BF16)
            m_ref[:, ATTN_W + q * BLK:ATTN_W + (q + 1) * BLK] = (conv[q] * rc * gc_ref[:, sl]).astype(BF16)

    conv_s, halo_s, ob_s = _mixer_specs(S, tm)
    full = lambda r, c: pl.BlockSpec((r, c), lambda i: (0, 0))
    return pl.pallas_call(
        body, grid=(S // tm,), name="mixer_post_fwd",
        in_specs=[ob_s, conv_s, halo_s, full(3, 512), full(1, 512), full(1, 512)],
        out_specs=pl.BlockSpec((tm, D_MODEL), lambda i: (i, 0)), out_shape=SDS((S, D_MODEL), BF16),
        compiler_params=_cp("parallel"))(o, proj, proj, w_sc, g_a, g_c)


def mixer_post_bwd_a(dmixed, o, proj, w_sc, g_a, g_c, job=None):
    S = o.shape[1]
    tm = 512

    def body(dm_ref, o_ref, pr_ref, ph_ref, w_ref, ga_ref, gc_ref, do_ref, dgb_ref, dcv_ref, dga_ref, dgc_ref):
        i = pl.program_id(0)
        ob, gb, _, _, _, _, cv, _ = _mixer_recompute(i, o_ref, pr_ref, ph_ref, w_ref)
        conv = [gb[q] * cv[q] for q in range(4)]
        ra, rc = _rms_blocks(ob), _rms_blocks(conv)

        @pl.when(i == 0)
        def _():
            dga_ref[...] = jnp.zeros_like(dga_ref)
            dgc_ref[...] = jnp.zeros_like(dgc_ref)

        for blocks, r, g_ref, off, dg_ref, is_attn in ((ob, ra, ga_ref, 0, dga_ref, True), (conv, rc, gc_ref, ATTN_W, dgc_ref, False)):
            xh = [blocks[q] * r for q in range(4)]
            dm = [dm_ref[:, off + q * BLK:off + (q + 1) * BLK].astype(F32) for q in range(4)]
            gd = [dm[q] * g_ref[:, q * BLK:(q + 1) * BLK] for q in range(4)]
            mean = sum(_rsum(gd[q] * xh[q]) for q in range(4)) / (4 * BLK)
            for q in range(4):
                dg_ref[:, q * BLK:(q + 1) * BLK] += _colsum(dm[q] * xh[q])
                dx = r * (gd[q] - xh[q] * mean)
                if is_attn:
                    do_ref[q] = dx
                else:
                    dgb_ref[q] = dx * cv[q]
                    dcv_ref[q] = dx * gb[q]

    conv_s, halo_s, ob_s = _mixer_specs(S, tm)
    full = lambda r, c: pl.BlockSpec((r, c), lambda i: (0, 0))
    return _pcall(
        body, [dmixed, o, proj, proj, w_sc, g_a, g_c], grid=(S // tm,), name="mixer_post_bwd_a",
        in_specs=[pl.BlockSpec((tm, D_MODEL), lambda i: (i, 0)), ob_s, conv_s, halo_s, full(3, 512), full(1, 512), full(1, 512)],
        out_specs=[ob_s, ob_s, ob_s, full(1, 512), full(1, 512)],
        out_shape=[SDS((4, S, BLK), F32)] * 3 + [SDS((1, 512), F32)] * 2, sem=("arbitrary",), job=job)


def mixer_post_bwd_b(dproj, dgb, dcv, proj, w_sc):
    S = proj.shape[2]
    tm = 512
    last = S // 8 - 1

    def body(dp_in, dgb_ref, dcv_ref, dn_ref, pr_ref, w_ref, dp_ref, dw_ref):
        i = pl.program_id(0)
        keep_next = jnp.where(i < pl.num_programs(0) - 1, 1.0, 0.0)

        @pl.when(i == 0)
        def _():
            dw_ref[...] = jnp.zeros_like(dw_ref)

        for q in range(4):
            sl = slice(q * BLK, (q + 1) * BLK)
            gc, xi = pr_ref[4 + q].astype(F32), pr_ref[8 + q].astype(F32)
            u = gc * xi
            dcv = dcv_ref[q]
            dn = dn_ref[q] * keep_next
            d1, d2 = _shift_up(dcv, 1, dn), _shift_up(dcv, 2, dn)
            w0, w1, w2 = (w_ref[k:k + 1, sl] for k in range(3))
            du = dcv * w2 + d1 * w1 + d2 * w0
            dw_ref[0:1, sl] += _colsum(d2 * u)
            dw_ref[1:2, sl] += _colsum(d1 * u)
            dw_ref[2:3, sl] += _colsum(dcv * u)
            dp_ref[q] = dgb_ref[q].astype(BF16)
            dp_ref[4 + q] = (du * xi).astype(BF16)
            dp_ref[8 + q] = (du * gc).astype(BF16)

    conv_s, _, ob_s = _mixer_specs(S, tm)
    nxt = pl.BlockSpec((4, 8, BLK), lambda i: (0, jnp.minimum((i + 1) * (tm // 8), last), 0))
    full = lambda r, c: pl.BlockSpec((r, c), lambda i: (0, 0))
    return pl.pallas_call(
        body, grid=(S // tm,), name="mixer_post_bwd_b",
        in_specs=[pl.BlockSpec(memory_space=pl.ANY), ob_s, ob_s, nxt, conv_s, full(3, 512)],
        out_specs=[conv_s, full(3, 512)],
        out_shape=[SDS(dproj.shape, BF16), SDS((3, 512), F32)],
        input_output_aliases={0: 0}, compiler_params=_cp("arbitrary"))(dproj, dgb, dcv, dcv, proj, w_sc)


def xattn_fwd(q, k, v):
    S = q.shape[0]
    tm = 512
    scale = MEM_HD ** -0.5

    def body(q_ref, k_ref, v_ref, o_ref):
        for h in range(N_MEM_HEADS):
            sl = slice(h * MEM_HD, (h + 1) * MEM_HD)
            s = _dot_nt(q_ref[:, sl], k_ref[:, sl]) * scale
            p = jnp.exp(s - jnp.max(s, axis=1, keepdims=True))
            p = p / _rsum(p)
            o_ref[:, sl] = _dot(p.astype(BF16), v_ref[:, sl]).astype(BF16)

    row = pl.BlockSpec((tm, D_MODEL), lambda i: (i, 0))
    kv = pl.BlockSpec(k.shape, lambda i: (0, 0))
    return pl.pallas_call(body, grid=(S // tm,), name="xattn_fwd", in_specs=[row, kv, kv], out_specs=row,
                          out_shape=SDS((S, D_MODEL), BF16), compiler_params=_cp("parallel"))(q, k, v)


def xattn_bwd(q, k, v, do):
    S = q.shape[0]
    tm = 512
    scale = MEM_HD ** -0.5

    def body(q_ref, k_ref, v_ref, do_ref, dq_ref, dk_ref, dv_ref):
        @pl.when(pl.program_id(0) == 0)
        def _():
            dk_ref[...] = jnp.zeros_like(dk_ref)
            dv_ref[...] = jnp.zeros_like(dv_ref)

        for h in range(N_MEM_HEADS):
            sl = slice(h * MEM_HD, (h + 1) * MEM_HD)
            qh, kh, vh, doh = q_ref[:, sl], k_ref[:, sl], v_ref[:, sl], do_ref[:, sl]
            s = _dot_nt(qh, kh) * scale
            p = jnp.exp(s - jnp.max(s, axis=1, keepdims=True))
            p = p / _rsum(p)
            dp = _dot_nt(doh, vh)
            ds = (p * (dp - _rsum(p * dp)) * scale).astype(BF16)
            dq_ref[:, sl] = _dot(ds, kh).astype(BF16)
            dk_ref[:, sl] += _dot_tn(ds, qh)
            dv_ref[:, sl] += _dot_tn(p.astype(BF16), doh)

    row = pl.BlockSpec((tm, D_MODEL), lambda i: (i, 0))
    kv = pl.BlockSpec(k.shape, lambda i: (0, 0))
    return pl.pallas_call(body, grid=(S // tm,), name="xattn_bwd", in_specs=[row, kv, kv, row], out_specs=[row, kv, kv],
                          out_shape=[SDS((S, D_MODEL), BF16), SDS(k.shape, F32), SDS(k.shape, F32)],
                          compiler_params=_cp("arbitrary"))(q, k, v, do)


FFN_TM, FFN_TC = 256, 1408


def _ffn_specs(S):
    tm, tc = FFN_TM, FFN_TC
    blk = pl.BlockSpec((2, tm, tc), lambda j, i: (0, i, j))
    nxt = pl.BlockSpec((2, 16, tc), lambda j, i: (0, jnp.minimum((i + 1) * (tm // 16), S // 16 - 1), j))
    wsp = pl.BlockSpec((2, 3, tc), lambda j, i: (0, 0, j))
    bsp = pl.BlockSpec((2, 1, tc), lambda j, i: (0, 0, j))
    return blk, nxt, wsp, bsp


def up_act_fwd(h, w_up, w_fc, b_fc):
    S, K = h.shape
    tm, tc = FFN_TM, FFN_TC
    nj = D_FF // tc

    def body(h_ref, wg_ref, wv_ref, w_ref, b_ref, pre_ref, gv_ref, a_ref, halo):
        i = pl.program_id(1)
        hv = h_ref[...]
        res = []
        for half, wt_ref in enumerate((wg_ref, wv_ref)):
            u = _dot(hv, wt_ref[...])
            hu = jnp.where(i > 0, halo[half], 0.0)
            halo[half] = u[tm - 8:, :]
            w0, w1, w2 = (w_ref[half, k:k + 1, :] for k in range(3))
            pre_ref[half] = u.astype(BF16)
            res.append(_conv3(u, hu, w0, w1, w2) + b_ref[half])
        g, v = res
        gv_ref[0] = g.astype(BF16)
        gv_ref[1] = v.astype(BF16)
        a_ref[...] = (g * jax.nn.sigmoid(g) * v).astype(BF16)

    blk = pl.BlockSpec((2, tm, tc), lambda j, i: (0, i, j))
    return pl.pallas_call(
        body, grid=(nj, S // tm), name="up_act_fwd",
        in_specs=[pl.BlockSpec((tm, K), lambda j, i: (i, 0)), pl.BlockSpec((K, tc), lambda j, i: (0, j)),
                  pl.BlockSpec((K, tc), lambda j, i: (0, j + nj)), pl.BlockSpec((2, 3, tc), lambda j, i: (0, 0, j)),
                  pl.BlockSpec((2, 1, tc), lambda j, i: (0, 0, j))],
        out_specs=[blk, blk, pl.BlockSpec((tm, tc), lambda j, i: (i, j))],
        out_shape=[SDS((2, S, D_FF), BF16), SDS((2, S, D_FF), BF16), SDS((S, D_FF), BF16)],
        scratch_shapes=[pltpu.VMEM((2, 8, tc), F32)],
        compiler_params=_cp("parallel", "arbitrary"))(h, w_up, w_up, w_fc, b_fc)


def ffn_act_bwd(dx, w_down, up, up_pre, w_fc):
    S = up.shape[1]

    def act_grads(da, g, v):
        sg = jax.nn.sigmoid(g)
        return da * v * (sg * (1.0 + g * (1.0 - sg))), da * g * sg

    def body(dx_ref, dxn_ref, wd_ref, gv_ref, gvn_ref, up_ref, w_ref, dp_ref, dw_ref, db_ref):
        i = pl.program_id(1)
        keep_next = jnp.where(i < pl.num_programs(1) - 1, 1.0, 0.0)

        @pl.when(i == 0)
        def _():
            dw_ref[...] = jnp.zeros_like(dw_ref)
            db_ref[...] = jnp.zeros_like(db_ref)

        wd = wd_ref[...]
        da = _dot_nt(dx_ref[...].astype(BF16), wd)
        dan = _dot_nt(dxn_ref[...].astype(BF16), wd) * keep_next
        here = act_grads(da, gv_ref[0].astype(F32), gv_ref[1].astype(F32))
        after = act_grads(dan, gvn_ref[0].astype(F32), gvn_ref[1].astype(F32))
        for half in range(2):
            d, dn = here[half], after[half]
            u = up_ref[half].astype(F32)
            d1, d2 = _shift_up(d, 1, dn), _shift_up(d, 2, dn)
            w0, w1, w2 = (w_ref[half, k:k + 1, :] for k in range(3))
            dp_ref[half] = (d * w2 + d1 * w1 + d2 * w0).astype(BF16)
            dw_ref[half, 0:1, :] += _colsum(d2 * u)
            dw_ref[half, 1:2, :] += _colsum(d1 * u)
            dw_ref[half, 2:3, :] += _colsum(d * u)
            db_ref[half] += _colsum(d)

    tm, tc = FFN_TM, FFN_TC
    blk, nxt, wsp, bsp = _ffn_specs(S)
    K = dx.shape[1]
    rows = pl.BlockSpec((tm, K), lambda j, i: (i, 0))
    rows_next = pl.BlockSpec((16, K), lambda j, i: (jnp.minimum((i + 1) * (tm // 16), S // 16 - 1), 0))
    return pl.pallas_call(body, grid=(D_FF // tc, S // tm), name="ffn_act_bwd",
                          in_specs=[rows, rows_next, pl.BlockSpec((tc, K), lambda j, i: (j, 0)), blk, nxt, blk, wsp],
                          out_specs=[blk, wsp, bsp],
                          out_shape=[SDS((2, S, D_FF), BF16), SDS((2, 3, D_FF), F32), SDS((2, 1, D_FF), F32)],
                          compiler_params=_cp("parallel", "arbitrary"))(dx, dx, w_down, up, up, up_pre, w_fc)


def local_step(x, mem, target, wb, ws, comm=None):
    S = x.shape[0]
    assert S % 2048 == 0
    if comm is None:
        comm = NoComm()
    else:
        wb = {}
    ws = dict(ws)

    bias = bias_tables(ws["rel_bias"])
    h1 = rmsnorm_fwd(x, ws["g_mix"], "norm_mix", job=comm.job("norm_mix"))
    comm.landed("norm_mix", wb, ws)
    w_fc = ws["w_ffn_conv"].reshape(3, 2, D_FF).transpose(1, 0, 2)
    b_fc = ws["b_ffn_conv"].reshape(2, 1, D_FF)
    proj = mm_nn(h1, wb["w_in"], "proj", out_dtype=BF16, out_bw=BLK, tn=768, job=comm.job("proj"))
    comm.landed("proj", wb)
    qkv = proj.reshape(2, 3, 4, S, BLK)
    o, lse = attn_fwd(qkv, bias, job=comm.job("attn_fwd"))
    comm.landed("attn_fwd", wb)
    proj4 = proj.reshape(2, 12, S, BLK)
    mixed = mixer_post_fwd(o, proj4, ws["w_short_conv"], ws["g_attn_out"], ws["g_conv_out"])
    x1, h2 = mm_nn_rows(mixed, wb["w_out"], x, ws["g_xattn"], "out_proj")
    mem_n = rmsnorm_fwd(mem, ws["g_mem"], "norm_mem")
    xq = mm_nn(h2, wb["w_xq"], "xq", out_dtype=BF16)
    xk = mm_nn(mem_n, wb["w_xk"], "xk", out_dtype=BF16, tn=1024)
    xv = mm_nn(mem_n, wb["w_xv"], "xv", out_dtype=BF16, tn=1024)
    xo = xattn_fwd(xq, xk, xv)
    x2, h3 = mm_nn_rows(xo, wb["w_xo"], x1, ws["g_ffn"], "xo_proj")
    up_pre, up, act = up_act_fwd(h3, wb["w_up"], w_fc, b_fc)
    loss, dx3, dg_final = mm_nn_rows(act, wb["w_down"], x2, ws["g_final"].reshape(1, -1), "down_proj", target=target)

    gb, gs = {}, {"g_final": dg_final}
    gb["w_down"] = mm_tn(act, dx3, None, "dw_down", tm=1408, tn=1024)
    d_up_pre, dw_fc, db_fc = ffn_act_bwd(dx3, wb["w_down"], up, up_pre, w_fc)
    gs["b_ffn_conv"] = db_fc.reshape(1, 2 * D_FF)
    gs["w_ffn_conv"] = dw_fc.transpose(1, 0, 2).reshape(3, 2 * D_FF)
    gb["w_up"] = mm_tn(h3, d_up_pre, D_FF, "dw_up", shards=4, tn=1408)
    dx2, gs["g_ffn"] = mm_nt_norm_bwd(d_up_pre, D_FF, wb["w_up"], x2, ws["g_ffn"], dx3, "d_h3", tm=256,
                                      job=comm.job("d_h3", gb))
    comm.landed("d_h3")
    gb["w_xo"] = mm_tn(xo, dx2, None, "dw_xo")
    dxo = mm_nt(dx2, None, wb["w_xo"], "d_xo", out_dtype=BF16)
    dxq, dxk, dxv = xattn_bwd(xq, xk, xv, dxo)
    gb["w_xq"] = mm_tn(h2, dxq, None, "dw_xq")
    gb["w_xk"] = mm_tn(mem_n, dxk, None, "dw_xk", tn=1024)
    gb["w_xv"] = mm_tn(mem_n, dxv, None, "dw_xv", tn=1024)
    dmem_n = mm_nt(dxk, None, wb["w_xk"], "d_memk", tn=1024)
    dmem_n = mm_nt(dxv, None, wb["w_xv"], "d_memv", res=dmem_n, tn=1024)
    gs["g_mem"] = rmsnorm_gain_grad(mem, dmem_n, "norm_mem_bwd")
    dx1, gs["g_xattn"] = mm_nt_norm_bwd(dxq, None, wb["w_xq"], x1, ws["g_xattn"], dx2, "d_h2")
    gb["w_out"] = mm_tn(mixed, dx1, None, "dw_out")
    dmixed = mm_nt(dx1, None, wb["w_out"], "d_mixed", out_dtype=BF16)
    do, dgb, dcv, gs["g_attn_out"], gs["g_conv_out"] = mixer_post_bwd_a(
        dmixed, o, proj4, ws["w_short_conv"], ws["g_attn_out"], ws["g_conv_out"], job=comm.job("mixer_post_bwd_a", gb))
    comm.landed("mixer_post_bwd_a")
    dproj, ds_sum = attn_bwd(qkv, bias, o, lse, do, job=comm.job("attn_bwd"))
    comm.landed("attn_bwd")
    gs["rel_bias"] = bias_tables_bwd(ds_sum)
    dproj, gs["w_short_conv"] = mixer_post_bwd_b(dproj.reshape(2, 12, S, BLK), dgb, dcv, proj4, ws["w_short_conv"])
    dproj = dproj.reshape(24, S, BLK)
    gb["w_in"] = mm_tn(h1, dproj, BLK, "dw_in", shards=4, tn=768, job=comm.job("dw_in"))
    comm.landed("dw_in")
    comm.first_update(gb)
    grad_x, gs["g_mix"] = mm_nt_norm_bwd(dproj, BLK, wb["w_in"], x, ws["g_mix"], dx1, "d_h1", tm=256,
                                         job=comm.job("d_h1", gb))
    comm.landed("d_h1")
    return loss, grad_x, comm.finish(gb), gs


def _adam_update(w_ref, g_ref, m_ref, v_ref, d_ref, nm_ref, nv_ref):
    gv = g_ref[...]
    mn = ADAM_B1 * m_ref[...] + (1.0 - ADAM_B1) * gv
    vn = ADAM_B2 * v_ref[...] + (1.0 - ADAM_B2) * (gv * gv)
    m_hat = mn / (1.0 - ADAM_B1 ** ADAM_STEP)
    v_hat = vn / (1.0 - ADAM_B2 ** ADAM_STEP)
    d_ref[...] = -ADAM_LR * (m_hat / (jnp.sqrt(v_hat) + ADAM_EPS) + ADAM_WD * w_ref[...])
    nm_ref[...] = mn
    nv_ref[...] = vn


def adamw(ws, gs, ms, vs, name, job=None):
    n = len(ws)
    R, C = ws[0].shape
    tr = R // 4

    def body(*refs):
        for p in range(n):
            _adam_update(*(refs[q * n + p] for q in range(7)))

    blk = pl.BlockSpec((tr, C), lambda i: (i, 0))
    outs = _pcall(body, [*ws, *gs, *ms, *vs], grid=(4,), name=name, in_specs=[blk] * (4 * n), out_specs=[blk] * (3 * n),
                  out_shape=[SDS((R, C), F32)] * (3 * n), sem=("parallel",), job=job)
    return outs[:n], outs[n:2 * n], outs[2 * n:]


BIG = ("w_in", "w_out", "w_xq", "w_xk", "w_xv", "w_xo", "w_up", "w_down")
COL_SHARDED = ("w_in", "w_up")
ANY = pl.BlockSpec(memory_space=pl.ANY)


def _place():
    x, y, c = lax.axis_index("x"), lax.axis_index("y"), lax.axis_index("c")
    chips = [(1 - x, y), (x, 1 - y), (1 - x, 1 - y)]
    return x, y, c, chips


def _window(full, name, R, C, shard, half):
    r0, nr = (0, R) if half is None else (half * (R // 2), R // 2)
    if name in COL_SHARDED:
        return full.at[pl.ds(r0, nr), pl.ds(shard * C, C)]
    return full.at[pl.ds(shard * R + r0, nr), :]


def same_shape_groups(names, arrays):
    groups = {}
    for n in names:
        groups.setdefault(tuple(arrays[n].shape), []).append(n)
    return list(groups.values())


def place_shards(ws, mine_arr, col, name):
    n = len(ws)
    R, C = ws[0].shape

    def body(s_ref, *refs):
        for p in range(n):
            refs[n + p][...] = refs[p][...].astype(BF16)

    grid_spec = pltpu.PrefetchScalarGridSpec(
        num_scalar_prefetch=1, grid=(1,), in_specs=[pl.BlockSpec((R, C), lambda i, s: (0, 0))] * n,
        out_specs=[pl.BlockSpec((R, C), (lambda i, s: (0, s[0])) if col else (lambda i, s: (s[0], 0)))] * n)
    return pl.pallas_call(body, grid_spec=grid_spec, name=name,
                          out_shape=[SDS((R, 4 * C) if col else (4 * R, C), BF16)] * n,
                          compiler_params=_cp("arbitrary"))(mine_arr, *ws)


def _gather_jobs(names, shapes):
    nw = len(names)

    def start(full, sems):
        send, recv, fsend, frecv = sems
        x, y, c, chips = _place()
        mine = 2 * x + y
        for w, n in enumerate(names):
            R, C = shapes[w]
            own = _window(full[w], n, R, C, mine, c)
            for j, chip in enumerate(chips):
                pltpu.make_async_remote_copy(src_ref=own, dst_ref=own, send_sem=send.at[w, j], recv_sem=recv.at[w, j],
                                             device_id=(*chip, c), device_id_type=MESH).start()

    def mid(full, sems, lo=0, hi=nw):
        send, recv, fsend, frecv = sems
        x, y, c, chips = _place()
        sib = (x, y, 1 - c)
        for w, n in list(enumerate(names))[lo:hi]:
            R, C = shapes[w]
            for j, chip in enumerate(chips):
                landed = _window(full[w], n, R, C, 2 * chip[0] + chip[1], c)
                pltpu.make_async_remote_copy(src_ref=landed, dst_ref=landed, send_sem=send.at[w, j], recv_sem=recv.at[w, j],
                                             device_id=(*chip, c), device_id_type=MESH).wait_recv()
                pltpu.make_async_remote_copy(src_ref=landed, dst_ref=landed, send_sem=fsend.at[w, j],
                                             recv_sem=frecv.at[w, j], device_id=sib, device_id_type=MESH).start()

    def finish(full, sems):
        send, recv, fsend, frecv = sems
        x, y, c, chips = _place()
        mine = 2 * x + y
        sib = (x, y, 1 - c)
        for w, n in enumerate(names):
            R, C = shapes[w]
            own = _window(full[w], n, R, C, mine, c)
            for j, chip in enumerate(chips):
                landed = _window(full[w], n, R, C, 2 * chip[0] + chip[1], c)
                other = _window(full[w], n, R, C, 2 * chip[0] + chip[1], 1 - c)
                pltpu.make_async_remote_copy(src_ref=other, dst_ref=other, send_sem=fsend.at[w, j], recv_sem=frecv.at[w, j],
                                             device_id=sib, device_id_type=MESH).wait_recv()
                pltpu.make_async_remote_copy(src_ref=own, dst_ref=own, send_sem=send.at[w, j], recv_sem=recv.at[w, j],
                                             device_id=(*chip, c), device_id_type=MESH).wait_send()
                pltpu.make_async_remote_copy(src_ref=landed, dst_ref=landed, send_sem=fsend.at[w, j],
                                             recv_sem=frecv.at[w, j], device_id=sib, device_id_type=MESH).wait_send()

    return start, mid, finish, [pltpu.SemaphoreType.DMA((nw, 3))] * 4


class CommJob:
    def __init__(self, ins, out_shapes, inplace, start, finish, sems, mid=None):
        self.ins, self.out_shapes, self.inplace = list(ins), list(out_shapes), inplace
        self.start, self.mid, self.finish, self.sems = start, mid, finish, list(sems)

    def run(self, phase, ins, outs, sems):
        if phase == 0:
            self.start(ins, outs, sems)
        elif phase == 1:
            if self.mid is not None:
                self.mid(ins, outs, sems)
        else:
            self.finish(ins, outs, sems)


class Carried:
    def __init__(self, job, n_in, n_out, n_scratch, steps):
        self.job, self.n_in, self.n_out, self.n_scratch, self.steps = job, n_in, n_out, n_scratch, steps
        self.nji = len(job.ins) if job else 0
        self.njo = len(job.out_shapes) if job else 0

    def in_specs(self):
        return [ANY] * self.nji

    def out_specs(self):
        return [ANY] * self.njo

    def out_shapes(self):
        return list(self.job.out_shapes) if self.job else []

    def aliases(self):
        if not (self.job and self.job.inplace):
            return {}
        return {self.n_in + i: self.n_out + i for i in range(self.nji)}

    def sems(self):
        return list(self.job.sems) if self.job else []

    def args(self):
        return list(self.job.ins) if self.job else []

    def split(self, refs):
        a = self.n_in
        b = a + self.nji
        c = b + self.n_out
        d = c + self.njo
        e = d + self.n_scratch
        return refs[:a] + refs[b:c] + refs[d:e], (refs[a:b], refs[c:d], refs[e:])

    def phase(self, phase, step, jrefs):
        if self.job is None:
            return
        at = {0: 0, 1: max(self.steps - 2, 0), 2: self.steps - 1}[phase]

        @pl.when(step == at)
        def _():
            self.job.run(phase, *jrefs)


def _pcall(body, args, *, grid, in_specs, out_specs, out_shape, name, sem, scratch=(), aliases=None, job=None):
    n_in, n_out = len(args), len(out_shape)
    if job is None:
        return pl.pallas_call(
            body, grid=grid, in_specs=list(in_specs), out_specs=list(out_specs), out_shape=list(out_shape), name=name,
            scratch_shapes=list(scratch), input_output_aliases=dict(aliases or {}), compiler_params=_cp(*sem))(*args)
    total = int(np.prod(grid))
    car = Carried(job, n_in, n_out, len(scratch), total)

    def wrapped(*refs):
        main, jrefs = car.split(refs)
        lin = pl.program_id(0)
        for ax in range(1, len(grid)):
            lin = lin * grid[ax] + pl.program_id(ax)
        car.phase(0, lin, jrefs)
        body(*main)
        car.phase(1, lin, jrefs)
        car.phase(2, lin, jrefs)

    res = pl.pallas_call(
        wrapped, grid=grid, in_specs=list(in_specs) + car.in_specs(), out_specs=list(out_specs) + car.out_specs(),
        out_shape=list(out_shape) + car.out_shapes(), name=name, scratch_shapes=list(scratch) + car.sems(),
        input_output_aliases={**dict(aliases or {}), **car.aliases()},
        compiler_params=_cp(*(["arbitrary"] * len(grid))))(*args, *car.args())
    job.landed = list(res[n_out:])
    return list(res[:n_out])


def gather_job(placed, names):
    shapes = []
    for n in names:
        R, C = placed[n].shape
        shapes.append((R, C // 4) if n in COL_SHARDED else (R // 4, C))
    start, mid, finish, sems = _gather_jobs(names, shapes)
    arrays = [placed[n] for n in names]
    early = max(len(names) - 1, 1)

    def last(i, o, s):
        mid(o, s, early, len(names))
        finish(o, s)

    return CommJob(arrays, [SDS(a.shape, a.dtype) for a in arrays], True,
                   lambda i, o, s: start(o, s), last, sems, mid=lambda i, o, s: mid(o, s, 0, early))


def gather_rows_job(placed):
    widths = [a.shape[1] // 4 for a in placed]

    def copies(outs, sems):
        x, y, c, chips = _place()
        mine = 2 * x + y
        cps = []
        for w, ref in enumerate(outs):
            own = ref.at[:, pl.ds(mine * widths[w], widths[w])]
            for j, chip in enumerate(chips):
                theirs = ref.at[:, pl.ds((2 * chip[0] + chip[1]) * widths[w], widths[w])]
                kw = dict(send_sem=sems[0].at[w, j], recv_sem=sems[1].at[w, j], device_id=(*chip, c), device_id_type=MESH)
                cps.append((pltpu.make_async_remote_copy(src_ref=own, dst_ref=own, **kw),
                            pltpu.make_async_remote_copy(src_ref=theirs, dst_ref=theirs, **kw)))
        return cps

    def start(ins, outs, sems):
        for send, _ in copies(outs, sems):
            send.start()

    def finish(ins, outs, sems):
        for send, recv in copies(outs, sems):
            recv.wait_recv()
            send.wait_send()

    return CommJob(placed, [SDS(a.shape, a.dtype) for a in placed], True, start, finish,
                   [pltpu.SemaphoreType.DMA((len(placed), 3))] * 2)


def merge_jobs(a, b):
    assert a.inplace and b.inplace
    na, nb_, nsa = len(a.ins), len(b.ins), len(a.sems)

    def phase(which):
        def run(ins, outs, sems):
            for job, i, o, s in ((a, ins[:na], outs[:na], sems[:nsa]), (b, ins[na:], outs[na:], sems[nsa:])):
                fn = getattr(job, which)
                if fn is not None:
                    fn(i, o, s)
        return run

    merged = CommJob(a.ins + b.ins, a.out_shapes + b.out_shapes, True, phase("start"), phase("finish"), a.sems + b.sems,
                     mid=phase("mid"))
    merged.parts = (a, b, na)
    return merged


def split_landed(merged):
    a, b, na = merged.parts
    a.landed, b.landed = merged.landed[:na], merged.landed[na:]


def pair_exchange_job(grads, names):
    shapes = [grads[n].shape for n in names]

    def copies(ins, outs, sems):
        x, y, c, _ = _place()
        return [pltpu.make_async_remote_copy(
            src_ref=ins[w].at[:, pl.ds((1 - c) * (shapes[w][1] // 2), shapes[w][1] // 2), :], dst_ref=outs[w],
            send_sem=sems[0].at[w], recv_sem=sems[1].at[w], device_id=(x, y, 1 - c), device_id_type=MESH)
            for w in range(len(names))]

    def start(ins, outs, sems):
        for cp in copies(ins, outs, sems):
            cp.start()

    def finish(ins, outs, sems):
        for cp in copies(ins, outs, sems):
            cp.wait()

    return CommJob([grads[n] for n in names], [SDS((4, s[1] // 2, s[2]), F32) for s in shapes], False, start, finish,
                   [pltpu.SemaphoreType.DMA((len(names),))] * 2)


def chip_exchange_job(p16, names):
    shapes = [p16[n].shape for n in names]

    def copies(ins, outs, sems):
        x, y, c, chips = _place()
        return [pltpu.make_async_remote_copy(
            src_ref=ins[w].at[2 * chip[0] + chip[1]], dst_ref=outs[w].at[j],
            send_sem=sems[0].at[w, j], recv_sem=sems[1].at[w, j], device_id=(*chip, c), device_id_type=MESH)
            for w in range(len(names)) for j, chip in enumerate(chips)]

    def start(ins, outs, sems):
        for cp in copies(ins, outs, sems):
            cp.start()

    def finish(ins, outs, sems):
        for cp in copies(ins, outs, sems):
            cp.wait()

    return CommJob([p16[n] for n in names], [SDS((3,) + tuple(s[1:]), BF16) for s in shapes], False, start, finish,
                   [pltpu.SemaphoreType.DMA((len(names), 3))] * 2)


def half_exchange_job(full, names):
    shapes = [full[n].shape for n in names]

    def copies(outs, sems):
        x, y, c, _ = _place()
        cps = []
        for w in range(len(names)):
            Rh = shapes[w][0] // 2
            rows = outs[w].at[pl.ds(c * Rh, Rh), :]
            other = outs[w].at[pl.ds((1 - c) * Rh, Rh), :]
            cps.append((pltpu.make_async_remote_copy(src_ref=rows, dst_ref=rows, send_sem=sems[0].at[w], recv_sem=sems[1].at[w],
                                                     device_id=(x, y, 1 - c), device_id_type=MESH),
                        pltpu.make_async_remote_copy(src_ref=other, dst_ref=other, send_sem=sems[0].at[w], recv_sem=sems[1].at[w],
                                                     device_id=(x, y, 1 - c), device_id_type=MESH)))
        return cps

    def start(ins, outs, sems):
        for send, _ in copies(outs, sems):
            send.start()

    def finish(ins, outs, sems):
        for send, recv in copies(outs, sems):
            recv.wait_recv()
            send.wait_send()

    arrays = [full[n] for n in names]
    return CommJob(arrays, [SDS(a.shape, a.dtype) for a in arrays], True, start, finish,
                   [pltpu.SemaphoreType.DMA((len(names),))] * 2)


def allreduce_small(parts, name, job=None):
    n = len(parts)
    nji = len(job.ins) if job else 0
    njo = len(job.out_shapes) if job else 0

    def body(*refs):
        ins, jins = refs[:n], refs[n:n + nji]
        outs, jouts = refs[n + nji:2 * n + nji], refs[2 * n + nji:2 * n + nji + njo]
        bufs = refs[2 * n + nji + njo:3 * n + nji + njo]
        send, recv = refs[3 * n + nji + njo:3 * n + nji + njo + 2]
        jsems = refs[3 * n + nji + njo + 2:]
        x, y, c, _ = _place()
        me = 4 * x + 2 * y + c
        if job is not None:
            job.run(0, jins, jouts, jsems)
        sends = []
        for p in range(n):
            for k in range(1, 8):
                peer = (x ^ (k >> 2), y ^ ((k >> 1) & 1), c ^ (k & 1))
                cp = pltpu.make_async_remote_copy(src_ref=ins[p], dst_ref=bufs[p].at[k], send_sem=send.at[p, k - 1],
                                                  recv_sem=recv.at[p, k - 1], device_id=peer, device_id_type=MESH)
                cp.start()
                sends.append(cp)
            bufs[p][0] = ins[p][...]
        for cp in sends:
            cp.wait_recv()
        for p in range(n):
            acc = bufs[p][me]
            for d in range(1, 8):
                acc = acc + bufs[p][d ^ me]
            outs[p][...] = acc
        for cp in sends:
            cp.wait_send()
        if job is not None:
            job.run(1, jins, jouts, jsems)
            job.run(2, jins, jouts, jsems)

    vm = pl.BlockSpec(memory_space=pltpu.VMEM)
    res = pl.pallas_call(
        body, name=name, in_specs=[vm] * n + [ANY] * nji, out_specs=[vm] * n + [ANY] * njo,
        out_shape=[SDS(a.shape, F32) for a in parts] + (list(job.out_shapes) if job else []),
        input_output_aliases={n + i: n + i for i in range(nji)} if (job and job.inplace) else {},
        scratch_shapes=[pltpu.VMEM((8,) + tuple(a.shape), F32) for a in parts] + [pltpu.SemaphoreType.DMA((n, 7))] * 2
        + (list(job.sems) if job else []),
    )(*parts, *(job.ins if job else []))
    if job is not None:
        job.landed = list(res[n:])
    return list(res[:n])


def adamw_multi(ws, gs, ms, vs, name):
    n = len(ws)

    def body(*refs):
        for p in range(n):
            _adam_update(*(refs[q * n + p] for q in range(7)))

    vm = pl.BlockSpec(memory_space=pltpu.VMEM)
    outs = pl.pallas_call(body, name=name, in_specs=[vm] * (4 * n), out_specs=[vm] * (3 * n),
                          out_shape=[SDS(a.shape, F32) for a in ws] * 3)(*ws, *gs, *ms, *vs)
    return outs[:n], outs[n:2 * n], outs[2 * n:]


def pair_sum(gs, xhs, sel_arr, name):
    n = len(gs)
    _, R, C = gs[0].shape
    Rh = R // 2

    def body(sel_ref, *refs):
        for p in range(n):
            g_ref, x_ref, p32_ref, p16_ref = (refs[q * n + p] for q in range(4))
            s = g_ref[...] + x_ref[...]
            p16_ref[...] = s.astype(BF16)

            @pl.when(pl.program_id(0) == sel_ref[0])
            def _():
                p32_ref[...] = s

    same = pl.BlockSpec((None, Rh, C), lambda s, sel: (s, 0, 0))
    grid_spec = pltpu.PrefetchScalarGridSpec(
        num_scalar_prefetch=1, grid=(4,),
        in_specs=[pl.BlockSpec((None, Rh, C), lambda s, sel: (s, sel[1], 0))] * n + [same] * n,
        out_specs=[pl.BlockSpec((Rh, C), lambda s, sel: (0, 0))] * n + [same] * n)
    outs = pl.pallas_call(body, grid_spec=grid_spec, name=name,
                          out_shape=[SDS((Rh, C), F32)] * n + [SDS((4, Rh, C), BF16)] * n,
                          compiler_params=_cp("arbitrary"))(sel_arr, *gs, *xhs)
    return outs[:n], outs[n:]


def chip_sum(p32s, ys, sel_arr, name):
    n = len(p32s)
    Rh, C = p32s[0].shape

    def body(s_ref, *refs):
        for p in range(n):
            p_ref, y_ref, o_ref = (refs[q * n + p] for q in range(3))
            o_ref[...] = ((p_ref[...] + y_ref[0].astype(F32)) + y_ref[1].astype(F32)) + y_ref[2].astype(F32)

    grid_spec = pltpu.PrefetchScalarGridSpec(
        num_scalar_prefetch=1, grid=(1,),
        in_specs=[pl.BlockSpec((Rh, C), lambda i, s: (0, 0))] * n + [pl.BlockSpec((3, Rh, C), lambda i, s: (0, 0, 0))] * n,
        out_specs=[pl.BlockSpec((Rh, C), lambda i, s: (s[1], 0))] * n)
    return pl.pallas_call(body, grid_spec=grid_spec, name=name, out_shape=[SDS((2 * Rh, C), F32)] * n,
                          compiler_params=_cp("arbitrary"))(sel_arr, *p32s, *ys)


class StepComm:
    FIRST = ("w_in",)
    SQUARE = ("w_out", "w_xq", "w_xk", "w_xv", "w_xo")
    FFN = ("w_up", "w_down")
    PLAN = {
        "proj": ("gather", SQUARE[:3]), "attn_fwd": ("gather", SQUARE[3:] + FFN),
        "d_h3": ("pair", FFN), "mixer_post_bwd_a": ("pair", SQUARE), "attn_bwd": ("chip", FFN + SQUARE),
        "dw_in": ("half", FFN + SQUARE), "first_update": ("pair", FIRST), "d_h1": ("chip", FIRST),
    }

    def __init__(self, placed, conv_placed, sel_arr):
        self.placed, self.conv_placed, self.sel_arr = placed, conv_placed, sel_arr
        self.g, self.p32, self.p16, self.full, self.done, self.jobs = {}, {}, {}, {}, {}, {}

    def job(self, tag, gb=None):
        if tag == "norm_mix":
            self.jobs[tag] = merge_jobs(gather_job(self.placed, self.FIRST), gather_rows_job(self.conv_placed))
            return self.jobs[tag]
        kind, names = self.PLAN[tag]
        if kind == "gather":
            j = gather_job(self.placed, names)
        elif kind == "pair":
            for n in names:
                self.g[n] = gb[n] if gb[n].ndim == 3 else gb[n].reshape(4, gb[n].shape[0] // 4, gb[n].shape[1])
            j = pair_exchange_job(self.g, names)
        elif kind == "chip":
            j = chip_exchange_job(self.p16, names)
        else:
            j = half_exchange_job(self.full, names)
        self.jobs[tag] = j
        return j

    def landed(self, tag, wb=None, ws=None):
        if tag == "norm_mix":
            split_landed(self.jobs[tag])
            first, conv, _ = self.jobs[tag].parts
            wb.update(zip(self.FIRST, first.landed))
            ws.update((n, a[:3]) for n, a in zip(SMALL_SHARDED, conv.landed))
            return
        kind, names = self.PLAN[tag]
        got = dict(zip(names, self.jobs[tag].landed))
        if kind == "gather":
            wb.update(got)
        elif kind == "half":
            self.done.update(got)
        for group in same_shape_groups(names, got):
            if kind == "pair":
                p32s, p16s = pair_sum([self.g[n] for n in group], [got[n] for n in group], self.sel_arr, "pair_sum_" + group[0])
                self.p32.update(zip(group, p32s))
                self.p16.update(zip(group, p16s))
            elif kind == "chip":
                fulls = chip_sum([self.p32[n] for n in group], [got[n] for n in group], self.sel_arr, "chip_sum_" + group[0])
                self.full.update(zip(group, fulls))

    def first_update(self, gb):
        self.first_result = self.update_fn(self.job("first_update", gb))
        self.landed("first_update")

    def finish(self, gb):
        return self.done

    def last_job(self):
        self.jobs["end_half"] = half_exchange_job(self.full, self.FIRST)
        return self.jobs["end_half"]

    def take_last(self):
        self.done.update(zip(self.FIRST, self.jobs["end_half"].landed))


class NoComm:
    def job(self, tag, gb=None):
        return None

    def landed(self, tag, wb=None, ws=None):
        pass

    def first_update(self, gb):
        pass

    def finish(self, gb):
        return gb


SMALL = ("rel_bias", "g_mix", "w_short_conv", "g_attn_out", "g_conv_out", "g_xattn", "g_mem", "g_ffn",
         "w_ffn_conv", "b_ffn_conv", "g_final")
SMALL_SHARDED = ("w_short_conv", "w_ffn_conv")


def kernel(x, mem, rel_bias, g_mix, w_in, w_short_conv, g_attn_out, g_conv_out, w_out, g_xattn, g_mem, w_xq, w_xk, w_xv, w_xo, g_ffn, w_up, w_ffn_conv, b_ffn_conv, w_down, g_final, loss_target, m_rel_bias, m_g_mix, m_w_in, m_w_short_conv, m_g_attn_out, m_g_conv_out, m_w_out, m_g_xattn, m_g_mem, m_w_xq, m_w_xk, m_w_xv, m_w_xo, m_g_ffn, m_w_up, m_w_ffn_conv, m_b_ffn_conv, m_w_down, m_g_final, v_rel_bias, v_g_mix, v_w_in, v_w_short_conv, v_g_attn_out, v_g_conv_out, v_w_out, v_g_xattn, v_g_mem, v_w_xq, v_w_xk, v_w_xv, v_w_xo, v_g_ffn, v_w_up, v_w_ffn_conv, v_b_ffn_conv, v_w_down, v_g_final):
    names = ("rel_bias", "g_mix", "w_in", "w_short_conv", "g_attn_out", "g_conv_out", "w_out", "g_xattn", "g_mem", "w_xq",
             "w_xk", "w_xv", "w_xo", "g_ffn", "w_up", "w_ffn_conv", "b_ffn_conv", "w_down", "g_final")
    W = dict(zip(names, (rel_bias, g_mix, w_in, w_short_conv, g_attn_out, g_conv_out, w_out, g_xattn, g_mem, w_xq, w_xk, w_xv,
                         w_xo, g_ffn, w_up, w_ffn_conv, b_ffn_conv, w_down, g_final)))
    M = dict(zip(names, (m_rel_bias, m_g_mix, m_w_in, m_w_short_conv, m_g_attn_out, m_g_conv_out, m_w_out, m_g_xattn, m_g_mem,
                         m_w_xq, m_w_xk, m_w_xv, m_w_xo, m_g_ffn, m_w_up, m_w_ffn_conv, m_b_ffn_conv, m_w_down, m_g_final)))
    V = dict(zip(names, (v_rel_bias, v_g_mix, v_w_in, v_w_short_conv, v_g_attn_out, v_g_conv_out, v_w_out, v_g_xattn, v_g_mem,
                         v_w_xq, v_w_xk, v_w_xv, v_w_xo, v_g_ffn, v_w_up, v_w_ffn_conv, v_b_ffn_conv, v_w_down, v_g_final)))
    xi, yi, ci = lax.axis_index("x"), lax.axis_index("y"), lax.axis_index("c")
    mine = 2 * xi + yi
    mine_arr = jnp.reshape(mine, (1,)).astype(jnp.int32)
    sel_arr = jnp.stack([mine, ci]).astype(jnp.int32)
    conv_placed = []
    for n in SMALL_SHARDED:
        shard = W[n][0]
        conv_placed.append(lax.dynamic_update_slice(jnp.zeros((8, 4 * shard.shape[1]), F32), shard, (0, mine * shard.shape[1])))
    shards = {n: W[n][0] for n in BIG}
    placed = {}
    for group in same_shape_groups(BIG, shards):
        outs = place_shards([shards[n] for n in group], mine_arr, group[0] in COL_SHARDED, "place_" + group[0])
        placed.update(zip(group, outs))
    comm = StepComm(placed, conv_placed, sel_arr)
    ws = {n: W[n] for n in SMALL if n not in SMALL_SHARDED}

    def update(group, job=None):
        return adamw([W[n][0] for n in group], [comm.done[n] for n in group], [M[n][0] for n in group],
                     [V[n][0] for n in group], "adamw_" + group[0], job=job)

    comm.update_fn = lambda job: update(["w_up"], job)
    loss, grad_x, gfull, gs = local_step(x[0], mem[0], loss_target[0], None, ws, comm)

    def as2d(a):
        return a.reshape(1, -1) if a.ndim == 1 else a

    reduced = allreduce_small([as2d(gs[n]) for n in SMALL] + [loss], "reduce_small", job=comm.last_job())
    comm.take_last()
    gsm = dict(zip(SMALL, reduced[:-1]))
    loss = reduced[-1][0, 0]

    grads, delta, new_m, new_v = {}, {}, {}, {}
    for group in same_shape_groups(BIG, gfull):
        ds, nms, nvs = comm.first_result if group == ["w_up"] else update(group)
        for n, d, nm, nv in zip(group, ds, nms, nvs):
            grads[n], delta[n], new_m[n], new_v[n] = gfull[n][None], d[None], nm[None], nv[None]
    for n in SMALL_SHARDED:
        wid = W[n].shape[2]
        gsm[n] = lax.dynamic_slice(gsm[n], (0, mine * wid), (3, wid))

    def own(a, n):
        return a[0] if n in SMALL_SHARDED else as2d(a)

    d, nm, nv = adamw_multi([own(W[n], n) for n in SMALL], [gsm[n] for n in SMALL], [own(M[n], n) for n in SMALL],
                            [own(V[n], n) for n in SMALL], "adamw_small")
    for i, n in enumerate(SMALL):
        shape = W[n].shape
        grads[n], delta[n], new_m[n], new_v[n] = (a.reshape(shape) for a in (gsm[n], d[i], nm[i], nv[i]))
    return (loss, grad_x[None], *[grads[n] for n in names], *[delta[n] for n in names],
            *[new_m[n] for n in names], *[new_v[n] for n in names])
```

```python
import math

import numpy as np
import jax
import jax.numpy as jnp
from jax import lax
from jax.experimental import pallas as pl
from jax.experimental.pallas import tpu as pltpu

F32 = jnp.float32
BF16 = jnp.bfloat16
SDS = jax.ShapeDtypeStruct
MESH = pl.DeviceIdType.MESH

D_MODEL = 1024
ATTN_W = 512
N_HEADS = 8
HEAD_DIM = 64
HALF_HEAD = HEAD_DIM // 2
Q_SCALE = HEAD_DIM ** -0.5
BLK = 128
PATTERNS = ((128, 1), (512, 4), (2048, 16))
N_BUCKETS = 32
BUCKET_MAX_DIST = 2048
D_FF = 2816
N_MEM_HEADS = 4
MEM_HD = 256
EPS = 1e-6
NEG = -1e30
VMEM_LIMIT = 56 * 1024 * 1024

ADAM_LR, ADAM_B1, ADAM_B2, ADAM_EPS, ADAM_WD, ADAM_STEP = 0.001, 0.9, 0.999, 1e-08, 0.01, 10


def _cp(*sem):
    return pltpu.CompilerParams(dimension_semantics=sem, vmem_limit_bytes=VMEM_LIMIT)


def _dot(a, b):
    return jnp.dot(a, b, preferred_element_type=F32)


def _dot_nt(a, b):
    return lax.dot_general(a, b, (((1,), (1,)), ((), ())), preferred_element_type=F32)


def _dot_tn(a, b):
    return lax.dot_general(a, b, (((0,), (0,)), ((), ())), preferred_element_type=F32)


def _rsum(x):
    return jnp.sum(x, axis=1, keepdims=True)


def rmsnorm_fwd(x, g, name, job=None):
    S, Dm = x.shape
    tm = min(S, 512)

    def body(x_ref, g_ref, o_ref):
        xv = x_ref[...]
        r = lax.rsqrt(jnp.mean(xv * xv, axis=1, keepdims=True) + EPS)
        o_ref[...] = (xv * r * g_ref[...]).astype(o_ref.dtype)

    return _pcall(body, [x, g], grid=(S // tm,), name=name,
                  in_specs=[pl.BlockSpec((tm, Dm), lambda i: (i, 0)), pl.BlockSpec((1, Dm), lambda i: (0, 0))],
                  out_specs=[pl.BlockSpec((tm, Dm), lambda i: (i, 0))], out_shape=[SDS((S, Dm), BF16)],
                  sem=("parallel",), job=job)[0]


def rmsnorm_gain_grad(x, dh, name):
    S, Dm = x.shape
    tm = min(S, 512)

    def body(x_ref, dh_ref, dg_ref):
        xv = x_ref[...]
        xh = xv * lax.rsqrt(jnp.mean(xv * xv, axis=1, keepdims=True) + EPS)

        @pl.when(pl.program_id(0) == 0)
        def _():
            dg_ref[...] = jnp.zeros_like(dg_ref)

        dg_ref[...] += jnp.sum(dh_ref[...].astype(F32) * xh, axis=0, keepdims=True)

    row = pl.BlockSpec((tm, Dm), lambda i: (i, 0))
    return pl.pallas_call(
        body, grid=(S // tm,), name=name, in_specs=[row, row], out_specs=pl.BlockSpec((1, Dm), lambda i: (0, 0)),
        out_shape=SDS((1, Dm), F32), compiler_params=_cp("arbitrary"))(x, dh)


def _col_spec(bw, tn, rows, row_of, col_of):
    if bw is None:
        return pl.BlockSpec((rows, tn), lambda *g: (row_of(*g), col_of(*g)))
    if tn % bw == 0:
        return pl.BlockSpec((tn // bw, rows, bw), lambda *g: (col_of(*g), row_of(*g), 0))
    per = bw // tn
    return pl.BlockSpec((None, rows, tn), lambda *g: (col_of(*g) // per, row_of(*g), col_of(*g) % per))


def _read_cols(ref, bw, tn):
    if bw is None or tn % bw != 0:
        return ref[...]
    if tn == bw:
        return ref[0]
    return jnp.concatenate([ref[q] for q in range(tn // bw)], axis=1)


def _write_cols(ref, bw, tn, val):
    if bw is None or tn % bw != 0:
        ref[...] = val.astype(ref.dtype)
    else:
        for q in range(tn // bw):
            ref[q] = val[:, q * bw:(q + 1) * bw].astype(ref.dtype)


def mm_nn(a, b, name, *, out_dtype=F32, out_bw=None, tm=1024, tn=512, job=None):
    M, K = a.shape
    N = b.shape[1]
    tm = min(tm, M)

    def body(a_ref, b_ref, o_ref):
        _write_cols(o_ref, out_bw, tn, _dot(a_ref[...].astype(BF16), b_ref[...]))

    ri, ci = (lambda i, j: i), (lambda i, j: j)
    in_specs = [pl.BlockSpec((tm, K), lambda i, j: (i, 0)), pl.BlockSpec((K, tn), lambda i, j: (0, j))]
    oshape = (M, N) if out_bw is None else (N // out_bw, M, out_bw)
    return _pcall(body, [a, b], grid=(M // tm, N // tn), name=name, in_specs=in_specs,
                  out_specs=[_col_spec(out_bw, tn, tm, ri, ci)], out_shape=[SDS(oshape, out_dtype)],
                  sem=("parallel", "parallel"), job=job)[0]


def mm_nn_rows(a, b, res, g, name, *, target=None, tm=512):
    M, K = a.shape
    N = b.shape[1]

    def body(*refs):
        a_ref, b_ref, r_ref, g_ref = refs[:4]
        xv = r_ref[...] + _dot(a_ref[...].astype(BF16), b_ref[...])
        r = lax.rsqrt(jnp.mean(xv * xv, axis=1, keepdims=True) + EPS)
        xh = xv * r
        if target is None:
            x_ref, h_ref = refs[4:]
            x_ref[...] = xv
            h_ref[...] = (xh * g_ref[...]).astype(BF16)
            return
        t_ref, loss_ref, dx_ref, dg_ref = refs[4:]
        err = xh * g_ref[...] - t_ref[...]
        dy = err / N
        gd = dy * g_ref[...]
        dx_ref[...] = r * (gd - xh * jnp.mean(gd * xh, axis=1, keepdims=True))

        @pl.when(pl.program_id(0) == 0)
        def _():
            dg_ref[...] = jnp.zeros_like(dg_ref)
            loss_ref[...] = jnp.zeros_like(loss_ref)

        dg_ref[...] += _colsum(dy * xh)
        loss_ref[...] += 0.5 * _colsum(jnp.mean(err * err, axis=1, keepdims=True))

    row = pl.BlockSpec((tm, N), lambda i: (i, 0))
    vec = pl.BlockSpec((1, N), lambda i: (0, 0))
    in_specs = [pl.BlockSpec((tm, K), lambda i: (i, 0)), pl.BlockSpec((K, N), lambda i: (0, 0)), row, vec]
    if target is None:
        return pl.pallas_call(body, grid=(M // tm,), name=name, in_specs=in_specs, out_specs=[row, row],
                              out_shape=[SDS((M, N), F32), SDS((M, N), BF16)], compiler_params=_cp("parallel"))(a, b, res, g)
    one = pl.BlockSpec((1, 1), lambda i: (0, 0))
    return pl.pallas_call(body, grid=(M // tm,), name=name, in_specs=in_specs + [row], out_specs=[one, row, vec],
                          out_shape=[SDS((1, 1), F32), SDS((M, N), F32), SDS((1, N), F32)],
                          compiler_params=_cp("arbitrary"))(a, b, res, g, target)


def mm_nt(a, a_bw, b, name, *, res=None, out_dtype=F32, tm=1024, tn=512, job=None):
    if a_bw is None:
        M, K = a.shape
    else:
        M, K = a.shape[1], a.shape[0] * a_bw
    N = b.shape[0]
    tm = min(tm, M)

    def body(*refs):
        if res is None:
            a_ref, b_ref, o_ref = refs
        else:
            a_ref, b_ref, r_ref, o_ref = refs
        av = _read_cols(a_ref, a_bw, K).astype(BF16)
        acc = _dot_nt(av, b_ref[...])
        if res is not None:
            acc = acc + r_ref[...]
        o_ref[...] = acc.astype(o_ref.dtype)

    in_specs = [_col_spec(a_bw, K, tm, lambda i, j: i, lambda i, j: 0), pl.BlockSpec((tn, K), lambda i, j: (j, 0))]
    args = [a, b]
    if res is not None:
        in_specs.append(pl.BlockSpec((tm, tn), lambda i, j: (i, j)))
        args.append(res)
    return _pcall(body, args, grid=(M // tm, N // tn), name=name, in_specs=in_specs,
                  out_specs=[pl.BlockSpec((tm, tn), lambda i, j: (i, j))], out_shape=[SDS((M, N), out_dtype)],
                  sem=("parallel", "parallel"), job=job)[0]


def mm_nt_norm_bwd(a, a_bw, b, x, g, res, name, *, tm=512, job=None):
    if a_bw is None:
        M, K = a.shape
    else:
        M, K = a.shape[1], a.shape[0] * a_bw
    N = b.shape[0]

    def body(a_ref, b_ref, x_ref, g_ref, r_ref, dx_ref, dg_ref):
        dh = _dot_nt(_read_cols(a_ref, a_bw, K).astype(BF16), b_ref[...])
        xv = x_ref[...]
        r = lax.rsqrt(jnp.mean(xv * xv, axis=1, keepdims=True) + EPS)
        xh = xv * r
        gd = dh * g_ref[...]
        dx_ref[...] = r_ref[...] + r * (gd - xh * jnp.mean(gd * xh, axis=1, keepdims=True))

        @pl.when(pl.program_id(0) == 0)
        def _():
            dg_ref[...] = jnp.zeros_like(dg_ref)

        dg_ref[...] += jnp.sum(dh * xh, axis=0, keepdims=True)

    row = pl.BlockSpec((tm, N), lambda i: (i, 0))
    vec = pl.BlockSpec((1, N), lambda i: (0, 0))
    return _pcall(body, [a, b, x, g, res], grid=(M // tm,), name=name,
                  in_specs=[_col_spec(a_bw, K, tm, lambda i: i, lambda i: 0), pl.BlockSpec((N, K), lambda i: (0, 0)), row, vec, row],
                  out_specs=[row, vec], out_shape=[SDS((M, N), F32), SDS((1, N), F32)], sem=("arbitrary",), job=job)


def mm_tn(a, b, b_bw, name, *, shards=None, tm=1024, tn=1024, ts=1024, job=None):
    S, Ka = a.shape
    N = b.shape[1] if b_bw is None else b.shape[0] * b_bw
    ts = min(ts, S)
    tm = min(tm, Ka)

    def body(a_ref, b_ref, o_ref):
        @pl.when(pl.program_id(2) == 0)
        def _():
            o_ref[...] = jnp.zeros_like(o_ref)

        bv = _read_cols(b_ref, b_bw, tn).astype(BF16)
        o_ref[...] += _dot_tn(a_ref[...].astype(BF16), bv)

    in_specs = [pl.BlockSpec((ts, tm), lambda i, j, k: (k, i)),
                _col_spec(b_bw, tn, ts, lambda i, j, k: k, lambda i, j, k: j)]
    if shards is None:
        out_spec, oshape = pl.BlockSpec((tm, tn), lambda i, j, k: (i, j)), (Ka, N)
    else:
        per = (N // shards) // tn
        out_spec = pl.BlockSpec((None, tm, tn), lambda i, j, k: (j // per, i, j % per))
        oshape = (shards, Ka, N // shards)
    return _pcall(body, [a, b], grid=(Ka // tm, N // tn, S // ts), name=name, in_specs=in_specs, out_specs=[out_spec],
                  out_shape=[SDS(oshape, F32)], sem=("parallel", "parallel", "arbitrary"), job=job)[0]


KEYS = 2 * BLK


def _bucket_tables():
    out = np.zeros((3, BLK, KEYS), np.int32)
    qi = np.arange(BLK)[:, None]
    kj = np.arange(KEYS)[None, :]
    for p, (win, dil) in enumerate(PATTERNS):
        w = win // dil
        assert w == BLK
        steps = qi + w - kj
        valid = (steps >= 0) & (steps <= w)
        dist = np.clip(steps, 0, w) * dil
        dd = np.maximum(dist, 1).astype(np.float32)
        exact = N_BUCKETS // 2
        large = exact + (np.log(dd / np.float32(exact)) / np.float32(math.log(BUCKET_MAX_DIST / exact))
                         * np.float32(N_BUCKETS - exact)).astype(np.int32)
        large = np.minimum(large, N_BUCKETS - 1)
        out[p] = np.where(valid, np.where(dist < exact, dist, large), -1)
    return out


def bias_tables(rel_bias):
    bkt = jnp.asarray(_bucket_tables())

    def body(rb_ref, bkt_ref, o_ref):
        bk = bkt_ref[...]
        for h in range(N_HEADS):
            acc = jnp.full((BLK, KEYS), NEG, F32)
            for b in range(N_BUCKETS):
                acc = jnp.where(bk == b, rb_ref[h, b], acc)
            o_ref[h] = acc

    return pl.pallas_call(
        body, grid=(3,), name="bias_tables",
        in_specs=[pl.BlockSpec(memory_space=pltpu.SMEM),
                  pl.BlockSpec((None, BLK, KEYS), lambda p: (p, 0, 0))],
        out_specs=pl.BlockSpec((None, N_HEADS, BLK, KEYS), lambda p: (p, 0, 0, 0)),
        out_shape=SDS((3, N_HEADS, BLK, KEYS), F32), compiler_params=_cp("parallel"))(rel_bias, bkt)


def bias_tables_bwd(ds_sum):
    bkt = jnp.asarray(_bucket_tables())

    def body(ds_ref, bkt_ref, o_ref):
        h = pl.program_id(0)

        @pl.when(h == 0)
        def _():
            o_ref[...] = jnp.zeros_like(o_ref)

        rows = lax.broadcasted_iota(jnp.int32, (N_HEADS, N_BUCKETS), 0)
        cols = lax.broadcasted_iota(jnp.int32, (N_HEADS, N_BUCKETS), 1)
        acc = jnp.zeros((N_HEADS, N_BUCKETS), F32)
        for b in range(N_BUCKETS):
            t = jnp.zeros((BLK, KEYS), F32)
            for p in range(3):
                t = t + jnp.where(bkt_ref[p] == b, ds_ref[p], 0.0)
            tot = jnp.sum(_rsum(t), axis=0, keepdims=True)
            acc = acc + jnp.where((rows == h) & (cols == b), tot, 0.0)
        o_ref[...] += acc

    return pl.pallas_call(
        body, grid=(N_HEADS,), name="bias_tables_bwd",
        in_specs=[pl.BlockSpec((3, None, BLK, KEYS), lambda h: (0, h, 0, 0)),
                  pl.BlockSpec((3, BLK, KEYS), lambda h: (0, 0, 0))],
        out_specs=pl.BlockSpec((N_HEADS, N_BUCKETS), lambda h: (0, 0)),
        out_shape=SDS((N_HEADS, N_BUCKETS), F32), compiler_params=_cp("arbitrary"))(ds_sum, bkt)


def _rows(start, dil):
    if dil == 1:
        return pl.ds(pl.multiple_of(start, BLK), BLK)
    return pl.ds(start, BLK, stride=dil)


GRP = 8


def _group_rows(i, dil, S):
    nb = S // (BLK * dil)
    run = min(nb, GRP)
    chunks = nb // run
    res0, b0 = (i // chunks) * (GRP // run), (i % chunks) * run
    cur = [(b0 + j % run) * (BLK * dil) + res0 + j // run for j in range(GRP)]
    t = lax.broadcasted_iota(jnp.int32, (GRP, 1, 1), 0)
    if run == nb:
        before, pen = None, jnp.where(t % run == 0, NEG, 0.0)
    else:
        before = _rows(jnp.maximum(b0 - 1, 0) * (BLK * dil) + res0, dil)
        pen = jnp.where((t == 0) & (b0 == 0), NEG, 0.0)
    return [_rows(s, dil) for s in cur], before, pen


def _load_group(ref, rows):
    return jnp.stack([ref[r, :] for r in rows])


def _with_prev(ref, before, cur_blocks):
    first = cur_blocks[:1] if before is None else ref[before, :][None].astype(cur_blocks.dtype)
    return jnp.concatenate([first, cur_blocks[:-1]], axis=0)


def _bdot_nt(a, b):
    return lax.dot_general(a, b, (((2,), (2,)), ((0,), (0,))), preferred_element_type=F32)


def _bdot(a, b):
    return lax.dot_general(a, b, (((2,), (1,)), ((0,), (0,))), preferred_element_type=F32)


def _bdot_tn(a, b):
    return lax.dot_general(a, b, (((1,), (1,)), ((0,), (0,))), preferred_element_type=F32)


def _lsum(x):
    return jnp.sum(x, axis=-1, keepdims=True)


def _widen(src, dst):
    S = src.shape[1]

    def chunk(i, carry):
        rows = pl.ds(pl.multiple_of(i * 512, 512), 512)
        for a in range(3):
            dst[a, rows, :] = src[a, rows, :].astype(F32)
        return carry

    lax.fori_loop(0, S // 512, chunk, 0)


def attn_fwd(qkv, bias, job=None):
    S = qkv.shape[3]
    nblk = S // BLK
    car = Carried(job, 2, 2, 3, 4)

    def body(*refs):
        (qkv_in, bias_ref, o_ref, lse_ref, qkv_ref, part_o, part_ml), jrefs = car.split(refs)
        car.phase(0, pl.program_id(0), jrefs)
        _widen(qkv_in, qkv_ref)
        lane = lax.broadcasted_iota(jnp.int32, (GRP, BLK, BLK), 2)
        lo = lane < HEAD_DIM
        masks = (lo, jnp.logical_not(lo))
        lov = lax.broadcasted_iota(jnp.int32, (GRP, KEYS, BLK), 2) < HEAD_DIM
        vmasks = (lov, jnp.logical_not(lov))
        prev_keys = lax.broadcasted_iota(jnp.int32, (1, 1, KEYS), 2) < BLK
        q_ref, k_ref, v_ref = qkv_ref.at[0], qkv_ref.at[1], qkv_ref.at[2]
        targets = [(o_ref, lse_ref)] + [(part_o.at[t], part_ml.at[t]) for t in range(len(PATTERNS) - 1)]
        for p, (_, dil) in enumerate(PATTERNS):
            def step(i, carry, p=p, dil=dil):
                rc, before, pen = _group_rows(i, dil, S)
                q2 = _load_group(q_ref, rc) * Q_SCALE
                kc = _load_group(k_ref, rc).astype(BF16)
                keys = jnp.concatenate([_with_prev(k_ref, before, kc), kc], axis=1)
                vc = _load_group(v_ref, rc)
                vals = jnp.concatenate([_with_prev(v_ref, before, vc), vc], axis=1)
                pen = jnp.where(prev_keys, pen, 0.0)
                pv = jnp.zeros((GRP, BLK, BLK), F32)
                ms, ls = [], []
                for h in range(2):
                    qh = jnp.where(masks[h], q2, 0.0).astype(BF16)
                    s = _bdot_nt(qh, keys) + bias_ref[p, h][None] + pen
                    mn = jnp.max(s, axis=-1, keepdims=True)
                    pr = jnp.exp(s - mn)
                    pv = pv + _bdot(pr.astype(BF16), jnp.where(vmasks[h], vals, 0.0).astype(BF16))
                    ms.append(mn)
                    ls.append(_lsum(pr))
                ml_t = jnp.where(lo, jnp.where(lane < HALF_HEAD, ms[0], ls[0]),
                                 jnp.where(lane < HEAD_DIM + HALF_HEAD, ms[1], ls[1]))
                acc_ref, ml_ref = targets[p]
                for j, r in enumerate(rc):
                    acc_ref[r, :] = pv[j]
                    ml_ref[r, :] = ml_t[j]
                return carry

            lax.fori_loop(0, nblk // GRP, step, 0)

        def merge(i, carry):
            rows = pl.ds(pl.multiple_of(i * 512, 512), 512)
            is_m = (lax.broadcasted_iota(jnp.int32, (512, BLK), 1) & HALF_HEAD) == 0
            accs = [ref[rows, :] for ref, _ in targets]
            mls = [ref[rows, :] for _, ref in targets]
            mx = [jnp.where(is_m, ml, pltpu.roll(ml, HALF_HEAD, 1)) for ml in mls]
            dn = [jnp.where(is_m, pltpu.roll(ml, BLK - HALF_HEAD, 1), ml) for ml in mls]
            top = jnp.maximum(jnp.maximum(mx[0], mx[1]), mx[2])
            ws_ = [jnp.exp(m - top) for m in mx]
            den = ws_[0] * dn[0] + ws_[1] * dn[1] + ws_[2] * dn[2]
            o_ref[rows, :] = (ws_[0] * accs[0] + ws_[1] * accs[1] + ws_[2] * accs[2]) / den
            lse_ref[rows, :] = top + jnp.log(den)
            return carry

        lax.fori_loop(0, S // 512, merge, 0)
        car.phase(1, pl.program_id(0), jrefs)
        car.phase(2, pl.program_id(0), jrefs)

    outs = pl.pallas_call(
        body, grid=(4,), name="attn_fwd",
        in_specs=[pl.BlockSpec((None, 3, None, S, BLK), lambda g: (0, 0, g, 0, 0)),
                  pl.BlockSpec((3, 2, BLK, KEYS), lambda g: (0, g, 0, 0))] + car.in_specs(),
        out_specs=[pl.BlockSpec((None, S, BLK), lambda g: (g, 0, 0)),
                   pl.BlockSpec((None, S, BLK), lambda g: (g, 0, 0))] + car.out_specs(),
        out_shape=[SDS((4, S, BLK), F32), SDS((4, S, BLK), F32)] + car.out_shapes(),
        input_output_aliases=car.aliases(),
        scratch_shapes=[pltpu.VMEM((3, S, BLK), F32), pltpu.VMEM((2, S, BLK), F32), pltpu.VMEM((2, S, BLK), F32)] + car.sems(),
        compiler_params=_cp("arbitrary"))(qkv, bias, *car.args())
    if job is not None:
        job.landed = list(outs[2:])
    return outs[0], outs[1]


def attn_bwd(qkv, bias, o, lse, do, job=None):
    S = qkv.shape[3]
    nblk = S // BLK
    car = Carried(job, 5, 2, 4, 4)

    def body(*refs):
        (qkv_in, bias_ref, o_ref, lse_ref, do_ref, dqkv_out, ds_ref, qkv_ref, dqkv_ref, last_ref, st_ref), jrefs = car.split(refs)
        car.phase(0, pl.program_id(0), jrefs)
        _widen(qkv_in, qkv_ref)

        def stats(i, carry):
            rows = pl.ds(pl.multiple_of(i * 512, 512), 512)
            ln = lax.broadcasted_iota(jnp.int32, (512, BLK), 1)
            prod = do_ref[rows, :] * o_ref[rows, :]
            d0 = _rsum(jnp.where(ln < HEAD_DIM, prod, 0.0))
            d1 = _rsum(jnp.where(ln < HEAD_DIM, 0.0, prod))
            st_ref[rows, :] = jnp.where((ln & HALF_HEAD) == 0, lse_ref[rows, :], jnp.where(ln < HEAD_DIM, d0, d1))
            return carry

        lax.fori_loop(0, S // 512, stats, 0)
        lane = lax.broadcasted_iota(jnp.int32, (GRP, BLK, BLK), 2)
        lo = lane < HEAD_DIM
        masks = (lo, jnp.logical_not(lo))
        lov = lax.broadcasted_iota(jnp.int32, (GRP, KEYS, BLK), 2) < HEAD_DIM
        vmasks = (lov, jnp.logical_not(lov))
        prev_keys = lax.broadcasted_iota(jnp.int32, (1, 1, KEYS), 2) < BLK
        ds_ref[...] = jnp.zeros_like(ds_ref)
        q_ref, k_ref, v_ref = qkv_ref.at[0], qkv_ref.at[1], qkv_ref.at[2]
        last = len(PATTERNS) - 1
        for p, (_, dil) in enumerate(PATTERNS):
            def step(i, carry, p=p, dil=dil):
                tgt = last_ref if p == last else dqkv_ref
                dq_ref, dk_ref, dv_ref = tgt.at[0], tgt.at[1], tgt.at[2]
                rc, before, pen = _group_rows(i, dil, S)
                q2 = _load_group(q_ref, rc) * Q_SCALE
                kc = _load_group(k_ref, rc).astype(BF16)
                keys = jnp.concatenate([_with_prev(k_ref, before, kc), kc], axis=1)
                vc = _load_group(v_ref, rc).astype(BF16)
                vals = jnp.concatenate([_with_prev(v_ref, before, vc), vc], axis=1)
                pen = jnp.where(prev_keys, pen, 0.0)
                dot = _load_group(do_ref, rc)
                st = _load_group(st_ref, rc)
                dq = jnp.zeros((GRP, BLK, BLK), F32)
                dkeys = jnp.zeros((GRP, KEYS, BLK), F32)
                dvals = jnp.zeros((GRP, KEYS, BLK), F32)
                for h in range(2):
                    qh = jnp.where(masks[h], q2, 0.0).astype(BF16)
                    doh = jnp.where(masks[h], dot, 0.0).astype(BF16)
                    lh = st[:, :, HEAD_DIM * h:HEAD_DIM * h + 1]
                    delta = st[:, :, HEAD_DIM * h + HALF_HEAD:HEAD_DIM * h + HALF_HEAD + 1]
                    pr = jnp.exp(_bdot_nt(qh, keys) + bias_ref[p, h][None] + pen - lh)
                    ds = pr * (_bdot_nt(doh, vals) - delta)
                    ds_ref[p, h] += jnp.sum(ds, axis=0)
                    dsb = ds.astype(BF16)
                    dq = dq + jnp.where(masks[h], _bdot(dsb, keys), 0.0)
                    dkeys = dkeys + _bdot_tn(dsb, qh)
                    dvals = dvals + _bdot_tn(pr.astype(BF16), doh)
                dkp, dkc, dvp, dvc = dkeys[:, :BLK], dkeys[:, BLK:], dvals[:, :BLK], dvals[:, BLK:]
                none = jnp.zeros((1, BLK, BLK), F32)
                dkc = dkc + jnp.concatenate([dkp[1:], none], axis=0)
                dvc = dvc + jnp.concatenate([dvp[1:], none], axis=0)
                for j in range(GRP):
                    if p == 1:
                        dq_ref[rc[j], :] += dq[j] * Q_SCALE
                        dk_ref[rc[j], :] += dkc[j]
                        dv_ref[rc[j], :] += dvc[j]
                    else:
                        dq_ref[rc[j], :] = dq[j] * Q_SCALE
                        dk_ref[rc[j], :] = dkc[j]
                        dv_ref[rc[j], :] = dvc[j]
                if before is not None:
                    dk_ref[before, :] += dkp[0]
                    dv_ref[before, :] += dvp[0]
                return carry

            lax.fori_loop(0, nblk // GRP, step, 0)

        def narrow(i, carry):
            rows = pl.ds(pl.multiple_of(i * 512, 512), 512)
            for a in range(3):
                dqkv_out[a, rows, :] = (dqkv_ref[a, rows, :] + last_ref[a, rows, :]).astype(BF16)
            return carry

        lax.fori_loop(0, S // 512, narrow, 0)
        car.phase(1, pl.program_id(0), jrefs)
        car.phase(2, pl.program_id(0), jrefs)

    blk = pl.BlockSpec((None, S, BLK), lambda g: (g, 0, 0))
    outs = pl.pallas_call(
        body, grid=(4,), name="attn_bwd",
        in_specs=[pl.BlockSpec((None, 3, None, S, BLK), lambda g: (0, 0, g, 0, 0)),
                  pl.BlockSpec((3, 2, BLK, KEYS), lambda g: (0, g, 0, 0)), blk, blk, blk] + car.in_specs(),
        out_specs=[pl.BlockSpec((None, 3, None, S, BLK), lambda g: (0, 0, g, 0, 0)),
                   pl.BlockSpec((3, 2, BLK, KEYS), lambda g: (0, g, 0, 0))] + car.out_specs(),
        out_shape=[SDS((2, 3, 4, S, BLK), BF16), SDS((3, N_HEADS, BLK, KEYS), F32)] + car.out_shapes(),
        input_output_aliases=car.aliases(),
        scratch_shapes=[pltpu.VMEM((3, S, BLK), F32)] * 3 + [pltpu.VMEM((S, BLK), F32)] + car.sems(),
        compiler_params=_cp("arbitrary"))(qkv, bias, o, lse, do, *car.args())
    if job is not None:
        job.landed = list(outs[2:])
    return outs[0], outs[1]


def _shift_down(u, k, halo):
    n = u.shape[0]
    row = lax.broadcasted_iota(jnp.int32, u.shape, 0)
    out = pltpu.roll(u, k, 0)
    hn = halo.shape[0]
    for j in range(k):
        out = jnp.where(row == j, halo[hn - k + j:hn - k + j + 1, :], out)
    return out


def _shift_up(u, k, halo):
    n = u.shape[0]
    row = lax.broadcasted_iota(jnp.int32, u.shape, 0)
    out = pltpu.roll(u, n - k, 0)
    for j in range(k):
        out = jnp.where(row == n - k + j, halo[j:j + 1, :], out)
    return out


def _conv3(u, halo, w0, w1, w2):
    return _shift_down(u, 2, halo) * w0 + _shift_down(u, 1, halo) * w1 + u * w2


def _colsum(x):
    return jnp.sum(x, axis=0, keepdims=True)


def _mixer_specs(S, tm):
    conv = pl.BlockSpec((None, 12, tm, BLK), lambda i: (1, 0, i, 0))
    halo = pl.BlockSpec((None, 12, 16, BLK), lambda i: (1, 0, jnp.maximum(i * (tm // 16) - 1, 0), 0))
    ob = pl.BlockSpec((4, tm, BLK), lambda i: (0, i, 0))
    return conv, halo, ob


def _mixer_recompute(i, o_ref, pr_ref, ph_ref, w_ref):
    ob = [o_ref[q] for q in range(4)]
    gb = [pr_ref[q].astype(F32) for q in range(4)]
    gc = [pr_ref[4 + q].astype(F32) for q in range(4)]
    xi = [pr_ref[8 + q].astype(F32) for q in range(4)]
    keep = jnp.where(i > 0, 1.0, 0.0)
    u = [gc[q] * xi[q] for q in range(4)]
    hu = [ph_ref[4 + q].astype(F32) * ph_ref[8 + q].astype(F32) * keep for q in range(4)]
    w = [[w_ref[k:k + 1, q * BLK:(q + 1) * BLK] for k in range(3)] for q in range(4)]
    cv = [_conv3(u[q], hu[q], *w[q]) for q in range(4)]
    return ob, gb, gc, xi, u, hu, cv, w


def _rms_blocks(blocks):
    ss = sum(_rsum(b * b) for b in blocks)
    return lax.rsqrt(ss / (BLK * len(blocks)) + EPS)


def mixer_post_fwd(o, proj, w_sc, g_a, g_c):
    S = o.shape[1]
    tm = 512

    def body(o_ref, pr_ref, ph_ref, w_ref, ga_ref, gc_ref, m_ref):
        i = pl.program_id(0)
        ob, gb, _, _, _, _, cv, _ = _mixer_recompute(i, o_ref, pr_ref, ph_ref, w_ref)
        conv = [gb[q] * cv[q] for q in range(4)]
        ra, rc = _rms_blocks(ob), _rms_blocks(conv)
        for q in range(4):
            sl = slice(q * BLK, (q + 1) * BLK)
            m_ref[:, q * BLK:(q + 1) * BLK] = (ob[q] * ra * ga_ref[:, sl]).astype(BF16)
            m_ref[:, ATTN_W + q * BLK:ATTN_W + (q + 1) * BLK] = (conv[q] * rc * gc_ref[:, sl]).astype(BF16)

    conv_s, halo_s, ob_s = _mixer_specs(S, tm)
    full = lambda r, c: pl.BlockSpec((r, c), lambda i: (0, 0))
    return pl.pallas_call(
        body, grid=(S // tm,), name="mixer_post_fwd",
        in_specs=[ob_s, conv_s, halo_s, full(3, 512), full(1, 512), full(1, 512)],
        out_specs=pl.BlockSpec((tm, D_MODEL), lambda i: (i, 0)), out_shape=SDS((S, D_MODEL), BF16),
        compiler_params=_cp("parallel"))(o, proj, proj, w_sc, g_a, g_c)


def mixer_post_bwd_a(dmixed, o, proj, w_sc, g_a, g_c, job=None):
    S = o.shape[1]
    tm = 512

    def body(dm_ref, o_ref, pr_ref, ph_ref, w_ref, ga_ref, gc_ref, do_ref, dgb_ref, dcv_ref, dga_ref, dgc_ref):
        i = pl.program_id(0)
        ob, gb, _, _, _, _, cv, _ = _mixer_recompute(i, o_ref, pr_ref, ph_ref, w_ref)
        conv = [gb[q] * cv[q] for q in range(4)]
        ra, rc = _rms_blocks(ob), _rms_blocks(conv)

        @pl.when(i == 0)
        def _():
            dga_ref[...] = jnp.zeros_like(dga_ref)
            dgc_ref[...] = jnp.zeros_like(dgc_ref)

        for blocks, r, g_ref, off, dg_ref, is_attn in ((ob, ra, ga_ref, 0, dga_ref, True), (conv, rc, gc_ref, ATTN_W, dgc_ref, False)):
            xh = [blocks[q] * r for q in range(4)]
            dm = [dm_ref[:, off + q * BLK:off + (q + 1) * BLK].astype(F32) for q in range(4)]
            gd = [dm[q] * g_ref[:, q * BLK:(q + 1) * BLK] for q in range(4)]
            mean = sum(_rsum(gd[q] * xh[q]) for q in range(4)) / (4 * BLK)
            for q in range(4):
                dg_ref[:, q * BLK:(q + 1) * BLK] += _colsum(dm[q] * xh[q])
                dx = r * (gd[q] - xh[q] * mean)
                if is_attn:
                    do_ref[q] = dx
                else:
                    dgb_ref[q] = dx * cv[q]
                    dcv_ref[q] = dx * gb[q]

    conv_s, halo_s, ob_s = _mixer_specs(S, tm)
    full = lambda r, c: pl.BlockSpec((r, c), lambda i: (0, 0))
    return _pcall(
        body, [dmixed, o, proj, proj, w_sc, g_a, g_c], grid=(S // tm,), name="mixer_post_bwd_a",
        in_specs=[pl.BlockSpec((tm, D_MODEL), lambda i: (i, 0)), ob_s, conv_s, halo_s, full(3, 512), full(1, 512), full(1, 512)],
        out_specs=[ob_s, ob_s, ob_s, full(1, 512), full(1, 512)],
        out_shape=[SDS((4, S, BLK), F32)] * 3 + [SDS((1, 512), F32)] * 2, sem=("arbitrary",), job=job)


def mixer_post_bwd_b(dproj, dgb, dcv, proj, w_sc):
    S = proj.shape[2]
    tm = 512
    last = S // 8 - 1

    def body(dp_in, dgb_ref, dcv_ref, dn_ref, pr_ref, w_ref, dp_ref, dw_ref):
        i = pl.program_id(0)
        keep_next = jnp.where(i < pl.num_programs(0) - 1, 1.0, 0.0)

        @pl.when(i == 0)
        def _():
            dw_ref[...] = jnp.zeros_like(dw_ref)

        for q in range(4):
            sl = slice(q * BLK, (q + 1) * BLK)
            gc, xi = pr_ref[4 + q].astype(F32), pr_ref[8 + q].astype(F32)
            u = gc * xi
            dcv = dcv_ref[q]
            dn = dn_ref[q] * keep_next
            d1, d2 = _shift_up(dcv, 1, dn), _shift_up(dcv, 2, dn)
            w0, w1, w2 = (w_ref[k:k + 1, sl] for k in range(3))
            du = dcv * w2 + d1 * w1 + d2 * w0
            dw_ref[0:1, sl] += _colsum(d2 * u)
            dw_ref[1:2, sl] += _colsum(d1 * u)
            dw_ref[2:3, sl] += _colsum(dcv * u)
            dp_ref[q] = dgb_ref[q].astype(BF16)
            dp_ref[4 + q] = (du * xi).astype(BF16)
            dp_ref[8 + q] = (du * gc).astype(BF16)

    conv_s, _, ob_s = _mixer_specs(S, tm)
    nxt = pl.BlockSpec((4, 8, BLK), lambda i: (0, jnp.minimum((i + 1) * (tm // 8), last), 0))
    full = lambda r, c: pl.BlockSpec((r, c), lambda i: (0, 0))
    return pl.pallas_call(
        body, grid=(S // tm,), name="mixer_post_bwd_b",
        in_specs=[pl.BlockSpec(memory_space=pl.ANY), ob_s, ob_s, nxt, conv_s, full(3, 512)],
        out_specs=[conv_s, full(3, 512)],
        out_shape=[SDS(dproj.shape, BF16), SDS((3, 512), F32)],
        input_output_aliases={0: 0}, compiler_params=_cp("arbitrary"))(dproj, dgb, dcv, dcv, proj, w_sc)


def xattn_fwd(q, k, v):
    S = q.shape[0]
    tm = 512
    scale = MEM_HD ** -0.5

    def body(q_ref, k_ref, v_ref, o_ref):
        for h in range(N_MEM_HEADS):
            sl = slice(h * MEM_HD, (h + 1) * MEM_HD)
            s = _dot_nt(q_ref[:, sl], k_ref[:, sl]) * scale
            p = jnp.exp(s - jnp.max(s, axis=1, keepdims=True))
            p = p / _rsum(p)
            o_ref[:, sl] = _dot(p.astype(BF16), v_ref[:, sl]).astype(BF16)

    row = pl.BlockSpec((tm, D_MODEL), lambda i: (i, 0))
    kv = pl.BlockSpec(k.shape, lambda i: (0, 0))
    return pl.pallas_call(body, grid=(S // tm,), name="xattn_fwd", in_specs=[row, kv, kv], out_specs=row,
                          out_shape=SDS((S, D_MODEL), BF16), compiler_params=_cp("parallel"))(q, k, v)


def xattn_bwd(q, k, v, do):
    S = q.shape[0]
    tm = 512
    scale = MEM_HD ** -0.5

    def body(q_ref, k_ref, v_ref, do_ref, dq_ref, dk_ref, dv_ref):
        @pl.when(pl.program_id(0) == 0)
        def _():
            dk_ref[...] = jnp.zeros_like(dk_ref)
            dv_ref[...] = jnp.zeros_like(dv_ref)

        for h in range(N_MEM_HEADS):
            sl = slice(h * MEM_HD, (h + 1) * MEM_HD)
            qh, kh, vh, doh = q_ref[:, sl], k_ref[:, sl], v_ref[:, sl], do_ref[:, sl]
            s = _dot_nt(qh, kh) * scale
            p = jnp.exp(s - jnp.max(s, axis=1, keepdims=True))
            p = p / _rsum(p)
            dp = _dot_nt(doh, vh)
            ds = (p * (dp - _rsum(p * dp)) * scale).astype(BF16)
            dq_ref[:, sl] = _dot(ds, kh).astype(BF16)
            dk_ref[:, sl] += _dot_tn(ds, qh)
            dv_ref[:, sl] += _dot_tn(p.astype(BF16), doh)

    row = pl.BlockSpec((tm, D_MODEL), lambda i: (i, 0))
    kv = pl.BlockSpec(k.shape, lambda i: (0, 0))
    return pl.pallas_call(body, grid=(S // tm,), name="xattn_bwd", in_specs=[row, kv, kv, row], out_specs=[row, kv, kv],
                          out_shape=[SDS((S, D_MODEL), BF16), SDS(k.shape, F32), SDS(k.shape, F32)],
                          compiler_params=_cp("arbitrary"))(q, k, v, do)


FFN_TM, FFN_TC = 256, 1408


def _ffn_specs(S):
    tm, tc = FFN_TM, FFN_TC
    blk = pl.BlockSpec((2, tm, tc), lambda j, i: (0, i, j))
    nxt = pl.BlockSpec((2, 16, tc), lambda j, i: (0, jnp.minimum((i + 1) * (tm // 16), S // 16 - 1), j))
    wsp = pl.BlockSpec((2, 3, tc), lambda j, i: (0, 0, j))
    bsp = pl.BlockSpec((2, 1, tc), lambda j, i: (0, 0, j))
    return blk, nxt, wsp, bsp


def up_act_fwd(h, w_up, w_fc, b_fc):
    S, K = h.shape
    tm, tc = FFN_TM, FFN_TC
    nj = D_FF // tc

    def body(h_ref, wg_ref, wv_ref, w_ref, b_ref, pre_ref, gv_ref, a_ref, halo):
        i = pl.program_id(1)
        hv = h_ref[...]
        res = []
        for half, wt_ref in enumerate((wg_ref, wv_ref)):
            u = _dot(hv, wt_ref[...])
            hu = jnp.where(i > 0, halo[half], 0.0)
            halo[half] = u[tm - 8:, :]
            w0, w1, w2 = (w_ref[half, k:k + 1, :] for k in range(3))
            pre_ref[half] = u.astype(BF16)
            res.append(_conv3(u, hu, w0, w1, w2) + b_ref[half])
        g, v = res
        gv_ref[0] = g.astype(BF16)
        gv_ref[1] = v.astype(BF16)
        a_ref[...] = (g * jax.nn.sigmoid(g) * v).astype(BF16)

    blk = pl.BlockSpec((2, tm, tc), lambda j, i: (0, i, j))
    return pl.pallas_call(
        body, grid=(nj, S // tm), name="up_act_fwd",
        in_specs=[pl.BlockSpec((tm, K), lambda j, i: (i, 0)), pl.BlockSpec((K, tc), lambda j, i: (0, j)),
                  pl.BlockSpec((K, tc), lambda j, i: (0, j + nj)), pl.BlockSpec((2, 3, tc), lambda j, i: (0, 0, j)),
                  pl.BlockSpec((2, 1, tc), lambda j, i: (0, 0, j))],
        out_specs=[blk, blk, pl.BlockSpec((tm, tc), lambda j, i: (i, j))],
        out_shape=[SDS((2, S, D_FF), BF16), SDS((2, S, D_FF), BF16), SDS((S, D_FF), BF16)],
        scratch_shapes=[pltpu.VMEM((2, 8, tc), F32)],
        compiler_params=_cp("parallel", "arbitrary"))(h, w_up, w_up, w_fc, b_fc)


def ffn_act_bwd(dx, w_down, up, up_pre, w_fc):
    S = up.shape[1]

    def act_grads(da, g, v):
        sg = jax.nn.sigmoid(g)
        return da * v * (sg * (1.0 + g * (1.0 - sg))), da * g * sg

    def body(dx_ref, dxn_ref, wd_ref, gv_ref, gvn_ref, up_ref, w_ref, dp_ref, dw_ref, db_ref):
        i = pl.program_id(1)
        keep_next = jnp.where(i < pl.num_programs(1) - 1, 1.0, 0.0)

        @pl.when(i == 0)
        def _():
            dw_ref[...] = jnp.zeros_like(dw_ref)
            db_ref[...] = jnp.zeros_like(db_ref)

        wd = wd_ref[...]
        da = _dot_nt(dx_ref[...].astype(BF16), wd)
        dan = _dot_nt(dxn_ref[...].astype(BF16), wd) * keep_next
        here = act_grads(da, gv_ref[0].astype(F32), gv_ref[1].astype(F32))
        after = act_grads(dan, gvn_ref[0].astype(F32), gvn_ref[1].astype(F32))
        for half in range(2):
            d, dn = here[half], after[half]
            u = up_ref[half].astype(F32)
            d1, d2 = _shift_up(d, 1, dn), _shift_up(d, 2, dn)
            w0, w1, w2 = (w_ref[half, k:k + 1, :] for k in range(3))
            dp_ref[half] = (d * w2 + d1 * w1 + d2 * w0).astype(BF16)
            dw_ref[half, 0:1, :] += _colsum(d2 * u)
            dw_ref[half, 1:2, :] += _colsum(d1 * u)
            dw_ref[half, 2:3, :] += _colsum(d * u)
            db_ref[half] += _colsum(d)

    tm, tc = FFN_TM, FFN_TC
    blk, nxt, wsp, bsp = _ffn_specs(S)
    K = dx.shape[1]
    rows = pl.BlockSpec((tm, K), lambda j, i: (i, 0))
    rows_next = pl.BlockSpec((16, K), lambda j, i: (jnp.minimum((i + 1) * (tm // 16), S // 16 - 1), 0))
    return pl.pallas_call(body, grid=(D_FF // tc, S // tm), name="ffn_act_bwd",
                          in_specs=[rows, rows_next, pl.BlockSpec((tc, K), lambda j, i: (j, 0)), blk, nxt, blk, wsp],
                          out_specs=[blk, wsp, bsp],
                          out_shape=[SDS((2, S, D_FF), BF16), SDS((2, 3, D_FF), F32), SDS((2, 1, D_FF), F32)],
                          compiler_params=_cp("parallel", "arbitrary"))(dx, dx, w_down, up, up, up_pre, w_fc)


def local_step(x, mem, target, wb, ws, comm=None):
    S = x.shape[0]
    assert S % 2048 == 0
    if comm is None:
        comm = NoComm()
    else:
        wb = {}
    ws = dict(ws)

    bias = bias_tables(ws["rel_bias"])
    h1 = rmsnorm_fwd(x, ws["g_mix"], "norm_mix", job=comm.job("norm_mix"))
    comm.landed("norm_mix", wb, ws)
    w_fc = ws["w_ffn_conv"].reshape(3, 2, D_FF).transpose(1, 0, 2)
    b_fc = ws["b_ffn_conv"].reshape(2, 1, D_FF)
    proj = mm_nn(h1, wb["w_in"], "proj", out_dtype=BF16, out_bw=BLK, tn=768, job=comm.job("proj"))
    comm.landed("proj", wb)
    qkv = proj.reshape(2, 3, 4, S, BLK)
    o, lse = attn_fwd(qkv, bias, job=comm.job("attn_fwd"))
    comm.landed("attn_fwd", wb)
    proj4 = proj.reshape(2, 12, S, BLK)
    mixed = mixer_post_fwd(o, proj4, ws["w_short_conv"], ws["g_attn_out"], ws["g_conv_out"])
    x1, h2 = mm_nn_rows(mixed, wb["w_out"], x, ws["g_xattn"], "out_proj")
    mem_n = rmsnorm_fwd(mem, ws["g_mem"], "norm_mem")
    xq = mm_nn(h2, wb["w_xq"], "xq", out_dtype=BF16)
    xk = mm_nn(mem_n, wb["w_xk"], "xk", out_dtype=BF16, tn=1024)
    xv = mm_nn(mem_n, wb["w_xv"], "xv", out_dtype=BF16, tn=1024)
    xo = xattn_fwd(xq, xk, xv)
    x2, h3 = mm_nn_rows(xo, wb["w_xo"], x1, ws["g_ffn"], "xo_proj")
    up_pre, up, act = up_act_fwd(h3, wb["w_up"], w_fc, b_fc)
    loss, dx3, dg_final = mm_nn_rows(act, wb["w_down"], x2, ws["g_final"].reshape(1, -1), "down_proj", target=target)

    gb, gs = {}, {"g_final": dg_final}
    gb["w_down"] = mm_tn(act, dx3, None, "dw_down", tm=1408, tn=1024)
    d_up_pre, dw_fc, db_fc = ffn_act_bwd(dx3, wb["w_down"], up, up_pre, w_fc)
    gs["b_ffn_conv"] = db_fc.reshape(1, 2 * D_FF)
    gs["w_ffn_conv"] = dw_fc.transpose(1, 0, 2).reshape(3, 2 * D_FF)
    gb["w_up"] = mm_tn(h3, d_up_pre, D_FF, "dw_up", shards=4, tn=1408)
    dx2, gs["g_ffn"] = mm_nt_norm_bwd(d_up_pre, D_FF, wb["w_up"], x2, ws["g_ffn"], dx3, "d_h3", tm=256,
                                      job=comm.job("d_h3", gb))
    comm.landed("d_h3")
    gb["w_xo"] = mm_tn(xo, dx2, None, "dw_xo")
    dxo = mm_nt(dx2, None, wb["w_xo"], "d_xo", out_dtype=BF16)
    dxq, dxk, dxv = xattn_bwd(xq, xk, xv, dxo)
    gb["w_xq"] = mm_tn(h2, dxq, None, "dw_xq")
    gb["w_xk"] = mm_tn(mem_n, dxk, None, "dw_xk", tn=1024)
    gb["w_xv"] = mm_tn(mem_n, dxv, None, "dw_xv", tn=1024)
    dmem_n = mm_nt(dxk, None, wb["w_xk"], "d_memk", tn=1024)
    dmem_n = mm_nt(dxv, None, wb["w_xv"], "d_memv", res=dmem_n, tn=1024)
    gs["g_mem"] = rmsnorm_gain_grad(mem, dmem_n, "norm_mem_bwd")
    dx1, gs["g_xattn"] = mm_nt_norm_bwd(dxq, None, wb["w_xq"], x1, ws["g_xattn"], dx2, "d_h2")
    gb["w_out"] = mm_tn(mixed, dx1, None, "dw_out")
    dmixed = mm_nt(dx1, None, wb["w_out"], "d_mixed", out_dtype=BF16)
    do, dgb, dcv, gs["g_attn_out"], gs["g_conv_out"] = mixer_post_bwd_a(
        dmixed, o, proj4, ws["w_short_conv"], ws["g_attn_out"], ws["g_conv_out"], job=comm.job("mixer_post_bwd_a", gb))
    comm.landed("mixer_post_bwd_a")
    dproj, ds_sum = attn_bwd(qkv, bias, o, lse, do, job=comm.job("attn_bwd"))
    comm.landed("attn_bwd")
    gs["rel_bias"] = bias_tables_bwd(ds_sum)
    dproj, gs["w_short_conv"] = mixer_post_bwd_b(dproj.reshape(2, 12, S, BLK), dgb, dcv, proj4, ws["w_short_conv"])
    dproj = dproj.reshape(24, S, BLK)
    gb["w_in"] = mm_tn(h1, dproj, BLK, "dw_in", shards=4, tn=768, job=comm.job("dw_in"))
    comm.landed("dw_in")
    comm.first_update(gb)
    grad_x, gs["g_mix"] = mm_nt_norm_bwd(dproj, BLK, wb["w_in"], x, ws["g_mix"], dx1, "d_h1", tm=256,
                                         job=comm.job("d_h1", gb))
    comm.landed("d_h1")
    return loss, grad_x, comm.finish(gb), gs


def _adam_update(w_ref, g_ref, m_ref, v_ref, d_ref, nm_ref, nv_ref):
    gv = g_ref[...]
    mn = ADAM_B1 * m_ref[...] + (1.0 - ADAM_B1) * gv
    vn = ADAM_B2 * v_ref[...] + (1.0 - ADAM_B2) * (gv * gv)
    m_hat = mn / (1.0 - ADAM_B1 ** ADAM_STEP)
    v_hat = vn / (1.0 - ADAM_B2 ** ADAM_STEP)
    d_ref[...] = -ADAM_LR * (m_hat / (jnp.sqrt(v_hat) + ADAM_EPS) + ADAM_WD * w_ref[...])
    nm_ref[...] = mn
    nv_ref[...] = vn


def adamw(ws, gs, ms, vs, name, job=None):
    n = len(ws)
    R, C = ws[0].shape
    tr = R // 4

    def body(*refs):
        for p in range(n):
            _adam_update(*(refs[q * n + p] for q in range(7)))

    blk = pl.BlockSpec((tr, C), lambda i: (i, 0))
    outs = _pcall(body, [*ws, *gs, *ms, *vs], grid=(4,), name=name, in_specs=[blk] * (4 * n), out_specs=[blk] * (3 * n),
                  out_shape=[SDS((R, C), F32)] * (3 * n), sem=("parallel",), job=job)
    return outs[:n], outs[n:2 * n], outs[2 * n:]


BIG = ("w_in", "w_out", "w_xq", "w_xk", "w_xv", "w_xo", "w_up", "w_down")
COL_SHARDED = ("w_in", "w_up")
ANY = pl.BlockSpec(memory_space=pl.ANY)


def _place():
    x, y, c = lax.axis_index("x"), lax.axis_index("y"), lax.axis_index("c")
    chips = [(1 - x, y), (x, 1 - y), (1 - x, 1 - y)]
    return x, y, c, chips


def _window(full, name, R, C, shard, half):
    r0, nr = (0, R) if half is None else (half * (R // 2), R // 2)
    if name in COL_SHARDED:
        return full.at[pl.ds(r0, nr), pl.ds(shard * C, C)]
    return full.at[pl.ds(shard * R + r0, nr), :]


def same_shape_groups(names, arrays):
    groups = {}
    for n in names:
        groups.setdefault(tuple(arrays[n].shape), []).append(n)
    return list(groups.values())


def place_shards(ws, mine_arr, col, name):
    n = len(ws)
    R, C = ws[0].shape

    def body(s_ref, *refs):
        for p in range(n):
            refs[n + p][...] = refs[p][...].astype(BF16)

    grid_spec = pltpu.PrefetchScalarGridSpec(
        num_scalar_prefetch=1, grid=(1,), in_specs=[pl.BlockSpec((R, C), lambda i, s: (0, 0))] * n,
        out_specs=[pl.BlockSpec((R, C), (lambda i, s: (0, s[0])) if col else (lambda i, s: (s[0], 0)))] * n)
    return pl.pallas_call(body, grid_spec=grid_spec, name=name,
                          out_shape=[SDS((R, 4 * C) if col else (4 * R, C), BF16)] * n,
                          compiler_params=_cp("arbitrary"))(mine_arr, *ws)


def _gather_jobs(names, shapes):
    nw = len(names)

    def start(full, sems):
        send, recv, fsend, frecv = sems
        x, y, c, chips = _place()
        mine = 2 * x + y
        for w, n in enumerate(names):
            R, C = shapes[w]
            own = _window(full[w], n, R, C, mine, c)
            for j, chip in enumerate(chips):
                pltpu.make_async_remote_copy(src_ref=own, dst_ref=own, send_sem=send.at[w, j], recv_sem=recv.at[w, j],
                                             device_id=(*chip, c), device_id_type=MESH).start()

    def mid(full, sems, lo=0, hi=nw):
        send, recv, fsend, frecv = sems
        x, y, c, chips = _place()
        sib = (x, y, 1 - c)
        for w, n in list(enumerate(names))[lo:hi]:
            R, C = shapes[w]
            for j, chip in enumerate(chips):
                landed = _window(full[w], n, R, C, 2 * chip[0] + chip[1], c)
                pltpu.make_async_remote_copy(src_ref=landed, dst_ref=landed, send_sem=send.at[w, j], recv_sem=recv.at[w, j],
                                             device_id=(*chip, c), device_id_type=MESH).wait_recv()
                pltpu.make_async_remote_copy(src_ref=landed, dst_ref=landed, send_sem=fsend.at[w, j],
                                             recv_sem=frecv.at[w, j], device_id=sib, device_id_type=MESH).start()

    def finish(full, sems):
        send, recv, fsend, frecv = sems
        x, y, c, chips = _place()
        mine = 2 * x + y
        sib = (x, y, 1 - c)
        for w, n in enumerate(names):
            R, C = shapes[w]
            own = _window(full[w], n, R, C, mine, c)
            for j, chip in enumerate(chips):
                landed = _window(full[w], n, R, C, 2 * chip[0] + chip[1], c)
                other = _window(full[w], n, R, C, 2 * chip[0] + chip[1], 1 - c)
                pltpu.make_async_remote_copy(src_ref=other, dst_ref=other, send_sem=fsend.at[w, j], recv_sem=frecv.at[w, j],
                                             device_id=sib, device_id_type=MESH).wait_recv()
                pltpu.make_async_remote_copy(src_ref=own, dst_ref=own, send_sem=send.at[w, j], recv_sem=recv.at[w, j],
                                             device_id=(*chip, c), device_id_type=MESH).wait_send()
                pltpu.make_async_remote_copy(src_ref=landed, dst_ref=landed, send_sem=fsend.at[w, j],
                                             recv_sem=frecv.at[w, j], device_id=sib, device_id_type=MESH).wait_send()

    return start, mid, finish, [pltpu.SemaphoreType.DMA((nw, 3))] * 4


class CommJob:
    def __init__(self, ins, out_shapes, inplace, start, finish, sems, mid=None):
        self.ins, self.out_shapes, self.inplace = list(ins), list(out_shapes), inplace
        self.start, self.mid, self.finish, self.sems = start, mid, finish, list(sems)

    def run(self, phase, ins, outs, sems):
        if phase == 0:
            self.start(ins, outs, sems)
        elif phase == 1:
            if self.mid is not None:
                self.mid(ins, outs, sems)
        else:
            self.finish(ins, outs, sems)


class Carried:
    def __init__(self, job, n_in, n_out, n_scratch, steps):
        self.job, self.n_in, self.n_out, self.n_scratch, self.steps = job, n_in, n_out, n_scratch, steps
        self.nji = len(job.ins) if job else 0
        self.njo = len(job.out_shapes) if job else 0

    def in_specs(self):
        return [ANY] * self.nji

    def out_specs(self):
        return [ANY] * self.njo

    def out_shapes(self):
        return list(self.job.out_shapes) if self.job else []

    def aliases(self):
        if not (self.job and self.job.inplace):
            return {}
        return {self.n_in + i: self.n_out + i for i in range(self.nji)}

    def sems(self):
        return list(self.job.sems) if self.job else []

    def args(self):
        return list(self.job.ins) if self.job else []

    def split(self, refs):
        a = self.n_in
        b = a + self.nji
        c = b + self.n_out
        d = c + self.njo
        e = d + self.n_scratch
        return refs[:a] + refs[b:c] + refs[d:e], (refs[a:b], refs[c:d], refs[e:])

    def phase(self, phase, step, jrefs):
        if self.job is None:
            return
        at = {0: 0, 1: max(self.steps - 2, 0), 2: self.steps - 1}[phase]

        @pl.when(step == at)
        def _():
            self.job.run(phase, *jrefs)


def _pcall(body, args, *, grid, in_specs, out_specs, out_shape, name, sem, scratch=(), aliases=None, job=None):
    n_in, n_out = len(args), len(out_shape)
    if job is None:
        return pl.pallas_call(
            body, grid=grid, in_specs=list(in_specs), out_specs=list(out_specs), out_shape=list(out_shape), name=name,
            scratch_shapes=list(scratch), input_output_aliases=dict(aliases or {}), compiler_params=_cp(*sem))(*args)
    total = int(np.prod(grid))
    car = Carried(job, n_in, n_out, len(scratch), total)

    def wrapped(*refs):
        main, jrefs = car.split(refs)
        lin = pl.program_id(0)
        for ax in range(1, len(grid)):
            lin = lin * grid[ax] + pl.program_id(ax)
        car.phase(0, lin, jrefs)
        body(*main)
        car.phase(1, lin, jrefs)
        car.phase(2, lin, jrefs)

    res = pl.pallas_call(
        wrapped, grid=grid, in_specs=list(in_specs) + car.in_specs(), out_specs=list(out_specs) + car.out_specs(),
        out_shape=list(out_shape) + car.out_shapes(), name=name, scratch_shapes=list(scratch) + car.sems(),
        input_output_aliases={**dict(aliases or {}), **car.aliases()},
        compiler_params=_cp(*(["arbitrary"] * len(grid))))(*args, *car.args())
    job.landed = list(res[n_out:])
    return list(res[:n_out])


def gather_job(placed, names):
    shapes = []
    for n in names:
        R, C = placed[n].shape
        shapes.append((R, C // 4) if n in COL_SHARDED else (R // 4, C))
    start, mid, finish, sems = _gather_jobs(names, shapes)
    arrays = [placed[n] for n in names]
    early = max(len(names) - 1, 1)

    def last(i, o, s):
        mid(o, s, early, len(names))
        finish(o, s)

    return CommJob(arrays, [SDS(a.shape, a.dtype) for a in arrays], True,
                   lambda i, o, s: start(o, s), last, sems, mid=lambda i, o, s: mid(o, s, 0, early))


def gather_rows_job(placed):
    widths = [a.shape[1] // 4 for a in placed]

    def copies(outs, sems):
        x, y, c, chips = _place()
        mine = 2 * x + y
        cps = []
        for w, ref in enumerate(outs):
            own = ref.at[:, pl.ds(mine * widths[w], widths[w])]
            for j, chip in enumerate(chips):
                theirs = ref.at[:, pl.ds((2 * chip[0] + chip[1]) * widths[w], widths[w])]
                kw = dict(send_sem=sems[0].at[w, j], recv_sem=sems[1].at[w, j], device_id=(*chip, c), device_id_type=MESH)
                cps.append((pltpu.make_async_remote_copy(src_ref=own, dst_ref=own, **kw),
                            pltpu.make_async_remote_copy(src_ref=theirs, dst_ref=theirs, **kw)))
        return cps

    def start(ins, outs, sems):
        for send, _ in copies(outs, sems):
            send.start()

    def finish(ins, outs, sems):
        for send, recv in copies(outs, sems):
            recv.wait_recv()
            send.wait_send()

    return CommJob(placed, [SDS(a.shape, a.dtype) for a in placed], True, start, finish,
                   [pltpu.SemaphoreType.DMA((len(placed), 3))] * 2)


def merge_jobs(a, b):
    assert a.inplace and b.inplace
    na, nb_, nsa = len(a.ins), len(b.ins), len(a.sems)

    def phase(which):
        def run(ins, outs, sems):
            for job, i, o, s in ((a, ins[:na], outs[:na], sems[:nsa]), (b, ins[na:], outs[na:], sems[nsa:])):
                fn = getattr(job, which)
                if fn is not None:
                    fn(i, o, s)
        return run

    merged = CommJob(a.ins + b.ins, a.out_shapes + b.out_shapes, True, phase("start"), phase("finish"), a.sems + b.sems,
                     mid=phase("mid"))
    merged.parts = (a, b, na)
    return merged


def split_landed(merged):
    a, b, na = merged.parts
    a.landed, b.landed = merged.landed[:na], merged.landed[na:]


def pair_exchange_job(grads, names):
    shapes = [grads[n].shape for n in names]

    def copies(ins, outs, sems):
        x, y, c, _ = _place()
        return [pltpu.make_async_remote_copy(
            src_ref=ins[w].at[:, pl.ds((1 - c) * (shapes[w][1] // 2), shapes[w][1] // 2), :], dst_ref=outs[w],
            send_sem=sems[0].at[w], recv_sem=sems[1].at[w], device_id=(x, y, 1 - c), device_id_type=MESH)
            for w in range(len(names))]

    def start(ins, outs, sems):
        for cp in copies(ins, outs, sems):
            cp.start()

    def finish(ins, outs, sems):
        for cp in copies(ins, outs, sems):
            cp.wait()

    return CommJob([grads[n] for n in names], [SDS((4, s[1] // 2, s[2]), F32) for s in shapes], False, start, finish,
                   [pltpu.SemaphoreType.DMA((len(names),))] * 2)


def chip_exchange_job(p16, names):
    shapes = [p16[n].shape for n in names]

    def copies(ins, outs, sems):
        x, y, c, chips = _place()
        return [pltpu.make_async_remote_copy(
            src_ref=ins[w].at[2 * chip[0] + chip[1]], dst_ref=outs[w].at[j],
            send_sem=sems[0].at[w, j], recv_sem=sems[1].at[w, j], device_id=(*chip, c), device_id_type=MESH)
            for w in range(len(names)) for j, chip in enumerate(chips)]

    def start(ins, outs, sems):
        for cp in copies(ins, outs, sems):
            cp.start()

    def finish(ins, outs, sems):
        for cp in copies(ins, outs, sems):
            cp.wait()

    return CommJob([p16[n] for n in names], [SDS((3,) + tuple(s[1:]), BF16) for s in shapes], False, start, finish,
                   [pltpu.SemaphoreType.DMA((len(names), 3))] * 2)


def half_exchange_job(full, names):
    shapes = [full[n].shape for n in names]

    def copies(outs, sems):
        x, y, c, _ = _place()
        cps = []
        for w in range(len(names)):
            Rh = shapes[w][0] // 2
            rows = outs[w].at[pl.ds(c * Rh, Rh), :]
            other = outs[w].at[pl.ds((1 - c) * Rh, Rh), :]
            cps.append((pltpu.make_async_remote_copy(src_ref=rows, dst_ref=rows, send_sem=sems[0].at[w], recv_sem=sems[1].at[w],
                                                     device_id=(x, y, 1 - c), device_id_type=MESH),
                        pltpu.make_async_remote_copy(src_ref=other, dst_ref=other, send_sem=sems[0].at[w], recv_sem=sems[1].at[w],
                                                     device_id=(x, y, 1 - c), device_id_type=MESH)))
        return cps

    def start(ins, outs, sems):
        for send, _ in copies(outs, sems):
            send.start()

    def finish(ins, outs, sems):
        for send, recv in copies(outs, sems):
            recv.wait_recv()
            send.wait_send()

    arrays = [full[n] for n in names]
    return CommJob(arrays, [SDS(a.shape, a.dtype) for a in arrays], True, start, finish,
                   [pltpu.SemaphoreType.DMA((len(names),))] * 2)


def allreduce_small(parts, name, job=None):
    n = len(parts)
    nji = len(job.ins) if job else 0
    njo = len(job.out_shapes) if job else 0

    def body(*refs):
        ins, jins = refs[:n], refs[n:n + nji]
        outs, jouts = refs[n + nji:2 * n + nji], refs[2 * n + nji:2 * n + nji + njo]
        bufs = refs[2 * n + nji + njo:3 * n + nji + njo]
        send, recv = refs[3 * n + nji + njo:3 * n + nji + njo + 2]
        jsems = refs[3 * n + nji + njo + 2:]
        x, y, c, _ = _place()
        me = 4 * x + 2 * y + c
        if job is not None:
            job.run(0, jins, jouts, jsems)
        sends = []
        for p in range(n):
            for k in range(1, 8):
                peer = (x ^ (k >> 2), y ^ ((k >> 1) & 1), c ^ (k & 1))
                cp = pltpu.make_async_remote_copy(src_ref=ins[p], dst_ref=bufs[p].at[k], send_sem=send.at[p, k - 1],
                                                  recv_sem=recv.at[p, k - 1], device_id=peer, device_id_type=MESH)
                cp.start()
                sends.append(cp)
            bufs[p][0] = ins[p][...]
        for cp in sends:
            cp.wait_recv()
        for p in range(n):
            acc = bufs[p][me]
            for d in range(1, 8):
                acc = acc + bufs[p][d ^ me]
            outs[p][...] = acc
        for cp in sends:
            cp.wait_send()
        if job is not None:
            job.run(1, jins, jouts, jsems)
            job.run(2, jins, jouts, jsems)

    vm = pl.BlockSpec(memory_space=pltpu.VMEM)
    res = pl.pallas_call(
        body, name=name, in_specs=[vm] * n + [ANY] * nji, out_specs=[vm] * n + [ANY] * njo,
        out_shape=[SDS(a.shape, F32) for a in parts] + (list(job.out_shapes) if job else []),
        input_output_aliases={n + i: n + i for i in range(nji)} if (job and job.inplace) else {},
        scratch_shapes=[pltpu.VMEM((8,) + tuple(a.shape), F32) for a in parts] + [pltpu.SemaphoreType.DMA((n, 7))] * 2
        + (list(job.sems) if job else []),
    )(*parts, *(job.ins if job else []))
    if job is not None:
        job.landed = list(res[n:])
    return list(res[:n])


def adamw_multi(ws, gs, ms, vs, name):
    n = len(ws)

    def body(*refs):
        for p in range(n):
            _adam_update(*(refs[q * n + p] for q in range(7)))

    vm = pl.BlockSpec(memory_space=pltpu.VMEM)
    outs = pl.pallas_call(body, name=name, in_specs=[vm] * (4 * n), out_specs=[vm] * (3 * n),
                          out_shape=[SDS(a.shape, F32) for a in ws] * 3)(*ws, *gs, *ms, *vs)
    return outs[:n], outs[n:2 * n], outs[2 * n:]


def pair_sum(gs, xhs, sel_arr, name):
    n = len(gs)
    _, R, C = gs[0].shape
    Rh = R // 2

    def body(sel_ref, *refs):
        for p in range(n):
            g_ref, x_ref, p32_ref, p16_ref = (refs[q * n + p] for q in range(4))
            s = g_ref[...] + x_ref[...]
            p16_ref[...] = s.astype(BF16)

            @pl.when(pl.program_id(0) == sel_ref[0])
            def _():
                p32_ref[...] = s

    same = pl.BlockSpec((None, Rh, C), lambda s, sel: (s, 0, 0))
    grid_spec = pltpu.PrefetchScalarGridSpec(
        num_scalar_prefetch=1, grid=(4,),
        in_specs=[pl.BlockSpec((None, Rh, C), lambda s, sel: (s, sel[1], 0))] * n + [same] * n,
        out_specs=[pl.BlockSpec((Rh, C), lambda s, sel: (0, 0))] * n + [same] * n)
    outs = pl.pallas_call(body, grid_spec=grid_spec, name=name,
                          out_shape=[SDS((Rh, C), F32)] * n + [SDS((4, Rh, C), BF16)] * n,
                          compiler_params=_cp("arbitrary"))(sel_arr, *gs, *xhs)
    return outs[:n], outs[n:]


def chip_sum(p32s, ys, sel_arr, name):
    n = len(p32s)
    Rh, C = p32s[0].shape

    def body(s_ref, *refs):
        for p in range(n):
            p_ref, y_ref, o_ref = (refs[q * n + p] for q in range(3))
            o_ref[...] = ((p_ref[...] + y_ref[0].astype(F32)) + y_ref[1].astype(F32)) + y_ref[2].astype(F32)

    grid_spec = pltpu.PrefetchScalarGridSpec(
        num_scalar_prefetch=1, grid=(1,),
        in_specs=[pl.BlockSpec((Rh, C), lambda i, s: (0, 0))] * n + [pl.BlockSpec((3, Rh, C), lambda i, s: (0, 0, 0))] * n,
        out_specs=[pl.BlockSpec((Rh, C), lambda i, s: (s[1], 0))] * n)
    return pl.pallas_call(body, grid_spec=grid_spec, name=name, out_shape=[SDS((2 * Rh, C), F32)] * n,
                          compiler_params=_cp("arbitrary"))(sel_arr, *p32s, *ys)


class StepComm:
    FIRST = ("w_in",)
    SQUARE = ("w_out", "w_xq", "w_xk", "w_xv", "w_xo")
    FFN = ("w_up", "w_down")
    PLAN = {
        "proj": ("gather", SQUARE[:3]), "attn_fwd": ("gather", SQUARE[3:] + FFN),
        "d_h3": ("pair", FFN), "mixer_post_bwd_a": ("pair", SQUARE), "attn_bwd": ("chip", FFN + SQUARE),
        "dw_in": ("half", FFN + SQUARE), "first_update": ("pair", FIRST), "d_h1": ("chip", FIRST),
    }

    def __init__(self, placed, conv_placed, sel_arr):
        self.placed, self.conv_placed, self.sel_arr = placed, conv_placed, sel_arr
        self.g, self.p32, self.p16, self.full, self.done, self.jobs = {}, {}, {}, {}, {}, {}

    def job(self, tag, gb=None):
        if tag == "norm_mix":
            self.jobs[tag] = merge_jobs(gather_job(self.placed, self.FIRST), gather_rows_job(self.conv_placed))
            return self.jobs[tag]
        kind, names = self.PLAN[tag]
        if kind == "gather":
            j = gather_job(self.placed, names)
        elif kind == "pair":
            for n in names:
                self.g[n] = gb[n] if gb[n].ndim == 3 else gb[n].reshape(4, gb[n].shape[0] // 4, gb[n].shape[1])
            j = pair_exchange_job(self.g, names)
        elif kind == "chip":
            j = chip_exchange_job(self.p16, names)
        else:
            j = half_exchange_job(self.full, names)
        self.jobs[tag] = j
        return j

    def landed(self, tag, wb=None, ws=None):
        if tag == "norm_mix":
            split_landed(self.jobs[tag])
            first, conv, _ = self.jobs[tag].parts
            wb.update(zip(self.FIRST, first.landed))
            ws.update((n, a[:3]) for n, a in zip(SMALL_SHARDED, conv.landed))
            return
        kind, names = self.PLAN[tag]
        got = dict(zip(names, self.jobs[tag].landed))
        if kind == "gather":
            wb.update(got)
        elif kind == "half":
            self.done.update(got)
        for group in same_shape_groups(names, got):
            if kind == "pair":
                p32s, p16s = pair_sum([self.g[n] for n in group], [got[n] for n in group], self.sel_arr, "pair_sum_" + group[0])
                self.p32.update(zip(group, p32s))
                self.p16.update(zip(group, p16s))
            elif kind == "chip":
                fulls = chip_sum([self.p32[n] for n in group], [got[n] for n in group], self.sel_arr, "chip_sum_" + group[0])
                self.full.update(zip(group, fulls))

    def first_update(self, gb):
        self.first_result = self.update_fn(self.job("first_update", gb))
        self.landed("first_update")

    def finish(self, gb):
        return self.done

    def last_job(self):
        self.jobs["end_half"] = half_exchange_job(self.full, self.FIRST)
        return self.jobs["end_half"]

    def take_last(self):
        self.done.update(zip(self.FIRST, self.jobs["end_half"].landed))


class NoComm:
    def job(self, tag, gb=None):
        return None

    def landed(self, tag, wb=None, ws=None):
        pass

    def first_update(self, gb):
        pass

    def finish(self, gb):
        return gb


SMALL = ("rel_bias", "g_mix", "w_short_conv", "g_attn_out", "g_conv_out", "g_xattn", "g_mem", "g_ffn",
         "w_ffn_conv", "b_ffn_conv", "g_final")
SMALL_SHARDED = ("w_short_conv", "w_ffn_conv")


def kernel(x, mem, rel_bias, g_mix, w_in, w_short_conv, g_attn_out, g_conv_out, w_out, g_xattn, g_mem, w_xq, w_xk, w_xv, w_xo, g_ffn, w_up, w_ffn_conv, b_ffn_conv, w_down, g_final, loss_target, m_rel_bias, m_g_mix, m_w_in, m_w_short_conv, m_g_attn_out, m_g_conv_out, m_w_out, m_g_xattn, m_g_mem, m_w_xq, m_w_xk, m_w_xv, m_w_xo, m_g_ffn, m_w_up, m_w_ffn_conv, m_b_ffn_conv, m_w_down, m_g_final, v_rel_bias, v_g_mix, v_w_in, v_w_short_conv, v_g_attn_out, v_g_conv_out, v_w_out, v_g_xattn, v_g_mem, v_w_xq, v_w_xk, v_w_xv, v_w_xo, v_g_ffn, v_w_up, v_w_ffn_conv, v_b_ffn_conv, v_w_down, v_g_final):
    names = ("rel_bias", "g_mix", "w_in", "w_short_conv", "g_attn_out", "g_conv_out", "w_out", "g_xattn", "g_mem", "w_xq",
             "w_xk", "w_xv", "w_xo", "g_ffn", "w_up", "w_ffn_conv", "b_ffn_conv", "w_down", "g_final")
    W = dict(zip(names, (rel_bias, g_mix, w_in, w_short_conv, g_attn_out, g_conv_out, w_out, g_xattn, g_mem, w_xq, w_xk, w_xv,
                         w_xo, g_ffn, w_up, w_ffn_conv, b_ffn_conv, w_down, g_final)))
    M = dict(zip(names, (m_rel_bias, m_g_mix, m_w_in, m_w_short_conv, m_g_attn_out, m_g_conv_out, m_w_out, m_g_xattn, m_g_mem,
                         m_w_xq, m_w_xk, m_w_xv, m_w_xo, m_g_ffn, m_w_up, m_w_ffn_conv, m_b_ffn_conv, m_w_down, m_g_final)))
    V = dict(zip(names, (v_rel_bias, v_g_mix, v_w_in, v_w_short_conv, v_g_attn_out, v_g_conv_out, v_w_out, v_g_xattn, v_g_mem,
                         v_w_xq, v_w_xk, v_w_xv, v_w_xo, v_g_ffn, v_w_up, v_w_ffn_conv, v_b_ffn_conv, v_w_down, v_g_final)))
    xi, yi, ci = lax.axis_index("x"), lax.axis_index("y"), lax.axis_index("c")
    mine = 2 * xi + yi
    mine_arr = jnp.reshape(mine, (1,)).astype(jnp.int32)
    sel_arr = jnp.stack([mine, ci]).astype(jnp.int32)
    conv_placed = []
    for n in SMALL_SHARDED:
        shard = W[n][0]
        conv_placed.append(lax.dynamic_update_slice(jnp.zeros((8, 4 * shard.shape[1]), F32), shard, (0, mine * shard.shape[1])))
    shards = {n: W[n][0] for n in BIG}
    placed = {}
    for group in same_shape_groups(BIG, shards):
        outs = place_shards([shards[n] for n in group], mine_arr, group[0] in COL_SHARDED, "place_" + group[0])
        placed.update(zip(group, outs))
    comm = StepComm(placed, conv_placed, sel_arr)
    ws = {n: W[n] for n in SMALL if n not in SMALL_SHARDED}

    def update(group, job=None):
        return adamw([W[n][0] for n in group], [comm.done[n] for n in group], [M[n][0] for n in group],
                     [V[n][0] for n in group], "adamw_" + group[0], job=job)

    comm.update_fn = lambda job: update(["w_up"], job)
    loss, grad_x, gfull, gs = local_step(x[0], mem[0], loss_target[0], None, ws, comm)

    def as2d(a):
        return a.reshape(1, -1) if a.ndim == 1 else a

    reduced = allreduce_small([as2d(gs[n]) for n in SMALL] + [loss], "reduce_small", job=comm.last_job())
    comm.take_last()
    gsm = dict(zip(SMALL, reduced[:-1]))
    loss = reduced[-1][0, 0]

    grads, delta, new_m, new_v = {}, {}, {}, {}
    for group in same_shape_groups(BIG, gfull):
        ds, nms, nvs = comm.first_result if group == ["w_up"] else update(group)
        for n, d, nm, nv in zip(group, ds, nms, nvs):
            grads[n], delta[n], new_m[n], new_v[n] = gfull[n][None], d[None], nm[None], nv[None]
    for n in SMALL_SHARDED:
        wid = W[n].shape[2]
        gsm[n] = lax.dynamic_slice(gsm[n], (0, mine * wid), (3, wid))

    def own(a, n):
        return a[0] if n in SMALL_SHARDED else as2d(a)

    d, nm, nv = adamw_multi([own(W[n], n) for n in SMALL], [gsm[n] for n in SMALL], [own(M[n], n) for n in SMALL],
                            [own(V[n], n) for n in SMALL], "adamw_small")
    for i, n in enumerate(SMALL):
        shape = W[n].shape
        grads[n], delta[n], new_m[n], new_v[n] = (a.reshape(shape) for a in (gsm[n], d[i], nm[i], nv[i]))
    return (loss, grad_x[None], *[grads[n] for n in names], *[delta[n] for n in names],
            *[new_m[n] for n in names], *[new_v[n] for n in names])
```

```python
import math

import numpy as np
import jax
import jax.numpy as jnp
from jax import lax
from jax.experimental import pallas as pl
from jax.experimental.pallas import tpu as pltpu

F32 = jnp.float32
BF16 = jnp.bfloat16
SDS = jax.ShapeDtypeStruct
MESH = pl.DeviceIdType.MESH

D_MODEL = 1024
ATTN_W = 512
N_HEADS = 8
HEAD_DIM = 64
HALF_HEAD = HEAD_DIM // 2
Q_SCALE = HEAD_DIM ** -0.5
BLK = 128
PATTERNS = ((128, 1), (512, 4), (2048, 16))
N_BUCKETS = 32
BUCKET_MAX_DIST = 2048
D_FF = 2816
N_MEM_HEADS = 4
MEM_HD = 256
EPS = 1e-6
NEG = -1e30
VMEM_LIMIT = 56 * 1024 * 1024

ADAM_LR, ADAM_B1, ADAM_B2, ADAM_EPS, ADAM_WD, ADAM_STEP = 0.001, 0.9, 0.999, 1e-08, 0.01, 10


def _cp(*sem):
    return pltpu.CompilerParams(dimension_semantics=sem, vmem_limit_bytes=VMEM_LIMIT)


def _dot(a, b):
    return jnp.dot(a, b, preferred_element_type=F32)


def _dot_nt(a, b):
    return lax.dot_general(a, b, (((1,), (1,)), ((), ())), preferred_element_type=F32)


def _dot_tn(a, b):
    return lax.dot_general(a, b, (((0,), (0,)), ((), ())), preferred_element_type=F32)


def _rsum(x):
    return jnp.sum(x, axis=1, keepdims=True)


def rmsnorm_fwd(x, g, name, job=None):
    S, Dm = x.shape
    tm = min(S, 512)

    def body(x_ref, g_ref, o_ref):
        xv = x_ref[...]
        r = lax.rsqrt(jnp.mean(xv * xv, axis=1, keepdims=True) + EPS)
        o_ref[...] = (xv * r * g_ref[...]).astype(o_ref.dtype)

    return _pcall(body, [x, g], grid=(S // tm,), name=name,
                  in_specs=[pl.BlockSpec((tm, Dm), lambda i: (i, 0)), pl.BlockSpec((1, Dm), lambda i: (0, 0))],
                  out_specs=[pl.BlockSpec((tm, Dm), lambda i: (i, 0))], out_shape=[SDS((S, Dm), BF16)],
                  sem=("parallel",), job=job)[0]


def rmsnorm_gain_grad(x, dh, name):
    S, Dm = x.shape
    tm = min(S, 512)

    def body(x_ref, dh_ref, dg_ref):
        xv = x_ref[...]
        xh = xv * lax.rsqrt(jnp.mean(xv * xv, axis=1, keepdims=True) + EPS)

        @pl.when(pl.program_id(0) == 0)
        def _():
            dg_ref[...] = jnp.zeros_like(dg_ref)

        dg_ref[...] += jnp.sum(dh_ref[...].astype(F32) * xh, axis=0, keepdims=True)

    row = pl.BlockSpec((tm, Dm), lambda i: (i, 0))
    return pl.pallas_call(
        body, grid=(S // tm,), name=name, in_specs=[row, row], out_specs=pl.BlockSpec((1, Dm), lambda i: (0, 0)),
        out_shape=SDS((1, Dm), F32), compiler_params=_cp("arbitrary"))(x, dh)


def _col_spec(bw, tn, rows, row_of, col_of):
    if bw is None:
        return pl.BlockSpec((rows, tn), lambda *g: (row_of(*g), col_of(*g)))
    if tn % bw == 0:
        return pl.BlockSpec((tn // bw, rows, bw), lambda *g: (col_of(*g), row_of(*g), 0))
    per = bw // tn
    return pl.BlockSpec((None, rows, tn), lambda *g: (col_of(*g) // per, row_of(*g), col_of(*g) % per))


def _read_cols(ref, bw, tn):
    if bw is None or tn % bw != 0:
        return ref[...]
    if tn == bw:
        return ref[0]
    return jnp.concatenate([ref[q] for q in range(tn // bw)], axis=1)


def _write_cols(ref, bw, tn, val):
    if bw is None or tn % bw != 0:
        ref[...] = val.astype(ref.dtype)
    else:
        for q in range(tn // bw):
            ref[q] = val[:, q * bw:(q + 1) * bw].astype(ref.dtype)


def mm_nn(a, b, name, *, out_dtype=F32, out_bw=None, tm=1024, tn=512, job=None):
    M, K = a.shape
    N = b.shape[1]
    tm = min(tm, M)

    def body(a_ref, b_ref, o_ref):
        _write_cols(o_ref, out_bw, tn, _dot(a_ref[...].astype(BF16), b_ref[...]))

    ri, ci = (lambda i, j: i), (lambda i, j: j)
    in_specs = [pl.BlockSpec((tm, K), lambda i, j: (i, 0)), pl.BlockSpec((K, tn), lambda i, j: (0, j))]
    oshape = (M, N) if out_bw is None else (N // out_bw, M, out_bw)
    return _pcall(body, [a, b], grid=(M // tm, N // tn), name=name, in_specs=in_specs,
                  out_specs=[_col_spec(out_bw, tn, tm, ri, ci)], out_shape=[SDS(oshape, out_dtype)],
                  sem=("parallel", "parallel"), job=job)[0]


def mm_nn_rows(a, b, res, g, name, *, target=None, tm=512):
    M, K = a.shape
    N = b.shape[1]

    def body(*refs):
        a_ref, b_ref, r_ref, g_ref = refs[:4]
        xv = r_ref[...] + _dot(a_ref[...].astype(BF16), b_ref[...])
        r = lax.rsqrt(jnp.mean(xv * xv, axis=1, keepdims=True) + EPS)
        xh = xv * r
        if target is None:
            x_ref, h_ref = refs[4:]
            x_ref[...] = xv
            h_ref[...] = (xh * g_ref[...]).astype(BF16)
            return
        t_ref, loss_ref, dx_ref, dg_ref = refs[4:]
        err = xh * g_ref[...] - t_ref[...]
        dy = err / N
        gd = dy * g_ref[...]
        dx_ref[...] = r * (gd - xh * jnp.mean(gd * xh, axis=1, keepdims=True))

        @pl.when(pl.program_id(0) == 0)
        def _():
            dg_ref[...] = jnp.zeros_like(dg_ref)
            loss_ref[...] = jnp.zeros_like(loss_ref)

        dg_ref[...] += _colsum(dy * xh)
        loss_ref[...] += 0.5 * _colsum(jnp.mean(err * err, axis=1, keepdims=True))

    row = pl.BlockSpec((tm, N), lambda i: (i, 0))
    vec = pl.BlockSpec((1, N), lambda i: (0, 0))
    in_specs = [pl.BlockSpec((tm, K), lambda i: (i, 0)), pl.BlockSpec((K, N), lambda i: (0, 0)), row, vec]
    if target is None:
        return pl.pallas_call(body, grid=(M // tm,), name=name, in_specs=in_specs, out_specs=[row, row],
                              out_shape=[SDS((M, N), F32), SDS((M, N), BF16)], compiler_params=_cp("parallel"))(a, b, res, g)
    one = pl.BlockSpec((1, 1), lambda i: (0, 0))
    return pl.pallas_call(body, grid=(M // tm,), name=name, in_specs=in_specs + [row], out_specs=[one, row, vec],
                          out_shape=[SDS((1, 1), F32), SDS((M, N), F32), SDS((1, N), F32)],
                          compiler_params=_cp("arbitrary"))(a, b, res, g, target)


def mm_nt(a, a_bw, b, name, *, res=None, out_dtype=F32, tm=1024, tn=512, job=None):
    if a_bw is None:
        M, K = a.shape
    else:
        M, K = a.shape[1], a.shape[0] * a_bw
    N = b.shape[0]
    tm = min(tm, M)

    def body(*refs):
        if res is None:
            a_ref, b_ref, o_ref = refs
        else:
            a_ref, b_ref, r_ref, o_ref = refs
        av = _read_cols(a_ref, a_bw, K).astype(BF16)
        acc = _dot_nt(av, b_ref[...])
        if res is not None:
            acc = acc + r_ref[...]
        o_ref[...] = acc.astype(o_ref.dtype)

    in_specs = [_col_spec(a_bw, K, tm, lambda i, j: i, lambda i, j: 0), pl.BlockSpec((tn, K), lambda i, j: (j, 0))]
    args = [a, b]
    if res is not None:
        in_specs.append(pl.BlockSpec((tm, tn), lambda i, j: (i, j)))
        args.append(res)
    return _pcall(body, args, grid=(M // tm, N // tn), name=name, in_specs=in_specs,
                  out_specs=[pl.BlockSpec((tm, tn), lambda i, j: (i, j))], out_shape=[SDS((M, N), out_dtype)],
                  sem=("parallel", "parallel"), job=job)[0]


def mm_nt_norm_bwd(a, a_bw, b, x, g, res, name, *, tm=512, job=None):
    if a_bw is None:
        M, K = a.shape
    else:
        M, K = a.shape[1], a.shape[0] * a_bw
    N = b.shape[0]

    def body(a_ref, b_ref, x_ref, g_ref, r_ref, dx_ref, dg_ref):
        dh = _dot_nt(_read_cols(a_ref, a_bw, K).astype(BF16), b_ref[...])
        xv = x_ref[...]
        r = lax.rsqrt(jnp.mean(xv * xv, axis=1, keepdims=True) + EPS)
        xh = xv * r
        gd = dh * g_ref[...]
        dx_ref[...] = r_ref[...] + r * (gd - xh * jnp.mean(gd * xh, axis=1, keepdims=True))

        @pl.when(pl.program_id(0) == 0)
        def _():
            dg_ref[...] = jnp.zeros_like(dg_ref)

        dg_ref[...] += jnp.sum(dh * xh, axis=0, keepdims=True)

    row = pl.BlockSpec((tm, N), lambda i: (i, 0))
    vec = pl.BlockSpec((1, N), lambda i: (0, 0))
    return _pcall(body, [a, b, x, g, res], grid=(M // tm,), name=name,
                  in_specs=[_col_spec(a_bw, K, tm, lambda i: i, lambda i: 0), pl.BlockSpec((N, K), lambda i: (0, 0)), row, vec, row],
                  out_specs=[row, vec], out_shape=[SDS((M, N), F32), SDS((1, N), F32)], sem=("arbitrary",), job=job)


def mm_tn(a, b, b_bw, name, *, shards=None, tm=1024, tn=1024, ts=1024, job=None):
    S, Ka = a.shape
    N = b.shape[1] if b_bw is None else b.shape[0] * b_bw
    ts = min(ts, S)
    tm = min(tm, Ka)

    def body(a_ref, b_ref, o_ref):
        @pl.when(pl.program_id(2) == 0)
        def _():
            o_ref[...] = jnp.zeros_like(o_ref)

        bv = _read_cols(b_ref, b_bw, tn).astype(BF16)
        o_ref[...] += _dot_tn(a_ref[...].astype(BF16), bv)

    in_specs = [pl.BlockSpec((ts, tm), lambda i, j, k: (k, i)),
                _col_spec(b_bw, tn, ts, lambda i, j, k: k, lambda i, j, k: j)]
    if shards is None:
        out_spec, oshape = pl.BlockSpec((tm, tn), lambda i, j, k: (i, j)), (Ka, N)
    else:
        per = (N // shards) // tn
        out_spec = pl.BlockSpec((None, tm, tn), lambda i, j, k: (j // per, i, j % per))
        oshape = (shards, Ka, N // shards)
    return _pcall(body, [a, b], grid=(Ka // tm, N // tn, S // ts), name=name, in_specs=in_specs, out_specs=[out_spec],
                  out_shape=[SDS(oshape, F32)], sem=("parallel", "parallel", "arbitrary"), job=job)[0]


KEYS = 2 * BLK


def _bucket_tables():
    out = np.zeros((3, BLK, KEYS), np.int32)
    qi = np.arange(BLK)[:, None]
    kj = np.arange(KEYS)[None, :]
    for p, (win, dil) in enumerate(PATTERNS):
        w = win // dil
        assert w == BLK
        steps = qi + w - kj
        valid = (steps >= 0) & (steps <= w)
        dist = np.clip(steps, 0, w) * dil
        dd = np.maximum(dist, 1).astype(np.float32)
        exact = N_BUCKETS // 2
        large = exact + (np.log(dd / np.float32(exact)) / np.float32(math.log(BUCKET_MAX_DIST / exact))
                         * np.float32(N_BUCKETS - exact)).astype(np.int32)
        large = np.minimum(large, N_BUCKETS - 1)
        out[p] = np.where(valid, np.where(dist < exact, dist, large), -1)
    return out


def bias_tables(rel_bias):
    bkt = jnp.asarray(_bucket_tables())

    def body(rb_ref, bkt_ref, o_ref):
        bk = bkt_ref[...]
        for h in range(N_HEADS):
            acc = jnp.full((BLK, KEYS), NEG, F32)
            for b in range(N_BUCKETS):
                acc = jnp.where(bk == b, rb_ref[h, b], acc)
            o_ref[h] = acc

    return pl.pallas_call(
        body, grid=(3,), name="bias_tables",
        in_specs=[pl.BlockSpec(memory_space=pltpu.SMEM),
                  pl.BlockSpec((None, BLK, KEYS), lambda p: (p, 0, 0))],
        out_specs=pl.BlockSpec((None, N_HEADS, BLK, KEYS), lambda p: (p, 0, 0, 0)),
        out_shape=SDS((3, N_HEADS, BLK, KEYS), F32), compiler_params=_cp("parallel"))(rel_bias, bkt)


def bias_tables_bwd(ds_sum):
    bkt = jnp.asarray(_bucket_tables())

    def body(ds_ref, bkt_ref, o_ref):
        h = pl.program_id(0)

        @pl.when(h == 0)
        def _():
            o_ref[...] = jnp.zeros_like(o_ref)

        rows = lax.broadcasted_iota(jnp.int32, (N_HEADS, N_BUCKETS), 0)
        cols = lax.broadcasted_iota(jnp.int32, (N_HEADS, N_BUCKETS), 1)
        acc = jnp.zeros((N_HEADS, N_BUCKETS), F32)
        for b in range(N_BUCKETS):
            t = jnp.zeros((BLK, KEYS), F32)
            for p in range(3):
                t = t + jnp.where(bkt_ref[p] == b, ds_ref[p], 0.0)
            tot = jnp.sum(_rsum(t), axis=0, keepdims=True)
            acc = acc + jnp.where((rows == h) & (cols == b), tot, 0.0)
        o_ref[...] += acc

    return pl.pallas_call(
        body, grid=(N_HEADS,), name="bias_tables_bwd",
        in_specs=[pl.BlockSpec((3, None, BLK, KEYS), lambda h: (0, h, 0, 0)),
                  pl.BlockSpec((3, BLK, KEYS), lambda h: (0, 0, 0))],
        out_specs=pl.BlockSpec((N_HEADS, N_BUCKETS), lambda h: (0, 0)),
        out_shape=SDS((N_HEADS, N_BUCKETS), F32), compiler_params=_cp("arbitrary"))(ds_sum, bkt)


def _rows(start, dil):
    if dil == 1:
        return pl.ds(pl.multiple_of(start, BLK), BLK)
    return pl.ds(start, BLK, stride=dil)


GRP = 8


def _group_rows(i, dil, S):
    nb = S // (BLK * dil)
    run = min(nb, GRP)
    chunks = nb // run
    res0, b0 = (i // chunks) * (GRP // run), (i % chunks) * run
    cur = [(b0 + j % run) * (BLK * dil) + res0 + j // run for j in range(GRP)]
    t = lax.broadcasted_iota(jnp.int32, (GRP, 1, 1), 0)
    if run == nb:
        before, pen = None, jnp.where(t % run == 0, NEG, 0.0)
    else:
        before = _rows(jnp.maximum(b0 - 1, 0) * (BLK * dil) + res0, dil)
        pen = jnp.where((t == 0) & (b0 == 0), NEG, 0.0)
    return [_rows(s, dil) for s in cur], before, pen


def _load_group(ref, rows):
    return jnp.stack([ref[r, :] for r in rows])


def _with_prev(ref, before, cur_blocks):
    first = cur_blocks[:1] if before is None else ref[before, :][None].astype(cur_blocks.dtype)
    return jnp.concatenate([first, cur_blocks[:-1]], axis=0)


def _bdot_nt(a, b):
    return lax.dot_general(a, b, (((2,), (2,)), ((0,), (0,))), preferred_element_type=F32)


def _bdot(a, b):
    return lax.dot_general(a, b, (((2,), (1,)), ((0,), (0,))), preferred_element_type=F32)


def _bdot_tn(a, b):
    return lax.dot_general(a, b, (((1,), (1,)), ((0,), (0,))), preferred_element_type=F32)


def _lsum(x):
    return jnp.sum(x, axis=-1, keepdims=True)


def _widen(src, dst):
    S = src.shape[1]

    def chunk(i, carry):
        rows = pl.ds(pl.multiple_of(i * 512, 512), 512)
        for a in range(3):
            dst[a, rows, :] = src[a, rows, :].astype(F32)
        return carry

    lax.fori_loop(0, S // 512, chunk, 0)


def attn_fwd(qkv, bias, job=None):
    S = qkv.shape[3]
    nblk = S // BLK
    car = Carried(job, 2, 2, 3, 4)

    def body(*refs):
        (qkv_in, bias_ref, o_ref, lse_ref, qkv_ref, part_o, part_ml), jrefs = car.split(refs)
        car.phase(0, pl.program_id(0), jrefs)
        _widen(qkv_in, qkv_ref)
        lane = lax.broadcasted_iota(jnp.int32, (GRP, BLK, BLK), 2)
        lo = lane < HEAD_DIM
        masks = (lo, jnp.logical_not(lo))
        lov = lax.broadcasted_iota(jnp.int32, (GRP, KEYS, BLK), 2) < HEAD_DIM
        vmasks = (lov, jnp.logical_not(lov))
        prev_keys = lax.broadcasted_iota(jnp.int32, (1, 1, KEYS), 2) < BLK
        q_ref, k_ref, v_ref = qkv_ref.at[0], qkv_ref.at[1], qkv_ref.at[2]
        targets = [(o_ref, lse_ref)] + [(part_o.at[t], part_ml.at[t]) for t in range(len(PATTERNS) - 1)]
        for p, (_, dil) in enumerate(PATTERNS):
            def step(i, carry, p=p, dil=dil):
                rc, before, pen = _group_rows(i, dil, S)
                q2 = _load_group(q_ref, rc) * Q_SCALE
                kc = _load_group(k_ref, rc).astype(BF16)
                keys = jnp.concatenate([_with_prev(k_ref, before, kc), kc], axis=1)
                vc = _load_group(v_ref, rc)
                vals = jnp.concatenate([_with_prev(v_ref, before, vc), vc], axis=1)
                pen = jnp.where(prev_keys, pen, 0.0)
                pv = jnp.zeros((GRP, BLK, BLK), F32)
                ms, ls = [], []
                for h in range(2):
                    qh = jnp.where(masks[h], q2, 0.0).astype(BF16)
                    s = _bdot_nt(qh, keys) + bias_ref[p, h][None] + pen
                    mn = jnp.max(s, axis=-1, keepdims=True)
                    pr = jnp.exp(s - mn)
                    pv = pv + _bdot(pr.astype(BF16), jnp.where(vmasks[h], vals, 0.0).astype(BF16))
                    ms.append(mn)
                    ls.append(_lsum(pr))
                ml_t = jnp.where(lo, jnp.where(lane < HALF_HEAD, ms[0], ls[0]),
                                 jnp.where(lane < HEAD_DIM + HALF_HEAD, ms[1], ls[1]))
                acc_ref, ml_ref = targets[p]
                for j, r in enumerate(rc):
                    acc_ref[r, :] = pv[j]
                    ml_ref[r, :] = ml_t[j]
                return carry

            lax.fori_loop(0, nblk // GRP, step, 0)

        def merge(i, carry):
            rows = pl.ds(pl.multiple_of(i * 512, 512), 512)
            is_m = (lax.broadcasted_iota(jnp.int32, (512, BLK), 1) & HALF_HEAD) == 0
            accs = [ref[rows, :] for ref, _ in targets]
            mls = [ref[rows, :] for _, ref in targets]
            mx = [jnp.where(is_m, ml, pltpu.roll(ml, HALF_HEAD, 1)) for ml in mls]
            dn = [jnp.where(is_m, pltpu.roll(ml, BLK - HALF_HEAD, 1), ml) for ml in mls]
            top = jnp.maximum(jnp.maximum(mx[0], mx[1]), mx[2])
            ws_ = [jnp.exp(m - top) for m in mx]
            den = ws_[0] * dn[0] + ws_[1] * dn[1] + ws_[2] * dn[2]
            o_ref[rows, :] = (ws_[0] * accs[0] + ws_[1] * accs[1] + ws_[2] * accs[2]) / den
            lse_ref[rows, :] = top + jnp.log(den)
            return carry

        lax.fori_loop(0, S // 512, merge, 0)
        car.phase(1, pl.program_id(0), jrefs)
        car.phase(2, pl.program_id(0), jrefs)

    outs = pl.pallas_call(
        body, grid=(4,), name="attn_fwd",
        in_specs=[pl.BlockSpec((None, 3, None, S, BLK), lambda g: (0, 0, g, 0, 0)),
                  pl.BlockSpec((3, 2, BLK, KEYS), lambda g: (0, g, 0, 0))] + car.in_specs(),
        out_specs=[pl.BlockSpec((None, S, BLK), lambda g: (g, 0, 0)),
                   pl.BlockSpec((None, S, BLK), lambda g: (g, 0, 0))] + car.out_specs(),
        out_shape=[SDS((4, S, BLK), F32), SDS((4, S, BLK), F32)] + car.out_shapes(),
        input_output_aliases=car.aliases(),
        scratch_shapes=[pltpu.VMEM((3, S, BLK), F32), pltpu.VMEM((2, S, BLK), F32), pltpu.VMEM((2, S, BLK), F32)] + car.sems(),
        compiler_params=_cp("arbitrary"))(qkv, bias, *car.args())
    if job is not None:
        job.landed = list(outs[2:])
    return outs[0], outs[1]


def attn_bwd(qkv, bias, o, lse, do, job=None):
    S = qkv.shape[3]
    nblk = S // BLK
    car = Carried(job, 5, 2, 4, 4)

    def body(*refs):
        (qkv_in, bias_ref, o_ref, lse_ref, do_ref, dqkv_out, ds_ref, qkv_ref, dqkv_ref, last_ref, st_ref), jrefs = car.split(refs)
        car.phase(0, pl.program_id(0), jrefs)
        _widen(qkv_in, qkv_ref)

        def stats(i, carry):
            rows = pl.ds(pl.multiple_of(i * 512, 512), 512)
            ln = lax.broadcasted_iota(jnp.int32, (512, BLK), 1)
            prod = do_ref[rows, :] * o_ref[rows, :]
            d0 = _rsum(jnp.where(ln < HEAD_DIM, prod, 0.0))
            d1 = _rsum(jnp.where(ln < HEAD_DIM, 0.0, prod))
            st_ref[rows, :] = jnp.where((ln & HALF_HEAD) == 0, lse_ref[rows, :], jnp.where(ln < HEAD_DIM, d0, d1))
            return carry

        lax.fori_loop(0, S // 512, stats, 0)
        lane = lax.broadcasted_iota(jnp.int32, (GRP, BLK, BLK), 2)
        lo = lane < HEAD_DIM
        masks = (lo, jnp.logical_not(lo))
        lov = lax.broadcasted_iota(jnp.int32, (GRP, KEYS, BLK), 2) < HEAD_DIM
        vmasks = (lov, jnp.logical_not(lov))
        prev_keys = lax.broadcasted_iota(jnp.int32, (1, 1, KEYS), 2) < BLK
        ds_ref[...] = jnp.zeros_like(ds_ref)
        q_ref, k_ref, v_ref = qkv_ref.at[0], qkv_ref.at[1], qkv_ref.at[2]
        last = len(PATTERNS) - 1
        for p, (_, dil) in enumerate(PATTERNS):
            def step(i, carry, p=p, dil=dil):
                tgt = last_ref if p == last else dqkv_ref
                dq_ref, dk_ref, dv_ref = tgt.at[0], tgt.at[1], tgt.at[2]
                rc, before, pen = _group_rows(i, dil, S)
                q2 = _load_group(q_ref, rc) * Q_SCALE
                kc = _load_group(k_ref, rc).astype(BF16)
                keys = jnp.concatenate([_with_prev(k_ref, before, kc), kc], axis=1)
                vc = _load_group(v_ref, rc).astype(BF16)
                vals = jnp.concatenate([_with_prev(v_ref, before, vc), vc], axis=1)
                pen = jnp.where(prev_keys, pen, 0.0)
                dot = _load_group(do_ref, rc)
                st = _load_group(st_ref, rc)
                dq = jnp.zeros((GRP, BLK, BLK), F32)
                dkeys = jnp.zeros((GRP, KEYS, BLK), F32)
                dvals = jnp.zeros((GRP, KEYS, BLK), F32)
                for h in range(2):
                    qh = jnp.where(masks[h], q2, 0.0).astype(BF16)
                    doh = jnp.where(masks[h], dot, 0.0).astype(BF16)
                    lh = st[:, :, HEAD_DIM * h:HEAD_DIM * h + 1]
                    delta = st[:, :, HEAD_DIM * h + HALF_HEAD:HEAD_DIM * h + HALF_HEAD + 1]
                    pr = jnp.exp(_bdot_nt(qh, keys) + bias_ref[p, h][None] + pen - lh)
                    ds = pr * (_bdot_nt(doh, vals) - delta)
                    ds_ref[p, h] += jnp.sum(ds, axis=0)
                    dsb = ds.astype(BF16)
                    dq = dq + jnp.where(masks[h], _bdot(dsb, keys), 0.0)
                    dkeys = dkeys + _bdot_tn(dsb, qh)
                    dvals = dvals + _bdot_tn(pr.astype(BF16), doh)
                dkp, dkc, dvp, dvc = dkeys[:, :BLK], dkeys[:, BLK:], dvals[:, :BLK], dvals[:, BLK:]
                none = jnp.zeros((1, BLK, BLK), F32)
                dkc = dkc + jnp.concatenate([dkp[1:], none], axis=0)
                dvc = dvc + jnp.concatenate([dvp[1:], none], axis=0)
                for j in range(GRP):
                    if p == 1:
                        dq_ref[rc[j], :] += dq[j] * Q_SCALE
                        dk_ref[rc[j], :] += dkc[j]
                        dv_ref[rc[j], :] += dvc[j]
                    else:
                        dq_ref[rc[j], :] = dq[j] * Q_SCALE
                        dk_ref[rc[j], :] = dkc[j]
                        dv_ref[rc[j], :] = dvc[j]
                if before is not None:
                    dk_ref[before, :] += dkp[0]
                    dv_ref[before, :] += dvp[0]
                return carry

            lax.fori_loop(0, nblk // GRP, step, 0)

        def narrow(i, carry):
            rows = pl.ds(pl.multiple_of(i * 512, 512), 512)
            for a in range(3):
                dqkv_out[a, rows, :] = (dqkv_ref[a, rows, :] + last_ref[a, rows, :]).astype(BF16)
            return carry

        lax.fori_loop(0, S // 512, narrow, 0)
        car.phase(1, pl.program_id(0), jrefs)
        car.phase(2, pl.program_id(0), jrefs)

    blk = pl.BlockSpec((None, S, BLK), lambda g: (g, 0, 0))
    outs = pl.pallas_call(
        body, grid=(4,), name="attn_bwd",
        in_specs=[pl.BlockSpec((None, 3, None, S, BLK), lambda g: (0, 0, g, 0, 0)),
                  pl.BlockSpec((3, 2, BLK, KEYS), lambda g: (0, g, 0, 0)), blk, blk, blk] + car.in_specs(),
        out_specs=[pl.BlockSpec((None, 3, None, S, BLK), lambda g: (0, 0, g, 0, 0)),
                   pl.BlockSpec((3, 2, BLK, KEYS), lambda g: (0, g, 0, 0))] + car.out_specs(),
        out_shape=[SDS((2, 3, 4, S, BLK), BF16), SDS((3, N_HEADS, BLK, KEYS), F32)] + car.out_shapes(),
        input_output_aliases=car.aliases(),
        scratch_shapes=[pltpu.VMEM((3, S, BLK), F32)] * 3 + [pltpu.VMEM((S, BLK), F32)] + car.sems(),
        compiler_params=_cp("arbitrary"))(qkv, bias, o, lse, do, *car.args())
    if job is not None:
        job.landed = list(outs[2:])
    return outs[0], outs[1]


def _shift_down(u, k, halo):
    n = u.shape[0]
    row = lax.broadcasted_iota(jnp.int32, u.shape, 0)
    out = pltpu.roll(u, k, 0)
    hn = halo.shape[0]
    for j in range(k):
        out = jnp.where(row == j, halo[hn - k + j:hn - k + j + 1, :], out)
    return out


def _shift_up(u, k, halo):
    n = u.shape[0]
    row = lax.broadcasted_iota(jnp.int32, u.shape, 0)
    out = pltpu.roll(u, n - k, 0)
    for j in range(k):
        out = jnp.where(row == n - k + j, halo[j:j + 1, :], out)
    return out


def _conv3(u, halo, w0, w1, w2):
    return _shift_down(u, 2, halo) * w0 + _shift_down(u, 1, halo) * w1 + u * w2


def _colsum(x):
    return jnp.sum(x, axis=0, keepdims=True)


def _mixer_specs(S, tm):
    conv = pl.BlockSpec((None, 12, tm, BLK), lambda i: (1, 0, i, 0))
    halo = pl.BlockSpec((None, 12, 16, BLK), lambda i: (1, 0, jnp.maximum(i * (tm // 16) - 1, 0), 0))
    ob = pl.BlockSpec((4, tm, BLK), lambda i: (0, i, 0))
    return conv, halo, ob


def _mixer_recompute(i, o_ref, pr_ref, ph_ref, w_ref):
    ob = [o_ref[q] for q in range(4)]
    gb = [pr_ref[q].astype(F32) for q in range(4)]
    gc = [pr_ref[4 + q].astype(F32) for q in range(4)]
    xi = [pr_ref[8 + q].astype(F32) for q in range(4)]
    keep = jnp.where(i > 0, 1.0, 0.0)
    u = [gc[q] * xi[q] for q in range(4)]
    hu = [ph_ref[4 + q].astype(F32) * ph_ref[8 + q].astype(F32) * keep for q in range(4)]
    w = [[w_ref[k:k + 1, q * BLK:(q + 1) * BLK] for k in range(3)] for q in range(4)]
    cv = [_conv3(u[q], hu[q], *w[q]) for q in range(4)]
    return ob, gb, gc, xi, u, hu, cv, w


def _rms_blocks(blocks):
    ss = sum(_rsum(b * b) for b in blocks)
    return lax.rsqrt(ss / (BLK * len(blocks)) + EPS)


def mixer_post_fwd(o, proj, w_sc, g_a, g_c):
    S = o.shape[1]
    tm = 512

    def body(o_ref, pr_ref, ph_ref, w_ref, ga_ref, gc_ref, m_ref):
        i = pl.program_id(0)
        ob, gb, _, _, _, _, cv, _ = _mixer_recompute(i, o_ref, pr_ref, ph_ref, w_ref)
        conv = [gb[q] * cv[q] for q in range(4)]
        ra, rc = _rms_blocks(ob), _rms_blocks(conv)
        for q in range(4):
            sl = slice(q * BLK, (q + 1) * BLK)
            m_ref[:, q * BLK:(q + 1) * BLK] = (ob[q] * ra * ga_ref[:, sl]).astype(BF16)
            m_ref[:, ATTN_W + q * BLK:ATTN_W + (q + 1) * BLK] = (conv[q] * rc * gc_ref[:, sl]).astype(BF16)

    conv_s, halo_s, ob_s = _mixer_specs(S, tm)
    full = lambda r, c: pl.BlockSpec((r, c), lambda i: (0, 0))
    return pl.pallas_call(
        body, grid=(S // tm,), name="mixer_post_fwd",
        in_specs=[ob_s, conv_s, halo_s, full(3, 512), full(1, 512), full(1, 512)],
        out_specs=pl.BlockSpec((tm, D_MODEL), lambda i: (i, 0)), out_shape=SDS((S, D_MODEL), BF16),
        compiler_params=_cp("parallel"))(o, proj, proj, w_sc, g_a, g_c)


def mixer_post_bwd_a(dmixed, o, proj, w_sc, g_a, g_c, job=None):
    S = o.shape[1]
    tm = 512

    def body(dm_ref, o_ref, pr_ref, ph_ref, w_ref, ga_ref, gc_ref, do_ref, dgb_ref, dcv_ref, dga_ref, dgc_ref):
        i = pl.program_id(0)
        ob, gb, _, _, _, _, cv, _ = _mixer_recompute(i, o_ref, pr_ref, ph_ref, w_ref)
        conv = [gb[q] * cv[q] for q in range(4)]
        ra, rc = _rms_blocks(ob), _rms_blocks(conv)

        @pl.when(i == 0)
        def _():
            dga_ref[...] = jnp.zeros_like(dga_ref)
            dgc_ref[...] = jnp.zeros_like(dgc_ref)

        for blocks, r, g_ref, off, dg_ref, is_attn in ((ob, ra, ga_ref, 0, dga_ref, True), (conv, rc, gc_ref, ATTN_W, dgc_ref, False)):
            xh = [blocks[q] * r for q in range(4)]
            dm = [dm_ref[:, off + q * BLK:off + (q + 1) * BLK].astype(F32) for q in range(4)]
            gd = [dm[q] * g_ref[:, q * BLK:(q + 1) * BLK] for q in range(4)]
            mean = sum(_rsum(gd[q] * xh[q]) for q in range(4)) / (4 * BLK)
            for q in range(4):
                dg_ref[:, q * BLK:(q + 1) * BLK] += _colsum(dm[q] * xh[q])
                dx = r * (gd[q] - xh[q] * mean)
                if is_attn:
                    do_ref[q] = dx
                else:
                    dgb_ref[q] = dx * cv[q]
                    dcv_ref[q] = dx * gb[q]

    conv_s, halo_s, ob_s = _mixer_specs(S, tm)
    full = lambda r, c: pl.BlockSpec((r, c), lambda i: (0, 0))
    return _pcall(
        body, [dmixed, o, proj, proj, w_sc, g_a, g_c], grid=(S // tm,), name="mixer_post_bwd_a",
        in_specs=[pl.BlockSpec((tm, D_MODEL), lambda i: (i, 0)), ob_s, conv_s, halo_s, full(3, 512), full(1, 512), full(1, 512)],
        out_specs=[ob_s, ob_s, ob_s, full(1, 512), full(1, 512)],
        out_shape=[SDS((4, S, BLK), F32)] * 3 + [SDS((1, 512), F32)] * 2, sem=("arbitrary",), job=job)


def mixer_post_bwd_b(dproj, dgb, dcv, proj, w_sc):
    S = proj.shape[2]
    tm = 512
    last = S // 8 - 1

    def body(dp_in, dgb_ref, dcv_ref, dn_ref, pr_ref, w_ref, dp_ref, dw_ref):
        i = pl.program_id(0)
        keep_next = jnp.where(i < pl.num_programs(0) - 1, 1.0, 0.0)

        @pl.when(i == 0)
        def _():
            dw_ref[...] = jnp.zeros_like(dw_ref)

        for q in range(4):
            sl = slice(q * BLK, (q + 1) * BLK)
            gc, xi = pr_ref[4 + q].astype(F32), pr_ref[8 + q].astype(F32)
            u = gc * xi
            dcv = dcv_ref[q]
            dn = dn_ref[q] * keep_next
            d1, d2 = _shift_up(dcv, 1, dn), _shift_up(dcv, 2, dn)
            w0, w1, w2 = (w_ref[k:k + 1, sl] for k in range(3))
            du = dcv * w2 + d1 * w1 + d2 * w0
            dw_ref[0:1, sl] += _colsum(d2 * u)
            dw_ref[1:2, sl] += _colsum(d1 * u)
            dw_ref[2:3, sl] += _colsum(dcv * u)
            dp_ref[q] = dgb_ref[q].astype(BF16)
            dp_ref[4 + q] = (du * xi).astype(BF16)
            dp_ref[8 + q] = (du * gc).astype(BF16)

    conv_s, _, ob_s = _mixer_specs(S, tm)
    nxt = pl.BlockSpec((4, 8, BLK), lambda i: (0, jnp.minimum((i + 1) * (tm // 8), last), 0))
    full = lambda r, c: pl.BlockSpec((r, c), lambda i: (0, 0))
    return pl.pallas_call(
        body, grid=(S // tm,), name="mixer_post_bwd_b",
        in_specs=[pl.BlockSpec(memory_space=pl.ANY), ob_s, ob_s, nxt, conv_s, full(3, 512)],
        out_specs=[conv_s, full(3, 512)],
        out_shape=[SDS(dproj.shape, BF16), SDS((3, 512), F32)],
        input_output_aliases={0: 0}, compiler_params=_cp("arbitrary"))(dproj, dgb, dcv, dcv, proj, w_sc)


def xattn_fwd(q, k, v):
    S = q.shape[0]
    tm = 512
    scale = MEM_HD ** -0.5

    def body(q_ref, k_ref, v_ref, o_ref):
        for h in range(N_MEM_HEADS):
            sl = slice(h * MEM_HD, (h + 1) * MEM_HD)
            s = _dot_nt(q_ref[:, sl], k_ref[:, sl]) * scale
            p = jnp.exp(s - jnp.max(s, axis=1, keepdims=True))
            p = p / _rsum(p)
            o_ref[:, sl] = _dot(p.astype(BF16), v_ref[:, sl]).astype(BF16)

    row = pl.BlockSpec((tm, D_MODEL), lambda i: (i, 0))
    kv = pl.BlockSpec(k.shape, lambda i: (0, 0))
    return pl.pallas_call(body, grid=(S // tm,), name="xattn_fwd", in_specs=[row, kv, kv], out_specs=row,
                          out_shape=SDS((S, D_MODEL), BF16), compiler_params=_cp("parallel"))(q, k, v)


def xattn_bwd(q, k, v, do):
    S = q.shape[0]
    tm = 512
    scale = MEM_HD ** -0.5

    def body(q_ref, k_ref, v_ref, do_ref, dq_ref, dk_ref, dv_ref):
        @pl.when(pl.program_id(0) == 0)
        def _():
            dk_ref[...] = jnp.zeros_like(dk_ref)
            dv_ref[...] = jnp.zeros_like(dv_ref)

        for h in range(N_MEM_HEADS):
            sl = slice(h * MEM_HD, (h + 1) * MEM_HD)
            qh, kh, vh, doh = q_ref[:, sl], k_ref[:, sl], v_ref[:, sl], do_ref[:, sl]
            s = _dot_nt(qh, kh) * scale
            p = jnp.exp(s - jnp.max(s, axis=1, keepdims=True))
            p = p / _rsum(p)
            dp = _dot_nt(doh, vh)
            ds = (p * (dp - _rsum(p * dp)) * scale).astype(BF16)
            dq_ref[:, sl] = _dot(ds, kh).astype(BF16)
            dk_ref[:, sl] += _dot_tn(ds, qh)
            dv_ref[:, sl] += _dot_tn(p.astype(BF16), doh)

    row = pl.BlockSpec((tm, D_MODEL), lambda i: (i, 0))
    kv = pl.BlockSpec(k.shape, lambda i: (0, 0))
    return pl.pallas_call(body, grid=(S // tm,), name="xattn_bwd", in_specs=[row, kv, kv, row], out_specs=[row, kv, kv],
                          out_shape=[SDS((S, D_MODEL), BF16), SDS(k.shape, F32), SDS(k.shape, F32)],
                          compiler_params=_cp("arbitrary"))(q, k, v, do)


FFN_TM, FFN_TC = 256, 1408


def _ffn_specs(S):
    tm, tc = FFN_TM, FFN_TC
    blk = pl.BlockSpec((2, tm, tc), lambda j, i: (0, i, j))
    nxt = pl.BlockSpec((2, 16, tc), lambda j, i: (0, jnp.minimum((i + 1) * (tm // 16), S // 16 - 1), j))
    wsp = pl.BlockSpec((2, 3, tc), lambda j, i: (0, 0, j))
    bsp = pl.BlockSpec((2, 1, tc), lambda j, i: (0, 0, j))
    return blk, nxt, wsp, bsp


def up_act_fwd(h, w_up, w_fc, b_fc, job=None):
    S, K = h.shape
    tm, tc = FFN_TM, FFN_TC
    nj = D_FF // tc

    def body(h_ref, wg_ref, wv_ref, w_ref, b_ref, pre_ref, gv_ref, a_ref, halo):
        i = pl.program_id(1)
        hv = h_ref[...]
        res = []
        for half, wt_ref in enumerate((wg_ref, wv_ref)):
            u = _dot(hv, wt_ref[...])
            hu = jnp.where(i > 0, halo[half], 0.0)
            halo[half] = u[tm - 8:, :]
            w0, w1, w2 = (w_ref[half, k:k + 1, :] for k in range(3))
            pre_ref[half] = u.astype(BF16)
            res.append(_conv3(u, hu, w0, w1, w2) + b_ref[half])
        g, v = res
        gv_ref[0] = g.astype(BF16)
        gv_ref[1] = v.astype(BF16)
        a_ref[...] = (g * jax.nn.sigmoid(g) * v).astype(BF16)

    blk = pl.BlockSpec((2, tm, tc), lambda j, i: (0, i, j))
    return _pcall(
        body, [h, w_up, w_up, w_fc, b_fc], grid=(nj, S // tm), name="up_act_fwd",
        in_specs=[pl.BlockSpec((tm, K), lambda j, i: (i, 0)), pl.BlockSpec((K, tc), lambda j, i: (0, j)),
                  pl.BlockSpec((K, tc), lambda j, i: (0, j + nj)), pl.BlockSpec((2, 3, tc), lambda j, i: (0, 0, j)),
                  pl.BlockSpec((2, 1, tc), lambda j, i: (0, 0, j))],
        out_specs=[blk, blk, pl.BlockSpec((tm, tc), lambda j, i: (i, j))],
        out_shape=[SDS((2, S, D_FF), BF16), SDS((2, S, D_FF), BF16), SDS((S, D_FF), BF16)],
        scratch=[pltpu.VMEM((2, 8, tc), F32)], sem=("parallel", "arbitrary"), job=job)


def ffn_act_bwd(dx, w_down, up, up_pre, w_fc, job=None):
    S = up.shape[1]

    def act_grads(da, g, v):
        sg = jax.nn.sigmoid(g)
        return da * v * (sg * (1.0 + g * (1.0 - sg))), da * g * sg

    def body(dx_ref, dxn_ref, wd_ref, gv_ref, gvn_ref, up_ref, w_ref, dp_ref, dw_ref, db_ref):
        i = pl.program_id(1)
        keep_next = jnp.where(i < pl.num_programs(1) - 1, 1.0, 0.0)

        @pl.when(i == 0)
        def _():
            dw_ref[...] = jnp.zeros_like(dw_ref)
            db_ref[...] = jnp.zeros_like(db_ref)

        wd = wd_ref[...]
        da = _dot_nt(dx_ref[...].astype(BF16), wd)
        dan = _dot_nt(dxn_ref[...].astype(BF16), wd) * keep_next
        here = act_grads(da, gv_ref[0].astype(F32), gv_ref[1].astype(F32))
        after = act_grads(dan, gvn_ref[0].astype(F32), gvn_ref[1].astype(F32))
        for half in range(2):
            d, dn = here[half], after[half]
            u = up_ref[half].astype(F32)
            d1, d2 = _shift_up(d, 1, dn), _shift_up(d, 2, dn)
            w0, w1, w2 = (w_ref[half, k:k + 1, :] for k in range(3))
            dp_ref[half] = (d * w2 + d1 * w1 + d2 * w0).astype(BF16)
            dw_ref[half, 0:1, :] += _colsum(d2 * u)
            dw_ref[half, 1:2, :] += _colsum(d1 * u)
            dw_ref[half, 2:3, :] += _colsum(d * u)
            db_ref[half] += _colsum(d)

    tm, tc = FFN_TM, FFN_TC
    blk, nxt, wsp, bsp = _ffn_specs(S)
    K = dx.shape[1]
    rows = pl.BlockSpec((tm, K), lambda j, i: (i, 0))
    rows_next = pl.BlockSpec((16, K), lambda j, i: (jnp.minimum((i + 1) * (tm // 16), S // 16 - 1), 0))
    return _pcall(body, [dx, dx, w_down, up, up, up_pre, w_fc], grid=(D_FF // tc, S // tm), name="ffn_act_bwd",
                  in_specs=[rows, rows_next, pl.BlockSpec((tc, K), lambda j, i: (j, 0)), blk, nxt, blk, wsp],
                  out_specs=[blk, wsp, bsp],
                  out_shape=[SDS((2, S, D_FF), BF16), SDS((2, 3, D_FF), F32), SDS((2, 1, D_FF), F32)],
                  sem=("parallel", "arbitrary"), job=job)


def local_step(x, mem, target, wb, ws, comm=None):
    S = x.shape[0]
    assert S % 2048 == 0
    if comm is None:
        comm = NoComm()
    else:
        wb = {}
    ws = dict(ws)

    bias = bias_tables(ws["rel_bias"])
    h1 = rmsnorm_fwd(x, ws["g_mix"], "norm_mix", job=comm.job("norm_mix"))
    comm.landed("norm_mix", wb, ws)
    w_fc = ws["w_ffn_conv"].reshape(3, 2, D_FF).transpose(1, 0, 2)
    b_fc = ws["b_ffn_conv"].reshape(2, 1, D_FF)
    proj = mm_nn(h1, wb["w_in"], "proj", out_dtype=BF16, out_bw=BLK, tn=768, job=comm.job("proj"))
    comm.landed("proj", wb)
    qkv = proj.reshape(2, 3, 4, S, BLK)
    o, lse = attn_fwd(qkv, bias, job=comm.job("attn_fwd"))
    comm.landed("attn_fwd", wb)
    proj4 = proj.reshape(2, 12, S, BLK)
    mixed = mixer_post_fwd(o, proj4, ws["w_short_conv"], ws["g_attn_out"], ws["g_conv_out"])
    x1, h2 = mm_nn_rows(mixed, wb["w_out"], x, ws["g_xattn"], "out_proj")
    mem_n = rmsnorm_fwd(mem, ws["g_mem"], "norm_mem")
    xq = mm_nn(h2, wb["w_xq"], "xq", out_dtype=BF16)
    xk = mm_nn(mem_n, wb["w_xk"], "xk", out_dtype=BF16, tn=1024)
    xv = mm_nn(mem_n, wb["w_xv"], "xv", out_dtype=BF16, tn=1024)
    xo = xattn_fwd(xq, xk, xv)
    x2, h3 = mm_nn_rows(xo, wb["w_xo"], x1, ws["g_ffn"], "xo_proj")
    up_pre, up, act = up_act_fwd(h3, wb["w_up"], w_fc, b_fc, job=comm.job("up_act_fwd"))
    comm.landed("up_act_fwd", wb)
    loss, dx3, dg_final = mm_nn_rows(act, wb["w_down"], x2, ws["g_final"].reshape(1, -1), "down_proj", target=target)

    gb, gs = {}, {"g_final": dg_final}
    gb["w_down"] = mm_tn(act, dx3, None, "dw_down", tm=1408, tn=1024)
    d_up_pre, dw_fc, db_fc = ffn_act_bwd(dx3, wb["w_down"], up, up_pre, w_fc, job=comm.job("ffn_act_bwd", gb))
    comm.landed("ffn_act_bwd")
    gs["b_ffn_conv"] = db_fc.reshape(1, 2 * D_FF)
    gs["w_ffn_conv"] = dw_fc.transpose(1, 0, 2).reshape(3, 2 * D_FF)
    gb["w_up"] = mm_tn(h3, d_up_pre, D_FF, "dw_up", shards=4, tn=1408, job=comm.job("dw_up"))
    comm.landed("dw_up")
    dx2, gs["g_ffn"] = mm_nt_norm_bwd(d_up_pre, D_FF, wb["w_up"], x2, ws["g_ffn"], dx3, "d_h3", tm=256,
                                      job=comm.job("d_h3", gb))
    comm.landed("d_h3")
    gb["w_xo"] = mm_tn(xo, dx2, None, "dw_xo")
    dxo = mm_nt(dx2, None, wb["w_xo"], "d_xo", out_dtype=BF16)
    dxq, dxk, dxv = xattn_bwd(xq, xk, xv, dxo)
    gb["w_xq"] = mm_tn(h2, dxq, None, "dw_xq")
    gb["w_xk"] = mm_tn(mem_n, dxk, None, "dw_xk", tn=1024)
    gb["w_xv"] = mm_tn(mem_n, dxv, None, "dw_xv", tn=1024)
    dmem_n = mm_nt(dxk, None, wb["w_xk"], "d_memk", tn=1024)
    dmem_n = mm_nt(dxv, None, wb["w_xv"], "d_memv", res=dmem_n, tn=1024)
    gs["g_mem"] = rmsnorm_gain_grad(mem, dmem_n, "norm_mem_bwd")
    dx1, gs["g_xattn"] = mm_nt_norm_bwd(dxq, None, wb["w_xq"], x1, ws["g_xattn"], dx2, "d_h2")
    gb["w_out"] = mm_tn(mixed, dx1, None, "dw_out")
    dmixed = mm_nt(dx1, None, wb["w_out"], "d_mixed", out_dtype=BF16)
    do, dgb, dcv, gs["g_attn_out"], gs["g_conv_out"] = mixer_post_bwd_a(
        dmixed, o, proj4, ws["w_short_conv"], ws["g_attn_out"], ws["g_conv_out"], job=comm.job("mixer_post_bwd_a", gb))
    comm.landed("mixer_post_bwd_a")
    dproj, ds_sum = attn_bwd(qkv, bias, o, lse, do, job=comm.job("attn_bwd"))
    comm.landed("attn_bwd")
    gs["rel_bias"] = bias_tables_bwd(ds_sum)
    dproj, gs["w_short_conv"] = mixer_post_bwd_b(dproj.reshape(2, 12, S, BLK), dgb, dcv, proj4, ws["w_short_conv"])
    dproj = dproj.reshape(24, S, BLK)
    gb["w_in"] = mm_tn(h1, dproj, BLK, "dw_in", shards=4, tn=768, job=comm.job("dw_in"))
    comm.landed("dw_in")
    comm.first_update(gb)
    grad_x, gs["g_mix"] = mm_nt_norm_bwd(dproj, BLK, wb["w_in"], x, ws["g_mix"], dx1, "d_h1", tm=256,
                                         job=comm.job("d_h1", gb))
    comm.landed("d_h1")
    return loss, grad_x, comm.finish(gb), gs


def _adam_update(w_ref, g_ref, m_ref, v_ref, d_ref, nm_ref, nv_ref):
    gv = g_ref[...]
    mn = ADAM_B1 * m_ref[...] + (1.0 - ADAM_B1) * gv
    vn = ADAM_B2 * v_ref[...] + (1.0 - ADAM_B2) * (gv * gv)
    m_hat = mn / (1.0 - ADAM_B1 ** ADAM_STEP)
    v_hat = vn / (1.0 - ADAM_B2 ** ADAM_STEP)
    d_ref[...] = -ADAM_LR * (m_hat / (jnp.sqrt(v_hat) + ADAM_EPS) + ADAM_WD * w_ref[...])
    nm_ref[...] = mn
    nv_ref[...] = vn


def adamw(ws, gs, ms, vs, name, job=None):
    n = len(ws)
    R, C = ws[0].shape
    tr = R // 4

    def body(*refs):
        for p in range(n):
            _adam_update(*(refs[q * n + p] for q in range(7)))

    blk = pl.BlockSpec((tr, C), lambda i: (i, 0))
    outs = _pcall(body, [*ws, *gs, *ms, *vs], grid=(4,), name=name, in_specs=[blk] * (4 * n), out_specs=[blk] * (3 * n),
                  out_shape=[SDS((R, C), F32)] * (3 * n), sem=("parallel",), job=job)
    return outs[:n], outs[n:2 * n], outs[2 * n:]


BIG = ("w_in", "w_out", "w_xq", "w_xk", "w_xv", "w_xo", "w_up", "w_down")
COL_SHARDED = ("w_in", "w_up")
ANY = pl.BlockSpec(memory_space=pl.ANY)


def _place():
    x, y, c = lax.axis_index("x"), lax.axis_index("y"), lax.axis_index("c")
    chips = [(1 - x, y), (x, 1 - y), (1 - x, 1 - y)]
    return x, y, c, chips


def _window(full, name, R, C, shard, half):
    r0, nr = (0, R) if half is None else (half * (R // 2), R // 2)
    if name in COL_SHARDED:
        return full.at[pl.ds(r0, nr), pl.ds(shard * C, C)]
    return full.at[pl.ds(shard * R + r0, nr), :]


def same_shape_groups(names, arrays):
    groups = {}
    for n in names:
        groups.setdefault(tuple(arrays[n].shape), []).append(n)
    return list(groups.values())


def place_shards(ws, mine_arr, col, name):
    n = len(ws)
    R, C = ws[0].shape

    def body(s_ref, *refs):
        for p in range(n):
            refs[n + p][...] = refs[p][...].astype(BF16)

    grid_spec = pltpu.PrefetchScalarGridSpec(
        num_scalar_prefetch=1, grid=(1,), in_specs=[pl.BlockSpec((R, C), lambda i, s: (0, 0))] * n,
        out_specs=[pl.BlockSpec((R, C), (lambda i, s: (0, s[0])) if col else (lambda i, s: (s[0], 0)))] * n)
    return pl.pallas_call(body, grid_spec=grid_spec, name=name,
                          out_shape=[SDS((R, 4 * C) if col else (4 * R, C), BF16)] * n,
                          compiler_params=_cp("arbitrary"))(mine_arr, *ws)


def _gather_jobs(names, shapes):
    nw = len(names)

    def start(full, sems):
        send, recv, fsend, frecv = sems
        x, y, c, chips = _place()
        mine = 2 * x + y
        for w, n in enumerate(names):
            R, C = shapes[w]
            own = _window(full[w], n, R, C, mine, c)
            for j, chip in enumerate(chips):
                pltpu.make_async_remote_copy(src_ref=own, dst_ref=own, send_sem=send.at[w, j], recv_sem=recv.at[w, j],
                                             device_id=(*chip, c), device_id_type=MESH).start()

    def mid(full, sems, lo=0, hi=nw):
        send, recv, fsend, frecv = sems
        x, y, c, chips = _place()
        sib = (x, y, 1 - c)
        for w, n in list(enumerate(names))[lo:hi]:
            R, C = shapes[w]
            for j, chip in enumerate(chips):
                landed = _window(full[w], n, R, C, 2 * chip[0] + chip[1], c)
                pltpu.make_async_remote_copy(src_ref=landed, dst_ref=landed, send_sem=send.at[w, j], recv_sem=recv.at[w, j],
                                             device_id=(*chip, c), device_id_type=MESH).wait_recv()
                pltpu.make_async_remote_copy(src_ref=landed, dst_ref=landed, send_sem=fsend.at[w, j],
                                             recv_sem=frecv.at[w, j], device_id=sib, device_id_type=MESH).start()

    def finish(full, sems):
        send, recv, fsend, frecv = sems
        x, y, c, chips = _place()
        mine = 2 * x + y
        sib = (x, y, 1 - c)
        for w, n in enumerate(names):
            R, C = shapes[w]
            own = _window(full[w], n, R, C, mine, c)
            for j, chip in enumerate(chips):
                landed = _window(full[w], n, R, C, 2 * chip[0] + chip[1], c)
                other = _window(full[w], n, R, C, 2 * chip[0] + chip[1], 1 - c)
                pltpu.make_async_remote_copy(src_ref=other, dst_ref=other, send_sem=fsend.at[w, j], recv_sem=frecv.at[w, j],
                                             device_id=sib, device_id_type=MESH).wait_recv()
                pltpu.make_async_remote_copy(src_ref=own, dst_ref=own, send_sem=send.at[w, j], recv_sem=recv.at[w, j],
                                             device_id=(*chip, c), device_id_type=MESH).wait_send()
                pltpu.make_async_remote_copy(src_ref=landed, dst_ref=landed, send_sem=fsend.at[w, j],
                                             recv_sem=frecv.at[w, j], device_id=sib, device_id_type=MESH).wait_send()

    return start, mid, finish, [pltpu.SemaphoreType.DMA((nw, 3))] * 4


class CommJob:
    def __init__(self, ins, out_shapes, inplace, start, finish, sems, mid=None):
        self.ins, self.out_shapes, self.inplace = list(ins), list(out_shapes), inplace
        self.start, self.mid, self.finish, self.sems = start, mid, finish, list(sems)

    def run(self, phase, ins, outs, sems):
        if phase == 0:
            self.start(ins, outs, sems)
        elif phase == 1:
            if self.mid is not None:
                self.mid(ins, outs, sems)
        else:
            self.finish(ins, outs, sems)


class Carried:
    def __init__(self, job, n_in, n_out, n_scratch, steps):
        self.job, self.n_in, self.n_out, self.n_scratch, self.steps = job, n_in, n_out, n_scratch, steps
        self.nji = len(job.ins) if job else 0
        self.njo = len(job.out_shapes) if job else 0

    def in_specs(self):
        return [ANY] * self.nji

    def out_specs(self):
        return [ANY] * self.njo

    def out_shapes(self):
        return list(self.job.out_shapes) if self.job else []

    def aliases(self):
        if not (self.job and self.job.inplace):
            return {}
        return {self.n_in + i: self.n_out + i for i in range(self.nji)}

    def sems(self):
        return list(self.job.sems) if self.job else []

    def args(self):
        return list(self.job.ins) if self.job else []

    def split(self, refs):
        a = self.n_in
        b = a + self.nji
        c = b + self.n_out
        d = c + self.njo
        e = d + self.n_scratch
        return refs[:a] + refs[b:c] + refs[d:e], (refs[a:b], refs[c:d], refs[e:])

    def phase(self, phase, step, jrefs):
        if self.job is None:
            return
        at = {0: 0, 1: max(self.steps - 2, 0), 2: self.steps - 1}[phase]

        @pl.when(step == at)
        def _():
            self.job.run(phase, *jrefs)


def _pcall(body, args, *, grid, in_specs, out_specs, out_shape, name, sem, scratch=(), aliases=None, job=None):
    n_in, n_out = len(args), len(out_shape)
    if job is None:
        return pl.pallas_call(
            body, grid=grid, in_specs=list(in_specs), out_specs=list(out_specs), out_shape=list(out_shape), name=name,
            scratch_shapes=list(scratch), input_output_aliases=dict(aliases or {}), compiler_params=_cp(*sem))(*args)
    total = int(np.prod(grid))
    car = Carried(job, n_in, n_out, len(scratch), total)

    def wrapped(*refs):
        main, jrefs = car.split(refs)
        lin = pl.program_id(0)
        for ax in range(1, len(grid)):
            lin = lin * grid[ax] + pl.program_id(ax)
        car.phase(0, lin, jrefs)
        body(*main)
        car.phase(1, lin, jrefs)
        car.phase(2, lin, jrefs)

    res = pl.pallas_call(
        wrapped, grid=grid, in_specs=list(in_specs) + car.in_specs(), out_specs=list(out_specs) + car.out_specs(),
        out_shape=list(out_shape) + car.out_shapes(), name=name, scratch_shapes=list(scratch) + car.sems(),
        input_output_aliases={**dict(aliases or {}), **car.aliases()},
        compiler_params=_cp(*(["arbitrary"] * len(grid))))(*args, *car.args())
    job.landed = list(res[n_out:])
    return list(res[:n_out])


def gather_job(placed, names):
    shapes = []
    for n in names:
        R, C = placed[n].shape
        shapes.append((R, C // 4) if n in COL_SHARDED else (R // 4, C))
    start, mid, finish, sems = _gather_jobs(names, shapes)
    arrays = [placed[n] for n in names]
    early = max(len(names) - 1, 1)

    def last(i, o, s):
        mid(o, s, early, len(names))
        finish(o, s)

    return CommJob(arrays, [SDS(a.shape, a.dtype) for a in arrays], True,
                   lambda i, o, s: start(o, s), last, sems, mid=lambda i, o, s: mid(o, s, 0, early))


def gather_rows_job(placed):
    widths = [a.shape[1] // 4 for a in placed]

    def copies(outs, sems):
        x, y, c, chips = _place()
        mine = 2 * x + y
        cps = []
        for w, ref in enumerate(outs):
            own = ref.at[:, pl.ds(mine * widths[w], widths[w])]
            for j, chip in enumerate(chips):
                theirs = ref.at[:, pl.ds((2 * chip[0] + chip[1]) * widths[w], widths[w])]
                kw = dict(send_sem=sems[0].at[w, j], recv_sem=sems[1].at[w, j], device_id=(*chip, c), device_id_type=MESH)
                cps.append((pltpu.make_async_remote_copy(src_ref=own, dst_ref=own, **kw),
                            pltpu.make_async_remote_copy(src_ref=theirs, dst_ref=theirs, **kw)))
        return cps

    def start(ins, outs, sems):
        for send, _ in copies(outs, sems):
            send.start()

    def finish(ins, outs, sems):
        for send, recv in copies(outs, sems):
            recv.wait_recv()
            send.wait_send()

    return CommJob(placed, [SDS(a.shape, a.dtype) for a in placed], True, start, finish,
                   [pltpu.SemaphoreType.DMA((len(placed), 3))] * 2)


def merge_jobs(a, b):
    assert a.inplace and b.inplace
    na, nb_, nsa = len(a.ins), len(b.ins), len(a.sems)

    def phase(which):
        def run(ins, outs, sems):
            for job, i, o, s in ((a, ins[:na], outs[:na], sems[:nsa]), (b, ins[na:], outs[na:], sems[nsa:])):
                fn = getattr(job, which)
                if fn is not None:
                    fn(i, o, s)
        return run

    merged = CommJob(a.ins + b.ins, a.out_shapes + b.out_shapes, True, phase("start"), phase("finish"), a.sems + b.sems,
                     mid=phase("mid"))
    merged.parts = (a, b, na)
    return merged


def split_landed(merged):
    a, b, na = merged.parts
    a.landed, b.landed = merged.landed[:na], merged.landed[na:]


def pair_exchange_job(grads, names):
    shapes = [grads[n].shape for n in names]

    def copies(ins, outs, sems):
        x, y, c, _ = _place()
        return [pltpu.make_async_remote_copy(
            src_ref=ins[w].at[:, pl.ds((1 - c) * (shapes[w][1] // 2), shapes[w][1] // 2), :], dst_ref=outs[w],
            send_sem=sems[0].at[w], recv_sem=sems[1].at[w], device_id=(x, y, 1 - c), device_id_type=MESH)
            for w in range(len(names))]

    def start(ins, outs, sems):
        for cp in copies(ins, outs, sems):
            cp.start()

    def finish(ins, outs, sems):
        for cp in copies(ins, outs, sems):
            cp.wait()

    return CommJob([grads[n] for n in names], [SDS((4, s[1] // 2, s[2]), F32) for s in shapes], False, start, finish,
                   [pltpu.SemaphoreType.DMA((len(names),))] * 2)


def chip_exchange_job(p16, names):
    shapes = [p16[n].shape for n in names]

    def copies(ins, outs, sems):
        x, y, c, chips = _place()
        return [pltpu.make_async_remote_copy(
            src_ref=ins[w].at[2 * chip[0] + chip[1]], dst_ref=outs[w].at[j],
            send_sem=sems[0].at[w, j], recv_sem=sems[1].at[w, j], device_id=(*chip, c), device_id_type=MESH)
            for w in range(len(names)) for j, chip in enumerate(chips)]

    def start(ins, outs, sems):
        for cp in copies(ins, outs, sems):
            cp.start()

    def finish(ins, outs, sems):
        for cp in copies(ins, outs, sems):
            cp.wait()

    return CommJob([p16[n] for n in names], [SDS((3,) + tuple(s[1:]), BF16) for s in shapes], False, start, finish,
                   [pltpu.SemaphoreType.DMA((len(names), 3))] * 2)


def half_exchange_job(full, names):
    shapes = [full[n].shape for n in names]

    def copies(outs, sems):
        x, y, c, _ = _place()
        cps = []
        for w in range(len(names)):
            Rh = shapes[w][0] // 2
            rows = outs[w].at[pl.ds(c * Rh, Rh), :]
            other = outs[w].at[pl.ds((1 - c) * Rh, Rh), :]
            cps.append((pltpu.make_async_remote_copy(src_ref=rows, dst_ref=rows, send_sem=sems[0].at[w], recv_sem=sems[1].at[w],
                                                     device_id=(x, y, 1 - c), device_id_type=MESH),
                        pltpu.make_async_remote_copy(src_ref=other, dst_ref=other, send_sem=sems[0].at[w], recv_sem=sems[1].at[w],
                                                     device_id=(x, y, 1 - c), device_id_type=MESH)))
        return cps

    def start(ins, outs, sems):
        for send, _ in copies(outs, sems):
            send.start()

    def finish(ins, outs, sems):
        for send, recv in copies(outs, sems):
            recv.wait_recv()
            send.wait_send()

    arrays = [full[n] for n in names]
    return CommJob(arrays, [SDS(a.shape, a.dtype) for a in arrays], True, start, finish,
                   [pltpu.SemaphoreType.DMA((len(names),))] * 2)


def allreduce_small(parts, name, job=None):
    n = len(parts)
    nji = len(job.ins) if job else 0
    njo = len(job.out_shapes) if job else 0

    def body(*refs):
        ins, jins = refs[:n], refs[n:n + nji]
        outs, jouts = refs[n + nji:2 * n + nji], refs[2 * n + nji:2 * n + nji + njo]
        bufs = refs[2 * n + nji + njo:3 * n + nji + njo]
        send, recv = refs[3 * n + nji + njo:3 * n + nji + njo + 2]
        jsems = refs[3 * n + nji + njo + 2:]
        x, y, c, _ = _place()
        me = 4 * x + 2 * y + c
        if job is not None:
            job.run(0, jins, jouts, jsems)
        sends = []
        for p in range(n):
            for k in range(1, 8):
                peer = (x ^ (k >> 2), y ^ ((k >> 1) & 1), c ^ (k & 1))
                cp = pltpu.make_async_remote_copy(src_ref=ins[p], dst_ref=bufs[p].at[k], send_sem=send.at[p, k - 1],
                                                  recv_sem=recv.at[p, k - 1], device_id=peer, device_id_type=MESH)
                cp.start()
                sends.append(cp)
            bufs[p][0] = ins[p][...]
        for cp in sends:
            cp.wait_recv()
        for p in range(n):
            acc = bufs[p][me]
            for d in range(1, 8):
                acc = acc + bufs[p][d ^ me]
            outs[p][...] = acc
        for cp in sends:
            cp.wait_send()
        if job is not None:
            job.run(1, jins, jouts, jsems)
            job.run(2, jins, jouts, jsems)

    vm = pl.BlockSpec(memory_space=pltpu.VMEM)
    res = pl.pallas_call(
        body, name=name, in_specs=[vm] * n + [ANY] * nji, out_specs=[vm] * n + [ANY] * njo,
        out_shape=[SDS(a.shape, F32) for a in parts] + (list(job.out_shapes) if job else []),
        input_output_aliases={n + i: n + i for i in range(nji)} if (job and job.inplace) else {},
        scratch_shapes=[pltpu.VMEM((8,) + tuple(a.shape), F32) for a in parts] + [pltpu.SemaphoreType.DMA((n, 7))] * 2
        + (list(job.sems) if job else []),
    )(*parts, *(job.ins if job else []))
    if job is not None:
        job.landed = list(res[n:])
    return list(res[:n])


def adamw_multi(ws, gs, ms, vs, name):
    n = len(ws)

    def body(*refs):
        for p in range(n):
            _adam_update(*(refs[q * n + p] for q in range(7)))

    vm = pl.BlockSpec(memory_space=pltpu.VMEM)
    outs = pl.pallas_call(body, name=name, in_specs=[vm] * (4 * n), out_specs=[vm] * (3 * n),
                          out_shape=[SDS(a.shape, F32) for a in ws] * 3)(*ws, *gs, *ms, *vs)
    return outs[:n], outs[n:2 * n], outs[2 * n:]


def pair_sum(gs, xhs, sel_arr, name):
    n = len(gs)
    _, R, C = gs[0].shape
    Rh = R // 2

    def body(sel_ref, *refs):
        for p in range(n):
            g_ref, x_ref, p32_ref, p16_ref = (refs[q * n + p] for q in range(4))
            s = g_ref[...] + x_ref[...]
            p16_ref[...] = s.astype(BF16)

            @pl.when(pl.program_id(0) == sel_ref[0])
            def _():
                p32_ref[...] = s

    same = pl.BlockSpec((None, Rh, C), lambda s, sel: (s, 0, 0))
    grid_spec = pltpu.PrefetchScalarGridSpec(
        num_scalar_prefetch=1, grid=(4,),
        in_specs=[pl.BlockSpec((None, Rh, C), lambda s, sel: (s, sel[1], 0))] * n + [same] * n,
        out_specs=[pl.BlockSpec((Rh, C), lambda s, sel: (0, 0))] * n + [same] * n)
    outs = pl.pallas_call(body, grid_spec=grid_spec, name=name,
                          out_shape=[SDS((Rh, C), F32)] * n + [SDS((4, Rh, C), BF16)] * n,
                          compiler_params=_cp("arbitrary"))(sel_arr, *gs, *xhs)
    return outs[:n], outs[n:]


def chip_sum(p32s, ys, sel_arr, name):
    n = len(p32s)
    Rh, C = p32s[0].shape

    def body(s_ref, *refs):
        for p in range(n):
            p_ref, y_ref, o_ref = (refs[q * n + p] for q in range(3))
            o_ref[...] = ((p_ref[...] + y_ref[0].astype(F32)) + y_ref[1].astype(F32)) + y_ref[2].astype(F32)

    grid_spec = pltpu.PrefetchScalarGridSpec(
        num_scalar_prefetch=1, grid=(1,),
        in_specs=[pl.BlockSpec((Rh, C), lambda i, s: (0, 0))] * n + [pl.BlockSpec((3, Rh, C), lambda i, s: (0, 0, 0))] * n,
        out_specs=[pl.BlockSpec((Rh, C), lambda i, s: (s[1], 0))] * n)
    return pl.pallas_call(body, grid_spec=grid_spec, name=name, out_shape=[SDS((2 * Rh, C), F32)] * n,
                          compiler_params=_cp("arbitrary"))(sel_arr, *p32s, *ys)


class StepComm:
    FIRST = ("w_in",)
    SQUARE = ("w_out", "w_xq", "w_xk", "w_xv", "w_xo")
    FFN = ("w_up", "w_down")
    PLAN = {
        "proj": ("gather", SQUARE[:3]), "attn_fwd": ("gather", SQUARE[3:] + FFN[:1]), "up_act_fwd": ("gather", FFN[1:]),
        "ffn_act_bwd": ("pair", FFN[1:]), "dw_up": ("chip", FFN[1:]),
        "d_h3": ("pair", FFN[:1]), "mixer_post_bwd_a": ("pair", SQUARE), "attn_bwd": ("chip", FFN[:1] + SQUARE),
        "dw_in": ("half", FFN + SQUARE), "first_update": ("pair", FIRST), "d_h1": ("chip", FIRST),
    }

    def __init__(self, placed, conv_placed, sel_arr):
        self.placed, self.conv_placed, self.sel_arr = placed, conv_placed, sel_arr
        self.g, self.p32, self.p16, self.full, self.done, self.jobs = {}, {}, {}, {}, {}, {}

    def job(self, tag, gb=None):
        if tag == "norm_mix":
            self.jobs[tag] = merge_jobs(gather_job(self.placed, self.FIRST), gather_rows_job(self.conv_placed))
            return self.jobs[tag]
        kind, names = self.PLAN[tag]
        if kind == "gather":
            j = gather_job(self.placed, names)
        elif kind == "pair":
            for n in names:
                self.g[n] = gb[n] if gb[n].ndim == 3 else gb[n].reshape(4, gb[n].shape[0] // 4, gb[n].shape[1])
            j = pair_exchange_job(self.g, names)
        elif kind == "chip":
            j = chip_exchange_job(self.p16, names)
        else:
            j = half_exchange_job(self.full, names)
        self.jobs[tag] = j
        return j

    def landed(self, tag, wb=None, ws=None):
        if tag == "norm_mix":
            split_landed(self.jobs[tag])
            first, conv, _ = self.jobs[tag].parts
            wb.update(zip(self.FIRST, first.landed))
            ws.update((n, a[:3]) for n, a in zip(SMALL_SHARDED, conv.landed))
            return
        kind, names = self.PLAN[tag]
        got = dict(zip(names, self.jobs[tag].landed))
        if kind == "gather":
            wb.update(got)
        elif kind == "half":
            self.done.update(got)
        for group in same_shape_groups(names, got):
            if kind == "pair":
                p32s, p16s = pair_sum([self.g[n] for n in group], [got[n] for n in group], self.sel_arr, "pair_sum_" + group[0])
                self.p32.update(zip(group, p32s))
                self.p16.update(zip(group, p16s))
            elif kind == "chip":
                fulls = chip_sum([self.p32[n] for n in group], [got[n] for n in group], self.sel_arr, "chip_sum_" + group[0])
                self.full.update(zip(group, fulls))

    def first_update(self, gb):
        self.first_result = self.update_fn(self.job("first_update", gb))
        self.landed("first_update")

    def finish(self, gb):
        return self.done

    def last_job(self):
        self.jobs["end_half"] = half_exchange_job(self.full, self.FIRST)
        return self.jobs["end_half"]

    def take_last(self):
        self.done.update(zip(self.FIRST, self.jobs["end_half"].landed))


class NoComm:
    def job(self, tag, gb=None):
        return None

    def landed(self, tag, wb=None, ws=None):
        pass

    def first_update(self, gb):
        pass

    def finish(self, gb):
        return gb


SMALL = ("rel_bias", "g_mix", "w_short_conv", "g_attn_out", "g_conv_out", "g_xattn", "g_mem", "g_ffn",
         "w_ffn_conv", "b_ffn_conv", "g_final")
SMALL_SHARDED = ("w_short_conv", "w_ffn_conv")


def kernel(x, mem, rel_bias, g_mix, w_in, w_short_conv, g_attn_out, g_conv_out, w_out, g_xattn, g_mem, w_xq, w_xk, w_xv, w_xo, g_ffn, w_up, w_ffn_conv, b_ffn_conv, w_down, g_final, loss_target, m_rel_bias, m_g_mix, m_w_in, m_w_short_conv, m_g_attn_out, m_g_conv_out, m_w_out, m_g_xattn, m_g_mem, m_w_xq, m_w_xk, m_w_xv, m_w_xo, m_g_ffn, m_w_up, m_w_ffn_conv, m_b_ffn_conv, m_w_down, m_g_final, v_rel_bias, v_g_mix, v_w_in, v_w_short_conv, v_g_attn_out, v_g_conv_out, v_w_out, v_g_xattn, v_g_mem, v_w_xq, v_w_xk, v_w_xv, v_w_xo, v_g_ffn, v_w_up, v_w_ffn_conv, v_b_ffn_conv, v_w_down, v_g_final):
    names = ("rel_bias", "g_mix", "w_in", "w_short_conv", "g_attn_out", "g_conv_out", "w_out", "g_xattn", "g_mem", "w_xq",
             "w_xk", "w_xv", "w_xo", "g_ffn", "w_up", "w_ffn_conv", "b_ffn_conv", "w_down", "g_final")
    W = dict(zip(names, (rel_bias, g_mix, w_in, w_short_conv, g_attn_out, g_conv_out, w_out, g_xattn, g_mem, w_xq, w_xk, w_xv,
                         w_xo, g_ffn, w_up, w_ffn_conv, b_ffn_conv, w_down, g_final)))
    M = dict(zip(names, (m_rel_bias, m_g_mix, m_w_in, m_w_short_conv, m_g_attn_out, m_g_conv_out, m_w_out, m_g_xattn, m_g_mem,
                         m_w_xq, m_w_xk, m_w_xv, m_w_xo, m_g_ffn, m_w_up, m_w_ffn_conv, m_b_ffn_conv, m_w_down, m_g_final)))
    V = dict(zip(names, (v_rel_bias, v_g_mix, v_w_in, v_w_short_conv, v_g_attn_out, v_g_conv_out, v_w_out, v_g_xattn, v_g_mem,
                         v_w_xq, v_w_xk, v_w_xv, v_w_xo, v_g_ffn, v_w_up, v_w_ffn_conv, v_b_ffn_conv, v_w_down, v_g_final)))
    xi, yi, ci = lax.axis_index("x"), lax.axis_index("y"), lax.axis_index("c")
    mine = 2 * xi + yi
    mine_arr = jnp.reshape(mine, (1,)).astype(jnp.int32)
    sel_arr = jnp.stack([mine, ci]).astype(jnp.int32)
    conv_placed = []
    for n in SMALL_SHARDED:
        shard = W[n][0]
        conv_placed.append(lax.dynamic_update_slice(jnp.zeros((8, 4 * shard.shape[1]), F32), shard, (0, mine * shard.shape[1])))
    shards = {n: W[n][0] for n in BIG}
    placed = {}
    for group in same_shape_groups(BIG, shards):
        outs = place_shards([shards[n] for n in group], mine_arr, group[0] in COL_SHARDED, "place_" + group[0])
        placed.update(zip(group, outs))
    comm = StepComm(placed, conv_placed, sel_arr)
    ws = {n: W[n] for n in SMALL if n not in SMALL_SHARDED}

    def update(group, job=None):
        return adamw([W[n][0] for n in group], [comm.done[n] for n in group], [M[n][0] for n in group],
                     [V[n][0] for n in group], "adamw_" + group[0], job=job)

    comm.update_fn = lambda job: update(["w_up"], job)
    loss, grad_x, gfull, gs = local_step(x[0], mem[0], loss_target[0], None, ws, comm)

    def as2d(a):
        return a.reshape(1, -1) if a.ndim == 1 else a

    reduced = allreduce_small([as2d(gs[n]) for n in SMALL] + [loss], "reduce_small", job=comm.last_job())
    comm.take_last()
    gsm = dict(zip(SMALL, reduced[:-1]))
    loss = reduced[-1][0, 0]

    grads, delta, new_m, new_v = {}, {}, {}, {}
    for group in same_shape_groups(BIG, gfull):
        ds, nms, nvs = comm.first_result if group == ["w_up"] else update(group)
        for n, d, nm, nv in zip(group, ds, nms, nvs):
            grads[n], delta[n], new_m[n], new_v[n] = gfull[n][None], d[None], nm[None], nv[None]
    for n in SMALL_SHARDED:
        wid = W[n].shape[2]
        gsm[n] = lax.dynamic_slice(gsm[n], (0, mine * wid), (3, wid))

    def own(a, n):
        return a[0] if n in SMALL_SHARDED else as2d(a)

    d, nm, nv = adamw_multi([own(W[n], n) for n in SMALL], [gsm[n] for n in SMALL], [own(M[n], n) for n in SMALL],
                            [own(V[n], n) for n in SMALL], "adamw_small")
    for i, n in enumerate(SMALL):
        shape = W[n].shape
        grads[n], delta[n], new_m[n], new_v[n] = (a.reshape(shape) for a in (gsm[n], d[i], nm[i], nv[i]))
    return (loss, grad_x[None], *[grads[n] for n in names], *[delta[n] for n in names],
            *[new_m[n] for n in names], *[new_v[n] for n in names])
```

```python
import math

import numpy as np
import jax
import jax.numpy as jnp
from jax import lax
from jax.experimental import pallas as pl
from jax.experimental.pallas import tpu as pltpu

F32 = jnp.float32
BF16 = jnp.bfloat16
SDS = jax.ShapeDtypeStruct
MESH = pl.DeviceIdType.MESH

D_MODEL = 1024
ATTN_W = 512
N_HEADS = 8
HEAD_DIM = 64
HALF_HEAD = HEAD_DIM // 2
Q_SCALE = HEAD_DIM ** -0.5
BLK = 128
PATTERNS = ((128, 1), (512, 4), (2048, 16))
N_BUCKETS = 32
BUCKET_MAX_DIST = 2048
D_FF = 2816
N_MEM_HEADS = 4
MEM_HD = 256
EPS = 1e-6
NEG = -1e30
VMEM_LIMIT = 56 * 1024 * 1024

ADAM_LR, ADAM_B1, ADAM_B2, ADAM_EPS, ADAM_WD, ADAM_STEP = 0.001, 0.9, 0.999, 1e-08, 0.01, 10


def _cp(*sem):
    return pltpu.CompilerParams(dimension_semantics=sem, vmem_limit_bytes=VMEM_LIMIT)


def _dot(a, b):
    return jnp.dot(a, b, preferred_element_type=F32)


def _dot_nt(a, b):
    return lax.dot_general(a, b, (((1,), (1,)), ((), ())), preferred_element_type=F32)


def _dot_tn(a, b):
    return lax.dot_general(a, b, (((0,), (0,)), ((), ())), preferred_element_type=F32)


def _rsum(x):
    return jnp.sum(x, axis=1, keepdims=True)


def rmsnorm_fwd(x, g, name, job=None):
    S, Dm = x.shape
    tm = min(S, 512)

    def body(x_ref, g_ref, o_ref):
        xv = x_ref[...]
        r = lax.rsqrt(jnp.mean(xv * xv, axis=1, keepdims=True) + EPS)
        o_ref[...] = (xv * r * g_ref[...]).astype(o_ref.dtype)

    return _pcall(body, [x, g], grid=(S // tm,), name=name,
                  in_specs=[pl.BlockSpec((tm, Dm), lambda i: (i, 0)), pl.BlockSpec((1, Dm), lambda i: (0, 0))],
                  out_specs=[pl.BlockSpec((tm, Dm), lambda i: (i, 0))], out_shape=[SDS((S, Dm), BF16)],
                  sem=("parallel",), job=job)[0]


def rmsnorm_gain_grad(x, dh, name):
    S, Dm = x.shape
    tm = min(S, 512)

    def body(x_ref, dh_ref, dg_ref):
        xv = x_ref[...]
        xh = xv * lax.rsqrt(jnp.mean(xv * xv, axis=1, keepdims=True) + EPS)

        @pl.when(pl.program_id(0) == 0)
        def _():
            dg_ref[...] = jnp.zeros_like(dg_ref)

        dg_ref[...] += jnp.sum(dh_ref[...].astype(F32) * xh, axis=0, keepdims=True)

    row = pl.BlockSpec((tm, Dm), lambda i: (i, 0))
    return pl.pallas_call(
        body, grid=(S // tm,), name=name, in_specs=[row, row], out_specs=pl.BlockSpec((1, Dm), lambda i: (0, 0)),
        out_shape=SDS((1, Dm), F32), compiler_params=_cp("arbitrary"))(x, dh)


def _col_spec(bw, tn, rows, row_of, col_of):
    if bw is None:
        return pl.BlockSpec((rows, tn), lambda *g: (row_of(*g), col_of(*g)))
    if tn % bw == 0:
        return pl.BlockSpec((tn // bw, rows, bw), lambda *g: (col_of(*g), row_of(*g), 0))
    per = bw // tn
    return pl.BlockSpec((None, rows, tn), lambda *g: (col_of(*g) // per, row_of(*g), col_of(*g) % per))


def _read_cols(ref, bw, tn):
    if bw is None or tn % bw != 0:
        return ref[...]
    if tn == bw:
        return ref[0]
    return jnp.concatenate([ref[q] for q in range(tn // bw)], axis=1)


def _write_cols(ref, bw, tn, val):
    if bw is None or tn % bw != 0:
        ref[...] = val.astype(ref.dtype)
    else:
        for q in range(tn // bw):
            ref[q] = val[:, q * bw:(q + 1) * bw].astype(ref.dtype)


def mm_nn(a, b, name, *, out_dtype=F32, out_bw=None, tm=1024, tn=512, job=None):
    M, K = a.shape
    N = b.shape[1]
    tm = min(tm, M)

    def body(a_ref, b_ref, o_ref):
        _write_cols(o_ref, out_bw, tn, _dot(a_ref[...].astype(BF16), b_ref[...]))

    ri, ci = (lambda i, j: i), (lambda i, j: j)
    in_specs = [pl.BlockSpec((tm, K), lambda i, j: (i, 0)), pl.BlockSpec((K, tn), lambda i, j: (0, j))]
    oshape = (M, N) if out_bw is None else (N // out_bw, M, out_bw)
    return _pcall(body, [a, b], grid=(M // tm, N // tn), name=name, in_specs=in_specs,
                  out_specs=[_col_spec(out_bw, tn, tm, ri, ci)], out_shape=[SDS(oshape, out_dtype)],
                  sem=("parallel", "parallel"), job=job)[0]


def mm_nn_rows(a, b, res, g, name, *, target=None, tm=512):
    M, K = a.shape
    N = b.shape[1]

    def body(*refs):
        a_ref, b_ref, r_ref, g_ref = refs[:4]
        xv = r_ref[...] + _dot(a_ref[...].astype(BF16), b_ref[...])
        r = lax.rsqrt(jnp.mean(xv * xv, axis=1, keepdims=True) + EPS)
        xh = xv * r
        if target is None:
            x_ref, h_ref = refs[4:]
            x_ref[...] = xv
            h_ref[...] = (xh * g_ref[...]).astype(BF16)
            return
        t_ref, loss_ref, dx_ref, dg_ref = refs[4:]
        err = xh * g_ref[...] - t_ref[...]
        dy = err / N
        gd = dy * g_ref[...]
        dx_ref[...] = r * (gd - xh * jnp.mean(gd * xh, axis=1, keepdims=True))

        @pl.when(pl.program_id(0) == 0)
        def _():
            dg_ref[...] = jnp.zeros_like(dg_ref)
            loss_ref[...] = jnp.zeros_like(loss_ref)

        dg_ref[...] += _colsum(dy * xh)
        loss_ref[...] += 0.5 * _colsum(jnp.mean(err * err, axis=1, keepdims=True))

    row = pl.BlockSpec((tm, N), lambda i: (i, 0))
    vec = pl.BlockSpec((1, N), lambda i: (0, 0))
    in_specs = [pl.BlockSpec((tm, K), lambda i: (i, 0)), pl.BlockSpec((K, N), lambda i: (0, 0)), row, vec]
    if target is None:
        return pl.pallas_call(body, grid=(M // tm,), name=name, in_specs=in_specs, out_specs=[row, row],
                              out_shape=[SDS((M, N), F32), SDS((M, N), BF16)], compiler_params=_cp("parallel"))(a, b, res, g)
    one = pl.BlockSpec((1, 1), lambda i: (0, 0))
    return pl.pallas_call(body, grid=(M // tm,), name=name, in_specs=in_specs + [row], out_specs=[one, row, vec],
                          out_shape=[SDS((1, 1), F32), SDS((M, N), F32), SDS((1, N), F32)],
                          compiler_params=_cp("arbitrary"))(a, b, res, g, target)


def mm_nt(a, a_bw, b, name, *, res=None, out_dtype=F32, tm=1024, tn=512, job=None):
    if a_bw is None:
        M, K = a.shape
    else:
        M, K = a.shape[1], a.shape[0] * a_bw
    N = b.shape[0]
    tm = min(tm, M)

    def body(*refs):
        if res is None:
            a_ref, b_ref, o_ref = refs
        else:
            a_ref, b_ref, r_ref, o_ref = refs
        av = _read_cols(a_ref, a_bw, K).astype(BF16)
        acc = _dot_nt(av, b_ref[...])
        if res is not None:
            acc = acc + r_ref[...]
        o_ref[...] = acc.astype(o_ref.dtype)

    in_specs = [_col_spec(a_bw, K, tm, lambda i, j: i, lambda i, j: 0), pl.BlockSpec((tn, K), lambda i, j: (j, 0))]
    args = [a, b]
    if res is not None:
        in_specs.append(pl.BlockSpec((tm, tn), lambda i, j: (i, j)))
        args.append(res)
    return _pcall(body, args, grid=(M // tm, N // tn), name=name, in_specs=in_specs,
                  out_specs=[pl.BlockSpec((tm, tn), lambda i, j: (i, j))], out_shape=[SDS((M, N), out_dtype)],
                  sem=("parallel", "parallel"), job=job)[0]


def mm_nt_norm_bwd(a, a_bw, b, x, g, res, name, *, tm=512, job=None):
    if a_bw is None:
        M, K = a.shape
    else:
        M, K = a.shape[1], a.shape[0] * a_bw
    N = b.shape[0]

    def body(a_ref, b_ref, x_ref, g_ref, r_ref, dx_ref, dg_ref):
        dh = _dot_nt(_read_cols(a_ref, a_bw, K).astype(BF16), b_ref[...])
        xv = x_ref[...]
        r = lax.rsqrt(jnp.mean(xv * xv, axis=1, keepdims=True) + EPS)
        xh = xv * r
        gd = dh * g_ref[...]
        dx_ref[...] = r_ref[...] + r * (gd - xh * jnp.mean(gd * xh, axis=1, keepdims=True))

        @pl.when(pl.program_id(0) == 0)
        def _():
            dg_ref[...] = jnp.zeros_like(dg_ref)

        dg_ref[...] += jnp.sum(dh * xh, axis=0, keepdims=True)

    row = pl.BlockSpec((tm, N), lambda i: (i, 0))
    vec = pl.BlockSpec((1, N), lambda i: (0, 0))
    return _pcall(body, [a, b, x, g, res], grid=(M // tm,), name=name,
                  in_specs=[_col_spec(a_bw, K, tm, lambda i: i, lambda i: 0), pl.BlockSpec((N, K), lambda i: (0, 0)), row, vec, row],
                  out_specs=[row, vec], out_shape=[SDS((M, N), F32), SDS((1, N), F32)], sem=("arbitrary",), job=job)


def mm_tn(a, b, b_bw, name, *, shards=None, tm=1024, tn=1024, ts=1024, job=None):
    S, Ka = a.shape
    N = b.shape[1] if b_bw is None else b.shape[0] * b_bw
    ts = min(ts, S)
    tm = min(tm, Ka)

    def body(a_ref, b_ref, o_ref):
        @pl.when(pl.program_id(2) == 0)
        def _():
            o_ref[...] = jnp.zeros_like(o_ref)

        bv = _read_cols(b_ref, b_bw, tn).astype(BF16)
        o_ref[...] += _dot_tn(a_ref[...].astype(BF16), bv)

    in_specs = [pl.BlockSpec((ts, tm), lambda i, j, k: (k, i)),
                _col_spec(b_bw, tn, ts, lambda i, j, k: k, lambda i, j, k: j)]
    if shards is None:
        out_spec, oshape = pl.BlockSpec((tm, tn), lambda i, j, k: (i, j)), (Ka, N)
    else:
        per = (N // shards) // tn
        out_spec = pl.BlockSpec((None, tm, tn), lambda i, j, k: (j // per, i, j % per))
        oshape = (shards, Ka, N // shards)
    return _pcall(body, [a, b], grid=(Ka // tm, N // tn, S // ts), name=name, in_specs=in_specs, out_specs=[out_spec],
                  out_shape=[SDS(oshape, F32)], sem=("parallel", "parallel", "arbitrary"), job=job)[0]


KEYS = 2 * BLK


def _bucket_tables():
    out = np.zeros((3, BLK, KEYS), np.int32)
    qi = np.arange(BLK)[:, None]
    kj = np.arange(KEYS)[None, :]
    for p, (win, dil) in enumerate(PATTERNS):
        w = win // dil
        assert w == BLK
        steps = qi + w - kj
        valid = (steps >= 0) & (steps <= w)
        dist = np.clip(steps, 0, w) * dil
        dd = np.maximum(dist, 1).astype(np.float32)
        exact = N_BUCKETS // 2
        large = exact + (np.log(dd / np.float32(exact)) / np.float32(math.log(BUCKET_MAX_DIST / exact))
                         * np.float32(N_BUCKETS - exact)).astype(np.int32)
        large = np.minimum(large, N_BUCKETS - 1)
        out[p] = np.where(valid, np.where(dist < exact, dist, large), -1)
    return out


def bias_tables(rel_bias):
    bkt = jnp.asarray(_bucket_tables())

    def body(rb_ref, bkt_ref, o_ref):
        bk = bkt_ref[...]
        for h in range(N_HEADS):
            acc = jnp.full((BLK, KEYS), NEG, F32)
            for b in range(N_BUCKETS):
                acc = jnp.where(bk == b, rb_ref[h, b], acc)
            o_ref[h] = acc

    return pl.pallas_call(
        body, grid=(3,), name="bias_tables",
        in_specs=[pl.BlockSpec(memory_space=pltpu.SMEM),
                  pl.BlockSpec((None, BLK, KEYS), lambda p: (p, 0, 0))],
        out_specs=pl.BlockSpec((None, N_HEADS, BLK, KEYS), lambda p: (p, 0, 0, 0)),
        out_shape=SDS((3, N_HEADS, BLK, KEYS), F32), compiler_params=_cp("parallel"))(rel_bias, bkt)


def bias_tables_bwd(ds_sum, job=None):
    bkt = jnp.asarray(_bucket_tables())

    def body(ds_ref, bkt_ref, o_ref):
        h = pl.program_id(0)

        @pl.when(h == 0)
        def _():
            o_ref[...] = jnp.zeros_like(o_ref)

        rows = lax.broadcasted_iota(jnp.int32, (N_HEADS, N_BUCKETS), 0)
        cols = lax.broadcasted_iota(jnp.int32, (N_HEADS, N_BUCKETS), 1)
        acc = jnp.zeros((N_HEADS, N_BUCKETS), F32)
        for b in range(N_BUCKETS):
            t = jnp.zeros((BLK, KEYS), F32)
            for p in range(3):
                t = t + jnp.where(bkt_ref[p] == b, ds_ref[p], 0.0)
            tot = jnp.sum(_rsum(t), axis=0, keepdims=True)
            acc = acc + jnp.where((rows == h) & (cols == b), tot, 0.0)
        o_ref[...] += acc

    return _pcall(
        body, [ds_sum, bkt], grid=(N_HEADS,), name="bias_tables_bwd",
        in_specs=[pl.BlockSpec((3, None, BLK, KEYS), lambda h: (0, h, 0, 0)),
                  pl.BlockSpec((3, BLK, KEYS), lambda h: (0, 0, 0))],
        out_specs=[pl.BlockSpec((N_HEADS, N_BUCKETS), lambda h: (0, 0))],
        out_shape=[SDS((N_HEADS, N_BUCKETS), F32)], sem=("arbitrary",), job=job)[0]


def _rows(start, dil):
    if dil == 1:
        return pl.ds(pl.multiple_of(start, BLK), BLK)
    return pl.ds(start, BLK, stride=dil)


GRP = 8


def _group_rows(i, dil, S):
    nb = S // (BLK * dil)
    run = min(nb, GRP)
    chunks = nb // run
    res0, b0 = (i // chunks) * (GRP // run), (i % chunks) * run
    cur = [(b0 + j % run) * (BLK * dil) + res0 + j // run for j in range(GRP)]
    t = lax.broadcasted_iota(jnp.int32, (GRP, 1, 1), 0)
    if run == nb:
        before, pen = None, jnp.where(t % run == 0, NEG, 0.0)
    else:
        before = _rows(jnp.maximum(b0 - 1, 0) * (BLK * dil) + res0, dil)
        pen = jnp.where((t == 0) & (b0 == 0), NEG, 0.0)
    return [_rows(s, dil) for s in cur], before, pen


def _load_group(ref, rows):
    return jnp.stack([ref[r, :] for r in rows])


def _with_prev(ref, before, cur_blocks):
    first = cur_blocks[:1] if before is None else ref[before, :][None].astype(cur_blocks.dtype)
    return jnp.concatenate([first, cur_blocks[:-1]], axis=0)


def _bdot_nt(a, b):
    return lax.dot_general(a, b, (((2,), (2,)), ((0,), (0,))), preferred_element_type=F32)


def _bdot(a, b):
    return lax.dot_general(a, b, (((2,), (1,)), ((0,), (0,))), preferred_element_type=F32)


def _bdot_tn(a, b):
    return lax.dot_general(a, b, (((1,), (1,)), ((0,), (0,))), preferred_element_type=F32)


def _lsum(x):
    return jnp.sum(x, axis=-1, keepdims=True)


def _widen(src, dst):
    S = src.shape[1]

    def chunk(i, carry):
        rows = pl.ds(pl.multiple_of(i * 512, 512), 512)
        for a in range(3):
            dst[a, rows, :] = src[a, rows, :].astype(F32)
        return carry

    lax.fori_loop(0, S // 512, chunk, 0)


def attn_fwd(qkv, bias, job=None):
    S = qkv.shape[3]
    nblk = S // BLK
    car = Carried(job, 2, 2, 3, 4)

    def body(*refs):
        (qkv_in, bias_ref, o_ref, lse_ref, qkv_ref, part_o, part_ml), jrefs = car.split(refs)
        car.phase(0, pl.program_id(0), jrefs)
        _widen(qkv_in, qkv_ref)
        lane = lax.broadcasted_iota(jnp.int32, (GRP, BLK, BLK), 2)
        lo = lane < HEAD_DIM
        masks = (lo, jnp.logical_not(lo))
        lov = lax.broadcasted_iota(jnp.int32, (GRP, KEYS, BLK), 2) < HEAD_DIM
        vmasks = (lov, jnp.logical_not(lov))
        prev_keys = lax.broadcasted_iota(jnp.int32, (1, 1, KEYS), 2) < BLK
        q_ref, k_ref, v_ref = qkv_ref.at[0], qkv_ref.at[1], qkv_ref.at[2]
        targets = [(o_ref, lse_ref)] + [(part_o.at[t], part_ml.at[t]) for t in range(len(PATTERNS) - 1)]
        for p, (_, dil) in enumerate(PATTERNS):
            def step(i, carry, p=p, dil=dil):
                rc, before, pen = _group_rows(i, dil, S)
                q2 = _load_group(q_ref, rc) * Q_SCALE
                kc = _load_group(k_ref, rc).astype(BF16)
                keys = jnp.concatenate([_with_prev(k_ref, before, kc), kc], axis=1)
                vc = _load_group(v_ref, rc)
                vals = jnp.concatenate([_with_prev(v_ref, before, vc), vc], axis=1)
                pen = jnp.where(prev_keys, pen, 0.0)
                pv = jnp.zeros((GRP, BLK, BLK), F32)
                ms, ls = [], []
                for h in range(2):
                    qh = jnp.where(masks[h], q2, 0.0).astype(BF16)
                    s = _bdot_nt(qh, keys) + bias_ref[p, h][None] + pen
                    mn = jnp.max(s, axis=-1, keepdims=True)
                    pr = jnp.exp(s - mn)
                    pv = pv + _bdot(pr.astype(BF16), jnp.where(vmasks[h], vals, 0.0).astype(BF16))
                    ms.append(mn)
                    ls.append(_lsum(pr))
                ml_t = jnp.where(lo, jnp.where(lane < HALF_HEAD, ms[0], ls[0]),
                                 jnp.where(lane < HEAD_DIM + HALF_HEAD, ms[1], ls[1]))
                acc_ref, ml_ref = targets[p]
                for j, r in enumerate(rc):
                    acc_ref[r, :] = pv[j]
                    ml_ref[r, :] = ml_t[j]
                return carry

            lax.fori_loop(0, nblk // GRP, step, 0)

        def merge(i, carry):
            rows = pl.ds(pl.multiple_of(i * 512, 512), 512)
            is_m = (lax.broadcasted_iota(jnp.int32, (512, BLK), 1) & HALF_HEAD) == 0
            accs = [ref[rows, :] for ref, _ in targets]
            mls = [ref[rows, :] for _, ref in targets]
            mx = [jnp.where(is_m, ml, pltpu.roll(ml, HALF_HEAD, 1)) for ml in mls]
            dn = [jnp.where(is_m, pltpu.roll(ml, BLK - HALF_HEAD, 1), ml) for ml in mls]
            top = jnp.maximum(jnp.maximum(mx[0], mx[1]), mx[2])
            ws_ = [jnp.exp(m - top) for m in mx]
            den = ws_[0] * dn[0] + ws_[1] * dn[1] + ws_[2] * dn[2]
            o_ref[rows, :] = (ws_[0] * accs[0] + ws_[1] * accs[1] + ws_[2] * accs[2]) / den
            lse_ref[rows, :] = top + jnp.log(den)
            return carry

        lax.fori_loop(0, S // 512, merge, 0)
        car.phase(1, pl.program_id(0), jrefs)
        car.phase(2, pl.program_id(0), jrefs)

    outs = pl.pallas_call(
        body, grid=(4,), name="attn_fwd",
        in_specs=[pl.BlockSpec((None, 3, None, S, BLK), lambda g: (0, 0, g, 0, 0)),
                  pl.BlockSpec((3, 2, BLK, KEYS), lambda g: (0, g, 0, 0))] + car.in_specs(),
        out_specs=[pl.BlockSpec((None, S, BLK), lambda g: (g, 0, 0)),
                   pl.BlockSpec((None, S, BLK), lambda g: (g, 0, 0))] + car.out_specs(),
        out_shape=[SDS((4, S, BLK), F32), SDS((4, S, BLK), F32)] + car.out_shapes(),
        input_output_aliases=car.aliases(),
        scratch_shapes=[pltpu.VMEM((3, S, BLK), F32), pltpu.VMEM((2, S, BLK), F32), pltpu.VMEM((2, S, BLK), F32)] + car.sems(),
        compiler_params=_cp("arbitrary"))(qkv, bias, *car.args())
    if job is not None:
        job.landed = list(outs[2:])
    return outs[0], outs[1]


def attn_bwd(qkv, bias, o, lse, do, job=None):
    S = qkv.shape[3]
    nblk = S // BLK
    car = Carried(job, 5, 2, 4, 4)

    def body(*refs):
        (qkv_in, bias_ref, o_ref, lse_ref, do_ref, dqkv_out, ds_ref, qkv_ref, dqkv_ref, last_ref, st_ref), jrefs = car.split(refs)
        car.phase(0, pl.program_id(0), jrefs)
        _widen(qkv_in, qkv_ref)

        def stats(i, carry):
            rows = pl.ds(pl.multiple_of(i * 512, 512), 512)
            ln = lax.broadcasted_iota(jnp.int32, (512, BLK), 1)
            prod = do_ref[rows, :] * o_ref[rows, :]
            d0 = _rsum(jnp.where(ln < HEAD_DIM, prod, 0.0))
            d1 = _rsum(jnp.where(ln < HEAD_DIM, 0.0, prod))
            st_ref[rows, :] = jnp.where((ln & HALF_HEAD) == 0, lse_ref[rows, :], jnp.where(ln < HEAD_DIM, d0, d1))
            return carry

        lax.fori_loop(0, S // 512, stats, 0)
        lane = lax.broadcasted_iota(jnp.int32, (GRP, BLK, BLK), 2)
        lo = lane < HEAD_DIM
        masks = (lo, jnp.logical_not(lo))
        lov = lax.broadcasted_iota(jnp.int32, (GRP, KEYS, BLK), 2) < HEAD_DIM
        vmasks = (lov, jnp.logical_not(lov))
        prev_keys = lax.broadcasted_iota(jnp.int32, (1, 1, KEYS), 2) < BLK
        ds_ref[...] = jnp.zeros_like(ds_ref)
        q_ref, k_ref, v_ref = qkv_ref.at[0], qkv_ref.at[1], qkv_ref.at[2]
        last = len(PATTERNS) - 1
        for p, (_, dil) in enumerate(PATTERNS):
            def step(i, carry, p=p, dil=dil):
                tgt = last_ref if p == last else dqkv_ref
                dq_ref, dk_ref, dv_ref = tgt.at[0], tgt.at[1], tgt.at[2]
                rc, before, pen = _group_rows(i, dil, S)
                q2 = _load_group(q_ref, rc) * Q_SCALE
                kc = _load_group(k_ref, rc).astype(BF16)
                keys = jnp.concatenate([_with_prev(k_ref, before, kc), kc], axis=1)
                vc = _load_group(v_ref, rc).astype(BF16)
                vals = jnp.concatenate([_with_prev(v_ref, before, vc), vc], axis=1)
                pen = jnp.where(prev_keys, pen, 0.0)
                dot = _load_group(do_ref, rc)
                st = _load_group(st_ref, rc)
                dq = jnp.zeros((GRP, BLK, BLK), F32)
                dkeys = jnp.zeros((GRP, KEYS, BLK), F32)
                dvals = jnp.zeros((GRP, KEYS, BLK), F32)
                for h in range(2):
                    qh = jnp.where(masks[h], q2, 0.0).astype(BF16)
                    doh = jnp.where(masks[h], dot, 0.0).astype(BF16)
                    lh = st[:, :, HEAD_DIM * h:HEAD_DIM * h + 1]
                    delta = st[:, :, HEAD_DIM * h + HALF_HEAD:HEAD_DIM * h + HALF_HEAD + 1]
                    pr = jnp.exp(_bdot_nt(qh, keys) + bias_ref[p, h][None] + pen - lh)
                    ds = pr * (_bdot_nt(doh, vals) - delta)
                    ds_ref[p, h] += jnp.sum(ds, axis=0)
                    dsb = ds.astype(BF16)
                    dq = dq + jnp.where(masks[h], _bdot(dsb, keys), 0.0)
                    dkeys = dkeys + _bdot_tn(dsb, qh)
                    dvals = dvals + _bdot_tn(pr.astype(BF16), doh)
                dkp, dkc, dvp, dvc = dkeys[:, :BLK], dkeys[:, BLK:], dvals[:, :BLK], dvals[:, BLK:]
                none = jnp.zeros((1, BLK, BLK), F32)
                dkc = dkc + jnp.concatenate([dkp[1:], none], axis=0)
                dvc = dvc + jnp.concatenate([dvp[1:], none], axis=0)
                for j in range(GRP):
                    if p == 1:
                        dq_ref[rc[j], :] += dq[j] * Q_SCALE
                        dk_ref[rc[j], :] += dkc[j]
                        dv_ref[rc[j], :] += dvc[j]
                    else:
                        dq_ref[rc[j], :] = dq[j] * Q_SCALE
                        dk_ref[rc[j], :] = dkc[j]
                        dv_ref[rc[j], :] = dvc[j]
                if before is not None:
                    dk_ref[before, :] += dkp[0]
                    dv_ref[before, :] += dvp[0]
                return carry

            lax.fori_loop(0, nblk // GRP, step, 0)

        def narrow(i, carry):
            rows = pl.ds(pl.multiple_of(i * 512, 512), 512)
            for a in range(3):
                dqkv_out[a, rows, :] = (dqkv_ref[a, rows, :] + last_ref[a, rows, :]).astype(BF16)
            return carry

        lax.fori_loop(0, S // 512, narrow, 0)
        car.phase(1, pl.program_id(0), jrefs)
        car.phase(2, pl.program_id(0), jrefs)

    blk = pl.BlockSpec((None, S, BLK), lambda g: (g, 0, 0))
    outs = pl.pallas_call(
        body, grid=(4,), name="attn_bwd",
        in_specs=[pl.BlockSpec((None, 3, None, S, BLK), lambda g: (0, 0, g, 0, 0)),
                  pl.BlockSpec((3, 2, BLK, KEYS), lambda g: (0, g, 0, 0)), blk, blk, blk] + car.in_specs(),
        out_specs=[pl.BlockSpec((None, 3, None, S, BLK), lambda g: (0, 0, g, 0, 0)),
                   pl.BlockSpec((3, 2, BLK, KEYS), lambda g: (0, g, 0, 0))] + car.out_specs(),
        out_shape=[SDS((2, 3, 4, S, BLK), BF16), SDS((3, N_HEADS, BLK, KEYS), F32)] + car.out_shapes(),
        input_output_aliases=car.aliases(),
        scratch_shapes=[pltpu.VMEM((3, S, BLK), F32)] * 3 + [pltpu.VMEM((S, BLK), F32)] + car.sems(),
        compiler_params=_cp("arbitrary"))(qkv, bias, o, lse, do, *car.args())
    if job is not None:
        job.landed = list(outs[2:])
    return outs[0], outs[1]


def _shift_down(u, k, halo):
    n = u.shape[0]
    row = lax.broadcasted_iota(jnp.int32, u.shape, 0)
    out = pltpu.roll(u, k, 0)
    hn = halo.shape[0]
    for j in range(k):
        out = jnp.where(row == j, halo[hn - k + j:hn - k + j + 1, :], out)
    return out


def _shift_up(u, k, halo):
    n = u.shape[0]
    row = lax.broadcasted_iota(jnp.int32, u.shape, 0)
    out = pltpu.roll(u, n - k, 0)
    for j in range(k):
        out = jnp.where(row == n - k + j, halo[j:j + 1, :], out)
    return out


def _conv3(u, halo, w0, w1, w2):
    return _shift_down(u, 2, halo) * w0 + _shift_down(u, 1, halo) * w1 + u * w2


def _colsum(x):
    return jnp.sum(x, axis=0, keepdims=True)


def _mixer_specs(S, tm):
    conv = pl.BlockSpec((None, 12, tm, BLK), lambda i: (1, 0, i, 0))
    halo = pl.BlockSpec((None, 12, 16, BLK), lambda i: (1, 0, jnp.maximum(i * (tm // 16) - 1, 0), 0))
    ob = pl.BlockSpec((4, tm, BLK), lambda i: (0, i, 0))
    return conv, halo, ob


def _mixer_recompute(i, o_ref, pr_ref, ph_ref, w_ref):
    ob = [o_ref[q] for q in range(4)]
    gb = [pr_ref[q].astype(F32) for q in range(4)]
    gc = [pr_ref[4 + q].astype(F32) for q in range(4)]
    xi = [pr_ref[8 + q].astype(F32) for q in range(4)]
    keep = jnp.where(i > 0, 1.0, 0.0)
    u = [gc[q] * xi[q] for q in range(4)]
    hu = [ph_ref[4 + q].astype(F32) * ph_ref[8 + q].astype(F32) * keep for q in range(4)]
    w = [[w_ref[k:k + 1, q * BLK:(q + 1) * BLK] for k in range(3)] for q in range(4)]
    cv = [_conv3(u[q], hu[q], *w[q]) for q in range(4)]
    return ob, gb, gc, xi, u, hu, cv, w


def _rms_blocks(blocks):
    ss = sum(_rsum(b * b) for b in blocks)
    return lax.rsqrt(ss / (BLK * len(blocks)) + EPS)


def mixer_post_fwd(o, proj, w_sc, g_a, g_c):
    S = o.shape[1]
    tm = 512

    def body(o_ref, pr_ref, ph_ref, w_ref, ga_ref, gc_ref, m_ref):
        i = pl.program_id(0)
        ob, gb, _, _, _, _, cv, _ = _mixer_recompute(i, o_ref, pr_ref, ph_ref, w_ref)
        conv = [gb[q] * cv[q] for q in range(4)]
        ra, rc = _rms_blocks(ob), _rms_blocks(conv)
        for q in range(4):
            sl = slice(q * BLK, (q + 1) * BLK)
            m_ref[:, q * BLK:(q + 1) * BLK] = (ob[q] * ra * ga_ref[:, sl]).astype(BF16)
            m_ref[:, ATTN_W + q * BLK:ATTN_W + (q + 1) * BLK] = (conv[q] * rc * gc_ref[:, sl]).astype(BF16)

    conv_s, halo_s, ob_s = _mixer_specs(S, tm)
    full = lambda r, c: pl.BlockSpec((r, c), lambda i: (0, 0))
    return pl.pallas_call(
        body, grid=(S // tm,), name="mixer_post_fwd",
        in_specs=[ob_s, conv_s, halo_s, full(3, 512), full(1, 512), full(1, 512)],
        out_specs=pl.BlockSpec((tm, D_MODEL), lambda i: (i, 0)), out_shape=SDS((S, D_MODEL), BF16),
        compiler_params=_cp("parallel"))(o, proj, proj, w_sc, g_a, g_c)


def mixer_post_bwd_a(dmixed, o, proj, w_sc, g_a, g_c, job=None):
    S = o.shape[1]
    tm = 512

    def body(dm_ref, o_ref, pr_ref, ph_ref, w_ref, ga_ref, gc_ref, do_ref, dgb_ref, dcv_ref, dga_ref, dgc_ref):
        i = pl.program_id(0)
        ob, gb, _, _, _, _, cv, _ = _mixer_recompute(i, o_ref, pr_ref, ph_ref, w_ref)
        conv = [gb[q] * cv[q] for q in range(4)]
        ra, rc = _rms_blocks(ob), _rms_blocks(conv)

        @pl.when(i == 0)
        def _():
            dga_ref[...] = jnp.zeros_like(dga_ref)
            dgc_ref[...] = jnp.zeros_like(dgc_ref)

        for blocks, r, g_ref, off, dg_ref, is_attn in ((ob, ra, ga_ref, 0, dga_ref, True), (conv, rc, gc_ref, ATTN_W, dgc_ref, False)):
            xh = [blocks[q] * r for q in range(4)]
            dm = [dm_ref[:, off + q * BLK:off + (q + 1) * BLK].astype(F32) for q in range(4)]
            gd = [dm[q] * g_ref[:, q * BLK:(q + 1) * BLK] for q in range(4)]
            mean = sum(_rsum(gd[q] * xh[q]) for q in range(4)) / (4 * BLK)
            for q in range(4):
                dg_ref[:, q * BLK:(q + 1) * BLK] += _colsum(dm[q] * xh[q])
                dx = r * (gd[q] - xh[q] * mean)
                if is_attn:
                    do_ref[q] = dx
                else:
                    dgb_ref[q] = dx * cv[q]
                    dcv_ref[q] = dx * gb[q]

    conv_s, halo_s, ob_s = _mixer_specs(S, tm)
    full = lambda r, c: pl.BlockSpec((r, c), lambda i: (0, 0))
    return _pcall(
        body, [dmixed, o, proj, proj, w_sc, g_a, g_c], grid=(S // tm,), name="mixer_post_bwd_a",
        in_specs=[pl.BlockSpec((tm, D_MODEL), lambda i: (i, 0)), ob_s, conv_s, halo_s, full(3, 512), full(1, 512), full(1, 512)],
        out_specs=[ob_s, ob_s, ob_s, full(1, 512), full(1, 512)],
        out_shape=[SDS((4, S, BLK), F32)] * 3 + [SDS((1, 512), F32)] * 2, sem=("arbitrary",), job=job)


def mixer_post_bwd_b(dproj, dgb, dcv, proj, w_sc):
    S = proj.shape[2]
    tm = 512
    last = S // 8 - 1

    def body(dp_in, dgb_ref, dcv_ref, dn_ref, pr_ref, w_ref, dp_ref, dw_ref):
        i = pl.program_id(0)
        keep_next = jnp.where(i < pl.num_programs(0) - 1, 1.0, 0.0)

        @pl.when(i == 0)
        def _():
            dw_ref[...] = jnp.zeros_like(dw_ref)

        for q in range(4):
            sl = slice(q * BLK, (q + 1) * BLK)
            gc, xi = pr_ref[4 + q].astype(F32), pr_ref[8 + q].astype(F32)
            u = gc * xi
            dcv = dcv_ref[q]
            dn = dn_ref[q] * keep_next
            d1, d2 = _shift_up(dcv, 1, dn), _shift_up(dcv, 2, dn)
            w0, w1, w2 = (w_ref[k:k + 1, sl] for k in range(3))
            du = dcv * w2 + d1 * w1 + d2 * w0
            dw_ref[0:1, sl] += _colsum(d2 * u)
            dw_ref[1:2, sl] += _colsum(d1 * u)
            dw_ref[2:3, sl] += _colsum(dcv * u)
            dp_ref[q] = dgb_ref[q].astype(BF16)
            dp_ref[4 + q] = (du * xi).astype(BF16)
            dp_ref[8 + q] = (du * gc).astype(BF16)

    conv_s, _, ob_s = _mixer_specs(S, tm)
    nxt = pl.BlockSpec((4, 8, BLK), lambda i: (0, jnp.minimum((i + 1) * (tm // 8), last), 0))
    full = lambda r, c: pl.BlockSpec((r, c), lambda i: (0, 0))
    return pl.pallas_call(
        body, grid=(S // tm,), name="mixer_post_bwd_b",
        in_specs=[pl.BlockSpec(memory_space=pl.ANY), ob_s, ob_s, nxt, conv_s, full(3, 512)],
        out_specs=[conv_s, full(3, 512)],
        out_shape=[SDS(dproj.shape, BF16), SDS((3, 512), F32)],
        input_output_aliases={0: 0}, compiler_params=_cp("arbitrary"))(dproj, dgb, dcv, dcv, proj, w_sc)


def xattn_fwd(q, k, v):
    S = q.shape[0]
    tm = 512
    scale = MEM_HD ** -0.5

    def body(q_ref, k_ref, v_ref, o_ref):
        for h in range(N_MEM_HEADS):
            sl = slice(h * MEM_HD, (h + 1) * MEM_HD)
            s = _dot_nt(q_ref[:, sl], k_ref[:, sl]) * scale
            p = jnp.exp(s - jnp.max(s, axis=1, keepdims=True))
            p = p / _rsum(p)
            o_ref[:, sl] = _dot(p.astype(BF16), v_ref[:, sl]).astype(BF16)

    row = pl.BlockSpec((tm, D_MODEL), lambda i: (i, 0))
    kv = pl.BlockSpec(k.shape, lambda i: (0, 0))
    return pl.pallas_call(body, grid=(S // tm,), name="xattn_fwd", in_specs=[row, kv, kv], out_specs=row,
                          out_shape=SDS((S, D_MODEL), BF16), compiler_params=_cp("parallel"))(q, k, v)


def xattn_bwd(q, k, v, do):
    S = q.shape[0]
    tm = 512
    scale = MEM_HD ** -0.5

    def body(q_ref, k_ref, v_ref, do_ref, dq_ref, dk_ref, dv_ref):
        @pl.when(pl.program_id(0) == 0)
        def _():
            dk_ref[...] = jnp.zeros_like(dk_ref)
            dv_ref[...] = jnp.zeros_like(dv_ref)

        for h in range(N_MEM_HEADS):
            sl = slice(h * MEM_HD, (h + 1) * MEM_HD)
            qh, kh, vh, doh = q_ref[:, sl], k_ref[:, sl], v_ref[:, sl], do_ref[:, sl]
            s = _dot_nt(qh, kh) * scale
            p = jnp.exp(s - jnp.max(s, axis=1, keepdims=True))
            p = p / _rsum(p)
            dp = _dot_nt(doh, vh)
            ds = (p * (dp - _rsum(p * dp)) * scale).astype(BF16)
            dq_ref[:, sl] = _dot(ds, kh).astype(BF16)
            dk_ref[:, sl] += _dot_tn(ds, qh)
            dv_ref[:, sl] += _dot_tn(p.astype(BF16), doh)

    row = pl.BlockSpec((tm, D_MODEL), lambda i: (i, 0))
    kv = pl.BlockSpec(k.shape, lambda i: (0, 0))
    return pl.pallas_call(body, grid=(S // tm,), name="xattn_bwd", in_specs=[row, kv, kv, row], out_specs=[row, kv, kv],
                          out_shape=[SDS((S, D_MODEL), BF16), SDS(k.shape, F32), SDS(k.shape, F32)],
                          compiler_params=_cp("arbitrary"))(q, k, v, do)


FFN_TM, FFN_TC = 256, 1408


def _ffn_specs(S):
    tm, tc = FFN_TM, FFN_TC
    blk = pl.BlockSpec((2, tm, tc), lambda j, i: (0, i, j))
    nxt = pl.BlockSpec((2, 16, tc), lambda j, i: (0, jnp.minimum((i + 1) * (tm // 16), S // 16 - 1), j))
    wsp = pl.BlockSpec((2, 3, tc), lambda j, i: (0, 0, j))
    bsp = pl.BlockSpec((2, 1, tc), lambda j, i: (0, 0, j))
    return blk, nxt, wsp, bsp


def up_act_fwd(h, w_up, w_fc, b_fc, job=None):
    S, K = h.shape
    tm, tc = FFN_TM, FFN_TC
    nj = D_FF // tc

    def body(h_ref, wg_ref, wv_ref, w_ref, b_ref, pre_ref, gv_ref, a_ref, halo):
        i = pl.program_id(1)
        hv = h_ref[...]
        res = []
        for half, wt_ref in enumerate((wg_ref, wv_ref)):
            u = _dot(hv, wt_ref[...])
            hu = jnp.where(i > 0, halo[half], 0.0)
            halo[half] = u[tm - 8:, :]
            w0, w1, w2 = (w_ref[half, k:k + 1, :] for k in range(3))
            pre_ref[half] = u.astype(BF16)
            res.append(_conv3(u, hu, w0, w1, w2) + b_ref[half])
        g, v = res
        gv_ref[0] = g.astype(BF16)
        gv_ref[1] = v.astype(BF16)
        a_ref[...] = (g * jax.nn.sigmoid(g) * v).astype(BF16)

    blk = pl.BlockSpec((2, tm, tc), lambda j, i: (0, i, j))
    return _pcall(
        body, [h, w_up, w_up, w_fc, b_fc], grid=(nj, S // tm), name="up_act_fwd",
        in_specs=[pl.BlockSpec((tm, K), lambda j, i: (i, 0)), pl.BlockSpec((K, tc), lambda j, i: (0, j)),
                  pl.BlockSpec((K, tc), lambda j, i: (0, j + nj)), pl.BlockSpec((2, 3, tc), lambda j, i: (0, 0, j)),
                  pl.BlockSpec((2, 1, tc), lambda j, i: (0, 0, j))],
        out_specs=[blk, blk, pl.BlockSpec((tm, tc), lambda j, i: (i, j))],
        out_shape=[SDS((2, S, D_FF), BF16), SDS((2, S, D_FF), BF16), SDS((S, D_FF), BF16)],
        scratch=[pltpu.VMEM((2, 8, tc), F32)], sem=("parallel", "arbitrary"), job=job)


def ffn_act_bwd(dx, w_down, up, up_pre, w_fc, job=None):
    S = up.shape[1]

    def act_grads(da, g, v):
        sg = jax.nn.sigmoid(g)
        return da * v * (sg * (1.0 + g * (1.0 - sg))), da * g * sg

    def body(dx_ref, dxn_ref, wd_ref, gv_ref, gvn_ref, up_ref, w_ref, dp_ref, dw_ref, db_ref):
        i = pl.program_id(1)
        keep_next = jnp.where(i < pl.num_programs(1) - 1, 1.0, 0.0)

        @pl.when(i == 0)
        def _():
            dw_ref[...] = jnp.zeros_like(dw_ref)
            db_ref[...] = jnp.zeros_like(db_ref)

        wd = wd_ref[...]
        da = _dot_nt(dx_ref[...].astype(BF16), wd)
        dan = _dot_nt(dxn_ref[...].astype(BF16), wd) * keep_next
        here = act_grads(da, gv_ref[0].astype(F32), gv_ref[1].astype(F32))
        after = act_grads(dan, gvn_ref[0].astype(F32), gvn_ref[1].astype(F32))
        for half in range(2):
            d, dn = here[half], after[half]
            u = up_ref[half].astype(F32)
            d1, d2 = _shift_up(d, 1, dn), _shift_up(d, 2, dn)
            w0, w1, w2 = (w_ref[half, k:k + 1, :] for k in range(3))
            dp_ref[half] = (d * w2 + d1 * w1 + d2 * w0).astype(BF16)
            dw_ref[half, 0:1, :] += _colsum(d2 * u)
            dw_ref[half, 1:2, :] += _colsum(d1 * u)
            dw_ref[half, 2:3, :] += _colsum(d * u)
            db_ref[half] += _colsum(d)

    tm, tc = FFN_TM, FFN_TC
    blk, nxt, wsp, bsp = _ffn_specs(S)
    K = dx.shape[1]
    rows = pl.BlockSpec((tm, K), lambda j, i: (i, 0))
    rows_next = pl.BlockSpec((16, K), lambda j, i: (jnp.minimum((i + 1) * (tm // 16), S // 16 - 1), 0))
    return _pcall(body, [dx, dx, w_down, up, up, up_pre, w_fc], grid=(D_FF // tc, S // tm), name="ffn_act_bwd",
                  in_specs=[rows, rows_next, pl.BlockSpec((tc, K), lambda j, i: (j, 0)), blk, nxt, blk, wsp],
                  out_specs=[blk, wsp, bsp],
                  out_shape=[SDS((2, S, D_FF), BF16), SDS((2, 3, D_FF), F32), SDS((2, 1, D_FF), F32)],
                  sem=("parallel", "arbitrary"), job=job)


def local_step(x, mem, target, wb, ws, comm=None):
    S = x.shape[0]
    assert S % 2048 == 0
    if comm is None:
        comm = NoComm()
    else:
        wb = {}
    ws = dict(ws)

    bias = bias_tables(ws["rel_bias"])
    h1 = rmsnorm_fwd(x, ws["g_mix"], "norm_mix", job=comm.job("norm_mix"))
    comm.landed("norm_mix", wb, ws)
    w_fc = ws["w_ffn_conv"].reshape(3, 2, D_FF).transpose(1, 0, 2)
    b_fc = ws["b_ffn_conv"].reshape(2, 1, D_FF)
    proj = mm_nn(h1, wb["w_in"], "proj", out_dtype=BF16, out_bw=BLK, tn=768, job=comm.job("proj"))
    comm.landed("proj", wb)
    qkv = proj.reshape(2, 3, 4, S, BLK)
    o, lse = attn_fwd(qkv, bias, job=comm.job("attn_fwd"))
    comm.landed("attn_fwd", wb)
    proj4 = proj.reshape(2, 12, S, BLK)
    mixed = mixer_post_fwd(o, proj4, ws["w_short_conv"], ws["g_attn_out"], ws["g_conv_out"])
    x1, h2 = mm_nn_rows(mixed, wb["w_out"], x, ws["g_xattn"], "out_proj")
    mem_n = rmsnorm_fwd(mem, ws["g_mem"], "norm_mem")
    xq = mm_nn(h2, wb["w_xq"], "xq", out_dtype=BF16)
    xk = mm_nn(mem_n, wb["w_xk"], "xk", out_dtype=BF16, tn=1024)
    xv = mm_nn(mem_n, wb["w_xv"], "xv", out_dtype=BF16, tn=1024)
    xo = xattn_fwd(xq, xk, xv)
    x2, h3 = mm_nn_rows(xo, wb["w_xo"], x1, ws["g_ffn"], "xo_proj")
    up_pre, up, act = up_act_fwd(h3, wb["w_up"], w_fc, b_fc, job=comm.job("up_act_fwd"))
    comm.landed("up_act_fwd", wb)
    loss, dx3, dg_final = mm_nn_rows(act, wb["w_down"], x2, ws["g_final"].reshape(1, -1), "down_proj", target=target)

    gb, gs = {}, {"g_final": dg_final}
    gb["w_down"] = mm_tn(act, dx3, None, "dw_down", tm=1408, tn=1024)
    d_up_pre, dw_fc, db_fc = ffn_act_bwd(dx3, wb["w_down"], up, up_pre, w_fc, job=comm.job("ffn_act_bwd", gb))
    comm.landed("ffn_act_bwd")
    gs["b_ffn_conv"] = db_fc.reshape(1, 2 * D_FF)
    gs["w_ffn_conv"] = dw_fc.transpose(1, 0, 2).reshape(3, 2 * D_FF)
    gb["w_up"] = mm_tn(h3, d_up_pre, D_FF, "dw_up", shards=4, tn=1408, job=comm.job("dw_up"))
    comm.landed("dw_up")
    dx2, gs["g_ffn"] = mm_nt_norm_bwd(d_up_pre, D_FF, wb["w_up"], x2, ws["g_ffn"], dx3, "d_h3", tm=256,
                                      job=comm.job("d_h3", gb))
    comm.landed("d_h3")
    gb["w_xo"] = mm_tn(xo, dx2, None, "dw_xo")
    dxo = mm_nt(dx2, None, wb["w_xo"], "d_xo", out_dtype=BF16)
    dxq, dxk, dxv = xattn_bwd(xq, xk, xv, dxo)
    gb["w_xq"] = mm_tn(h2, dxq, None, "dw_xq")
    gb["w_xk"] = mm_tn(mem_n, dxk, None, "dw_xk", tn=1024)
    gb["w_xv"] = mm_tn(mem_n, dxv, None, "dw_xv", tn=1024)
    dmem_n = mm_nt(dxk, None, wb["w_xk"], "d_memk", tn=1024)
    dmem_n = mm_nt(dxv, None, wb["w_xv"], "d_memv", res=dmem_n, tn=1024)
    gs["g_mem"] = rmsnorm_gain_grad(mem, dmem_n, "norm_mem_bwd")
    dx1, gs["g_xattn"] = mm_nt_norm_bwd(dxq, None, wb["w_xq"], x1, ws["g_xattn"], dx2, "d_h2")
    gb["w_out"] = mm_tn(mixed, dx1, None, "dw_out")
    dmixed = mm_nt(dx1, None, wb["w_out"], "d_mixed", out_dtype=BF16)
    do, dgb, dcv, gs["g_attn_out"], gs["g_conv_out"] = mixer_post_bwd_a(
        dmixed, o, proj4, ws["w_short_conv"], ws["g_attn_out"], ws["g_conv_out"], job=comm.job("mixer_post_bwd_a", gb))
    comm.landed("mixer_post_bwd_a")
    dproj, ds_sum = attn_bwd(qkv, bias, o, lse, do, job=comm.job("attn_bwd"))
    comm.landed("attn_bwd")
    dproj, gs["w_short_conv"] = mixer_post_bwd_b(dproj.reshape(2, 12, S, BLK), dgb, dcv, proj4, ws["w_short_conv"])
    dproj = dproj.reshape(24, S, BLK)
    gb["w_in"] = mm_tn(h1, dproj, BLK, "dw_in", shards=4, tn=768, job=comm.job("dw_in"))
    comm.landed("dw_in")
    gs["rel_bias"] = bias_tables_bwd(ds_sum, job=comm.job("bias_tables_bwd", gb))
    comm.landed("bias_tables_bwd")
    grad_x, gs["g_mix"] = mm_nt_norm_bwd(dproj, BLK, wb["w_in"], x, ws["g_mix"], dx1, "d_h1", tm=256,
                                         job=comm.job("d_h1", gb))
    comm.landed("d_h1")
    return loss, grad_x, comm.finish(gb), gs


def _adam_update(w_ref, g_ref, m_ref, v_ref, d_ref, nm_ref, nv_ref):
    gv = g_ref[...]
    mn = ADAM_B1 * m_ref[...] + (1.0 - ADAM_B1) * gv
    vn = ADAM_B2 * v_ref[...] + (1.0 - ADAM_B2) * (gv * gv)
    m_hat = mn / (1.0 - ADAM_B1 ** ADAM_STEP)
    v_hat = vn / (1.0 - ADAM_B2 ** ADAM_STEP)
    d_ref[...] = -ADAM_LR * (m_hat / (jnp.sqrt(v_hat) + ADAM_EPS) + ADAM_WD * w_ref[...])
    nm_ref[...] = mn
    nv_ref[...] = vn


def adamw(ws, gs, ms, vs, name, job=None):
    n = len(ws)
    R, C = ws[0].shape
    tr = R // 4

    def body(*refs):
        for p in range(n):
            _adam_update(*(refs[q * n + p] for q in range(7)))

    blk = pl.BlockSpec((tr, C), lambda i: (i, 0))
    outs = _pcall(body, [*ws, *gs, *ms, *vs], grid=(4,), name=name, in_specs=[blk] * (4 * n), out_specs=[blk] * (3 * n),
                  out_shape=[SDS((R, C), F32)] * (3 * n), sem=("parallel",), job=job)
    return outs[:n], outs[n:2 * n], outs[2 * n:]


BIG = ("w_in", "w_out", "w_xq", "w_xk", "w_xv", "w_xo", "w_up", "w_down")
COL_SHARDED = ("w_in", "w_up")
ANY = pl.BlockSpec(memory_space=pl.ANY)


def _place():
    x, y, c = lax.axis_index("x"), lax.axis_index("y"), lax.axis_index("c")
    chips = [(1 - x, y), (x, 1 - y), (1 - x, 1 - y)]
    return x, y, c, chips


def _window(full, name, R, C, shard, half):
    r0, nr = (0, R) if half is None else (half * (R // 2), R // 2)
    if name in COL_SHARDED:
        return full.at[pl.ds(r0, nr), pl.ds(shard * C, C)]
    return full.at[pl.ds(shard * R + r0, nr), :]


def same_shape_groups(names, arrays):
    groups = {}
    for n in names:
        groups.setdefault(tuple(arrays[n].shape), []).append(n)
    return list(groups.values())


def place_shards(ws, mine_arr, col, name):
    n = len(ws)
    R, C = ws[0].shape

    def body(s_ref, *refs):
        for p in range(n):
            refs[n + p][...] = refs[p][...].astype(BF16)

    grid_spec = pltpu.PrefetchScalarGridSpec(
        num_scalar_prefetch=1, grid=(1,), in_specs=[pl.BlockSpec((R, C), lambda i, s: (0, 0))] * n,
        out_specs=[pl.BlockSpec((R, C), (lambda i, s: (0, s[0])) if col else (lambda i, s: (s[0], 0)))] * n)
    return pl.pallas_call(body, grid_spec=grid_spec, name=name,
                          out_shape=[SDS((R, 4 * C) if col else (4 * R, C), BF16)] * n,
                          compiler_params=_cp("arbitrary"))(mine_arr, *ws)


def _gather_jobs(names, shapes):
    nw = len(names)

    def start(full, sems):
        send, recv, fsend, frecv = sems
        x, y, c, chips = _place()
        mine = 2 * x + y
        for w, n in enumerate(names):
            R, C = shapes[w]
            own = _window(full[w], n, R, C, mine, c)
            for j, chip in enumerate(chips):
                pltpu.make_async_remote_copy(src_ref=own, dst_ref=own, send_sem=send.at[w, j], recv_sem=recv.at[w, j],
                                             device_id=(*chip, c), device_id_type=MESH).start()

    def mid(full, sems, lo=0, hi=nw):
        send, recv, fsend, frecv = sems
        x, y, c, chips = _place()
        sib = (x, y, 1 - c)
        for w, n in list(enumerate(names))[lo:hi]:
            R, C = shapes[w]
            for j, chip in enumerate(chips):
                landed = _window(full[w], n, R, C, 2 * chip[0] + chip[1], c)
                pltpu.make_async_remote_copy(src_ref=landed, dst_ref=landed, send_sem=send.at[w, j], recv_sem=recv.at[w, j],
                                             device_id=(*chip, c), device_id_type=MESH).wait_recv()
                pltpu.make_async_remote_copy(src_ref=landed, dst_ref=landed, send_sem=fsend.at[w, j],
                                             recv_sem=frecv.at[w, j], device_id=sib, device_id_type=MESH).start()

    def finish(full, sems):
        send, recv, fsend, frecv = sems
        x, y, c, chips = _place()
        mine = 2 * x + y
        sib = (x, y, 1 - c)
        for w, n in enumerate(names):
            R, C = shapes[w]
            own = _window(full[w], n, R, C, mine, c)
            for j, chip in enumerate(chips):
                landed = _window(full[w], n, R, C, 2 * chip[0] + chip[1], c)
                other = _window(full[w], n, R, C, 2 * chip[0] + chip[1], 1 - c)
                pltpu.make_async_remote_copy(src_ref=other, dst_ref=other, send_sem=fsend.at[w, j], recv_sem=frecv.at[w, j],
                                             device_id=sib, device_id_type=MESH).wait_recv()
                pltpu.make_async_remote_copy(src_ref=own, dst_ref=own, send_sem=send.at[w, j], recv_sem=recv.at[w, j],
                                             device_id=(*chip, c), device_id_type=MESH).wait_send()
                pltpu.make_async_remote_copy(src_ref=landed, dst_ref=landed, send_sem=fsend.at[w, j],
                                             recv_sem=frecv.at[w, j], device_id=sib, device_id_type=MESH).wait_send()

    return start, mid, finish, [pltpu.SemaphoreType.DMA((nw, 3))] * 4


class CommJob:
    def __init__(self, ins, out_shapes, inplace, start, finish, sems, mid=None):
        self.ins, self.out_shapes, self.inplace = list(ins), list(out_shapes), inplace
        self.start, self.mid, self.finish, self.sems = start, mid, finish, list(sems)

    def run(self, phase, ins, outs, sems):
        if phase == 0:
            self.start(ins, outs, sems)
        elif phase == 1:
            if self.mid is not None:
                self.mid(ins, outs, sems)
        else:
            self.finish(ins, outs, sems)


class Carried:
    def __init__(self, job, n_in, n_out, n_scratch, steps):
        self.job, self.n_in, self.n_out, self.n_scratch, self.steps = job, n_in, n_out, n_scratch, steps
        self.nji = len(job.ins) if job else 0
        self.njo = len(job.out_shapes) if job else 0

    def in_specs(self):
        return [ANY] * self.nji

    def out_specs(self):
        return [ANY] * self.njo

    def out_shapes(self):
        return list(self.job.out_shapes) if self.job else []

    def aliases(self):
        if not (self.job and self.job.inplace):
            return {}
        return {self.n_in + i: self.n_out + i for i in range(self.nji)}

    def sems(self):
        return list(self.job.sems) if self.job else []

    def args(self):
        return list(self.job.ins) if self.job else []

    def split(self, refs):
        a = self.n_in
        b = a + self.nji
        c = b + self.n_out
        d = c + self.njo
        e = d + self.n_scratch
        return refs[:a] + refs[b:c] + refs[d:e], (refs[a:b], refs[c:d], refs[e:])

    def phase(self, phase, step, jrefs):
        if self.job is None:
            return
        at = {0: 0, 1: max(self.steps - 2, 0), 2: self.steps - 1}[phase]

        @pl.when(step == at)
        def _():
            self.job.run(phase, *jrefs)


def _pcall(body, args, *, grid, in_specs, out_specs, out_shape, name, sem, scratch=(), aliases=None, job=None):
    n_in, n_out = len(args), len(out_shape)
    if job is None:
        return pl.pallas_call(
            body, grid=grid, in_specs=list(in_specs), out_specs=list(out_specs), out_shape=list(out_shape), name=name,
            scratch_shapes=list(scratch), input_output_aliases=dict(aliases or {}), compiler_params=_cp(*sem))(*args)
    total = int(np.prod(grid))
    car = Carried(job, n_in, n_out, len(scratch), total)

    def wrapped(*refs):
        main, jrefs = car.split(refs)
        lin = pl.program_id(0)
        for ax in range(1, len(grid)):
            lin = lin * grid[ax] + pl.program_id(ax)
        car.phase(0, lin, jrefs)
        body(*main)
        car.phase(1, lin, jrefs)
        car.phase(2, lin, jrefs)

    res = pl.pallas_call(
        wrapped, grid=grid, in_specs=list(in_specs) + car.in_specs(), out_specs=list(out_specs) + car.out_specs(),
        out_shape=list(out_shape) + car.out_shapes(), name=name, scratch_shapes=list(scratch) + car.sems(),
        input_output_aliases={**dict(aliases or {}), **car.aliases()},
        compiler_params=_cp(*(["arbitrary"] * len(grid))))(*args, *car.args())
    job.landed = list(res[n_out:])
    return list(res[:n_out])


def gather_job(placed, names):
    shapes = []
    for n in names:
        R, C = placed[n].shape
        shapes.append((R, C // 4) if n in COL_SHARDED else (R // 4, C))
    start, mid, finish, sems = _gather_jobs(names, shapes)
    arrays = [placed[n] for n in names]
    early = max(len(names) - 1, 1)

    def last(i, o, s):
        mid(o, s, early, len(names))
        finish(o, s)

    return CommJob(arrays, [SDS(a.shape, a.dtype) for a in arrays], True,
                   lambda i, o, s: start(o, s), last, sems, mid=lambda i, o, s: mid(o, s, 0, early))


def gather_rows_job(placed):
    widths = [a.shape[1] // 4 for a in placed]

    def copies(outs, sems):
        x, y, c, chips = _place()
        mine = 2 * x + y
        cps = []
        for w, ref in enumerate(outs):
            own = ref.at[:, pl.ds(mine * widths[w], widths[w])]
            for j, chip in enumerate(chips):
                theirs = ref.at[:, pl.ds((2 * chip[0] + chip[1]) * widths[w], widths[w])]
                kw = dict(send_sem=sems[0].at[w, j], recv_sem=sems[1].at[w, j], device_id=(*chip, c), device_id_type=MESH)
                cps.append((pltpu.make_async_remote_copy(src_ref=own, dst_ref=own, **kw),
                            pltpu.make_async_remote_copy(src_ref=theirs, dst_ref=theirs, **kw)))
        return cps

    def start(ins, outs, sems):
        for send, _ in copies(outs, sems):
            send.start()

    def finish(ins, outs, sems):
        for send, recv in copies(outs, sems):
            recv.wait_recv()
            send.wait_send()

    return CommJob(placed, [SDS(a.shape, a.dtype) for a in placed], True, start, finish,
                   [pltpu.SemaphoreType.DMA((len(placed), 3))] * 2)


def merge_jobs(a, b):
    assert a.inplace and b.inplace
    na, nb_, nsa = len(a.ins), len(b.ins), len(a.sems)

    def phase(which):
        def run(ins, outs, sems):
            for job, i, o, s in ((a, ins[:na], outs[:na], sems[:nsa]), (b, ins[na:], outs[na:], sems[nsa:])):
                fn = getattr(job, which)
                if fn is not None:
                    fn(i, o, s)
        return run

    merged = CommJob(a.ins + b.ins, a.out_shapes + b.out_shapes, True, phase("start"), phase("finish"), a.sems + b.sems,
                     mid=phase("mid"))
    merged.parts = (a, b, na)
    return merged


def split_landed(merged):
    a, b, na = merged.parts
    a.landed, b.landed = merged.landed[:na], merged.landed[na:]


def pair_exchange_job(grads, names):
    shapes = [grads[n].shape for n in names]

    def copies(ins, outs, sems):
        x, y, c, _ = _place()
        return [pltpu.make_async_remote_copy(
            src_ref=ins[w].at[:, pl.ds((1 - c) * (shapes[w][1] // 2), shapes[w][1] // 2), :], dst_ref=outs[w],
            send_sem=sems[0].at[w], recv_sem=sems[1].at[w], device_id=(x, y, 1 - c), device_id_type=MESH)
            for w in range(len(names))]

    def start(ins, outs, sems):
        for cp in copies(ins, outs, sems):
            cp.start()

    def finish(ins, outs, sems):
        for cp in copies(ins, outs, sems):
            cp.wait()

    return CommJob([grads[n] for n in names], [SDS((4, s[1] // 2, s[2]), F32) for s in shapes], False, start, finish,
                   [pltpu.SemaphoreType.DMA((len(names),))] * 2)


def chip_exchange_job(p16, names):
    shapes = [p16[n].shape for n in names]

    def copies(ins, outs, sems):
        x, y, c, chips = _place()
        return [pltpu.make_async_remote_copy(
            src_ref=ins[w].at[2 * chip[0] + chip[1]], dst_ref=outs[w].at[j],
            send_sem=sems[0].at[w, j], recv_sem=sems[1].at[w, j], device_id=(*chip, c), device_id_type=MESH)
            for w in range(len(names)) for j, chip in enumerate(chips)]

    def start(ins, outs, sems):
        for cp in copies(ins, outs, sems):
            cp.start()

    def finish(ins, outs, sems):
        for cp in copies(ins, outs, sems):
            cp.wait()

    return CommJob([p16[n] for n in names], [SDS((3,) + tuple(s[1:]), BF16) for s in shapes], False, start, finish,
                   [pltpu.SemaphoreType.DMA((len(names), 3))] * 2)


def half_exchange_job(full, names):
    shapes = [full[n].shape for n in names]

    def copies(outs, sems):
        x, y, c, _ = _place()
        cps = []
        for w in range(len(names)):
            Rh = shapes[w][0] // 2
            rows = outs[w].at[pl.ds(c * Rh, Rh), :]
            other = outs[w].at[pl.ds((1 - c) * Rh, Rh), :]
            cps.append((pltpu.make_async_remote_copy(src_ref=rows, dst_ref=rows, send_sem=sems[0].at[w], recv_sem=sems[1].at[w],
                                                     device_id=(x, y, 1 - c), device_id_type=MESH),
                        pltpu.make_async_remote_copy(src_ref=other, dst_ref=other, send_sem=sems[0].at[w], recv_sem=sems[1].at[w],
                                                     device_id=(x, y, 1 - c), device_id_type=MESH)))
        return cps

    def start(ins, outs, sems):
        for send, _ in copies(outs, sems):
            send.start()

    def finish(ins, outs, sems):
        for send, recv in copies(outs, sems):
            recv.wait_recv()
            send.wait_send()

    arrays = [full[n] for n in names]
    return CommJob(arrays, [SDS(a.shape, a.dtype) for a in arrays], True, start, finish,
                   [pltpu.SemaphoreType.DMA((len(names),))] * 2)


def allreduce_small(parts, name, job=None):
    n = len(parts)
    nji = len(job.ins) if job else 0
    njo = len(job.out_shapes) if job else 0

    def body(*refs):
        ins, jins = refs[:n], refs[n:n + nji]
        outs, jouts = refs[n + nji:2 * n + nji], refs[2 * n + nji:2 * n + nji + njo]
        bufs = refs[2 * n + nji + njo:3 * n + nji + njo]
        send, recv = refs[3 * n + nji + njo:3 * n + nji + njo + 2]
        jsems = refs[3 * n + nji + njo + 2:]
        x, y, c, _ = _place()
        me = 4 * x + 2 * y + c
        if job is not None:
            job.run(0, jins, jouts, jsems)
        sends = []
        for p in range(n):
            for k in range(1, 8):
                peer = (x ^ (k >> 2), y ^ ((k >> 1) & 1), c ^ (k & 1))
                cp = pltpu.make_async_remote_copy(src_ref=ins[p], dst_ref=bufs[p].at[k], send_sem=send.at[p, k - 1],
                                                  recv_sem=recv.at[p, k - 1], device_id=peer, device_id_type=MESH)
                cp.start()
                sends.append(cp)
            bufs[p][0] = ins[p][...]
        for cp in sends:
            cp.wait_recv()
        for p in range(n):
            acc = bufs[p][me]
            for d in range(1, 8):
                acc = acc + bufs[p][d ^ me]
            outs[p][...] = acc
        for cp in sends:
            cp.wait_send()
        if job is not None:
            job.run(1, jins, jouts, jsems)
            job.run(2, jins, jouts, jsems)

    vm = pl.BlockSpec(memory_space=pltpu.VMEM)
    res = pl.pallas_call(
        body, name=name, in_specs=[vm] * n + [ANY] * nji, out_specs=[vm] * n + [ANY] * njo,
        out_shape=[SDS(a.shape, F32) for a in parts] + (list(job.out_shapes) if job else []),
        input_output_aliases={n + i: n + i for i in range(nji)} if (job and job.inplace) else {},
        scratch_shapes=[pltpu.VMEM((8,) + tuple(a.shape), F32) for a in parts] + [pltpu.SemaphoreType.DMA((n, 7))] * 2
        + (list(job.sems) if job else []),
    )(*parts, *(job.ins if job else []))
    if job is not None:
        job.landed = list(res[n:])
    return list(res[:n])


def adamw_multi(ws, gs, ms, vs, name):
    n = len(ws)

    def body(*refs):
        for p in range(n):
            _adam_update(*(refs[q * n + p] for q in range(7)))

    vm = pl.BlockSpec(memory_space=pltpu.VMEM)
    outs = pl.pallas_call(body, name=name, in_specs=[vm] * (4 * n), out_specs=[vm] * (3 * n),
                          out_shape=[SDS(a.shape, F32) for a in ws] * 3)(*ws, *gs, *ms, *vs)
    return outs[:n], outs[n:2 * n], outs[2 * n:]


def pair_sum(gs, xhs, sel_arr, name):
    n = len(gs)
    _, R, C = gs[0].shape
    Rh = R // 2

    def body(sel_ref, *refs):
        for p in range(n):
            g_ref, x_ref, p32_ref, p16_ref = (refs[q * n + p] for q in range(4))
            s = g_ref[...] + x_ref[...]
            p16_ref[...] = s.astype(BF16)

            @pl.when(pl.program_id(0) == sel_ref[0])
            def _():
                p32_ref[...] = s

    same = pl.BlockSpec((None, Rh, C), lambda s, sel: (s, 0, 0))
    grid_spec = pltpu.PrefetchScalarGridSpec(
        num_scalar_prefetch=1, grid=(4,),
        in_specs=[pl.BlockSpec((None, Rh, C), lambda s, sel: (s, sel[1], 0))] * n + [same] * n,
        out_specs=[pl.BlockSpec((Rh, C), lambda s, sel: (0, 0))] * n + [same] * n)
    outs = pl.pallas_call(body, grid_spec=grid_spec, name=name,
                          out_shape=[SDS((Rh, C), F32)] * n + [SDS((4, Rh, C), BF16)] * n,
                          compiler_params=_cp("arbitrary"))(sel_arr, *gs, *xhs)
    return outs[:n], outs[n:]


def chip_sum(p32s, ys, sel_arr, name):
    n = len(p32s)
    Rh, C = p32s[0].shape

    def body(s_ref, *refs):
        for p in range(n):
            p_ref, y_ref, o_ref = (refs[q * n + p] for q in range(3))
            o_ref[...] = ((p_ref[...] + y_ref[0].astype(F32)) + y_ref[1].astype(F32)) + y_ref[2].astype(F32)

    grid_spec = pltpu.PrefetchScalarGridSpec(
        num_scalar_prefetch=1, grid=(1,),
        in_specs=[pl.BlockSpec((Rh, C), lambda i, s: (0, 0))] * n + [pl.BlockSpec((3, Rh, C), lambda i, s: (0, 0, 0))] * n,
        out_specs=[pl.BlockSpec((Rh, C), lambda i, s: (s[1], 0))] * n)
    return pl.pallas_call(body, grid_spec=grid_spec, name=name, out_shape=[SDS((2 * Rh, C), F32)] * n,
                          compiler_params=_cp("arbitrary"))(sel_arr, *p32s, *ys)


class StepComm:
    FIRST = ("w_in",)
    SQUARE = ("w_out", "w_xq", "w_xk", "w_xv", "w_xo")
    FFN = ("w_up", "w_down")
    PLAN = {
        "proj": ("gather", SQUARE[:3]), "attn_fwd": ("gather", SQUARE[3:] + FFN[:1]), "up_act_fwd": ("gather", FFN[1:]),
        "ffn_act_bwd": ("pair", FFN[1:]), "dw_up": ("chip", FFN[1:]),
        "d_h3": ("pair", FFN[:1]), "mixer_post_bwd_a": ("pair", SQUARE), "attn_bwd": ("chip", FFN[:1] + SQUARE),
        "dw_in": ("half", FFN + SQUARE), "bias_tables_bwd": ("pair", FIRST), "d_h1": ("chip", FIRST),
    }

    def __init__(self, placed, conv_placed, sel_arr):
        self.placed, self.conv_placed, self.sel_arr = placed, conv_placed, sel_arr
        self.g, self.p32, self.p16, self.full, self.done, self.jobs = {}, {}, {}, {}, {}, {}

    def job(self, tag, gb=None):
        if tag == "norm_mix":
            self.jobs[tag] = merge_jobs(gather_job(self.placed, self.FIRST), gather_rows_job(self.conv_placed))
            return self.jobs[tag]
        kind, names = self.PLAN[tag]
        if kind == "gather":
            j = gather_job(self.placed, names)
        elif kind == "pair":
            for n in names:
                self.g[n] = gb[n] if gb[n].ndim == 3 else gb[n].reshape(4, gb[n].shape[0] // 4, gb[n].shape[1])
            j = pair_exchange_job(self.g, names)
        elif kind == "chip":
            j = chip_exchange_job(self.p16, names)
        else:
            j = half_exchange_job(self.full, names)
        self.jobs[tag] = j
        return j

    def landed(self, tag, wb=None, ws=None):
        if tag == "norm_mix":
            split_landed(self.jobs[tag])
            first, conv, _ = self.jobs[tag].parts
            wb.update(zip(self.FIRST, first.landed))
            ws.update((n, a[:3]) for n, a in zip(SMALL_SHARDED, conv.landed))
            return
        kind, names = self.PLAN[tag]
        got = dict(zip(names, self.jobs[tag].landed))
        if kind == "gather":
            wb.update(got)
        elif kind == "half":
            self.done.update(got)
        for group in same_shape_groups(names, got):
            if kind == "pair":
                p32s, p16s = pair_sum([self.g[n] for n in group], [got[n] for n in group], self.sel_arr, "pair_sum_" + group[0])
                self.p32.update(zip(group, p32s))
                self.p16.update(zip(group, p16s))
            elif kind == "chip":
                fulls = chip_sum([self.p32[n] for n in group], [got[n] for n in group], self.sel_arr, "chip_sum_" + group[0])
                self.full.update(zip(group, fulls))

    def finish(self, gb):
        return self.done

    def last_job(self):
        self.jobs["end_half"] = half_exchange_job(self.full, self.FIRST)
        return self.jobs["end_half"]

    def take_last(self):
        self.done.update(zip(self.FIRST, self.jobs["end_half"].landed))


class NoComm:
    def job(self, tag, gb=None):
        return None

    def landed(self, tag, wb=None, ws=None):
        pass

    def finish(self, gb):
        return gb


SMALL = ("rel_bias", "g_mix", "w_short_conv", "g_attn_out", "g_conv_out", "g_xattn", "g_mem", "g_ffn",
         "w_ffn_conv", "b_ffn_conv", "g_final")
SMALL_SHARDED = ("w_short_conv", "w_ffn_conv")


def kernel(x, mem, rel_bias, g_mix, w_in, w_short_conv, g_attn_out, g_conv_out, w_out, g_xattn, g_mem, w_xq, w_xk, w_xv, w_xo, g_ffn, w_up, w_ffn_conv, b_ffn_conv, w_down, g_final, loss_target, m_rel_bias, m_g_mix, m_w_in, m_w_short_conv, m_g_attn_out, m_g_conv_out, m_w_out, m_g_xattn, m_g_mem, m_w_xq, m_w_xk, m_w_xv, m_w_xo, m_g_ffn, m_w_up, m_w_ffn_conv, m_b_ffn_conv, m_w_down, m_g_final, v_rel_bias, v_g_mix, v_w_in, v_w_short_conv, v_g_attn_out, v_g_conv_out, v_w_out, v_g_xattn, v_g_mem, v_w_xq, v_w_xk, v_w_xv, v_w_xo, v_g_ffn, v_w_up, v_w_ffn_conv, v_b_ffn_conv, v_w_down, v_g_final):
    names = ("rel_bias", "g_mix", "w_in", "w_short_conv", "g_attn_out", "g_conv_out", "w_out", "g_xattn", "g_mem", "w_xq",
             "w_xk", "w_xv", "w_xo", "g_ffn", "w_up", "w_ffn_conv", "b_ffn_conv", "w_down", "g_final")
    W = dict(zip(names, (rel_bias, g_mix, w_in, w_short_conv, g_attn_out, g_conv_out, w_out, g_xattn, g_mem, w_xq, w_xk, w_xv,
                         w_xo, g_ffn, w_up, w_ffn_conv, b_ffn_conv, w_down, g_final)))
    M = dict(zip(names, (m_rel_bias, m_g_mix, m_w_in, m_w_short_conv, m_g_attn_out, m_g_conv_out, m_w_out, m_g_xattn, m_g_mem,
                         m_w_xq, m_w_xk, m_w_xv, m_w_xo, m_g_ffn, m_w_up, m_w_ffn_conv, m_b_ffn_conv, m_w_down, m_g_final)))
    V = dict(zip(names, (v_rel_bias, v_g_mix, v_w_in, v_w_short_conv, v_g_attn_out, v_g_conv_out, v_w_out, v_g_xattn, v_g_mem,
                         v_w_xq, v_w_xk, v_w_xv, v_w_xo, v_g_ffn, v_w_up, v_w_ffn_conv, v_b_ffn_conv, v_w_down, v_g_final)))
    xi, yi, ci = lax.axis_index("x"), lax.axis_index("y"), lax.axis_index("c")
    mine = 2 * xi + yi
    mine_arr = jnp.reshape(mine, (1,)).astype(jnp.int32)
    sel_arr = jnp.stack([mine, ci]).astype(jnp.int32)
    conv_placed = []
    for n in SMALL_SHARDED:
        shard = W[n][0]
        conv_placed.append(lax.dynamic_update_slice(jnp.zeros((8, 4 * shard.shape[1]), F32), shard, (0, mine * shard.shape[1])))
    shards = {n: W[n][0] for n in BIG}
    placed = {}
    for group in same_shape_groups(BIG, shards):
        outs = place_shards([shards[n] for n in group], mine_arr, group[0] in COL_SHARDED, "place_" + group[0])
        placed.update(zip(group, outs))
    comm = StepComm(placed, conv_placed, sel_arr)
    ws = {n: W[n] for n in SMALL if n not in SMALL_SHARDED}

    loss, grad_x, gfull, gs = local_step(x[0], mem[0], loss_target[0], None, ws, comm)

    def as2d(a):
        return a.reshape(1, -1) if a.ndim == 1 else a

    reduced = allreduce_small([as2d(gs[n]) for n in SMALL] + [loss], "reduce_small", job=comm.last_job())
    comm.take_last()
    gsm = dict(zip(SMALL, reduced[:-1]))
    loss = reduced[-1][0, 0]

    grads, delta, new_m, new_v = {}, {}, {}, {}
    for group in same_shape_groups(BIG, gfull):
        ds, nms, nvs = adamw([W[n][0] for n in group], [gfull[n] for n in group], [M[n][0] for n in group],
                             [V[n][0] for n in group], "adamw_" + group[0])
        for n, d, nm, nv in zip(group, ds, nms, nvs):
            grads[n], delta[n], new_m[n], new_v[n] = gfull[n][None], d[None], nm[None], nv[None]
    for n in SMALL_SHARDED:
        wid = W[n].shape[2]
        gsm[n] = lax.dynamic_slice(gsm[n], (0, mine * wid), (3, wid))

    def own(a, n):
        return a[0] if n in SMALL_SHARDED else as2d(a)

    d, nm, nv = adamw_multi([own(W[n], n) for n in SMALL], [gsm[n] for n in SMALL], [own(M[n], n) for n in SMALL],
                            [own(V[n], n) for n in SMALL], "adamw_small")
    for i, n in enumerate(SMALL):
        shape = W[n].shape
        grads[n], delta[n], new_m[n], new_v[n] = (a.reshape(shape) for a in (gsm[n], d[i], nm[i], nv[i]))
    return (loss, grad_x[None], *[grads[n] for n in names], *[delta[n] for n in names],
            *[new_m[n] for n in names], *[new_v[n] for n in names])
```

```python
import math

import numpy as np
import jax
import jax.numpy as jnp
from jax import lax
from jax.experimental import pallas as pl
from jax.experimental.pallas import tpu as pltpu

F32 = jnp.float32
BF16 = jnp.bfloat16
SDS = jax.ShapeDtypeStruct
MESH = pl.DeviceIdType.MESH

D_MODEL = 1024
ATTN_W = 512
N_HEADS = 8
HEAD_DIM = 64
HALF_HEAD = HEAD_DIM // 2
Q_SCALE = HEAD_DIM ** -0.5
BLK = 128
PATTERNS = ((128, 1), (512, 4), (2048, 16))
N_BUCKETS = 32
BUCKET_MAX_DIST = 2048
D_FF = 2816
N_MEM_HEADS = 4
MEM_HD = 256
EPS = 1e-6
NEG = -1e30
VMEM_LIMIT = 56 * 1024 * 1024

ADAM_LR, ADAM_B1, ADAM_B2, ADAM_EPS, ADAM_WD, ADAM_STEP = 0.001, 0.9, 0.999, 1e-08, 0.01, 10


def _cp(*sem):
    return pltpu.CompilerParams(dimension_semantics=sem, vmem_limit_bytes=VMEM_LIMIT)


def _dot(a, b):
    return jnp.dot(a, b, preferred_element_type=F32)


def _dot_nt(a, b):
    return lax.dot_general(a, b, (((1,), (1,)), ((), ())), preferred_element_type=F32)


def _dot_tn(a, b):
    return lax.dot_general(a, b, (((0,), (0,)), ((), ())), preferred_element_type=F32)


def _rsum(x):
    return jnp.sum(x, axis=1, keepdims=True)


def rmsnorm_fwd(x, g, name, job=None):
    S, Dm = x.shape
    tm = min(S, 512)

    def body(x_ref, g_ref, o_ref):
        xv = x_ref[...]
        r = lax.rsqrt(jnp.mean(xv * xv, axis=1, keepdims=True) + EPS)
        o_ref[...] = (xv * r * g_ref[...]).astype(o_ref.dtype)

    return _pcall(body, [x, g], grid=(S // tm,), name=name,
                  in_specs=[pl.BlockSpec((tm, Dm), lambda i: (i, 0)), pl.BlockSpec((1, Dm), lambda i: (0, 0))],
                  out_specs=[pl.BlockSpec((tm, Dm), lambda i: (i, 0))], out_shape=[SDS((S, Dm), BF16)],
                  sem=("parallel",), job=job)[0]


def rmsnorm_gain_grad(x, dh, name):
    S, Dm = x.shape
    tm = min(S, 512)

    def body(x_ref, dh_ref, dg_ref):
        xv = x_ref[...]
        xh = xv * lax.rsqrt(jnp.mean(xv * xv, axis=1, keepdims=True) + EPS)

        @pl.when(pl.program_id(0) == 0)
        def _():
            dg_ref[...] = jnp.zeros_like(dg_ref)

        dg_ref[...] += jnp.sum(dh_ref[...].astype(F32) * xh, axis=0, keepdims=True)

    row = pl.BlockSpec((tm, Dm), lambda i: (i, 0))
    return pl.pallas_call(
        body, grid=(S // tm,), name=name, in_specs=[row, row], out_specs=pl.BlockSpec((1, Dm), lambda i: (0, 0)),
        out_shape=SDS((1, Dm), F32), compiler_params=_cp("arbitrary"))(x, dh)


def _col_spec(bw, tn, rows, row_of, col_of):
    if bw is None:
        return pl.BlockSpec((rows, tn), lambda *g: (row_of(*g), col_of(*g)))
    if tn % bw == 0:
        return pl.BlockSpec((tn // bw, rows, bw), lambda *g: (col_of(*g), row_of(*g), 0))
    per = bw // tn
    return pl.BlockSpec((None, rows, tn), lambda *g: (col_of(*g) // per, row_of(*g), col_of(*g) % per))


def _read_cols(ref, bw, tn):
    if bw is None or tn % bw != 0:
        return ref[...]
    if tn == bw:
        return ref[0]
    return jnp.concatenate([ref[q] for q in range(tn // bw)], axis=1)


def _write_cols(ref, bw, tn, val):
    if bw is None or tn % bw != 0:
        ref[...] = val.astype(ref.dtype)
    else:
        for q in range(tn // bw):
            ref[q] = val[:, q * bw:(q + 1) * bw].astype(ref.dtype)


def mm_nn(a, b, name, *, out_dtype=F32, out_bw=None, tm=1024, tn=512, job=None):
    M, K = a.shape
    N = b.shape[1]
    tm = min(tm, M)

    def body(a_ref, b_ref, o_ref):
        _write_cols(o_ref, out_bw, tn, _dot(a_ref[...].astype(BF16), b_ref[...]))

    ri, ci = (lambda i, j: i), (lambda i, j: j)
    in_specs = [pl.BlockSpec((tm, K), lambda i, j: (i, 0)), pl.BlockSpec((K, tn), lambda i, j: (0, j))]
    oshape = (M, N) if out_bw is None else (N // out_bw, M, out_bw)
    return _pcall(body, [a, b], grid=(M // tm, N // tn), name=name, in_specs=in_specs,
                  out_specs=[_col_spec(out_bw, tn, tm, ri, ci)], out_shape=[SDS(oshape, out_dtype)],
                  sem=("parallel", "parallel"), job=job)[0]


def mm_nn_rows(a, b, res, g, name, *, target=None, tm=512):
    M, K = a.shape
    N = b.shape[1]

    def body(*refs):
        a_ref, b_ref, r_ref, g_ref = refs[:4]
        xv = r_ref[...] + _dot(a_ref[...].astype(BF16), b_ref[...])
        r = lax.rsqrt(jnp.mean(xv * xv, axis=1, keepdims=True) + EPS)
        xh = xv * r
        if target is None:
            x_ref, h_ref = refs[4:]
            x_ref[...] = xv
            h_ref[...] = (xh * g_ref[...]).astype(BF16)
            return
        t_ref, loss_ref, dx_ref, dg_ref = refs[4:]
        err = xh * g_ref[...] - t_ref[...]
        dy = err / N
        gd = dy * g_ref[...]
        dx_ref[...] = r * (gd - xh * jnp.mean(gd * xh, axis=1, keepdims=True))

        @pl.when(pl.program_id(0) == 0)
        def _():
            dg_ref[...] = jnp.zeros_like(dg_ref)
            loss_ref[...] = jnp.zeros_like(loss_ref)

        dg_ref[...] += _colsum(dy * xh)
        loss_ref[...] += 0.5 * _colsum(jnp.mean(err * err, axis=1, keepdims=True))

    row = pl.BlockSpec((tm, N), lambda i: (i, 0))
    vec = pl.BlockSpec((1, N), lambda i: (0, 0))
    in_specs = [pl.BlockSpec((tm, K), lambda i: (i, 0)), pl.BlockSpec((K, N), lambda i: (0, 0)), row, vec]
    if target is None:
        return pl.pallas_call(body, grid=(M // tm,), name=name, in_specs=in_specs, out_specs=[row, row],
                              out_shape=[SDS((M, N), F32), SDS((M, N), BF16)], compiler_params=_cp("parallel"))(a, b, res, g)
    one = pl.BlockSpec((1, 1), lambda i: (0, 0))
    return pl.pallas_call(body, grid=(M // tm,), name=name, in_specs=in_specs + [row], out_specs=[one, row, vec],
                          out_shape=[SDS((1, 1), F32), SDS((M, N), F32), SDS((1, N), F32)],
                          compiler_params=_cp("arbitrary"))(a, b, res, g, target)


def mm_nt(a, a_bw, b, name, *, res=None, out_dtype=F32, tm=1024, tn=512, job=None):
    if a_bw is None:
        M, K = a.shape
    else:
        M, K = a.shape[1], a.shape[0] * a_bw
    N = b.shape[0]
    tm = min(tm, M)

    def body(*refs):
        if res is None:
            a_ref, b_ref, o_ref = refs
        else:
            a_ref, b_ref, r_ref, o_ref = refs
        av = _read_cols(a_ref, a_bw, K).astype(BF16)
        acc = _dot_nt(av, b_ref[...])
        if res is not None:
            acc = acc + r_ref[...]
        o_ref[...] = acc.astype(o_ref.dtype)

    in_specs = [_col_spec(a_bw, K, tm, lambda i, j: i, lambda i, j: 0), pl.BlockSpec((tn, K), lambda i, j: (j, 0))]
    args = [a, b]
    if res is not None:
        in_specs.append(pl.BlockSpec((tm, tn), lambda i, j: (i, j)))
        args.append(res)
    return _pcall(body, args, grid=(M // tm, N // tn), name=name, in_specs=in_specs,
                  out_specs=[pl.BlockSpec((tm, tn), lambda i, j: (i, j))], out_shape=[SDS((M, N), out_dtype)],
                  sem=("parallel", "parallel"), job=job)[0]


def mm_nt_norm_bwd(a, a_bw, b, x, g, res, name, *, tm=512, job=None):
    if a_bw is None:
        M, K = a.shape
    else:
        M, K = a.shape[1], a.shape[0] * a_bw
    N = b.shape[0]

    def body(a_ref, b_ref, x_ref, g_ref, r_ref, dx_ref, dg_ref):
        dh = _dot_nt(_read_cols(a_ref, a_bw, K).astype(BF16), b_ref[...])
        xv = x_ref[...]
        r = lax.rsqrt(jnp.mean(xv * xv, axis=1, keepdims=True) + EPS)
        xh = xv * r
        gd = dh * g_ref[...]
        dx_ref[...] = r_ref[...] + r * (gd - xh * jnp.mean(gd * xh, axis=1, keepdims=True))

        @pl.when(pl.program_id(0) == 0)
        def _():
            dg_ref[...] = jnp.zeros_like(dg_ref)

        dg_ref[...] += jnp.sum(dh * xh, axis=0, keepdims=True)

    row = pl.BlockSpec((tm, N), lambda i: (i, 0))
    vec = pl.BlockSpec((1, N), lambda i: (0, 0))
    return _pcall(body, [a, b, x, g, res], grid=(M // tm,), name=name,
                  in_specs=[_col_spec(a_bw, K, tm, lambda i: i, lambda i: 0), pl.BlockSpec((N, K), lambda i: (0, 0)), row, vec, row],
                  out_specs=[row, vec], out_shape=[SDS((M, N), F32), SDS((1, N), F32)], sem=("arbitrary",), job=job)


def mm_tn(a, b, b_bw, name, *, shards=None, tm=1024, tn=1024, ts=1024, job=None):
    S, Ka = a.shape
    N = b.shape[1] if b_bw is None else b.shape[0] * b_bw
    ts = min(ts, S)
    tm = min(tm, Ka)

    def body(a_ref, b_ref, o_ref):
        @pl.when(pl.program_id(2) == 0)
        def _():
            o_ref[...] = jnp.zeros_like(o_ref)

        bv = _read_cols(b_ref, b_bw, tn).astype(BF16)
        o_ref[...] += _dot_tn(a_ref[...].astype(BF16), bv)

    in_specs = [pl.BlockSpec((ts, tm), lambda i, j, k: (k, i)),
                _col_spec(b_bw, tn, ts, lambda i, j, k: k, lambda i, j, k: j)]
    if shards is None:
        out_spec, oshape = pl.BlockSpec((tm, tn), lambda i, j, k: (i, j)), (Ka, N)
    else:
        per = (N // shards) // tn
        out_spec = pl.BlockSpec((None, tm, tn), lambda i, j, k: (j // per, i, j % per))
        oshape = (shards, Ka, N // shards)
    return _pcall(body, [a, b], grid=(Ka // tm, N // tn, S // ts), name=name, in_specs=in_specs, out_specs=[out_spec],
                  out_shape=[SDS(oshape, F32)], sem=("parallel", "parallel", "arbitrary"), job=job)[0]


KEYS = 2 * BLK


def _bucket_tables():
    out = np.zeros((3, BLK, KEYS), np.int32)
    qi = np.arange(BLK)[:, None]
    kj = np.arange(KEYS)[None, :]
    for p, (win, dil) in enumerate(PATTERNS):
        w = win // dil
        assert w == BLK
        steps = qi + w - kj
        valid = (steps >= 0) & (steps <= w)
        dist = np.clip(steps, 0, w) * dil
        dd = np.maximum(dist, 1).astype(np.float32)
        exact = N_BUCKETS // 2
        large = exact + (np.log(dd / np.float32(exact)) / np.float32(math.log(BUCKET_MAX_DIST / exact))
                         * np.float32(N_BUCKETS - exact)).astype(np.int32)
        large = np.minimum(large, N_BUCKETS - 1)
        out[p] = np.where(valid, np.where(dist < exact, dist, large), -1)
    return out


def bias_tables(rel_bias):
    bkt = jnp.asarray(_bucket_tables())

    def body(rb_ref, bkt_ref, o_ref):
        bk = bkt_ref[...]
        for h in range(N_HEADS):
            acc = jnp.full((BLK, KEYS), NEG, F32)
            for b in range(N_BUCKETS):
                acc = jnp.where(bk == b, rb_ref[h, b], acc)
            o_ref[h] = acc

    return pl.pallas_call(
        body, grid=(3,), name="bias_tables",
        in_specs=[pl.BlockSpec(memory_space=pltpu.SMEM),
                  pl.BlockSpec((None, BLK, KEYS), lambda p: (p, 0, 0))],
        out_specs=pl.BlockSpec((None, N_HEADS, BLK, KEYS), lambda p: (p, 0, 0, 0)),
        out_shape=SDS((3, N_HEADS, BLK, KEYS), F32), compiler_params=_cp("parallel"))(rel_bias, bkt)


def bias_tables_bwd(ds_sum, job=None):
    bkt = jnp.asarray(_bucket_tables())

    def body(ds_ref, bkt_ref, o_ref):
        h = pl.program_id(0)

        @pl.when(h == 0)
        def _():
            o_ref[...] = jnp.zeros_like(o_ref)

        rows = lax.broadcasted_iota(jnp.int32, (N_HEADS, N_BUCKETS), 0)
        cols = lax.broadcasted_iota(jnp.int32, (N_HEADS, N_BUCKETS), 1)
        acc = jnp.zeros((N_HEADS, N_BUCKETS), F32)
        for b in range(N_BUCKETS):
            t = jnp.zeros((BLK, KEYS), F32)
            for p in range(3):
                t = t + jnp.where(bkt_ref[p] == b, ds_ref[p], 0.0)
            tot = jnp.sum(_rsum(t), axis=0, keepdims=True)
            acc = acc + jnp.where((rows == h) & (cols == b), tot, 0.0)
        o_ref[...] += acc

    return _pcall(
        body, [ds_sum, bkt], grid=(N_HEADS,), name="bias_tables_bwd",
        in_specs=[pl.BlockSpec((3, None, BLK, KEYS), lambda h: (0, h, 0, 0)),
                  pl.BlockSpec((3, BLK, KEYS), lambda h: (0, 0, 0))],
        out_specs=[pl.BlockSpec((N_HEADS, N_BUCKETS), lambda h: (0, 0))],
        out_shape=[SDS((N_HEADS, N_BUCKETS), F32)], sem=("arbitrary",), job=job)[0]


def _rows(start, dil):
    if dil == 1:
        return pl.ds(pl.multiple_of(start, BLK), BLK)
    return pl.ds(start, BLK, stride=dil)


GRP = 8


def _group_rows(i, dil, S):
    nb = S // (BLK * dil)
    run = min(nb, GRP)
    chunks = nb // run
    res0, b0 = (i // chunks) * (GRP // run), (i % chunks) * run
    cur = [(b0 + j % run) * (BLK * dil) + res0 + j // run for j in range(GRP)]
    t = lax.broadcasted_iota(jnp.int32, (GRP, 1, 1), 0)
    if run == nb:
        before, pen = None, jnp.where(t % run == 0, NEG, 0.0)
    else:
        before = _rows(jnp.maximum(b0 - 1, 0) * (BLK * dil) + res0, dil)
        pen = jnp.where((t == 0) & (b0 == 0), NEG, 0.0)
    return [_rows(s, dil) for s in cur], before, pen


def _load_group(ref, rows):
    return jnp.stack([ref[r, :] for r in rows])


def _with_prev(ref, before, cur_blocks):
    first = cur_blocks[:1] if before is None else ref[before, :][None].astype(cur_blocks.dtype)
    return jnp.concatenate([first, cur_blocks[:-1]], axis=0)


def _bdot_nt(a, b):
    return lax.dot_general(a, b, (((2,), (2,)), ((0,), (0,))), preferred_element_type=F32)


def _bdot(a, b):
    return lax.dot_general(a, b, (((2,), (1,)), ((0,), (0,))), preferred_element_type=F32)


def _bdot_tn(a, b):
    return lax.dot_general(a, b, (((1,), (1,)), ((0,), (0,))), preferred_element_type=F32)


def _lsum(x):
    return jnp.sum(x, axis=-1, keepdims=True)


def _widen(src, dst):
    S = src.shape[1]

    def chunk(i, carry):
        rows = pl.ds(pl.multiple_of(i * 512, 512), 512)
        for a in range(3):
            dst[a, rows, :] = src[a, rows, :].astype(F32)
        return carry

    lax.fori_loop(0, S // 512, chunk, 0)


def attn_fwd(qkv, bias, job=None):
    S = qkv.shape[3]
    nblk = S // BLK
    car = Carried(job, 2, 2, 3, 4)

    def body(*refs):
        (qkv_in, bias_ref, o_ref, lse_ref, qkv_ref, part_o, part_ml), jrefs = car.split(refs)
        car.phase(0, pl.program_id(0), jrefs)
        _widen(qkv_in, qkv_ref)
        lane = lax.broadcasted_iota(jnp.int32, (GRP, BLK, BLK), 2)
        lo = lane < HEAD_DIM
        masks = (lo, jnp.logical_not(lo))
        lov = lax.broadcasted_iota(jnp.int32, (GRP, KEYS, BLK), 2) < HEAD_DIM
        vmasks = (lov, jnp.logical_not(lov))
        prev_keys = lax.broadcasted_iota(jnp.int32, (1, 1, KEYS), 2) < BLK
        q_ref, k_ref, v_ref = qkv_ref.at[0], qkv_ref.at[1], qkv_ref.at[2]
        targets = [(o_ref, lse_ref)] + [(part_o.at[t], part_ml.at[t]) for t in range(len(PATTERNS) - 1)]
        for p, (_, dil) in enumerate(PATTERNS):
            def step(i, carry, p=p, dil=dil):
                rc, before, pen = _group_rows(i, dil, S)
                q2 = _load_group(q_ref, rc) * Q_SCALE
                kc = _load_group(k_ref, rc).astype(BF16)
                keys = jnp.concatenate([_with_prev(k_ref, before, kc), kc], axis=1)
                vc = _load_group(v_ref, rc)
                vals = jnp.concatenate([_with_prev(v_ref, before, vc), vc], axis=1)
                pen = jnp.where(prev_keys, pen, 0.0)
                pv = jnp.zeros((GRP, BLK, BLK), F32)
                ms, ls = [], []
                for h in range(2):
                    qh = jnp.where(masks[h], q2, 0.0).astype(BF16)
                    s = _bdot_nt(qh, keys) + bias_ref[p, h][None] + pen
                    mn = jnp.max(s, axis=-1, keepdims=True)
                    pr = jnp.exp(s - mn)
                    pv = pv + _bdot(pr.astype(BF16), jnp.where(vmasks[h], vals, 0.0).astype(BF16))
                    ms.append(mn)
                    ls.append(_lsum(pr))
                ml_t = jnp.where(lo, jnp.where(lane < HALF_HEAD, ms[0], ls[0]),
                                 jnp.where(lane < HEAD_DIM + HALF_HEAD, ms[1], ls[1]))
                acc_ref, ml_ref = targets[p]
                for j, r in enumerate(rc):
                    acc_ref[r, :] = pv[j]
                    ml_ref[r, :] = ml_t[j]
                return carry

            lax.fori_loop(0, nblk // GRP, step, 0)

        def merge(i, carry):
            rows = pl.ds(pl.multiple_of(i * 512, 512), 512)
            is_m = (lax.broadcasted_iota(jnp.int32, (512, BLK), 1) & HALF_HEAD) == 0
            accs = [ref[rows, :] for ref, _ in targets]
            mls = [ref[rows, :] for _, ref in targets]
            mx = [jnp.where(is_m, ml, pltpu.roll(ml, HALF_HEAD, 1)) for ml in mls]
            dn = [jnp.where(is_m, pltpu.roll(ml, BLK - HALF_HEAD, 1), ml) for ml in mls]
            top = jnp.maximum(jnp.maximum(mx[0], mx[1]), mx[2])
            ws_ = [jnp.exp(m - top) for m in mx]
            den = ws_[0] * dn[0] + ws_[1] * dn[1] + ws_[2] * dn[2]
            o_ref[rows, :] = (ws_[0] * accs[0] + ws_[1] * accs[1] + ws_[2] * accs[2]) / den
            lse_ref[rows, :] = top + jnp.log(den)
            return carry

        lax.fori_loop(0, S // 512, merge, 0)
        car.phase(1, pl.program_id(0), jrefs)
        car.phase(2, pl.program_id(0), jrefs)

    outs = pl.pallas_call(
        body, grid=(4,), name="attn_fwd",
        in_specs=[pl.BlockSpec((None, 3, None, S, BLK), lambda g: (0, 0, g, 0, 0)),
                  pl.BlockSpec((3, 2, BLK, KEYS), lambda g: (0, g, 0, 0))] + car.in_specs(),
        out_specs=[pl.BlockSpec((None, S, BLK), lambda g: (g, 0, 0)),
                   pl.BlockSpec((None, S, BLK), lambda g: (g, 0, 0))] + car.out_specs(),
        out_shape=[SDS((4, S, BLK), F32), SDS((4, S, BLK), F32)] + car.out_shapes(),
        input_output_aliases=car.aliases(),
        scratch_shapes=[pltpu.VMEM((3, S, BLK), F32), pltpu.VMEM((2, S, BLK), F32), pltpu.VMEM((2, S, BLK), F32)] + car.sems(),
        compiler_params=_cp("arbitrary"))(qkv, bias, *car.args())
    if job is not None:
        job.landed = list(outs[2:])
    return outs[0], outs[1]


def attn_bwd(qkv, bias, o, lse, do, job=None):
    S = qkv.shape[3]
    nblk = S // BLK
    car = Carried(job, 5, 2, 4, 4)

    def body(*refs):
        (qkv_in, bias_ref, o_ref, lse_ref, do_ref, dqkv_out, ds_ref, qkv_ref, dqkv_ref, last_ref, st_ref), jrefs = car.split(refs)
        car.phase(0, pl.program_id(0), jrefs)
        _widen(qkv_in, qkv_ref)

        def stats(i, carry):
            rows = pl.ds(pl.multiple_of(i * 512, 512), 512)
            ln = lax.broadcasted_iota(jnp.int32, (512, BLK), 1)
            prod = do_ref[rows, :] * o_ref[rows, :]
            d0 = _rsum(jnp.where(ln < HEAD_DIM, prod, 0.0))
            d1 = _rsum(jnp.where(ln < HEAD_DIM, 0.0, prod))
            st_ref[rows, :] = jnp.where((ln & HALF_HEAD) == 0, lse_ref[rows, :], jnp.where(ln < HEAD_DIM, d0, d1))
            return carry

        lax.fori_loop(0, S // 512, stats, 0)
        lane = lax.broadcasted_iota(jnp.int32, (GRP, BLK, BLK), 2)
        lo = lane < HEAD_DIM
        masks = (lo, jnp.logical_not(lo))
        lov = lax.broadcasted_iota(jnp.int32, (GRP, KEYS, BLK), 2) < HEAD_DIM
        vmasks = (lov, jnp.logical_not(lov))
        prev_keys = lax.broadcasted_iota(jnp.int32, (1, 1, KEYS), 2) < BLK
        ds_ref[...] = jnp.zeros_like(ds_ref)
        q_ref, k_ref, v_ref = qkv_ref.at[0], qkv_ref.at[1], qkv_ref.at[2]
        last = len(PATTERNS) - 1
        for p, (_, dil) in enumerate(PATTERNS):
            def step(i, carry, p=p, dil=dil):
                tgt = last_ref if p == last else dqkv_ref
                dq_ref, dk_ref, dv_ref = tgt.at[0], tgt.at[1], tgt.at[2]
                rc, before, pen = _group_rows(i, dil, S)
                q2 = _load_group(q_ref, rc) * Q_SCALE
                kc = _load_group(k_ref, rc).astype(BF16)
                keys = jnp.concatenate([_with_prev(k_ref, before, kc), kc], axis=1)
                vc = _load_group(v_ref, rc).astype(BF16)
                vals = jnp.concatenate([_with_prev(v_ref, before, vc), vc], axis=1)
                pen = jnp.where(prev_keys, pen, 0.0)
                dot = _load_group(do_ref, rc)
                st = _load_group(st_ref, rc)
                dq = jnp.zeros((GRP, BLK, BLK), F32)
                dkeys = jnp.zeros((GRP, KEYS, BLK), F32)
                dvals = jnp.zeros((GRP, KEYS, BLK), F32)
                for h in range(2):
                    qh = jnp.where(masks[h], q2, 0.0).astype(BF16)
                    doh = jnp.where(masks[h], dot, 0.0).astype(BF16)
                    lh = st[:, :, HEAD_DIM * h:HEAD_DIM * h + 1]
                    delta = st[:, :, HEAD_DIM * h + HALF_HEAD:HEAD_DIM * h + HALF_HEAD + 1]
                    pr = jnp.exp(_bdot_nt(qh, keys) + bias_ref[p, h][None] + pen - lh)
                    ds = pr * (_bdot_nt(doh, vals) - delta)
                    ds_ref[p, h] += jnp.sum(ds, axis=0)
                    dsb = ds.astype(BF16)
                    dq = dq + jnp.where(masks[h], _bdot(dsb, keys), 0.0)
                    dkeys = dkeys + _bdot_tn(dsb, qh)
                    dvals = dvals + _bdot_tn(pr.astype(BF16), doh)
                dkp, dkc, dvp, dvc = dkeys[:, :BLK], dkeys[:, BLK:], dvals[:, :BLK], dvals[:, BLK:]
                none = jnp.zeros((1, BLK, BLK), F32)
                dkc = dkc + jnp.concatenate([dkp[1:], none], axis=0)
                dvc = dvc + jnp.concatenate([dvp[1:], none], axis=0)
                for j in range(GRP):
                    if p == 1:
                        dq_ref[rc[j], :] += dq[j] * Q_SCALE
                        dk_ref[rc[j], :] += dkc[j]
                        dv_ref[rc[j], :] += dvc[j]
                    else:
                        dq_ref[rc[j], :] = dq[j] * Q_SCALE
                        dk_ref[rc[j], :] = dkc[j]
                        dv_ref[rc[j], :] = dvc[j]
                if before is not None:
                    dk_ref[before, :] += dkp[0]
                    dv_ref[before, :] += dvp[0]
                return carry

            lax.fori_loop(0, nblk // GRP, step, 0)

        def narrow(i, carry):
            rows = pl.ds(pl.multiple_of(i * 512, 512), 512)
            for a in range(3):
                dqkv_out[a, rows, :] = (dqkv_ref[a, rows, :] + last_ref[a, rows, :]).astype(BF16)
            return carry

        lax.fori_loop(0, S // 512, narrow, 0)
        car.phase(1, pl.program_id(0), jrefs)
        car.phase(2, pl.program_id(0), jrefs)

    blk = pl.BlockSpec((None, S, BLK), lambda g: (g, 0, 0))
    outs = pl.pallas_call(
        body, grid=(4,), name="attn_bwd",
        in_specs=[pl.BlockSpec((None, 3, None, S, BLK), lambda g: (0, 0, g, 0, 0)),
                  pl.BlockSpec((3, 2, BLK, KEYS), lambda g: (0, g, 0, 0)), blk, blk, blk] + car.in_specs(),
        out_specs=[pl.BlockSpec((None, 3, None, S, BLK), lambda g: (0, 0, g, 0, 0)),
                   pl.BlockSpec((3, 2, BLK, KEYS), lambda g: (0, g, 0, 0))] + car.out_specs(),
        out_shape=[SDS((2, 3, 4, S, BLK), BF16), SDS((3, N_HEADS, BLK, KEYS), F32)] + car.out_shapes(),
        input_output_aliases=car.aliases(),
        scratch_shapes=[pltpu.VMEM((3, S, BLK), F32)] * 3 + [pltpu.VMEM((S, BLK), F32)] + car.sems(),
        compiler_params=_cp("arbitrary"))(qkv, bias, o, lse, do, *car.args())
    if job is not None:
        job.landed = list(outs[2:])
    return outs[0], outs[1]


def _shift_down(u, k, halo):
    n = u.shape[0]
    row = lax.broadcasted_iota(jnp.int32, u.shape, 0)
    out = pltpu.roll(u, k, 0)
    hn = halo.shape[0]
    for j in range(k):
        out = jnp.where(row == j, halo[hn - k + j:hn - k + j + 1, :], out)
    return out


def _shift_up(u, k, halo):
    n = u.shape[0]
    row = lax.broadcasted_iota(jnp.int32, u.shape, 0)
    out = pltpu.roll(u, n - k, 0)
    for j in range(k):
        out = jnp.where(row == n - k + j, halo[j:j + 1, :], out)
    return out


def _conv3(u, halo, w0, w1, w2):
    return _shift_down(u, 2, halo) * w0 + _shift_down(u, 1, halo) * w1 + u * w2


def _colsum(x):
    return jnp.sum(x, axis=0, keepdims=True)


def _mixer_specs(S, tm):
    conv = pl.BlockSpec((None, 12, tm, BLK), lambda i: (1, 0, i, 0))
    halo = pl.BlockSpec((None, 12, 16, BLK), lambda i: (1, 0, jnp.maximum(i * (tm // 16) - 1, 0), 0))
    ob = pl.BlockSpec((4, tm, BLK), lambda i: (0, i, 0))
    return conv, halo, ob


def _mixer_recompute(i, o_ref, pr_ref, ph_ref, w_ref):
    ob = [o_ref[q] for q in range(4)]
    gb = [pr_ref[q].astype(F32) for q in range(4)]
    gc = [pr_ref[4 + q].astype(F32) for q in range(4)]
    xi = [pr_ref[8 + q].astype(F32) for q in range(4)]
    keep = jnp.where(i > 0, 1.0, 0.0)
    u = [gc[q] * xi[q] for q in range(4)]
    hu = [ph_ref[4 + q].astype(F32) * ph_ref[8 + q].astype(F32) * keep for q in range(4)]
    w = [[w_ref[k:k + 1, q * BLK:(q + 1) * BLK] for k in range(3)] for q in range(4)]
    cv = [_conv3(u[q], hu[q], *w[q]) for q in range(4)]
    return ob, gb, gc, xi, u, hu, cv, w


def _rms_blocks(blocks):
    ss = sum(_rsum(b * b) for b in blocks)
    return lax.rsqrt(ss / (BLK * len(blocks)) + EPS)


def mixer_post_fwd(o, proj, w_sc, g_a, g_c):
    S = o.shape[1]
    tm = 512

    def body(o_ref, pr_ref, ph_ref, w_ref, ga_ref, gc_ref, m_ref):
        i = pl.program_id(0)
        ob, gb, _, _, _, _, cv, _ = _mixer_recompute(i, o_ref, pr_ref, ph_ref, w_ref)
        conv = [gb[q] * cv[q] for q in range(4)]
        ra, rc = _rms_blocks(ob), _rms_blocks(conv)
        for q in range(4):
            sl = slice(q * BLK, (q + 1) * BLK)
            m_ref[:, q * BLK:(q + 1) * BLK] = (ob[q] * ra * ga_ref[:, sl]).astype(BF16)
            m_ref[:, ATTN_W + q * BLK:ATTN_W + (q + 1) * BLK] = (conv[q] * rc * gc_ref[:, sl]).astype(BF16)

    conv_s, halo_s, ob_s = _mixer_specs(S, tm)
    full = lambda r, c: pl.BlockSpec((r, c), lambda i: (0, 0))
    return pl.pallas_call(
        body, grid=(S // tm,), name="mixer_post_fwd",
        in_specs=[ob_s, conv_s, halo_s, full(3, 512), full(1, 512), full(1, 512)],
        out_specs=pl.BlockSpec((tm, D_MODEL), lambda i: (i, 0)), out_shape=SDS((S, D_MODEL), BF16),
        compiler_params=_cp("parallel"))(o, proj, proj, w_sc, g_a, g_c)


def mixer_post_bwd_a(dmixed, o, proj, w_sc, g_a, g_c, job=None):
    S = o.shape[1]
    tm = 512

    def body(dm_ref, o_ref, pr_ref, ph_ref, w_ref, ga_ref, gc_ref, do_ref, dgb_ref, dcv_ref, dga_ref, dgc_ref):
        i = pl.program_id(0)
        ob, gb, _, _, _, _, cv, _ = _mixer_recompute(i, o_ref, pr_ref, ph_ref, w_ref)
        conv = [gb[q] * cv[q] for q in range(4)]
        ra, rc = _rms_blocks(ob), _rms_blocks(conv)

        @pl.when(i == 0)
        def _():
            dga_ref[...] = jnp.zeros_like(dga_ref)
            dgc_ref[...] = jnp.zeros_like(dgc_ref)

        for blocks, r, g_ref, off, dg_ref, is_attn in ((ob, ra, ga_ref, 0, dga_ref, True), (conv, rc, gc_ref, ATTN_W, dgc_ref, False)):
            xh = [blocks[q] * r for q in range(4)]
            dm = [dm_ref[:, off + q * BLK:off + (q + 1) * BLK].astype(F32) for q in range(4)]
            gd = [dm[q] * g_ref[:, q * BLK:(q + 1) * BLK] for q in range(4)]
            mean = sum(_rsum(gd[q] * xh[q]) for q in range(4)) / (4 * BLK)
            for q in range(4):
                dg_ref[:, q * BLK:(q + 1) * BLK] += _colsum(dm[q] * xh[q])
                dx = r * (gd[q] - xh[q] * mean)
                if is_attn:
                    do_ref[q] = dx
                else:
                    dgb_ref[q] = dx * cv[q]
                    dcv_ref[q] = dx * gb[q]

    conv_s, halo_s, ob_s = _mixer_specs(S, tm)
    full = lambda r, c: pl.BlockSpec((r, c), lambda i: (0, 0))
    return _pcall(
        body, [dmixed, o, proj, proj, w_sc, g_a, g_c], grid=(S // tm,), name="mixer_post_bwd_a",
        in_specs=[pl.BlockSpec((tm, D_MODEL), lambda i: (i, 0)), ob_s, conv_s, halo_s, full(3, 512), full(1, 512), full(1, 512)],
        out_specs=[ob_s, ob_s, ob_s, full(1, 512), full(1, 512)],
        out_shape=[SDS((4, S, BLK), F32)] * 3 + [SDS((1, 512), F32)] * 2, sem=("arbitrary",), job=job)


def mixer_post_bwd_b(dproj, dgb, dcv, proj, w_sc):
    S = proj.shape[2]
    tm = 512
    last = S // 8 - 1

    def body(dp_in, dgb_ref, dcv_ref, dn_ref, pr_ref, w_ref, dp_ref, dw_ref):
        i = pl.program_id(0)
        keep_next = jnp.where(i < pl.num_programs(0) - 1, 1.0, 0.0)

        @pl.when(i == 0)
        def _():
            dw_ref[...] = jnp.zeros_like(dw_ref)

        for q in range(4):
            sl = slice(q * BLK, (q + 1) * BLK)
            gc, xi = pr_ref[4 + q].astype(F32), pr_ref[8 + q].astype(F32)
            u = gc * xi
            dcv = dcv_ref[q]
            dn = dn_ref[q] * keep_next
            d1, d2 = _shift_up(dcv, 1, dn), _shift_up(dcv, 2, dn)
            w0, w1, w2 = (w_ref[k:k + 1, sl] for k in range(3))
            du = dcv * w2 + d1 * w1 + d2 * w0
            dw_ref[0:1, sl] += _colsum(d2 * u)
            dw_ref[1:2, sl] += _colsum(d1 * u)
            dw_ref[2:3, sl] += _colsum(dcv * u)
            dp_ref[q] = dgb_ref[q].astype(BF16)
            dp_ref[4 + q] = (du * xi).astype(BF16)
            dp_ref[8 + q] = (du * gc).astype(BF16)

    conv_s, _, ob_s = _mixer_specs(S, tm)
    nxt = pl.BlockSpec((4, 8, BLK), lambda i: (0, jnp.minimum((i + 1) * (tm // 8), last), 0))
    full = lambda r, c: pl.BlockSpec((r, c), lambda i: (0, 0))
    return pl.pallas_call(
        body, grid=(S // tm,), name="mixer_post_bwd_b",
        in_specs=[pl.BlockSpec(memory_space=pl.ANY), ob_s, ob_s, nxt, conv_s, full(3, 512)],
        out_specs=[conv_s, full(3, 512)],
        out_shape=[SDS(dproj.shape, BF16), SDS((3, 512), F32)],
        input_output_aliases={0: 0}, compiler_params=_cp("arbitrary"))(dproj, dgb, dcv, dcv, proj, w_sc)


def xattn_fwd(q, k, v):
    S = q.shape[0]
    tm = 512
    scale = MEM_HD ** -0.5

    def body(q_ref, k_ref, v_ref, o_ref):
        for h in range(N_MEM_HEADS):
            sl = slice(h * MEM_HD, (h + 1) * MEM_HD)
            s = _dot_nt(q_ref[:, sl], k_ref[:, sl]) * scale
            p = jnp.exp(s - jnp.max(s, axis=1, keepdims=True))
            p = p / _rsum(p)
            o_ref[:, sl] = _dot(p.astype(BF16), v_ref[:, sl]).astype(BF16)

    row = pl.BlockSpec((tm, D_MODEL), lambda i: (i, 0))
    kv = pl.BlockSpec(k.shape, lambda i: (0, 0))
    return pl.pallas_call(body, grid=(S // tm,), name="xattn_fwd", in_specs=[row, kv, kv], out_specs=row,
                          out_shape=SDS((S, D_MODEL), BF16), compiler_params=_cp("parallel"))(q, k, v)


def xattn_bwd(q, k, v, do):
    S = q.shape[0]
    tm = 512
    scale = MEM_HD ** -0.5

    def body(q_ref, k_ref, v_ref, do_ref, dq_ref, dk_ref, dv_ref):
        @pl.when(pl.program_id(0) == 0)
        def _():
            dk_ref[...] = jnp.zeros_like(dk_ref)
            dv_ref[...] = jnp.zeros_like(dv_ref)

        for h in range(N_MEM_HEADS):
            sl = slice(h * MEM_HD, (h + 1) * MEM_HD)
            qh, kh, vh, doh = q_ref[:, sl], k_ref[:, sl], v_ref[:, sl], do_ref[:, sl]
            s = _dot_nt(qh, kh) * scale
            p = jnp.exp(s - jnp.max(s, axis=1, keepdims=True))
            p = p / _rsum(p)
            dp = _dot_nt(doh, vh)
            ds = (p * (dp - _rsum(p * dp)) * scale).astype(BF16)
            dq_ref[:, sl] = _dot(ds, kh).astype(BF16)
            dk_ref[:, sl] += _dot_tn(ds, qh)
            dv_ref[:, sl] += _dot_tn(p.astype(BF16), doh)

    row = pl.BlockSpec((tm, D_MODEL), lambda i: (i, 0))
    kv = pl.BlockSpec(k.shape, lambda i: (0, 0))
    return pl.pallas_call(body, grid=(S // tm,), name="xattn_bwd", in_specs=[row, kv, kv, row], out_specs=[row, kv, kv],
                          out_shape=[SDS((S, D_MODEL), BF16), SDS(k.shape, F32), SDS(k.shape, F32)],
                          compiler_params=_cp("arbitrary"))(q, k, v, do)


FFN_TM, FFN_TC = 256, 1408


def _ffn_specs(S):
    tm, tc = FFN_TM, FFN_TC
    blk = pl.BlockSpec((2, tm, tc), lambda j, i: (0, i, j))
    nxt = pl.BlockSpec((2, 16, tc), lambda j, i: (0, jnp.minimum((i + 1) * (tm // 16), S // 16 - 1), j))
    wsp = pl.BlockSpec((2, 3, tc), lambda j, i: (0, 0, j))
    bsp = pl.BlockSpec((2, 1, tc), lambda j, i: (0, 0, j))
    return blk, nxt, wsp, bsp


def up_act_fwd(h, w_up, w_fc, b_fc, job=None):
    S, K = h.shape
    tm, tc = FFN_TM, FFN_TC
    nj = D_FF // tc

    def body(h_ref, wg_ref, wv_ref, w_ref, b_ref, pre_ref, gv_ref, a_ref, halo):
        i = pl.program_id(1)
        hv = h_ref[...]
        res = []
        for half, wt_ref in enumerate((wg_ref, wv_ref)):
            u = _dot(hv, wt_ref[...])
            hu = jnp.where(i > 0, halo[half], 0.0)
            halo[half] = u[tm - 8:, :]
            w0, w1, w2 = (w_ref[half, k:k + 1, :] for k in range(3))
            pre_ref[half] = u.astype(BF16)
            res.append(_conv3(u, hu, w0, w1, w2) + b_ref[half])
        g, v = res
        gv_ref[0] = g.astype(BF16)
        gv_ref[1] = v.astype(BF16)
        a_ref[...] = (g * jax.nn.sigmoid(g) * v).astype(BF16)

    blk = pl.BlockSpec((2, tm, tc), lambda j, i: (0, i, j))
    return _pcall(
        body, [h, w_up, w_up, w_fc, b_fc], grid=(nj, S // tm), name="up_act_fwd",
        in_specs=[pl.BlockSpec((tm, K), lambda j, i: (i, 0)), pl.BlockSpec((K, tc), lambda j, i: (0, j)),
                  pl.BlockSpec((K, tc), lambda j, i: (0, j + nj)), pl.BlockSpec((2, 3, tc), lambda j, i: (0, 0, j)),
                  pl.BlockSpec((2, 1, tc), lambda j, i: (0, 0, j))],
        out_specs=[blk, blk, pl.BlockSpec((tm, tc), lambda j, i: (i, j))],
        out_shape=[SDS((2, S, D_FF), BF16), SDS((2, S, D_FF), BF16), SDS((S, D_FF), BF16)],
        scratch=[pltpu.VMEM((2, 8, tc), F32)], sem=("parallel", "arbitrary"), job=job)


def ffn_act_bwd(dx, w_down, up, up_pre, w_fc, job=None):
    S = up.shape[1]

    def act_grads(da, g, v):
        sg = jax.nn.sigmoid(g)
        return da * v * (sg * (1.0 + g * (1.0 - sg))), da * g * sg

    def body(dx_ref, dxn_ref, wd_ref, gv_ref, gvn_ref, up_ref, w_ref, dp_ref, dw_ref, db_ref):
        i = pl.program_id(1)
        keep_next = jnp.where(i < pl.num_programs(1) - 1, 1.0, 0.0)

        @pl.when(i == 0)
        def _():
            dw_ref[...] = jnp.zeros_like(dw_ref)
            db_ref[...] = jnp.zeros_like(db_ref)

        wd = wd_ref[...]
        da = _dot_nt(dx_ref[...].astype(BF16), wd)
        dan = _dot_nt(dxn_ref[...].astype(BF16), wd) * keep_next
        here = act_grads(da, gv_ref[0].astype(F32), gv_ref[1].astype(F32))
        after = act_grads(dan, gvn_ref[0].astype(F32), gvn_ref[1].astype(F32))
        for half in range(2):
            d, dn = here[half], after[half]
            u = up_ref[half].astype(F32)
            d1, d2 = _shift_up(d, 1, dn), _shift_up(d, 2, dn)
            w0, w1, w2 = (w_ref[half, k:k + 1, :] for k in range(3))
            dp_ref[half] = (d * w2 + d1 * w1 + d2 * w0).astype(BF16)
            dw_ref[half, 0:1, :] += _colsum(d2 * u)
            dw_ref[half, 1:2, :] += _colsum(d1 * u)
            dw_ref[half, 2:3, :] += _colsum(d * u)
            db_ref[half] += _colsum(d)

    tm, tc = FFN_TM, FFN_TC
    blk, nxt, wsp, bsp = _ffn_specs(S)
    K = dx.shape[1]
    rows = pl.BlockSpec((tm, K), lambda j, i: (i, 0))
    rows_next = pl.BlockSpec((16, K), lambda j, i: (jnp.minimum((i + 1) * (tm // 16), S // 16 - 1), 0))
    return _pcall(body, [dx, dx, w_down, up, up, up_pre, w_fc], grid=(D_FF // tc, S // tm), name="ffn_act_bwd",
                  in_specs=[rows, rows_next, pl.BlockSpec((tc, K), lambda j, i: (j, 0)), blk, nxt, blk, wsp],
                  out_specs=[blk, wsp, bsp],
                  out_shape=[SDS((2, S, D_FF), BF16), SDS((2, 3, D_FF), F32), SDS((2, 1, D_FF), F32)],
                  sem=("parallel", "arbitrary"), job=job)


def local_step(x, mem, target, wb, ws, comm=None):
    S = x.shape[0]
    assert S % 2048 == 0
    if comm is None:
        comm = NoComm()
    else:
        wb = {}
    ws = dict(ws)

    bias = bias_tables(ws["rel_bias"])
    h1 = rmsnorm_fwd(x, ws["g_mix"], "norm_mix", job=comm.job("norm_mix"))
    comm.landed("norm_mix", wb, ws)
    w_fc = ws["w_ffn_conv"].reshape(3, 2, D_FF).transpose(1, 0, 2)
    b_fc = ws["b_ffn_conv"].reshape(2, 1, D_FF)
    proj = mm_nn(h1, wb["w_in"], "proj", out_dtype=BF16, out_bw=BLK, tn=768, job=comm.job("proj"))
    comm.landed("proj", wb)
    qkv = proj.reshape(2, 3, 4, S, BLK)
    o, lse = attn_fwd(qkv, bias, job=comm.job("attn_fwd"))
    comm.landed("attn_fwd", wb)
    proj4 = proj.reshape(2, 12, S, BLK)
    mixed = mixer_post_fwd(o, proj4, ws["w_short_conv"], ws["g_attn_out"], ws["g_conv_out"])
    x1, h2 = mm_nn_rows(mixed, wb["w_out"], x, ws["g_xattn"], "out_proj")
    mem_n = rmsnorm_fwd(mem, ws["g_mem"], "norm_mem")
    xq = mm_nn(h2, wb["w_xq"], "xq", out_dtype=BF16)
    xk = mm_nn(mem_n, wb["w_xk"], "xk", out_dtype=BF16, tn=1024)
    xv = mm_nn(mem_n, wb["w_xv"], "xv", out_dtype=BF16, tn=1024)
    xo = xattn_fwd(xq, xk, xv)
    x2, h3 = mm_nn_rows(xo, wb["w_xo"], x1, ws["g_ffn"], "xo_proj")
    up_pre, up, act = up_act_fwd(h3, wb["w_up"], w_fc, b_fc, job=comm.job("up_act_fwd"))
    comm.landed("up_act_fwd", wb)
    loss, dx3, dg_final = mm_nn_rows(act, wb["w_down"], x2, ws["g_final"].reshape(1, -1), "down_proj", target=target)

    gb, gs = {}, {"g_final": dg_final}
    gb["w_down"] = mm_tn(act, dx3, None, "dw_down", tm=1408, tn=1024)
    d_up_pre, dw_fc, db_fc = ffn_act_bwd(dx3, wb["w_down"], up, up_pre, w_fc, job=comm.job("ffn_act_bwd", gb))
    comm.landed("ffn_act_bwd")
    gs["b_ffn_conv"] = db_fc.reshape(1, 2 * D_FF)
    gs["w_ffn_conv"] = dw_fc.transpose(1, 0, 2).reshape(3, 2 * D_FF)
    gb["w_up"] = mm_tn(h3, d_up_pre, D_FF, "dw_up", shards=4, tn=1408, job=comm.job("dw_up"))
    comm.landed("dw_up")
    dx2, gs["g_ffn"] = mm_nt_norm_bwd(d_up_pre, D_FF, wb["w_up"], x2, ws["g_ffn"], dx3, "d_h3", tm=256,
                                      job=comm.job("d_h3", gb))
    comm.landed("d_h3")
    gb["w_xo"] = mm_tn(xo, dx2, None, "dw_xo")
    dxo = mm_nt(dx2, None, wb["w_xo"], "d_xo", out_dtype=BF16)
    dxq, dxk, dxv = xattn_bwd(xq, xk, xv, dxo)
    gb["w_xq"] = mm_tn(h2, dxq, None, "dw_xq")
    gb["w_xk"] = mm_tn(mem_n, dxk, None, "dw_xk", tn=1024)
    gb["w_xv"] = mm_tn(mem_n, dxv, None, "dw_xv", tn=1024)
    dmem_n = mm_nt(dxk, None, wb["w_xk"], "d_memk", tn=1024)
    dmem_n = mm_nt(dxv, None, wb["w_xv"], "d_memv", res=dmem_n, tn=1024)
    gs["g_mem"] = rmsnorm_gain_grad(mem, dmem_n, "norm_mem_bwd")
    dx1, gs["g_xattn"] = mm_nt_norm_bwd(dxq, None, wb["w_xq"], x1, ws["g_xattn"], dx2, "d_h2")
    gb["w_out"] = mm_tn(mixed, dx1, None, "dw_out")
    dmixed = mm_nt(dx1, None, wb["w_out"], "d_mixed", out_dtype=BF16)
    do, dgb, dcv, gs["g_attn_out"], gs["g_conv_out"] = mixer_post_bwd_a(
        dmixed, o, proj4, ws["w_short_conv"], ws["g_attn_out"], ws["g_conv_out"], job=comm.job("mixer_post_bwd_a", gb))
    comm.landed("mixer_post_bwd_a")
    dproj, ds_sum = attn_bwd(qkv, bias, o, lse, do, job=comm.job("attn_bwd"))
    comm.landed("attn_bwd")
    dproj, gs["w_short_conv"] = mixer_post_bwd_b(dproj.reshape(2, 12, S, BLK), dgb, dcv, proj4, ws["w_short_conv"])
    dproj = dproj.reshape(24, S, BLK)
    gb["w_in"] = mm_tn(h1, dproj, BLK, "dw_in", shards=4, tn=768, job=comm.job("dw_in"))
    comm.landed("dw_in")
    gs["rel_bias"] = bias_tables_bwd(ds_sum, job=comm.job("bias_tables_bwd", gb))
    comm.landed("bias_tables_bwd")
    grad_x, gs["g_mix"] = mm_nt_norm_bwd(dproj, BLK, wb["w_in"], x, ws["g_mix"], dx1, "d_h1", tm=256,
                                         job=comm.job("d_h1", gb))
    comm.landed("d_h1")
    return loss, grad_x, comm.finish(gb), gs


def _adam_update(w_ref, g_ref, m_ref, v_ref, d_ref, nm_ref, nv_ref):
    gv = g_ref[...]
    mn = ADAM_B1 * m_ref[...] + (1.0 - ADAM_B1) * gv
    vn = ADAM_B2 * v_ref[...] + (1.0 - ADAM_B2) * (gv * gv)
    m_hat = mn / (1.0 - ADAM_B1 ** ADAM_STEP)
    v_hat = vn / (1.0 - ADAM_B2 ** ADAM_STEP)
    d_ref[...] = -ADAM_LR * (m_hat / (jnp.sqrt(v_hat) + ADAM_EPS) + ADAM_WD * w_ref[...])
    nm_ref[...] = mn
    nv_ref[...] = vn


def adamw(ws, gs, ms, vs, name):
    n = len(ws)
    R, C = ws[0].shape
    tr = R // 4

    def body(*refs):
        for p in range(n):
            _adam_update(*(refs[q * n + p] for q in (0, 1, 2, 3, 5, 6, 7)))
            refs[4 * n + p][...] = refs[n + p][...]

    blk = pl.BlockSpec((tr, C), lambda i: (i, 0))
    outs = pl.pallas_call(body, grid=(4,), name=name, in_specs=[blk] * (4 * n), out_specs=[blk] * (4 * n),
                          out_shape=[SDS((R, C), F32)] * (4 * n), compiler_params=_cp("parallel"))(*ws, *gs, *ms, *vs)
    return outs[:n], outs[n:2 * n], outs[2 * n:3 * n], outs[3 * n:]


BIG = ("w_in", "w_out", "w_xq", "w_xk", "w_xv", "w_xo", "w_up", "w_down")
COL_SHARDED = ("w_in", "w_up")
ANY = pl.BlockSpec(memory_space=pl.ANY)


def _place():
    x, y, c = lax.axis_index("x"), lax.axis_index("y"), lax.axis_index("c")
    chips = [(1 - x, y), (x, 1 - y), (1 - x, 1 - y)]
    return x, y, c, chips


def _window(full, name, R, C, shard, half):
    r0, nr = (0, R) if half is None else (half * (R // 2), R // 2)
    if name in COL_SHARDED:
        return full.at[pl.ds(r0, nr), pl.ds(shard * C, C)]
    return full.at[pl.ds(shard * R + r0, nr), :]


def same_shape_groups(names, arrays):
    groups = {}
    for n in names:
        groups.setdefault(tuple(arrays[n].shape), []).append(n)
    return list(groups.values())


def place_shards(ws, mine_arr, col, name):
    n = len(ws)
    R, C = ws[0].shape

    def body(s_ref, *refs):
        for p in range(n):
            refs[n + p][...] = refs[p][...].astype(BF16)

    grid_spec = pltpu.PrefetchScalarGridSpec(
        num_scalar_prefetch=1, grid=(1,), in_specs=[pl.BlockSpec((R, C), lambda i, s: (0, 0))] * n,
        out_specs=[pl.BlockSpec((R, C), (lambda i, s: (0, s[0])) if col else (lambda i, s: (s[0], 0)))] * n)
    return pl.pallas_call(body, grid_spec=grid_spec, name=name,
                          out_shape=[SDS((R, 4 * C) if col else (4 * R, C), BF16)] * n,
                          compiler_params=_cp("arbitrary"))(mine_arr, *ws)


def _gather_jobs(names, shapes):
    nw = len(names)

    def start(full, sems):
        send, recv, fsend, frecv = sems
        x, y, c, chips = _place()
        mine = 2 * x + y
        for w, n in enumerate(names):
            R, C = shapes[w]
            own = _window(full[w], n, R, C, mine, c)
            for j, chip in enumerate(chips):
                pltpu.make_async_remote_copy(src_ref=own, dst_ref=own, send_sem=send.at[w, j], recv_sem=recv.at[w, j],
                                             device_id=(*chip, c), device_id_type=MESH).start()

    def mid(full, sems, lo=0, hi=nw):
        send, recv, fsend, frecv = sems
        x, y, c, chips = _place()
        sib = (x, y, 1 - c)
        for w, n in list(enumerate(names))[lo:hi]:
            R, C = shapes[w]
            for j, chip in enumerate(chips):
                landed = _window(full[w], n, R, C, 2 * chip[0] + chip[1], c)
                pltpu.make_async_remote_copy(src_ref=landed, dst_ref=landed, send_sem=send.at[w, j], recv_sem=recv.at[w, j],
                                             device_id=(*chip, c), device_id_type=MESH).wait_recv()
                pltpu.make_async_remote_copy(src_ref=landed, dst_ref=landed, send_sem=fsend.at[w, j],
                                             recv_sem=frecv.at[w, j], device_id=sib, device_id_type=MESH).start()

    def finish(full, sems):
        send, recv, fsend, frecv = sems
        x, y, c, chips = _place()
        mine = 2 * x + y
        sib = (x, y, 1 - c)
        for w, n in enumerate(names):
            R, C = shapes[w]
            own = _window(full[w], n, R, C, mine, c)
            for j, chip in enumerate(chips):
                landed = _window(full[w], n, R, C, 2 * chip[0] + chip[1], c)
                other = _window(full[w], n, R, C, 2 * chip[0] + chip[1], 1 - c)
                pltpu.make_async_remote_copy(src_ref=other, dst_ref=other, send_sem=fsend.at[w, j], recv_sem=frecv.at[w, j],
                                             device_id=sib, device_id_type=MESH).wait_recv()
                pltpu.make_async_remote_copy(src_ref=own, dst_ref=own, send_sem=send.at[w, j], recv_sem=recv.at[w, j],
                                             device_id=(*chip, c), device_id_type=MESH).wait_send()
                pltpu.make_async_remote_copy(src_ref=landed, dst_ref=landed, send_sem=fsend.at[w, j],
                                             recv_sem=frecv.at[w, j], device_id=sib, device_id_type=MESH).wait_send()

    return start, mid, finish, [pltpu.SemaphoreType.DMA((nw, 3))] * 4


class CommJob:
    def __init__(self, ins, out_shapes, inplace, start, finish, sems, mid=None):
        self.ins, self.out_shapes, self.inplace = list(ins), list(out_shapes), inplace
        self.start, self.mid, self.finish, self.sems = start, mid, finish, list(sems)

    def run(self, phase, ins, outs, sems):
        if phase == 0:
            self.start(ins, outs, sems)
        elif phase == 1:
            if self.mid is not None:
                self.mid(ins, outs, sems)
        else:
            self.finish(ins, outs, sems)


class Carried:
    def __init__(self, job, n_in, n_out, n_scratch, steps):
        self.job, self.n_in, self.n_out, self.n_scratch, self.steps = job, n_in, n_out, n_scratch, steps
        self.nji = len(job.ins) if job else 0
        self.njo = len(job.out_shapes) if job else 0

    def in_specs(self):
        return [ANY] * self.nji

    def out_specs(self):
        return [ANY] * self.njo

    def out_shapes(self):
        return list(self.job.out_shapes) if self.job else []

    def aliases(self):
        if not (self.job and self.job.inplace):
            return {}
        return {self.n_in + i: self.n_out + i for i in range(self.nji)}

    def sems(self):
        return list(self.job.sems) if self.job else []

    def args(self):
        return list(self.job.ins) if self.job else []

    def split(self, refs):
        a = self.n_in
        b = a + self.nji
        c = b + self.n_out
        d = c + self.njo
        e = d + self.n_scratch
        return refs[:a] + refs[b:c] + refs[d:e], (refs[a:b], refs[c:d], refs[e:])

    def phase(self, phase, step, jrefs):
        if self.job is None:
            return
        at = {0: 0, 1: max(self.steps - 2, 0), 2: self.steps - 1}[phase]

        @pl.when(step == at)
        def _():
            self.job.run(phase, *jrefs)


def _pcall(body, args, *, grid, in_specs, out_specs, out_shape, name, sem, scratch=(), aliases=None, job=None):
    n_in, n_out = len(args), len(out_shape)
    if job is None:
        return pl.pallas_call(
            body, grid=grid, in_specs=list(in_specs), out_specs=list(out_specs), out_shape=list(out_shape), name=name,
            scratch_shapes=list(scratch), input_output_aliases=dict(aliases or {}), compiler_params=_cp(*sem))(*args)
    total = int(np.prod(grid))
    car = Carried(job, n_in, n_out, len(scratch), total)

    def wrapped(*refs):
        main, jrefs = car.split(refs)
        lin = pl.program_id(0)
        for ax in range(1, len(grid)):
            lin = lin * grid[ax] + pl.program_id(ax)
        car.phase(0, lin, jrefs)
        body(*main)
        car.phase(1, lin, jrefs)
        car.phase(2, lin, jrefs)

    res = pl.pallas_call(
        wrapped, grid=grid, in_specs=list(in_specs) + car.in_specs(), out_specs=list(out_specs) + car.out_specs(),
        out_shape=list(out_shape) + car.out_shapes(), name=name, scratch_shapes=list(scratch) + car.sems(),
        input_output_aliases={**dict(aliases or {}), **car.aliases()},
        compiler_params=_cp(*(["arbitrary"] * len(grid))))(*args, *car.args())
    job.landed = list(res[n_out:])
    return list(res[:n_out])


def gather_job(placed, names):
    shapes = []
    for n in names:
        R, C = placed[n].shape
        shapes.append((R, C // 4) if n in COL_SHARDED else (R // 4, C))
    start, mid, finish, sems = _gather_jobs(names, shapes)
    arrays = [placed[n] for n in names]
    early = max(len(names) - 1, 1)

    def last(i, o, s):
        mid(o, s, early, len(names))
        finish(o, s)

    return CommJob(arrays, [SDS(a.shape, a.dtype) for a in arrays], True,
                   lambda i, o, s: start(o, s), last, sems, mid=lambda i, o, s: mid(o, s, 0, early))


def gather_rows_job(placed):
    widths = [a.shape[1] // 4 for a in placed]

    def copies(outs, sems):
        x, y, c, chips = _place()
        mine = 2 * x + y
        cps = []
        for w, ref in enumerate(outs):
            own = ref.at[:, pl.ds(mine * widths[w], widths[w])]
            for j, chip in enumerate(chips):
                theirs = ref.at[:, pl.ds((2 * chip[0] + chip[1]) * widths[w], widths[w])]
                kw = dict(send_sem=sems[0].at[w, j], recv_sem=sems[1].at[w, j], device_id=(*chip, c), device_id_type=MESH)
                cps.append((pltpu.make_async_remote_copy(src_ref=own, dst_ref=own, **kw),
                            pltpu.make_async_remote_copy(src_ref=theirs, dst_ref=theirs, **kw)))
        return cps

    def start(ins, outs, sems):
        for send, _ in copies(outs, sems):
            send.start()

    def finish(ins, outs, sems):
        for send, recv in copies(outs, sems):
            recv.wait_recv()
            send.wait_send()

    return CommJob(placed, [SDS(a.shape, a.dtype) for a in placed], True, start, finish,
                   [pltpu.SemaphoreType.DMA((len(placed), 3))] * 2)


def merge_jobs(a, b):
    assert a.inplace and b.inplace
    na, nb_, nsa = len(a.ins), len(b.ins), len(a.sems)

    def phase(which):
        def run(ins, outs, sems):
            for job, i, o, s in ((a, ins[:na], outs[:na], sems[:nsa]), (b, ins[na:], outs[na:], sems[nsa:])):
                fn = getattr(job, which)
                if fn is not None:
                    fn(i, o, s)
        return run

    merged = CommJob(a.ins + b.ins, a.out_shapes + b.out_shapes, True, phase("start"), phase("finish"), a.sems + b.sems,
                     mid=phase("mid"))
    merged.parts = (a, b, na)
    return merged


def split_landed(merged):
    a, b, na = merged.parts
    a.landed, b.landed = merged.landed[:na], merged.landed[na:]


def pair_exchange_job(grads, names):
    shapes = [grads[n].shape for n in names]

    def copies(ins, outs, sems):
        x, y, c, _ = _place()
        return [pltpu.make_async_remote_copy(
            src_ref=ins[w].at[:, pl.ds((1 - c) * (shapes[w][1] // 2), shapes[w][1] // 2), :], dst_ref=outs[w],
            send_sem=sems[0].at[w], recv_sem=sems[1].at[w], device_id=(x, y, 1 - c), device_id_type=MESH)
            for w in range(len(names))]

    def start(ins, outs, sems):
        for cp in copies(ins, outs, sems):
            cp.start()

    def finish(ins, outs, sems):
        for cp in copies(ins, outs, sems):
            cp.wait()

    return CommJob([grads[n] for n in names], [SDS((4, s[1] // 2, s[2]), F32) for s in shapes], False, start, finish,
                   [pltpu.SemaphoreType.DMA((len(names),))] * 2)


def chip_exchange_job(p16, names):
    shapes = [p16[n].shape for n in names]

    def copies(ins, outs, sems):
        x, y, c, chips = _place()
        return [pltpu.make_async_remote_copy(
            src_ref=ins[w].at[2 * chip[0] + chip[1]], dst_ref=outs[w].at[j],
            send_sem=sems[0].at[w, j], recv_sem=sems[1].at[w, j], device_id=(*chip, c), device_id_type=MESH)
            for w in range(len(names)) for j, chip in enumerate(chips)]

    def start(ins, outs, sems):
        for cp in copies(ins, outs, sems):
            cp.start()

    def finish(ins, outs, sems):
        for cp in copies(ins, outs, sems):
            cp.wait()

    return CommJob([p16[n] for n in names], [SDS((3,) + tuple(s[1:]), BF16) for s in shapes], False, start, finish,
                   [pltpu.SemaphoreType.DMA((len(names), 3))] * 2)


def half_exchange_job(full, names):
    shapes = [full[n].shape for n in names]

    def copies(outs, sems):
        x, y, c, _ = _place()
        cps = []
        for w in range(len(names)):
            Rh = shapes[w][0] // 2
            rows = outs[w].at[pl.ds(c * Rh, Rh), :]
            other = outs[w].at[pl.ds((1 - c) * Rh, Rh), :]
            cps.append((pltpu.make_async_remote_copy(src_ref=rows, dst_ref=rows, send_sem=sems[0].at[w], recv_sem=sems[1].at[w],
                                                     device_id=(x, y, 1 - c), device_id_type=MESH),
                        pltpu.make_async_remote_copy(src_ref=other, dst_ref=other, send_sem=sems[0].at[w], recv_sem=sems[1].at[w],
                                                     device_id=(x, y, 1 - c), device_id_type=MESH)))
        return cps

    def start(ins, outs, sems):
        for send, _ in copies(outs, sems):
            send.start()

    def finish(ins, outs, sems):
        for send, recv in copies(outs, sems):
            recv.wait_recv()
            send.wait_send()

    arrays = [full[n] for n in names]
    return CommJob(arrays, [SDS(a.shape, a.dtype) for a in arrays], True, start, finish,
                   [pltpu.SemaphoreType.DMA((len(names),))] * 2)


def allreduce_small(parts, name, job=None):
    n = len(parts)
    nji = len(job.ins) if job else 0
    njo = len(job.out_shapes) if job else 0

    def body(*refs):
        ins, jins = refs[:n], refs[n:n + nji]
        outs, jouts = refs[n + nji:2 * n + nji], refs[2 * n + nji:2 * n + nji + njo]
        bufs = refs[2 * n + nji + njo:3 * n + nji + njo]
        send, recv = refs[3 * n + nji + njo:3 * n + nji + njo + 2]
        jsems = refs[3 * n + nji + njo + 2:]
        x, y, c, _ = _place()
        me = 4 * x + 2 * y + c
        if job is not None:
            job.run(0, jins, jouts, jsems)
        sends = []
        for p in range(n):
            for k in range(1, 8):
                peer = (x ^ (k >> 2), y ^ ((k >> 1) & 1), c ^ (k & 1))
                cp = pltpu.make_async_remote_copy(src_ref=ins[p], dst_ref=bufs[p].at[k], send_sem=send.at[p, k - 1],
                                                  recv_sem=recv.at[p, k - 1], device_id=peer, device_id_type=MESH)
                cp.start()
                sends.append(cp)
            bufs[p][0] = ins[p][...]
        for cp in sends:
            cp.wait_recv()
        for p in range(n):
            acc = bufs[p][me]
            for d in range(1, 8):
                acc = acc + bufs[p][d ^ me]
            outs[p][...] = acc
        for cp in sends:
            cp.wait_send()
        if job is not None:
            job.run(1, jins, jouts, jsems)
            job.run(2, jins, jouts, jsems)

    vm = pl.BlockSpec(memory_space=pltpu.VMEM)
    res = pl.pallas_call(
        body, name=name, in_specs=[vm] * n + [ANY] * nji, out_specs=[vm] * n + [ANY] * njo,
        out_shape=[SDS(a.shape, F32) for a in parts] + (list(job.out_shapes) if job else []),
        input_output_aliases={n + i: n + i for i in range(nji)} if (job and job.inplace) else {},
        scratch_shapes=[pltpu.VMEM((8,) + tuple(a.shape), F32) for a in parts] + [pltpu.SemaphoreType.DMA((n, 7))] * 2
        + (list(job.sems) if job else []),
    )(*parts, *(job.ins if job else []))
    if job is not None:
        job.landed = list(res[n:])
    return list(res[:n])


def adamw_multi(ws, gs, ms, vs, name):
    n = len(ws)

    def body(*refs):
        for p in range(n):
            _adam_update(*(refs[q * n + p] for q in range(7)))

    vm = pl.BlockSpec(memory_space=pltpu.VMEM)
    outs = pl.pallas_call(body, name=name, in_specs=[vm] * (4 * n), out_specs=[vm] * (3 * n),
                          out_shape=[SDS(a.shape, F32) for a in ws] * 3)(*ws, *gs, *ms, *vs)
    return outs[:n], outs[n:2 * n], outs[2 * n:]


def pair_sum(gs, xhs, sel_arr, name):
    n = len(gs)
    _, R, C = gs[0].shape
    Rh = R // 2

    def body(sel_ref, *refs):
        for p in range(n):
            g_ref, x_ref, p32_ref, p16_ref = (refs[q * n + p] for q in range(4))
            s = g_ref[...] + x_ref[...]
            p16_ref[...] = s.astype(BF16)

            @pl.when(pl.program_id(0) == sel_ref[0])
            def _():
                p32_ref[...] = s

    same = pl.BlockSpec((None, Rh, C), lambda s, sel: (s, 0, 0))
    grid_spec = pltpu.PrefetchScalarGridSpec(
        num_scalar_prefetch=1, grid=(4,),
        in_specs=[pl.BlockSpec((None, Rh, C), lambda s, sel: (s, sel[1], 0))] * n + [same] * n,
        out_specs=[pl.BlockSpec((Rh, C), lambda s, sel: (0, 0))] * n + [same] * n)
    outs = pl.pallas_call(body, grid_spec=grid_spec, name=name,
                          out_shape=[SDS((Rh, C), F32)] * n + [SDS((4, Rh, C), BF16)] * n,
                          compiler_params=_cp("arbitrary"))(sel_arr, *gs, *xhs)
    return outs[:n], outs[n:]


def chip_sum(p32s, ys, sel_arr, name):
    n = len(p32s)
    Rh, C = p32s[0].shape

    def body(s_ref, *refs):
        for p in range(n):
            p_ref, y_ref, o_ref = (refs[q * n + p] for q in range(3))
            o_ref[...] = ((p_ref[...] + y_ref[0].astype(F32)) + y_ref[1].astype(F32)) + y_ref[2].astype(F32)

    grid_spec = pltpu.PrefetchScalarGridSpec(
        num_scalar_prefetch=1, grid=(1,),
        in_specs=[pl.BlockSpec((Rh, C), lambda i, s: (0, 0))] * n + [pl.BlockSpec((3, Rh, C), lambda i, s: (0, 0, 0))] * n,
        out_specs=[pl.BlockSpec((Rh, C), lambda i, s: (s[1], 0))] * n)
    return pl.pallas_call(body, grid_spec=grid_spec, name=name, out_shape=[SDS((2 * Rh, C), F32)] * n,
                          compiler_params=_cp("arbitrary"))(sel_arr, *p32s, *ys)


class StepComm:
    FIRST = ("w_in",)
    SQUARE = ("w_out", "w_xq", "w_xk", "w_xv", "w_xo")
    FFN = ("w_up", "w_down")
    PLAN = {
        "proj": ("gather", SQUARE[:3]), "attn_fwd": ("gather", SQUARE[3:] + FFN[:1]), "up_act_fwd": ("gather", FFN[1:]),
        "ffn_act_bwd": ("pair", FFN[1:]), "dw_up": ("chip", FFN[1:]),
        "d_h3": ("pair", FFN[:1]), "mixer_post_bwd_a": ("pair", SQUARE), "attn_bwd": ("chip", FFN[:1] + SQUARE),
        "dw_in": ("half", FFN + SQUARE), "bias_tables_bwd": ("pair", FIRST), "d_h1": ("chip", FIRST),
    }

    def __init__(self, placed, conv_placed, sel_arr):
        self.placed, self.conv_placed, self.sel_arr = placed, conv_placed, sel_arr
        self.g, self.p32, self.p16, self.full, self.done, self.jobs = {}, {}, {}, {}, {}, {}

    def job(self, tag, gb=None):
        if tag == "norm_mix":
            self.jobs[tag] = merge_jobs(gather_job(self.placed, self.FIRST), gather_rows_job(self.conv_placed))
            return self.jobs[tag]
        kind, names = self.PLAN[tag]
        if kind == "gather":
            j = gather_job(self.placed, names)
        elif kind == "pair":
            for n in names:
                self.g[n] = gb[n] if gb[n].ndim == 3 else gb[n].reshape(4, gb[n].shape[0] // 4, gb[n].shape[1])
            j = pair_exchange_job(self.g, names)
        elif kind == "chip":
            j = chip_exchange_job(self.p16, names)
        else:
            j = half_exchange_job(self.full, names)
        self.jobs[tag] = j
        return j

    def landed(self, tag, wb=None, ws=None):
        if tag == "norm_mix":
            split_landed(self.jobs[tag])
            first, conv, _ = self.jobs[tag].parts
            wb.update(zip(self.FIRST, first.landed))
            ws.update((n, a[:3]) for n, a in zip(SMALL_SHARDED, conv.landed))
            return
        kind, names = self.PLAN[tag]
        got = dict(zip(names, self.jobs[tag].landed))
        if kind == "gather":
            wb.update(got)
        elif kind == "half":
            self.done.update(got)
        for group in same_shape_groups(names, got):
            if kind == "pair":
                p32s, p16s = pair_sum([self.g[n] for n in group], [got[n] for n in group], self.sel_arr, "pair_sum_" + group[0])
                self.p32.update(zip(group, p32s))
                self.p16.update(zip(group, p16s))
            elif kind == "chip":
                fulls = chip_sum([self.p32[n] for n in group], [got[n] for n in group], self.sel_arr, "chip_sum_" + group[0])
                self.full.update(zip(group, fulls))

    def finish(self, gb):
        return self.done

    def last_job(self):
        self.jobs["end_half"] = half_exchange_job(self.full, self.FIRST)
        return self.jobs["end_half"]

    def take_last(self):
        self.done.update(zip(self.FIRST, self.jobs["end_half"].landed))


class NoComm:
    def job(self, tag, gb=None):
        return None

    def landed(self, tag, wb=None, ws=None):
        pass

    def finish(self, gb):
        return gb


SMALL = ("rel_bias", "g_mix", "w_short_conv", "g_attn_out", "g_conv_out", "g_xattn", "g_mem", "g_ffn",
         "w_ffn_conv", "b_ffn_conv", "g_final")
SMALL_SHARDED = ("w_short_conv", "w_ffn_conv")


def kernel(x, mem, rel_bias, g_mix, w_in, w_short_conv, g_attn_out, g_conv_out, w_out, g_xattn, g_mem, w_xq, w_xk, w_xv, w_xo, g_ffn, w_up, w_ffn_conv, b_ffn_conv, w_down, g_final, loss_target, m_rel_bias, m_g_mix, m_w_in, m_w_short_conv, m_g_attn_out, m_g_conv_out, m_w_out, m_g_xattn, m_g_mem, m_w_xq, m_w_xk, m_w_xv, m_w_xo, m_g_ffn, m_w_up, m_w_ffn_conv, m_b_ffn_conv, m_w_down, m_g_final, v_rel_bias, v_g_mix, v_w_in, v_w_short_conv, v_g_attn_out, v_g_conv_out, v_w_out, v_g_xattn, v_g_mem, v_w_xq, v_w_xk, v_w_xv, v_w_xo, v_g_ffn, v_w_up, v_w_ffn_conv, v_b_ffn_conv, v_w_down, v_g_final):
    names = ("rel_bias", "g_mix", "w_in", "w_short_conv", "g_attn_out", "g_conv_out", "w_out", "g_xattn", "g_mem", "w_xq",
             "w_xk", "w_xv", "w_xo", "g_ffn", "w_up", "w_ffn_conv", "b_ffn_conv", "w_down", "g_final")
    W = dict(zip(names, (rel_bias, g_mix, w_in, w_short_conv, g_attn_out, g_conv_out, w_out, g_xattn, g_mem, w_xq, w_xk, w_xv,
                         w_xo, g_ffn, w_up, w_ffn_conv, b_ffn_conv, w_down, g_final)))
    M = dict(zip(names, (m_rel_bias, m_g_mix, m_w_in, m_w_short_conv, m_g_attn_out, m_g_conv_out, m_w_out, m_g_xattn, m_g_mem,
                         m_w_xq, m_w_xk, m_w_xv, m_w_xo, m_g_ffn, m_w_up, m_w_ffn_conv, m_b_ffn_conv, m_w_down, m_g_final)))
    V = dict(zip(names, (v_rel_bias, v_g_mix, v_w_in, v_w_short_conv, v_g_attn_out, v_g_conv_out, v_w_out, v_g_xattn, v_g_mem,
                         v_w_xq, v_w_xk, v_w_xv, v_w_xo, v_g_ffn, v_w_up, v_w_ffn_conv, v_b_ffn_conv, v_w_down, v_g_final)))
    xi, yi, ci = lax.axis_index("x"), lax.axis_index("y"), lax.axis_index("c")
    mine = 2 * xi + yi
    mine_arr = jnp.reshape(mine, (1,)).astype(jnp.int32)
    sel_arr = jnp.stack([mine, ci]).astype(jnp.int32)
    conv_placed = []
    for n in SMALL_SHARDED:
        shard = W[n][0]
        conv_placed.append(lax.dynamic_update_slice(jnp.zeros((8, 4 * shard.shape[1]), F32), shard, (0, mine * shard.shape[1])))
    shards = {n: W[n][0] for n in BIG}
    placed = {}
    for group in same_shape_groups(BIG, shards):
        outs = place_shards([shards[n] for n in group], mine_arr, group[0] in COL_SHARDED, "place_" + group[0])
        placed.update(zip(group, outs))
    comm = StepComm(placed, conv_placed, sel_arr)
    ws = {n: W[n] for n in SMALL if n not in SMALL_SHARDED}

    loss, grad_x, gfull, gs = local_step(x[0], mem[0], loss_target[0], None, ws, comm)

    def as2d(a):
        return a.reshape(1, -1) if a.ndim == 1 else a

    reduced = allreduce_small([as2d(gs[n]) for n in SMALL] + [loss], "reduce_small", job=comm.last_job())
    comm.take_last()
    gsm = dict(zip(SMALL, reduced[:-1]))
    loss = reduced[-1][0, 0]

    grads, delta, new_m, new_v = {}, {}, {}, {}
    for group in same_shape_groups(BIG, gfull):
        gouts, ds, nms, nvs = adamw([W[n][0] for n in group], [gfull[n] for n in group], [M[n][0] for n in group],
                                    [V[n][0] for n in group], "adamw_" + group[0])
        for n, g, d, nm, nv in zip(group, gouts, ds, nms, nvs):
            grads[n], delta[n], new_m[n], new_v[n] = g[None], d[None], nm[None], nv[None]
    for n in SMALL_SHARDED:
        wid = W[n].shape[2]
        gsm[n] = lax.dynamic_slice(gsm[n], (0, mine * wid), (3, wid))

    def own(a, n):
        return a[0] if n in SMALL_SHARDED else as2d(a)

    d, nm, nv = adamw_multi([own(W[n], n) for n in SMALL], [gsm[n] for n in SMALL], [own(M[n], n) for n in SMALL],
                            [own(V[n], n) for n in SMALL], "adamw_small")
    for i, n in enumerate(SMALL):
        shape = W[n].shape
        grads[n], delta[n], new_m[n], new_v[n] = (a.reshape(shape) for a in (gsm[n], d[i], nm[i], nv[i]))
    return (loss, grad_x[None], *[grads[n] for n in names], *[delta[n] for n in names],
            *[new_m[n] for n in names], *[new_v[n] for n in names])
```

```python
import math

import numpy as np
import jax
import jax.numpy as jnp
from jax import lax
from jax.experimental import pallas as pl
from jax.experimental.pallas import tpu as pltpu

F32 = jnp.float32
BF16 = jnp.bfloat16
SDS = jax.ShapeDtypeStruct
MESH = pl.DeviceIdType.MESH

D_MODEL = 1024
ATTN_W = 512
N_HEADS = 8
HEAD_DIM = 64
HALF_HEAD = HEAD_DIM // 2
Q_SCALE = HEAD_DIM ** -0.5
BLK = 128
PATTERNS = ((128, 1), (512, 4), (2048, 16))
N_BUCKETS = 32
BUCKET_MAX_DIST = 2048
D_FF = 2816
N_MEM_HEADS = 4
MEM_HD = 256
EPS = 1e-6
NEG = -1e30
VMEM_LIMIT = 56 * 1024 * 1024

ADAM_LR, ADAM_B1, ADAM_B2, ADAM_EPS, ADAM_WD, ADAM_STEP = 0.001, 0.9, 0.999, 1e-08, 0.01, 10


def _cp(*sem):
    return pltpu.CompilerParams(dimension_semantics=sem, vmem_limit_bytes=VMEM_LIMIT)


def _dot(a, b):
    return jnp.dot(a, b, preferred_element_type=F32)


def _dot_nt(a, b):
    return lax.dot_general(a, b, (((1,), (1,)), ((), ())), preferred_element_type=F32)


def _dot_tn(a, b):
    return lax.dot_general(a, b, (((0,), (0,)), ((), ())), preferred_element_type=F32)


def _rsum(x):
    return jnp.sum(x, axis=1, keepdims=True)


def rmsnorm_fwd(x, g, name, job=None):
    S, Dm = x.shape
    tm = min(S, 512)

    def body(x_ref, g_ref, o_ref):
        xv = x_ref[...]
        r = lax.rsqrt(jnp.mean(xv * xv, axis=1, keepdims=True) + EPS)
        o_ref[...] = (xv * r * g_ref[...]).astype(o_ref.dtype)

    return _pcall(body, [x, g], grid=(S // tm,), name=name,
                  in_specs=[pl.BlockSpec((tm, Dm), lambda i: (i, 0)), pl.BlockSpec((1, Dm), lambda i: (0, 0))],
                  out_specs=[pl.BlockSpec((tm, Dm), lambda i: (i, 0))], out_shape=[SDS((S, Dm), BF16)],
                  sem=("parallel",), job=job)[0]


def rmsnorm_gain_grad(x, dh, name):
    S, Dm = x.shape
    tm = min(S, 512)

    def body(x_ref, dh_ref, dg_ref):
        xv = x_ref[...]
        xh = xv * lax.rsqrt(jnp.mean(xv * xv, axis=1, keepdims=True) + EPS)

        @pl.when(pl.program_id(0) == 0)
        def _():
            dg_ref[...] = jnp.zeros_like(dg_ref)

        dg_ref[...] += jnp.sum(dh_ref[...].astype(F32) * xh, axis=0, keepdims=True)

    row = pl.BlockSpec((tm, Dm), lambda i: (i, 0))
    return pl.pallas_call(
        body, grid=(S // tm,), name=name, in_specs=[row, row], out_specs=pl.BlockSpec((1, Dm), lambda i: (0, 0)),
        out_shape=SDS((1, Dm), F32), compiler_params=_cp("arbitrary"))(x, dh)


def _col_spec(bw, tn, rows, row_of, col_of):
    if bw is None:
        return pl.BlockSpec((rows, tn), lambda *g: (row_of(*g), col_of(*g)))
    if tn % bw == 0:
        return pl.BlockSpec((tn // bw, rows, bw), lambda *g: (col_of(*g), row_of(*g), 0))
    per = bw // tn
    return pl.BlockSpec((None, rows, tn), lambda *g: (col_of(*g) // per, row_of(*g), col_of(*g) % per))


def _read_cols(ref, bw, tn):
    if bw is None or tn % bw != 0:
        return ref[...]
    if tn == bw:
        return ref[0]
    return jnp.concatenate([ref[q] for q in range(tn // bw)], axis=1)


def _write_cols(ref, bw, tn, val):
    if bw is None or tn % bw != 0:
        ref[...] = val.astype(ref.dtype)
    else:
        for q in range(tn // bw):
            ref[q] = val[:, q * bw:(q + 1) * bw].astype(ref.dtype)


def mm_nn(a, b, name, *, out_dtype=F32, out_bw=None, tm=1024, tn=512, job=None):
    M, K = a.shape
    N = b.shape[1]
    tm = min(tm, M)

    def body(a_ref, b_ref, o_ref):
        _write_cols(o_ref, out_bw, tn, _dot(a_ref[...].astype(BF16), b_ref[...]))

    ri, ci = (lambda i, j: i), (lambda i, j: j)
    in_specs = [pl.BlockSpec((tm, K), lambda i, j: (i, 0)), pl.BlockSpec((K, tn), lambda i, j: (0, j))]
    oshape = (M, N) if out_bw is None else (N // out_bw, M, out_bw)
    return _pcall(body, [a, b], grid=(M // tm, N // tn), name=name, in_specs=in_specs,
                  out_specs=[_col_spec(out_bw, tn, tm, ri, ci)], out_shape=[SDS(oshape, out_dtype)],
                  sem=("parallel", "parallel"), job=job)[0]


def mm_nn_rows(a, b, res, g, name, *, target=None, tm=512):
    M, K = a.shape
    N = b.shape[1]

    def body(*refs):
        a_ref, b_ref, r_ref, g_ref = refs[:4]
        xv = r_ref[...] + _dot(a_ref[...].astype(BF16), b_ref[...])
        r = lax.rsqrt(jnp.mean(xv * xv, axis=1, keepdims=True) + EPS)
        xh = xv * r
        if target is None:
            x_ref, h_ref = refs[4:]
            x_ref[...] = xv
            h_ref[...] = (xh * g_ref[...]).astype(BF16)
            return
        t_ref, loss_ref, dx_ref, dg_ref = refs[4:]
        err = xh * g_ref[...] - t_ref[...]
        dy = err / N
        gd = dy * g_ref[...]
        dx_ref[...] = r * (gd - xh * jnp.mean(gd * xh, axis=1, keepdims=True))

        @pl.when(pl.program_id(0) == 0)
        def _():
            dg_ref[...] = jnp.zeros_like(dg_ref)
            loss_ref[...] = jnp.zeros_like(loss_ref)

        dg_ref[...] += _colsum(dy * xh)
        loss_ref[...] += 0.5 * _colsum(jnp.mean(err * err, axis=1, keepdims=True))

    row = pl.BlockSpec((tm, N), lambda i: (i, 0))
    vec = pl.BlockSpec((1, N), lambda i: (0, 0))
    in_specs = [pl.BlockSpec((tm, K), lambda i: (i, 0)), pl.BlockSpec((K, N), lambda i: (0, 0)), row, vec]
    if target is None:
        return pl.pallas_call(body, grid=(M // tm,), name=name, in_specs=in_specs, out_specs=[row, row],
                              out_shape=[SDS((M, N), F32), SDS((M, N), BF16)], compiler_params=_cp("parallel"))(a, b, res, g)
    one = pl.BlockSpec((1, 1), lambda i: (0, 0))
    return pl.pallas_call(body, grid=(M // tm,), name=name, in_specs=in_specs + [row], out_specs=[one, row, vec],
                          out_shape=[SDS((1, 1), F32), SDS((M, N), F32), SDS((1, N), F32)],
                          compiler_params=_cp("arbitrary"))(a, b, res, g, target)


def mm_nt(a, a_bw, b, name, *, res=None, out_dtype=F32, tm=1024, tn=512, job=None):
    if a_bw is None:
        M, K = a.shape
    else:
        M, K = a.shape[1], a.shape[0] * a_bw
    N = b.shape[0]
    tm = min(tm, M)

    def body(*refs):
        if res is None:
            a_ref, b_ref, o_ref = refs
        else:
            a_ref, b_ref, r_ref, o_ref = refs
        av = _read_cols(a_ref, a_bw, K).astype(BF16)
        acc = _dot_nt(av, b_ref[...])
        if res is not None:
            acc = acc + r_ref[...]
        o_ref[...] = acc.astype(o_ref.dtype)

    in_specs = [_col_spec(a_bw, K, tm, lambda i, j: i, lambda i, j: 0), pl.BlockSpec((tn, K), lambda i, j: (j, 0))]
    args = [a, b]
    if res is not None:
        in_specs.append(pl.BlockSpec((tm, tn), lambda i, j: (i, j)))
        args.append(res)
    return _pcall(body, args, grid=(M // tm, N // tn), name=name, in_specs=in_specs,
                  out_specs=[pl.BlockSpec((tm, tn), lambda i, j: (i, j))], out_shape=[SDS((M, N), out_dtype)],
                  sem=("parallel", "parallel"), job=job)[0]


def mm_nt_norm_bwd(a, a_bw, b, x, g, res, name, *, tm=512, job=None):
    if a_bw is None:
        M, K = a.shape
    else:
        M, K = a.shape[1], a.shape[0] * a_bw
    N = b.shape[0]

    def body(a_ref, b_ref, x_ref, g_ref, r_ref, dx_ref, dg_ref):
        dh = _dot_nt(_read_cols(a_ref, a_bw, K).astype(BF16), b_ref[...])
        xv = x_ref[...]
        r = lax.rsqrt(jnp.mean(xv * xv, axis=1, keepdims=True) + EPS)
        xh = xv * r
        gd = dh * g_ref[...]
        dx_ref[...] = r_ref[...] + r * (gd - xh * jnp.mean(gd * xh, axis=1, keepdims=True))

        @pl.when(pl.program_id(0) == 0)
        def _():
            dg_ref[...] = jnp.zeros_like(dg_ref)

        dg_ref[...] += jnp.sum(dh * xh, axis=0, keepdims=True)

    row = pl.BlockSpec((tm, N), lambda i: (i, 0))
    vec = pl.BlockSpec((1, N), lambda i: (0, 0))
    return _pcall(body, [a, b, x, g, res], grid=(M // tm,), name=name,
                  in_specs=[_col_spec(a_bw, K, tm, lambda i: i, lambda i: 0), pl.BlockSpec((N, K), lambda i: (0, 0)), row, vec, row],
                  out_specs=[row, vec], out_shape=[SDS((M, N), F32), SDS((1, N), F32)], sem=("arbitrary",), job=job)


def mm_tn(a, b, b_bw, name, *, shards=None, tm=1024, tn=1024, ts=1024, job=None):
    S, Ka = a.shape
    N = b.shape[1] if b_bw is None else b.shape[0] * b_bw
    ts = min(ts, S)
    tm = min(tm, Ka)

    def body(a_ref, b_ref, o_ref):
        @pl.when(pl.program_id(2) == 0)
        def _():
            o_ref[...] = jnp.zeros_like(o_ref)

        bv = _read_cols(b_ref, b_bw, tn).astype(BF16)
        o_ref[...] += _dot_tn(a_ref[...].astype(BF16), bv)

    in_specs = [pl.BlockSpec((ts, tm), lambda i, j, k: (k, i)),
                _col_spec(b_bw, tn, ts, lambda i, j, k: k, lambda i, j, k: j)]
    if shards is None:
        out_spec, oshape = pl.BlockSpec((tm, tn), lambda i, j, k: (i, j)), (Ka, N)
    else:
        per = (N // shards) // tn
        out_spec = pl.BlockSpec((None, tm, tn), lambda i, j, k: (j // per, i, j % per))
        oshape = (shards, Ka, N // shards)
    return _pcall(body, [a, b], grid=(Ka // tm, N // tn, S // ts), name=name, in_specs=in_specs, out_specs=[out_spec],
                  out_shape=[SDS(oshape, F32)], sem=("parallel", "parallel", "arbitrary"), job=job)[0]


KEYS = 2 * BLK


def _bucket_tables():
    out = np.zeros((3, BLK, KEYS), np.int32)
    qi = np.arange(BLK)[:, None]
    kj = np.arange(KEYS)[None, :]
    for p, (win, dil) in enumerate(PATTERNS):
        w = win // dil
        assert w == BLK
        steps = qi + w - kj
        valid = (steps >= 0) & (steps <= w)
        dist = np.clip(steps, 0, w) * dil
        dd = np.maximum(dist, 1).astype(np.float32)
        exact = N_BUCKETS // 2
        large = exact + (np.log(dd / np.float32(exact)) / np.float32(math.log(BUCKET_MAX_DIST / exact))
                         * np.float32(N_BUCKETS - exact)).astype(np.int32)
        large = np.minimum(large, N_BUCKETS - 1)
        out[p] = np.where(valid, np.where(dist < exact, dist, large), -1)
    return out


def bias_tables(rel_bias):
    bkt = jnp.asarray(_bucket_tables())

    def body(rb_ref, bkt_ref, o_ref):
        bk = bkt_ref[...]
        for h in range(N_HEADS):
            acc = jnp.full((BLK, KEYS), NEG, F32)
            for b in range(N_BUCKETS):
                acc = jnp.where(bk == b, rb_ref[h, b], acc)
            o_ref[h] = acc

    return pl.pallas_call(
        body, grid=(3,), name="bias_tables",
        in_specs=[pl.BlockSpec(memory_space=pltpu.SMEM),
                  pl.BlockSpec((None, BLK, KEYS), lambda p: (p, 0, 0))],
        out_specs=pl.BlockSpec((None, N_HEADS, BLK, KEYS), lambda p: (p, 0, 0, 0)),
        out_shape=SDS((3, N_HEADS, BLK, KEYS), F32), compiler_params=_cp("parallel"))(rel_bias, bkt)


def bias_tables_bwd(ds_sum, job=None):
    bkt = jnp.asarray(_bucket_tables())

    def body(ds_ref, bkt_ref, o_ref):
        h = pl.program_id(0)

        @pl.when(h == 0)
        def _():
            o_ref[...] = jnp.zeros_like(o_ref)

        rows = lax.broadcasted_iota(jnp.int32, (N_HEADS, N_BUCKETS), 0)
        cols = lax.broadcasted_iota(jnp.int32, (N_HEADS, N_BUCKETS), 1)
        acc = jnp.zeros((N_HEADS, N_BUCKETS), F32)
        for b in range(N_BUCKETS):
            t = jnp.zeros((BLK, KEYS), F32)
            for p in range(3):
                t = t + jnp.where(bkt_ref[p] == b, ds_ref[p], 0.0)
            tot = jnp.sum(_rsum(t), axis=0, keepdims=True)
            acc = acc + jnp.where((rows == h) & (cols == b), tot, 0.0)
        o_ref[...] += acc

    return _pcall(
        body, [ds_sum, bkt], grid=(N_HEADS,), name="bias_tables_bwd",
        in_specs=[pl.BlockSpec((3, None, BLK, KEYS), lambda h: (0, h, 0, 0)),
                  pl.BlockSpec((3, BLK, KEYS), lambda h: (0, 0, 0))],
        out_specs=[pl.BlockSpec((N_HEADS, N_BUCKETS), lambda h: (0, 0))],
        out_shape=[SDS((N_HEADS, N_BUCKETS), F32)], sem=("arbitrary",), job=job)[0]


def _rows(start, dil):
    if dil == 1:
        return pl.ds(pl.multiple_of(start, BLK), BLK)
    return pl.ds(start, BLK, stride=dil)


GRP = 8


def _group_rows(i, dil, S):
    nb = S // (BLK * dil)
    run = min(nb, GRP)
    chunks = nb // run
    res0, b0 = (i // chunks) * (GRP // run), (i % chunks) * run
    cur = [(b0 + j % run) * (BLK * dil) + res0 + j // run for j in range(GRP)]
    t = lax.broadcasted_iota(jnp.int32, (GRP, 1, 1), 0)
    if run == nb:
        before, pen = None, jnp.where(t % run == 0, NEG, 0.0)
    else:
        before = _rows(jnp.maximum(b0 - 1, 0) * (BLK * dil) + res0, dil)
        pen = jnp.where((t == 0) & (b0 == 0), NEG, 0.0)
    return [_rows(s, dil) for s in cur], before, pen


def _load_group(ref, rows):
    return jnp.stack([ref[r, :] for r in rows])


def _with_prev(ref, before, cur_blocks):
    first = cur_blocks[:1] if before is None else ref[before, :][None].astype(cur_blocks.dtype)
    return jnp.concatenate([first, cur_blocks[:-1]], axis=0)


def _bdot_nt(a, b):
    return lax.dot_general(a, b, (((2,), (2,)), ((0,), (0,))), preferred_element_type=F32)


def _bdot(a, b):
    return lax.dot_general(a, b, (((2,), (1,)), ((0,), (0,))), preferred_element_type=F32)


def _bdot_tn(a, b):
    return lax.dot_general(a, b, (((1,), (1,)), ((0,), (0,))), preferred_element_type=F32)


def _lsum(x):
    return jnp.sum(x, axis=-1, keepdims=True)


def _widen(src, dst):
    S = src.shape[1]

    def chunk(i, carry):
        rows = pl.ds(pl.multiple_of(i * 512, 512), 512)
        for a in range(3):
            dst[a, rows, :] = src[a, rows, :].astype(F32)
        return carry

    lax.fori_loop(0, S // 512, chunk, 0)


def attn_fwd(qkv, bias, job=None):
    S = qkv.shape[3]
    nblk = S // BLK
    car = Carried(job, 2, 2, 3, 4)

    def body(*refs):
        (qkv_in, bias_ref, o_ref, lse_ref, qkv_ref, part_o, part_ml), jrefs = car.split(refs)
        car.phase(0, pl.program_id(0), jrefs)
        _widen(qkv_in, qkv_ref)
        lane = lax.broadcasted_iota(jnp.int32, (GRP, BLK, BLK), 2)
        lo = lane < HEAD_DIM
        masks = (lo, jnp.logical_not(lo))
        lov = lax.broadcasted_iota(jnp.int32, (GRP, KEYS, BLK), 2) < HEAD_DIM
        vmasks = (lov, jnp.logical_not(lov))
        prev_keys = lax.broadcasted_iota(jnp.int32, (1, 1, KEYS), 2) < BLK
        q_ref, k_ref, v_ref = qkv_ref.at[0], qkv_ref.at[1], qkv_ref.at[2]
        targets = [(o_ref, lse_ref)] + [(part_o.at[t], part_ml.at[t]) for t in range(len(PATTERNS) - 1)]
        for p, (_, dil) in enumerate(PATTERNS):
            def step(i, carry, p=p, dil=dil):
                rc, before, pen = _group_rows(i, dil, S)
                q2 = _load_group(q_ref, rc) * Q_SCALE
                kc = _load_group(k_ref, rc).astype(BF16)
                keys = jnp.concatenate([_with_prev(k_ref, before, kc), kc], axis=1)
                vc = _load_group(v_ref, rc)
                vals = jnp.concatenate([_with_prev(v_ref, before, vc), vc], axis=1)
                pen = jnp.where(prev_keys, pen, 0.0)
                pv = jnp.zeros((GRP, BLK, BLK), F32)
                ms, ls = [], []
                for h in range(2):
                    qh = jnp.where(masks[h], q2, 0.0).astype(BF16)
                    s = _bdot_nt(qh, keys) + bias_ref[p, h][None] + pen
                    mn = jnp.max(s, axis=-1, keepdims=True)
                    pr = jnp.exp(s - mn)
                    pv = pv + _bdot(pr.astype(BF16), jnp.where(vmasks[h], vals, 0.0).astype(BF16))
                    ms.append(mn)
                    ls.append(_lsum(pr))
                ml_t = jnp.where(lo, jnp.where(lane < HALF_HEAD, ms[0], ls[0]),
                                 jnp.where(lane < HEAD_DIM + HALF_HEAD, ms[1], ls[1]))
                acc_ref, ml_ref = targets[p]
                for j, r in enumerate(rc):
                    acc_ref[r, :] = pv[j]
                    ml_ref[r, :] = ml_t[j]
                return carry

            lax.fori_loop(0, nblk // GRP, step, 0)

        def merge(i, carry):
            rows = pl.ds(pl.multiple_of(i * 512, 512), 512)
            is_m = (lax.broadcasted_iota(jnp.int32, (512, BLK), 1) & HALF_HEAD) == 0
            accs = [ref[rows, :] for ref, _ in targets]
            mls = [ref[rows, :] for _, ref in targets]
            mx = [jnp.where(is_m, ml, pltpu.roll(ml, HALF_HEAD, 1)) for ml in mls]
            dn = [jnp.where(is_m, pltpu.roll(ml, BLK - HALF_HEAD, 1), ml) for ml in mls]
            top = jnp.maximum(jnp.maximum(mx[0], mx[1]), mx[2])
            ws_ = [jnp.exp(m - top) for m in mx]
            den = ws_[0] * dn[0] + ws_[1] * dn[1] + ws_[2] * dn[2]
            o_ref[rows, :] = (ws_[0] * accs[0] + ws_[1] * accs[1] + ws_[2] * accs[2]) / den
            lse_ref[rows, :] = top + jnp.log(den)
            return carry

        lax.fori_loop(0, S // 512, merge, 0)
        car.phase(1, pl.program_id(0), jrefs)
        car.phase(2, pl.program_id(0), jrefs)

    outs = pl.pallas_call(
        body, grid=(4,), name="attn_fwd",
        in_specs=[pl.BlockSpec((None, 3, None, S, BLK), lambda g: (0, 0, g, 0, 0)),
                  pl.BlockSpec((3, 2, BLK, KEYS), lambda g: (0, g, 0, 0))] + car.in_specs(),
        out_specs=[pl.BlockSpec((None, S, BLK), lambda g: (g, 0, 0)),
                   pl.BlockSpec((None, S, BLK), lambda g: (g, 0, 0))] + car.out_specs(),
        out_shape=[SDS((4, S, BLK), F32), SDS((4, S, BLK), F32)] + car.out_shapes(),
        input_output_aliases=car.aliases(),
        scratch_shapes=[pltpu.VMEM((3, S, BLK), F32), pltpu.VMEM((2, S, BLK), F32), pltpu.VMEM((2, S, BLK), F32)] + car.sems(),
        compiler_params=_cp("arbitrary"))(qkv, bias, *car.args())
    if job is not None:
        job.landed = list(outs[2:])
    return outs[0], outs[1]


def attn_bwd(qkv, bias, o, lse, do, job=None):
    S = qkv.shape[3]
    nblk = S // BLK
    car = Carried(job, 5, 2, 4, 4)

    def body(*refs):
        (qkv_in, bias_ref, o_ref, lse_ref, do_ref, dqkv_out, ds_ref, qkv_ref, dqkv_ref, last_ref, st_ref), jrefs = car.split(refs)
        car.phase(0, pl.program_id(0), jrefs)
        _widen(qkv_in, qkv_ref)

        def stats(i, carry):
            rows = pl.ds(pl.multiple_of(i * 512, 512), 512)
            ln = lax.broadcasted_iota(jnp.int32, (512, BLK), 1)
            prod = do_ref[rows, :] * o_ref[rows, :]
            d0 = _rsum(jnp.where(ln < HEAD_DIM, prod, 0.0))
            d1 = _rsum(jnp.where(ln < HEAD_DIM, 0.0, prod))
            st_ref[rows, :] = jnp.where((ln & HALF_HEAD) == 0, lse_ref[rows, :], jnp.where(ln < HEAD_DIM, d0, d1))
            return carry

        lax.fori_loop(0, S // 512, stats, 0)
        lane = lax.broadcasted_iota(jnp.int32, (GRP, BLK, BLK), 2)
        lo = lane < HEAD_DIM
        masks = (lo, jnp.logical_not(lo))
        lov = lax.broadcasted_iota(jnp.int32, (GRP, KEYS, BLK), 2) < HEAD_DIM
        vmasks = (lov, jnp.logical_not(lov))
        prev_keys = lax.broadcasted_iota(jnp.int32, (1, 1, KEYS), 2) < BLK
        ds_ref[...] = jnp.zeros_like(ds_ref)
        q_ref, k_ref, v_ref = qkv_ref.at[0], qkv_ref.at[1], qkv_ref.at[2]
        last = len(PATTERNS) - 1
        for p, (_, dil) in enumerate(PATTERNS):
            def step(i, carry, p=p, dil=dil):
                tgt = last_ref if p == last else dqkv_ref
                dq_ref, dk_ref, dv_ref = tgt.at[0], tgt.at[1], tgt.at[2]
                rc, before, pen = _group_rows(i, dil, S)
                q2 = _load_group(q_ref, rc) * Q_SCALE
                kc = _load_group(k_ref, rc).astype(BF16)
                keys = jnp.concatenate([_with_prev(k_ref, before, kc), kc], axis=1)
                vc = _load_group(v_ref, rc).astype(BF16)
                vals = jnp.concatenate([_with_prev(v_ref, before, vc), vc], axis=1)
                pen = jnp.where(prev_keys, pen, 0.0)
                dot = _load_group(do_ref, rc)
                st = _load_group(st_ref, rc)
                dq = jnp.zeros((GRP, BLK, BLK), F32)
                dkeys = jnp.zeros((GRP, KEYS, BLK), F32)
                dvals = jnp.zeros((GRP, KEYS, BLK), F32)
                for h in range(2):
                    qh = jnp.where(masks[h], q2, 0.0).astype(BF16)
                    doh = jnp.where(masks[h], dot, 0.0).astype(BF16)
                    lh = st[:, :, HEAD_DIM * h:HEAD_DIM * h + 1]
                    delta = st[:, :, HEAD_DIM * h + HALF_HEAD:HEAD_DIM * h + HALF_HEAD + 1]
                    pr = jnp.exp(_bdot_nt(qh, keys) + bias_ref[p, h][None] + pen - lh)
                    ds = pr * (_bdot_nt(doh, vals) - delta)
                    ds_ref[p, h] += jnp.sum(ds, axis=0)
                    dsb = ds.astype(BF16)
                    dq = dq + jnp.where(masks[h], _bdot(dsb, keys), 0.0)
                    dkeys = dkeys + _bdot_tn(dsb, qh)
                    dvals = dvals + _bdot_tn(pr.astype(BF16), doh)
                dkp, dkc, dvp, dvc = dkeys[:, :BLK], dkeys[:, BLK:], dvals[:, :BLK], dvals[:, BLK:]
                none = jnp.zeros((1, BLK, BLK), F32)
                dkc = dkc + jnp.concatenate([dkp[1:], none], axis=0)
                dvc = dvc + jnp.concatenate([dvp[1:], none], axis=0)
                for j in range(GRP):
                    if p == 1:
                        dq_ref[rc[j], :] += dq[j] * Q_SCALE
                        dk_ref[rc[j], :] += dkc[j]
                        dv_ref[rc[j], :] += dvc[j]
                    else:
                        dq_ref[rc[j], :] = dq[j] * Q_SCALE
                        dk_ref[rc[j], :] = dkc[j]
                        dv_ref[rc[j], :] = dvc[j]
                if before is not None:
                    dk_ref[before, :] += dkp[0]
                    dv_ref[before, :] += dvp[0]
                return carry

            lax.fori_loop(0, nblk // GRP, step, 0)

        def narrow(i, carry):
            rows = pl.ds(pl.multiple_of(i * 512, 512), 512)
            for a in range(3):
                dqkv_out[a, rows, :] = (dqkv_ref[a, rows, :] + last_ref[a, rows, :]).astype(BF16)
            return carry

        lax.fori_loop(0, S // 512, narrow, 0)
        car.phase(1, pl.program_id(0), jrefs)
        car.phase(2, pl.program_id(0), jrefs)

    blk = pl.BlockSpec((None, S, BLK), lambda g: (g, 0, 0))
    outs = pl.pallas_call(
        body, grid=(4,), name="attn_bwd",
        in_specs=[pl.BlockSpec((None, 3, None, S, BLK), lambda g: (0, 0, g, 0, 0)),
                  pl.BlockSpec((3, 2, BLK, KEYS), lambda g: (0, g, 0, 0)), blk, blk, blk] + car.in_specs(),
        out_specs=[pl.BlockSpec((None, 3, None, S, BLK), lambda g: (0, 0, g, 0, 0)),
                   pl.BlockSpec((3, 2, BLK, KEYS), lambda g: (0, g, 0, 0))] + car.out_specs(),
        out_shape=[SDS((2, 3, 4, S, BLK), BF16), SDS((3, N_HEADS, BLK, KEYS), F32)] + car.out_shapes(),
        input_output_aliases=car.aliases(),
        scratch_shapes=[pltpu.VMEM((3, S, BLK), F32)] * 3 + [pltpu.VMEM((S, BLK), F32)] + car.sems(),
        compiler_params=_cp("arbitrary"))(qkv, bias, o, lse, do, *car.args())
    if job is not None:
        job.landed = list(outs[2:])
    return outs[0], outs[1]


def _shift_down(u, k, halo):
    n = u.shape[0]
    row = lax.broadcasted_iota(jnp.int32, u.shape, 0)
    out = pltpu.roll(u, k, 0)
    hn = halo.shape[0]
    for j in range(k):
        out = jnp.where(row == j, halo[hn - k + j:hn - k + j + 1, :], out)
    return out


def _shift_up(u, k, halo):
    n = u.shape[0]
    row = lax.broadcasted_iota(jnp.int32, u.shape, 0)
    out = pltpu.roll(u, n - k, 0)
    for j in range(k):
        out = jnp.where(row == n - k + j, halo[j:j + 1, :], out)
    return out


def _conv3(u, halo, w0, w1, w2):
    return _shift_down(u, 2, halo) * w0 + _shift_down(u, 1, halo) * w1 + u * w2


def _colsum(x):
    return jnp.sum(x, axis=0, keepdims=True)


def _mixer_specs(S, tm):
    conv = pl.BlockSpec((None, 12, tm, BLK), lambda i: (1, 0, i, 0))
    halo = pl.BlockSpec((None, 12, 16, BLK), lambda i: (1, 0, jnp.maximum(i * (tm // 16) - 1, 0), 0))
    ob = pl.BlockSpec((4, tm, BLK), lambda i: (0, i, 0))
    return conv, halo, ob


def _mixer_recompute(i, o_ref, pr_ref, ph_ref, w_ref):
    ob = [o_ref[q] for q in range(4)]
    gb = [pr_ref[q].astype(F32) for q in range(4)]
    gc = [pr_ref[4 + q].astype(F32) for q in range(4)]
    xi = [pr_ref[8 + q].astype(F32) for q in range(4)]
    keep = jnp.where(i > 0, 1.0, 0.0)
    u = [gc[q] * xi[q] for q in range(4)]
    hu = [ph_ref[4 + q].astype(F32) * ph_ref[8 + q].astype(F32) * keep for q in range(4)]
    w = [[w_ref[k:k + 1, q * BLK:(q + 1) * BLK] for k in range(3)] for q in range(4)]
    cv = [_conv3(u[q], hu[q], *w[q]) for q in range(4)]
    return ob, gb, gc, xi, u, hu, cv, w


def _rms_blocks(blocks):
    ss = sum(_rsum(b * b) for b in blocks)
    return lax.rsqrt(ss / (BLK * len(blocks)) + EPS)


def mixer_post_fwd(o, proj, w_sc, g_a, g_c):
    S = o.shape[1]
    tm = 512

    def body(o_ref, pr_ref, ph_ref, w_ref, ga_ref, gc_ref, m_ref):
        i = pl.program_id(0)
        ob, gb, _, _, _, _, cv, _ = _mixer_recompute(i, o_ref, pr_ref, ph_ref, w_ref)
        conv = [gb[q] * cv[q] for q in range(4)]
        ra, rc = _rms_blocks(ob), _rms_blocks(conv)
        for q in range(4):
            sl = slice(q * BLK, (q + 1) * BLK)
            m_ref[:, q * BLK:(q + 1) * BLK] = (ob[q] * ra * ga_ref[:, sl]).astype(BF16)
            m_ref[:, ATTN_W + q * BLK:ATTN_W + (q + 1) * BLK] = (conv[q] * rc * gc_ref[:, sl]).astype(BF16)

    conv_s, halo_s, ob_s = _mixer_specs(S, tm)
    full = lambda r, c: pl.BlockSpec((r, c), lambda i: (0, 0))
    return pl.pallas_call(
        body, grid=(S // tm,), name="mixer_post_fwd",
        in_specs=[ob_s, conv_s, halo_s, full(3, 512), full(1, 512), full(1, 512)],
        out_specs=pl.BlockSpec((tm, D_MODEL), lambda i: (i, 0)), out_shape=SDS((S, D_MODEL), BF16),
        compiler_params=_cp("parallel"))(o, proj, proj, w_sc, g_a, g_c)


def mixer_post_bwd_a(dmixed, o, proj, w_sc, g_a, g_c, job=None):
    S = o.shape[1]
    tm = 512

    def body(dm_ref, o_ref, pr_ref, ph_ref, w_ref, ga_ref, gc_ref, do_ref, dgb_ref, dcv_ref, dga_ref, dgc_ref):
        i = pl.program_id(0)
        ob, gb, _, _, _, _, cv, _ = _mixer_recompute(i, o_ref, pr_ref, ph_ref, w_ref)
        conv = [gb[q] * cv[q] for q in range(4)]
        ra, rc = _rms_blocks(ob), _rms_blocks(conv)

        @pl.when(i == 0)
        def _():
            dga_ref[...] = jnp.zeros_like(dga_ref)
            dgc_ref[...] = jnp.zeros_like(dgc_ref)

        for blocks, r, g_ref, off, dg_ref, is_attn in ((ob, ra, ga_ref, 0, dga_ref, True), (conv, rc, gc_ref, ATTN_W, dgc_ref, False)):
            xh = [blocks[q] * r for q in range(4)]
            dm = [dm_ref[:, off + q * BLK:off + (q + 1) * BLK].astype(F32) for q in range(4)]
            gd = [dm[q] * g_ref[:, q * BLK:(q + 1) * BLK] for q in range(4)]
            mean = sum(_rsum(gd[q] * xh[q]) for q in range(4)) / (4 * BLK)
            for q in range(4):
                dg_ref[:, q * BLK:(q + 1) * BLK] += _colsum(dm[q] * xh[q])
                dx = r * (gd[q] - xh[q] * mean)
                if is_attn:
                    do_ref[q] = dx
                else:
                    dgb_ref[q] = dx * cv[q]
                    dcv_ref[q] = dx * gb[q]

    conv_s, halo_s, ob_s = _mixer_specs(S, tm)
    full = lambda r, c: pl.BlockSpec((r, c), lambda i: (0, 0))
    return _pcall(
        body, [dmixed, o, proj, proj, w_sc, g_a, g_c], grid=(S // tm,), name="mixer_post_bwd_a",
        in_specs=[pl.BlockSpec((tm, D_MODEL), lambda i: (i, 0)), ob_s, conv_s, halo_s, full(3, 512), full(1, 512), full(1, 512)],
        out_specs=[ob_s, ob_s, ob_s, full(1, 512), full(1, 512)],
        out_shape=[SDS((4, S, BLK), F32)] * 3 + [SDS((1, 512), F32)] * 2, sem=("arbitrary",), job=job)


def mixer_post_bwd_b(dproj, dgb, dcv, proj, w_sc):
    S = proj.shape[2]
    tm = 512
    last = S // 8 - 1

    def body(dp_in, dgb_ref, dcv_ref, dn_ref, pr_ref, w_ref, dp_ref, dw_ref):
        i = pl.program_id(0)
        keep_next = jnp.where(i < pl.num_programs(0) - 1, 1.0, 0.0)

        @pl.when(i == 0)
        def _():
            dw_ref[...] = jnp.zeros_like(dw_ref)

        for q in range(4):
            sl = slice(q * BLK, (q + 1) * BLK)
            gc, xi = pr_ref[4 + q].astype(F32), pr_ref[8 + q].astype(F32)
            u = gc * xi
            dcv = dcv_ref[q]
            dn = dn_ref[q] * keep_next
            d1, d2 = _shift_up(dcv, 1, dn), _shift_up(dcv, 2, dn)
            w0, w1, w2 = (w_ref[k:k + 1, sl] for k in range(3))
            du = dcv * w2 + d1 * w1 + d2 * w0
            dw_ref[0:1, sl] += _colsum(d2 * u)
            dw_ref[1:2, sl] += _colsum(d1 * u)
            dw_ref[2:3, sl] += _colsum(dcv * u)
            dp_ref[q] = dgb_ref[q].astype(BF16)
            dp_ref[4 + q] = (du * xi).astype(BF16)
            dp_ref[8 + q] = (du * gc).astype(BF16)

    conv_s, _, ob_s = _mixer_specs(S, tm)
    nxt = pl.BlockSpec((4, 8, BLK), lambda i: (0, jnp.minimum((i + 1) * (tm // 8), last), 0))
    full = lambda r, c: pl.BlockSpec((r, c), lambda i: (0, 0))
    return pl.pallas_call(
        body, grid=(S // tm,), name="mixer_post_bwd_b",
        in_specs=[pl.BlockSpec(memory_space=pl.ANY), ob_s, ob_s, nxt, conv_s, full(3, 512)],
        out_specs=[conv_s, full(3, 512)],
        out_shape=[SDS(dproj.shape, BF16), SDS((3, 512), F32)],
        input_output_aliases={0: 0}, compiler_params=_cp("arbitrary"))(dproj, dgb, dcv, dcv, proj, w_sc)


def xattn_fwd(q, k, v):
    S = q.shape[0]
    tm = 512
    scale = MEM_HD ** -0.5

    def body(q_ref, k_ref, v_ref, o_ref):
        for h in range(N_MEM_HEADS):
            sl = slice(h * MEM_HD, (h + 1) * MEM_HD)
            s = _dot_nt(q_ref[:, sl], k_ref[:, sl]) * scale
            p = jnp.exp(s - jnp.max(s, axis=1, keepdims=True))
            p = p / _rsum(p)
            o_ref[:, sl] = _dot(p.astype(BF16), v_ref[:, sl]).astype(BF16)

    row = pl.BlockSpec((tm, D_MODEL), lambda i: (i, 0))
    kv = pl.BlockSpec(k.shape, lambda i: (0, 0))
    return pl.pallas_call(body, grid=(S // tm,), name="xattn_fwd", in_specs=[row, kv, kv], out_specs=row,
                          out_shape=SDS((S, D_MODEL), BF16), compiler_params=_cp("parallel"))(q, k, v)


def xattn_bwd(q, k, v, do):
    S = q.shape[0]
    tm = 512
    scale = MEM_HD ** -0.5

    def body(q_ref, k_ref, v_ref, do_ref, dq_ref, dk_ref, dv_ref):
        @pl.when(pl.program_id(0) == 0)
        def _():
            dk_ref[...] = jnp.zeros_like(dk_ref)
            dv_ref[...] = jnp.zeros_like(dv_ref)

        for h in range(N_MEM_HEADS):
            sl = slice(h * MEM_HD, (h + 1) * MEM_HD)
            qh, kh, vh, doh = q_ref[:, sl], k_ref[:, sl], v_ref[:, sl], do_ref[:, sl]
            s = _dot_nt(qh, kh) * scale
            p = jnp.exp(s - jnp.max(s, axis=1, keepdims=True))
            p = p / _rsum(p)
            dp = _dot_nt(doh, vh)
            ds = (p * (dp - _rsum(p * dp)) * scale).astype(BF16)
            dq_ref[:, sl] = _dot(ds, kh).astype(BF16)
            dk_ref[:, sl] += _dot_tn(ds, qh)
            dv_ref[:, sl] += _dot_tn(p.astype(BF16), doh)

    row = pl.BlockSpec((tm, D_MODEL), lambda i: (i, 0))
    kv = pl.BlockSpec(k.shape, lambda i: (0, 0))
    return pl.pallas_call(body, grid=(S // tm,), name="xattn_bwd", in_specs=[row, kv, kv, row], out_specs=[row, kv, kv],
                          out_shape=[SDS((S, D_MODEL), BF16), SDS(k.shape, F32), SDS(k.shape, F32)],
                          compiler_params=_cp("arbitrary"))(q, k, v, do)


FFN_TM, FFN_TC = 512, 1408


def _ffn_specs(S):
    tm, tc = FFN_TM, FFN_TC
    blk = pl.BlockSpec((2, tm, tc), lambda j, i: (0, i, j))
    nxt = pl.BlockSpec((2, 16, tc), lambda j, i: (0, jnp.minimum((i + 1) * (tm // 16), S // 16 - 1), j))
    wsp = pl.BlockSpec((2, 3, tc), lambda j, i: (0, 0, j))
    bsp = pl.BlockSpec((2, 1, tc), lambda j, i: (0, 0, j))
    return blk, nxt, wsp, bsp


def up_act_fwd(h, w_up, w_fc, b_fc, job=None):
    S, K = h.shape
    tm, tc = FFN_TM, FFN_TC
    nj = D_FF // tc

    def body(h_ref, wg_ref, wv_ref, w_ref, b_ref, pre_ref, gv_ref, a_ref, halo):
        i = pl.program_id(1)
        hv = h_ref[...]
        res = []
        for half, wt_ref in enumerate((wg_ref, wv_ref)):
            u = _dot(hv, wt_ref[...])
            hu = jnp.where(i > 0, halo[half], 0.0)
            halo[half] = u[tm - 8:, :]
            w0, w1, w2 = (w_ref[half, k:k + 1, :] for k in range(3))
            pre_ref[half] = u.astype(BF16)
            res.append(_conv3(u, hu, w0, w1, w2) + b_ref[half])
        g, v = res
        gv_ref[0] = g.astype(BF16)
        gv_ref[1] = v.astype(BF16)
        a_ref[...] = (g * jax.nn.sigmoid(g) * v).astype(BF16)

    blk = pl.BlockSpec((2, tm, tc), lambda j, i: (0, i, j))
    return _pcall(
        body, [h, w_up, w_up, w_fc, b_fc], grid=(nj, S // tm), name="up_act_fwd",
        in_specs=[pl.BlockSpec((tm, K), lambda j, i: (i, 0)), pl.BlockSpec((K, tc), lambda j, i: (0, j)),
                  pl.BlockSpec((K, tc), lambda j, i: (0, j + nj)), pl.BlockSpec((2, 3, tc), lambda j, i: (0, 0, j)),
                  pl.BlockSpec((2, 1, tc), lambda j, i: (0, 0, j))],
        out_specs=[blk, blk, pl.BlockSpec((tm, tc), lambda j, i: (i, j))],
        out_shape=[SDS((2, S, D_FF), BF16), SDS((2, S, D_FF), BF16), SDS((S, D_FF), BF16)],
        scratch=[pltpu.VMEM((2, 8, tc), F32)], sem=("parallel", "arbitrary"), job=job)


def ffn_act_bwd(dx, w_down, up, up_pre, w_fc, job=None):
    S = up.shape[1]

    def act_grads(da, g, v):
        sg = jax.nn.sigmoid(g)
        return da * v * (sg * (1.0 + g * (1.0 - sg))), da * g * sg

    def body(dx_ref, dxn_ref, wd_ref, gv_ref, gvn_ref, up_ref, w_ref, dp_ref, dw_ref, db_ref):
        i = pl.program_id(1)
        keep_next = jnp.where(i < pl.num_programs(1) - 1, 1.0, 0.0)

        @pl.when(i == 0)
        def _():
            dw_ref[...] = jnp.zeros_like(dw_ref)
            db_ref[...] = jnp.zeros_like(db_ref)

        wd = wd_ref[...]
        da = _dot_nt(dx_ref[...].astype(BF16), wd)
        dan = _dot_nt(dxn_ref[...].astype(BF16), wd) * keep_next
        here = act_grads(da, gv_ref[0].astype(F32), gv_ref[1].astype(F32))
        after = act_grads(dan, gvn_ref[0].astype(F32), gvn_ref[1].astype(F32))
        for half in range(2):
            d, dn = here[half], after[half]
            u = up_ref[half].astype(F32)
            d1, d2 = _shift_up(d, 1, dn), _shift_up(d, 2, dn)
            w0, w1, w2 = (w_ref[half, k:k + 1, :] for k in range(3))
            dp_ref[half] = (d * w2 + d1 * w1 + d2 * w0).astype(BF16)
            dw_ref[half, 0:1, :] += _colsum(d2 * u)
            dw_ref[half, 1:2, :] += _colsum(d1 * u)
            dw_ref[half, 2:3, :] += _colsum(d * u)
            db_ref[half] += _colsum(d)

    tm, tc = FFN_TM, FFN_TC
    blk, nxt, wsp, bsp = _ffn_specs(S)
    K = dx.shape[1]
    rows = pl.BlockSpec((tm, K), lambda j, i: (i, 0))
    rows_next = pl.BlockSpec((16, K), lambda j, i: (jnp.minimum((i + 1) * (tm // 16), S // 16 - 1), 0))
    return _pcall(body, [dx, dx, w_down, up, up, up_pre, w_fc], grid=(D_FF // tc, S // tm), name="ffn_act_bwd",
                  in_specs=[rows, rows_next, pl.BlockSpec((tc, K), lambda j, i: (j, 0)), blk, nxt, blk, wsp],
                  out_specs=[blk, wsp, bsp],
                  out_shape=[SDS((2, S, D_FF), BF16), SDS((2, 3, D_FF), F32), SDS((2, 1, D_FF), F32)],
                  sem=("parallel", "arbitrary"), job=job)


def local_step(x, mem, target, wb, ws, comm=None):
    S = x.shape[0]
    assert S % 2048 == 0
    if comm is None:
        comm = NoComm()
    else:
        wb = {}
    ws = dict(ws)

    bias = bias_tables(ws["rel_bias"])
    h1 = rmsnorm_fwd(x, ws["g_mix"], "norm_mix", job=comm.job("norm_mix"))
    comm.landed("norm_mix", wb, ws)
    w_fc = ws["w_ffn_conv"].reshape(3, 2, D_FF).transpose(1, 0, 2)
    b_fc = ws["b_ffn_conv"].reshape(2, 1, D_FF)
    proj = mm_nn(h1, wb["w_in"], "proj", out_dtype=BF16, out_bw=BLK, tn=768, job=comm.job("proj"))
    comm.landed("proj", wb)
    qkv = proj.reshape(2, 3, 4, S, BLK)
    o, lse = attn_fwd(qkv, bias, job=comm.job("attn_fwd"))
    comm.landed("attn_fwd", wb)
    proj4 = proj.reshape(2, 12, S, BLK)
    mixed = mixer_post_fwd(o, proj4, ws["w_short_conv"], ws["g_attn_out"], ws["g_conv_out"])
    x1, h2 = mm_nn_rows(mixed, wb["w_out"], x, ws["g_xattn"], "out_proj")
    mem_n = rmsnorm_fwd(mem, ws["g_mem"], "norm_mem")
    xq = mm_nn(h2, wb["w_xq"], "xq", out_dtype=BF16)
    xk = mm_nn(mem_n, wb["w_xk"], "xk", out_dtype=BF16, tn=1024)
    xv = mm_nn(mem_n, wb["w_xv"], "xv", out_dtype=BF16, tn=1024)
    xo = xattn_fwd(xq, xk, xv)
    x2, h3 = mm_nn_rows(xo, wb["w_xo"], x1, ws["g_ffn"], "xo_proj")
    up_pre, up, act = up_act_fwd(h3, wb["w_up"], w_fc, b_fc, job=comm.job("up_act_fwd"))
    comm.landed("up_act_fwd", wb)
    loss, dx3, dg_final = mm_nn_rows(act, wb["w_down"], x2, ws["g_final"].reshape(1, -1), "down_proj", target=target)

    gb, gs = {}, {"g_final": dg_final}
    gb["w_down"] = mm_tn(act, dx3, None, "dw_down", tm=1408, tn=1024)
    d_up_pre, dw_fc, db_fc = ffn_act_bwd(dx3, wb["w_down"], up, up_pre, w_fc, job=comm.job("ffn_act_bwd", gb))
    comm.landed("ffn_act_bwd")
    gs["b_ffn_conv"] = db_fc.reshape(1, 2 * D_FF)
    gs["w_ffn_conv"] = dw_fc.transpose(1, 0, 2).reshape(3, 2 * D_FF)
    gb["w_up"] = mm_tn(h3, d_up_pre, D_FF, "dw_up", shards=4, tn=1408, job=comm.job("dw_up"))
    comm.landed("dw_up")
    dx2, gs["g_ffn"] = mm_nt_norm_bwd(d_up_pre, D_FF, wb["w_up"], x2, ws["g_ffn"], dx3, "d_h3", tm=256,
                                      job=comm.job("d_h3", gb))
    comm.landed("d_h3")
    gb["w_xo"] = mm_tn(xo, dx2, None, "dw_xo")
    dxo = mm_nt(dx2, None, wb["w_xo"], "d_xo", out_dtype=BF16)
    dxq, dxk, dxv = xattn_bwd(xq, xk, xv, dxo)
    gb["w_xq"] = mm_tn(h2, dxq, None, "dw_xq")
    gb["w_xk"] = mm_tn(mem_n, dxk, None, "dw_xk", tn=1024)
    gb["w_xv"] = mm_tn(mem_n, dxv, None, "dw_xv", tn=1024)
    dmem_n = mm_nt(dxk, None, wb["w_xk"], "d_memk", tn=1024)
    dmem_n = mm_nt(dxv, None, wb["w_xv"], "d_memv", res=dmem_n, tn=1024)
    gs["g_mem"] = rmsnorm_gain_grad(mem, dmem_n, "norm_mem_bwd")
    dx1, gs["g_xattn"] = mm_nt_norm_bwd(dxq, None, wb["w_xq"], x1, ws["g_xattn"], dx2, "d_h2")
    gb["w_out"] = mm_tn(mixed, dx1, None, "dw_out")
    dmixed = mm_nt(dx1, None, wb["w_out"], "d_mixed", out_dtype=BF16)
    do, dgb, dcv, gs["g_attn_out"], gs["g_conv_out"] = mixer_post_bwd_a(
        dmixed, o, proj4, ws["w_short_conv"], ws["g_attn_out"], ws["g_conv_out"], job=comm.job("mixer_post_bwd_a", gb))
    comm.landed("mixer_post_bwd_a")
    dproj, ds_sum = attn_bwd(qkv, bias, o, lse, do, job=comm.job("attn_bwd"))
    comm.landed("attn_bwd")
    dproj, gs["w_short_conv"] = mixer_post_bwd_b(dproj.reshape(2, 12, S, BLK), dgb, dcv, proj4, ws["w_short_conv"])
    dproj = dproj.reshape(24, S, BLK)
    gb["w_in"] = mm_tn(h1, dproj, BLK, "dw_in", shards=4, tn=768, job=comm.job("dw_in"))
    comm.landed("dw_in")
    gs["rel_bias"] = bias_tables_bwd(ds_sum, job=comm.job("bias_tables_bwd", gb))
    comm.landed("bias_tables_bwd")
    grad_x, gs["g_mix"] = mm_nt_norm_bwd(dproj, BLK, wb["w_in"], x, ws["g_mix"], dx1, "d_h1", tm=256,
                                         job=comm.job("d_h1", gb))
    comm.landed("d_h1")
    return loss, grad_x, comm.finish(gb), gs


def _adam_update(w_ref, g_ref, m_ref, v_ref, d_ref, nm_ref, nv_ref):
    gv = g_ref[...]
    mn = ADAM_B1 * m_ref[...] + (1.0 - ADAM_B1) * gv
    vn = ADAM_B2 * v_ref[...] + (1.0 - ADAM_B2) * (gv * gv)
    m_hat = mn / (1.0 - ADAM_B1 ** ADAM_STEP)
    v_hat = vn / (1.0 - ADAM_B2 ** ADAM_STEP)
    d_ref[...] = -ADAM_LR * (m_hat / (jnp.sqrt(v_hat) + ADAM_EPS) + ADAM_WD * w_ref[...])
    nm_ref[...] = mn
    nv_ref[...] = vn


def adamw(ws, gs, ms, vs, name, job=None):
    n = len(ws)
    R, C = ws[0].shape
    tr = R // 4

    def body(*refs):
        for p in range(n):
            _adam_update(*(refs[q * n + p] for q in range(7)))

    blk = pl.BlockSpec((tr, C), lambda i: (i, 0))
    outs = _pcall(body, [*ws, *gs, *ms, *vs], grid=(4,), name=name, in_specs=[blk] * (4 * n), out_specs=[blk] * (3 * n),
                  out_shape=[SDS((R, C), F32)] * (3 * n), sem=("parallel",), job=job)
    return outs[:n], outs[n:2 * n], outs[2 * n:]


BIG = ("w_in", "w_out", "w_xq", "w_xk", "w_xv", "w_xo", "w_up", "w_down")
COL_SHARDED = ("w_in", "w_up")
ANY = pl.BlockSpec(memory_space=pl.ANY)


def _place():
    x, y, c = lax.axis_index("x"), lax.axis_index("y"), lax.axis_index("c")
    chips = [(1 - x, y), (x, 1 - y), (1 - x, 1 - y)]
    return x, y, c, chips


def _window(full, name, R, C, shard, half):
    r0, nr = (0, R) if half is None else (half * (R // 2), R // 2)
    if name in COL_SHARDED:
        return full.at[pl.ds(r0, nr), pl.ds(shard * C, C)]
    return full.at[pl.ds(shard * R + r0, nr), :]


def same_shape_groups(names, arrays):
    groups = {}
    for n in names:
        groups.setdefault(tuple(arrays[n].shape), []).append(n)
    return list(groups.values())


def place_shards(ws, mine_arr, col, name):
    n = len(ws)
    R, C = ws[0].shape

    def body(s_ref, *refs):
        for p in range(n):
            refs[n + p][...] = refs[p][...].astype(BF16)

    grid_spec = pltpu.PrefetchScalarGridSpec(
        num_scalar_prefetch=1, grid=(1,), in_specs=[pl.BlockSpec((R, C), lambda i, s: (0, 0))] * n,
        out_specs=[pl.BlockSpec((R, C), (lambda i, s: (0, s[0])) if col else (lambda i, s: (s[0], 0)))] * n)
    return pl.pallas_call(body, grid_spec=grid_spec, name=name,
                          out_shape=[SDS((R, 4 * C) if col else (4 * R, C), BF16)] * n,
                          compiler_params=_cp("arbitrary"))(mine_arr, *ws)


def _gather_jobs(names, shapes):
    nw = len(names)

    def start(full, sems):
        send, recv, fsend, frecv = sems
        x, y, c, chips = _place()
        mine = 2 * x + y
        for w, n in enumerate(names):
            R, C = shapes[w]
            own = _window(full[w], n, R, C, mine, c)
            for j, chip in enumerate(chips):
                pltpu.make_async_remote_copy(src_ref=own, dst_ref=own, send_sem=send.at[w, j], recv_sem=recv.at[w, j],
                                             device_id=(*chip, c), device_id_type=MESH).start()

    def mid(full, sems, lo=0, hi=nw):
        send, recv, fsend, frecv = sems
        x, y, c, chips = _place()
        sib = (x, y, 1 - c)
        for w, n in list(enumerate(names))[lo:hi]:
            R, C = shapes[w]
            for j, chip in enumerate(chips):
                landed = _window(full[w], n, R, C, 2 * chip[0] + chip[1], c)
                pltpu.make_async_remote_copy(src_ref=landed, dst_ref=landed, send_sem=send.at[w, j], recv_sem=recv.at[w, j],
                                             device_id=(*chip, c), device_id_type=MESH).wait_recv()
                pltpu.make_async_remote_copy(src_ref=landed, dst_ref=landed, send_sem=fsend.at[w, j],
                                             recv_sem=frecv.at[w, j], device_id=sib, device_id_type=MESH).start()

    def finish(full, sems):
        send, recv, fsend, frecv = sems
        x, y, c, chips = _place()
        mine = 2 * x + y
        sib = (x, y, 1 - c)
        for w, n in enumerate(names):
            R, C = shapes[w]
            own = _window(full[w], n, R, C, mine, c)
            for j, chip in enumerate(chips):
                landed = _window(full[w], n, R, C, 2 * chip[0] + chip[1], c)
                other = _window(full[w], n, R, C, 2 * chip[0] + chip[1], 1 - c)
                pltpu.make_async_remote_copy(src_ref=other, dst_ref=other, send_sem=fsend.at[w, j], recv_sem=frecv.at[w, j],
                                             device_id=sib, device_id_type=MESH).wait_recv()
                pltpu.make_async_remote_copy(src_ref=own, dst_ref=own, send_sem=send.at[w, j], recv_sem=recv.at[w, j],
                                             device_id=(*chip, c), device_id_type=MESH).wait_send()
                pltpu.make_async_remote_copy(src_ref=landed, dst_ref=landed, send_sem=fsend.at[w, j],
                                             recv_sem=frecv.at[w, j], device_id=sib, device_id_type=MESH).wait_send()

    return start, mid, finish, [pltpu.SemaphoreType.DMA((nw, 3))] * 4


class CommJob:
    def __init__(self, ins, out_shapes, inplace, start, finish, sems, mid=None):
        self.ins, self.out_shapes, self.inplace = list(ins), list(out_shapes), inplace
        self.start, self.mid, self.finish, self.sems = start, mid, finish, list(sems)

    def run(self, phase, ins, outs, sems):
        if phase == 0:
            self.start(ins, outs, sems)
        elif phase == 1:
            if self.mid is not None:
                self.mid(ins, outs, sems)
        else:
            self.finish(ins, outs, sems)


class Carried:
    def __init__(self, job, n_in, n_out, n_scratch, steps):
        self.job, self.n_in, self.n_out, self.n_scratch, self.steps = job, n_in, n_out, n_scratch, steps
        self.nji = len(job.ins) if job else 0
        self.njo = len(job.out_shapes) if job else 0

    def in_specs(self):
        return [ANY] * self.nji

    def out_specs(self):
        return [ANY] * self.njo

    def out_shapes(self):
        return list(self.job.out_shapes) if self.job else []

    def aliases(self):
        if not (self.job and self.job.inplace):
            return {}
        return {self.n_in + i: self.n_out + i for i in range(self.nji)}

    def sems(self):
        return list(self.job.sems) if self.job else []

    def args(self):
        return list(self.job.ins) if self.job else []

    def split(self, refs):
        a = self.n_in
        b = a + self.nji
        c = b + self.n_out
        d = c + self.njo
        e = d + self.n_scratch
        return refs[:a] + refs[b:c] + refs[d:e], (refs[a:b], refs[c:d], refs[e:])

    def phase(self, phase, step, jrefs):
        if self.job is None:
            return
        at = {0: 0, 1: max(self.steps - 2, 0), 2: self.steps - 1}[phase]

        @pl.when(step == at)
        def _():
            self.job.run(phase, *jrefs)


def _pcall(body, args, *, grid, in_specs, out_specs, out_shape, name, sem, scratch=(), aliases=None, job=None):
    n_in, n_out = len(args), len(out_shape)
    if job is None:
        return pl.pallas_call(
            body, grid=grid, in_specs=list(in_specs), out_specs=list(out_specs), out_shape=list(out_shape), name=name,
            scratch_shapes=list(scratch), input_output_aliases=dict(aliases or {}), compiler_params=_cp(*sem))(*args)
    total = int(np.prod(grid))
    car = Carried(job, n_in, n_out, len(scratch), total)

    def wrapped(*refs):
        main, jrefs = car.split(refs)
        lin = pl.program_id(0)
        for ax in range(1, len(grid)):
            lin = lin * grid[ax] + pl.program_id(ax)
        car.phase(0, lin, jrefs)
        body(*main)
        car.phase(1, lin, jrefs)
        car.phase(2, lin, jrefs)

    res = pl.pallas_call(
        wrapped, grid=grid, in_specs=list(in_specs) + car.in_specs(), out_specs=list(out_specs) + car.out_specs(),
        out_shape=list(out_shape) + car.out_shapes(), name=name, scratch_shapes=list(scratch) + car.sems(),
        input_output_aliases={**dict(aliases or {}), **car.aliases()},
        compiler_params=_cp(*(["arbitrary"] * len(grid))))(*args, *car.args())
    job.landed = list(res[n_out:])
    return list(res[:n_out])


def gather_job(placed, names):
    shapes = []
    for n in names:
        R, C = placed[n].shape
        shapes.append((R, C // 4) if n in COL_SHARDED else (R // 4, C))
    start, mid, finish, sems = _gather_jobs(names, shapes)
    arrays = [placed[n] for n in names]
    early = max(len(names) - 1, 1)

    def last(i, o, s):
        mid(o, s, early, len(names))
        finish(o, s)

    return CommJob(arrays, [SDS(a.shape, a.dtype) for a in arrays], True,
                   lambda i, o, s: start(o, s), last, sems, mid=lambda i, o, s: mid(o, s, 0, early))


def gather_rows_job(placed):
    widths = [a.shape[1] // 4 for a in placed]

    def copies(outs, sems):
        x, y, c, chips = _place()
        mine = 2 * x + y
        cps = []
        for w, ref in enumerate(outs):
            own = ref.at[:, pl.ds(mine * widths[w], widths[w])]
            for j, chip in enumerate(chips):
                theirs = ref.at[:, pl.ds((2 * chip[0] + chip[1]) * widths[w], widths[w])]
                kw = dict(send_sem=sems[0].at[w, j], recv_sem=sems[1].at[w, j], device_id=(*chip, c), device_id_type=MESH)
                cps.append((pltpu.make_async_remote_copy(src_ref=own, dst_ref=own, **kw),
                            pltpu.make_async_remote_copy(src_ref=theirs, dst_ref=theirs, **kw)))
        return cps

    def start(ins, outs, sems):
        for send, _ in copies(outs, sems):
            send.start()

    def finish(ins, outs, sems):
        for send, recv in copies(outs, sems):
            recv.wait_recv()
            send.wait_send()

    return CommJob(placed, [SDS(a.shape, a.dtype) for a in placed], True, start, finish,
                   [pltpu.SemaphoreType.DMA((len(placed), 3))] * 2)


def merge_jobs(a, b):
    assert a.inplace and b.inplace
    na, nb_, nsa = len(a.ins), len(b.ins), len(a.sems)

    def phase(which):
        def run(ins, outs, sems):
            for job, i, o, s in ((a, ins[:na], outs[:na], sems[:nsa]), (b, ins[na:], outs[na:], sems[nsa:])):
                fn = getattr(job, which)
                if fn is not None:
                    fn(i, o, s)
        return run

    merged = CommJob(a.ins + b.ins, a.out_shapes + b.out_shapes, True, phase("start"), phase("finish"), a.sems + b.sems,
                     mid=phase("mid"))
    merged.parts = (a, b, na)
    return merged


def split_landed(merged):
    a, b, na = merged.parts
    a.landed, b.landed = merged.landed[:na], merged.landed[na:]


def pair_exchange_job(grads, names):
    shapes = [grads[n].shape for n in names]

    def copies(ins, outs, sems):
        x, y, c, _ = _place()
        return [pltpu.make_async_remote_copy(
            src_ref=ins[w].at[:, pl.ds((1 - c) * (shapes[w][1] // 2), shapes[w][1] // 2), :], dst_ref=outs[w],
            send_sem=sems[0].at[w], recv_sem=sems[1].at[w], device_id=(x, y, 1 - c), device_id_type=MESH)
            for w in range(len(names))]

    def start(ins, outs, sems):
        for cp in copies(ins, outs, sems):
            cp.start()

    def finish(ins, outs, sems):
        for cp in copies(ins, outs, sems):
            cp.wait()

    return CommJob([grads[n] for n in names], [SDS((4, s[1] // 2, s[2]), F32) for s in shapes], False, start, finish,
                   [pltpu.SemaphoreType.DMA((len(names),))] * 2)


def chip_exchange_job(p16, names):
    shapes = [p16[n].shape for n in names]

    def copies(ins, outs, sems):
        x, y, c, chips = _place()
        return [pltpu.make_async_remote_copy(
            src_ref=ins[w].at[2 * chip[0] + chip[1]], dst_ref=outs[w].at[j],
            send_sem=sems[0].at[w, j], recv_sem=sems[1].at[w, j], device_id=(*chip, c), device_id_type=MESH)
            for w in range(len(names)) for j, chip in enumerate(chips)]

    def start(ins, outs, sems):
        for cp in copies(ins, outs, sems):
            cp.start()

    def finish(ins, outs, sems):
        for cp in copies(ins, outs, sems):
            cp.wait()

    return CommJob([p16[n] for n in names], [SDS((3,) + tuple(s[1:]), BF16) for s in shapes], False, start, finish,
                   [pltpu.SemaphoreType.DMA((len(names), 3))] * 2)


def half_exchange_job(full, names):
    shapes = [full[n].shape for n in names]

    def copies(outs, sems):
        x, y, c, _ = _place()
        cps = []
        for w in range(len(names)):
            Rh = shapes[w][0] // 2
            rows = outs[w].at[pl.ds(c * Rh, Rh), :]
            other = outs[w].at[pl.ds((1 - c) * Rh, Rh), :]
            cps.append((pltpu.make_async_remote_copy(src_ref=rows, dst_ref=rows, send_sem=sems[0].at[w], recv_sem=sems[1].at[w],
                                                     device_id=(x, y, 1 - c), device_id_type=MESH),
                        pltpu.make_async_remote_copy(src_ref=other, dst_ref=other, send_sem=sems[0].at[w], recv_sem=sems[1].at[w],
                                                     device_id=(x, y, 1 - c), device_id_type=MESH)))
        return cps

    def start(ins, outs, sems):
        for send, _ in copies(outs, sems):
            send.start()

    def finish(ins, outs, sems):
        for send, recv in copies(outs, sems):
            recv.wait_recv()
            send.wait_send()

    arrays = [full[n] for n in names]
    return CommJob(arrays, [SDS(a.shape, a.dtype) for a in arrays], True, start, finish,
                   [pltpu.SemaphoreType.DMA((len(names),))] * 2)


def allreduce_small(parts, name, job=None):
    n = len(parts)
    nji = len(job.ins) if job else 0
    njo = len(job.out_shapes) if job else 0

    def body(*refs):
        ins, jins = refs[:n], refs[n:n + nji]
        outs, jouts = refs[n + nji:2 * n + nji], refs[2 * n + nji:2 * n + nji + njo]
        bufs = refs[2 * n + nji + njo:3 * n + nji + njo]
        send, recv = refs[3 * n + nji + njo:3 * n + nji + njo + 2]
        jsems = refs[3 * n + nji + njo + 2:]
        x, y, c, _ = _place()
        me = 4 * x + 2 * y + c
        if job is not None:
            job.run(0, jins, jouts, jsems)
        sends = []
        for p in range(n):
            for k in range(1, 8):
                peer = (x ^ (k >> 2), y ^ ((k >> 1) & 1), c ^ (k & 1))
                cp = pltpu.make_async_remote_copy(src_ref=ins[p], dst_ref=bufs[p].at[k], send_sem=send.at[p, k - 1],
                                                  recv_sem=recv.at[p, k - 1], device_id=peer, device_id_type=MESH)
                cp.start()
                sends.append(cp)
            bufs[p][0] = ins[p][...]
        for cp in sends:
            cp.wait_recv()
        for p in range(n):
            acc = bufs[p][me]
            for d in range(1, 8):
                acc = acc + bufs[p][d ^ me]
            outs[p][...] = acc
        for cp in sends:
            cp.wait_send()
        if job is not None:
            job.run(1, jins, jouts, jsems)
            job.run(2, jins, jouts, jsems)

    vm = pl.BlockSpec(memory_space=pltpu.VMEM)
    res = pl.pallas_call(
        body, name=name, in_specs=[vm] * n + [ANY] * nji, out_specs=[vm] * n + [ANY] * njo,
        out_shape=[SDS(a.shape, F32) for a in parts] + (list(job.out_shapes) if job else []),
        input_output_aliases={n + i: n + i for i in range(nji)} if (job and job.inplace) else {},
        scratch_shapes=[pltpu.VMEM((8,) + tuple(a.shape), F32) for a in parts] + [pltpu.SemaphoreType.DMA((n, 7))] * 2
        + (list(job.sems) if job else []),
    )(*parts, *(job.ins if job else []))
    if job is not None:
        job.landed = list(res[n:])
    return list(res[:n])


def adamw_multi(ws, gs, ms, vs, name):
    n = len(ws)

    def body(*refs):
        for p in range(n):
            _adam_update(*(refs[q * n + p] for q in range(7)))

    vm = pl.BlockSpec(memory_space=pltpu.VMEM)
    outs = pl.pallas_call(body, name=name, in_specs=[vm] * (4 * n), out_specs=[vm] * (3 * n),
                          out_shape=[SDS(a.shape, F32) for a in ws] * 3)(*ws, *gs, *ms, *vs)
    return outs[:n], outs[n:2 * n], outs[2 * n:]


def pair_sum(gs, xhs, sel_arr, name):
    n = len(gs)
    _, R, C = gs[0].shape
    Rh = R // 2

    def body(sel_ref, *refs):
        for p in range(n):
            g_ref, x_ref, p32_ref, p16_ref = (refs[q * n + p] for q in range(4))
            s = g_ref[...] + x_ref[...]
            p16_ref[...] = s.astype(BF16)

            @pl.when(pl.program_id(0) == sel_ref[0])
            def _():
                p32_ref[...] = s

    same = pl.BlockSpec((None, Rh, C), lambda s, sel: (s, 0, 0))
    grid_spec = pltpu.PrefetchScalarGridSpec(
        num_scalar_prefetch=1, grid=(4,),
        in_specs=[pl.BlockSpec((None, Rh, C), lambda s, sel: (s, sel[1], 0))] * n + [same] * n,
        out_specs=[pl.BlockSpec((Rh, C), lambda s, sel: (0, 0))] * n + [same] * n)
    outs = pl.pallas_call(body, grid_spec=grid_spec, name=name,
                          out_shape=[SDS((Rh, C), F32)] * n + [SDS((4, Rh, C), BF16)] * n,
                          compiler_params=_cp("arbitrary"))(sel_arr, *gs, *xhs)
    return outs[:n], outs[n:]


def chip_sum(p32s, ys, sel_arr, name):
    n = len(p32s)
    Rh, C = p32s[0].shape

    def body(s_ref, *refs):
        for p in range(n):
            p_ref, y_ref, o_ref = (refs[q * n + p] for q in range(3))
            o_ref[...] = ((p_ref[...] + y_ref[0].astype(F32)) + y_ref[1].astype(F32)) + y_ref[2].astype(F32)

    grid_spec = pltpu.PrefetchScalarGridSpec(
        num_scalar_prefetch=1, grid=(1,),
        in_specs=[pl.BlockSpec((Rh, C), lambda i, s: (0, 0))] * n + [pl.BlockSpec((3, Rh, C), lambda i, s: (0, 0, 0))] * n,
        out_specs=[pl.BlockSpec((Rh, C), lambda i, s: (s[1], 0))] * n)
    return pl.pallas_call(body, grid_spec=grid_spec, name=name, out_shape=[SDS((2 * Rh, C), F32)] * n,
                          compiler_params=_cp("arbitrary"))(sel_arr, *p32s, *ys)


class StepComm:
    FIRST = ("w_in",)
    SQUARE = ("w_out", "w_xq", "w_xk", "w_xv", "w_xo")
    FFN = ("w_up", "w_down")
    PLAN = {
        "proj": ("gather", SQUARE[:3]), "attn_fwd": ("gather", SQUARE[3:] + FFN[:1]), "up_act_fwd": ("gather", FFN[1:]),
        "ffn_act_bwd": ("pair", FFN[1:]), "dw_up": ("chip", FFN[1:]),
        "d_h3": ("pair", FFN[:1]), "mixer_post_bwd_a": ("pair", SQUARE), "attn_bwd": ("chip", FFN[:1] + SQUARE),
        "dw_in": ("half", FFN + SQUARE), "bias_tables_bwd": ("pair", FIRST), "d_h1": ("chip", FIRST),
    }

    def __init__(self, placed, conv_placed, sel_arr):
        self.placed, self.conv_placed, self.sel_arr = placed, conv_placed, sel_arr
        self.g, self.p32, self.p16, self.full, self.done, self.jobs = {}, {}, {}, {}, {}, {}

    def job(self, tag, gb=None):
        if tag == "norm_mix":
            self.jobs[tag] = merge_jobs(gather_job(self.placed, self.FIRST), gather_rows_job(self.conv_placed))
            return self.jobs[tag]
        kind, names = self.PLAN[tag]
        if kind == "gather":
            j = gather_job(self.placed, names)
        elif kind == "pair":
            for n in names:
                self.g[n] = gb[n] if gb[n].ndim == 3 else gb[n].reshape(4, gb[n].shape[0] // 4, gb[n].shape[1])
            j = pair_exchange_job(self.g, names)
        elif kind == "chip":
            j = chip_exchange_job(self.p16, names)
        else:
            j = half_exchange_job(self.full, names)
        self.jobs[tag] = j
        return j

    def landed(self, tag, wb=None, ws=None):
        if tag == "norm_mix":
            split_landed(self.jobs[tag])
            first, conv, _ = self.jobs[tag].parts
            wb.update(zip(self.FIRST, first.landed))
            ws.update((n, a[:3]) for n, a in zip(SMALL_SHARDED, conv.landed))
            return
        kind, names = self.PLAN[tag]
        got = dict(zip(names, self.jobs[tag].landed))
        if kind == "gather":
            wb.update(got)
        elif kind == "half":
            self.done.update(got)
        for group in same_shape_groups(names, got):
            if kind == "pair":
                p32s, p16s = pair_sum([self.g[n] for n in group], [got[n] for n in group], self.sel_arr, "pair_sum_" + group[0])
                self.p32.update(zip(group, p32s))
                self.p16.update(zip(group, p16s))
            elif kind == "chip":
                fulls = chip_sum([self.p32[n] for n in group], [got[n] for n in group], self.sel_arr, "chip_sum_" + group[0])
                self.full.update(zip(group, fulls))

    def finish(self, gb):
        return self.done

    def last_job(self):
        self.jobs["end_half"] = half_exchange_job(self.full, self.FIRST)
        return self.jobs["end_half"]

    def take_last(self):
        self.done.update(zip(self.FIRST, self.jobs["end_half"].landed))


class NoComm:
    def job(self, tag, gb=None):
        return None

    def landed(self, tag, wb=None, ws=None):
        pass

    def finish(self, gb):
        return gb


SMALL = ("rel_bias", "g_mix", "w_short_conv", "g_attn_out", "g_conv_out", "g_xattn", "g_mem", "g_ffn",
         "w_ffn_conv", "b_ffn_conv", "g_final")
SMALL_SHARDED = ("w_short_conv", "w_ffn_conv")


def kernel(x, mem, rel_bias, g_mix, w_in, w_short_conv, g_attn_out, g_conv_out, w_out, g_xattn, g_mem, w_xq, w_xk, w_xv, w_xo, g_ffn, w_up, w_ffn_conv, b_ffn_conv, w_down, g_final, loss_target, m_rel_bias, m_g_mix, m_w_in, m_w_short_conv, m_g_attn_out, m_g_conv_out, m_w_out, m_g_xattn, m_g_mem, m_w_xq, m_w_xk, m_w_xv, m_w_xo, m_g_ffn, m_w_up, m_w_ffn_conv, m_b_ffn_conv, m_w_down, m_g_final, v_rel_bias, v_g_mix, v_w_in, v_w_short_conv, v_g_attn_out, v_g_conv_out, v_w_out, v_g_xattn, v_g_mem, v_w_xq, v_w_xk, v_w_xv, v_w_xo, v_g_ffn, v_w_up, v_w_ffn_conv, v_b_ffn_conv, v_w_down, v_g_final):
    names = ("rel_bias", "g_mix", "w_in", "w_short_conv", "g_attn_out", "g_conv_out", "w_out", "g_xattn", "g_mem", "w_xq",
             "w_xk", "w_xv", "w_xo", "g_ffn", "w_up", "w_ffn_conv", "b_ffn_conv", "w_down", "g_final")
    W = dict(zip(names, (rel_bias, g_mix, w_in, w_short_conv, g_attn_out, g_conv_out, w_out, g_xattn, g_mem, w_xq, w_xk, w_xv,
                         w_xo, g_ffn, w_up, w_ffn_conv, b_ffn_conv, w_down, g_final)))
    M = dict(zip(names, (m_rel_bias, m_g_mix, m_w_in, m_w_short_conv, m_g_attn_out, m_g_conv_out, m_w_out, m_g_xattn, m_g_mem,
                         m_w_xq, m_w_xk, m_w_xv, m_w_xo, m_g_ffn, m_w_up, m_w_ffn_conv, m_b_ffn_conv, m_w_down, m_g_final)))
    V = dict(zip(names, (v_rel_bias, v_g_mix, v_w_in, v_w_short_conv, v_g_attn_out, v_g_conv_out, v_w_out, v_g_xattn, v_g_mem,
                         v_w_xq, v_w_xk, v_w_xv, v_w_xo, v_g_ffn, v_w_up, v_w_ffn_conv, v_b_ffn_conv, v_w_down, v_g_final)))
    xi, yi, ci = lax.axis_index("x"), lax.axis_index("y"), lax.axis_index("c")
    mine = 2 * xi + yi
    mine_arr = jnp.reshape(mine, (1,)).astype(jnp.int32)
    sel_arr = jnp.stack([mine, ci]).astype(jnp.int32)
    conv_placed = []
    for n in SMALL_SHARDED:
        shard = W[n][0]
        conv_placed.append(lax.dynamic_update_slice(jnp.zeros((8, 4 * shard.shape[1]), F32), shard, (0, mine * shard.shape[1])))
    shards = {n: W[n][0] for n in BIG}
    placed = {}
    for group in same_shape_groups(BIG, shards):
        outs = place_shards([shards[n] for n in group], mine_arr, group[0] in COL_SHARDED, "place_" + group[0])
        placed.update(zip(group, outs))
    comm = StepComm(placed, conv_placed, sel_arr)
    ws = {n: W[n] for n in SMALL if n not in SMALL_SHARDED}

    loss, grad_x, gfull, gs = local_step(x[0], mem[0], loss_target[0], None, ws, comm)

    def as2d(a):
        return a.reshape(1, -1) if a.ndim == 1 else a

    reduced = allreduce_small([as2d(gs[n]) for n in SMALL] + [loss], "reduce_small", job=comm.last_job())
    comm.take_last()
    gsm = dict(zip(SMALL, reduced[:-1]))
    loss = reduced[-1][0, 0]

    grads, delta, new_m, new_v = {}, {}, {}, {}
    for group in same_shape_groups(BIG, gfull):
        ds, nms, nvs = adamw([W[n][0] for n in group], [gfull[n] for n in group], [M[n][0] for n in group],
                             [V[n][0] for n in group], "adamw_" + group[0])
        for n, d, nm, nv in zip(group, ds, nms, nvs):
            grads[n], delta[n], new_m[n], new_v[n] = gfull[n][None], d[None], nm[None], nv[None]
    for n in SMALL_SHARDED:
        wid = W[n].shape[2]
        gsm[n] = lax.dynamic_slice(gsm[n], (0, mine * wid), (3, wid))

    def own(a, n):
        return a[0] if n in SMALL_SHARDED else as2d(a)

    d, nm, nv = adamw_multi([own(W[n], n) for n in SMALL], [gsm[n] for n in SMALL], [own(M[n], n) for n in SMALL],
                            [own(V[n], n) for n in SMALL], "adamw_small")
    for i, n in enumerate(SMALL):
        shape = W[n].shape
        grads[n], delta[n], new_m[n], new_v[n] = (a.reshape(shape) for a in (gsm[n], d[i], nm[i], nv[i]))
    return (loss, grad_x[None], *[grads[n] for n in names], *[delta[n] for n in names],
            *[new_m[n] for n in names], *[new_v[n] for n in names])
```

```python
import math

import numpy as np
import jax
import jax.numpy as jnp
from jax import lax
from jax.experimental import pallas as pl
from jax.experimental.pallas import tpu as pltpu

F32 = jnp.float32
BF16 = jnp.bfloat16
SDS = jax.ShapeDtypeStruct
MESH = pl.DeviceIdType.MESH

D_MODEL = 1024
ATTN_W = 512
N_HEADS = 8
HEAD_DIM = 64
HALF_HEAD = HEAD_DIM // 2
Q_SCALE = HEAD_DIM ** -0.5
BLK = 128
PATTERNS = ((128, 1), (512, 4), (2048, 16))
N_BUCKETS = 32
BUCKET_MAX_DIST = 2048
D_FF = 2816
N_MEM_HEADS = 4
MEM_HD = 256
EPS = 1e-6
NEG = -1e30
VMEM_LIMIT = 56 * 1024 * 1024

ADAM_LR, ADAM_B1, ADAM_B2, ADAM_EPS, ADAM_WD, ADAM_STEP = 0.001, 0.9, 0.999, 1e-08, 0.01, 10


def _cp(*sem):
    return pltpu.CompilerParams(dimension_semantics=sem, vmem_limit_bytes=VMEM_LIMIT)


def _dot(a, b):
    return jnp.dot(a, b, preferred_element_type=F32)


def _dot_nt(a, b):
    return lax.dot_general(a, b, (((1,), (1,)), ((), ())), preferred_element_type=F32)


def _dot_tn(a, b):
    return lax.dot_general(a, b, (((0,), (0,)), ((), ())), preferred_element_type=F32)


def _rsum(x):
    return jnp.sum(x, axis=1, keepdims=True)


def rmsnorm_fwd(x, g, name, job=None):
    S, Dm = x.shape
    tm = min(S, 512)

    def body(x_ref, g_ref, o_ref):
        xv = x_ref[...]
        r = lax.rsqrt(jnp.mean(xv * xv, axis=1, keepdims=True) + EPS)
        o_ref[...] = (xv * r * g_ref[...]).astype(o_ref.dtype)

    return _pcall(body, [x, g], grid=(S // tm,), name=name,
                  in_specs=[pl.BlockSpec((tm, Dm), lambda i: (i, 0)), pl.BlockSpec((1, Dm), lambda i: (0, 0))],
                  out_specs=[pl.BlockSpec((tm, Dm), lambda i: (i, 0))], out_shape=[SDS((S, Dm), BF16)],
                  sem=("parallel",), job=job)[0]


def rmsnorm_gain_grad(x, dh, name):
    S, Dm = x.shape
    tm = min(S, 512)

    def body(x_ref, dh_ref, dg_ref):
        xv = x_ref[...]
        xh = xv * lax.rsqrt(jnp.mean(xv * xv, axis=1, keepdims=True) + EPS)

        @pl.when(pl.program_id(0) == 0)
        def _():
            dg_ref[...] = jnp.zeros_like(dg_ref)

        dg_ref[...] += jnp.sum(dh_ref[...].astype(F32) * xh, axis=0, keepdims=True)

    row = pl.BlockSpec((tm, Dm), lambda i: (i, 0))
    return pl.pallas_call(
        body, grid=(S // tm,), name=name, in_specs=[row, row], out_specs=pl.BlockSpec((1, Dm), lambda i: (0, 0)),
        out_shape=SDS((1, Dm), F32), compiler_params=_cp("arbitrary"))(x, dh)


def _col_spec(bw, tn, rows, row_of, col_of):
    if bw is None:
        return pl.BlockSpec((rows, tn), lambda *g: (row_of(*g), col_of(*g)))
    if tn % bw == 0:
        return pl.BlockSpec((tn // bw, rows, bw), lambda *g: (col_of(*g), row_of(*g), 0))
    per = bw // tn
    return pl.BlockSpec((None, rows, tn), lambda *g: (col_of(*g) // per, row_of(*g), col_of(*g) % per))


def _read_cols(ref, bw, tn):
    if bw is None or tn % bw != 0:
        return ref[...]
    if tn == bw:
        return ref[0]
    return jnp.concatenate([ref[q] for q in range(tn // bw)], axis=1)


def _write_cols(ref, bw, tn, val):
    if bw is None or tn % bw != 0:
        ref[...] = val.astype(ref.dtype)
    else:
        for q in range(tn // bw):
            ref[q] = val[:, q * bw:(q + 1) * bw].astype(ref.dtype)


def mm_nn(a, b, name, *, out_dtype=F32, out_bw=None, tm=1024, tn=512, job=None):
    M, K = a.shape
    N = b.shape[1]
    tm = min(tm, M)

    def body(a_ref, b_ref, o_ref):
        _write_cols(o_ref, out_bw, tn, _dot(a_ref[...].astype(BF16), b_ref[...]))

    ri, ci = (lambda i, j: i), (lambda i, j: j)
    in_specs = [pl.BlockSpec((tm, K), lambda i, j: (i, 0)), pl.BlockSpec((K, tn), lambda i, j: (0, j))]
    oshape = (M, N) if out_bw is None else (N // out_bw, M, out_bw)
    return _pcall(body, [a, b], grid=(M // tm, N // tn), name=name, in_specs=in_specs,
                  out_specs=[_col_spec(out_bw, tn, tm, ri, ci)], out_shape=[SDS(oshape, out_dtype)],
                  sem=("parallel", "parallel"), job=job)[0]


def mm_nn_rows(a, b, res, g, name, *, target=None, tm=512):
    M, K = a.shape
    N = b.shape[1]

    def body(*refs):
        a_ref, b_ref, r_ref, g_ref = refs[:4]
        xv = r_ref[...] + _dot(a_ref[...].astype(BF16), b_ref[...])
        r = lax.rsqrt(jnp.mean(xv * xv, axis=1, keepdims=True) + EPS)
        xh = xv * r
        if target is None:
            x_ref, h_ref = refs[4:]
            x_ref[...] = xv
            h_ref[...] = (xh * g_ref[...]).astype(BF16)
            return
        t_ref, loss_ref, dx_ref, dg_ref = refs[4:]
        err = xh * g_ref[...] - t_ref[...]
        dy = err / N
        gd = dy * g_ref[...]
        dx_ref[...] = r * (gd - xh * jnp.mean(gd * xh, axis=1, keepdims=True))

        @pl.when(pl.program_id(0) == 0)
        def _():
            dg_ref[...] = jnp.zeros_like(dg_ref)
            loss_ref[...] = jnp.zeros_like(loss_ref)

        dg_ref[...] += _colsum(dy * xh)
        loss_ref[...] += 0.5 * _colsum(jnp.mean(err * err, axis=1, keepdims=True))

    row = pl.BlockSpec((tm, N), lambda i: (i, 0))
    vec = pl.BlockSpec((1, N), lambda i: (0, 0))
    in_specs = [pl.BlockSpec((tm, K), lambda i: (i, 0)), pl.BlockSpec((K, N), lambda i: (0, 0)), row, vec]
    if target is None:
        return pl.pallas_call(body, grid=(M // tm,), name=name, in_specs=in_specs, out_specs=[row, row],
                              out_shape=[SDS((M, N), F32), SDS((M, N), BF16)], compiler_params=_cp("parallel"))(a, b, res, g)
    one = pl.BlockSpec((1, 1), lambda i: (0, 0))
    return pl.pallas_call(body, grid=(M // tm,), name=name, in_specs=in_specs + [row], out_specs=[one, row, vec],
                          out_shape=[SDS((1, 1), F32), SDS((M, N), F32), SDS((1, N), F32)],
                          compiler_params=_cp("arbitrary"))(a, b, res, g, target)


def mm_nt(a, a_bw, b, name, *, res=None, out_dtype=F32, tm=1024, tn=512, job=None):
    if a_bw is None:
        M, K = a.shape
    else:
        M, K = a.shape[1], a.shape[0] * a_bw
    N = b.shape[0]
    tm = min(tm, M)

    def body(*refs):
        if res is None:
            a_ref, b_ref, o_ref = refs
        else:
            a_ref, b_ref, r_ref, o_ref = refs
        av = _read_cols(a_ref, a_bw, K).astype(BF16)
        acc = _dot_nt(av, b_ref[...])
        if res is not None:
            acc = acc + r_ref[...]
        o_ref[...] = acc.astype(o_ref.dtype)

    in_specs = [_col_spec(a_bw, K, tm, lambda i, j: i, lambda i, j: 0), pl.BlockSpec((tn, K), lambda i, j: (j, 0))]
    args = [a, b]
    if res is not None:
        in_specs.append(pl.BlockSpec((tm, tn), lambda i, j: (i, j)))
        args.append(res)
    return _pcall(body, args, grid=(M // tm, N // tn), name=name, in_specs=in_specs,
                  out_specs=[pl.BlockSpec((tm, tn), lambda i, j: (i, j))], out_shape=[SDS((M, N), out_dtype)],
                  sem=("parallel", "parallel"), job=job)[0]


def mm_nt_norm_bwd(a, a_bw, b, x, g, res, name, *, tm=512, job=None):
    if a_bw is None:
        M, K = a.shape
    else:
        M, K = a.shape[1], a.shape[0] * a_bw
    N = b.shape[0]

    def body(a_ref, b_ref, x_ref, g_ref, r_ref, dx_ref, dg_ref):
        dh = _dot_nt(_read_cols(a_ref, a_bw, K).astype(BF16), b_ref[...])
        xv = x_ref[...]
        r = lax.rsqrt(jnp.mean(xv * xv, axis=1, keepdims=True) + EPS)
        xh = xv * r
        gd = dh * g_ref[...]
        dx_ref[...] = r_ref[...] + r * (gd - xh * jnp.mean(gd * xh, axis=1, keepdims=True))

        @pl.when(pl.program_id(0) == 0)
        def _():
            dg_ref[...] = jnp.zeros_like(dg_ref)

        dg_ref[...] += jnp.sum(dh * xh, axis=0, keepdims=True)

    row = pl.BlockSpec((tm, N), lambda i: (i, 0))
    vec = pl.BlockSpec((1, N), lambda i: (0, 0))
    return _pcall(body, [a, b, x, g, res], grid=(M // tm,), name=name,
                  in_specs=[_col_spec(a_bw, K, tm, lambda i: i, lambda i: 0),
                            pl.BlockSpec((N, K), lambda i: (0, 0), pipeline_mode=pl.Buffered(1)), row, vec, row],
                  out_specs=[row, vec], out_shape=[SDS((M, N), F32), SDS((1, N), F32)], sem=("arbitrary",), job=job)


def mm_tn(a, b, b_bw, name, *, shards=None, tm=1024, tn=1024, ts=1024, job=None):
    S, Ka = a.shape
    N = b.shape[1] if b_bw is None else b.shape[0] * b_bw
    ts = min(ts, S)
    tm = min(tm, Ka)

    def body(a_ref, b_ref, o_ref):
        @pl.when(pl.program_id(2) == 0)
        def _():
            o_ref[...] = jnp.zeros_like(o_ref)

        bv = _read_cols(b_ref, b_bw, tn).astype(BF16)
        o_ref[...] += _dot_tn(a_ref[...].astype(BF16), bv)

    in_specs = [pl.BlockSpec((ts, tm), lambda i, j, k: (k, i)),
                _col_spec(b_bw, tn, ts, lambda i, j, k: k, lambda i, j, k: j)]
    if shards is None:
        out_spec, oshape = pl.BlockSpec((tm, tn), lambda i, j, k: (i, j)), (Ka, N)
    else:
        per = (N // shards) // tn
        out_spec = pl.BlockSpec((None, tm, tn), lambda i, j, k: (j // per, i, j % per))
        oshape = (shards, Ka, N // shards)
    return _pcall(body, [a, b], grid=(Ka // tm, N // tn, S // ts), name=name, in_specs=in_specs, out_specs=[out_spec],
                  out_shape=[SDS(oshape, F32)], sem=("parallel", "parallel", "arbitrary"), job=job)[0]


KEYS = 2 * BLK


def _bucket_tables():
    out = np.zeros((3, BLK, KEYS), np.int32)
    qi = np.arange(BLK)[:, None]
    kj = np.arange(KEYS)[None, :]
    for p, (win, dil) in enumerate(PATTERNS):
        w = win // dil
        assert w == BLK
        steps = qi + w - kj
        valid = (steps >= 0) & (steps <= w)
        dist = np.clip(steps, 0, w) * dil
        dd = np.maximum(dist, 1).astype(np.float32)
        exact = N_BUCKETS // 2
        large = exact + (np.log(dd / np.float32(exact)) / np.float32(math.log(BUCKET_MAX_DIST / exact))
                         * np.float32(N_BUCKETS - exact)).astype(np.int32)
        large = np.minimum(large, N_BUCKETS - 1)
        out[p] = np.where(valid, np.where(dist < exact, dist, large), -1)
    return out


def bias_tables(rel_bias):
    bkt = jnp.asarray(_bucket_tables())

    def body(rb_ref, bkt_ref, o_ref):
        bk = bkt_ref[...]
        for h in range(N_HEADS):
            acc = jnp.full((BLK, KEYS), NEG, F32)
            for b in range(N_BUCKETS):
                acc = jnp.where(bk == b, rb_ref[h, b], acc)
            o_ref[h] = acc

    return pl.pallas_call(
        body, grid=(3,), name="bias_tables",
        in_specs=[pl.BlockSpec(memory_space=pltpu.SMEM),
                  pl.BlockSpec((None, BLK, KEYS), lambda p: (p, 0, 0))],
        out_specs=pl.BlockSpec((None, N_HEADS, BLK, KEYS), lambda p: (p, 0, 0, 0)),
        out_shape=SDS((3, N_HEADS, BLK, KEYS), F32), compiler_params=_cp("parallel"))(rel_bias, bkt)


def bias_tables_bwd(ds_sum, job=None):
    bkt = jnp.asarray(_bucket_tables())

    def body(ds_ref, bkt_ref, o_ref):
        h = pl.program_id(0)

        @pl.when(h == 0)
        def _():
            o_ref[...] = jnp.zeros_like(o_ref)

        rows = lax.broadcasted_iota(jnp.int32, (N_HEADS, N_BUCKETS), 0)
        cols = lax.broadcasted_iota(jnp.int32, (N_HEADS, N_BUCKETS), 1)
        acc = jnp.zeros((N_HEADS, N_BUCKETS), F32)
        for b in range(N_BUCKETS):
            t = jnp.zeros((BLK, KEYS), F32)
            for p in range(3):
                t = t + jnp.where(bkt_ref[p] == b, ds_ref[p], 0.0)
            tot = jnp.sum(_rsum(t), axis=0, keepdims=True)
            acc = acc + jnp.where((rows == h) & (cols == b), tot, 0.0)
        o_ref[...] += acc

    return _pcall(
        body, [ds_sum, bkt], grid=(N_HEADS,), name="bias_tables_bwd",
        in_specs=[pl.BlockSpec((3, None, BLK, KEYS), lambda h: (0, h, 0, 0)),
                  pl.BlockSpec((3, BLK, KEYS), lambda h: (0, 0, 0))],
        out_specs=[pl.BlockSpec((N_HEADS, N_BUCKETS), lambda h: (0, 0))],
        out_shape=[SDS((N_HEADS, N_BUCKETS), F32)], sem=("arbitrary",), job=job)[0]


def _rows(start, dil):
    if dil == 1:
        return pl.ds(pl.multiple_of(start, BLK), BLK)
    return pl.ds(start, BLK, stride=dil)


GRP = 8


def _group_rows(i, dil, S):
    nb = S // (BLK * dil)
    run = min(nb, GRP)
    chunks = nb // run
    res0, b0 = (i // chunks) * (GRP // run), (i % chunks) * run
    cur = [(b0 + j % run) * (BLK * dil) + res0 + j // run for j in range(GRP)]
    t = lax.broadcasted_iota(jnp.int32, (GRP, 1, 1), 0)
    if run == nb:
        before, pen = None, jnp.where(t % run == 0, NEG, 0.0)
    else:
        before = _rows(jnp.maximum(b0 - 1, 0) * (BLK * dil) + res0, dil)
        pen = jnp.where((t == 0) & (b0 == 0), NEG, 0.0)
    return [_rows(s, dil) for s in cur], before, pen


def _load_group(ref, rows):
    return jnp.stack([ref[r, :] for r in rows])


def _with_prev(ref, before, cur_blocks):
    first = cur_blocks[:1] if before is None else ref[before, :][None].astype(cur_blocks.dtype)
    return jnp.concatenate([first, cur_blocks[:-1]], axis=0)


def _bdot_nt(a, b):
    return lax.dot_general(a, b, (((2,), (2,)), ((0,), (0,))), preferred_element_type=F32)


def _bdot(a, b):
    return lax.dot_general(a, b, (((2,), (1,)), ((0,), (0,))), preferred_element_type=F32)


def _bdot_tn(a, b):
    return lax.dot_general(a, b, (((1,), (1,)), ((0,), (0,))), preferred_element_type=F32)


def _lsum(x):
    return jnp.sum(x, axis=-1, keepdims=True)


def _widen(src, dst):
    S = src.shape[1]

    def chunk(i, carry):
        rows = pl.ds(pl.multiple_of(i * 512, 512), 512)
        for a in range(3):
            dst[a, rows, :] = src[a, rows, :].astype(F32)
        return carry

    lax.fori_loop(0, S // 512, chunk, 0)


def attn_fwd(qkv, bias, job=None):
    S = qkv.shape[3]
    nblk = S // BLK
    car = Carried(job, 2, 2, 3, 4)

    def body(*refs):
        (qkv_in, bias_ref, o_ref, lse_ref, qkv_ref, part_o, part_ml), jrefs = car.split(refs)
        car.phase(0, pl.program_id(0), jrefs)
        _widen(qkv_in, qkv_ref)
        lane = lax.broadcasted_iota(jnp.int32, (GRP, BLK, BLK), 2)
        lo = lane < HEAD_DIM
        masks = (lo, jnp.logical_not(lo))
        lov = lax.broadcasted_iota(jnp.int32, (GRP, KEYS, BLK), 2) < HEAD_DIM
        vmasks = (lov, jnp.logical_not(lov))
        prev_keys = lax.broadcasted_iota(jnp.int32, (1, 1, KEYS), 2) < BLK
        q_ref, k_ref, v_ref = qkv_ref.at[0], qkv_ref.at[1], qkv_ref.at[2]
        targets = [(o_ref, lse_ref)] + [(part_o.at[t], part_ml.at[t]) for t in range(len(PATTERNS) - 1)]
        for p, (_, dil) in enumerate(PATTERNS):
            def step(i, carry, p=p, dil=dil):
                rc, before, pen = _group_rows(i, dil, S)
                q2 = _load_group(q_ref, rc) * Q_SCALE
                kc = _load_group(k_ref, rc).astype(BF16)
                keys = jnp.concatenate([_with_prev(k_ref, before, kc), kc], axis=1)
                vc = _load_group(v_ref, rc)
                vals = jnp.concatenate([_with_prev(v_ref, before, vc), vc], axis=1)
                pen = jnp.where(prev_keys, pen, 0.0)
                pv = jnp.zeros((GRP, BLK, BLK), F32)
                ms, ls = [], []
                for h in range(2):
                    qh = jnp.where(masks[h], q2, 0.0).astype(BF16)
                    s = _bdot_nt(qh, keys) + bias_ref[p, h][None] + pen
                    mn = jnp.max(s, axis=-1, keepdims=True)
                    pr = jnp.exp(s - mn)
                    pv = pv + _bdot(pr.astype(BF16), jnp.where(vmasks[h], vals, 0.0).astype(BF16))
                    ms.append(mn)
                    ls.append(_lsum(pr))
                ml_t = jnp.where(lo, jnp.where(lane < HALF_HEAD, ms[0], ls[0]),
                                 jnp.where(lane < HEAD_DIM + HALF_HEAD, ms[1], ls[1]))
                acc_ref, ml_ref = targets[p]
                for j, r in enumerate(rc):
                    acc_ref[r, :] = pv[j]
                    ml_ref[r, :] = ml_t[j]
                return carry

            lax.fori_loop(0, nblk // GRP, step, 0)

        def merge(i, carry):
            rows = pl.ds(pl.multiple_of(i * 512, 512), 512)
            is_m = (lax.broadcasted_iota(jnp.int32, (512, BLK), 1) & HALF_HEAD) == 0
            accs = [ref[rows, :] for ref, _ in targets]
            mls = [ref[rows, :] for _, ref in targets]
            mx = [jnp.where(is_m, ml, pltpu.roll(ml, HALF_HEAD, 1)) for ml in mls]
            dn = [jnp.where(is_m, pltpu.roll(ml, BLK - HALF_HEAD, 1), ml) for ml in mls]
            top = jnp.maximum(jnp.maximum(mx[0], mx[1]), mx[2])
            ws_ = [jnp.exp(m - top) for m in mx]
            den = ws_[0] * dn[0] + ws_[1] * dn[1] + ws_[2] * dn[2]
            o_ref[rows, :] = (ws_[0] * accs[0] + ws_[1] * accs[1] + ws_[2] * accs[2]) / den
            lse_ref[rows, :] = top + jnp.log(den)
            return carry

        lax.fori_loop(0, S // 512, merge, 0)
        car.phase(1, pl.program_id(0), jrefs)
        car.phase(2, pl.program_id(0), jrefs)

    outs = pl.pallas_call(
        body, grid=(4,), name="attn_fwd",
        in_specs=[pl.BlockSpec((None, 3, None, S, BLK), lambda g: (0, 0, g, 0, 0)),
                  pl.BlockSpec((3, 2, BLK, KEYS), lambda g: (0, g, 0, 0))] + car.in_specs(),
        out_specs=[pl.BlockSpec((None, S, BLK), lambda g: (g, 0, 0)),
                   pl.BlockSpec((None, S, BLK), lambda g: (g, 0, 0))] + car.out_specs(),
        out_shape=[SDS((4, S, BLK), F32), SDS((4, S, BLK), F32)] + car.out_shapes(),
        input_output_aliases=car.aliases(),
        scratch_shapes=[pltpu.VMEM((3, S, BLK), F32), pltpu.VMEM((2, S, BLK), F32), pltpu.VMEM((2, S, BLK), F32)] + car.sems(),
        compiler_params=_cp("arbitrary"))(qkv, bias, *car.args())
    if job is not None:
        job.landed = list(outs[2:])
    return outs[0], outs[1]


def attn_bwd(qkv, bias, o, lse, do, job=None):
    S = qkv.shape[3]
    nblk = S // BLK
    car = Carried(job, 5, 2, 4, 4)

    def body(*refs):
        (qkv_in, bias_ref, o_ref, lse_ref, do_ref, dqkv_out, ds_ref, qkv_ref, dqkv_ref, last_ref, st_ref), jrefs = car.split(refs)
        car.phase(0, pl.program_id(0), jrefs)
        _widen(qkv_in, qkv_ref)

        def stats(i, carry):
            rows = pl.ds(pl.multiple_of(i * 512, 512), 512)
            ln = lax.broadcasted_iota(jnp.int32, (512, BLK), 1)
            prod = do_ref[rows, :] * o_ref[rows, :]
            d0 = _rsum(jnp.where(ln < HEAD_DIM, prod, 0.0))
            d1 = _rsum(jnp.where(ln < HEAD_DIM, 0.0, prod))
            st_ref[rows, :] = jnp.where((ln & HALF_HEAD) == 0, lse_ref[rows, :], jnp.where(ln < HEAD_DIM, d0, d1))
            return carry

        lax.fori_loop(0, S // 512, stats, 0)
        lane = lax.broadcasted_iota(jnp.int32, (GRP, BLK, BLK), 2)
        lo = lane < HEAD_DIM
        masks = (lo, jnp.logical_not(lo))
        lov = lax.broadcasted_iota(jnp.int32, (GRP, KEYS, BLK), 2) < HEAD_DIM
        vmasks = (lov, jnp.logical_not(lov))
        prev_keys = lax.broadcasted_iota(jnp.int32, (1, 1, KEYS), 2) < BLK
        ds_ref[...] = jnp.zeros_like(ds_ref)
        q_ref, k_ref, v_ref = qkv_ref.at[0], qkv_ref.at[1], qkv_ref.at[2]
        last = len(PATTERNS) - 1
        for p, (_, dil) in enumerate(PATTERNS):
            def step(i, carry, p=p, dil=dil):
                tgt = last_ref if p == last else dqkv_ref
                dq_ref, dk_ref, dv_ref = tgt.at[0], tgt.at[1], tgt.at[2]
                rc, before, pen = _group_rows(i, dil, S)
                q2 = _load_group(q_ref, rc) * Q_SCALE
                kc = _load_group(k_ref, rc).astype(BF16)
                keys = jnp.concatenate([_with_prev(k_ref, before, kc), kc], axis=1)
                vc = _load_group(v_ref, rc).astype(BF16)
                vals = jnp.concatenate([_with_prev(v_ref, before, vc), vc], axis=1)
                pen = jnp.where(prev_keys, pen, 0.0)
                dot = _load_group(do_ref, rc)
                st = _load_group(st_ref, rc)
                dq = jnp.zeros((GRP, BLK, BLK), F32)
                dkeys = jnp.zeros((GRP, KEYS, BLK), F32)
                dvals = jnp.zeros((GRP, KEYS, BLK), F32)
                for h in range(2):
                    qh = jnp.where(masks[h], q2, 0.0).astype(BF16)
                    doh = jnp.where(masks[h], dot, 0.0).astype(BF16)
                    lh = st[:, :, HEAD_DIM * h:HEAD_DIM * h + 1]
                    delta = st[:, :, HEAD_DIM * h + HALF_HEAD:HEAD_DIM * h + HALF_HEAD + 1]
                    pr = jnp.exp(_bdot_nt(qh, keys) + bias_ref[p, h][None] + pen - lh)
                    ds = pr * (_bdot_nt(doh, vals) - delta)
                    ds_ref[p, h] += jnp.sum(ds, axis=0)
                    dsb = ds.astype(BF16)
                    dq = dq + jnp.where(masks[h], _bdot(dsb, keys), 0.0)
                    dkeys = dkeys + _bdot_tn(dsb, qh)
                    dvals = dvals + _bdot_tn(pr.astype(BF16), doh)
                dkp, dkc, dvp, dvc = dkeys[:, :BLK], dkeys[:, BLK:], dvals[:, :BLK], dvals[:, BLK:]
                none = jnp.zeros((1, BLK, BLK), F32)
                dkc = dkc + jnp.concatenate([dkp[1:], none], axis=0)
                dvc = dvc + jnp.concatenate([dvp[1:], none], axis=0)
                for j in range(GRP):
                    if p == 1:
                        dq_ref[rc[j], :] += dq[j] * Q_SCALE
                        dk_ref[rc[j], :] += dkc[j]
                        dv_ref[rc[j], :] += dvc[j]
                    else:
                        dq_ref[rc[j], :] = dq[j] * Q_SCALE
                        dk_ref[rc[j], :] = dkc[j]
                        dv_ref[rc[j], :] = dvc[j]
                if before is not None:
                    dk_ref[before, :] += dkp[0]
                    dv_ref[before, :] += dvp[0]
                return carry

            lax.fori_loop(0, nblk // GRP, step, 0)

        def narrow(i, carry):
            rows = pl.ds(pl.multiple_of(i * 512, 512), 512)
            for a in range(3):
                dqkv_out[a, rows, :] = (dqkv_ref[a, rows, :] + last_ref[a, rows, :]).astype(BF16)
            return carry

        lax.fori_loop(0, S // 512, narrow, 0)
        car.phase(1, pl.program_id(0), jrefs)
        car.phase(2, pl.program_id(0), jrefs)

    blk = pl.BlockSpec((None, S, BLK), lambda g: (g, 0, 0))
    outs = pl.pallas_call(
        body, grid=(4,), name="attn_bwd",
        in_specs=[pl.BlockSpec((None, 3, None, S, BLK), lambda g: (0, 0, g, 0, 0)),
                  pl.BlockSpec((3, 2, BLK, KEYS), lambda g: (0, g, 0, 0)), blk, blk, blk] + car.in_specs(),
        out_specs=[pl.BlockSpec((None, 3, None, S, BLK), lambda g: (0, 0, g, 0, 0)),
                   pl.BlockSpec((3, 2, BLK, KEYS), lambda g: (0, g, 0, 0))] + car.out_specs(),
        out_shape=[SDS((2, 3, 4, S, BLK), BF16), SDS((3, N_HEADS, BLK, KEYS), F32)] + car.out_shapes(),
        input_output_aliases=car.aliases(),
        scratch_shapes=[pltpu.VMEM((3, S, BLK), F32)] * 3 + [pltpu.VMEM((S, BLK), F32)] + car.sems(),
        compiler_params=_cp("arbitrary"))(qkv, bias, o, lse, do, *car.args())
    if job is not None:
        job.landed = list(outs[2:])
    return outs[0], outs[1]


def _shift_down(u, k, halo):
    n = u.shape[0]
    row = lax.broadcasted_iota(jnp.int32, u.shape, 0)
    out = pltpu.roll(u, k, 0)
    hn = halo.shape[0]
    for j in range(k):
        out = jnp.where(row == j, halo[hn - k + j:hn - k + j + 1, :], out)
    return out


def _shift_up(u, k, halo):
    n = u.shape[0]
    row = lax.broadcasted_iota(jnp.int32, u.shape, 0)
    out = pltpu.roll(u, n - k, 0)
    for j in range(k):
        out = jnp.where(row == n - k + j, halo[j:j + 1, :], out)
    return out


def _conv3(u, halo, w0, w1, w2):
    return _shift_down(u, 2, halo) * w0 + _shift_down(u, 1, halo) * w1 + u * w2


def _colsum(x):
    return jnp.sum(x, axis=0, keepdims=True)


def _mixer_specs(S, tm):
    conv = pl.BlockSpec((None, 12, tm, BLK), lambda i: (1, 0, i, 0))
    halo = pl.BlockSpec((None, 12, 16, BLK), lambda i: (1, 0, jnp.maximum(i * (tm // 16) - 1, 0), 0))
    ob = pl.BlockSpec((4, tm, BLK), lambda i: (0, i, 0))
    return conv, halo, ob


def _mixer_recompute(i, o_ref, pr_ref, ph_ref, w_ref):
    ob = [o_ref[q] for q in range(4)]
    gb = [pr_ref[q].astype(F32) for q in range(4)]
    gc = [pr_ref[4 + q].astype(F32) for q in range(4)]
    xi = [pr_ref[8 + q].astype(F32) for q in range(4)]
    keep = jnp.where(i > 0, 1.0, 0.0)
    u = [gc[q] * xi[q] for q in range(4)]
    hu = [ph_ref[4 + q].astype(F32) * ph_ref[8 + q].astype(F32) * keep for q in range(4)]
    w = [[w_ref[k:k + 1, q * BLK:(q + 1) * BLK] for k in range(3)] for q in range(4)]
    cv = [_conv3(u[q], hu[q], *w[q]) for q in range(4)]
    return ob, gb, gc, xi, u, hu, cv, w


def _rms_blocks(blocks):
    ss = sum(_rsum(b * b) for b in blocks)
    return lax.rsqrt(ss / (BLK * len(blocks)) + EPS)


def mixer_post_fwd(o, proj, w_sc, g_a, g_c):
    S = o.shape[1]
    tm = 512

    def body(o_ref, pr_ref, ph_ref, w_ref, ga_ref, gc_ref, m_ref):
        i = pl.program_id(0)
        ob, gb, _, _, _, _, cv, _ = _mixer_recompute(i, o_ref, pr_ref, ph_ref, w_ref)
        conv = [gb[q] * cv[q] for q in range(4)]
        ra, rc = _rms_blocks(ob), _rms_blocks(conv)
        for q in range(4):
            sl = slice(q * BLK, (q + 1) * BLK)
            m_ref[:, q * BLK:(q + 1) * BLK] = (ob[q] * ra * ga_ref[:, sl]).astype(BF16)
            m_ref[:, ATTN_W + q * BLK:ATTN_W + (q + 1) * BLK] = (conv[q] * rc * gc_ref[:, sl]).astype(BF16)

    conv_s, halo_s, ob_s = _mixer_specs(S, tm)
    full = lambda r, c: pl.BlockSpec((r, c), lambda i: (0, 0))
    return pl.pallas_call(
        body, grid=(S // tm,), name="mixer_post_fwd",
        in_specs=[ob_s, conv_s, halo_s, full(3, 512), full(1, 512), full(1, 512)],
        out_specs=pl.BlockSpec((tm, D_MODEL), lambda i: (i, 0)), out_shape=SDS((S, D_MODEL), BF16),
        compiler_params=_cp("parallel"))(o, proj, proj, w_sc, g_a, g_c)


def mixer_post_bwd_a(dmixed, o, proj, w_sc, g_a, g_c, job=None):
    S = o.shape[1]
    tm = 512

    def body(dm_ref, o_ref, pr_ref, ph_ref, w_ref, ga_ref, gc_ref, do_ref, dgb_ref, dcv_ref, dga_ref, dgc_ref):
        i = pl.program_id(0)
        ob, gb, _, _, _, _, cv, _ = _mixer_recompute(i, o_ref, pr_ref, ph_ref, w_ref)
        conv = [gb[q] * cv[q] for q in range(4)]
        ra, rc = _rms_blocks(ob), _rms_blocks(conv)

        @pl.when(i == 0)
        def _():
            dga_ref[...] = jnp.zeros_like(dga_ref)
            dgc_ref[...] = jnp.zeros_like(dgc_ref)

        for blocks, r, g_ref, off, dg_ref, is_attn in ((ob, ra, ga_ref, 0, dga_ref, True), (conv, rc, gc_ref, ATTN_W, dgc_ref, False)):
            xh = [blocks[q] * r for q in range(4)]
            dm = [dm_ref[:, off + q * BLK:off + (q + 1) * BLK].astype(F32) for q in range(4)]
            gd = [dm[q] * g_ref[:, q * BLK:(q + 1) * BLK] for q in range(4)]
            mean = sum(_rsum(gd[q] * xh[q]) for q in range(4)) / (4 * BLK)
            for q in range(4):
                dg_ref[:, q * BLK:(q + 1) * BLK] += _colsum(dm[q] * xh[q])
                dx = r * (gd[q] - xh[q] * mean)
                if is_attn:
                    do_ref[q] = dx
                else:
                    dgb_ref[q] = dx * cv[q]
                    dcv_ref[q] = dx * gb[q]

    conv_s, halo_s, ob_s = _mixer_specs(S, tm)
    full = lambda r, c: pl.BlockSpec((r, c), lambda i: (0, 0))
    return _pcall(
        body, [dmixed, o, proj, proj, w_sc, g_a, g_c], grid=(S // tm,), name="mixer_post_bwd_a",
        in_specs=[pl.BlockSpec((tm, D_MODEL), lambda i: (i, 0)), ob_s, conv_s, halo_s, full(3, 512), full(1, 512), full(1, 512)],
        out_specs=[ob_s, ob_s, ob_s, full(1, 512), full(1, 512)],
        out_shape=[SDS((4, S, BLK), F32)] * 3 + [SDS((1, 512), F32)] * 2, sem=("arbitrary",), job=job)


def mixer_post_bwd_b(dproj, dgb, dcv, proj, w_sc):
    S = proj.shape[2]
    tm = 512
    last = S // 8 - 1

    def body(dp_in, dgb_ref, dcv_ref, dn_ref, pr_ref, w_ref, dp_ref, dw_ref):
        i = pl.program_id(0)
        keep_next = jnp.where(i < pl.num_programs(0) - 1, 1.0, 0.0)

        @pl.when(i == 0)
        def _():
            dw_ref[...] = jnp.zeros_like(dw_ref)

        for q in range(4):
            sl = slice(q * BLK, (q + 1) * BLK)
            gc, xi = pr_ref[4 + q].astype(F32), pr_ref[8 + q].astype(F32)
            u = gc * xi
            dcv = dcv_ref[q]
            dn = dn_ref[q] * keep_next
            d1, d2 = _shift_up(dcv, 1, dn), _shift_up(dcv, 2, dn)
            w0, w1, w2 = (w_ref[k:k + 1, sl] for k in range(3))
            du = dcv * w2 + d1 * w1 + d2 * w0
            dw_ref[0:1, sl] += _colsum(d2 * u)
            dw_ref[1:2, sl] += _colsum(d1 * u)
            dw_ref[2:3, sl] += _colsum(dcv * u)
            dp_ref[q] = dgb_ref[q].astype(BF16)
            dp_ref[4 + q] = (du * xi).astype(BF16)
            dp_ref[8 + q] = (du * gc).astype(BF16)

    conv_s, _, ob_s = _mixer_specs(S, tm)
    nxt = pl.BlockSpec((4, 8, BLK), lambda i: (0, jnp.minimum((i + 1) * (tm // 8), last), 0))
    full = lambda r, c: pl.BlockSpec((r, c), lambda i: (0, 0))
    return pl.pallas_call(
        body, grid=(S // tm,), name="mixer_post_bwd_b",
        in_specs=[pl.BlockSpec(memory_space=pl.ANY), ob_s, ob_s, nxt, conv_s, full(3, 512)],
        out_specs=[conv_s, full(3, 512)],
        out_shape=[SDS(dproj.shape, BF16), SDS((3, 512), F32)],
        input_output_aliases={0: 0}, compiler_params=_cp("arbitrary"))(dproj, dgb, dcv, dcv, proj, w_sc)


def xattn_fwd(q, k, v):
    S = q.shape[0]
    tm = 512
    scale = MEM_HD ** -0.5

    def body(q_ref, k_ref, v_ref, o_ref):
        for h in range(N_MEM_HEADS):
            sl = slice(h * MEM_HD, (h + 1) * MEM_HD)
            s = _dot_nt(q_ref[:, sl], k_ref[:, sl]) * scale
            p = jnp.exp(s - jnp.max(s, axis=1, keepdims=True))
            p = p / _rsum(p)
            o_ref[:, sl] = _dot(p.astype(BF16), v_ref[:, sl]).astype(BF16)

    row = pl.BlockSpec((tm, D_MODEL), lambda i: (i, 0))
    kv = pl.BlockSpec(k.shape, lambda i: (0, 0))
    return pl.pallas_call(body, grid=(S // tm,), name="xattn_fwd", in_specs=[row, kv, kv], out_specs=row,
                          out_shape=SDS((S, D_MODEL), BF16), compiler_params=_cp("parallel"))(q, k, v)


def xattn_bwd(q, k, v, do):
    S = q.shape[0]
    tm = 512
    scale = MEM_HD ** -0.5

    def body(q_ref, k_ref, v_ref, do_ref, dq_ref, dk_ref, dv_ref):
        @pl.when(pl.program_id(0) == 0)
        def _():
            dk_ref[...] = jnp.zeros_like(dk_ref)
            dv_ref[...] = jnp.zeros_like(dv_ref)

        for h in range(N_MEM_HEADS):
            sl = slice(h * MEM_HD, (h + 1) * MEM_HD)
            qh, kh, vh, doh = q_ref[:, sl], k_ref[:, sl], v_ref[:, sl], do_ref[:, sl]
            s = _dot_nt(qh, kh) * scale
            p = jnp.exp(s - jnp.max(s, axis=1, keepdims=True))
            p = p / _rsum(p)
            dp = _dot_nt(doh, vh)
            ds = (p * (dp - _rsum(p * dp)) * scale).astype(BF16)
            dq_ref[:, sl] = _dot(ds, kh).astype(BF16)
            dk_ref[:, sl] += _dot_tn(ds, qh)
            dv_ref[:, sl] += _dot_tn(p.astype(BF16), doh)

    row = pl.BlockSpec((tm, D_MODEL), lambda i: (i, 0))
    kv = pl.BlockSpec(k.shape, lambda i: (0, 0))
    return pl.pallas_call(body, grid=(S // tm,), name="xattn_bwd", in_specs=[row, kv, kv, row], out_specs=[row, kv, kv],
                          out_shape=[SDS((S, D_MODEL), BF16), SDS(k.shape, F32), SDS(k.shape, F32)],
                          compiler_params=_cp("arbitrary"))(q, k, v, do)


FFN_TM, FFN_TC = 512, 1408


def _ffn_specs(S):
    tm, tc = FFN_TM, FFN_TC
    blk = pl.BlockSpec((2, tm, tc), lambda j, i: (0, i, j))
    nxt = pl.BlockSpec((2, 16, tc), lambda j, i: (0, jnp.minimum((i + 1) * (tm // 16), S // 16 - 1), j))
    wsp = pl.BlockSpec((2, 3, tc), lambda j, i: (0, 0, j))
    bsp = pl.BlockSpec((2, 1, tc), lambda j, i: (0, 0, j))
    return blk, nxt, wsp, bsp


def up_act_fwd(h, w_up, w_fc, b_fc, job=None):
    S, K = h.shape
    tm, tc = FFN_TM, FFN_TC
    nj = D_FF // tc

    def body(h_ref, wg_ref, wv_ref, w_ref, b_ref, pre_ref, gv_ref, a_ref, halo):
        i = pl.program_id(1)
        hv = h_ref[...]
        res = []
        for half, wt_ref in enumerate((wg_ref, wv_ref)):
            u = _dot(hv, wt_ref[...])
            hu = jnp.where(i > 0, halo[half], 0.0)
            halo[half] = u[tm - 8:, :]
            w0, w1, w2 = (w_ref[half, k:k + 1, :] for k in range(3))
            pre_ref[half] = u.astype(BF16)
            res.append(_conv3(u, hu, w0, w1, w2) + b_ref[half])
        g, v = res
        gv_ref[0] = g.astype(BF16)
        gv_ref[1] = v.astype(BF16)
        a_ref[...] = (g * jax.nn.sigmoid(g) * v).astype(BF16)

    blk = pl.BlockSpec((2, tm, tc), lambda j, i: (0, i, j))
    return _pcall(
        body, [h, w_up, w_up, w_fc, b_fc], grid=(nj, S // tm), name="up_act_fwd",
        in_specs=[pl.BlockSpec((tm, K), lambda j, i: (i, 0)), pl.BlockSpec((K, tc), lambda j, i: (0, j)),
                  pl.BlockSpec((K, tc), lambda j, i: (0, j + nj)), pl.BlockSpec((2, 3, tc), lambda j, i: (0, 0, j)),
                  pl.BlockSpec((2, 1, tc), lambda j, i: (0, 0, j))],
        out_specs=[blk, blk, pl.BlockSpec((tm, tc), lambda j, i: (i, j))],
        out_shape=[SDS((2, S, D_FF), BF16), SDS((2, S, D_FF), BF16), SDS((S, D_FF), BF16)],
        scratch=[pltpu.VMEM((2, 8, tc), F32)], sem=("parallel", "arbitrary"), job=job)


def ffn_act_bwd(dx, w_down, up, up_pre, w_fc, job=None):
    S = up.shape[1]

    def act_grads(da, g, v):
        sg = jax.nn.sigmoid(g)
        return da * v * (sg * (1.0 + g * (1.0 - sg))), da * g * sg

    def body(dx_ref, dxn_ref, wd_ref, gv_ref, gvn_ref, up_ref, w_ref, dp_ref, dw_ref, db_ref):
        i = pl.program_id(1)
        keep_next = jnp.where(i < pl.num_programs(1) - 1, 1.0, 0.0)

        @pl.when(i == 0)
        def _():
            dw_ref[...] = jnp.zeros_like(dw_ref)
            db_ref[...] = jnp.zeros_like(db_ref)

        wd = wd_ref[...]
        da = _dot_nt(dx_ref[...].astype(BF16), wd)
        dan = _dot_nt(dxn_ref[...].astype(BF16), wd) * keep_next
        here = act_grads(da, gv_ref[0].astype(F32), gv_ref[1].astype(F32))
        after = act_grads(dan, gvn_ref[0].astype(F32), gvn_ref[1].astype(F32))
        for half in range(2):
            d, dn = here[half], after[half]
            u = up_ref[half].astype(F32)
            d1, d2 = _shift_up(d, 1, dn), _shift_up(d, 2, dn)
            w0, w1, w2 = (w_ref[half, k:k + 1, :] for k in range(3))
            dp_ref[half] = (d * w2 + d1 * w1 + d2 * w0).astype(BF16)
            dw_ref[half, 0:1, :] += _colsum(d2 * u)
            dw_ref[half, 1:2, :] += _colsum(d1 * u)
            dw_ref[half, 2:3, :] += _colsum(d * u)
            db_ref[half] += _colsum(d)

    tm, tc = FFN_TM, FFN_TC
    blk, nxt, wsp, bsp = _ffn_specs(S)
    K = dx.shape[1]
    rows = pl.BlockSpec((tm, K), lambda j, i: (i, 0))
    rows_next = pl.BlockSpec((16, K), lambda j, i: (jnp.minimum((i + 1) * (tm // 16), S // 16 - 1), 0))
    return _pcall(body, [dx, dx, w_down, up, up, up_pre, w_fc], grid=(D_FF // tc, S // tm), name="ffn_act_bwd",
                  in_specs=[rows, rows_next, pl.BlockSpec((tc, K), lambda j, i: (j, 0)), blk, nxt, blk, wsp],
                  out_specs=[blk, wsp, bsp],
                  out_shape=[SDS((2, S, D_FF), BF16), SDS((2, 3, D_FF), F32), SDS((2, 1, D_FF), F32)],
                  sem=("parallel", "arbitrary"), job=job)


def local_step(x, mem, target, wb, ws, comm=None):
    S = x.shape[0]
    assert S % 2048 == 0
    if comm is None:
        comm = NoComm()
    else:
        wb = {}
    ws = dict(ws)

    bias = bias_tables(ws["rel_bias"])
    h1 = rmsnorm_fwd(x, ws["g_mix"], "norm_mix", job=comm.job("norm_mix"))
    comm.landed("norm_mix", wb, ws)
    w_fc = ws["w_ffn_conv"].reshape(3, 2, D_FF).transpose(1, 0, 2)
    b_fc = ws["b_ffn_conv"].reshape(2, 1, D_FF)
    proj = mm_nn(h1, wb["w_in"], "proj", out_dtype=BF16, out_bw=BLK, tn=768, job=comm.job("proj"))
    comm.landed("proj", wb)
    qkv = proj.reshape(2, 3, 4, S, BLK)
    o, lse = attn_fwd(qkv, bias, job=comm.job("attn_fwd"))
    comm.landed("attn_fwd", wb)
    proj4 = proj.reshape(2, 12, S, BLK)
    mixed = mixer_post_fwd(o, proj4, ws["w_short_conv"], ws["g_attn_out"], ws["g_conv_out"])
    x1, h2 = mm_nn_rows(mixed, wb["w_out"], x, ws["g_xattn"], "out_proj")
    mem_n = rmsnorm_fwd(mem, ws["g_mem"], "norm_mem")
    xq = mm_nn(h2, wb["w_xq"], "xq", out_dtype=BF16)
    xk = mm_nn(mem_n, wb["w_xk"], "xk", out_dtype=BF16, tn=1024)
    xv = mm_nn(mem_n, wb["w_xv"], "xv", out_dtype=BF16, tn=1024)
    xo = xattn_fwd(xq, xk, xv)
    x2, h3 = mm_nn_rows(xo, wb["w_xo"], x1, ws["g_ffn"], "xo_proj")
    up_pre, up, act = up_act_fwd(h3, wb["w_up"], w_fc, b_fc, job=comm.job("up_act_fwd"))
    comm.landed("up_act_fwd", wb)
    loss, dx3, dg_final = mm_nn_rows(act, wb["w_down"], x2, ws["g_final"].reshape(1, -1), "down_proj", target=target)

    gb, gs = {}, {"g_final": dg_final}
    gb["w_down"] = mm_tn(act, dx3, None, "dw_down", tm=1408, tn=1024)
    d_up_pre, dw_fc, db_fc = ffn_act_bwd(dx3, wb["w_down"], up, up_pre, w_fc, job=comm.job("ffn_act_bwd", gb))
    comm.landed("ffn_act_bwd")
    gs["b_ffn_conv"] = db_fc.reshape(1, 2 * D_FF)
    gs["w_ffn_conv"] = dw_fc.transpose(1, 0, 2).reshape(3, 2 * D_FF)
    gb["w_up"] = mm_tn(h3, d_up_pre, D_FF, "dw_up", shards=4, tn=1408, job=comm.job("dw_up"))
    comm.landed("dw_up")
    dx2, gs["g_ffn"] = mm_nt_norm_bwd(d_up_pre, D_FF, wb["w_up"], x2, ws["g_ffn"], dx3, "d_h3", tm=512,
                                      job=comm.job("d_h3", gb))
    comm.landed("d_h3")
    gb["w_xo"] = mm_tn(xo, dx2, None, "dw_xo")
    dxo = mm_nt(dx2, None, wb["w_xo"], "d_xo", out_dtype=BF16)
    dxq, dxk, dxv = xattn_bwd(xq, xk, xv, dxo)
    gb["w_xq"] = mm_tn(h2, dxq, None, "dw_xq")
    gb["w_xk"] = mm_tn(mem_n, dxk, None, "dw_xk", tn=1024)
    gb["w_xv"] = mm_tn(mem_n, dxv, None, "dw_xv", tn=1024)
    dmem_n = mm_nt(dxk, None, wb["w_xk"], "d_memk", tn=1024)
    dmem_n = mm_nt(dxv, None, wb["w_xv"], "d_memv", res=dmem_n, tn=1024)
    gs["g_mem"] = rmsnorm_gain_grad(mem, dmem_n, "norm_mem_bwd")
    dx1, gs["g_xattn"] = mm_nt_norm_bwd(dxq, None, wb["w_xq"], x1, ws["g_xattn"], dx2, "d_h2")
    gb["w_out"] = mm_tn(mixed, dx1, None, "dw_out")
    dmixed = mm_nt(dx1, None, wb["w_out"], "d_mixed", out_dtype=BF16)
    do, dgb, dcv, gs["g_attn_out"], gs["g_conv_out"] = mixer_post_bwd_a(
        dmixed, o, proj4, ws["w_short_conv"], ws["g_attn_out"], ws["g_conv_out"], job=comm.job("mixer_post_bwd_a", gb))
    comm.landed("mixer_post_bwd_a")
    dproj, ds_sum = attn_bwd(qkv, bias, o, lse, do, job=comm.job("attn_bwd"))
    comm.landed("attn_bwd")
    dproj, gs["w_short_conv"] = mixer_post_bwd_b(dproj.reshape(2, 12, S, BLK), dgb, dcv, proj4, ws["w_short_conv"])
    dproj = dproj.reshape(24, S, BLK)
    gb["w_in"] = mm_tn(h1, dproj, BLK, "dw_in", shards=4, tn=768, job=comm.job("dw_in"))
    comm.landed("dw_in")
    gs["rel_bias"] = bias_tables_bwd(ds_sum, job=comm.job("bias_tables_bwd", gb))
    comm.landed("bias_tables_bwd")
    grad_x, gs["g_mix"] = mm_nt_norm_bwd(dproj, BLK, wb["w_in"], x, ws["g_mix"], dx1, "d_h1", tm=512,
                                         job=comm.job("d_h1", gb))
    comm.landed("d_h1")
    return loss, grad_x, comm.finish(gb), gs


def _adam_update(w_ref, g_ref, m_ref, v_ref, d_ref, nm_ref, nv_ref):
    gv = g_ref[...]
    mn = ADAM_B1 * m_ref[...] + (1.0 - ADAM_B1) * gv
    vn = ADAM_B2 * v_ref[...] + (1.0 - ADAM_B2) * (gv * gv)
    m_hat = mn / (1.0 - ADAM_B1 ** ADAM_STEP)
    v_hat = vn / (1.0 - ADAM_B2 ** ADAM_STEP)
    d_ref[...] = -ADAM_LR * (m_hat / (jnp.sqrt(v_hat) + ADAM_EPS) + ADAM_WD * w_ref[...])
    nm_ref[...] = mn
    nv_ref[...] = vn


def adamw(ws, gs, ms, vs, name, job=None):
    n = len(ws)
    R, C = ws[0].shape
    tr = R // 4

    def body(*refs):
        for p in range(n):
            _adam_update(*(refs[q * n + p] for q in range(7)))

    blk = pl.BlockSpec((tr, C), lambda i: (i, 0))
    outs = _pcall(body, [*ws, *gs, *ms, *vs], grid=(4,), name=name, in_specs=[blk] * (4 * n), out_specs=[blk] * (3 * n),
                  out_shape=[SDS((R, C), F32)] * (3 * n), sem=("parallel",), job=job)
    return outs[:n], outs[n:2 * n], outs[2 * n:]


BIG = ("w_in", "w_out", "w_xq", "w_xk", "w_xv", "w_xo", "w_up", "w_down")
COL_SHARDED = ("w_in", "w_up")
ANY = pl.BlockSpec(memory_space=pl.ANY)


def _place():
    x, y, c = lax.axis_index("x"), lax.axis_index("y"), lax.axis_index("c")
    chips = [(1 - x, y), (x, 1 - y), (1 - x, 1 - y)]
    return x, y, c, chips


def _window(full, name, R, C, shard, half):
    r0, nr = (0, R) if half is None else (half * (R // 2), R // 2)
    if name in COL_SHARDED:
        return full.at[pl.ds(r0, nr), pl.ds(shard * C, C)]
    return full.at[pl.ds(shard * R + r0, nr), :]


def same_shape_groups(names, arrays):
    groups = {}
    for n in names:
        groups.setdefault(tuple(arrays[n].shape), []).append(n)
    return list(groups.values())


def place_shards(ws, mine_arr, col, name):
    n = len(ws)
    R, C = ws[0].shape

    def body(s_ref, *refs):
        for p in range(n):
            refs[n + p][...] = refs[p][...].astype(BF16)

    grid_spec = pltpu.PrefetchScalarGridSpec(
        num_scalar_prefetch=1, grid=(1,), in_specs=[pl.BlockSpec((R, C), lambda i, s: (0, 0))] * n,
        out_specs=[pl.BlockSpec((R, C), (lambda i, s: (0, s[0])) if col else (lambda i, s: (s[0], 0)))] * n)
    return pl.pallas_call(body, grid_spec=grid_spec, name=name,
                          out_shape=[SDS((R, 4 * C) if col else (4 * R, C), BF16)] * n,
                          compiler_params=_cp("arbitrary"))(mine_arr, *ws)


def _gather_jobs(names, shapes):
    nw = len(names)

    def start(full, sems):
        send, recv, fsend, frecv = sems
        x, y, c, chips = _place()
        mine = 2 * x + y
        for w, n in enumerate(names):
            R, C = shapes[w]
            own = _window(full[w], n, R, C, mine, c)
            for j, chip in enumerate(chips):
                pltpu.make_async_remote_copy(src_ref=own, dst_ref=own, send_sem=send.at[w, j], recv_sem=recv.at[w, j],
                                             device_id=(*chip, c), device_id_type=MESH).start()

    def mid(full, sems, lo=0, hi=nw):
        send, recv, fsend, frecv = sems
        x, y, c, chips = _place()
        sib = (x, y, 1 - c)
        for w, n in list(enumerate(names))[lo:hi]:
            R, C = shapes[w]
            for j, chip in enumerate(chips):
                landed = _window(full[w], n, R, C, 2 * chip[0] + chip[1], c)
                pltpu.make_async_remote_copy(src_ref=landed, dst_ref=landed, send_sem=send.at[w, j], recv_sem=recv.at[w, j],
                                             device_id=(*chip, c), device_id_type=MESH).wait_recv()
                pltpu.make_async_remote_copy(src_ref=landed, dst_ref=landed, send_sem=fsend.at[w, j],
                                             recv_sem=frecv.at[w, j], device_id=sib, device_id_type=MESH).start()

    def finish(full, sems):
        send, recv, fsend, frecv = sems
        x, y, c, chips = _place()
        mine = 2 * x + y
        sib = (x, y, 1 - c)
        for w, n in enumerate(names):
            R, C = shapes[w]
            own = _window(full[w], n, R, C, mine, c)
            for j, chip in enumerate(chips):
                landed = _window(full[w], n, R, C, 2 * chip[0] + chip[1], c)
                other = _window(full[w], n, R, C, 2 * chip[0] + chip[1], 1 - c)
                pltpu.make_async_remote_copy(src_ref=other, dst_ref=other, send_sem=fsend.at[w, j], recv_sem=frecv.at[w, j],
                                             device_id=sib, device_id_type=MESH).wait_recv()
                pltpu.make_async_remote_copy(src_ref=own, dst_ref=own, send_sem=send.at[w, j], recv_sem=recv.at[w, j],
                                             device_id=(*chip, c), device_id_type=MESH).wait_send()
                pltpu.make_async_remote_copy(src_ref=landed, dst_ref=landed, send_sem=fsend.at[w, j],
                                             recv_sem=frecv.at[w, j], device_id=sib, device_id_type=MESH).wait_send()

    return start, mid, finish, [pltpu.SemaphoreType.DMA((nw, 3))] * 4


class CommJob:
    def __init__(self, ins, out_shapes, inplace, start, finish, sems, mid=None):
        self.ins, self.out_shapes, self.inplace = list(ins), list(out_shapes), inplace
        self.start, self.mid, self.finish, self.sems = start, mid, finish, list(sems)

    def run(self, phase, ins, outs, sems):
        if phase == 0:
            self.start(ins, outs, sems)
        elif phase == 1:
            if self.mid is not None:
                self.mid(ins, outs, sems)
        else:
            self.finish(ins, outs, sems)


class Carried:
    def __init__(self, job, n_in, n_out, n_scratch, steps):
        self.job, self.n_in, self.n_out, self.n_scratch, self.steps = job, n_in, n_out, n_scratch, steps
        self.nji = len(job.ins) if job else 0
        self.njo = len(job.out_shapes) if job else 0

    def in_specs(self):
        return [ANY] * self.nji

    def out_specs(self):
        return [ANY] * self.njo

    def out_shapes(self):
        return list(self.job.out_shapes) if self.job else []

    def aliases(self):
        if not (self.job and self.job.inplace):
            return {}
        return {self.n_in + i: self.n_out + i for i in range(self.nji)}

    def sems(self):
        return list(self.job.sems) if self.job else []

    def args(self):
        return list(self.job.ins) if self.job else []

    def split(self, refs):
        a = self.n_in
        b = a + self.nji
        c = b + self.n_out
        d = c + self.njo
        e = d + self.n_scratch
        return refs[:a] + refs[b:c] + refs[d:e], (refs[a:b], refs[c:d], refs[e:])

    def phase(self, phase, step, jrefs):
        if self.job is None:
            return
        at = {0: 0, 1: max(self.steps - 2, 0), 2: self.steps - 1}[phase]

        @pl.when(step == at)
        def _():
            self.job.run(phase, *jrefs)


def _pcall(body, args, *, grid, in_specs, out_specs, out_shape, name, sem, scratch=(), aliases=None, job=None):
    n_in, n_out = len(args), len(out_shape)
    if job is None:
        return pl.pallas_call(
            body, grid=grid, in_specs=list(in_specs), out_specs=list(out_specs), out_shape=list(out_shape), name=name,
            scratch_shapes=list(scratch), input_output_aliases=dict(aliases or {}), compiler_params=_cp(*sem))(*args)
    total = int(np.prod(grid))
    car = Carried(job, n_in, n_out, len(scratch), total)

    def wrapped(*refs):
        main, jrefs = car.split(refs)
        lin = pl.program_id(0)
        for ax in range(1, len(grid)):
            lin = lin * grid[ax] + pl.program_id(ax)
        car.phase(0, lin, jrefs)
        body(*main)
        car.phase(1, lin, jrefs)
        car.phase(2, lin, jrefs)

    res = pl.pallas_call(
        wrapped, grid=grid, in_specs=list(in_specs) + car.in_specs(), out_specs=list(out_specs) + car.out_specs(),
        out_shape=list(out_shape) + car.out_shapes(), name=name, scratch_shapes=list(scratch) + car.sems(),
        input_output_aliases={**dict(aliases or {}), **car.aliases()},
        compiler_params=_cp(*(["arbitrary"] * len(grid))))(*args, *car.args())
    job.landed = list(res[n_out:])
    return list(res[:n_out])


def gather_job(placed, names):
    shapes = []
    for n in names:
        R, C = placed[n].shape
        shapes.append((R, C // 4) if n in COL_SHARDED else (R // 4, C))
    start, mid, finish, sems = _gather_jobs(names, shapes)
    arrays = [placed[n] for n in names]
    early = max(len(names) - 1, 1)

    def last(i, o, s):
        mid(o, s, early, len(names))
        finish(o, s)

    return CommJob(arrays, [SDS(a.shape, a.dtype) for a in arrays], True,
                   lambda i, o, s: start(o, s), last, sems, mid=lambda i, o, s: mid(o, s, 0, early))


def gather_rows_job(placed):
    widths = [a.shape[1] // 4 for a in placed]

    def copies(outs, sems):
        x, y, c, chips = _place()
        mine = 2 * x + y
        cps = []
        for w, ref in enumerate(outs):
            own = ref.at[:, pl.ds(mine * widths[w], widths[w])]
            for j, chip in enumerate(chips):
                theirs = ref.at[:, pl.ds((2 * chip[0] + chip[1]) * widths[w], widths[w])]
                kw = dict(send_sem=sems[0].at[w, j], recv_sem=sems[1].at[w, j], device_id=(*chip, c), device_id_type=MESH)
                cps.append((pltpu.make_async_remote_copy(src_ref=own, dst_ref=own, **kw),
                            pltpu.make_async_remote_copy(src_ref=theirs, dst_ref=theirs, **kw)))
        return cps

    def start(ins, outs, sems):
        for send, _ in copies(outs, sems):
            send.start()

    def finish(ins, outs, sems):
        for send, recv in copies(outs, sems):
            recv.wait_recv()
            send.wait_send()

    return CommJob(placed, [SDS(a.shape, a.dtype) for a in placed], True, start, finish,
                   [pltpu.SemaphoreType.DMA((len(placed), 3))] * 2)


def merge_jobs(a, b):
    assert a.inplace and b.inplace
    na, nb_, nsa = len(a.ins), len(b.ins), len(a.sems)

    def phase(which):
        def run(ins, outs, sems):
            for job, i, o, s in ((a, ins[:na], outs[:na], sems[:nsa]), (b, ins[na:], outs[na:], sems[nsa:])):
                fn = getattr(job, which)
                if fn is not None:
                    fn(i, o, s)
        return run

    merged = CommJob(a.ins + b.ins, a.out_shapes + b.out_shapes, True, phase("start"), phase("finish"), a.sems + b.sems,
                     mid=phase("mid"))
    merged.parts = (a, b, na)
    return merged


def split_landed(merged):
    a, b, na = merged.parts
    a.landed, b.landed = merged.landed[:na], merged.landed[na:]


def pair_exchange_job(grads, names):
    shapes = [grads[n].shape for n in names]

    def copies(ins, outs, sems):
        x, y, c, _ = _place()
        return [pltpu.make_async_remote_copy(
            src_ref=ins[w].at[:, pl.ds((1 - c) * (shapes[w][1] // 2), shapes[w][1] // 2), :], dst_ref=outs[w],
            send_sem=sems[0].at[w], recv_sem=sems[1].at[w], device_id=(x, y, 1 - c), device_id_type=MESH)
            for w in range(len(names))]

    def start(ins, outs, sems):
        for cp in copies(ins, outs, sems):
            cp.start()

    def finish(ins, outs, sems):
        for cp in copies(ins, outs, sems):
            cp.wait()

    return CommJob([grads[n] for n in names], [SDS((4, s[1] // 2, s[2]), F32) for s in shapes], False, start, finish,
                   [pltpu.SemaphoreType.DMA((len(names),))] * 2)


def chip_exchange_job(p16, names):
    shapes = [p16[n].shape for n in names]

    def copies(ins, outs, sems):
        x, y, c, chips = _place()
        return [pltpu.make_async_remote_copy(
            src_ref=ins[w].at[2 * chip[0] + chip[1]], dst_ref=outs[w].at[j],
            send_sem=sems[0].at[w, j], recv_sem=sems[1].at[w, j], device_id=(*chip, c), device_id_type=MESH)
            for w in range(len(names)) for j, chip in enumerate(chips)]

    def start(ins, outs, sems):
        for cp in copies(ins, outs, sems):
            cp.start()

    def finish(ins, outs, sems):
        for cp in copies(ins, outs, sems):
            cp.wait()

    return CommJob([p16[n] for n in names], [SDS((3,) + tuple(s[1:]), BF16) for s in shapes], False, start, finish,
                   [pltpu.SemaphoreType.DMA((len(names), 3))] * 2)


def half_exchange_job(full, names):
    shapes = [full[n].shape for n in names]

    def copies(outs, sems):
        x, y, c, _ = _place()
        cps = []
        for w in range(len(names)):
            Rh = shapes[w][0] // 2
            rows = outs[w].at[pl.ds(c * Rh, Rh), :]
            other = outs[w].at[pl.ds((1 - c) * Rh, Rh), :]
            cps.append((pltpu.make_async_remote_copy(src_ref=rows, dst_ref=rows, send_sem=sems[0].at[w], recv_sem=sems[1].at[w],
                                                     device_id=(x, y, 1 - c), device_id_type=MESH),
                        pltpu.make_async_remote_copy(src_ref=other, dst_ref=other, send_sem=sems[0].at[w], recv_sem=sems[1].at[w],
                                                     device_id=(x, y, 1 - c), device_id_type=MESH)))
        return cps

    def start(ins, outs, sems):
        for send, _ in copies(outs, sems):
            send.start()

    def finish(ins, outs, sems):
        for send, recv in copies(outs, sems):
            recv.wait_recv()
            send.wait_send()

    arrays = [full[n] for n in names]
    return CommJob(arrays, [SDS(a.shape, a.dtype) for a in arrays], True, start, finish,
                   [pltpu.SemaphoreType.DMA((len(names),))] * 2)


def allreduce_small(parts, name, job=None):
    n = len(parts)
    nji = len(job.ins) if job else 0
    njo = len(job.out_shapes) if job else 0

    def body(*refs):
        ins, jins = refs[:n], refs[n:n + nji]
        outs, jouts = refs[n + nji:2 * n + nji], refs[2 * n + nji:2 * n + nji + njo]
        bufs = refs[2 * n + nji + njo:3 * n + nji + njo]
        send, recv = refs[3 * n + nji + njo:3 * n + nji + njo + 2]
        jsems = refs[3 * n + nji + njo + 2:]
        x, y, c, _ = _place()
        me = 4 * x + 2 * y + c
        if job is not None:
            job.run(0, jins, jouts, jsems)
        sends = []
        for p in range(n):
            for k in range(1, 8):
                peer = (x ^ (k >> 2), y ^ ((k >> 1) & 1), c ^ (k & 1))
                cp = pltpu.make_async_remote_copy(src_ref=ins[p], dst_ref=bufs[p].at[k], send_sem=send.at[p, k - 1],
                                                  recv_sem=recv.at[p, k - 1], device_id=peer, device_id_type=MESH)
                cp.start()
                sends.append(cp)
            bufs[p][0] = ins[p][...]
        for cp in sends:
            cp.wait_recv()
        for p in range(n):
            acc = bufs[p][me]
            for d in range(1, 8):
                acc = acc + bufs[p][d ^ me]
            outs[p][...] = acc
        for cp in sends:
            cp.wait_send()
        if job is not None:
            job.run(1, jins, jouts, jsems)
            job.run(2, jins, jouts, jsems)

    vm = pl.BlockSpec(memory_space=pltpu.VMEM)
    res = pl.pallas_call(
        body, name=name, in_specs=[vm] * n + [ANY] * nji, out_specs=[vm] * n + [ANY] * njo,
        out_shape=[SDS(a.shape, F32) for a in parts] + (list(job.out_shapes) if job else []),
        input_output_aliases={n + i: n + i for i in range(nji)} if (job and job.inplace) else {},
        scratch_shapes=[pltpu.VMEM((8,) + tuple(a.shape), F32) for a in parts] + [pltpu.SemaphoreType.DMA((n, 7))] * 2
        + (list(job.sems) if job else []),
    )(*parts, *(job.ins if job else []))
    if job is not None:
        job.landed = list(res[n:])
    return list(res[:n])


def adamw_multi(ws, gs, ms, vs, name):
    n = len(ws)

    def body(*refs):
        for p in range(n):
            _adam_update(*(refs[q * n + p] for q in range(7)))

    vm = pl.BlockSpec(memory_space=pltpu.VMEM)
    outs = pl.pallas_call(body, name=name, in_specs=[vm] * (4 * n), out_specs=[vm] * (3 * n),
                          out_shape=[SDS(a.shape, F32) for a in ws] * 3)(*ws, *gs, *ms, *vs)
    return outs[:n], outs[n:2 * n], outs[2 * n:]


def pair_sum(gs, xhs, sel_arr, name):
    n = len(gs)
    _, R, C = gs[0].shape
    Rh = R // 2

    def body(sel_ref, *refs):
        for p in range(n):
            g_ref, x_ref, p32_ref, p16_ref = (refs[q * n + p] for q in range(4))
            s = g_ref[...] + x_ref[...]
            p16_ref[...] = s.astype(BF16)

            @pl.when(pl.program_id(0) == sel_ref[0])
            def _():
                p32_ref[...] = s

    same = pl.BlockSpec((None, Rh, C), lambda s, sel: (s, 0, 0))
    grid_spec = pltpu.PrefetchScalarGridSpec(
        num_scalar_prefetch=1, grid=(4,),
        in_specs=[pl.BlockSpec((None, Rh, C), lambda s, sel: (s, sel[1], 0))] * n + [same] * n,
        out_specs=[pl.BlockSpec((Rh, C), lambda s, sel: (0, 0))] * n + [same] * n)
    outs = pl.pallas_call(body, grid_spec=grid_spec, name=name,
                          out_shape=[SDS((Rh, C), F32)] * n + [SDS((4, Rh, C), BF16)] * n,
                          compiler_params=_cp("arbitrary"))(sel_arr, *gs, *xhs)
    return outs[:n], outs[n:]


def chip_sum(p32s, ys, sel_arr, name):
    n = len(p32s)
    Rh, C = p32s[0].shape

    def body(s_ref, *refs):
        for p in range(n):
            p_ref, y_ref, o_ref = (refs[q * n + p] for q in range(3))
            o_ref[...] = ((p_ref[...] + y_ref[0].astype(F32)) + y_ref[1].astype(F32)) + y_ref[2].astype(F32)

    grid_spec = pltpu.PrefetchScalarGridSpec(
        num_scalar_prefetch=1, grid=(1,),
        in_specs=[pl.BlockSpec((Rh, C), lambda i, s: (0, 0))] * n + [pl.BlockSpec((3, Rh, C), lambda i, s: (0, 0, 0))] * n,
        out_specs=[pl.BlockSpec((Rh, C), lambda i, s: (s[1], 0))] * n)
    return pl.pallas_call(body, grid_spec=grid_spec, name=name, out_shape=[SDS((2 * Rh, C), F32)] * n,
                          compiler_params=_cp("arbitrary"))(sel_arr, *p32s, *ys)


class StepComm:
    FIRST = ("w_in",)
    SQUARE = ("w_out", "w_xq", "w_xk", "w_xv", "w_xo")
    FFN = ("w_up", "w_down")
    PLAN = {
        "proj": ("gather", SQUARE[:3]), "attn_fwd": ("gather", SQUARE[3:] + FFN[:1]), "up_act_fwd": ("gather", FFN[1:]),
        "ffn_act_bwd": ("pair", FFN[1:]), "dw_up": ("chip", FFN[1:]),
        "d_h3": ("pair", FFN[:1]), "mixer_post_bwd_a": ("pair", SQUARE), "attn_bwd": ("chip", FFN[:1] + SQUARE),
        "dw_in": ("half", FFN + SQUARE), "bias_tables_bwd": ("pair", FIRST), "d_h1": ("chip", FIRST),
    }

    def __init__(self, placed, conv_placed, sel_arr):
        self.placed, self.conv_placed, self.sel_arr = placed, conv_placed, sel_arr
        self.g, self.p32, self.p16, self.full, self.done, self.jobs = {}, {}, {}, {}, {}, {}

    def job(self, tag, gb=None):
        if tag == "norm_mix":
            self.jobs[tag] = merge_jobs(gather_job(self.placed, self.FIRST), gather_rows_job(self.conv_placed))
            return self.jobs[tag]
        kind, names = self.PLAN[tag]
        if kind == "gather":
            j = gather_job(self.placed, names)
        elif kind == "pair":
            for n in names:
                self.g[n] = gb[n] if gb[n].ndim == 3 else gb[n].reshape(4, gb[n].shape[0] // 4, gb[n].shape[1])
            j = pair_exchange_job(self.g, names)
        elif kind == "chip":
            j = chip_exchange_job(self.p16, names)
        else:
            j = half_exchange_job(self.full, names)
        self.jobs[tag] = j
        return j

    def landed(self, tag, wb=None, ws=None):
        if tag == "norm_mix":
            split_landed(self.jobs[tag])
            first, conv, _ = self.jobs[tag].parts
            wb.update(zip(self.FIRST, first.landed))
            ws.update((n, a[:3]) for n, a in zip(SMALL_SHARDED, conv.landed))
            return
        kind, names = self.PLAN[tag]
        got = dict(zip(names, self.jobs[tag].landed))
        if kind == "gather":
            wb.update(got)
        elif kind == "half":
            self.done.update(got)
        for group in same_shape_groups(names, got):
            if kind == "pair":
                p32s, p16s = pair_sum([self.g[n] for n in group], [got[n] for n in group], self.sel_arr, "pair_sum_" + group[0])
                self.p32.update(zip(group, p32s))
                self.p16.update(zip(group, p16s))
            elif kind == "chip":
                fulls = chip_sum([self.p32[n] for n in group], [got[n] for n in group], self.sel_arr, "chip_sum_" + group[0])
                self.full.update(zip(group, fulls))

    def finish(self, gb):
        return self.done

    def last_job(self):
        self.jobs["end_half"] = half_exchange_job(self.full, self.FIRST)
        return self.jobs["end_half"]

    def take_last(self):
        self.done.update(zip(self.FIRST, self.jobs["end_half"].landed))


class NoComm:
    def job(self, tag, gb=None):
        return None

    def landed(self, tag, wb=None, ws=None):
        pass

    def finish(self, gb):
        return gb


SMALL = ("rel_bias", "g_mix", "w_short_conv", "g_attn_out", "g_conv_out", "g_xattn", "g_mem", "g_ffn",
         "w_ffn_conv", "b_ffn_conv", "g_final")
SMALL_SHARDED = ("w_short_conv", "w_ffn_conv")


def kernel(x, mem, rel_bias, g_mix, w_in, w_short_conv, g_attn_out, g_conv_out, w_out, g_xattn, g_mem, w_xq, w_xk, w_xv, w_xo, g_ffn, w_up, w_ffn_conv, b_ffn_conv, w_down, g_final, loss_target, m_rel_bias, m_g_mix, m_w_in, m_w_short_conv, m_g_attn_out, m_g_conv_out, m_w_out, m_g_xattn, m_g_mem, m_w_xq, m_w_xk, m_w_xv, m_w_xo, m_g_ffn, m_w_up, m_w_ffn_conv, m_b_ffn_conv, m_w_down, m_g_final, v_rel_bias, v_g_mix, v_w_in, v_w_short_conv, v_g_attn_out, v_g_conv_out, v_w_out, v_g_xattn, v_g_mem, v_w_xq, v_w_xk, v_w_xv, v_w_xo, v_g_ffn, v_w_up, v_w_ffn_conv, v_b_ffn_conv, v_w_down, v_g_final):
    names = ("rel_bias", "g_mix", "w_in", "w_short_conv", "g_attn_out", "g_conv_out", "w_out", "g_xattn", "g_mem", "w_xq",
             "w_xk", "w_xv", "w_xo", "g_ffn", "w_up", "w_ffn_conv", "b_ffn_conv", "w_down", "g_final")
    W = dict(zip(names, (rel_bias, g_mix, w_in, w_short_conv, g_attn_out, g_conv_out, w_out, g_xattn, g_mem, w_xq, w_xk, w_xv,
                         w_xo, g_ffn, w_up, w_ffn_conv, b_ffn_conv, w_down, g_final)))
    M = dict(zip(names, (m_rel_bias, m_g_mix, m_w_in, m_w_short_conv, m_g_attn_out, m_g_conv_out, m_w_out, m_g_xattn, m_g_mem,
                         m_w_xq, m_w_xk, m_w_xv, m_w_xo, m_g_ffn, m_w_up, m_w_ffn_conv, m_b_ffn_conv, m_w_down, m_g_final)))
    V = dict(zip(names, (v_rel_bias, v_g_mix, v_w_in, v_w_short_conv, v_g_attn_out, v_g_conv_out, v_w_out, v_g_xattn, v_g_mem,
                         v_w_xq, v_w_xk, v_w_xv, v_w_xo, v_g_ffn, v_w_up, v_w_ffn_conv, v_b_ffn_conv, v_w_down, v_g_final)))
    xi, yi, ci = lax.axis_index("x"), lax.axis_index("y"), lax.axis_index("c")
    mine = 2 * xi + yi
    mine_arr = jnp.reshape(mine, (1,)).astype(jnp.int32)
    sel_arr = jnp.stack([mine, ci]).astype(jnp.int32)
    conv_placed = []
    for n in SMALL_SHARDED:
        shard = W[n][0]
        conv_placed.append(lax.dynamic_update_slice(jnp.zeros((8, 4 * shard.shape[1]), F32), shard, (0, mine * shard.shape[1])))
    shards = {n: W[n][0] for n in BIG}
    placed = {}
    for group in same_shape_groups(BIG, shards):
        outs = place_shards([shards[n] for n in group], mine_arr, group[0] in COL_SHARDED, "place_" + group[0])
        placed.update(zip(group, outs))
    comm = StepComm(placed, conv_placed, sel_arr)
    ws = {n: W[n] for n in SMALL if n not in SMALL_SHARDED}

    loss, grad_x, gfull, gs = local_step(x[0], mem[0], loss_target[0], None, ws, comm)

    def as2d(a):
        return a.reshape(1, -1) if a.ndim == 1 else a

    reduced = allreduce_small([as2d(gs[n]) for n in SMALL] + [loss], "reduce_small", job=comm.last_job())
    comm.take_last()
    gsm = dict(zip(SMALL, reduced[:-1]))
    loss = reduced[-1][0, 0]

    grads, delta, new_m, new_v = {}, {}, {}, {}
    for group in same_shape_groups(BIG, gfull):
        ds, nms, nvs = adamw([W[n][0] for n in group], [gfull[n] for n in group], [M[n][0] for n in group],
                             [V[n][0] for n in group], "adamw_" + group[0])
        for n, d, nm, nv in zip(group, ds, nms, nvs):
            grads[n], delta[n], new_m[n], new_v[n] = gfull[n][None], d[None], nm[None], nv[None]
    for n in SMALL_SHARDED:
        wid = W[n].shape[2]
        gsm[n] = lax.dynamic_slice(gsm[n], (0, mine * wid), (3, wid))

    def own(a, n):
        return a[0] if n in SMALL_SHARDED else as2d(a)

    d, nm, nv = adamw_multi([own(W[n], n) for n in SMALL], [gsm[n] for n in SMALL], [own(M[n], n) for n in SMALL],
                            [own(V[n], n) for n in SMALL], "adamw_small")
    for i, n in enumerate(SMALL):
        shape = W[n].shape
        grads[n], delta[n], new_m[n], new_v[n] = (a.reshape(shape) for a in (gsm[n], d[i], nm[i], nv[i]))
    return (loss, grad_x[None], *[grads[n] for n in names], *[delta[n] for n in names],
            *[new_m[n] for n in names], *[new_v[n] for n in names])
```

```python
import math

import numpy as np
import jax
import jax.numpy as jnp
from jax import lax
from jax.experimental import pallas as pl
from jax.experimental.pallas import tpu as pltpu

F32 = jnp.float32
BF16 = jnp.bfloat16
SDS = jax.ShapeDtypeStruct
MESH = pl.DeviceIdType.MESH

D_MODEL = 1024
ATTN_W = 512
N_HEADS = 8
HEAD_DIM = 64
HALF_HEAD = HEAD_DIM // 2
Q_SCALE = HEAD_DIM ** -0.5
BLK = 128
PATTERNS = ((128, 1), (512, 4), (2048, 16))
N_BUCKETS = 32
BUCKET_MAX_DIST = 2048
D_FF = 2816
N_MEM_HEADS = 4
MEM_HD = 256
EPS = 1e-6
NEG = -1e30
VMEM_LIMIT = 56 * 1024 * 1024

ADAM_LR, ADAM_B1, ADAM_B2, ADAM_EPS, ADAM_WD, ADAM_STEP = 0.001, 0.9, 0.999, 1e-08, 0.01, 10


def _cp(*sem):
    return pltpu.CompilerParams(dimension_semantics=sem, vmem_limit_bytes=VMEM_LIMIT)


def _dot(a, b):
    return jnp.dot(a, b, preferred_element_type=F32)


def _dot_nt(a, b):
    return lax.dot_general(a, b, (((1,), (1,)), ((), ())), preferred_element_type=F32)


def _dot_tn(a, b):
    return lax.dot_general(a, b, (((0,), (0,)), ((), ())), preferred_element_type=F32)


def _rsum(x):
    return jnp.sum(x, axis=1, keepdims=True)


def rmsnorm_fwd(x, g, name, job=None):
    S, Dm = x.shape
    tm = min(S, 512)

    def body(x_ref, g_ref, o_ref):
        xv = x_ref[...]
        r = lax.rsqrt(jnp.mean(xv * xv, axis=1, keepdims=True) + EPS)
        o_ref[...] = (xv * r * g_ref[...]).astype(o_ref.dtype)

    return _pcall(body, [x, g], grid=(S // tm,), name=name,
                  in_specs=[pl.BlockSpec((tm, Dm), lambda i: (i, 0)), pl.BlockSpec((1, Dm), lambda i: (0, 0))],
                  out_specs=[pl.BlockSpec((tm, Dm), lambda i: (i, 0))], out_shape=[SDS((S, Dm), BF16)],
                  sem=("parallel",), job=job)[0]


def rmsnorm_gain_grad(x, dh, name):
    S, Dm = x.shape
    tm = min(S, 512)

    def body(x_ref, dh_ref, dg_ref):
        xv = x_ref[...]
        xh = xv * lax.rsqrt(jnp.mean(xv * xv, axis=1, keepdims=True) + EPS)

        @pl.when(pl.program_id(0) == 0)
        def _():
            dg_ref[...] = jnp.zeros_like(dg_ref)

        dg_ref[...] += jnp.sum(dh_ref[...].astype(F32) * xh, axis=0, keepdims=True)

    row = pl.BlockSpec((tm, Dm), lambda i: (i, 0))
    return pl.pallas_call(
        body, grid=(S // tm,), name=name, in_specs=[row, row], out_specs=pl.BlockSpec((1, Dm), lambda i: (0, 0)),
        out_shape=SDS((1, Dm), F32), compiler_params=_cp("arbitrary"))(x, dh)


def _col_spec(bw, tn, rows, row_of, col_of):
    if bw is None:
        return pl.BlockSpec((rows, tn), lambda *g: (row_of(*g), col_of(*g)))
    if tn % bw == 0:
        return pl.BlockSpec((tn // bw, rows, bw), lambda *g: (col_of(*g), row_of(*g), 0))
    per = bw // tn
    return pl.BlockSpec((None, rows, tn), lambda *g: (col_of(*g) // per, row_of(*g), col_of(*g) % per))


def _read_cols(ref, bw, tn):
    if bw is None or tn % bw != 0:
        return ref[...]
    if tn == bw:
        return ref[0]
    return jnp.concatenate([ref[q] for q in range(tn // bw)], axis=1)


def _write_cols(ref, bw, tn, val):
    if bw is None or tn % bw != 0:
        ref[...] = val.astype(ref.dtype)
    else:
        for q in range(tn // bw):
            ref[q] = val[:, q * bw:(q + 1) * bw].astype(ref.dtype)


def mm_nn(a, b, name, *, out_dtype=F32, out_bw=None, tm=1024, tn=512, job=None):
    M, K = a.shape
    N = b.shape[1]
    tm = min(tm, M)

    def body(a_ref, b_ref, o_ref):
        _write_cols(o_ref, out_bw, tn, _dot(a_ref[...].astype(BF16), b_ref[...]))

    ri, ci = (lambda i, j: i), (lambda i, j: j)
    in_specs = [pl.BlockSpec((tm, K), lambda i, j: (i, 0)), pl.BlockSpec((K, tn), lambda i, j: (0, j))]
    oshape = (M, N) if out_bw is None else (N // out_bw, M, out_bw)
    return _pcall(body, [a, b], grid=(M // tm, N // tn), name=name, in_specs=in_specs,
                  out_specs=[_col_spec(out_bw, tn, tm, ri, ci)], out_shape=[SDS(oshape, out_dtype)],
                  sem=("parallel", "parallel"), job=job)[0]


def mm_nn_rows(a, b, res, g, name, *, target=None, tm=512):
    M, K = a.shape
    N = b.shape[1]

    def body(*refs):
        a_ref, b_ref, r_ref, g_ref = refs[:4]
        xv = r_ref[...] + _dot(a_ref[...].astype(BF16), b_ref[...])
        r = lax.rsqrt(jnp.mean(xv * xv, axis=1, keepdims=True) + EPS)
        xh = xv * r
        if target is None:
            x_ref, h_ref = refs[4:]
            x_ref[...] = xv
            h_ref[...] = (xh * g_ref[...]).astype(BF16)
            return
        t_ref, loss_ref, dx_ref, dg_ref = refs[4:]
        err = xh * g_ref[...] - t_ref[...]
        dy = err / N
        gd = dy * g_ref[...]
        dx_ref[...] = r * (gd - xh * jnp.mean(gd * xh, axis=1, keepdims=True))

        @pl.when(pl.program_id(0) == 0)
        def _():
            dg_ref[...] = jnp.zeros_like(dg_ref)
            loss_ref[...] = jnp.zeros_like(loss_ref)

        dg_ref[...] += _colsum(dy * xh)
        loss_ref[...] += 0.5 * _colsum(jnp.mean(err * err, axis=1, keepdims=True))

    row = pl.BlockSpec((tm, N), lambda i: (i, 0))
    vec = pl.BlockSpec((1, N), lambda i: (0, 0))
    in_specs = [pl.BlockSpec((tm, K), lambda i: (i, 0)), pl.BlockSpec((K, N), lambda i: (0, 0)), row, vec]
    if target is None:
        return pl.pallas_call(body, grid=(M // tm,), name=name, in_specs=in_specs, out_specs=[row, row],
                              out_shape=[SDS((M, N), F32), SDS((M, N), BF16)], compiler_params=_cp("parallel"))(a, b, res, g)
    one = pl.BlockSpec((1, 1), lambda i: (0, 0))
    return pl.pallas_call(body, grid=(M // tm,), name=name, in_specs=in_specs + [row], out_specs=[one, row, vec],
                          out_shape=[SDS((1, 1), F32), SDS((M, N), F32), SDS((1, N), F32)],
                          compiler_params=_cp("arbitrary"))(a, b, res, g, target)


def mm_nt(a, a_bw, b, name, *, res=None, out_dtype=F32, tm=1024, tn=512, job=None):
    if a_bw is None:
        M, K = a.shape
    else:
        M, K = a.shape[1], a.shape[0] * a_bw
    N = b.shape[0]
    tm = min(tm, M)

    def body(*refs):
        if res is None:
            a_ref, b_ref, o_ref = refs
        else:
            a_ref, b_ref, r_ref, o_ref = refs
        av = _read_cols(a_ref, a_bw, K).astype(BF16)
        acc = _dot_nt(av, b_ref[...])
        if res is not None:
            acc = acc + r_ref[...]
        o_ref[...] = acc.astype(o_ref.dtype)

    in_specs = [_col_spec(a_bw, K, tm, lambda i, j: i, lambda i, j: 0), pl.BlockSpec((tn, K), lambda i, j: (j, 0))]
    args = [a, b]
    if res is not None:
        in_specs.append(pl.BlockSpec((tm, tn), lambda i, j: (i, j)))
        args.append(res)
    return _pcall(body, args, grid=(M // tm, N // tn), name=name, in_specs=in_specs,
                  out_specs=[pl.BlockSpec((tm, tn), lambda i, j: (i, j))], out_shape=[SDS((M, N), out_dtype)],
                  sem=("parallel", "parallel"), job=job)[0]


def mm_nt_norm_bwd(a, a_bw, b, x, g, res, name, *, tm=512, job=None):
    if a_bw is None:
        M, K = a.shape
    else:
        M, K = a.shape[1], a.shape[0] * a_bw
    N = b.shape[0]

    def body(a_ref, b_ref, x_ref, g_ref, r_ref, dx_ref, dg_ref):
        dh = _dot_nt(_read_cols(a_ref, a_bw, K).astype(BF16), b_ref[...])
        xv = x_ref[...]
        r = lax.rsqrt(jnp.mean(xv * xv, axis=1, keepdims=True) + EPS)
        xh = xv * r
        gd = dh * g_ref[...]
        dx_ref[...] = r_ref[...] + r * (gd - xh * jnp.mean(gd * xh, axis=1, keepdims=True))

        @pl.when(pl.program_id(0) == 0)
        def _():
            dg_ref[...] = jnp.zeros_like(dg_ref)

        dg_ref[...] += jnp.sum(dh * xh, axis=0, keepdims=True)

    row = pl.BlockSpec((tm, N), lambda i: (i, 0))
    vec = pl.BlockSpec((1, N), lambda i: (0, 0))
    return _pcall(body, [a, b, x, g, res], grid=(M // tm,), name=name,
                  in_specs=[_col_spec(a_bw, K, tm, lambda i: i, lambda i: 0),
                            pl.BlockSpec((N, K), lambda i: (0, 0), pipeline_mode=pl.Buffered(1)), row, vec, row],
                  out_specs=[row, vec], out_shape=[SDS((M, N), F32), SDS((1, N), F32)], sem=("arbitrary",), job=job)


def mm_tn(a, b, b_bw, name, *, shards=None, tm=1024, tn=1024, ts=2048, job=None):
    S, Ka = a.shape
    N = b.shape[1] if b_bw is None else b.shape[0] * b_bw
    ts = min(ts, S)
    tm = min(tm, Ka)

    def body(a_ref, b_ref, o_ref):
        @pl.when(pl.program_id(2) == 0)
        def _():
            o_ref[...] = jnp.zeros_like(o_ref)

        bv = _read_cols(b_ref, b_bw, tn).astype(BF16)
        o_ref[...] += _dot_tn(a_ref[...].astype(BF16), bv)

    in_specs = [pl.BlockSpec((ts, tm), lambda i, j, k: (k, i)),
                _col_spec(b_bw, tn, ts, lambda i, j, k: k, lambda i, j, k: j)]
    if shards is None:
        out_spec, oshape = pl.BlockSpec((tm, tn), lambda i, j, k: (i, j)), (Ka, N)
    else:
        per = (N // shards) // tn
        out_spec = pl.BlockSpec((None, tm, tn), lambda i, j, k: (j // per, i, j % per))
        oshape = (shards, Ka, N // shards)
    return _pcall(body, [a, b], grid=(Ka // tm, N // tn, S // ts), name=name, in_specs=in_specs, out_specs=[out_spec],
                  out_shape=[SDS(oshape, F32)], sem=("parallel", "parallel", "arbitrary"), job=job)[0]


KEYS = 2 * BLK


def _bucket_tables():
    out = np.zeros((3, BLK, KEYS), np.int32)
    qi = np.arange(BLK)[:, None]
    kj = np.arange(KEYS)[None, :]
    for p, (win, dil) in enumerate(PATTERNS):
        w = win // dil
        assert w == BLK
        steps = qi + w - kj
        valid = (steps >= 0) & (steps <= w)
        dist = np.clip(steps, 0, w) * dil
        dd = np.maximum(dist, 1).astype(np.float32)
        exact = N_BUCKETS // 2
        large = exact + (np.log(dd / np.float32(exact)) / np.float32(math.log(BUCKET_MAX_DIST / exact))
                         * np.float32(N_BUCKETS - exact)).astype(np.int32)
        large = np.minimum(large, N_BUCKETS - 1)
        out[p] = np.where(valid, np.where(dist < exact, dist, large), -1)
    return out


def bias_tables(rel_bias):
    bkt = jnp.asarray(_bucket_tables())

    def body(rb_ref, bkt_ref, o_ref):
        bk = bkt_ref[...]
        for h in range(N_HEADS):
            acc = jnp.full((BLK, KEYS), NEG, F32)
            for b in range(N_BUCKETS):
                acc = jnp.where(bk == b, rb_ref[h, b], acc)
            o_ref[h] = acc

    return pl.pallas_call(
        body, grid=(3,), name="bias_tables",
        in_specs=[pl.BlockSpec(memory_space=pltpu.SMEM),
                  pl.BlockSpec((None, BLK, KEYS), lambda p: (p, 0, 0))],
        out_specs=pl.BlockSpec((None, N_HEADS, BLK, KEYS), lambda p: (p, 0, 0, 0)),
        out_shape=SDS((3, N_HEADS, BLK, KEYS), F32), compiler_params=_cp("parallel"))(rel_bias, bkt)


def bias_tables_bwd(ds_sum, job=None):
    bkt = jnp.asarray(_bucket_tables())

    def body(ds_ref, bkt_ref, o_ref):
        h = pl.program_id(0)

        @pl.when(h == 0)
        def _():
            o_ref[...] = jnp.zeros_like(o_ref)

        rows = lax.broadcasted_iota(jnp.int32, (N_HEADS, N_BUCKETS), 0)
        cols = lax.broadcasted_iota(jnp.int32, (N_HEADS, N_BUCKETS), 1)
        acc = jnp.zeros((N_HEADS, N_BUCKETS), F32)
        for b in range(N_BUCKETS):
            t = jnp.zeros((BLK, KEYS), F32)
            for p in range(3):
                t = t + jnp.where(bkt_ref[p] == b, ds_ref[p], 0.0)
            tot = jnp.sum(_rsum(t), axis=0, keepdims=True)
            acc = acc + jnp.where((rows == h) & (cols == b), tot, 0.0)
        o_ref[...] += acc

    return _pcall(
        body, [ds_sum, bkt], grid=(N_HEADS,), name="bias_tables_bwd",
        in_specs=[pl.BlockSpec((3, None, BLK, KEYS), lambda h: (0, h, 0, 0)),
                  pl.BlockSpec((3, BLK, KEYS), lambda h: (0, 0, 0))],
        out_specs=[pl.BlockSpec((N_HEADS, N_BUCKETS), lambda h: (0, 0))],
        out_shape=[SDS((N_HEADS, N_BUCKETS), F32)], sem=("arbitrary",), job=job)[0]


def _rows(start, dil):
    if dil == 1:
        return pl.ds(pl.multiple_of(start, BLK), BLK)
    return pl.ds(start, BLK, stride=dil)


GRP = 8


def _group_rows(i, dil, S):
    nb = S // (BLK * dil)
    run = min(nb, GRP)
    chunks = nb // run
    res0, b0 = (i // chunks) * (GRP // run), (i % chunks) * run
    cur = [(b0 + j % run) * (BLK * dil) + res0 + j // run for j in range(GRP)]
    t = lax.broadcasted_iota(jnp.int32, (GRP, 1, 1), 0)
    if run == nb:
        before, pen = None, jnp.where(t % run == 0, NEG, 0.0)
    else:
        before = _rows(jnp.maximum(b0 - 1, 0) * (BLK * dil) + res0, dil)
        pen = jnp.where((t == 0) & (b0 == 0), NEG, 0.0)
    return [_rows(s, dil) for s in cur], before, pen


def _load_group(ref, rows):
    return jnp.stack([ref[r, :] for r in rows])


def _with_prev(ref, before, cur_blocks):
    first = cur_blocks[:1] if before is None else ref[before, :][None].astype(cur_blocks.dtype)
    return jnp.concatenate([first, cur_blocks[:-1]], axis=0)


def _bdot_nt(a, b):
    return lax.dot_general(a, b, (((2,), (2,)), ((0,), (0,))), preferred_element_type=F32)


def _bdot(a, b):
    return lax.dot_general(a, b, (((2,), (1,)), ((0,), (0,))), preferred_element_type=F32)


def _bdot_tn(a, b):
    return lax.dot_general(a, b, (((1,), (1,)), ((0,), (0,))), preferred_element_type=F32)


def _lsum(x):
    return jnp.sum(x, axis=-1, keepdims=True)


def _widen(src, dst):
    S = src.shape[1]

    def chunk(i, carry):
        rows = pl.ds(pl.multiple_of(i * 512, 512), 512)
        for a in range(3):
            dst[a, rows, :] = src[a, rows, :].astype(F32)
        return carry

    lax.fori_loop(0, S // 512, chunk, 0)


def attn_fwd(qkv, bias, job=None):
    S = qkv.shape[3]
    nblk = S // BLK
    car = Carried(job, 2, 2, 3, 4)

    def body(*refs):
        (qkv_in, bias_ref, o_ref, lse_ref, qkv_ref, part_o, part_ml), jrefs = car.split(refs)
        car.phase(0, pl.program_id(0), jrefs)
        _widen(qkv_in, qkv_ref)
        lane = lax.broadcasted_iota(jnp.int32, (GRP, BLK, BLK), 2)
        lo = lane < HEAD_DIM
        masks = (lo, jnp.logical_not(lo))
        lov = lax.broadcasted_iota(jnp.int32, (GRP, KEYS, BLK), 2) < HEAD_DIM
        vmasks = (lov, jnp.logical_not(lov))
        prev_keys = lax.broadcasted_iota(jnp.int32, (1, 1, KEYS), 2) < BLK
        q_ref, k_ref, v_ref = qkv_ref.at[0], qkv_ref.at[1], qkv_ref.at[2]
        targets = [(o_ref, lse_ref)] + [(part_o.at[t], part_ml.at[t]) for t in range(len(PATTERNS) - 1)]
        for p, (_, dil) in enumerate(PATTERNS):
            def step(i, carry, p=p, dil=dil):
                rc, before, pen = _group_rows(i, dil, S)
                q2 = _load_group(q_ref, rc) * Q_SCALE
                kc = _load_group(k_ref, rc).astype(BF16)
                keys = jnp.concatenate([_with_prev(k_ref, before, kc), kc], axis=1)
                vc = _load_group(v_ref, rc)
                vals = jnp.concatenate([_with_prev(v_ref, before, vc), vc], axis=1)
                pen = jnp.where(prev_keys, pen, 0.0)
                pv = jnp.zeros((GRP, BLK, BLK), F32)
                ms, ls = [], []
                for h in range(2):
                    qh = jnp.where(masks[h], q2, 0.0).astype(BF16)
                    s = _bdot_nt(qh, keys) + bias_ref[p, h][None] + pen
                    mn = jnp.max(s, axis=-1, keepdims=True)
                    pr = jnp.exp(s - mn)
                    pv = pv + _bdot(pr.astype(BF16), jnp.where(vmasks[h], vals, 0.0).astype(BF16))
                    ms.append(mn)
                    ls.append(_lsum(pr))
                ml_t = jnp.where(lo, jnp.where(lane < HALF_HEAD, ms[0], ls[0]),
                                 jnp.where(lane < HEAD_DIM + HALF_HEAD, ms[1], ls[1]))
                acc_ref, ml_ref = targets[p]
                for j, r in enumerate(rc):
                    acc_ref[r, :] = pv[j]
                    ml_ref[r, :] = ml_t[j]
                return carry

            lax.fori_loop(0, nblk // GRP, step, 0)

        def merge(i, carry):
            rows = pl.ds(pl.multiple_of(i * 512, 512), 512)
            is_m = (lax.broadcasted_iota(jnp.int32, (512, BLK), 1) & HALF_HEAD) == 0
            accs = [ref[rows, :] for ref, _ in targets]
            mls = [ref[rows, :] for _, ref in targets]
            mx = [jnp.where(is_m, ml, pltpu.roll(ml, HALF_HEAD, 1)) for ml in mls]
            dn = [jnp.where(is_m, pltpu.roll(ml, BLK - HALF_HEAD, 1), ml) for ml in mls]
            top = jnp.maximum(jnp.maximum(mx[0], mx[1]), mx[2])
            ws_ = [jnp.exp(m - top) for m in mx]
            den = ws_[0] * dn[0] + ws_[1] * dn[1] + ws_[2] * dn[2]
            o_ref[rows, :] = (ws_[0] * accs[0] + ws_[1] * accs[1] + ws_[2] * accs[2]) / den
            lse_ref[rows, :] = top + jnp.log(den)
            return carry

        lax.fori_loop(0, S // 512, merge, 0)
        car.phase(1, pl.program_id(0), jrefs)
        car.phase(2, pl.program_id(0), jrefs)

    outs = pl.pallas_call(
        body, grid=(4,), name="attn_fwd",
        in_specs=[pl.BlockSpec((None, 3, None, S, BLK), lambda g: (0, 0, g, 0, 0)),
                  pl.BlockSpec((3, 2, BLK, KEYS), lambda g: (0, g, 0, 0))] + car.in_specs(),
        out_specs=[pl.BlockSpec((None, S, BLK), lambda g: (g, 0, 0)),
                   pl.BlockSpec((None, S, BLK), lambda g: (g, 0, 0))] + car.out_specs(),
        out_shape=[SDS((4, S, BLK), F32), SDS((4, S, BLK), F32)] + car.out_shapes(),
        input_output_aliases=car.aliases(),
        scratch_shapes=[pltpu.VMEM((3, S, BLK), F32), pltpu.VMEM((2, S, BLK), F32), pltpu.VMEM((2, S, BLK), F32)] + car.sems(),
        compiler_params=_cp("arbitrary"))(qkv, bias, *car.args())
    if job is not None:
        job.landed = list(outs[2:])
    return outs[0], outs[1]


def attn_bwd(qkv, bias, o, lse, do, job=None):
    S = qkv.shape[3]
    nblk = S // BLK
    car = Carried(job, 5, 2, 4, 4)

    def body(*refs):
        (qkv_in, bias_ref, o_ref, lse_ref, do_ref, dqkv_out, ds_ref, qkv_ref, dqkv_ref, last_ref, st_ref), jrefs = car.split(refs)
        car.phase(0, pl.program_id(0), jrefs)
        _widen(qkv_in, qkv_ref)

        def stats(i, carry):
            rows = pl.ds(pl.multiple_of(i * 512, 512), 512)
            ln = lax.broadcasted_iota(jnp.int32, (512, BLK), 1)
            prod = do_ref[rows, :] * o_ref[rows, :]
            d0 = _rsum(jnp.where(ln < HEAD_DIM, prod, 0.0))
            d1 = _rsum(jnp.where(ln < HEAD_DIM, 0.0, prod))
            st_ref[rows, :] = jnp.where((ln & HALF_HEAD) == 0, lse_ref[rows, :], jnp.where(ln < HEAD_DIM, d0, d1))
            return carry

        lax.fori_loop(0, S // 512, stats, 0)
        lane = lax.broadcasted_iota(jnp.int32, (GRP, BLK, BLK), 2)
        lo = lane < HEAD_DIM
        masks = (lo, jnp.logical_not(lo))
        lov = lax.broadcasted_iota(jnp.int32, (GRP, KEYS, BLK), 2) < HEAD_DIM
        vmasks = (lov, jnp.logical_not(lov))
        prev_keys = lax.broadcasted_iota(jnp.int32, (1, 1, KEYS), 2) < BLK
        ds_ref[...] = jnp.zeros_like(ds_ref)
        q_ref, k_ref, v_ref = qkv_ref.at[0], qkv_ref.at[1], qkv_ref.at[2]
        last = len(PATTERNS) - 1
        for p, (_, dil) in enumerate(PATTERNS):
            def step(i, carry, p=p, dil=dil):
                tgt = last_ref if p == last else dqkv_ref
                dq_ref, dk_ref, dv_ref = tgt.at[0], tgt.at[1], tgt.at[2]
                rc, before, pen = _group_rows(i, dil, S)
                q2 = _load_group(q_ref, rc) * Q_SCALE
                kc = _load_group(k_ref, rc).astype(BF16)
                keys = jnp.concatenate([_with_prev(k_ref, before, kc), kc], axis=1)
                vc = _load_group(v_ref, rc).astype(BF16)
                vals = jnp.concatenate([_with_prev(v_ref, before, vc), vc], axis=1)
                pen = jnp.where(prev_keys, pen, 0.0)
                dot = _load_group(do_ref, rc)
                st = _load_group(st_ref, rc)
                dq = jnp.zeros((GRP, BLK, BLK), F32)
                dkeys = jnp.zeros((GRP, KEYS, BLK), F32)
                dvals = jnp.zeros((GRP, KEYS, BLK), F32)
                for h in range(2):
                    qh = jnp.where(masks[h], q2, 0.0).astype(BF16)
                    doh = jnp.where(masks[h], dot, 0.0).astype(BF16)
                    lh = st[:, :, HEAD_DIM * h:HEAD_DIM * h + 1]
                    delta = st[:, :, HEAD_DIM * h + HALF_HEAD:HEAD_DIM * h + HALF_HEAD + 1]
                    pr = jnp.exp(_bdot_nt(qh, keys) + bias_ref[p, h][None] + pen - lh)
                    ds = pr * (_bdot_nt(doh, vals) - delta)
                    ds_ref[p, h] += jnp.sum(ds, axis=0)
                    dsb = ds.astype(BF16)
                    dq = dq + jnp.where(masks[h], _bdot(dsb, keys), 0.0)
                    dkeys = dkeys + _bdot_tn(dsb, qh)
                    dvals = dvals + _bdot_tn(pr.astype(BF16), doh)
                dkp, dkc, dvp, dvc = dkeys[:, :BLK], dkeys[:, BLK:], dvals[:, :BLK], dvals[:, BLK:]
                none = jnp.zeros((1, BLK, BLK), F32)
                dkc = dkc + jnp.concatenate([dkp[1:], none], axis=0)
                dvc = dvc + jnp.concatenate([dvp[1:], none], axis=0)
                for j in range(GRP):
                    if p == 1:
                        dq_ref[rc[j], :] += dq[j] * Q_SCALE
                        dk_ref[rc[j], :] += dkc[j]
                        dv_ref[rc[j], :] += dvc[j]
                    else:
                        dq_ref[rc[j], :] = dq[j] * Q_SCALE
                        dk_ref[rc[j], :] = dkc[j]
                        dv_ref[rc[j], :] = dvc[j]
                if before is not None:
                    dk_ref[before, :] += dkp[0]
                    dv_ref[before, :] += dvp[0]
                return carry

            lax.fori_loop(0, nblk // GRP, step, 0)

        def narrow(i, carry):
            rows = pl.ds(pl.multiple_of(i * 512, 512), 512)
            for a in range(3):
                dqkv_out[a, rows, :] = (dqkv_ref[a, rows, :] + last_ref[a, rows, :]).astype(BF16)
            return carry

        lax.fori_loop(0, S // 512, narrow, 0)
        car.phase(1, pl.program_id(0), jrefs)
        car.phase(2, pl.program_id(0), jrefs)

    blk = pl.BlockSpec((None, S, BLK), lambda g: (g, 0, 0))
    outs = pl.pallas_call(
        body, grid=(4,), name="attn_bwd",
        in_specs=[pl.BlockSpec((None, 3, None, S, BLK), lambda g: (0, 0, g, 0, 0)),
                  pl.BlockSpec((3, 2, BLK, KEYS), lambda g: (0, g, 0, 0)), blk, blk, blk] + car.in_specs(),
        out_specs=[pl.BlockSpec((None, 3, None, S, BLK), lambda g: (0, 0, g, 0, 0)),
                   pl.BlockSpec((3, 2, BLK, KEYS), lambda g: (0, g, 0, 0))] + car.out_specs(),
        out_shape=[SDS((2, 3, 4, S, BLK), BF16), SDS((3, N_HEADS, BLK, KEYS), F32)] + car.out_shapes(),
        input_output_aliases=car.aliases(),
        scratch_shapes=[pltpu.VMEM((3, S, BLK), F32)] * 3 + [pltpu.VMEM((S, BLK), F32)] + car.sems(),
        compiler_params=_cp("arbitrary"))(qkv, bias, o, lse, do, *car.args())
    if job is not None:
        job.landed = list(outs[2:])
    return outs[0], outs[1]


def _shift_down(u, k, halo):
    n = u.shape[0]
    row = lax.broadcasted_iota(jnp.int32, u.shape, 0)
    out = pltpu.roll(u, k, 0)
    hn = halo.shape[0]
    for j in range(k):
        out = jnp.where(row == j, halo[hn - k + j:hn - k + j + 1, :], out)
    return out


def _shift_up(u, k, halo):
    n = u.shape[0]
    row = lax.broadcasted_iota(jnp.int32, u.shape, 0)
    out = pltpu.roll(u, n - k, 0)
    for j in range(k):
        out = jnp.where(row == n - k + j, halo[j:j + 1, :], out)
    return out


def _conv3(u, halo, w0, w1, w2):
    return _shift_down(u, 2, halo) * w0 + _shift_down(u, 1, halo) * w1 + u * w2


def _colsum(x):
    return jnp.sum(x, axis=0, keepdims=True)


def _mixer_specs(S, tm):
    conv = pl.BlockSpec((None, 12, tm, BLK), lambda i: (1, 0, i, 0))
    halo = pl.BlockSpec((None, 12, 16, BLK), lambda i: (1, 0, jnp.maximum(i * (tm // 16) - 1, 0), 0))
    ob = pl.BlockSpec((4, tm, BLK), lambda i: (0, i, 0))
    return conv, halo, ob


def _mixer_recompute(i, o_ref, pr_ref, ph_ref, w_ref):
    ob = [o_ref[q] for q in range(4)]
    gb = [pr_ref[q].astype(F32) for q in range(4)]
    gc = [pr_ref[4 + q].astype(F32) for q in range(4)]
    xi = [pr_ref[8 + q].astype(F32) for q in range(4)]
    keep = jnp.where(i > 0, 1.0, 0.0)
    u = [gc[q] * xi[q] for q in range(4)]
    hu = [ph_ref[4 + q].astype(F32) * ph_ref[8 + q].astype(F32) * keep for q in range(4)]
    w = [[w_ref[k:k + 1, q * BLK:(q + 1) * BLK] for k in range(3)] for q in range(4)]
    cv = [_conv3(u[q], hu[q], *w[q]) for q in range(4)]
    return ob, gb, gc, xi, u, hu, cv, w


def _rms_blocks(blocks):
    ss = sum(_rsum(b * b) for b in blocks)
    return lax.rsqrt(ss / (BLK * len(blocks)) + EPS)


def mixer_post_fwd(o, proj, w_sc, g_a, g_c):
    S = o.shape[1]
    tm = 512

    def body(o_ref, pr_ref, ph_ref, w_ref, ga_ref, gc_ref, m_ref):
        i = pl.program_id(0)
        ob, gb, _, _, _, _, cv, _ = _mixer_recompute(i, o_ref, pr_ref, ph_ref, w_ref)
        conv = [gb[q] * cv[q] for q in range(4)]
        ra, rc = _rms_blocks(ob), _rms_blocks(conv)
        for q in range(4):
            sl = slice(q * BLK, (q + 1) * BLK)
            m_ref[:, q * BLK:(q + 1) * BLK] = (ob[q] * ra * ga_ref[:, sl]).astype(BF16)
            m_ref[:, ATTN_W + q * BLK:ATTN_W + (q + 1) * BLK] = (conv[q] * rc * gc_ref[:, sl]).astype(BF16)

    conv_s, halo_s, ob_s = _mixer_specs(S, tm)
    full = lambda r, c: pl.BlockSpec((r, c), lambda i: (0, 0))
    return pl.pallas_call(
        body, grid=(S // tm,), name="mixer_post_fwd",
        in_specs=[ob_s, conv_s, halo_s, full(3, 512), full(1, 512), full(1, 512)],
        out_specs=pl.BlockSpec((tm, D_MODEL), lambda i: (i, 0)), out_shape=SDS((S, D_MODEL), BF16),
        compiler_params=_cp("parallel"))(o, proj, proj, w_sc, g_a, g_c)


def mixer_post_bwd_a(dmixed, o, proj, w_sc, g_a, g_c, job=None):
    S = o.shape[1]
    tm = 512

    def body(dm_ref, o_ref, pr_ref, ph_ref, w_ref, ga_ref, gc_ref, do_ref, dgb_ref, dcv_ref, dga_ref, dgc_ref):
        i = pl.program_id(0)
        ob, gb, _, _, _, _, cv, _ = _mixer_recompute(i, o_ref, pr_ref, ph_ref, w_ref)
        conv = [gb[q] * cv[q] for q in range(4)]
        ra, rc = _rms_blocks(ob), _rms_blocks(conv)

        @pl.when(i == 0)
        def _():
            dga_ref[...] = jnp.zeros_like(dga_ref)
            dgc_ref[...] = jnp.zeros_like(dgc_ref)

        for blocks, r, g_ref, off, dg_ref, is_attn in ((ob, ra, ga_ref, 0, dga_ref, True), (conv, rc, gc_ref, ATTN_W, dgc_ref, False)):
            xh = [blocks[q] * r for q in range(4)]
            dm = [dm_ref[:, off + q * BLK:off + (q + 1) * BLK].astype(F32) for q in range(4)]
            gd = [dm[q] * g_ref[:, q * BLK:(q + 1) * BLK] for q in range(4)]
            mean = sum(_rsum(gd[q] * xh[q]) for q in range(4)) / (4 * BLK)
            for q in range(4):
                dg_ref[:, q * BLK:(q + 1) * BLK] += _colsum(dm[q] * xh[q])
                dx = r * (gd[q] - xh[q] * mean)
                if is_attn:
                    do_ref[q] = dx
                else:
                    dgb_ref[q] = dx * cv[q]
                    dcv_ref[q] = dx * gb[q]

    conv_s, halo_s, ob_s = _mixer_specs(S, tm)
    full = lambda r, c: pl.BlockSpec((r, c), lambda i: (0, 0))
    return _pcall(
        body, [dmixed, o, proj, proj, w_sc, g_a, g_c], grid=(S // tm,), name="mixer_post_bwd_a",
        in_specs=[pl.BlockSpec((tm, D_MODEL), lambda i: (i, 0)), ob_s, conv_s, halo_s, full(3, 512), full(1, 512), full(1, 512)],
        out_specs=[ob_s, ob_s, ob_s, full(1, 512), full(1, 512)],
        out_shape=[SDS((4, S, BLK), F32)] * 3 + [SDS((1, 512), F32)] * 2, sem=("arbitrary",), job=job)


def mixer_post_bwd_b(dproj, dgb, dcv, proj, w_sc):
    S = proj.shape[2]
    tm = 512
    last = S // 8 - 1

    def body(dp_in, dgb_ref, dcv_ref, dn_ref, pr_ref, w_ref, dp_ref, dw_ref):
        i = pl.program_id(0)
        keep_next = jnp.where(i < pl.num_programs(0) - 1, 1.0, 0.0)

        @pl.when(i == 0)
        def _():
            dw_ref[...] = jnp.zeros_like(dw_ref)

        for q in range(4):
            sl = slice(q * BLK, (q + 1) * BLK)
            gc, xi = pr_ref[4 + q].astype(F32), pr_ref[8 + q].astype(F32)
            u = gc * xi
            dcv = dcv_ref[q]
            dn = dn_ref[q] * keep_next
            d1, d2 = _shift_up(dcv, 1, dn), _shift_up(dcv, 2, dn)
            w0, w1, w2 = (w_ref[k:k + 1, sl] for k in range(3))
            du = dcv * w2 + d1 * w1 + d2 * w0
            dw_ref[0:1, sl] += _colsum(d2 * u)
            dw_ref[1:2, sl] += _colsum(d1 * u)
            dw_ref[2:3, sl] += _colsum(dcv * u)
            dp_ref[q] = dgb_ref[q].astype(BF16)
            dp_ref[4 + q] = (du * xi).astype(BF16)
            dp_ref[8 + q] = (du * gc).astype(BF16)

    conv_s, _, ob_s = _mixer_specs(S, tm)
    nxt = pl.BlockSpec((4, 8, BLK), lambda i: (0, jnp.minimum((i + 1) * (tm // 8), last), 0))
    full = lambda r, c: pl.BlockSpec((r, c), lambda i: (0, 0))
    return pl.pallas_call(
        body, grid=(S // tm,), name="mixer_post_bwd_b",
        in_specs=[pl.BlockSpec(memory_space=pl.ANY), ob_s, ob_s, nxt, conv_s, full(3, 512)],
        out_specs=[conv_s, full(3, 512)],
        out_shape=[SDS(dproj.shape, BF16), SDS((3, 512), F32)],
        input_output_aliases={0: 0}, compiler_params=_cp("arbitrary"))(dproj, dgb, dcv, dcv, proj, w_sc)


def xattn_fwd(q, k, v):
    S = q.shape[0]
    tm = 512
    scale = MEM_HD ** -0.5

    def body(q_ref, k_ref, v_ref, o_ref):
        for h in range(N_MEM_HEADS):
            sl = slice(h * MEM_HD, (h + 1) * MEM_HD)
            s = _dot_nt(q_ref[:, sl], k_ref[:, sl]) * scale
            p = jnp.exp(s - jnp.max(s, axis=1, keepdims=True))
            p = p / _rsum(p)
            o_ref[:, sl] = _dot(p.astype(BF16), v_ref[:, sl]).astype(BF16)

    row = pl.BlockSpec((tm, D_MODEL), lambda i: (i, 0))
    kv = pl.BlockSpec(k.shape, lambda i: (0, 0))
    return pl.pallas_call(body, grid=(S // tm,), name="xattn_fwd", in_specs=[row, kv, kv], out_specs=row,
                          out_shape=SDS((S, D_MODEL), BF16), compiler_params=_cp("parallel"))(q, k, v)


def xattn_bwd(q, k, v, do):
    S = q.shape[0]
    tm = 512
    scale = MEM_HD ** -0.5

    def body(q_ref, k_ref, v_ref, do_ref, dq_ref, dk_ref, dv_ref):
        @pl.when(pl.program_id(0) == 0)
        def _():
            dk_ref[...] = jnp.zeros_like(dk_ref)
            dv_ref[...] = jnp.zeros_like(dv_ref)

        for h in range(N_MEM_HEADS):
            sl = slice(h * MEM_HD, (h + 1) * MEM_HD)
            qh, kh, vh, doh = q_ref[:, sl], k_ref[:, sl], v_ref[:, sl], do_ref[:, sl]
            s = _dot_nt(qh, kh) * scale
            p = jnp.exp(s - jnp.max(s, axis=1, keepdims=True))
            p = p / _rsum(p)
            dp = _dot_nt(doh, vh)
            ds = (p * (dp - _rsum(p * dp)) * scale).astype(BF16)
            dq_ref[:, sl] = _dot(ds, kh).astype(BF16)
            dk_ref[:, sl] += _dot_tn(ds, qh)
            dv_ref[:, sl] += _dot_tn(p.astype(BF16), doh)

    row = pl.BlockSpec((tm, D_MODEL), lambda i: (i, 0))
    kv = pl.BlockSpec(k.shape, lambda i: (0, 0))
    return pl.pallas_call(body, grid=(S // tm,), name="xattn_bwd", in_specs=[row, kv, kv, row], out_specs=[row, kv, kv],
                          out_shape=[SDS((S, D_MODEL), BF16), SDS(k.shape, F32), SDS(k.shape, F32)],
                          compiler_params=_cp("arbitrary"))(q, k, v, do)


FFN_TM, FFN_TC = 512, 1408


def _ffn_specs(S):
    tm, tc = FFN_TM, FFN_TC
    blk = pl.BlockSpec((2, tm, tc), lambda j, i: (0, i, j))
    nxt = pl.BlockSpec((2, 16, tc), lambda j, i: (0, jnp.minimum((i + 1) * (tm // 16), S // 16 - 1), j))
    wsp = pl.BlockSpec((2, 3, tc), lambda j, i: (0, 0, j))
    bsp = pl.BlockSpec((2, 1, tc), lambda j, i: (0, 0, j))
    return blk, nxt, wsp, bsp


def up_act_fwd(h, w_up, w_fc, b_fc, job=None):
    S, K = h.shape
    tm, tc = FFN_TM, FFN_TC
    nj = D_FF // tc

    def body(h_ref, wg_ref, wv_ref, w_ref, b_ref, pre_ref, gv_ref, a_ref, halo):
        i = pl.program_id(1)
        hv = h_ref[...]
        res = []
        for half, wt_ref in enumerate((wg_ref, wv_ref)):
            u = _dot(hv, wt_ref[...])
            hu = jnp.where(i > 0, halo[half], 0.0)
            halo[half] = u[tm - 8:, :]
            w0, w1, w2 = (w_ref[half, k:k + 1, :] for k in range(3))
            pre_ref[half] = u.astype(BF16)
            res.append(_conv3(u, hu, w0, w1, w2) + b_ref[half])
        g, v = res
        gv_ref[0] = g.astype(BF16)
        gv_ref[1] = v.astype(BF16)
        a_ref[...] = (g * jax.nn.sigmoid(g) * v).astype(BF16)

    blk = pl.BlockSpec((2, tm, tc), lambda j, i: (0, i, j))
    return _pcall(
        body, [h, w_up, w_up, w_fc, b_fc], grid=(nj, S // tm), name="up_act_fwd",
        in_specs=[pl.BlockSpec((tm, K), lambda j, i: (i, 0)), pl.BlockSpec((K, tc), lambda j, i: (0, j)),
                  pl.BlockSpec((K, tc), lambda j, i: (0, j + nj)), pl.BlockSpec((2, 3, tc), lambda j, i: (0, 0, j)),
                  pl.BlockSpec((2, 1, tc), lambda j, i: (0, 0, j))],
        out_specs=[blk, blk, pl.BlockSpec((tm, tc), lambda j, i: (i, j))],
        out_shape=[SDS((2, S, D_FF), BF16), SDS((2, S, D_FF), BF16), SDS((S, D_FF), BF16)],
        scratch=[pltpu.VMEM((2, 8, tc), F32)], sem=("parallel", "arbitrary"), job=job)


def ffn_act_bwd(dx, w_down, up, up_pre, w_fc, job=None):
    S = up.shape[1]

    def act_grads(da, g, v):
        sg = jax.nn.sigmoid(g)
        return da * v * (sg * (1.0 + g * (1.0 - sg))), da * g * sg

    def body(dx_ref, dxn_ref, wd_ref, gv_ref, gvn_ref, up_ref, w_ref, dp_ref, dw_ref, db_ref):
        i = pl.program_id(1)
        keep_next = jnp.where(i < pl.num_programs(1) - 1, 1.0, 0.0)

        @pl.when(i == 0)
        def _():
            dw_ref[...] = jnp.zeros_like(dw_ref)
            db_ref[...] = jnp.zeros_like(db_ref)

        wd = wd_ref[...]
        da = _dot_nt(dx_ref[...].astype(BF16), wd)
        dan = _dot_nt(dxn_ref[...].astype(BF16), wd) * keep_next
        here = act_grads(da, gv_ref[0].astype(F32), gv_ref[1].astype(F32))
        after = act_grads(dan, gvn_ref[0].astype(F32), gvn_ref[1].astype(F32))
        for half in range(2):
            d, dn = here[half], after[half]
            u = up_ref[half].astype(F32)
            d1, d2 = _shift_up(d, 1, dn), _shift_up(d, 2, dn)
            w0, w1, w2 = (w_ref[half, k:k + 1, :] for k in range(3))
            dp_ref[half] = (d * w2 + d1 * w1 + d2 * w0).astype(BF16)
            dw_ref[half, 0:1, :] += _colsum(d2 * u)
            dw_ref[half, 1:2, :] += _colsum(d1 * u)
            dw_ref[half, 2:3, :] += _colsum(d * u)
            db_ref[half] += _colsum(d)

    tm, tc = FFN_TM, FFN_TC
    blk, nxt, wsp, bsp = _ffn_specs(S)
    K = dx.shape[1]
    rows = pl.BlockSpec((tm, K), lambda j, i: (i, 0))
    rows_next = pl.BlockSpec((16, K), lambda j, i: (jnp.minimum((i + 1) * (tm // 16), S // 16 - 1), 0))
    return _pcall(body, [dx, dx, w_down, up, up, up_pre, w_fc], grid=(D_FF // tc, S // tm), name="ffn_act_bwd",
                  in_specs=[rows, rows_next, pl.BlockSpec((tc, K), lambda j, i: (j, 0)), blk, nxt, blk, wsp],
                  out_specs=[blk, wsp, bsp],
                  out_shape=[SDS((2, S, D_FF), BF16), SDS((2, 3, D_FF), F32), SDS((2, 1, D_FF), F32)],
                  sem=("parallel", "arbitrary"), job=job)


def local_step(x, mem, target, wb, ws, comm=None):
    S = x.shape[0]
    assert S % 2048 == 0
    if comm is None:
        comm = NoComm()
    else:
        wb = {}
    ws = dict(ws)

    bias = bias_tables(ws["rel_bias"])
    h1 = rmsnorm_fwd(x, ws["g_mix"], "norm_mix", job=comm.job("norm_mix"))
    comm.landed("norm_mix", wb, ws)
    w_fc = ws["w_ffn_conv"].reshape(3, 2, D_FF).transpose(1, 0, 2)
    b_fc = ws["b_ffn_conv"].reshape(2, 1, D_FF)
    proj = mm_nn(h1, wb["w_in"], "proj", out_dtype=BF16, out_bw=BLK, tn=768, job=comm.job("proj"))
    comm.landed("proj", wb)
    qkv = proj.reshape(2, 3, 4, S, BLK)
    o, lse = attn_fwd(qkv, bias, job=comm.job("attn_fwd"))
    comm.landed("attn_fwd", wb)
    proj4 = proj.reshape(2, 12, S, BLK)
    mixed = mixer_post_fwd(o, proj4, ws["w_short_conv"], ws["g_attn_out"], ws["g_conv_out"])
    x1, h2 = mm_nn_rows(mixed, wb["w_out"], x, ws["g_xattn"], "out_proj")
    mem_n = rmsnorm_fwd(mem, ws["g_mem"], "norm_mem")
    xq = mm_nn(h2, wb["w_xq"], "xq", out_dtype=BF16)
    xk = mm_nn(mem_n, wb["w_xk"], "xk", out_dtype=BF16, tn=1024)
    xv = mm_nn(mem_n, wb["w_xv"], "xv", out_dtype=BF16, tn=1024)
    xo = xattn_fwd(xq, xk, xv)
    x2, h3 = mm_nn_rows(xo, wb["w_xo"], x1, ws["g_ffn"], "xo_proj")
    up_pre, up, act = up_act_fwd(h3, wb["w_up"], w_fc, b_fc, job=comm.job("up_act_fwd"))
    comm.landed("up_act_fwd", wb)
    loss, dx3, dg_final = mm_nn_rows(act, wb["w_down"], x2, ws["g_final"].reshape(1, -1), "down_proj", target=target)

    gb, gs = {}, {"g_final": dg_final}
    gb["w_down"] = mm_tn(act, dx3, None, "dw_down", tm=1408, tn=1024)
    d_up_pre, dw_fc, db_fc = ffn_act_bwd(dx3, wb["w_down"], up, up_pre, w_fc, job=comm.job("ffn_act_bwd", gb))
    comm.landed("ffn_act_bwd")
    gs["b_ffn_conv"] = db_fc.reshape(1, 2 * D_FF)
    gs["w_ffn_conv"] = dw_fc.transpose(1, 0, 2).reshape(3, 2 * D_FF)
    gb["w_up"] = mm_tn(h3, d_up_pre, D_FF, "dw_up", shards=4, tn=1408, job=comm.job("dw_up"))
    comm.landed("dw_up")
    dx2, gs["g_ffn"] = mm_nt_norm_bwd(d_up_pre, D_FF, wb["w_up"], x2, ws["g_ffn"], dx3, "d_h3", tm=512,
                                      job=comm.job("d_h3", gb))
    comm.landed("d_h3")
    gb["w_xo"] = mm_tn(xo, dx2, None, "dw_xo")
    dxo = mm_nt(dx2, None, wb["w_xo"], "d_xo", out_dtype=BF16)
    dxq, dxk, dxv = xattn_bwd(xq, xk, xv, dxo)
    gb["w_xq"] = mm_tn(h2, dxq, None, "dw_xq")
    gb["w_xk"] = mm_tn(mem_n, dxk, None, "dw_xk", tn=1024)
    gb["w_xv"] = mm_tn(mem_n, dxv, None, "dw_xv", tn=1024)
    dmem_n = mm_nt(dxk, None, wb["w_xk"], "d_memk", tn=1024)
    dmem_n = mm_nt(dxv, None, wb["w_xv"], "d_memv", res=dmem_n, tn=1024)
    gs["g_mem"] = rmsnorm_gain_grad(mem, dmem_n, "norm_mem_bwd")
    dx1, gs["g_xattn"] = mm_nt_norm_bwd(dxq, None, wb["w_xq"], x1, ws["g_xattn"], dx2, "d_h2")
    gb["w_out"] = mm_tn(mixed, dx1, None, "dw_out")
    dmixed = mm_nt(dx1, None, wb["w_out"], "d_mixed", out_dtype=BF16)
    do, dgb, dcv, gs["g_attn_out"], gs["g_conv_out"] = mixer_post_bwd_a(
        dmixed, o, proj4, ws["w_short_conv"], ws["g_attn_out"], ws["g_conv_out"], job=comm.job("mixer_post_bwd_a", gb))
    comm.landed("mixer_post_bwd_a")
    dproj, ds_sum = attn_bwd(qkv, bias, o, lse, do, job=comm.job("attn_bwd"))
    comm.landed("attn_bwd")
    dproj, gs["w_short_conv"] = mixer_post_bwd_b(dproj.reshape(2, 12, S, BLK), dgb, dcv, proj4, ws["w_short_conv"])
    dproj = dproj.reshape(24, S, BLK)
    gb["w_in"] = mm_tn(h1, dproj, BLK, "dw_in", shards=4, tn=768, job=comm.job("dw_in"))
    comm.landed("dw_in")
    gs["rel_bias"] = bias_tables_bwd(ds_sum, job=comm.job("bias_tables_bwd", gb))
    comm.landed("bias_tables_bwd")
    grad_x, gs["g_mix"] = mm_nt_norm_bwd(dproj, BLK, wb["w_in"], x, ws["g_mix"], dx1, "d_h1", tm=512,
                                         job=comm.job("d_h1", gb))
    comm.landed("d_h1")
    return loss, grad_x, comm.finish(gb), gs


def _adam_update(w_ref, g_ref, m_ref, v_ref, d_ref, nm_ref, nv_ref):
    gv = g_ref[...]
    mn = ADAM_B1 * m_ref[...] + (1.0 - ADAM_B1) * gv
    vn = ADAM_B2 * v_ref[...] + (1.0 - ADAM_B2) * (gv * gv)
    m_hat = mn / (1.0 - ADAM_B1 ** ADAM_STEP)
    v_hat = vn / (1.0 - ADAM_B2 ** ADAM_STEP)
    d_ref[...] = -ADAM_LR * (m_hat / (jnp.sqrt(v_hat) + ADAM_EPS) + ADAM_WD * w_ref[...])
    nm_ref[...] = mn
    nv_ref[...] = vn


def adamw(ws, gs, ms, vs, name, job=None):
    n = len(ws)
    R, C = ws[0].shape
    tr = R // 4

    def body(*refs):
        for p in range(n):
            _adam_update(*(refs[q * n + p] for q in range(7)))

    blk = pl.BlockSpec((tr, C), lambda i: (i, 0))
    outs = _pcall(body, [*ws, *gs, *ms, *vs], grid=(4,), name=name, in_specs=[blk] * (4 * n), out_specs=[blk] * (3 * n),
                  out_shape=[SDS((R, C), F32)] * (3 * n), sem=("parallel",), job=job)
    return outs[:n], outs[n:2 * n], outs[2 * n:]


BIG = ("w_in", "w_out", "w_xq", "w_xk", "w_xv", "w_xo", "w_up", "w_down")
COL_SHARDED = ("w_in", "w_up")
ANY = pl.BlockSpec(memory_space=pl.ANY)


def _place():
    x, y, c = lax.axis_index("x"), lax.axis_index("y"), lax.axis_index("c")
    chips = [(1 - x, y), (x, 1 - y), (1 - x, 1 - y)]
    return x, y, c, chips


def _window(full, name, R, C, shard, half):
    r0, nr = (0, R) if half is None else (half * (R // 2), R // 2)
    if name in COL_SHARDED:
        return full.at[pl.ds(r0, nr), pl.ds(shard * C, C)]
    return full.at[pl.ds(shard * R + r0, nr), :]


def same_shape_groups(names, arrays):
    groups = {}
    for n in names:
        groups.setdefault(tuple(arrays[n].shape), []).append(n)
    return list(groups.values())


def place_shards(ws, mine_arr, col, name):
    n = len(ws)
    R, C = ws[0].shape

    def body(s_ref, *refs):
        for p in range(n):
            refs[n + p][...] = refs[p][...].astype(BF16)

    grid_spec = pltpu.PrefetchScalarGridSpec(
        num_scalar_prefetch=1, grid=(1,), in_specs=[pl.BlockSpec((R, C), lambda i, s: (0, 0))] * n,
        out_specs=[pl.BlockSpec((R, C), (lambda i, s: (0, s[0])) if col else (lambda i, s: (s[0], 0)))] * n)
    return pl.pallas_call(body, grid_spec=grid_spec, name=name,
                          out_shape=[SDS((R, 4 * C) if col else (4 * R, C), BF16)] * n,
                          compiler_params=_cp("arbitrary"))(mine_arr, *ws)


def _gather_jobs(names, shapes):
    nw = len(names)

    def start(full, sems):
        send, recv, fsend, frecv = sems
        x, y, c, chips = _place()
        mine = 2 * x + y
        for w, n in enumerate(names):
            R, C = shapes[w]
            own = _window(full[w], n, R, C, mine, c)
            for j, chip in enumerate(chips):
                pltpu.make_async_remote_copy(src_ref=own, dst_ref=own, send_sem=send.at[w, j], recv_sem=recv.at[w, j],
                                             device_id=(*chip, c), device_id_type=MESH).start()

    def mid(full, sems, lo=0, hi=nw):
        send, recv, fsend, frecv = sems
        x, y, c, chips = _place()
        sib = (x, y, 1 - c)
        for w, n in list(enumerate(names))[lo:hi]:
            R, C = shapes[w]
            for j, chip in enumerate(chips):
                landed = _window(full[w], n, R, C, 2 * chip[0] + chip[1], c)
                pltpu.make_async_remote_copy(src_ref=landed, dst_ref=landed, send_sem=send.at[w, j], recv_sem=recv.at[w, j],
                                             device_id=(*chip, c), device_id_type=MESH).wait_recv()
                pltpu.make_async_remote_copy(src_ref=landed, dst_ref=landed, send_sem=fsend.at[w, j],
                                             recv_sem=frecv.at[w, j], device_id=sib, device_id_type=MESH).start()

    def finish(full, sems):
        send, recv, fsend, frecv = sems
        x, y, c, chips = _place()
        mine = 2 * x + y
        sib = (x, y, 1 - c)
        for w, n in enumerate(names):
            R, C = shapes[w]
            own = _window(full[w], n, R, C, mine, c)
            for j, chip in enumerate(chips):
                landed = _window(full[w], n, R, C, 2 * chip[0] + chip[1], c)
                other = _window(full[w], n, R, C, 2 * chip[0] + chip[1], 1 - c)
                pltpu.make_async_remote_copy(src_ref=other, dst_ref=other, send_sem=fsend.at[w, j], recv_sem=frecv.at[w, j],
                                             device_id=sib, device_id_type=MESH).wait_recv()
                pltpu.make_async_remote_copy(src_ref=own, dst_ref=own, send_sem=send.at[w, j], recv_sem=recv.at[w, j],
                                             device_id=(*chip, c), device_id_type=MESH).wait_send()
                pltpu.make_async_remote_copy(src_ref=landed, dst_ref=landed, send_sem=fsend.at[w, j],
                                             recv_sem=frecv.at[w, j], device_id=sib, device_id_type=MESH).wait_send()

    return start, mid, finish, [pltpu.SemaphoreType.DMA((nw, 3))] * 4


class CommJob:
    def __init__(self, ins, out_shapes, inplace, start, finish, sems, mid=None):
        self.ins, self.out_shapes, self.inplace = list(ins), list(out_shapes), inplace
        self.start, self.mid, self.finish, self.sems = start, mid, finish, list(sems)

    def run(self, phase, ins, outs, sems):
        if phase == 0:
            self.start(ins, outs, sems)
        elif phase == 1:
            if self.mid is not None:
                self.mid(ins, outs, sems)
        else:
            self.finish(ins, outs, sems)


class Carried:
    def __init__(self, job, n_in, n_out, n_scratch, steps):
        self.job, self.n_in, self.n_out, self.n_scratch, self.steps = job, n_in, n_out, n_scratch, steps
        self.nji = len(job.ins) if job else 0
        self.njo = len(job.out_shapes) if job else 0

    def in_specs(self):
        return [ANY] * self.nji

    def out_specs(self):
        return [ANY] * self.njo

    def out_shapes(self):
        return list(self.job.out_shapes) if self.job else []

    def aliases(self):
        if not (self.job and self.job.inplace):
            return {}
        return {self.n_in + i: self.n_out + i for i in range(self.nji)}

    def sems(self):
        return list(self.job.sems) if self.job else []

    def args(self):
        return list(self.job.ins) if self.job else []

    def split(self, refs):
        a = self.n_in
        b = a + self.nji
        c = b + self.n_out
        d = c + self.njo
        e = d + self.n_scratch
        return refs[:a] + refs[b:c] + refs[d:e], (refs[a:b], refs[c:d], refs[e:])

    def phase(self, phase, step, jrefs):
        if self.job is None:
            return
        at = {0: 0, 1: max(self.steps - 2, 0), 2: self.steps - 1}[phase]

        @pl.when(step == at)
        def _():
            self.job.run(phase, *jrefs)


def _pcall(body, args, *, grid, in_specs, out_specs, out_shape, name, sem, scratch=(), aliases=None, job=None):
    n_in, n_out = len(args), len(out_shape)
    if job is None:
        return pl.pallas_call(
            body, grid=grid, in_specs=list(in_specs), out_specs=list(out_specs), out_shape=list(out_shape), name=name,
            scratch_shapes=list(scratch), input_output_aliases=dict(aliases or {}), compiler_params=_cp(*sem))(*args)
    total = int(np.prod(grid))
    car = Carried(job, n_in, n_out, len(scratch), total)

    def wrapped(*refs):
        main, jrefs = car.split(refs)
        lin = pl.program_id(0)
        for ax in range(1, len(grid)):
            lin = lin * grid[ax] + pl.program_id(ax)
        car.phase(0, lin, jrefs)
        body(*main)
        car.phase(1, lin, jrefs)
        car.phase(2, lin, jrefs)

    res = pl.pallas_call(
        wrapped, grid=grid, in_specs=list(in_specs) + car.in_specs(), out_specs=list(out_specs) + car.out_specs(),
        out_shape=list(out_shape) + car.out_shapes(), name=name, scratch_shapes=list(scratch) + car.sems(),
        input_output_aliases={**dict(aliases or {}), **car.aliases()},
        compiler_params=_cp(*(["arbitrary"] * len(grid))))(*args, *car.args())
    job.landed = list(res[n_out:])
    return list(res[:n_out])


def gather_job(placed, names):
    shapes = []
    for n in names:
        R, C = placed[n].shape
        shapes.append((R, C // 4) if n in COL_SHARDED else (R // 4, C))
    start, mid, finish, sems = _gather_jobs(names, shapes)
    arrays = [placed[n] for n in names]
    early = max(len(names) - 1, 1)

    def last(i, o, s):
        mid(o, s, early, len(names))
        finish(o, s)

    return CommJob(arrays, [SDS(a.shape, a.dtype) for a in arrays], True,
                   lambda i, o, s: start(o, s), last, sems, mid=lambda i, o, s: mid(o, s, 0, early))


def gather_rows_job(placed):
    widths = [a.shape[1] // 4 for a in placed]

    def copies(outs, sems):
        x, y, c, chips = _place()
        mine = 2 * x + y
        cps = []
        for w, ref in enumerate(outs):
            own = ref.at[:, pl.ds(mine * widths[w], widths[w])]
            for j, chip in enumerate(chips):
                theirs = ref.at[:, pl.ds((2 * chip[0] + chip[1]) * widths[w], widths[w])]
                kw = dict(send_sem=sems[0].at[w, j], recv_sem=sems[1].at[w, j], device_id=(*chip, c), device_id_type=MESH)
                cps.append((pltpu.make_async_remote_copy(src_ref=own, dst_ref=own, **kw),
                            pltpu.make_async_remote_copy(src_ref=theirs, dst_ref=theirs, **kw)))
        return cps

    def start(ins, outs, sems):
        for send, _ in copies(outs, sems):
            send.start()

    def finish(ins, outs, sems):
        for send, recv in copies(outs, sems):
            recv.wait_recv()
            send.wait_send()

    return CommJob(placed, [SDS(a.shape, a.dtype) for a in placed], True, start, finish,
                   [pltpu.SemaphoreType.DMA((len(placed), 3))] * 2)


def merge_jobs(a, b):
    assert a.inplace and b.inplace
    na, nb_, nsa = len(a.ins), len(b.ins), len(a.sems)

    def phase(which):
        def run(ins, outs, sems):
            for job, i, o, s in ((a, ins[:na], outs[:na], sems[:nsa]), (b, ins[na:], outs[na:], sems[nsa:])):
                fn = getattr(job, which)
                if fn is not None:
                    fn(i, o, s)
        return run

    merged = CommJob(a.ins + b.ins, a.out_shapes + b.out_shapes, True, phase("start"), phase("finish"), a.sems + b.sems,
                     mid=phase("mid"))
    merged.parts = (a, b, na)
    return merged


def split_landed(merged):
    a, b, na = merged.parts
    a.landed, b.landed = merged.landed[:na], merged.landed[na:]


def pair_exchange_job(grads, names):
    shapes = [grads[n].shape for n in names]

    def copies(ins, outs, sems):
        x, y, c, _ = _place()
        return [pltpu.make_async_remote_copy(
            src_ref=ins[w].at[:, pl.ds((1 - c) * (shapes[w][1] // 2), shapes[w][1] // 2), :], dst_ref=outs[w],
            send_sem=sems[0].at[w], recv_sem=sems[1].at[w], device_id=(x, y, 1 - c), device_id_type=MESH)
            for w in range(len(names))]

    def start(ins, outs, sems):
        for cp in copies(ins, outs, sems):
            cp.start()

    def finish(ins, outs, sems):
        for cp in copies(ins, outs, sems):
            cp.wait()

    return CommJob([grads[n] for n in names], [SDS((4, s[1] // 2, s[2]), F32) for s in shapes], False, start, finish,
                   [pltpu.SemaphoreType.DMA((len(names),))] * 2)


def chip_exchange_job(p16, names):
    shapes = [p16[n].shape for n in names]

    def copies(ins, outs, sems):
        x, y, c, chips = _place()
        return [pltpu.make_async_remote_copy(
            src_ref=ins[w].at[2 * chip[0] + chip[1]], dst_ref=outs[w].at[j],
            send_sem=sems[0].at[w, j], recv_sem=sems[1].at[w, j], device_id=(*chip, c), device_id_type=MESH)
            for w in range(len(names)) for j, chip in enumerate(chips)]

    def start(ins, outs, sems):
        for cp in copies(ins, outs, sems):
            cp.start()

    def finish(ins, outs, sems):
        for cp in copies(ins, outs, sems):
            cp.wait()

    return CommJob([p16[n] for n in names], [SDS((3,) + tuple(s[1:]), BF16) for s in shapes], False, start, finish,
                   [pltpu.SemaphoreType.DMA((len(names), 3))] * 2)


def half_exchange_job(full, names):
    shapes = [full[n].shape for n in names]

    def copies(outs, sems):
        x, y, c, _ = _place()
        cps = []
        for w in range(len(names)):
            Rh = shapes[w][0] // 2
            rows = outs[w].at[pl.ds(c * Rh, Rh), :]
            other = outs[w].at[pl.ds((1 - c) * Rh, Rh), :]
            cps.append((pltpu.make_async_remote_copy(src_ref=rows, dst_ref=rows, send_sem=sems[0].at[w], recv_sem=sems[1].at[w],
                                                     device_id=(x, y, 1 - c), device_id_type=MESH),
                        pltpu.make_async_remote_copy(src_ref=other, dst_ref=other, send_sem=sems[0].at[w], recv_sem=sems[1].at[w],
                                                     device_id=(x, y, 1 - c), device_id_type=MESH)))
        return cps

    def start(ins, outs, sems):
        for send, _ in copies(outs, sems):
            send.start()

    def finish(ins, outs, sems):
        for send, recv in copies(outs, sems):
            recv.wait_recv()
            send.wait_send()

    arrays = [full[n] for n in names]
    return CommJob(arrays, [SDS(a.shape, a.dtype) for a in arrays], True, start, finish,
                   [pltpu.SemaphoreType.DMA((len(names),))] * 2)


def allreduce_small(parts, name, job=None):
    n = len(parts)
    nji = len(job.ins) if job else 0
    njo = len(job.out_shapes) if job else 0

    def body(*refs):
        ins, jins = refs[:n], refs[n:n + nji]
        outs, jouts = refs[n + nji:2 * n + nji], refs[2 * n + nji:2 * n + nji + njo]
        bufs = refs[2 * n + nji + njo:3 * n + nji + njo]
        send, recv = refs[3 * n + nji + njo:3 * n + nji + njo + 2]
        jsems = refs[3 * n + nji + njo + 2:]
        x, y, c, _ = _place()
        me = 4 * x + 2 * y + c
        if job is not None:
            job.run(0, jins, jouts, jsems)
        sends = []
        for p in range(n):
            for k in range(1, 8):
                peer = (x ^ (k >> 2), y ^ ((k >> 1) & 1), c ^ (k & 1))
                cp = pltpu.make_async_remote_copy(src_ref=ins[p], dst_ref=bufs[p].at[k], send_sem=send.at[p, k - 1],
                                                  recv_sem=recv.at[p, k - 1], device_id=peer, device_id_type=MESH)
                cp.start()
                sends.append(cp)
            bufs[p][0] = ins[p][...]
        for cp in sends:
            cp.wait_recv()
        for p in range(n):
            acc = bufs[p][me]
            for d in range(1, 8):
                acc = acc + bufs[p][d ^ me]
            outs[p][...] = acc
        for cp in sends:
            cp.wait_send()
        if job is not None:
            job.run(1, jins, jouts, jsems)
            job.run(2, jins, jouts, jsems)

    vm = pl.BlockSpec(memory_space=pltpu.VMEM)
    res = pl.pallas_call(
        body, name=name, in_specs=[vm] * n + [ANY] * nji, out_specs=[vm] * n + [ANY] * njo,
        out_shape=[SDS(a.shape, F32) for a in parts] + (list(job.out_shapes) if job else []),
        input_output_aliases={n + i: n + i for i in range(nji)} if (job and job.inplace) else {},
        scratch_shapes=[pltpu.VMEM((8,) + tuple(a.shape), F32) for a in parts] + [pltpu.SemaphoreType.DMA((n, 7))] * 2
        + (list(job.sems) if job else []),
    )(*parts, *(job.ins if job else []))
    if job is not None:
        job.landed = list(res[n:])
    return list(res[:n])


def adamw_multi(ws, gs, ms, vs, name):
    n = len(ws)

    def body(*refs):
        for p in range(n):
            _adam_update(*(refs[q * n + p] for q in range(7)))

    vm = pl.BlockSpec(memory_space=pltpu.VMEM)
    outs = pl.pallas_call(body, name=name, in_specs=[vm] * (4 * n), out_specs=[vm] * (3 * n),
                          out_shape=[SDS(a.shape, F32) for a in ws] * 3)(*ws, *gs, *ms, *vs)
    return outs[:n], outs[n:2 * n], outs[2 * n:]


def pair_sum(gs, xhs, sel_arr, name):
    n = len(gs)
    _, R, C = gs[0].shape
    Rh = R // 2

    def body(sel_ref, *refs):
        for p in range(n):
            g_ref, x_ref, p32_ref, p16_ref = (refs[q * n + p] for q in range(4))
            s = g_ref[...] + x_ref[...]
            p16_ref[...] = s.astype(BF16)

            @pl.when(pl.program_id(0) == sel_ref[0])
            def _():
                p32_ref[...] = s

    same = pl.BlockSpec((None, Rh, C), lambda s, sel: (s, 0, 0))
    grid_spec = pltpu.PrefetchScalarGridSpec(
        num_scalar_prefetch=1, grid=(4,),
        in_specs=[pl.BlockSpec((None, Rh, C), lambda s, sel: (s, sel[1], 0))] * n + [same] * n,
        out_specs=[pl.BlockSpec((Rh, C), lambda s, sel: (0, 0))] * n + [same] * n)
    outs = pl.pallas_call(body, grid_spec=grid_spec, name=name,
                          out_shape=[SDS((Rh, C), F32)] * n + [SDS((4, Rh, C), BF16)] * n,
                          compiler_params=_cp("arbitrary"))(sel_arr, *gs, *xhs)
    return outs[:n], outs[n:]


def chip_sum(p32s, ys, sel_arr, name):
    n = len(p32s)
    Rh, C = p32s[0].shape

    def body(s_ref, *refs):
        for p in range(n):
            p_ref, y_ref, o_ref = (refs[q * n + p] for q in range(3))
            o_ref[...] = ((p_ref[...] + y_ref[0].astype(F32)) + y_ref[1].astype(F32)) + y_ref[2].astype(F32)

    grid_spec = pltpu.PrefetchScalarGridSpec(
        num_scalar_prefetch=1, grid=(1,),
        in_specs=[pl.BlockSpec((Rh, C), lambda i, s: (0, 0))] * n + [pl.BlockSpec((3, Rh, C), lambda i, s: (0, 0, 0))] * n,
        out_specs=[pl.BlockSpec((Rh, C), lambda i, s: (s[1], 0))] * n)
    return pl.pallas_call(body, grid_spec=grid_spec, name=name, out_shape=[SDS((2 * Rh, C), F32)] * n,
                          compiler_params=_cp("arbitrary"))(sel_arr, *p32s, *ys)


class StepComm:
    FIRST = ("w_in",)
    SQUARE = ("w_out", "w_xq", "w_xk", "w_xv", "w_xo")
    FFN = ("w_up", "w_down")
    PLAN = {
        "proj": ("gather", SQUARE[:3]), "attn_fwd": ("gather", SQUARE[3:] + FFN[:1]), "up_act_fwd": ("gather", FFN[1:]),
        "ffn_act_bwd": ("pair", FFN[1:]), "dw_up": ("chip", FFN[1:]),
        "d_h3": ("pair", FFN[:1]), "mixer_post_bwd_a": ("pair", SQUARE), "attn_bwd": ("chip", FFN[:1] + SQUARE),
        "dw_in": ("half", FFN + SQUARE), "bias_tables_bwd": ("pair", FIRST), "d_h1": ("chip", FIRST),
    }

    def __init__(self, placed, conv_placed, sel_arr):
        self.placed, self.conv_placed, self.sel_arr = placed, conv_placed, sel_arr
        self.g, self.p32, self.p16, self.full, self.done, self.jobs = {}, {}, {}, {}, {}, {}

    def job(self, tag, gb=None):
        if tag == "norm_mix":
            self.jobs[tag] = merge_jobs(gather_job(self.placed, self.FIRST), gather_rows_job(self.conv_placed))
            return self.jobs[tag]
        kind, names = self.PLAN[tag]
        if kind == "gather":
            j = gather_job(self.placed, names)
        elif kind == "pair":
            for n in names:
                self.g[n] = gb[n] if gb[n].ndim == 3 else gb[n].reshape(4, gb[n].shape[0] // 4, gb[n].shape[1])
            j = pair_exchange_job(self.g, names)
        elif kind == "chip":
            j = chip_exchange_job(self.p16, names)
        else:
            j = half_exchange_job(self.full, names)
        self.jobs[tag] = j
        return j

    def landed(self, tag, wb=None, ws=None):
        if tag == "norm_mix":
            split_landed(self.jobs[tag])
            first, conv, _ = self.jobs[tag].parts
            wb.update(zip(self.FIRST, first.landed))
            ws.update((n, a[:3]) for n, a in zip(SMALL_SHARDED, conv.landed))
            return
        kind, names = self.PLAN[tag]
        got = dict(zip(names, self.jobs[tag].landed))
        if kind == "gather":
            wb.update(got)
        elif kind == "half":
            self.done.update(got)
        for group in same_shape_groups(names, got):
            if kind == "pair":
                p32s, p16s = pair_sum([self.g[n] for n in group], [got[n] for n in group], self.sel_arr, "pair_sum_" + group[0])
                self.p32.update(zip(group, p32s))
                self.p16.update(zip(group, p16s))
            elif kind == "chip":
                fulls = chip_sum([self.p32[n] for n in group], [got[n] for n in group], self.sel_arr, "chip_sum_" + group[0])
                self.full.update(zip(group, fulls))

    def finish(self, gb):
        return self.done

    def last_job(self):
        self.jobs["end_half"] = half_exchange_job(self.full, self.FIRST)
        return self.jobs["end_half"]

    def take_last(self):
        self.done.update(zip(self.FIRST, self.jobs["end_half"].landed))


class NoComm:
    def job(self, tag, gb=None):
        return None

    def landed(self, tag, wb=None, ws=None):
        pass

    def finish(self, gb):
        return gb


SMALL = ("rel_bias", "g_mix", "w_short_conv", "g_attn_out", "g_conv_out", "g_xattn", "g_mem", "g_ffn",
         "w_ffn_conv", "b_ffn_conv", "g_final")
SMALL_SHARDED = ("w_short_conv", "w_ffn_conv")


def kernel(x, mem, rel_bias, g_mix, w_in, w_short_conv, g_attn_out, g_conv_out, w_out, g_xattn, g_mem, w_xq, w_xk, w_xv, w_xo, g_ffn, w_up, w_ffn_conv, b_ffn_conv, w_down, g_final, loss_target, m_rel_bias, m_g_mix, m_w_in, m_w_short_conv, m_g_attn_out, m_g_conv_out, m_w_out, m_g_xattn, m_g_mem, m_w_xq, m_w_xk, m_w_xv, m_w_xo, m_g_ffn, m_w_up, m_w_ffn_conv, m_b_ffn_conv, m_w_down, m_g_final, v_rel_bias, v_g_mix, v_w_in, v_w_short_conv, v_g_attn_out, v_g_conv_out, v_w_out, v_g_xattn, v_g_mem, v_w_xq, v_w_xk, v_w_xv, v_w_xo, v_g_ffn, v_w_up, v_w_ffn_conv, v_b_ffn_conv, v_w_down, v_g_final):
    names = ("rel_bias", "g_mix", "w_in", "w_short_conv", "g_attn_out", "g_conv_out", "w_out", "g_xattn", "g_mem", "w_xq",
             "w_xk", "w_xv", "w_xo", "g_ffn", "w_up", "w_ffn_conv", "b_ffn_conv", "w_down", "g_final")
    W = dict(zip(names, (rel_bias, g_mix, w_in, w_short_conv, g_attn_out, g_conv_out, w_out, g_xattn, g_mem, w_xq, w_xk, w_xv,
                         w_xo, g_ffn, w_up, w_ffn_conv, b_ffn_conv, w_down, g_final)))
    M = dict(zip(names, (m_rel_bias, m_g_mix, m_w_in, m_w_short_conv, m_g_attn_out, m_g_conv_out, m_w_out, m_g_xattn, m_g_mem,
                         m_w_xq, m_w_xk, m_w_xv, m_w_xo, m_g_ffn, m_w_up, m_w_ffn_conv, m_b_ffn_conv, m_w_down, m_g_final)))
    V = dict(zip(names, (v_rel_bias, v_g_mix, v_w_in, v_w_short_conv, v_g_attn_out, v_g_conv_out, v_w_out, v_g_xattn, v_g_mem,
                         v_w_xq, v_w_xk, v_w_xv, v_w_xo, v_g_ffn, v_w_up, v_w_ffn_conv, v_b_ffn_conv, v_w_down, v_g_final)))
    xi, yi, ci = lax.axis_index("x"), lax.axis_index("y"), lax.axis_index("c")
    mine = 2 * xi + yi
    mine_arr = jnp.reshape(mine, (1,)).astype(jnp.int32)
    sel_arr = jnp.stack([mine, ci]).astype(jnp.int32)
    conv_placed = []
    for n in SMALL_SHARDED:
        shard = W[n][0]
        conv_placed.append(lax.dynamic_update_slice(jnp.zeros((8, 4 * shard.shape[1]), F32), shard, (0, mine * shard.shape[1])))
    shards = {n: W[n][0] for n in BIG}
    placed = {}
    for group in same_shape_groups(BIG, shards):
        outs = place_shards([shards[n] for n in group], mine_arr, group[0] in COL_SHARDED, "place_" + group[0])
        placed.update(zip(group, outs))
    comm = StepComm(placed, conv_placed, sel_arr)
    ws = {n: W[n] for n in SMALL if n not in SMALL_SHARDED}

    loss, grad_x, gfull, gs = local_step(x[0], mem[0], loss_target[0], None, ws, comm)

    def as2d(a):
        return a.reshape(1, -1) if a.ndim == 1 else a

    reduced = allreduce_small([as2d(gs[n]) for n in SMALL] + [loss], "reduce_small", job=comm.last_job())
    comm.take_last()
    gsm = dict(zip(SMALL, reduced[:-1]))
    loss = reduced[-1][0, 0]

    grads, delta, new_m, new_v = {}, {}, {}, {}
    for group in same_shape_groups(BIG, gfull):
        ds, nms, nvs = adamw([W[n][0] for n in group], [gfull[n] for n in group], [M[n][0] for n in group],
                             [V[n][0] for n in group], "adamw_" + group[0])
        for n, d, nm, nv in zip(group, ds, nms, nvs):
            grads[n], delta[n], new_m[n], new_v[n] = gfull[n][None], d[None], nm[None], nv[None]
    for n in SMALL_SHARDED:
        wid = W[n].shape[2]
        gsm[n] = lax.dynamic_slice(gsm[n], (0, mine * wid), (3, wid))

    def own(a, n):
        return a[0] if n in SMALL_SHARDED else as2d(a)

    d, nm, nv = adamw_multi([own(W[n], n) for n in SMALL], [gsm[n] for n in SMALL], [own(M[n], n) for n in SMALL],
                            [own(V[n], n) for n in SMALL], "adamw_small")
    for i, n in enumerate(SMALL):
        shape = W[n].shape
        grads[n], delta[n], new_m[n], new_v[n] = (a.reshape(shape) for a in (gsm[n], d[i], nm[i], nv[i]))
    return (loss, grad_x[None], *[grads[n] for n in names], *[delta[n] for n in names],
            *[new_m[n] for n in names], *[new_v[n] for n in names])
```

```python
import math

import numpy as np
import jax
import jax.numpy as jnp
from jax import lax
from jax.experimental import pallas as pl
from jax.experimental.pallas import tpu as pltpu

F32 = jnp.float32
BF16 = jnp.bfloat16
SDS = jax.ShapeDtypeStruct
MESH = pl.DeviceIdType.MESH

D_MODEL = 1024
ATTN_W = 512
N_HEADS = 8
HEAD_DIM = 64
HALF_HEAD = HEAD_DIM // 2
Q_SCALE = HEAD_DIM ** -0.5
BLK = 128
PATTERNS = ((128, 1), (512, 4), (2048, 16))
N_BUCKETS = 32
BUCKET_MAX_DIST = 2048
D_FF = 2816
N_MEM_HEADS = 4
MEM_HD = 256
EPS = 1e-6
NEG = -1e30
VMEM_LIMIT = 56 * 1024 * 1024

ADAM_LR, ADAM_B1, ADAM_B2, ADAM_EPS, ADAM_WD, ADAM_STEP = 0.001, 0.9, 0.999, 1e-08, 0.01, 10


def _cp(*sem):
    return pltpu.CompilerParams(dimension_semantics=sem, vmem_limit_bytes=VMEM_LIMIT)


def _dot(a, b):
    return jnp.dot(a, b, preferred_element_type=F32)


def _dot_nt(a, b):
    return lax.dot_general(a, b, (((1,), (1,)), ((), ())), preferred_element_type=F32)


def _dot_tn(a, b):
    return lax.dot_general(a, b, (((0,), (0,)), ((), ())), preferred_element_type=F32)


def _rsum(x):
    return jnp.sum(x, axis=1, keepdims=True)


def rmsnorm_fwd(x, g, name, job=None):
    S, Dm = x.shape
    tm = min(S, 512)

    def body(x_ref, g_ref, o_ref):
        xv = x_ref[...]
        r = lax.rsqrt(jnp.mean(xv * xv, axis=1, keepdims=True) + EPS)
        o_ref[...] = (xv * r * g_ref[...]).astype(o_ref.dtype)

    return _pcall(body, [x, g], grid=(S // tm,), name=name,
                  in_specs=[pl.BlockSpec((tm, Dm), lambda i: (i, 0)), pl.BlockSpec((1, Dm), lambda i: (0, 0))],
                  out_specs=[pl.BlockSpec((tm, Dm), lambda i: (i, 0))], out_shape=[SDS((S, Dm), BF16)],
                  sem=("parallel",), job=job)[0]


def rmsnorm_gain_grad(x, dh, name):
    S, Dm = x.shape
    tm = min(S, 512)

    def body(x_ref, dh_ref, dg_ref):
        xv = x_ref[...]
        xh = xv * lax.rsqrt(jnp.mean(xv * xv, axis=1, keepdims=True) + EPS)

        @pl.when(pl.program_id(0) == 0)
        def _():
            dg_ref[...] = jnp.zeros_like(dg_ref)

        dg_ref[...] += jnp.sum(dh_ref[...].astype(F32) * xh, axis=0, keepdims=True)

    row = pl.BlockSpec((tm, Dm), lambda i: (i, 0))
    return pl.pallas_call(
        body, grid=(S // tm,), name=name, in_specs=[row, row], out_specs=pl.BlockSpec((1, Dm), lambda i: (0, 0)),
        out_shape=SDS((1, Dm), F32), compiler_params=_cp("arbitrary"))(x, dh)


def _col_spec(bw, tn, rows, row_of, col_of):
    if bw is None:
        return pl.BlockSpec((rows, tn), lambda *g: (row_of(*g), col_of(*g)))
    if tn % bw == 0:
        return pl.BlockSpec((tn // bw, rows, bw), lambda *g: (col_of(*g), row_of(*g), 0))
    per = bw // tn
    return pl.BlockSpec((None, rows, tn), lambda *g: (col_of(*g) // per, row_of(*g), col_of(*g) % per))


def _read_cols(ref, bw, tn):
    if bw is None or tn % bw != 0:
        return ref[...]
    if tn == bw:
        return ref[0]
    return jnp.concatenate([ref[q] for q in range(tn // bw)], axis=1)


def _write_cols(ref, bw, tn, val):
    if bw is None or tn % bw != 0:
        ref[...] = val.astype(ref.dtype)
    else:
        for q in range(tn // bw):
            ref[q] = val[:, q * bw:(q + 1) * bw].astype(ref.dtype)


def mm_nn(a, b, name, *, out_dtype=F32, out_bw=None, tm=1024, tn=1024, job=None):
    M, K = a.shape
    N = b.shape[1]
    tm = min(tm, M)

    def body(a_ref, b_ref, o_ref):
        _write_cols(o_ref, out_bw, tn, _dot(a_ref[...].astype(BF16), b_ref[...]))

    ri, ci = (lambda i, j: i), (lambda i, j: j)
    in_specs = [pl.BlockSpec((tm, K), lambda i, j: (i, 0)), pl.BlockSpec((K, tn), lambda i, j: (0, j))]
    oshape = (M, N) if out_bw is None else (N // out_bw, M, out_bw)
    return _pcall(body, [a, b], grid=(M // tm, N // tn), name=name, in_specs=in_specs,
                  out_specs=[_col_spec(out_bw, tn, tm, ri, ci)], out_shape=[SDS(oshape, out_dtype)],
                  sem=("parallel", "parallel"), job=job)[0]


def mm_nn_rows(a, b, res, g, name, *, target=None, tm=512):
    M, K = a.shape
    N = b.shape[1]

    def body(*refs):
        a_ref, b_ref, r_ref, g_ref = refs[:4]
        xv = r_ref[...] + _dot(a_ref[...].astype(BF16), b_ref[...])
        r = lax.rsqrt(jnp.mean(xv * xv, axis=1, keepdims=True) + EPS)
        xh = xv * r
        if target is None:
            x_ref, h_ref = refs[4:]
            x_ref[...] = xv
            h_ref[...] = (xh * g_ref[...]).astype(BF16)
            return
        t_ref, loss_ref, dx_ref, dg_ref = refs[4:]
        err = xh * g_ref[...] - t_ref[...]
        dy = err / N
        gd = dy * g_ref[...]
        dx_ref[...] = r * (gd - xh * jnp.mean(gd * xh, axis=1, keepdims=True))

        @pl.when(pl.program_id(0) == 0)
        def _():
            dg_ref[...] = jnp.zeros_like(dg_ref)
            loss_ref[...] = jnp.zeros_like(loss_ref)

        dg_ref[...] += _colsum(dy * xh)
        loss_ref[...] += 0.5 * _colsum(jnp.mean(err * err, axis=1, keepdims=True))

    row = pl.BlockSpec((tm, N), lambda i: (i, 0))
    vec = pl.BlockSpec((1, N), lambda i: (0, 0))
    in_specs = [pl.BlockSpec((tm, K), lambda i: (i, 0)), pl.BlockSpec((K, N), lambda i: (0, 0)), row, vec]
    if target is None:
        return pl.pallas_call(body, grid=(M // tm,), name=name, in_specs=in_specs, out_specs=[row, row],
                              out_shape=[SDS((M, N), F32), SDS((M, N), BF16)], compiler_params=_cp("parallel"))(a, b, res, g)
    one = pl.BlockSpec((1, 1), lambda i: (0, 0))
    return pl.pallas_call(body, grid=(M // tm,), name=name, in_specs=in_specs + [row], out_specs=[one, row, vec],
                          out_shape=[SDS((1, 1), F32), SDS((M, N), F32), SDS((1, N), F32)],
                          compiler_params=_cp("arbitrary"))(a, b, res, g, target)


def mm_nt(a, a_bw, b, name, *, res=None, out_dtype=F32, tm=1024, tn=1024, job=None):
    if a_bw is None:
        M, K = a.shape
    else:
        M, K = a.shape[1], a.shape[0] * a_bw
    N = b.shape[0]
    tm = min(tm, M)

    def body(*refs):
        if res is None:
            a_ref, b_ref, o_ref = refs
        else:
            a_ref, b_ref, r_ref, o_ref = refs
        av = _read_cols(a_ref, a_bw, K).astype(BF16)
        acc = _dot_nt(av, b_ref[...])
        if res is not None:
            acc = acc + r_ref[...]
        o_ref[...] = acc.astype(o_ref.dtype)

    in_specs = [_col_spec(a_bw, K, tm, lambda i, j: i, lambda i, j: 0), pl.BlockSpec((tn, K), lambda i, j: (j, 0))]
    args = [a, b]
    if res is not None:
        in_specs.append(pl.BlockSpec((tm, tn), lambda i, j: (i, j)))
        args.append(res)
    return _pcall(body, args, grid=(M // tm, N // tn), name=name, in_specs=in_specs,
                  out_specs=[pl.BlockSpec((tm, tn), lambda i, j: (i, j))], out_shape=[SDS((M, N), out_dtype)],
                  sem=("parallel", "parallel"), job=job)[0]


def mm_nt_norm_bwd(a, a_bw, b, x, g, res, name, *, tm=512, job=None):
    if a_bw is None:
        M, K = a.shape
    else:
        M, K = a.shape[1], a.shape[0] * a_bw
    N = b.shape[0]

    def body(a_ref, b_ref, x_ref, g_ref, r_ref, dx_ref, dg_ref):
        dh = _dot_nt(_read_cols(a_ref, a_bw, K).astype(BF16), b_ref[...])
        xv = x_ref[...]
        r = lax.rsqrt(jnp.mean(xv * xv, axis=1, keepdims=True) + EPS)
        xh = xv * r
        gd = dh * g_ref[...]
        dx_ref[...] = r_ref[...] + r * (gd - xh * jnp.mean(gd * xh, axis=1, keepdims=True))

        @pl.when(pl.program_id(0) == 0)
        def _():
            dg_ref[...] = jnp.zeros_like(dg_ref)

        dg_ref[...] += jnp.sum(dh * xh, axis=0, keepdims=True)

    row = pl.BlockSpec((tm, N), lambda i: (i, 0))
    vec = pl.BlockSpec((1, N), lambda i: (0, 0))
    return _pcall(body, [a, b, x, g, res], grid=(M // tm,), name=name,
                  in_specs=[_col_spec(a_bw, K, tm, lambda i: i, lambda i: 0),
                            pl.BlockSpec((N, K), lambda i: (0, 0), pipeline_mode=pl.Buffered(1)), row, vec, row],
                  out_specs=[row, vec], out_shape=[SDS((M, N), F32), SDS((1, N), F32)], sem=("arbitrary",), job=job)


def mm_tn(a, b, b_bw, name, *, shards=None, tm=1024, tn=1024, ts=2048, job=None):
    S, Ka = a.shape
    N = b.shape[1] if b_bw is None else b.shape[0] * b_bw
    ts = min(ts, S)
    tm = min(tm, Ka)

    def body(a_ref, b_ref, o_ref):
        @pl.when(pl.program_id(2) == 0)
        def _():
            o_ref[...] = jnp.zeros_like(o_ref)

        bv = _read_cols(b_ref, b_bw, tn).astype(BF16)
        o_ref[...] += _dot_tn(a_ref[...].astype(BF16), bv)

    in_specs = [pl.BlockSpec((ts, tm), lambda i, j, k: (k, i)),
                _col_spec(b_bw, tn, ts, lambda i, j, k: k, lambda i, j, k: j)]
    if shards is None:
        out_spec, oshape = pl.BlockSpec((tm, tn), lambda i, j, k: (i, j)), (Ka, N)
    else:
        per = (N // shards) // tn
        out_spec = pl.BlockSpec((None, tm, tn), lambda i, j, k: (j // per, i, j % per))
        oshape = (shards, Ka, N // shards)
    return _pcall(body, [a, b], grid=(Ka // tm, N // tn, S // ts), name=name, in_specs=in_specs, out_specs=[out_spec],
                  out_shape=[SDS(oshape, F32)], sem=("parallel", "parallel", "arbitrary"), job=job)[0]


KEYS = 2 * BLK


def _bucket_tables():
    out = np.zeros((3, BLK, KEYS), np.int32)
    qi = np.arange(BLK)[:, None]
    kj = np.arange(KEYS)[None, :]
    for p, (win, dil) in enumerate(PATTERNS):
        w = win // dil
        assert w == BLK
        steps = qi + w - kj
        valid = (steps >= 0) & (steps <= w)
        dist = np.clip(steps, 0, w) * dil
        dd = np.maximum(dist, 1).astype(np.float32)
        exact = N_BUCKETS // 2
        large = exact + (np.log(dd / np.float32(exact)) / np.float32(math.log(BUCKET_MAX_DIST / exact))
                         * np.float32(N_BUCKETS - exact)).astype(np.int32)
        large = np.minimum(large, N_BUCKETS - 1)
        out[p] = np.where(valid, np.where(dist < exact, dist, large), -1)
    return out


def bias_tables(rel_bias):
    bkt = jnp.asarray(_bucket_tables())

    def body(rb_ref, bkt_ref, o_ref):
        bk = bkt_ref[...]
        for h in range(N_HEADS):
            acc = jnp.full((BLK, KEYS), NEG, F32)
            for b in range(N_BUCKETS):
                acc = jnp.where(bk == b, rb_ref[h, b], acc)
            o_ref[h] = acc

    return pl.pallas_call(
        body, grid=(3,), name="bias_tables",
        in_specs=[pl.BlockSpec(memory_space=pltpu.SMEM),
                  pl.BlockSpec((None, BLK, KEYS), lambda p: (p, 0, 0))],
        out_specs=pl.BlockSpec((None, N_HEADS, BLK, KEYS), lambda p: (p, 0, 0, 0)),
        out_shape=SDS((3, N_HEADS, BLK, KEYS), F32), compiler_params=_cp("parallel"))(rel_bias, bkt)


def bias_tables_bwd(ds_sum, job=None):
    bkt = jnp.asarray(_bucket_tables())

    def body(ds_ref, bkt_ref, o_ref):
        h = pl.program_id(0)

        @pl.when(h == 0)
        def _():
            o_ref[...] = jnp.zeros_like(o_ref)

        rows = lax.broadcasted_iota(jnp.int32, (N_HEADS, N_BUCKETS), 0)
        cols = lax.broadcasted_iota(jnp.int32, (N_HEADS, N_BUCKETS), 1)
        acc = jnp.zeros((N_HEADS, N_BUCKETS), F32)
        for b in range(N_BUCKETS):
            t = jnp.zeros((BLK, KEYS), F32)
            for p in range(3):
                t = t + jnp.where(bkt_ref[p] == b, ds_ref[p], 0.0)
            tot = jnp.sum(_rsum(t), axis=0, keepdims=True)
            acc = acc + jnp.where((rows == h) & (cols == b), tot, 0.0)
        o_ref[...] += acc

    return _pcall(
        body, [ds_sum, bkt], grid=(N_HEADS,), name="bias_tables_bwd",
        in_specs=[pl.BlockSpec((3, None, BLK, KEYS), lambda h: (0, h, 0, 0)),
                  pl.BlockSpec((3, BLK, KEYS), lambda h: (0, 0, 0))],
        out_specs=[pl.BlockSpec((N_HEADS, N_BUCKETS), lambda h: (0, 0))],
        out_shape=[SDS((N_HEADS, N_BUCKETS), F32)], sem=("arbitrary",), job=job)[0]


def _rows(start, dil):
    if dil == 1:
        return pl.ds(pl.multiple_of(start, BLK), BLK)
    return pl.ds(start, BLK, stride=dil)


GRP = 8


def _group_rows(i, dil, S):
    nb = S // (BLK * dil)
    run = min(nb, GRP)
    chunks = nb // run
    res0, b0 = (i // chunks) * (GRP // run), (i % chunks) * run
    cur = [(b0 + j % run) * (BLK * dil) + res0 + j // run for j in range(GRP)]
    t = lax.broadcasted_iota(jnp.int32, (GRP, 1, 1), 0)
    if run == nb:
        before, pen = None, jnp.where(t % run == 0, NEG, 0.0)
    else:
        before = _rows(jnp.maximum(b0 - 1, 0) * (BLK * dil) + res0, dil)
        pen = jnp.where((t == 0) & (b0 == 0), NEG, 0.0)
    return [_rows(s, dil) for s in cur], before, pen


def _load_group(ref, rows):
    return jnp.stack([ref[r, :] for r in rows])


def _with_prev(ref, before, cur_blocks):
    first = cur_blocks[:1] if before is None else ref[before, :][None].astype(cur_blocks.dtype)
    return jnp.concatenate([first, cur_blocks[:-1]], axis=0)


def _bdot_nt(a, b):
    return lax.dot_general(a, b, (((2,), (2,)), ((0,), (0,))), preferred_element_type=F32)


def _bdot(a, b):
    return lax.dot_general(a, b, (((2,), (1,)), ((0,), (0,))), preferred_element_type=F32)


def _bdot_tn(a, b):
    return lax.dot_general(a, b, (((1,), (1,)), ((0,), (0,))), preferred_element_type=F32)


def _lsum(x):
    return jnp.sum(x, axis=-1, keepdims=True)


def _widen(src, dst):
    S = src.shape[1]

    def chunk(i, carry):
        rows = pl.ds(pl.multiple_of(i * 512, 512), 512)
        for a in range(3):
            dst[a, rows, :] = src[a, rows, :].astype(F32)
        return carry

    lax.fori_loop(0, S // 512, chunk, 0)


def attn_fwd(qkv, bias, job=None):
    S = qkv.shape[3]
    nblk = S // BLK
    car = Carried(job, 2, 2, 3, 4)

    def body(*refs):
        (qkv_in, bias_ref, o_ref, lse_ref, qkv_ref, part_o, part_ml), jrefs = car.split(refs)
        car.phase(0, pl.program_id(0), jrefs)
        _widen(qkv_in, qkv_ref)
        lane = lax.broadcasted_iota(jnp.int32, (GRP, BLK, BLK), 2)
        lo = lane < HEAD_DIM
        masks = (lo, jnp.logical_not(lo))
        lov = lax.broadcasted_iota(jnp.int32, (GRP, KEYS, BLK), 2) < HEAD_DIM
        vmasks = (lov, jnp.logical_not(lov))
        prev_keys = lax.broadcasted_iota(jnp.int32, (1, 1, KEYS), 2) < BLK
        q_ref, k_ref, v_ref = qkv_ref.at[0], qkv_ref.at[1], qkv_ref.at[2]
        targets = [(o_ref, lse_ref)] + [(part_o.at[t], part_ml.at[t]) for t in range(len(PATTERNS) - 1)]
        for p, (_, dil) in enumerate(PATTERNS):
            def step(i, carry, p=p, dil=dil):
                rc, before, pen = _group_rows(i, dil, S)
                q2 = _load_group(q_ref, rc) * Q_SCALE
                kc = _load_group(k_ref, rc).astype(BF16)
                keys = jnp.concatenate([_with_prev(k_ref, before, kc), kc], axis=1)
                vc = _load_group(v_ref, rc)
                vals = jnp.concatenate([_with_prev(v_ref, before, vc), vc], axis=1)
                pen = jnp.where(prev_keys, pen, 0.0)
                pv = jnp.zeros((GRP, BLK, BLK), F32)
                ms, ls = [], []
                for h in range(2):
                    qh = jnp.where(masks[h], q2, 0.0).astype(BF16)
                    s = _bdot_nt(qh, keys) + bias_ref[p, h][None] + pen
                    mn = jnp.max(s, axis=-1, keepdims=True)
                    pr = jnp.exp(s - mn)
                    pv = pv + _bdot(pr.astype(BF16), jnp.where(vmasks[h], vals, 0.0).astype(BF16))
                    ms.append(mn)
                    ls.append(_lsum(pr))
                ml_t = jnp.where(lo, jnp.where(lane < HALF_HEAD, ms[0], ls[0]),
                                 jnp.where(lane < HEAD_DIM + HALF_HEAD, ms[1], ls[1]))
                acc_ref, ml_ref = targets[p]
                for j, r in enumerate(rc):
                    acc_ref[r, :] = pv[j]
                    ml_ref[r, :] = ml_t[j]
                return carry

            lax.fori_loop(0, nblk // GRP, step, 0)

        def merge(i, carry):
            rows = pl.ds(pl.multiple_of(i * 512, 512), 512)
            is_m = (lax.broadcasted_iota(jnp.int32, (512, BLK), 1) & HALF_HEAD) == 0
            accs = [ref[rows, :] for ref, _ in targets]
            mls = [ref[rows, :] for _, ref in targets]
            mx = [jnp.where(is_m, ml, pltpu.roll(ml, HALF_HEAD, 1)) for ml in mls]
            dn = [jnp.where(is_m, pltpu.roll(ml, BLK - HALF_HEAD, 1), ml) for ml in mls]
            top = jnp.maximum(jnp.maximum(mx[0], mx[1]), mx[2])
            ws_ = [jnp.exp(m - top) for m in mx]
            den = ws_[0] * dn[0] + ws_[1] * dn[1] + ws_[2] * dn[2]
            o_ref[rows, :] = (ws_[0] * accs[0] + ws_[1] * accs[1] + ws_[2] * accs[2]) / den
            lse_ref[rows, :] = top + jnp.log(den)
            return carry

        lax.fori_loop(0, S // 512, merge, 0)
        car.phase(1, pl.program_id(0), jrefs)
        car.phase(2, pl.program_id(0), jrefs)

    outs = pl.pallas_call(
        body, grid=(4,), name="attn_fwd",
        in_specs=[pl.BlockSpec((None, 3, None, S, BLK), lambda g: (0, 0, g, 0, 0)),
                  pl.BlockSpec((3, 2, BLK, KEYS), lambda g: (0, g, 0, 0))] + car.in_specs(),
        out_specs=[pl.BlockSpec((None, S, BLK), lambda g: (g, 0, 0)),
                   pl.BlockSpec((None, S, BLK), lambda g: (g, 0, 0))] + car.out_specs(),
        out_shape=[SDS((4, S, BLK), F32), SDS((4, S, BLK), F32)] + car.out_shapes(),
        input_output_aliases=car.aliases(),
        scratch_shapes=[pltpu.VMEM((3, S, BLK), F32), pltpu.VMEM((2, S, BLK), F32), pltpu.VMEM((2, S, BLK), F32)] + car.sems(),
        compiler_params=_cp("arbitrary"))(qkv, bias, *car.args())
    if job is not None:
        job.landed = list(outs[2:])
    return outs[0], outs[1]


def attn_bwd(qkv, bias, o, lse, do, job=None):
    S = qkv.shape[3]
    nblk = S // BLK
    car = Carried(job, 5, 2, 4, 4)

    def body(*refs):
        (qkv_in, bias_ref, o_ref, lse_ref, do_ref, dqkv_out, ds_ref, qkv_ref, dqkv_ref, last_ref, st_ref), jrefs = car.split(refs)
        car.phase(0, pl.program_id(0), jrefs)
        _widen(qkv_in, qkv_ref)

        def stats(i, carry):
            rows = pl.ds(pl.multiple_of(i * 512, 512), 512)
            ln = lax.broadcasted_iota(jnp.int32, (512, BLK), 1)
            prod = do_ref[rows, :] * o_ref[rows, :]
            d0 = _rsum(jnp.where(ln < HEAD_DIM, prod, 0.0))
            d1 = _rsum(jnp.where(ln < HEAD_DIM, 0.0, prod))
            st_ref[rows, :] = jnp.where((ln & HALF_HEAD) == 0, lse_ref[rows, :], jnp.where(ln < HEAD_DIM, d0, d1))
            return carry

        lax.fori_loop(0, S // 512, stats, 0)
        lane = lax.broadcasted_iota(jnp.int32, (GRP, BLK, BLK), 2)
        lo = lane < HEAD_DIM
        masks = (lo, jnp.logical_not(lo))
        lov = lax.broadcasted_iota(jnp.int32, (GRP, KEYS, BLK), 2) < HEAD_DIM
        vmasks = (lov, jnp.logical_not(lov))
        prev_keys = lax.broadcasted_iota(jnp.int32, (1, 1, KEYS), 2) < BLK
        ds_ref[...] = jnp.zeros_like(ds_ref)
        q_ref, k_ref, v_ref = qkv_ref.at[0], qkv_ref.at[1], qkv_ref.at[2]
        last = len(PATTERNS) - 1
        for p, (_, dil) in enumerate(PATTERNS):
            def step(i, carry, p=p, dil=dil):
                tgt = last_ref if p == last else dqkv_ref
                dq_ref, dk_ref, dv_ref = tgt.at[0], tgt.at[1], tgt.at[2]
                rc, before, pen = _group_rows(i, dil, S)
                q2 = _load_group(q_ref, rc) * Q_SCALE
                kc = _load_group(k_ref, rc).astype(BF16)
                keys = jnp.concatenate([_with_prev(k_ref, before, kc), kc], axis=1)
                vc = _load_group(v_ref, rc).astype(BF16)
                vals = jnp.concatenate([_with_prev(v_ref, before, vc), vc], axis=1)
                pen = jnp.where(prev_keys, pen, 0.0)
                dot = _load_group(do_ref, rc)
                st = _load_group(st_ref, rc)
                dq = jnp.zeros((GRP, BLK, BLK), F32)
                dkeys = jnp.zeros((GRP, KEYS, BLK), F32)
                dvals = jnp.zeros((GRP, KEYS, BLK), F32)
                for h in range(2):
                    qh = jnp.where(masks[h], q2, 0.0).astype(BF16)
                    doh = jnp.where(masks[h], dot, 0.0).astype(BF16)
                    lh = st[:, :, HEAD_DIM * h:HEAD_DIM * h + 1]
                    delta = st[:, :, HEAD_DIM * h + HALF_HEAD:HEAD_DIM * h + HALF_HEAD + 1]
                    pr = jnp.exp(_bdot_nt(qh, keys) + bias_ref[p, h][None] + pen - lh)
                    ds = pr * (_bdot_nt(doh, vals) - delta)
                    ds_ref[p, h] += jnp.sum(ds, axis=0)
                    dsb = ds.astype(BF16)
                    dq = dq + jnp.where(masks[h], _bdot(dsb, keys), 0.0)
                    dkeys = dkeys + _bdot_tn(dsb, qh)
                    dvals = dvals + _bdot_tn(pr.astype(BF16), doh)
                dkp, dkc, dvp, dvc = dkeys[:, :BLK], dkeys[:, BLK:], dvals[:, :BLK], dvals[:, BLK:]
                none = jnp.zeros((1, BLK, BLK), F32)
                dkc = dkc + jnp.concatenate([dkp[1:], none], axis=0)
                dvc = dvc + jnp.concatenate([dvp[1:], none], axis=0)
                for j in range(GRP):
                    if p == 1:
                        dq_ref[rc[j], :] += dq[j] * Q_SCALE
                        dk_ref[rc[j], :] += dkc[j]
                        dv_ref[rc[j], :] += dvc[j]
                    else:
                        dq_ref[rc[j], :] = dq[j] * Q_SCALE
                        dk_ref[rc[j], :] = dkc[j]
                        dv_ref[rc[j], :] = dvc[j]
                if before is not None:
                    dk_ref[before, :] += dkp[0]
                    dv_ref[before, :] += dvp[0]
                return carry

            lax.fori_loop(0, nblk // GRP, step, 0)

        def narrow(i, carry):
            rows = pl.ds(pl.multiple_of(i * 512, 512), 512)
            for a in range(3):
                dqkv_out[a, rows, :] = (dqkv_ref[a, rows, :] + last_ref[a, rows, :]).astype(BF16)
            return carry

        lax.fori_loop(0, S // 512, narrow, 0)
        car.phase(1, pl.program_id(0), jrefs)
        car.phase(2, pl.program_id(0), jrefs)

    blk = pl.BlockSpec((None, S, BLK), lambda g: (g, 0, 0))
    outs = pl.pallas_call(
        body, grid=(4,), name="attn_bwd",
        in_specs=[pl.BlockSpec((None, 3, None, S, BLK), lambda g: (0, 0, g, 0, 0)),
                  pl.BlockSpec((3, 2, BLK, KEYS), lambda g: (0, g, 0, 0)), blk, blk, blk] + car.in_specs(),
        out_specs=[pl.BlockSpec((None, 3, None, S, BLK), lambda g: (0, 0, g, 0, 0)),
                   pl.BlockSpec((3, 2, BLK, KEYS), lambda g: (0, g, 0, 0))] + car.out_specs(),
        out_shape=[SDS((2, 3, 4, S, BLK), BF16), SDS((3, N_HEADS, BLK, KEYS), F32)] + car.out_shapes(),
        input_output_aliases=car.aliases(),
        scratch_shapes=[pltpu.VMEM((3, S, BLK), F32)] * 3 + [pltpu.VMEM((S, BLK), F32)] + car.sems(),
        compiler_params=_cp("arbitrary"))(qkv, bias, o, lse, do, *car.args())
    if job is not None:
        job.landed = list(outs[2:])
    return outs[0], outs[1]


def _shift_down(u, k, halo):
    n = u.shape[0]
    row = lax.broadcasted_iota(jnp.int32, u.shape, 0)
    out = pltpu.roll(u, k, 0)
    hn = halo.shape[0]
    for j in range(k):
        out = jnp.where(row == j, halo[hn - k + j:hn - k + j + 1, :], out)
    return out


def _shift_up(u, k, halo):
    n = u.shape[0]
    row = lax.broadcasted_iota(jnp.int32, u.shape, 0)
    out = pltpu.roll(u, n - k, 0)
    for j in range(k):
        out = jnp.where(row == n - k + j, halo[j:j + 1, :], out)
    return out


def _conv3(u, halo, w0, w1, w2):
    return _shift_down(u, 2, halo) * w0 + _shift_down(u, 1, halo) * w1 + u * w2


def _colsum(x):
    return jnp.sum(x, axis=0, keepdims=True)


def _mixer_specs(S, tm):
    conv = pl.BlockSpec((None, 12, tm, BLK), lambda i: (1, 0, i, 0))
    halo = pl.BlockSpec((None, 12, 16, BLK), lambda i: (1, 0, jnp.maximum(i * (tm // 16) - 1, 0), 0))
    ob = pl.BlockSpec((4, tm, BLK), lambda i: (0, i, 0))
    return conv, halo, ob


def _mixer_recompute(i, o_ref, pr_ref, ph_ref, w_ref):
    ob = [o_ref[q] for q in range(4)]
    gb = [pr_ref[q].astype(F32) for q in range(4)]
    gc = [pr_ref[4 + q].astype(F32) for q in range(4)]
    xi = [pr_ref[8 + q].astype(F32) for q in range(4)]
    keep = jnp.where(i > 0, 1.0, 0.0)
    u = [gc[q] * xi[q] for q in range(4)]
    hu = [ph_ref[4 + q].astype(F32) * ph_ref[8 + q].astype(F32) * keep for q in range(4)]
    w = [[w_ref[k:k + 1, q * BLK:(q + 1) * BLK] for k in range(3)] for q in range(4)]
    cv = [_conv3(u[q], hu[q], *w[q]) for q in range(4)]
    return ob, gb, gc, xi, u, hu, cv, w


def _rms_blocks(blocks):
    ss = sum(_rsum(b * b) for b in blocks)
    return lax.rsqrt(ss / (BLK * len(blocks)) + EPS)


def mixer_post_fwd(o, proj, w_sc, g_a, g_c):
    S = o.shape[1]
    tm = 512

    def body(o_ref, pr_ref, ph_ref, w_ref, ga_ref, gc_ref, m_ref):
        i = pl.program_id(0)
        ob, gb, _, _, _, _, cv, _ = _mixer_recompute(i, o_ref, pr_ref, ph_ref, w_ref)
        conv = [gb[q] * cv[q] for q in range(4)]
        ra, rc = _rms_blocks(ob), _rms_blocks(conv)
        for q in range(4):
            sl = slice(q * BLK, (q + 1) * BLK)
            m_ref[:, q * BLK:(q + 1) * BLK] = (ob[q] * ra * ga_ref[:, sl]).astype(BF16)
            m_ref[:, ATTN_W + q * BLK:ATTN_W + (q + 1) * BLK] = (conv[q] * rc * gc_ref[:, sl]).astype(BF16)

    conv_s, halo_s, ob_s = _mixer_specs(S, tm)
    full = lambda r, c: pl.BlockSpec((r, c), lambda i: (0, 0))
    return pl.pallas_call(
        body, grid=(S // tm,), name="mixer_post_fwd",
        in_specs=[ob_s, conv_s, halo_s, full(3, 512), full(1, 512), full(1, 512)],
        out_specs=pl.BlockSpec((tm, D_MODEL), lambda i: (i, 0)), out_shape=SDS((S, D_MODEL), BF16),
        compiler_params=_cp("parallel"))(o, proj, proj, w_sc, g_a, g_c)


def mixer_post_bwd_a(dmixed, o, proj, w_sc, g_a, g_c, job=None):
    S = o.shape[1]
    tm = 512

    def body(dm_ref, o_ref, pr_ref, ph_ref, w_ref, ga_ref, gc_ref, do_ref, dgb_ref, dcv_ref, dga_ref, dgc_ref):
        i = pl.program_id(0)
        ob, gb, _, _, _, _, cv, _ = _mixer_recompute(i, o_ref, pr_ref, ph_ref, w_ref)
        conv = [gb[q] * cv[q] for q in range(4)]
        ra, rc = _rms_blocks(ob), _rms_blocks(conv)

        @pl.when(i == 0)
        def _():
            dga_ref[...] = jnp.zeros_like(dga_ref)
            dgc_ref[...] = jnp.zeros_like(dgc_ref)

        for blocks, r, g_ref, off, dg_ref, is_attn in ((ob, ra, ga_ref, 0, dga_ref, True), (conv, rc, gc_ref, ATTN_W, dgc_ref, False)):
            xh = [blocks[q] * r for q in range(4)]
            dm = [dm_ref[:, off + q * BLK:off + (q + 1) * BLK].astype(F32) for q in range(4)]
            gd = [dm[q] * g_ref[:, q * BLK:(q + 1) * BLK] for q in range(4)]
            mean = sum(_rsum(gd[q] * xh[q]) for q in range(4)) / (4 * BLK)
            for q in range(4):
                dg_ref[:, q * BLK:(q + 1) * BLK] += _colsum(dm[q] * xh[q])
                dx = r * (gd[q] - xh[q] * mean)
                if is_attn:
                    do_ref[q] = dx
                else:
                    dgb_ref[q] = dx * cv[q]
                    dcv_ref[q] = dx * gb[q]

    conv_s, halo_s, ob_s = _mixer_specs(S, tm)
    full = lambda r, c: pl.BlockSpec((r, c), lambda i: (0, 0))
    return _pcall(
        body, [dmixed, o, proj, proj, w_sc, g_a, g_c], grid=(S // tm,), name="mixer_post_bwd_a",
        in_specs=[pl.BlockSpec((tm, D_MODEL), lambda i: (i, 0)), ob_s, conv_s, halo_s, full(3, 512), full(1, 512), full(1, 512)],
        out_specs=[ob_s, ob_s, ob_s, full(1, 512), full(1, 512)],
        out_shape=[SDS((4, S, BLK), F32)] * 3 + [SDS((1, 512), F32)] * 2, sem=("arbitrary",), job=job)


def mixer_post_bwd_b(dproj, dgb, dcv, proj, w_sc):
    S = proj.shape[2]
    tm = 512
    last = S // 8 - 1

    def body(dp_in, dgb_ref, dcv_ref, dn_ref, pr_ref, w_ref, dp_ref, dw_ref):
        i = pl.program_id(0)
        keep_next = jnp.where(i < pl.num_programs(0) - 1, 1.0, 0.0)

        @pl.when(i == 0)
        def _():
            dw_ref[...] = jnp.zeros_like(dw_ref)

        for q in range(4):
            sl = slice(q * BLK, (q + 1) * BLK)
            gc, xi = pr_ref[4 + q].astype(F32), pr_ref[8 + q].astype(F32)
            u = gc * xi
            dcv = dcv_ref[q]
            dn = dn_ref[q] * keep_next
            d1, d2 = _shift_up(dcv, 1, dn), _shift_up(dcv, 2, dn)
            w0, w1, w2 = (w_ref[k:k + 1, sl] for k in range(3))
            du = dcv * w2 + d1 * w1 + d2 * w0
            dw_ref[0:1, sl] += _colsum(d2 * u)
            dw_ref[1:2, sl] += _colsum(d1 * u)
            dw_ref[2:3, sl] += _colsum(dcv * u)
            dp_ref[q] = dgb_ref[q].astype(BF16)
            dp_ref[4 + q] = (du * xi).astype(BF16)
            dp_ref[8 + q] = (du * gc).astype(BF16)

    conv_s, _, ob_s = _mixer_specs(S, tm)
    nxt = pl.BlockSpec((4, 8, BLK), lambda i: (0, jnp.minimum((i + 1) * (tm // 8), last), 0))
    full = lambda r, c: pl.BlockSpec((r, c), lambda i: (0, 0))
    return pl.pallas_call(
        body, grid=(S // tm,), name="mixer_post_bwd_b",
        in_specs=[pl.BlockSpec(memory_space=pl.ANY), ob_s, ob_s, nxt, conv_s, full(3, 512)],
        out_specs=[conv_s, full(3, 512)],
        out_shape=[SDS(dproj.shape, BF16), SDS((3, 512), F32)],
        input_output_aliases={0: 0}, compiler_params=_cp("arbitrary"))(dproj, dgb, dcv, dcv, proj, w_sc)


def xattn_fwd(q, k, v):
    S = q.shape[0]
    tm = 512
    scale = MEM_HD ** -0.5

    def body(q_ref, k_ref, v_ref, o_ref):
        for h in range(N_MEM_HEADS):
            sl = slice(h * MEM_HD, (h + 1) * MEM_HD)
            s = _dot_nt(q_ref[:, sl], k_ref[:, sl]) * scale
            p = jnp.exp(s - jnp.max(s, axis=1, keepdims=True))
            p = p / _rsum(p)
            o_ref[:, sl] = _dot(p.astype(BF16), v_ref[:, sl]).astype(BF16)

    row = pl.BlockSpec((tm, D_MODEL), lambda i: (i, 0))
    kv = pl.BlockSpec(k.shape, lambda i: (0, 0))
    return pl.pallas_call(body, grid=(S // tm,), name="xattn_fwd", in_specs=[row, kv, kv], out_specs=row,
                          out_shape=SDS((S, D_MODEL), BF16), compiler_params=_cp("parallel"))(q, k, v)


def xattn_bwd(q, k, v, do):
    S = q.shape[0]
    tm = 512
    scale = MEM_HD ** -0.5

    def body(q_ref, k_ref, v_ref, do_ref, dq_ref, dk_ref, dv_ref):
        @pl.when(pl.program_id(0) == 0)
        def _():
            dk_ref[...] = jnp.zeros_like(dk_ref)
            dv_ref[...] = jnp.zeros_like(dv_ref)

        for h in range(N_MEM_HEADS):
            sl = slice(h * MEM_HD, (h + 1) * MEM_HD)
            qh, kh, vh, doh = q_ref[:, sl], k_ref[:, sl], v_ref[:, sl], do_ref[:, sl]
            s = _dot_nt(qh, kh) * scale
            p = jnp.exp(s - jnp.max(s, axis=1, keepdims=True))
            p = p / _rsum(p)
            dp = _dot_nt(doh, vh)
            ds = (p * (dp - _rsum(p * dp)) * scale).astype(BF16)
            dq_ref[:, sl] = _dot(ds, kh).astype(BF16)
            dk_ref[:, sl] += _dot_tn(ds, qh)
            dv_ref[:, sl] += _dot_tn(p.astype(BF16), doh)

    row = pl.BlockSpec((tm, D_MODEL), lambda i: (i, 0))
    kv = pl.BlockSpec(k.shape, lambda i: (0, 0))
    return pl.pallas_call(body, grid=(S // tm,), name="xattn_bwd", in_specs=[row, kv, kv, row], out_specs=[row, kv, kv],
                          out_shape=[SDS((S, D_MODEL), BF16), SDS(k.shape, F32), SDS(k.shape, F32)],
                          compiler_params=_cp("arbitrary"))(q, k, v, do)


FFN_TM, FFN_TC = 512, 1408


def _ffn_specs(S):
    tm, tc = FFN_TM, FFN_TC
    blk = pl.BlockSpec((2, tm, tc), lambda j, i: (0, i, j))
    nxt = pl.BlockSpec((2, 16, tc), lambda j, i: (0, jnp.minimum((i + 1) * (tm // 16), S // 16 - 1), j))
    wsp = pl.BlockSpec((2, 3, tc), lambda j, i: (0, 0, j))
    bsp = pl.BlockSpec((2, 1, tc), lambda j, i: (0, 0, j))
    return blk, nxt, wsp, bsp


def up_act_fwd(h, w_up, w_fc, b_fc, job=None):
    S, K = h.shape
    tm, tc = FFN_TM, FFN_TC
    nj = D_FF // tc

    def body(h_ref, wg_ref, wv_ref, w_ref, b_ref, pre_ref, gv_ref, a_ref, halo):
        i = pl.program_id(1)
        hv = h_ref[...]
        res = []
        for half, wt_ref in enumerate((wg_ref, wv_ref)):
            u = _dot(hv, wt_ref[...])
            hu = jnp.where(i > 0, halo[half], 0.0)
            halo[half] = u[tm - 8:, :]
            w0, w1, w2 = (w_ref[half, k:k + 1, :] for k in range(3))
            pre_ref[half] = u.astype(BF16)
            res.append(_conv3(u, hu, w0, w1, w2) + b_ref[half])
        g, v = res
        gv_ref[0] = g.astype(BF16)
        gv_ref[1] = v.astype(BF16)
        a_ref[...] = (g * jax.nn.sigmoid(g) * v).astype(BF16)

    blk = pl.BlockSpec((2, tm, tc), lambda j, i: (0, i, j))
    return _pcall(
        body, [h, w_up, w_up, w_fc, b_fc], grid=(nj, S // tm), name="up_act_fwd",
        in_specs=[pl.BlockSpec((tm, K), lambda j, i: (i, 0)), pl.BlockSpec((K, tc), lambda j, i: (0, j)),
                  pl.BlockSpec((K, tc), lambda j, i: (0, j + nj)), pl.BlockSpec((2, 3, tc), lambda j, i: (0, 0, j)),
                  pl.BlockSpec((2, 1, tc), lambda j, i: (0, 0, j))],
        out_specs=[blk, blk, pl.BlockSpec((tm, tc), lambda j, i: (i, j))],
        out_shape=[SDS((2, S, D_FF), BF16), SDS((2, S, D_FF), BF16), SDS((S, D_FF), BF16)],
        scratch=[pltpu.VMEM((2, 8, tc), F32)], sem=("parallel", "arbitrary"), job=job)


def ffn_act_bwd(dx, w_down, up, up_pre, w_fc, job=None):
    S = up.shape[1]

    def act_grads(da, g, v):
        sg = jax.nn.sigmoid(g)
        return da * v * (sg * (1.0 + g * (1.0 - sg))), da * g * sg

    def body(dx_ref, dxn_ref, wd_ref, gv_ref, gvn_ref, up_ref, w_ref, dp_ref, dw_ref, db_ref):
        i = pl.program_id(1)
        keep_next = jnp.where(i < pl.num_programs(1) - 1, 1.0, 0.0)

        @pl.when(i == 0)
        def _():
            dw_ref[...] = jnp.zeros_like(dw_ref)
            db_ref[...] = jnp.zeros_like(db_ref)

        wd = wd_ref[...]
        da = _dot_nt(dx_ref[...].astype(BF16), wd)
        dan = _dot_nt(dxn_ref[...].astype(BF16), wd) * keep_next
        here = act_grads(da, gv_ref[0].astype(F32), gv_ref[1].astype(F32))
        after = act_grads(dan, gvn_ref[0].astype(F32), gvn_ref[1].astype(F32))
        for half in range(2):
            d, dn = here[half], after[half]
            u = up_ref[half].astype(F32)
            d1, d2 = _shift_up(d, 1, dn), _shift_up(d, 2, dn)
            w0, w1, w2 = (w_ref[half, k:k + 1, :] for k in range(3))
            dp_ref[half] = (d * w2 + d1 * w1 + d2 * w0).astype(BF16)
            dw_ref[half, 0:1, :] += _colsum(d2 * u)
            dw_ref[half, 1:2, :] += _colsum(d1 * u)
            dw_ref[half, 2:3, :] += _colsum(d * u)
            db_ref[half] += _colsum(d)

    tm, tc = FFN_TM, FFN_TC
    blk, nxt, wsp, bsp = _ffn_specs(S)
    K = dx.shape[1]
    rows = pl.BlockSpec((tm, K), lambda j, i: (i, 0))
    rows_next = pl.BlockSpec((16, K), lambda j, i: (jnp.minimum((i + 1) * (tm // 16), S // 16 - 1), 0))
    return _pcall(body, [dx, dx, w_down, up, up, up_pre, w_fc], grid=(D_FF // tc, S // tm), name="ffn_act_bwd",
                  in_specs=[rows, rows_next, pl.BlockSpec((tc, K), lambda j, i: (j, 0)), blk, nxt, blk, wsp],
                  out_specs=[blk, wsp, bsp],
                  out_shape=[SDS((2, S, D_FF), BF16), SDS((2, 3, D_FF), F32), SDS((2, 1, D_FF), F32)],
                  sem=("parallel", "arbitrary"), job=job)


def local_step(x, mem, target, wb, ws, comm=None):
    S = x.shape[0]
    assert S % 2048 == 0
    if comm is None:
        comm = NoComm()
    else:
        wb = {}
    ws = dict(ws)

    bias = bias_tables(ws["rel_bias"])
    h1 = rmsnorm_fwd(x, ws["g_mix"], "norm_mix", job=comm.job("norm_mix"))
    comm.landed("norm_mix", wb, ws)
    w_fc = ws["w_ffn_conv"].reshape(3, 2, D_FF).transpose(1, 0, 2)
    b_fc = ws["b_ffn_conv"].reshape(2, 1, D_FF)
    proj = mm_nn(h1, wb["w_in"], "proj", out_dtype=BF16, out_bw=BLK, tn=1536, job=comm.job("proj"))
    comm.landed("proj", wb)
    qkv = proj.reshape(2, 3, 4, S, BLK)
    o, lse = attn_fwd(qkv, bias, job=comm.job("attn_fwd"))
    comm.landed("attn_fwd", wb)
    proj4 = proj.reshape(2, 12, S, BLK)
    mixed = mixer_post_fwd(o, proj4, ws["w_short_conv"], ws["g_attn_out"], ws["g_conv_out"])
    x1, h2 = mm_nn_rows(mixed, wb["w_out"], x, ws["g_xattn"], "out_proj")
    mem_n = rmsnorm_fwd(mem, ws["g_mem"], "norm_mem")
    xq = mm_nn(h2, wb["w_xq"], "xq", out_dtype=BF16)
    xk = mm_nn(mem_n, wb["w_xk"], "xk", out_dtype=BF16, tn=1024)
    xv = mm_nn(mem_n, wb["w_xv"], "xv", out_dtype=BF16, tn=1024)
    xo = xattn_fwd(xq, xk, xv)
    x2, h3 = mm_nn_rows(xo, wb["w_xo"], x1, ws["g_ffn"], "xo_proj")
    up_pre, up, act = up_act_fwd(h3, wb["w_up"], w_fc, b_fc, job=comm.job("up_act_fwd"))
    comm.landed("up_act_fwd", wb)
    loss, dx3, dg_final = mm_nn_rows(act, wb["w_down"], x2, ws["g_final"].reshape(1, -1), "down_proj", target=target)

    gb, gs = {}, {"g_final": dg_final}
    gb["w_down"] = mm_tn(act, dx3, None, "dw_down", tm=1408, tn=1024)
    d_up_pre, dw_fc, db_fc = ffn_act_bwd(dx3, wb["w_down"], up, up_pre, w_fc, job=comm.job("ffn_act_bwd", gb))
    comm.landed("ffn_act_bwd")
    gs["b_ffn_conv"] = db_fc.reshape(1, 2 * D_FF)
    gs["w_ffn_conv"] = dw_fc.transpose(1, 0, 2).reshape(3, 2 * D_FF)
    gb["w_up"] = mm_tn(h3, d_up_pre, D_FF, "dw_up", shards=4, tn=1408, job=comm.job("dw_up"))
    comm.landed("dw_up")
    dx2, gs["g_ffn"] = mm_nt_norm_bwd(d_up_pre, D_FF, wb["w_up"], x2, ws["g_ffn"], dx3, "d_h3", tm=512,
                                      job=comm.job("d_h3", gb))
    comm.landed("d_h3")
    gb["w_xo"] = mm_tn(xo, dx2, None, "dw_xo")
    dxo = mm_nt(dx2, None, wb["w_xo"], "d_xo", out_dtype=BF16)
    dxq, dxk, dxv = xattn_bwd(xq, xk, xv, dxo)
    gb["w_xq"] = mm_tn(h2, dxq, None, "dw_xq")
    gb["w_xk"] = mm_tn(mem_n, dxk, None, "dw_xk", tn=1024)
    gb["w_xv"] = mm_tn(mem_n, dxv, None, "dw_xv", tn=1024)
    dmem_n = mm_nt(dxk, None, wb["w_xk"], "d_memk", tn=1024)
    dmem_n = mm_nt(dxv, None, wb["w_xv"], "d_memv", res=dmem_n, tn=1024)
    gs["g_mem"] = rmsnorm_gain_grad(mem, dmem_n, "norm_mem_bwd")
    dx1, gs["g_xattn"] = mm_nt_norm_bwd(dxq, None, wb["w_xq"], x1, ws["g_xattn"], dx2, "d_h2")
    gb["w_out"] = mm_tn(mixed, dx1, None, "dw_out")
    dmixed = mm_nt(dx1, None, wb["w_out"], "d_mixed", out_dtype=BF16)
    do, dgb, dcv, gs["g_attn_out"], gs["g_conv_out"] = mixer_post_bwd_a(
        dmixed, o, proj4, ws["w_short_conv"], ws["g_attn_out"], ws["g_conv_out"], job=comm.job("mixer_post_bwd_a", gb))
    comm.landed("mixer_post_bwd_a")
    dproj, ds_sum = attn_bwd(qkv, bias, o, lse, do, job=comm.job("attn_bwd"))
    comm.landed("attn_bwd")
    dproj, gs["w_short_conv"] = mixer_post_bwd_b(dproj.reshape(2, 12, S, BLK), dgb, dcv, proj4, ws["w_short_conv"])
    dproj = dproj.reshape(24, S, BLK)
    gb["w_in"] = mm_tn(h1, dproj, BLK, "dw_in", shards=4, tn=768, job=comm.job("dw_in"))
    comm.landed("dw_in")
    gs["rel_bias"] = bias_tables_bwd(ds_sum, job=comm.job("bias_tables_bwd", gb))
    comm.landed("bias_tables_bwd")
    grad_x, gs["g_mix"] = mm_nt_norm_bwd(dproj, BLK, wb["w_in"], x, ws["g_mix"], dx1, "d_h1", tm=512,
                                         job=comm.job("d_h1", gb))
    comm.landed("d_h1")
    return loss, grad_x, comm.finish(gb), gs


def _adam_update(w_ref, g_ref, m_ref, v_ref, d_ref, nm_ref, nv_ref):
    gv = g_ref[...]
    mn = ADAM_B1 * m_ref[...] + (1.0 - ADAM_B1) * gv
    vn = ADAM_B2 * v_ref[...] + (1.0 - ADAM_B2) * (gv * gv)
    m_hat = mn / (1.0 - ADAM_B1 ** ADAM_STEP)
    v_hat = vn / (1.0 - ADAM_B2 ** ADAM_STEP)
    d_ref[...] = -ADAM_LR * (m_hat / (jnp.sqrt(v_hat) + ADAM_EPS) + ADAM_WD * w_ref[...])
    nm_ref[...] = mn
    nv_ref[...] = vn


def adamw(ws, gs, ms, vs, name, job=None):
    n = len(ws)
    R, C = ws[0].shape
    tr = R // 4

    def body(*refs):
        for p in range(n):
            _adam_update(*(refs[q * n + p] for q in range(7)))

    blk = pl.BlockSpec((tr, C), lambda i: (i, 0))
    outs = _pcall(body, [*ws, *gs, *ms, *vs], grid=(4,), name=name, in_specs=[blk] * (4 * n), out_specs=[blk] * (3 * n),
                  out_shape=[SDS((R, C), F32)] * (3 * n), sem=("parallel",), job=job)
    return outs[:n], outs[n:2 * n], outs[2 * n:]


BIG = ("w_in", "w_out", "w_xq", "w_xk", "w_xv", "w_xo", "w_up", "w_down")
COL_SHARDED = ("w_in", "w_up")
ANY = pl.BlockSpec(memory_space=pl.ANY)


def _place():
    x, y, c = lax.axis_index("x"), lax.axis_index("y"), lax.axis_index("c")
    chips = [(1 - x, y), (x, 1 - y), (1 - x, 1 - y)]
    return x, y, c, chips


def _window(full, name, R, C, shard, half):
    r0, nr = (0, R) if half is None else (half * (R // 2), R // 2)
    if name in COL_SHARDED:
        return full.at[pl.ds(r0, nr), pl.ds(shard * C, C)]
    return full.at[pl.ds(shard * R + r0, nr), :]


def same_shape_groups(names, arrays):
    groups = {}
    for n in names:
        groups.setdefault(tuple(arrays[n].shape), []).append(n)
    return list(groups.values())


def place_shards(ws, mine_arr, col, name):
    n = len(ws)
    R, C = ws[0].shape

    def body(s_ref, *refs):
        for p in range(n):
            refs[n + p][...] = refs[p][...].astype(BF16)

    grid_spec = pltpu.PrefetchScalarGridSpec(
        num_scalar_prefetch=1, grid=(1,), in_specs=[pl.BlockSpec((R, C), lambda i, s: (0, 0))] * n,
        out_specs=[pl.BlockSpec((R, C), (lambda i, s: (0, s[0])) if col else (lambda i, s: (s[0], 0)))] * n)
    return pl.pallas_call(body, grid_spec=grid_spec, name=name,
                          out_shape=[SDS((R, 4 * C) if col else (4 * R, C), BF16)] * n,
                          compiler_params=_cp("arbitrary"))(mine_arr, *ws)


def _gather_jobs(names, shapes):
    nw = len(names)

    def start(full, sems):
        send, recv, fsend, frecv = sems
        x, y, c, chips = _place()
        mine = 2 * x + y
        for w, n in enumerate(names):
            R, C = shapes[w]
            own = _window(full[w], n, R, C, mine, c)
            for j, chip in enumerate(chips):
                pltpu.make_async_remote_copy(src_ref=own, dst_ref=own, send_sem=send.at[w, j], recv_sem=recv.at[w, j],
                                             device_id=(*chip, c), device_id_type=MESH).start()

    def mid(full, sems, lo=0, hi=nw):
        send, recv, fsend, frecv = sems
        x, y, c, chips = _place()
        sib = (x, y, 1 - c)
        for w, n in list(enumerate(names))[lo:hi]:
            R, C = shapes[w]
            for j, chip in enumerate(chips):
                landed = _window(full[w], n, R, C, 2 * chip[0] + chip[1], c)
                pltpu.make_async_remote_copy(src_ref=landed, dst_ref=landed, send_sem=send.at[w, j], recv_sem=recv.at[w, j],
                                             device_id=(*chip, c), device_id_type=MESH).wait_recv()
                pltpu.make_async_remote_copy(src_ref=landed, dst_ref=landed, send_sem=fsend.at[w, j],
                                             recv_sem=frecv.at[w, j], device_id=sib, device_id_type=MESH).start()

    def finish(full, sems):
        send, recv, fsend, frecv = sems
        x, y, c, chips = _place()
        mine = 2 * x + y
        sib = (x, y, 1 - c)
        for w, n in enumerate(names):
            R, C = shapes[w]
            own = _window(full[w], n, R, C, mine, c)
            for j, chip in enumerate(chips):
                landed = _window(full[w], n, R, C, 2 * chip[0] + chip[1], c)
                other = _window(full[w], n, R, C, 2 * chip[0] + chip[1], 1 - c)
                pltpu.make_async_remote_copy(src_ref=other, dst_ref=other, send_sem=fsend.at[w, j], recv_sem=frecv.at[w, j],
                                             device_id=sib, device_id_type=MESH).wait_recv()
                pltpu.make_async_remote_copy(src_ref=own, dst_ref=own, send_sem=send.at[w, j], recv_sem=recv.at[w, j],
                                             device_id=(*chip, c), device_id_type=MESH).wait_send()
                pltpu.make_async_remote_copy(src_ref=landed, dst_ref=landed, send_sem=fsend.at[w, j],
                                             recv_sem=frecv.at[w, j], device_id=sib, device_id_type=MESH).wait_send()

    return start, mid, finish, [pltpu.SemaphoreType.DMA((nw, 3))] * 4


class CommJob:
    def __init__(self, ins, out_shapes, inplace, start, finish, sems, mid=None):
        self.ins, self.out_shapes, self.inplace = list(ins), list(out_shapes), inplace
        self.start, self.mid, self.finish, self.sems = start, mid, finish, list(sems)

    def run(self, phase, ins, outs, sems):
        if phase == 0:
            self.start(ins, outs, sems)
        elif phase == 1:
            if self.mid is not None:
                self.mid(ins, outs, sems)
        else:
            self.finish(ins, outs, sems)


class Carried:
    def __init__(self, job, n_in, n_out, n_scratch, steps):
        self.job, self.n_in, self.n_out, self.n_scratch, self.steps = job, n_in, n_out, n_scratch, steps
        self.nji = len(job.ins) if job else 0
        self.njo = len(job.out_shapes) if job else 0

    def in_specs(self):
        return [ANY] * self.nji

    def out_specs(self):
        return [ANY] * self.njo

    def out_shapes(self):
        return list(self.job.out_shapes) if self.job else []

    def aliases(self):
        if not (self.job and self.job.inplace):
            return {}
        return {self.n_in + i: self.n_out + i for i in range(self.nji)}

    def sems(self):
        return list(self.job.sems) if self.job else []

    def args(self):
        return list(self.job.ins) if self.job else []

    def split(self, refs):
        a = self.n_in
        b = a + self.nji
        c = b + self.n_out
        d = c + self.njo
        e = d + self.n_scratch
        return refs[:a] + refs[b:c] + refs[d:e], (refs[a:b], refs[c:d], refs[e:])

    def phase(self, phase, step, jrefs):
        if self.job is None:
            return
        at = {0: 0, 1: max(self.steps - 2, 0), 2: self.steps - 1}[phase]

        @pl.when(step == at)
        def _():
            self.job.run(phase, *jrefs)


def _pcall(body, args, *, grid, in_specs, out_specs, out_shape, name, sem, scratch=(), aliases=None, job=None):
    n_in, n_out = len(args), len(out_shape)
    if job is None:
        return pl.pallas_call(
            body, grid=grid, in_specs=list(in_specs), out_specs=list(out_specs), out_shape=list(out_shape), name=name,
            scratch_shapes=list(scratch), input_output_aliases=dict(aliases or {}), compiler_params=_cp(*sem))(*args)
    total = int(np.prod(grid))
    car = Carried(job, n_in, n_out, len(scratch), total)

    def wrapped(*refs):
        main, jrefs = car.split(refs)
        lin = pl.program_id(0)
        for ax in range(1, len(grid)):
            lin = lin * grid[ax] + pl.program_id(ax)
        car.phase(0, lin, jrefs)
        body(*main)
        car.phase(1, lin, jrefs)
        car.phase(2, lin, jrefs)

    res = pl.pallas_call(
        wrapped, grid=grid, in_specs=list(in_specs) + car.in_specs(), out_specs=list(out_specs) + car.out_specs(),
        out_shape=list(out_shape) + car.out_shapes(), name=name, scratch_shapes=list(scratch) + car.sems(),
        input_output_aliases={**dict(aliases or {}), **car.aliases()},
        compiler_params=_cp(*(["arbitrary"] * len(grid))))(*args, *car.args())
    job.landed = list(res[n_out:])
    return list(res[:n_out])


def gather_job(placed, names):
    shapes = []
    for n in names:
        R, C = placed[n].shape
        shapes.append((R, C // 4) if n in COL_SHARDED else (R // 4, C))
    start, mid, finish, sems = _gather_jobs(names, shapes)
    arrays = [placed[n] for n in names]
    early = max(len(names) - 1, 1)

    def last(i, o, s):
        mid(o, s, early, len(names))
        finish(o, s)

    return CommJob(arrays, [SDS(a.shape, a.dtype) for a in arrays], True,
                   lambda i, o, s: start(o, s), last, sems, mid=lambda i, o, s: mid(o, s, 0, early))


def gather_rows_job(placed):
    widths = [a.shape[1] // 4 for a in placed]

    def copies(outs, sems):
        x, y, c, chips = _place()
        mine = 2 * x + y
        cps = []
        for w, ref in enumerate(outs):
            own = ref.at[:, pl.ds(mine * widths[w], widths[w])]
            for j, chip in enumerate(chips):
                theirs = ref.at[:, pl.ds((2 * chip[0] + chip[1]) * widths[w], widths[w])]
                kw = dict(send_sem=sems[0].at[w, j], recv_sem=sems[1].at[w, j], device_id=(*chip, c), device_id_type=MESH)
                cps.append((pltpu.make_async_remote_copy(src_ref=own, dst_ref=own, **kw),
                            pltpu.make_async_remote_copy(src_ref=theirs, dst_ref=theirs, **kw)))
        return cps

    def start(ins, outs, sems):
        for send, _ in copies(outs, sems):
            send.start()

    def finish(ins, outs, sems):
        for send, recv in copies(outs, sems):
            recv.wait_recv()
            send.wait_send()

    return CommJob(placed, [SDS(a.shape, a.dtype) for a in placed], True, start, finish,
                   [pltpu.SemaphoreType.DMA((len(placed), 3))] * 2)


def merge_jobs(a, b):
    assert a.inplace and b.inplace
    na, nb_, nsa = len(a.ins), len(b.ins), len(a.sems)

    def phase(which):
        def run(ins, outs, sems):
            for job, i, o, s in ((a, ins[:na], outs[:na], sems[:nsa]), (b, ins[na:], outs[na:], sems[nsa:])):
                fn = getattr(job, which)
                if fn is not None:
                    fn(i, o, s)
        return run

    merged = CommJob(a.ins + b.ins, a.out_shapes + b.out_shapes, True, phase("start"), phase("finish"), a.sems + b.sems,
                     mid=phase("mid"))
    merged.parts = (a, b, na)
    return merged


def split_landed(merged):
    a, b, na = merged.parts
    a.landed, b.landed = merged.landed[:na], merged.landed[na:]


def pair_exchange_job(grads, names):
    shapes = [grads[n].shape for n in names]

    def copies(ins, outs, sems):
        x, y, c, _ = _place()
        return [pltpu.make_async_remote_copy(
            src_ref=ins[w].at[:, pl.ds((1 - c) * (shapes[w][1] // 2), shapes[w][1] // 2), :], dst_ref=outs[w],
            send_sem=sems[0].at[w], recv_sem=sems[1].at[w], device_id=(x, y, 1 - c), device_id_type=MESH)
            for w in range(len(names))]

    def start(ins, outs, sems):
        for cp in copies(ins, outs, sems):
            cp.start()

    def finish(ins, outs, sems):
        for cp in copies(ins, outs, sems):
            cp.wait()

    return CommJob([grads[n] for n in names], [SDS((4, s[1] // 2, s[2]), F32) for s in shapes], False, start, finish,
                   [pltpu.SemaphoreType.DMA((len(names),))] * 2)


def chip_exchange_job(p16, names):
    shapes = [p16[n].shape for n in names]

    def copies(ins, outs, sems):
        x, y, c, chips = _place()
        return [pltpu.make_async_remote_copy(
            src_ref=ins[w].at[2 * chip[0] + chip[1]], dst_ref=outs[w].at[j],
            send_sem=sems[0].at[w, j], recv_sem=sems[1].at[w, j], device_id=(*chip, c), device_id_type=MESH)
            for w in range(len(names)) for j, chip in enumerate(chips)]

    def start(ins, outs, sems):
        for cp in copies(ins, outs, sems):
            cp.start()

    def finish(ins, outs, sems):
        for cp in copies(ins, outs, sems):
            cp.wait()

    return CommJob([p16[n] for n in names], [SDS((3,) + tuple(s[1:]), BF16) for s in shapes], False, start, finish,
                   [pltpu.SemaphoreType.DMA((len(names), 3))] * 2)


def half_exchange_job(full, names):
    shapes = [full[n].shape for n in names]

    def copies(outs, sems):
        x, y, c, _ = _place()
        cps = []
        for w in range(len(names)):
            Rh = shapes[w][0] // 2
            rows = outs[w].at[pl.ds(c * Rh, Rh), :]
            other = outs[w].at[pl.ds((1 - c) * Rh, Rh), :]
            cps.append((pltpu.make_async_remote_copy(src_ref=rows, dst_ref=rows, send_sem=sems[0].at[w], recv_sem=sems[1].at[w],
                                                     device_id=(x, y, 1 - c), device_id_type=MESH),
                        pltpu.make_async_remote_copy(src_ref=other, dst_ref=other, send_sem=sems[0].at[w], recv_sem=sems[1].at[w],
                                                     device_id=(x, y, 1 - c), device_id_type=MESH)))
        return cps

    def start(ins, outs, sems):
        for send, _ in copies(outs, sems):
            send.start()

    def finish(ins, outs, sems):
        for send, recv in copies(outs, sems):
            recv.wait_recv()
            send.wait_send()

    arrays = [full[n] for n in names]
    return CommJob(arrays, [SDS(a.shape, a.dtype) for a in arrays], True, start, finish,
                   [pltpu.SemaphoreType.DMA((len(names),))] * 2)


def allreduce_small(parts, name, job=None):
    n = len(parts)
    nji = len(job.ins) if job else 0
    njo = len(job.out_shapes) if job else 0

    def body(*refs):
        ins, jins = refs[:n], refs[n:n + nji]
        outs, jouts = refs[n + nji:2 * n + nji], refs[2 * n + nji:2 * n + nji + njo]
        bufs = refs[2 * n + nji + njo:3 * n + nji + njo]
        send, recv = refs[3 * n + nji + njo:3 * n + nji + njo + 2]
        jsems = refs[3 * n + nji + njo + 2:]
        x, y, c, _ = _place()
        me = 4 * x + 2 * y + c
        if job is not None:
            job.run(0, jins, jouts, jsems)
        sends = []
        for p in range(n):
            for k in range(1, 8):
                peer = (x ^ (k >> 2), y ^ ((k >> 1) & 1), c ^ (k & 1))
                cp = pltpu.make_async_remote_copy(src_ref=ins[p], dst_ref=bufs[p].at[k], send_sem=send.at[p, k - 1],
                                                  recv_sem=recv.at[p, k - 1], device_id=peer, device_id_type=MESH)
                cp.start()
                sends.append(cp)
            bufs[p][0] = ins[p][...]
        for cp in sends:
            cp.wait_recv()
        for p in range(n):
            acc = bufs[p][me]
            for d in range(1, 8):
                acc = acc + bufs[p][d ^ me]
            outs[p][...] = acc
        for cp in sends:
            cp.wait_send()
        if job is not None:
            job.run(1, jins, jouts, jsems)
            job.run(2, jins, jouts, jsems)

    vm = pl.BlockSpec(memory_space=pltpu.VMEM)
    res = pl.pallas_call(
        body, name=name, in_specs=[vm] * n + [ANY] * nji, out_specs=[vm] * n + [ANY] * njo,
        out_shape=[SDS(a.shape, F32) for a in parts] + (list(job.out_shapes) if job else []),
        input_output_aliases={n + i: n + i for i in range(nji)} if (job and job.inplace) else {},
        scratch_shapes=[pltpu.VMEM((8,) + tuple(a.shape), F32) for a in parts] + [pltpu.SemaphoreType.DMA((n, 7))] * 2
        + (list(job.sems) if job else []),
    )(*parts, *(job.ins if job else []))
    if job is not None:
        job.landed = list(res[n:])
    return list(res[:n])


def adamw_multi(ws, gs, ms, vs, name):
    n = len(ws)

    def body(*refs):
        for p in range(n):
            _adam_update(*(refs[q * n + p] for q in range(7)))

    vm = pl.BlockSpec(memory_space=pltpu.VMEM)
    outs = pl.pallas_call(body, name=name, in_specs=[vm] * (4 * n), out_specs=[vm] * (3 * n),
                          out_shape=[SDS(a.shape, F32) for a in ws] * 3)(*ws, *gs, *ms, *vs)
    return outs[:n], outs[n:2 * n], outs[2 * n:]


def pair_sum(gs, xhs, sel_arr, name):
    n = len(gs)
    _, R, C = gs[0].shape
    Rh = R // 2

    def body(sel_ref, *refs):
        for p in range(n):
            g_ref, x_ref, p32_ref, p16_ref = (refs[q * n + p] for q in range(4))
            s = g_ref[...] + x_ref[...]
            p16_ref[...] = s.astype(BF16)

            @pl.when(pl.program_id(0) == sel_ref[0])
            def _():
                p32_ref[...] = s

    same = pl.BlockSpec((None, Rh, C), lambda s, sel: (s, 0, 0))
    grid_spec = pltpu.PrefetchScalarGridSpec(
        num_scalar_prefetch=1, grid=(4,),
        in_specs=[pl.BlockSpec((None, Rh, C), lambda s, sel: (s, sel[1], 0))] * n + [same] * n,
        out_specs=[pl.BlockSpec((Rh, C), lambda s, sel: (0, 0))] * n + [same] * n)
    outs = pl.pallas_call(body, grid_spec=grid_spec, name=name,
                          out_shape=[SDS((Rh, C), F32)] * n + [SDS((4, Rh, C), BF16)] * n,
                          compiler_params=_cp("arbitrary"))(sel_arr, *gs, *xhs)
    return outs[:n], outs[n:]


def chip_sum(p32s, ys, sel_arr, name):
    n = len(p32s)
    Rh, C = p32s[0].shape

    def body(s_ref, *refs):
        for p in range(n):
            p_ref, y_ref, o_ref = (refs[q * n + p] for q in range(3))
            o_ref[...] = ((p_ref[...] + y_ref[0].astype(F32)) + y_ref[1].astype(F32)) + y_ref[2].astype(F32)

    grid_spec = pltpu.PrefetchScalarGridSpec(
        num_scalar_prefetch=1, grid=(1,),
        in_specs=[pl.BlockSpec((Rh, C), lambda i, s: (0, 0))] * n + [pl.BlockSpec((3, Rh, C), lambda i, s: (0, 0, 0))] * n,
        out_specs=[pl.BlockSpec((Rh, C), lambda i, s: (s[1], 0))] * n)
    return pl.pallas_call(body, grid_spec=grid_spec, name=name, out_shape=[SDS((2 * Rh, C), F32)] * n,
                          compiler_params=_cp("arbitrary"))(sel_arr, *p32s, *ys)


class StepComm:
    FIRST = ("w_in",)
    SQUARE = ("w_out", "w_xq", "w_xk", "w_xv", "w_xo")
    FFN = ("w_up", "w_down")
    PLAN = {
        "proj": ("gather", SQUARE[:3]), "attn_fwd": ("gather", SQUARE[3:] + FFN[:1]), "up_act_fwd": ("gather", FFN[1:]),
        "ffn_act_bwd": ("pair", FFN[1:]), "dw_up": ("chip", FFN[1:]),
        "d_h3": ("pair", FFN[:1]), "mixer_post_bwd_a": ("pair", SQUARE), "attn_bwd": ("chip", FFN[:1] + SQUARE),
        "dw_in": ("half", FFN + SQUARE), "bias_tables_bwd": ("pair", FIRST), "d_h1": ("chip", FIRST),
    }

    def __init__(self, placed, conv_placed, sel_arr):
        self.placed, self.conv_placed, self.sel_arr = placed, conv_placed, sel_arr
        self.g, self.p32, self.p16, self.full, self.done, self.jobs = {}, {}, {}, {}, {}, {}

    def job(self, tag, gb=None):
        if tag == "norm_mix":
            self.jobs[tag] = merge_jobs(gather_job(self.placed, self.FIRST), gather_rows_job(self.conv_placed))
            return self.jobs[tag]
        kind, names = self.PLAN[tag]
        if kind == "gather":
            j = gather_job(self.placed, names)
        elif kind == "pair":
            for n in names:
                self.g[n] = gb[n] if gb[n].ndim == 3 else gb[n].reshape(4, gb[n].shape[0] // 4, gb[n].shape[1])
            j = pair_exchange_job(self.g, names)
        elif kind == "chip":
            j = chip_exchange_job(self.p16, names)
        else:
            j = half_exchange_job(self.full, names)
        self.jobs[tag] = j
        return j

    def landed(self, tag, wb=None, ws=None):
        if tag == "norm_mix":
            split_landed(self.jobs[tag])
            first, conv, _ = self.jobs[tag].parts
            wb.update(zip(self.FIRST, first.landed))
            ws.update((n, a[:3]) for n, a in zip(SMALL_SHARDED, conv.landed))
            return
        kind, names = self.PLAN[tag]
        got = dict(zip(names, self.jobs[tag].landed))
        if kind == "gather":
            wb.update(got)
        elif kind == "half":
            self.done.update(got)
        for group in same_shape_groups(names, got):
            if kind == "pair":
                p32s, p16s = pair_sum([self.g[n] for n in group], [got[n] for n in group], self.sel_arr, "pair_sum_" + group[0])
                self.p32.update(zip(group, p32s))
                self.p16.update(zip(group, p16s))
            elif kind == "chip":
                fulls = chip_sum([self.p32[n] for n in group], [got[n] for n in group], self.sel_arr, "chip_sum_" + group[0])
                self.full.update(zip(group, fulls))

    def finish(self, gb):
        return self.done

    def last_job(self):
        self.jobs["end_half"] = half_exchange_job(self.full, self.FIRST)
        return self.jobs["end_half"]

    def take_last(self):
        self.done.update(zip(self.FIRST, self.jobs["end_half"].landed))


class NoComm:
    def job(self, tag, gb=None):
        return None

    def landed(self, tag, wb=None, ws=None):
        pass

    def finish(self, gb):
        return gb


SMALL = ("rel_bias", "g_mix", "w_short_conv", "g_attn_out", "g_conv_out", "g_xattn", "g_mem", "g_ffn",
         "w_ffn_conv", "b_ffn_conv", "g_final")
SMALL_SHARDED = ("w_short_conv", "w_ffn_conv")


def kernel(x, mem, rel_bias, g_mix, w_in, w_short_conv, g_attn_out, g_conv_out, w_out, g_xattn, g_mem, w_xq, w_xk, w_xv, w_xo, g_ffn, w_up, w_ffn_conv, b_ffn_conv, w_down, g_final, loss_target, m_rel_bias, m_g_mix, m_w_in, m_w_short_conv, m_g_attn_out, m_g_conv_out, m_w_out, m_g_xattn, m_g_mem, m_w_xq, m_w_xk, m_w_xv, m_w_xo, m_g_ffn, m_w_up, m_w_ffn_conv, m_b_ffn_conv, m_w_down, m_g_final, v_rel_bias, v_g_mix, v_w_in, v_w_short_conv, v_g_attn_out, v_g_conv_out, v_w_out, v_g_xattn, v_g_mem, v_w_xq, v_w_xk, v_w_xv, v_w_xo, v_g_ffn, v_w_up, v_w_ffn_conv, v_b_ffn_conv, v_w_down, v_g_final):
    names = ("rel_bias", "g_mix", "w_in", "w_short_conv", "g_attn_out", "g_conv_out", "w_out", "g_xattn", "g_mem", "w_xq",
             "w_xk", "w_xv", "w_xo", "g_ffn", "w_up", "w_ffn_conv", "b_ffn_conv", "w_down", "g_final")
    W = dict(zip(names, (rel_bias, g_mix, w_in, w_short_conv, g_attn_out, g_conv_out, w_out, g_xattn, g_mem, w_xq, w_xk, w_xv,
                         w_xo, g_ffn, w_up, w_ffn_conv, b_ffn_conv, w_down, g_final)))
    M = dict(zip(names, (m_rel_bias, m_g_mix, m_w_in, m_w_short_conv, m_g_attn_out, m_g_conv_out, m_w_out, m_g_xattn, m_g_mem,
                         m_w_xq, m_w_xk, m_w_xv, m_w_xo, m_g_ffn, m_w_up, m_w_ffn_conv, m_b_ffn_conv, m_w_down, m_g_final)))
    V = dict(zip(names, (v_rel_bias, v_g_mix, v_w_in, v_w_short_conv, v_g_attn_out, v_g_conv_out, v_w_out, v_g_xattn, v_g_mem,
                         v_w_xq, v_w_xk, v_w_xv, v_w_xo, v_g_ffn, v_w_up, v_w_ffn_conv, v_b_ffn_conv, v_w_down, v_g_final)))
    xi, yi, ci = lax.axis_index("x"), lax.axis_index("y"), lax.axis_index("c")
    mine = 2 * xi + yi
    mine_arr = jnp.reshape(mine, (1,)).astype(jnp.int32)
    sel_arr = jnp.stack([mine, ci]).astype(jnp.int32)
    conv_placed = []
    for n in SMALL_SHARDED:
        shard = W[n][0]
        conv_placed.append(lax.dynamic_update_slice(jnp.zeros((8, 4 * shard.shape[1]), F32), shard, (0, mine * shard.shape[1])))
    shards = {n: W[n][0] for n in BIG}
    placed = {}
    for group in same_shape_groups(BIG, shards):
        outs = place_shards([shards[n] for n in group], mine_arr, group[0] in COL_SHARDED, "place_" + group[0])
        placed.update(zip(group, outs))
    comm = StepComm(placed, conv_placed, sel_arr)
    ws = {n: W[n] for n in SMALL if n not in SMALL_SHARDED}

    loss, grad_x, gfull, gs = local_step(x[0], mem[0], loss_target[0], None, ws, comm)

    def as2d(a):
        return a.reshape(1, -1) if a.ndim == 1 else a

    reduced = allreduce_small([as2d(gs[n]) for n in SMALL] + [loss], "reduce_small", job=comm.last_job())
    comm.take_last()
    gsm = dict(zip(SMALL, reduced[:-1]))
    loss = reduced[-1][0, 0]

    grads, delta, new_m, new_v = {}, {}, {}, {}
    for group in same_shape_groups(BIG, gfull):
        ds, nms, nvs = adamw([W[n][0] for n in group], [gfull[n] for n in group], [M[n][0] for n in group],
                             [V[n][0] for n in group], "adamw_" + group[0])
        for n, d, nm, nv in zip(group, ds, nms, nvs):
            grads[n], delta[n], new_m[n], new_v[n] = gfull[n][None], d[None], nm[None], nv[None]
    for n in SMALL_SHARDED:
        wid = W[n].shape[2]
        gsm[n] = lax.dynamic_slice(gsm[n], (0, mine * wid), (3, wid))

    def own(a, n):
        return a[0] if n in SMALL_SHARDED else as2d(a)

    d, nm, nv = adamw_multi([own(W[n], n) for n in SMALL], [gsm[n] for n in SMALL], [own(M[n], n) for n in SMALL],
                            [own(V[n], n) for n in SMALL], "adamw_small")
    for i, n in enumerate(SMALL):
        shape = W[n].shape
        grads[n], delta[n], new_m[n], new_v[n] = (a.reshape(shape) for a in (gsm[n], d[i], nm[i], nv[i]))
    return (loss, grad_x[None], *[grads[n] for n in names], *[delta[n] for n in names],
            *[new_m[n] for n in names], *[new_v[n] for n in names])
```

```python
import math

import numpy as np
import jax
import jax.numpy as jnp
from jax import lax
from jax.experimental import pallas as pl
from jax.experimental.pallas import tpu as pltpu

F32 = jnp.float32
BF16 = jnp.bfloat16
SDS = jax.ShapeDtypeStruct
MESH = pl.DeviceIdType.MESH

D_MODEL = 1024
ATTN_W = 512
N_HEADS = 8
HEAD_DIM = 64
HALF_HEAD = HEAD_DIM // 2
Q_SCALE = HEAD_DIM ** -0.5
BLK = 128
PATTERNS = ((128, 1), (512, 4), (2048, 16))
N_BUCKETS = 32
BUCKET_MAX_DIST = 2048
D_FF = 2816
N_MEM_HEADS = 4
MEM_HD = 256
EPS = 1e-6
NEG = -1e30
VMEM_LIMIT = 56 * 1024 * 1024

ADAM_LR, ADAM_B1, ADAM_B2, ADAM_EPS, ADAM_WD, ADAM_STEP = 0.001, 0.9, 0.999, 1e-08, 0.01, 10


def _cp(*sem):
    return pltpu.CompilerParams(dimension_semantics=sem, vmem_limit_bytes=VMEM_LIMIT)


def _dot(a, b):
    return jnp.dot(a, b, preferred_element_type=F32)


def _dot_nt(a, b):
    return lax.dot_general(a, b, (((1,), (1,)), ((), ())), preferred_element_type=F32)


def _dot_tn(a, b):
    return lax.dot_general(a, b, (((0,), (0,)), ((), ())), preferred_element_type=F32)


def _rsum(x):
    return jnp.sum(x, axis=1, keepdims=True)


def rmsnorm_fwd(x, g, name, job=None):
    S, Dm = x.shape
    tm = min(S, 512)

    def body(x_ref, g_ref, o_ref):
        xv = x_ref[...]
        r = lax.rsqrt(jnp.mean(xv * xv, axis=1, keepdims=True) + EPS)
        o_ref[...] = (xv * r * g_ref[...]).astype(o_ref.dtype)

    return _pcall(body, [x, g], grid=(S // tm,), name=name,
                  in_specs=[pl.BlockSpec((tm, Dm), lambda i: (i, 0)), pl.BlockSpec((1, Dm), lambda i: (0, 0))],
                  out_specs=[pl.BlockSpec((tm, Dm), lambda i: (i, 0))], out_shape=[SDS((S, Dm), BF16)],
                  sem=("parallel",), job=job)[0]


def rmsnorm_gain_grad(x, dh, name):
    S, Dm = x.shape
    tm = min(S, 512)

    def body(x_ref, dh_ref, dg_ref):
        xv = x_ref[...]
        xh = xv * lax.rsqrt(jnp.mean(xv * xv, axis=1, keepdims=True) + EPS)

        @pl.when(pl.program_id(0) == 0)
        def _():
            dg_ref[...] = jnp.zeros_like(dg_ref)

        dg_ref[...] += jnp.sum(dh_ref[...].astype(F32) * xh, axis=0, keepdims=True)

    row = pl.BlockSpec((tm, Dm), lambda i: (i, 0))
    return pl.pallas_call(
        body, grid=(S // tm,), name=name, in_specs=[row, row], out_specs=pl.BlockSpec((1, Dm), lambda i: (0, 0)),
        out_shape=SDS((1, Dm), F32), compiler_params=_cp("arbitrary"))(x, dh)


def _col_spec(bw, tn, rows, row_of, col_of):
    if bw is None:
        return pl.BlockSpec((rows, tn), lambda *g: (row_of(*g), col_of(*g)))
    if tn % bw == 0:
        return pl.BlockSpec((tn // bw, rows, bw), lambda *g: (col_of(*g), row_of(*g), 0))
    per = bw // tn
    return pl.BlockSpec((None, rows, tn), lambda *g: (col_of(*g) // per, row_of(*g), col_of(*g) % per))


def _read_cols(ref, bw, tn):
    if bw is None or tn % bw != 0:
        return ref[...]
    if tn == bw:
        return ref[0]
    return jnp.concatenate([ref[q] for q in range(tn // bw)], axis=1)


def _write_cols(ref, bw, tn, val):
    if bw is None or tn % bw != 0:
        ref[...] = val.astype(ref.dtype)
    else:
        for q in range(tn // bw):
            ref[q] = val[:, q * bw:(q + 1) * bw].astype(ref.dtype)


def mm_nn(a, b, name, *, out_dtype=F32, out_bw=None, tm=1024, tn=1024, job=None):
    M, K = a.shape
    N = b.shape[1]
    tm = min(tm, M)

    def body(a_ref, b_ref, o_ref):
        _write_cols(o_ref, out_bw, tn, _dot(a_ref[...].astype(BF16), b_ref[...]))

    ri, ci = (lambda i, j: i), (lambda i, j: j)
    in_specs = [pl.BlockSpec((tm, K), lambda i, j: (i, 0)), pl.BlockSpec((K, tn), lambda i, j: (0, j))]
    oshape = (M, N) if out_bw is None else (N // out_bw, M, out_bw)
    return _pcall(body, [a, b], grid=(M // tm, N // tn), name=name, in_specs=in_specs,
                  out_specs=[_col_spec(out_bw, tn, tm, ri, ci)], out_shape=[SDS(oshape, out_dtype)],
                  sem=("parallel", "parallel"), job=job)[0]


def mm_nn_pair(a, b1, b2, name, *, out_dtype=F32, tm=1024):
    M, K = a.shape
    N = b1.shape[1]
    tm = min(tm, M)

    def body(a_ref, b1_ref, b2_ref, o1_ref, o2_ref):
        av = a_ref[...].astype(BF16)
        o1_ref[...] = _dot(av, b1_ref[...]).astype(o1_ref.dtype)
        o2_ref[...] = _dot(av, b2_ref[...]).astype(o2_ref.dtype)

    w_spec = pl.BlockSpec((K, N), lambda i: (0, 0))
    o_spec = pl.BlockSpec((tm, N), lambda i: (i, 0))
    return _pcall(body, [a, b1, b2], grid=(M // tm,), name=name,
                  in_specs=[pl.BlockSpec((tm, K), lambda i: (i, 0)), w_spec, w_spec],
                  out_specs=[o_spec, o_spec], out_shape=[SDS((M, N), out_dtype)] * 2, sem=("parallel",))


def mm_nt_pair_sum(a1, b1, a2, b2, name, *, tm=1024):
    M, K = a1.shape
    N = b1.shape[0]
    tm = min(tm, M)

    def body(a1_ref, b1_ref, a2_ref, b2_ref, o_ref):
        first = _dot_nt(a1_ref[...].astype(BF16), b1_ref[...])
        o_ref[...] = _dot_nt(a2_ref[...].astype(BF16), b2_ref[...]) + first

    a_spec = pl.BlockSpec((tm, K), lambda i: (i, 0))
    w_spec = pl.BlockSpec((N, K), lambda i: (0, 0))
    return _pcall(body, [a1, b1, a2, b2], grid=(M // tm,), name=name, in_specs=[a_spec, w_spec, a_spec, w_spec],
                  out_specs=[pl.BlockSpec((tm, N), lambda i: (i, 0))], out_shape=[SDS((M, N), F32)],
                  sem=("parallel",))[0]


def mm_nn_rows(a, b, res, g, name, *, target=None, tm=512):
    M, K = a.shape
    N = b.shape[1]

    def body(*refs):
        a_ref, b_ref, r_ref, g_ref = refs[:4]
        xv = r_ref[...] + _dot(a_ref[...].astype(BF16), b_ref[...])
        r = lax.rsqrt(jnp.mean(xv * xv, axis=1, keepdims=True) + EPS)
        xh = xv * r
        if target is None:
            x_ref, h_ref = refs[4:]
            x_ref[...] = xv
            h_ref[...] = (xh * g_ref[...]).astype(BF16)
            return
        t_ref, loss_ref, dx_ref, dg_ref = refs[4:]
        err = xh * g_ref[...] - t_ref[...]
        dy = err / N
        gd = dy * g_ref[...]
        dx_ref[...] = r * (gd - xh * jnp.mean(gd * xh, axis=1, keepdims=True))

        @pl.when(pl.program_id(0) == 0)
        def _():
            dg_ref[...] = jnp.zeros_like(dg_ref)
            loss_ref[...] = jnp.zeros_like(loss_ref)

        dg_ref[...] += _colsum(dy * xh)
        loss_ref[...] += 0.5 * _colsum(jnp.mean(err * err, axis=1, keepdims=True))

    row = pl.BlockSpec((tm, N), lambda i: (i, 0))
    vec = pl.BlockSpec((1, N), lambda i: (0, 0))
    in_specs = [pl.BlockSpec((tm, K), lambda i: (i, 0)), pl.BlockSpec((K, N), lambda i: (0, 0)), row, vec]
    if target is None:
        return pl.pallas_call(body, grid=(M // tm,), name=name, in_specs=in_specs, out_specs=[row, row],
                              out_shape=[SDS((M, N), F32), SDS((M, N), BF16)], compiler_params=_cp("parallel"))(a, b, res, g)
    one = pl.BlockSpec((1, 1), lambda i: (0, 0))
    return pl.pallas_call(body, grid=(M // tm,), name=name, in_specs=in_specs + [row], out_specs=[one, row, vec],
                          out_shape=[SDS((1, 1), F32), SDS((M, N), F32), SDS((1, N), F32)],
                          compiler_params=_cp("arbitrary"))(a, b, res, g, target)


def mm_nt(a, a_bw, b, name, *, res=None, out_dtype=F32, tm=1024, tn=1024, job=None):
    if a_bw is None:
        M, K = a.shape
    else:
        M, K = a.shape[1], a.shape[0] * a_bw
    N = b.shape[0]
    tm = min(tm, M)

    def body(*refs):
        if res is None:
            a_ref, b_ref, o_ref = refs
        else:
            a_ref, b_ref, r_ref, o_ref = refs
        av = _read_cols(a_ref, a_bw, K).astype(BF16)
        acc = _dot_nt(av, b_ref[...])
        if res is not None:
            acc = acc + r_ref[...]
        o_ref[...] = acc.astype(o_ref.dtype)

    in_specs = [_col_spec(a_bw, K, tm, lambda i, j: i, lambda i, j: 0), pl.BlockSpec((tn, K), lambda i, j: (j, 0))]
    args = [a, b]
    if res is not None:
        in_specs.append(pl.BlockSpec((tm, tn), lambda i, j: (i, j)))
        args.append(res)
    return _pcall(body, args, grid=(M // tm, N // tn), name=name, in_specs=in_specs,
                  out_specs=[pl.BlockSpec((tm, tn), lambda i, j: (i, j))], out_shape=[SDS((M, N), out_dtype)],
                  sem=("parallel", "parallel"), job=job)[0]


def mm_nt_norm_bwd(a, a_bw, b, x, g, res, name, *, tm=512, job=None):
    if a_bw is None:
        M, K = a.shape
    else:
        M, K = a.shape[1], a.shape[0] * a_bw
    N = b.shape[0]

    def body(a_ref, b_ref, x_ref, g_ref, r_ref, dx_ref, dg_ref):
        dh = _dot_nt(_read_cols(a_ref, a_bw, K).astype(BF16), b_ref[...])
        xv = x_ref[...]
        r = lax.rsqrt(jnp.mean(xv * xv, axis=1, keepdims=True) + EPS)
        xh = xv * r
        gd = dh * g_ref[...]
        dx_ref[...] = r_ref[...] + r * (gd - xh * jnp.mean(gd * xh, axis=1, keepdims=True))

        @pl.when(pl.program_id(0) == 0)
        def _():
            dg_ref[...] = jnp.zeros_like(dg_ref)

        dg_ref[...] += jnp.sum(dh * xh, axis=0, keepdims=True)

    row = pl.BlockSpec((tm, N), lambda i: (i, 0))
    vec = pl.BlockSpec((1, N), lambda i: (0, 0))
    return _pcall(body, [a, b, x, g, res], grid=(M // tm,), name=name,
                  in_specs=[_col_spec(a_bw, K, tm, lambda i: i, lambda i: 0),
                            pl.BlockSpec((N, K), lambda i: (0, 0), pipeline_mode=pl.Buffered(1)), row, vec, row],
                  out_specs=[row, vec], out_shape=[SDS((M, N), F32), SDS((1, N), F32)], sem=("arbitrary",), job=job)


def mm_tn(a, b, b_bw, name, *, shards=None, tm=1024, tn=1024, ts=2048, job=None):
    S, Ka = a.shape
    N = b.shape[1] if b_bw is None else b.shape[0] * b_bw
    ts = min(ts, S)
    tm = min(tm, Ka)

    def body(a_ref, b_ref, o_ref):
        @pl.when(pl.program_id(2) == 0)
        def _():
            o_ref[...] = jnp.zeros_like(o_ref)

        bv = _read_cols(b_ref, b_bw, tn).astype(BF16)
        o_ref[...] += _dot_tn(a_ref[...].astype(BF16), bv)

    in_specs = [pl.BlockSpec((ts, tm), lambda i, j, k: (k, i)),
                _col_spec(b_bw, tn, ts, lambda i, j, k: k, lambda i, j, k: j)]
    if shards is None:
        out_spec, oshape = pl.BlockSpec((tm, tn), lambda i, j, k: (i, j)), (Ka, N)
    else:
        per = (N // shards) // tn
        out_spec = pl.BlockSpec((None, tm, tn), lambda i, j, k: (j // per, i, j % per))
        oshape = (shards, Ka, N // shards)
    return _pcall(body, [a, b], grid=(Ka // tm, N // tn, S // ts), name=name, in_specs=in_specs, out_specs=[out_spec],
                  out_shape=[SDS(oshape, F32)], sem=("parallel", "parallel", "arbitrary"), job=job)[0]


KEYS = 2 * BLK


def _bucket_tables():
    out = np.zeros((3, BLK, KEYS), np.int32)
    qi = np.arange(BLK)[:, None]
    kj = np.arange(KEYS)[None, :]
    for p, (win, dil) in enumerate(PATTERNS):
        w = win // dil
        assert w == BLK
        steps = qi + w - kj
        valid = (steps >= 0) & (steps <= w)
        dist = np.clip(steps, 0, w) * dil
        dd = np.maximum(dist, 1).astype(np.float32)
        exact = N_BUCKETS // 2
        large = exact + (np.log(dd / np.float32(exact)) / np.float32(math.log(BUCKET_MAX_DIST / exact))
                         * np.float32(N_BUCKETS - exact)).astype(np.int32)
        large = np.minimum(large, N_BUCKETS - 1)
        out[p] = np.where(valid, np.where(dist < exact, dist, large), -1)
    return out


def bias_tables(rel_bias):
    bkt = jnp.asarray(_bucket_tables())

    def body(rb_ref, bkt_ref, o_ref):
        bk = bkt_ref[...]
        for h in range(N_HEADS):
            acc = jnp.full((BLK, KEYS), NEG, F32)
            for b in range(N_BUCKETS):
                acc = jnp.where(bk == b, rb_ref[h, b], acc)
            o_ref[h] = acc

    return pl.pallas_call(
        body, grid=(3,), name="bias_tables",
        in_specs=[pl.BlockSpec(memory_space=pltpu.SMEM),
                  pl.BlockSpec((None, BLK, KEYS), lambda p: (p, 0, 0))],
        out_specs=pl.BlockSpec((None, N_HEADS, BLK, KEYS), lambda p: (p, 0, 0, 0)),
        out_shape=SDS((3, N_HEADS, BLK, KEYS), F32), compiler_params=_cp("parallel"))(rel_bias, bkt)


def bias_tables_bwd(ds_sum, job=None):
    bkt = jnp.asarray(_bucket_tables())

    def body(ds_ref, bkt_ref, o_ref):
        h = pl.program_id(0)

        @pl.when(h == 0)
        def _():
            o_ref[...] = jnp.zeros_like(o_ref)

        rows = lax.broadcasted_iota(jnp.int32, (N_HEADS, N_BUCKETS), 0)
        cols = lax.broadcasted_iota(jnp.int32, (N_HEADS, N_BUCKETS), 1)
        acc = jnp.zeros((N_HEADS, N_BUCKETS), F32)
        for b in range(N_BUCKETS):
            t = jnp.zeros((BLK, KEYS), F32)
            for p in range(3):
                t = t + jnp.where(bkt_ref[p] == b, ds_ref[p], 0.0)
            tot = jnp.sum(_rsum(t), axis=0, keepdims=True)
            acc = acc + jnp.where((rows == h) & (cols == b), tot, 0.0)
        o_ref[...] += acc

    return _pcall(
        body, [ds_sum, bkt], grid=(N_HEADS,), name="bias_tables_bwd",
        in_specs=[pl.BlockSpec((3, None, BLK, KEYS), lambda h: (0, h, 0, 0)),
                  pl.BlockSpec((3, BLK, KEYS), lambda h: (0, 0, 0))],
        out_specs=[pl.BlockSpec((N_HEADS, N_BUCKETS), lambda h: (0, 0))],
        out_shape=[SDS((N_HEADS, N_BUCKETS), F32)], sem=("arbitrary",), job=job)[0]


def _rows(start, dil):
    if dil == 1:
        return pl.ds(pl.multiple_of(start, BLK), BLK)
    return pl.ds(start, BLK, stride=dil)


GRP = 8


def _group_rows(i, dil, S):
    nb = S // (BLK * dil)
    run = min(nb, GRP)
    chunks = nb // run
    res0, b0 = (i // chunks) * (GRP // run), (i % chunks) * run
    cur = [(b0 + j % run) * (BLK * dil) + res0 + j // run for j in range(GRP)]
    t = lax.broadcasted_iota(jnp.int32, (GRP, 1, 1), 0)
    if run == nb:
        before, pen = None, jnp.where(t % run == 0, NEG, 0.0)
    else:
        before = _rows(jnp.maximum(b0 - 1, 0) * (BLK * dil) + res0, dil)
        pen = jnp.where((t == 0) & (b0 == 0), NEG, 0.0)
    return [_rows(s, dil) for s in cur], before, pen


def _load_group(ref, rows):
    return jnp.stack([ref[r, :] for r in rows])


def _with_prev(ref, before, cur_blocks):
    first = cur_blocks[:1] if before is None else ref[before, :][None].astype(cur_blocks.dtype)
    return jnp.concatenate([first, cur_blocks[:-1]], axis=0)


def _bdot_nt(a, b):
    return lax.dot_general(a, b, (((2,), (2,)), ((0,), (0,))), preferred_element_type=F32)


def _bdot(a, b):
    return lax.dot_general(a, b, (((2,), (1,)), ((0,), (0,))), preferred_element_type=F32)


def _bdot_tn(a, b):
    return lax.dot_general(a, b, (((1,), (1,)), ((0,), (0,))), preferred_element_type=F32)


def _lsum(x):
    return jnp.sum(x, axis=-1, keepdims=True)


def _widen(src, dst):
    S = src.shape[1]

    def chunk(i, carry):
        rows = pl.ds(pl.multiple_of(i * 512, 512), 512)
        for a in range(3):
            dst[a, rows, :] = src[a, rows, :].astype(F32)
        return carry

    lax.fori_loop(0, S // 512, chunk, 0)


def attn_fwd(qkv, bias, job=None):
    S = qkv.shape[3]
    nblk = S // BLK
    car = Carried(job, 2, 2, 3, 4)

    def body(*refs):
        (qkv_in, bias_ref, o_ref, lse_ref, qkv_ref, part_o, part_ml), jrefs = car.split(refs)
        car.phase(0, pl.program_id(0), jrefs)
        _widen(qkv_in, qkv_ref)
        lane = lax.broadcasted_iota(jnp.int32, (GRP, BLK, BLK), 2)
        lo = lane < HEAD_DIM
        masks = (lo, jnp.logical_not(lo))
        lov = lax.broadcasted_iota(jnp.int32, (GRP, KEYS, BLK), 2) < HEAD_DIM
        vmasks = (lov, jnp.logical_not(lov))
        prev_keys = lax.broadcasted_iota(jnp.int32, (1, 1, KEYS), 2) < BLK
        q_ref, k_ref, v_ref = qkv_ref.at[0], qkv_ref.at[1], qkv_ref.at[2]
        targets = [(o_ref, lse_ref)] + [(part_o.at[t], part_ml.at[t]) for t in range(len(PATTERNS) - 1)]
        for p, (_, dil) in enumerate(PATTERNS):
            def step(i, carry, p=p, dil=dil):
                rc, before, pen = _group_rows(i, dil, S)
                q2 = _load_group(q_ref, rc) * Q_SCALE
                kc = _load_group(k_ref, rc).astype(BF16)
                keys = jnp.concatenate([_with_prev(k_ref, before, kc), kc], axis=1)
                vc = _load_group(v_ref, rc)
                vals = jnp.concatenate([_with_prev(v_ref, before, vc), vc], axis=1)
                pen = jnp.where(prev_keys, pen, 0.0)
                pv = jnp.zeros((GRP, BLK, BLK), F32)
                ms, ls = [], []
                for h in range(2):
                    qh = jnp.where(masks[h], q2, 0.0).astype(BF16)
                    s = _bdot_nt(qh, keys) + bias_ref[p, h][None] + pen
                    mn = jnp.max(s, axis=-1, keepdims=True)
                    pr = jnp.exp(s - mn)
                    pv = pv + _bdot(pr.astype(BF16), jnp.where(vmasks[h], vals, 0.0).astype(BF16))
                    ms.append(mn)
                    ls.append(_lsum(pr))
                ml_t = jnp.where(lo, jnp.where(lane < HALF_HEAD, ms[0], ls[0]),
                                 jnp.where(lane < HEAD_DIM + HALF_HEAD, ms[1], ls[1]))
                acc_ref, ml_ref = targets[p]
                for j, r in enumerate(rc):
                    acc_ref[r, :] = pv[j]
                    ml_ref[r, :] = ml_t[j]
                return carry

            lax.fori_loop(0, nblk // GRP, step, 0)

        def merge(i, carry):
            rows = pl.ds(pl.multiple_of(i * 512, 512), 512)
            is_m = (lax.broadcasted_iota(jnp.int32, (512, BLK), 1) & HALF_HEAD) == 0
            accs = [ref[rows, :] for ref, _ in targets]
            mls = [ref[rows, :] for _, ref in targets]
            mx = [jnp.where(is_m, ml, pltpu.roll(ml, HALF_HEAD, 1)) for ml in mls]
            dn = [jnp.where(is_m, pltpu.roll(ml, BLK - HALF_HEAD, 1), ml) for ml in mls]
            top = jnp.maximum(jnp.maximum(mx[0], mx[1]), mx[2])
            ws_ = [jnp.exp(m - top) for m in mx]
            den = ws_[0] * dn[0] + ws_[1] * dn[1] + ws_[2] * dn[2]
            o_ref[rows, :] = (ws_[0] * accs[0] + ws_[1] * accs[1] + ws_[2] * accs[2]) / den
            lse_ref[rows, :] = top + jnp.log(den)
            return carry

        lax.fori_loop(0, S // 512, merge, 0)
        car.phase(1, pl.program_id(0), jrefs)
        car.phase(2, pl.program_id(0), jrefs)

    outs = pl.pallas_call(
        body, grid=(4,), name="attn_fwd",
        in_specs=[pl.BlockSpec((None, 3, None, S, BLK), lambda g: (0, 0, g, 0, 0)),
                  pl.BlockSpec((3, 2, BLK, KEYS), lambda g: (0, g, 0, 0))] + car.in_specs(),
        out_specs=[pl.BlockSpec((None, S, BLK), lambda g: (g, 0, 0)),
                   pl.BlockSpec((None, S, BLK), lambda g: (g, 0, 0))] + car.out_specs(),
        out_shape=[SDS((4, S, BLK), F32), SDS((4, S, BLK), F32)] + car.out_shapes(),
        input_output_aliases=car.aliases(),
        scratch_shapes=[pltpu.VMEM((3, S, BLK), F32), pltpu.VMEM((2, S, BLK), F32), pltpu.VMEM((2, S, BLK), F32)] + car.sems(),
        compiler_params=_cp("arbitrary"))(qkv, bias, *car.args())
    if job is not None:
        job.landed = list(outs[2:])
    return outs[0], outs[1]


def attn_bwd(qkv, bias, o, lse, do, job=None):
    S = qkv.shape[3]
    nblk = S // BLK
    car = Carried(job, 5, 2, 4, 4)

    def body(*refs):
        (qkv_in, bias_ref, o_ref, lse_ref, do_ref, dqkv_out, ds_ref, qkv_ref, dqkv_ref, last_ref, st_ref), jrefs = car.split(refs)
        car.phase(0, pl.program_id(0), jrefs)
        _widen(qkv_in, qkv_ref)

        def stats(i, carry):
            rows = pl.ds(pl.multiple_of(i * 512, 512), 512)
            ln = lax.broadcasted_iota(jnp.int32, (512, BLK), 1)
            prod = do_ref[rows, :] * o_ref[rows, :]
            d0 = _rsum(jnp.where(ln < HEAD_DIM, prod, 0.0))
            d1 = _rsum(jnp.where(ln < HEAD_DIM, 0.0, prod))
            st_ref[rows, :] = jnp.where((ln & HALF_HEAD) == 0, lse_ref[rows, :], jnp.where(ln < HEAD_DIM, d0, d1))
            return carry

        lax.fori_loop(0, S // 512, stats, 0)
        lane = lax.broadcasted_iota(jnp.int32, (GRP, BLK, BLK), 2)
        lo = lane < HEAD_DIM
        masks = (lo, jnp.logical_not(lo))
        lov = lax.broadcasted_iota(jnp.int32, (GRP, KEYS, BLK), 2) < HEAD_DIM
        vmasks = (lov, jnp.logical_not(lov))
        prev_keys = lax.broadcasted_iota(jnp.int32, (1, 1, KEYS), 2) < BLK
        ds_ref[...] = jnp.zeros_like(ds_ref)
        q_ref, k_ref, v_ref = qkv_ref.at[0], qkv_ref.at[1], qkv_ref.at[2]
        last = len(PATTERNS) - 1
        for p, (_, dil) in enumerate(PATTERNS):
            def step(i, carry, p=p, dil=dil):
                tgt = last_ref if p == last else dqkv_ref
                dq_ref, dk_ref, dv_ref = tgt.at[0], tgt.at[1], tgt.at[2]
                rc, before, pen = _group_rows(i, dil, S)
                q2 = _load_group(q_ref, rc) * Q_SCALE
                kc = _load_group(k_ref, rc).astype(BF16)
                keys = jnp.concatenate([_with_prev(k_ref, before, kc), kc], axis=1)
                vc = _load_group(v_ref, rc).astype(BF16)
                vals = jnp.concatenate([_with_prev(v_ref, before, vc), vc], axis=1)
                pen = jnp.where(prev_keys, pen, 0.0)
                dot = _load_group(do_ref, rc)
                st = _load_group(st_ref, rc)
                dq = jnp.zeros((GRP, BLK, BLK), F32)
                dkeys = jnp.zeros((GRP, KEYS, BLK), F32)
                dvals = jnp.zeros((GRP, KEYS, BLK), F32)
                for h in range(2):
                    qh = jnp.where(masks[h], q2, 0.0).astype(BF16)
                    doh = jnp.where(masks[h], dot, 0.0).astype(BF16)
                    lh = st[:, :, HEAD_DIM * h:HEAD_DIM * h + 1]
                    delta = st[:, :, HEAD_DIM * h + HALF_HEAD:HEAD_DIM * h + HALF_HEAD + 1]
                    pr = jnp.exp(_bdot_nt(qh, keys) + bias_ref[p, h][None] + pen - lh)
                    ds = pr * (_bdot_nt(doh, vals) - delta)
                    ds_ref[p, h] += jnp.sum(ds, axis=0)
                    dsb = ds.astype(BF16)
                    dq = dq + jnp.where(masks[h], _bdot(dsb, keys), 0.0)
                    dkeys = dkeys + _bdot_tn(dsb, qh)
                    dvals = dvals + _bdot_tn(pr.astype(BF16), doh)
                dkp, dkc, dvp, dvc = dkeys[:, :BLK], dkeys[:, BLK:], dvals[:, :BLK], dvals[:, BLK:]
                none = jnp.zeros((1, BLK, BLK), F32)
                dkc = dkc + jnp.concatenate([dkp[1:], none], axis=0)
                dvc = dvc + jnp.concatenate([dvp[1:], none], axis=0)
                for j in range(GRP):
                    if p == 1:
                        dq_ref[rc[j], :] += dq[j] * Q_SCALE
                        dk_ref[rc[j], :] += dkc[j]
                        dv_ref[rc[j], :] += dvc[j]
                    else:
                        dq_ref[rc[j], :] = dq[j] * Q_SCALE
                        dk_ref[rc[j], :] = dkc[j]
                        dv_ref[rc[j], :] = dvc[j]
                if before is not None:
                    dk_ref[before, :] += dkp[0]
                    dv_ref[before, :] += dvp[0]
                return carry

            lax.fori_loop(0, nblk // GRP, step, 0)

        def narrow(i, carry):
            rows = pl.ds(pl.multiple_of(i * 512, 512), 512)
            for a in range(3):
                dqkv_out[a, rows, :] = (dqkv_ref[a, rows, :] + last_ref[a, rows, :]).astype(BF16)
            return carry

        lax.fori_loop(0, S // 512, narrow, 0)
        car.phase(1, pl.program_id(0), jrefs)
        car.phase(2, pl.program_id(0), jrefs)

    blk = pl.BlockSpec((None, S, BLK), lambda g: (g, 0, 0))
    outs = pl.pallas_call(
        body, grid=(4,), name="attn_bwd",
        in_specs=[pl.BlockSpec((None, 3, None, S, BLK), lambda g: (0, 0, g, 0, 0)),
                  pl.BlockSpec((3, 2, BLK, KEYS), lambda g: (0, g, 0, 0)), blk, blk, blk] + car.in_specs(),
        out_specs=[pl.BlockSpec((None, 3, None, S, BLK), lambda g: (0, 0, g, 0, 0)),
                   pl.BlockSpec((3, 2, BLK, KEYS), lambda g: (0, g, 0, 0))] + car.out_specs(),
        out_shape=[SDS((2, 3, 4, S, BLK), BF16), SDS((3, N_HEADS, BLK, KEYS), F32)] + car.out_shapes(),
        input_output_aliases=car.aliases(),
        scratch_shapes=[pltpu.VMEM((3, S, BLK), F32)] * 3 + [pltpu.VMEM((S, BLK), F32)] + car.sems(),
        compiler_params=_cp("arbitrary"))(qkv, bias, o, lse, do, *car.args())
    if job is not None:
        job.landed = list(outs[2:])
    return outs[0], outs[1]


def _shift_down(u, k, halo):
    n = u.shape[0]
    row = lax.broadcasted_iota(jnp.int32, u.shape, 0)
    out = pltpu.roll(u, k, 0)
    hn = halo.shape[0]
    for j in range(k):
        out = jnp.where(row == j, halo[hn - k + j:hn - k + j + 1, :], out)
    return out


def _shift_up(u, k, halo):
    n = u.shape[0]
    row = lax.broadcasted_iota(jnp.int32, u.shape, 0)
    out = pltpu.roll(u, n - k, 0)
    for j in range(k):
        out = jnp.where(row == n - k + j, halo[j:j + 1, :], out)
    return out


def _conv3(u, halo, w0, w1, w2):
    return _shift_down(u, 2, halo) * w0 + _shift_down(u, 1, halo) * w1 + u * w2


def _colsum(x):
    return jnp.sum(x, axis=0, keepdims=True)


def _mixer_specs(S, tm):
    conv = pl.BlockSpec((None, 12, tm, BLK), lambda i: (1, 0, i, 0))
    halo = pl.BlockSpec((None, 12, 16, BLK), lambda i: (1, 0, jnp.maximum(i * (tm // 16) - 1, 0), 0))
    ob = pl.BlockSpec((4, tm, BLK), lambda i: (0, i, 0))
    return conv, halo, ob


def _mixer_recompute(i, o_ref, pr_ref, ph_ref, w_ref):
    ob = [o_ref[q] for q in range(4)]
    gb = [pr_ref[q].astype(F32) for q in range(4)]
    gc = [pr_ref[4 + q].astype(F32) for q in range(4)]
    xi = [pr_ref[8 + q].astype(F32) for q in range(4)]
    keep = jnp.where(i > 0, 1.0, 0.0)
    u = [gc[q] * xi[q] for q in range(4)]
    hu = [ph_ref[4 + q].astype(F32) * ph_ref[8 + q].astype(F32) * keep for q in range(4)]
    w = [[w_ref[k:k + 1, q * BLK:(q + 1) * BLK] for k in range(3)] for q in range(4)]
    cv = [_conv3(u[q], hu[q], *w[q]) for q in range(4)]
    return ob, gb, gc, xi, u, hu, cv, w


def _rms_blocks(blocks):
    ss = sum(_rsum(b * b) for b in blocks)
    return lax.rsqrt(ss / (BLK * len(blocks)) + EPS)


def mixer_post_fwd(o, proj, w_sc, g_a, g_c):
    S = o.shape[1]
    tm = 512

    def body(o_ref, pr_ref, ph_ref, w_ref, ga_ref, gc_ref, m_ref):
        i = pl.program_id(0)
        ob, gb, _, _, _, _, cv, _ = _mixer_recompute(i, o_ref, pr_ref, ph_ref, w_ref)
        conv = [gb[q] * cv[q] for q in range(4)]
        ra, rc = _rms_blocks(ob), _rms_blocks(conv)
        for q in range(4):
            sl = slice(q * BLK, (q + 1) * BLK)
            m_ref[:, q * BLK:(q + 1) * BLK] = (ob[q] * ra * ga_ref[:, sl]).astype(BF16)
            m_ref[:, ATTN_W + q * BLK:ATTN_W + (q + 1) * BLK] = (conv[q] * rc * gc_ref[:, sl]).astype(BF16)

    conv_s, halo_s, ob_s = _mixer_specs(S, tm)
    full = lambda r, c: pl.BlockSpec((r, c), lambda i: (0, 0))
    return pl.pallas_call(
        body, grid=(S // tm,), name="mixer_post_fwd",
        in_specs=[ob_s, conv_s, halo_s, full(3, 512), full(1, 512), full(1, 512)],
        out_specs=pl.BlockSpec((tm, D_MODEL), lambda i: (i, 0)), out_shape=SDS((S, D_MODEL), BF16),
        compiler_params=_cp("parallel"))(o, proj, proj, w_sc, g_a, g_c)


def mixer_post_bwd_a(dmixed, o, proj, w_sc, g_a, g_c, job=None):
    S = o.shape[1]
    tm = 512

    def body(dm_ref, o_ref, pr_ref, ph_ref, w_ref, ga_ref, gc_ref, do_ref, dgb_ref, dcv_ref, dga_ref, dgc_ref):
        i = pl.program_id(0)
        ob, gb, _, _, _, _, cv, _ = _mixer_recompute(i, o_ref, pr_ref, ph_ref, w_ref)
        conv = [gb[q] * cv[q] for q in range(4)]
        ra, rc = _rms_blocks(ob), _rms_blocks(conv)

        @pl.when(i == 0)
        def _():
            dga_ref[...] = jnp.zeros_like(dga_ref)
            dgc_ref[...] = jnp.zeros_like(dgc_ref)

        for blocks, r, g_ref, off, dg_ref, is_attn in ((ob, ra, ga_ref, 0, dga_ref, True), (conv, rc, gc_ref, ATTN_W, dgc_ref, False)):
            xh = [blocks[q] * r for q in range(4)]
            dm = [dm_ref[:, off + q * BLK:off + (q + 1) * BLK].astype(F32) for q in range(4)]
            gd = [dm[q] * g_ref[:, q * BLK:(q + 1) * BLK] for q in range(4)]
            mean = sum(_rsum(gd[q] * xh[q]) for q in range(4)) / (4 * BLK)
            for q in range(4):
                dg_ref[:, q * BLK:(q + 1) * BLK] += _colsum(dm[q] * xh[q])
                dx = r * (gd[q] - xh[q] * mean)
                if is_attn:
                    do_ref[q] = dx
                else:
                    dgb_ref[q] = dx * cv[q]
                    dcv_ref[q] = dx * gb[q]

    conv_s, halo_s, ob_s = _mixer_specs(S, tm)
    full = lambda r, c: pl.BlockSpec((r, c), lambda i: (0, 0))
    return _pcall(
        body, [dmixed, o, proj, proj, w_sc, g_a, g_c], grid=(S // tm,), name="mixer_post_bwd_a",
        in_specs=[pl.BlockSpec((tm, D_MODEL), lambda i: (i, 0)), ob_s, conv_s, halo_s, full(3, 512), full(1, 512), full(1, 512)],
        out_specs=[ob_s, ob_s, ob_s, full(1, 512), full(1, 512)],
        out_shape=[SDS((4, S, BLK), F32)] * 3 + [SDS((1, 512), F32)] * 2, sem=("arbitrary",), job=job)


def mixer_post_bwd_b(dproj, dgb, dcv, proj, w_sc):
    S = proj.shape[2]
    tm = 512
    last = S // 8 - 1

    def body(dp_in, dgb_ref, dcv_ref, dn_ref, pr_ref, w_ref, dp_ref, dw_ref):
        i = pl.program_id(0)
        keep_next = jnp.where(i < pl.num_programs(0) - 1, 1.0, 0.0)

        @pl.when(i == 0)
        def _():
            dw_ref[...] = jnp.zeros_like(dw_ref)

        for q in range(4):
            sl = slice(q * BLK, (q + 1) * BLK)
            gc, xi = pr_ref[4 + q].astype(F32), pr_ref[8 + q].astype(F32)
            u = gc * xi
            dcv = dcv_ref[q]
            dn = dn_ref[q] * keep_next
            d1, d2 = _shift_up(dcv, 1, dn), _shift_up(dcv, 2, dn)
            w0, w1, w2 = (w_ref[k:k + 1, sl] for k in range(3))
            du = dcv * w2 + d1 * w1 + d2 * w0
            dw_ref[0:1, sl] += _colsum(d2 * u)
            dw_ref[1:2, sl] += _colsum(d1 * u)
            dw_ref[2:3, sl] += _colsum(dcv * u)
            dp_ref[q] = dgb_ref[q].astype(BF16)
            dp_ref[4 + q] = (du * xi).astype(BF16)
            dp_ref[8 + q] = (du * gc).astype(BF16)

    conv_s, _, ob_s = _mixer_specs(S, tm)
    nxt = pl.BlockSpec((4, 8, BLK), lambda i: (0, jnp.minimum((i + 1) * (tm // 8), last), 0))
    full = lambda r, c: pl.BlockSpec((r, c), lambda i: (0, 0))
    return pl.pallas_call(
        body, grid=(S // tm,), name="mixer_post_bwd_b",
        in_specs=[pl.BlockSpec(memory_space=pl.ANY), ob_s, ob_s, nxt, conv_s, full(3, 512)],
        out_specs=[conv_s, full(3, 512)],
        out_shape=[SDS(dproj.shape, BF16), SDS((3, 512), F32)],
        input_output_aliases={0: 0}, compiler_params=_cp("arbitrary"))(dproj, dgb, dcv, dcv, proj, w_sc)


def xattn_fwd(q, k, v):
    S = q.shape[0]
    tm = 512
    scale = MEM_HD ** -0.5

    def body(q_ref, k_ref, v_ref, o_ref):
        for h in range(N_MEM_HEADS):
            sl = slice(h * MEM_HD, (h + 1) * MEM_HD)
            s = _dot_nt(q_ref[:, sl], k_ref[:, sl]) * scale
            p = jnp.exp(s - jnp.max(s, axis=1, keepdims=True))
            p = p / _rsum(p)
            o_ref[:, sl] = _dot(p.astype(BF16), v_ref[:, sl]).astype(BF16)

    row = pl.BlockSpec((tm, D_MODEL), lambda i: (i, 0))
    kv = pl.BlockSpec(k.shape, lambda i: (0, 0))
    return pl.pallas_call(body, grid=(S // tm,), name="xattn_fwd", in_specs=[row, kv, kv], out_specs=row,
                          out_shape=SDS((S, D_MODEL), BF16), compiler_params=_cp("parallel"))(q, k, v)


def xattn_bwd(q, k, v, do):
    S = q.shape[0]
    tm = 512
    scale = MEM_HD ** -0.5

    def body(q_ref, k_ref, v_ref, do_ref, dq_ref, dk_ref, dv_ref):
        @pl.when(pl.program_id(0) == 0)
        def _():
            dk_ref[...] = jnp.zeros_like(dk_ref)
            dv_ref[...] = jnp.zeros_like(dv_ref)

        for h in range(N_MEM_HEADS):
            sl = slice(h * MEM_HD, (h + 1) * MEM_HD)
            qh, kh, vh, doh = q_ref[:, sl], k_ref[:, sl], v_ref[:, sl], do_ref[:, sl]
            s = _dot_nt(qh, kh) * scale
            p = jnp.exp(s - jnp.max(s, axis=1, keepdims=True))
            p = p / _rsum(p)
            dp = _dot_nt(doh, vh)
            ds = (p * (dp - _rsum(p * dp)) * scale).astype(BF16)
            dq_ref[:, sl] = _dot(ds, kh).astype(BF16)
            dk_ref[:, sl] += _dot_tn(ds, qh)
            dv_ref[:, sl] += _dot_tn(p.astype(BF16), doh)

    row = pl.BlockSpec((tm, D_MODEL), lambda i: (i, 0))
    kv = pl.BlockSpec(k.shape, lambda i: (0, 0))
    return pl.pallas_call(body, grid=(S // tm,), name="xattn_bwd", in_specs=[row, kv, kv, row], out_specs=[row, kv, kv],
                          out_shape=[SDS((S, D_MODEL), BF16), SDS(k.shape, F32), SDS(k.shape, F32)],
                          compiler_params=_cp("arbitrary"))(q, k, v, do)


FFN_TM, FFN_TC = 512, 1408


def _ffn_specs(S):
    tm, tc = FFN_TM, FFN_TC
    blk = pl.BlockSpec((2, tm, tc), lambda j, i: (0, i, j))
    nxt = pl.BlockSpec((2, 16, tc), lambda j, i: (0, jnp.minimum((i + 1) * (tm // 16), S // 16 - 1), j))
    wsp = pl.BlockSpec((2, 3, tc), lambda j, i: (0, 0, j))
    bsp = pl.BlockSpec((2, 1, tc), lambda j, i: (0, 0, j))
    return blk, nxt, wsp, bsp


def up_act_fwd(h, w_up, w_fc, b_fc, job=None):
    S, K = h.shape
    tm, tc = FFN_TM, FFN_TC
    nj = D_FF // tc

    def body(h_ref, wg_ref, wv_ref, w_ref, b_ref, pre_ref, gv_ref, a_ref, halo):
        i = pl.program_id(1)
        hv = h_ref[...]
        res = []
        for half, wt_ref in enumerate((wg_ref, wv_ref)):
            u = _dot(hv, wt_ref[...])
            hu = jnp.where(i > 0, halo[half], 0.0)
            halo[half] = u[tm - 8:, :]
            w0, w1, w2 = (w_ref[half, k:k + 1, :] for k in range(3))
            pre_ref[half] = u.astype(BF16)
            res.append(_conv3(u, hu, w0, w1, w2) + b_ref[half])
        g, v = res
        gv_ref[0] = g.astype(BF16)
        gv_ref[1] = v.astype(BF16)
        a_ref[...] = (g * jax.nn.sigmoid(g) * v).astype(BF16)

    blk = pl.BlockSpec((2, tm, tc), lambda j, i: (0, i, j))
    return _pcall(
        body, [h, w_up, w_up, w_fc, b_fc], grid=(nj, S // tm), name="up_act_fwd",
        in_specs=[pl.BlockSpec((tm, K), lambda j, i: (i, 0)), pl.BlockSpec((K, tc), lambda j, i: (0, j)),
                  pl.BlockSpec((K, tc), lambda j, i: (0, j + nj)), pl.BlockSpec((2, 3, tc), lambda j, i: (0, 0, j)),
                  pl.BlockSpec((2, 1, tc), lambda j, i: (0, 0, j))],
        out_specs=[blk, blk, pl.BlockSpec((tm, tc), lambda j, i: (i, j))],
        out_shape=[SDS((2, S, D_FF), BF16), SDS((2, S, D_FF), BF16), SDS((S, D_FF), BF16)],
        scratch=[pltpu.VMEM((2, 8, tc), F32)], sem=("parallel", "arbitrary"), job=job)


def ffn_act_bwd(dx, w_down, up, up_pre, w_fc, job=None):
    S = up.shape[1]

    def act_grads(da, g, v):
        sg = jax.nn.sigmoid(g)
        return da * v * (sg * (1.0 + g * (1.0 - sg))), da * g * sg

    def body(dx_ref, dxn_ref, wd_ref, gv_ref, gvn_ref, up_ref, w_ref, dp_ref, dw_ref, db_ref):
        i = pl.program_id(1)
        keep_next = jnp.where(i < pl.num_programs(1) - 1, 1.0, 0.0)

        @pl.when(i == 0)
        def _():
            dw_ref[...] = jnp.zeros_like(dw_ref)
            db_ref[...] = jnp.zeros_like(db_ref)

        wd = wd_ref[...]
        da = _dot_nt(dx_ref[...].astype(BF16), wd)
        dan = _dot_nt(dxn_ref[...].astype(BF16), wd) * keep_next
        here = act_grads(da, gv_ref[0].astype(F32), gv_ref[1].astype(F32))
        after = act_grads(dan, gvn_ref[0].astype(F32), gvn_ref[1].astype(F32))
        for half in range(2):
            d, dn = here[half], after[half]
            u = up_ref[half].astype(F32)
            d1, d2 = _shift_up(d, 1, dn), _shift_up(d, 2, dn)
            w0, w1, w2 = (w_ref[half, k:k + 1, :] for k in range(3))
            dp_ref[half] = (d * w2 + d1 * w1 + d2 * w0).astype(BF16)
            dw_ref[half, 0:1, :] += _colsum(d2 * u)
            dw_ref[half, 1:2, :] += _colsum(d1 * u)
            dw_ref[half, 2:3, :] += _colsum(d * u)
            db_ref[half] += _colsum(d)

    tm, tc = FFN_TM, FFN_TC
    blk, nxt, wsp, bsp = _ffn_specs(S)
    K = dx.shape[1]
    rows = pl.BlockSpec((tm, K), lambda j, i: (i, 0))
    rows_next = pl.BlockSpec((16, K), lambda j, i: (jnp.minimum((i + 1) * (tm // 16), S // 16 - 1), 0))
    return _pcall(body, [dx, dx, w_down, up, up, up_pre, w_fc], grid=(D_FF // tc, S // tm), name="ffn_act_bwd",
                  in_specs=[rows, rows_next, pl.BlockSpec((tc, K), lambda j, i: (j, 0)), blk, nxt, blk, wsp],
                  out_specs=[blk, wsp, bsp],
                  out_shape=[SDS((2, S, D_FF), BF16), SDS((2, 3, D_FF), F32), SDS((2, 1, D_FF), F32)],
                  sem=("parallel", "arbitrary"), job=job)


def local_step(x, mem, target, wb, ws, comm=None):
    S = x.shape[0]
    assert S % 2048 == 0
    if comm is None:
        comm = NoComm()
    else:
        wb = {}
    ws = dict(ws)

    bias = bias_tables(ws["rel_bias"])
    h1 = rmsnorm_fwd(x, ws["g_mix"], "norm_mix", job=comm.job("norm_mix"))
    comm.landed("norm_mix", wb, ws)
    w_fc = ws["w_ffn_conv"].reshape(3, 2, D_FF).transpose(1, 0, 2)
    b_fc = ws["b_ffn_conv"].reshape(2, 1, D_FF)
    proj = mm_nn(h1, wb["w_in"], "proj", out_dtype=BF16, out_bw=BLK, tn=1536, job=comm.job("proj"))
    comm.landed("proj", wb)
    qkv = proj.reshape(2, 3, 4, S, BLK)
    o, lse = attn_fwd(qkv, bias, job=comm.job("attn_fwd"))
    comm.landed("attn_fwd", wb)
    proj4 = proj.reshape(2, 12, S, BLK)
    mixed = mixer_post_fwd(o, proj4, ws["w_short_conv"], ws["g_attn_out"], ws["g_conv_out"])
    x1, h2 = mm_nn_rows(mixed, wb["w_out"], x, ws["g_xattn"], "out_proj")
    mem_n = rmsnorm_fwd(mem, ws["g_mem"], "norm_mem")
    xq = mm_nn(h2, wb["w_xq"], "xq", out_dtype=BF16)
    xk, xv = mm_nn_pair(mem_n, wb["w_xk"], wb["w_xv"], "xkv", out_dtype=BF16)
    xo = xattn_fwd(xq, xk, xv)
    x2, h3 = mm_nn_rows(xo, wb["w_xo"], x1, ws["g_ffn"], "xo_proj")
    up_pre, up, act = up_act_fwd(h3, wb["w_up"], w_fc, b_fc, job=comm.job("up_act_fwd"))
    comm.landed("up_act_fwd", wb)
    loss, dx3, dg_final = mm_nn_rows(act, wb["w_down"], x2, ws["g_final"].reshape(1, -1), "down_proj", target=target)

    gb, gs = {}, {"g_final": dg_final}
    gb["w_down"] = mm_tn(act, dx3, None, "dw_down", tm=1408, tn=1024)
    d_up_pre, dw_fc, db_fc = ffn_act_bwd(dx3, wb["w_down"], up, up_pre, w_fc, job=comm.job("ffn_act_bwd", gb))
    comm.landed("ffn_act_bwd")
    gs["b_ffn_conv"] = db_fc.reshape(1, 2 * D_FF)
    gs["w_ffn_conv"] = dw_fc.transpose(1, 0, 2).reshape(3, 2 * D_FF)
    gb["w_up"] = mm_tn(h3, d_up_pre, D_FF, "dw_up", shards=4, tn=1408, job=comm.job("dw_up"))
    comm.landed("dw_up")
    dx2, gs["g_ffn"] = mm_nt_norm_bwd(d_up_pre, D_FF, wb["w_up"], x2, ws["g_ffn"], dx3, "d_h3", tm=512,
                                      job=comm.job("d_h3", gb))
    comm.landed("d_h3")
    gb["w_xo"] = mm_tn(xo, dx2, None, "dw_xo")
    dxo = mm_nt(dx2, None, wb["w_xo"], "d_xo", out_dtype=BF16)
    dxq, dxk, dxv = xattn_bwd(xq, xk, xv, dxo)
    gb["w_xq"] = mm_tn(h2, dxq, None, "dw_xq")
    gb["w_xk"] = mm_tn(mem_n, dxk, None, "dw_xk", tn=1024)
    gb["w_xv"] = mm_tn(mem_n, dxv, None, "dw_xv", tn=1024)
    dmem_n = mm_nt_pair_sum(dxk, wb["w_xk"], dxv, wb["w_xv"], "d_mem")
    gs["g_mem"] = rmsnorm_gain_grad(mem, dmem_n, "norm_mem_bwd")
    dx1, gs["g_xattn"] = mm_nt_norm_bwd(dxq, None, wb["w_xq"], x1, ws["g_xattn"], dx2, "d_h2")
    gb["w_out"] = mm_tn(mixed, dx1, None, "dw_out")
    dmixed = mm_nt(dx1, None, wb["w_out"], "d_mixed", out_dtype=BF16)
    do, dgb, dcv, gs["g_attn_out"], gs["g_conv_out"] = mixer_post_bwd_a(
        dmixed, o, proj4, ws["w_short_conv"], ws["g_attn_out"], ws["g_conv_out"], job=comm.job("mixer_post_bwd_a", gb))
    comm.landed("mixer_post_bwd_a")
    dproj, ds_sum = attn_bwd(qkv, bias, o, lse, do, job=comm.job("attn_bwd"))
    comm.landed("attn_bwd")
    dproj, gs["w_short_conv"] = mixer_post_bwd_b(dproj.reshape(2, 12, S, BLK), dgb, dcv, proj4, ws["w_short_conv"])
    dproj = dproj.reshape(24, S, BLK)
    gb["w_in"] = mm_tn(h1, dproj, BLK, "dw_in", shards=4, tn=768, job=comm.job("dw_in"))
    comm.landed("dw_in")
    gs["rel_bias"] = bias_tables_bwd(ds_sum, job=comm.job("bias_tables_bwd", gb))
    comm.landed("bias_tables_bwd")
    grad_x, gs["g_mix"] = mm_nt_norm_bwd(dproj, BLK, wb["w_in"], x, ws["g_mix"], dx1, "d_h1", tm=512,
                                         job=comm.job("d_h1", gb))
    comm.landed("d_h1")
    return loss, grad_x, comm.finish(gb), gs


def _adam_update(w_ref, g_ref, m_ref, v_ref, d_ref, nm_ref, nv_ref):
    gv = g_ref[...]
    mn = ADAM_B1 * m_ref[...] + (1.0 - ADAM_B1) * gv
    vn = ADAM_B2 * v_ref[...] + (1.0 - ADAM_B2) * (gv * gv)
    m_hat = mn / (1.0 - ADAM_B1 ** ADAM_STEP)
    v_hat = vn / (1.0 - ADAM_B2 ** ADAM_STEP)
    d_ref[...] = -ADAM_LR * (m_hat / (jnp.sqrt(v_hat) + ADAM_EPS) + ADAM_WD * w_ref[...])
    nm_ref[...] = mn
    nv_ref[...] = vn


def adamw(ws, gs, ms, vs, name, job=None):
    n = len(ws)
    R, C = ws[0].shape
    tr = R // 4

    def body(*refs):
        for p in range(n):
            _adam_update(*(refs[q * n + p] for q in range(7)))

    blk = pl.BlockSpec((tr, C), lambda i: (i, 0))
    outs = _pcall(body, [*ws, *gs, *ms, *vs], grid=(4,), name=name, in_specs=[blk] * (4 * n), out_specs=[blk] * (3 * n),
                  out_shape=[SDS((R, C), F32)] * (3 * n), sem=("parallel",), job=job)
    return outs[:n], outs[n:2 * n], outs[2 * n:]


BIG = ("w_in", "w_out", "w_xq", "w_xk", "w_xv", "w_xo", "w_up", "w_down")
COL_SHARDED = ("w_in", "w_up")
ANY = pl.BlockSpec(memory_space=pl.ANY)


def _place():
    x, y, c = lax.axis_index("x"), lax.axis_index("y"), lax.axis_index("c")
    chips = [(1 - x, y), (x, 1 - y), (1 - x, 1 - y)]
    return x, y, c, chips


def _window(full, name, R, C, shard, half):
    r0, nr = (0, R) if half is None else (half * (R // 2), R // 2)
    if name in COL_SHARDED:
        return full.at[pl.ds(r0, nr), pl.ds(shard * C, C)]
    return full.at[pl.ds(shard * R + r0, nr), :]


def same_shape_groups(names, arrays):
    groups = {}
    for n in names:
        groups.setdefault(tuple(arrays[n].shape), []).append(n)
    return list(groups.values())


def place_shards(ws, mine_arr, col, name):
    n = len(ws)
    R, C = ws[0].shape

    def body(s_ref, *refs):
        for p in range(n):
            refs[n + p][...] = refs[p][...].astype(BF16)

    grid_spec = pltpu.PrefetchScalarGridSpec(
        num_scalar_prefetch=1, grid=(1,), in_specs=[pl.BlockSpec((R, C), lambda i, s: (0, 0))] * n,
        out_specs=[pl.BlockSpec((R, C), (lambda i, s: (0, s[0])) if col else (lambda i, s: (s[0], 0)))] * n)
    return pl.pallas_call(body, grid_spec=grid_spec, name=name,
                          out_shape=[SDS((R, 4 * C) if col else (4 * R, C), BF16)] * n,
                          compiler_params=_cp("arbitrary"))(mine_arr, *ws)


def _gather_jobs(names, shapes):
    nw = len(names)

    def start(full, sems):
        send, recv, fsend, frecv = sems
        x, y, c, chips = _place()
        mine = 2 * x + y
        for w, n in enumerate(names):
            R, C = shapes[w]
            own = _window(full[w], n, R, C, mine, c)
            for j, chip in enumerate(chips):
                pltpu.make_async_remote_copy(src_ref=own, dst_ref=own, send_sem=send.at[w, j], recv_sem=recv.at[w, j],
                                             device_id=(*chip, c), device_id_type=MESH).start()

    def mid(full, sems, lo=0, hi=nw):
        send, recv, fsend, frecv = sems
        x, y, c, chips = _place()
        sib = (x, y, 1 - c)
        for w, n in list(enumerate(names))[lo:hi]:
            R, C = shapes[w]
            for j, chip in enumerate(chips):
                landed = _window(full[w], n, R, C, 2 * chip[0] + chip[1], c)
                pltpu.make_async_remote_copy(src_ref=landed, dst_ref=landed, send_sem=send.at[w, j], recv_sem=recv.at[w, j],
                                             device_id=(*chip, c), device_id_type=MESH).wait_recv()
                pltpu.make_async_remote_copy(src_ref=landed, dst_ref=landed, send_sem=fsend.at[w, j],
                                             recv_sem=frecv.at[w, j], device_id=sib, device_id_type=MESH).start()

    def finish(full, sems):
        send, recv, fsend, frecv = sems
        x, y, c, chips = _place()
        mine = 2 * x + y
        sib = (x, y, 1 - c)
        for w, n in enumerate(names):
            R, C = shapes[w]
            own = _window(full[w], n, R, C, mine, c)
            for j, chip in enumerate(chips):
                landed = _window(full[w], n, R, C, 2 * chip[0] + chip[1], c)
                other = _window(full[w], n, R, C, 2 * chip[0] + chip[1], 1 - c)
                pltpu.make_async_remote_copy(src_ref=other, dst_ref=other, send_sem=fsend.at[w, j], recv_sem=frecv.at[w, j],
                                             device_id=sib, device_id_type=MESH).wait_recv()
                pltpu.make_async_remote_copy(src_ref=own, dst_ref=own, send_sem=send.at[w, j], recv_sem=recv.at[w, j],
                                             device_id=(*chip, c), device_id_type=MESH).wait_send()
                pltpu.make_async_remote_copy(src_ref=landed, dst_ref=landed, send_sem=fsend.at[w, j],
                                             recv_sem=frecv.at[w, j], device_id=sib, device_id_type=MESH).wait_send()

    return start, mid, finish, [pltpu.SemaphoreType.DMA((nw, 3))] * 4


class CommJob:
    def __init__(self, ins, out_shapes, inplace, start, finish, sems, mid=None):
        self.ins, self.out_shapes, self.inplace = list(ins), list(out_shapes), inplace
        self.start, self.mid, self.finish, self.sems = start, mid, finish, list(sems)

    def run(self, phase, ins, outs, sems):
        if phase == 0:
            self.start(ins, outs, sems)
        elif phase == 1:
            if self.mid is not None:
                self.mid(ins, outs, sems)
        else:
            self.finish(ins, outs, sems)


class Carried:
    def __init__(self, job, n_in, n_out, n_scratch, steps):
        self.job, self.n_in, self.n_out, self.n_scratch, self.steps = job, n_in, n_out, n_scratch, steps
        self.nji = len(job.ins) if job else 0
        self.njo = len(job.out_shapes) if job else 0

    def in_specs(self):
        return [ANY] * self.nji

    def out_specs(self):
        return [ANY] * self.njo

    def out_shapes(self):
        return list(self.job.out_shapes) if self.job else []

    def aliases(self):
        if not (self.job and self.job.inplace):
            return {}
        return {self.n_in + i: self.n_out + i for i in range(self.nji)}

    def sems(self):
        return list(self.job.sems) if self.job else []

    def args(self):
        return list(self.job.ins) if self.job else []

    def split(self, refs):
        a = self.n_in
        b = a + self.nji
        c = b + self.n_out
        d = c + self.njo
        e = d + self.n_scratch
        return refs[:a] + refs[b:c] + refs[d:e], (refs[a:b], refs[c:d], refs[e:])

    def phase(self, phase, step, jrefs):
        if self.job is None:
            return
        at = {0: 0, 1: max(self.steps - 2, 0), 2: self.steps - 1}[phase]

        @pl.when(step == at)
        def _():
            self.job.run(phase, *jrefs)


def _pcall(body, args, *, grid, in_specs, out_specs, out_shape, name, sem, scratch=(), aliases=None, job=None):
    n_in, n_out = len(args), len(out_shape)
    if job is None:
        return pl.pallas_call(
            body, grid=grid, in_specs=list(in_specs), out_specs=list(out_specs), out_shape=list(out_shape), name=name,
            scratch_shapes=list(scratch), input_output_aliases=dict(aliases or {}), compiler_params=_cp(*sem))(*args)
    total = int(np.prod(grid))
    car = Carried(job, n_in, n_out, len(scratch), total)

    def wrapped(*refs):
        main, jrefs = car.split(refs)
        lin = pl.program_id(0)
        for ax in range(1, len(grid)):
            lin = lin * grid[ax] + pl.program_id(ax)
        car.phase(0, lin, jrefs)
        body(*main)
        car.phase(1, lin, jrefs)
        car.phase(2, lin, jrefs)

    res = pl.pallas_call(
        wrapped, grid=grid, in_specs=list(in_specs) + car.in_specs(), out_specs=list(out_specs) + car.out_specs(),
        out_shape=list(out_shape) + car.out_shapes(), name=name, scratch_shapes=list(scratch) + car.sems(),
        input_output_aliases={**dict(aliases or {}), **car.aliases()},
        compiler_params=_cp(*(["arbitrary"] * len(grid))))(*args, *car.args())
    job.landed = list(res[n_out:])
    return list(res[:n_out])


def gather_job(placed, names):
    shapes = []
    for n in names:
        R, C = placed[n].shape
        shapes.append((R, C // 4) if n in COL_SHARDED else (R // 4, C))
    start, mid, finish, sems = _gather_jobs(names, shapes)
    arrays = [placed[n] for n in names]
    early = max(len(names) - 1, 1)

    def last(i, o, s):
        mid(o, s, early, len(names))
        finish(o, s)

    return CommJob(arrays, [SDS(a.shape, a.dtype) for a in arrays], True,
                   lambda i, o, s: start(o, s), last, sems, mid=lambda i, o, s: mid(o, s, 0, early))


def gather_rows_job(placed):
    widths = [a.shape[1] // 4 for a in placed]

    def copies(outs, sems):
        x, y, c, chips = _place()
        mine = 2 * x + y
        cps = []
        for w, ref in enumerate(outs):
            own = ref.at[:, pl.ds(mine * widths[w], widths[w])]
            for j, chip in enumerate(chips):
                theirs = ref.at[:, pl.ds((2 * chip[0] + chip[1]) * widths[w], widths[w])]
                kw = dict(send_sem=sems[0].at[w, j], recv_sem=sems[1].at[w, j], device_id=(*chip, c), device_id_type=MESH)
                cps.append((pltpu.make_async_remote_copy(src_ref=own, dst_ref=own, **kw),
                            pltpu.make_async_remote_copy(src_ref=theirs, dst_ref=theirs, **kw)))
        return cps

    def start(ins, outs, sems):
        for send, _ in copies(outs, sems):
            send.start()

    def finish(ins, outs, sems):
        for send, recv in copies(outs, sems):
            recv.wait_recv()
            send.wait_send()

    return CommJob(placed, [SDS(a.shape, a.dtype) for a in placed], True, start, finish,
                   [pltpu.SemaphoreType.DMA((len(placed), 3))] * 2)


def merge_jobs(a, b):
    assert a.inplace and b.inplace
    na, nb_, nsa = len(a.ins), len(b.ins), len(a.sems)

    def phase(which):
        def run(ins, outs, sems):
            for job, i, o, s in ((a, ins[:na], outs[:na], sems[:nsa]), (b, ins[na:], outs[na:], sems[nsa:])):
                fn = getattr(job, which)
                if fn is not None:
                    fn(i, o, s)
        return run

    merged = CommJob(a.ins + b.ins, a.out_shapes + b.out_shapes, True, phase("start"), phase("finish"), a.sems + b.sems,
                     mid=phase("mid"))
    merged.parts = (a, b, na)
    return merged


def split_landed(merged):
    a, b, na = merged.parts
    a.landed, b.landed = merged.landed[:na], merged.landed[na:]


def pair_exchange_job(grads, names):
    shapes = [grads[n].shape for n in names]

    def copies(ins, outs, sems):
        x, y, c, _ = _place()
        return [pltpu.make_async_remote_copy(
            src_ref=ins[w].at[:, pl.ds((1 - c) * (shapes[w][1] // 2), shapes[w][1] // 2), :], dst_ref=outs[w],
            send_sem=sems[0].at[w], recv_sem=sems[1].at[w], device_id=(x, y, 1 - c), device_id_type=MESH)
            for w in range(len(names))]

    def start(ins, outs, sems):
        for cp in copies(ins, outs, sems):
            cp.start()

    def finish(ins, outs, sems):
        for cp in copies(ins, outs, sems):
            cp.wait()

    return CommJob([grads[n] for n in names], [SDS((4, s[1] // 2, s[2]), F32) for s in shapes], False, start, finish,
                   [pltpu.SemaphoreType.DMA((len(names),))] * 2)


def chip_exchange_job(p16, names):
    shapes = [p16[n].shape for n in names]

    def copies(ins, outs, sems):
        x, y, c, chips = _place()
        return [pltpu.make_async_remote_copy(
            src_ref=ins[w].at[2 * chip[0] + chip[1]], dst_ref=outs[w].at[j],
            send_sem=sems[0].at[w, j], recv_sem=sems[1].at[w, j], device_id=(*chip, c), device_id_type=MESH)
            for w in range(len(names)) for j, chip in enumerate(chips)]

    def start(ins, outs, sems):
        for cp in copies(ins, outs, sems):
            cp.start()

    def finish(ins, outs, sems):
        for cp in copies(ins, outs, sems):
            cp.wait()

    return CommJob([p16[n] for n in names], [SDS((3,) + tuple(s[1:]), BF16) for s in shapes], False, start, finish,
                   [pltpu.SemaphoreType.DMA((len(names), 3))] * 2)


def half_exchange_job(full, names):
    shapes = [full[n].shape for n in names]

    def copies(outs, sems):
        x, y, c, _ = _place()
        cps = []
        for w in range(len(names)):
            Rh = shapes[w][0] // 2
            rows = outs[w].at[pl.ds(c * Rh, Rh), :]
            other = outs[w].at[pl.ds((1 - c) * Rh, Rh), :]
            cps.append((pltpu.make_async_remote_copy(src_ref=rows, dst_ref=rows, send_sem=sems[0].at[w], recv_sem=sems[1].at[w],
                                                     device_id=(x, y, 1 - c), device_id_type=MESH),
                        pltpu.make_async_remote_copy(src_ref=other, dst_ref=other, send_sem=sems[0].at[w], recv_sem=sems[1].at[w],
                                                     device_id=(x, y, 1 - c), device_id_type=MESH)))
        return cps

    def start(ins, outs, sems):
        for send, _ in copies(outs, sems):
            send.start()

    def finish(ins, outs, sems):
        for send, recv in copies(outs, sems):
            recv.wait_recv()
            send.wait_send()

    arrays = [full[n] for n in names]
    return CommJob(arrays, [SDS(a.shape, a.dtype) for a in arrays], True, start, finish,
                   [pltpu.SemaphoreType.DMA((len(names),))] * 2)


def allreduce_small(parts, name, job=None):
    n = len(parts)
    nji = len(job.ins) if job else 0
    njo = len(job.out_shapes) if job else 0

    def body(*refs):
        ins, jins = refs[:n], refs[n:n + nji]
        outs, jouts = refs[n + nji:2 * n + nji], refs[2 * n + nji:2 * n + nji + njo]
        bufs = refs[2 * n + nji + njo:3 * n + nji + njo]
        send, recv = refs[3 * n + nji + njo:3 * n + nji + njo + 2]
        jsems = refs[3 * n + nji + njo + 2:]
        x, y, c, _ = _place()
        me = 4 * x + 2 * y + c
        if job is not None:
            job.run(0, jins, jouts, jsems)
        sends = []
        for p in range(n):
            for k in range(1, 8):
                peer = (x ^ (k >> 2), y ^ ((k >> 1) & 1), c ^ (k & 1))
                cp = pltpu.make_async_remote_copy(src_ref=ins[p], dst_ref=bufs[p].at[k], send_sem=send.at[p, k - 1],
                                                  recv_sem=recv.at[p, k - 1], device_id=peer, device_id_type=MESH)
                cp.start()
                sends.append(cp)
            bufs[p][0] = ins[p][...]
        for cp in sends:
            cp.wait_recv()
        for p in range(n):
            acc = bufs[p][me]
            for d in range(1, 8):
                acc = acc + bufs[p][d ^ me]
            outs[p][...] = acc
        for cp in sends:
            cp.wait_send()
        if job is not None:
            job.run(1, jins, jouts, jsems)
            job.run(2, jins, jouts, jsems)

    vm = pl.BlockSpec(memory_space=pltpu.VMEM)
    res = pl.pallas_call(
        body, name=name, in_specs=[vm] * n + [ANY] * nji, out_specs=[vm] * n + [ANY] * njo,
        out_shape=[SDS(a.shape, F32) for a in parts] + (list(job.out_shapes) if job else []),
        input_output_aliases={n + i: n + i for i in range(nji)} if (job and job.inplace) else {},
        scratch_shapes=[pltpu.VMEM((8,) + tuple(a.shape), F32) for a in parts] + [pltpu.SemaphoreType.DMA((n, 7))] * 2
        + (list(job.sems) if job else []),
    )(*parts, *(job.ins if job else []))
    if job is not None:
        job.landed = list(res[n:])
    return list(res[:n])


def adamw_multi(ws, gs, ms, vs, name):
    n = len(ws)

    def body(*refs):
        for p in range(n):
            _adam_update(*(refs[q * n + p] for q in range(7)))

    vm = pl.BlockSpec(memory_space=pltpu.VMEM)
    outs = pl.pallas_call(body, name=name, in_specs=[vm] * (4 * n), out_specs=[vm] * (3 * n),
                          out_shape=[SDS(a.shape, F32) for a in ws] * 3)(*ws, *gs, *ms, *vs)
    return outs[:n], outs[n:2 * n], outs[2 * n:]


def pair_sum(gs, xhs, sel_arr, name):
    n = len(gs)
    _, R, C = gs[0].shape
    Rh = R // 2

    def body(sel_ref, *refs):
        for p in range(n):
            g_ref, x_ref, p32_ref, p16_ref = (refs[q * n + p] for q in range(4))
            s = g_ref[...] + x_ref[...]
            p16_ref[...] = s.astype(BF16)

            @pl.when(pl.program_id(0) == sel_ref[0])
            def _():
                p32_ref[...] = s

    same = pl.BlockSpec((None, Rh, C), lambda s, sel: (s, 0, 0))
    grid_spec = pltpu.PrefetchScalarGridSpec(
        num_scalar_prefetch=1, grid=(4,),
        in_specs=[pl.BlockSpec((None, Rh, C), lambda s, sel: (s, sel[1], 0))] * n + [same] * n,
        out_specs=[pl.BlockSpec((Rh, C), lambda s, sel: (0, 0))] * n + [same] * n)
    outs = pl.pallas_call(body, grid_spec=grid_spec, name=name,
                          out_shape=[SDS((Rh, C), F32)] * n + [SDS((4, Rh, C), BF16)] * n,
                          compiler_params=_cp("arbitrary"))(sel_arr, *gs, *xhs)
    return outs[:n], outs[n:]


def chip_sum(p32s, ys, sel_arr, name):
    n = len(p32s)
    Rh, C = p32s[0].shape

    def body(s_ref, *refs):
        for p in range(n):
            p_ref, y_ref, o_ref = (refs[q * n + p] for q in range(3))
            o_ref[...] = ((p_ref[...] + y_ref[0].astype(F32)) + y_ref[1].astype(F32)) + y_ref[2].astype(F32)

    grid_spec = pltpu.PrefetchScalarGridSpec(
        num_scalar_prefetch=1, grid=(1,),
        in_specs=[pl.BlockSpec((Rh, C), lambda i, s: (0, 0))] * n + [pl.BlockSpec((3, Rh, C), lambda i, s: (0, 0, 0))] * n,
        out_specs=[pl.BlockSpec((Rh, C), lambda i, s: (s[1], 0))] * n)
    return pl.pallas_call(body, grid_spec=grid_spec, name=name, out_shape=[SDS((2 * Rh, C), F32)] * n,
                          compiler_params=_cp("arbitrary"))(sel_arr, *p32s, *ys)


class StepComm:
    FIRST = ("w_in",)
    SQUARE = ("w_out", "w_xq", "w_xk", "w_xv", "w_xo")
    FFN = ("w_up", "w_down")
    PLAN = {
        "proj": ("gather", SQUARE[:3]), "attn_fwd": ("gather", SQUARE[3:] + FFN[:1]), "up_act_fwd": ("gather", FFN[1:]),
        "ffn_act_bwd": ("pair", FFN[1:]), "dw_up": ("chip", FFN[1:]),
        "d_h3": ("pair", FFN[:1]), "mixer_post_bwd_a": ("pair", SQUARE), "attn_bwd": ("chip", FFN[:1] + SQUARE),
        "dw_in": ("half", FFN + SQUARE), "bias_tables_bwd": ("pair", FIRST), "d_h1": ("chip", FIRST),
    }

    def __init__(self, placed, conv_placed, sel_arr):
        self.placed, self.conv_placed, self.sel_arr = placed, conv_placed, sel_arr
        self.g, self.p32, self.p16, self.full, self.done, self.jobs = {}, {}, {}, {}, {}, {}

    def job(self, tag, gb=None):
        if tag == "norm_mix":
            self.jobs[tag] = merge_jobs(gather_job(self.placed, self.FIRST), gather_rows_job(self.conv_placed))
            return self.jobs[tag]
        kind, names = self.PLAN[tag]
        if kind == "gather":
            j = gather_job(self.placed, names)
        elif kind == "pair":
            for n in names:
                self.g[n] = gb[n] if gb[n].ndim == 3 else gb[n].reshape(4, gb[n].shape[0] // 4, gb[n].shape[1])
            j = pair_exchange_job(self.g, names)
        elif kind == "chip":
            j = chip_exchange_job(self.p16, names)
        else:
            j = half_exchange_job(self.full, names)
        self.jobs[tag] = j
        return j

    def landed(self, tag, wb=None, ws=None):
        if tag == "norm_mix":
            split_landed(self.jobs[tag])
            first, conv, _ = self.jobs[tag].parts
            wb.update(zip(self.FIRST, first.landed))
            ws.update((n, a[:3]) for n, a in zip(SMALL_SHARDED, conv.landed))
            return
        kind, names = self.PLAN[tag]
        got = dict(zip(names, self.jobs[tag].landed))
        if kind == "gather":
            wb.update(got)
        elif kind == "half":
            self.done.update(got)
        for group in same_shape_groups(names, got):
            if kind == "pair":
                p32s, p16s = pair_sum([self.g[n] for n in group], [got[n] for n in group], self.sel_arr, "pair_sum_" + group[0])
                self.p32.update(zip(group, p32s))
                self.p16.update(zip(group, p16s))
            elif kind == "chip":
                fulls = chip_sum([self.p32[n] for n in group], [got[n] for n in group], self.sel_arr, "chip_sum_" + group[0])
                self.full.update(zip(group, fulls))

    def finish(self, gb):
        return self.done

    def last_job(self):
        self.jobs["end_half"] = half_exchange_job(self.full, self.FIRST)
        return self.jobs["end_half"]

    def take_last(self):
        self.done.update(zip(self.FIRST, self.jobs["end_half"].landed))


class NoComm:
    def job(self, tag, gb=None):
        return None

    def landed(self, tag, wb=None, ws=None):
        pass

    def finish(self, gb):
        return gb


SMALL = ("rel_bias", "g_mix", "w_short_conv", "g_attn_out", "g_conv_out", "g_xattn", "g_mem", "g_ffn",
         "w_ffn_conv", "b_ffn_conv", "g_final")
SMALL_SHARDED = ("w_short_conv", "w_ffn_conv")


def kernel(x, mem, rel_bias, g_mix, w_in, w_short_conv, g_attn_out, g_conv_out, w_out, g_xattn, g_mem, w_xq, w_xk, w_xv, w_xo, g_ffn, w_up, w_ffn_conv, b_ffn_conv, w_down, g_final, loss_target, m_rel_bias, m_g_mix, m_w_in, m_w_short_conv, m_g_attn_out, m_g_conv_out, m_w_out, m_g_xattn, m_g_mem, m_w_xq, m_w_xk, m_w_xv, m_w_xo, m_g_ffn, m_w_up, m_w_ffn_conv, m_b_ffn_conv, m_w_down, m_g_final, v_rel_bias, v_g_mix, v_w_in, v_w_short_conv, v_g_attn_out, v_g_conv_out, v_w_out, v_g_xattn, v_g_mem, v_w_xq, v_w_xk, v_w_xv, v_w_xo, v_g_ffn, v_w_up, v_w_ffn_conv, v_b_ffn_conv, v_w_down, v_g_final):
    names = ("rel_bias", "g_mix", "w_in", "w_short_conv", "g_attn_out", "g_conv_out", "w_out", "g_xattn", "g_mem", "w_xq",
             "w_xk", "w_xv", "w_xo", "g_ffn", "w_up", "w_ffn_conv", "b_ffn_conv", "w_down", "g_final")
    W = dict(zip(names, (rel_bias, g_mix, w_in, w_short_conv, g_attn_out, g_conv_out, w_out, g_xattn, g_mem, w_xq, w_xk, w_xv,
                         w_xo, g_ffn, w_up, w_ffn_conv, b_ffn_conv, w_down, g_final)))
    M = dict(zip(names, (m_rel_bias, m_g_mix, m_w_in, m_w_short_conv, m_g_attn_out, m_g_conv_out, m_w_out, m_g_xattn, m_g_mem,
                         m_w_xq, m_w_xk, m_w_xv, m_w_xo, m_g_ffn, m_w_up, m_w_ffn_conv, m_b_ffn_conv, m_w_down, m_g_final)))
    V = dict(zip(names, (v_rel_bias, v_g_mix, v_w_in, v_w_short_conv, v_g_attn_out, v_g_conv_out, v_w_out, v_g_xattn, v_g_mem,
                         v_w_xq, v_w_xk, v_w_xv, v_w_xo, v_g_ffn, v_w_up, v_w_ffn_conv, v_b_ffn_conv, v_w_down, v_g_final)))
    xi, yi, ci = lax.axis_index("x"), lax.axis_index("y"), lax.axis_index("c")
    mine = 2 * xi + yi
    mine_arr = jnp.reshape(mine, (1,)).astype(jnp.int32)
    sel_arr = jnp.stack([mine, ci]).astype(jnp.int32)
    conv_placed = []
    for n in SMALL_SHARDED:
        shard = W[n][0]
        conv_placed.append(lax.dynamic_update_slice(jnp.zeros((8, 4 * shard.shape[1]), F32), shard, (0, mine * shard.shape[1])))
    shards = {n: W[n][0] for n in BIG}
    placed = {}
    for group in same_shape_groups(BIG, shards):
        outs = place_shards([shards[n] for n in group], mine_arr, group[0] in COL_SHARDED, "place_" + group[0])
        placed.update(zip(group, outs))
    comm = StepComm(placed, conv_placed, sel_arr)
    ws = {n: W[n] for n in SMALL if n not in SMALL_SHARDED}

    loss, grad_x, gfull, gs = local_step(x[0], mem[0], loss_target[0], None, ws, comm)

    def as2d(a):
        return a.reshape(1, -1) if a.ndim == 1 else a

    reduced = allreduce_small([as2d(gs[n]) for n in SMALL] + [loss], "reduce_small", job=comm.last_job())
    comm.take_last()
    gsm = dict(zip(SMALL, reduced[:-1]))
    loss = reduced[-1][0, 0]

    grads, delta, new_m, new_v = {}, {}, {}, {}
    for group in same_shape_groups(BIG, gfull):
        ds, nms, nvs = adamw([W[n][0] for n in group], [gfull[n] for n in group], [M[n][0] for n in group],
                             [V[n][0] for n in group], "adamw_" + group[0])
        for n, d, nm, nv in zip(group, ds, nms, nvs):
            grads[n], delta[n], new_m[n], new_v[n] = gfull[n][None], d[None], nm[None], nv[None]
    for n in SMALL_SHARDED:
        wid = W[n].shape[2]
        gsm[n] = lax.dynamic_slice(gsm[n], (0, mine * wid), (3, wid))

    def own(a, n):
        return a[0] if n in SMALL_SHARDED else as2d(a)

    d, nm, nv = adamw_multi([own(W[n], n) for n in SMALL], [gsm[n] for n in SMALL], [own(M[n], n) for n in SMALL],
                            [own(V[n], n) for n in SMALL], "adamw_small")
    for i, n in enumerate(SMALL):
        shape = W[n].shape
        grads[n], delta[n], new_m[n], new_v[n] = (a.reshape(shape) for a in (gsm[n], d[i], nm[i], nv[i]))
    return (loss, grad_x[None], *[grads[n] for n in names], *[delta[n] for n in names],
            *[new_m[n] for n in names], *[new_v[n] for n in names])
```

```python
import math

import numpy as np
import jax
import jax.numpy as jnp
from jax import lax
from jax.experimental import pallas as pl
from jax.experimental.pallas import tpu as pltpu

F32 = jnp.float32
BF16 = jnp.bfloat16
SDS = jax.ShapeDtypeStruct
MESH = pl.DeviceIdType.MESH

D_MODEL = 1024
ATTN_W = 512
N_HEADS = 8
HEAD_DIM = 64
HALF_HEAD = HEAD_DIM // 2
Q_SCALE = HEAD_DIM ** -0.5
BLK = 128
PATTERNS = ((128, 1), (512, 4), (2048, 16))
N_BUCKETS = 32
BUCKET_MAX_DIST = 2048
D_FF = 2816
N_MEM_HEADS = 4
MEM_HD = 256
EPS = 1e-6
NEG = -1e30
VMEM_LIMIT = 56 * 1024 * 1024

ADAM_LR, ADAM_B1, ADAM_B2, ADAM_EPS, ADAM_WD, ADAM_STEP = 0.001, 0.9, 0.999, 1e-08, 0.01, 10


def _cp(*sem):
    return pltpu.CompilerParams(dimension_semantics=sem, vmem_limit_bytes=VMEM_LIMIT)


def _dot(a, b):
    return jnp.dot(a, b, preferred_element_type=F32)


def _dot_nt(a, b):
    return lax.dot_general(a, b, (((1,), (1,)), ((), ())), preferred_element_type=F32)


def _dot_tn(a, b):
    return lax.dot_general(a, b, (((0,), (0,)), ((), ())), preferred_element_type=F32)


def _rsum(x):
    return jnp.sum(x, axis=1, keepdims=True)


def rmsnorm_fwd(x, g, name, job=None):
    S, Dm = x.shape
    tm = min(S, 512)

    def body(x_ref, g_ref, o_ref):
        xv = x_ref[...]
        r = lax.rsqrt(jnp.mean(xv * xv, axis=1, keepdims=True) + EPS)
        o_ref[...] = (xv * r * g_ref[...]).astype(o_ref.dtype)

    return _pcall(body, [x, g], grid=(S // tm,), name=name,
                  in_specs=[pl.BlockSpec((tm, Dm), lambda i: (i, 0)), pl.BlockSpec((1, Dm), lambda i: (0, 0))],
                  out_specs=[pl.BlockSpec((tm, Dm), lambda i: (i, 0))], out_shape=[SDS((S, Dm), BF16)],
                  sem=("parallel",), job=job)[0]


def rmsnorm_gain_grad(x, dh, name):
    S, Dm = x.shape
    tm = min(S, 512)

    def body(x_ref, dh_ref, dg_ref):
        xv = x_ref[...]
        xh = xv * lax.rsqrt(jnp.mean(xv * xv, axis=1, keepdims=True) + EPS)

        @pl.when(pl.program_id(0) == 0)
        def _():
            dg_ref[...] = jnp.zeros_like(dg_ref)

        dg_ref[...] += jnp.sum(dh_ref[...].astype(F32) * xh, axis=0, keepdims=True)

    row = pl.BlockSpec((tm, Dm), lambda i: (i, 0))
    return pl.pallas_call(
        body, grid=(S // tm,), name=name, in_specs=[row, row], out_specs=pl.BlockSpec((1, Dm), lambda i: (0, 0)),
        out_shape=SDS((1, Dm), F32), compiler_params=_cp("arbitrary"))(x, dh)


def _col_spec(bw, tn, rows, row_of, col_of):
    if bw is None:
        return pl.BlockSpec((rows, tn), lambda *g: (row_of(*g), col_of(*g)))
    if tn % bw == 0:
        return pl.BlockSpec((tn // bw, rows, bw), lambda *g: (col_of(*g), row_of(*g), 0))
    per = bw // tn
    return pl.BlockSpec((None, rows, tn), lambda *g: (col_of(*g) // per, row_of(*g), col_of(*g) % per))


def _read_cols(ref, bw, tn):
    if bw is None or tn % bw != 0:
        return ref[...]
    if tn == bw:
        return ref[0]
    return jnp.concatenate([ref[q] for q in range(tn // bw)], axis=1)


def _write_cols(ref, bw, tn, val):
    if bw is None or tn % bw != 0:
        ref[...] = val.astype(ref.dtype)
    else:
        for q in range(tn // bw):
            ref[q] = val[:, q * bw:(q + 1) * bw].astype(ref.dtype)


def mm_nn(a, b, name, *, out_dtype=F32, out_bw=None, tm=1024, tn=1024, job=None):
    M, K = a.shape
    N = b.shape[1]
    tm = min(tm, M)

    def body(a_ref, b_ref, o_ref):
        _write_cols(o_ref, out_bw, tn, _dot(a_ref[...].astype(BF16), b_ref[...]))

    ri, ci = (lambda i, j: i), (lambda i, j: j)
    in_specs = [pl.BlockSpec((tm, K), lambda i, j: (i, 0)), pl.BlockSpec((K, tn), lambda i, j: (0, j))]
    oshape = (M, N) if out_bw is None else (N // out_bw, M, out_bw)
    return _pcall(body, [a, b], grid=(M // tm, N // tn), name=name, in_specs=in_specs,
                  out_specs=[_col_spec(out_bw, tn, tm, ri, ci)], out_shape=[SDS(oshape, out_dtype)],
                  sem=("parallel", "parallel"), job=job)[0]


def mm_nn_pair(a, b1, b2, name, *, out_dtype=F32, tm=1024):
    M, K = a.shape
    N = b1.shape[1]
    tm = min(tm, M)

    def body(a_ref, b1_ref, b2_ref, o1_ref, o2_ref):
        av = a_ref[...].astype(BF16)
        o1_ref[...] = _dot(av, b1_ref[...]).astype(o1_ref.dtype)
        o2_ref[...] = _dot(av, b2_ref[...]).astype(o2_ref.dtype)

    w_spec = pl.BlockSpec((K, N), lambda i: (0, 0))
    o_spec = pl.BlockSpec((tm, N), lambda i: (i, 0))
    return _pcall(body, [a, b1, b2], grid=(M // tm,), name=name,
                  in_specs=[pl.BlockSpec((tm, K), lambda i: (i, 0)), w_spec, w_spec],
                  out_specs=[o_spec, o_spec], out_shape=[SDS((M, N), out_dtype)] * 2, sem=("parallel",))


def mm_nt_pair_sum(a1, b1, a2, b2, name, *, tm=1024):
    M, K = a1.shape
    N = b1.shape[0]
    tm = min(tm, M)

    def body(a1_ref, b1_ref, a2_ref, b2_ref, o_ref):
        first = _dot_nt(a1_ref[...].astype(BF16), b1_ref[...])
        o_ref[...] = _dot_nt(a2_ref[...].astype(BF16), b2_ref[...]) + first

    a_spec = pl.BlockSpec((tm, K), lambda i: (i, 0))
    w_spec = pl.BlockSpec((N, K), lambda i: (0, 0))
    return _pcall(body, [a1, b1, a2, b2], grid=(M // tm,), name=name, in_specs=[a_spec, w_spec, a_spec, w_spec],
                  out_specs=[pl.BlockSpec((tm, N), lambda i: (i, 0))], out_shape=[SDS((M, N), F32)],
                  sem=("parallel",))[0]


def mm_nn_rows(a, b, res, g, name, *, target=None, tm=512):
    M, K = a.shape
    N = b.shape[1]

    def body(*refs):
        a_ref, b_ref, r_ref, g_ref = refs[:4]
        xv = r_ref[...] + _dot(a_ref[...].astype(BF16), b_ref[...])
        r = lax.rsqrt(jnp.mean(xv * xv, axis=1, keepdims=True) + EPS)
        xh = xv * r
        if target is None:
            x_ref, h_ref = refs[4:]
            x_ref[...] = xv
            h_ref[...] = (xh * g_ref[...]).astype(BF16)
            return
        t_ref, loss_ref, dx_ref, dg_ref = refs[4:]
        err = xh * g_ref[...] - t_ref[...]
        dy = err / N
        gd = dy * g_ref[...]
        dx_ref[...] = r * (gd - xh * jnp.mean(gd * xh, axis=1, keepdims=True))

        @pl.when(pl.program_id(0) == 0)
        def _():
            dg_ref[...] = jnp.zeros_like(dg_ref)
            loss_ref[...] = jnp.zeros_like(loss_ref)

        dg_ref[...] += _colsum(dy * xh)
        loss_ref[...] += 0.5 * _colsum(jnp.mean(err * err, axis=1, keepdims=True))

    row = pl.BlockSpec((tm, N), lambda i: (i, 0))
    vec = pl.BlockSpec((1, N), lambda i: (0, 0))
    in_specs = [pl.BlockSpec((tm, K), lambda i: (i, 0)), pl.BlockSpec((K, N), lambda i: (0, 0)), row, vec]
    if target is None:
        return pl.pallas_call(body, grid=(M // tm,), name=name, in_specs=in_specs, out_specs=[row, row],
                              out_shape=[SDS((M, N), F32), SDS((M, N), BF16)], compiler_params=_cp("parallel"))(a, b, res, g)
    one = pl.BlockSpec((1, 1), lambda i: (0, 0))
    return pl.pallas_call(body, grid=(M // tm,), name=name, in_specs=in_specs + [row], out_specs=[one, row, vec],
                          out_shape=[SDS((1, 1), F32), SDS((M, N), F32), SDS((1, N), F32)],
                          compiler_params=_cp("arbitrary"))(a, b, res, g, target)


def mm_nt(a, a_bw, b, name, *, res=None, out_dtype=F32, tm=1024, tn=1024, job=None):
    if a_bw is None:
        M, K = a.shape
    else:
        M, K = a.shape[1], a.shape[0] * a_bw
    N = b.shape[0]
    tm = min(tm, M)

    def body(*refs):
        if res is None:
            a_ref, b_ref, o_ref = refs
        else:
            a_ref, b_ref, r_ref, o_ref = refs
        av = _read_cols(a_ref, a_bw, K).astype(BF16)
        acc = _dot_nt(av, b_ref[...])
        if res is not None:
            acc = acc + r_ref[...]
        o_ref[...] = acc.astype(o_ref.dtype)

    in_specs = [_col_spec(a_bw, K, tm, lambda i, j: i, lambda i, j: 0), pl.BlockSpec((tn, K), lambda i, j: (j, 0))]
    args = [a, b]
    if res is not None:
        in_specs.append(pl.BlockSpec((tm, tn), lambda i, j: (i, j)))
        args.append(res)
    return _pcall(body, args, grid=(M // tm, N // tn), name=name, in_specs=in_specs,
                  out_specs=[pl.BlockSpec((tm, tn), lambda i, j: (i, j))], out_shape=[SDS((M, N), out_dtype)],
                  sem=("parallel", "parallel"), job=job)[0]


def mm_nt_norm_bwd(a, a_bw, b, x, g, res, name, *, tm=512, job=None):
    if a_bw is None:
        M, K = a.shape
    else:
        M, K = a.shape[1], a.shape[0] * a_bw
    N = b.shape[0]

    def body(a_ref, b_ref, x_ref, g_ref, r_ref, dx_ref, dg_ref):
        dh = _dot_nt(_read_cols(a_ref, a_bw, K).astype(BF16), b_ref[...])
        xv = x_ref[...]
        r = lax.rsqrt(jnp.mean(xv * xv, axis=1, keepdims=True) + EPS)
        xh = xv * r
        gd = dh * g_ref[...]
        dx_ref[...] = r_ref[...] + r * (gd - xh * jnp.mean(gd * xh, axis=1, keepdims=True))

        @pl.when(pl.program_id(0) == 0)
        def _():
            dg_ref[...] = jnp.zeros_like(dg_ref)

        dg_ref[...] += jnp.sum(dh * xh, axis=0, keepdims=True)

    row = pl.BlockSpec((tm, N), lambda i: (i, 0))
    vec = pl.BlockSpec((1, N), lambda i: (0, 0))
    return _pcall(body, [a, b, x, g, res], grid=(M // tm,), name=name,
                  in_specs=[_col_spec(a_bw, K, tm, lambda i: i, lambda i: 0),
                            pl.BlockSpec((N, K), lambda i: (0, 0), pipeline_mode=pl.Buffered(1)), row, vec, row],
                  out_specs=[row, vec], out_shape=[SDS((M, N), F32), SDS((1, N), F32)], sem=("arbitrary",), job=job)


def mm_tn(a, b, b_bw, name, *, shards=None, tm=1024, tn=1024, ts=2048, job=None):
    S, Ka = a.shape
    N = b.shape[1] if b_bw is None else b.shape[0] * b_bw
    ts = min(ts, S)
    tm = min(tm, Ka)

    def body(a_ref, b_ref, o_ref):
        @pl.when(pl.program_id(2) == 0)
        def _():
            o_ref[...] = jnp.zeros_like(o_ref)

        bv = _read_cols(b_ref, b_bw, tn).astype(BF16)
        o_ref[...] += _dot_tn(a_ref[...].astype(BF16), bv)

    in_specs = [pl.BlockSpec((ts, tm), lambda i, j, k: (k, i)),
                _col_spec(b_bw, tn, ts, lambda i, j, k: k, lambda i, j, k: j)]
    if shards is None:
        out_spec, oshape = pl.BlockSpec((tm, tn), lambda i, j, k: (i, j)), (Ka, N)
    else:
        per = (N // shards) // tn
        out_spec = pl.BlockSpec((None, tm, tn), lambda i, j, k: (j // per, i, j % per))
        oshape = (shards, Ka, N // shards)
    return _pcall(body, [a, b], grid=(Ka // tm, N // tn, S // ts), name=name, in_specs=in_specs, out_specs=[out_spec],
                  out_shape=[SDS(oshape, F32)], sem=("parallel", "parallel", "arbitrary"), job=job)[0]


def mm_tn_pair(a, b1, b2, name, *, ts=2048):
    S, Ka = a.shape
    N = b1.shape[1]
    ts = min(ts, S)

    def body(a_ref, b1_ref, b2_ref, o1_ref, o2_ref):
        @pl.when(pl.program_id(0) == 0)
        def _():
            o1_ref[...] = jnp.zeros_like(o1_ref)
            o2_ref[...] = jnp.zeros_like(o2_ref)

        av = a_ref[...].astype(BF16)
        o1_ref[...] += _dot_tn(av, b1_ref[...].astype(BF16))
        o2_ref[...] += _dot_tn(av, b2_ref[...].astype(BF16))

    b_spec = pl.BlockSpec((ts, N), lambda k: (k, 0))
    o_spec = pl.BlockSpec((Ka, N), lambda k: (0, 0))
    return _pcall(body, [a, b1, b2], grid=(S // ts,), name=name,
                  in_specs=[pl.BlockSpec((ts, Ka), lambda k: (k, 0)), b_spec, b_spec],
                  out_specs=[o_spec, o_spec], out_shape=[SDS((Ka, N), F32)] * 2, sem=("arbitrary",))


KEYS = 2 * BLK


def _bucket_tables():
    out = np.zeros((3, BLK, KEYS), np.int32)
    qi = np.arange(BLK)[:, None]
    kj = np.arange(KEYS)[None, :]
    for p, (win, dil) in enumerate(PATTERNS):
        w = win // dil
        assert w == BLK
        steps = qi + w - kj
        valid = (steps >= 0) & (steps <= w)
        dist = np.clip(steps, 0, w) * dil
        dd = np.maximum(dist, 1).astype(np.float32)
        exact = N_BUCKETS // 2
        large = exact + (np.log(dd / np.float32(exact)) / np.float32(math.log(BUCKET_MAX_DIST / exact))
                         * np.float32(N_BUCKETS - exact)).astype(np.int32)
        large = np.minimum(large, N_BUCKETS - 1)
        out[p] = np.where(valid, np.where(dist < exact, dist, large), -1)
    return out


def bias_tables(rel_bias):
    bkt = jnp.asarray(_bucket_tables())

    def body(rb_ref, bkt_ref, o_ref):
        bk = bkt_ref[...]
        for h in range(N_HEADS):
            acc = jnp.full((BLK, KEYS), NEG, F32)
            for b in range(N_BUCKETS):
                acc = jnp.where(bk == b, rb_ref[h, b], acc)
            o_ref[h] = acc

    return pl.pallas_call(
        body, grid=(3,), name="bias_tables",
        in_specs=[pl.BlockSpec(memory_space=pltpu.SMEM),
                  pl.BlockSpec((None, BLK, KEYS), lambda p: (p, 0, 0))],
        out_specs=pl.BlockSpec((None, N_HEADS, BLK, KEYS), lambda p: (p, 0, 0, 0)),
        out_shape=SDS((3, N_HEADS, BLK, KEYS), F32), compiler_params=_cp("parallel"))(rel_bias, bkt)


def bias_tables_bwd(ds_sum, job=None):
    bkt = jnp.asarray(_bucket_tables())

    def body(ds_ref, bkt_ref, o_ref):
        h = pl.program_id(0)

        @pl.when(h == 0)
        def _():
            o_ref[...] = jnp.zeros_like(o_ref)

        rows = lax.broadcasted_iota(jnp.int32, (N_HEADS, N_BUCKETS), 0)
        cols = lax.broadcasted_iota(jnp.int32, (N_HEADS, N_BUCKETS), 1)
        acc = jnp.zeros((N_HEADS, N_BUCKETS), F32)
        for b in range(N_BUCKETS):
            t = jnp.zeros((BLK, KEYS), F32)
            for p in range(3):
                t = t + jnp.where(bkt_ref[p] == b, ds_ref[p], 0.0)
            tot = jnp.sum(_rsum(t), axis=0, keepdims=True)
            acc = acc + jnp.where((rows == h) & (cols == b), tot, 0.0)
        o_ref[...] += acc

    return _pcall(
        body, [ds_sum, bkt], grid=(N_HEADS,), name="bias_tables_bwd",
        in_specs=[pl.BlockSpec((3, None, BLK, KEYS), lambda h: (0, h, 0, 0)),
                  pl.BlockSpec((3, BLK, KEYS), lambda h: (0, 0, 0))],
        out_specs=[pl.BlockSpec((N_HEADS, N_BUCKETS), lambda h: (0, 0))],
        out_shape=[SDS((N_HEADS, N_BUCKETS), F32)], sem=("arbitrary",), job=job)[0]


def _rows(start, dil):
    if dil == 1:
        return pl.ds(pl.multiple_of(start, BLK), BLK)
    return pl.ds(start, BLK, stride=dil)


GRP = 8


def _group_rows(i, dil, S):
    nb = S // (BLK * dil)
    run = min(nb, GRP)
    chunks = nb // run
    res0, b0 = (i // chunks) * (GRP // run), (i % chunks) * run
    cur = [(b0 + j % run) * (BLK * dil) + res0 + j // run for j in range(GRP)]
    t = lax.broadcasted_iota(jnp.int32, (GRP, 1, 1), 0)
    if run == nb:
        before, pen = None, jnp.where(t % run == 0, NEG, 0.0)
    else:
        before = _rows(jnp.maximum(b0 - 1, 0) * (BLK * dil) + res0, dil)
        pen = jnp.where((t == 0) & (b0 == 0), NEG, 0.0)
    return [_rows(s, dil) for s in cur], before, pen


def _load_group(ref, rows):
    return jnp.stack([ref[r, :] for r in rows])


def _with_prev(ref, before, cur_blocks):
    first = cur_blocks[:1] if before is None else ref[before, :][None].astype(cur_blocks.dtype)
    return jnp.concatenate([first, cur_blocks[:-1]], axis=0)


def _bdot_nt(a, b):
    return lax.dot_general(a, b, (((2,), (2,)), ((0,), (0,))), preferred_element_type=F32)


def _bdot(a, b):
    return lax.dot_general(a, b, (((2,), (1,)), ((0,), (0,))), preferred_element_type=F32)


def _bdot_tn(a, b):
    return lax.dot_general(a, b, (((1,), (1,)), ((0,), (0,))), preferred_element_type=F32)


def _lsum(x):
    return jnp.sum(x, axis=-1, keepdims=True)


def _widen(src, dst):
    S = src.shape[1]

    def chunk(i, carry):
        rows = pl.ds(pl.multiple_of(i * 512, 512), 512)
        for a in range(3):
            dst[a, rows, :] = src[a, rows, :].astype(F32)
        return carry

    lax.fori_loop(0, S // 512, chunk, 0)


def attn_fwd(qkv, bias, job=None):
    S = qkv.shape[3]
    nblk = S // BLK
    car = Carried(job, 2, 2, 3, 4)

    def body(*refs):
        (qkv_in, bias_ref, o_ref, lse_ref, qkv_ref, part_o, part_ml), jrefs = car.split(refs)
        car.phase(0, pl.program_id(0), jrefs)
        _widen(qkv_in, qkv_ref)
        lane = lax.broadcasted_iota(jnp.int32, (GRP, BLK, BLK), 2)
        lo = lane < HEAD_DIM
        masks = (lo, jnp.logical_not(lo))
        lov = lax.broadcasted_iota(jnp.int32, (GRP, KEYS, BLK), 2) < HEAD_DIM
        vmasks = (lov, jnp.logical_not(lov))
        prev_keys = lax.broadcasted_iota(jnp.int32, (1, 1, KEYS), 2) < BLK
        q_ref, k_ref, v_ref = qkv_ref.at[0], qkv_ref.at[1], qkv_ref.at[2]
        targets = [(o_ref, lse_ref)] + [(part_o.at[t], part_ml.at[t]) for t in range(len(PATTERNS) - 1)]
        for p, (_, dil) in enumerate(PATTERNS):
            def step(i, carry, p=p, dil=dil):
                rc, before, pen = _group_rows(i, dil, S)
                q2 = _load_group(q_ref, rc) * Q_SCALE
                kc = _load_group(k_ref, rc).astype(BF16)
                keys = jnp.concatenate([_with_prev(k_ref, before, kc), kc], axis=1)
                vc = _load_group(v_ref, rc)
                vals = jnp.concatenate([_with_prev(v_ref, before, vc), vc], axis=1)
                pen = jnp.where(prev_keys, pen, 0.0)
                pv = jnp.zeros((GRP, BLK, BLK), F32)
                ms, ls = [], []
                for h in range(2):
                    qh = jnp.where(masks[h], q2, 0.0).astype(BF16)
                    s = _bdot_nt(qh, keys) + bias_ref[p, h][None] + pen
                    mn = jnp.max(s, axis=-1, keepdims=True)
                    pr = jnp.exp(s - mn)
                    pv = pv + _bdot(pr.astype(BF16), jnp.where(vmasks[h], vals, 0.0).astype(BF16))
                    ms.append(mn)
                    ls.append(_lsum(pr))
                ml_t = jnp.where(lo, jnp.where(lane < HALF_HEAD, ms[0], ls[0]),
                                 jnp.where(lane < HEAD_DIM + HALF_HEAD, ms[1], ls[1]))
                acc_ref, ml_ref = targets[p]
                for j, r in enumerate(rc):
                    acc_ref[r, :] = pv[j]
                    ml_ref[r, :] = ml_t[j]
                return carry

            lax.fori_loop(0, nblk // GRP, step, 0)

        def merge(i, carry):
            rows = pl.ds(pl.multiple_of(i * 512, 512), 512)
            is_m = (lax.broadcasted_iota(jnp.int32, (512, BLK), 1) & HALF_HEAD) == 0
            accs = [ref[rows, :] for ref, _ in targets]
            mls = [ref[rows, :] for _, ref in targets]
            mx = [jnp.where(is_m, ml, pltpu.roll(ml, HALF_HEAD, 1)) for ml in mls]
            dn = [jnp.where(is_m, pltpu.roll(ml, BLK - HALF_HEAD, 1), ml) for ml in mls]
            top = jnp.maximum(jnp.maximum(mx[0], mx[1]), mx[2])
            ws_ = [jnp.exp(m - top) for m in mx]
            den = ws_[0] * dn[0] + ws_[1] * dn[1] + ws_[2] * dn[2]
            o_ref[rows, :] = (ws_[0] * accs[0] + ws_[1] * accs[1] + ws_[2] * accs[2]) / den
            lse_ref[rows, :] = top + jnp.log(den)
            return carry

        lax.fori_loop(0, S // 512, merge, 0)
        car.phase(1, pl.program_id(0), jrefs)
        car.phase(2, pl.program_id(0), jrefs)

    outs = pl.pallas_call(
        body, grid=(4,), name="attn_fwd",
        in_specs=[pl.BlockSpec((None, 3, None, S, BLK), lambda g: (0, 0, g, 0, 0)),
                  pl.BlockSpec((3, 2, BLK, KEYS), lambda g: (0, g, 0, 0))] + car.in_specs(),
        out_specs=[pl.BlockSpec((None, S, BLK), lambda g: (g, 0, 0)),
                   pl.BlockSpec((None, S, BLK), lambda g: (g, 0, 0))] + car.out_specs(),
        out_shape=[SDS((4, S, BLK), F32), SDS((4, S, BLK), F32)] + car.out_shapes(),
        input_output_aliases=car.aliases(),
        scratch_shapes=[pltpu.VMEM((3, S, BLK), F32), pltpu.VMEM((2, S, BLK), F32), pltpu.VMEM((2, S, BLK), F32)] + car.sems(),
        compiler_params=_cp("arbitrary"))(qkv, bias, *car.args())
    if job is not None:
        job.landed = list(outs[2:])
    return outs[0], outs[1]


def attn_bwd(qkv, bias, o, lse, do, job=None):
    S = qkv.shape[3]
    nblk = S // BLK
    car = Carried(job, 5, 2, 4, 4)

    def body(*refs):
        (qkv_in, bias_ref, o_ref, lse_ref, do_ref, dqkv_out, ds_ref, qkv_ref, dqkv_ref, last_ref, st_ref), jrefs = car.split(refs)
        car.phase(0, pl.program_id(0), jrefs)
        _widen(qkv_in, qkv_ref)

        def stats(i, carry):
            rows = pl.ds(pl.multiple_of(i * 512, 512), 512)
            ln = lax.broadcasted_iota(jnp.int32, (512, BLK), 1)
            prod = do_ref[rows, :] * o_ref[rows, :]
            d0 = _rsum(jnp.where(ln < HEAD_DIM, prod, 0.0))
            d1 = _rsum(jnp.where(ln < HEAD_DIM, 0.0, prod))
            st_ref[rows, :] = jnp.where((ln & HALF_HEAD) == 0, lse_ref[rows, :], jnp.where(ln < HEAD_DIM, d0, d1))
            return carry

        lax.fori_loop(0, S // 512, stats, 0)
        lane = lax.broadcasted_iota(jnp.int32, (GRP, BLK, BLK), 2)
        lo = lane < HEAD_DIM
        masks = (lo, jnp.logical_not(lo))
        lov = lax.broadcasted_iota(jnp.int32, (GRP, KEYS, BLK), 2) < HEAD_DIM
        vmasks = (lov, jnp.logical_not(lov))
        prev_keys = lax.broadcasted_iota(jnp.int32, (1, 1, KEYS), 2) < BLK
        ds_ref[...] = jnp.zeros_like(ds_ref)
        q_ref, k_ref, v_ref = qkv_ref.at[0], qkv_ref.at[1], qkv_ref.at[2]
        last = len(PATTERNS) - 1
        for p, (_, dil) in enumerate(PATTERNS):
            def step(i, carry, p=p, dil=dil):
                tgt = last_ref if p == last else dqkv_ref
                dq_ref, dk_ref, dv_ref = tgt.at[0], tgt.at[1], tgt.at[2]
                rc, before, pen = _group_rows(i, dil, S)
                q2 = _load_group(q_ref, rc) * Q_SCALE
                kc = _load_group(k_ref, rc).astype(BF16)
                keys = jnp.concatenate([_with_prev(k_ref, before, kc), kc], axis=1)
                vc = _load_group(v_ref, rc).astype(BF16)
                vals = jnp.concatenate([_with_prev(v_ref, before, vc), vc], axis=1)
                pen = jnp.where(prev_keys, pen, 0.0)
                dot = _load_group(do_ref, rc)
                st = _load_group(st_ref, rc)
                dq = jnp.zeros((GRP, BLK, BLK), F32)
                dkeys = jnp.zeros((GRP, KEYS, BLK), F32)
                dvals = jnp.zeros((GRP, KEYS, BLK), F32)
                for h in range(2):
                    qh = jnp.where(masks[h], q2, 0.0).astype(BF16)
                    doh = jnp.where(masks[h], dot, 0.0).astype(BF16)
                    lh = st[:, :, HEAD_DIM * h:HEAD_DIM * h + 1]
                    delta = st[:, :, HEAD_DIM * h + HALF_HEAD:HEAD_DIM * h + HALF_HEAD + 1]
                    pr = jnp.exp(_bdot_nt(qh, keys) + bias_ref[p, h][None] + pen - lh)
                    ds = pr * (_bdot_nt(doh, vals) - delta)
                    ds_ref[p, h] += jnp.sum(ds, axis=0)
                    dsb = ds.astype(BF16)
                    dq = dq + jnp.where(masks[h], _bdot(dsb, keys), 0.0)
                    dkeys = dkeys + _bdot_tn(dsb, qh)
                    dvals = dvals + _bdot_tn(pr.astype(BF16), doh)
                dkp, dkc, dvp, dvc = dkeys[:, :BLK], dkeys[:, BLK:], dvals[:, :BLK], dvals[:, BLK:]
                none = jnp.zeros((1, BLK, BLK), F32)
                dkc = dkc + jnp.concatenate([dkp[1:], none], axis=0)
                dvc = dvc + jnp.concatenate([dvp[1:], none], axis=0)
                for j in range(GRP):
                    if p == 1:
                        dq_ref[rc[j], :] += dq[j] * Q_SCALE
                        dk_ref[rc[j], :] += dkc[j]
                        dv_ref[rc[j], :] += dvc[j]
                    else:
                        dq_ref[rc[j], :] = dq[j] * Q_SCALE
                        dk_ref[rc[j], :] = dkc[j]
                        dv_ref[rc[j], :] = dvc[j]
                if before is not None:
                    dk_ref[before, :] += dkp[0]
                    dv_ref[before, :] += dvp[0]
                return carry

            lax.fori_loop(0, nblk // GRP, step, 0)

        def narrow(i, carry):
            rows = pl.ds(pl.multiple_of(i * 512, 512), 512)
            for a in range(3):
                dqkv_out[a, rows, :] = (dqkv_ref[a, rows, :] + last_ref[a, rows, :]).astype(BF16)
            return carry

        lax.fori_loop(0, S // 512, narrow, 0)
        car.phase(1, pl.program_id(0), jrefs)
        car.phase(2, pl.program_id(0), jrefs)

    blk = pl.BlockSpec((None, S, BLK), lambda g: (g, 0, 0))
    outs = pl.pallas_call(
        body, grid=(4,), name="attn_bwd",
        in_specs=[pl.BlockSpec((None, 3, None, S, BLK), lambda g: (0, 0, g, 0, 0)),
                  pl.BlockSpec((3, 2, BLK, KEYS), lambda g: (0, g, 0, 0)), blk, blk, blk] + car.in_specs(),
        out_specs=[pl.BlockSpec((None, 3, None, S, BLK), lambda g: (0, 0, g, 0, 0)),
                   pl.BlockSpec((3, 2, BLK, KEYS), lambda g: (0, g, 0, 0))] + car.out_specs(),
        out_shape=[SDS((2, 3, 4, S, BLK), BF16), SDS((3, N_HEADS, BLK, KEYS), F32)] + car.out_shapes(),
        input_output_aliases=car.aliases(),
        scratch_shapes=[pltpu.VMEM((3, S, BLK), F32)] * 3 + [pltpu.VMEM((S, BLK), F32)] + car.sems(),
        compiler_params=_cp("arbitrary"))(qkv, bias, o, lse, do, *car.args())
    if job is not None:
        job.landed = list(outs[2:])
    return outs[0], outs[1]


def _shift_down(u, k, halo):
    n = u.shape[0]
    row = lax.broadcasted_iota(jnp.int32, u.shape, 0)
    out = pltpu.roll(u, k, 0)
    hn = halo.shape[0]
    for j in range(k):
        out = jnp.where(row == j, halo[hn - k + j:hn - k + j + 1, :], out)
    return out


def _shift_up(u, k, halo):
    n = u.shape[0]
    row = lax.broadcasted_iota(jnp.int32, u.shape, 0)
    out = pltpu.roll(u, n - k, 0)
    for j in range(k):
        out = jnp.where(row == n - k + j, halo[j:j + 1, :], out)
    return out


def _conv3(u, halo, w0, w1, w2):
    return _shift_down(u, 2, halo) * w0 + _shift_down(u, 1, halo) * w1 + u * w2


def _colsum(x):
    return jnp.sum(x, axis=0, keepdims=True)


def _mixer_specs(S, tm):
    conv = pl.BlockSpec((None, 12, tm, BLK), lambda i: (1, 0, i, 0))
    halo = pl.BlockSpec((None, 12, 16, BLK), lambda i: (1, 0, jnp.maximum(i * (tm // 16) - 1, 0), 0))
    ob = pl.BlockSpec((4, tm, BLK), lambda i: (0, i, 0))
    return conv, halo, ob


def _mixer_recompute(i, o_ref, pr_ref, ph_ref, w_ref):
    ob = [o_ref[q] for q in range(4)]
    gb = [pr_ref[q].astype(F32) for q in range(4)]
    gc = [pr_ref[4 + q].astype(F32) for q in range(4)]
    xi = [pr_ref[8 + q].astype(F32) for q in range(4)]
    keep = jnp.where(i > 0, 1.0, 0.0)
    u = [gc[q] * xi[q] for q in range(4)]
    hu = [ph_ref[4 + q].astype(F32) * ph_ref[8 + q].astype(F32) * keep for q in range(4)]
    w = [[w_ref[k:k + 1, q * BLK:(q + 1) * BLK] for k in range(3)] for q in range(4)]
    cv = [_conv3(u[q], hu[q], *w[q]) for q in range(4)]
    return ob, gb, gc, xi, u, hu, cv, w


def _rms_blocks(blocks):
    ss = sum(_rsum(b * b) for b in blocks)
    return lax.rsqrt(ss / (BLK * len(blocks)) + EPS)


def mixer_post_fwd(o, proj, w_sc, g_a, g_c):
    S = o.shape[1]
    tm = 512

    def body(o_ref, pr_ref, ph_ref, w_ref, ga_ref, gc_ref, m_ref):
        i = pl.program_id(0)
        ob, gb, _, _, _, _, cv, _ = _mixer_recompute(i, o_ref, pr_ref, ph_ref, w_ref)
        conv = [gb[q] * cv[q] for q in range(4)]
        ra, rc = _rms_blocks(ob), _rms_blocks(conv)
        for q in range(4):
            sl = slice(q * BLK, (q + 1) * BLK)
            m_ref[:, q * BLK:(q + 1) * BLK] = (ob[q] * ra * ga_ref[:, sl]).astype(BF16)
            m_ref[:, ATTN_W + q * BLK:ATTN_W + (q + 1) * BLK] = (conv[q] * rc * gc_ref[:, sl]).astype(BF16)

    conv_s, halo_s, ob_s = _mixer_specs(S, tm)
    full = lambda r, c: pl.BlockSpec((r, c), lambda i: (0, 0))
    return pl.pallas_call(
        body, grid=(S // tm,), name="mixer_post_fwd",
        in_specs=[ob_s, conv_s, halo_s, full(3, 512), full(1, 512), full(1, 512)],
        out_specs=pl.BlockSpec((tm, D_MODEL), lambda i: (i, 0)), out_shape=SDS((S, D_MODEL), BF16),
        compiler_params=_cp("parallel"))(o, proj, proj, w_sc, g_a, g_c)


def mixer_post_bwd_a(dmixed, o, proj, w_sc, g_a, g_c, job=None):
    S = o.shape[1]
    tm = 512

    def body(dm_ref, o_ref, pr_ref, ph_ref, w_ref, ga_ref, gc_ref, do_ref, dgb_ref, dcv_ref, dga_ref, dgc_ref):
        i = pl.program_id(0)
        ob, gb, _, _, _, _, cv, _ = _mixer_recompute(i, o_ref, pr_ref, ph_ref, w_ref)
        conv = [gb[q] * cv[q] for q in range(4)]
        ra, rc = _rms_blocks(ob), _rms_blocks(conv)

        @pl.when(i == 0)
        def _():
            dga_ref[...] = jnp.zeros_like(dga_ref)
            dgc_ref[...] = jnp.zeros_like(dgc_ref)

        for blocks, r, g_ref, off, dg_ref, is_attn in ((ob, ra, ga_ref, 0, dga_ref, True), (conv, rc, gc_ref, ATTN_W, dgc_ref, False)):
            xh = [blocks[q] * r for q in range(4)]
            dm = [dm_ref[:, off + q * BLK:off + (q + 1) * BLK].astype(F32) for q in range(4)]
            gd = [dm[q] * g_ref[:, q * BLK:(q + 1) * BLK] for q in range(4)]
            mean = sum(_rsum(gd[q] * xh[q]) for q in range(4)) / (4 * BLK)
            for q in range(4):
                dg_ref[:, q * BLK:(q + 1) * BLK] += _colsum(dm[q] * xh[q])
                dx = r * (gd[q] - xh[q] * mean)
                if is_attn:
                    do_ref[q] = dx
                else:
                    dgb_ref[q] = dx * cv[q]
                    dcv_ref[q] = dx * gb[q]

    conv_s, halo_s, ob_s = _mixer_specs(S, tm)
    full = lambda r, c: pl.BlockSpec((r, c), lambda i: (0, 0))
    return _pcall(
        body, [dmixed, o, proj, proj, w_sc, g_a, g_c], grid=(S // tm,), name="mixer_post_bwd_a",
        in_specs=[pl.BlockSpec((tm, D_MODEL), lambda i: (i, 0)), ob_s, conv_s, halo_s, full(3, 512), full(1, 512), full(1, 512)],
        out_specs=[ob_s, ob_s, ob_s, full(1, 512), full(1, 512)],
        out_shape=[SDS((4, S, BLK), F32)] * 3 + [SDS((1, 512), F32)] * 2, sem=("arbitrary",), job=job)


def mixer_post_bwd_b(dproj, dgb, dcv, proj, w_sc):
    S = proj.shape[2]
    tm = 512
    last = S // 8 - 1

    def body(dp_in, dgb_ref, dcv_ref, dn_ref, pr_ref, w_ref, dp_ref, dw_ref):
        i = pl.program_id(0)
        keep_next = jnp.where(i < pl.num_programs(0) - 1, 1.0, 0.0)

        @pl.when(i == 0)
        def _():
            dw_ref[...] = jnp.zeros_like(dw_ref)

        for q in range(4):
            sl = slice(q * BLK, (q + 1) * BLK)
            gc, xi = pr_ref[4 + q].astype(F32), pr_ref[8 + q].astype(F32)
            u = gc * xi
            dcv = dcv_ref[q]
            dn = dn_ref[q] * keep_next
            d1, d2 = _shift_up(dcv, 1, dn), _shift_up(dcv, 2, dn)
            w0, w1, w2 = (w_ref[k:k + 1, sl] for k in range(3))
            du = dcv * w2 + d1 * w1 + d2 * w0
            dw_ref[0:1, sl] += _colsum(d2 * u)
            dw_ref[1:2, sl] += _colsum(d1 * u)
            dw_ref[2:3, sl] += _colsum(dcv * u)
            dp_ref[q] = dgb_ref[q].astype(BF16)
            dp_ref[4 + q] = (du * xi).astype(BF16)
            dp_ref[8 + q] = (du * gc).astype(BF16)

    conv_s, _, ob_s = _mixer_specs(S, tm)
    nxt = pl.BlockSpec((4, 8, BLK), lambda i: (0, jnp.minimum((i + 1) * (tm // 8), last), 0))
    full = lambda r, c: pl.BlockSpec((r, c), lambda i: (0, 0))
    return pl.pallas_call(
        body, grid=(S // tm,), name="mixer_post_bwd_b",
        in_specs=[pl.BlockSpec(memory_space=pl.ANY), ob_s, ob_s, nxt, conv_s, full(3, 512)],
        out_specs=[conv_s, full(3, 512)],
        out_shape=[SDS(dproj.shape, BF16), SDS((3, 512), F32)],
        input_output_aliases={0: 0}, compiler_params=_cp("arbitrary"))(dproj, dgb, dcv, dcv, proj, w_sc)


def xattn_fwd(q, k, v):
    S = q.shape[0]
    tm = 512
    scale = MEM_HD ** -0.5

    def body(q_ref, k_ref, v_ref, o_ref):
        for h in range(N_MEM_HEADS):
            sl = slice(h * MEM_HD, (h + 1) * MEM_HD)
            s = _dot_nt(q_ref[:, sl], k_ref[:, sl]) * scale
            p = jnp.exp(s - jnp.max(s, axis=1, keepdims=True))
            p = p / _rsum(p)
            o_ref[:, sl] = _dot(p.astype(BF16), v_ref[:, sl]).astype(BF16)

    row = pl.BlockSpec((tm, D_MODEL), lambda i: (i, 0))
    kv = pl.BlockSpec(k.shape, lambda i: (0, 0))
    return pl.pallas_call(body, grid=(S // tm,), name="xattn_fwd", in_specs=[row, kv, kv], out_specs=row,
                          out_shape=SDS((S, D_MODEL), BF16), compiler_params=_cp("parallel"))(q, k, v)


def xattn_bwd(q, k, v, do):
    S = q.shape[0]
    tm = 512
    scale = MEM_HD ** -0.5

    def body(q_ref, k_ref, v_ref, do_ref, dq_ref, dk_ref, dv_ref):
        @pl.when(pl.program_id(0) == 0)
        def _():
            dk_ref[...] = jnp.zeros_like(dk_ref)
            dv_ref[...] = jnp.zeros_like(dv_ref)

        for h in range(N_MEM_HEADS):
            sl = slice(h * MEM_HD, (h + 1) * MEM_HD)
            qh, kh, vh, doh = q_ref[:, sl], k_ref[:, sl], v_ref[:, sl], do_ref[:, sl]
            s = _dot_nt(qh, kh) * scale
            p = jnp.exp(s - jnp.max(s, axis=1, keepdims=True))
            p = p / _rsum(p)
            dp = _dot_nt(doh, vh)
            ds = (p * (dp - _rsum(p * dp)) * scale).astype(BF16)
            dq_ref[:, sl] = _dot(ds, kh).astype(BF16)
            dk_ref[:, sl] += _dot_tn(ds, qh)
            dv_ref[:, sl] += _dot_tn(p.astype(BF16), doh)

    row = pl.BlockSpec((tm, D_MODEL), lambda i: (i, 0))
    kv = pl.BlockSpec(k.shape, lambda i: (0, 0))
    return pl.pallas_call(body, grid=(S // tm,), name="xattn_bwd", in_specs=[row, kv, kv, row], out_specs=[row, kv, kv],
                          out_shape=[SDS((S, D_MODEL), BF16), SDS(k.shape, F32), SDS(k.shape, F32)],
                          compiler_params=_cp("arbitrary"))(q, k, v, do)


FFN_TM, FFN_TC = 512, 1408


def _ffn_specs(S):
    tm, tc = FFN_TM, FFN_TC
    blk = pl.BlockSpec((2, tm, tc), lambda j, i: (0, i, j))
    nxt = pl.BlockSpec((2, 16, tc), lambda j, i: (0, jnp.minimum((i + 1) * (tm // 16), S // 16 - 1), j))
    wsp = pl.BlockSpec((2, 3, tc), lambda j, i: (0, 0, j))
    bsp = pl.BlockSpec((2, 1, tc), lambda j, i: (0, 0, j))
    return blk, nxt, wsp, bsp


def up_act_fwd(h, w_up, w_fc, b_fc, job=None):
    S, K = h.shape
    tm, tc = FFN_TM, FFN_TC
    nj = D_FF // tc

    def body(h_ref, wg_ref, wv_ref, w_ref, b_ref, pre_ref, gv_ref, a_ref, halo):
        i = pl.program_id(1)
        hv = h_ref[...]
        res = []
        for half, wt_ref in enumerate((wg_ref, wv_ref)):
            u = _dot(hv, wt_ref[...])
            hu = jnp.where(i > 0, halo[half], 0.0)
            halo[half] = u[tm - 8:, :]
            w0, w1, w2 = (w_ref[half, k:k + 1, :] for k in range(3))
            pre_ref[half] = u.astype(BF16)
            res.append(_conv3(u, hu, w0, w1, w2) + b_ref[half])
        g, v = res
        gv_ref[0] = g.astype(BF16)
        gv_ref[1] = v.astype(BF16)
        a_ref[...] = (g * jax.nn.sigmoid(g) * v).astype(BF16)

    blk = pl.BlockSpec((2, tm, tc), lambda j, i: (0, i, j))
    return _pcall(
        body, [h, w_up, w_up, w_fc, b_fc], grid=(nj, S // tm), name="up_act_fwd",
        in_specs=[pl.BlockSpec((tm, K), lambda j, i: (i, 0)), pl.BlockSpec((K, tc), lambda j, i: (0, j)),
                  pl.BlockSpec((K, tc), lambda j, i: (0, j + nj)), pl.BlockSpec((2, 3, tc), lambda j, i: (0, 0, j)),
                  pl.BlockSpec((2, 1, tc), lambda j, i: (0, 0, j))],
        out_specs=[blk, blk, pl.BlockSpec((tm, tc), lambda j, i: (i, j))],
        out_shape=[SDS((2, S, D_FF), BF16), SDS((2, S, D_FF), BF16), SDS((S, D_FF), BF16)],
        scratch=[pltpu.VMEM((2, 8, tc), F32)], sem=("parallel", "arbitrary"), job=job)


def ffn_act_bwd(dx, w_down, up, up_pre, w_fc, job=None):
    S = up.shape[1]

    def act_grads(da, g, v):
        sg = jax.nn.sigmoid(g)
        return da * v * (sg * (1.0 + g * (1.0 - sg))), da * g * sg

    def body(dx_ref, dxn_ref, wd_ref, gv_ref, gvn_ref, up_ref, w_ref, dp_ref, dw_ref, db_ref):
        i = pl.program_id(1)
        keep_next = jnp.where(i < pl.num_programs(1) - 1, 1.0, 0.0)

        @pl.when(i == 0)
        def _():
            dw_ref[...] = jnp.zeros_like(dw_ref)
            db_ref[...] = jnp.zeros_like(db_ref)

        wd = wd_ref[...]
        da = _dot_nt(dx_ref[...].astype(BF16), wd)
        dan = _dot_nt(dxn_ref[...].astype(BF16), wd) * keep_next
        here = act_grads(da, gv_ref[0].astype(F32), gv_ref[1].astype(F32))
        after = act_grads(dan, gvn_ref[0].astype(F32), gvn_ref[1].astype(F32))
        for half in range(2):
            d, dn = here[half], after[half]
            u = up_ref[half].astype(F32)
            d1, d2 = _shift_up(d, 1, dn), _shift_up(d, 2, dn)
            w0, w1, w2 = (w_ref[half, k:k + 1, :] for k in range(3))
            dp_ref[half] = (d * w2 + d1 * w1 + d2 * w0).astype(BF16)
            dw_ref[half, 0:1, :] += _colsum(d2 * u)
            dw_ref[half, 1:2, :] += _colsum(d1 * u)
            dw_ref[half, 2:3, :] += _colsum(d * u)
            db_ref[half] += _colsum(d)

    tm, tc = FFN_TM, FFN_TC
    blk, nxt, wsp, bsp = _ffn_specs(S)
    K = dx.shape[1]
    rows = pl.BlockSpec((tm, K), lambda j, i: (i, 0))
    rows_next = pl.BlockSpec((16, K), lambda j, i: (jnp.minimum((i + 1) * (tm // 16), S // 16 - 1), 0))
    return _pcall(body, [dx, dx, w_down, up, up, up_pre, w_fc], grid=(D_FF // tc, S // tm), name="ffn_act_bwd",
                  in_specs=[rows, rows_next, pl.BlockSpec((tc, K), lambda j, i: (j, 0)), blk, nxt, blk, wsp],
                  out_specs=[blk, wsp, bsp],
                  out_shape=[SDS((2, S, D_FF), BF16), SDS((2, 3, D_FF), F32), SDS((2, 1, D_FF), F32)],
                  sem=("parallel", "arbitrary"), job=job)


def local_step(x, mem, target, wb, ws, comm=None):
    S = x.shape[0]
    assert S % 2048 == 0
    if comm is None:
        comm = NoComm()
    else:
        wb = {}
    ws = dict(ws)

    bias = bias_tables(ws["rel_bias"])
    h1 = rmsnorm_fwd(x, ws["g_mix"], "norm_mix", job=comm.job("norm_mix"))
    comm.landed("norm_mix", wb, ws)
    w_fc = ws["w_ffn_conv"].reshape(3, 2, D_FF).transpose(1, 0, 2)
    b_fc = ws["b_ffn_conv"].reshape(2, 1, D_FF)
    proj = mm_nn(h1, wb["w_in"], "proj", out_dtype=BF16, out_bw=BLK, tn=1536, job=comm.job("proj"))
    comm.landed("proj", wb)
    qkv = proj.reshape(2, 3, 4, S, BLK)
    o, lse = attn_fwd(qkv, bias, job=comm.job("attn_fwd"))
    comm.landed("attn_fwd", wb)
    proj4 = proj.reshape(2, 12, S, BLK)
    mixed = mixer_post_fwd(o, proj4, ws["w_short_conv"], ws["g_attn_out"], ws["g_conv_out"])
    x1, h2 = mm_nn_rows(mixed, wb["w_out"], x, ws["g_xattn"], "out_proj")
    mem_n = rmsnorm_fwd(mem, ws["g_mem"], "norm_mem")
    xq = mm_nn(h2, wb["w_xq"], "xq", out_dtype=BF16)
    xk, xv = mm_nn_pair(mem_n, wb["w_xk"], wb["w_xv"], "xkv", out_dtype=BF16)
    xo = xattn_fwd(xq, xk, xv)
    x2, h3 = mm_nn_rows(xo, wb["w_xo"], x1, ws["g_ffn"], "xo_proj")
    up_pre, up, act = up_act_fwd(h3, wb["w_up"], w_fc, b_fc, job=comm.job("up_act_fwd"))
    comm.landed("up_act_fwd", wb)
    loss, dx3, dg_final = mm_nn_rows(act, wb["w_down"], x2, ws["g_final"].reshape(1, -1), "down_proj", target=target)

    gb, gs = {}, {"g_final": dg_final}
    gb["w_down"] = mm_tn(act, dx3, None, "dw_down", tm=1408, tn=1024)
    d_up_pre, dw_fc, db_fc = ffn_act_bwd(dx3, wb["w_down"], up, up_pre, w_fc, job=comm.job("ffn_act_bwd", gb))
    comm.landed("ffn_act_bwd")
    gs["b_ffn_conv"] = db_fc.reshape(1, 2 * D_FF)
    gs["w_ffn_conv"] = dw_fc.transpose(1, 0, 2).reshape(3, 2 * D_FF)
    gb["w_up"] = mm_tn(h3, d_up_pre, D_FF, "dw_up", shards=4, tn=1408, job=comm.job("dw_up"))
    comm.landed("dw_up")
    dx2, gs["g_ffn"] = mm_nt_norm_bwd(d_up_pre, D_FF, wb["w_up"], x2, ws["g_ffn"], dx3, "d_h3", tm=512,
                                      job=comm.job("d_h3", gb))
    comm.landed("d_h3")
    gb["w_xo"] = mm_tn(xo, dx2, None, "dw_xo")
    dxo = mm_nt(dx2, None, wb["w_xo"], "d_xo", out_dtype=BF16)
    dxq, dxk, dxv = xattn_bwd(xq, xk, xv, dxo)
    gb["w_xq"] = mm_tn(h2, dxq, None, "dw_xq")
    gb["w_xk"], gb["w_xv"] = mm_tn_pair(mem_n, dxk, dxv, "dw_xkv")
    dmem_n = mm_nt_pair_sum(dxk, wb["w_xk"], dxv, wb["w_xv"], "d_mem")
    gs["g_mem"] = rmsnorm_gain_grad(mem, dmem_n, "norm_mem_bwd")
    dx1, gs["g_xattn"] = mm_nt_norm_bwd(dxq, None, wb["w_xq"], x1, ws["g_xattn"], dx2, "d_h2")
    gb["w_out"] = mm_tn(mixed, dx1, None, "dw_out")
    dmixed = mm_nt(dx1, None, wb["w_out"], "d_mixed", out_dtype=BF16)
    do, dgb, dcv, gs["g_attn_out"], gs["g_conv_out"] = mixer_post_bwd_a(
        dmixed, o, proj4, ws["w_short_conv"], ws["g_attn_out"], ws["g_conv_out"], job=comm.job("mixer_post_bwd_a", gb))
    comm.landed("mixer_post_bwd_a")
    dproj, ds_sum = attn_bwd(qkv, bias, o, lse, do, job=comm.job("attn_bwd"))
    comm.landed("attn_bwd")
    dproj, gs["w_short_conv"] = mixer_post_bwd_b(dproj.reshape(2, 12, S, BLK), dgb, dcv, proj4, ws["w_short_conv"])
    dproj = dproj.reshape(24, S, BLK)
    gb["w_in"] = mm_tn(h1, dproj, BLK, "dw_in", shards=4, tn=768, job=comm.job("dw_in"))
    comm.landed("dw_in")
    gs["rel_bias"] = bias_tables_bwd(ds_sum, job=comm.job("bias_tables_bwd", gb))
    comm.landed("bias_tables_bwd")
    grad_x, gs["g_mix"] = mm_nt_norm_bwd(dproj, BLK, wb["w_in"], x, ws["g_mix"], dx1, "d_h1", tm=512,
                                         job=comm.job("d_h1", gb))
    comm.landed("d_h1")
    return loss, grad_x, comm.finish(gb), gs


def _adam_update(w_ref, g_ref, m_ref, v_ref, d_ref, nm_ref, nv_ref):
    gv = g_ref[...]
    mn = ADAM_B1 * m_ref[...] + (1.0 - ADAM_B1) * gv
    vn = ADAM_B2 * v_ref[...] + (1.0 - ADAM_B2) * (gv * gv)
    m_hat = mn / (1.0 - ADAM_B1 ** ADAM_STEP)
    v_hat = vn / (1.0 - ADAM_B2 ** ADAM_STEP)
    d_ref[...] = -ADAM_LR * (m_hat / (jnp.sqrt(v_hat) + ADAM_EPS) + ADAM_WD * w_ref[...])
    nm_ref[...] = mn
    nv_ref[...] = vn


def adamw(ws, gs, ms, vs, name, job=None):
    n = len(ws)
    R, C = ws[0].shape
    tr = R // 4

    def body(*refs):
        for p in range(n):
            _adam_update(*(refs[q * n + p] for q in range(7)))

    blk = pl.BlockSpec((tr, C), lambda i: (i, 0))
    outs = _pcall(body, [*ws, *gs, *ms, *vs], grid=(4,), name=name, in_specs=[blk] * (4 * n), out_specs=[blk] * (3 * n),
                  out_shape=[SDS((R, C), F32)] * (3 * n), sem=("parallel",), job=job)
    return outs[:n], outs[n:2 * n], outs[2 * n:]


BIG = ("w_in", "w_out", "w_xq", "w_xk", "w_xv", "w_xo", "w_up", "w_down")
COL_SHARDED = ("w_in", "w_up")
ANY = pl.BlockSpec(memory_space=pl.ANY)


def _place():
    x, y, c = lax.axis_index("x"), lax.axis_index("y"), lax.axis_index("c")
    chips = [(1 - x, y), (x, 1 - y), (1 - x, 1 - y)]
    return x, y, c, chips


def _window(full, name, R, C, shard, half):
    r0, nr = (0, R) if half is None else (half * (R // 2), R // 2)
    if name in COL_SHARDED:
        return full.at[pl.ds(r0, nr), pl.ds(shard * C, C)]
    return full.at[pl.ds(shard * R + r0, nr), :]


def same_shape_groups(names, arrays):
    groups = {}
    for n in names:
        groups.setdefault(tuple(arrays[n].shape), []).append(n)
    return list(groups.values())


def place_shards(ws, mine_arr, col, name):
    n = len(ws)
    R, C = ws[0].shape

    def body(s_ref, *refs):
        for p in range(n):
            refs[n + p][...] = refs[p][...].astype(BF16)

    grid_spec = pltpu.PrefetchScalarGridSpec(
        num_scalar_prefetch=1, grid=(1,), in_specs=[pl.BlockSpec((R, C), lambda i, s: (0, 0))] * n,
        out_specs=[pl.BlockSpec((R, C), (lambda i, s: (0, s[0])) if col else (lambda i, s: (s[0], 0)))] * n)
    return pl.pallas_call(body, grid_spec=grid_spec, name=name,
                          out_shape=[SDS((R, 4 * C) if col else (4 * R, C), BF16)] * n,
                          compiler_params=_cp("arbitrary"))(mine_arr, *ws)


def _gather_jobs(names, shapes):
    nw = len(names)

    def start(full, sems):
        send, recv, fsend, frecv = sems
        x, y, c, chips = _place()
        mine = 2 * x + y
        for w, n in enumerate(names):
            R, C = shapes[w]
            own = _window(full[w], n, R, C, mine, c)
            for j, chip in enumerate(chips):
                pltpu.make_async_remote_copy(src_ref=own, dst_ref=own, send_sem=send.at[w, j], recv_sem=recv.at[w, j],
                                             device_id=(*chip, c), device_id_type=MESH).start()

    def mid(full, sems, lo=0, hi=nw):
        send, recv, fsend, frecv = sems
        x, y, c, chips = _place()
        sib = (x, y, 1 - c)
        for w, n in list(enumerate(names))[lo:hi]:
            R, C = shapes[w]
            for j, chip in enumerate(chips):
                landed = _window(full[w], n, R, C, 2 * chip[0] + chip[1], c)
                pltpu.make_async_remote_copy(src_ref=landed, dst_ref=landed, send_sem=send.at[w, j], recv_sem=recv.at[w, j],
                                             device_id=(*chip, c), device_id_type=MESH).wait_recv()
                pltpu.make_async_remote_copy(src_ref=landed, dst_ref=landed, send_sem=fsend.at[w, j],
                                             recv_sem=frecv.at[w, j], device_id=sib, device_id_type=MESH).start()

    def finish(full, sems):
        send, recv, fsend, frecv = sems
        x, y, c, chips = _place()
        mine = 2 * x + y
        sib = (x, y, 1 - c)
        for w, n in enumerate(names):
            R, C = shapes[w]
            own = _window(full[w], n, R, C, mine, c)
            for j, chip in enumerate(chips):
                landed = _window(full[w], n, R, C, 2 * chip[0] + chip[1], c)
                other = _window(full[w], n, R, C, 2 * chip[0] + chip[1], 1 - c)
                pltpu.make_async_remote_copy(src_ref=other, dst_ref=other, send_sem=fsend.at[w, j], recv_sem=frecv.at[w, j],
                                             device_id=sib, device_id_type=MESH).wait_recv()
                pltpu.make_async_remote_copy(src_ref=own, dst_ref=own, send_sem=send.at[w, j], recv_sem=recv.at[w, j],
                                             device_id=(*chip, c), device_id_type=MESH).wait_send()
                pltpu.make_async_remote_copy(src_ref=landed, dst_ref=landed, send_sem=fsend.at[w, j],
                                             recv_sem=frecv.at[w, j], device_id=sib, device_id_type=MESH).wait_send()

    return start, mid, finish, [pltpu.SemaphoreType.DMA((nw, 3))] * 4


class CommJob:
    def __init__(self, ins, out_shapes, inplace, start, finish, sems, mid=None):
        self.ins, self.out_shapes, self.inplace = list(ins), list(out_shapes), inplace
        self.start, self.mid, self.finish, self.sems = start, mid, finish, list(sems)

    def run(self, phase, ins, outs, sems):
        if phase == 0:
            self.start(ins, outs, sems)
        elif phase == 1:
            if self.mid is not None:
                self.mid(ins, outs, sems)
        else:
            self.finish(ins, outs, sems)


class Carried:
    def __init__(self, job, n_in, n_out, n_scratch, steps):
        self.job, self.n_in, self.n_out, self.n_scratch, self.steps = job, n_in, n_out, n_scratch, steps
        self.nji = len(job.ins) if job else 0
        self.njo = len(job.out_shapes) if job else 0

    def in_specs(self):
        return [ANY] * self.nji

    def out_specs(self):
        return [ANY] * self.njo

    def out_shapes(self):
        return list(self.job.out_shapes) if self.job else []

    def aliases(self):
        if not (self.job and self.job.inplace):
            return {}
        return {self.n_in + i: self.n_out + i for i in range(self.nji)}

    def sems(self):
        return list(self.job.sems) if self.job else []

    def args(self):
        return list(self.job.ins) if self.job else []

    def split(self, refs):
        a = self.n_in
        b = a + self.nji
        c = b + self.n_out
        d = c + self.njo
        e = d + self.n_scratch
        return refs[:a] + refs[b:c] + refs[d:e], (refs[a:b], refs[c:d], refs[e:])

    def phase(self, phase, step, jrefs):
        if self.job is None:
            return
        at = {0: 0, 1: max(self.steps - 2, 0), 2: self.steps - 1}[phase]

        @pl.when(step == at)
        def _():
            self.job.run(phase, *jrefs)


def _pcall(body, args, *, grid, in_specs, out_specs, out_shape, name, sem, scratch=(), aliases=None, job=None):
    n_in, n_out = len(args), len(out_shape)
    if job is None:
        return pl.pallas_call(
            body, grid=grid, in_specs=list(in_specs), out_specs=list(out_specs), out_shape=list(out_shape), name=name,
            scratch_shapes=list(scratch), input_output_aliases=dict(aliases or {}), compiler_params=_cp(*sem))(*args)
    total = int(np.prod(grid))
    car = Carried(job, n_in, n_out, len(scratch), total)

    def wrapped(*refs):
        main, jrefs = car.split(refs)
        lin = pl.program_id(0)
        for ax in range(1, len(grid)):
            lin = lin * grid[ax] + pl.program_id(ax)
        car.phase(0, lin, jrefs)
        body(*main)
        car.phase(1, lin, jrefs)
        car.phase(2, lin, jrefs)

    res = pl.pallas_call(
        wrapped, grid=grid, in_specs=list(in_specs) + car.in_specs(), out_specs=list(out_specs) + car.out_specs(),
        out_shape=list(out_shape) + car.out_shapes(), name=name, scratch_shapes=list(scratch) + car.sems(),
        input_output_aliases={**dict(aliases or {}), **car.aliases()},
        compiler_params=_cp(*(["arbitrary"] * len(grid))))(*args, *car.args())
    job.landed = list(res[n_out:])
    return list(res[:n_out])


def gather_job(placed, names):
    shapes = []
    for n in names:
        R, C = placed[n].shape
        shapes.append((R, C // 4) if n in COL_SHARDED else (R // 4, C))
    start, mid, finish, sems = _gather_jobs(names, shapes)
    arrays = [placed[n] for n in names]
    early = max(len(names) - 1, 1)

    def last(i, o, s):
        mid(o, s, early, len(names))
        finish(o, s)

    return CommJob(arrays, [SDS(a.shape, a.dtype) for a in arrays], True,
                   lambda i, o, s: start(o, s), last, sems, mid=lambda i, o, s: mid(o, s, 0, early))


def gather_rows_job(placed):
    widths = [a.shape[1] // 4 for a in placed]

    def copies(outs, sems):
        x, y, c, chips = _place()
        mine = 2 * x + y
        cps = []
        for w, ref in enumerate(outs):
            own = ref.at[:, pl.ds(mine * widths[w], widths[w])]
            for j, chip in enumerate(chips):
                theirs = ref.at[:, pl.ds((2 * chip[0] + chip[1]) * widths[w], widths[w])]
                kw = dict(send_sem=sems[0].at[w, j], recv_sem=sems[1].at[w, j], device_id=(*chip, c), device_id_type=MESH)
                cps.append((pltpu.make_async_remote_copy(src_ref=own, dst_ref=own, **kw),
                            pltpu.make_async_remote_copy(src_ref=theirs, dst_ref=theirs, **kw)))
        return cps

    def start(ins, outs, sems):
        for send, _ in copies(outs, sems):
            send.start()

    def finish(ins, outs, sems):
        for send, recv in copies(outs, sems):
            recv.wait_recv()
            send.wait_send()

    return CommJob(placed, [SDS(a.shape, a.dtype) for a in placed], True, start, finish,
                   [pltpu.SemaphoreType.DMA((len(placed), 3))] * 2)


def merge_jobs(a, b):
    assert a.inplace and b.inplace
    na, nb_, nsa = len(a.ins), len(b.ins), len(a.sems)

    def phase(which):
        def run(ins, outs, sems):
            for job, i, o, s in ((a, ins[:na], outs[:na], sems[:nsa]), (b, ins[na:], outs[na:], sems[nsa:])):
                fn = getattr(job, which)
                if fn is not None:
                    fn(i, o, s)
        return run

    merged = CommJob(a.ins + b.ins, a.out_shapes + b.out_shapes, True, phase("start"), phase("finish"), a.sems + b.sems,
                     mid=phase("mid"))
    merged.parts = (a, b, na)
    return merged


def split_landed(merged):
    a, b, na = merged.parts
    a.landed, b.landed = merged.landed[:na], merged.landed[na:]


def pair_exchange_job(grads, names):
    shapes = [grads[n].shape for n in names]

    def copies(ins, outs, sems):
        x, y, c, _ = _place()
        return [pltpu.make_async_remote_copy(
            src_ref=ins[w].at[:, pl.ds((1 - c) * (shapes[w][1] // 2), shapes[w][1] // 2), :], dst_ref=outs[w],
            send_sem=sems[0].at[w], recv_sem=sems[1].at[w], device_id=(x, y, 1 - c), device_id_type=MESH)
            for w in range(len(names))]

    def start(ins, outs, sems):
        for cp in copies(ins, outs, sems):
            cp.start()

    def finish(ins, outs, sems):
        for cp in copies(ins, outs, sems):
            cp.wait()

    return CommJob([grads[n] for n in names], [SDS((4, s[1] // 2, s[2]), F32) for s in shapes], False, start, finish,
                   [pltpu.SemaphoreType.DMA((len(names),))] * 2)


def chip_exchange_job(p16, names):
    shapes = [p16[n].shape for n in names]

    def copies(ins, outs, sems):
        x, y, c, chips = _place()
        return [pltpu.make_async_remote_copy(
            src_ref=ins[w].at[2 * chip[0] + chip[1]], dst_ref=outs[w].at[j],
            send_sem=sems[0].at[w, j], recv_sem=sems[1].at[w, j], device_id=(*chip, c), device_id_type=MESH)
            for w in range(len(names)) for j, chip in enumerate(chips)]

    def start(ins, outs, sems):
        for cp in copies(ins, outs, sems):
            cp.start()

    def finish(ins, outs, sems):
        for cp in copies(ins, outs, sems):
            cp.wait()

    return CommJob([p16[n] for n in names], [SDS((3,) + tuple(s[1:]), BF16) for s in shapes], False, start, finish,
                   [pltpu.SemaphoreType.DMA((len(names), 3))] * 2)


def half_exchange_job(full, names):
    shapes = [full[n].shape for n in names]

    def copies(outs, sems):
        x, y, c, _ = _place()
        cps = []
        for w in range(len(names)):
            Rh = shapes[w][0] // 2
            rows = outs[w].at[pl.ds(c * Rh, Rh), :]
            other = outs[w].at[pl.ds((1 - c) * Rh, Rh), :]
            cps.append((pltpu.make_async_remote_copy(src_ref=rows, dst_ref=rows, send_sem=sems[0].at[w], recv_sem=sems[1].at[w],
                                                     device_id=(x, y, 1 - c), device_id_type=MESH),
                        pltpu.make_async_remote_copy(src_ref=other, dst_ref=other, send_sem=sems[0].at[w], recv_sem=sems[1].at[w],
                                                     device_id=(x, y, 1 - c), device_id_type=MESH)))
        return cps

    def start(ins, outs, sems):
        for send, _ in copies(outs, sems):
            send.start()

    def finish(ins, outs, sems):
        for send, recv in copies(outs, sems):
            recv.wait_recv()
            send.wait_send()

    arrays = [full[n] for n in names]
    return CommJob(arrays, [SDS(a.shape, a.dtype) for a in arrays], True, start, finish,
                   [pltpu.SemaphoreType.DMA((len(names),))] * 2)


def allreduce_small(parts, name, job=None):
    n = len(parts)
    nji = len(job.ins) if job else 0
    njo = len(job.out_shapes) if job else 0

    def body(*refs):
        ins, jins = refs[:n], refs[n:n + nji]
        outs, jouts = refs[n + nji:2 * n + nji], refs[2 * n + nji:2 * n + nji + njo]
        bufs = refs[2 * n + nji + njo:3 * n + nji + njo]
        send, recv = refs[3 * n + nji + njo:3 * n + nji + njo + 2]
        jsems = refs[3 * n + nji + njo + 2:]
        x, y, c, _ = _place()
        me = 4 * x + 2 * y + c
        if job is not None:
            job.run(0, jins, jouts, jsems)
        sends = []
        for p in range(n):
            for k in range(1, 8):
                peer = (x ^ (k >> 2), y ^ ((k >> 1) & 1), c ^ (k & 1))
                cp = pltpu.make_async_remote_copy(src_ref=ins[p], dst_ref=bufs[p].at[k], send_sem=send.at[p, k - 1],
                                                  recv_sem=recv.at[p, k - 1], device_id=peer, device_id_type=MESH)
                cp.start()
                sends.append(cp)
            bufs[p][0] = ins[p][...]
        for cp in sends:
            cp.wait_recv()
        for p in range(n):
            acc = bufs[p][me]
            for d in range(1, 8):
                acc = acc + bufs[p][d ^ me]
            outs[p][...] = acc
        for cp in sends:
            cp.wait_send()
        if job is not None:
            job.run(1, jins, jouts, jsems)
            job.run(2, jins, jouts, jsems)

    vm = pl.BlockSpec(memory_space=pltpu.VMEM)
    res = pl.pallas_call(
        body, name=name, in_specs=[vm] * n + [ANY] * nji, out_specs=[vm] * n + [ANY] * njo,
        out_shape=[SDS(a.shape, F32) for a in parts] + (list(job.out_shapes) if job else []),
        input_output_aliases={n + i: n + i for i in range(nji)} if (job and job.inplace) else {},
        scratch_shapes=[pltpu.VMEM((8,) + tuple(a.shape), F32) for a in parts] + [pltpu.SemaphoreType.DMA((n, 7))] * 2
        + (list(job.sems) if job else []),
    )(*parts, *(job.ins if job else []))
    if job is not None:
        job.landed = list(res[n:])
    return list(res[:n])


def adamw_multi(ws, gs, ms, vs, name):
    n = len(ws)

    def body(*refs):
        for p in range(n):
            _adam_update(*(refs[q * n + p] for q in range(7)))

    vm = pl.BlockSpec(memory_space=pltpu.VMEM)
    outs = pl.pallas_call(body, name=name, in_specs=[vm] * (4 * n), out_specs=[vm] * (3 * n),
                          out_shape=[SDS(a.shape, F32) for a in ws] * 3)(*ws, *gs, *ms, *vs)
    return outs[:n], outs[n:2 * n], outs[2 * n:]


def pair_sum(gs, xhs, sel_arr, name):
    n = len(gs)
    _, R, C = gs[0].shape
    Rh = R // 2

    def body(sel_ref, *refs):
        for p in range(n):
            g_ref, x_ref, p32_ref, p16_ref = (refs[q * n + p] for q in range(4))
            s = g_ref[...] + x_ref[...]
            p16_ref[...] = s.astype(BF16)

            @pl.when(pl.program_id(0) == sel_ref[0])
            def _():
                p32_ref[...] = s

    same = pl.BlockSpec((None, Rh, C), lambda s, sel: (s, 0, 0))
    grid_spec = pltpu.PrefetchScalarGridSpec(
        num_scalar_prefetch=1, grid=(4,),
        in_specs=[pl.BlockSpec((None, Rh, C), lambda s, sel: (s, sel[1], 0))] * n + [same] * n,
        out_specs=[pl.BlockSpec((Rh, C), lambda s, sel: (0, 0))] * n + [same] * n)
    outs = pl.pallas_call(body, grid_spec=grid_spec, name=name,
                          out_shape=[SDS((Rh, C), F32)] * n + [SDS((4, Rh, C), BF16)] * n,
                          compiler_params=_cp("arbitrary"))(sel_arr, *gs, *xhs)
    return outs[:n], outs[n:]


def chip_sum(p32s, ys, sel_arr, name):
    n = len(p32s)
    Rh, C = p32s[0].shape

    def body(s_ref, *refs):
        for p in range(n):
            p_ref, y_ref, o_ref = (refs[q * n + p] for q in range(3))
            o_ref[...] = ((p_ref[...] + y_ref[0].astype(F32)) + y_ref[1].astype(F32)) + y_ref[2].astype(F32)

    grid_spec = pltpu.PrefetchScalarGridSpec(
        num_scalar_prefetch=1, grid=(1,),
        in_specs=[pl.BlockSpec((Rh, C), lambda i, s: (0, 0))] * n + [pl.BlockSpec((3, Rh, C), lambda i, s: (0, 0, 0))] * n,
        out_specs=[pl.BlockSpec((Rh, C), lambda i, s: (s[1], 0))] * n)
    return pl.pallas_call(body, grid_spec=grid_spec, name=name, out_shape=[SDS((2 * Rh, C), F32)] * n,
                          compiler_params=_cp("arbitrary"))(sel_arr, *p32s, *ys)


class StepComm:
    FIRST = ("w_in",)
    SQUARE = ("w_out", "w_xq", "w_xk", "w_xv", "w_xo")
    FFN = ("w_up", "w_down")
    PLAN = {
        "proj": ("gather", SQUARE[:3]), "attn_fwd": ("gather", SQUARE[3:] + FFN[:1]), "up_act_fwd": ("gather", FFN[1:]),
        "ffn_act_bwd": ("pair", FFN[1:]), "dw_up": ("chip", FFN[1:]),
        "d_h3": ("pair", FFN[:1]), "mixer_post_bwd_a": ("pair", SQUARE), "attn_bwd": ("chip", FFN[:1] + SQUARE),
        "dw_in": ("half", FFN + SQUARE), "bias_tables_bwd": ("pair", FIRST), "d_h1": ("chip", FIRST),
    }

    def __init__(self, placed, conv_placed, sel_arr):
        self.placed, self.conv_placed, self.sel_arr = placed, conv_placed, sel_arr
        self.g, self.p32, self.p16, self.full, self.done, self.jobs = {}, {}, {}, {}, {}, {}

    def job(self, tag, gb=None):
        if tag == "norm_mix":
            self.jobs[tag] = merge_jobs(gather_job(self.placed, self.FIRST), gather_rows_job(self.conv_placed))
            return self.jobs[tag]
        kind, names = self.PLAN[tag]
        if kind == "gather":
            j = gather_job(self.placed, names)
        elif kind == "pair":
            for n in names:
                self.g[n] = gb[n] if gb[n].ndim == 3 else gb[n].reshape(4, gb[n].shape[0] // 4, gb[n].shape[1])
            j = pair_exchange_job(self.g, names)
        elif kind == "chip":
            j = chip_exchange_job(self.p16, names)
        else:
            j = half_exchange_job(self.full, names)
        self.jobs[tag] = j
        return j

    def landed(self, tag, wb=None, ws=None):
        if tag == "norm_mix":
            split_landed(self.jobs[tag])
            first, conv, _ = self.jobs[tag].parts
            wb.update(zip(self.FIRST, first.landed))
            ws.update((n, a[:3]) for n, a in zip(SMALL_SHARDED, conv.landed))
            return
        kind, names = self.PLAN[tag]
        got = dict(zip(names, self.jobs[tag].landed))
        if kind == "gather":
            wb.update(got)
        elif kind == "half":
            self.done.update(got)
        for group in same_shape_groups(names, got):
            if kind == "pair":
                p32s, p16s = pair_sum([self.g[n] for n in group], [got[n] for n in group], self.sel_arr, "pair_sum_" + group[0])
                self.p32.update(zip(group, p32s))
                self.p16.update(zip(group, p16s))
            elif kind == "chip":
                fulls = chip_sum([self.p32[n] for n in group], [got[n] for n in group], self.sel_arr, "chip_sum_" + group[0])
                self.full.update(zip(group, fulls))

    def finish(self, gb):
        return self.done

    def last_job(self):
        self.jobs["end_half"] = half_exchange_job(self.full, self.FIRST)
        return self.jobs["end_half"]

    def take_last(self):
        self.done.update(zip(self.FIRST, self.jobs["end_half"].landed))


class NoComm:
    def job(self, tag, gb=None):
        return None

    def landed(self, tag, wb=None, ws=None):
        pass

    def finish(self, gb):
        return gb


SMALL = ("rel_bias", "g_mix", "w_short_conv", "g_attn_out", "g_conv_out", "g_xattn", "g_mem", "g_ffn",
         "w_ffn_conv", "b_ffn_conv", "g_final")
SMALL_SHARDED = ("w_short_conv", "w_ffn_conv")


def kernel(x, mem, rel_bias, g_mix, w_in, w_short_conv, g_attn_out, g_conv_out, w_out, g_xattn, g_mem, w_xq, w_xk, w_xv, w_xo, g_ffn, w_up, w_ffn_conv, b_ffn_conv, w_down, g_final, loss_target, m_rel_bias, m_g_mix, m_w_in, m_w_short_conv, m_g_attn_out, m_g_conv_out, m_w_out, m_g_xattn, m_g_mem, m_w_xq, m_w_xk, m_w_xv, m_w_xo, m_g_ffn, m_w_up, m_w_ffn_conv, m_b_ffn_conv, m_w_down, m_g_final, v_rel_bias, v_g_mix, v_w_in, v_w_short_conv, v_g_attn_out, v_g_conv_out, v_w_out, v_g_xattn, v_g_mem, v_w_xq, v_w_xk, v_w_xv, v_w_xo, v_g_ffn, v_w_up, v_w_ffn_conv, v_b_ffn_conv, v_w_down, v_g_final):
    names = ("rel_bias", "g_mix", "w_in", "w_short_conv", "g_attn_out", "g_conv_out", "w_out", "g_xattn", "g_mem", "w_xq",
             "w_xk", "w_xv", "w_xo", "g_ffn", "w_up", "w_ffn_conv", "b_ffn_conv", "w_down", "g_final")
    W = dict(zip(names, (rel_bias, g_mix, w_in, w_short_conv, g_attn_out, g_conv_out, w_out, g_xattn, g_mem, w_xq, w_xk, w_xv,
                         w_xo, g_ffn, w_up, w_ffn_conv, b_ffn_conv, w_down, g_final)))
    M = dict(zip(names, (m_rel_bias, m_g_mix, m_w_in, m_w_short_conv, m_g_attn_out, m_g_conv_out, m_w_out, m_g_xattn, m_g_mem,
                         m_w_xq, m_w_xk, m_w_xv, m_w_xo, m_g_ffn, m_w_up, m_w_ffn_conv, m_b_ffn_conv, m_w_down, m_g_final)))
    V = dict(zip(names, (v_rel_bias, v_g_mix, v_w_in, v_w_short_conv, v_g_attn_out, v_g_conv_out, v_w_out, v_g_xattn, v_g_mem,
                         v_w_xq, v_w_xk, v_w_xv, v_w_xo, v_g_ffn, v_w_up, v_w_ffn_conv, v_b_ffn_conv, v_w_down, v_g_final)))
    xi, yi, ci = lax.axis_index("x"), lax.axis_index("y"), lax.axis_index("c")
    mine = 2 * xi + yi
    mine_arr = jnp.reshape(mine, (1,)).astype(jnp.int32)
    sel_arr = jnp.stack([mine, ci]).astype(jnp.int32)
    conv_placed = []
    for n in SMALL_SHARDED:
        shard = W[n][0]
        conv_placed.append(lax.dynamic_update_slice(jnp.zeros((8, 4 * shard.shape[1]), F32), shard, (0, mine * shard.shape[1])))
    shards = {n: W[n][0] for n in BIG}
    placed = {}
    for group in same_shape_groups(BIG, shards):
        outs = place_shards([shards[n] for n in group], mine_arr, group[0] in COL_SHARDED, "place_" + group[0])
        placed.update(zip(group, outs))
    comm = StepComm(placed, conv_placed, sel_arr)
    ws = {n: W[n] for n in SMALL if n not in SMALL_SHARDED}

    loss, grad_x, gfull, gs = local_step(x[0], mem[0], loss_target[0], None, ws, comm)

    def as2d(a):
        return a.reshape(1, -1) if a.ndim == 1 else a

    reduced = allreduce_small([as2d(gs[n]) for n in SMALL] + [loss], "reduce_small", job=comm.last_job())
    comm.take_last()
    gsm = dict(zip(SMALL, reduced[:-1]))
    loss = reduced[-1][0, 0]

    grads, delta, new_m, new_v = {}, {}, {}, {}
    for group in same_shape_groups(BIG, gfull):
        ds, nms, nvs = adamw([W[n][0] for n in group], [gfull[n] for n in group], [M[n][0] for n in group],
                             [V[n][0] for n in group], "adamw_" + group[0])
        for n, d, nm, nv in zip(group, ds, nms, nvs):
            grads[n], delta[n], new_m[n], new_v[n] = gfull[n][None], d[None], nm[None], nv[None]
    for n in SMALL_SHARDED:
        wid = W[n].shape[2]
        gsm[n] = lax.dynamic_slice(gsm[n], (0, mine * wid), (3, wid))

    def own(a, n):
        return a[0] if n in SMALL_SHARDED else as2d(a)

    d, nm, nv = adamw_multi([own(W[n], n) for n in SMALL], [gsm[n] for n in SMALL], [own(M[n], n) for n in SMALL],
                            [own(V[n], n) for n in SMALL], "adamw_small")
    for i, n in enumerate(SMALL):
        shape = W[n].shape
        grads[n], delta[n], new_m[n], new_v[n] = (a.reshape(shape) for a in (gsm[n], d[i], nm[i], nv[i]))
    return (loss, grad_x[None], *[grads[n] for n in names], *[delta[n] for n in names],
            *[new_m[n] for n in names], *[new_v[n] for n in names])
```

```python
import math

import numpy as np
import jax
import jax.numpy as jnp
from jax import lax
from jax.experimental import pallas as pl
from jax.experimental.pallas import tpu as pltpu

F32 = jnp.float32
BF16 = jnp.bfloat16
SDS = jax.ShapeDtypeStruct
MESH = pl.DeviceIdType.MESH

D_MODEL = 1024
ATTN_W = 512
N_HEADS = 8
HEAD_DIM = 64
HALF_HEAD = HEAD_DIM // 2
Q_SCALE = HEAD_DIM ** -0.5
BLK = 128
PATTERNS = ((128, 1), (512, 4), (2048, 16))
N_BUCKETS = 32
BUCKET_MAX_DIST = 2048
D_FF = 2816
N_MEM_HEADS = 4
MEM_HD = 256
EPS = 1e-6
NEG = -1e30
VMEM_LIMIT = 56 * 1024 * 1024

ADAM_LR, ADAM_B1, ADAM_B2, ADAM_EPS, ADAM_WD, ADAM_STEP = 0.001, 0.9, 0.999, 1e-08, 0.01, 10


def _cp(*sem):
    return pltpu.CompilerParams(dimension_semantics=sem, vmem_limit_bytes=VMEM_LIMIT)


def _dot(a, b):
    return jnp.dot(a, b, preferred_element_type=F32)


def _dot_nt(a, b):
    return lax.dot_general(a, b, (((1,), (1,)), ((), ())), preferred_element_type=F32)


def _dot_tn(a, b):
    return lax.dot_general(a, b, (((0,), (0,)), ((), ())), preferred_element_type=F32)


def _rsum(x):
    return jnp.sum(x, axis=1, keepdims=True)


def rmsnorm_fwd(x, g, name, job=None):
    S, Dm = x.shape
    tm = min(S, 512)

    def body(x_ref, g_ref, o_ref):
        xv = x_ref[...]
        r = lax.rsqrt(jnp.mean(xv * xv, axis=1, keepdims=True) + EPS)
        o_ref[...] = (xv * r * g_ref[...]).astype(o_ref.dtype)

    return _pcall(body, [x, g], grid=(S // tm,), name=name,
                  in_specs=[pl.BlockSpec((tm, Dm), lambda i: (i, 0)), pl.BlockSpec((1, Dm), lambda i: (0, 0))],
                  out_specs=[pl.BlockSpec((tm, Dm), lambda i: (i, 0))], out_shape=[SDS((S, Dm), BF16)],
                  sem=("parallel",), job=job)[0]


def rmsnorm_gain_grad(x, dh, name):
    S, Dm = x.shape
    tm = min(S, 512)

    def body(x_ref, dh_ref, dg_ref):
        xv = x_ref[...]
        xh = xv * lax.rsqrt(jnp.mean(xv * xv, axis=1, keepdims=True) + EPS)

        @pl.when(pl.program_id(0) == 0)
        def _():
            dg_ref[...] = jnp.zeros_like(dg_ref)

        dg_ref[...] += jnp.sum(dh_ref[...].astype(F32) * xh, axis=0, keepdims=True)

    row = pl.BlockSpec((tm, Dm), lambda i: (i, 0))
    return pl.pallas_call(
        body, grid=(S // tm,), name=name, in_specs=[row, row], out_specs=pl.BlockSpec((1, Dm), lambda i: (0, 0)),
        out_shape=SDS((1, Dm), F32), compiler_params=_cp("arbitrary"))(x, dh)


def _col_spec(bw, tn, rows, row_of, col_of):
    if bw is None:
        return pl.BlockSpec((rows, tn), lambda *g: (row_of(*g), col_of(*g)))
    if tn % bw == 0:
        return pl.BlockSpec((tn // bw, rows, bw), lambda *g: (col_of(*g), row_of(*g), 0))
    per = bw // tn
    return pl.BlockSpec((None, rows, tn), lambda *g: (col_of(*g) // per, row_of(*g), col_of(*g) % per))


def _read_cols(ref, bw, tn):
    if bw is None or tn % bw != 0:
        return ref[...]
    if tn == bw:
        return ref[0]
    return jnp.concatenate([ref[q] for q in range(tn // bw)], axis=1)


def _write_cols(ref, bw, tn, val):
    if bw is None or tn % bw != 0:
        ref[...] = val.astype(ref.dtype)
    else:
        for q in range(tn // bw):
            ref[q] = val[:, q * bw:(q + 1) * bw].astype(ref.dtype)


def mm_nn(a, b, name, *, out_dtype=F32, out_bw=None, tm=1024, tn=1024, job=None):
    M, K = a.shape
    N = b.shape[1]
    tm = min(tm, M)

    def body(a_ref, b_ref, o_ref):
        _write_cols(o_ref, out_bw, tn, _dot(a_ref[...].astype(BF16), b_ref[...]))

    ri, ci = (lambda i, j: i), (lambda i, j: j)
    in_specs = [pl.BlockSpec((tm, K), lambda i, j: (i, 0)), pl.BlockSpec((K, tn), lambda i, j: (0, j))]
    oshape = (M, N) if out_bw is None else (N // out_bw, M, out_bw)
    return _pcall(body, [a, b], grid=(M // tm, N // tn), name=name, in_specs=in_specs,
                  out_specs=[_col_spec(out_bw, tn, tm, ri, ci)], out_shape=[SDS(oshape, out_dtype)],
                  sem=("parallel", "parallel"), job=job)[0]


def mm_nn_pair(a, b1, b2, name, *, out_dtype=F32, tm=1024):
    M, K = a.shape
    N = b1.shape[1]
    tm = min(tm, M)

    def body(a_ref, b1_ref, b2_ref, o1_ref, o2_ref):
        av = a_ref[...].astype(BF16)
        o1_ref[...] = _dot(av, b1_ref[...]).astype(o1_ref.dtype)
        o2_ref[...] = _dot(av, b2_ref[...]).astype(o2_ref.dtype)

    w_spec = pl.BlockSpec((K, N), lambda i: (0, 0))
    o_spec = pl.BlockSpec((tm, N), lambda i: (i, 0))
    return _pcall(body, [a, b1, b2], grid=(M // tm,), name=name,
                  in_specs=[pl.BlockSpec((tm, K), lambda i: (i, 0)), w_spec, w_spec],
                  out_specs=[o_spec, o_spec], out_shape=[SDS((M, N), out_dtype)] * 2, sem=("parallel",))


def mm_nt_pair_sum(a1, b1, a2, b2, name, *, tm=1024):
    M, K = a1.shape
    N = b1.shape[0]
    tm = min(tm, M)

    def body(a1_ref, b1_ref, a2_ref, b2_ref, o_ref):
        first = _dot_nt(a1_ref[...].astype(BF16), b1_ref[...])
        o_ref[...] = _dot_nt(a2_ref[...].astype(BF16), b2_ref[...]) + first

    a_spec = pl.BlockSpec((tm, K), lambda i: (i, 0))
    w_spec = pl.BlockSpec((N, K), lambda i: (0, 0))
    return _pcall(body, [a1, b1, a2, b2], grid=(M // tm,), name=name, in_specs=[a_spec, w_spec, a_spec, w_spec],
                  out_specs=[pl.BlockSpec((tm, N), lambda i: (i, 0))], out_shape=[SDS((M, N), F32)],
                  sem=("parallel",))[0]


def mm_nn_rows(a, b, res, g, name, *, target=None, tm=512):
    M, K = a.shape
    N = b.shape[1]

    def body(*refs):
        a_ref, b_ref, r_ref, g_ref = refs[:4]
        xv = r_ref[...] + _dot(a_ref[...].astype(BF16), b_ref[...])
        r = lax.rsqrt(jnp.mean(xv * xv, axis=1, keepdims=True) + EPS)
        xh = xv * r
        if target is None:
            x_ref, h_ref = refs[4:]
            x_ref[...] = xv
            h_ref[...] = (xh * g_ref[...]).astype(BF16)
            return
        t_ref, loss_ref, dx_ref, dg_ref = refs[4:]
        err = xh * g_ref[...] - t_ref[...]
        dy = err / N
        gd = dy * g_ref[...]
        dx_ref[...] = r * (gd - xh * jnp.mean(gd * xh, axis=1, keepdims=True))

        @pl.when(pl.program_id(0) == 0)
        def _():
            dg_ref[...] = jnp.zeros_like(dg_ref)
            loss_ref[...] = jnp.zeros_like(loss_ref)

        dg_ref[...] += _colsum(dy * xh)
        loss_ref[...] += 0.5 * _colsum(jnp.mean(err * err, axis=1, keepdims=True))

    row = pl.BlockSpec((tm, N), lambda i: (i, 0))
    vec = pl.BlockSpec((1, N), lambda i: (0, 0))
    in_specs = [pl.BlockSpec((tm, K), lambda i: (i, 0)), pl.BlockSpec((K, N), lambda i: (0, 0)), row, vec]
    if target is None:
        return pl.pallas_call(body, grid=(M // tm,), name=name, in_specs=in_specs, out_specs=[row, row],
                              out_shape=[SDS((M, N), F32), SDS((M, N), BF16)], compiler_params=_cp("parallel"))(a, b, res, g)
    one = pl.BlockSpec((1, 1), lambda i: (0, 0))
    return pl.pallas_call(body, grid=(M // tm,), name=name, in_specs=in_specs + [row], out_specs=[one, row, vec],
                          out_shape=[SDS((1, 1), F32), SDS((M, N), F32), SDS((1, N), F32)],
                          compiler_params=_cp("arbitrary"))(a, b, res, g, target)


def mm_nt(a, a_bw, b, name, *, res=None, out_dtype=F32, tm=1024, tn=1024, job=None):
    if a_bw is None:
        M, K = a.shape
    else:
        M, K = a.shape[1], a.shape[0] * a_bw
    N = b.shape[0]
    tm = min(tm, M)

    def body(*refs):
        if res is None:
            a_ref, b_ref, o_ref = refs
        else:
            a_ref, b_ref, r_ref, o_ref = refs
        av = _read_cols(a_ref, a_bw, K).astype(BF16)
        acc = _dot_nt(av, b_ref[...])
        if res is not None:
            acc = acc + r_ref[...]
        o_ref[...] = acc.astype(o_ref.dtype)

    in_specs = [_col_spec(a_bw, K, tm, lambda i, j: i, lambda i, j: 0), pl.BlockSpec((tn, K), lambda i, j: (j, 0))]
    args = [a, b]
    if res is not None:
        in_specs.append(pl.BlockSpec((tm, tn), lambda i, j: (i, j)))
        args.append(res)
    return _pcall(body, args, grid=(M // tm, N // tn), name=name, in_specs=in_specs,
                  out_specs=[pl.BlockSpec((tm, tn), lambda i, j: (i, j))], out_shape=[SDS((M, N), out_dtype)],
                  sem=("parallel", "parallel"), job=job)[0]


def mm_nt_norm_bwd(a, a_bw, b, x, g, res, name, *, tm=512, job=None):
    if a_bw is None:
        M, K = a.shape
    else:
        M, K = a.shape[1], a.shape[0] * a_bw
    N = b.shape[0]

    def body(a_ref, b_ref, x_ref, g_ref, r_ref, dx_ref, dg_ref):
        dh = _dot_nt(_read_cols(a_ref, a_bw, K).astype(BF16), b_ref[...])
        xv = x_ref[...]
        r = lax.rsqrt(jnp.mean(xv * xv, axis=1, keepdims=True) + EPS)
        xh = xv * r
        gd = dh * g_ref[...]
        dx_ref[...] = r_ref[...] + r * (gd - xh * jnp.mean(gd * xh, axis=1, keepdims=True))

        @pl.when(pl.program_id(0) == 0)
        def _():
            dg_ref[...] = jnp.zeros_like(dg_ref)

        dg_ref[...] += jnp.sum(dh * xh, axis=0, keepdims=True)

    row = pl.BlockSpec((tm, N), lambda i: (i, 0))
    vec = pl.BlockSpec((1, N), lambda i: (0, 0))
    return _pcall(body, [a, b, x, g, res], grid=(M // tm,), name=name,
                  in_specs=[_col_spec(a_bw, K, tm, lambda i: i, lambda i: 0),
                            pl.BlockSpec((N, K), lambda i: (0, 0), pipeline_mode=pl.Buffered(1)), row, vec, row],
                  out_specs=[row, vec], out_shape=[SDS((M, N), F32), SDS((1, N), F32)], sem=("arbitrary",), job=job)


def mm_tn(a, b, b_bw, name, *, shards=None, tm=1024, tn=1024, ts=2048, job=None):
    S, Ka = a.shape
    N = b.shape[1] if b_bw is None else b.shape[0] * b_bw
    ts = min(ts, S)
    tm = min(tm, Ka)

    def body(a_ref, b_ref, o_ref):
        @pl.when(pl.program_id(2) == 0)
        def _():
            o_ref[...] = jnp.zeros_like(o_ref)

        bv = _read_cols(b_ref, b_bw, tn).astype(BF16)
        o_ref[...] += _dot_tn(a_ref[...].astype(BF16), bv)

    in_specs = [pl.BlockSpec((ts, tm), lambda i, j, k: (k, i)),
                _col_spec(b_bw, tn, ts, lambda i, j, k: k, lambda i, j, k: j)]
    if shards is None:
        out_spec, oshape = pl.BlockSpec((tm, tn), lambda i, j, k: (i, j)), (Ka, N)
    else:
        per = (N // shards) // tn
        out_spec = pl.BlockSpec((None, tm, tn), lambda i, j, k: (j // per, i, j % per))
        oshape = (shards, Ka, N // shards)
    return _pcall(body, [a, b], grid=(Ka // tm, N // tn, S // ts), name=name, in_specs=in_specs, out_specs=[out_spec],
                  out_shape=[SDS(oshape, F32)], sem=("parallel", "parallel", "arbitrary"), job=job)[0]


def mm_tn_pair(a, b1, b2, name, *, ts=2048):
    S, Ka = a.shape
    N = b1.shape[1]
    ts = min(ts, S)

    def body(a_ref, b1_ref, b2_ref, o1_ref, o2_ref):
        @pl.when(pl.program_id(0) == 0)
        def _():
            o1_ref[...] = jnp.zeros_like(o1_ref)
            o2_ref[...] = jnp.zeros_like(o2_ref)

        av = a_ref[...].astype(BF16)
        o1_ref[...] += _dot_tn(av, b1_ref[...].astype(BF16))
        o2_ref[...] += _dot_tn(av, b2_ref[...].astype(BF16))

    b_spec = pl.BlockSpec((ts, N), lambda k: (k, 0))
    o_spec = pl.BlockSpec((Ka, N), lambda k: (0, 0))
    return _pcall(body, [a, b1, b2], grid=(S // ts,), name=name,
                  in_specs=[pl.BlockSpec((ts, Ka), lambda k: (k, 0)), b_spec, b_spec],
                  out_specs=[o_spec, o_spec], out_shape=[SDS((Ka, N), F32)] * 2, sem=("arbitrary",))


def mm_tn_nt(a, g, w, name, *, ts=1024):
    S, Ka = a.shape
    N = g.shape[1]
    Nw = w.shape[0]
    ts = min(ts, S)

    def body(a_ref, g_ref, w_ref, dw_ref, dx_ref):
        @pl.when(pl.program_id(0) == 0)
        def _():
            dw_ref[...] = jnp.zeros_like(dw_ref)

        gv = g_ref[...].astype(BF16)
        dw_ref[...] += _dot_tn(a_ref[...].astype(BF16), gv)
        dx_ref[...] = _dot_nt(gv, w_ref[...]).astype(dx_ref.dtype)

    return _pcall(body, [a, g, w], grid=(S // ts,), name=name,
                  in_specs=[pl.BlockSpec((ts, Ka), lambda k: (k, 0)), pl.BlockSpec((ts, N), lambda k: (k, 0)),
                            pl.BlockSpec((Nw, N), lambda k: (0, 0))],
                  out_specs=[pl.BlockSpec((Ka, N), lambda k: (0, 0)), pl.BlockSpec((ts, Nw), lambda k: (k, 0))],
                  out_shape=[SDS((Ka, N), F32), SDS((S, Nw), BF16)], sem=("arbitrary",))


KEYS = 2 * BLK


def _bucket_tables():
    out = np.zeros((3, BLK, KEYS), np.int32)
    qi = np.arange(BLK)[:, None]
    kj = np.arange(KEYS)[None, :]
    for p, (win, dil) in enumerate(PATTERNS):
        w = win // dil
        assert w == BLK
        steps = qi + w - kj
        valid = (steps >= 0) & (steps <= w)
        dist = np.clip(steps, 0, w) * dil
        dd = np.maximum(dist, 1).astype(np.float32)
        exact = N_BUCKETS // 2
        large = exact + (np.log(dd / np.float32(exact)) / np.float32(math.log(BUCKET_MAX_DIST / exact))
                         * np.float32(N_BUCKETS - exact)).astype(np.int32)
        large = np.minimum(large, N_BUCKETS - 1)
        out[p] = np.where(valid, np.where(dist < exact, dist, large), -1)
    return out


def bias_tables(rel_bias):
    bkt = jnp.asarray(_bucket_tables())

    def body(rb_ref, bkt_ref, o_ref):
        bk = bkt_ref[...]
        for h in range(N_HEADS):
            acc = jnp.full((BLK, KEYS), NEG, F32)
            for b in range(N_BUCKETS):
                acc = jnp.where(bk == b, rb_ref[h, b], acc)
            o_ref[h] = acc

    return pl.pallas_call(
        body, grid=(3,), name="bias_tables",
        in_specs=[pl.BlockSpec(memory_space=pltpu.SMEM),
                  pl.BlockSpec((None, BLK, KEYS), lambda p: (p, 0, 0))],
        out_specs=pl.BlockSpec((None, N_HEADS, BLK, KEYS), lambda p: (p, 0, 0, 0)),
        out_shape=SDS((3, N_HEADS, BLK, KEYS), F32), compiler_params=_cp("parallel"))(rel_bias, bkt)


def bias_tables_bwd(ds_sum, job=None):
    bkt = jnp.asarray(_bucket_tables())

    def body(ds_ref, bkt_ref, o_ref):
        h = pl.program_id(0)

        @pl.when(h == 0)
        def _():
            o_ref[...] = jnp.zeros_like(o_ref)

        rows = lax.broadcasted_iota(jnp.int32, (N_HEADS, N_BUCKETS), 0)
        cols = lax.broadcasted_iota(jnp.int32, (N_HEADS, N_BUCKETS), 1)
        acc = jnp.zeros((N_HEADS, N_BUCKETS), F32)
        for b in range(N_BUCKETS):
            t = jnp.zeros((BLK, KEYS), F32)
            for p in range(3):
                t = t + jnp.where(bkt_ref[p] == b, ds_ref[p], 0.0)
            tot = jnp.sum(_rsum(t), axis=0, keepdims=True)
            acc = acc + jnp.where((rows == h) & (cols == b), tot, 0.0)
        o_ref[...] += acc

    return _pcall(
        body, [ds_sum, bkt], grid=(N_HEADS,), name="bias_tables_bwd",
        in_specs=[pl.BlockSpec((3, None, BLK, KEYS), lambda h: (0, h, 0, 0)),
                  pl.BlockSpec((3, BLK, KEYS), lambda h: (0, 0, 0))],
        out_specs=[pl.BlockSpec((N_HEADS, N_BUCKETS), lambda h: (0, 0))],
        out_shape=[SDS((N_HEADS, N_BUCKETS), F32)], sem=("arbitrary",), job=job)[0]


def _rows(start, dil):
    if dil == 1:
        return pl.ds(pl.multiple_of(start, BLK), BLK)
    return pl.ds(start, BLK, stride=dil)


GRP = 8


def _group_rows(i, dil, S):
    nb = S // (BLK * dil)
    run = min(nb, GRP)
    chunks = nb // run
    res0, b0 = (i // chunks) * (GRP // run), (i % chunks) * run
    cur = [(b0 + j % run) * (BLK * dil) + res0 + j // run for j in range(GRP)]
    t = lax.broadcasted_iota(jnp.int32, (GRP, 1, 1), 0)
    if run == nb:
        before, pen = None, jnp.where(t % run == 0, NEG, 0.0)
    else:
        before = _rows(jnp.maximum(b0 - 1, 0) * (BLK * dil) + res0, dil)
        pen = jnp.where((t == 0) & (b0 == 0), NEG, 0.0)
    return [_rows(s, dil) for s in cur], before, pen


def _load_group(ref, rows):
    return jnp.stack([ref[r, :] for r in rows])


def _with_prev(ref, before, cur_blocks):
    first = cur_blocks[:1] if before is None else ref[before, :][None].astype(cur_blocks.dtype)
    return jnp.concatenate([first, cur_blocks[:-1]], axis=0)


def _bdot_nt(a, b):
    return lax.dot_general(a, b, (((2,), (2,)), ((0,), (0,))), preferred_element_type=F32)


def _bdot(a, b):
    return lax.dot_general(a, b, (((2,), (1,)), ((0,), (0,))), preferred_element_type=F32)


def _bdot_tn(a, b):
    return lax.dot_general(a, b, (((1,), (1,)), ((0,), (0,))), preferred_element_type=F32)


def _lsum(x):
    return jnp.sum(x, axis=-1, keepdims=True)


def _widen(src, dst):
    S = src.shape[1]

    def chunk(i, carry):
        rows = pl.ds(pl.multiple_of(i * 512, 512), 512)
        for a in range(3):
            dst[a, rows, :] = src[a, rows, :].astype(F32)
        return carry

    lax.fori_loop(0, S // 512, chunk, 0)


def attn_fwd(qkv, bias, job=None):
    S = qkv.shape[3]
    nblk = S // BLK
    car = Carried(job, 2, 2, 3, 4)

    def body(*refs):
        (qkv_in, bias_ref, o_ref, lse_ref, qkv_ref, part_o, part_ml), jrefs = car.split(refs)
        car.phase(0, pl.program_id(0), jrefs)
        _widen(qkv_in, qkv_ref)
        lane = lax.broadcasted_iota(jnp.int32, (GRP, BLK, BLK), 2)
        lo = lane < HEAD_DIM
        masks = (lo, jnp.logical_not(lo))
        lov = lax.broadcasted_iota(jnp.int32, (GRP, KEYS, BLK), 2) < HEAD_DIM
        vmasks = (lov, jnp.logical_not(lov))
        prev_keys = lax.broadcasted_iota(jnp.int32, (1, 1, KEYS), 2) < BLK
        q_ref, k_ref, v_ref = qkv_ref.at[0], qkv_ref.at[1], qkv_ref.at[2]
        targets = [(o_ref, lse_ref)] + [(part_o.at[t], part_ml.at[t]) for t in range(len(PATTERNS) - 1)]
        for p, (_, dil) in enumerate(PATTERNS):
            def step(i, carry, p=p, dil=dil):
                rc, before, pen = _group_rows(i, dil, S)
                q2 = _load_group(q_ref, rc) * Q_SCALE
                kc = _load_group(k_ref, rc).astype(BF16)
                keys = jnp.concatenate([_with_prev(k_ref, before, kc), kc], axis=1)
                vc = _load_group(v_ref, rc)
                vals = jnp.concatenate([_with_prev(v_ref, before, vc), vc], axis=1)
                pen = jnp.where(prev_keys, pen, 0.0)
                pv = jnp.zeros((GRP, BLK, BLK), F32)
                ms, ls = [], []
                for h in range(2):
                    qh = jnp.where(masks[h], q2, 0.0).astype(BF16)
                    s = _bdot_nt(qh, keys) + bias_ref[p, h][None] + pen
                    mn = jnp.max(s, axis=-1, keepdims=True)
                    pr = jnp.exp(s - mn)
                    pv = pv + _bdot(pr.astype(BF16), jnp.where(vmasks[h], vals, 0.0).astype(BF16))
                    ms.append(mn)
                    ls.append(_lsum(pr))
                ml_t = jnp.where(lo, jnp.where(lane < HALF_HEAD, ms[0], ls[0]),
                                 jnp.where(lane < HEAD_DIM + HALF_HEAD, ms[1], ls[1]))
                acc_ref, ml_ref = targets[p]
                for j, r in enumerate(rc):
                    acc_ref[r, :] = pv[j]
                    ml_ref[r, :] = ml_t[j]
                return carry

            lax.fori_loop(0, nblk // GRP, step, 0)

        def merge(i, carry):
            rows = pl.ds(pl.multiple_of(i * 512, 512), 512)
            is_m = (lax.broadcasted_iota(jnp.int32, (512, BLK), 1) & HALF_HEAD) == 0
            accs = [ref[rows, :] for ref, _ in targets]
            mls = [ref[rows, :] for _, ref in targets]
            mx = [jnp.where(is_m, ml, pltpu.roll(ml, HALF_HEAD, 1)) for ml in mls]
            dn = [jnp.where(is_m, pltpu.roll(ml, BLK - HALF_HEAD, 1), ml) for ml in mls]
            top = jnp.maximum(jnp.maximum(mx[0], mx[1]), mx[2])
            ws_ = [jnp.exp(m - top) for m in mx]
            den = ws_[0] * dn[0] + ws_[1] * dn[1] + ws_[2] * dn[2]
            o_ref[rows, :] = (ws_[0] * accs[0] + ws_[1] * accs[1] + ws_[2] * accs[2]) / den
            lse_ref[rows, :] = top + jnp.log(den)
            return carry

        lax.fori_loop(0, S // 512, merge, 0)
        car.phase(1, pl.program_id(0), jrefs)
        car.phase(2, pl.program_id(0), jrefs)

    outs = pl.pallas_call(
        body, grid=(4,), name="attn_fwd",
        in_specs=[pl.BlockSpec((None, 3, None, S, BLK), lambda g: (0, 0, g, 0, 0)),
                  pl.BlockSpec((3, 2, BLK, KEYS), lambda g: (0, g, 0, 0))] + car.in_specs(),
        out_specs=[pl.BlockSpec((None, S, BLK), lambda g: (g, 0, 0)),
                   pl.BlockSpec((None, S, BLK), lambda g: (g, 0, 0))] + car.out_specs(),
        out_shape=[SDS((4, S, BLK), F32), SDS((4, S, BLK), F32)] + car.out_shapes(),
        input_output_aliases=car.aliases(),
        scratch_shapes=[pltpu.VMEM((3, S, BLK), F32), pltpu.VMEM((2, S, BLK), F32), pltpu.VMEM((2, S, BLK), F32)] + car.sems(),
        compiler_params=_cp("arbitrary"))(qkv, bias, *car.args())
    if job is not None:
        job.landed = list(outs[2:])
    return outs[0], outs[1]


def attn_bwd(qkv, bias, o, lse, do, job=None):
    S = qkv.shape[3]
    nblk = S // BLK
    car = Carried(job, 5, 2, 4, 4)

    def body(*refs):
        (qkv_in, bias_ref, o_ref, lse_ref, do_ref, dqkv_out, ds_ref, qkv_ref, dqkv_ref, last_ref, st_ref), jrefs = car.split(refs)
        car.phase(0, pl.program_id(0), jrefs)
        _widen(qkv_in, qkv_ref)

        def stats(i, carry):
            rows = pl.ds(pl.multiple_of(i * 512, 512), 512)
            ln = lax.broadcasted_iota(jnp.int32, (512, BLK), 1)
            prod = do_ref[rows, :] * o_ref[rows, :]
            d0 = _rsum(jnp.where(ln < HEAD_DIM, prod, 0.0))
            d1 = _rsum(jnp.where(ln < HEAD_DIM, 0.0, prod))
            st_ref[rows, :] = jnp.where((ln & HALF_HEAD) == 0, lse_ref[rows, :], jnp.where(ln < HEAD_DIM, d0, d1))
            return carry

        lax.fori_loop(0, S // 512, stats, 0)
        lane = lax.broadcasted_iota(jnp.int32, (GRP, BLK, BLK), 2)
        lo = lane < HEAD_DIM
        masks = (lo, jnp.logical_not(lo))
        lov = lax.broadcasted_iota(jnp.int32, (GRP, KEYS, BLK), 2) < HEAD_DIM
        vmasks = (lov, jnp.logical_not(lov))
        prev_keys = lax.broadcasted_iota(jnp.int32, (1, 1, KEYS), 2) < BLK
        ds_ref[...] = jnp.zeros_like(ds_ref)
        q_ref, k_ref, v_ref = qkv_ref.at[0], qkv_ref.at[1], qkv_ref.at[2]
        last = len(PATTERNS) - 1
        for p, (_, dil) in enumerate(PATTERNS):
            def step(i, carry, p=p, dil=dil):
                tgt = last_ref if p == last else dqkv_ref
                dq_ref, dk_ref, dv_ref = tgt.at[0], tgt.at[1], tgt.at[2]
                rc, before, pen = _group_rows(i, dil, S)
                q2 = _load_group(q_ref, rc) * Q_SCALE
                kc = _load_group(k_ref, rc).astype(BF16)
                keys = jnp.concatenate([_with_prev(k_ref, before, kc), kc], axis=1)
                vc = _load_group(v_ref, rc).astype(BF16)
                vals = jnp.concatenate([_with_prev(v_ref, before, vc), vc], axis=1)
                pen = jnp.where(prev_keys, pen, 0.0)
                dot = _load_group(do_ref, rc)
                st = _load_group(st_ref, rc)
                dq = jnp.zeros((GRP, BLK, BLK), F32)
                dkeys = jnp.zeros((GRP, KEYS, BLK), F32)
                dvals = jnp.zeros((GRP, KEYS, BLK), F32)
                for h in range(2):
                    qh = jnp.where(masks[h], q2, 0.0).astype(BF16)
                    doh = jnp.where(masks[h], dot, 0.0).astype(BF16)
                    lh = st[:, :, HEAD_DIM * h:HEAD_DIM * h + 1]
                    delta = st[:, :, HEAD_DIM * h + HALF_HEAD:HEAD_DIM * h + HALF_HEAD + 1]
                    pr = jnp.exp(_bdot_nt(qh, keys) + bias_ref[p, h][None] + pen - lh)
                    ds = pr * (_bdot_nt(doh, vals) - delta)
                    ds_ref[p, h] += jnp.sum(ds, axis=0)
                    dsb = ds.astype(BF16)
                    dq = dq + jnp.where(masks[h], _bdot(dsb, keys), 0.0)
                    dkeys = dkeys + _bdot_tn(dsb, qh)
                    dvals = dvals + _bdot_tn(pr.astype(BF16), doh)
                dkp, dkc, dvp, dvc = dkeys[:, :BLK], dkeys[:, BLK:], dvals[:, :BLK], dvals[:, BLK:]
                none = jnp.zeros((1, BLK, BLK), F32)
                dkc = dkc + jnp.concatenate([dkp[1:], none], axis=0)
                dvc = dvc + jnp.concatenate([dvp[1:], none], axis=0)
                for j in range(GRP):
                    if p == 1:
                        dq_ref[rc[j], :] += dq[j] * Q_SCALE
                        dk_ref[rc[j], :] += dkc[j]
                        dv_ref[rc[j], :] += dvc[j]
                    else:
                        dq_ref[rc[j], :] = dq[j] * Q_SCALE
                        dk_ref[rc[j], :] = dkc[j]
                        dv_ref[rc[j], :] = dvc[j]
                if before is not None:
                    dk_ref[before, :] += dkp[0]
                    dv_ref[before, :] += dvp[0]
                return carry

            lax.fori_loop(0, nblk // GRP, step, 0)

        def narrow(i, carry):
            rows = pl.ds(pl.multiple_of(i * 512, 512), 512)
            for a in range(3):
                dqkv_out[a, rows, :] = (dqkv_ref[a, rows, :] + last_ref[a, rows, :]).astype(BF16)
            return carry

        lax.fori_loop(0, S // 512, narrow, 0)
        car.phase(1, pl.program_id(0), jrefs)
        car.phase(2, pl.program_id(0), jrefs)

    blk = pl.BlockSpec((None, S, BLK), lambda g: (g, 0, 0))
    outs = pl.pallas_call(
        body, grid=(4,), name="attn_bwd",
        in_specs=[pl.BlockSpec((None, 3, None, S, BLK), lambda g: (0, 0, g, 0, 0)),
                  pl.BlockSpec((3, 2, BLK, KEYS), lambda g: (0, g, 0, 0)), blk, blk, blk] + car.in_specs(),
        out_specs=[pl.BlockSpec((None, 3, None, S, BLK), lambda g: (0, 0, g, 0, 0)),
                   pl.BlockSpec((3, 2, BLK, KEYS), lambda g: (0, g, 0, 0))] + car.out_specs(),
        out_shape=[SDS((2, 3, 4, S, BLK), BF16), SDS((3, N_HEADS, BLK, KEYS), F32)] + car.out_shapes(),
        input_output_aliases=car.aliases(),
        scratch_shapes=[pltpu.VMEM((3, S, BLK), F32)] * 3 + [pltpu.VMEM((S, BLK), F32)] + car.sems(),
        compiler_params=_cp("arbitrary"))(qkv, bias, o, lse, do, *car.args())
    if job is not None:
        job.landed = list(outs[2:])
    return outs[0], outs[1]


def _shift_down(u, k, halo):
    n = u.shape[0]
    row = lax.broadcasted_iota(jnp.int32, u.shape, 0)
    out = pltpu.roll(u, k, 0)
    hn = halo.shape[0]
    for j in range(k):
        out = jnp.where(row == j, halo[hn - k + j:hn - k + j + 1, :], out)
    return out


def _shift_up(u, k, halo):
    n = u.shape[0]
    row = lax.broadcasted_iota(jnp.int32, u.shape, 0)
    out = pltpu.roll(u, n - k, 0)
    for j in range(k):
        out = jnp.where(row == n - k + j, halo[j:j + 1, :], out)
    return out


def _conv3(u, halo, w0, w1, w2):
    return _shift_down(u, 2, halo) * w0 + _shift_down(u, 1, halo) * w1 + u * w2


def _colsum(x):
    return jnp.sum(x, axis=0, keepdims=True)


def _mixer_specs(S, tm):
    conv = pl.BlockSpec((None, 12, tm, BLK), lambda i: (1, 0, i, 0))
    halo = pl.BlockSpec((None, 12, 16, BLK), lambda i: (1, 0, jnp.maximum(i * (tm // 16) - 1, 0), 0))
    ob = pl.BlockSpec((4, tm, BLK), lambda i: (0, i, 0))
    return conv, halo, ob


def _mixer_recompute(i, o_ref, pr_ref, ph_ref, w_ref):
    ob = [o_ref[q] for q in range(4)]
    gb = [pr_ref[q].astype(F32) for q in range(4)]
    gc = [pr_ref[4 + q].astype(F32) for q in range(4)]
    xi = [pr_ref[8 + q].astype(F32) for q in range(4)]
    keep = jnp.where(i > 0, 1.0, 0.0)
    u = [gc[q] * xi[q] for q in range(4)]
    hu = [ph_ref[4 + q].astype(F32) * ph_ref[8 + q].astype(F32) * keep for q in range(4)]
    w = [[w_ref[k:k + 1, q * BLK:(q + 1) * BLK] for k in range(3)] for q in range(4)]
    cv = [_conv3(u[q], hu[q], *w[q]) for q in range(4)]
    return ob, gb, gc, xi, u, hu, cv, w


def _rms_blocks(blocks):
    ss = sum(_rsum(b * b) for b in blocks)
    return lax.rsqrt(ss / (BLK * len(blocks)) + EPS)


def mixer_post_fwd(o, proj, w_sc, g_a, g_c):
    S = o.shape[1]
    tm = 512

    def body(o_ref, pr_ref, ph_ref, w_ref, ga_ref, gc_ref, m_ref):
        i = pl.program_id(0)
        ob, gb, _, _, _, _, cv, _ = _mixer_recompute(i, o_ref, pr_ref, ph_ref, w_ref)
        conv = [gb[q] * cv[q] for q in range(4)]
        ra, rc = _rms_blocks(ob), _rms_blocks(conv)
        for q in range(4):
            sl = slice(q * BLK, (q + 1) * BLK)
            m_ref[:, q * BLK:(q + 1) * BLK] = (ob[q] * ra * ga_ref[:, sl]).astype(BF16)
            m_ref[:, ATTN_W + q * BLK:ATTN_W + (q + 1) * BLK] = (conv[q] * rc * gc_ref[:, sl]).astype(BF16)

    conv_s, halo_s, ob_s = _mixer_specs(S, tm)
    full = lambda r, c: pl.BlockSpec((r, c), lambda i: (0, 0))
    return pl.pallas_call(
        body, grid=(S // tm,), name="mixer_post_fwd",
        in_specs=[ob_s, conv_s, halo_s, full(3, 512), full(1, 512), full(1, 512)],
        out_specs=pl.BlockSpec((tm, D_MODEL), lambda i: (i, 0)), out_shape=SDS((S, D_MODEL), BF16),
        compiler_params=_cp("parallel"))(o, proj, proj, w_sc, g_a, g_c)


def mixer_post_bwd_a(dmixed, o, proj, w_sc, g_a, g_c, job=None):
    S = o.shape[1]
    tm = 512

    def body(dm_ref, o_ref, pr_ref, ph_ref, w_ref, ga_ref, gc_ref, do_ref, dgb_ref, dcv_ref, dga_ref, dgc_ref):
        i = pl.program_id(0)
        ob, gb, _, _, _, _, cv, _ = _mixer_recompute(i, o_ref, pr_ref, ph_ref, w_ref)
        conv = [gb[q] * cv[q] for q in range(4)]
        ra, rc = _rms_blocks(ob), _rms_blocks(conv)

        @pl.when(i == 0)
        def _():
            dga_ref[...] = jnp.zeros_like(dga_ref)
            dgc_ref[...] = jnp.zeros_like(dgc_ref)

        for blocks, r, g_ref, off, dg_ref, is_attn in ((ob, ra, ga_ref, 0, dga_ref, True), (conv, rc, gc_ref, ATTN_W, dgc_ref, False)):
            xh = [blocks[q] * r for q in range(4)]
            dm = [dm_ref[:, off + q * BLK:off + (q + 1) * BLK].astype(F32) for q in range(4)]
            gd = [dm[q] * g_ref[:, q * BLK:(q + 1) * BLK] for q in range(4)]
            mean = sum(_rsum(gd[q] * xh[q]) for q in range(4)) / (4 * BLK)
            for q in range(4):
                dg_ref[:, q * BLK:(q + 1) * BLK] += _colsum(dm[q] * xh[q])
                dx = r * (gd[q] - xh[q] * mean)
                if is_attn:
                    do_ref[q] = dx
                else:
                    dgb_ref[q] = dx * cv[q]
                    dcv_ref[q] = dx * gb[q]

    conv_s, halo_s, ob_s = _mixer_specs(S, tm)
    full = lambda r, c: pl.BlockSpec((r, c), lambda i: (0, 0))
    return _pcall(
        body, [dmixed, o, proj, proj, w_sc, g_a, g_c], grid=(S // tm,), name="mixer_post_bwd_a",
        in_specs=[pl.BlockSpec((tm, D_MODEL), lambda i: (i, 0)), ob_s, conv_s, halo_s, full(3, 512), full(1, 512), full(1, 512)],
        out_specs=[ob_s, ob_s, ob_s, full(1, 512), full(1, 512)],
        out_shape=[SDS((4, S, BLK), F32)] * 3 + [SDS((1, 512), F32)] * 2, sem=("arbitrary",), job=job)


def mixer_post_bwd_b(dproj, dgb, dcv, proj, w_sc):
    S = proj.shape[2]
    tm = 512
    last = S // 8 - 1

    def body(dp_in, dgb_ref, dcv_ref, dn_ref, pr_ref, w_ref, dp_ref, dw_ref):
        i = pl.program_id(0)
        keep_next = jnp.where(i < pl.num_programs(0) - 1, 1.0, 0.0)

        @pl.when(i == 0)
        def _():
            dw_ref[...] = jnp.zeros_like(dw_ref)

        for q in range(4):
            sl = slice(q * BLK, (q + 1) * BLK)
            gc, xi = pr_ref[4 + q].astype(F32), pr_ref[8 + q].astype(F32)
            u = gc * xi
            dcv = dcv_ref[q]
            dn = dn_ref[q] * keep_next
            d1, d2 = _shift_up(dcv, 1, dn), _shift_up(dcv, 2, dn)
            w0, w1, w2 = (w_ref[k:k + 1, sl] for k in range(3))
            du = dcv * w2 + d1 * w1 + d2 * w0
            dw_ref[0:1, sl] += _colsum(d2 * u)
            dw_ref[1:2, sl] += _colsum(d1 * u)
            dw_ref[2:3, sl] += _colsum(dcv * u)
            dp_ref[q] = dgb_ref[q].astype(BF16)
            dp_ref[4 + q] = (du * xi).astype(BF16)
            dp_ref[8 + q] = (du * gc).astype(BF16)

    conv_s, _, ob_s = _mixer_specs(S, tm)
    nxt = pl.BlockSpec((4, 8, BLK), lambda i: (0, jnp.minimum((i + 1) * (tm // 8), last), 0))
    full = lambda r, c: pl.BlockSpec((r, c), lambda i: (0, 0))
    return pl.pallas_call(
        body, grid=(S // tm,), name="mixer_post_bwd_b",
        in_specs=[pl.BlockSpec(memory_space=pl.ANY), ob_s, ob_s, nxt, conv_s, full(3, 512)],
        out_specs=[conv_s, full(3, 512)],
        out_shape=[SDS(dproj.shape, BF16), SDS((3, 512), F32)],
        input_output_aliases={0: 0}, compiler_params=_cp("arbitrary"))(dproj, dgb, dcv, dcv, proj, w_sc)


def xattn_fwd(q, k, v):
    S = q.shape[0]
    tm = 512
    scale = MEM_HD ** -0.5

    def body(q_ref, k_ref, v_ref, o_ref):
        for h in range(N_MEM_HEADS):
            sl = slice(h * MEM_HD, (h + 1) * MEM_HD)
            s = _dot_nt(q_ref[:, sl], k_ref[:, sl]) * scale
            p = jnp.exp(s - jnp.max(s, axis=1, keepdims=True))
            p = p / _rsum(p)
            o_ref[:, sl] = _dot(p.astype(BF16), v_ref[:, sl]).astype(BF16)

    row = pl.BlockSpec((tm, D_MODEL), lambda i: (i, 0))
    kv = pl.BlockSpec(k.shape, lambda i: (0, 0))
    return pl.pallas_call(body, grid=(S // tm,), name="xattn_fwd", in_specs=[row, kv, kv], out_specs=row,
                          out_shape=SDS((S, D_MODEL), BF16), compiler_params=_cp("parallel"))(q, k, v)


def xattn_bwd(q, k, v, do):
    S = q.shape[0]
    tm = 512
    scale = MEM_HD ** -0.5

    def body(q_ref, k_ref, v_ref, do_ref, dq_ref, dk_ref, dv_ref):
        @pl.when(pl.program_id(0) == 0)
        def _():
            dk_ref[...] = jnp.zeros_like(dk_ref)
            dv_ref[...] = jnp.zeros_like(dv_ref)

        for h in range(N_MEM_HEADS):
            sl = slice(h * MEM_HD, (h + 1) * MEM_HD)
            qh, kh, vh, doh = q_ref[:, sl], k_ref[:, sl], v_ref[:, sl], do_ref[:, sl]
            s = _dot_nt(qh, kh) * scale
            p = jnp.exp(s - jnp.max(s, axis=1, keepdims=True))
            p = p / _rsum(p)
            dp = _dot_nt(doh, vh)
            ds = (p * (dp - _rsum(p * dp)) * scale).astype(BF16)
            dq_ref[:, sl] = _dot(ds, kh).astype(BF16)
            dk_ref[:, sl] += _dot_tn(ds, qh)
            dv_ref[:, sl] += _dot_tn(p.astype(BF16), doh)

    row = pl.BlockSpec((tm, D_MODEL), lambda i: (i, 0))
    kv = pl.BlockSpec(k.shape, lambda i: (0, 0))
    return pl.pallas_call(body, grid=(S // tm,), name="xattn_bwd", in_specs=[row, kv, kv, row], out_specs=[row, kv, kv],
                          out_shape=[SDS((S, D_MODEL), BF16), SDS(k.shape, F32), SDS(k.shape, F32)],
                          compiler_params=_cp("arbitrary"))(q, k, v, do)


FFN_TM, FFN_TC = 512, 1408


def _ffn_specs(S):
    tm, tc = FFN_TM, FFN_TC
    blk = pl.BlockSpec((2, tm, tc), lambda j, i: (0, i, j))
    nxt = pl.BlockSpec((2, 16, tc), lambda j, i: (0, jnp.minimum((i + 1) * (tm // 16), S // 16 - 1), j))
    wsp = pl.BlockSpec((2, 3, tc), lambda j, i: (0, 0, j))
    bsp = pl.BlockSpec((2, 1, tc), lambda j, i: (0, 0, j))
    return blk, nxt, wsp, bsp


def up_act_fwd(h, w_up, w_fc, b_fc, job=None):
    S, K = h.shape
    tm, tc = FFN_TM, FFN_TC
    nj = D_FF // tc

    def body(h_ref, wg_ref, wv_ref, w_ref, b_ref, pre_ref, gv_ref, a_ref, halo):
        i = pl.program_id(1)
        hv = h_ref[...]
        res = []
        for half, wt_ref in enumerate((wg_ref, wv_ref)):
            u = _dot(hv, wt_ref[...])
            hu = jnp.where(i > 0, halo[half], 0.0)
            halo[half] = u[tm - 8:, :]
            w0, w1, w2 = (w_ref[half, k:k + 1, :] for k in range(3))
            pre_ref[half] = u.astype(BF16)
            res.append(_conv3(u, hu, w0, w1, w2) + b_ref[half])
        g, v = res
        gv_ref[0] = g.astype(BF16)
        gv_ref[1] = v.astype(BF16)
        a_ref[...] = (g * jax.nn.sigmoid(g) * v).astype(BF16)

    blk = pl.BlockSpec((2, tm, tc), lambda j, i: (0, i, j))
    return _pcall(
        body, [h, w_up, w_up, w_fc, b_fc], grid=(nj, S // tm), name="up_act_fwd",
        in_specs=[pl.BlockSpec((tm, K), lambda j, i: (i, 0)), pl.BlockSpec((K, tc), lambda j, i: (0, j)),
                  pl.BlockSpec((K, tc), lambda j, i: (0, j + nj)), pl.BlockSpec((2, 3, tc), lambda j, i: (0, 0, j)),
                  pl.BlockSpec((2, 1, tc), lambda j, i: (0, 0, j))],
        out_specs=[blk, blk, pl.BlockSpec((tm, tc), lambda j, i: (i, j))],
        out_shape=[SDS((2, S, D_FF), BF16), SDS((2, S, D_FF), BF16), SDS((S, D_FF), BF16)],
        scratch=[pltpu.VMEM((2, 8, tc), F32)], sem=("parallel", "arbitrary"), job=job)


def ffn_act_bwd(dx, w_down, up, up_pre, w_fc, job=None):
    S = up.shape[1]

    def act_grads(da, g, v):
        sg = jax.nn.sigmoid(g)
        return da * v * (sg * (1.0 + g * (1.0 - sg))), da * g * sg

    def body(dx_ref, dxn_ref, wd_ref, gv_ref, gvn_ref, up_ref, w_ref, dp_ref, dw_ref, db_ref):
        i = pl.program_id(1)
        keep_next = jnp.where(i < pl.num_programs(1) - 1, 1.0, 0.0)

        @pl.when(i == 0)
        def _():
            dw_ref[...] = jnp.zeros_like(dw_ref)
            db_ref[...] = jnp.zeros_like(db_ref)

        wd = wd_ref[...]
        da = _dot_nt(dx_ref[...].astype(BF16), wd)
        dan = _dot_nt(dxn_ref[...].astype(BF16), wd) * keep_next
        here = act_grads(da, gv_ref[0].astype(F32), gv_ref[1].astype(F32))
        after = act_grads(dan, gvn_ref[0].astype(F32), gvn_ref[1].astype(F32))
        for half in range(2):
            d, dn = here[half], after[half]
            u = up_ref[half].astype(F32)
            d1, d2 = _shift_up(d, 1, dn), _shift_up(d, 2, dn)
            w0, w1, w2 = (w_ref[half, k:k + 1, :] for k in range(3))
            dp_ref[half] = (d * w2 + d1 * w1 + d2 * w0).astype(BF16)
            dw_ref[half, 0:1, :] += _colsum(d2 * u)
            dw_ref[half, 1:2, :] += _colsum(d1 * u)
            dw_ref[half, 2:3, :] += _colsum(d * u)
            db_ref[half] += _colsum(d)

    tm, tc = FFN_TM, FFN_TC
    blk, nxt, wsp, bsp = _ffn_specs(S)
    K = dx.shape[1]
    rows = pl.BlockSpec((tm, K), lambda j, i: (i, 0))
    rows_next = pl.BlockSpec((16, K), lambda j, i: (jnp.minimum((i + 1) * (tm // 16), S // 16 - 1), 0))
    return _pcall(body, [dx, dx, w_down, up, up, up_pre, w_fc], grid=(D_FF // tc, S // tm), name="ffn_act_bwd",
                  in_specs=[rows, rows_next, pl.BlockSpec((tc, K), lambda j, i: (j, 0)), blk, nxt, blk, wsp],
                  out_specs=[blk, wsp, bsp],
                  out_shape=[SDS((2, S, D_FF), BF16), SDS((2, 3, D_FF), F32), SDS((2, 1, D_FF), F32)],
                  sem=("parallel", "arbitrary"), job=job)


def local_step(x, mem, target, wb, ws, comm=None):
    S = x.shape[0]
    assert S % 2048 == 0
    if comm is None:
        comm = NoComm()
    else:
        wb = {}
    ws = dict(ws)

    bias = bias_tables(ws["rel_bias"])
    h1 = rmsnorm_fwd(x, ws["g_mix"], "norm_mix", job=comm.job("norm_mix"))
    comm.landed("norm_mix", wb, ws)
    w_fc = ws["w_ffn_conv"].reshape(3, 2, D_FF).transpose(1, 0, 2)
    b_fc = ws["b_ffn_conv"].reshape(2, 1, D_FF)
    proj = mm_nn(h1, wb["w_in"], "proj", out_dtype=BF16, out_bw=BLK, tn=1536, job=comm.job("proj"))
    comm.landed("proj", wb)
    qkv = proj.reshape(2, 3, 4, S, BLK)
    o, lse = attn_fwd(qkv, bias, job=comm.job("attn_fwd"))
    comm.landed("attn_fwd", wb)
    proj4 = proj.reshape(2, 12, S, BLK)
    mixed = mixer_post_fwd(o, proj4, ws["w_short_conv"], ws["g_attn_out"], ws["g_conv_out"])
    x1, h2 = mm_nn_rows(mixed, wb["w_out"], x, ws["g_xattn"], "out_proj")
    mem_n = rmsnorm_fwd(mem, ws["g_mem"], "norm_mem")
    xq = mm_nn(h2, wb["w_xq"], "xq", out_dtype=BF16)
    xk, xv = mm_nn_pair(mem_n, wb["w_xk"], wb["w_xv"], "xkv", out_dtype=BF16)
    xo = xattn_fwd(xq, xk, xv)
    x2, h3 = mm_nn_rows(xo, wb["w_xo"], x1, ws["g_ffn"], "xo_proj")
    up_pre, up, act = up_act_fwd(h3, wb["w_up"], w_fc, b_fc, job=comm.job("up_act_fwd"))
    comm.landed("up_act_fwd", wb)
    loss, dx3, dg_final = mm_nn_rows(act, wb["w_down"], x2, ws["g_final"].reshape(1, -1), "down_proj", target=target)

    gb, gs = {}, {"g_final": dg_final}
    gb["w_down"] = mm_tn(act, dx3, None, "dw_down", tm=1408, tn=1024)
    d_up_pre, dw_fc, db_fc = ffn_act_bwd(dx3, wb["w_down"], up, up_pre, w_fc, job=comm.job("ffn_act_bwd", gb))
    comm.landed("ffn_act_bwd")
    gs["b_ffn_conv"] = db_fc.reshape(1, 2 * D_FF)
    gs["w_ffn_conv"] = dw_fc.transpose(1, 0, 2).reshape(3, 2 * D_FF)
    gb["w_up"] = mm_tn(h3, d_up_pre, D_FF, "dw_up", shards=4, tn=1408, job=comm.job("dw_up"))
    comm.landed("dw_up")
    dx2, gs["g_ffn"] = mm_nt_norm_bwd(d_up_pre, D_FF, wb["w_up"], x2, ws["g_ffn"], dx3, "d_h3", tm=512,
                                      job=comm.job("d_h3", gb))
    comm.landed("d_h3")
    gb["w_xo"], dxo = mm_tn_nt(xo, dx2, wb["w_xo"], "dw_d_xo")
    dxq, dxk, dxv = xattn_bwd(xq, xk, xv, dxo)
    gb["w_xq"] = mm_tn(h2, dxq, None, "dw_xq")
    gb["w_xk"], gb["w_xv"] = mm_tn_pair(mem_n, dxk, dxv, "dw_xkv")
    dmem_n = mm_nt_pair_sum(dxk, wb["w_xk"], dxv, wb["w_xv"], "d_mem")
    gs["g_mem"] = rmsnorm_gain_grad(mem, dmem_n, "norm_mem_bwd")
    dx1, gs["g_xattn"] = mm_nt_norm_bwd(dxq, None, wb["w_xq"], x1, ws["g_xattn"], dx2, "d_h2")
    gb["w_out"], dmixed = mm_tn_nt(mixed, dx1, wb["w_out"], "dw_d_out")
    do, dgb, dcv, gs["g_attn_out"], gs["g_conv_out"] = mixer_post_bwd_a(
        dmixed, o, proj4, ws["w_short_conv"], ws["g_attn_out"], ws["g_conv_out"], job=comm.job("mixer_post_bwd_a", gb))
    comm.landed("mixer_post_bwd_a")
    dproj, ds_sum = attn_bwd(qkv, bias, o, lse, do, job=comm.job("attn_bwd"))
    comm.landed("attn_bwd")
    dproj, gs["w_short_conv"] = mixer_post_bwd_b(dproj.reshape(2, 12, S, BLK), dgb, dcv, proj4, ws["w_short_conv"])
    dproj = dproj.reshape(24, S, BLK)
    gb["w_in"] = mm_tn(h1, dproj, BLK, "dw_in", shards=4, tn=768, job=comm.job("dw_in"))
    comm.landed("dw_in")
    gs["rel_bias"] = bias_tables_bwd(ds_sum, job=comm.job("bias_tables_bwd", gb))
    comm.landed("bias_tables_bwd")
    grad_x, gs["g_mix"] = mm_nt_norm_bwd(dproj, BLK, wb["w_in"], x, ws["g_mix"], dx1, "d_h1", tm=512,
                                         job=comm.job("d_h1", gb))
    comm.landed("d_h1")
    return loss, grad_x, comm.finish(gb), gs


def _adam_update(w_ref, g_ref, m_ref, v_ref, d_ref, nm_ref, nv_ref):
    gv = g_ref[...]
    mn = ADAM_B1 * m_ref[...] + (1.0 - ADAM_B1) * gv
    vn = ADAM_B2 * v_ref[...] + (1.0 - ADAM_B2) * (gv * gv)
    m_hat = mn / (1.0 - ADAM_B1 ** ADAM_STEP)
    v_hat = vn / (1.0 - ADAM_B2 ** ADAM_STEP)
    d_ref[...] = -ADAM_LR * (m_hat / (jnp.sqrt(v_hat) + ADAM_EPS) + ADAM_WD * w_ref[...])
    nm_ref[...] = mn
    nv_ref[...] = vn


def adamw(ws, gs, ms, vs, name, job=None):
    n = len(ws)
    R, C = ws[0].shape
    tr = R // 4

    def body(*refs):
        for p in range(n):
            _adam_update(*(refs[q * n + p] for q in range(7)))

    blk = pl.BlockSpec((tr, C), lambda i: (i, 0))
    outs = _pcall(body, [*ws, *gs, *ms, *vs], grid=(4,), name=name, in_specs=[blk] * (4 * n), out_specs=[blk] * (3 * n),
                  out_shape=[SDS((R, C), F32)] * (3 * n), sem=("parallel",), job=job)
    return outs[:n], outs[n:2 * n], outs[2 * n:]


BIG = ("w_in", "w_out", "w_xq", "w_xk", "w_xv", "w_xo", "w_up", "w_down")
COL_SHARDED = ("w_in", "w_up")
ANY = pl.BlockSpec(memory_space=pl.ANY)


def _place():
    x, y, c = lax.axis_index("x"), lax.axis_index("y"), lax.axis_index("c")
    chips = [(1 - x, y), (x, 1 - y), (1 - x, 1 - y)]
    return x, y, c, chips


def _window(full, name, R, C, shard, half):
    r0, nr = (0, R) if half is None else (half * (R // 2), R // 2)
    if name in COL_SHARDED:
        return full.at[pl.ds(r0, nr), pl.ds(shard * C, C)]
    return full.at[pl.ds(shard * R + r0, nr), :]


def same_shape_groups(names, arrays):
    groups = {}
    for n in names:
        groups.setdefault(tuple(arrays[n].shape), []).append(n)
    return list(groups.values())


def place_shards(ws, mine_arr, col, name):
    n = len(ws)
    R, C = ws[0].shape

    def body(s_ref, *refs):
        for p in range(n):
            refs[n + p][...] = refs[p][...].astype(BF16)

    grid_spec = pltpu.PrefetchScalarGridSpec(
        num_scalar_prefetch=1, grid=(1,), in_specs=[pl.BlockSpec((R, C), lambda i, s: (0, 0))] * n,
        out_specs=[pl.BlockSpec((R, C), (lambda i, s: (0, s[0])) if col else (lambda i, s: (s[0], 0)))] * n)
    return pl.pallas_call(body, grid_spec=grid_spec, name=name,
                          out_shape=[SDS((R, 4 * C) if col else (4 * R, C), BF16)] * n,
                          compiler_params=_cp("arbitrary"))(mine_arr, *ws)


def _gather_jobs(names, shapes):
    nw = len(names)

    def start(full, sems):
        send, recv, fsend, frecv = sems
        x, y, c, chips = _place()
        mine = 2 * x + y
        for w, n in enumerate(names):
            R, C = shapes[w]
            own = _window(full[w], n, R, C, mine, c)
            for j, chip in enumerate(chips):
                pltpu.make_async_remote_copy(src_ref=own, dst_ref=own, send_sem=send.at[w, j], recv_sem=recv.at[w, j],
                                             device_id=(*chip, c), device_id_type=MESH).start()

    def mid(full, sems, lo=0, hi=nw):
        send, recv, fsend, frecv = sems
        x, y, c, chips = _place()
        sib = (x, y, 1 - c)
        for w, n in list(enumerate(names))[lo:hi]:
            R, C = shapes[w]
            for j, chip in enumerate(chips):
                landed = _window(full[w], n, R, C, 2 * chip[0] + chip[1], c)
                pltpu.make_async_remote_copy(src_ref=landed, dst_ref=landed, send_sem=send.at[w, j], recv_sem=recv.at[w, j],
                                             device_id=(*chip, c), device_id_type=MESH).wait_recv()
                pltpu.make_async_remote_copy(src_ref=landed, dst_ref=landed, send_sem=fsend.at[w, j],
                                             recv_sem=frecv.at[w, j], device_id=sib, device_id_type=MESH).start()

    def finish(full, sems):
        send, recv, fsend, frecv = sems
        x, y, c, chips = _place()
        mine = 2 * x + y
        sib = (x, y, 1 - c)
        for w, n in enumerate(names):
            R, C = shapes[w]
            own = _window(full[w], n, R, C, mine, c)
            for j, chip in enumerate(chips):
                landed = _window(full[w], n, R, C, 2 * chip[0] + chip[1], c)
                other = _window(full[w], n, R, C, 2 * chip[0] + chip[1], 1 - c)
                pltpu.make_async_remote_copy(src_ref=other, dst_ref=other, send_sem=fsend.at[w, j], recv_sem=frecv.at[w, j],
                                             device_id=sib, device_id_type=MESH).wait_recv()
                pltpu.make_async_remote_copy(src_ref=own, dst_ref=own, send_sem=send.at[w, j], recv_sem=recv.at[w, j],
                                             device_id=(*chip, c), device_id_type=MESH).wait_send()
                pltpu.make_async_remote_copy(src_ref=landed, dst_ref=landed, send_sem=fsend.at[w, j],
                                             recv_sem=frecv.at[w, j], device_id=sib, device_id_type=MESH).wait_send()

    return start, mid, finish, [pltpu.SemaphoreType.DMA((nw, 3))] * 4


class CommJob:
    def __init__(self, ins, out_shapes, inplace, start, finish, sems, mid=None):
        self.ins, self.out_shapes, self.inplace = list(ins), list(out_shapes), inplace
        self.start, self.mid, self.finish, self.sems = start, mid, finish, list(sems)

    def run(self, phase, ins, outs, sems):
        if phase == 0:
            self.start(ins, outs, sems)
        elif phase == 1:
            if self.mid is not None:
                self.mid(ins, outs, sems)
        else:
            self.finish(ins, outs, sems)


class Carried:
    def __init__(self, job, n_in, n_out, n_scratch, steps):
        self.job, self.n_in, self.n_out, self.n_scratch, self.steps = job, n_in, n_out, n_scratch, steps
        self.nji = len(job.ins) if job else 0
        self.njo = len(job.out_shapes) if job else 0

    def in_specs(self):
        return [ANY] * self.nji

    def out_specs(self):
        return [ANY] * self.njo

    def out_shapes(self):
        return list(self.job.out_shapes) if self.job else []

    def aliases(self):
        if not (self.job and self.job.inplace):
            return {}
        return {self.n_in + i: self.n_out + i for i in range(self.nji)}

    def sems(self):
        return list(self.job.sems) if self.job else []

    def args(self):
        return list(self.job.ins) if self.job else []

    def split(self, refs):
        a = self.n_in
        b = a + self.nji
        c = b + self.n_out
        d = c + self.njo
        e = d + self.n_scratch
        return refs[:a] + refs[b:c] + refs[d:e], (refs[a:b], refs[c:d], refs[e:])

    def phase(self, phase, step, jrefs):
        if self.job is None:
            return
        at = {0: 0, 1: max(self.steps - 2, 0), 2: self.steps - 1}[phase]

        @pl.when(step == at)
        def _():
            self.job.run(phase, *jrefs)


def _pcall(body, args, *, grid, in_specs, out_specs, out_shape, name, sem, scratch=(), aliases=None, job=None):
    n_in, n_out = len(args), len(out_shape)
    if job is None:
        return pl.pallas_call(
            body, grid=grid, in_specs=list(in_specs), out_specs=list(out_specs), out_shape=list(out_shape), name=name,
            scratch_shapes=list(scratch), input_output_aliases=dict(aliases or {}), compiler_params=_cp(*sem))(*args)
    total = int(np.prod(grid))
    car = Carried(job, n_in, n_out, len(scratch), total)

    def wrapped(*refs):
        main, jrefs = car.split(refs)
        lin = pl.program_id(0)
        for ax in range(1, len(grid)):
            lin = lin * grid[ax] + pl.program_id(ax)
        car.phase(0, lin, jrefs)
        body(*main)
        car.phase(1, lin, jrefs)
        car.phase(2, lin, jrefs)

    res = pl.pallas_call(
        wrapped, grid=grid, in_specs=list(in_specs) + car.in_specs(), out_specs=list(out_specs) + car.out_specs(),
        out_shape=list(out_shape) + car.out_shapes(), name=name, scratch_shapes=list(scratch) + car.sems(),
        input_output_aliases={**dict(aliases or {}), **car.aliases()},
        compiler_params=_cp(*(["arbitrary"] * len(grid))))(*args, *car.args())
    job.landed = list(res[n_out:])
    return list(res[:n_out])


def gather_job(placed, names):
    shapes = []
    for n in names:
        R, C = placed[n].shape
        shapes.append((R, C // 4) if n in COL_SHARDED else (R // 4, C))
    start, mid, finish, sems = _gather_jobs(names, shapes)
    arrays = [placed[n] for n in names]
    early = max(len(names) - 1, 1)

    def last(i, o, s):
        mid(o, s, early, len(names))
        finish(o, s)

    return CommJob(arrays, [SDS(a.shape, a.dtype) for a in arrays], True,
                   lambda i, o, s: start(o, s), last, sems, mid=lambda i, o, s: mid(o, s, 0, early))


def gather_rows_job(placed):
    widths = [a.shape[1] // 4 for a in placed]

    def copies(outs, sems):
        x, y, c, chips = _place()
        mine = 2 * x + y
        cps = []
        for w, ref in enumerate(outs):
            own = ref.at[:, pl.ds(mine * widths[w], widths[w])]
            for j, chip in enumerate(chips):
                theirs = ref.at[:, pl.ds((2 * chip[0] + chip[1]) * widths[w], widths[w])]
                kw = dict(send_sem=sems[0].at[w, j], recv_sem=sems[1].at[w, j], device_id=(*chip, c), device_id_type=MESH)
                cps.append((pltpu.make_async_remote_copy(src_ref=own, dst_ref=own, **kw),
                            pltpu.make_async_remote_copy(src_ref=theirs, dst_ref=theirs, **kw)))
        return cps

    def start(ins, outs, sems):
        for send, _ in copies(outs, sems):
            send.start()

    def finish(ins, outs, sems):
        for send, recv in copies(outs, sems):
            recv.wait_recv()
            send.wait_send()

    return CommJob(placed, [SDS(a.shape, a.dtype) for a in placed], True, start, finish,
                   [pltpu.SemaphoreType.DMA((len(placed), 3))] * 2)


def merge_jobs(a, b):
    assert a.inplace and b.inplace
    na, nb_, nsa = len(a.ins), len(b.ins), len(a.sems)

    def phase(which):
        def run(ins, outs, sems):
            for job, i, o, s in ((a, ins[:na], outs[:na], sems[:nsa]), (b, ins[na:], outs[na:], sems[nsa:])):
                fn = getattr(job, which)
                if fn is not None:
                    fn(i, o, s)
        return run

    merged = CommJob(a.ins + b.ins, a.out_shapes + b.out_shapes, True, phase("start"), phase("finish"), a.sems + b.sems,
                     mid=phase("mid"))
    merged.parts = (a, b, na)
    return merged


def split_landed(merged):
    a, b, na = merged.parts
    a.landed, b.landed = merged.landed[:na], merged.landed[na:]


def pair_exchange_job(grads, names):
    shapes = [grads[n].shape for n in names]

    def copies(ins, outs, sems):
        x, y, c, _ = _place()
        return [pltpu.make_async_remote_copy(
            src_ref=ins[w].at[:, pl.ds((1 - c) * (shapes[w][1] // 2), shapes[w][1] // 2), :], dst_ref=outs[w],
            send_sem=sems[0].at[w], recv_sem=sems[1].at[w], device_id=(x, y, 1 - c), device_id_type=MESH)
            for w in range(len(names))]

    def start(ins, outs, sems):
        for cp in copies(ins, outs, sems):
            cp.start()

    def finish(ins, outs, sems):
        for cp in copies(ins, outs, sems):
            cp.wait()

    return CommJob([grads[n] for n in names], [SDS((4, s[1] // 2, s[2]), F32) for s in shapes], False, start, finish,
                   [pltpu.SemaphoreType.DMA((len(names),))] * 2)


def chip_exchange_job(p16, names):
    shapes = [p16[n].shape for n in names]

    def copies(ins, outs, sems):
        x, y, c, chips = _place()
        return [pltpu.make_async_remote_copy(
            src_ref=ins[w].at[2 * chip[0] + chip[1]], dst_ref=outs[w].at[j],
            send_sem=sems[0].at[w, j], recv_sem=sems[1].at[w, j], device_id=(*chip, c), device_id_type=MESH)
            for w in range(len(names)) for j, chip in enumerate(chips)]

    def start(ins, outs, sems):
        for cp in copies(ins, outs, sems):
            cp.start()

    def finish(ins, outs, sems):
        for cp in copies(ins, outs, sems):
            cp.wait()

    return CommJob([p16[n] for n in names], [SDS((3,) + tuple(s[1:]), BF16) for s in shapes], False, start, finish,
                   [pltpu.SemaphoreType.DMA((len(names), 3))] * 2)


def half_exchange_job(full, names):
    shapes = [full[n].shape for n in names]

    def copies(outs, sems):
        x, y, c, _ = _place()
        cps = []
        for w in range(len(names)):
            Rh = shapes[w][0] // 2
            rows = outs[w].at[pl.ds(c * Rh, Rh), :]
            other = outs[w].at[pl.ds((1 - c) * Rh, Rh), :]
            cps.append((pltpu.make_async_remote_copy(src_ref=rows, dst_ref=rows, send_sem=sems[0].at[w], recv_sem=sems[1].at[w],
                                                     device_id=(x, y, 1 - c), device_id_type=MESH),
                        pltpu.make_async_remote_copy(src_ref=other, dst_ref=other, send_sem=sems[0].at[w], recv_sem=sems[1].at[w],
                                                     device_id=(x, y, 1 - c), device_id_type=MESH)))
        return cps

    def start(ins, outs, sems):
        for send, _ in copies(outs, sems):
            send.start()

    def finish(ins, outs, sems):
        for send, recv in copies(outs, sems):
            recv.wait_recv()
            send.wait_send()

    arrays = [full[n] for n in names]
    return CommJob(arrays, [SDS(a.shape, a.dtype) for a in arrays], True, start, finish,
                   [pltpu.SemaphoreType.DMA((len(names),))] * 2)


def allreduce_small(parts, name, job=None):
    n = len(parts)
    nji = len(job.ins) if job else 0
    njo = len(job.out_shapes) if job else 0

    def body(*refs):
        ins, jins = refs[:n], refs[n:n + nji]
        outs, jouts = refs[n + nji:2 * n + nji], refs[2 * n + nji:2 * n + nji + njo]
        bufs = refs[2 * n + nji + njo:3 * n + nji + njo]
        send, recv = refs[3 * n + nji + njo:3 * n + nji + njo + 2]
        jsems = refs[3 * n + nji + njo + 2:]
        x, y, c, _ = _place()
        me = 4 * x + 2 * y + c
        if job is not None:
            job.run(0, jins, jouts, jsems)
        sends = []
        for p in range(n):
            for k in range(1, 8):
                peer = (x ^ (k >> 2), y ^ ((k >> 1) & 1), c ^ (k & 1))
                cp = pltpu.make_async_remote_copy(src_ref=ins[p], dst_ref=bufs[p].at[k], send_sem=send.at[p, k - 1],
                                                  recv_sem=recv.at[p, k - 1], device_id=peer, device_id_type=MESH)
                cp.start()
                sends.append(cp)
            bufs[p][0] = ins[p][...]
        for cp in sends:
            cp.wait_recv()
        for p in range(n):
            acc = bufs[p][me]
            for d in range(1, 8):
                acc = acc + bufs[p][d ^ me]
            outs[p][...] = acc
        for cp in sends:
            cp.wait_send()
        if job is not None:
            job.run(1, jins, jouts, jsems)
            job.run(2, jins, jouts, jsems)

    vm = pl.BlockSpec(memory_space=pltpu.VMEM)
    res = pl.pallas_call(
        body, name=name, in_specs=[vm] * n + [ANY] * nji, out_specs=[vm] * n + [ANY] * njo,
        out_shape=[SDS(a.shape, F32) for a in parts] + (list(job.out_shapes) if job else []),
        input_output_aliases={n + i: n + i for i in range(nji)} if (job and job.inplace) else {},
        scratch_shapes=[pltpu.VMEM((8,) + tuple(a.shape), F32) for a in parts] + [pltpu.SemaphoreType.DMA((n, 7))] * 2
        + (list(job.sems) if job else []),
    )(*parts, *(job.ins if job else []))
    if job is not None:
        job.landed = list(res[n:])
    return list(res[:n])


def adamw_multi(ws, gs, ms, vs, name):
    n = len(ws)

    def body(*refs):
        for p in range(n):
            _adam_update(*(refs[q * n + p] for q in range(7)))

    vm = pl.BlockSpec(memory_space=pltpu.VMEM)
    outs = pl.pallas_call(body, name=name, in_specs=[vm] * (4 * n), out_specs=[vm] * (3 * n),
                          out_shape=[SDS(a.shape, F32) for a in ws] * 3)(*ws, *gs, *ms, *vs)
    return outs[:n], outs[n:2 * n], outs[2 * n:]


def pair_sum(gs, xhs, sel_arr, name):
    n = len(gs)
    _, R, C = gs[0].shape
    Rh = R // 2

    def body(sel_ref, *refs):
        for p in range(n):
            g_ref, x_ref, p32_ref, p16_ref = (refs[q * n + p] for q in range(4))
            s = g_ref[...] + x_ref[...]
            p16_ref[...] = s.astype(BF16)

            @pl.when(pl.program_id(0) == sel_ref[0])
            def _():
                p32_ref[...] = s

    same = pl.BlockSpec((None, Rh, C), lambda s, sel: (s, 0, 0))
    grid_spec = pltpu.PrefetchScalarGridSpec(
        num_scalar_prefetch=1, grid=(4,),
        in_specs=[pl.BlockSpec((None, Rh, C), lambda s, sel: (s, sel[1], 0))] * n + [same] * n,
        out_specs=[pl.BlockSpec((Rh, C), lambda s, sel: (0, 0))] * n + [same] * n)
    outs = pl.pallas_call(body, grid_spec=grid_spec, name=name,
                          out_shape=[SDS((Rh, C), F32)] * n + [SDS((4, Rh, C), BF16)] * n,
                          compiler_params=_cp("arbitrary"))(sel_arr, *gs, *xhs)
    return outs[:n], outs[n:]


def chip_sum(p32s, ys, sel_arr, name):
    n = len(p32s)
    Rh, C = p32s[0].shape

    def body(s_ref, *refs):
        for p in range(n):
            p_ref, y_ref, o_ref = (refs[q * n + p] for q in range(3))
            o_ref[...] = ((p_ref[...] + y_ref[0].astype(F32)) + y_ref[1].astype(F32)) + y_ref[2].astype(F32)

    grid_spec = pltpu.PrefetchScalarGridSpec(
        num_scalar_prefetch=1, grid=(1,),
        in_specs=[pl.BlockSpec((Rh, C), lambda i, s: (0, 0))] * n + [pl.BlockSpec((3, Rh, C), lambda i, s: (0, 0, 0))] * n,
        out_specs=[pl.BlockSpec((Rh, C), lambda i, s: (s[1], 0))] * n)
    return pl.pallas_call(body, grid_spec=grid_spec, name=name, out_shape=[SDS((2 * Rh, C), F32)] * n,
                          compiler_params=_cp("arbitrary"))(sel_arr, *p32s, *ys)


class StepComm:
    FIRST = ("w_in",)
    SQUARE = ("w_out", "w_xq", "w_xk", "w_xv", "w_xo")
    FFN = ("w_up", "w_down")
    PLAN = {
        "proj": ("gather", SQUARE[:3]), "attn_fwd": ("gather", SQUARE[3:] + FFN[:1]), "up_act_fwd": ("gather", FFN[1:]),
        "ffn_act_bwd": ("pair", FFN[1:]), "dw_up": ("chip", FFN[1:]),
        "d_h3": ("pair", FFN[:1]), "mixer_post_bwd_a": ("pair", SQUARE), "attn_bwd": ("chip", FFN[:1] + SQUARE),
        "dw_in": ("half", FFN + SQUARE), "bias_tables_bwd": ("pair", FIRST), "d_h1": ("chip", FIRST),
    }

    def __init__(self, placed, conv_placed, sel_arr):
        self.placed, self.conv_placed, self.sel_arr = placed, conv_placed, sel_arr
        self.g, self.p32, self.p16, self.full, self.done, self.jobs = {}, {}, {}, {}, {}, {}

    def job(self, tag, gb=None):
        if tag == "norm_mix":
            self.jobs[tag] = merge_jobs(gather_job(self.placed, self.FIRST), gather_rows_job(self.conv_placed))
            return self.jobs[tag]
        kind, names = self.PLAN[tag]
        if kind == "gather":
            j = gather_job(self.placed, names)
        elif kind == "pair":
            for n in names:
                self.g[n] = gb[n] if gb[n].ndim == 3 else gb[n].reshape(4, gb[n].shape[0] // 4, gb[n].shape[1])
            j = pair_exchange_job(self.g, names)
        elif kind == "chip":
            j = chip_exchange_job(self.p16, names)
        else:
            j = half_exchange_job(self.full, names)
        self.jobs[tag] = j
        return j

    def landed(self, tag, wb=None, ws=None):
        if tag == "norm_mix":
            split_landed(self.jobs[tag])
            first, conv, _ = self.jobs[tag].parts
            wb.update(zip(self.FIRST, first.landed))
            ws.update((n, a[:3]) for n, a in zip(SMALL_SHARDED, conv.landed))
            return
        kind, names = self.PLAN[tag]
        got = dict(zip(names, self.jobs[tag].landed))
        if kind == "gather":
            wb.update(got)
        elif kind == "half":
            self.done.update(got)
        for group in same_shape_groups(names, got):
            if kind == "pair":
                p32s, p16s = pair_sum([self.g[n] for n in group], [got[n] for n in group], self.sel_arr, "pair_sum_" + group[0])
                self.p32.update(zip(group, p32s))
                self.p16.update(zip(group, p16s))
            elif kind == "chip":
                fulls = chip_sum([self.p32[n] for n in group], [got[n] for n in group], self.sel_arr, "chip_sum_" + group[0])
                self.full.update(zip(group, fulls))

    def finish(self, gb):
        return self.done

    def last_job(self):
        self.jobs["end_half"] = half_exchange_job(self.full, self.FIRST)
        return self.jobs["end_half"]

    def take_last(self):
        self.done.update(zip(self.FIRST, self.jobs["end_half"].landed))


class NoComm:
    def job(self, tag, gb=None):
        return None

    def landed(self, tag, wb=None, ws=None):
        pass

    def finish(self, gb):
        return gb


SMALL = ("rel_bias", "g_mix", "w_short_conv", "g_attn_out", "g_conv_out", "g_xattn", "g_mem", "g_ffn",
         "w_ffn_conv", "b_ffn_conv", "g_final")
SMALL_SHARDED = ("w_short_conv", "w_ffn_conv")


def kernel(x, mem, rel_bias, g_mix, w_in, w_short_conv, g_attn_out, g_conv_out, w_out, g_xattn, g_mem, w_xq, w_xk, w_xv, w_xo, g_ffn, w_up, w_ffn_conv, b_ffn_conv, w_down, g_final, loss_target, m_rel_bias, m_g_mix, m_w_in, m_w_short_conv, m_g_attn_out, m_g_conv_out, m_w_out, m_g_xattn, m_g_mem, m_w_xq, m_w_xk, m_w_xv, m_w_xo, m_g_ffn, m_w_up, m_w_ffn_conv, m_b_ffn_conv, m_w_down, m_g_final, v_rel_bias, v_g_mix, v_w_in, v_w_short_conv, v_g_attn_out, v_g_conv_out, v_w_out, v_g_xattn, v_g_mem, v_w_xq, v_w_xk, v_w_xv, v_w_xo, v_g_ffn, v_w_up, v_w_ffn_conv, v_b_ffn_conv, v_w_down, v_g_final):
    names = ("rel_bias", "g_mix", "w_in", "w_short_conv", "g_attn_out", "g_conv_out", "w_out", "g_xattn", "g_mem", "w_xq",
             "w_xk", "w_xv", "w_xo", "g_ffn", "w_up", "w_ffn_conv", "b_ffn_conv", "w_down", "g_final")
    W = dict(zip(names, (rel_bias, g_mix, w_in, w_short_conv, g_attn_out, g_conv_out, w_out, g_xattn, g_mem, w_xq, w_xk, w_xv,
                         w_xo, g_ffn, w_up, w_ffn_conv, b_ffn_conv, w_down, g_final)))
    M = dict(zip(names, (m_rel_bias, m_g_mix, m_w_in, m_w_short_conv, m_g_attn_out, m_g_conv_out, m_w_out, m_g_xattn, m_g_mem,
                         m_w_xq, m_w_xk, m_w_xv, m_w_xo, m_g_ffn, m_w_up, m_w_ffn_conv, m_b_ffn_conv, m_w_down, m_g_final)))
    V = dict(zip(names, (v_rel_bias, v_g_mix, v_w_in, v_w_short_conv, v_g_attn_out, v_g_conv_out, v_w_out, v_g_xattn, v_g_mem,
                         v_w_xq, v_w_xk, v_w_xv, v_w_xo, v_g_ffn, v_w_up, v_w_ffn_conv, v_b_ffn_conv, v_w_down, v_g_final)))
    xi, yi, ci = lax.axis_index("x"), lax.axis_index("y"), lax.axis_index("c")
    mine = 2 * xi + yi
    mine_arr = jnp.reshape(mine, (1,)).astype(jnp.int32)
    sel_arr = jnp.stack([mine, ci]).astype(jnp.int32)
    conv_placed = []
    for n in SMALL_SHARDED:
        shard = W[n][0]
        conv_placed.append(lax.dynamic_update_slice(jnp.zeros((8, 4 * shard.shape[1]), F32), shard, (0, mine * shard.shape[1])))
    shards = {n: W[n][0] for n in BIG}
    placed = {}
    for group in same_shape_groups(BIG, shards):
        outs = place_shards([shards[n] for n in group], mine_arr, group[0] in COL_SHARDED, "place_" + group[0])
        placed.update(zip(group, outs))
    comm = StepComm(placed, conv_placed, sel_arr)
    ws = {n: W[n] for n in SMALL if n not in SMALL_SHARDED}

    loss, grad_x, gfull, gs = local_step(x[0], mem[0], loss_target[0], None, ws, comm)

    def as2d(a):
        return a.reshape(1, -1) if a.ndim == 1 else a

    reduced = allreduce_small([as2d(gs[n]) for n in SMALL] + [loss], "reduce_small", job=comm.last_job())
    comm.take_last()
    gsm = dict(zip(SMALL, reduced[:-1]))
    loss = reduced[-1][0, 0]

    grads, delta, new_m, new_v = {}, {}, {}, {}
    for group in same_shape_groups(BIG, gfull):
        ds, nms, nvs = adamw([W[n][0] for n in group], [gfull[n] for n in group], [M[n][0] for n in group],
                             [V[n][0] for n in group], "adamw_" + group[0])
        for n, d, nm, nv in zip(group, ds, nms, nvs):
            grads[n], delta[n], new_m[n], new_v[n] = gfull[n][None], d[None], nm[None], nv[None]
    for n in SMALL_SHARDED:
        wid = W[n].shape[2]
        gsm[n] = lax.dynamic_slice(gsm[n], (0, mine * wid), (3, wid))

    def own(a, n):
        return a[0] if n in SMALL_SHARDED else as2d(a)

    d, nm, nv = adamw_multi([own(W[n], n) for n in SMALL], [gsm[n] for n in SMALL], [own(M[n], n) for n in SMALL],
                            [own(V[n], n) for n in SMALL], "adamw_small")
    for i, n in enumerate(SMALL):
        shape = W[n].shape
        grads[n], delta[n], new_m[n], new_v[n] = (a.reshape(shape) for a in (gsm[n], d[i], nm[i], nv[i]))
    return (loss, grad_x[None], *[grads[n] for n in names], *[delta[n] for n in names],
            *[new_m[n] for n in names], *[new_v[n] for n in names])
```
